```python
import jax, jax.numpy as jnp
from jax import lax
import numpy as np

D_MODEL = 1024
BATCH = 8
SEQ = 4096
DEPTH = 2

HEAD_DIM = 64
BLOCK = 128
A_Q_HEADS = 8
A_KV_HEADS = 2
A_GROUP = A_Q_HEADS // A_KV_HEADS
A_WINDOW = 128
B_HEADS = 8
C_HEADS = 16
C_PATTERNS = ((128, 1), (512, 4), (2048, 16))
MEM_LEN = 256
X_HEADS = 4
X_HEAD_DIM = D_MODEL // X_HEADS
D_FF = 2816
RMS_EPS = 1e-6

A_Q_W = A_Q_HEADS * HEAD_DIM
A_KV_W = A_KV_HEADS * HEAD_DIM
B_W = B_HEADS * HEAD_DIM
EVEN_IN = A_Q_W + 2 * A_KV_W + 3 * B_W
EVEN_MIX = A_Q_W + B_W
ODD_IN = 3 * C_HEADS * HEAD_DIM
ODD_MIX = C_HEADS * HEAD_DIM

kernel_name = 'hybrid_swa_stickbreak_dilated_block'


def rms_norm(x, g):
    xf = x.astype(jnp.float32)
    y = xf * lax.rsqrt(jnp.mean(xf * xf, axis=-1, keepdims=True) + RMS_EPS)
    return (y * g.astype(jnp.float32)).astype(x.dtype)


def alibi_slopes(n_heads):
    return jnp.asarray(2.0 ** (-8.0 * np.arange(1, n_heads + 1) / n_heads), dtype=jnp.float32)


def swiglu_ffn(x, w_gu, w_down):
    gate, up = jnp.split(x @ w_gu, 2, axis=-1)
    return (jax.nn.silu(gate) * up) @ w_down


def banded_attention(q, k, v, slopes, max_dist, step, sinks=None):
    b, l, hkv, g, dh = q.shape
    nb = -(-l // BLOCK)
    lp = nb * BLOCK
    pad = lp - l
    qb = jnp.pad(q, ((0, 0), (0, pad), (0, 0), (0, 0), (0, 0))).reshape(b, nb, BLOCK, hkv, g, dh)
    kv_pad = ((0, 0), (BLOCK, pad), (0, 0), (0, 0))
    k = jnp.pad(k, kv_pad).reshape(b, nb + 1, BLOCK, hkv, dh)
    v = jnp.pad(v, kv_pad).reshape(b, nb + 1, BLOCK, hkv, dh)
    kb = jnp.concatenate([k[:, :-1], k[:, 1:]], axis=2)
    vb = jnp.concatenate([v[:, :-1], v[:, 1:]], axis=2)
    s = jnp.einsum('bnqhgd,bnkhd->bnhgqk', qb, kb).astype(jnp.float32) * (dh ** -0.5)
    dist = jnp.arange(BLOCK)[:, None] + BLOCK - jnp.arange(2 * BLOCK)[None, :]
    kpos = jnp.arange(nb)[:, None] * BLOCK - BLOCK + jnp.arange(2 * BLOCK)[None, :]
    valid = (dist >= 0) & (dist <= max_dist) & (kpos[:, None, :] >= 0)
    bias = -(slopes.astype(jnp.float32) * step)[:, :, None, None] * dist.astype(jnp.float32)
    s = jnp.where(valid[None, :, None, None], s + bias[None, None], -jnp.inf)
    m = jnp.max(s, axis=-1)
    if sinks is not None:
        sk = sinks.astype(jnp.float32)[..., None]
        m = jnp.maximum(m, sk)
    p = jnp.exp(s - m[..., None])
    denom = jnp.sum(p, axis=-1)
    if sinks is not None:
        denom = denom + jnp.exp(sk - m)
    o = jnp.einsum('bnhgqk,bnkhd->bnqhgd', (p / denom[..., None]).astype(v.dtype), vb)
    lse = m + jnp.log(denom)
    o = o.reshape(b, lp, hkv, g, dh)[:, :l]
    lse = jnp.moveaxis(lse, -1, 2).reshape(b, lp, hkv, g)[:, :l]
    return o, lse


def stick_breaking_attention(q, k, v):
    b, s, h, dh = q.shape
    nb = s // BLOCK
    qb = q.reshape(b, nb, BLOCK, h, dh).transpose(1, 0, 2, 3, 4)
    spos = jnp.arange(s)
    scale = dh ** -0.5

    def one_block(args):
        i, qi = args
        z = jnp.einsum('bqhd,bkhd->bhqk', qi, k).astype(jnp.float32) * scale
        tpos = i * BLOCK + jnp.arange(BLOCK)
        strict = spos[None, :] < tpos[:, None]
        log_keep = jnp.where(strict, jax.nn.log_sigmoid(-z), 0.0)
        log_after = lax.cumsum(log_keep, axis=3, reverse=True) - log_keep
        a = jnp.where(strict, jnp.exp(jax.nn.log_sigmoid(z) + log_after), 0.0)
        return jnp.einsum('bhqk,bkhd->bqhd', a.astype(v.dtype), v)

    o = lax.map(one_block, (jnp.arange(nb), qb))
    return o.transpose(1, 0, 2, 3, 4).reshape(b, s, h, dh)


def dilated_attention(q, k, v, slopes):
    b, s, h, dh = q.shape
    outs, lses = [], []
    for window, dil in C_PATTERNS:
        sp = -(-s // dil) * dil
        ls = sp // dil

        def strided(t):
            t = jnp.pad(t, ((0, 0), (0, sp - s), (0, 0), (0, 0)))
            return t.reshape(b, ls, dil, h, dh).transpose(0, 2, 1, 3, 4).reshape(b * dil, ls, h, dh)

        o, lse = banded_attention(strided(q)[:, :, :, None], strided(k), strided(v),
                                  slopes[:, None], window // dil, dil)
        o = o[:, :, :, 0].reshape(b, dil, ls, h, dh).transpose(0, 2, 1, 3, 4).reshape(b, sp, h, dh)[:, :s]
        lse = lse[..., 0].reshape(b, dil, ls, h).transpose(0, 2, 1, 3).reshape(b, sp, h)[:, :s]
        outs.append(o)
        lses.append(lse)
    w = jax.nn.softmax(jnp.stack(lses), axis=0)
    o = jnp.sum(w[..., None] * jnp.stack(outs).astype(jnp.float32), axis=0)
    return o.astype(q.dtype)


def even_mixer(h, w_in, q_gain, k_gain, sinks, w_out):
    b, s, _ = h.shape
    cuts = np.cumsum([A_Q_W, A_KV_W, A_KV_W, B_W, B_W]).tolist()
    qa, ka, va, qb, kb, vb = jnp.split(h @ w_in, cuts, axis=-1)
    qa = rms_norm(qa.reshape(b, s, A_KV_HEADS, A_GROUP, HEAD_DIM), q_gain)
    ka = rms_norm(ka.reshape(b, s, A_KV_HEADS, HEAD_DIM), k_gain)
    va = va.reshape(b, s, A_KV_HEADS, HEAD_DIM)
    slopes_a = alibi_slopes(A_Q_HEADS).reshape(A_KV_HEADS, A_GROUP)
    o_a, _ = banded_attention(qa, ka, va, slopes_a, A_WINDOW - 1, 1,
                              sinks.reshape(A_KV_HEADS, A_GROUP))
    o_b = stick_breaking_attention(qb.reshape(b, s, B_HEADS, HEAD_DIM),
                                   kb.reshape(b, s, B_HEADS, HEAD_DIM),
                                   vb.reshape(b, s, B_HEADS, HEAD_DIM))
    o = jnp.concatenate([o_a.reshape(b, s, A_Q_W), o_b.reshape(b, s, B_W)], axis=-1)
    return o @ w_out


def odd_mixer(h, w_in, q_gain, k_gain, w_out):
    b, s, _ = h.shape
    qkv = (h @ w_in).reshape(b, s, 3, C_HEADS, HEAD_DIM)
    q = rms_norm(qkv[:, :, 0], q_gain)
    k = rms_norm(qkv[:, :, 1], k_gain)
    o = dilated_attention(q, k, qkv[:, :, 2], alibi_slopes(C_HEADS))
    return o.reshape(b, s, ODD_MIX) @ w_out


def memory_cross_attention(h, m, w_q, w_kv, q_gain, k_gain, w_o):
    b, s, _ = h.shape
    q = rms_norm((h @ w_q).reshape(b, s, X_HEADS, X_HEAD_DIM), q_gain)
    kv = (m @ w_kv).reshape(b, m.shape[1], 2, X_HEADS, X_HEAD_DIM)
    k = rms_norm(kv[:, :, 0], k_gain)
    v = kv[:, :, 1]
    sc = jnp.einsum('bqhd,bkhd->bhqk', q, k).astype(jnp.float32) * (X_HEAD_DIM ** -0.5)
    p = jax.nn.softmax(sc, axis=-1).astype(v.dtype)
    o = jnp.einsum('bhqk,bkhd->bqhd', p, v).reshape(b, s, X_HEADS * X_HEAD_DIM)
    return o @ w_o


def _fwd_setup_inputs(seed: int = 0) -> dict:
    key = jax.random.key(seed)
    k = jax.random.split(key, 25)
    n_even = (DEPTH + 1) // 2
    n_odd = DEPTH // 2
    f32 = jnp.float32

    def dense(kk, shape, fan_in):
        return jax.random.normal(kk, shape, f32) * (fan_in ** -0.5)

    def gain(kk, shape):
        return 1.0 + 0.02 * jax.random.normal(kk, shape, f32)

    return {
        'x': jax.random.normal(k[0], (BATCH, SEQ, D_MODEL), f32),
        'mem': jax.random.normal(k[1], (BATCH, MEM_LEN, D_MODEL), f32),
        'ffn1_norm': gain(k[2], (DEPTH, D_MODEL)),
        'ffn1_w_gu': dense(k[3], (DEPTH, D_MODEL, 2 * D_FF), D_MODEL),
        'ffn1_w_down': dense(k[4], (DEPTH, D_FF, D_MODEL), D_FF),
        'mix_norm': gain(k[5], (DEPTH, D_MODEL)),
        'ev_w_in': dense(k[6], (n_even, D_MODEL, EVEN_IN), D_MODEL),
        'ev_q_gain': gain(k[7], (n_even, HEAD_DIM)),
        'ev_k_gain': gain(k[8], (n_even, HEAD_DIM)),
        'ev_sinks': 0.5 * jax.random.normal(k[9], (n_even, A_Q_HEADS), f32),
        'ev_w_out': dense(k[10], (n_even, EVEN_MIX, D_MODEL), EVEN_MIX),
        'od_w_in': dense(k[11], (n_odd, D_MODEL, ODD_IN), D_MODEL),
        'od_q_gain': gain(k[12], (n_odd, HEAD_DIM)),
        'od_k_gain': gain(k[13], (n_odd, HEAD_DIM)),
        'od_w_out': dense(k[14], (n_odd, ODD_MIX, D_MODEL), ODD_MIX),
        'xa_norm': gain(k[15], (DEPTH, D_MODEL)),
        'xa_mem_norm': gain(k[16], (DEPTH, D_MODEL)),
        'xa_w_q': dense(k[17], (DEPTH, D_MODEL, X_HEADS * X_HEAD_DIM), D_MODEL),
        'xa_w_kv': dense(k[18], (DEPTH, D_MODEL, 2 * X_HEADS * X_HEAD_DIM), D_MODEL),
        'xa_q_gain': gain(k[19], (DEPTH, X_HEAD_DIM)),
        'xa_k_gain': gain(k[20], (DEPTH, X_HEAD_DIM)),
        'xa_w_o': dense(k[21], (DEPTH, X_HEADS * X_HEAD_DIM, D_MODEL), X_HEADS * X_HEAD_DIM),
        'ffn2_norm': gain(k[22], (DEPTH, D_MODEL)),
        'ffn2_w_gu': dense(k[23], (DEPTH, D_MODEL, 2 * D_FF), D_MODEL),
        'ffn2_w_down': dense(k[24], (DEPTH, D_FF, D_MODEL), D_FF),
    }


def _fwd_reference(x, mem, ffn1_norm, ffn1_w_gu, ffn1_w_down, mix_norm,
              ev_w_in, ev_q_gain, ev_k_gain, ev_sinks, ev_w_out,
              od_w_in, od_q_gain, od_k_gain, od_w_out,
              xa_norm, xa_mem_norm, xa_w_q, xa_w_kv, xa_q_gain, xa_k_gain, xa_w_o,
              ffn2_norm, ffn2_w_gu, ffn2_w_down):
    for layer in range(DEPTH):
        x = x + 0.5 * swiglu_ffn(rms_norm(x, ffn1_norm[layer]), ffn1_w_gu[layer], ffn1_w_down[layer])
        h = rms_norm(x, mix_norm[layer])
        if layer % 2 == 0:
            j = layer // 2
            x = x + even_mixer(h, ev_w_in[j], ev_q_gain[j], ev_k_gain[j], ev_sinks[j], ev_w_out[j])
        else:
            j = layer // 2
            x = x + odd_mixer(h, od_w_in[j], od_q_gain[j], od_k_gain[j], od_w_out[j])
        x = x + memory_cross_attention(rms_norm(x, xa_norm[layer]), rms_norm(mem, xa_mem_norm[layer]),
                                       xa_w_q[layer], xa_w_kv[layer], xa_q_gain[layer],
                                       xa_k_gain[layer], xa_w_o[layer])
        x = x + 0.5 * swiglu_ffn(rms_norm(x, ffn2_norm[layer]), ffn2_w_gu[layer], ffn2_w_down[layer])
    return x


import jax as _jax
import jax.numpy as _jnp

TWIN_FORMAT = 'train_step'
FWD_PARAMS = ['x', 'mem', 'ffn1_norm', 'ffn1_w_gu', 'ffn1_w_down', 'mix_norm', 'ev_w_in', 'ev_q_gain', 'ev_k_gain', 'ev_sinks', 'ev_w_out', 'od_w_in', 'od_q_gain', 'od_k_gain', 'od_w_out', 'xa_norm', 'xa_mem_norm', 'xa_w_q', 'xa_w_kv', 'xa_q_gain', 'xa_k_gain', 'xa_w_o', 'ffn2_norm', 'ffn2_w_gu', 'ffn2_w_down']
TWIN_WEIGHTS = ['ffn1_norm', 'ffn1_w_gu', 'ffn1_w_down', 'mix_norm', 'ev_w_in', 'ev_q_gain', 'ev_k_gain', 'ev_sinks', 'ev_w_out', 'od_w_in', 'od_q_gain', 'od_k_gain', 'od_w_out', 'xa_norm', 'xa_mem_norm', 'xa_w_q', 'xa_w_kv', 'xa_q_gain', 'xa_k_gain', 'xa_w_o', 'ffn2_norm', 'ffn2_w_gu', 'ffn2_w_down']
TWIN_DIFF_INPUT = 'x'
TWIN_INPUTS = ['x', 'mem', 'ffn1_norm', 'ffn1_w_gu', 'ffn1_w_down', 'mix_norm', 'ev_w_in', 'ev_q_gain', 'ev_k_gain', 'ev_sinks', 'ev_w_out', 'od_w_in', 'od_q_gain', 'od_k_gain', 'od_w_out', 'xa_norm', 'xa_mem_norm', 'xa_w_q', 'xa_w_kv', 'xa_q_gain', 'xa_k_gain', 'xa_w_o', 'ffn2_norm', 'ffn2_w_gu', 'ffn2_w_down', 'loss_target', 'm_ffn1_norm', 'm_ffn1_w_gu', 'm_ffn1_w_down', 'm_mix_norm', 'm_ev_w_in', 'm_ev_q_gain', 'm_ev_k_gain', 'm_ev_sinks', 'm_ev_w_out', 'm_od_w_in', 'm_od_q_gain', 'm_od_k_gain', 'm_od_w_out', 'm_xa_norm', 'm_xa_mem_norm', 'm_xa_w_q', 'm_xa_w_kv', 'm_xa_q_gain', 'm_xa_k_gain', 'm_xa_w_o', 'm_ffn2_norm', 'm_ffn2_w_gu', 'm_ffn2_w_down', 'v_ffn1_norm', 'v_ffn1_w_gu', 'v_ffn1_w_down', 'v_mix_norm', 'v_ev_w_in', 'v_ev_q_gain', 'v_ev_k_gain', 'v_ev_sinks', 'v_ev_w_out', 'v_od_w_in', 'v_od_q_gain', 'v_od_k_gain', 'v_od_w_out', 'v_xa_norm', 'v_xa_mem_norm', 'v_xa_w_q', 'v_xa_w_kv', 'v_xa_q_gain', 'v_xa_k_gain', 'v_xa_w_o', 'v_ffn2_norm', 'v_ffn2_w_gu', 'v_ffn2_w_down']
TWIN_OUTPUTS = ['loss', 'grad_x', 'grad_ffn1_norm', 'grad_ffn1_w_gu', 'grad_ffn1_w_down', 'grad_mix_norm', 'grad_ev_w_in', 'grad_ev_q_gain', 'grad_ev_k_gain', 'grad_ev_sinks', 'grad_ev_w_out', 'grad_od_w_in', 'grad_od_q_gain', 'grad_od_k_gain', 'grad_od_w_out', 'grad_xa_norm', 'grad_xa_mem_norm', 'grad_xa_w_q', 'grad_xa_w_kv', 'grad_xa_q_gain', 'grad_xa_k_gain', 'grad_xa_w_o', 'grad_ffn2_norm', 'grad_ffn2_w_gu', 'grad_ffn2_w_down', 'delta_ffn1_norm', 'delta_ffn1_w_gu', 'delta_ffn1_w_down', 'delta_mix_norm', 'delta_ev_w_in', 'delta_ev_q_gain', 'delta_ev_k_gain', 'delta_ev_sinks', 'delta_ev_w_out', 'delta_od_w_in', 'delta_od_q_gain', 'delta_od_k_gain', 'delta_od_w_out', 'delta_xa_norm', 'delta_xa_mem_norm', 'delta_xa_w_q', 'delta_xa_w_kv', 'delta_xa_q_gain', 'delta_xa_k_gain', 'delta_xa_w_o', 'delta_ffn2_norm', 'delta_ffn2_w_gu', 'delta_ffn2_w_down', 'new_m_ffn1_norm', 'new_m_ffn1_w_gu', 'new_m_ffn1_w_down', 'new_m_mix_norm', 'new_m_ev_w_in', 'new_m_ev_q_gain', 'new_m_ev_k_gain', 'new_m_ev_sinks', 'new_m_ev_w_out', 'new_m_od_w_in', 'new_m_od_q_gain', 'new_m_od_k_gain', 'new_m_od_w_out', 'new_m_xa_norm', 'new_m_xa_mem_norm', 'new_m_xa_w_q', 'new_m_xa_w_kv', 'new_m_xa_q_gain', 'new_m_xa_k_gain', 'new_m_xa_w_o', 'new_m_ffn2_norm', 'new_m_ffn2_w_gu', 'new_m_ffn2_w_down', 'new_v_ffn1_norm', 'new_v_ffn1_w_gu', 'new_v_ffn1_w_down', 'new_v_mix_norm', 'new_v_ev_w_in', 'new_v_ev_q_gain', 'new_v_ev_k_gain', 'new_v_ev_sinks', 'new_v_ev_w_out', 'new_v_od_w_in', 'new_v_od_q_gain', 'new_v_od_k_gain', 'new_v_od_w_out', 'new_v_xa_norm', 'new_v_xa_mem_norm', 'new_v_xa_w_q', 'new_v_xa_w_kv', 'new_v_xa_q_gain', 'new_v_xa_k_gain', 'new_v_xa_w_o', 'new_v_ffn2_norm', 'new_v_ffn2_w_gu', 'new_v_ffn2_w_down']
TWIN_LEAF_KINDS = {'loss': 'loss', 'grad_x': 'grad_x', 'grad_ffn1_norm': 'grad_w', 'grad_ffn1_w_gu': 'grad_w', 'grad_ffn1_w_down': 'grad_w', 'grad_mix_norm': 'grad_w', 'grad_ev_w_in': 'grad_w', 'grad_ev_q_gain': 'grad_w', 'grad_ev_k_gain': 'grad_w', 'grad_ev_sinks': 'grad_w', 'grad_ev_w_out': 'grad_w', 'grad_od_w_in': 'grad_w', 'grad_od_q_gain': 'grad_w', 'grad_od_k_gain': 'grad_w', 'grad_od_w_out': 'grad_w', 'grad_xa_norm': 'grad_w', 'grad_xa_mem_norm': 'grad_w', 'grad_xa_w_q': 'grad_w', 'grad_xa_w_kv': 'grad_w', 'grad_xa_q_gain': 'grad_w', 'grad_xa_k_gain': 'grad_w', 'grad_xa_w_o': 'grad_w', 'grad_ffn2_norm': 'grad_w', 'grad_ffn2_w_gu': 'grad_w', 'grad_ffn2_w_down': 'grad_w', 'delta_ffn1_norm': 'delta_w', 'delta_ffn1_w_gu': 'delta_w', 'delta_ffn1_w_down': 'delta_w', 'delta_mix_norm': 'delta_w', 'delta_ev_w_in': 'delta_w', 'delta_ev_q_gain': 'delta_w', 'delta_ev_k_gain': 'delta_w', 'delta_ev_sinks': 'delta_w', 'delta_ev_w_out': 'delta_w', 'delta_od_w_in': 'delta_w', 'delta_od_q_gain': 'delta_w', 'delta_od_k_gain': 'delta_w', 'delta_od_w_out': 'delta_w', 'delta_xa_norm': 'delta_w', 'delta_xa_mem_norm': 'delta_w', 'delta_xa_w_q': 'delta_w', 'delta_xa_w_kv': 'delta_w', 'delta_xa_q_gain': 'delta_w', 'delta_xa_k_gain': 'delta_w', 'delta_xa_w_o': 'delta_w', 'delta_ffn2_norm': 'delta_w', 'delta_ffn2_w_gu': 'delta_w', 'delta_ffn2_w_down': 'delta_w', 'new_m_ffn1_norm': 'new_m', 'new_m_ffn1_w_gu': 'new_m', 'new_m_ffn1_w_down': 'new_m', 'new_m_mix_norm': 'new_m', 'new_m_ev_w_in': 'new_m', 'new_m_ev_q_gain': 'new_m', 'new_m_ev_k_gain': 'new_m', 'new_m_ev_sinks': 'new_m', 'new_m_ev_w_out': 'new_m', 'new_m_od_w_in': 'new_m', 'new_m_od_q_gain': 'new_m', 'new_m_od_k_gain': 'new_m', 'new_m_od_w_out': 'new_m', 'new_m_xa_norm': 'new_m', 'new_m_xa_mem_norm': 'new_m', 'new_m_xa_w_q': 'new_m', 'new_m_xa_w_kv': 'new_m', 'new_m_xa_q_gain': 'new_m', 'new_m_xa_k_gain': 'new_m', 'new_m_xa_w_o': 'new_m', 'new_m_ffn2_norm': 'new_m', 'new_m_ffn2_w_gu': 'new_m', 'new_m_ffn2_w_down': 'new_m', 'new_v_ffn1_norm': 'new_v', 'new_v_ffn1_w_gu': 'new_v', 'new_v_ffn1_w_down': 'new_v', 'new_v_mix_norm': 'new_v', 'new_v_ev_w_in': 'new_v', 'new_v_ev_q_gain': 'new_v', 'new_v_ev_k_gain': 'new_v', 'new_v_ev_sinks': 'new_v', 'new_v_ev_w_out': 'new_v', 'new_v_od_w_in': 'new_v', 'new_v_od_q_gain': 'new_v', 'new_v_od_k_gain': 'new_v', 'new_v_od_w_out': 'new_v', 'new_v_xa_norm': 'new_v', 'new_v_xa_mem_norm': 'new_v', 'new_v_xa_w_q': 'new_v', 'new_v_xa_w_kv': 'new_v', 'new_v_xa_q_gain': 'new_v', 'new_v_xa_k_gain': 'new_v', 'new_v_xa_w_o': 'new_v', 'new_v_ffn2_norm': 'new_v', 'new_v_ffn2_w_gu': 'new_v', 'new_v_ffn2_w_down': 'new_v'}


def _forward(args):
    return _fwd_reference(*[args[k] for k in FWD_PARAMS])


def _output_shape():
    def fwd():
        inp = _fwd_setup_inputs(0)
        return _fwd_reference(*[inp[k] for k in FWD_PARAMS])
    out = _jax.eval_shape(fwd)
    return out.shape, out.dtype

N_MICROBATCH = 1
ADAM_LR = 0.001
ADAM_B1 = 0.9
ADAM_B2 = 0.999
ADAM_EPS = 1e-08
ADAM_WD = 0.01
ADAM_STEP = 10
PER_EXAMPLE_BATCH_AXIS = {'x': 0, 'mem': 0, 'loss_target': 0}
SHARED_INPUTS = []
_WEIGHT_DTYPES = {'ffn1_norm': _jnp.float32, 'ffn1_w_gu': _jnp.float32, 'ffn1_w_down': _jnp.float32, 'mix_norm': _jnp.float32, 'ev_w_in': _jnp.float32, 'ev_q_gain': _jnp.float32, 'ev_k_gain': _jnp.float32, 'ev_sinks': _jnp.float32, 'ev_w_out': _jnp.float32, 'od_w_in': _jnp.float32, 'od_q_gain': _jnp.float32, 'od_k_gain': _jnp.float32, 'od_w_out': _jnp.float32, 'xa_norm': _jnp.float32, 'xa_mem_norm': _jnp.float32, 'xa_w_q': _jnp.float32, 'xa_w_kv': _jnp.float32, 'xa_q_gain': _jnp.float32, 'xa_k_gain': _jnp.float32, 'xa_w_o': _jnp.float32, 'ffn2_norm': _jnp.float32, 'ffn2_w_gu': _jnp.float32, 'ffn2_w_down': _jnp.float32}
MOMENT_SCALE = {'ffn1_norm': 6.134044e+00, 'ffn1_w_gu': 9.406982e-02, 'ffn1_w_down': 1.659940e-01, 'mix_norm': 7.827635e+00, 'ev_w_in': 2.867918e-01, 'ev_q_gain': 9.802485e+00, 'ev_k_gain': 9.823209e+00, 'ev_sinks': 3.254506e+01, 'ev_w_out': 3.812397e-01, 'od_w_in': 3.090257e-01, 'od_q_gain': 1.283800e+01, 'od_k_gain': 1.278249e+01, 'od_w_out': 3.946017e-01, 'xa_norm': 5.186213e-02, 'xa_mem_norm': 4.055777e-01, 'xa_w_q': 5.379482e-02, 'xa_w_kv': 8.909799e-02, 'xa_q_gain': 1.286905e+00, 'xa_k_gain': 1.284090e+00, 'xa_w_o': 1.040767e-01, 'ffn2_norm': 6.147615e+00, 'ffn2_w_gu': 8.283509e-02, 'ffn2_w_down': 1.511817e-01}


def _to_microbatches(a, axis):
    t = _jnp.moveaxis(a, axis, 0)
    t = t.reshape((N_MICROBATCH, t.shape[0] // N_MICROBATCH) + t.shape[1:])
    return _jnp.moveaxis(t, 1, axis + 1)


def setup_inputs(seed: int = 0) -> dict:
    inp = _fwd_setup_inputs(seed)
    key = _jax.random.fold_in(_jax.random.key(seed), 7919)
    shape, _ = _output_shape()
    out = dict(inp)
    out["loss_target"] = _jax.random.normal(_jax.random.fold_in(key, 0), shape, _jnp.float32)
    for i, name in enumerate(TWIN_WEIGHTS):
        w = inp[name].astype(_jnp.float32)
        if MOMENT_SCALE is None:
            s = _jnp.sqrt(_jnp.mean(_jnp.square(w)) + 1e-30)
        else:
            s = MOMENT_SCALE[name]
        km, kv = _jax.random.split(_jax.random.fold_in(key, i + 1))
        out[name] = w
        out["m_" + name] = s * _jax.random.normal(km, w.shape, _jnp.float32)
        out["v_" + name] = (s * s) * _jax.random.uniform(kv, w.shape, _jnp.float32, 0.5, 1.5)
    if N_MICROBATCH > 1:
        for name, axis in PER_EXAMPLE_BATCH_AXIS.items():
            out[name] = _to_microbatches(out[name], axis)
    return {'x': out['x'], 'mem': out['mem'], 'ffn1_norm': out['ffn1_norm'], 'ffn1_w_gu': out['ffn1_w_gu'], 'ffn1_w_down': out['ffn1_w_down'], 'mix_norm': out['mix_norm'], 'ev_w_in': out['ev_w_in'], 'ev_q_gain': out['ev_q_gain'], 'ev_k_gain': out['ev_k_gain'], 'ev_sinks': out['ev_sinks'], 'ev_w_out': out['ev_w_out'], 'od_w_in': out['od_w_in'], 'od_q_gain': out['od_q_gain'], 'od_k_gain': out['od_k_gain'], 'od_w_out': out['od_w_out'], 'xa_norm': out['xa_norm'], 'xa_mem_norm': out['xa_mem_norm'], 'xa_w_q': out['xa_w_q'], 'xa_w_kv': out['xa_w_kv'], 'xa_q_gain': out['xa_q_gain'], 'xa_k_gain': out['xa_k_gain'], 'xa_w_o': out['xa_w_o'], 'ffn2_norm': out['ffn2_norm'], 'ffn2_w_gu': out['ffn2_w_gu'], 'ffn2_w_down': out['ffn2_w_down'], 'loss_target': out['loss_target'], 'm_ffn1_norm': out['m_ffn1_norm'], 'm_ffn1_w_gu': out['m_ffn1_w_gu'], 'm_ffn1_w_down': out['m_ffn1_w_down'], 'm_mix_norm': out['m_mix_norm'], 'm_ev_w_in': out['m_ev_w_in'], 'm_ev_q_gain': out['m_ev_q_gain'], 'm_ev_k_gain': out['m_ev_k_gain'], 'm_ev_sinks': out['m_ev_sinks'], 'm_ev_w_out': out['m_ev_w_out'], 'm_od_w_in': out['m_od_w_in'], 'm_od_q_gain': out['m_od_q_gain'], 'm_od_k_gain': out['m_od_k_gain'], 'm_od_w_out': out['m_od_w_out'], 'm_xa_norm': out['m_xa_norm'], 'm_xa_mem_norm': out['m_xa_mem_norm'], 'm_xa_w_q': out['m_xa_w_q'], 'm_xa_w_kv': out['m_xa_w_kv'], 'm_xa_q_gain': out['m_xa_q_gain'], 'm_xa_k_gain': out['m_xa_k_gain'], 'm_xa_w_o': out['m_xa_w_o'], 'm_ffn2_norm': out['m_ffn2_norm'], 'm_ffn2_w_gu': out['m_ffn2_w_gu'], 'm_ffn2_w_down': out['m_ffn2_w_down'], 'v_ffn1_norm': out['v_ffn1_norm'], 'v_ffn1_w_gu': out['v_ffn1_w_gu'], 'v_ffn1_w_down': out['v_ffn1_w_down'], 'v_mix_norm': out['v_mix_norm'], 'v_ev_w_in': out['v_ev_w_in'], 'v_ev_q_gain': out['v_ev_q_gain'], 'v_ev_k_gain': out['v_ev_k_gain'], 'v_ev_sinks': out['v_ev_sinks'], 'v_ev_w_out': out['v_ev_w_out'], 'v_od_w_in': out['v_od_w_in'], 'v_od_q_gain': out['v_od_q_gain'], 'v_od_k_gain': out['v_od_k_gain'], 'v_od_w_out': out['v_od_w_out'], 'v_xa_norm': out['v_xa_norm'], 'v_xa_mem_norm': out['v_xa_mem_norm'], 'v_xa_w_q': out['v_xa_w_q'], 'v_xa_w_kv': out['v_xa_w_kv'], 'v_xa_q_gain': out['v_xa_q_gain'], 'v_xa_k_gain': out['v_xa_k_gain'], 'v_xa_w_o': out['v_xa_w_o'], 'v_ffn2_norm': out['v_ffn2_norm'], 'v_ffn2_w_gu': out['v_ffn2_w_gu'], 'v_ffn2_w_down': out['v_ffn2_w_down']}


def _loss(weights, diff, rest, loss_target):
    with _jax.named_scope("forward"):
        args = {**rest, TWIN_DIFF_INPUT: diff, **{k: w.astype(_WEIGHT_DTYPES[k]) for k, w in weights.items()}}
        y = _forward(args)
    with _jax.named_scope("loss_head"):
        err = _jnp.square(y.astype(_jnp.float32) - loss_target)
        return 0.5 * _jnp.sum(_jnp.mean(err, axis=-1)) if err.ndim else 0.5 * err


def _adamw(w, g, m, v):
    m = ADAM_B1 * m + (1.0 - ADAM_B1) * g
    v = ADAM_B2 * v + (1.0 - ADAM_B2) * _jnp.square(g)
    m_hat = m / (1.0 - ADAM_B1 ** ADAM_STEP)
    v_hat = v / (1.0 - ADAM_B2 ** ADAM_STEP)
    delta = -ADAM_LR * (m_hat / (_jnp.sqrt(v_hat) + ADAM_EPS) + ADAM_WD * w)
    return delta, m, v


def reference(x, mem, ffn1_norm, ffn1_w_gu, ffn1_w_down, mix_norm, ev_w_in, ev_q_gain, ev_k_gain, ev_sinks, ev_w_out, od_w_in, od_q_gain, od_k_gain, od_w_out, xa_norm, xa_mem_norm, xa_w_q, xa_w_kv, xa_q_gain, xa_k_gain, xa_w_o, ffn2_norm, ffn2_w_gu, ffn2_w_down, loss_target, m_ffn1_norm, m_ffn1_w_gu, m_ffn1_w_down, m_mix_norm, m_ev_w_in, m_ev_q_gain, m_ev_k_gain, m_ev_sinks, m_ev_w_out, m_od_w_in, m_od_q_gain, m_od_k_gain, m_od_w_out, m_xa_norm, m_xa_mem_norm, m_xa_w_q, m_xa_w_kv, m_xa_q_gain, m_xa_k_gain, m_xa_w_o, m_ffn2_norm, m_ffn2_w_gu, m_ffn2_w_down, v_ffn1_norm, v_ffn1_w_gu, v_ffn1_w_down, v_mix_norm, v_ev_w_in, v_ev_q_gain, v_ev_k_gain, v_ev_sinks, v_ev_w_out, v_od_w_in, v_od_q_gain, v_od_k_gain, v_od_w_out, v_xa_norm, v_xa_mem_norm, v_xa_w_q, v_xa_w_kv, v_xa_q_gain, v_xa_k_gain, v_xa_w_o, v_ffn2_norm, v_ffn2_w_gu, v_ffn2_w_down):
    given = dict(x=x, mem=mem, ffn1_norm=ffn1_norm, ffn1_w_gu=ffn1_w_gu, ffn1_w_down=ffn1_w_down, mix_norm=mix_norm, ev_w_in=ev_w_in, ev_q_gain=ev_q_gain, ev_k_gain=ev_k_gain, ev_sinks=ev_sinks, ev_w_out=ev_w_out, od_w_in=od_w_in, od_q_gain=od_q_gain, od_k_gain=od_k_gain, od_w_out=od_w_out, xa_norm=xa_norm, xa_mem_norm=xa_mem_norm, xa_w_q=xa_w_q, xa_w_kv=xa_w_kv, xa_q_gain=xa_q_gain, xa_k_gain=xa_k_gain, xa_w_o=xa_w_o, ffn2_norm=ffn2_norm, ffn2_w_gu=ffn2_w_gu, ffn2_w_down=ffn2_w_down, loss_target=loss_target, m_ffn1_norm=m_ffn1_norm, m_ffn1_w_gu=m_ffn1_w_gu, m_ffn1_w_down=m_ffn1_w_down, m_mix_norm=m_mix_norm, m_ev_w_in=m_ev_w_in, m_ev_q_gain=m_ev_q_gain, m_ev_k_gain=m_ev_k_gain, m_ev_sinks=m_ev_sinks, m_ev_w_out=m_ev_w_out, m_od_w_in=m_od_w_in, m_od_q_gain=m_od_q_gain, m_od_k_gain=m_od_k_gain, m_od_w_out=m_od_w_out, m_xa_norm=m_xa_norm, m_xa_mem_norm=m_xa_mem_norm, m_xa_w_q=m_xa_w_q, m_xa_w_kv=m_xa_w_kv, m_xa_q_gain=m_xa_q_gain, m_xa_k_gain=m_xa_k_gain, m_xa_w_o=m_xa_w_o, m_ffn2_norm=m_ffn2_norm, m_ffn2_w_gu=m_ffn2_w_gu, m_ffn2_w_down=m_ffn2_w_down, v_ffn1_norm=v_ffn1_norm, v_ffn1_w_gu=v_ffn1_w_gu, v_ffn1_w_down=v_ffn1_w_down, v_mix_norm=v_mix_norm, v_ev_w_in=v_ev_w_in, v_ev_q_gain=v_ev_q_gain, v_ev_k_gain=v_ev_k_gain, v_ev_sinks=v_ev_sinks, v_ev_w_out=v_ev_w_out, v_od_w_in=v_od_w_in, v_od_q_gain=v_od_q_gain, v_od_k_gain=v_od_k_gain, v_od_w_out=v_od_w_out, v_xa_norm=v_xa_norm, v_xa_mem_norm=v_xa_mem_norm, v_xa_w_q=v_xa_w_q, v_xa_w_kv=v_xa_w_kv, v_xa_q_gain=v_xa_q_gain, v_xa_k_gain=v_xa_k_gain, v_xa_w_o=v_xa_w_o, v_ffn2_norm=v_ffn2_norm, v_ffn2_w_gu=v_ffn2_w_gu, v_ffn2_w_down=v_ffn2_w_down)
    weights = {n: given[n] for n in TWIN_WEIGHTS}
    shared = {n: given[n] for n in SHARED_INPUTS}
    per_example = {n: given[n] for n in ['x', 'mem']}
    grad_fn = _jax.value_and_grad(_loss, argnums=(0, 1))

    def one_microbatch(ex, loss_target):
        ex = dict(ex)
        diff = ex.pop(TWIN_DIFF_INPUT)
        return grad_fn(weights, diff, {**shared, **ex}, loss_target)

    if N_MICROBATCH == 1:
        loss, (grad_w, grad_x) = one_microbatch(per_example, given["loss_target"])
    else:
        def body(carry, xs):
            loss_sum, grad_sum = carry
            l_k, (gw_k, gx_k) = one_microbatch(xs[0], xs[1])
            with _jax.named_scope("update"):
                return (loss_sum + l_k, _jax.tree.map(_jnp.add, grad_sum, gw_k)), gx_k

        init = (_jnp.zeros((), _jnp.float32), _jax.tree.map(_jnp.zeros_like, weights))
        (loss, grad_w), grad_x = _jax.lax.scan(body, init, (per_example, given["loss_target"]))
    with _jax.named_scope("update"):
        delta_w, new_m, new_v = {}, {}, {}
        for n in TWIN_WEIGHTS:
            delta_w[n], new_m[n], new_v[n] = _adamw(weights[n], grad_w[n], given["m_" + n], given["v_" + n])
    return (loss, grad_x, *[grad_w[n] for n in TWIN_WEIGHTS], *[delta_w[n] for n in TWIN_WEIGHTS],
            *[new_m[n] for n in TWIN_WEIGHTS], *[new_v[n] for n in TWIN_WEIGHTS])
```

```python
import functools

import numpy as np
import jax
import jax.numpy as jnp
from jax import lax
from jax.experimental import pallas as pl
from jax.experimental.pallas import tpu as pltpu

F32 = jnp.float32
BF16 = jnp.bfloat16
MESH = pl.DeviceIdType.MESH

HEAD_DIM = 64
BLOCK = 128
RMS_EPS = 1e-6
A_Q_HEADS, A_KV_HEADS = 8, 2
B_HEADS = 8
C_HEADS = 16
C_PATTERNS = ((128, 1), (512, 4), (2048, 16))
X_HEADS = 4
N_DEV = 8
LANES = 1024
VMEM_LIMIT_BYTES = 56 * 1024 * 1024
SB_SKIP_LOG = -110.0
NEG_BIG = -1e30

ADAM_LR, ADAM_B1, ADAM_B2, ADAM_EPS, ADAM_WD, ADAM_STEP = 0.001, 0.9, 0.999, 1e-08, 0.01, 10

NN = (((1,), (0,)), ((), ()))
NT = (((1,), (1,)), ((), ()))
TN = (((0,), (0,)), ((), ()))


def _pcall(body, **kw):
    return pl.pallas_call(body, **kw)


def _params(**kw):
    return pltpu.CompilerParams(vmem_limit_bytes=VMEM_LIMIT_BYTES, **kw)


def _tile(dim, cap, unit=128):
    if dim <= cap:
        return dim
    t = (cap // unit) * unit
    while t >= unit:
        if dim % t == 0:
            return t
        t -= unit
    raise ValueError(f"no tile for {dim} under {cap}")


def _dot(a, b, dims):
    return lax.dot_general(a.astype(BF16), b.astype(BF16), dims, preferred_element_type=F32)


@functools.partial(jax.custom_vjp, nondiff_argnums=(2,))
def _dot_vjp(a, b, nt):
    return _dot(a, b, NT if nt else NN)


def _dot_vjp_fwd(a, b, nt):
    return _dot(a, b, NT if nt else NN), (a.astype(BF16), b.astype(BF16))


def _dot_vjp_bwd(nt, res, g):
    a, b = res
    if nt:
        return _dot(g, b, NN), _dot(g, a, TN)
    return _dot(g, b, NT), _dot(a, g, TN)


_dot_vjp.defvjp(_dot_vjp_fwd, _dot_vjp_bwd)


def _plain_dot(a, b, nt):
    return _dot(a, b, NT if nt else NN)


def _split_dot(x, mat):
    hi = x.astype(BF16)
    lo = (x - hi.astype(F32)).astype(BF16)
    return (lax.dot_general(hi, mat, NN, preferred_element_type=F32)
            + lax.dot_general(lo, mat, NN, preferred_element_type=F32))


def _tri(after):
    j = lax.broadcasted_iota(jnp.int32, (BLOCK, BLOCK), 0)
    s = lax.broadcasted_iota(jnp.int32, (BLOCK, BLOCK), 1)
    return jnp.where(j > s if after else j < s, 1.0, 0.0).astype(BF16)


def _suffix_sum(x):
    return _split_dot(x, _tri(True))


@jax.custom_vjp
def _suffix_sum_vjp(x):
    return _suffix_sum(x)


def _suffix_sum_vjp_fwd(x):
    return _suffix_sum(x), None


def _suffix_sum_vjp_bwd(_, g):
    return (_split_dot(g, _tri(False)),)


_suffix_sum_vjp.defvjp(_suffix_sum_vjp_fwd, _suffix_sum_vjp_bwd)


def _in(a, block, imap):
    return (a, block, imap)


def _out(shape, dtype, block, imap, acc=False):
    return (shape, dtype, block, imap, acc)


def tcall(fn, grid, ins, outs, name):
    nin = len(ins)
    ngrid = len(grid)

    def body(*refs):
        ids = tuple(pl.program_id(a) for a in range(ngrid))
        res = fn(ids, *[r[...] for r in refs[:nin]])
        first = ids[0] == 0
        for a in range(1, ngrid):
            first = jnp.logical_and(first, ids[a] == 0)
        for o_ref, r, spec in zip(refs[nin:], res, outs):
            if spec[4]:
                @pl.when(first)
                def _(o_ref=o_ref):
                    o_ref[...] = jnp.zeros(o_ref.shape, o_ref.dtype)
                o_ref[...] += r.astype(o_ref.dtype)
            else:
                o_ref[...] = r.astype(o_ref.dtype)

    return _pcall(
        body, name=name, grid=grid,
        in_specs=[pl.BlockSpec(b, m) for (_, b, m) in ins],
        out_specs=[pl.BlockSpec(b, m) for (_, _, b, m, _) in outs],
        out_shape=[jax.ShapeDtypeStruct(s, d) for (s, d, _, _, _) in outs],
        compiler_params=_params(),
    )(*[a for (a, _, _) in ins])


def _row(a, tm, width=None, cb=0):
    width = a.shape[1] if width is None else width
    return _in(a, (tm, width), lambda i, cb=cb: (i, cb))


def _full(a):
    zeros = (0,) * a.ndim
    return _in(a, a.shape, lambda *ids: zeros)


def _row_out(n, width, dtype, tm):
    return _out((n, width), dtype, (tm, width), lambda i: (i, 0))


def _acc_out(shape):
    zeros = (0,) * len(shape)
    return _out(shape, F32, shape, lambda *ids: zeros, acc=True)


def mm(a, b, mode, name, *, out_dtype=F32, scale=1.0, res=None):
    if mode == "nn":
        (m, k), (k2, n) = a.shape, b.shape
    elif mode == "nt":
        (m, k), (n, k2) = a.shape, b.shape
    else:
        (k, m), (k2, n) = a.shape, b.shape
    assert k == k2, (a.shape, b.shape, mode)
    tm, tn, tk = _tile(m, 512), _tile(n, 1408), _tile(k, 1408)
    nk = k // tk
    dims = {"nn": NN, "nt": NT, "tn": TN}[mode]
    has_res = res is not None

    def body(*refs):
        if has_res:
            a_ref, b_ref, r_ref, o_ref, acc_ref = refs
        else:
            a_ref, b_ref, o_ref, acc_ref = refs
        kk = pl.program_id(2)

        @pl.when(kk == 0)
        def _():
            acc_ref[...] = jnp.zeros(acc_ref.shape, F32)

        acc_ref[...] += _dot(a_ref[...], b_ref[...], dims)

        @pl.when(kk == nk - 1)
        def _():
            out = acc_ref[...]
            if scale != 1.0:
                out = out * scale
            if has_res:
                out = out + r_ref[...]
            o_ref[...] = out.astype(o_ref.dtype)

    a_spec = (pl.BlockSpec((tk, tm), lambda i, j, kk: (kk, i)) if mode == "tn"
              else pl.BlockSpec((tm, tk), lambda i, j, kk: (i, kk)))
    b_spec = (pl.BlockSpec((tn, tk), lambda i, j, kk: (j, kk)) if mode == "nt"
              else pl.BlockSpec((tk, tn), lambda i, j, kk: (kk, j)))
    in_specs = [a_spec, b_spec]
    args = [a, b]
    if has_res:
        in_specs.append(pl.BlockSpec((tm, tn), lambda i, j, kk: (i, j)))
        args.append(res)
    return _pcall(
        body, name=name, grid=(m // tm, n // tn, nk),
        in_specs=in_specs,
        out_specs=pl.BlockSpec((tm, tn), lambda i, j, kk: (i, j)),
        out_shape=jax.ShapeDtypeStruct((m, n), out_dtype),
        scratch_shapes=[pltpu.VMEM((tm, tn), F32)],
        compiler_params=_params(dimension_semantics=("parallel", "parallel", "arbitrary")),
    )(*args)


def _rms(x, g):
    return x * lax.rsqrt(jnp.mean(x * x, axis=-1, keepdims=True) + RMS_EPS) * g


def _silu_mul(gate, up):
    return gate / (1.0 + jnp.exp(-gate)) * up


def rmsnorm_fwd(x, g, name):
    n, d = x.shape
    tm = _tile(n, 512, 8)
    (h,) = tcall(lambda ids, xt, gt: (_rms(xt, gt),), (n // tm,), [_row(x, tm), _full(g)],
                 [_row_out(n, d, BF16, tm)], name)
    return h


def rmsnorm_bwd(x, g, dh, dres, name):
    n, d = x.shape
    tm = _tile(n, 256, 8)

    def fn(ids, xt, gt, dht, *rest):
        _, vjp = jax.vjp(_rms, xt, gt)
        dx, dg = vjp(dht.astype(F32))
        if rest:
            dx = dx + rest[0]
        return dx, dg

    ins = [_row(x, tm), _full(g), _row(dh, tm)] + ([_row(dres, tm)] if dres is not None else [])
    return tcall(fn, (n // tm,), ins, [_row_out(n, d, F32, tm), _acc_out(g.shape)], name)


def ffn_fwd(x, g, w_gu, w_down, tag):
    n = x.shape[0]
    f = w_down.shape[0]
    h = rmsnorm_fwd(x, g, tag + "_norm")
    gu = mm(h, w_gu, "nn", tag + "_gu")
    tm = _tile(n, 128, 8)
    (a,) = tcall(lambda ids, gt, ut: (_silu_mul(gt, ut),), (n // tm,),
                 [_row(gu, tm, f, 0), _row(gu, tm, f, 1)], [_row_out(n, f, BF16, tm)], tag + "_act")
    y = mm(a, w_down, "nn", tag + "_down", scale=0.5, res=x)
    return y, (x, h, gu, a)


def ffn_bwd(dy, saved, g, w_gu, w_down, tag):
    x, h, gu, a = saved
    n = x.shape[0]
    f = w_down.shape[0]
    da = mm(dy, w_down, "nt", tag + "_da", scale=0.5)
    d_wdown = mm(a, dy, "tn", tag + "_dwd", scale=0.5)
    tm = _tile(n, 128, 8)

    def act_bwd(ids, gt, ut, dat):
        _, vjp = jax.vjp(_silu_mul, gt, ut)
        dg, du = vjp(dat)
        return (jnp.concatenate([dg, du], axis=1),)

    (dgu,) = tcall(act_bwd, (n // tm,), [_row(gu, tm, f, 0), _row(gu, tm, f, 1), _row(da, tm)],
                   [_row_out(n, 2 * f, BF16, tm)], tag + "_dact")
    dh = mm(dgu, w_gu, "nt", tag + "_dh")
    d_wgu = mm(h, dgu, "tn", tag + "_dwgu")
    dx, dg = rmsnorm_bwd(x, g, dh, dy, tag + "_dnorm")
    return dx, dg, d_wgu, d_wdown


def _alibi(n_heads):
    return [float(s) for s in np.asarray(2.0 ** (-8.0 * np.arange(1, n_heads + 1) / n_heads), dtype=np.float32)]


def _banded_tile(dot, first, q, kp, kc, vp, vc, qg, kg, sinks, *, hkv, grp, max_dist, step, slopes, want_lse):
    row = lax.broadcasted_iota(jnp.int32, (BLOCK, 2 * BLOCK), 0)
    col = lax.broadcasted_iota(jnp.int32, (BLOCK, 2 * BLOCK), 1)
    dist = row + BLOCK - col
    valid = (dist >= 0) & (dist <= max_dist) & ((col >= BLOCK) | jnp.logical_not(first))
    distf = dist.astype(F32)
    outs, lses = [], []
    for hk in range(hkv):
        sl = slice(hk * HEAD_DIM, (hk + 1) * HEAD_DIM)
        k2 = _rms(jnp.concatenate([kp[:, sl], kc[:, sl]], axis=0), kg)
        v2 = jnp.concatenate([vp[:, sl], vc[:, sl]], axis=0)
        for gi in range(grp):
            hd = hk * grp + gi
            qh = _rms(q[:, hd * HEAD_DIM:(hd + 1) * HEAD_DIM], qg)
            s = dot(qh, k2, True) * (HEAD_DIM ** -0.5)
            s = jnp.where(valid, s - (slopes[hd] * step) * distf, NEG_BIG)
            m = jnp.max(s, axis=-1, keepdims=True)
            if sinks is not None:
                pick = lax.broadcasted_iota(jnp.int32, sinks.shape, 1) == hd
                sk = jnp.sum(jnp.where(pick, sinks, 0.0), axis=1, keepdims=True)
                m = jnp.maximum(m, sk)
            m = lax.stop_gradient(m)
            p = jnp.exp(s - m)
            denom = jnp.sum(p, axis=-1, keepdims=True)
            if sinks is not None:
                denom = denom + jnp.exp(sk - m)
            outs.append(dot(p / denom, v2, False))
            if want_lse:
                lses.append(jnp.broadcast_to(m + jnp.log(denom), (BLOCK, HEAD_DIM)))
    o = jnp.concatenate(outs, axis=1)
    if want_lse:
        return o, jnp.concatenate(lses, axis=1)
    return (o,)


def _banded_specs(view, qcol, kcol, vcol, wq, wkv):
    def at(colfn, prev):
        if prev:
            return lambda r, n: (jnp.maximum(n - 1, 0), colfn(r))
        return lambda r, n: (n, colfn(r))
    return [
        _in(view, (BLOCK, wq), at(qcol, False)),
        _in(view, (BLOCK, wkv), at(kcol, True)),
        _in(view, (BLOCK, wkv), at(kcol, False)),
        _in(view, (BLOCK, wkv), at(vcol, True)),
        _in(view, (BLOCK, wkv), at(vcol, False)),
    ]


def banded_fwd(view, dil, cols, qg, kg, sinks, cfg, name):
    ns = view.shape[0]
    nb = ns // BLOCK
    wq, wkv = cfg["hkv"] * cfg["grp"] * HEAD_DIM, cfg["hkv"] * HEAD_DIM
    has_sinks = sinks is not None

    def fn(ids, q, kp, kc, vp, vc, qgt, kgt, *rest):
        return _banded_tile(_plain_dot, ids[1] == 0, q, kp, kc, vp, vc, qgt, kgt,
                            rest[0] if has_sinks else None, **cfg)

    ins = _banded_specs(view, *cols, wq, wkv) + [_full(qg), _full(kg)] + ([_full(sinks)] if has_sinks else [])
    outs = [_out((ns, dil * wq), BF16, (BLOCK, wq), lambda r, n: (n, r))]
    if cfg["want_lse"]:
        outs.append(_out((ns, dil * wq), F32, (BLOCK, wq), lambda r, n: (n, r)))
    return tcall(fn, (dil, nb), ins, outs, name)


def banded_bwd(view, dil, cols, qg, kg, sinks, cfg, cts, name):
    ns = view.shape[0]
    nb = ns // BLOCK
    wq, wkv = cfg["hkv"] * cfg["grp"] * HEAD_DIM, cfg["hkv"] * HEAD_DIM
    has_sinks = sinks is not None
    assert len(cts) == (2 if cfg["want_lse"] else 1)

    def fn(ids, q, kp, kc, vp, vc, qgt, kgt, *rest):
        sk = rest[0] if has_sinks else None
        ct = rest[1 if has_sinks else 0:]
        first = ids[1] == 0

        def f(q, kp, kc, vp, vc, qgt, kgt, *s):
            return _banded_tile(_dot_vjp, first, q, kp, kc, vp, vc, qgt, kgt, s[0] if has_sinks else None, **cfg)

        prim = (q, kp, kc, vp, vc, qgt, kgt) + ((sk,) if has_sinks else ())
        _, vjp = jax.vjp(f, *prim)
        return vjp(tuple(c.astype(F32) for c in ct))

    ins = (_banded_specs(view, *cols, wq, wkv) + [_full(qg), _full(kg)] + ([_full(sinks)] if has_sinks else [])
           + [_in(a, (BLOCK, wq), (lambda r, n, cf=cf: (n, cf(r)))) for (a, cf) in cts])
    blk = lambda w: _out((ns, dil * w), F32, (BLOCK, w), lambda r, n: (n, r))
    outs = [blk(wq), blk(wkv), blk(wkv), blk(wkv), blk(wkv), _acc_out(qg.shape), _acc_out(kg.shape)]
    if has_sinks:
        outs.append(_acc_out(sinks.shape))
    res = tcall(fn, (dil, nb), ins, outs, name)
    dq, dkp, dkc, dvp, dvc = res[:5]

    def shift_add(ids, kc_, kn_, vc_, vn_):
        keep = ids[1] < nb - 1
        return kc_ + jnp.where(keep, kn_, 0.0), vc_ + jnp.where(keep, vn_, 0.0)

    here = lambda r, n: (n, r)
    nxt = lambda r, n: (jnp.minimum(n + 1, nb - 1), r)
    dk, dv = tcall(shift_add, (dil, nb),
                   [_in(dkc, (BLOCK, wkv), here), _in(dkp, (BLOCK, wkv), nxt),
                    _in(dvc, (BLOCK, wkv), here), _in(dvp, (BLOCK, wkv), nxt)],
                   [blk(wkv), blk(wkv)], name + "_shift")
    return (dq, dk, dv) + tuple(res[5:])


def _log_sigmoid(z):
    return jnp.minimum(z, 0.0) - jnp.log(1.0 + jnp.exp(-jnp.abs(z)))


def _sb_pair(dot, suffix, qh, kb, vb, r_in, diag):
    z = dot(qh, kb, True) * (HEAD_DIM ** -0.5)
    row = lax.broadcasted_iota(jnp.int32, (BLOCK, BLOCK), 0)
    col = lax.broadcasted_iota(jnp.int32, (BLOCK, BLOCK), 1)
    mask = (col < row) | jnp.logical_not(diag)
    lsp = _log_sigmoid(z)
    log_keep = jnp.where(mask, lsp - z, 0.0)
    log_after = suffix(log_keep) + r_in
    a = jnp.where(mask, jnp.exp(lsp + log_after), 0.0)
    return dot(a, vb, False), r_in + jnp.sum(log_keep, axis=1, keepdims=True)


def sb_fwd(qkv, qcb, kcb, vcb, name):
    s = qkv.shape[0]
    nb = s // BLOCK
    pairs = B_HEADS // 2

    def body(q_ref, k_ref, v_ref, o_ref):
        n = pl.program_id(1)
        for h in range(2):
            sl = slice(h * HEAD_DIM, (h + 1) * HEAD_DIM)
            qh = q_ref[:, sl]

            def cond(c):
                return jnp.logical_and(c[0] >= 0, c[3] > SB_SKIP_LOG)

            def step(c, sl=sl, qh=qh):
                kb, r, acc, _ = c
                rows = pl.ds(pl.multiple_of(kb * BLOCK, BLOCK), BLOCK)
                o_part, r_out = _sb_pair(_plain_dot, _suffix_sum, qh, k_ref[rows, sl], v_ref[rows, sl], r, kb == n)
                return kb - 1, r_out, acc + o_part, jnp.max(r_out)

            init = (n, jnp.zeros((BLOCK, 1), F32), jnp.zeros((BLOCK, HEAD_DIM), F32), jnp.float32(0.0))
            acc = lax.while_loop(cond, step, init)[2]
            o_ref[:, sl] = acc.astype(o_ref.dtype)

    return _pcall(
        body, name=name, grid=(pairs, nb),
        in_specs=[pl.BlockSpec((BLOCK, BLOCK), lambda p, n: (n, qcb + p)),
                  pl.BlockSpec((s, BLOCK), lambda p, n: (0, kcb + p)),
                  pl.BlockSpec((s, BLOCK), lambda p, n: (0, vcb + p))],
        out_specs=pl.BlockSpec((BLOCK, BLOCK), lambda p, n: (n, p)),
        out_shape=jax.ShapeDtypeStruct((s, pairs * BLOCK), BF16),
        compiler_params=_params(),
    )(qkv, qkv, qkv)


def sb_bwd(qkv, qcb, kcb, vcb, do, docb, name):
    s = qkv.shape[0]
    nb = s // BLOCK
    pairs = B_HEADS // 2

    def body(q_ref, k_ref, v_ref, do_ref, dq_ref, dk_ref, dv_ref, r_ref):
        n = pl.program_id(1)

        @pl.when(n == 0)
        def _():
            dk_ref[...] = jnp.zeros(dk_ref.shape, F32)
            dv_ref[...] = jnp.zeros(dv_ref.shape, F32)

        for h in range(2):
            sl = slice(h * HEAD_DIM, (h + 1) * HEAD_DIM)
            qh = q_ref[:, sl]
            doh = do_ref[:, sl].astype(F32)

            def cond(c):
                return jnp.logical_and(c[0] >= 0, c[2] > SB_SKIP_LOG)

            def down(c, sl=sl, qh=qh):
                kb, r, _ = c
                r_ref[kb] = r
                rows = pl.ds(pl.multiple_of(kb * BLOCK, BLOCK), BLOCK)
                z = _dot(qh, k_ref[rows, sl], NT) * (HEAD_DIM ** -0.5)
                row = lax.broadcasted_iota(jnp.int32, (BLOCK, BLOCK), 0)
                col = lax.broadcasted_iota(jnp.int32, (BLOCK, BLOCK), 1)
                mask = (col < row) | (kb != n)
                log_keep = jnp.where(mask, _log_sigmoid(z) - z, 0.0)
                r_out = r + jnp.sum(log_keep, axis=1, keepdims=True)
                return kb - 1, r_out, jnp.max(r_out)

            last = lax.while_loop(cond, down, (n, jnp.zeros((BLOCK, 1), F32), jnp.float32(0.0)))[0] + 1

            def up(kb, c, sl=sl, qh=qh, doh=doh):
                dq, g_r = c
                rows = pl.ds(pl.multiple_of(kb * BLOCK, BLOCK), BLOCK)
                diag = kb == n
                _, vjp = jax.vjp(lambda q_, k_, v_, r_: _sb_pair(_dot_vjp, _suffix_sum_vjp, q_, k_, v_, r_, diag),
                                 qh, k_ref[rows, sl], v_ref[rows, sl], r_ref[kb])
                dq_c, dk_c, dv_c, g_in = vjp((doh, g_r))
                dk_ref[rows, sl] += dk_c
                dv_ref[rows, sl] += dv_c
                return dq + dq_c, g_in

            dq = lax.fori_loop(last, n + 1, up, (jnp.zeros((BLOCK, HEAD_DIM), F32), jnp.zeros((BLOCK, 1), F32)))[0]
            dq_ref[:, sl] = dq

    wide = jax.ShapeDtypeStruct((s, pairs * BLOCK), F32)
    return _pcall(
        body, name=name, grid=(pairs, nb),
        in_specs=[pl.BlockSpec((BLOCK, BLOCK), lambda p, n: (n, qcb + p)),
                  pl.BlockSpec((s, BLOCK), lambda p, n: (0, kcb + p)),
                  pl.BlockSpec((s, BLOCK), lambda p, n: (0, vcb + p)),
                  pl.BlockSpec((BLOCK, BLOCK), lambda p, n: (n, docb + p))],
        out_specs=[pl.BlockSpec((BLOCK, BLOCK), lambda p, n: (n, p)),
                   pl.BlockSpec((s, BLOCK), lambda p, n: (0, p)),
                   pl.BlockSpec((s, BLOCK), lambda p, n: (0, p))],
        out_shape=[wide, wide, wide],
        scratch_shapes=[pltpu.VMEM((nb, BLOCK, 1), F32)],
        compiler_params=_params(),
    )(qkv, qkv, qkv, do)


def _xa_tile(dot, q, kv, qg, kg):
    hd = q.shape[1] // X_HEADS
    outs = []
    for h in range(X_HEADS):
        qh = _rms(q[:, h * hd:(h + 1) * hd], qg)
        kh = _rms(kv[:, h * hd:(h + 1) * hd], kg)
        vh = kv[:, (X_HEADS + h) * hd:(X_HEADS + h + 1) * hd]
        sc = dot(qh, kh, True) * (hd ** -0.5)
        m = lax.stop_gradient(jnp.max(sc, axis=-1, keepdims=True))
        p = jnp.exp(sc - m)
        outs.append(dot(p / jnp.sum(p, axis=-1, keepdims=True), vh, False))
    return jnp.concatenate(outs, axis=1)


def xa_core_fwd(q, kv, qg, kg, name):
    n, d = q.shape
    tm = _tile(n, 256, 8)
    (o,) = tcall(lambda ids, qt, kvt, qgt, kgt: (_xa_tile(_plain_dot, qt, kvt, qgt, kgt),), (n // tm,),
                 [_row(q, tm), _full(kv), _full(qg), _full(kg)], [_row_out(n, d, BF16, tm)], name)
    return o


def xa_core_bwd(q, kv, qg, kg, do, name):
    n, d = q.shape
    tm = _tile(n, 256, 8)

    def fn(ids, qt, kvt, qgt, kgt, dot_):
        _, vjp = jax.vjp(functools.partial(_xa_tile, _dot_vjp), qt, kvt, qgt, kgt)
        return vjp(dot_.astype(F32))

    return tcall(fn, (n // tm,), [_row(q, tm), _full(kv), _full(qg), _full(kg), _row(do, tm)],
                 [_row_out(n, d, BF16, tm), _acc_out(kv.shape), _acc_out(qg.shape), _acc_out(kg.shape)], name)


_EV_ORDER = np.concatenate([np.arange(0, 512), np.arange(768, 2304), np.arange(512, 768)])
_EV_INV = np.argsort(_EV_ORDER)
_A_CFG = dict(hkv=A_KV_HEADS, grp=A_Q_HEADS // A_KV_HEADS, max_dist=BLOCK - 1, step=1.0, slopes=_alibi(A_Q_HEADS),
              want_lse=False)
_A_COLS = (lambda r: 0, lambda r: 16, lambda r: 17)


def even_mixer_fwd(x, g, w_in, qg, kg, sinks, w_out, tag):
    h = rmsnorm_fwd(x, g, tag + "_norm")
    qkv = mm(h, w_in, "nn", tag + "_in")
    (o_a,) = banded_fwd(qkv, 1, _A_COLS, qg, kg, sinks, _A_CFG, tag + "_swa")
    o_b = sb_fwd(qkv, 4, 8, 12, tag + "_sb")
    o = jnp.concatenate([o_a, o_b], axis=1)
    y = mm(o, w_out, "nn", tag + "_out", res=x)
    return y, (x, h, qkv, o)


def even_mixer_bwd(dy, saved, g, w_in, qg, kg, sinks, w_out, tag):
    x, h, qkv, o = saved
    n = x.shape[0]
    do = mm(dy, w_out, "nt", tag + "_do")
    d_wout = mm(o, dy, "tn", tag + "_dwout")
    dqa, dka, dva, dqg, dkg, dsinks = banded_bwd(qkv, 1, _A_COLS, qg, kg, sinks, _A_CFG, [(do, lambda r: 0)],
                                                  tag + "_dswa")
    dqb, dkb, dvb = sb_bwd(qkv, 4, 8, 12, do, 4, tag + "_dsb")
    tm = _tile(n, 256, 8)
    (dqkv,) = tcall(lambda ids, *t: (jnp.concatenate(t, axis=1),), (n // tm,),
                    [_row(a, tm) for a in (dqa, dqb, dkb, dvb, dka, dva)],
                    [_row_out(n, qkv.shape[1], BF16, tm)], tag + "_dqkv")
    dh = mm(dqkv, w_in, "nt", tag + "_dh")
    d_win = mm(h, dqkv, "tn", tag + "_dwin")
    dx, dg = rmsnorm_bwd(x, g, dh, dy, tag + "_dnorm")
    return dx, dg, d_win, dqg, dkg, dsinks, d_wout


def _c_cfg(window, dil):
    return dict(hkv=C_HEADS, grp=1, max_dist=window // dil, step=float(dil), slopes=_alibi(C_HEADS), want_lse=True)


_C_COLS = (lambda r: 3 * r, lambda r: 3 * r + 1, lambda r: 3 * r + 2)


def _combine(o1, o2, o3, l1, l2, l3):
    m = lax.stop_gradient(jnp.maximum(jnp.maximum(l1, l2), l3))
    e1, e2, e3 = jnp.exp(l1 - m), jnp.exp(l2 - m), jnp.exp(l3 - m)
    tot = e1 + e2 + e3
    return (e1 / tot) * o1 + (e2 / tot) * o2 + (e3 / tot) * o3


def odd_mixer_fwd(x, g, w_in, qg, kg, w_out, tag):
    n, d = x.shape
    h = rmsnorm_fwd(x, g, tag + "_norm")
    qkv = mm(h, w_in, "nn", tag + "_in")
    w3 = qkv.shape[1]
    os_, ls_ = [], []
    for window, dil in C_PATTERNS:
        o_p, l_p = banded_fwd(qkv.reshape(n // dil, dil * w3), dil, _C_COLS, qg, kg, None, _c_cfg(window, dil),
                              f"{tag}_dil{dil}")
        os_.append(o_p.reshape(n, d))
        ls_.append(l_p.reshape(n, d))
    tm = _tile(n, 128, 8)
    (o,) = tcall(lambda ids, *t: (_combine(*[a.astype(F32) for a in t]),), (n // tm,),
                 [_row(a, tm) for a in os_ + ls_], [_row_out(n, d, BF16, tm)], tag + "_comb")
    y = mm(o, w_out, "nn", tag + "_out", res=x)
    return y, (x, h, qkv, os_, ls_, o)


def odd_mixer_bwd(dy, saved, g, w_in, qg, kg, w_out, tag):
    x, h, qkv, os_, ls_, o = saved
    n, d = x.shape
    w3 = qkv.shape[1]
    do = mm(dy, w_out, "nt", tag + "_do")
    d_wout = mm(o, dy, "tn", tag + "_dwout")
    tm = _tile(n, 128, 8)

    def comb_bwd(ids, *t):
        _, vjp = jax.vjp(_combine, *[a.astype(F32) for a in t[:6]])
        return vjp(t[6])

    cts = tcall(comb_bwd, (n // tm,), [_row(a, tm) for a in os_ + ls_ + [do]],
                [_row_out(n, d, F32, tm) for _ in range(6)], tag + "_dcomb")
    dqs, dks, dvs = [], [], []
    dqg = dkg = None
    for p, (window, dil) in enumerate(C_PATTERNS):
        lay = (n // dil, dil * d)
        dq, dk, dv, dqg_p, dkg_p = banded_bwd(
            qkv.reshape(n // dil, dil * w3), dil, _C_COLS, qg, kg, None, _c_cfg(window, dil),
            [(cts[p].reshape(lay), lambda r: r), (cts[3 + p].reshape(lay), lambda r: r)], f"{tag}_ddil{dil}")
        dqs.append(dq.reshape(n, d))
        dks.append(dk.reshape(n, d))
        dvs.append(dv.reshape(n, d))
        dqg = dqg_p if dqg is None else dqg + dqg_p
        dkg = dkg_p if dkg is None else dkg + dkg_p
    tm = _tile(n, 256, 8)

    def gather(ids, *t):
        return (jnp.concatenate([t[0] + t[1] + t[2], t[3] + t[4] + t[5], t[6] + t[7] + t[8]], axis=1),)

    (dqkv,) = tcall(gather, (n // tm,), [_row(a, tm) for a in dqs + dks + dvs], [_row_out(n, w3, BF16, tm)],
                    tag + "_dqkv")
    dh = mm(dqkv, w_in, "nt", tag + "_dh")
    d_win = mm(h, dqkv, "tn", tag + "_dwin")
    dx, dg = rmsnorm_bwd(x, g, dh, dy, tag + "_dnorm")
    return dx, dg, d_win, dqg, dkg, d_wout


def xa_fwd(x, mem, g, gm, w_q, w_kv, qg, kg, w_o, tag):
    h = rmsnorm_fwd(x, g, tag + "_norm")
    q = mm(h, w_q, "nn", tag + "_q")
    mn = rmsnorm_fwd(mem, gm, tag + "_mnorm")
    kv = mm(mn, w_kv, "nn", tag + "_kv")
    o = xa_core_fwd(q, kv, qg, kg, tag + "_core")
    y = mm(o, w_o, "nn", tag + "_o", res=x)
    return y, (x, h, q, mn, kv, o)


def xa_bwd(dy, saved, mem, g, gm, w_q, w_kv, qg, kg, w_o, tag):
    x, h, q, mn, kv, o = saved
    do = mm(dy, w_o, "nt", tag + "_do", out_dtype=BF16)
    d_wo = mm(o, dy, "tn", tag + "_dwo")
    dq, dkv, dqg, dkg = xa_core_bwd(q, kv, qg, kg, do, tag + "_dcore")
    dh = mm(dq, w_q, "nt", tag + "_dh")
    d_wq = mm(h, dq, "tn", tag + "_dwq")
    dx, dg = rmsnorm_bwd(x, g, dh, dy, tag + "_dnorm")
    dmn = mm(dkv, w_kv, "nt", tag + "_dmn")
    d_wkv = mm(mn, dkv, "tn", tag + "_dwkv")
    _, dgm = rmsnorm_bwd(mem, gm, dmn, None, tag + "_dmnorm")
    return dx, dg, dgm, d_wq, d_wkv, dqg, dkg, d_wo


def loss_head(y, target, name):
    n, d = y.shape
    tm = _tile(n, 512, 8)

    def fn(ids, yt, tt):
        e = yt - tt
        return e * (1.0 / d), jnp.sum(e * e, axis=0, keepdims=True)

    return tcall(fn, (n // tm,), [_row(y, tm), _row(target, tm)], [_row_out(n, d, F32, tm), _acc_out((1, d))], name)


_ANY = pl.BlockSpec(memory_space=pl.ANY)


def all_gather_blocks(block):
    def body(x_ref, out_ref, send_sems, recv_sems, local_sem):
        x, y, c = lax.axis_index("x"), lax.axis_index("y"), lax.axis_index("c")
        me, sibling = (x, y, c), (x, y, 1 - c)
        chips = [(1 - x, y), (x, 1 - y), (1 - x, 1 - y)]

        def slot(px, py, pc):
            return out_ref.at[4 * px + 2 * py + pc]

        def copy(k, blk, to, src=None):
            return pltpu.make_async_remote_copy(
                src_ref=slot(*blk) if src is None else src, dst_ref=slot(*blk),
                send_sem=send_sems.at[k], recv_sem=recv_sems.at[k], device_id=to, device_id_type=MESH)

        mine = pltpu.make_async_copy(x_ref, slot(*me), local_sem)
        mine.start()
        first = [copy(0, me, sibling, src=x_ref)]
        first += [copy(1 + j, me, (*chip, c), src=x_ref) for j, chip in enumerate(chips)]
        for cp in first:
            cp.start()
        passed = [copy(4 + j, (*chip, c), sibling) for j, chip in enumerate(chips)]
        for j, chip in enumerate(chips):
            copy(1 + j, (*chip, c), me).wait_recv()
            passed[j].start()
        copy(0, sibling, me).wait_recv()
        for j, chip in enumerate(chips):
            copy(4 + j, (*chip, 1 - c), me).wait_recv()
        for cp in first + passed:
            cp.wait_send()
        mine.wait()

    return _pcall(
        body, name="weights_all_gather",
        in_specs=[_ANY], out_specs=_ANY,
        out_shape=jax.ShapeDtypeStruct((N_DEV,) + block.shape, block.dtype),
        scratch_shapes=[pltpu.SemaphoreType.DMA((7,)), pltpu.SemaphoreType.DMA((7,)), pltpu.SemaphoreType.DMA],
    )(block)


def pair_exchange(g):
    def body(g_ref, out_ref, send_sems, recv_sems):
        x, y, c = lax.axis_index("x"), lax.axis_index("y"), lax.axis_index("c")
        copies = []
        for j in range(4):
            cp = pltpu.make_async_remote_copy(
                src_ref=g_ref.at[2 * j + (1 - c)], dst_ref=out_ref.at[j],
                send_sem=send_sems.at[j], recv_sem=recv_sems.at[j], device_id=(x, y, 1 - c), device_id_type=MESH)
            cp.start()
            copies.append(cp)
        for cp in copies:
            cp.wait()

    return _pcall(
        body, name="grads_pair_exchange",
        in_specs=[_ANY], out_specs=_ANY,
        out_shape=jax.ShapeDtypeStruct((4,) + g.shape[1:], g.dtype),
        scratch_shapes=[pltpu.SemaphoreType.DMA((4,)), pltpu.SemaphoreType.DMA((4,))],
    )(g)


def pair_sum(g, got, c):
    r = g.shape[1]
    tr = _tile(r, 512, 8)

    def body(c_ref, a_ref, b_ref, o_ref):
        o_ref[...] = a_ref[...] + b_ref[...]

    return _pcall(
        body, name="grads_pair_sum",
        grid_spec=pltpu.PrefetchScalarGridSpec(
            num_scalar_prefetch=1, grid=(4, r // tr),
            in_specs=[pl.BlockSpec((None, tr, LANES), lambda j, i, c_ref: (2 * j + c_ref[0], i, 0)),
                      pl.BlockSpec((None, tr, LANES), lambda j, i, c_ref: (j, i, 0))],
            out_specs=pl.BlockSpec((None, tr, LANES), lambda j, i, c_ref: (j, i, 0))),
        out_shape=jax.ShapeDtypeStruct((4,) + g.shape[1:], F32),
        compiler_params=_params(),
    )(c, g, got)


def chip_exchange(p):
    def body(p_ref, out_ref, send_sems, recv_sems, local_sem):
        x, y, c = lax.axis_index("x"), lax.axis_index("y"), lax.axis_index("c")
        my_chip = 2 * x + y
        mine = pltpu.make_async_copy(p_ref.at[my_chip], out_ref.at[my_chip], local_sem)
        mine.start()
        copies = []
        for k, (tx, ty) in enumerate([(1 - x, y), (x, 1 - y), (1 - x, 1 - y)]):
            cp = pltpu.make_async_remote_copy(
                src_ref=p_ref.at[2 * tx + ty], dst_ref=out_ref.at[my_chip],
                send_sem=send_sems.at[k], recv_sem=recv_sems.at[k], device_id=(tx, ty, c), device_id_type=MESH)
            cp.start()
            copies.append(cp)
        for cp in copies:
            cp.wait()
        mine.wait()

    return _pcall(
        body, name="grads_chip_exchange",
        in_specs=[_ANY], out_specs=_ANY,
        out_shape=jax.ShapeDtypeStruct(p.shape, p.dtype),
        scratch_shapes=[pltpu.SemaphoreType.DMA((3,)), pltpu.SemaphoreType.DMA((3,)), pltpu.SemaphoreType.DMA],
    )(p)


def chip_sum(parts):
    r = parts.shape[1]
    tr = _tile(r, 512, 8)
    spec = lambda j: _in(parts, (None, tr, LANES), lambda i, j=j: (j, i, 0))
    (out,) = tcall(lambda ids, a, b, c_, d: (((a + b) + c_) + d,), (r // tr,), [spec(j) for j in range(4)],
                   [_out((r, LANES), F32, (tr, LANES), lambda i: (i, 0))], "grads_chip_sum")
    return out


def adamw(w, g, m, v, name):
    shape = w.shape
    cols = shape[-1]
    rows = int(np.prod(shape[:-1]))
    w2, g2, m2, v2 = [a.reshape(rows, cols) for a in (w, g, m, v)]
    tr = _tile(rows, 256, 8) if rows % 8 == 0 else rows

    def fn(ids, wt, gt, mt, vt):
        m_new = ADAM_B1 * mt + (1.0 - ADAM_B1) * gt
        v_new = ADAM_B2 * vt + (1.0 - ADAM_B2) * (gt * gt)
        m_hat = m_new / (1.0 - ADAM_B1 ** ADAM_STEP)
        v_hat = v_new / (1.0 - ADAM_B2 ** ADAM_STEP)
        delta = -ADAM_LR * (m_hat / (jnp.sqrt(v_hat) + ADAM_EPS) + ADAM_WD * wt)
        return delta, m_new, v_new

    res = tcall(fn, (rows // tr,), [_row(a, tr) for a in (w2, g2, m2, v2)],
                [_row_out(rows, cols, F32, tr) for _ in range(3)], name)
    return [a.reshape(shape) for a in res]


_MATS = [("ffn1_w_gu", "col"), ("ffn1_w_down", "row"), ("ev_w_in", "col"), ("ev_w_out", "row"),
         ("od_w_in", "col"), ("od_w_out", "row"), ("xa_w_q", "row"), ("xa_w_kv", "col"), ("xa_w_o", "row"),
         ("ffn2_w_gu", "col"), ("ffn2_w_down", "row")]
_VECS = ["ffn1_norm", "mix_norm", "ev_q_gain", "ev_k_gain", "ev_sinks", "od_q_gain", "od_k_gain", "xa_norm",
         "xa_mem_norm", "xa_q_gain", "xa_k_gain", "ffn2_norm"]
_WEIGHTS = ["ffn1_norm", "ffn1_w_gu", "ffn1_w_down", "mix_norm", "ev_w_in", "ev_q_gain", "ev_k_gain", "ev_sinks",
            "ev_w_out", "od_w_in", "od_q_gain", "od_k_gain", "od_w_out", "xa_norm", "xa_mem_norm", "xa_w_q", "xa_w_kv",
            "xa_q_gain", "xa_k_gain", "xa_w_o", "ffn2_norm", "ffn2_w_gu", "ffn2_w_down"]


def _gather_weights(shards):
    pieces = [shards[n].astype(BF16).reshape(-1, LANES) for n, _ in _MATS]
    packed = jnp.concatenate(pieces, axis=0)
    got = all_gather_blocks(packed)
    full, off = {}, 0
    for (n, axis), piece in zip(_MATS, pieces):
        l, a, b = shards[n].shape
        seg = got[:, off:off + piece.shape[0], :].reshape(N_DEV, l, a, b)
        off += piece.shape[0]
        if axis == "row":
            full[n] = seg.transpose(1, 0, 2, 3).reshape(l, N_DEV * a, b)
        else:
            full[n] = seg.transpose(1, 2, 0, 3).reshape(l, a, N_DEV * b)
    return full


def _reduce_gradients(mats, vecs, c):
    rows = []
    for n, axis in _MATS:
        gr = mats[n]
        l, a, b = gr.shape
        if axis == "row":
            rows.append(gr.reshape(l, N_DEV, a // N_DEV, b).transpose(1, 0, 2, 3).reshape(N_DEV, -1))
        else:
            rows.append(gr.reshape(l, a, N_DEV, b // N_DEV).transpose(2, 0, 1, 3).reshape(N_DEV, -1))
    vec = jnp.concatenate([vecs[n].reshape(-1) for n in _VECS])
    total = sum(r.shape[1] for r in rows) + vec.shape[0]
    padded = -(-total // (8 * LANES)) * (8 * LANES)
    vec = jnp.pad(vec, (0, padded - total))
    buf = jnp.concatenate(rows + [jnp.broadcast_to(vec[None], (N_DEV, vec.shape[0]))], axis=1)
    buf = buf.reshape(N_DEV, padded // LANES, LANES)
    part = pair_sum(buf, pair_exchange(buf), c)
    summed = chip_sum(chip_exchange(part)).reshape(-1)
    out, off = {}, 0
    for (n, axis), r in zip(_MATS, rows):
        l, a, b = mats[n].shape
        shape = (l, a // N_DEV, b) if axis == "row" else (l, a, b // N_DEV)
        out[n] = summed[off:off + r.shape[1]].reshape(shape)
        off += r.shape[1]
    for n in _VECS:
        out[n] = summed[off:off + vecs[n].size].reshape(vecs[n].shape)
        off += vecs[n].size
    return out


def _local_step(x, mem, target, w, full):
    depth = w["ffn1_norm"].shape[0]
    ev_in = full["ev_w_in"][:, :, _EV_ORDER]
    row = lambda a, l: a[l:l + 1]
    saved = []
    for l in range(depth):
        t = f"l{l}"
        j = l // 2
        x, s1 = ffn_fwd(x, row(w["ffn1_norm"], l), full["ffn1_w_gu"][l], full["ffn1_w_down"][l], t + "_ffn1")
        if l % 2 == 0:
            x, s2 = even_mixer_fwd(x, row(w["mix_norm"], l), ev_in[j], row(w["ev_q_gain"], j), row(w["ev_k_gain"], j),
                                   row(w["ev_sinks"], j), full["ev_w_out"][j], t + "_ev")
        else:
            x, s2 = odd_mixer_fwd(x, row(w["mix_norm"], l), full["od_w_in"][j], row(w["od_q_gain"], j),
                                  row(w["od_k_gain"], j), full["od_w_out"][j], t + "_od")
        x, s3 = xa_fwd(x, mem, row(w["xa_norm"], l), row(w["xa_mem_norm"], l), full["xa_w_q"][l], full["xa_w_kv"][l],
                       row(w["xa_q_gain"], l), row(w["xa_k_gain"], l), full["xa_w_o"][l], t + "_xa")
        x, s4 = ffn_fwd(x, row(w["ffn2_norm"], l), full["ffn2_w_gu"][l], full["ffn2_w_down"][l], t + "_ffn2")
        saved.append((s1, s2, s3, s4))
    dx, sq = loss_head(x, target, "loss_head")
    loss = 0.5 * jnp.sum(sq) / x.shape[1]

    gm = {n: [None] * full[n].shape[0] for n, _ in _MATS}
    gv = {n: [None] * w[n].shape[0] for n in _VECS}
    for l in reversed(range(depth)):
        t = f"l{l}"
        j = l // 2
        s1, s2, s3, s4 = saved[l]
        dx, gv["ffn2_norm"][l], gm["ffn2_w_gu"][l], gm["ffn2_w_down"][l] = ffn_bwd(
            dx, s4, row(w["ffn2_norm"], l), full["ffn2_w_gu"][l], full["ffn2_w_down"][l], t + "_ffn2")
        (dx, gv["xa_norm"][l], gv["xa_mem_norm"][l], gm["xa_w_q"][l], gm["xa_w_kv"][l], gv["xa_q_gain"][l],
         gv["xa_k_gain"][l], gm["xa_w_o"][l]) = xa_bwd(
            dx, s3, mem, row(w["xa_norm"], l), row(w["xa_mem_norm"], l), full["xa_w_q"][l], full["xa_w_kv"][l],
            row(w["xa_q_gain"], l), row(w["xa_k_gain"], l), full["xa_w_o"][l], t + "_xa")
        if l % 2 == 0:
            (dx, gv["mix_norm"][l], d_win, gv["ev_q_gain"][j], gv["ev_k_gain"][j], gv["ev_sinks"][j],
             gm["ev_w_out"][j]) = even_mixer_bwd(
                dx, s2, row(w["mix_norm"], l), ev_in[j], row(w["ev_q_gain"], j), row(w["ev_k_gain"], j),
                row(w["ev_sinks"], j), full["ev_w_out"][j], t + "_ev")
            gm["ev_w_in"][j] = d_win[:, _EV_INV]
        else:
            (dx, gv["mix_norm"][l], gm["od_w_in"][j], gv["od_q_gain"][j], gv["od_k_gain"][j],
             gm["od_w_out"][j]) = odd_mixer_bwd(
                dx, s2, row(w["mix_norm"], l), full["od_w_in"][j], row(w["od_q_gain"], j), row(w["od_k_gain"], j),
                full["od_w_out"][j], t + "_od")
        dx, gv["ffn1_norm"][l], gm["ffn1_w_gu"][l], gm["ffn1_w_down"][l] = ffn_bwd(
            dx, s1, row(w["ffn1_norm"], l), full["ffn1_w_gu"][l], full["ffn1_w_down"][l], t + "_ffn1")
    mats = {n: jnp.stack(v) for n, v in gm.items()}
    vecs = {n: jnp.concatenate(v, axis=0) for n, v in gv.items()}
    return loss, dx, mats, vecs


def kernel(x, mem, ffn1_norm, ffn1_w_gu, ffn1_w_down, mix_norm, ev_w_in, ev_q_gain, ev_k_gain, ev_sinks, ev_w_out, od_w_in, od_q_gain, od_k_gain, od_w_out, xa_norm, xa_mem_norm, xa_w_q, xa_w_kv, xa_q_gain, xa_k_gain, xa_w_o, ffn2_norm, ffn2_w_gu, ffn2_w_down, loss_target, m_ffn1_norm, m_ffn1_w_gu, m_ffn1_w_down, m_mix_norm, m_ev_w_in, m_ev_q_gain, m_ev_k_gain, m_ev_sinks, m_ev_w_out, m_od_w_in, m_od_q_gain, m_od_k_gain, m_od_w_out, m_xa_norm, m_xa_mem_norm, m_xa_w_q, m_xa_w_kv, m_xa_q_gain, m_xa_k_gain, m_xa_w_o, m_ffn2_norm, m_ffn2_w_gu, m_ffn2_w_down, v_ffn1_norm, v_ffn1_w_gu, v_ffn1_w_down, v_mix_norm, v_ev_w_in, v_ev_q_gain, v_ev_k_gain, v_ev_sinks, v_ev_w_out, v_od_w_in, v_od_q_gain, v_od_k_gain, v_od_w_out, v_xa_norm, v_xa_mem_norm, v_xa_w_q, v_xa_w_kv, v_xa_q_gain, v_xa_k_gain, v_xa_w_o, v_ffn2_norm, v_ffn2_w_gu, v_ffn2_w_down):
    w = dict(ffn1_norm=ffn1_norm, ffn1_w_gu=ffn1_w_gu, ffn1_w_down=ffn1_w_down, mix_norm=mix_norm, ev_w_in=ev_w_in, ev_q_gain=ev_q_gain, ev_k_gain=ev_k_gain, ev_sinks=ev_sinks, ev_w_out=ev_w_out, od_w_in=od_w_in, od_q_gain=od_q_gain, od_k_gain=od_k_gain, od_w_out=od_w_out, xa_norm=xa_norm, xa_mem_norm=xa_mem_norm, xa_w_q=xa_w_q, xa_w_kv=xa_w_kv, xa_q_gain=xa_q_gain, xa_k_gain=xa_k_gain, xa_w_o=xa_w_o, ffn2_norm=ffn2_norm, ffn2_w_gu=ffn2_w_gu, ffn2_w_down=ffn2_w_down)
    m = dict(ffn1_norm=m_ffn1_norm, ffn1_w_gu=m_ffn1_w_gu, ffn1_w_down=m_ffn1_w_down, mix_norm=m_mix_norm, ev_w_in=m_ev_w_in, ev_q_gain=m_ev_q_gain, ev_k_gain=m_ev_k_gain, ev_sinks=m_ev_sinks, ev_w_out=m_ev_w_out, od_w_in=m_od_w_in, od_q_gain=m_od_q_gain, od_k_gain=m_od_k_gain, od_w_out=m_od_w_out, xa_norm=m_xa_norm, xa_mem_norm=m_xa_mem_norm, xa_w_q=m_xa_w_q, xa_w_kv=m_xa_w_kv, xa_q_gain=m_xa_q_gain, xa_k_gain=m_xa_k_gain, xa_w_o=m_xa_w_o, ffn2_norm=m_ffn2_norm, ffn2_w_gu=m_ffn2_w_gu, ffn2_w_down=m_ffn2_w_down)
    v = dict(ffn1_norm=v_ffn1_norm, ffn1_w_gu=v_ffn1_w_gu, ffn1_w_down=v_ffn1_w_down, mix_norm=v_mix_norm, ev_w_in=v_ev_w_in, ev_q_gain=v_ev_q_gain, ev_k_gain=v_ev_k_gain, ev_sinks=v_ev_sinks, ev_w_out=v_ev_w_out, od_w_in=v_od_w_in, od_q_gain=v_od_q_gain, od_k_gain=v_od_k_gain, od_w_out=v_od_w_out, xa_norm=v_xa_norm, xa_mem_norm=v_xa_mem_norm, xa_w_q=v_xa_w_q, xa_w_kv=v_xa_w_kv, xa_q_gain=v_xa_q_gain, xa_k_gain=v_xa_k_gain, xa_w_o=v_xa_w_o, ffn2_norm=v_ffn2_norm, ffn2_w_gu=v_ffn2_w_gu, ffn2_w_down=v_ffn2_w_down)

    full = _gather_weights(w)
    loss, dx, mats, vecs = _local_step(x[0], mem[0], loss_target[0], w, full)
    c = lax.axis_index("c").astype(jnp.int32).reshape(1)
    grads = _reduce_gradients(mats, vecs, c)
    loss = lax.psum(loss, ("x", "y", "c"))

    delta, new_m, new_v = {}, {}, {}
    for n in _WEIGHTS:
        delta[n], new_m[n], new_v[n] = adamw(w[n], grads[n], m[n], v[n], "adamw_" + n)
    return (loss, dx[None], *[grads[n] for n in _WEIGHTS], *[delta[n] for n in _WEIGHTS],
            *[new_m[n] for n in _WEIGHTS], *[new_v[n] for n in _WEIGHTS])
```

```python
import functools

import numpy as np
import jax
import jax.numpy as jnp
from jax import lax
from jax.experimental import pallas as pl
from jax.experimental.pallas import tpu as pltpu

F32 = jnp.float32
BF16 = jnp.bfloat16
MESH = pl.DeviceIdType.MESH

HEAD_DIM = 64
BLOCK = 128
RMS_EPS = 1e-6
A_Q_HEADS, A_KV_HEADS = 8, 2
B_HEADS = 8
C_HEADS = 16
C_PATTERNS = ((128, 1), (512, 4), (2048, 16))
X_HEADS = 4
N_DEV = 8
LANES = 1024
VMEM_LIMIT_BYTES = 56 * 1024 * 1024
SB_SKIP_LOG = -110.0
NEG_BIG = -1e30

ADAM_LR, ADAM_B1, ADAM_B2, ADAM_EPS, ADAM_WD, ADAM_STEP = 0.001, 0.9, 0.999, 1e-08, 0.01, 10

NN = (((1,), (0,)), ((), ()))
NT = (((1,), (1,)), ((), ()))
TN = (((0,), (0,)), ((), ()))


def _pcall(body, **kw):
    return pl.pallas_call(body, **kw)


def _params(**kw):
    return pltpu.CompilerParams(vmem_limit_bytes=VMEM_LIMIT_BYTES, **kw)


def _tile(dim, cap, unit=128):
    if dim <= cap:
        return dim
    t = (cap // unit) * unit
    while t >= unit:
        if dim % t == 0:
            return t
        t -= unit
    raise ValueError(f"no tile for {dim} under {cap}")


def _dot(a, b, dims):
    return lax.dot_general(a.astype(BF16), b.astype(BF16), dims, preferred_element_type=F32)


@functools.partial(jax.custom_vjp, nondiff_argnums=(2,))
def _dot_vjp(a, b, nt):
    return _dot(a, b, NT if nt else NN)


def _dot_vjp_fwd(a, b, nt):
    return _dot(a, b, NT if nt else NN), (a.astype(BF16), b.astype(BF16))


def _dot_vjp_bwd(nt, res, g):
    a, b = res
    if nt:
        return _dot(g, b, NN), _dot(g, a, TN)
    return _dot(g, b, NT), _dot(a, g, TN)


_dot_vjp.defvjp(_dot_vjp_fwd, _dot_vjp_bwd)


def _plain_dot(a, b, nt):
    return _dot(a, b, NT if nt else NN)


def _split_dot(x, mat, terms=2):
    out, rem = None, x
    for t in range(terms):
        part = rem.astype(BF16)
        d = lax.dot_general(part, mat, NN, preferred_element_type=F32)
        out = d if out is None else out + d
        if t + 1 < terms:
            rem = rem - part.astype(F32)
    return out


@functools.partial(jax.custom_vjp, nondiff_argnums=(3,))
def _split_dot_vjp(x, mat, mat_t, terms):
    return _split_dot(x, mat, terms)


def _split_dot_vjp_fwd(x, mat, mat_t, terms):
    return _split_dot(x, mat, terms), mat_t


def _split_dot_vjp_bwd(terms, mat_t, g):
    return _split_dot(g, mat_t, terms), None, None


_split_dot_vjp.defvjp(_split_dot_vjp_fwd, _split_dot_vjp_bwd)


def _plain_split(x, mat, mat_t, terms):
    return _split_dot(x, mat, terms)


def _tri(after):
    j = lax.broadcasted_iota(jnp.int32, (BLOCK, BLOCK), 0)
    s = lax.broadcasted_iota(jnp.int32, (BLOCK, BLOCK), 1)
    return jnp.where(j > s if after else j < s, 1.0, 0.0).astype(BF16)


def _suffix_sum(x):
    return _split_dot(x, _tri(True))


@jax.custom_vjp
def _suffix_sum_vjp(x):
    return _suffix_sum(x)


def _suffix_sum_vjp_fwd(x):
    return _suffix_sum(x), None


def _suffix_sum_vjp_bwd(_, g):
    return (_split_dot(g, _tri(False)),)


_suffix_sum_vjp.defvjp(_suffix_sum_vjp_fwd, _suffix_sum_vjp_bwd)


def _in(a, block, imap):
    return (a, block, imap)


def _out(shape, dtype, block, imap, acc=False):
    return (shape, dtype, block, imap, acc)


def tcall(fn, grid, ins, outs, name):
    nin = len(ins)
    ngrid = len(grid)

    def body(*refs):
        ids = tuple(pl.program_id(a) for a in range(ngrid))
        res = fn(ids, *[r[...] for r in refs[:nin]])
        first = ids[0] == 0
        for a in range(1, ngrid):
            first = jnp.logical_and(first, ids[a] == 0)
        for o_ref, r, spec in zip(refs[nin:], res, outs):
            if spec[4]:
                @pl.when(first)
                def _(o_ref=o_ref):
                    o_ref[...] = jnp.zeros(o_ref.shape, o_ref.dtype)
                o_ref[...] += r.astype(o_ref.dtype)
            else:
                o_ref[...] = r.astype(o_ref.dtype)

    return _pcall(
        body, name=name, grid=grid,
        in_specs=[pl.BlockSpec(b, m) for (_, b, m) in ins],
        out_specs=[pl.BlockSpec(b, m) for (_, _, b, m, _) in outs],
        out_shape=[jax.ShapeDtypeStruct(s, d) for (s, d, _, _, _) in outs],
        compiler_params=_params(),
    )(*[a for (a, _, _) in ins])


def _row(a, tm, width=None, cb=0):
    width = a.shape[1] if width is None else width
    return _in(a, (tm, width), lambda i, cb=cb: (i, cb))


def _full(a):
    zeros = (0,) * a.ndim
    return _in(a, a.shape, lambda *ids: zeros)


def _row_out(n, width, dtype, tm):
    return _out((n, width), dtype, (tm, width), lambda i: (i, 0))


def _acc_out(shape):
    zeros = (0,) * len(shape)
    return _out(shape, F32, shape, lambda *ids: zeros, acc=True)


def mm(a, b, mode, name, *, out_dtype=F32, scale=1.0, res=None):
    if mode == "nn":
        (m, k), (k2, n) = a.shape, b.shape
    elif mode == "nt":
        (m, k), (n, k2) = a.shape, b.shape
    else:
        (k, m), (k2, n) = a.shape, b.shape
    assert k == k2, (a.shape, b.shape, mode)
    tm, tn, tk = _tile(m, 512), _tile(n, 1408), _tile(k, 1408)
    nk = k // tk
    dims = {"nn": NN, "nt": NT, "tn": TN}[mode]
    has_res = res is not None

    def body(*refs):
        if has_res:
            a_ref, b_ref, r_ref, o_ref, acc_ref = refs
        else:
            a_ref, b_ref, o_ref, acc_ref = refs
        kk = pl.program_id(2)

        @pl.when(kk == 0)
        def _():
            acc_ref[...] = jnp.zeros(acc_ref.shape, F32)

        acc_ref[...] += _dot(a_ref[...], b_ref[...], dims)

        @pl.when(kk == nk - 1)
        def _():
            out = acc_ref[...]
            if scale != 1.0:
                out = out * scale
            if has_res:
                out = out + r_ref[...]
            o_ref[...] = out.astype(o_ref.dtype)

    a_spec = (pl.BlockSpec((tk, tm), lambda i, j, kk: (kk, i)) if mode == "tn"
              else pl.BlockSpec((tm, tk), lambda i, j, kk: (i, kk)))
    b_spec = (pl.BlockSpec((tn, tk), lambda i, j, kk: (j, kk)) if mode == "nt"
              else pl.BlockSpec((tk, tn), lambda i, j, kk: (kk, j)))
    in_specs = [a_spec, b_spec]
    args = [a, b]
    if has_res:
        in_specs.append(pl.BlockSpec((tm, tn), lambda i, j, kk: (i, j)))
        args.append(res)
    return _pcall(
        body, name=name, grid=(m // tm, n // tn, nk),
        in_specs=in_specs,
        out_specs=pl.BlockSpec((tm, tn), lambda i, j, kk: (i, j)),
        out_shape=jax.ShapeDtypeStruct((m, n), out_dtype),
        scratch_shapes=[pltpu.VMEM((tm, tn), F32)],
        compiler_params=_params(dimension_semantics=("parallel", "parallel", "arbitrary")),
    )(*args)


def _rms(x, g):
    return x * lax.rsqrt(jnp.mean(x * x, axis=-1, keepdims=True) + RMS_EPS) * g


def _silu_mul(gate, up):
    return gate / (1.0 + jnp.exp(-gate)) * up


def _indicator(shape, head_axis, mod):
    lane = lax.broadcasted_iota(jnp.int32, shape, head_axis)
    other = lax.broadcasted_iota(jnp.int32, shape, 1 - head_axis)
    lane = jnp.bitwise_and(lane, HEAD_DIM - 1) if mod else jnp.right_shift(lane, 6)
    return jnp.where(lane == other, 1.0, 0.0).astype(BF16)


def _head_rms(split, xs, g):
    w = xs.shape[1]
    to_head, from_head = _indicator((w, BLOCK), 0, False), _indicator((BLOCK, w), 1, False)
    to_lane, from_lane = _indicator((HEAD_DIM, w), 1, True), _indicator((w, HEAD_DIM), 0, True)
    ss = split(xs * xs, to_head, from_head, 3)
    r = lax.rsqrt(ss * (1.0 / HEAD_DIM) + RMS_EPS)
    g_all = split(jnp.broadcast_to(g, (8, HEAD_DIM)), to_lane, from_lane, 3)[0:1]
    return xs * split(r, from_head, to_head, 3) * g_all


def _prep(split, x, qg, kg, segs):
    parts = []
    for start, width, kind in segs:
        xs = x[:, start:start + width]
        parts.append(xs if kind == "raw" else _head_rms(split, xs, qg if kind == "q" else kg))
    return jnp.concatenate(parts, axis=1)


def prep_fwd(x, qg, kg, segs, name):
    n, w = x.shape
    tm = _tile(n, 256, 8)
    (out,) = tcall(lambda ids, xt, a, b: (_prep(_plain_split, xt, a, b, segs),), (n // tm,),
                   [_row(x, tm), _full(qg), _full(kg)], [_row_out(n, w, BF16, tm)], name)
    return out


def prep_bwd(x, qg, kg, segs, grads, gather, name):
    n, w = x.shape
    tm = _tile(n, 128, 8)

    def fn(ids, xt, a, b, *t):
        _, vjp = jax.vjp(lambda x_, a_, b_: _prep(_split_dot_vjp, x_, a_, b_, segs), xt, a, b)
        return vjp(gather(*t))

    return tcall(fn, (n // tm,), [_row(x, tm), _full(qg), _full(kg)] + [_row(a, tm) for a in grads],
                 [_row_out(n, w, BF16, tm), _acc_out(qg.shape), _acc_out(kg.shape)], name)


def rmsnorm_fwd(x, g, name):
    n, d = x.shape
    tm = _tile(n, 512, 8)
    (h,) = tcall(lambda ids, xt, gt: (_rms(xt, gt),), (n // tm,), [_row(x, tm), _full(g)],
                 [_row_out(n, d, BF16, tm)], name)
    return h


def rmsnorm_bwd(x, g, dh, dres, name):
    n, d = x.shape
    tm = _tile(n, 256, 8)

    def fn(ids, xt, gt, dht, *rest):
        _, vjp = jax.vjp(_rms, xt, gt)
        dx, dg = vjp(dht.astype(F32))
        if rest:
            dx = dx + rest[0]
        return dx, dg

    ins = [_row(x, tm), _full(g), _row(dh, tm)] + ([_row(dres, tm)] if dres is not None else [])
    return tcall(fn, (n // tm,), ins, [_row_out(n, d, F32, tm), _acc_out(g.shape)], name)


def ffn_fwd(x, g, w_gu, w_down, tag):
    n = x.shape[0]
    f = w_down.shape[0]
    h = rmsnorm_fwd(x, g, tag + "_norm")
    gu = mm(h, w_gu, "nn", tag + "_gu")
    tm = _tile(n, 128, 8)
    (a,) = tcall(lambda ids, gt, ut: (_silu_mul(gt, ut),), (n // tm,),
                 [_row(gu, tm, f, 0), _row(gu, tm, f, 1)], [_row_out(n, f, BF16, tm)], tag + "_act")
    y = mm(a, w_down, "nn", tag + "_down", scale=0.5, res=x)
    return y, (x, h, gu, a)


def ffn_bwd(dy, saved, g, w_gu, w_down, tag):
    x, h, gu, a = saved
    n = x.shape[0]
    f = w_down.shape[0]
    da = mm(dy, w_down, "nt", tag + "_da", scale=0.5)
    d_wdown = mm(a, dy, "tn", tag + "_dwd", scale=0.5)
    tm = _tile(n, 128, 8)

    def act_bwd(ids, gt, ut, dat):
        _, vjp = jax.vjp(_silu_mul, gt, ut)
        dg, du = vjp(dat)
        return (jnp.concatenate([dg, du], axis=1),)

    (dgu,) = tcall(act_bwd, (n // tm,), [_row(gu, tm, f, 0), _row(gu, tm, f, 1), _row(da, tm)],
                   [_row_out(n, 2 * f, BF16, tm)], tag + "_dact")
    dh = mm(dgu, w_gu, "nt", tag + "_dh")
    d_wgu = mm(h, dgu, "tn", tag + "_dwgu")
    dx, dg = rmsnorm_bwd(x, g, dh, dy, tag + "_dnorm")
    return dx, dg, d_wgu, d_wdown


def _alibi(n_heads):
    return [float(s) for s in np.asarray(2.0 ** (-8.0 * np.arange(1, n_heads + 1) / n_heads), dtype=np.float32)]


def _banded_tile(dot, first, q, kp, kc, vp, vc, sinks, *, hkv, grp, max_dist, step, slopes, want_lse):
    row = lax.broadcasted_iota(jnp.int32, (BLOCK, 2 * BLOCK), 0)
    col = lax.broadcasted_iota(jnp.int32, (BLOCK, 2 * BLOCK), 1)
    dist = row + BLOCK - col
    valid = (dist >= 0) & (dist <= max_dist) & ((col >= BLOCK) | jnp.logical_not(first))
    distf = dist.astype(F32)

    def head(hd, qh, k2, v2):
        s = dot(qh, k2, True) * (HEAD_DIM ** -0.5)
        s = jnp.where(valid, s - (slopes[hd] * step) * distf, NEG_BIG)
        m = jnp.max(s, axis=-1, keepdims=True)
        if sinks is not None:
            pick = lax.broadcasted_iota(jnp.int32, sinks.shape, 1) == hd
            sk = jnp.sum(jnp.where(pick, sinks, 0.0), axis=1, keepdims=True)
            m = jnp.maximum(m, sk)
        m = lax.stop_gradient(m)
        p = jnp.exp(s - m)
        denom = jnp.sum(p, axis=-1, keepdims=True)
        if sinks is not None:
            denom = denom + jnp.exp(sk - m)
        return dot(p / denom, v2, False), m + jnp.log(denom)

    outs, lses = [], []
    if grp == 1:
        low = lax.broadcasted_iota(jnp.int32, (BLOCK, BLOCK), 1) < HEAD_DIM
        for pr in range(hkv // 2):
            sl = slice(pr * BLOCK, (pr + 1) * BLOCK)
            q2 = q[:, sl]
            k2 = jnp.concatenate([kp[:, sl], kc[:, sl]], axis=0)
            v2 = jnp.concatenate([vp[:, sl], vc[:, sl]], axis=0)
            o0, l0 = head(2 * pr, jnp.where(low, q2, 0.0), k2, v2)
            o1, l1 = head(2 * pr + 1, jnp.where(low, 0.0, q2), k2, v2)
            outs.append(jnp.where(low, o0, o1))
            lses.append(jnp.where(low, l0, l1))
    else:
        for hk in range(hkv):
            sl = slice(hk * HEAD_DIM, (hk + 1) * HEAD_DIM)
            k2 = jnp.concatenate([kp[:, sl], kc[:, sl]], axis=0)
            v2 = jnp.concatenate([vp[:, sl], vc[:, sl]], axis=0)
            for gi in range(grp):
                hd = hk * grp + gi
                o_h, l_h = head(hd, q[:, hd * HEAD_DIM:(hd + 1) * HEAD_DIM], k2, v2)
                outs.append(o_h)
                lses.append(jnp.broadcast_to(l_h, (BLOCK, HEAD_DIM)))
    o = jnp.concatenate(outs, axis=1)
    if want_lse:
        return o, jnp.concatenate(lses, axis=1)
    return (o,)


def _banded_specs(view, qcol, kcol, vcol, wq, wkv):
    def at(colfn, prev):
        if prev:
            return lambda r, n: (jnp.maximum(n - 1, 0), colfn(r))
        return lambda r, n: (n, colfn(r))
    return [
        _in(view, (BLOCK, wq), at(qcol, False)),
        _in(view, (BLOCK, wkv), at(kcol, True)),
        _in(view, (BLOCK, wkv), at(kcol, False)),
        _in(view, (BLOCK, wkv), at(vcol, True)),
        _in(view, (BLOCK, wkv), at(vcol, False)),
    ]


def banded_fwd(view, dil, cols, sinks, cfg, name):
    ns = view.shape[0]
    nb = ns // BLOCK
    wq, wkv = cfg["hkv"] * cfg["grp"] * HEAD_DIM, cfg["hkv"] * HEAD_DIM
    has_sinks = sinks is not None

    def fn(ids, q, kp, kc, vp, vc, *rest):
        q, kp, kc, vp, vc = [a.astype(F32) for a in (q, kp, kc, vp, vc)]
        return _banded_tile(_plain_dot, ids[1] == 0, q, kp, kc, vp, vc, rest[0] if has_sinks else None, **cfg)

    ins = _banded_specs(view, *cols, wq, wkv) + ([_full(sinks)] if has_sinks else [])
    outs = [_out((ns, dil * wq), BF16, (BLOCK, wq), lambda r, n: (n, r))]
    if cfg["want_lse"]:
        outs.append(_out((ns, dil * wq), F32, (BLOCK, wq), lambda r, n: (n, r)))
    return tcall(fn, (dil, nb), ins, outs, name)


def banded_bwd(view, dil, cols, sinks, cfg, cts, name):
    ns = view.shape[0]
    nb = ns // BLOCK
    wq, wkv = cfg["hkv"] * cfg["grp"] * HEAD_DIM, cfg["hkv"] * HEAD_DIM
    has_sinks = sinks is not None
    assert len(cts) == (2 if cfg["want_lse"] else 1)

    def fn(ids, q, kp, kc, vp, vc, *rest):
        sk = rest[0] if has_sinks else None
        ct = rest[1 if has_sinks else 0:]
        first = ids[1] == 0

        def f(q, kp, kc, vp, vc, *s):
            return _banded_tile(_dot_vjp, first, q, kp, kc, vp, vc, s[0] if has_sinks else None, **cfg)

        prim = tuple(a.astype(F32) for a in (q, kp, kc, vp, vc)) + ((sk,) if has_sinks else ())
        _, vjp = jax.vjp(f, *prim)
        return vjp(tuple(c.astype(F32) for c in ct))

    ins = (_banded_specs(view, *cols, wq, wkv) + ([_full(sinks)] if has_sinks else [])
           + [_in(a, (BLOCK, wq), (lambda r, n, cf=cf: (n, cf(r)))) for (a, cf) in cts])
    blk = lambda w: _out((ns, dil * w), F32, (BLOCK, w), lambda r, n: (n, r))
    outs = [blk(wq), blk(wkv), blk(wkv), blk(wkv), blk(wkv)]
    if has_sinks:
        outs.append(_acc_out(sinks.shape))
    res = tcall(fn, (dil, nb), ins, outs, name)
    dq, dkp, dkc, dvp, dvc = res[:5]

    def shift_add(ids, kc_, kn_, vc_, vn_):
        keep = ids[1] < nb - 1
        return kc_ + jnp.where(keep, kn_, 0.0), vc_ + jnp.where(keep, vn_, 0.0)

    here = lambda r, n: (n, r)
    nxt = lambda r, n: (jnp.minimum(n + 1, nb - 1), r)
    dk, dv = tcall(shift_add, (dil, nb),
                   [_in(dkc, (BLOCK, wkv), here), _in(dkp, (BLOCK, wkv), nxt),
                    _in(dvc, (BLOCK, wkv), here), _in(dvp, (BLOCK, wkv), nxt)],
                   [blk(wkv), blk(wkv)], name + "_shift")
    return (dq, dk, dv) + tuple(res[5:])


def _log_sigmoid(z):
    return jnp.minimum(z, 0.0) - jnp.log(1.0 + jnp.exp(-jnp.abs(z)))


def _sb_pair(dot, suffix, qh, kb, vb, r_in, diag):
    z = dot(qh, kb, True) * (HEAD_DIM ** -0.5)
    row = lax.broadcasted_iota(jnp.int32, (BLOCK, BLOCK), 0)
    col = lax.broadcasted_iota(jnp.int32, (BLOCK, BLOCK), 1)
    mask = (col < row) | jnp.logical_not(diag)
    lsp = _log_sigmoid(z)
    log_keep = jnp.where(mask, lsp - z, 0.0)
    log_after = suffix(log_keep) + r_in
    a = jnp.where(mask, jnp.exp(lsp + log_after), 0.0)
    return dot(a, vb, False), r_in + jnp.sum(log_keep, axis=1, keepdims=True)


def sb_fwd(qkv, qcb, kcb, vcb, name):
    s = qkv.shape[0]
    nb = s // BLOCK
    pairs = B_HEADS // 2

    def body(q_ref, k_ref, v_ref, o_ref):
        n = pl.program_id(1)
        low = lax.broadcasted_iota(jnp.int32, (BLOCK, BLOCK), 1) < HEAD_DIM
        q2 = q_ref[...].astype(F32)
        accs = []
        for h in range(2):
            qm = jnp.where(low, q2, 0.0) if h == 0 else jnp.where(low, 0.0, q2)

            def cond(c):
                return jnp.logical_and(c[0] >= 0, c[3] > SB_SKIP_LOG)

            def step(c, qm=qm):
                kb, r, acc, _ = c
                rows = pl.ds(pl.multiple_of(kb * BLOCK, BLOCK), BLOCK)
                o_part, r_out = _sb_pair(_plain_dot, _suffix_sum, qm, k_ref[rows, :], v_ref[rows, :], r, kb == n)
                return kb - 1, r_out, acc + o_part, jnp.max(r_out)

            init = (n, jnp.zeros((BLOCK, 1), F32), jnp.zeros((BLOCK, BLOCK), F32), jnp.float32(0.0))
            accs.append(lax.while_loop(cond, step, init)[2])
        o_ref[...] = jnp.where(low, accs[0], accs[1]).astype(o_ref.dtype)

    return _pcall(
        body, name=name, grid=(pairs, nb),
        in_specs=[pl.BlockSpec((BLOCK, BLOCK), lambda p, n: (n, qcb + p)),
                  pl.BlockSpec((s, BLOCK), lambda p, n: (0, kcb + p)),
                  pl.BlockSpec((s, BLOCK), lambda p, n: (0, vcb + p))],
        out_specs=pl.BlockSpec((BLOCK, BLOCK), lambda p, n: (n, p)),
        out_shape=jax.ShapeDtypeStruct((s, pairs * BLOCK), BF16),
        compiler_params=_params(),
    )(qkv, qkv, qkv)


def sb_bwd(qkv, qcb, kcb, vcb, do, docb, name):
    s = qkv.shape[0]
    nb = s // BLOCK
    pairs = B_HEADS // 2

    def body(q_ref, k_ref, v_ref, do_ref, dq_ref, dk_ref, dv_ref, r_ref):
        n = pl.program_id(1)

        @pl.when(n == 0)
        def _():
            dk_ref[...] = jnp.zeros(dk_ref.shape, F32)
            dv_ref[...] = jnp.zeros(dv_ref.shape, F32)

        low = lax.broadcasted_iota(jnp.int32, (BLOCK, BLOCK), 1) < HEAD_DIM
        q2 = q_ref[...].astype(F32)
        do2 = do_ref[...].astype(F32)
        dqs = []
        for h in range(2):
            qh = jnp.where(low, q2, 0.0) if h == 0 else jnp.where(low, 0.0, q2)
            doh = jnp.where(low, do2, 0.0) if h == 0 else jnp.where(low, 0.0, do2)

            def cond(c):
                return jnp.logical_and(c[0] >= 0, c[2] > SB_SKIP_LOG)

            def down(c, qh=qh):
                kb, r, _ = c
                r_ref[kb] = r
                rows = pl.ds(pl.multiple_of(kb * BLOCK, BLOCK), BLOCK)
                z = _dot(qh, k_ref[rows, :], NT) * (HEAD_DIM ** -0.5)
                row = lax.broadcasted_iota(jnp.int32, (BLOCK, BLOCK), 0)
                col = lax.broadcasted_iota(jnp.int32, (BLOCK, BLOCK), 1)
                mask = (col < row) | (kb != n)
                log_keep = jnp.where(mask, _log_sigmoid(z) - z, 0.0)
                r_out = r + jnp.sum(log_keep, axis=1, keepdims=True)
                return kb - 1, r_out, jnp.max(r_out)

            last = lax.while_loop(cond, down, (n, jnp.zeros((BLOCK, 1), F32), jnp.float32(0.0)))[0] + 1

            def up(kb, c, qh=qh, doh=doh):
                dq, g_r = c
                rows = pl.ds(pl.multiple_of(kb * BLOCK, BLOCK), BLOCK)
                diag = kb == n
                _, vjp = jax.vjp(lambda q_, k_, v_, r_: _sb_pair(_dot_vjp, _suffix_sum_vjp, q_, k_, v_, r_, diag),
                                 qh, k_ref[rows, :].astype(F32), v_ref[rows, :].astype(F32), r_ref[kb])
                dq_c, dk_c, dv_c, g_in = vjp((doh, g_r))
                dk_ref[rows, :] += dk_c
                dv_ref[rows, :] += dv_c
                return dq + dq_c, g_in

            dqs.append(lax.fori_loop(last, n + 1, up, (jnp.zeros((BLOCK, BLOCK), F32), jnp.zeros((BLOCK, 1), F32)))[0])
        dq_ref[...] = jnp.where(low, dqs[0], dqs[1])

    wide = jax.ShapeDtypeStruct((s, pairs * BLOCK), F32)
    return _pcall(
        body, name=name, grid=(pairs, nb),
        in_specs=[pl.BlockSpec((BLOCK, BLOCK), lambda p, n: (n, qcb + p)),
                  pl.BlockSpec((s, BLOCK), lambda p, n: (0, kcb + p)),
                  pl.BlockSpec((s, BLOCK), lambda p, n: (0, vcb + p)),
                  pl.BlockSpec((BLOCK, BLOCK), lambda p, n: (n, docb + p))],
        out_specs=[pl.BlockSpec((BLOCK, BLOCK), lambda p, n: (n, p)),
                   pl.BlockSpec((s, BLOCK), lambda p, n: (0, p)),
                   pl.BlockSpec((s, BLOCK), lambda p, n: (0, p))],
        out_shape=[wide, wide, wide],
        scratch_shapes=[pltpu.VMEM((nb, BLOCK, 1), F32)],
        compiler_params=_params(),
    )(qkv, qkv, qkv, do)


def _xa_tile(dot, q, kv, qg, kg):
    hd = q.shape[1] // X_HEADS
    outs = []
    for h in range(X_HEADS):
        qh = _rms(q[:, h * hd:(h + 1) * hd], qg)
        kh = _rms(kv[:, h * hd:(h + 1) * hd], kg)
        vh = kv[:, (X_HEADS + h) * hd:(X_HEADS + h + 1) * hd]
        sc = dot(qh, kh, True) * (hd ** -0.5)
        m = lax.stop_gradient(jnp.max(sc, axis=-1, keepdims=True))
        p = jnp.exp(sc - m)
        outs.append(dot(p / jnp.sum(p, axis=-1, keepdims=True), vh, False))
    return jnp.concatenate(outs, axis=1)


def xa_core_fwd(q, kv, qg, kg, name):
    n, d = q.shape
    tm = _tile(n, 256, 8)
    (o,) = tcall(lambda ids, qt, kvt, qgt, kgt: (_xa_tile(_plain_dot, qt, kvt, qgt, kgt),), (n // tm,),
                 [_row(q, tm), _full(kv), _full(qg), _full(kg)], [_row_out(n, d, BF16, tm)], name)
    return o


def xa_core_bwd(q, kv, qg, kg, do, name):
    n, d = q.shape
    tm = _tile(n, 256, 8)

    def fn(ids, qt, kvt, qgt, kgt, dot_):
        _, vjp = jax.vjp(functools.partial(_xa_tile, _dot_vjp), qt, kvt, qgt, kgt)
        return vjp(dot_.astype(F32))

    return tcall(fn, (n // tm,), [_row(q, tm), _full(kv), _full(qg), _full(kg), _row(do, tm)],
                 [_row_out(n, d, BF16, tm), _acc_out(kv.shape), _acc_out(qg.shape), _acc_out(kg.shape)], name)


def _ev_reorder(a):
    return jnp.concatenate([a[..., 0:512], a[..., 768:2304], a[..., 512:768]], axis=-1)


def _ev_restore(a):
    return jnp.concatenate([a[..., 0:512], a[..., 2048:2304], a[..., 512:2048]], axis=-1)


_EV_SEGS = ((0, 512, "q"), (512, 1536, "raw"), (2048, 128, "k"), (2176, 128, "raw"))
_A_CFG = dict(hkv=A_KV_HEADS, grp=A_Q_HEADS // A_KV_HEADS, max_dist=BLOCK - 1, step=1.0, slopes=_alibi(A_Q_HEADS),
              want_lse=False)
_A_COLS = (lambda r: 0, lambda r: 16, lambda r: 17)


def even_mixer_fwd(x, g, w_in, qg, kg, sinks, w_out, tag):
    h = rmsnorm_fwd(x, g, tag + "_norm")
    qkv = mm(h, w_in, "nn", tag + "_in")
    ops = prep_fwd(qkv, qg, kg, _EV_SEGS, tag + "_prep")
    (o_a,) = banded_fwd(ops, 1, _A_COLS, sinks, _A_CFG, tag + "_swa")
    o_b = sb_fwd(ops, 4, 8, 12, tag + "_sb")
    o = jnp.concatenate([o_a, o_b], axis=1)
    y = mm(o, w_out, "nn", tag + "_out", res=x)
    return y, (x, h, qkv, ops, o)


def even_mixer_bwd(dy, saved, g, w_in, qg, kg, sinks, w_out, tag):
    x, h, qkv, ops, o = saved
    do = mm(dy, w_out, "nt", tag + "_do")
    d_wout = mm(o, dy, "tn", tag + "_dwout")
    dqa, dka, dva, dsinks = banded_bwd(ops, 1, _A_COLS, sinks, _A_CFG, [(do, lambda r: 0)], tag + "_dswa")
    dqb, dkb, dvb = sb_bwd(ops, 4, 8, 12, do, 4, tag + "_dsb")
    dqkv, dqg, dkg = prep_bwd(qkv, qg, kg, _EV_SEGS, (dqa, dqb, dkb, dvb, dka, dva),
                              lambda *t: jnp.concatenate(t, axis=1), tag + "_dqkv")
    dh = mm(dqkv, w_in, "nt", tag + "_dh")
    d_win = mm(h, dqkv, "tn", tag + "_dwin")
    dx, dg = rmsnorm_bwd(x, g, dh, dy, tag + "_dnorm")
    return dx, dg, d_win, dqg, dkg, dsinks, d_wout


def _c_cfg(window, dil):
    return dict(hkv=C_HEADS, grp=1, max_dist=window // dil, step=float(dil), slopes=_alibi(C_HEADS), want_lse=True)


_C_COLS = (lambda r: 3 * r, lambda r: 3 * r + 1, lambda r: 3 * r + 2)
_OD_SEGS = ((0, 1024, "q"), (1024, 1024, "k"), (2048, 1024, "raw"))


def _combine(o1, o2, o3, l1, l2, l3):
    m = lax.stop_gradient(jnp.maximum(jnp.maximum(l1, l2), l3))
    e1, e2, e3 = jnp.exp(l1 - m), jnp.exp(l2 - m), jnp.exp(l3 - m)
    tot = e1 + e2 + e3
    return (e1 / tot) * o1 + (e2 / tot) * o2 + (e3 / tot) * o3


def odd_mixer_fwd(x, g, w_in, qg, kg, w_out, tag):
    n, d = x.shape
    h = rmsnorm_fwd(x, g, tag + "_norm")
    qkv = mm(h, w_in, "nn", tag + "_in")
    w3 = qkv.shape[1]
    ops = prep_fwd(qkv, qg, kg, _OD_SEGS, tag + "_prep")
    os_, ls_ = [], []
    for window, dil in C_PATTERNS:
        o_p, l_p = banded_fwd(ops.reshape(n // dil, dil * w3), dil, _C_COLS, None, _c_cfg(window, dil),
                              f"{tag}_dil{dil}")
        os_.append(o_p.reshape(n, d))
        ls_.append(l_p.reshape(n, d))
    tm = _tile(n, 128, 8)
    (o,) = tcall(lambda ids, *t: (_combine(*[a.astype(F32) for a in t]),), (n // tm,),
                 [_row(a, tm) for a in os_ + ls_], [_row_out(n, d, BF16, tm)], tag + "_comb")
    y = mm(o, w_out, "nn", tag + "_out", res=x)
    return y, (x, h, qkv, ops, os_, ls_, o)


def odd_mixer_bwd(dy, saved, g, w_in, qg, kg, w_out, tag):
    x, h, qkv, ops, os_, ls_, o = saved
    n, d = x.shape
    w3 = qkv.shape[1]
    do = mm(dy, w_out, "nt", tag + "_do")
    d_wout = mm(o, dy, "tn", tag + "_dwout")
    tm = _tile(n, 128, 8)

    def comb_bwd(ids, *t):
        _, vjp = jax.vjp(_combine, *[a.astype(F32) for a in t[:6]])
        return vjp(t[6])

    cts = tcall(comb_bwd, (n // tm,), [_row(a, tm) for a in os_ + ls_ + [do]],
                [_row_out(n, d, F32, tm) for _ in range(6)], tag + "_dcomb")
    dqs, dks, dvs = [], [], []
    for p, (window, dil) in enumerate(C_PATTERNS):
        lay = (n // dil, dil * d)
        dq, dk, dv = banded_bwd(
            ops.reshape(n // dil, dil * w3), dil, _C_COLS, None, _c_cfg(window, dil),
            [(cts[p].reshape(lay), lambda r: r), (cts[3 + p].reshape(lay), lambda r: r)], f"{tag}_ddil{dil}")
        dqs.append(dq.reshape(n, d))
        dks.append(dk.reshape(n, d))
        dvs.append(dv.reshape(n, d))

    def gather(*t):
        return jnp.concatenate([t[0] + t[1] + t[2], t[3] + t[4] + t[5], t[6] + t[7] + t[8]], axis=1)

    dqkv, dqg, dkg = prep_bwd(qkv, qg, kg, _OD_SEGS, dqs + dks + dvs, gather, tag + "_dqkv")
    dh = mm(dqkv, w_in, "nt", tag + "_dh")
    d_win = mm(h, dqkv, "tn", tag + "_dwin")
    dx, dg = rmsnorm_bwd(x, g, dh, dy, tag + "_dnorm")
    return dx, dg, d_win, dqg, dkg, d_wout


def xa_fwd(x, mem, g, gm, w_q, w_kv, qg, kg, w_o, tag):
    h = rmsnorm_fwd(x, g, tag + "_norm")
    q = mm(h, w_q, "nn", tag + "_q")
    mn = rmsnorm_fwd(mem, gm, tag + "_mnorm")
    kv = mm(mn, w_kv, "nn", tag + "_kv")
    o = xa_core_fwd(q, kv, qg, kg, tag + "_core")
    y = mm(o, w_o, "nn", tag + "_o", res=x)
    return y, (x, h, q, mn, kv, o)


def xa_bwd(dy, saved, mem, g, gm, w_q, w_kv, qg, kg, w_o, tag):
    x, h, q, mn, kv, o = saved
    do = mm(dy, w_o, "nt", tag + "_do", out_dtype=BF16)
    d_wo = mm(o, dy, "tn", tag + "_dwo")
    dq, dkv, dqg, dkg = xa_core_bwd(q, kv, qg, kg, do, tag + "_dcore")
    dh = mm(dq, w_q, "nt", tag + "_dh")
    d_wq = mm(h, dq, "tn", tag + "_dwq")
    dx, dg = rmsnorm_bwd(x, g, dh, dy, tag + "_dnorm")
    dmn = mm(dkv, w_kv, "nt", tag + "_dmn")
    d_wkv = mm(mn, dkv, "tn", tag + "_dwkv")
    _, dgm = rmsnorm_bwd(mem, gm, dmn, None, tag + "_dmnorm")
    return dx, dg, dgm, d_wq, d_wkv, dqg, dkg, d_wo


def loss_head(y, target, name):
    n, d = y.shape
    tm = _tile(n, 512, 8)

    def fn(ids, yt, tt):
        e = yt - tt
        return e * (1.0 / d), jnp.sum(e * e, axis=0, keepdims=True)

    return tcall(fn, (n // tm,), [_row(y, tm), _row(target, tm)], [_row_out(n, d, F32, tm), _acc_out((1, d))], name)


_ANY = pl.BlockSpec(memory_space=pl.ANY)


def all_gather_blocks(block):
    def body(x_ref, out_ref, send_sems, recv_sems, local_sem):
        x, y, c = lax.axis_index("x"), lax.axis_index("y"), lax.axis_index("c")
        me, sibling = (x, y, c), (x, y, 1 - c)
        chips = [(1 - x, y), (x, 1 - y), (1 - x, 1 - y)]

        def slot(px, py, pc):
            return out_ref.at[4 * px + 2 * py + pc]

        def copy(k, blk, to, src=None):
            return pltpu.make_async_remote_copy(
                src_ref=slot(*blk) if src is None else src, dst_ref=slot(*blk),
                send_sem=send_sems.at[k], recv_sem=recv_sems.at[k], device_id=to, device_id_type=MESH)

        mine = pltpu.make_async_copy(x_ref, slot(*me), local_sem)
        mine.start()
        first = [copy(0, me, sibling, src=x_ref)]
        first += [copy(1 + j, me, (*chip, c), src=x_ref) for j, chip in enumerate(chips)]
        for cp in first:
            cp.start()
        passed = [copy(4 + j, (*chip, c), sibling) for j, chip in enumerate(chips)]
        for j, chip in enumerate(chips):
            copy(1 + j, (*chip, c), me).wait_recv()
            passed[j].start()
        copy(0, sibling, me).wait_recv()
        for j, chip in enumerate(chips):
            copy(4 + j, (*chip, 1 - c), me).wait_recv()
        for cp in first + passed:
            cp.wait_send()
        mine.wait()

    return _pcall(
        body, name="weights_all_gather",
        in_specs=[_ANY], out_specs=_ANY,
        out_shape=jax.ShapeDtypeStruct((N_DEV,) + block.shape, block.dtype),
        scratch_shapes=[pltpu.SemaphoreType.DMA((7,)), pltpu.SemaphoreType.DMA((7,)), pltpu.SemaphoreType.DMA],
    )(block)


def pair_exchange(gm, gv):
    def body(gm_ref, gv_ref, om_ref, ov_ref, send_sems, recv_sems):
        x, y, c = lax.axis_index("x"), lax.axis_index("y"), lax.axis_index("c")
        copies = []
        for b, (src, dst) in enumerate([(gm_ref, om_ref), (gv_ref, ov_ref)]):
            for j in range(4):
                cp = pltpu.make_async_remote_copy(
                    src_ref=src.at[2 * j + (1 - c)], dst_ref=dst.at[j], send_sem=send_sems.at[4 * b + j],
                    recv_sem=recv_sems.at[4 * b + j], device_id=(x, y, 1 - c), device_id_type=MESH)
                cp.start()
                copies.append(cp)
        for cp in copies:
            cp.wait()

    return _pcall(
        body, name="grads_pair_exchange",
        in_specs=[_ANY, _ANY], out_specs=[_ANY, _ANY],
        out_shape=[jax.ShapeDtypeStruct((4,) + gm.shape[1:], gm.dtype),
                   jax.ShapeDtypeStruct((4,) + gv.shape[1:], gv.dtype)],
        scratch_shapes=[pltpu.SemaphoreType.DMA((8,)), pltpu.SemaphoreType.DMA((8,))],
    )(gm, gv)


def pair_sum(g, got, c, out_dtype, name):
    r = g.shape[1]
    tr = _tile(r, 512, 16)

    def body(c_ref, a_ref, b_ref, o_ref):
        o_ref[...] = (a_ref[...] + b_ref[...]).astype(o_ref.dtype)

    return _pcall(
        body, name=name,
        grid_spec=pltpu.PrefetchScalarGridSpec(
            num_scalar_prefetch=1, grid=(4, r // tr),
            in_specs=[pl.BlockSpec((None, tr, LANES), lambda j, i, c_ref: (2 * j + c_ref[0], i, 0)),
                      pl.BlockSpec((None, tr, LANES), lambda j, i, c_ref: (j, i, 0))],
            out_specs=pl.BlockSpec((None, tr, LANES), lambda j, i, c_ref: (j, i, 0))),
        out_shape=jax.ShapeDtypeStruct((4,) + g.shape[1:], out_dtype),
        compiler_params=_params(),
    )(c, g, got)


def chip_exchange(pm, pv):
    def body(pm_ref, pv_ref, om_ref, ov_ref, send_sems, recv_sems, local_sems):
        x, y, c = lax.axis_index("x"), lax.axis_index("y"), lax.axis_index("c")
        my_chip = 2 * x + y
        copies = []
        for b, (src, dst) in enumerate([(pm_ref, om_ref), (pv_ref, ov_ref)]):
            mine = pltpu.make_async_copy(src.at[my_chip], dst.at[my_chip], local_sems.at[b])
            mine.start()
            copies.append(mine)
            for k, (tx, ty) in enumerate([(1 - x, y), (x, 1 - y), (1 - x, 1 - y)]):
                cp = pltpu.make_async_remote_copy(
                    src_ref=src.at[2 * tx + ty], dst_ref=dst.at[my_chip], send_sem=send_sems.at[3 * b + k],
                    recv_sem=recv_sems.at[3 * b + k], device_id=(tx, ty, c), device_id_type=MESH)
                cp.start()
                copies.append(cp)
        for cp in copies:
            cp.wait()

    return _pcall(
        body, name="grads_chip_exchange",
        in_specs=[_ANY, _ANY], out_specs=[_ANY, _ANY],
        out_shape=[jax.ShapeDtypeStruct(pm.shape, pm.dtype), jax.ShapeDtypeStruct(pv.shape, pv.dtype)],
        scratch_shapes=[pltpu.SemaphoreType.DMA((6,)), pltpu.SemaphoreType.DMA((6,)), pltpu.SemaphoreType.DMA((2,))],
    )(pm, pv)


def chip_sum(parts, name):
    r = parts.shape[1]
    tr = _tile(r, 512, 16)
    spec = lambda j: _in(parts, (None, tr, LANES), lambda i, j=j: (j, i, 0))

    def fn(ids, a, b, c_, d):
        a, b, c_, d = [t.astype(F32) for t in (a, b, c_, d)]
        return (((a + b) + c_) + d,)

    (out,) = tcall(fn, (r // tr,), [spec(j) for j in range(4)],
                   [_out((r, LANES), F32, (tr, LANES), lambda i: (i, 0))], name)
    return out


def adamw(w, g, m, v, name):
    shape = w.shape
    cols = shape[-1]
    rows = int(np.prod(shape[:-1]))
    w2, g2, m2, v2 = [a.reshape(rows, cols) for a in (w, g, m, v)]
    tr = _tile(rows, 256, 8) if rows % 8 == 0 else rows

    def fn(ids, wt, gt, mt, vt):
        m_new = ADAM_B1 * mt + (1.0 - ADAM_B1) * gt
        v_new = ADAM_B2 * vt + (1.0 - ADAM_B2) * (gt * gt)
        m_hat = m_new / (1.0 - ADAM_B1 ** ADAM_STEP)
        v_hat = v_new / (1.0 - ADAM_B2 ** ADAM_STEP)
        delta = -ADAM_LR * (m_hat / (jnp.sqrt(v_hat) + ADAM_EPS) + ADAM_WD * wt)
        return delta, m_new, v_new

    res = tcall(fn, (rows // tr,), [_row(a, tr) for a in (w2, g2, m2, v2)],
                [_row_out(rows, cols, F32, tr) for _ in range(3)], name)
    return [a.reshape(shape) for a in res]


_MATS = [("ffn1_w_gu", "col"), ("ffn1_w_down", "row"), ("ev_w_in", "col"), ("ev_w_out", "row"),
         ("od_w_in", "col"), ("od_w_out", "row"), ("xa_w_q", "row"), ("xa_w_kv", "col"), ("xa_w_o", "row"),
         ("ffn2_w_gu", "col"), ("ffn2_w_down", "row")]
_VECS = ["ffn1_norm", "mix_norm", "ev_q_gain", "ev_k_gain", "ev_sinks", "od_q_gain", "od_k_gain", "xa_norm",
         "xa_mem_norm", "xa_q_gain", "xa_k_gain", "ffn2_norm"]
_WEIGHTS = ["ffn1_norm", "ffn1_w_gu", "ffn1_w_down", "mix_norm", "ev_w_in", "ev_q_gain", "ev_k_gain", "ev_sinks",
            "ev_w_out", "od_w_in", "od_q_gain", "od_k_gain", "od_w_out", "xa_norm", "xa_mem_norm", "xa_w_q", "xa_w_kv",
            "xa_q_gain", "xa_k_gain", "xa_w_o", "ffn2_norm", "ffn2_w_gu", "ffn2_w_down"]


def _gather_weights(shards):
    pieces = [shards[n].astype(BF16).reshape(-1, LANES) for n, _ in _MATS]
    packed = jnp.concatenate(pieces, axis=0)
    got = all_gather_blocks(packed)
    full, off = {}, 0
    for (n, axis), piece in zip(_MATS, pieces):
        l, a, b = shards[n].shape
        seg = got[:, off:off + piece.shape[0], :].reshape(N_DEV, l, a, b)
        off += piece.shape[0]
        if axis == "row":
            full[n] = seg.transpose(1, 0, 2, 3).reshape(l, N_DEV * a, b)
        else:
            full[n] = seg.transpose(1, 2, 0, 3).reshape(l, a, N_DEV * b)
    return full


def _reduce_gradients(mats, vecs, c):
    rows = []
    for n, axis in _MATS:
        gr = mats[n]
        l, a, b = gr.shape
        if axis == "row":
            rows.append(gr.reshape(l, N_DEV, a // N_DEV, b).transpose(1, 0, 2, 3).reshape(N_DEV, -1))
        else:
            rows.append(gr.reshape(l, a, N_DEV, b // N_DEV).transpose(2, 0, 1, 3).reshape(N_DEV, -1))
    total = sum(r.shape[1] for r in rows)
    padded = -(-total // (BLOCK * LANES)) * (BLOCK * LANES)
    buf_m = jnp.concatenate(rows + [jnp.zeros((N_DEV, padded - total), F32)], axis=1)
    buf_m = buf_m.reshape(N_DEV, padded // LANES, LANES)
    vec = jnp.concatenate([vecs[n].reshape(-1) for n in _VECS])
    vec = jnp.pad(vec, (0, -vec.shape[0] % (16 * LANES)))
    buf_v = jnp.broadcast_to(vec.reshape(1, -1, LANES), (N_DEV, vec.shape[0] // LANES, LANES))
    got_m, got_v = pair_exchange(buf_m, buf_v)
    part_m = pair_sum(buf_m, got_m, c, BF16, "grads_pair_sum")
    part_v = pair_sum(buf_v, got_v, c, F32, "grads_pair_sum_vec")
    all_m, all_v = chip_exchange(part_m, part_v)
    sum_m = chip_sum(all_m, "grads_chip_sum").reshape(-1)
    sum_v = chip_sum(all_v, "grads_chip_sum_vec").reshape(-1)
    out, off = {}, 0
    for (n, axis), r in zip(_MATS, rows):
        l, a, b = mats[n].shape
        shape = (l, a // N_DEV, b) if axis == "row" else (l, a, b // N_DEV)
        out[n] = sum_m[off:off + r.shape[1]].reshape(shape)
        off += r.shape[1]
    off = 0
    for n in _VECS:
        out[n] = sum_v[off:off + vecs[n].size].reshape(vecs[n].shape)
        off += vecs[n].size
    return out


def _local_step(x, mem, target, w, full):
    depth = w["ffn1_norm"].shape[0]
    ev_in = _ev_reorder(full["ev_w_in"])
    row = lambda a, l: a[l:l + 1]
    saved = []
    for l in range(depth):
        t = f"l{l}"
        j = l // 2
        x, s1 = ffn_fwd(x, row(w["ffn1_norm"], l), full["ffn1_w_gu"][l], full["ffn1_w_down"][l], t + "_ffn1")
        if l % 2 == 0:
            x, s2 = even_mixer_fwd(x, row(w["mix_norm"], l), ev_in[j], row(w["ev_q_gain"], j), row(w["ev_k_gain"], j),
                                   row(w["ev_sinks"], j), full["ev_w_out"][j], t + "_ev")
        else:
            x, s2 = odd_mixer_fwd(x, row(w["mix_norm"], l), full["od_w_in"][j], row(w["od_q_gain"], j),
                                  row(w["od_k_gain"], j), full["od_w_out"][j], t + "_od")
        x, s3 = xa_fwd(x, mem, row(w["xa_norm"], l), row(w["xa_mem_norm"], l), full["xa_w_q"][l], full["xa_w_kv"][l],
                       row(w["xa_q_gain"], l), row(w["xa_k_gain"], l), full["xa_w_o"][l], t + "_xa")
        x, s4 = ffn_fwd(x, row(w["ffn2_norm"], l), full["ffn2_w_gu"][l], full["ffn2_w_down"][l], t + "_ffn2")
        saved.append((s1, s2, s3, s4))
    dx, sq = loss_head(x, target, "loss_head")
    loss = 0.5 * jnp.sum(sq) / x.shape[1]

    gm = {n: [None] * full[n].shape[0] for n, _ in _MATS}
    gv = {n: [None] * w[n].shape[0] for n in _VECS}
    for l in reversed(range(depth)):
        t = f"l{l}"
        j = l // 2
        s1, s2, s3, s4 = saved[l]
        dx, gv["ffn2_norm"][l], gm["ffn2_w_gu"][l], gm["ffn2_w_down"][l] = ffn_bwd(
            dx, s4, row(w["ffn2_norm"], l), full["ffn2_w_gu"][l], full["ffn2_w_down"][l], t + "_ffn2")
        (dx, gv["xa_norm"][l], gv["xa_mem_norm"][l], gm["xa_w_q"][l], gm["xa_w_kv"][l], gv["xa_q_gain"][l],
         gv["xa_k_gain"][l], gm["xa_w_o"][l]) = xa_bwd(
            dx, s3, mem, row(w["xa_norm"], l), row(w["xa_mem_norm"], l), full["xa_w_q"][l], full["xa_w_kv"][l],
            row(w["xa_q_gain"], l), row(w["xa_k_gain"], l), full["xa_w_o"][l], t + "_xa")
        if l % 2 == 0:
            (dx, gv["mix_norm"][l], d_win, gv["ev_q_gain"][j], gv["ev_k_gain"][j], gv["ev_sinks"][j],
             gm["ev_w_out"][j]) = even_mixer_bwd(
                dx, s2, row(w["mix_norm"], l), ev_in[j], row(w["ev_q_gain"], j), row(w["ev_k_gain"], j),
                row(w["ev_sinks"], j), full["ev_w_out"][j], t + "_ev")
            gm["ev_w_in"][j] = _ev_restore(d_win)
        else:
            (dx, gv["mix_norm"][l], gm["od_w_in"][j], gv["od_q_gain"][j], gv["od_k_gain"][j],
             gm["od_w_out"][j]) = odd_mixer_bwd(
                dx, s2, row(w["mix_norm"], l), full["od_w_in"][j], row(w["od_q_gain"], j), row(w["od_k_gain"], j),
                full["od_w_out"][j], t + "_od")
        dx, gv["ffn1_norm"][l], gm["ffn1_w_gu"][l], gm["ffn1_w_down"][l] = ffn_bwd(
            dx, s1, row(w["ffn1_norm"], l), full["ffn1_w_gu"][l], full["ffn1_w_down"][l], t + "_ffn1")
    mats = {n: jnp.stack(v) for n, v in gm.items()}
    vecs = {n: jnp.concatenate(v, axis=0) for n, v in gv.items()}
    return loss, dx, mats, vecs


def kernel(x, mem, ffn1_norm, ffn1_w_gu, ffn1_w_down, mix_norm, ev_w_in, ev_q_gain, ev_k_gain, ev_sinks, ev_w_out, od_w_in, od_q_gain, od_k_gain, od_w_out, xa_norm, xa_mem_norm, xa_w_q, xa_w_kv, xa_q_gain, xa_k_gain, xa_w_o, ffn2_norm, ffn2_w_gu, ffn2_w_down, loss_target, m_ffn1_norm, m_ffn1_w_gu, m_ffn1_w_down, m_mix_norm, m_ev_w_in, m_ev_q_gain, m_ev_k_gain, m_ev_sinks, m_ev_w_out, m_od_w_in, m_od_q_gain, m_od_k_gain, m_od_w_out, m_xa_norm, m_xa_mem_norm, m_xa_w_q, m_xa_w_kv, m_xa_q_gain, m_xa_k_gain, m_xa_w_o, m_ffn2_norm, m_ffn2_w_gu, m_ffn2_w_down, v_ffn1_norm, v_ffn1_w_gu, v_ffn1_w_down, v_mix_norm, v_ev_w_in, v_ev_q_gain, v_ev_k_gain, v_ev_sinks, v_ev_w_out, v_od_w_in, v_od_q_gain, v_od_k_gain, v_od_w_out, v_xa_norm, v_xa_mem_norm, v_xa_w_q, v_xa_w_kv, v_xa_q_gain, v_xa_k_gain, v_xa_w_o, v_ffn2_norm, v_ffn2_w_gu, v_ffn2_w_down):
    w = dict(ffn1_norm=ffn1_norm, ffn1_w_gu=ffn1_w_gu, ffn1_w_down=ffn1_w_down, mix_norm=mix_norm, ev_w_in=ev_w_in, ev_q_gain=ev_q_gain, ev_k_gain=ev_k_gain, ev_sinks=ev_sinks, ev_w_out=ev_w_out, od_w_in=od_w_in, od_q_gain=od_q_gain, od_k_gain=od_k_gain, od_w_out=od_w_out, xa_norm=xa_norm, xa_mem_norm=xa_mem_norm, xa_w_q=xa_w_q, xa_w_kv=xa_w_kv, xa_q_gain=xa_q_gain, xa_k_gain=xa_k_gain, xa_w_o=xa_w_o, ffn2_norm=ffn2_norm, ffn2_w_gu=ffn2_w_gu, ffn2_w_down=ffn2_w_down)
    m = dict(ffn1_norm=m_ffn1_norm, ffn1_w_gu=m_ffn1_w_gu, ffn1_w_down=m_ffn1_w_down, mix_norm=m_mix_norm, ev_w_in=m_ev_w_in, ev_q_gain=m_ev_q_gain, ev_k_gain=m_ev_k_gain, ev_sinks=m_ev_sinks, ev_w_out=m_ev_w_out, od_w_in=m_od_w_in, od_q_gain=m_od_q_gain, od_k_gain=m_od_k_gain, od_w_out=m_od_w_out, xa_norm=m_xa_norm, xa_mem_norm=m_xa_mem_norm, xa_w_q=m_xa_w_q, xa_w_kv=m_xa_w_kv, xa_q_gain=m_xa_q_gain, xa_k_gain=m_xa_k_gain, xa_w_o=m_xa_w_o, ffn2_norm=m_ffn2_norm, ffn2_w_gu=m_ffn2_w_gu, ffn2_w_down=m_ffn2_w_down)
    v = dict(ffn1_norm=v_ffn1_norm, ffn1_w_gu=v_ffn1_w_gu, ffn1_w_down=v_ffn1_w_down, mix_norm=v_mix_norm, ev_w_in=v_ev_w_in, ev_q_gain=v_ev_q_gain, ev_k_gain=v_ev_k_gain, ev_sinks=v_ev_sinks, ev_w_out=v_ev_w_out, od_w_in=v_od_w_in, od_q_gain=v_od_q_gain, od_k_gain=v_od_k_gain, od_w_out=v_od_w_out, xa_norm=v_xa_norm, xa_mem_norm=v_xa_mem_norm, xa_w_q=v_xa_w_q, xa_w_kv=v_xa_w_kv, xa_q_gain=v_xa_q_gain, xa_k_gain=v_xa_k_gain, xa_w_o=v_xa_w_o, ffn2_norm=v_ffn2_norm, ffn2_w_gu=v_ffn2_w_gu, ffn2_w_down=v_ffn2_w_down)

    full = _gather_weights(w)
    loss, dx, mats, vecs = _local_step(x[0], mem[0], loss_target[0], w, full)
    c = lax.axis_index("c").astype(jnp.int32).reshape(1)
    grads = _reduce_gradients(mats, vecs, c)
    loss = lax.psum(loss, ("x", "y", "c"))

    delta, new_m, new_v = {}, {}, {}
    for n in _WEIGHTS:
        delta[n], new_m[n], new_v[n] = adamw(w[n], grads[n], m[n], v[n], "adamw_" + n)
    return (loss, dx[None], *[grads[n] for n in _WEIGHTS], *[delta[n] for n in _WEIGHTS],
            *[new_m[n] for n in _WEIGHTS], *[new_v[n] for n in _WEIGHTS])
```

```python
import functools

import numpy as np
import jax
import jax.numpy as jnp
from jax import lax
from jax.experimental import pallas as pl
from jax.experimental.pallas import tpu as pltpu

F32 = jnp.float32
BF16 = jnp.bfloat16
MESH = pl.DeviceIdType.MESH

HEAD_DIM = 64
BLOCK = 128
RMS_EPS = 1e-6
A_Q_HEADS, A_KV_HEADS = 8, 2
B_HEADS = 8
C_HEADS = 16
C_PATTERNS = ((128, 1), (512, 4), (2048, 16))
X_HEADS = 4
N_DEV = 8
LANES = 1024
VMEM_LIMIT_BYTES = 56 * 1024 * 1024
SB_SKIP_LOG = -110.0
NEG_BIG = -1e30

ADAM_LR, ADAM_B1, ADAM_B2, ADAM_EPS, ADAM_WD, ADAM_STEP = 0.001, 0.9, 0.999, 1e-08, 0.01, 10

NN = (((1,), (0,)), ((), ()))
NT = (((1,), (1,)), ((), ()))
TN = (((0,), (0,)), ((), ()))


def _pcall(body, **kw):
    return pl.pallas_call(body, **kw)


def _params(**kw):
    return pltpu.CompilerParams(vmem_limit_bytes=VMEM_LIMIT_BYTES, **kw)


def _tile(dim, cap, unit=128):
    if dim <= cap:
        return dim
    t = (cap // unit) * unit
    while t >= unit:
        if dim % t == 0:
            return t
        t -= unit
    raise ValueError(f"no tile for {dim} under {cap}")


def _dot(a, b, dims):
    return lax.dot_general(a.astype(BF16), b.astype(BF16), dims, preferred_element_type=F32)


@functools.partial(jax.custom_vjp, nondiff_argnums=(2,))
def _dot_vjp(a, b, nt):
    return _dot(a, b, NT if nt else NN)


def _dot_vjp_fwd(a, b, nt):
    return _dot(a, b, NT if nt else NN), (a.astype(BF16), b.astype(BF16))


def _dot_vjp_bwd(nt, res, g):
    a, b = res
    if nt:
        return _dot(g, b, NN), _dot(g, a, TN)
    return _dot(g, b, NT), _dot(a, g, TN)


_dot_vjp.defvjp(_dot_vjp_fwd, _dot_vjp_bwd)


def _plain_dot(a, b, nt):
    return _dot(a, b, NT if nt else NN)


def _split_dot(x, mat, terms=2):
    out, rem = None, x
    for t in range(terms):
        part = rem.astype(BF16)
        d = lax.dot_general(part, mat, NN, preferred_element_type=F32)
        out = d if out is None else out + d
        if t + 1 < terms:
            rem = rem - part.astype(F32)
    return out


@functools.partial(jax.custom_vjp, nondiff_argnums=(3,))
def _split_dot_vjp(x, mat, mat_t, terms):
    return _split_dot(x, mat, terms)


def _split_dot_vjp_fwd(x, mat, mat_t, terms):
    return _split_dot(x, mat, terms), mat_t


def _split_dot_vjp_bwd(terms, mat_t, g):
    return _split_dot(g, mat_t, terms), None, None


_split_dot_vjp.defvjp(_split_dot_vjp_fwd, _split_dot_vjp_bwd)


def _plain_split(x, mat, mat_t, terms):
    return _split_dot(x, mat, terms)


def _tri(after):
    j = lax.broadcasted_iota(jnp.int32, (BLOCK, BLOCK), 0)
    s = lax.broadcasted_iota(jnp.int32, (BLOCK, BLOCK), 1)
    return jnp.where(j > s if after else j < s, 1.0, 0.0).astype(BF16)


def _suffix_sum(x):
    return _split_dot(x, _tri(True))


@jax.custom_vjp
def _suffix_sum_vjp(x):
    return _suffix_sum(x)


def _suffix_sum_vjp_fwd(x):
    return _suffix_sum(x), None


def _suffix_sum_vjp_bwd(_, g):
    return (_split_dot(g, _tri(False)),)


_suffix_sum_vjp.defvjp(_suffix_sum_vjp_fwd, _suffix_sum_vjp_bwd)


def _in(a, block, imap):
    return (a, block, imap)


def _out(shape, dtype, block, imap, acc=False):
    return (shape, dtype, block, imap, acc)


def tcall(fn, grid, ins, outs, name):
    nin = len(ins)
    ngrid = len(grid)

    def body(*refs):
        ids = tuple(pl.program_id(a) for a in range(ngrid))
        res = fn(ids, *[r[...] for r in refs[:nin]])
        first = ids[0] == 0
        for a in range(1, ngrid):
            first = jnp.logical_and(first, ids[a] == 0)
        for o_ref, r, spec in zip(refs[nin:], res, outs):
            if spec[4]:
                @pl.when(first)
                def _(o_ref=o_ref):
                    o_ref[...] = jnp.zeros(o_ref.shape, o_ref.dtype)
                o_ref[...] += r.astype(o_ref.dtype)
            else:
                o_ref[...] = r.astype(o_ref.dtype)

    return _pcall(
        body, name=name, grid=grid,
        in_specs=[pl.BlockSpec(b, m) for (_, b, m) in ins],
        out_specs=[pl.BlockSpec(b, m) for (_, _, b, m, _) in outs],
        out_shape=[jax.ShapeDtypeStruct(s, d) for (s, d, _, _, _) in outs],
        compiler_params=_params(),
    )(*[a for (a, _, _) in ins])


def _row(a, tm, width=None, cb=0):
    width = a.shape[1] if width is None else width
    return _in(a, (tm, width), lambda i, cb=cb: (i, cb))


def _full(a):
    zeros = (0,) * a.ndim
    return _in(a, a.shape, lambda *ids: zeros)


def _row_out(n, width, dtype, tm):
    return _out((n, width), dtype, (tm, width), lambda i: (i, 0))


def _acc_out(shape):
    zeros = (0,) * len(shape)
    return _out(shape, F32, shape, lambda *ids: zeros, acc=True)


def mm(a, b, mode, name, *, out_dtype=F32, scale=1.0, res=None):
    if mode == "nn":
        (m, k), (k2, n) = a.shape, b.shape
    elif mode == "nt":
        (m, k), (n, k2) = a.shape, b.shape
    else:
        (k, m), (k2, n) = a.shape, b.shape
    assert k == k2, (a.shape, b.shape, mode)
    tm, tn, tk = _tile(m, 512), _tile(n, 1408), _tile(k, 1408)
    nk = k // tk
    dims = {"nn": NN, "nt": NT, "tn": TN}[mode]
    has_res = res is not None

    def body(*refs):
        if has_res:
            a_ref, b_ref, r_ref, o_ref, acc_ref = refs
        else:
            a_ref, b_ref, o_ref, acc_ref = refs
        kk = pl.program_id(2)

        @pl.when(kk == 0)
        def _():
            acc_ref[...] = jnp.zeros(acc_ref.shape, F32)

        acc_ref[...] += _dot(a_ref[...], b_ref[...], dims)

        @pl.when(kk == nk - 1)
        def _():
            out = acc_ref[...]
            if scale != 1.0:
                out = out * scale
            if has_res:
                out = out + r_ref[...]
            o_ref[...] = out.astype(o_ref.dtype)

    a_spec = (pl.BlockSpec((tk, tm), lambda i, j, kk: (kk, i)) if mode == "tn"
              else pl.BlockSpec((tm, tk), lambda i, j, kk: (i, kk)))
    b_spec = (pl.BlockSpec((tn, tk), lambda i, j, kk: (j, kk)) if mode == "nt"
              else pl.BlockSpec((tk, tn), lambda i, j, kk: (kk, j)))
    in_specs = [a_spec, b_spec]
    args = [a, b]
    if has_res:
        in_specs.append(pl.BlockSpec((tm, tn), lambda i, j, kk: (i, j)))
        args.append(res)
    return _pcall(
        body, name=name, grid=(m // tm, n // tn, nk),
        in_specs=in_specs,
        out_specs=pl.BlockSpec((tm, tn), lambda i, j, kk: (i, j)),
        out_shape=jax.ShapeDtypeStruct((m, n), out_dtype),
        scratch_shapes=[pltpu.VMEM((tm, tn), F32)],
        compiler_params=_params(dimension_semantics=("parallel", "parallel", "arbitrary")),
    )(*args)


def _rms(x, g):
    return x * lax.rsqrt(jnp.mean(x * x, axis=-1, keepdims=True) + RMS_EPS) * g


def _silu_mul(gate, up):
    return gate / (1.0 + jnp.exp(-gate)) * up


def _indicator(shape, head_axis, mod):
    lane = lax.broadcasted_iota(jnp.int32, shape, head_axis)
    other = lax.broadcasted_iota(jnp.int32, shape, 1 - head_axis)
    lane = jnp.bitwise_and(lane, HEAD_DIM - 1) if mod else jnp.right_shift(lane, 6)
    return jnp.where(lane == other, 1.0, 0.0).astype(BF16)


def _head_rms(split, xs, g):
    w = xs.shape[1]
    to_head, from_head = _indicator((w, BLOCK), 0, False), _indicator((BLOCK, w), 1, False)
    to_lane, from_lane = _indicator((HEAD_DIM, w), 1, True), _indicator((w, HEAD_DIM), 0, True)
    ss = split(xs * xs, to_head, from_head, 3)
    r = lax.rsqrt(ss * (1.0 / HEAD_DIM) + RMS_EPS)
    g_all = split(jnp.broadcast_to(g, (8, HEAD_DIM)), to_lane, from_lane, 3)[0:1]
    return xs * split(r, from_head, to_head, 3) * g_all


def _prep(split, x, qg, kg, segs):
    parts = []
    for start, width, kind in segs:
        xs = x[:, start:start + width]
        parts.append(xs if kind == "raw" else _head_rms(split, xs, qg if kind == "q" else kg))
    return jnp.concatenate(parts, axis=1)


def prep_fwd(x, qg, kg, segs, name):
    n, w = x.shape
    tm = _tile(n, 256, 8)
    (out,) = tcall(lambda ids, xt, a, b: (_prep(_plain_split, xt, a, b, segs),), (n // tm,),
                   [_row(x, tm), _full(qg), _full(kg)], [_row_out(n, w, BF16, tm)], name)
    return out


def prep_bwd(x, qg, kg, segs, grads, gather, name):
    n, w = x.shape
    tm = _tile(n, 128, 8)

    def fn(ids, xt, a, b, *t):
        _, vjp = jax.vjp(lambda x_, a_, b_: _prep(_split_dot_vjp, x_, a_, b_, segs), xt, a, b)
        return vjp(gather(*t))

    return tcall(fn, (n // tm,), [_row(x, tm), _full(qg), _full(kg)] + [_row(a, tm) for a in grads],
                 [_row_out(n, w, BF16, tm), _acc_out(qg.shape), _acc_out(kg.shape)], name)


def rmsnorm_fwd(x, g, name):
    n, d = x.shape
    tm = _tile(n, 512, 8)
    (h,) = tcall(lambda ids, xt, gt: (_rms(xt, gt),), (n // tm,), [_row(x, tm), _full(g)],
                 [_row_out(n, d, BF16, tm)], name)
    return h


def rmsnorm_bwd(x, g, dh, dres, name):
    n, d = x.shape
    tm = _tile(n, 256, 8)

    def fn(ids, xt, gt, dht, *rest):
        _, vjp = jax.vjp(_rms, xt, gt)
        dx, dg = vjp(dht.astype(F32))
        if rest:
            dx = dx + rest[0]
        return dx, dg

    ins = [_row(x, tm), _full(g), _row(dh, tm)] + ([_row(dres, tm)] if dres is not None else [])
    return tcall(fn, (n // tm,), ins, [_row_out(n, d, F32, tm), _acc_out(g.shape)], name)


def ffn_fwd(x, g, w_gu, w_down, tag):
    n = x.shape[0]
    f = w_down.shape[0]
    h = rmsnorm_fwd(x, g, tag + "_norm")
    gu = mm(h, w_gu, "nn", tag + "_gu")
    tm = _tile(n, 128, 8)
    (a,) = tcall(lambda ids, gt, ut: (_silu_mul(gt, ut),), (n // tm,),
                 [_row(gu, tm, f, 0), _row(gu, tm, f, 1)], [_row_out(n, f, BF16, tm)], tag + "_act")
    y = mm(a, w_down, "nn", tag + "_down", scale=0.5, res=x)
    return y, (x, h, gu, a)


def ffn_bwd(dy, saved, g, w_gu, w_down, tag):
    x, h, gu, a = saved
    n = x.shape[0]
    f = w_down.shape[0]
    da = mm(dy, w_down, "nt", tag + "_da", scale=0.5)
    d_wdown = mm(a, dy, "tn", tag + "_dwd", scale=0.5)
    tm = _tile(n, 128, 8)

    def act_bwd(ids, gt, ut, dat):
        _, vjp = jax.vjp(_silu_mul, gt, ut)
        dg, du = vjp(dat)
        return (jnp.concatenate([dg, du], axis=1),)

    (dgu,) = tcall(act_bwd, (n // tm,), [_row(gu, tm, f, 0), _row(gu, tm, f, 1), _row(da, tm)],
                   [_row_out(n, 2 * f, BF16, tm)], tag + "_dact")
    dh = mm(dgu, w_gu, "nt", tag + "_dh")
    d_wgu = mm(h, dgu, "tn", tag + "_dwgu")
    dx, dg = rmsnorm_bwd(x, g, dh, dy, tag + "_dnorm")
    return dx, dg, d_wgu, d_wdown


def _alibi(n_heads):
    return [float(s) for s in np.asarray(2.0 ** (-8.0 * np.arange(1, n_heads + 1) / n_heads), dtype=np.float32)]


def _banded_tile(dot, first, q, kp, kc, vp, vc, sinks, *, hkv, grp, max_dist, step, slopes, want_lse):
    row = lax.broadcasted_iota(jnp.int32, (BLOCK, 2 * BLOCK), 0)
    col = lax.broadcasted_iota(jnp.int32, (BLOCK, 2 * BLOCK), 1)
    dist = row + BLOCK - col
    valid = (dist >= 0) & (dist <= max_dist) & ((col >= BLOCK) | jnp.logical_not(first))
    distf = dist.astype(F32)

    def head(hd, qh, k2, v2):
        s = dot(qh, k2, True) * (HEAD_DIM ** -0.5)
        s = jnp.where(valid, s - (slopes[hd] * step) * distf, NEG_BIG)
        m = jnp.max(s, axis=-1, keepdims=True)
        if sinks is not None:
            pick = lax.broadcasted_iota(jnp.int32, sinks.shape, 1) == hd
            sk = jnp.sum(jnp.where(pick, sinks, 0.0), axis=1, keepdims=True)
            m = jnp.maximum(m, sk)
        m = lax.stop_gradient(m)
        p = jnp.exp(s - m)
        denom = jnp.sum(p, axis=-1, keepdims=True)
        if sinks is not None:
            denom = denom + jnp.exp(sk - m)
        return dot(p / denom, v2, False), m + jnp.log(denom)

    outs, lses = [], []
    if grp == 1:
        low = lax.broadcasted_iota(jnp.int32, (BLOCK, BLOCK), 1) < HEAD_DIM
        for pr in range(hkv // 2):
            sl = slice(pr * BLOCK, (pr + 1) * BLOCK)
            q2 = q[:, sl]
            k2 = jnp.concatenate([kp[:, sl], kc[:, sl]], axis=0)
            v2 = jnp.concatenate([vp[:, sl], vc[:, sl]], axis=0)
            o0, l0 = head(2 * pr, jnp.where(low, q2, 0.0), k2, v2)
            o1, l1 = head(2 * pr + 1, jnp.where(low, 0.0, q2), k2, v2)
            outs.append(jnp.where(low, o0, o1))
            lses.append(jnp.where(low, l0, l1))
    else:
        for hk in range(hkv):
            sl = slice(hk * HEAD_DIM, (hk + 1) * HEAD_DIM)
            k2 = jnp.concatenate([kp[:, sl], kc[:, sl]], axis=0)
            v2 = jnp.concatenate([vp[:, sl], vc[:, sl]], axis=0)
            for gi in range(grp):
                hd = hk * grp + gi
                o_h, l_h = head(hd, q[:, hd * HEAD_DIM:(hd + 1) * HEAD_DIM], k2, v2)
                outs.append(o_h)
                lses.append(jnp.broadcast_to(l_h, (BLOCK, HEAD_DIM)))
    o = jnp.concatenate(outs, axis=1)
    if want_lse:
        return o, jnp.concatenate(lses, axis=1)
    return (o,)


def _banded_specs(view, qcol, kcol, vcol, wq, wkv):
    def at(colfn, prev):
        if prev:
            return lambda r, n: (jnp.maximum(n - 1, 0), colfn(r))
        return lambda r, n: (n, colfn(r))
    return [
        _in(view, (BLOCK, wq), at(qcol, False)),
        _in(view, (BLOCK, wkv), at(kcol, True)),
        _in(view, (BLOCK, wkv), at(kcol, False)),
        _in(view, (BLOCK, wkv), at(vcol, True)),
        _in(view, (BLOCK, wkv), at(vcol, False)),
    ]


def banded_fwd(view, dil, cols, sinks, cfg, name):
    ns = view.shape[0]
    nb = ns // BLOCK
    wq, wkv = cfg["hkv"] * cfg["grp"] * HEAD_DIM, cfg["hkv"] * HEAD_DIM
    has_sinks = sinks is not None

    def fn(ids, q, kp, kc, vp, vc, *rest):
        q, kp, kc, vp, vc = [a.astype(F32) for a in (q, kp, kc, vp, vc)]
        return _banded_tile(_plain_dot, ids[1] == 0, q, kp, kc, vp, vc, rest[0] if has_sinks else None, **cfg)

    ins = _banded_specs(view, *cols, wq, wkv) + ([_full(sinks)] if has_sinks else [])
    outs = [_out((ns, dil * wq), BF16, (BLOCK, wq), lambda r, n: (n, r))]
    if cfg["want_lse"]:
        outs.append(_out((ns, dil * wq), F32, (BLOCK, wq), lambda r, n: (n, r)))
    return tcall(fn, (dil, nb), ins, outs, name)


def banded_bwd(view, dil, cols, sinks, cfg, cts, name):
    ns = view.shape[0]
    nb = ns // BLOCK
    wq, wkv = cfg["hkv"] * cfg["grp"] * HEAD_DIM, cfg["hkv"] * HEAD_DIM
    has_sinks = sinks is not None
    assert len(cts) == (2 if cfg["want_lse"] else 1)

    def fn(ids, q, kp, kc, vp, vc, *rest):
        sk = rest[0] if has_sinks else None
        ct = rest[1 if has_sinks else 0:]
        first = ids[1] == 0

        def f(q, kp, kc, vp, vc, *s):
            return _banded_tile(_dot_vjp, first, q, kp, kc, vp, vc, s[0] if has_sinks else None, **cfg)

        prim = tuple(a.astype(F32) for a in (q, kp, kc, vp, vc)) + ((sk,) if has_sinks else ())
        _, vjp = jax.vjp(f, *prim)
        return vjp(tuple(c.astype(F32) for c in ct))

    ins = (_banded_specs(view, *cols, wq, wkv) + ([_full(sinks)] if has_sinks else [])
           + [_in(a, (BLOCK, wq), (lambda r, n, cf=cf: (n, cf(r)))) for (a, cf) in cts])
    blk = lambda w: _out((ns, dil * w), F32, (BLOCK, w), lambda r, n: (n, r))
    outs = [blk(wq), blk(wkv), blk(wkv), blk(wkv), blk(wkv)]
    if has_sinks:
        outs.append(_acc_out(sinks.shape))
    res = tcall(fn, (dil, nb), ins, outs, name)
    dq, dkp, dkc, dvp, dvc = res[:5]

    def shift_add(ids, kc_, kn_, vc_, vn_):
        keep = ids[1] < nb - 1
        return kc_ + jnp.where(keep, kn_, 0.0), vc_ + jnp.where(keep, vn_, 0.0)

    here = lambda r, n: (n, r)
    nxt = lambda r, n: (jnp.minimum(n + 1, nb - 1), r)
    dk, dv = tcall(shift_add, (dil, nb),
                   [_in(dkc, (BLOCK, wkv), here), _in(dkp, (BLOCK, wkv), nxt),
                    _in(dvc, (BLOCK, wkv), here), _in(dvp, (BLOCK, wkv), nxt)],
                   [blk(wkv), blk(wkv)], name + "_shift")
    return (dq, dk, dv) + tuple(res[5:])


def _log_sigmoid(z):
    return jnp.minimum(z, 0.0) - jnp.log(1.0 + jnp.exp(-jnp.abs(z)))


def _sb_pair(dot, suffix, qh, kb, vb, r_in, diag):
    z = dot(qh, kb, True) * (HEAD_DIM ** -0.5)
    row = lax.broadcasted_iota(jnp.int32, (BLOCK, BLOCK), 0)
    col = lax.broadcasted_iota(jnp.int32, (BLOCK, BLOCK), 1)
    mask = (col < row) | jnp.logical_not(diag)
    lsp = _log_sigmoid(z)
    log_keep = jnp.where(mask, lsp - z, 0.0)
    log_after = suffix(log_keep) + r_in
    a = jnp.where(mask, jnp.exp(lsp + log_after), 0.0)
    return dot(a, vb, False), r_in + jnp.sum(log_keep, axis=1, keepdims=True)


def sb_fwd(qkv, qcb, kcb, vcb, name):
    s = qkv.shape[0]
    nb = s // BLOCK
    pairs = B_HEADS // 2

    def body(q_ref, k_ref, v_ref, o_ref):
        n = pl.program_id(1)
        low = lax.broadcasted_iota(jnp.int32, (BLOCK, BLOCK), 1) < HEAD_DIM
        q2 = q_ref[...].astype(F32)
        accs = []
        for h in range(2):
            qm = jnp.where(low, q2, 0.0) if h == 0 else jnp.where(low, 0.0, q2)

            def cond(c):
                return jnp.logical_and(c[0] >= 0, c[3] > SB_SKIP_LOG)

            def step(c, qm=qm):
                kb, r, acc, _ = c
                rows = pl.ds(pl.multiple_of(kb * BLOCK, BLOCK), BLOCK)
                o_part, r_out = _sb_pair(_plain_dot, _suffix_sum, qm, k_ref[rows, :], v_ref[rows, :], r, kb == n)
                return kb - 1, r_out, acc + o_part, jnp.max(r_out)

            init = (n, jnp.zeros((BLOCK, 1), F32), jnp.zeros((BLOCK, BLOCK), F32), jnp.float32(0.0))
            accs.append(lax.while_loop(cond, step, init)[2])
        o_ref[...] = jnp.where(low, accs[0], accs[1]).astype(o_ref.dtype)

    return _pcall(
        body, name=name, grid=(pairs, nb),
        in_specs=[pl.BlockSpec((BLOCK, BLOCK), lambda p, n: (n, qcb + p)),
                  pl.BlockSpec((s, BLOCK), lambda p, n: (0, kcb + p)),
                  pl.BlockSpec((s, BLOCK), lambda p, n: (0, vcb + p))],
        out_specs=pl.BlockSpec((BLOCK, BLOCK), lambda p, n: (n, p)),
        out_shape=jax.ShapeDtypeStruct((s, pairs * BLOCK), BF16),
        compiler_params=_params(),
    )(qkv, qkv, qkv)


def sb_bwd(qkv, qcb, kcb, vcb, do, docb, name):
    s = qkv.shape[0]
    nb = s // BLOCK
    pairs = B_HEADS // 2

    def body(q_ref, k_ref, v_ref, do_ref, dq_ref, dk_ref, dv_ref, r_ref):
        n = pl.program_id(1)

        @pl.when(n == 0)
        def _():
            dk_ref[...] = jnp.zeros(dk_ref.shape, F32)
            dv_ref[...] = jnp.zeros(dv_ref.shape, F32)

        low = lax.broadcasted_iota(jnp.int32, (BLOCK, BLOCK), 1) < HEAD_DIM
        q2 = q_ref[...].astype(F32)
        do2 = do_ref[...].astype(F32)
        dqs = []
        for h in range(2):
            qh = jnp.where(low, q2, 0.0) if h == 0 else jnp.where(low, 0.0, q2)
            doh = jnp.where(low, do2, 0.0) if h == 0 else jnp.where(low, 0.0, do2)

            def cond(c):
                return jnp.logical_and(c[0] >= 0, c[2] > SB_SKIP_LOG)

            def down(c, qh=qh):
                kb, r, _ = c
                r_ref[kb] = r
                rows = pl.ds(pl.multiple_of(kb * BLOCK, BLOCK), BLOCK)
                z = _dot(qh, k_ref[rows, :], NT) * (HEAD_DIM ** -0.5)
                row = lax.broadcasted_iota(jnp.int32, (BLOCK, BLOCK), 0)
                col = lax.broadcasted_iota(jnp.int32, (BLOCK, BLOCK), 1)
                mask = (col < row) | (kb != n)
                log_keep = jnp.where(mask, _log_sigmoid(z) - z, 0.0)
                r_out = r + jnp.sum(log_keep, axis=1, keepdims=True)
                return kb - 1, r_out, jnp.max(r_out)

            last = lax.while_loop(cond, down, (n, jnp.zeros((BLOCK, 1), F32), jnp.float32(0.0)))[0] + 1

            def up(kb, c, qh=qh, doh=doh):
                dq, g_r = c
                rows = pl.ds(pl.multiple_of(kb * BLOCK, BLOCK), BLOCK)
                diag = kb == n
                _, vjp = jax.vjp(lambda q_, k_, v_, r_: _sb_pair(_dot_vjp, _suffix_sum_vjp, q_, k_, v_, r_, diag),
                                 qh, k_ref[rows, :].astype(F32), v_ref[rows, :].astype(F32), r_ref[kb])
                dq_c, dk_c, dv_c, g_in = vjp((doh, g_r))
                dk_ref[rows, :] += dk_c
                dv_ref[rows, :] += dv_c
                return dq + dq_c, g_in

            dqs.append(lax.fori_loop(last, n + 1, up, (jnp.zeros((BLOCK, BLOCK), F32), jnp.zeros((BLOCK, 1), F32)))[0])
        dq_ref[...] = jnp.where(low, dqs[0], dqs[1])

    wide = jax.ShapeDtypeStruct((s, pairs * BLOCK), F32)
    return _pcall(
        body, name=name, grid=(pairs, nb),
        in_specs=[pl.BlockSpec((BLOCK, BLOCK), lambda p, n: (n, qcb + p)),
                  pl.BlockSpec((s, BLOCK), lambda p, n: (0, kcb + p)),
                  pl.BlockSpec((s, BLOCK), lambda p, n: (0, vcb + p)),
                  pl.BlockSpec((BLOCK, BLOCK), lambda p, n: (n, docb + p))],
        out_specs=[pl.BlockSpec((BLOCK, BLOCK), lambda p, n: (n, p)),
                   pl.BlockSpec((s, BLOCK), lambda p, n: (0, p)),
                   pl.BlockSpec((s, BLOCK), lambda p, n: (0, p))],
        out_shape=[wide, wide, wide],
        scratch_shapes=[pltpu.VMEM((nb, BLOCK, 1), F32)],
        compiler_params=_params(),
    )(qkv, qkv, qkv, do)


def _xa_tile(dot, q, kv, qg, kg):
    hd = q.shape[1] // X_HEADS
    outs = []
    for h in range(X_HEADS):
        qh = _rms(q[:, h * hd:(h + 1) * hd], qg)
        kh = _rms(kv[:, h * hd:(h + 1) * hd], kg)
        vh = kv[:, (X_HEADS + h) * hd:(X_HEADS + h + 1) * hd]
        sc = dot(qh, kh, True) * (hd ** -0.5)
        m = lax.stop_gradient(jnp.max(sc, axis=-1, keepdims=True))
        p = jnp.exp(sc - m)
        outs.append(dot(p / jnp.sum(p, axis=-1, keepdims=True), vh, False))
    return jnp.concatenate(outs, axis=1)


def xa_core_fwd(q, kv, qg, kg, name):
    n, d = q.shape
    tm = _tile(n, 256, 8)
    (o,) = tcall(lambda ids, qt, kvt, qgt, kgt: (_xa_tile(_plain_dot, qt, kvt, qgt, kgt),), (n // tm,),
                 [_row(q, tm), _full(kv), _full(qg), _full(kg)], [_row_out(n, d, BF16, tm)], name)
    return o


def xa_core_bwd(q, kv, qg, kg, do, name):
    n, d = q.shape
    tm = _tile(n, 256, 8)

    def fn(ids, qt, kvt, qgt, kgt, dot_):
        _, vjp = jax.vjp(functools.partial(_xa_tile, _dot_vjp), qt, kvt, qgt, kgt)
        return vjp(dot_.astype(F32))

    return tcall(fn, (n // tm,), [_row(q, tm), _full(kv), _full(qg), _full(kg), _row(do, tm)],
                 [_row_out(n, d, BF16, tm), _acc_out(kv.shape), _acc_out(qg.shape), _acc_out(kg.shape)], name)


def _ev_reorder(a):
    return jnp.concatenate([a[..., 0:512], a[..., 768:2304], a[..., 512:768]], axis=-1)


def _ev_restore(a):
    return jnp.concatenate([a[..., 0:512], a[..., 2048:2304], a[..., 512:2048]], axis=-1)


_EV_SEGS = ((0, 512, "q"), (512, 1536, "raw"), (2048, 128, "k"), (2176, 128, "raw"))
_A_CFG = dict(hkv=A_KV_HEADS, grp=A_Q_HEADS // A_KV_HEADS, max_dist=BLOCK - 1, step=1.0, slopes=_alibi(A_Q_HEADS),
              want_lse=False)
_A_COLS = (lambda r: 0, lambda r: 16, lambda r: 17)


def even_mixer_fwd(x, g, w_in, qg, kg, sinks, w_out, tag):
    h = rmsnorm_fwd(x, g, tag + "_norm")
    qkv = mm(h, w_in, "nn", tag + "_in")
    ops = prep_fwd(qkv, qg, kg, _EV_SEGS, tag + "_prep")
    (o_a,) = banded_fwd(ops, 1, _A_COLS, sinks, _A_CFG, tag + "_swa")
    o_b = sb_fwd(ops, 4, 8, 12, tag + "_sb")
    o = jnp.concatenate([o_a, o_b], axis=1)
    y = mm(o, w_out, "nn", tag + "_out", res=x)
    return y, (x, h, qkv, ops, o)


def even_mixer_bwd(dy, saved, g, w_in, qg, kg, sinks, w_out, tag):
    x, h, qkv, ops, o = saved
    do = mm(dy, w_out, "nt", tag + "_do")
    d_wout = mm(o, dy, "tn", tag + "_dwout")
    dqa, dka, dva, dsinks = banded_bwd(ops, 1, _A_COLS, sinks, _A_CFG, [(do, lambda r: 0)], tag + "_dswa")
    dqb, dkb, dvb = sb_bwd(ops, 4, 8, 12, do, 4, tag + "_dsb")
    dqkv, dqg, dkg = prep_bwd(qkv, qg, kg, _EV_SEGS, (dqa, dqb, dkb, dvb, dka, dva),
                              lambda *t: jnp.concatenate(t, axis=1), tag + "_dqkv")
    dh = mm(dqkv, w_in, "nt", tag + "_dh")
    d_win = mm(h, dqkv, "tn", tag + "_dwin")
    dx, dg = rmsnorm_bwd(x, g, dh, dy, tag + "_dnorm")
    return dx, dg, d_win, dqg, dkg, dsinks, d_wout


def _c_cfg(window, dil):
    return dict(hkv=C_HEADS, grp=1, max_dist=window // dil, step=float(dil), slopes=_alibi(C_HEADS), want_lse=True)


_C_COLS = (lambda r: 3 * r, lambda r: 3 * r + 1, lambda r: 3 * r + 2)
_OD_SEGS = ((0, 1024, "q"), (1024, 1024, "k"), (2048, 1024, "raw"))


def _combine(o1, o2, o3, l1, l2, l3):
    m = lax.stop_gradient(jnp.maximum(jnp.maximum(l1, l2), l3))
    e1, e2, e3 = jnp.exp(l1 - m), jnp.exp(l2 - m), jnp.exp(l3 - m)
    tot = e1 + e2 + e3
    return (e1 / tot) * o1 + (e2 / tot) * o2 + (e3 / tot) * o3


def odd_mixer_fwd(x, g, w_in, qg, kg, w_out, tag):
    n, d = x.shape
    h = rmsnorm_fwd(x, g, tag + "_norm")
    qkv = mm(h, w_in, "nn", tag + "_in")
    w3 = qkv.shape[1]
    ops = prep_fwd(qkv, qg, kg, _OD_SEGS, tag + "_prep")
    os_, ls_ = [], []
    for window, dil in C_PATTERNS:
        o_p, l_p = banded_fwd(ops.reshape(n // dil, dil * w3), dil, _C_COLS, None, _c_cfg(window, dil),
                              f"{tag}_dil{dil}")
        os_.append(o_p.reshape(n, d))
        ls_.append(l_p.reshape(n, d))
    tm = _tile(n, 128, 8)
    (o,) = tcall(lambda ids, *t: (_combine(*[a.astype(F32) for a in t]),), (n // tm,),
                 [_row(a, tm) for a in os_ + ls_], [_row_out(n, d, BF16, tm)], tag + "_comb")
    y = mm(o, w_out, "nn", tag + "_out", res=x)
    return y, (x, h, qkv, ops, os_, ls_, o)


def odd_mixer_bwd(dy, saved, g, w_in, qg, kg, w_out, tag):
    x, h, qkv, ops, os_, ls_, o = saved
    n, d = x.shape
    w3 = qkv.shape[1]
    do = mm(dy, w_out, "nt", tag + "_do")
    d_wout = mm(o, dy, "tn", tag + "_dwout")
    tm = _tile(n, 128, 8)

    def comb_bwd(ids, *t):
        _, vjp = jax.vjp(_combine, *[a.astype(F32) for a in t[:6]])
        return vjp(t[6])

    cts = tcall(comb_bwd, (n // tm,), [_row(a, tm) for a in os_ + ls_ + [do]],
                [_row_out(n, d, F32, tm) for _ in range(6)], tag + "_dcomb")
    dqs, dks, dvs = [], [], []
    for p, (window, dil) in enumerate(C_PATTERNS):
        lay = (n // dil, dil * d)
        dq, dk, dv = banded_bwd(
            ops.reshape(n // dil, dil * w3), dil, _C_COLS, None, _c_cfg(window, dil),
            [(cts[p].reshape(lay), lambda r: r), (cts[3 + p].reshape(lay), lambda r: r)], f"{tag}_ddil{dil}")
        dqs.append(dq.reshape(n, d))
        dks.append(dk.reshape(n, d))
        dvs.append(dv.reshape(n, d))

    def gather(*t):
        return jnp.concatenate([t[0] + t[1] + t[2], t[3] + t[4] + t[5], t[6] + t[7] + t[8]], axis=1)

    dqkv, dqg, dkg = prep_bwd(qkv, qg, kg, _OD_SEGS, dqs + dks + dvs, gather, tag + "_dqkv")
    dh = mm(dqkv, w_in, "nt", tag + "_dh")
    d_win = mm(h, dqkv, "tn", tag + "_dwin")
    dx, dg = rmsnorm_bwd(x, g, dh, dy, tag + "_dnorm")
    return dx, dg, d_win, dqg, dkg, d_wout


def xa_fwd(x, mem, g, gm, w_q, w_kv, qg, kg, w_o, tag):
    h = rmsnorm_fwd(x, g, tag + "_norm")
    q = mm(h, w_q, "nn", tag + "_q")
    mn = rmsnorm_fwd(mem, gm, tag + "_mnorm")
    kv = mm(mn, w_kv, "nn", tag + "_kv")
    o = xa_core_fwd(q, kv, qg, kg, tag + "_core")
    y = mm(o, w_o, "nn", tag + "_o", res=x)
    return y, (x, h, q, mn, kv, o)


def xa_bwd(dy, saved, mem, g, gm, w_q, w_kv, qg, kg, w_o, tag):
    x, h, q, mn, kv, o = saved
    do = mm(dy, w_o, "nt", tag + "_do", out_dtype=BF16)
    d_wo = mm(o, dy, "tn", tag + "_dwo")
    dq, dkv, dqg, dkg = xa_core_bwd(q, kv, qg, kg, do, tag + "_dcore")
    dh = mm(dq, w_q, "nt", tag + "_dh")
    d_wq = mm(h, dq, "tn", tag + "_dwq")
    dx, dg = rmsnorm_bwd(x, g, dh, dy, tag + "_dnorm")
    dmn = mm(dkv, w_kv, "nt", tag + "_dmn")
    d_wkv = mm(mn, dkv, "tn", tag + "_dwkv")
    _, dgm = rmsnorm_bwd(mem, gm, dmn, None, tag + "_dmnorm")
    return dx, dg, dgm, d_wq, d_wkv, dqg, dkg, d_wo


def loss_head(y, target, name):
    n, d = y.shape
    tm = _tile(n, 512, 8)

    def fn(ids, yt, tt):
        e = yt - tt
        return e * (1.0 / d), jnp.sum(e * e, axis=0, keepdims=True)

    return tcall(fn, (n // tm,), [_row(y, tm), _row(target, tm)], [_row_out(n, d, F32, tm), _acc_out((1, d))], name)


_ANY = pl.BlockSpec(memory_space=pl.ANY)


def all_gather_blocks(blocks):
    nb = len(blocks)

    def body(*refs):
        x_refs, out_refs = refs[:nb], refs[nb:2 * nb]
        send_sems, recv_sems, local_sems = refs[2 * nb:]
        x, y, c = lax.axis_index("x"), lax.axis_index("y"), lax.axis_index("c")
        me, sibling = (x, y, c), (x, y, 1 - c)
        chips = [(1 - x, y), (x, 1 - y), (1 - x, 1 - y)]

        def copy(b, k, blk, to, own=False):
            px, py, pc = blk
            slot = out_refs[b].at[4 * px + 2 * py + pc]
            return pltpu.make_async_remote_copy(
                src_ref=x_refs[b] if own else slot, dst_ref=slot,
                send_sem=send_sems.at[7 * b + k], recv_sem=recv_sems.at[7 * b + k], device_id=to, device_id_type=MESH)

        mine = [pltpu.make_async_copy(x_refs[b], out_refs[b].at[4 * x + 2 * y + c], local_sems.at[b]) for b in range(nb)]
        for cp in mine:
            cp.start()
        first = []
        for b in range(nb):
            first.append(copy(b, 0, me, sibling, own=True))
            first += [copy(b, 1 + j, me, (*chip, c), own=True) for j, chip in enumerate(chips)]
        for cp in first:
            cp.start()
        passed = []
        for j, chip in enumerate(chips):
            for b in range(nb):
                copy(b, 1 + j, (*chip, c), me).wait_recv()
                fwd = copy(b, 4 + j, (*chip, c), sibling)
                fwd.start()
                passed.append(fwd)
        for b in range(nb):
            copy(b, 0, sibling, me).wait_recv()
            for j, chip in enumerate(chips):
                copy(b, 4 + j, (*chip, 1 - c), me).wait_recv()
        for cp in first + passed:
            cp.wait_send()
        for cp in mine:
            cp.wait()

    return _pcall(
        body, name="weights_all_gather",
        in_specs=[_ANY] * nb, out_specs=[_ANY] * nb,
        out_shape=[jax.ShapeDtypeStruct((N_DEV,) + a.shape, a.dtype) for a in blocks],
        scratch_shapes=[pltpu.SemaphoreType.DMA((7 * nb,)), pltpu.SemaphoreType.DMA((7 * nb,)),
                        pltpu.SemaphoreType.DMA((nb,))],
    )(*blocks)


def pair_exchange(bufs):
    nb = len(bufs)

    def body(*refs):
        srcs, dsts = refs[:nb], refs[nb:2 * nb]
        send_sems, recv_sems = refs[2 * nb:]
        x, y, c = lax.axis_index("x"), lax.axis_index("y"), lax.axis_index("c")
        copies = []
        for b in range(nb):
            for j in range(4):
                cp = pltpu.make_async_remote_copy(
                    src_ref=srcs[b].at[2 * j + (1 - c)], dst_ref=dsts[b].at[j], send_sem=send_sems.at[4 * b + j],
                    recv_sem=recv_sems.at[4 * b + j], device_id=(x, y, 1 - c), device_id_type=MESH)
                cp.start()
                copies.append(cp)
        for cp in copies:
            cp.wait()

    return _pcall(
        body, name="grads_pair_exchange",
        in_specs=[_ANY] * nb, out_specs=[_ANY] * nb,
        out_shape=[jax.ShapeDtypeStruct((4,) + a.shape[1:], a.dtype) for a in bufs],
        scratch_shapes=[pltpu.SemaphoreType.DMA((4 * nb,)), pltpu.SemaphoreType.DMA((4 * nb,))],
    )(*bufs)


def pair_sum(g, got, c, out_dtype, name):
    r, w = g.shape[1:]
    tr = _tile(r, 512, 16)

    def body(c_ref, a_ref, b_ref, o_ref):
        o_ref[...] = (a_ref[...] + b_ref[...]).astype(o_ref.dtype)

    return _pcall(
        body, name=name,
        grid_spec=pltpu.PrefetchScalarGridSpec(
            num_scalar_prefetch=1, grid=(4, r // tr),
            in_specs=[pl.BlockSpec((None, tr, w), lambda j, i, c_ref: (2 * j + c_ref[0], i, 0)),
                      pl.BlockSpec((None, tr, w), lambda j, i, c_ref: (j, i, 0))],
            out_specs=pl.BlockSpec((None, tr, w), lambda j, i, c_ref: (j, i, 0))),
        out_shape=jax.ShapeDtypeStruct((4,) + g.shape[1:], out_dtype),
        compiler_params=_params(),
    )(c, g, got)


def chip_exchange(parts):
    nb = len(parts)

    def body(*refs):
        srcs, dsts = refs[:nb], refs[nb:2 * nb]
        send_sems, recv_sems, local_sems = refs[2 * nb:]
        x, y, c = lax.axis_index("x"), lax.axis_index("y"), lax.axis_index("c")
        my_chip = 2 * x + y
        copies = []
        for b in range(nb):
            mine = pltpu.make_async_copy(srcs[b].at[my_chip], dsts[b].at[my_chip], local_sems.at[b])
            mine.start()
            copies.append(mine)
            for k, (tx, ty) in enumerate([(1 - x, y), (x, 1 - y), (1 - x, 1 - y)]):
                cp = pltpu.make_async_remote_copy(
                    src_ref=srcs[b].at[2 * tx + ty], dst_ref=dsts[b].at[my_chip], send_sem=send_sems.at[3 * b + k],
                    recv_sem=recv_sems.at[3 * b + k], device_id=(tx, ty, c), device_id_type=MESH)
                cp.start()
                copies.append(cp)
        for cp in copies:
            cp.wait()

    return _pcall(
        body, name="grads_chip_exchange",
        in_specs=[_ANY] * nb, out_specs=[_ANY] * nb,
        out_shape=[jax.ShapeDtypeStruct(a.shape, a.dtype) for a in parts],
        scratch_shapes=[pltpu.SemaphoreType.DMA((3 * nb,)), pltpu.SemaphoreType.DMA((3 * nb,)),
                        pltpu.SemaphoreType.DMA((nb,))],
    )(*parts)


def chip_sum(parts, name):
    r, w = parts.shape[1:]
    tr = _tile(r, 512, 16)
    spec = lambda j: _in(parts, (None, tr, w), lambda i, j=j: (j, i, 0))

    def fn(ids, a, b, c_, d):
        a, b, c_, d = [t.astype(F32) for t in (a, b, c_, d)]
        return (((a + b) + c_) + d,)

    (out,) = tcall(fn, (r // tr,), [spec(j) for j in range(4)],
                   [_out((r, w), F32, (tr, w), lambda i: (i, 0))], name)
    return out


def adamw(w, g, m, v, name):
    shape = w.shape
    cols = shape[-1]
    rows = int(np.prod(shape[:-1]))
    w2, g2, m2, v2 = [a.reshape(rows, cols) for a in (w, g, m, v)]
    tr = _tile(rows, 256, 8) if rows % 8 == 0 else rows

    def fn(ids, wt, gt, mt, vt):
        m_new = ADAM_B1 * mt + (1.0 - ADAM_B1) * gt
        v_new = ADAM_B2 * vt + (1.0 - ADAM_B2) * (gt * gt)
        m_hat = m_new / (1.0 - ADAM_B1 ** ADAM_STEP)
        v_hat = v_new / (1.0 - ADAM_B2 ** ADAM_STEP)
        delta = -ADAM_LR * (m_hat / (jnp.sqrt(v_hat) + ADAM_EPS) + ADAM_WD * wt)
        return delta, m_new, v_new

    res = tcall(fn, (rows // tr,), [_row(a, tr) for a in (w2, g2, m2, v2)],
                [_row_out(rows, cols, F32, tr) for _ in range(3)], name)
    return [a.reshape(shape) for a in res]


_MATS = [("ffn1_w_gu", "col"), ("ffn1_w_down", "row"), ("ev_w_in", "col"), ("ev_w_out", "row"),
         ("od_w_in", "col"), ("od_w_out", "row"), ("xa_w_q", "row"), ("xa_w_kv", "col"), ("xa_w_o", "row"),
         ("ffn2_w_gu", "col"), ("ffn2_w_down", "row")]
_VECS = ["ffn1_norm", "mix_norm", "ev_q_gain", "ev_k_gain", "ev_sinks", "od_q_gain", "od_k_gain", "xa_norm",
         "xa_mem_norm", "xa_q_gain", "xa_k_gain", "ffn2_norm"]
_WEIGHTS = ["ffn1_norm", "ffn1_w_gu", "ffn1_w_down", "mix_norm", "ev_w_in", "ev_q_gain", "ev_k_gain", "ev_sinks",
            "ev_w_out", "od_w_in", "od_q_gain", "od_k_gain", "od_w_out", "xa_norm", "xa_mem_norm", "xa_w_q", "xa_w_kv",
            "xa_q_gain", "xa_k_gain", "xa_w_o", "ffn2_norm", "ffn2_w_gu", "ffn2_w_down"]


_GROUPS = [["ffn1_w_gu", "ffn2_w_gu"], ["ev_w_in"], ["od_w_in"], ["xa_w_kv"],
           ["ffn1_w_down", "ffn2_w_down", "ev_w_out", "od_w_out", "xa_w_q", "xa_w_o"]]
_AXIS = dict(_MATS)


def _gather_weights(shards):
    blocks = []
    for names in _GROUPS:
        rows = [shards[n].reshape(-1, shards[n].shape[-1]).astype(BF16) for n in names]
        blocks.append(rows[0] if len(rows) == 1 else jnp.concatenate(rows, axis=0))
    gathered = all_gather_blocks(blocks)
    full = {}
    for names, got in zip(_GROUPS, gathered):
        off = 0
        for n in names:
            l, a, b = shards[n].shape
            seg = got[:, off:off + l * a, :].reshape(N_DEV, l, a, b)
            off += l * a
            if _AXIS[n] == "row":
                full[n] = seg.transpose(1, 0, 2, 3).reshape(l, N_DEV * a, b)
            else:
                full[n] = seg.transpose(1, 2, 0, 3).reshape(l, a, N_DEV * b)
    return full


def _reduce_gradients(mats, vecs, c):
    bufs = []
    for names in _GROUPS:
        rows = []
        for n in names:
            gr = mats[n]
            l, a, b = gr.shape
            if _AXIS[n] == "row":
                rows.append(gr.reshape(l, N_DEV, a // N_DEV, b).transpose(1, 0, 2, 3).reshape(N_DEV, -1, b))
            else:
                rows.append(gr.reshape(l, a, N_DEV, b // N_DEV).transpose(2, 0, 1, 3).reshape(N_DEV, l * a, b // N_DEV))
        bufs.append(rows[0] if len(rows) == 1 else jnp.concatenate(rows, axis=1))
    vec = jnp.concatenate([vecs[n].reshape(-1) for n in _VECS])
    vec = jnp.pad(vec, (0, -vec.shape[0] % (16 * LANES)))
    bufs.append(jnp.broadcast_to(vec.reshape(1, -1, LANES), (N_DEV, vec.shape[0] // LANES, LANES)))
    got = pair_exchange(bufs)
    nm = len(_GROUPS)
    parts = [pair_sum(b, g, c, BF16 if i < nm else F32, f"grads_pair_sum{i}") for i, (b, g) in enumerate(zip(bufs, got))]
    sums = [chip_sum(p, f"grads_chip_sum{i}") for i, p in enumerate(chip_exchange(parts))]
    out = {}
    for names, tot in zip(_GROUPS, sums[:nm]):
        off = 0
        for n in names:
            l, a, b = mats[n].shape
            shape = (l, a // N_DEV, b) if _AXIS[n] == "row" else (l, a, b // N_DEV)
            out[n] = tot[off:off + shape[0] * shape[1]].reshape(shape)
            off += shape[0] * shape[1]
    flat, off = sums[nm].reshape(-1), 0
    for n in _VECS:
        out[n] = flat[off:off + vecs[n].size].reshape(vecs[n].shape)
        off += vecs[n].size
    return out


def _local_step(x, mem, target, w, full):
    depth = w["ffn1_norm"].shape[0]
    ev_in = _ev_reorder(full["ev_w_in"])
    row = lambda a, l: a[l:l + 1]
    saved = []
    for l in range(depth):
        t = f"l{l}"
        j = l // 2
        x, s1 = ffn_fwd(x, row(w["ffn1_norm"], l), full["ffn1_w_gu"][l], full["ffn1_w_down"][l], t + "_ffn1")
        if l % 2 == 0:
            x, s2 = even_mixer_fwd(x, row(w["mix_norm"], l), ev_in[j], row(w["ev_q_gain"], j), row(w["ev_k_gain"], j),
                                   row(w["ev_sinks"], j), full["ev_w_out"][j], t + "_ev")
        else:
            x, s2 = odd_mixer_fwd(x, row(w["mix_norm"], l), full["od_w_in"][j], row(w["od_q_gain"], j),
                                  row(w["od_k_gain"], j), full["od_w_out"][j], t + "_od")
        x, s3 = xa_fwd(x, mem, row(w["xa_norm"], l), row(w["xa_mem_norm"], l), full["xa_w_q"][l], full["xa_w_kv"][l],
                       row(w["xa_q_gain"], l), row(w["xa_k_gain"], l), full["xa_w_o"][l], t + "_xa")
        x, s4 = ffn_fwd(x, row(w["ffn2_norm"], l), full["ffn2_w_gu"][l], full["ffn2_w_down"][l], t + "_ffn2")
        saved.append((s1, s2, s3, s4))
    dx, sq = loss_head(x, target, "loss_head")
    loss = 0.5 * jnp.sum(sq) / x.shape[1]

    gm = {n: [None] * full[n].shape[0] for n, _ in _MATS}
    gv = {n: [None] * w[n].shape[0] for n in _VECS}
    for l in reversed(range(depth)):
        t = f"l{l}"
        j = l // 2
        s1, s2, s3, s4 = saved[l]
        dx, gv["ffn2_norm"][l], gm["ffn2_w_gu"][l], gm["ffn2_w_down"][l] = ffn_bwd(
            dx, s4, row(w["ffn2_norm"], l), full["ffn2_w_gu"][l], full["ffn2_w_down"][l], t + "_ffn2")
        (dx, gv["xa_norm"][l], gv["xa_mem_norm"][l], gm["xa_w_q"][l], gm["xa_w_kv"][l], gv["xa_q_gain"][l],
         gv["xa_k_gain"][l], gm["xa_w_o"][l]) = xa_bwd(
            dx, s3, mem, row(w["xa_norm"], l), row(w["xa_mem_norm"], l), full["xa_w_q"][l], full["xa_w_kv"][l],
            row(w["xa_q_gain"], l), row(w["xa_k_gain"], l), full["xa_w_o"][l], t + "_xa")
        if l % 2 == 0:
            (dx, gv["mix_norm"][l], d_win, gv["ev_q_gain"][j], gv["ev_k_gain"][j], gv["ev_sinks"][j],
             gm["ev_w_out"][j]) = even_mixer_bwd(
                dx, s2, row(w["mix_norm"], l), ev_in[j], row(w["ev_q_gain"], j), row(w["ev_k_gain"], j),
                row(w["ev_sinks"], j), full["ev_w_out"][j], t + "_ev")
            gm["ev_w_in"][j] = _ev_restore(d_win)
        else:
            (dx, gv["mix_norm"][l], gm["od_w_in"][j], gv["od_q_gain"][j], gv["od_k_gain"][j],
             gm["od_w_out"][j]) = odd_mixer_bwd(
                dx, s2, row(w["mix_norm"], l), full["od_w_in"][j], row(w["od_q_gain"], j), row(w["od_k_gain"], j),
                full["od_w_out"][j], t + "_od")
        dx, gv["ffn1_norm"][l], gm["ffn1_w_gu"][l], gm["ffn1_w_down"][l] = ffn_bwd(
            dx, s1, row(w["ffn1_norm"], l), full["ffn1_w_gu"][l], full["ffn1_w_down"][l], t + "_ffn1")
    mats = {n: jnp.stack(v) for n, v in gm.items()}
    vecs = {n: jnp.concatenate(v, axis=0) for n, v in gv.items()}
    return loss, dx, mats, vecs


def kernel(x, mem, ffn1_norm, ffn1_w_gu, ffn1_w_down, mix_norm, ev_w_in, ev_q_gain, ev_k_gain, ev_sinks, ev_w_out, od_w_in, od_q_gain, od_k_gain, od_w_out, xa_norm, xa_mem_norm, xa_w_q, xa_w_kv, xa_q_gain, xa_k_gain, xa_w_o, ffn2_norm, ffn2_w_gu, ffn2_w_down, loss_target, m_ffn1_norm, m_ffn1_w_gu, m_ffn1_w_down, m_mix_norm, m_ev_w_in, m_ev_q_gain, m_ev_k_gain, m_ev_sinks, m_ev_w_out, m_od_w_in, m_od_q_gain, m_od_k_gain, m_od_w_out, m_xa_norm, m_xa_mem_norm, m_xa_w_q, m_xa_w_kv, m_xa_q_gain, m_xa_k_gain, m_xa_w_o, m_ffn2_norm, m_ffn2_w_gu, m_ffn2_w_down, v_ffn1_norm, v_ffn1_w_gu, v_ffn1_w_down, v_mix_norm, v_ev_w_in, v_ev_q_gain, v_ev_k_gain, v_ev_sinks, v_ev_w_out, v_od_w_in, v_od_q_gain, v_od_k_gain, v_od_w_out, v_xa_norm, v_xa_mem_norm, v_xa_w_q, v_xa_w_kv, v_xa_q_gain, v_xa_k_gain, v_xa_w_o, v_ffn2_norm, v_ffn2_w_gu, v_ffn2_w_down):
    w = dict(ffn1_norm=ffn1_norm, ffn1_w_gu=ffn1_w_gu, ffn1_w_down=ffn1_w_down, mix_norm=mix_norm, ev_w_in=ev_w_in, ev_q_gain=ev_q_gain, ev_k_gain=ev_k_gain, ev_sinks=ev_sinks, ev_w_out=ev_w_out, od_w_in=od_w_in, od_q_gain=od_q_gain, od_k_gain=od_k_gain, od_w_out=od_w_out, xa_norm=xa_norm, xa_mem_norm=xa_mem_norm, xa_w_q=xa_w_q, xa_w_kv=xa_w_kv, xa_q_gain=xa_q_gain, xa_k_gain=xa_k_gain, xa_w_o=xa_w_o, ffn2_norm=ffn2_norm, ffn2_w_gu=ffn2_w_gu, ffn2_w_down=ffn2_w_down)
    m = dict(ffn1_norm=m_ffn1_norm, ffn1_w_gu=m_ffn1_w_gu, ffn1_w_down=m_ffn1_w_down, mix_norm=m_mix_norm, ev_w_in=m_ev_w_in, ev_q_gain=m_ev_q_gain, ev_k_gain=m_ev_k_gain, ev_sinks=m_ev_sinks, ev_w_out=m_ev_w_out, od_w_in=m_od_w_in, od_q_gain=m_od_q_gain, od_k_gain=m_od_k_gain, od_w_out=m_od_w_out, xa_norm=m_xa_norm, xa_mem_norm=m_xa_mem_norm, xa_w_q=m_xa_w_q, xa_w_kv=m_xa_w_kv, xa_q_gain=m_xa_q_gain, xa_k_gain=m_xa_k_gain, xa_w_o=m_xa_w_o, ffn2_norm=m_ffn2_norm, ffn2_w_gu=m_ffn2_w_gu, ffn2_w_down=m_ffn2_w_down)
    v = dict(ffn1_norm=v_ffn1_norm, ffn1_w_gu=v_ffn1_w_gu, ffn1_w_down=v_ffn1_w_down, mix_norm=v_mix_norm, ev_w_in=v_ev_w_in, ev_q_gain=v_ev_q_gain, ev_k_gain=v_ev_k_gain, ev_sinks=v_ev_sinks, ev_w_out=v_ev_w_out, od_w_in=v_od_w_in, od_q_gain=v_od_q_gain, od_k_gain=v_od_k_gain, od_w_out=v_od_w_out, xa_norm=v_xa_norm, xa_mem_norm=v_xa_mem_norm, xa_w_q=v_xa_w_q, xa_w_kv=v_xa_w_kv, xa_q_gain=v_xa_q_gain, xa_k_gain=v_xa_k_gain, xa_w_o=v_xa_w_o, ffn2_norm=v_ffn2_norm, ffn2_w_gu=v_ffn2_w_gu, ffn2_w_down=v_ffn2_w_down)

    full = _gather_weights(w)
    loss, dx, mats, vecs = _local_step(x[0], mem[0], loss_target[0], w, full)
    c = lax.axis_index("c").astype(jnp.int32).reshape(1)
    grads = _reduce_gradients(mats, vecs, c)
    loss = lax.psum(loss, ("x", "y", "c"))

    delta, new_m, new_v = {}, {}, {}
    for n in _WEIGHTS:
        delta[n], new_m[n], new_v[n] = adamw(w[n], grads[n], m[n], v[n], "adamw_" + n)
    return (loss, dx[None], *[grads[n] for n in _WEIGHTS], *[delta[n] for n in _WEIGHTS],
            *[new_m[n] for n in _WEIGHTS], *[new_v[n] for n in _WEIGHTS])
```

```python
import functools

import numpy as np
import jax
import jax.numpy as jnp
from jax import lax
from jax.experimental import pallas as pl
from jax.experimental.pallas import tpu as pltpu

F32 = jnp.float32
BF16 = jnp.bfloat16
MESH = pl.DeviceIdType.MESH

HEAD_DIM = 64
BLOCK = 128
RMS_EPS = 1e-6
A_Q_HEADS, A_KV_HEADS = 8, 2
B_HEADS = 8
C_HEADS = 16
C_PATTERNS = ((128, 1), (512, 4), (2048, 16))
X_HEADS = 4
N_DEV = 8
LANES = 1024
VMEM_LIMIT_BYTES = 56 * 1024 * 1024
SB_SKIP_LOG = -110.0
NEG_BIG = -1e30

ADAM_LR, ADAM_B1, ADAM_B2, ADAM_EPS, ADAM_WD, ADAM_STEP = 0.001, 0.9, 0.999, 1e-08, 0.01, 10

NN = (((1,), (0,)), ((), ()))
NT = (((1,), (1,)), ((), ()))
TN = (((0,), (0,)), ((), ()))


def _pcall(body, **kw):
    return pl.pallas_call(body, **kw)


def _params(**kw):
    return pltpu.CompilerParams(vmem_limit_bytes=VMEM_LIMIT_BYTES, **kw)


def _tile(dim, cap, unit=128):
    if dim <= cap:
        return dim
    t = (cap // unit) * unit
    while t >= unit:
        if dim % t == 0:
            return t
        t -= unit
    raise ValueError(f"no tile for {dim} under {cap}")


def _dot(a, b, dims):
    return lax.dot_general(a.astype(BF16), b.astype(BF16), dims, preferred_element_type=F32)


@functools.partial(jax.custom_vjp, nondiff_argnums=(2,))
def _dot_vjp(a, b, nt):
    return _dot(a, b, NT if nt else NN)


def _dot_vjp_fwd(a, b, nt):
    return _dot(a, b, NT if nt else NN), (a.astype(BF16), b.astype(BF16))


def _dot_vjp_bwd(nt, res, g):
    a, b = res
    if nt:
        return _dot(g, b, NN), _dot(g, a, TN)
    return _dot(g, b, NT), _dot(a, g, TN)


_dot_vjp.defvjp(_dot_vjp_fwd, _dot_vjp_bwd)


def _plain_dot(a, b, nt):
    return _dot(a, b, NT if nt else NN)


def _split_dot(x, mat, terms=2):
    out, rem = None, x
    for t in range(terms):
        part = rem.astype(BF16)
        d = lax.dot_general(part, mat, NN, preferred_element_type=F32)
        out = d if out is None else out + d
        if t + 1 < terms:
            rem = rem - part.astype(F32)
    return out


@functools.partial(jax.custom_vjp, nondiff_argnums=(3,))
def _split_dot_vjp(x, mat, mat_t, terms):
    return _split_dot(x, mat, terms)


def _split_dot_vjp_fwd(x, mat, mat_t, terms):
    return _split_dot(x, mat, terms), mat_t


def _split_dot_vjp_bwd(terms, mat_t, g):
    return _split_dot(g, mat_t, terms), None, None


_split_dot_vjp.defvjp(_split_dot_vjp_fwd, _split_dot_vjp_bwd)


def _plain_split(x, mat, mat_t, terms):
    return _split_dot(x, mat, terms)


def _tri(after):
    j = lax.broadcasted_iota(jnp.int32, (BLOCK, BLOCK), 0)
    s = lax.broadcasted_iota(jnp.int32, (BLOCK, BLOCK), 1)
    return jnp.where(j > s if after else j < s, 1.0, 0.0).astype(BF16)


def _in(a, block, imap):
    return (a, block, imap)


def _out(shape, dtype, block, imap, acc=False):
    return (shape, dtype, block, imap, acc)


def tcall(fn, grid, ins, outs, name):
    nin = len(ins)
    ngrid = len(grid)

    def body(*refs):
        ids = tuple(pl.program_id(a) for a in range(ngrid))
        res = fn(ids, *[r[...] for r in refs[:nin]])
        first = ids[0] == 0
        for a in range(1, ngrid):
            first = jnp.logical_and(first, ids[a] == 0)
        for o_ref, r, spec in zip(refs[nin:], res, outs):
            if spec[4]:
                @pl.when(first)
                def _(o_ref=o_ref):
                    o_ref[...] = jnp.zeros(o_ref.shape, o_ref.dtype)
                o_ref[...] += r.astype(o_ref.dtype)
            else:
                o_ref[...] = r.astype(o_ref.dtype)

    return _pcall(
        body, name=name, grid=grid,
        in_specs=[pl.BlockSpec(b, m) for (_, b, m) in ins],
        out_specs=[pl.BlockSpec(b, m) for (_, _, b, m, _) in outs],
        out_shape=[jax.ShapeDtypeStruct(s, d) for (s, d, _, _, _) in outs],
        compiler_params=_params(),
    )(*[a for (a, _, _) in ins])


def _row(a, tm, width=None, cb=0):
    width = a.shape[1] if width is None else width
    return _in(a, (tm, width), lambda i, cb=cb: (i, cb))


def _full(a):
    zeros = (0,) * a.ndim
    return _in(a, a.shape, lambda *ids: zeros)


def _row_out(n, width, dtype, tm):
    return _out((n, width), dtype, (tm, width), lambda i: (i, 0))


def _acc_out(shape):
    zeros = (0,) * len(shape)
    return _out(shape, F32, shape, lambda *ids: zeros, acc=True)


def mm(a, b, mode, name, *, out_dtype=F32, scale=1.0, res=None):
    if mode == "nn":
        (m, k), (k2, n) = a.shape, b.shape
    elif mode == "nt":
        (m, k), (n, k2) = a.shape, b.shape
    else:
        (k, m), (k2, n) = a.shape, b.shape
    assert k == k2, (a.shape, b.shape, mode)
    tm, tn, tk = _tile(m, 512), _tile(n, 1408), _tile(k, 1408)
    nk = k // tk
    dims = {"nn": NN, "nt": NT, "tn": TN}[mode]
    has_res = res is not None

    def body(*refs):
        if has_res:
            a_ref, b_ref, r_ref, o_ref, acc_ref = refs
        else:
            a_ref, b_ref, o_ref, acc_ref = refs
        kk = pl.program_id(2)

        @pl.when(kk == 0)
        def _():
            acc_ref[...] = jnp.zeros(acc_ref.shape, F32)

        acc_ref[...] += _dot(a_ref[...], b_ref[...], dims)

        @pl.when(kk == nk - 1)
        def _():
            out = acc_ref[...]
            if scale != 1.0:
                out = out * scale
            if has_res:
                out = out + r_ref[...]
            o_ref[...] = out.astype(o_ref.dtype)

    a_spec = (pl.BlockSpec((tk, tm), lambda i, j, kk: (kk, i)) if mode == "tn"
              else pl.BlockSpec((tm, tk), lambda i, j, kk: (i, kk)))
    b_spec = (pl.BlockSpec((tn, tk), lambda i, j, kk: (j, kk)) if mode == "nt"
              else pl.BlockSpec((tk, tn), lambda i, j, kk: (kk, j)))
    in_specs = [a_spec, b_spec]
    args = [a, b]
    if has_res:
        in_specs.append(pl.BlockSpec((tm, tn), lambda i, j, kk: (i, j)))
        args.append(res)
    return _pcall(
        body, name=name, grid=(m // tm, n // tn, nk),
        in_specs=in_specs,
        out_specs=pl.BlockSpec((tm, tn), lambda i, j, kk: (i, j)),
        out_shape=jax.ShapeDtypeStruct((m, n), out_dtype),
        scratch_shapes=[pltpu.VMEM((tm, tn), F32)],
        compiler_params=_params(dimension_semantics=("parallel", "parallel", "arbitrary")),
    )(*args)


def _rms(x, g):
    return x * lax.rsqrt(jnp.mean(x * x, axis=-1, keepdims=True) + RMS_EPS) * g


def _silu_mul(gate, up):
    return gate / (1.0 + jnp.exp(-gate)) * up


def _indicator(shape, head_axis, mod):
    lane = lax.broadcasted_iota(jnp.int32, shape, head_axis)
    other = lax.broadcasted_iota(jnp.int32, shape, 1 - head_axis)
    lane = jnp.bitwise_and(lane, HEAD_DIM - 1) if mod else jnp.right_shift(lane, 6)
    return jnp.where(lane == other, 1.0, 0.0).astype(BF16)


def _head_rms(split, xs, g):
    w = xs.shape[1]
    to_head, from_head = _indicator((w, BLOCK), 0, False), _indicator((BLOCK, w), 1, False)
    to_lane, from_lane = _indicator((HEAD_DIM, w), 1, True), _indicator((w, HEAD_DIM), 0, True)
    ss = split(xs * xs, to_head, from_head, 3)
    r = lax.rsqrt(ss * (1.0 / HEAD_DIM) + RMS_EPS)
    g_all = split(jnp.broadcast_to(g, (8, HEAD_DIM)), to_lane, from_lane, 3)[0:1]
    return xs * split(r, from_head, to_head, 3) * g_all


def _prep(split, x, qg, kg, segs):
    parts = []
    for start, width, kind in segs:
        xs = x[:, start:start + width]
        parts.append(xs if kind == "raw" else _head_rms(split, xs, qg if kind == "q" else kg))
    return jnp.concatenate(parts, axis=1)


def prep_fwd(x, qg, kg, segs, name):
    n, w = x.shape
    tm = _tile(n, 256, 8)
    (out,) = tcall(lambda ids, xt, a, b: (_prep(_plain_split, xt, a, b, segs),), (n // tm,),
                   [_row(x, tm), _full(qg), _full(kg)], [_row_out(n, w, BF16, tm)], name)
    return out


def prep_bwd(x, qg, kg, segs, grads, gather, name):
    n, w = x.shape
    tm = BLOCK
    nblk = n // tm

    def fn(ids, xt, a, b, *t):
        t = [jnp.where(ids[0] + sh < nblk, ti, 0.0) if sh else ti for ti, (_, sh) in zip(t, grads)]
        _, vjp = jax.vjp(lambda x_, a_, b_: _prep(_split_dot_vjp, x_, a_, b_, segs), xt, a, b)
        return vjp(gather(*t))

    specs = [_in(a, (tm, a.shape[1]), (lambda i, sh=sh: (jnp.minimum(i + sh, nblk - 1), 0))) for a, sh in grads]
    return tcall(fn, (nblk,), [_row(x, tm), _full(qg), _full(kg)] + specs,
                 [_row_out(n, w, BF16, tm), _acc_out(qg.shape), _acc_out(kg.shape)], name)


def rmsnorm_fwd(x, g, name):
    n, d = x.shape
    tm = _tile(n, 512, 8)
    (h,) = tcall(lambda ids, xt, gt: (_rms(xt, gt),), (n // tm,), [_row(x, tm), _full(g)],
                 [_row_out(n, d, BF16, tm)], name)
    return h


def rmsnorm_bwd(x, g, dh, dres, name):
    n, d = x.shape
    tm = _tile(n, 256, 8)

    def fn(ids, xt, gt, dht, *rest):
        _, vjp = jax.vjp(_rms, xt, gt)
        dx, dg = vjp(dht.astype(F32))
        if rest:
            dx = dx + rest[0]
        return dx, dg

    ins = [_row(x, tm), _full(g), _row(dh, tm)] + ([_row(dres, tm)] if dres is not None else [])
    return tcall(fn, (n // tm,), ins, [_row_out(n, d, F32, tm), _acc_out(g.shape)], name)


def ffn_fwd(x, g, w_gu, w_down, tag):
    n = x.shape[0]
    f = w_down.shape[0]
    h = rmsnorm_fwd(x, g, tag + "_norm")
    gu = mm(h, w_gu, "nn", tag + "_gu")
    tm = _tile(n, 128, 8)
    (a,) = tcall(lambda ids, gt, ut: (_silu_mul(gt, ut),), (n // tm,),
                 [_row(gu, tm, f, 0), _row(gu, tm, f, 1)], [_row_out(n, f, BF16, tm)], tag + "_act")
    y = mm(a, w_down, "nn", tag + "_down", scale=0.5, res=x)
    return y, (x, h, gu, a)


def ffn_bwd(dy, saved, g, w_gu, w_down, tag):
    x, h, gu, a = saved
    n = x.shape[0]
    f = w_down.shape[0]
    da = mm(dy, w_down, "nt", tag + "_da", scale=0.5)
    d_wdown = mm(a, dy, "tn", tag + "_dwd", scale=0.5)
    tm = _tile(n, 128, 8)

    def act_bwd(ids, gt, ut, dat):
        _, vjp = jax.vjp(_silu_mul, gt, ut)
        dg, du = vjp(dat)
        return (jnp.concatenate([dg, du], axis=1),)

    (dgu,) = tcall(act_bwd, (n // tm,), [_row(gu, tm, f, 0), _row(gu, tm, f, 1), _row(da, tm)],
                   [_row_out(n, 2 * f, BF16, tm)], tag + "_dact")
    dh = mm(dgu, w_gu, "nt", tag + "_dh")
    d_wgu = mm(h, dgu, "tn", tag + "_dwgu")
    dx, dg = rmsnorm_bwd(x, g, dh, dy, tag + "_dnorm")
    return dx, dg, d_wgu, d_wdown


def _alibi(n_heads):
    return [float(s) for s in np.asarray(2.0 ** (-8.0 * np.arange(1, n_heads + 1) / n_heads), dtype=np.float32)]


def _banded_tile(dot, first, q, kp, kc, vp, vc, sinks, *, hkv, grp, max_dist, step, slopes, want_lse):
    row = lax.broadcasted_iota(jnp.int32, (BLOCK, 2 * BLOCK), 0)
    col = lax.broadcasted_iota(jnp.int32, (BLOCK, 2 * BLOCK), 1)
    dist = row + BLOCK - col
    valid = (dist >= 0) & (dist <= max_dist) & ((col >= BLOCK) | jnp.logical_not(first))
    distf = dist.astype(F32)

    def head(hd, qh, k2, v2):
        s = dot(qh, k2, True) * (HEAD_DIM ** -0.5)
        s = jnp.where(valid, s - (slopes[hd] * step) * distf, NEG_BIG)
        m = jnp.max(s, axis=-1, keepdims=True)
        if sinks is not None:
            pick = lax.broadcasted_iota(jnp.int32, sinks.shape, 1) == hd
            sk = jnp.sum(jnp.where(pick, sinks, 0.0), axis=1, keepdims=True)
            m = jnp.maximum(m, sk)
        m = lax.stop_gradient(m)
        p = jnp.exp(s - m)
        denom = jnp.sum(p, axis=-1, keepdims=True)
        if sinks is not None:
            denom = denom + jnp.exp(sk - m)
        return dot(p / denom, v2, False), m + jnp.log(denom)

    outs, lses = [], []
    if grp == 1:
        low = lax.broadcasted_iota(jnp.int32, (BLOCK, BLOCK), 1) < HEAD_DIM
        for pr in range(hkv // 2):
            sl = slice(pr * BLOCK, (pr + 1) * BLOCK)
            q2 = q[:, sl]
            k2 = jnp.concatenate([kp[:, sl], kc[:, sl]], axis=0)
            v2 = jnp.concatenate([vp[:, sl], vc[:, sl]], axis=0)
            o0, l0 = head(2 * pr, jnp.where(low, q2, 0.0), k2, v2)
            o1, l1 = head(2 * pr + 1, jnp.where(low, 0.0, q2), k2, v2)
            outs.append(jnp.where(low, o0, o1))
            lses.append(jnp.where(low, l0, l1))
    else:
        for hk in range(hkv):
            sl = slice(hk * HEAD_DIM, (hk + 1) * HEAD_DIM)
            k2 = jnp.concatenate([kp[:, sl], kc[:, sl]], axis=0)
            v2 = jnp.concatenate([vp[:, sl], vc[:, sl]], axis=0)
            for gi in range(grp):
                hd = hk * grp + gi
                o_h, l_h = head(hd, q[:, hd * HEAD_DIM:(hd + 1) * HEAD_DIM], k2, v2)
                outs.append(o_h)
                lses.append(jnp.broadcast_to(l_h, (BLOCK, HEAD_DIM)))
    o = jnp.concatenate(outs, axis=1)
    if want_lse:
        return o, jnp.concatenate(lses, axis=1)
    return (o,)


def _banded_specs(view, qcol, kcol, vcol, wq, wkv):
    def at(colfn, prev):
        if prev:
            return lambda r, n: (jnp.maximum(n - 1, 0), colfn(r))
        return lambda r, n: (n, colfn(r))
    return [
        _in(view, (BLOCK, wq), at(qcol, False)),
        _in(view, (BLOCK, wkv), at(kcol, True)),
        _in(view, (BLOCK, wkv), at(kcol, False)),
        _in(view, (BLOCK, wkv), at(vcol, True)),
        _in(view, (BLOCK, wkv), at(vcol, False)),
    ]


def banded_fwd(view, dil, cols, sinks, cfg, name):
    ns = view.shape[0]
    nb = ns // BLOCK
    wq, wkv = cfg["hkv"] * cfg["grp"] * HEAD_DIM, cfg["hkv"] * HEAD_DIM
    has_sinks = sinks is not None

    def fn(ids, q, kp, kc, vp, vc, *rest):
        q, kp, kc, vp, vc = [a.astype(F32) for a in (q, kp, kc, vp, vc)]
        return _banded_tile(_plain_dot, ids[1] == 0, q, kp, kc, vp, vc, rest[0] if has_sinks else None, **cfg)

    ins = _banded_specs(view, *cols, wq, wkv) + ([_full(sinks)] if has_sinks else [])
    outs = [_out((ns, dil * wq), BF16, (BLOCK, wq), lambda r, n: (n, r))]
    if cfg["want_lse"]:
        outs.append(_out((ns, dil * wq), F32, (BLOCK, wq), lambda r, n: (n, r)))
    return tcall(fn, (dil, nb), ins, outs, name)


def banded_bwd(view, dil, cols, sinks, cfg, cts, name):
    ns = view.shape[0]
    nb = ns // BLOCK
    wq, wkv = cfg["hkv"] * cfg["grp"] * HEAD_DIM, cfg["hkv"] * HEAD_DIM
    has_sinks = sinks is not None
    assert len(cts) == (2 if cfg["want_lse"] else 1)

    def fn(ids, q, kp, kc, vp, vc, *rest):
        sk = rest[0] if has_sinks else None
        ct = rest[1 if has_sinks else 0:]
        first = ids[1] == 0

        def f(q, kp, kc, vp, vc, *s):
            return _banded_tile(_dot_vjp, first, q, kp, kc, vp, vc, s[0] if has_sinks else None, **cfg)

        prim = tuple(a.astype(F32) for a in (q, kp, kc, vp, vc)) + ((sk,) if has_sinks else ())
        _, vjp = jax.vjp(f, *prim)
        return vjp(tuple(c.astype(F32) for c in ct))

    ins = (_banded_specs(view, *cols, wq, wkv) + ([_full(sinks)] if has_sinks else [])
           + [_in(a, (BLOCK, wq), (lambda r, n, cf=cf: (n, cf(r)))) for (a, cf) in cts])
    blk = lambda w: _out((ns, dil * w), F32, (BLOCK, w), lambda r, n: (n, r))
    outs = [blk(wq), blk(wkv), blk(wkv), blk(wkv), blk(wkv)]
    if has_sinks:
        outs.append(_acc_out(sinks.shape))
    return tcall(fn, (dil, nb), ins, outs, name)


def _log_sigmoid(z):
    return jnp.minimum(z, 0.0) - jnp.log(1.0 + jnp.exp(-jnp.abs(z)))


SB_PAIRS = 2


def _sb_pair(dot, suffix, qh, kb, vb, r_in, mask):
    z = dot(qh, kb, True) * (HEAD_DIM ** -0.5)
    lsp = _log_sigmoid(z)
    log_keep = jnp.where(mask, lsp - z, 0.0)
    log_after = suffix(log_keep) + r_in
    a = jnp.where(mask, jnp.exp(lsp + log_after), 0.0)
    return dot(a, vb, False), r_in + jnp.sum(log_keep, axis=1, keepdims=True)


def sb_fwd(qkv, qcb, kcb, vcb, name):
    s = qkv.shape[0]
    nb = s // BLOCK
    pairs = B_HEADS // 2
    wide = SB_PAIRS * BLOCK
    nh = 2 * SB_PAIRS
    assert pairs % SB_PAIRS == 0 and qcb % SB_PAIRS == 0 and kcb % SB_PAIRS == 0 and vcb % SB_PAIRS == 0

    def body(q_ref, k_ref, v_ref, o_ref):
        n = pl.program_id(1)
        low = lax.broadcasted_iota(jnp.int32, (BLOCK, BLOCK), 1) < HEAD_DIM
        before = (lax.broadcasted_iota(jnp.int32, (BLOCK, BLOCK), 1)
                  < lax.broadcasted_iota(jnp.int32, (BLOCK, BLOCK), 0))
        after = _tri(True)
        suffix = lambda t: _split_dot(t, after)
        qs = []
        for p in range(SB_PAIRS):
            q2 = q_ref[:, p * BLOCK:(p + 1) * BLOCK].astype(F32)
            qs += [jnp.where(low, q2, 0.0), jnp.where(low, 0.0, q2)]

        def cond(c):
            return jnp.logical_and(c[0] >= 0, c[1] > SB_SKIP_LOG)

        def step(c):
            kb, _, rs, accs = c
            rows = pl.ds(pl.multiple_of(kb * BLOCK, BLOCK), BLOCK)
            mask = jnp.logical_or(before, kb != n)
            new_r, new_acc, top = [], [], None
            for h in range(nh):
                cols = slice((h // 2) * BLOCK, (h // 2 + 1) * BLOCK)
                o_part, r_out = _sb_pair(_plain_dot, suffix, qs[h], k_ref[rows, cols], v_ref[rows, cols], rs[h], mask)
                new_r.append(r_out)
                new_acc.append(accs[h] + o_part)
                top = jnp.max(r_out) if top is None else jnp.maximum(top, jnp.max(r_out))
            return kb - 1, top, tuple(new_r), tuple(new_acc)

        init = (n, jnp.float32(0.0), tuple(jnp.zeros((BLOCK, 1), F32) for _ in range(nh)),
                tuple(jnp.zeros((BLOCK, BLOCK), F32) for _ in range(nh)))
        accs = lax.while_loop(cond, step, init)[3]
        for p in range(SB_PAIRS):
            o_ref[:, p * BLOCK:(p + 1) * BLOCK] = jnp.where(low, accs[2 * p], accs[2 * p + 1]).astype(o_ref.dtype)

    return _pcall(
        body, name=name, grid=(pairs // SB_PAIRS, nb),
        in_specs=[pl.BlockSpec((BLOCK, wide), lambda g, n: (n, qcb // SB_PAIRS + g)),
                  pl.BlockSpec((s, wide), lambda g, n: (0, kcb // SB_PAIRS + g)),
                  pl.BlockSpec((s, wide), lambda g, n: (0, vcb // SB_PAIRS + g))],
        out_specs=pl.BlockSpec((BLOCK, wide), lambda g, n: (n, g)),
        out_shape=jax.ShapeDtypeStruct((s, pairs * BLOCK), BF16),
        compiler_params=_params(),
    )(qkv, qkv, qkv)


def sb_bwd(qkv, qcb, kcb, vcb, do, docb, name):
    s = qkv.shape[0]
    nb = s // BLOCK
    pairs = B_HEADS // 2
    wide = SB_PAIRS * BLOCK
    nh = 2 * SB_PAIRS
    assert docb % SB_PAIRS == 0

    def body(q_ref, k_ref, v_ref, do_ref, dq_ref, dk_ref, dv_ref, r_ref):
        n = pl.program_id(1)

        @pl.when(n == 0)
        def _():
            dk_ref[...] = jnp.zeros(dk_ref.shape, F32)
            dv_ref[...] = jnp.zeros(dv_ref.shape, F32)

        low = lax.broadcasted_iota(jnp.int32, (BLOCK, BLOCK), 1) < HEAD_DIM
        before = (lax.broadcasted_iota(jnp.int32, (BLOCK, BLOCK), 1)
                  < lax.broadcasted_iota(jnp.int32, (BLOCK, BLOCK), 0))
        after, earlier = _tri(True), _tri(False)
        suffix = lambda t: _split_dot_vjp(t, after, earlier, 2)
        qs, dos = [], []
        for p in range(SB_PAIRS):
            q2 = q_ref[:, p * BLOCK:(p + 1) * BLOCK].astype(F32)
            do2 = do_ref[:, p * BLOCK:(p + 1) * BLOCK].astype(F32)
            qs += [jnp.where(low, q2, 0.0), jnp.where(low, 0.0, q2)]
            dos += [jnp.where(low, do2, 0.0), jnp.where(low, 0.0, do2)]

        def cond(c):
            return jnp.logical_and(c[0] >= 0, c[1] > SB_SKIP_LOG)

        def down(c):
            kb, _, rs = c
            rows = pl.ds(pl.multiple_of(kb * BLOCK, BLOCK), BLOCK)
            mask = jnp.logical_or(before, kb != n)
            new_r, top = [], None
            for h in range(nh):
                cols = slice((h // 2) * BLOCK, (h // 2 + 1) * BLOCK)
                r_ref[h, kb] = rs[h]
                z = _dot(qs[h], k_ref[rows, cols], NT) * (HEAD_DIM ** -0.5)
                log_keep = jnp.where(mask, _log_sigmoid(z) - z, 0.0)
                r_out = rs[h] + jnp.sum(log_keep, axis=1, keepdims=True)
                new_r.append(r_out)
                top = jnp.max(r_out) if top is None else jnp.maximum(top, jnp.max(r_out))
            return kb - 1, top, tuple(new_r)

        init = (n, jnp.float32(0.0), tuple(jnp.zeros((BLOCK, 1), F32) for _ in range(nh)))
        last = lax.while_loop(cond, down, init)[0] + 1

        def up(kb, c):
            dqs, g_rs = c
            rows = pl.ds(pl.multiple_of(kb * BLOCK, BLOCK), BLOCK)
            mask = jnp.logical_or(before, kb != n)
            new_dq, new_g = [], []
            for h in range(nh):
                cols = slice((h // 2) * BLOCK, (h // 2 + 1) * BLOCK)
                _, vjp = jax.vjp(lambda q_, k_, v_, r_: _sb_pair(_dot_vjp, suffix, q_, k_, v_, r_, mask),
                                 qs[h], k_ref[rows, cols].astype(F32), v_ref[rows, cols].astype(F32), r_ref[h, kb])
                dq_c, dk_c, dv_c, g_in = vjp((dos[h], g_rs[h]))
                dk_ref[rows, cols] += dk_c
                dv_ref[rows, cols] += dv_c
                new_dq.append(dqs[h] + dq_c)
                new_g.append(g_in)
            return tuple(new_dq), tuple(new_g)

        init = (tuple(jnp.zeros((BLOCK, BLOCK), F32) for _ in range(nh)),
                tuple(jnp.zeros((BLOCK, 1), F32) for _ in range(nh)))
        dqs = lax.fori_loop(last, n + 1, up, init)[0]
        for p in range(SB_PAIRS):
            dq_ref[:, p * BLOCK:(p + 1) * BLOCK] = jnp.where(low, dqs[2 * p], dqs[2 * p + 1])

    full = jax.ShapeDtypeStruct((s, pairs * BLOCK), F32)
    return _pcall(
        body, name=name, grid=(pairs // SB_PAIRS, nb),
        in_specs=[pl.BlockSpec((BLOCK, wide), lambda g, n: (n, qcb // SB_PAIRS + g)),
                  pl.BlockSpec((s, wide), lambda g, n: (0, kcb // SB_PAIRS + g)),
                  pl.BlockSpec((s, wide), lambda g, n: (0, vcb // SB_PAIRS + g)),
                  pl.BlockSpec((BLOCK, wide), lambda g, n: (n, docb // SB_PAIRS + g))],
        out_specs=[pl.BlockSpec((BLOCK, wide), lambda g, n: (n, g)),
                   pl.BlockSpec((s, wide), lambda g, n: (0, g)),
                   pl.BlockSpec((s, wide), lambda g, n: (0, g))],
        out_shape=[full, full, full],
        scratch_shapes=[pltpu.VMEM((nh, nb, BLOCK, 1), F32)],
        compiler_params=_params(),
    )(qkv, qkv, qkv, do)


def _xa_tile(dot, q, kv, qg, kg):
    hd = q.shape[1] // X_HEADS
    outs = []
    for h in range(X_HEADS):
        qh = _rms(q[:, h * hd:(h + 1) * hd], qg)
        kh = _rms(kv[:, h * hd:(h + 1) * hd], kg)
        vh = kv[:, (X_HEADS + h) * hd:(X_HEADS + h + 1) * hd]
        sc = dot(qh, kh, True) * (hd ** -0.5)
        m = lax.stop_gradient(jnp.max(sc, axis=-1, keepdims=True))
        p = jnp.exp(sc - m)
        outs.append(dot(p / jnp.sum(p, axis=-1, keepdims=True), vh, False))
    return jnp.concatenate(outs, axis=1)


def xa_core_fwd(q, kv, qg, kg, name):
    n, d = q.shape
    tm = _tile(n, 256, 8)
    (o,) = tcall(lambda ids, qt, kvt, qgt, kgt: (_xa_tile(_plain_dot, qt, kvt, qgt, kgt),), (n // tm,),
                 [_row(q, tm), _full(kv), _full(qg), _full(kg)], [_row_out(n, d, BF16, tm)], name)
    return o


def xa_core_bwd(q, kv, qg, kg, do, name):
    n, d = q.shape
    tm = _tile(n, 256, 8)

    def fn(ids, qt, kvt, qgt, kgt, dot_):
        _, vjp = jax.vjp(functools.partial(_xa_tile, _dot_vjp), qt, kvt, qgt, kgt)
        return vjp(dot_.astype(F32))

    return tcall(fn, (n // tm,), [_row(q, tm), _full(kv), _full(qg), _full(kg), _row(do, tm)],
                 [_row_out(n, d, BF16, tm), _acc_out(kv.shape), _acc_out(qg.shape), _acc_out(kg.shape)], name)


def _ev_reorder(a):
    return jnp.concatenate([a[..., 0:512], a[..., 768:2304], a[..., 512:768]], axis=-1)


def _ev_restore(a):
    return jnp.concatenate([a[..., 0:512], a[..., 2048:2304], a[..., 512:2048]], axis=-1)


_EV_SEGS = ((0, 512, "q"), (512, 1536, "raw"), (2048, 128, "k"), (2176, 128, "raw"))
_A_CFG = dict(hkv=A_KV_HEADS, grp=A_Q_HEADS // A_KV_HEADS, max_dist=BLOCK - 1, step=1.0, slopes=_alibi(A_Q_HEADS),
              want_lse=False)
_A_COLS = (lambda r: 0, lambda r: 16, lambda r: 17)


def even_mixer_fwd(x, g, w_in, qg, kg, sinks, w_out, tag):
    h = rmsnorm_fwd(x, g, tag + "_norm")
    qkv = mm(h, w_in, "nn", tag + "_in")
    ops = prep_fwd(qkv, qg, kg, _EV_SEGS, tag + "_prep")
    (o_a,) = banded_fwd(ops, 1, _A_COLS, sinks, _A_CFG, tag + "_swa")
    o_b = sb_fwd(ops, 4, 8, 12, tag + "_sb")
    o = jnp.concatenate([o_a, o_b], axis=1)
    y = mm(o, w_out, "nn", tag + "_out", res=x)
    return y, (x, h, qkv, ops, o)


def even_mixer_bwd(dy, saved, g, w_in, qg, kg, sinks, w_out, tag):
    x, h, qkv, ops, o = saved
    do = mm(dy, w_out, "nt", tag + "_do")
    d_wout = mm(o, dy, "tn", tag + "_dwout")
    dqa, dkp, dkc, dvp, dvc, dsinks = banded_bwd(ops, 1, _A_COLS, sinks, _A_CFG, [(do, lambda r: 0)], tag + "_dswa")
    dqb, dkb, dvb = sb_bwd(ops, 4, 8, 12, do, 4, tag + "_dsb")
    dqkv, dqg, dkg = prep_bwd(
        qkv, qg, kg, _EV_SEGS, [(dqa, 0), (dqb, 0), (dkb, 0), (dvb, 0), (dkc, 0), (dkp, 1), (dvc, 0), (dvp, 1)],
        lambda qa, qb, kb, vb, kc, kp, vc, vp: jnp.concatenate([qa, qb, kb, vb, kc + kp, vc + vp], axis=1),
        tag + "_dqkv")
    dh = mm(dqkv, w_in, "nt", tag + "_dh")
    d_win = mm(h, dqkv, "tn", tag + "_dwin")
    dx, dg = rmsnorm_bwd(x, g, dh, dy, tag + "_dnorm")
    return dx, dg, d_win, dqg, dkg, dsinks, d_wout


def _c_cfg(window, dil):
    return dict(hkv=C_HEADS, grp=1, max_dist=window // dil, step=float(dil), slopes=_alibi(C_HEADS), want_lse=True)


_C_COLS = (lambda r: 3 * r, lambda r: 3 * r + 1, lambda r: 3 * r + 2)
_OD_SEGS = ((0, 1024, "q"), (1024, 1024, "k"), (2048, 1024, "raw"))


def _combine(o1, o2, o3, l1, l2, l3):
    m = lax.stop_gradient(jnp.maximum(jnp.maximum(l1, l2), l3))
    e1, e2, e3 = jnp.exp(l1 - m), jnp.exp(l2 - m), jnp.exp(l3 - m)
    tot = e1 + e2 + e3
    return (e1 / tot) * o1 + (e2 / tot) * o2 + (e3 / tot) * o3


def odd_mixer_fwd(x, g, w_in, qg, kg, w_out, tag):
    n, d = x.shape
    h = rmsnorm_fwd(x, g, tag + "_norm")
    qkv = mm(h, w_in, "nn", tag + "_in")
    w3 = qkv.shape[1]
    ops = prep_fwd(qkv, qg, kg, _OD_SEGS, tag + "_prep")
    os_, ls_ = [], []
    for window, dil in C_PATTERNS:
        o_p, l_p = banded_fwd(ops.reshape(n // dil, dil * w3), dil, _C_COLS, None, _c_cfg(window, dil),
                              f"{tag}_dil{dil}")
        os_.append(o_p.reshape(n, d))
        ls_.append(l_p.reshape(n, d))
    tm = _tile(n, 128, 8)
    (o,) = tcall(lambda ids, *t: (_combine(*[a.astype(F32) for a in t]),), (n // tm,),
                 [_row(a, tm) for a in os_ + ls_], [_row_out(n, d, BF16, tm)], tag + "_comb")
    y = mm(o, w_out, "nn", tag + "_out", res=x)
    return y, (x, h, qkv, ops, os_, ls_, o)


def odd_mixer_bwd(dy, saved, g, w_in, qg, kg, w_out, tag):
    x, h, qkv, ops, os_, ls_, o = saved
    n, d = x.shape
    w3 = qkv.shape[1]
    do = mm(dy, w_out, "nt", tag + "_do")
    d_wout = mm(o, dy, "tn", tag + "_dwout")
    tm = _tile(n, 128, 8)

    def comb_bwd(ids, *t):
        _, vjp = jax.vjp(_combine, *[a.astype(F32) for a in t[:6]])
        return vjp(t[6])

    cts = tcall(comb_bwd, (n // tm,), [_row(a, tm) for a in os_ + ls_ + [do]],
                [_row_out(n, d, F32, tm) for _ in range(6)], tag + "_dcomb")
    dqs, dks, dvs = [], [], []
    for p, (window, dil) in enumerate(C_PATTERNS):
        lay = (n // dil, dil * d)
        dq, dkp, dkc, dvp, dvc = banded_bwd(
            ops.reshape(n // dil, dil * w3), dil, _C_COLS, None, _c_cfg(window, dil),
            [(cts[p].reshape(lay), lambda r: r), (cts[3 + p].reshape(lay), lambda r: r)], f"{tag}_ddil{dil}")
        dqs.append((dq.reshape(n, d), 0))
        dks += [(dkc.reshape(n, d), 0), (dkp.reshape(n, d), dil)]
        dvs += [(dvc.reshape(n, d), 0), (dvp.reshape(n, d), dil)]

    def gather(*t):
        total = lambda parts: functools.reduce(lambda a, b: a + b, parts)
        return jnp.concatenate([total(t[0:3]), total(t[3:9]), total(t[9:15])], axis=1)

    dqkv, dqg, dkg = prep_bwd(qkv, qg, kg, _OD_SEGS, dqs + dks + dvs, gather, tag + "_dqkv")
    dh = mm(dqkv, w_in, "nt", tag + "_dh")
    d_win = mm(h, dqkv, "tn", tag + "_dwin")
    dx, dg = rmsnorm_bwd(x, g, dh, dy, tag + "_dnorm")
    return dx, dg, d_win, dqg, dkg, d_wout


def xa_fwd(x, mem, g, gm, w_q, w_kv, qg, kg, w_o, tag):
    h = rmsnorm_fwd(x, g, tag + "_norm")
    q = mm(h, w_q, "nn", tag + "_q")
    mn = rmsnorm_fwd(mem, gm, tag + "_mnorm")
    kv = mm(mn, w_kv, "nn", tag + "_kv")
    o = xa_core_fwd(q, kv, qg, kg, tag + "_core")
    y = mm(o, w_o, "nn", tag + "_o", res=x)
    return y, (x, h, q, mn, kv, o)


def xa_bwd(dy, saved, mem, g, gm, w_q, w_kv, qg, kg, w_o, tag):
    x, h, q, mn, kv, o = saved
    do = mm(dy, w_o, "nt", tag + "_do", out_dtype=BF16)
    d_wo = mm(o, dy, "tn", tag + "_dwo")
    dq, dkv, dqg, dkg = xa_core_bwd(q, kv, qg, kg, do, tag + "_dcore")
    dh = mm(dq, w_q, "nt", tag + "_dh")
    d_wq = mm(h, dq, "tn", tag + "_dwq")
    dx, dg = rmsnorm_bwd(x, g, dh, dy, tag + "_dnorm")
    dmn = mm(dkv, w_kv, "nt", tag + "_dmn")
    d_wkv = mm(mn, dkv, "tn", tag + "_dwkv")
    _, dgm = rmsnorm_bwd(mem, gm, dmn, None, tag + "_dmnorm")
    return dx, dg, dgm, d_wq, d_wkv, dqg, dkg, d_wo


def loss_head(y, target, name):
    n, d = y.shape
    tm = _tile(n, 512, 8)

    def fn(ids, yt, tt):
        e = yt - tt
        return e * (1.0 / d), jnp.sum(e * e, axis=0, keepdims=True)

    return tcall(fn, (n // tm,), [_row(y, tm), _row(target, tm)], [_row_out(n, d, F32, tm), _acc_out((1, d))], name)


_ANY = pl.BlockSpec(memory_space=pl.ANY)


def all_gather_blocks(blocks):
    nb = len(blocks)

    def body(*refs):
        x_refs, out_refs = refs[:nb], refs[nb:2 * nb]
        send_sems, recv_sems, local_sems = refs[2 * nb:]
        x, y, c = lax.axis_index("x"), lax.axis_index("y"), lax.axis_index("c")
        me, sibling = (x, y, c), (x, y, 1 - c)
        chips = [(1 - x, y), (x, 1 - y), (1 - x, 1 - y)]

        def copy(b, k, blk, to, own=False):
            px, py, pc = blk
            slot = out_refs[b].at[4 * px + 2 * py + pc]
            return pltpu.make_async_remote_copy(
                src_ref=x_refs[b] if own else slot, dst_ref=slot,
                send_sem=send_sems.at[7 * b + k], recv_sem=recv_sems.at[7 * b + k], device_id=to, device_id_type=MESH)

        mine = [pltpu.make_async_copy(x_refs[b], out_refs[b].at[4 * x + 2 * y + c], local_sems.at[b]) for b in range(nb)]
        for cp in mine:
            cp.start()
        first = []
        for b in range(nb):
            first.append(copy(b, 0, me, sibling, own=True))
            first += [copy(b, 1 + j, me, (*chip, c), own=True) for j, chip in enumerate(chips)]
        for cp in first:
            cp.start()
        passed = []
        for j, chip in enumerate(chips):
            for b in range(nb):
                copy(b, 1 + j, (*chip, c), me).wait_recv()
                fwd = copy(b, 4 + j, (*chip, c), sibling)
                fwd.start()
                passed.append(fwd)
        for b in range(nb):
            copy(b, 0, sibling, me).wait_recv()
            for j, chip in enumerate(chips):
                copy(b, 4 + j, (*chip, 1 - c), me).wait_recv()
        for cp in first + passed:
            cp.wait_send()
        for cp in mine:
            cp.wait()

    return _pcall(
        body, name="weights_all_gather",
        in_specs=[_ANY] * nb, out_specs=[_ANY] * nb,
        out_shape=[jax.ShapeDtypeStruct((N_DEV,) + a.shape, a.dtype) for a in blocks],
        scratch_shapes=[pltpu.SemaphoreType.DMA((7 * nb,)), pltpu.SemaphoreType.DMA((7 * nb,)),
                        pltpu.SemaphoreType.DMA((nb,))],
    )(*blocks)


def pair_exchange(bufs):
    nb = len(bufs)

    def body(*refs):
        srcs, dsts = refs[:nb], refs[nb:2 * nb]
        send_sems, recv_sems = refs[2 * nb:]
        x, y, c = lax.axis_index("x"), lax.axis_index("y"), lax.axis_index("c")
        copies = []
        for b in range(nb):
            for j in range(4):
                cp = pltpu.make_async_remote_copy(
                    src_ref=srcs[b].at[2 * j + (1 - c)], dst_ref=dsts[b].at[j], send_sem=send_sems.at[4 * b + j],
                    recv_sem=recv_sems.at[4 * b + j], device_id=(x, y, 1 - c), device_id_type=MESH)
                cp.start()
                copies.append(cp)
        for cp in copies:
            cp.wait()

    return _pcall(
        body, name="grads_pair_exchange",
        in_specs=[_ANY] * nb, out_specs=[_ANY] * nb,
        out_shape=[jax.ShapeDtypeStruct((4,) + a.shape[1:], a.dtype) for a in bufs],
        scratch_shapes=[pltpu.SemaphoreType.DMA((4 * nb,)), pltpu.SemaphoreType.DMA((4 * nb,))],
    )(*bufs)


def pair_sum(g, got, c, out_dtype, name):
    r, w = g.shape[1:]
    tr = _tile(r, 512, 16)

    def body(c_ref, a_ref, b_ref, o_ref):
        o_ref[...] = (a_ref[...] + b_ref[...]).astype(o_ref.dtype)

    return _pcall(
        body, name=name,
        grid_spec=pltpu.PrefetchScalarGridSpec(
            num_scalar_prefetch=1, grid=(4, r // tr),
            in_specs=[pl.BlockSpec((None, tr, w), lambda j, i, c_ref: (2 * j + c_ref[0], i, 0)),
                      pl.BlockSpec((None, tr, w), lambda j, i, c_ref: (j, i, 0))],
            out_specs=pl.BlockSpec((None, tr, w), lambda j, i, c_ref: (j, i, 0))),
        out_shape=jax.ShapeDtypeStruct((4,) + g.shape[1:], out_dtype),
        compiler_params=_params(),
    )(c, g, got)


def chip_exchange(parts):
    nb = len(parts)

    def body(*refs):
        srcs, dsts = refs[:nb], refs[nb:2 * nb]
        send_sems, recv_sems, local_sems = refs[2 * nb:]
        x, y, c = lax.axis_index("x"), lax.axis_index("y"), lax.axis_index("c")
        my_chip = 2 * x + y
        copies = []
        for b in range(nb):
            mine = pltpu.make_async_copy(srcs[b].at[my_chip], dsts[b].at[my_chip], local_sems.at[b])
            mine.start()
            copies.append(mine)
            for k, (tx, ty) in enumerate([(1 - x, y), (x, 1 - y), (1 - x, 1 - y)]):
                cp = pltpu.make_async_remote_copy(
                    src_ref=srcs[b].at[2 * tx + ty], dst_ref=dsts[b].at[my_chip], send_sem=send_sems.at[3 * b + k],
                    recv_sem=recv_sems.at[3 * b + k], device_id=(tx, ty, c), device_id_type=MESH)
                cp.start()
                copies.append(cp)
        for cp in copies:
            cp.wait()

    return _pcall(
        body, name="grads_chip_exchange",
        in_specs=[_ANY] * nb, out_specs=[_ANY] * nb,
        out_shape=[jax.ShapeDtypeStruct(a.shape, a.dtype) for a in parts],
        scratch_shapes=[pltpu.SemaphoreType.DMA((3 * nb,)), pltpu.SemaphoreType.DMA((3 * nb,)),
                        pltpu.SemaphoreType.DMA((nb,))],
    )(*parts)


def chip_sum(parts, name):
    r, w = parts.shape[1:]
    tr = _tile(r, 512, 16)
    spec = lambda j: _in(parts, (None, tr, w), lambda i, j=j: (j, i, 0))

    def fn(ids, a, b, c_, d):
        a, b, c_, d = [t.astype(F32) for t in (a, b, c_, d)]
        return (((a + b) + c_) + d,)

    (out,) = tcall(fn, (r // tr,), [spec(j) for j in range(4)],
                   [_out((r, w), F32, (tr, w), lambda i: (i, 0))], name)
    return out


def adamw(w, g, m, v, name):
    shape = w.shape
    cols = shape[-1]
    rows = int(np.prod(shape[:-1]))
    w2, g2, m2, v2 = [a.reshape(rows, cols) for a in (w, g, m, v)]
    tr = _tile(rows, 256, 8) if rows % 8 == 0 else rows

    def fn(ids, wt, gt, mt, vt):
        m_new = ADAM_B1 * mt + (1.0 - ADAM_B1) * gt
        v_new = ADAM_B2 * vt + (1.0 - ADAM_B2) * (gt * gt)
        m_hat = m_new / (1.0 - ADAM_B1 ** ADAM_STEP)
        v_hat = v_new / (1.0 - ADAM_B2 ** ADAM_STEP)
        delta = -ADAM_LR * (m_hat / (jnp.sqrt(v_hat) + ADAM_EPS) + ADAM_WD * wt)
        return delta, m_new, v_new

    res = tcall(fn, (rows // tr,), [_row(a, tr) for a in (w2, g2, m2, v2)],
                [_row_out(rows, cols, F32, tr) for _ in range(3)], name)
    return [a.reshape(shape) for a in res]


_MATS = [("ffn1_w_gu", "col"), ("ffn1_w_down", "row"), ("ev_w_in", "col"), ("ev_w_out", "row"),
         ("od_w_in", "col"), ("od_w_out", "row"), ("xa_w_q", "row"), ("xa_w_kv", "col"), ("xa_w_o", "row"),
         ("ffn2_w_gu", "col"), ("ffn2_w_down", "row")]
_VECS = ["ffn1_norm", "mix_norm", "ev_q_gain", "ev_k_gain", "ev_sinks", "od_q_gain", "od_k_gain", "xa_norm",
         "xa_mem_norm", "xa_q_gain", "xa_k_gain", "ffn2_norm"]
_WEIGHTS = ["ffn1_norm", "ffn1_w_gu", "ffn1_w_down", "mix_norm", "ev_w_in", "ev_q_gain", "ev_k_gain", "ev_sinks",
            "ev_w_out", "od_w_in", "od_q_gain", "od_k_gain", "od_w_out", "xa_norm", "xa_mem_norm", "xa_w_q", "xa_w_kv",
            "xa_q_gain", "xa_k_gain", "xa_w_o", "ffn2_norm", "ffn2_w_gu", "ffn2_w_down"]


_GROUPS = [["ffn1_w_gu", "ffn2_w_gu"], ["ev_w_in"], ["od_w_in"], ["xa_w_kv"],
           ["ffn1_w_down", "ffn2_w_down", "ev_w_out", "od_w_out", "xa_w_q", "xa_w_o"]]
_AXIS = dict(_MATS)


def _gather_weights(shards):
    blocks = []
    for names in _GROUPS:
        rows = [shards[n].reshape(-1, shards[n].shape[-1]).astype(BF16) for n in names]
        blocks.append(rows[0] if len(rows) == 1 else jnp.concatenate(rows, axis=0))
    gathered = all_gather_blocks(blocks)
    full = {}
    for names, got in zip(_GROUPS, gathered):
        off = 0
        for n in names:
            l, a, b = shards[n].shape
            seg = got[:, off:off + l * a, :].reshape(N_DEV, l, a, b)
            off += l * a
            if _AXIS[n] == "row":
                full[n] = seg.transpose(1, 0, 2, 3).reshape(l, N_DEV * a, b)
            else:
                full[n] = seg.transpose(1, 2, 0, 3).reshape(l, a, N_DEV * b)
    return full


def _reduce_gradients(mats, vecs, c):
    bufs = []
    for names in _GROUPS:
        rows = []
        for n in names:
            gr = mats[n]
            l, a, b = gr.shape
            if _AXIS[n] == "row":
                rows.append(gr.reshape(l, N_DEV, a // N_DEV, b).transpose(1, 0, 2, 3).reshape(N_DEV, -1, b))
            else:
                rows.append(gr.reshape(l, a, N_DEV, b // N_DEV).transpose(2, 0, 1, 3).reshape(N_DEV, l * a, b // N_DEV))
        bufs.append(rows[0] if len(rows) == 1 else jnp.concatenate(rows, axis=1))
    vec = jnp.concatenate([vecs[n].reshape(-1) for n in _VECS])
    vec = jnp.pad(vec, (0, -vec.shape[0] % (16 * LANES)))
    bufs.append(jnp.broadcast_to(vec.reshape(1, -1, LANES), (N_DEV, vec.shape[0] // LANES, LANES)))
    got = pair_exchange(bufs)
    nm = len(_GROUPS)
    parts = [pair_sum(b, g, c, BF16 if i < nm else F32, f"grads_pair_sum{i}") for i, (b, g) in enumerate(zip(bufs, got))]
    sums = [chip_sum(p, f"grads_chip_sum{i}") for i, p in enumerate(chip_exchange(parts))]
    out = {}
    for names, tot in zip(_GROUPS, sums[:nm]):
        off = 0
        for n in names:
            l, a, b = mats[n].shape
            shape = (l, a // N_DEV, b) if _AXIS[n] == "row" else (l, a, b // N_DEV)
            out[n] = tot[off:off + shape[0] * shape[1]].reshape(shape)
            off += shape[0] * shape[1]
    flat, off = sums[nm].reshape(-1), 0
    for n in _VECS:
        out[n] = flat[off:off + vecs[n].size].reshape(vecs[n].shape)
        off += vecs[n].size
    return out


def _local_step(x, mem, target, w, full):
    depth = w["ffn1_norm"].shape[0]
    ev_in = _ev_reorder(full["ev_w_in"])
    row = lambda a, l: a[l:l + 1]
    saved = []
    for l in range(depth):
        t = f"l{l}"
        j = l // 2
        x, s1 = ffn_fwd(x, row(w["ffn1_norm"], l), full["ffn1_w_gu"][l], full["ffn1_w_down"][l], t + "_ffn1")
        if l % 2 == 0:
            x, s2 = even_mixer_fwd(x, row(w["mix_norm"], l), ev_in[j], row(w["ev_q_gain"], j), row(w["ev_k_gain"], j),
                                   row(w["ev_sinks"], j), full["ev_w_out"][j], t + "_ev")
        else:
            x, s2 = odd_mixer_fwd(x, row(w["mix_norm"], l), full["od_w_in"][j], row(w["od_q_gain"], j),
                                  row(w["od_k_gain"], j), full["od_w_out"][j], t + "_od")
        x, s3 = xa_fwd(x, mem, row(w["xa_norm"], l), row(w["xa_mem_norm"], l), full["xa_w_q"][l], full["xa_w_kv"][l],
                       row(w["xa_q_gain"], l), row(w["xa_k_gain"], l), full["xa_w_o"][l], t + "_xa")
        x, s4 = ffn_fwd(x, row(w["ffn2_norm"], l), full["ffn2_w_gu"][l], full["ffn2_w_down"][l], t + "_ffn2")
        saved.append((s1, s2, s3, s4))
    dx, sq = loss_head(x, target, "loss_head")
    loss = 0.5 * jnp.sum(sq) / x.shape[1]

    gm = {n: [None] * full[n].shape[0] for n, _ in _MATS}
    gv = {n: [None] * w[n].shape[0] for n in _VECS}
    for l in reversed(range(depth)):
        t = f"l{l}"
        j = l // 2
        s1, s2, s3, s4 = saved[l]
        dx, gv["ffn2_norm"][l], gm["ffn2_w_gu"][l], gm["ffn2_w_down"][l] = ffn_bwd(
            dx, s4, row(w["ffn2_norm"], l), full["ffn2_w_gu"][l], full["ffn2_w_down"][l], t + "_ffn2")
        (dx, gv["xa_norm"][l], gv["xa_mem_norm"][l], gm["xa_w_q"][l], gm["xa_w_kv"][l], gv["xa_q_gain"][l],
         gv["xa_k_gain"][l], gm["xa_w_o"][l]) = xa_bwd(
            dx, s3, mem, row(w["xa_norm"], l), row(w["xa_mem_norm"], l), full["xa_w_q"][l], full["xa_w_kv"][l],
            row(w["xa_q_gain"], l), row(w["xa_k_gain"], l), full["xa_w_o"][l], t + "_xa")
        if l % 2 == 0:
            (dx, gv["mix_norm"][l], d_win, gv["ev_q_gain"][j], gv["ev_k_gain"][j], gv["ev_sinks"][j],
             gm["ev_w_out"][j]) = even_mixer_bwd(
                dx, s2, row(w["mix_norm"], l), ev_in[j], row(w["ev_q_gain"], j), row(w["ev_k_gain"], j),
                row(w["ev_sinks"], j), full["ev_w_out"][j], t + "_ev")
            gm["ev_w_in"][j] = _ev_restore(d_win)
        else:
            (dx, gv["mix_norm"][l], gm["od_w_in"][j], gv["od_q_gain"][j], gv["od_k_gain"][j],
             gm["od_w_out"][j]) = odd_mixer_bwd(
                dx, s2, row(w["mix_norm"], l), full["od_w_in"][j], row(w["od_q_gain"], j), row(w["od_k_gain"], j),
                full["od_w_out"][j], t + "_od")
        dx, gv["ffn1_norm"][l], gm["ffn1_w_gu"][l], gm["ffn1_w_down"][l] = ffn_bwd(
            dx, s1, row(w["ffn1_norm"], l), full["ffn1_w_gu"][l], full["ffn1_w_down"][l], t + "_ffn1")
    mats = {n: jnp.stack(v) for n, v in gm.items()}
    vecs = {n: jnp.concatenate(v, axis=0) for n, v in gv.items()}
    return loss, dx, mats, vecs


def kernel(x, mem, ffn1_norm, ffn1_w_gu, ffn1_w_down, mix_norm, ev_w_in, ev_q_gain, ev_k_gain, ev_sinks, ev_w_out, od_w_in, od_q_gain, od_k_gain, od_w_out, xa_norm, xa_mem_norm, xa_w_q, xa_w_kv, xa_q_gain, xa_k_gain, xa_w_o, ffn2_norm, ffn2_w_gu, ffn2_w_down, loss_target, m_ffn1_norm, m_ffn1_w_gu, m_ffn1_w_down, m_mix_norm, m_ev_w_in, m_ev_q_gain, m_ev_k_gain, m_ev_sinks, m_ev_w_out, m_od_w_in, m_od_q_gain, m_od_k_gain, m_od_w_out, m_xa_norm, m_xa_mem_norm, m_xa_w_q, m_xa_w_kv, m_xa_q_gain, m_xa_k_gain, m_xa_w_o, m_ffn2_norm, m_ffn2_w_gu, m_ffn2_w_down, v_ffn1_norm, v_ffn1_w_gu, v_ffn1_w_down, v_mix_norm, v_ev_w_in, v_ev_q_gain, v_ev_k_gain, v_ev_sinks, v_ev_w_out, v_od_w_in, v_od_q_gain, v_od_k_gain, v_od_w_out, v_xa_norm, v_xa_mem_norm, v_xa_w_q, v_xa_w_kv, v_xa_q_gain, v_xa_k_gain, v_xa_w_o, v_ffn2_norm, v_ffn2_w_gu, v_ffn2_w_down):
    w = dict(ffn1_norm=ffn1_norm, ffn1_w_gu=ffn1_w_gu, ffn1_w_down=ffn1_w_down, mix_norm=mix_norm, ev_w_in=ev_w_in, ev_q_gain=ev_q_gain, ev_k_gain=ev_k_gain, ev_sinks=ev_sinks, ev_w_out=ev_w_out, od_w_in=od_w_in, od_q_gain=od_q_gain, od_k_gain=od_k_gain, od_w_out=od_w_out, xa_norm=xa_norm, xa_mem_norm=xa_mem_norm, xa_w_q=xa_w_q, xa_w_kv=xa_w_kv, xa_q_gain=xa_q_gain, xa_k_gain=xa_k_gain, xa_w_o=xa_w_o, ffn2_norm=ffn2_norm, ffn2_w_gu=ffn2_w_gu, ffn2_w_down=ffn2_w_down)
    m = dict(ffn1_norm=m_ffn1_norm, ffn1_w_gu=m_ffn1_w_gu, ffn1_w_down=m_ffn1_w_down, mix_norm=m_mix_norm, ev_w_in=m_ev_w_in, ev_q_gain=m_ev_q_gain, ev_k_gain=m_ev_k_gain, ev_sinks=m_ev_sinks, ev_w_out=m_ev_w_out, od_w_in=m_od_w_in, od_q_gain=m_od_q_gain, od_k_gain=m_od_k_gain, od_w_out=m_od_w_out, xa_norm=m_xa_norm, xa_mem_norm=m_xa_mem_norm, xa_w_q=m_xa_w_q, xa_w_kv=m_xa_w_kv, xa_q_gain=m_xa_q_gain, xa_k_gain=m_xa_k_gain, xa_w_o=m_xa_w_o, ffn2_norm=m_ffn2_norm, ffn2_w_gu=m_ffn2_w_gu, ffn2_w_down=m_ffn2_w_down)
    v = dict(ffn1_norm=v_ffn1_norm, ffn1_w_gu=v_ffn1_w_gu, ffn1_w_down=v_ffn1_w_down, mix_norm=v_mix_norm, ev_w_in=v_ev_w_in, ev_q_gain=v_ev_q_gain, ev_k_gain=v_ev_k_gain, ev_sinks=v_ev_sinks, ev_w_out=v_ev_w_out, od_w_in=v_od_w_in, od_q_gain=v_od_q_gain, od_k_gain=v_od_k_gain, od_w_out=v_od_w_out, xa_norm=v_xa_norm, xa_mem_norm=v_xa_mem_norm, xa_w_q=v_xa_w_q, xa_w_kv=v_xa_w_kv, xa_q_gain=v_xa_q_gain, xa_k_gain=v_xa_k_gain, xa_w_o=v_xa_w_o, ffn2_norm=v_ffn2_norm, ffn2_w_gu=v_ffn2_w_gu, ffn2_w_down=v_ffn2_w_down)

    full = _gather_weights(w)
    loss, dx, mats, vecs = _local_step(x[0], mem[0], loss_target[0], w, full)
    c = lax.axis_index("c").astype(jnp.int32).reshape(1)
    grads = _reduce_gradients(mats, vecs, c)
    loss = lax.psum(loss, ("x", "y", "c"))

    delta, new_m, new_v = {}, {}, {}
    for n in _WEIGHTS:
        delta[n], new_m[n], new_v[n] = adamw(w[n], grads[n], m[n], v[n], "adamw_" + n)
    return (loss, dx[None], *[grads[n] for n in _WEIGHTS], *[delta[n] for n in _WEIGHTS],
            *[new_m[n] for n in _WEIGHTS], *[new_v[n] for n in _WEIGHTS])
```

```python
import functools

import numpy as np
import jax
import jax.numpy as jnp
from jax import lax
from jax.experimental import pallas as pl
from jax.experimental.pallas import tpu as pltpu

F32 = jnp.float32
BF16 = jnp.bfloat16
MESH = pl.DeviceIdType.MESH

HEAD_DIM = 64
BLOCK = 128
RMS_EPS = 1e-6
A_Q_HEADS, A_KV_HEADS = 8, 2
B_HEADS = 8
C_HEADS = 16
C_PATTERNS = ((128, 1), (512, 4), (2048, 16))
X_HEADS = 4
N_DEV = 8
LANES = 1024
VMEM_LIMIT_BYTES = 56 * 1024 * 1024
SB_SKIP_LOG = -110.0
NEG_BIG = -1e30

ADAM_LR, ADAM_B1, ADAM_B2, ADAM_EPS, ADAM_WD, ADAM_STEP = 0.001, 0.9, 0.999, 1e-08, 0.01, 10

NN = (((1,), (0,)), ((), ()))
NT = (((1,), (1,)), ((), ()))
TN = (((0,), (0,)), ((), ()))


def _pcall(body, **kw):
    return pl.pallas_call(body, **kw)


def _params(**kw):
    return pltpu.CompilerParams(vmem_limit_bytes=VMEM_LIMIT_BYTES, **kw)


def _tile(dim, cap, unit=128):
    if dim <= cap:
        return dim
    t = (cap // unit) * unit
    while t >= unit:
        if dim % t == 0:
            return t
        t -= unit
    raise ValueError(f"no tile for {dim} under {cap}")


def _dot(a, b, dims):
    return lax.dot_general(a.astype(BF16), b.astype(BF16), dims, preferred_element_type=F32)


@functools.partial(jax.custom_vjp, nondiff_argnums=(2,))
def _dot_vjp(a, b, nt):
    return _dot(a, b, NT if nt else NN)


def _dot_vjp_fwd(a, b, nt):
    return _dot(a, b, NT if nt else NN), (a.astype(BF16), b.astype(BF16))


def _dot_vjp_bwd(nt, res, g):
    a, b = res
    if nt:
        return _dot(g, b, NN), _dot(g, a, TN)
    return _dot(g, b, NT), _dot(a, g, TN)


_dot_vjp.defvjp(_dot_vjp_fwd, _dot_vjp_bwd)


def _plain_dot(a, b, nt):
    return _dot(a, b, NT if nt else NN)


def _split_dot(x, mat, terms=2):
    out, rem = None, x
    for t in range(terms):
        part = rem.astype(BF16)
        d = lax.dot_general(part, mat, NN, preferred_element_type=F32)
        out = d if out is None else out + d
        if t + 1 < terms:
            rem = rem - part.astype(F32)
    return out


@functools.partial(jax.custom_vjp, nondiff_argnums=(3,))
def _split_dot_vjp(x, mat, mat_t, terms):
    return _split_dot(x, mat, terms)


def _split_dot_vjp_fwd(x, mat, mat_t, terms):
    return _split_dot(x, mat, terms), mat_t


def _split_dot_vjp_bwd(terms, mat_t, g):
    return _split_dot(g, mat_t, terms), None, None


_split_dot_vjp.defvjp(_split_dot_vjp_fwd, _split_dot_vjp_bwd)


def _plain_split(x, mat, mat_t, terms):
    return _split_dot(x, mat, terms)


def _tri(after):
    j = lax.broadcasted_iota(jnp.int32, (BLOCK, BLOCK), 0)
    s = lax.broadcasted_iota(jnp.int32, (BLOCK, BLOCK), 1)
    return jnp.where(j > s if after else j < s, 1.0, 0.0).astype(BF16)


def _in(a, block, imap):
    return (a, block, imap)


def _out(shape, dtype, block, imap, acc=False):
    return (shape, dtype, block, imap, acc)


def tcall(fn, grid, ins, outs, name, scratch=None):
    nin = len(ins)
    nout = len(outs)
    ngrid = len(grid)

    def body(*refs):
        ids = tuple(pl.program_id(a) for a in range(ngrid))
        extra = {} if scratch is None else {"scratch": refs[nin + nout]}
        res = fn(ids, *[r[...] for r in refs[:nin]], **extra)
        first = ids[0] == 0
        for a in range(1, ngrid):
            first = jnp.logical_and(first, ids[a] == 0)
        for o_ref, r, spec in zip(refs[nin:nin + nout], res, outs):
            if spec[4]:
                @pl.when(first)
                def _(o_ref=o_ref):
                    o_ref[...] = jnp.zeros(o_ref.shape, o_ref.dtype)
                o_ref[...] += r.astype(o_ref.dtype)
            else:
                o_ref[...] = r.astype(o_ref.dtype)

    return _pcall(
        body, name=name, grid=grid,
        in_specs=[pl.BlockSpec(b, m) for (_, b, m) in ins],
        out_specs=[pl.BlockSpec(b, m) for (_, _, b, m, _) in outs],
        out_shape=[jax.ShapeDtypeStruct(s, d) for (s, d, _, _, _) in outs],
        scratch_shapes=[] if scratch is None else [pltpu.VMEM(*scratch)],
        compiler_params=_params(),
    )(*[a for (a, _, _) in ins])


def _to_strided(scr, nat, d):
    if d == 1:
        return nat
    t, w = nat.shape
    nc = w // BLOCK
    for c in range(nc):
        scr[c * t:(c + 1) * t, :] = nat[:, c * BLOCK:(c + 1) * BLOCK]
    return jnp.concatenate([scr[pl.ds(c * t + r, t // d, stride=d), :] for r in range(d) for c in range(nc)], axis=1)


def _to_natural(scr, st, d):
    if d == 1:
        return st.astype(F32)
    t, w = st.shape[0] * d, st.shape[1] // d
    nc = w // BLOCK
    st = st.astype(F32)
    for r in range(d):
        for c in range(nc):
            scr[pl.ds(c * t + r, t // d, stride=d), :] = st[:, r * w + c * BLOCK:r * w + (c + 1) * BLOCK]
    return jnp.concatenate([scr[c * t:(c + 1) * t, :] for c in range(nc)], axis=1)


def _row(a, tm, width=None, cb=0):
    width = a.shape[1] if width is None else width
    return _in(a, (tm, width), lambda i, cb=cb: (i, cb))


def _full(a):
    zeros = (0,) * a.ndim
    return _in(a, a.shape, lambda *ids: zeros)


def _row_out(n, width, dtype, tm):
    return _out((n, width), dtype, (tm, width), lambda i: (i, 0))


def _acc_out(shape):
    zeros = (0,) * len(shape)
    return _out(shape, F32, shape, lambda *ids: zeros, acc=True)


def mm(a, b, mode, name, *, out_dtype=F32, scale=1.0, res=None):
    if mode == "nn":
        (m, k), (k2, n) = a.shape, b.shape
    elif mode == "nt":
        (m, k), (n, k2) = a.shape, b.shape
    else:
        (k, m), (k2, n) = a.shape, b.shape
    assert k == k2, (a.shape, b.shape, mode)
    tm, tn, tk = _tile(m, 512), _tile(n, 1408), _tile(k, 1408)
    nk = k // tk
    dims = {"nn": NN, "nt": NT, "tn": TN}[mode]
    has_res = res is not None

    def body(*refs):
        if has_res:
            a_ref, b_ref, r_ref, o_ref, acc_ref = refs
        else:
            a_ref, b_ref, o_ref, acc_ref = refs
        kk = pl.program_id(2)

        @pl.when(kk == 0)
        def _():
            acc_ref[...] = jnp.zeros(acc_ref.shape, F32)

        acc_ref[...] += _dot(a_ref[...], b_ref[...], dims)

        @pl.when(kk == nk - 1)
        def _():
            out = acc_ref[...]
            if scale != 1.0:
                out = out * scale
            if has_res:
                out = out + r_ref[...]
            o_ref[...] = out.astype(o_ref.dtype)

    a_spec = (pl.BlockSpec((tk, tm), lambda i, j, kk: (kk, i)) if mode == "tn"
              else pl.BlockSpec((tm, tk), lambda i, j, kk: (i, kk)))
    b_spec = (pl.BlockSpec((tn, tk), lambda i, j, kk: (j, kk)) if mode == "nt"
              else pl.BlockSpec((tk, tn), lambda i, j, kk: (kk, j)))
    in_specs = [a_spec, b_spec]
    args = [a, b]
    if has_res:
        in_specs.append(pl.BlockSpec((tm, tn), lambda i, j, kk: (i, j)))
        args.append(res)
    return _pcall(
        body, name=name, grid=(m // tm, n // tn, nk),
        in_specs=in_specs,
        out_specs=pl.BlockSpec((tm, tn), lambda i, j, kk: (i, j)),
        out_shape=jax.ShapeDtypeStruct((m, n), out_dtype),
        scratch_shapes=[pltpu.VMEM((tm, tn), F32)],
        compiler_params=_params(dimension_semantics=("parallel", "parallel", "arbitrary")),
    )(*args)


def _rms(x, g):
    return x * lax.rsqrt(jnp.mean(x * x, axis=-1, keepdims=True) + RMS_EPS) * g


def _silu_mul(gate, up):
    return gate / (1.0 + jnp.exp(-gate)) * up


def _indicator(shape, head_axis, mod):
    lane = lax.broadcasted_iota(jnp.int32, shape, head_axis)
    other = lax.broadcasted_iota(jnp.int32, shape, 1 - head_axis)
    lane = jnp.bitwise_and(lane, HEAD_DIM - 1) if mod else jnp.right_shift(lane, 6)
    return jnp.where(lane == other, 1.0, 0.0).astype(BF16)


def _head_rms(split, xs, g):
    w = xs.shape[1]
    to_head, from_head = _indicator((w, BLOCK), 0, False), _indicator((BLOCK, w), 1, False)
    to_lane, from_lane = _indicator((HEAD_DIM, w), 1, True), _indicator((w, HEAD_DIM), 0, True)
    ss = split(xs * xs, to_head, from_head, 3)
    r = lax.rsqrt(ss * (1.0 / HEAD_DIM) + RMS_EPS)
    g_all = split(jnp.broadcast_to(g, (8, HEAD_DIM)), to_lane, from_lane, 3)[0:1]
    return xs * split(r, from_head, to_head, 3) * g_all


def _prep(split, x, qg, kg, segs):
    parts = []
    for start, width, kind in segs:
        xs = x[:, start:start + width]
        parts.append(xs if kind == "raw" else _head_rms(split, xs, qg if kind == "q" else kg))
    return jnp.concatenate(parts, axis=1)


def prep_fwd(x, qg, kg, segs, dils, name):
    n, w = x.shape
    tm = _tile(n, 256, 8)

    def fn(ids, xt, a, b, scratch):
        ops = _prep(_plain_split, xt, a, b, segs)
        return tuple(_to_strided(scratch, ops, d) for d in dils)

    return tcall(fn, (n // tm,), [_row(x, tm), _full(qg), _full(kg)],
                 [_out((n // d, d * w), BF16, (tm // d, d * w), lambda i: (i, 0)) for d in dils], name,
                 scratch=((w // BLOCK * tm, BLOCK), F32))


def prep_bwd(x, qg, kg, segs, grads, gather, name):
    n, w = x.shape
    tm = BLOCK
    nblk = n // tm

    def fn(ids, xt, a, b, *t, scratch):
        t = [_to_natural(scratch, ti, d) for ti, (_, _, d) in zip(t, grads)]
        t = [jnp.where(ids[0] + sh < nblk, ti, 0.0) if sh else ti for ti, (_, sh, _) in zip(t, grads)]
        _, vjp = jax.vjp(lambda x_, a_, b_: _prep(_split_dot_vjp, x_, a_, b_, segs), xt, a, b)
        return vjp(gather(*t))

    specs = [_in(a, (tm // d, a.shape[1]), (lambda i, sh=sh: (jnp.minimum(i + sh, nblk - 1), 0))) for a, sh, d in grads]
    wmax = max(a.shape[1] // d for a, _, d in grads)
    return tcall(fn, (nblk,), [_row(x, tm), _full(qg), _full(kg)] + specs,
                 [_row_out(n, w, BF16, tm), _acc_out(qg.shape), _acc_out(kg.shape)], name,
                 scratch=((wmax // BLOCK * tm, BLOCK), F32))


def rmsnorm_fwd(x, g, name):
    n, d = x.shape
    tm = _tile(n, 512, 8)
    (h,) = tcall(lambda ids, xt, gt: (_rms(xt, gt),), (n // tm,), [_row(x, tm), _full(g)],
                 [_row_out(n, d, BF16, tm)], name)
    return h


def rmsnorm_bwd(x, g, dh, dres, name):
    n, d = x.shape
    tm = _tile(n, 256, 8)

    def fn(ids, xt, gt, dht, *rest):
        _, vjp = jax.vjp(_rms, xt, gt)
        dx, dg = vjp(dht.astype(F32))
        if rest:
            dx = dx + rest[0]
        return dx, dg

    ins = [_row(x, tm), _full(g), _row(dh, tm)] + ([_row(dres, tm)] if dres is not None else [])
    return tcall(fn, (n // tm,), ins, [_row_out(n, d, F32, tm), _acc_out(g.shape)], name)


def ffn_fwd(x, g, w_gu, w_down, tag):
    n = x.shape[0]
    f = w_down.shape[0]
    h = rmsnorm_fwd(x, g, tag + "_norm")
    gu = mm(h, w_gu, "nn", tag + "_gu")
    tm = _tile(n, 128, 8)
    (a,) = tcall(lambda ids, gt, ut: (_silu_mul(gt, ut),), (n // tm,),
                 [_row(gu, tm, f, 0), _row(gu, tm, f, 1)], [_row_out(n, f, BF16, tm)], tag + "_act")
    y = mm(a, w_down, "nn", tag + "_down", scale=0.5, res=x)
    return y, (x, h, gu, a)


def ffn_bwd(dy, saved, g, w_gu, w_down, tag):
    x, h, gu, a = saved
    n = x.shape[0]
    f = w_down.shape[0]
    da = mm(dy, w_down, "nt", tag + "_da", scale=0.5)
    d_wdown = mm(a, dy, "tn", tag + "_dwd", scale=0.5)
    tm = _tile(n, 128, 8)

    def act_bwd(ids, gt, ut, dat):
        _, vjp = jax.vjp(_silu_mul, gt, ut)
        dg, du = vjp(dat)
        return (jnp.concatenate([dg, du], axis=1),)

    (dgu,) = tcall(act_bwd, (n // tm,), [_row(gu, tm, f, 0), _row(gu, tm, f, 1), _row(da, tm)],
                   [_row_out(n, 2 * f, BF16, tm)], tag + "_dact")
    dh = mm(dgu, w_gu, "nt", tag + "_dh")
    d_wgu = mm(h, dgu, "tn", tag + "_dwgu")
    dx, dg = rmsnorm_bwd(x, g, dh, dy, tag + "_dnorm")
    return dx, dg, d_wgu, d_wdown


def _alibi(n_heads):
    return [float(s) for s in np.asarray(2.0 ** (-8.0 * np.arange(1, n_heads + 1) / n_heads), dtype=np.float32)]


def _banded_tile(dot, first, q, kp, kc, vp, vc, sinks, *, hkv, grp, max_dist, step, slopes, want_lse):
    row = lax.broadcasted_iota(jnp.int32, (BLOCK, 2 * BLOCK), 0)
    col = lax.broadcasted_iota(jnp.int32, (BLOCK, 2 * BLOCK), 1)
    dist = row + BLOCK - col
    valid = (dist >= 0) & (dist <= max_dist) & ((col >= BLOCK) | jnp.logical_not(first))
    distf = dist.astype(F32)

    def head(hd, qh, k2, v2):
        s = dot(qh, k2, True) * (HEAD_DIM ** -0.5)
        s = jnp.where(valid, s - (slopes[hd] * step) * distf, NEG_BIG)
        m = jnp.max(s, axis=-1, keepdims=True)
        if sinks is not None:
            pick = lax.broadcasted_iota(jnp.int32, sinks.shape, 1) == hd
            sk = jnp.sum(jnp.where(pick, sinks, 0.0), axis=1, keepdims=True)
            m = jnp.maximum(m, sk)
        m = lax.stop_gradient(m)
        p = jnp.exp(s - m)
        denom = jnp.sum(p, axis=-1, keepdims=True)
        if sinks is not None:
            denom = denom + jnp.exp(sk - m)
        return dot(p / denom, v2, False), m + jnp.log(denom)

    outs, lses = [], []
    if grp == 1:
        low = lax.broadcasted_iota(jnp.int32, (BLOCK, BLOCK), 1) < HEAD_DIM
        for pr in range(hkv // 2):
            sl = slice(pr * BLOCK, (pr + 1) * BLOCK)
            q2 = q[:, sl]
            k2 = jnp.concatenate([kp[:, sl], kc[:, sl]], axis=0)
            v2 = jnp.concatenate([vp[:, sl], vc[:, sl]], axis=0)
            o0, l0 = head(2 * pr, jnp.where(low, q2, 0.0), k2, v2)
            o1, l1 = head(2 * pr + 1, jnp.where(low, 0.0, q2), k2, v2)
            outs.append(jnp.where(low, o0, o1))
            lses.append(jnp.where(low, l0, l1))
    else:
        for hk in range(hkv):
            sl = slice(hk * HEAD_DIM, (hk + 1) * HEAD_DIM)
            k2 = jnp.concatenate([kp[:, sl], kc[:, sl]], axis=0)
            v2 = jnp.concatenate([vp[:, sl], vc[:, sl]], axis=0)
            for gi in range(grp):
                hd = hk * grp + gi
                o_h, l_h = head(hd, q[:, hd * HEAD_DIM:(hd + 1) * HEAD_DIM], k2, v2)
                outs.append(o_h)
                lses.append(jnp.broadcast_to(l_h, (BLOCK, HEAD_DIM)))
    o = jnp.concatenate(outs, axis=1)
    if want_lse:
        return o, jnp.concatenate(lses, axis=1)
    return (o,)


def _banded_specs(view, qcol, kcol, vcol, wq, wkv):
    def at(colfn, prev):
        if prev:
            return lambda r, n: (jnp.maximum(n - 1, 0), colfn(r))
        return lambda r, n: (n, colfn(r))
    return [
        _in(view, (BLOCK, wq), at(qcol, False)),
        _in(view, (BLOCK, wkv), at(kcol, True)),
        _in(view, (BLOCK, wkv), at(kcol, False)),
        _in(view, (BLOCK, wkv), at(vcol, True)),
        _in(view, (BLOCK, wkv), at(vcol, False)),
    ]


def banded_fwd(view, dil, cols, sinks, cfg, name):
    ns = view.shape[0]
    nb = ns // BLOCK
    wq, wkv = cfg["hkv"] * cfg["grp"] * HEAD_DIM, cfg["hkv"] * HEAD_DIM
    has_sinks = sinks is not None

    def fn(ids, q, kp, kc, vp, vc, *rest):
        q, kp, kc, vp, vc = [a.astype(F32) for a in (q, kp, kc, vp, vc)]
        return _banded_tile(_plain_dot, ids[1] == 0, q, kp, kc, vp, vc, rest[0] if has_sinks else None, **cfg)

    ins = _banded_specs(view, *cols, wq, wkv) + ([_full(sinks)] if has_sinks else [])
    outs = [_out((ns, dil * wq), F32 if cfg["want_lse"] else BF16, (BLOCK, wq), lambda r, n: (n, r))]
    if cfg["want_lse"]:
        outs.append(_out((ns, dil * wq), F32, (BLOCK, wq), lambda r, n: (n, r)))
    return tcall(fn, (dil, nb), ins, outs, name)


def banded_bwd(view, dil, cols, sinks, cfg, cts, name):
    ns = view.shape[0]
    nb = ns // BLOCK
    wq, wkv = cfg["hkv"] * cfg["grp"] * HEAD_DIM, cfg["hkv"] * HEAD_DIM
    has_sinks = sinks is not None
    assert len(cts) == (2 if cfg["want_lse"] else 1)

    def fn(ids, q, kp, kc, vp, vc, *rest):
        sk = rest[0] if has_sinks else None
        ct = rest[1 if has_sinks else 0:]
        first = ids[1] == 0

        def f(q, kp, kc, vp, vc, *s):
            return _banded_tile(_dot_vjp, first, q, kp, kc, vp, vc, s[0] if has_sinks else None, **cfg)

        prim = tuple(a.astype(F32) for a in (q, kp, kc, vp, vc)) + ((sk,) if has_sinks else ())
        _, vjp = jax.vjp(f, *prim)
        return vjp(tuple(c.astype(F32) for c in ct))

    ins = (_banded_specs(view, *cols, wq, wkv) + ([_full(sinks)] if has_sinks else [])
           + [_in(a, (BLOCK, wq), (lambda r, n, cf=cf: (n, cf(r)))) for (a, cf) in cts])
    blk = lambda w: _out((ns, dil * w), F32, (BLOCK, w), lambda r, n: (n, r))
    outs = [blk(wq), blk(wkv), blk(wkv), blk(wkv), blk(wkv)]
    if has_sinks:
        outs.append(_acc_out(sinks.shape))
    return tcall(fn, (dil, nb), ins, outs, name)


def _log_sigmoid(z):
    return jnp.minimum(z, 0.0) - jnp.log(1.0 + jnp.exp(-jnp.abs(z)))


SB_PAIRS = 4


def _sb_pair(dot, suffix, qh, kb, vb, r_in, mask):
    z = dot(qh, kb, True) * (HEAD_DIM ** -0.5)
    lsp = _log_sigmoid(z)
    log_keep = jnp.where(mask, lsp - z, 0.0)
    log_after = suffix(log_keep) + r_in
    a = jnp.where(mask, jnp.exp(lsp + log_after), 0.0)
    return dot(a, vb, False), r_in + jnp.sum(log_keep, axis=1, keepdims=True)


def sb_fwd(qkv, qcb, kcb, vcb, name):
    s = qkv.shape[0]
    nb = s // BLOCK
    pairs = B_HEADS // 2
    wide = SB_PAIRS * BLOCK
    nh = 2 * SB_PAIRS
    assert pairs % SB_PAIRS == 0 and qcb % SB_PAIRS == 0 and kcb % SB_PAIRS == 0 and vcb % SB_PAIRS == 0

    def body(q_ref, k_ref, v_ref, o_ref):
        n = pl.program_id(1)
        low = lax.broadcasted_iota(jnp.int32, (BLOCK, BLOCK), 1) < HEAD_DIM
        before = (lax.broadcasted_iota(jnp.int32, (BLOCK, BLOCK), 1)
                  < lax.broadcasted_iota(jnp.int32, (BLOCK, BLOCK), 0))
        after = _tri(True)
        suffix = lambda t: _split_dot(t, after)
        qs = []
        for p in range(SB_PAIRS):
            q2 = q_ref[:, p * BLOCK:(p + 1) * BLOCK].astype(F32)
            qs += [jnp.where(low, q2, 0.0), jnp.where(low, 0.0, q2)]

        def cond(c):
            return jnp.logical_and(c[0] >= 0, c[1] > SB_SKIP_LOG)

        def step(c):
            kb, _, rs, accs = c
            rows = pl.ds(pl.multiple_of(kb * BLOCK, BLOCK), BLOCK)
            mask = jnp.logical_or(before, kb != n)
            new_r, new_acc, top = [], [], None
            for h in range(nh):
                cols = slice((h // 2) * BLOCK, (h // 2 + 1) * BLOCK)
                o_part, r_out = _sb_pair(_plain_dot, suffix, qs[h], k_ref[rows, cols], v_ref[rows, cols], rs[h], mask)
                new_r.append(r_out)
                new_acc.append(accs[h] + o_part)
                top = jnp.max(r_out) if top is None else jnp.maximum(top, jnp.max(r_out))
            return kb - 1, top, tuple(new_r), tuple(new_acc)

        init = (n, jnp.float32(0.0), tuple(jnp.zeros((BLOCK, 1), F32) for _ in range(nh)),
                tuple(jnp.zeros((BLOCK, BLOCK), F32) for _ in range(nh)))
        accs = lax.while_loop(cond, step, init)[3]
        for p in range(SB_PAIRS):
            o_ref[:, p * BLOCK:(p + 1) * BLOCK] = jnp.where(low, accs[2 * p], accs[2 * p + 1]).astype(o_ref.dtype)

    return _pcall(
        body, name=name, grid=(pairs // SB_PAIRS, nb),
        in_specs=[pl.BlockSpec((BLOCK, wide), lambda g, n: (n, qcb // SB_PAIRS + g)),
                  pl.BlockSpec((s, wide), lambda g, n: (0, kcb // SB_PAIRS + g), pipeline_mode=pl.Buffered(1)),
                  pl.BlockSpec((s, wide), lambda g, n: (0, vcb // SB_PAIRS + g), pipeline_mode=pl.Buffered(1))],
        out_specs=pl.BlockSpec((BLOCK, wide), lambda g, n: (n, g)),
        out_shape=jax.ShapeDtypeStruct((s, pairs * BLOCK), BF16),
        compiler_params=_params(),
    )(qkv, qkv, qkv)


def sb_bwd(qkv, qcb, kcb, vcb, do, docb, name):
    s = qkv.shape[0]
    nb = s // BLOCK
    pairs = B_HEADS // 2
    wide = SB_PAIRS * BLOCK
    nh = 2 * SB_PAIRS
    assert docb % SB_PAIRS == 0

    def body(q_ref, k_ref, v_ref, do_ref, dq_ref, dk_ref, dv_ref, r_ref):
        n = pl.program_id(1)

        @pl.when(n == 0)
        def _():
            dk_ref[...] = jnp.zeros(dk_ref.shape, F32)
            dv_ref[...] = jnp.zeros(dv_ref.shape, F32)

        low = lax.broadcasted_iota(jnp.int32, (BLOCK, BLOCK), 1) < HEAD_DIM
        before = (lax.broadcasted_iota(jnp.int32, (BLOCK, BLOCK), 1)
                  < lax.broadcasted_iota(jnp.int32, (BLOCK, BLOCK), 0))
        after, earlier = _tri(True), _tri(False)
        suffix = lambda t: _split_dot_vjp(t, after, earlier, 2)
        qs, dos = [], []
        for p in range(SB_PAIRS):
            q2 = q_ref[:, p * BLOCK:(p + 1) * BLOCK].astype(F32)
            do2 = do_ref[:, p * BLOCK:(p + 1) * BLOCK].astype(F32)
            qs += [jnp.where(low, q2, 0.0), jnp.where(low, 0.0, q2)]
            dos += [jnp.where(low, do2, 0.0), jnp.where(low, 0.0, do2)]

        def cond(c):
            return jnp.logical_and(c[0] >= 0, c[1] > SB_SKIP_LOG)

        def down(c):
            kb, _, rs = c
            rows = pl.ds(pl.multiple_of(kb * BLOCK, BLOCK), BLOCK)
            mask = jnp.logical_or(before, kb != n)
            new_r, top = [], None
            for h in range(nh):
                cols = slice((h // 2) * BLOCK, (h // 2 + 1) * BLOCK)
                r_ref[h, kb] = rs[h]
                z = _dot(qs[h], k_ref[rows, cols], NT) * (HEAD_DIM ** -0.5)
                log_keep = jnp.where(mask, _log_sigmoid(z) - z, 0.0)
                r_out = rs[h] + jnp.sum(log_keep, axis=1, keepdims=True)
                new_r.append(r_out)
                top = jnp.max(r_out) if top is None else jnp.maximum(top, jnp.max(r_out))
            return kb - 1, top, tuple(new_r)

        init = (n, jnp.float32(0.0), tuple(jnp.zeros((BLOCK, 1), F32) for _ in range(nh)))
        last = lax.while_loop(cond, down, init)[0] + 1

        def up(kb, c):
            dqs, g_rs = c
            rows = pl.ds(pl.multiple_of(kb * BLOCK, BLOCK), BLOCK)
            mask = jnp.logical_or(before, kb != n)
            new_dq, new_g = [], []
            for h in range(nh):
                cols = slice((h // 2) * BLOCK, (h // 2 + 1) * BLOCK)
                _, vjp = jax.vjp(lambda q_, k_, v_, r_: _sb_pair(_dot_vjp, suffix, q_, k_, v_, r_, mask),
                                 qs[h], k_ref[rows, cols].astype(F32), v_ref[rows, cols].astype(F32), r_ref[h, kb])
                dq_c, dk_c, dv_c, g_in = vjp((dos[h], g_rs[h]))
                dk_ref[rows, cols] += dk_c
                dv_ref[rows, cols] += dv_c
                new_dq.append(dqs[h] + dq_c)
                new_g.append(g_in)
            return tuple(new_dq), tuple(new_g)

        init = (tuple(jnp.zeros((BLOCK, BLOCK), F32) for _ in range(nh)),
                tuple(jnp.zeros((BLOCK, 1), F32) for _ in range(nh)))
        dqs = lax.fori_loop(last, n + 1, up, init)[0]
        for p in range(SB_PAIRS):
            dq_ref[:, p * BLOCK:(p + 1) * BLOCK] = jnp.where(low, dqs[2 * p], dqs[2 * p + 1])

    full = jax.ShapeDtypeStruct((s, pairs * BLOCK), F32)
    return _pcall(
        body, name=name, grid=(pairs // SB_PAIRS, nb),
        in_specs=[pl.BlockSpec((BLOCK, wide), lambda g, n: (n, qcb // SB_PAIRS + g)),
                  pl.BlockSpec((s, wide), lambda g, n: (0, kcb // SB_PAIRS + g), pipeline_mode=pl.Buffered(1)),
                  pl.BlockSpec((s, wide), lambda g, n: (0, vcb // SB_PAIRS + g), pipeline_mode=pl.Buffered(1)),
                  pl.BlockSpec((BLOCK, wide), lambda g, n: (n, docb // SB_PAIRS + g))],
        out_specs=[pl.BlockSpec((BLOCK, wide), lambda g, n: (n, g)),
                   pl.BlockSpec((s, wide), lambda g, n: (0, g), pipeline_mode=pl.Buffered(1)),
                   pl.BlockSpec((s, wide), lambda g, n: (0, g), pipeline_mode=pl.Buffered(1))],
        out_shape=[full, full, full],
        scratch_shapes=[pltpu.VMEM((nh, nb, BLOCK, 1), F32)],
        compiler_params=_params(),
    )(qkv, qkv, qkv, do)


def _xa_tile(dot, q, kv, qg, kg):
    hd = q.shape[1] // X_HEADS
    outs = []
    for h in range(X_HEADS):
        qh = _rms(q[:, h * hd:(h + 1) * hd], qg)
        kh = _rms(kv[:, h * hd:(h + 1) * hd], kg)
        vh = kv[:, (X_HEADS + h) * hd:(X_HEADS + h + 1) * hd]
        sc = dot(qh, kh, True) * (hd ** -0.5)
        m = lax.stop_gradient(jnp.max(sc, axis=-1, keepdims=True))
        p = jnp.exp(sc - m)
        outs.append(dot(p / jnp.sum(p, axis=-1, keepdims=True), vh, False))
    return jnp.concatenate(outs, axis=1)


def xa_core_fwd(q, kv, qg, kg, name):
    n, d = q.shape
    tm = _tile(n, 256, 8)
    (o,) = tcall(lambda ids, qt, kvt, qgt, kgt: (_xa_tile(_plain_dot, qt, kvt, qgt, kgt),), (n // tm,),
                 [_row(q, tm), _full(kv), _full(qg), _full(kg)], [_row_out(n, d, BF16, tm)], name)
    return o


def xa_core_bwd(q, kv, qg, kg, do, name):
    n, d = q.shape
    tm = _tile(n, 256, 8)

    def fn(ids, qt, kvt, qgt, kgt, dot_):
        _, vjp = jax.vjp(functools.partial(_xa_tile, _dot_vjp), qt, kvt, qgt, kgt)
        return vjp(dot_.astype(F32))

    return tcall(fn, (n // tm,), [_row(q, tm), _full(kv), _full(qg), _full(kg), _row(do, tm)],
                 [_row_out(n, d, BF16, tm), _acc_out(kv.shape), _acc_out(qg.shape), _acc_out(kg.shape)], name)


def _ev_reorder(a):
    return jnp.concatenate([a[..., 0:512], a[..., 768:2304], a[..., 512:768]], axis=-1)


def _ev_restore(a):
    return jnp.concatenate([a[..., 0:512], a[..., 2048:2304], a[..., 512:2048]], axis=-1)


_EV_SEGS = ((0, 512, "q"), (512, 1536, "raw"), (2048, 128, "k"), (2176, 128, "raw"))
_A_CFG = dict(hkv=A_KV_HEADS, grp=A_Q_HEADS // A_KV_HEADS, max_dist=BLOCK - 1, step=1.0, slopes=_alibi(A_Q_HEADS),
              want_lse=False)
_A_COLS = (lambda r: 0, lambda r: 16, lambda r: 17)


def even_mixer_fwd(x, g, w_in, qg, kg, sinks, w_out, tag):
    h = rmsnorm_fwd(x, g, tag + "_norm")
    qkv = mm(h, w_in, "nn", tag + "_in")
    (ops,) = prep_fwd(qkv, qg, kg, _EV_SEGS, (1,), tag + "_prep")
    (o_a,) = banded_fwd(ops, 1, _A_COLS, sinks, _A_CFG, tag + "_swa")
    o_b = sb_fwd(ops, 4, 8, 12, tag + "_sb")
    o = jnp.concatenate([o_a, o_b], axis=1)
    y = mm(o, w_out, "nn", tag + "_out", res=x)
    return y, (x, h, qkv, ops, o)


def even_mixer_bwd(dy, saved, g, w_in, qg, kg, sinks, w_out, tag):
    x, h, qkv, ops, o = saved
    do = mm(dy, w_out, "nt", tag + "_do")
    d_wout = mm(o, dy, "tn", tag + "_dwout")
    dqa, dkp, dkc, dvp, dvc, dsinks = banded_bwd(ops, 1, _A_COLS, sinks, _A_CFG, [(do, lambda r: 0)], tag + "_dswa")
    dqb, dkb, dvb = sb_bwd(ops, 4, 8, 12, do, 4, tag + "_dsb")
    dqkv, dqg, dkg = prep_bwd(
        qkv, qg, kg, _EV_SEGS,
        [(dqa, 0, 1), (dqb, 0, 1), (dkb, 0, 1), (dvb, 0, 1), (dkc, 0, 1), (dkp, 1, 1), (dvc, 0, 1), (dvp, 1, 1)],
        lambda qa, qb, kb, vb, kc, kp, vc, vp: jnp.concatenate([qa, qb, kb, vb, kc + kp, vc + vp], axis=1),
        tag + "_dqkv")
    dh = mm(dqkv, w_in, "nt", tag + "_dh")
    d_win = mm(h, dqkv, "tn", tag + "_dwin")
    dx, dg = rmsnorm_bwd(x, g, dh, dy, tag + "_dnorm")
    return dx, dg, d_win, dqg, dkg, dsinks, d_wout


def _c_cfg(window, dil):
    return dict(hkv=C_HEADS, grp=1, max_dist=window // dil, step=float(dil), slopes=_alibi(C_HEADS), want_lse=True)


_C_COLS = (lambda r: 3 * r, lambda r: 3 * r + 1, lambda r: 3 * r + 2)
_OD_SEGS = ((0, 1024, "q"), (1024, 1024, "k"), (2048, 1024, "raw"))


def _combine(o1, o2, o3, l1, l2, l3):
    m = lax.stop_gradient(jnp.maximum(jnp.maximum(l1, l2), l3))
    e1, e2, e3 = jnp.exp(l1 - m), jnp.exp(l2 - m), jnp.exp(l3 - m)
    tot = e1 + e2 + e3
    return (e1 / tot) * o1 + (e2 / tot) * o2 + (e3 / tot) * o3


def odd_mixer_fwd(x, g, w_in, qg, kg, w_out, tag):
    n, d = x.shape
    h = rmsnorm_fwd(x, g, tag + "_norm")
    qkv = mm(h, w_in, "nn", tag + "_in")
    dils = [dil for _, dil in C_PATTERNS]
    ops = prep_fwd(qkv, qg, kg, _OD_SEGS, dils, tag + "_prep")
    os_, ls_ = [], []
    for (window, dil), ops_d in zip(C_PATTERNS, ops):
        o_p, l_p = banded_fwd(ops_d, dil, _C_COLS, None, _c_cfg(window, dil), f"{tag}_dil{dil}")
        os_.append(o_p)
        ls_.append(l_p)
    tm = BLOCK
    lay = lambda a, dil: _in(a, (tm // dil, a.shape[1]), lambda i: (i, 0))
    views = [lay(a, dil) for a, dil in zip(os_ + ls_, dils + dils)]

    def comb(ids, *t, scratch):
        return (_combine(*[_to_natural(scratch, a, dil) for a, dil in zip(t, dils + dils)]),)

    (o,) = tcall(comb, (n // tm,), views, [_row_out(n, d, BF16, tm)], tag + "_comb",
                 scratch=((d // BLOCK * tm, BLOCK), F32))
    y = mm(o, w_out, "nn", tag + "_out", res=x)
    return y, (x, h, qkv, ops, views, o)


def odd_mixer_bwd(dy, saved, g, w_in, qg, kg, w_out, tag):
    x, h, qkv, ops, views, o = saved
    n, d = x.shape
    do = mm(dy, w_out, "nt", tag + "_do")
    d_wout = mm(o, dy, "tn", tag + "_dwout")
    tm = BLOCK
    dils = [dil for _, dil in C_PATTERNS]

    def comb_bwd(ids, *t, scratch):
        _, vjp = jax.vjp(_combine, *[_to_natural(scratch, a, dil) for a, dil in zip(t[:6], dils + dils)])
        return tuple(_to_strided(scratch, c, dil) for c, dil in zip(vjp(t[6]), dils + dils))

    cts = tcall(comb_bwd, (n // tm,), views + [_row(do, tm)],
                [_out((n // dil, dil * d), F32, (tm // dil, dil * d), lambda i: (i, 0)) for dil in dils + dils],
                tag + "_dcomb", scratch=((d // BLOCK * tm, BLOCK), F32))
    dqs, dks, dvs = [], [], []
    for p, ((window, dil), ops_d) in enumerate(zip(C_PATTERNS, ops)):
        dq, dkp, dkc, dvp, dvc = banded_bwd(ops_d, dil, _C_COLS, None, _c_cfg(window, dil),
                                            [(cts[p], lambda r: r), (cts[3 + p], lambda r: r)], f"{tag}_ddil{dil}")
        dqs.append((dq, 0, dil))
        dks += [(dkc, 0, dil), (dkp, dil, dil)]
        dvs += [(dvc, 0, dil), (dvp, dil, dil)]

    def gather(*t):
        total = lambda parts: functools.reduce(lambda a, b: a + b, parts)
        return jnp.concatenate([total(t[0:3]), total(t[3:9]), total(t[9:15])], axis=1)

    dqkv, dqg, dkg = prep_bwd(qkv, qg, kg, _OD_SEGS, dqs + dks + dvs, gather, tag + "_dqkv")
    dh = mm(dqkv, w_in, "nt", tag + "_dh")
    d_win = mm(h, dqkv, "tn", tag + "_dwin")
    dx, dg = rmsnorm_bwd(x, g, dh, dy, tag + "_dnorm")
    return dx, dg, d_win, dqg, dkg, d_wout


def xa_fwd(x, mem, g, gm, w_q, w_kv, qg, kg, w_o, tag):
    h = rmsnorm_fwd(x, g, tag + "_norm")
    q = mm(h, w_q, "nn", tag + "_q")
    mn = rmsnorm_fwd(mem, gm, tag + "_mnorm")
    kv = mm(mn, w_kv, "nn", tag + "_kv")
    o = xa_core_fwd(q, kv, qg, kg, tag + "_core")
    y = mm(o, w_o, "nn", tag + "_o", res=x)
    return y, (x, h, q, mn, kv, o)


def xa_bwd(dy, saved, mem, g, gm, w_q, w_kv, qg, kg, w_o, tag):
    x, h, q, mn, kv, o = saved
    do = mm(dy, w_o, "nt", tag + "_do", out_dtype=BF16)
    d_wo = mm(o, dy, "tn", tag + "_dwo")
    dq, dkv, dqg, dkg = xa_core_bwd(q, kv, qg, kg, do, tag + "_dcore")
    dh = mm(dq, w_q, "nt", tag + "_dh")
    d_wq = mm(h, dq, "tn", tag + "_dwq")
    dx, dg = rmsnorm_bwd(x, g, dh, dy, tag + "_dnorm")
    dmn = mm(dkv, w_kv, "nt", tag + "_dmn")
    d_wkv = mm(mn, dkv, "tn", tag + "_dwkv")
    _, dgm = rmsnorm_bwd(mem, gm, dmn, None, tag + "_dmnorm")
    return dx, dg, dgm, d_wq, d_wkv, dqg, dkg, d_wo


def loss_head(y, target, name):
    n, d = y.shape
    tm = _tile(n, 512, 8)

    def fn(ids, yt, tt):
        e = yt - tt
        return e * (1.0 / d), jnp.sum(e * e, axis=0, keepdims=True)

    return tcall(fn, (n // tm,), [_row(y, tm), _row(target, tm)], [_row_out(n, d, F32, tm), _acc_out((1, d))], name)


_ANY = pl.BlockSpec(memory_space=pl.ANY)


def all_gather_blocks(blocks):
    nb = len(blocks)

    def body(*refs):
        x_refs, out_refs = refs[:nb], refs[nb:2 * nb]
        send_sems, recv_sems, local_sems = refs[2 * nb:]
        x, y, c = lax.axis_index("x"), lax.axis_index("y"), lax.axis_index("c")
        me, sibling = (x, y, c), (x, y, 1 - c)
        chips = [(1 - x, y), (x, 1 - y), (1 - x, 1 - y)]

        def copy(b, k, blk, to, own=False):
            px, py, pc = blk
            slot = out_refs[b].at[4 * px + 2 * py + pc]
            return pltpu.make_async_remote_copy(
                src_ref=x_refs[b] if own else slot, dst_ref=slot,
                send_sem=send_sems.at[7 * b + k], recv_sem=recv_sems.at[7 * b + k], device_id=to, device_id_type=MESH)

        mine = [pltpu.make_async_copy(x_refs[b], out_refs[b].at[4 * x + 2 * y + c], local_sems.at[b]) for b in range(nb)]
        for cp in mine:
            cp.start()
        first = []
        for b in range(nb):
            first.append(copy(b, 0, me, sibling, own=True))
            first += [copy(b, 1 + j, me, (*chip, c), own=True) for j, chip in enumerate(chips)]
        for cp in first:
            cp.start()
        passed = []
        for j, chip in enumerate(chips):
            for b in range(nb):
                copy(b, 1 + j, (*chip, c), me).wait_recv()
                fwd = copy(b, 4 + j, (*chip, c), sibling)
                fwd.start()
                passed.append(fwd)
        for b in range(nb):
            copy(b, 0, sibling, me).wait_recv()
            for j, chip in enumerate(chips):
                copy(b, 4 + j, (*chip, 1 - c), me).wait_recv()
        for cp in first + passed:
            cp.wait_send()
        for cp in mine:
            cp.wait()

    return _pcall(
        body, name="weights_all_gather",
        in_specs=[_ANY] * nb, out_specs=[_ANY] * nb,
        out_shape=[jax.ShapeDtypeStruct((N_DEV,) + a.shape, a.dtype) for a in blocks],
        scratch_shapes=[pltpu.SemaphoreType.DMA((7 * nb,)), pltpu.SemaphoreType.DMA((7 * nb,)),
                        pltpu.SemaphoreType.DMA((nb,))],
    )(*blocks)


def pair_exchange(bufs):
    nb = len(bufs)

    def body(*refs):
        srcs, dsts = refs[:nb], refs[nb:2 * nb]
        send_sems, recv_sems = refs[2 * nb:]
        x, y, c = lax.axis_index("x"), lax.axis_index("y"), lax.axis_index("c")
        copies = []
        for b in range(nb):
            for j in range(4):
                cp = pltpu.make_async_remote_copy(
                    src_ref=srcs[b].at[2 * j + (1 - c)], dst_ref=dsts[b].at[j], send_sem=send_sems.at[4 * b + j],
                    recv_sem=recv_sems.at[4 * b + j], device_id=(x, y, 1 - c), device_id_type=MESH)
                cp.start()
                copies.append(cp)
        for cp in copies:
            cp.wait()

    return _pcall(
        body, name="grads_pair_exchange",
        in_specs=[_ANY] * nb, out_specs=[_ANY] * nb,
        out_shape=[jax.ShapeDtypeStruct((4,) + a.shape[1:], a.dtype) for a in bufs],
        scratch_shapes=[pltpu.SemaphoreType.DMA((4 * nb,)), pltpu.SemaphoreType.DMA((4 * nb,))],
    )(*bufs)


def pair_sum(g, got, c, out_dtype, name):
    r, w = g.shape[1:]
    tr = _tile(r, 512, 16)

    def body(c_ref, a_ref, b_ref, o_ref):
        o_ref[...] = (a_ref[...] + b_ref[...]).astype(o_ref.dtype)

    return _pcall(
        body, name=name,
        grid_spec=pltpu.PrefetchScalarGridSpec(
            num_scalar_prefetch=1, grid=(4, r // tr),
            in_specs=[pl.BlockSpec((None, tr, w), lambda j, i, c_ref: (2 * j + c_ref[0], i, 0)),
                      pl.BlockSpec((None, tr, w), lambda j, i, c_ref: (j, i, 0))],
            out_specs=pl.BlockSpec((None, tr, w), lambda j, i, c_ref: (j, i, 0))),
        out_shape=jax.ShapeDtypeStruct((4,) + g.shape[1:], out_dtype),
        compiler_params=_params(),
    )(c, g, got)


def chip_exchange(parts):
    nb = len(parts)

    def body(*refs):
        srcs, dsts = refs[:nb], refs[nb:2 * nb]
        send_sems, recv_sems, local_sems = refs[2 * nb:]
        x, y, c = lax.axis_index("x"), lax.axis_index("y"), lax.axis_index("c")
        my_chip = 2 * x + y
        copies = []
        for b in range(nb):
            mine = pltpu.make_async_copy(srcs[b].at[my_chip], dsts[b].at[my_chip], local_sems.at[b])
            mine.start()
            copies.append(mine)
            for k, (tx, ty) in enumerate([(1 - x, y), (x, 1 - y), (1 - x, 1 - y)]):
                cp = pltpu.make_async_remote_copy(
                    src_ref=srcs[b].at[2 * tx + ty], dst_ref=dsts[b].at[my_chip], send_sem=send_sems.at[3 * b + k],
                    recv_sem=recv_sems.at[3 * b + k], device_id=(tx, ty, c), device_id_type=MESH)
                cp.start()
                copies.append(cp)
        for cp in copies:
            cp.wait()

    return _pcall(
        body, name="grads_chip_exchange",
        in_specs=[_ANY] * nb, out_specs=[_ANY] * nb,
        out_shape=[jax.ShapeDtypeStruct(a.shape, a.dtype) for a in parts],
        scratch_shapes=[pltpu.SemaphoreType.DMA((3 * nb,)), pltpu.SemaphoreType.DMA((3 * nb,)),
                        pltpu.SemaphoreType.DMA((nb,))],
    )(*parts)


def chip_sum(parts, name):
    r, w = parts.shape[1:]
    tr = _tile(r, 512, 16)
    spec = lambda j: _in(parts, (None, tr, w), lambda i, j=j: (j, i, 0))

    def fn(ids, a, b, c_, d):
        a, b, c_, d = [t.astype(F32) for t in (a, b, c_, d)]
        return (((a + b) + c_) + d,)

    (out,) = tcall(fn, (r // tr,), [spec(j) for j in range(4)],
                   [_out((r, w), F32, (tr, w), lambda i: (i, 0))], name)
    return out


def adamw(w, g, m, v, name):
    shape = w.shape
    cols = shape[-1]
    rows = int(np.prod(shape[:-1]))
    w2, g2, m2, v2 = [a.reshape(rows, cols) for a in (w, g, m, v)]
    tr = _tile(rows, 256, 8) if rows % 8 == 0 else rows

    def fn(ids, wt, gt, mt, vt):
        m_new = ADAM_B1 * mt + (1.0 - ADAM_B1) * gt
        v_new = ADAM_B2 * vt + (1.0 - ADAM_B2) * (gt * gt)
        m_hat = m_new / (1.0 - ADAM_B1 ** ADAM_STEP)
        v_hat = v_new / (1.0 - ADAM_B2 ** ADAM_STEP)
        delta = -ADAM_LR * (m_hat / (jnp.sqrt(v_hat) + ADAM_EPS) + ADAM_WD * wt)
        return delta, m_new, v_new

    res = tcall(fn, (rows // tr,), [_row(a, tr) for a in (w2, g2, m2, v2)],
                [_row_out(rows, cols, F32, tr) for _ in range(3)], name)
    return [a.reshape(shape) for a in res]


_MATS = [("ffn1_w_gu", "col"), ("ffn1_w_down", "row"), ("ev_w_in", "col"), ("ev_w_out", "row"),
         ("od_w_in", "col"), ("od_w_out", "row"), ("xa_w_q", "row"), ("xa_w_kv", "col"), ("xa_w_o", "row"),
         ("ffn2_w_gu", "col"), ("ffn2_w_down", "row")]
_VECS = ["ffn1_norm", "mix_norm", "ev_q_gain", "ev_k_gain", "ev_sinks", "od_q_gain", "od_k_gain", "xa_norm",
         "xa_mem_norm", "xa_q_gain", "xa_k_gain", "ffn2_norm"]
_WEIGHTS = ["ffn1_norm", "ffn1_w_gu", "ffn1_w_down", "mix_norm", "ev_w_in", "ev_q_gain", "ev_k_gain", "ev_sinks",
            "ev_w_out", "od_w_in", "od_q_gain", "od_k_gain", "od_w_out", "xa_norm", "xa_mem_norm", "xa_w_q", "xa_w_kv",
            "xa_q_gain", "xa_k_gain", "xa_w_o", "ffn2_norm", "ffn2_w_gu", "ffn2_w_down"]


_GROUPS = [["ffn1_w_gu", "ffn2_w_gu"], ["ev_w_in"], ["od_w_in"], ["xa_w_kv"],
           ["ffn1_w_down", "ffn2_w_down", "ev_w_out", "od_w_out", "xa_w_q", "xa_w_o"]]
_AXIS = dict(_MATS)


def _gather_weights(shards):
    blocks = []
    for names in _GROUPS:
        rows = [shards[n].reshape(-1, shards[n].shape[-1]).astype(BF16) for n in names]
        blocks.append(rows[0] if len(rows) == 1 else jnp.concatenate(rows, axis=0))
    gathered = all_gather_blocks(blocks)
    full = {}
    for names, got in zip(_GROUPS, gathered):
        off = 0
        for n in names:
            l, a, b = shards[n].shape
            seg = got[:, off:off + l * a, :].reshape(N_DEV, l, a, b)
            off += l * a
            if _AXIS[n] == "row":
                full[n] = seg.transpose(1, 0, 2, 3).reshape(l, N_DEV * a, b)
            else:
                full[n] = seg.transpose(1, 2, 0, 3).reshape(l, a, N_DEV * b)
    return full


def _reduce_gradients(mats, vecs, c):
    bufs = []
    for names in _GROUPS:
        rows = []
        for n in names:
            gr = mats[n]
            l, a, b = gr.shape
            if _AXIS[n] == "row":
                rows.append(gr.reshape(l, N_DEV, a // N_DEV, b).transpose(1, 0, 2, 3).reshape(N_DEV, -1, b))
            else:
                rows.append(gr.reshape(l, a, N_DEV, b // N_DEV).transpose(2, 0, 1, 3).reshape(N_DEV, l * a, b // N_DEV))
        bufs.append(rows[0] if len(rows) == 1 else jnp.concatenate(rows, axis=1))
    vec = jnp.concatenate([vecs[n].reshape(-1) for n in _VECS])
    vec = jnp.pad(vec, (0, -vec.shape[0] % (16 * LANES)))
    bufs.append(jnp.broadcast_to(vec.reshape(1, -1, LANES), (N_DEV, vec.shape[0] // LANES, LANES)))
    got = pair_exchange(bufs)
    nm = len(_GROUPS)
    parts = [pair_sum(b, g, c, BF16 if i < nm else F32, f"grads_pair_sum{i}") for i, (b, g) in enumerate(zip(bufs, got))]
    sums = [chip_sum(p, f"grads_chip_sum{i}") for i, p in enumerate(chip_exchange(parts))]
    out = {}
    for names, tot in zip(_GROUPS, sums[:nm]):
        off = 0
        for n in names:
            l, a, b = mats[n].shape
            shape = (l, a // N_DEV, b) if _AXIS[n] == "row" else (l, a, b // N_DEV)
            out[n] = tot[off:off + shape[0] * shape[1]].reshape(shape)
            off += shape[0] * shape[1]
    flat, off = sums[nm].reshape(-1), 0
    for n in _VECS:
        out[n] = flat[off:off + vecs[n].size].reshape(vecs[n].shape)
        off += vecs[n].size
    return out


def _local_step(x, mem, target, w, full):
    depth = w["ffn1_norm"].shape[0]
    ev_in = _ev_reorder(full["ev_w_in"])
    row = lambda a, l: a[l:l + 1]
    saved = []
    for l in range(depth):
        t = f"l{l}"
        j = l // 2
        x, s1 = ffn_fwd(x, row(w["ffn1_norm"], l), full["ffn1_w_gu"][l], full["ffn1_w_down"][l], t + "_ffn1")
        if l % 2 == 0:
            x, s2 = even_mixer_fwd(x, row(w["mix_norm"], l), ev_in[j], row(w["ev_q_gain"], j), row(w["ev_k_gain"], j),
                                   row(w["ev_sinks"], j), full["ev_w_out"][j], t + "_ev")
        else:
            x, s2 = odd_mixer_fwd(x, row(w["mix_norm"], l), full["od_w_in"][j], row(w["od_q_gain"], j),
                                  row(w["od_k_gain"], j), full["od_w_out"][j], t + "_od")
        x, s3 = xa_fwd(x, mem, row(w["xa_norm"], l), row(w["xa_mem_norm"], l), full["xa_w_q"][l], full["xa_w_kv"][l],
                       row(w["xa_q_gain"], l), row(w["xa_k_gain"], l), full["xa_w_o"][l], t + "_xa")
        x, s4 = ffn_fwd(x, row(w["ffn2_norm"], l), full["ffn2_w_gu"][l], full["ffn2_w_down"][l], t + "_ffn2")
        saved.append((s1, s2, s3, s4))
    dx, sq = loss_head(x, target, "loss_head")
    loss = 0.5 * jnp.sum(sq) / x.shape[1]

    gm = {n: [None] * full[n].shape[0] for n, _ in _MATS}
    gv = {n: [None] * w[n].shape[0] for n in _VECS}
    for l in reversed(range(depth)):
        t = f"l{l}"
        j = l // 2
        s1, s2, s3, s4 = saved[l]
        dx, gv["ffn2_norm"][l], gm["ffn2_w_gu"][l], gm["ffn2_w_down"][l] = ffn_bwd(
            dx, s4, row(w["ffn2_norm"], l), full["ffn2_w_gu"][l], full["ffn2_w_down"][l], t + "_ffn2")
        (dx, gv["xa_norm"][l], gv["xa_mem_norm"][l], gm["xa_w_q"][l], gm["xa_w_kv"][l], gv["xa_q_gain"][l],
         gv["xa_k_gain"][l], gm["xa_w_o"][l]) = xa_bwd(
            dx, s3, mem, row(w["xa_norm"], l), row(w["xa_mem_norm"], l), full["xa_w_q"][l], full["xa_w_kv"][l],
            row(w["xa_q_gain"], l), row(w["xa_k_gain"], l), full["xa_w_o"][l], t + "_xa")
        if l % 2 == 0:
            (dx, gv["mix_norm"][l], d_win, gv["ev_q_gain"][j], gv["ev_k_gain"][j], gv["ev_sinks"][j],
             gm["ev_w_out"][j]) = even_mixer_bwd(
                dx, s2, row(w["mix_norm"], l), ev_in[j], row(w["ev_q_gain"], j), row(w["ev_k_gain"], j),
                row(w["ev_sinks"], j), full["ev_w_out"][j], t + "_ev")
            gm["ev_w_in"][j] = _ev_restore(d_win)
        else:
            (dx, gv["mix_norm"][l], gm["od_w_in"][j], gv["od_q_gain"][j], gv["od_k_gain"][j],
             gm["od_w_out"][j]) = odd_mixer_bwd(
                dx, s2, row(w["mix_norm"], l), full["od_w_in"][j], row(w["od_q_gain"], j), row(w["od_k_gain"], j),
                full["od_w_out"][j], t + "_od")
        dx, gv["ffn1_norm"][l], gm["ffn1_w_gu"][l], gm["ffn1_w_down"][l] = ffn_bwd(
            dx, s1, row(w["ffn1_norm"], l), full["ffn1_w_gu"][l], full["ffn1_w_down"][l], t + "_ffn1")
    mats = {n: jnp.stack(v) for n, v in gm.items()}
    vecs = {n: jnp.concatenate(v, axis=0) for n, v in gv.items()}
    return loss, dx, mats, vecs


def kernel(x, mem, ffn1_norm, ffn1_w_gu, ffn1_w_down, mix_norm, ev_w_in, ev_q_gain, ev_k_gain, ev_sinks, ev_w_out, od_w_in, od_q_gain, od_k_gain, od_w_out, xa_norm, xa_mem_norm, xa_w_q, xa_w_kv, xa_q_gain, xa_k_gain, xa_w_o, ffn2_norm, ffn2_w_gu, ffn2_w_down, loss_target, m_ffn1_norm, m_ffn1_w_gu, m_ffn1_w_down, m_mix_norm, m_ev_w_in, m_ev_q_gain, m_ev_k_gain, m_ev_sinks, m_ev_w_out, m_od_w_in, m_od_q_gain, m_od_k_gain, m_od_w_out, m_xa_norm, m_xa_mem_norm, m_xa_w_q, m_xa_w_kv, m_xa_q_gain, m_xa_k_gain, m_xa_w_o, m_ffn2_norm, m_ffn2_w_gu, m_ffn2_w_down, v_ffn1_norm, v_ffn1_w_gu, v_ffn1_w_down, v_mix_norm, v_ev_w_in, v_ev_q_gain, v_ev_k_gain, v_ev_sinks, v_ev_w_out, v_od_w_in, v_od_q_gain, v_od_k_gain, v_od_w_out, v_xa_norm, v_xa_mem_norm, v_xa_w_q, v_xa_w_kv, v_xa_q_gain, v_xa_k_gain, v_xa_w_o, v_ffn2_norm, v_ffn2_w_gu, v_ffn2_w_down):
    w = dict(ffn1_norm=ffn1_norm, ffn1_w_gu=ffn1_w_gu, ffn1_w_down=ffn1_w_down, mix_norm=mix_norm, ev_w_in=ev_w_in, ev_q_gain=ev_q_gain, ev_k_gain=ev_k_gain, ev_sinks=ev_sinks, ev_w_out=ev_w_out, od_w_in=od_w_in, od_q_gain=od_q_gain, od_k_gain=od_k_gain, od_w_out=od_w_out, xa_norm=xa_norm, xa_mem_norm=xa_mem_norm, xa_w_q=xa_w_q, xa_w_kv=xa_w_kv, xa_q_gain=xa_q_gain, xa_k_gain=xa_k_gain, xa_w_o=xa_w_o, ffn2_norm=ffn2_norm, ffn2_w_gu=ffn2_w_gu, ffn2_w_down=ffn2_w_down)
    m = dict(ffn1_norm=m_ffn1_norm, ffn1_w_gu=m_ffn1_w_gu, ffn1_w_down=m_ffn1_w_down, mix_norm=m_mix_norm, ev_w_in=m_ev_w_in, ev_q_gain=m_ev_q_gain, ev_k_gain=m_ev_k_gain, ev_sinks=m_ev_sinks, ev_w_out=m_ev_w_out, od_w_in=m_od_w_in, od_q_gain=m_od_q_gain, od_k_gain=m_od_k_gain, od_w_out=m_od_w_out, xa_norm=m_xa_norm, xa_mem_norm=m_xa_mem_norm, xa_w_q=m_xa_w_q, xa_w_kv=m_xa_w_kv, xa_q_gain=m_xa_q_gain, xa_k_gain=m_xa_k_gain, xa_w_o=m_xa_w_o, ffn2_norm=m_ffn2_norm, ffn2_w_gu=m_ffn2_w_gu, ffn2_w_down=m_ffn2_w_down)
    v = dict(ffn1_norm=v_ffn1_norm, ffn1_w_gu=v_ffn1_w_gu, ffn1_w_down=v_ffn1_w_down, mix_norm=v_mix_norm, ev_w_in=v_ev_w_in, ev_q_gain=v_ev_q_gain, ev_k_gain=v_ev_k_gain, ev_sinks=v_ev_sinks, ev_w_out=v_ev_w_out, od_w_in=v_od_w_in, od_q_gain=v_od_q_gain, od_k_gain=v_od_k_gain, od_w_out=v_od_w_out, xa_norm=v_xa_norm, xa_mem_norm=v_xa_mem_norm, xa_w_q=v_xa_w_q, xa_w_kv=v_xa_w_kv, xa_q_gain=v_xa_q_gain, xa_k_gain=v_xa_k_gain, xa_w_o=v_xa_w_o, ffn2_norm=v_ffn2_norm, ffn2_w_gu=v_ffn2_w_gu, ffn2_w_down=v_ffn2_w_down)

    full = _gather_weights(w)
    loss, dx, mats, vecs = _local_step(x[0], mem[0], loss_target[0], w, full)
    c = lax.axis_index("c").astype(jnp.int32).reshape(1)
    grads = _reduce_gradients(mats, vecs, c)
    loss = lax.psum(loss, ("x", "y", "c"))

    delta, new_m, new_v = {}, {}, {}
    for n in _WEIGHTS:
        delta[n], new_m[n], new_v[n] = adamw(w[n], grads[n], m[n], v[n], "adamw_" + n)
    return (loss, dx[None], *[grads[n] for n in _WEIGHTS], *[delta[n] for n in _WEIGHTS],
            *[new_m[n] for n in _WEIGHTS], *[new_v[n] for n in _WEIGHTS])
```

```python
import functools

import numpy as np
import jax
import jax.numpy as jnp
from jax import lax
from jax.experimental import pallas as pl
from jax.experimental.pallas import tpu as pltpu

F32 = jnp.float32
BF16 = jnp.bfloat16
MESH = pl.DeviceIdType.MESH

HEAD_DIM = 64
BLOCK = 128
RMS_EPS = 1e-6
A_Q_HEADS, A_KV_HEADS = 8, 2
B_HEADS = 8
C_HEADS = 16
C_PATTERNS = ((128, 1), (512, 4), (2048, 16))
X_HEADS = 4
N_DEV = 8
LANES = 1024
VMEM_LIMIT_BYTES = 56 * 1024 * 1024
SB_SKIP_LOG = -110.0
NEG_BIG = -1e30

ADAM_LR, ADAM_B1, ADAM_B2, ADAM_EPS, ADAM_WD, ADAM_STEP = 0.001, 0.9, 0.999, 1e-08, 0.01, 10

NN = (((1,), (0,)), ((), ()))
NT = (((1,), (1,)), ((), ()))
TN = (((0,), (0,)), ((), ()))


def _pcall(body, **kw):
    return pl.pallas_call(body, **kw)


def _params(**kw):
    return pltpu.CompilerParams(vmem_limit_bytes=VMEM_LIMIT_BYTES, **kw)


def _tile(dim, cap, unit=128):
    if dim <= cap:
        return dim
    t = (cap // unit) * unit
    while t >= unit:
        if dim % t == 0:
            return t
        t -= unit
    raise ValueError(f"no tile for {dim} under {cap}")


def _dot(a, b, dims):
    return lax.dot_general(a.astype(BF16), b.astype(BF16), dims, preferred_element_type=F32)


@functools.partial(jax.custom_vjp, nondiff_argnums=(2,))
def _dot_vjp(a, b, nt):
    return _dot(a, b, NT if nt else NN)


def _dot_vjp_fwd(a, b, nt):
    return _dot(a, b, NT if nt else NN), (a.astype(BF16), b.astype(BF16))


def _dot_vjp_bwd(nt, res, g):
    a, b = res
    if nt:
        return _dot(g, b, NN), _dot(g, a, TN)
    return _dot(g, b, NT), _dot(a, g, TN)


_dot_vjp.defvjp(_dot_vjp_fwd, _dot_vjp_bwd)


def _plain_dot(a, b, nt):
    return _dot(a, b, NT if nt else NN)


def _split_dot(x, mat, terms=2):
    out, rem = None, x
    for t in range(terms):
        part = rem.astype(BF16)
        d = lax.dot_general(part, mat, NN, preferred_element_type=F32)
        out = d if out is None else out + d
        if t + 1 < terms:
            rem = rem - part.astype(F32)
    return out


@functools.partial(jax.custom_vjp, nondiff_argnums=(3,))
def _split_dot_vjp(x, mat, mat_t, terms):
    return _split_dot(x, mat, terms)


def _split_dot_vjp_fwd(x, mat, mat_t, terms):
    return _split_dot(x, mat, terms), mat_t


def _split_dot_vjp_bwd(terms, mat_t, g):
    return _split_dot(g, mat_t, terms), None, None


_split_dot_vjp.defvjp(_split_dot_vjp_fwd, _split_dot_vjp_bwd)


def _plain_split(x, mat, mat_t, terms):
    return _split_dot(x, mat, terms)


def _tri(after):
    j = lax.broadcasted_iota(jnp.int32, (BLOCK, BLOCK), 0)
    s = lax.broadcasted_iota(jnp.int32, (BLOCK, BLOCK), 1)
    return jnp.where(j > s if after else j < s, 1.0, 0.0).astype(BF16)


def _in(a, block, imap):
    return (a, block, imap)


def _out(shape, dtype, block, imap, acc=False):
    return (shape, dtype, block, imap, acc)


def tcall(fn, grid, ins, outs, name, scratch=None):
    nin = len(ins)
    nout = len(outs)
    ngrid = len(grid)

    def body(*refs):
        ids = tuple(pl.program_id(a) for a in range(ngrid))
        extra = {} if scratch is None else {"scratch": refs[nin + nout]}
        res = fn(ids, *[r[...] for r in refs[:nin]], **extra)
        first = ids[0] == 0
        for a in range(1, ngrid):
            first = jnp.logical_and(first, ids[a] == 0)
        for o_ref, r, spec in zip(refs[nin:nin + nout], res, outs):
            if spec[4]:
                @pl.when(first)
                def _(o_ref=o_ref):
                    o_ref[...] = jnp.zeros(o_ref.shape, o_ref.dtype)
                o_ref[...] += r.astype(o_ref.dtype)
            else:
                o_ref[...] = r.astype(o_ref.dtype)

    return _pcall(
        body, name=name, grid=grid,
        in_specs=[pl.BlockSpec(b, m) for (_, b, m) in ins],
        out_specs=[pl.BlockSpec(b, m) for (_, _, b, m, _) in outs],
        out_shape=[jax.ShapeDtypeStruct(s, d) for (s, d, _, _, _) in outs],
        scratch_shapes=[] if scratch is None else [pltpu.VMEM(*scratch)],
        compiler_params=_params(),
    )(*[a for (a, _, _) in ins])


def _to_strided(scr, nat, d):
    if d == 1:
        return nat
    t, w = nat.shape
    nc = w // BLOCK
    for c in range(nc):
        scr[c * t:(c + 1) * t, :] = nat[:, c * BLOCK:(c + 1) * BLOCK]
    return jnp.concatenate([scr[pl.ds(c * t + r, t // d, stride=d), :] for r in range(d) for c in range(nc)], axis=1)


def _to_natural(scr, st, d):
    if d == 1:
        return st.astype(F32)
    t, w = st.shape[0] * d, st.shape[1] // d
    nc = w // BLOCK
    st = st.astype(F32)
    for r in range(d):
        for c in range(nc):
            scr[pl.ds(c * t + r, t // d, stride=d), :] = st[:, r * w + c * BLOCK:r * w + (c + 1) * BLOCK]
    return jnp.concatenate([scr[c * t:(c + 1) * t, :] for c in range(nc)], axis=1)


def _row(a, tm, width=None, cb=0):
    width = a.shape[1] if width is None else width
    return _in(a, (tm, width), lambda i, cb=cb: (i, cb))


def _full(a):
    zeros = (0,) * a.ndim
    return _in(a, a.shape, lambda *ids: zeros)


def _row_out(n, width, dtype, tm):
    return _out((n, width), dtype, (tm, width), lambda i: (i, 0))


def _acc_out(shape):
    zeros = (0,) * len(shape)
    return _out(shape, F32, shape, lambda *ids: zeros, acc=True)


def mm(a, b, mode, name, *, out_dtype=F32, scale=1.0, res=None):
    if mode == "nn":
        (m, k), (k2, n) = a.shape, b.shape
    elif mode == "nt":
        (m, k), (n, k2) = a.shape, b.shape
    else:
        (k, m), (k2, n) = a.shape, b.shape
    assert k == k2, (a.shape, b.shape, mode)
    tm, tn, tk = _tile(m, 512), _tile(n, 1408), _tile(k, 1408)
    nk = k // tk
    dims = {"nn": NN, "nt": NT, "tn": TN}[mode]
    has_res = res is not None

    def body(*refs):
        if has_res:
            a_ref, b_ref, r_ref, o_ref, acc_ref = refs
        else:
            a_ref, b_ref, o_ref, acc_ref = refs
        kk = pl.program_id(2)

        @pl.when(kk == 0)
        def _():
            acc_ref[...] = jnp.zeros(acc_ref.shape, F32)

        acc_ref[...] += _dot(a_ref[...], b_ref[...], dims)

        @pl.when(kk == nk - 1)
        def _():
            out = acc_ref[...]
            if scale != 1.0:
                out = out * scale
            if has_res:
                out = out + r_ref[...]
            o_ref[...] = out.astype(o_ref.dtype)

    a_spec = (pl.BlockSpec((tk, tm), lambda i, j, kk: (kk, i)) if mode == "tn"
              else pl.BlockSpec((tm, tk), lambda i, j, kk: (i, kk)))
    b_spec = (pl.BlockSpec((tn, tk), lambda i, j, kk: (j, kk)) if mode == "nt"
              else pl.BlockSpec((tk, tn), lambda i, j, kk: (kk, j)))
    in_specs = [a_spec, b_spec]
    args = [a, b]
    if has_res:
        in_specs.append(pl.BlockSpec((tm, tn), lambda i, j, kk: (i, j)))
        args.append(res)
    return _pcall(
        body, name=name, grid=(m // tm, n // tn, nk),
        in_specs=in_specs,
        out_specs=pl.BlockSpec((tm, tn), lambda i, j, kk: (i, j)),
        out_shape=jax.ShapeDtypeStruct((m, n), out_dtype),
        scratch_shapes=[pltpu.VMEM((tm, tn), F32)],
        compiler_params=_params(dimension_semantics=("parallel", "parallel", "arbitrary")),
    )(*args)


def _rms(x, g):
    return x * lax.rsqrt(jnp.mean(x * x, axis=-1, keepdims=True) + RMS_EPS) * g


def _silu_mul(gate, up):
    return gate / (1.0 + jnp.exp(-gate)) * up


def _indicator(shape, head_axis, mod):
    lane = lax.broadcasted_iota(jnp.int32, shape, head_axis)
    other = lax.broadcasted_iota(jnp.int32, shape, 1 - head_axis)
    lane = jnp.bitwise_and(lane, HEAD_DIM - 1) if mod else jnp.right_shift(lane, 6)
    return jnp.where(lane == other, 1.0, 0.0).astype(BF16)


def _head_rms(split, xs, g):
    w = xs.shape[1]
    to_head, from_head = _indicator((w, BLOCK), 0, False), _indicator((BLOCK, w), 1, False)
    to_lane, from_lane = _indicator((HEAD_DIM, w), 1, True), _indicator((w, HEAD_DIM), 0, True)
    ss = split(xs * xs, to_head, from_head, 3)
    r = lax.rsqrt(ss * (1.0 / HEAD_DIM) + RMS_EPS)
    g_all = split(jnp.broadcast_to(g, (8, HEAD_DIM)), to_lane, from_lane, 3)[0:1]
    return xs * split(r, from_head, to_head, 3) * g_all


def _prep(split, x, qg, kg, segs):
    parts = []
    for start, width, kind in segs:
        xs = x[:, start:start + width]
        parts.append(xs if kind == "raw" else _head_rms(split, xs, qg if kind == "q" else kg))
    return jnp.concatenate(parts, axis=1)


def prep_fwd(x, qg, kg, segs, dils, name):
    n, w = x.shape
    tm = _tile(n, 256, 8)

    def fn(ids, xt, a, b, scratch):
        ops = _prep(_plain_split, xt, a, b, segs)
        return tuple(_to_strided(scratch, ops, d) for d in dils)

    return tcall(fn, (n // tm,), [_row(x, tm), _full(qg), _full(kg)],
                 [_out((n // d, d * w), BF16, (tm // d, d * w), lambda i: (i, 0)) for d in dils], name,
                 scratch=((w // BLOCK * tm, BLOCK), F32))


def prep_bwd(x, qg, kg, segs, grads, gather, name):
    n, w = x.shape
    tm = BLOCK
    nblk = n // tm

    def fn(ids, xt, a, b, *t, scratch):
        t = [_to_natural(scratch, ti, d) for ti, (_, _, d) in zip(t, grads)]
        t = [jnp.where(ids[0] + sh < nblk, ti, 0.0) if sh else ti for ti, (_, sh, _) in zip(t, grads)]
        _, vjp = jax.vjp(lambda x_, a_, b_: _prep(_split_dot_vjp, x_, a_, b_, segs), xt, a, b)
        return vjp(gather(*t))

    specs = [_in(a, (tm // d, a.shape[1]), (lambda i, sh=sh: (jnp.minimum(i + sh, nblk - 1), 0))) for a, sh, d in grads]
    wmax = max(a.shape[1] // d for a, _, d in grads)
    return tcall(fn, (nblk,), [_row(x, tm), _full(qg), _full(kg)] + specs,
                 [_row_out(n, w, BF16, tm), _acc_out(qg.shape), _acc_out(kg.shape)], name,
                 scratch=((wmax // BLOCK * tm, BLOCK), F32))


def rmsnorm_fwd(x, g, name):
    n, d = x.shape
    tm = _tile(n, 512, 8)
    (h,) = tcall(lambda ids, xt, gt: (_rms(xt, gt),), (n // tm,), [_row(x, tm), _full(g)],
                 [_row_out(n, d, BF16, tm)], name)
    return h


def rmsnorm_bwd(x, g, dh, dres, name):
    n, d = x.shape
    tm = _tile(n, 256, 8)

    def fn(ids, xt, gt, dht, *rest):
        _, vjp = jax.vjp(_rms, xt, gt)
        dx, dg = vjp(dht.astype(F32))
        if rest:
            dx = dx + rest[0]
        return dx, dg

    ins = [_row(x, tm), _full(g), _row(dh, tm)] + ([_row(dres, tm)] if dres is not None else [])
    return tcall(fn, (n // tm,), ins, [_row_out(n, d, F32, tm), _acc_out(g.shape)], name)


def ffn_fwd(x, g, w_gu, w_down, tag):
    n = x.shape[0]
    f = w_down.shape[0]
    h = rmsnorm_fwd(x, g, tag + "_norm")
    gu = mm(h, w_gu, "nn", tag + "_gu")
    tm = _tile(n, 128, 8)
    (a,) = tcall(lambda ids, gt, ut: (_silu_mul(gt, ut),), (n // tm,),
                 [_row(gu, tm, f, 0), _row(gu, tm, f, 1)], [_row_out(n, f, BF16, tm)], tag + "_act")
    y = mm(a, w_down, "nn", tag + "_down", scale=0.5, res=x)
    return y, (x, h, gu, a)


def ffn_bwd(dy, saved, g, w_gu, w_down, tag):
    x, h, gu, a = saved
    n = x.shape[0]
    f = w_down.shape[0]
    da = mm(dy, w_down, "nt", tag + "_da", scale=0.5)
    d_wdown = mm(a, dy, "tn", tag + "_dwd", scale=0.5)
    tm = _tile(n, 128, 8)

    def act_bwd(ids, gt, ut, dat):
        _, vjp = jax.vjp(_silu_mul, gt, ut)
        dg, du = vjp(dat)
        return (jnp.concatenate([dg, du], axis=1),)

    (dgu,) = tcall(act_bwd, (n // tm,), [_row(gu, tm, f, 0), _row(gu, tm, f, 1), _row(da, tm)],
                   [_row_out(n, 2 * f, BF16, tm)], tag + "_dact")
    dh = mm(dgu, w_gu, "nt", tag + "_dh")
    d_wgu = mm(h, dgu, "tn", tag + "_dwgu")
    dx, dg = rmsnorm_bwd(x, g, dh, dy, tag + "_dnorm")
    return dx, dg, d_wgu, d_wdown


def _alibi(n_heads):
    return [float(s) for s in np.asarray(2.0 ** (-8.0 * np.arange(1, n_heads + 1) / n_heads), dtype=np.float32)]


def _banded_tile(dot, first, q, kp, kc, vp, vc, sinks, *, hkv, grp, max_dist, step, slopes, want_lse):
    row = lax.broadcasted_iota(jnp.int32, (BLOCK, 2 * BLOCK), 0)
    col = lax.broadcasted_iota(jnp.int32, (BLOCK, 2 * BLOCK), 1)
    dist = row + BLOCK - col
    valid = (dist >= 0) & (dist <= max_dist) & ((col >= BLOCK) | jnp.logical_not(first))
    distf = dist.astype(F32)

    def head(hd, qh, k2, v2):
        s = dot(qh, k2, True) * (HEAD_DIM ** -0.5)
        s = jnp.where(valid, s - (slopes[hd] * step) * distf, NEG_BIG)
        m = jnp.max(s, axis=-1, keepdims=True)
        if sinks is not None:
            pick = lax.broadcasted_iota(jnp.int32, sinks.shape, 1) == hd
            sk = jnp.sum(jnp.where(pick, sinks, 0.0), axis=1, keepdims=True)
            m = jnp.maximum(m, sk)
        m = lax.stop_gradient(m)
        p = jnp.exp(s - m)
        denom = jnp.sum(p, axis=-1, keepdims=True)
        if sinks is not None:
            denom = denom + jnp.exp(sk - m)
        return dot(p / denom, v2, False), m + jnp.log(denom)

    outs, lses = [], []
    if grp == 1:
        low = lax.broadcasted_iota(jnp.int32, (BLOCK, BLOCK), 1) < HEAD_DIM
        for pr in range(hkv // 2):
            sl = slice(pr * BLOCK, (pr + 1) * BLOCK)
            q2 = q[:, sl]
            k2 = jnp.concatenate([kp[:, sl], kc[:, sl]], axis=0)
            v2 = jnp.concatenate([vp[:, sl], vc[:, sl]], axis=0)
            o0, l0 = head(2 * pr, jnp.where(low, q2, 0.0), k2, v2)
            o1, l1 = head(2 * pr + 1, jnp.where(low, 0.0, q2), k2, v2)
            outs.append(jnp.where(low, o0, o1))
            lses.append(jnp.where(low, l0, l1))
    else:
        for hk in range(hkv):
            sl = slice(hk * HEAD_DIM, (hk + 1) * HEAD_DIM)
            k2 = jnp.concatenate([kp[:, sl], kc[:, sl]], axis=0)
            v2 = jnp.concatenate([vp[:, sl], vc[:, sl]], axis=0)
            for gi in range(grp):
                hd = hk * grp + gi
                o_h, l_h = head(hd, q[:, hd * HEAD_DIM:(hd + 1) * HEAD_DIM], k2, v2)
                outs.append(o_h)
                lses.append(jnp.broadcast_to(l_h, (BLOCK, HEAD_DIM)))
    o = jnp.concatenate(outs, axis=1)
    if want_lse:
        return o, jnp.concatenate(lses, axis=1)
    return (o,)


def _banded_specs(view, qcol, kcol, vcol, wq, wkv):
    def at(colfn, prev):
        if prev:
            return lambda r, n: (jnp.maximum(n - 1, 0), colfn(r))
        return lambda r, n: (n, colfn(r))
    return [
        _in(view, (BLOCK, wq), at(qcol, False)),
        _in(view, (BLOCK, wkv), at(kcol, True)),
        _in(view, (BLOCK, wkv), at(kcol, False)),
        _in(view, (BLOCK, wkv), at(vcol, True)),
        _in(view, (BLOCK, wkv), at(vcol, False)),
    ]


def banded_fwd(view, dil, cols, sinks, cfg, name):
    ns = view.shape[0]
    nb = ns // BLOCK
    wq, wkv = cfg["hkv"] * cfg["grp"] * HEAD_DIM, cfg["hkv"] * HEAD_DIM
    has_sinks = sinks is not None

    def fn(ids, q, kp, kc, vp, vc, *rest):
        q, kp, kc, vp, vc = [a.astype(F32) for a in (q, kp, kc, vp, vc)]
        return _banded_tile(_plain_dot, ids[1] == 0, q, kp, kc, vp, vc, rest[0] if has_sinks else None, **cfg)

    ins = _banded_specs(view, *cols, wq, wkv) + ([_full(sinks)] if has_sinks else [])
    outs = [_out((ns, dil * wq), F32 if cfg["want_lse"] else BF16, (BLOCK, wq), lambda r, n: (n, r))]
    if cfg["want_lse"]:
        outs.append(_out((ns, dil * wq), F32, (BLOCK, wq), lambda r, n: (n, r)))
    return tcall(fn, (dil, nb), ins, outs, name)


def banded_bwd(view, dil, cols, sinks, cfg, cts, name):
    ns = view.shape[0]
    nb = ns // BLOCK
    wq, wkv = cfg["hkv"] * cfg["grp"] * HEAD_DIM, cfg["hkv"] * HEAD_DIM
    has_sinks = sinks is not None
    assert len(cts) == (2 if cfg["want_lse"] else 1)

    def fn(ids, q, kp, kc, vp, vc, *rest):
        sk = rest[0] if has_sinks else None
        ct = rest[1 if has_sinks else 0:]
        first = ids[1] == 0

        def f(q, kp, kc, vp, vc, *s):
            return _banded_tile(_dot_vjp, first, q, kp, kc, vp, vc, s[0] if has_sinks else None, **cfg)

        prim = tuple(a.astype(F32) for a in (q, kp, kc, vp, vc)) + ((sk,) if has_sinks else ())
        _, vjp = jax.vjp(f, *prim)
        return vjp(tuple(c.astype(F32) for c in ct))

    ins = (_banded_specs(view, *cols, wq, wkv) + ([_full(sinks)] if has_sinks else [])
           + [_in(a, (BLOCK, wq), (lambda r, n, cf=cf: (n, cf(r)))) for (a, cf) in cts])
    blk = lambda w: _out((ns, dil * w), F32, (BLOCK, w), lambda r, n: (n, r))
    outs = [blk(wq), blk(wkv), blk(wkv), blk(wkv), blk(wkv)]
    if has_sinks:
        outs.append(_acc_out(sinks.shape))
    return tcall(fn, (dil, nb), ins, outs, name)


def _log_sigmoid(z):
    return jnp.minimum(z, 0.0) - jnp.log(1.0 + jnp.exp(-jnp.abs(z)))


SB_PAIRS = 4


def _sb_pair(dot, suffix, qh, kb, vb, r_in, mask):
    z = dot(qh, kb, True) * (HEAD_DIM ** -0.5)
    lsp = _log_sigmoid(z)
    log_keep = jnp.where(mask, lsp - z, 0.0)
    log_after = suffix(log_keep) + r_in
    a = jnp.where(mask, jnp.exp(lsp + log_after), 0.0)
    return dot(a, vb, False), r_in + jnp.sum(log_keep, axis=1, keepdims=True)


def sb_fwd(qkv, qcb, kcb, vcb, name):
    s = qkv.shape[0]
    nb = s // BLOCK
    pairs = B_HEADS // 2
    wide = SB_PAIRS * BLOCK
    nh = 2 * SB_PAIRS
    assert pairs % SB_PAIRS == 0 and qcb % SB_PAIRS == 0 and kcb % SB_PAIRS == 0 and vcb % SB_PAIRS == 0

    def body(q_ref, k_ref, v_ref, o_ref):
        n = pl.program_id(1)
        low = lax.broadcasted_iota(jnp.int32, (BLOCK, BLOCK), 1) < HEAD_DIM
        before = (lax.broadcasted_iota(jnp.int32, (BLOCK, BLOCK), 1)
                  < lax.broadcasted_iota(jnp.int32, (BLOCK, BLOCK), 0))
        after = _tri(True)
        suffix = lambda t: _split_dot(t, after)
        qs = []
        for p in range(SB_PAIRS):
            q2 = q_ref[:, p * BLOCK:(p + 1) * BLOCK].astype(F32)
            qs += [jnp.where(low, q2, 0.0), jnp.where(low, 0.0, q2)]

        def cond(c):
            return jnp.logical_and(c[0] >= 0, c[1] > SB_SKIP_LOG)

        def step(c):
            kb, _, rs, accs = c
            rows = pl.ds(pl.multiple_of(kb * BLOCK, BLOCK), BLOCK)
            mask = jnp.logical_or(before, kb != n)
            new_r, new_acc, top = [], [], None
            for h in range(nh):
                cols = slice((h // 2) * BLOCK, (h // 2 + 1) * BLOCK)
                o_part, r_out = _sb_pair(_plain_dot, suffix, qs[h], k_ref[rows, cols], v_ref[rows, cols], rs[h], mask)
                new_r.append(r_out)
                new_acc.append(accs[h] + o_part)
                top = jnp.max(r_out) if top is None else jnp.maximum(top, jnp.max(r_out))
            return kb - 1, top, tuple(new_r), tuple(new_acc)

        init = (n, jnp.float32(0.0), tuple(jnp.zeros((BLOCK, 1), F32) for _ in range(nh)),
                tuple(jnp.zeros((BLOCK, BLOCK), F32) for _ in range(nh)))
        accs = lax.while_loop(cond, step, init)[3]
        for p in range(SB_PAIRS):
            o_ref[:, p * BLOCK:(p + 1) * BLOCK] = jnp.where(low, accs[2 * p], accs[2 * p + 1]).astype(o_ref.dtype)

    return _pcall(
        body, name=name, grid=(pairs // SB_PAIRS, nb),
        in_specs=[pl.BlockSpec((BLOCK, wide), lambda g, n: (n, qcb // SB_PAIRS + g)),
                  pl.BlockSpec((s, wide), lambda g, n: (0, kcb // SB_PAIRS + g), pipeline_mode=pl.Buffered(1)),
                  pl.BlockSpec((s, wide), lambda g, n: (0, vcb // SB_PAIRS + g), pipeline_mode=pl.Buffered(1))],
        out_specs=pl.BlockSpec((BLOCK, wide), lambda g, n: (n, g)),
        out_shape=jax.ShapeDtypeStruct((s, pairs * BLOCK), BF16),
        compiler_params=_params(),
    )(qkv, qkv, qkv)


def sb_bwd(qkv, qcb, kcb, vcb, do, docb, name):
    s = qkv.shape[0]
    nb = s // BLOCK
    pairs = B_HEADS // 2
    wide = SB_PAIRS * BLOCK
    nh = 2 * SB_PAIRS
    assert docb % SB_PAIRS == 0

    def body(q_ref, k_ref, v_ref, do_ref, dq_ref, dk_ref, dv_ref, r_ref):
        n = pl.program_id(1)

        @pl.when(n == 0)
        def _():
            dk_ref[...] = jnp.zeros(dk_ref.shape, F32)
            dv_ref[...] = jnp.zeros(dv_ref.shape, F32)

        low = lax.broadcasted_iota(jnp.int32, (BLOCK, BLOCK), 1) < HEAD_DIM
        before = (lax.broadcasted_iota(jnp.int32, (BLOCK, BLOCK), 1)
                  < lax.broadcasted_iota(jnp.int32, (BLOCK, BLOCK), 0))
        after, earlier = _tri(True), _tri(False)
        suffix = lambda t: _split_dot_vjp(t, after, earlier, 2)
        qs, dos = [], []
        for p in range(SB_PAIRS):
            q2 = q_ref[:, p * BLOCK:(p + 1) * BLOCK].astype(F32)
            do2 = do_ref[:, p * BLOCK:(p + 1) * BLOCK].astype(F32)
            qs += [jnp.where(low, q2, 0.0), jnp.where(low, 0.0, q2)]
            dos += [jnp.where(low, do2, 0.0), jnp.where(low, 0.0, do2)]

        def cond(c):
            return jnp.logical_and(c[0] >= 0, c[1] > SB_SKIP_LOG)

        def down(c):
            kb, _, rs = c
            rows = pl.ds(pl.multiple_of(kb * BLOCK, BLOCK), BLOCK)
            mask = jnp.logical_or(before, kb != n)
            new_r, top = [], None
            for h in range(nh):
                cols = slice((h // 2) * BLOCK, (h // 2 + 1) * BLOCK)
                r_ref[h, kb] = rs[h]
                z = _dot(qs[h], k_ref[rows, cols], NT) * (HEAD_DIM ** -0.5)
                log_keep = jnp.where(mask, _log_sigmoid(z) - z, 0.0)
                r_out = rs[h] + jnp.sum(log_keep, axis=1, keepdims=True)
                new_r.append(r_out)
                top = jnp.max(r_out) if top is None else jnp.maximum(top, jnp.max(r_out))
            return kb - 1, top, tuple(new_r)

        init = (n, jnp.float32(0.0), tuple(jnp.zeros((BLOCK, 1), F32) for _ in range(nh)))
        last = lax.while_loop(cond, down, init)[0] + 1

        def up(kb, c):
            dqs, g_rs = c
            rows = pl.ds(pl.multiple_of(kb * BLOCK, BLOCK), BLOCK)
            mask = jnp.logical_or(before, kb != n)
            new_dq, new_g = [], []
            for h in range(nh):
                cols = slice((h // 2) * BLOCK, (h // 2 + 1) * BLOCK)
                _, vjp = jax.vjp(lambda q_, k_, v_, r_: _sb_pair(_dot_vjp, suffix, q_, k_, v_, r_, mask),
                                 qs[h], k_ref[rows, cols].astype(F32), v_ref[rows, cols].astype(F32), r_ref[h, kb])
                dq_c, dk_c, dv_c, g_in = vjp((dos[h], g_rs[h]))
                dk_ref[rows, cols] += dk_c
                dv_ref[rows, cols] += dv_c
                new_dq.append(dqs[h] + dq_c)
                new_g.append(g_in)
            return tuple(new_dq), tuple(new_g)

        init = (tuple(jnp.zeros((BLOCK, BLOCK), F32) for _ in range(nh)),
                tuple(jnp.zeros((BLOCK, 1), F32) for _ in range(nh)))
        dqs = lax.fori_loop(last, n + 1, up, init)[0]
        for p in range(SB_PAIRS):
            dq_ref[:, p * BLOCK:(p + 1) * BLOCK] = jnp.where(low, dqs[2 * p], dqs[2 * p + 1])

    full = jax.ShapeDtypeStruct((s, pairs * BLOCK), F32)
    return _pcall(
        body, name=name, grid=(pairs // SB_PAIRS, nb),
        in_specs=[pl.BlockSpec((BLOCK, wide), lambda g, n: (n, qcb // SB_PAIRS + g)),
                  pl.BlockSpec((s, wide), lambda g, n: (0, kcb // SB_PAIRS + g), pipeline_mode=pl.Buffered(1)),
                  pl.BlockSpec((s, wide), lambda g, n: (0, vcb // SB_PAIRS + g), pipeline_mode=pl.Buffered(1)),
                  pl.BlockSpec((BLOCK, wide), lambda g, n: (n, docb // SB_PAIRS + g))],
        out_specs=[pl.BlockSpec((BLOCK, wide), lambda g, n: (n, g)),
                   pl.BlockSpec((s, wide), lambda g, n: (0, g), pipeline_mode=pl.Buffered(1)),
                   pl.BlockSpec((s, wide), lambda g, n: (0, g), pipeline_mode=pl.Buffered(1))],
        out_shape=[full, full, full],
        scratch_shapes=[pltpu.VMEM((nh, nb, BLOCK, 1), F32)],
        compiler_params=_params(),
    )(qkv, qkv, qkv, do)


def _xa_tile(dot, q, kv, qg, kg):
    hd = q.shape[1] // X_HEADS
    outs = []
    for h in range(X_HEADS):
        qh = _rms(q[:, h * hd:(h + 1) * hd], qg)
        kh = _rms(kv[:, h * hd:(h + 1) * hd], kg)
        vh = kv[:, (X_HEADS + h) * hd:(X_HEADS + h + 1) * hd]
        sc = dot(qh, kh, True) * (hd ** -0.5)
        m = lax.stop_gradient(jnp.max(sc, axis=-1, keepdims=True))
        p = jnp.exp(sc - m)
        outs.append(dot(p / jnp.sum(p, axis=-1, keepdims=True), vh, False))
    return jnp.concatenate(outs, axis=1)


def xa_core_fwd(q, kv, qg, kg, name):
    n, d = q.shape
    tm = _tile(n, 256, 8)
    (o,) = tcall(lambda ids, qt, kvt, qgt, kgt: (_xa_tile(_plain_dot, qt, kvt, qgt, kgt),), (n // tm,),
                 [_row(q, tm), _full(kv), _full(qg), _full(kg)], [_row_out(n, d, BF16, tm)], name)
    return o


def xa_core_bwd(q, kv, qg, kg, do, name):
    n, d = q.shape
    tm = _tile(n, 256, 8)

    def fn(ids, qt, kvt, qgt, kgt, dot_):
        _, vjp = jax.vjp(functools.partial(_xa_tile, _dot_vjp), qt, kvt, qgt, kgt)
        return vjp(dot_.astype(F32))

    return tcall(fn, (n // tm,), [_row(q, tm), _full(kv), _full(qg), _full(kg), _row(do, tm)],
                 [_row_out(n, d, BF16, tm), _acc_out(kv.shape), _acc_out(qg.shape), _acc_out(kg.shape)], name)


def _ev_reorder(a):
    return jnp.concatenate([a[..., 0:512], a[..., 768:2304], a[..., 512:768]], axis=-1)


def _ev_restore(a):
    return jnp.concatenate([a[..., 0:512], a[..., 2048:2304], a[..., 512:2048]], axis=-1)


_EV_SEGS = ((0, 512, "q"), (512, 1536, "raw"), (2048, 128, "k"), (2176, 128, "raw"))
_A_CFG = dict(hkv=A_KV_HEADS, grp=A_Q_HEADS // A_KV_HEADS, max_dist=BLOCK - 1, step=1.0, slopes=_alibi(A_Q_HEADS),
              want_lse=False)
_A_COLS = (lambda r: 0, lambda r: 16, lambda r: 17)


def even_mixer_fwd(x, g, w_in, qg, kg, sinks, w_out, tag):
    h = rmsnorm_fwd(x, g, tag + "_norm")
    qkv = mm(h, w_in, "nn", tag + "_in")
    (ops,) = prep_fwd(qkv, qg, kg, _EV_SEGS, (1,), tag + "_prep")
    (o_a,) = banded_fwd(ops, 1, _A_COLS, sinks, _A_CFG, tag + "_swa")
    o_b = sb_fwd(ops, 4, 8, 12, tag + "_sb")
    o = jnp.concatenate([o_a, o_b], axis=1)
    y = mm(o, w_out, "nn", tag + "_out", res=x)
    return y, (x, h, qkv, ops, o)


def even_mixer_bwd(dy, saved, g, w_in, qg, kg, sinks, w_out, tag):
    x, h, qkv, ops, o = saved
    do = mm(dy, w_out, "nt", tag + "_do")
    d_wout = mm(o, dy, "tn", tag + "_dwout")
    dqa, dkp, dkc, dvp, dvc, dsinks = banded_bwd(ops, 1, _A_COLS, sinks, _A_CFG, [(do, lambda r: 0)], tag + "_dswa")
    dqb, dkb, dvb = sb_bwd(ops, 4, 8, 12, do, 4, tag + "_dsb")
    dqkv, dqg, dkg = prep_bwd(
        qkv, qg, kg, _EV_SEGS,
        [(dqa, 0, 1), (dqb, 0, 1), (dkb, 0, 1), (dvb, 0, 1), (dkc, 0, 1), (dkp, 1, 1), (dvc, 0, 1), (dvp, 1, 1)],
        lambda qa, qb, kb, vb, kc, kp, vc, vp: jnp.concatenate([qa, qb, kb, vb, kc + kp, vc + vp], axis=1),
        tag + "_dqkv")
    dh = mm(dqkv, w_in, "nt", tag + "_dh")
    d_win = mm(h, dqkv, "tn", tag + "_dwin")
    dx, dg = rmsnorm_bwd(x, g, dh, dy, tag + "_dnorm")
    return dx, dg, d_win, dqg, dkg, dsinks, d_wout


def _c_cfg(window, dil):
    return dict(hkv=C_HEADS, grp=1, max_dist=window // dil, step=float(dil), slopes=_alibi(C_HEADS), want_lse=True)


_C_COLS = (lambda r: 3 * r, lambda r: 3 * r + 1, lambda r: 3 * r + 2)
_OD_SEGS = ((0, 1024, "q"), (1024, 1024, "k"), (2048, 1024, "raw"))


def _combine(o1, o2, o3, l1, l2, l3):
    m = lax.stop_gradient(jnp.maximum(jnp.maximum(l1, l2), l3))
    e1, e2, e3 = jnp.exp(l1 - m), jnp.exp(l2 - m), jnp.exp(l3 - m)
    tot = e1 + e2 + e3
    return (e1 / tot) * o1 + (e2 / tot) * o2 + (e3 / tot) * o3


def odd_mixer_fwd(x, g, w_in, qg, kg, w_out, tag):
    n, d = x.shape
    h = rmsnorm_fwd(x, g, tag + "_norm")
    qkv = mm(h, w_in, "nn", tag + "_in")
    dils = [dil for _, dil in C_PATTERNS]
    ops = prep_fwd(qkv, qg, kg, _OD_SEGS, dils, tag + "_prep")
    os_, ls_ = [], []
    for (window, dil), ops_d in zip(C_PATTERNS, ops):
        o_p, l_p = banded_fwd(ops_d, dil, _C_COLS, None, _c_cfg(window, dil), f"{tag}_dil{dil}")
        os_.append(o_p)
        ls_.append(l_p)
    tm = BLOCK
    lay = lambda a, dil: _in(a, (tm // dil, a.shape[1]), lambda i: (i, 0))
    views = [lay(a, dil) for a, dil in zip(os_ + ls_, dils + dils)]

    def comb(ids, *t, scratch):
        return (_combine(*[_to_natural(scratch, a, dil) for a, dil in zip(t, dils + dils)]),)

    (o,) = tcall(comb, (n // tm,), views, [_row_out(n, d, BF16, tm)], tag + "_comb",
                 scratch=((d // BLOCK * tm, BLOCK), F32))
    y = mm(o, w_out, "nn", tag + "_out", res=x)
    return y, (x, h, qkv, ops, views, o)


def odd_mixer_bwd(dy, saved, g, w_in, qg, kg, w_out, tag):
    x, h, qkv, ops, views, o = saved
    n, d = x.shape
    do = mm(dy, w_out, "nt", tag + "_do")
    d_wout = mm(o, dy, "tn", tag + "_dwout")
    tm = BLOCK
    dils = [dil for _, dil in C_PATTERNS]

    def comb_bwd(ids, *t, scratch):
        _, vjp = jax.vjp(_combine, *[_to_natural(scratch, a, dil) for a, dil in zip(t[:6], dils + dils)])
        return tuple(_to_strided(scratch, c, dil) for c, dil in zip(vjp(t[6]), dils + dils))

    cts = tcall(comb_bwd, (n // tm,), views + [_row(do, tm)],
                [_out((n // dil, dil * d), F32, (tm // dil, dil * d), lambda i: (i, 0)) for dil in dils + dils],
                tag + "_dcomb", scratch=((d // BLOCK * tm, BLOCK), F32))
    dqs, dks, dvs = [], [], []
    for p, ((window, dil), ops_d) in enumerate(zip(C_PATTERNS, ops)):
        dq, dkp, dkc, dvp, dvc = banded_bwd(ops_d, dil, _C_COLS, None, _c_cfg(window, dil),
                                            [(cts[p], lambda r: r), (cts[3 + p], lambda r: r)], f"{tag}_ddil{dil}")
        dqs.append((dq, 0, dil))
        dks += [(dkc, 0, dil), (dkp, dil, dil)]
        dvs += [(dvc, 0, dil), (dvp, dil, dil)]

    def gather(*t):
        total = lambda parts: functools.reduce(lambda a, b: a + b, parts)
        return jnp.concatenate([total(t[0:3]), total(t[3:9]), total(t[9:15])], axis=1)

    dqkv, dqg, dkg = prep_bwd(qkv, qg, kg, _OD_SEGS, dqs + dks + dvs, gather, tag + "_dqkv")
    dh = mm(dqkv, w_in, "nt", tag + "_dh")
    d_win = mm(h, dqkv, "tn", tag + "_dwin")
    dx, dg = rmsnorm_bwd(x, g, dh, dy, tag + "_dnorm")
    return dx, dg, d_win, dqg, dkg, d_wout


def xa_fwd(x, mem, g, gm, w_q, w_kv, qg, kg, w_o, tag):
    h = rmsnorm_fwd(x, g, tag + "_norm")
    q = mm(h, w_q, "nn", tag + "_q")
    mn = rmsnorm_fwd(mem, gm, tag + "_mnorm")
    kv = mm(mn, w_kv, "nn", tag + "_kv")
    o = xa_core_fwd(q, kv, qg, kg, tag + "_core")
    y = mm(o, w_o, "nn", tag + "_o", res=x)
    return y, (x, h, q, mn, kv, o)


def xa_bwd(dy, saved, mem, g, gm, w_q, w_kv, qg, kg, w_o, tag):
    x, h, q, mn, kv, o = saved
    do = mm(dy, w_o, "nt", tag + "_do", out_dtype=BF16)
    d_wo = mm(o, dy, "tn", tag + "_dwo")
    dq, dkv, dqg, dkg = xa_core_bwd(q, kv, qg, kg, do, tag + "_dcore")
    dh = mm(dq, w_q, "nt", tag + "_dh")
    d_wq = mm(h, dq, "tn", tag + "_dwq")
    dx, dg = rmsnorm_bwd(x, g, dh, dy, tag + "_dnorm")
    dmn = mm(dkv, w_kv, "nt", tag + "_dmn")
    d_wkv = mm(mn, dkv, "tn", tag + "_dwkv")
    _, dgm = rmsnorm_bwd(mem, gm, dmn, None, tag + "_dmnorm")
    return dx, dg, dgm, d_wq, d_wkv, dqg, dkg, d_wo


def loss_head(y, target, name):
    n, d = y.shape
    tm = _tile(n, 512, 8)

    def fn(ids, yt, tt):
        e = yt - tt
        return e * (1.0 / d), jnp.sum(e * e, axis=0, keepdims=True)

    return tcall(fn, (n // tm,), [_row(y, tm), _row(target, tm)], [_row_out(n, d, F32, tm), _acc_out((1, d))], name)


_ANY = pl.BlockSpec(memory_space=pl.ANY)


def all_gather_blocks(blocks):
    nb = len(blocks)

    def body(*refs):
        x_refs, out_refs = refs[:nb], refs[nb:2 * nb]
        send_sems, recv_sems, local_sems = refs[2 * nb:]
        x, y, c = lax.axis_index("x"), lax.axis_index("y"), lax.axis_index("c")
        me, sibling = (x, y, c), (x, y, 1 - c)
        over_x, over_y, diagonal = (1 - x, y), (x, 1 - y), (1 - x, 1 - y)
        relay_of = ((1 - x) * (1 - c) + x * c, y * (1 - c) + (1 - y) * c)
        relay_to = (x * (1 - c) + (1 - x) * c, (1 - y) * (1 - c) + y * c)

        def copy(b, k, blk, to, own=False):
            px, py, pc = blk
            slot = out_refs[b].at[4 * px + 2 * py + pc]
            return pltpu.make_async_remote_copy(
                src_ref=x_refs[b] if own else slot, dst_ref=slot,
                send_sem=send_sems.at[7 * b + k], recv_sem=recv_sems.at[7 * b + k], device_id=to, device_id_type=MESH)

        mine = [pltpu.make_async_copy(x_refs[b], out_refs[b].at[4 * x + 2 * y + c], local_sems.at[b]) for b in range(nb)]
        for cp in mine:
            cp.start()
        sent = []
        for b in range(nb):
            sent += [copy(b, 0, me, sibling, own=True), copy(b, 1, me, (*over_x, c), own=True),
                     copy(b, 2, me, (*over_y, c), own=True)]
        for cp in sent:
            cp.start()
        for b in range(nb):
            copy(b, 1, (*over_x, c), me).wait_recv()
            copy(b, 2, (*over_y, c), me).wait_recv()
            later = [copy(b, 3, (*relay_of, c), (*relay_to, c)), copy(b, 4, (*over_x, c), sibling),
                     copy(b, 5, (*over_y, c), sibling)]
            for cp in later:
                cp.start()
            sent += later
        for b in range(nb):
            copy(b, 3, (*diagonal, c), me).wait_recv()
            fwd = copy(b, 6, (*diagonal, c), sibling)
            fwd.start()
            sent.append(fwd)
        for b in range(nb):
            copy(b, 0, sibling, me).wait_recv()
            for k, chip in ((4, over_x), (5, over_y), (6, diagonal)):
                copy(b, k, (*chip, 1 - c), me).wait_recv()
        for cp in sent:
            cp.wait_send()
        for cp in mine:
            cp.wait()

    return _pcall(
        body, name="weights_all_gather",
        in_specs=[_ANY] * nb, out_specs=[_ANY] * nb,
        out_shape=[jax.ShapeDtypeStruct((N_DEV,) + a.shape, a.dtype) for a in blocks],
        scratch_shapes=[pltpu.SemaphoreType.DMA((7 * nb,)), pltpu.SemaphoreType.DMA((7 * nb,)),
                        pltpu.SemaphoreType.DMA((nb,))],
    )(*blocks)


def pair_exchange(bufs):
    nb = len(bufs)

    def body(*refs):
        srcs, dsts = refs[:nb], refs[nb:2 * nb]
        send_sems, recv_sems = refs[2 * nb:]
        x, y, c = lax.axis_index("x"), lax.axis_index("y"), lax.axis_index("c")
        copies = []
        for b in range(nb):
            for j in range(4):
                cp = pltpu.make_async_remote_copy(
                    src_ref=srcs[b].at[2 * j + (1 - c)], dst_ref=dsts[b].at[j], send_sem=send_sems.at[4 * b + j],
                    recv_sem=recv_sems.at[4 * b + j], device_id=(x, y, 1 - c), device_id_type=MESH)
                cp.start()
                copies.append(cp)
        for cp in copies:
            cp.wait()

    return _pcall(
        body, name="grads_pair_exchange",
        in_specs=[_ANY] * nb, out_specs=[_ANY] * nb,
        out_shape=[jax.ShapeDtypeStruct((4,) + a.shape[1:], a.dtype) for a in bufs],
        scratch_shapes=[pltpu.SemaphoreType.DMA((4 * nb,)), pltpu.SemaphoreType.DMA((4 * nb,))],
    )(*bufs)


def pair_sum(g, got, c, out_dtype, name):
    r, w = g.shape[1:]
    tr = _tile(r, 512, 16)

    def body(c_ref, a_ref, b_ref, o_ref):
        o_ref[...] = (a_ref[...].astype(F32) + b_ref[...].astype(F32)).astype(o_ref.dtype)

    return _pcall(
        body, name=name,
        grid_spec=pltpu.PrefetchScalarGridSpec(
            num_scalar_prefetch=1, grid=(4, r // tr),
            in_specs=[pl.BlockSpec((None, tr, w), lambda j, i, c_ref: (2 * j + c_ref[0], i, 0)),
                      pl.BlockSpec((None, tr, w), lambda j, i, c_ref: (j, i, 0))],
            out_specs=pl.BlockSpec((None, tr, w), lambda j, i, c_ref: (j, i, 0))),
        out_shape=jax.ShapeDtypeStruct((4,) + g.shape[1:], out_dtype),
        compiler_params=_params(),
    )(c, g, got)


def chip_exchange(parts):
    nb = len(parts)

    def body(*refs):
        srcs, dsts = refs[:nb], refs[nb:2 * nb]
        send_sems, recv_sems, local_sems = refs[2 * nb:]
        x, y, c = lax.axis_index("x"), lax.axis_index("y"), lax.axis_index("c")
        my_chip = 2 * x + y
        copies = []
        for b in range(nb):
            mine = pltpu.make_async_copy(srcs[b].at[my_chip], dsts[b].at[my_chip], local_sems.at[b])
            mine.start()
            copies.append(mine)
            for k, (tx, ty) in enumerate([(1 - x, y), (x, 1 - y), (1 - x, 1 - y)]):
                cp = pltpu.make_async_remote_copy(
                    src_ref=srcs[b].at[2 * tx + ty], dst_ref=dsts[b].at[my_chip], send_sem=send_sems.at[3 * b + k],
                    recv_sem=recv_sems.at[3 * b + k], device_id=(tx, ty, c), device_id_type=MESH)
                cp.start()
                copies.append(cp)
        for cp in copies:
            cp.wait()

    return _pcall(
        body, name="grads_chip_exchange",
        in_specs=[_ANY] * nb, out_specs=[_ANY] * nb,
        out_shape=[jax.ShapeDtypeStruct(a.shape, a.dtype) for a in parts],
        scratch_shapes=[pltpu.SemaphoreType.DMA((3 * nb,)), pltpu.SemaphoreType.DMA((3 * nb,)),
                        pltpu.SemaphoreType.DMA((nb,))],
    )(*parts)


def chip_sum(parts, name):
    r, w = parts.shape[1:]
    tr = _tile(r, 512, 16)
    spec = lambda j: _in(parts, (None, tr, w), lambda i, j=j: (j, i, 0))

    def fn(ids, a, b, c_, d):
        a, b, c_, d = [t.astype(F32) for t in (a, b, c_, d)]
        return (((a + b) + c_) + d,)

    (out,) = tcall(fn, (r // tr,), [spec(j) for j in range(4)],
                   [_out((r, w), F32, (tr, w), lambda i: (i, 0))], name)
    return out


def adamw(w, g, m, v, name):
    shape = w.shape
    cols = shape[-1]
    rows = int(np.prod(shape[:-1]))
    w2, g2, m2, v2 = [a.reshape(rows, cols) for a in (w, g, m, v)]
    tr = _tile(rows, 256, 8) if rows % 8 == 0 else rows

    def fn(ids, wt, gt, mt, vt):
        m_new = ADAM_B1 * mt + (1.0 - ADAM_B1) * gt
        v_new = ADAM_B2 * vt + (1.0 - ADAM_B2) * (gt * gt)
        m_hat = m_new / (1.0 - ADAM_B1 ** ADAM_STEP)
        v_hat = v_new / (1.0 - ADAM_B2 ** ADAM_STEP)
        delta = -ADAM_LR * (m_hat / (jnp.sqrt(v_hat) + ADAM_EPS) + ADAM_WD * wt)
        return delta, m_new, v_new

    res = tcall(fn, (rows // tr,), [_row(a, tr) for a in (w2, g2, m2, v2)],
                [_row_out(rows, cols, F32, tr) for _ in range(3)], name)
    return [a.reshape(shape) for a in res]


_MATS = [("ffn1_w_gu", "col"), ("ffn1_w_down", "row"), ("ev_w_in", "col"), ("ev_w_out", "row"),
         ("od_w_in", "col"), ("od_w_out", "row"), ("xa_w_q", "row"), ("xa_w_kv", "col"), ("xa_w_o", "row"),
         ("ffn2_w_gu", "col"), ("ffn2_w_down", "row")]
_VECS = ["ffn1_norm", "mix_norm", "ev_q_gain", "ev_k_gain", "ev_sinks", "od_q_gain", "od_k_gain", "xa_norm",
         "xa_mem_norm", "xa_q_gain", "xa_k_gain", "ffn2_norm"]
_WEIGHTS = ["ffn1_norm", "ffn1_w_gu", "ffn1_w_down", "mix_norm", "ev_w_in", "ev_q_gain", "ev_k_gain", "ev_sinks",
            "ev_w_out", "od_w_in", "od_q_gain", "od_k_gain", "od_w_out", "xa_norm", "xa_mem_norm", "xa_w_q", "xa_w_kv",
            "xa_q_gain", "xa_k_gain", "xa_w_o", "ffn2_norm", "ffn2_w_gu", "ffn2_w_down"]


_GROUPS = [["ffn1_w_gu", "ffn2_w_gu"], ["ev_w_in"], ["od_w_in"], ["xa_w_kv"],
           ["ffn1_w_down", "ffn2_w_down", "ev_w_out", "od_w_out", "xa_w_q", "xa_w_o"]]
_AXIS = dict(_MATS)


def _gather_weights(shards):
    blocks = []
    for names in _GROUPS:
        rows = [shards[n].reshape(-1, shards[n].shape[-1]).astype(BF16) for n in names]
        blocks.append(rows[0] if len(rows) == 1 else jnp.concatenate(rows, axis=0))
    gathered = all_gather_blocks(blocks)
    full = {}
    for names, got in zip(_GROUPS, gathered):
        off = 0
        for n in names:
            l, a, b = shards[n].shape
            seg = got[:, off:off + l * a, :].reshape(N_DEV, l, a, b)
            off += l * a
            if _AXIS[n] == "row":
                full[n] = seg.transpose(1, 0, 2, 3).reshape(l, N_DEV * a, b)
            else:
                full[n] = seg.transpose(1, 2, 0, 3).reshape(l, a, N_DEV * b)
    return full


def _reduce_gradients(mats, vecs, c):
    bufs = []
    for names in _GROUPS:
        rows = []
        for n in names:
            gr = mats[n]
            l, a, b = gr.shape
            if _AXIS[n] == "row":
                rows.append(gr.reshape(l, N_DEV, a // N_DEV, b).transpose(1, 0, 2, 3).reshape(N_DEV, -1, b))
            else:
                rows.append(gr.reshape(l, a, N_DEV, b // N_DEV).transpose(2, 0, 1, 3).reshape(N_DEV, l * a, b // N_DEV))
        bufs.append((rows[0] if len(rows) == 1 else jnp.concatenate(rows, axis=1)).astype(BF16))
    vec = jnp.concatenate([vecs[n].reshape(-1) for n in _VECS])
    vec = jnp.pad(vec, (0, -vec.shape[0] % (16 * LANES)))
    bufs.append(jnp.broadcast_to(vec.reshape(1, -1, LANES), (N_DEV, vec.shape[0] // LANES, LANES)))
    got = pair_exchange(bufs)
    nm = len(_GROUPS)
    parts = [pair_sum(b, g, c, BF16 if i < nm else F32, f"grads_pair_sum{i}") for i, (b, g) in enumerate(zip(bufs, got))]
    sums = [chip_sum(p, f"grads_chip_sum{i}") for i, p in enumerate(chip_exchange(parts))]
    out = {}
    for names, tot in zip(_GROUPS, sums[:nm]):
        off = 0
        for n in names:
            l, a, b = mats[n].shape
            shape = (l, a // N_DEV, b) if _AXIS[n] == "row" else (l, a, b // N_DEV)
            out[n] = tot[off:off + shape[0] * shape[1]].reshape(shape)
            off += shape[0] * shape[1]
    flat, off = sums[nm].reshape(-1), 0
    for n in _VECS:
        out[n] = flat[off:off + vecs[n].size].reshape(vecs[n].shape)
        off += vecs[n].size
    return out


def _local_step(x, mem, target, w, full):
    depth = w["ffn1_norm"].shape[0]
    ev_in = _ev_reorder(full["ev_w_in"])
    row = lambda a, l: a[l:l + 1]
    saved = []
    for l in range(depth):
        t = f"l{l}"
        j = l // 2
        x, s1 = ffn_fwd(x, row(w["ffn1_norm"], l), full["ffn1_w_gu"][l], full["ffn1_w_down"][l], t + "_ffn1")
        if l % 2 == 0:
            x, s2 = even_mixer_fwd(x, row(w["mix_norm"], l), ev_in[j], row(w["ev_q_gain"], j), row(w["ev_k_gain"], j),
                                   row(w["ev_sinks"], j), full["ev_w_out"][j], t + "_ev")
        else:
            x, s2 = odd_mixer_fwd(x, row(w["mix_norm"], l), full["od_w_in"][j], row(w["od_q_gain"], j),
                                  row(w["od_k_gain"], j), full["od_w_out"][j], t + "_od")
        x, s3 = xa_fwd(x, mem, row(w["xa_norm"], l), row(w["xa_mem_norm"], l), full["xa_w_q"][l], full["xa_w_kv"][l],
                       row(w["xa_q_gain"], l), row(w["xa_k_gain"], l), full["xa_w_o"][l], t + "_xa")
        x, s4 = ffn_fwd(x, row(w["ffn2_norm"], l), full["ffn2_w_gu"][l], full["ffn2_w_down"][l], t + "_ffn2")
        saved.append((s1, s2, s3, s4))
    dx, sq = loss_head(x, target, "loss_head")
    loss = 0.5 * jnp.sum(sq) / x.shape[1]

    gm = {n: [None] * full[n].shape[0] for n, _ in _MATS}
    gv = {n: [None] * w[n].shape[0] for n in _VECS}
    for l in reversed(range(depth)):
        t = f"l{l}"
        j = l // 2
        s1, s2, s3, s4 = saved[l]
        dx, gv["ffn2_norm"][l], gm["ffn2_w_gu"][l], gm["ffn2_w_down"][l] = ffn_bwd(
            dx, s4, row(w["ffn2_norm"], l), full["ffn2_w_gu"][l], full["ffn2_w_down"][l], t + "_ffn2")
        (dx, gv["xa_norm"][l], gv["xa_mem_norm"][l], gm["xa_w_q"][l], gm["xa_w_kv"][l], gv["xa_q_gain"][l],
         gv["xa_k_gain"][l], gm["xa_w_o"][l]) = xa_bwd(
            dx, s3, mem, row(w["xa_norm"], l), row(w["xa_mem_norm"], l), full["xa_w_q"][l], full["xa_w_kv"][l],
            row(w["xa_q_gain"], l), row(w["xa_k_gain"], l), full["xa_w_o"][l], t + "_xa")
        if l % 2 == 0:
            (dx, gv["mix_norm"][l], d_win, gv["ev_q_gain"][j], gv["ev_k_gain"][j], gv["ev_sinks"][j],
             gm["ev_w_out"][j]) = even_mixer_bwd(
                dx, s2, row(w["mix_norm"], l), ev_in[j], row(w["ev_q_gain"], j), row(w["ev_k_gain"], j),
                row(w["ev_sinks"], j), full["ev_w_out"][j], t + "_ev")
            gm["ev_w_in"][j] = _ev_restore(d_win)
        else:
            (dx, gv["mix_norm"][l], gm["od_w_in"][j], gv["od_q_gain"][j], gv["od_k_gain"][j],
             gm["od_w_out"][j]) = odd_mixer_bwd(
                dx, s2, row(w["mix_norm"], l), full["od_w_in"][j], row(w["od_q_gain"], j), row(w["od_k_gain"], j),
                full["od_w_out"][j], t + "_od")
        dx, gv["ffn1_norm"][l], gm["ffn1_w_gu"][l], gm["ffn1_w_down"][l] = ffn_bwd(
            dx, s1, row(w["ffn1_norm"], l), full["ffn1_w_gu"][l], full["ffn1_w_down"][l], t + "_ffn1")
    mats = {n: jnp.stack(v) for n, v in gm.items()}
    vecs = {n: jnp.concatenate(v, axis=0) for n, v in gv.items()}
    return loss, dx, mats, vecs


def kernel(x, mem, ffn1_norm, ffn1_w_gu, ffn1_w_down, mix_norm, ev_w_in, ev_q_gain, ev_k_gain, ev_sinks, ev_w_out, od_w_in, od_q_gain, od_k_gain, od_w_out, xa_norm, xa_mem_norm, xa_w_q, xa_w_kv, xa_q_gain, xa_k_gain, xa_w_o, ffn2_norm, ffn2_w_gu, ffn2_w_down, loss_target, m_ffn1_norm, m_ffn1_w_gu, m_ffn1_w_down, m_mix_norm, m_ev_w_in, m_ev_q_gain, m_ev_k_gain, m_ev_sinks, m_ev_w_out, m_od_w_in, m_od_q_gain, m_od_k_gain, m_od_w_out, m_xa_norm, m_xa_mem_norm, m_xa_w_q, m_xa_w_kv, m_xa_q_gain, m_xa_k_gain, m_xa_w_o, m_ffn2_norm, m_ffn2_w_gu, m_ffn2_w_down, v_ffn1_norm, v_ffn1_w_gu, v_ffn1_w_down, v_mix_norm, v_ev_w_in, v_ev_q_gain, v_ev_k_gain, v_ev_sinks, v_ev_w_out, v_od_w_in, v_od_q_gain, v_od_k_gain, v_od_w_out, v_xa_norm, v_xa_mem_norm, v_xa_w_q, v_xa_w_kv, v_xa_q_gain, v_xa_k_gain, v_xa_w_o, v_ffn2_norm, v_ffn2_w_gu, v_ffn2_w_down):
    w = dict(ffn1_norm=ffn1_norm, ffn1_w_gu=ffn1_w_gu, ffn1_w_down=ffn1_w_down, mix_norm=mix_norm, ev_w_in=ev_w_in, ev_q_gain=ev_q_gain, ev_k_gain=ev_k_gain, ev_sinks=ev_sinks, ev_w_out=ev_w_out, od_w_in=od_w_in, od_q_gain=od_q_gain, od_k_gain=od_k_gain, od_w_out=od_w_out, xa_norm=xa_norm, xa_mem_norm=xa_mem_norm, xa_w_q=xa_w_q, xa_w_kv=xa_w_kv, xa_q_gain=xa_q_gain, xa_k_gain=xa_k_gain, xa_w_o=xa_w_o, ffn2_norm=ffn2_norm, ffn2_w_gu=ffn2_w_gu, ffn2_w_down=ffn2_w_down)
    m = dict(ffn1_norm=m_ffn1_norm, ffn1_w_gu=m_ffn1_w_gu, ffn1_w_down=m_ffn1_w_down, mix_norm=m_mix_norm, ev_w_in=m_ev_w_in, ev_q_gain=m_ev_q_gain, ev_k_gain=m_ev_k_gain, ev_sinks=m_ev_sinks, ev_w_out=m_ev_w_out, od_w_in=m_od_w_in, od_q_gain=m_od_q_gain, od_k_gain=m_od_k_gain, od_w_out=m_od_w_out, xa_norm=m_xa_norm, xa_mem_norm=m_xa_mem_norm, xa_w_q=m_xa_w_q, xa_w_kv=m_xa_w_kv, xa_q_gain=m_xa_q_gain, xa_k_gain=m_xa_k_gain, xa_w_o=m_xa_w_o, ffn2_norm=m_ffn2_norm, ffn2_w_gu=m_ffn2_w_gu, ffn2_w_down=m_ffn2_w_down)
    v = dict(ffn1_norm=v_ffn1_norm, ffn1_w_gu=v_ffn1_w_gu, ffn1_w_down=v_ffn1_w_down, mix_norm=v_mix_norm, ev_w_in=v_ev_w_in, ev_q_gain=v_ev_q_gain, ev_k_gain=v_ev_k_gain, ev_sinks=v_ev_sinks, ev_w_out=v_ev_w_out, od_w_in=v_od_w_in, od_q_gain=v_od_q_gain, od_k_gain=v_od_k_gain, od_w_out=v_od_w_out, xa_norm=v_xa_norm, xa_mem_norm=v_xa_mem_norm, xa_w_q=v_xa_w_q, xa_w_kv=v_xa_w_kv, xa_q_gain=v_xa_q_gain, xa_k_gain=v_xa_k_gain, xa_w_o=v_xa_w_o, ffn2_norm=v_ffn2_norm, ffn2_w_gu=v_ffn2_w_gu, ffn2_w_down=v_ffn2_w_down)

    full = _gather_weights(w)
    loss, dx, mats, vecs = _local_step(x[0], mem[0], loss_target[0], w, full)
    c = lax.axis_index("c").astype(jnp.int32).reshape(1)
    grads = _reduce_gradients(mats, vecs, c)
    loss = lax.psum(loss, ("x", "y", "c"))

    delta, new_m, new_v = {}, {}, {}
    for n in _WEIGHTS:
        delta[n], new_m[n], new_v[n] = adamw(w[n], grads[n], m[n], v[n], "adamw_" + n)
    return (loss, dx[None], *[grads[n] for n in _WEIGHTS], *[delta[n] for n in _WEIGHTS],
            *[new_m[n] for n in _WEIGHTS], *[new_v[n] for n in _WEIGHTS])
```

```python
import functools

import numpy as np
import jax
import jax.numpy as jnp
from jax import lax
from jax.experimental import pallas as pl
from jax.experimental.pallas import tpu as pltpu

F32 = jnp.float32
BF16 = jnp.bfloat16
MESH = pl.DeviceIdType.MESH

HEAD_DIM = 64
BLOCK = 128
RMS_EPS = 1e-6
A_Q_HEADS, A_KV_HEADS = 8, 2
B_HEADS = 8
C_HEADS = 16
C_PATTERNS = ((128, 1), (512, 4), (2048, 16))
X_HEADS = 4
N_DEV = 8
LANES = 1024
VMEM_LIMIT_BYTES = 56 * 1024 * 1024
SB_SKIP_LOG = -110.0
NEG_BIG = -1e30

ADAM_LR, ADAM_B1, ADAM_B2, ADAM_EPS, ADAM_WD, ADAM_STEP = 0.001, 0.9, 0.999, 1e-08, 0.01, 10

NN = (((1,), (0,)), ((), ()))
NT = (((1,), (1,)), ((), ()))
TN = (((0,), (0,)), ((), ()))


class Side:
    def __init__(self, arrays, out_shapes, n_remote, n_local, plan, aliased=False):
        self.arrays, self.out_shapes, self.plan, self.aliased = list(arrays), list(out_shapes), plan, aliased
        self.sems = [pltpu.SemaphoreType.DMA((n_remote,)), pltpu.SemaphoreType.DMA((n_remote,)),
                     pltpu.SemaphoreType.DMA((max(n_local, 1),))]

    def start(self, ins, outs, sems):
        local, sends, _ = self.plan(ins, outs, *sems)
        for make in local + sends:
            make().start()

    def wait(self, ins, outs, sems):
        local, sends, recvs = self.plan(ins, outs, *sems)
        for make in sends:
            make().wait_send()
        for make in recvs:
            make().wait_recv()
        for make in local:
            make().wait()


def _pcall(body, side=None, **kw):
    if side is None:
        return pl.pallas_call(body, **kw)
    grid = kw["grid"]
    single = not isinstance(kw["out_specs"], (list, tuple))
    out_specs = [kw["out_specs"]] if single else list(kw["out_specs"])
    out_shape = [kw["out_shape"]] if single else list(kw["out_shape"])
    scratch = list(kw.get("scratch_shapes", []))
    n_in, n_out, n_scr, n_side = len(kw["in_specs"]), len(out_specs), len(scratch), len(side.arrays)
    n_sout = len(side.out_shapes)

    def hosted(*refs):
        ins, s_in = refs[:n_in], refs[n_in:n_in + n_side]
        outs = refs[n_in + n_side:n_in + n_side + n_out]
        s_out = refs[n_in + n_side + n_out:n_in + n_side + n_out + n_sout]
        rest = refs[n_in + n_side + n_out + n_sout:]
        scr, sems = rest[:n_scr], rest[n_scr:]
        first = last = None
        for a, size in enumerate(grid):
            f, l = pl.program_id(a) == 0, pl.program_id(a) == size - 1
            first = f if first is None else jnp.logical_and(first, f)
            last = l if last is None else jnp.logical_and(last, l)

        @pl.when(first)
        def _():
            side.start(s_in, s_out, sems)

        body(*ins, *outs, *scr)

        @pl.when(last)
        def _():
            side.wait(s_in, s_out, sems)

    any_space = pl.BlockSpec(memory_space=pl.ANY)
    kw2 = dict(kw)
    kw2.update(in_specs=list(kw["in_specs"]) + [any_space] * n_side, out_specs=out_specs + [any_space] * n_sout,
               out_shape=out_shape + side.out_shapes, scratch_shapes=scratch + side.sems)
    if side.aliased:
        kw2["input_output_aliases"] = {n_in + i: n_out + i for i in range(n_side)}
    call = pl.pallas_call(hosted, **kw2)

    def run(*args):
        res = call(*args, *side.arrays)
        return (res[0] if single else list(res[:n_out])), list(res[n_out:])

    return run


def _params(**kw):
    return pltpu.CompilerParams(vmem_limit_bytes=VMEM_LIMIT_BYTES, **kw)


def _tile(dim, cap, unit=128):
    if dim <= cap:
        return dim
    t = (cap // unit) * unit
    while t >= unit:
        if dim % t == 0:
            return t
        t -= unit
    raise ValueError(f"no tile for {dim} under {cap}")


def _dot(a, b, dims):
    return lax.dot_general(a.astype(BF16), b.astype(BF16), dims, preferred_element_type=F32)


@functools.partial(jax.custom_vjp, nondiff_argnums=(2,))
def _dot_vjp(a, b, nt):
    return _dot(a, b, NT if nt else NN)


def _dot_vjp_fwd(a, b, nt):
    return _dot(a, b, NT if nt else NN), (a.astype(BF16), b.astype(BF16))


def _dot_vjp_bwd(nt, res, g):
    a, b = res
    if nt:
        return _dot(g, b, NN), _dot(g, a, TN)
    return _dot(g, b, NT), _dot(a, g, TN)


_dot_vjp.defvjp(_dot_vjp_fwd, _dot_vjp_bwd)


def _plain_dot(a, b, nt):
    return _dot(a, b, NT if nt else NN)


def _split_dot(x, mat, terms=2):
    out, rem = None, x
    for t in range(terms):
        part = rem.astype(BF16)
        d = lax.dot_general(part, mat, NN, preferred_element_type=F32)
        out = d if out is None else out + d
        if t + 1 < terms:
            rem = rem - part.astype(F32)
    return out


@functools.partial(jax.custom_vjp, nondiff_argnums=(3,))
def _split_dot_vjp(x, mat, mat_t, terms):
    return _split_dot(x, mat, terms)


def _split_dot_vjp_fwd(x, mat, mat_t, terms):
    return _split_dot(x, mat, terms), mat_t


def _split_dot_vjp_bwd(terms, mat_t, g):
    return _split_dot(g, mat_t, terms), None, None


_split_dot_vjp.defvjp(_split_dot_vjp_fwd, _split_dot_vjp_bwd)


def _plain_split(x, mat, mat_t, terms):
    return _split_dot(x, mat, terms)


def _tri(after):
    j = lax.broadcasted_iota(jnp.int32, (BLOCK, BLOCK), 0)
    s = lax.broadcasted_iota(jnp.int32, (BLOCK, BLOCK), 1)
    return jnp.where(j > s if after else j < s, 1.0, 0.0).astype(BF16)


def _in(a, block, imap):
    return (a, block, imap)


def _out(shape, dtype, block, imap, acc=False):
    return (shape, dtype, block, imap, acc)


def tcall(fn, grid, ins, outs, name, scratch=None):
    nin = len(ins)
    nout = len(outs)
    ngrid = len(grid)

    def body(*refs):
        ids = tuple(pl.program_id(a) for a in range(ngrid))
        extra = {} if scratch is None else {"scratch": refs[nin + nout]}
        res = fn(ids, *[r[...] for r in refs[:nin]], **extra)
        first = ids[0] == 0
        for a in range(1, ngrid):
            first = jnp.logical_and(first, ids[a] == 0)
        for o_ref, r, spec in zip(refs[nin:nin + nout], res, outs):
            if spec[4]:
                @pl.when(first)
                def _(o_ref=o_ref):
                    o_ref[...] = jnp.zeros(o_ref.shape, o_ref.dtype)
                o_ref[...] += r.astype(o_ref.dtype)
            else:
                o_ref[...] = r.astype(o_ref.dtype)

    return _pcall(
        body, name=name, grid=grid,
        in_specs=[pl.BlockSpec(b, m) for (_, b, m) in ins],
        out_specs=[pl.BlockSpec(b, m) for (_, _, b, m, _) in outs],
        out_shape=[jax.ShapeDtypeStruct(s, d) for (s, d, _, _, _) in outs],
        scratch_shapes=[] if scratch is None else [pltpu.VMEM(*scratch)],
        compiler_params=_params(),
    )(*[a for (a, _, _) in ins])


def _to_strided(scr, nat, d):
    if d == 1:
        return nat
    t, w = nat.shape
    nc = w // BLOCK
    for c in range(nc):
        scr[c * t:(c + 1) * t, :] = nat[:, c * BLOCK:(c + 1) * BLOCK]
    return jnp.concatenate([scr[pl.ds(c * t + r, t // d, stride=d), :] for r in range(d) for c in range(nc)], axis=1)


def _to_natural(scr, st, d):
    if d == 1:
        return st.astype(F32)
    t, w = st.shape[0] * d, st.shape[1] // d
    nc = w // BLOCK
    st = st.astype(F32)
    for r in range(d):
        for c in range(nc):
            scr[pl.ds(c * t + r, t // d, stride=d), :] = st[:, r * w + c * BLOCK:r * w + (c + 1) * BLOCK]
    return jnp.concatenate([scr[c * t:(c + 1) * t, :] for c in range(nc)], axis=1)


def _row(a, tm, width=None, cb=0):
    width = a.shape[1] if width is None else width
    return _in(a, (tm, width), lambda i, cb=cb: (i, cb))


def _full(a):
    zeros = (0,) * a.ndim
    return _in(a, a.shape, lambda *ids: zeros)


def _row_out(n, width, dtype, tm):
    return _out((n, width), dtype, (tm, width), lambda i: (i, 0))


def _acc_out(shape):
    zeros = (0,) * len(shape)
    return _out(shape, F32, shape, lambda *ids: zeros, acc=True)


def mm(a, b, mode, name, *, out_dtype=F32, scale=1.0, res=None, side=None):
    if mode == "nn":
        (m, k), (k2, n) = a.shape, b.shape
    elif mode == "nt":
        (m, k), (n, k2) = a.shape, b.shape
    else:
        (k, m), (k2, n) = a.shape, b.shape
    assert k == k2, (a.shape, b.shape, mode)
    tm, tn, tk = _tile(m, 512), _tile(n, 1408), _tile(k, 1408)
    nk = k // tk
    dims = {"nn": NN, "nt": NT, "tn": TN}[mode]
    has_res = res is not None

    def body(*refs):
        if has_res:
            a_ref, b_ref, r_ref, o_ref, acc_ref = refs
        else:
            a_ref, b_ref, o_ref, acc_ref = refs
        kk = pl.program_id(2)

        @pl.when(kk == 0)
        def _():
            acc_ref[...] = jnp.zeros(acc_ref.shape, F32)

        acc_ref[...] += _dot(a_ref[...], b_ref[...], dims)

        @pl.when(kk == nk - 1)
        def _():
            out = acc_ref[...]
            if scale != 1.0:
                out = out * scale
            if has_res:
                out = out + r_ref[...]
            o_ref[...] = out.astype(o_ref.dtype)

    a_spec = (pl.BlockSpec((tk, tm), lambda i, j, kk: (kk, i)) if mode == "tn"
              else pl.BlockSpec((tm, tk), lambda i, j, kk: (i, kk)))
    b_spec = (pl.BlockSpec((tn, tk), lambda i, j, kk: (j, kk)) if mode == "nt"
              else pl.BlockSpec((tk, tn), lambda i, j, kk: (kk, j)))
    in_specs = [a_spec, b_spec]
    args = [a, b]
    if has_res:
        in_specs.append(pl.BlockSpec((tm, tn), lambda i, j, kk: (i, j)))
        args.append(res)
    order = ("parallel", "parallel", "arbitrary") if side is None else ("arbitrary",) * 3
    return _pcall(
        body, side=side, name=name, grid=(m // tm, n // tn, nk),
        in_specs=in_specs,
        out_specs=pl.BlockSpec((tm, tn), lambda i, j, kk: (i, j)),
        out_shape=jax.ShapeDtypeStruct((m, n), out_dtype),
        scratch_shapes=[pltpu.VMEM((tm, tn), F32)],
        compiler_params=_params(dimension_semantics=order),
    )(*args)


def _rms(x, g):
    return x * lax.rsqrt(jnp.mean(x * x, axis=-1, keepdims=True) + RMS_EPS) * g


def _silu_mul(gate, up):
    return gate / (1.0 + jnp.exp(-gate)) * up


def _indicator(shape, head_axis, mod):
    lane = lax.broadcasted_iota(jnp.int32, shape, head_axis)
    other = lax.broadcasted_iota(jnp.int32, shape, 1 - head_axis)
    lane = jnp.bitwise_and(lane, HEAD_DIM - 1) if mod else jnp.right_shift(lane, 6)
    return jnp.where(lane == other, 1.0, 0.0).astype(BF16)


def _head_rms(split, xs, g):
    w = xs.shape[1]
    to_head, from_head = _indicator((w, BLOCK), 0, False), _indicator((BLOCK, w), 1, False)
    to_lane, from_lane = _indicator((HEAD_DIM, w), 1, True), _indicator((w, HEAD_DIM), 0, True)
    ss = split(xs * xs, to_head, from_head, 3)
    r = lax.rsqrt(ss * (1.0 / HEAD_DIM) + RMS_EPS)
    g_all = split(jnp.broadcast_to(g, (8, HEAD_DIM)), to_lane, from_lane, 3)[0:1]
    return xs * split(r, from_head, to_head, 3) * g_all


def _prep(split, x, qg, kg, segs):
    parts = []
    for start, width, kind in segs:
        xs = x[:, start:start + width]
        parts.append(xs if kind == "raw" else _head_rms(split, xs, qg if kind == "q" else kg))
    return jnp.concatenate(parts, axis=1)


def prep_fwd(x, qg, kg, segs, dils, name):
    n, w = x.shape
    tm = _tile(n, 256, 8)

    def fn(ids, xt, a, b, scratch):
        ops = _prep(_plain_split, xt, a, b, segs)
        return tuple(_to_strided(scratch, ops, d) for d in dils)

    return tcall(fn, (n // tm,), [_row(x, tm), _full(qg), _full(kg)],
                 [_out((n // d, d * w), BF16, (tm // d, d * w), lambda i: (i, 0)) for d in dils], name,
                 scratch=((w // BLOCK * tm, BLOCK), F32))


def prep_bwd(x, qg, kg, segs, grads, gather, name):
    n, w = x.shape
    tm = BLOCK
    nblk = n // tm

    def fn(ids, xt, a, b, *t, scratch):
        t = [_to_natural(scratch, ti, d) for ti, (_, _, d) in zip(t, grads)]
        t = [jnp.where(ids[0] + sh < nblk, ti, 0.0) if sh else ti for ti, (_, sh, _) in zip(t, grads)]
        _, vjp = jax.vjp(lambda x_, a_, b_: _prep(_split_dot_vjp, x_, a_, b_, segs), xt, a, b)
        return vjp(gather(*t))

    specs = [_in(a, (tm // d, a.shape[1]), (lambda i, sh=sh: (jnp.minimum(i + sh, nblk - 1), 0))) for a, sh, d in grads]
    wmax = max(a.shape[1] // d for a, _, d in grads)
    return tcall(fn, (nblk,), [_row(x, tm), _full(qg), _full(kg)] + specs,
                 [_row_out(n, w, BF16, tm), _acc_out(qg.shape), _acc_out(kg.shape)], name,
                 scratch=((wmax // BLOCK * tm, BLOCK), F32))


def rmsnorm_fwd(x, g, name):
    n, d = x.shape
    tm = _tile(n, 512, 8)
    (h,) = tcall(lambda ids, xt, gt: (_rms(xt, gt),), (n // tm,), [_row(x, tm), _full(g)],
                 [_row_out(n, d, BF16, tm)], name)
    return h


def rmsnorm_bwd(x, g, dh, dres, name):
    n, d = x.shape
    tm = _tile(n, 256, 8)

    def fn(ids, xt, gt, dht, *rest):
        _, vjp = jax.vjp(_rms, xt, gt)
        dx, dg = vjp(dht.astype(F32))
        if rest:
            dx = dx + rest[0]
        return dx, dg

    ins = [_row(x, tm), _full(g), _row(dh, tm)] + ([_row(dres, tm)] if dres is not None else [])
    return tcall(fn, (n // tm,), ins, [_row_out(n, d, F32, tm), _acc_out(g.shape)], name)


def ffn_fwd(x, g, w_gu, w_down, tag, relay=None):
    n = x.shape[0]
    f = w_down.shape[0]
    h = rmsnorm_fwd(x, g, tag + "_norm")
    gu = mm(h, w_gu, "nn", tag + "_gu", side=None if relay is None else gather_side(1, relay))
    if relay is not None:
        gu, relay = gu
    tm = _tile(n, 128, 8)
    (a,) = tcall(lambda ids, gt, ut: (_silu_mul(gt, ut),), (n // tm,),
                 [_row(gu, tm, f, 0), _row(gu, tm, f, 1)], [_row_out(n, f, BF16, tm)], tag + "_act")
    y = mm(a, w_down, "nn", tag + "_down", scale=0.5, res=x, side=None if relay is None else gather_side(2, relay))
    if relay is not None:
        y, relay = y
        return y, (x, h, gu, a), relay
    return y, (x, h, gu, a)


def ffn_bwd(dy, saved, g, w_gu, w_down, tag, pair=None):
    x, h, gu, a = saved
    n = x.shape[0]
    f = w_down.shape[0]
    da = mm(dy, w_down, "nt", tag + "_da", scale=0.5)
    d_wdown = mm(a, dy, "tn", tag + "_dwd", scale=0.5, side=None if pair is None else pair_side(pair))
    got = None
    if pair is not None:
        d_wdown, got = d_wdown
    tm = _tile(n, 128, 8)

    def act_bwd(ids, gt, ut, dat):
        _, vjp = jax.vjp(_silu_mul, gt, ut)
        dg, du = vjp(dat)
        return (jnp.concatenate([dg, du], axis=1),)

    (dgu,) = tcall(act_bwd, (n // tm,), [_row(gu, tm, f, 0), _row(gu, tm, f, 1), _row(da, tm)],
                   [_row_out(n, 2 * f, BF16, tm)], tag + "_dact")
    dh = mm(dgu, w_gu, "nt", tag + "_dh")
    d_wgu = mm(h, dgu, "tn", tag + "_dwgu")
    dx, dg = rmsnorm_bwd(x, g, dh, dy, tag + "_dnorm")
    if pair is not None:
        return dx, dg, d_wgu, d_wdown, got
    return dx, dg, d_wgu, d_wdown


def _alibi(n_heads):
    return [float(s) for s in np.asarray(2.0 ** (-8.0 * np.arange(1, n_heads + 1) / n_heads), dtype=np.float32)]


def _banded_tile(dot, first, q, kp, kc, vp, vc, sinks, *, hkv, grp, max_dist, step, slopes, want_lse):
    row = lax.broadcasted_iota(jnp.int32, (BLOCK, 2 * BLOCK), 0)
    col = lax.broadcasted_iota(jnp.int32, (BLOCK, 2 * BLOCK), 1)
    dist = row + BLOCK - col
    valid = (dist >= 0) & (dist <= max_dist) & ((col >= BLOCK) | jnp.logical_not(first))
    distf = dist.astype(F32)

    def head(hd, qh, k2, v2):
        s = dot(qh, k2, True) * (HEAD_DIM ** -0.5)
        s = jnp.where(valid, s - (slopes[hd] * step) * distf, NEG_BIG)
        m = jnp.max(s, axis=-1, keepdims=True)
        if sinks is not None:
            pick = lax.broadcasted_iota(jnp.int32, sinks.shape, 1) == hd
            sk = jnp.sum(jnp.where(pick, sinks, 0.0), axis=1, keepdims=True)
            m = jnp.maximum(m, sk)
        m = lax.stop_gradient(m)
        p = jnp.exp(s - m)
        denom = jnp.sum(p, axis=-1, keepdims=True)
        if sinks is not None:
            denom = denom + jnp.exp(sk - m)
        return dot(p / denom, v2, False), m + jnp.log(denom)

    outs, lses = [], []
    if grp == 1:
        low = lax.broadcasted_iota(jnp.int32, (BLOCK, BLOCK), 1) < HEAD_DIM
        for pr in range(hkv // 2):
            sl = slice(pr * BLOCK, (pr + 1) * BLOCK)
            q2 = q[:, sl]
            k2 = jnp.concatenate([kp[:, sl], kc[:, sl]], axis=0)
            v2 = jnp.concatenate([vp[:, sl], vc[:, sl]], axis=0)
            o0, l0 = head(2 * pr, jnp.where(low, q2, 0.0), k2, v2)
            o1, l1 = head(2 * pr + 1, jnp.where(low, 0.0, q2), k2, v2)
            outs.append(jnp.where(low, o0, o1))
            lses.append(jnp.where(low, l0, l1))
    else:
        for hk in range(hkv):
            sl = slice(hk * HEAD_DIM, (hk + 1) * HEAD_DIM)
            k2 = jnp.concatenate([kp[:, sl], kc[:, sl]], axis=0)
            v2 = jnp.concatenate([vp[:, sl], vc[:, sl]], axis=0)
            for gi in range(grp):
                hd = hk * grp + gi
                o_h, l_h = head(hd, q[:, hd * HEAD_DIM:(hd + 1) * HEAD_DIM], k2, v2)
                outs.append(o_h)
                lses.append(jnp.broadcast_to(l_h, (BLOCK, HEAD_DIM)))
    o = jnp.concatenate(outs, axis=1)
    if want_lse:
        return o, jnp.concatenate(lses, axis=1)
    return (o,)


def _banded_specs(view, qcol, kcol, vcol, wq, wkv):
    def at(colfn, prev):
        if prev:
            return lambda r, n: (jnp.maximum(n - 1, 0), colfn(r))
        return lambda r, n: (n, colfn(r))
    return [
        _in(view, (BLOCK, wq), at(qcol, False)),
        _in(view, (BLOCK, wkv), at(kcol, True)),
        _in(view, (BLOCK, wkv), at(kcol, False)),
        _in(view, (BLOCK, wkv), at(vcol, True)),
        _in(view, (BLOCK, wkv), at(vcol, False)),
    ]


def banded_fwd(view, dil, cols, sinks, cfg, name):
    ns = view.shape[0]
    nb = ns // BLOCK
    wq, wkv = cfg["hkv"] * cfg["grp"] * HEAD_DIM, cfg["hkv"] * HEAD_DIM
    has_sinks = sinks is not None

    def fn(ids, q, kp, kc, vp, vc, *rest):
        q, kp, kc, vp, vc = [a.astype(F32) for a in (q, kp, kc, vp, vc)]
        return _banded_tile(_plain_dot, ids[1] == 0, q, kp, kc, vp, vc, rest[0] if has_sinks else None, **cfg)

    ins = _banded_specs(view, *cols, wq, wkv) + ([_full(sinks)] if has_sinks else [])
    outs = [_out((ns, dil * wq), F32 if cfg["want_lse"] else BF16, (BLOCK, wq), lambda r, n: (n, r))]
    if cfg["want_lse"]:
        outs.append(_out((ns, dil * wq), F32, (BLOCK, wq), lambda r, n: (n, r)))
    return tcall(fn, (dil, nb), ins, outs, name)


def banded_bwd(view, dil, cols, sinks, cfg, cts, name):
    ns = view.shape[0]
    nb = ns // BLOCK
    wq, wkv = cfg["hkv"] * cfg["grp"] * HEAD_DIM, cfg["hkv"] * HEAD_DIM
    has_sinks = sinks is not None
    assert len(cts) == (2 if cfg["want_lse"] else 1)

    def fn(ids, q, kp, kc, vp, vc, *rest):
        sk = rest[0] if has_sinks else None
        ct = rest[1 if has_sinks else 0:]
        first = ids[1] == 0

        def f(q, kp, kc, vp, vc, *s):
            return _banded_tile(_dot_vjp, first, q, kp, kc, vp, vc, s[0] if has_sinks else None, **cfg)

        prim = tuple(a.astype(F32) for a in (q, kp, kc, vp, vc)) + ((sk,) if has_sinks else ())
        _, vjp = jax.vjp(f, *prim)
        return vjp(tuple(c.astype(F32) for c in ct))

    ins = (_banded_specs(view, *cols, wq, wkv) + ([_full(sinks)] if has_sinks else [])
           + [_in(a, (BLOCK, wq), (lambda r, n, cf=cf: (n, cf(r)))) for (a, cf) in cts])
    blk = lambda w: _out((ns, dil * w), F32, (BLOCK, w), lambda r, n: (n, r))
    outs = [blk(wq), blk(wkv), blk(wkv), blk(wkv), blk(wkv)]
    if has_sinks:
        outs.append(_acc_out(sinks.shape))
    return tcall(fn, (dil, nb), ins, outs, name)


def _log_sigmoid(z):
    return jnp.minimum(z, 0.0) - jnp.log(1.0 + jnp.exp(-jnp.abs(z)))


SB_PAIRS = 4


def _sb_pair(dot, suffix, qh, kb, vb, r_in, mask):
    z = dot(qh, kb, True) * (HEAD_DIM ** -0.5)
    lsp = _log_sigmoid(z)
    log_keep = jnp.where(mask, lsp - z, 0.0)
    log_after = suffix(log_keep) + r_in
    a = jnp.where(mask, jnp.exp(lsp + log_after), 0.0)
    return dot(a, vb, False), r_in + jnp.sum(log_keep, axis=1, keepdims=True)


def sb_fwd(qkv, qcb, kcb, vcb, name, side=None):
    s = qkv.shape[0]
    nb = s // BLOCK
    pairs = B_HEADS // 2
    wide = SB_PAIRS * BLOCK
    nh = 2 * SB_PAIRS
    assert pairs % SB_PAIRS == 0 and qcb % SB_PAIRS == 0 and kcb % SB_PAIRS == 0 and vcb % SB_PAIRS == 0

    def body(q_ref, k_ref, v_ref, o_ref):
        n = pl.program_id(1)
        low = lax.broadcasted_iota(jnp.int32, (BLOCK, BLOCK), 1) < HEAD_DIM
        before = (lax.broadcasted_iota(jnp.int32, (BLOCK, BLOCK), 1)
                  < lax.broadcasted_iota(jnp.int32, (BLOCK, BLOCK), 0))
        after = _tri(True)
        suffix = lambda t: _split_dot(t, after)
        qs = []
        for p in range(SB_PAIRS):
            q2 = q_ref[:, p * BLOCK:(p + 1) * BLOCK].astype(F32)
            qs += [jnp.where(low, q2, 0.0), jnp.where(low, 0.0, q2)]

        def cond(c):
            return jnp.logical_and(c[0] >= 0, c[1] > SB_SKIP_LOG)

        def step(c):
            kb, _, rs, accs = c
            rows = pl.ds(pl.multiple_of(kb * BLOCK, BLOCK), BLOCK)
            mask = jnp.logical_or(before, kb != n)
            new_r, new_acc, top = [], [], None
            for h in range(nh):
                cols = slice((h // 2) * BLOCK, (h // 2 + 1) * BLOCK)
                o_part, r_out = _sb_pair(_plain_dot, suffix, qs[h], k_ref[rows, cols], v_ref[rows, cols], rs[h], mask)
                new_r.append(r_out)
                new_acc.append(accs[h] + o_part)
                top = jnp.max(r_out) if top is None else jnp.maximum(top, jnp.max(r_out))
            return kb - 1, top, tuple(new_r), tuple(new_acc)

        init = (n, jnp.float32(0.0), tuple(jnp.zeros((BLOCK, 1), F32) for _ in range(nh)),
                tuple(jnp.zeros((BLOCK, BLOCK), F32) for _ in range(nh)))
        accs = lax.while_loop(cond, step, init)[3]
        for p in range(SB_PAIRS):
            o_ref[:, p * BLOCK:(p + 1) * BLOCK] = jnp.where(low, accs[2 * p], accs[2 * p + 1]).astype(o_ref.dtype)

    return _pcall(
        body, side=side, name=name, grid=(pairs // SB_PAIRS, nb),
        in_specs=[pl.BlockSpec((BLOCK, wide), lambda g, n: (n, qcb // SB_PAIRS + g)),
                  pl.BlockSpec((s, wide), lambda g, n: (0, kcb // SB_PAIRS + g), pipeline_mode=pl.Buffered(1)),
                  pl.BlockSpec((s, wide), lambda g, n: (0, vcb // SB_PAIRS + g), pipeline_mode=pl.Buffered(1))],
        out_specs=pl.BlockSpec((BLOCK, wide), lambda g, n: (n, g)),
        out_shape=jax.ShapeDtypeStruct((s, pairs * BLOCK), BF16),
        compiler_params=_params(),
    )(qkv, qkv, qkv)


def sb_bwd(qkv, qcb, kcb, vcb, do, docb, name, side=None):
    s = qkv.shape[0]
    nb = s // BLOCK
    pairs = B_HEADS // 2
    wide = SB_PAIRS * BLOCK
    nh = 2 * SB_PAIRS
    assert docb % SB_PAIRS == 0

    def body(q_ref, k_ref, v_ref, do_ref, dq_ref, dk_ref, dv_ref, r_ref):
        n = pl.program_id(1)

        @pl.when(n == 0)
        def _():
            dk_ref[...] = jnp.zeros(dk_ref.shape, F32)
            dv_ref[...] = jnp.zeros(dv_ref.shape, F32)

        low = lax.broadcasted_iota(jnp.int32, (BLOCK, BLOCK), 1) < HEAD_DIM
        before = (lax.broadcasted_iota(jnp.int32, (BLOCK, BLOCK), 1)
                  < lax.broadcasted_iota(jnp.int32, (BLOCK, BLOCK), 0))
        after, earlier = _tri(True), _tri(False)
        suffix = lambda t: _split_dot_vjp(t, after, earlier, 2)
        qs, dos = [], []
        for p in range(SB_PAIRS):
            q2 = q_ref[:, p * BLOCK:(p + 1) * BLOCK].astype(F32)
            do2 = do_ref[:, p * BLOCK:(p + 1) * BLOCK].astype(F32)
            qs += [jnp.where(low, q2, 0.0), jnp.where(low, 0.0, q2)]
            dos += [jnp.where(low, do2, 0.0), jnp.where(low, 0.0, do2)]

        def cond(c):
            return jnp.logical_and(c[0] >= 0, c[1] > SB_SKIP_LOG)

        def down(c):
            kb, _, rs = c
            rows = pl.ds(pl.multiple_of(kb * BLOCK, BLOCK), BLOCK)
            mask = jnp.logical_or(before, kb != n)
            new_r, top = [], None
            for h in range(nh):
                cols = slice((h // 2) * BLOCK, (h // 2 + 1) * BLOCK)
                r_ref[h, kb] = rs[h]
                z = _dot(qs[h], k_ref[rows, cols], NT) * (HEAD_DIM ** -0.5)
                log_keep = jnp.where(mask, _log_sigmoid(z) - z, 0.0)
                r_out = rs[h] + jnp.sum(log_keep, axis=1, keepdims=True)
                new_r.append(r_out)
                top = jnp.max(r_out) if top is None else jnp.maximum(top, jnp.max(r_out))
            return kb - 1, top, tuple(new_r)

        init = (n, jnp.float32(0.0), tuple(jnp.zeros((BLOCK, 1), F32) for _ in range(nh)))
        last = lax.while_loop(cond, down, init)[0] + 1

        def up(kb, c):
            dqs, g_rs = c
            rows = pl.ds(pl.multiple_of(kb * BLOCK, BLOCK), BLOCK)
            mask = jnp.logical_or(before, kb != n)
            new_dq, new_g = [], []
            for h in range(nh):
                cols = slice((h // 2) * BLOCK, (h // 2 + 1) * BLOCK)
                _, vjp = jax.vjp(lambda q_, k_, v_, r_: _sb_pair(_dot_vjp, suffix, q_, k_, v_, r_, mask),
                                 qs[h], k_ref[rows, cols].astype(F32), v_ref[rows, cols].astype(F32), r_ref[h, kb])
                dq_c, dk_c, dv_c, g_in = vjp((dos[h], g_rs[h]))
                dk_ref[rows, cols] += dk_c
                dv_ref[rows, cols] += dv_c
                new_dq.append(dqs[h] + dq_c)
                new_g.append(g_in)
            return tuple(new_dq), tuple(new_g)

        init = (tuple(jnp.zeros((BLOCK, BLOCK), F32) for _ in range(nh)),
                tuple(jnp.zeros((BLOCK, 1), F32) for _ in range(nh)))
        dqs = lax.fori_loop(last, n + 1, up, init)[0]
        for p in range(SB_PAIRS):
            dq_ref[:, p * BLOCK:(p + 1) * BLOCK] = jnp.where(low, dqs[2 * p], dqs[2 * p + 1])

    full = jax.ShapeDtypeStruct((s, pairs * BLOCK), F32)
    return _pcall(
        body, side=side, name=name, grid=(pairs // SB_PAIRS, nb),
        in_specs=[pl.BlockSpec((BLOCK, wide), lambda g, n: (n, qcb // SB_PAIRS + g)),
                  pl.BlockSpec((s, wide), lambda g, n: (0, kcb // SB_PAIRS + g), pipeline_mode=pl.Buffered(1)),
                  pl.BlockSpec((s, wide), lambda g, n: (0, vcb // SB_PAIRS + g), pipeline_mode=pl.Buffered(1)),
                  pl.BlockSpec((BLOCK, wide), lambda g, n: (n, docb // SB_PAIRS + g))],
        out_specs=[pl.BlockSpec((BLOCK, wide), lambda g, n: (n, g)),
                   pl.BlockSpec((s, wide), lambda g, n: (0, g), pipeline_mode=pl.Buffered(1)),
                   pl.BlockSpec((s, wide), lambda g, n: (0, g), pipeline_mode=pl.Buffered(1))],
        out_shape=[full, full, full],
        scratch_shapes=[pltpu.VMEM((nh, nb, BLOCK, 1), F32)],
        compiler_params=_params(),
    )(qkv, qkv, qkv, do)


def _xa_tile(dot, q, kv, qg, kg):
    hd = q.shape[1] // X_HEADS
    outs = []
    for h in range(X_HEADS):
        qh = _rms(q[:, h * hd:(h + 1) * hd], qg)
        kh = _rms(kv[:, h * hd:(h + 1) * hd], kg)
        vh = kv[:, (X_HEADS + h) * hd:(X_HEADS + h + 1) * hd]
        sc = dot(qh, kh, True) * (hd ** -0.5)
        m = lax.stop_gradient(jnp.max(sc, axis=-1, keepdims=True))
        p = jnp.exp(sc - m)
        outs.append(dot(p / jnp.sum(p, axis=-1, keepdims=True), vh, False))
    return jnp.concatenate(outs, axis=1)


def xa_core_fwd(q, kv, qg, kg, name):
    n, d = q.shape
    tm = _tile(n, 256, 8)
    (o,) = tcall(lambda ids, qt, kvt, qgt, kgt: (_xa_tile(_plain_dot, qt, kvt, qgt, kgt),), (n // tm,),
                 [_row(q, tm), _full(kv), _full(qg), _full(kg)], [_row_out(n, d, BF16, tm)], name)
    return o


def xa_core_bwd(q, kv, qg, kg, do, name):
    n, d = q.shape
    tm = _tile(n, 256, 8)

    def fn(ids, qt, kvt, qgt, kgt, dot_):
        _, vjp = jax.vjp(functools.partial(_xa_tile, _dot_vjp), qt, kvt, qgt, kgt)
        return vjp(dot_.astype(F32))

    return tcall(fn, (n // tm,), [_row(q, tm), _full(kv), _full(qg), _full(kg), _row(do, tm)],
                 [_row_out(n, d, BF16, tm), _acc_out(kv.shape), _acc_out(qg.shape), _acc_out(kg.shape)], name)


def _ev_reorder(a):
    return jnp.concatenate([a[..., 0:512], a[..., 768:2304], a[..., 512:768]], axis=-1)


def _ev_restore(a):
    return jnp.concatenate([a[..., 0:512], a[..., 2048:2304], a[..., 512:2048]], axis=-1)


_EV_SEGS = ((0, 512, "q"), (512, 1536, "raw"), (2048, 128, "k"), (2176, 128, "raw"))
_A_CFG = dict(hkv=A_KV_HEADS, grp=A_Q_HEADS // A_KV_HEADS, max_dist=BLOCK - 1, step=1.0, slopes=_alibi(A_Q_HEADS),
              want_lse=False)
_A_COLS = (lambda r: 0, lambda r: 16, lambda r: 17)


def even_mixer_fwd(x, g, w_in, qg, kg, sinks, w_out, tag, side=None):
    h = rmsnorm_fwd(x, g, tag + "_norm")
    qkv = mm(h, w_in, "nn", tag + "_in")
    (ops,) = prep_fwd(qkv, qg, kg, _EV_SEGS, (1,), tag + "_prep")
    (o_a,) = banded_fwd(ops, 1, _A_COLS, sinks, _A_CFG, tag + "_swa")
    o_b = sb_fwd(ops, 4, 8, 12, tag + "_sb", side=side)
    carried = None
    if side is not None:
        o_b, carried = o_b
    o = jnp.concatenate([o_a, o_b], axis=1)
    y = mm(o, w_out, "nn", tag + "_out", res=x)
    return y, (x, h, qkv, ops, o), carried


def even_mixer_bwd(dy, saved, g, w_in, qg, kg, sinks, w_out, tag, side=None):
    x, h, qkv, ops, o = saved
    do = mm(dy, w_out, "nt", tag + "_do")
    d_wout = mm(o, dy, "tn", tag + "_dwout")
    dqa, dkp, dkc, dvp, dvc, dsinks = banded_bwd(ops, 1, _A_COLS, sinks, _A_CFG, [(do, lambda r: 0)], tag + "_dswa")
    res = sb_bwd(ops, 4, 8, 12, do, 4, tag + "_dsb", side=side)
    carried = None
    if side is not None:
        res, carried = res
    dqb, dkb, dvb = res
    dqkv, dqg, dkg = prep_bwd(
        qkv, qg, kg, _EV_SEGS,
        [(dqa, 0, 1), (dqb, 0, 1), (dkb, 0, 1), (dvb, 0, 1), (dkc, 0, 1), (dkp, 1, 1), (dvc, 0, 1), (dvp, 1, 1)],
        lambda qa, qb, kb, vb, kc, kp, vc, vp: jnp.concatenate([qa, qb, kb, vb, kc + kp, vc + vp], axis=1),
        tag + "_dqkv")
    dh = mm(dqkv, w_in, "nt", tag + "_dh")
    d_win = mm(h, dqkv, "tn", tag + "_dwin")
    dx, dg = rmsnorm_bwd(x, g, dh, dy, tag + "_dnorm")
    return dx, dg, d_win, dqg, dkg, dsinks, d_wout, carried


def _c_cfg(window, dil):
    return dict(hkv=C_HEADS, grp=1, max_dist=window // dil, step=float(dil), slopes=_alibi(C_HEADS), want_lse=True)


_C_COLS = (lambda r: 3 * r, lambda r: 3 * r + 1, lambda r: 3 * r + 2)
_OD_SEGS = ((0, 1024, "q"), (1024, 1024, "k"), (2048, 1024, "raw"))


def _combine(o1, o2, o3, l1, l2, l3):
    m = lax.stop_gradient(jnp.maximum(jnp.maximum(l1, l2), l3))
    e1, e2, e3 = jnp.exp(l1 - m), jnp.exp(l2 - m), jnp.exp(l3 - m)
    tot = e1 + e2 + e3
    return (e1 / tot) * o1 + (e2 / tot) * o2 + (e3 / tot) * o3


def odd_mixer_fwd(x, g, w_in, qg, kg, w_out, tag):
    n, d = x.shape
    h = rmsnorm_fwd(x, g, tag + "_norm")
    qkv = mm(h, w_in, "nn", tag + "_in")
    dils = [dil for _, dil in C_PATTERNS]
    ops = prep_fwd(qkv, qg, kg, _OD_SEGS, dils, tag + "_prep")
    os_, ls_ = [], []
    for (window, dil), ops_d in zip(C_PATTERNS, ops):
        o_p, l_p = banded_fwd(ops_d, dil, _C_COLS, None, _c_cfg(window, dil), f"{tag}_dil{dil}")
        os_.append(o_p)
        ls_.append(l_p)
    tm = BLOCK
    lay = lambda a, dil: _in(a, (tm // dil, a.shape[1]), lambda i: (i, 0))
    views = [lay(a, dil) for a, dil in zip(os_ + ls_, dils + dils)]

    def comb(ids, *t, scratch):
        return (_combine(*[_to_natural(scratch, a, dil) for a, dil in zip(t, dils + dils)]),)

    (o,) = tcall(comb, (n // tm,), views, [_row_out(n, d, BF16, tm)], tag + "_comb",
                 scratch=((d // BLOCK * tm, BLOCK), F32))
    y = mm(o, w_out, "nn", tag + "_out", res=x)
    return y, (x, h, qkv, ops, views, o)


def odd_mixer_bwd(dy, saved, g, w_in, qg, kg, w_out, tag):
    x, h, qkv, ops, views, o = saved
    n, d = x.shape
    do = mm(dy, w_out, "nt", tag + "_do")
    d_wout = mm(o, dy, "tn", tag + "_dwout")
    tm = BLOCK
    dils = [dil for _, dil in C_PATTERNS]

    def comb_bwd(ids, *t, scratch):
        _, vjp = jax.vjp(_combine, *[_to_natural(scratch, a, dil) for a, dil in zip(t[:6], dils + dils)])
        return tuple(_to_strided(scratch, c, dil) for c, dil in zip(vjp(t[6]), dils + dils))

    cts = tcall(comb_bwd, (n // tm,), views + [_row(do, tm)],
                [_out((n // dil, dil * d), F32, (tm // dil, dil * d), lambda i: (i, 0)) for dil in dils + dils],
                tag + "_dcomb", scratch=((d // BLOCK * tm, BLOCK), F32))
    dqs, dks, dvs = [], [], []
    for p, ((window, dil), ops_d) in enumerate(zip(C_PATTERNS, ops)):
        dq, dkp, dkc, dvp, dvc = banded_bwd(ops_d, dil, _C_COLS, None, _c_cfg(window, dil),
                                            [(cts[p], lambda r: r), (cts[3 + p], lambda r: r)], f"{tag}_ddil{dil}")
        dqs.append((dq, 0, dil))
        dks += [(dkc, 0, dil), (dkp, dil, dil)]
        dvs += [(dvc, 0, dil), (dvp, dil, dil)]

    def gather(*t):
        total = lambda parts: functools.reduce(lambda a, b: a + b, parts)
        return jnp.concatenate([total(t[0:3]), total(t[3:9]), total(t[9:15])], axis=1)

    dqkv, dqg, dkg = prep_bwd(qkv, qg, kg, _OD_SEGS, dqs + dks + dvs, gather, tag + "_dqkv")
    dh = mm(dqkv, w_in, "nt", tag + "_dh")
    d_win = mm(h, dqkv, "tn", tag + "_dwin")
    dx, dg = rmsnorm_bwd(x, g, dh, dy, tag + "_dnorm")
    return dx, dg, d_win, dqg, dkg, d_wout


def xa_fwd(x, mem, g, gm, w_q, w_kv, qg, kg, w_o, tag):
    h = rmsnorm_fwd(x, g, tag + "_norm")
    q = mm(h, w_q, "nn", tag + "_q")
    mn = rmsnorm_fwd(mem, gm, tag + "_mnorm")
    kv = mm(mn, w_kv, "nn", tag + "_kv")
    o = xa_core_fwd(q, kv, qg, kg, tag + "_core")
    y = mm(o, w_o, "nn", tag + "_o", res=x)
    return y, (x, h, q, mn, kv, o)


def xa_bwd(dy, saved, mem, g, gm, w_q, w_kv, qg, kg, w_o, tag):
    x, h, q, mn, kv, o = saved
    do = mm(dy, w_o, "nt", tag + "_do", out_dtype=BF16)
    d_wo = mm(o, dy, "tn", tag + "_dwo")
    dq, dkv, dqg, dkg = xa_core_bwd(q, kv, qg, kg, do, tag + "_dcore")
    dh = mm(dq, w_q, "nt", tag + "_dh")
    d_wq = mm(h, dq, "tn", tag + "_dwq")
    dx, dg = rmsnorm_bwd(x, g, dh, dy, tag + "_dnorm")
    dmn = mm(dkv, w_kv, "nt", tag + "_dmn")
    d_wkv = mm(mn, dkv, "tn", tag + "_dwkv")
    _, dgm = rmsnorm_bwd(mem, gm, dmn, None, tag + "_dmnorm")
    return dx, dg, dgm, d_wq, d_wkv, dqg, dkg, d_wo


def loss_head(y, target, name):
    n, d = y.shape
    tm = _tile(n, 512, 8)

    def fn(ids, yt, tt):
        e = yt - tt
        return e * (1.0 / d), jnp.sum(e * e, axis=0, keepdims=True)

    return tcall(fn, (n // tm,), [_row(y, tm), _row(target, tm)], [_row_out(n, d, F32, tm), _acc_out((1, d))], name)


_ANY = pl.BlockSpec(memory_space=pl.ANY)


def all_gather_blocks(blocks):
    nb = len(blocks)

    def body(*refs):
        x_refs, out_refs = refs[:nb], refs[nb:2 * nb]
        send_sems, recv_sems, local_sems = refs[2 * nb:]
        x, y, c = lax.axis_index("x"), lax.axis_index("y"), lax.axis_index("c")
        me, sibling = (x, y, c), (x, y, 1 - c)
        over_x, over_y, diagonal = (1 - x, y), (x, 1 - y), (1 - x, 1 - y)
        relay_of = ((1 - x) * (1 - c) + x * c, y * (1 - c) + (1 - y) * c)
        relay_to = (x * (1 - c) + (1 - x) * c, (1 - y) * (1 - c) + y * c)

        def copy(b, k, blk, to, own=False):
            px, py, pc = blk
            slot = out_refs[b].at[4 * px + 2 * py + pc]
            return pltpu.make_async_remote_copy(
                src_ref=x_refs[b] if own else slot, dst_ref=slot,
                send_sem=send_sems.at[7 * b + k], recv_sem=recv_sems.at[7 * b + k], device_id=to, device_id_type=MESH)

        mine = [pltpu.make_async_copy(x_refs[b], out_refs[b].at[4 * x + 2 * y + c], local_sems.at[b]) for b in range(nb)]
        for cp in mine:
            cp.start()
        sent = []
        for b in range(nb):
            sent += [copy(b, 0, me, sibling, own=True), copy(b, 1, me, (*over_x, c), own=True),
                     copy(b, 2, me, (*over_y, c), own=True)]
        for cp in sent:
            cp.start()
        for b in range(nb):
            copy(b, 1, (*over_x, c), me).wait_recv()
            copy(b, 2, (*over_y, c), me).wait_recv()
            later = [copy(b, 3, (*relay_of, c), (*relay_to, c)), copy(b, 4, (*over_x, c), sibling),
                     copy(b, 5, (*over_y, c), sibling)]
            for cp in later:
                cp.start()
            sent += later
        for b in range(nb):
            copy(b, 3, (*diagonal, c), me).wait_recv()
            fwd = copy(b, 6, (*diagonal, c), sibling)
            fwd.start()
            sent.append(fwd)
        for b in range(nb):
            copy(b, 0, sibling, me).wait_recv()
            for k, chip in ((4, over_x), (5, over_y), (6, diagonal)):
                copy(b, k, (*chip, 1 - c), me).wait_recv()
        for cp in sent:
            cp.wait_send()
        for cp in mine:
            cp.wait()

    return _pcall(
        body, name="weights_all_gather",
        in_specs=[_ANY] * nb, out_specs=[_ANY] * nb,
        out_shape=[jax.ShapeDtypeStruct((N_DEV,) + a.shape, a.dtype) for a in blocks],
        scratch_shapes=[pltpu.SemaphoreType.DMA((7 * nb,)), pltpu.SemaphoreType.DMA((7 * nb,)),
                        pltpu.SemaphoreType.DMA((nb,))],
    )(*blocks)


def pair_exchange(bufs):
    nb = len(bufs)

    def body(*refs):
        srcs, dsts = refs[:nb], refs[nb:2 * nb]
        send_sems, recv_sems = refs[2 * nb:]
        x, y, c = lax.axis_index("x"), lax.axis_index("y"), lax.axis_index("c")
        copies = []
        for b in range(nb):
            for j in range(4):
                cp = pltpu.make_async_remote_copy(
                    src_ref=srcs[b].at[2 * j + (1 - c)], dst_ref=dsts[b].at[j], send_sem=send_sems.at[4 * b + j],
                    recv_sem=recv_sems.at[4 * b + j], device_id=(x, y, 1 - c), device_id_type=MESH)
                cp.start()
                copies.append(cp)
        for cp in copies:
            cp.wait()

    return _pcall(
        body, name="grads_pair_exchange",
        in_specs=[_ANY] * nb, out_specs=[_ANY] * nb,
        out_shape=[jax.ShapeDtypeStruct((4,) + a.shape[1:], a.dtype) for a in bufs],
        scratch_shapes=[pltpu.SemaphoreType.DMA((4 * nb,)), pltpu.SemaphoreType.DMA((4 * nb,))],
    )(*bufs)


def pair_sum(g, got, c, out_dtype, name):
    r, w = g.shape[1:]
    tr = _tile(r, 512, 16)

    def body(c_ref, a_ref, b_ref, o_ref):
        o_ref[...] = (a_ref[...].astype(F32) + b_ref[...].astype(F32)).astype(o_ref.dtype)

    return _pcall(
        body, name=name,
        grid_spec=pltpu.PrefetchScalarGridSpec(
            num_scalar_prefetch=1, grid=(4, r // tr),
            in_specs=[pl.BlockSpec((None, tr, w), lambda j, i, c_ref: (2 * j + c_ref[0], i, 0)),
                      pl.BlockSpec((None, tr, w), lambda j, i, c_ref: (j, i, 0))],
            out_specs=pl.BlockSpec((None, tr, w), lambda j, i, c_ref: (j, i, 0))),
        out_shape=jax.ShapeDtypeStruct((4,) + g.shape[1:], out_dtype),
        compiler_params=_params(),
    )(c, g, got)


def chip_exchange(parts):
    nb = len(parts)

    def body(*refs):
        srcs, dsts = refs[:nb], refs[nb:2 * nb]
        send_sems, recv_sems, local_sems = refs[2 * nb:]
        x, y, c = lax.axis_index("x"), lax.axis_index("y"), lax.axis_index("c")
        my_chip = 2 * x + y
        copies = []
        for b in range(nb):
            mine = pltpu.make_async_copy(srcs[b].at[my_chip], dsts[b].at[my_chip], local_sems.at[b])
            mine.start()
            copies.append(mine)
            for k, (tx, ty) in enumerate([(1 - x, y), (x, 1 - y), (1 - x, 1 - y)]):
                cp = pltpu.make_async_remote_copy(
                    src_ref=srcs[b].at[2 * tx + ty], dst_ref=dsts[b].at[my_chip], send_sem=send_sems.at[3 * b + k],
                    recv_sem=recv_sems.at[3 * b + k], device_id=(tx, ty, c), device_id_type=MESH)
                cp.start()
                copies.append(cp)
        for cp in copies:
            cp.wait()

    return _pcall(
        body, name="grads_chip_exchange",
        in_specs=[_ANY] * nb, out_specs=[_ANY] * nb,
        out_shape=[jax.ShapeDtypeStruct(a.shape, a.dtype) for a in parts],
        scratch_shapes=[pltpu.SemaphoreType.DMA((3 * nb,)), pltpu.SemaphoreType.DMA((3 * nb,)),
                        pltpu.SemaphoreType.DMA((nb,))],
    )(*parts)


def chip_sum(parts, name):
    r, w = parts.shape[1:]
    tr = _tile(r, 512, 16)
    spec = lambda j: _in(parts, (None, tr, w), lambda i, j=j: (j, i, 0))

    def fn(ids, a, b, c_, d):
        a, b, c_, d = [t.astype(F32) for t in (a, b, c_, d)]
        return (((a + b) + c_) + d,)

    (out,) = tcall(fn, (r // tr,), [spec(j) for j in range(4)],
                   [_out((r, w), F32, (tr, w), lambda i: (i, 0))], name)
    return out


def _remote(src, dst, send_sems, recv_sems, k, to):
    return functools.partial(pltpu.make_async_remote_copy, src_ref=src, dst_ref=dst, send_sem=send_sems.at[k],
                             recv_sem=recv_sems.at[k], device_id=to, device_id_type=MESH)


def _gather_plan(phase, nb):
    def plan(ins, outs, send_sems, recv_sems, local_sems):
        x, y, c = lax.axis_index("x"), lax.axis_index("y"), lax.axis_index("c")
        me, sibling = (x, y, c), (x, y, 1 - c)
        over_x, over_y, diagonal = (1 - x, y), (x, 1 - y), (1 - x, 1 - y)
        relay_of = ((1 - x) * (1 - c) + x * c, y * (1 - c) + (1 - y) * c)
        relay_to = (x * (1 - c) + (1 - x) * c, (1 - y) * (1 - c) + y * c)
        local, sends, recvs = [], [], []
        for b in range(nb):
            slot = lambda chip, core, b=b: outs[b].at[4 * chip[0] + 2 * chip[1] + core]
            if phase == 0:
                local.append(functools.partial(pltpu.make_async_copy, ins[b], slot((x, y), c), local_sems.at[b]))
                moves = [(ins[b], slot((x, y), c), to) for to in (sibling, (*over_x, c), (*over_y, c))]
                arrive = [slot((x, y), 1 - c), slot(over_x, c), slot(over_y, c)]
            elif phase == 1:
                moves = [(slot(relay_of, c), slot(relay_of, c), (*relay_to, c)),
                         (slot(over_x, c), slot(over_x, c), sibling), (slot(over_y, c), slot(over_y, c), sibling)]
                arrive = [slot(diagonal, c), slot(over_x, 1 - c), slot(over_y, 1 - c)]
            else:
                moves = [(slot(diagonal, c), slot(diagonal, c), sibling)]
                arrive = [slot(diagonal, 1 - c)]
            sends += [_remote(src, dst, send_sems, recv_sems, 3 * b + k, to) for k, (src, dst, to) in enumerate(moves)]
            recvs += [_remote(dst, dst, send_sems, recv_sems, 3 * b + k, me) for k, dst in enumerate(arrive)]
        return local, sends, recvs
    return plan


def gather_side(phase, arrays):
    nb = len(arrays)
    if phase == 0:
        shapes = [jax.ShapeDtypeStruct((N_DEV,) + a.shape, a.dtype) for a in arrays]
        return Side(arrays, shapes, 3 * nb, nb, _gather_plan(0, nb))
    shapes = [jax.ShapeDtypeStruct(a.shape, a.dtype) for a in arrays]
    return Side(arrays, shapes, 3 * nb, 0, _gather_plan(phase, nb), aliased=True)


def pair_side(bufs):
    nb = len(bufs)

    def plan(ins, outs, send_sems, recv_sems, local_sems):
        x, y, c = lax.axis_index("x"), lax.axis_index("y"), lax.axis_index("c")
        sends = [_remote(ins[b].at[2 * j + (1 - c)], outs[b].at[j], send_sems, recv_sems, 4 * b + j, (x, y, 1 - c))
                 for b in range(nb) for j in range(4)]
        recvs = [_remote(outs[b].at[j], outs[b].at[j], send_sems, recv_sems, 4 * b + j, (x, y, c))
                 for b in range(nb) for j in range(4)]
        return [], sends, recvs

    shapes = [jax.ShapeDtypeStruct((4,) + a.shape[1:], a.dtype) for a in bufs]
    return Side(bufs, shapes, 4 * nb, 0, plan)


def chip_side(parts):
    nb = len(parts)

    def plan(ins, outs, send_sems, recv_sems, local_sems):
        x, y, c = lax.axis_index("x"), lax.axis_index("y"), lax.axis_index("c")
        my_chip = 2 * x + y
        peers = [(1 - x, y), (x, 1 - y), (1 - x, 1 - y)]
        local = [functools.partial(pltpu.make_async_copy, ins[b].at[my_chip], outs[b].at[my_chip], local_sems.at[b])
                 for b in range(nb)]
        sends = [_remote(ins[b].at[2 * tx + ty], outs[b].at[my_chip], send_sems, recv_sems, 3 * b + k, (tx, ty, c))
                 for b in range(nb) for k, (tx, ty) in enumerate(peers)]
        recvs = [_remote(outs[b].at[2 * tx + ty], outs[b].at[2 * tx + ty], send_sems, recv_sems, 3 * b + k, (x, y, c))
                 for b in range(nb) for k, (tx, ty) in enumerate(peers)]
        return local, sends, recvs

    shapes = [jax.ShapeDtypeStruct(a.shape, a.dtype) for a in parts]
    return Side(parts, shapes, 3 * nb, nb, plan)


def adamw(w, g, m, v, name):
    shape = w.shape
    cols = shape[-1]
    rows = int(np.prod(shape[:-1]))
    w2, g2, m2, v2 = [a.reshape(rows, cols) for a in (w, g, m, v)]
    tr = _tile(rows, 256, 8) if rows % 8 == 0 else rows

    def fn(ids, wt, gt, mt, vt):
        m_new = ADAM_B1 * mt + (1.0 - ADAM_B1) * gt
        v_new = ADAM_B2 * vt + (1.0 - ADAM_B2) * (gt * gt)
        m_hat = m_new / (1.0 - ADAM_B1 ** ADAM_STEP)
        v_hat = v_new / (1.0 - ADAM_B2 ** ADAM_STEP)
        delta = -ADAM_LR * (m_hat / (jnp.sqrt(v_hat) + ADAM_EPS) + ADAM_WD * wt)
        return delta, m_new, v_new

    res = tcall(fn, (rows // tr,), [_row(a, tr) for a in (w2, g2, m2, v2)],
                [_row_out(rows, cols, F32, tr) for _ in range(3)], name)
    return [a.reshape(shape) for a in res]


_MATS = [("ffn1_w_gu", "col"), ("ffn1_w_down", "row"), ("ev_w_in", "col"), ("ev_w_out", "row"),
         ("od_w_in", "col"), ("od_w_out", "row"), ("xa_w_q", "row"), ("xa_w_kv", "col"), ("xa_w_o", "row"),
         ("ffn2_w_gu", "col"), ("ffn2_w_down", "row")]
_VECS = ["ffn1_norm", "mix_norm", "ev_q_gain", "ev_k_gain", "ev_sinks", "od_q_gain", "od_k_gain", "xa_norm",
         "xa_mem_norm", "xa_q_gain", "xa_k_gain", "ffn2_norm"]
_WEIGHTS = ["ffn1_norm", "ffn1_w_gu", "ffn1_w_down", "mix_norm", "ev_w_in", "ev_q_gain", "ev_k_gain", "ev_sinks",
            "ev_w_out", "od_w_in", "od_q_gain", "od_k_gain", "od_w_out", "xa_norm", "xa_mem_norm", "xa_w_q", "xa_w_kv",
            "xa_q_gain", "xa_k_gain", "xa_w_o", "ffn2_norm", "ffn2_w_gu", "ffn2_w_down"]


_AXIS = dict(_MATS)
DEPTH = 2


def _layer_groups(l):
    w_in, w_out = ("ev_w_in", "ev_w_out") if l % 2 == 0 else ("od_w_in", "od_w_out")
    return [[("ffn1_w_gu", l), ("ffn2_w_gu", l)], [(w_in, l // 2)], [("xa_w_kv", l)],
            [("ffn1_w_down", l), ("ffn2_w_down", l), (w_out, l // 2), ("xa_w_q", l), ("xa_w_o", l)]]


def _weight_blocks(shards, l):
    blocks = []
    for group in _layer_groups(l):
        rows = [shards[n][j].astype(BF16) for n, j in group]
        blocks.append(rows[0] if len(rows) == 1 else jnp.concatenate(rows, axis=0))
    return blocks


def _whole_weights(shards, l, gathered):
    full = {}
    for group, got in zip(_layer_groups(l), gathered):
        off = 0
        for n, j in group:
            a, b = shards[n].shape[1:]
            seg = got[:, off:off + a, :]
            off += a
            full[n] = seg.reshape(N_DEV * a, b) if _AXIS[n] == "row" else seg.transpose(1, 0, 2).reshape(a, N_DEV * b)
    return full


def _gradient_buffers(grads, l):
    bufs = []
    for group in _layer_groups(l):
        rows = []
        for n, _ in group:
            a, b = grads[n].shape
            if _AXIS[n] == "row":
                rows.append(grads[n].reshape(N_DEV, a // N_DEV, b))
            else:
                rows.append(grads[n].reshape(a, N_DEV, b // N_DEV).transpose(1, 0, 2))
        bufs.append((rows[0] if len(rows) == 1 else jnp.concatenate(rows, axis=1)).astype(BF16))
    return bufs


def _gradient_blocks(shards, l, sums):
    out = {}
    for group, tot in zip(_layer_groups(l), sums):
        off = 0
        for n, j in group:
            a = shards[n].shape[1]
            out[n, j] = tot[off:off + a]
            off += a
    return out


class _Exchange:
    def __init__(self, shards, c):
        self.shards, self.c = shards, c

    def weights_first(self):
        return _whole_weights(self.shards, 0, all_gather_blocks(_weight_blocks(self.shards, 0)))

    def gather_start(self):
        return gather_side(0, _weight_blocks(self.shards, 1))

    def weights_next(self, gathered):
        return _whole_weights(self.shards, 1, gathered)

    def pack(self, grads):
        return _gradient_buffers(grads, 1)

    def pair_sums(self, bufs, got, tag="l1"):
        return [pair_sum(b, g, self.c, b.dtype, f"grads_pair_sum_{tag}_{i}") for i, (b, g) in enumerate(zip(bufs, got))]

    def chip_sums(self, parts, tag="l1"):
        return [chip_sum(p, f"grads_chip_sum_{tag}_{i}") for i, p in enumerate(parts)]

    def finish(self, gm, gv, sums1):
        vecs = {n: jnp.concatenate(v, axis=0) for n, v in gv.items()}
        bufs = _gradient_buffers(gm[0], 0)
        vec = jnp.concatenate([vecs[n].reshape(-1) for n in _VECS])
        vec = jnp.pad(vec, (0, -vec.shape[0] % (16 * LANES)))
        bufs.append(jnp.broadcast_to(vec.reshape(1, -1, LANES), (N_DEV, vec.shape[0] // LANES, LANES)))
        parts = self.pair_sums(bufs, pair_exchange(bufs), "l0")
        sums0 = self.chip_sums(chip_exchange(parts), "l0")
        blocks = {**_gradient_blocks(self.shards, 0, sums0[:-1]), **_gradient_blocks(self.shards, 1, sums1)}
        out = {n: jnp.stack([blocks[n, j] for j in range(self.shards[n].shape[0])]) for n, _ in _MATS}
        flat, off = sums0[-1].reshape(-1), 0
        for n in _VECS:
            out[n] = flat[off:off + vecs[n].size].reshape(vecs[n].shape)
            off += vecs[n].size
        return out


class _NoExchange:
    def __init__(self, full):
        self.full = full

    def weights_first(self):
        return self.full[0]

    def gather_start(self):
        return None

    def weights_next(self, gathered):
        return self.full[1]

    def pack(self, grads):
        return None

    def finish(self, gm, gv, sums1):
        mats = {}
        for l in range(DEPTH):
            for group in _layer_groups(l):
                for n, j in group:
                    mats.setdefault(n, {})[j] = gm[l][n]
        mats = {n: jnp.stack([v[j] for j in sorted(v)]) for n, v in mats.items()}
        return mats, {n: jnp.concatenate(v, axis=0) for n, v in gv.items()}


def _local_step(x, mem, target, w, ex):
    assert w["ffn1_norm"].shape[0] == DEPTH
    row = lambda a, l: a[l:l + 1]
    full = [ex.weights_first(), None]
    saved = []
    for l in range(DEPTH):
        t, j, f = f"l{l}", l // 2, full[l]
        x, s1 = ffn_fwd(x, row(w["ffn1_norm"], l), f["ffn1_w_gu"], f["ffn1_w_down"], t + "_ffn1")
        relay = None
        if l % 2 == 0:
            side = ex.gather_start() if l + 1 < DEPTH else None
            x, s2, relay = even_mixer_fwd(x, row(w["mix_norm"], l), _ev_reorder(f["ev_w_in"]), row(w["ev_q_gain"], j),
                                          row(w["ev_k_gain"], j), row(w["ev_sinks"], j), f["ev_w_out"], t + "_ev", side)
        else:
            x, s2 = odd_mixer_fwd(x, row(w["mix_norm"], l), f["od_w_in"], row(w["od_q_gain"], j),
                                  row(w["od_k_gain"], j), f["od_w_out"], t + "_od")
        x, s3 = xa_fwd(x, mem, row(w["xa_norm"], l), row(w["xa_mem_norm"], l), f["xa_w_q"], f["xa_w_kv"],
                       row(w["xa_q_gain"], l), row(w["xa_k_gain"], l), f["xa_w_o"], t + "_xa")
        if relay is None:
            x, s4 = ffn_fwd(x, row(w["ffn2_norm"], l), f["ffn2_w_gu"], f["ffn2_w_down"], t + "_ffn2")
        else:
            x, s4, relay = ffn_fwd(x, row(w["ffn2_norm"], l), f["ffn2_w_gu"], f["ffn2_w_down"], t + "_ffn2", relay)
        if l + 1 < DEPTH:
            full[l + 1] = ex.weights_next(relay)
        saved.append((s1, s2, s3, s4))
    dx, sq = loss_head(x, target, "loss_head")
    loss = 0.5 * jnp.sum(sq) / x.shape[1]

    gm = [dict() for _ in range(DEPTH)]
    gv = {n: [None] * w[n].shape[0] for n in _VECS}
    packed = sums1 = None
    for l in reversed(range(DEPTH)):
        t, j, f = f"l{l}", l // 2, full[l]
        s1, s2, s3, s4 = saved[l]
        parts = None
        if l == 0 and packed is not None:
            dx, gv["ffn2_norm"][l], gm[l]["ffn2_w_gu"], gm[l]["ffn2_w_down"], got = ffn_bwd(
                dx, s4, row(w["ffn2_norm"], l), f["ffn2_w_gu"], f["ffn2_w_down"], t + "_ffn2", packed)
            parts = ex.pair_sums(packed, got)
        else:
            dx, gv["ffn2_norm"][l], gm[l]["ffn2_w_gu"], gm[l]["ffn2_w_down"] = ffn_bwd(
                dx, s4, row(w["ffn2_norm"], l), f["ffn2_w_gu"], f["ffn2_w_down"], t + "_ffn2")
        (dx, gv["xa_norm"][l], gv["xa_mem_norm"][l], gm[l]["xa_w_q"], gm[l]["xa_w_kv"], gv["xa_q_gain"][l],
         gv["xa_k_gain"][l], gm[l]["xa_w_o"]) = xa_bwd(
            dx, s3, mem, row(w["xa_norm"], l), row(w["xa_mem_norm"], l), f["xa_w_q"], f["xa_w_kv"],
            row(w["xa_q_gain"], l), row(w["xa_k_gain"], l), f["xa_w_o"], t + "_xa")
        if l % 2 == 0:
            (dx, gv["mix_norm"][l], d_win, gv["ev_q_gain"][j], gv["ev_k_gain"][j], gv["ev_sinks"][j],
             gm[l]["ev_w_out"], carried) = even_mixer_bwd(
                dx, s2, row(w["mix_norm"], l), _ev_reorder(f["ev_w_in"]), row(w["ev_q_gain"], j), row(w["ev_k_gain"], j),
                row(w["ev_sinks"], j), f["ev_w_out"], t + "_ev", None if parts is None else chip_side(parts))
            gm[l]["ev_w_in"] = _ev_restore(d_win)
            if carried is not None:
                sums1 = ex.chip_sums(carried)
        else:
            (dx, gv["mix_norm"][l], gm[l]["od_w_in"], gv["od_q_gain"][j], gv["od_k_gain"][j],
             gm[l]["od_w_out"]) = odd_mixer_bwd(
                dx, s2, row(w["mix_norm"], l), f["od_w_in"], row(w["od_q_gain"], j), row(w["od_k_gain"], j),
                f["od_w_out"], t + "_od")
        dx, gv["ffn1_norm"][l], gm[l]["ffn1_w_gu"], gm[l]["ffn1_w_down"] = ffn_bwd(
            dx, s1, row(w["ffn1_norm"], l), f["ffn1_w_gu"], f["ffn1_w_down"], t + "_ffn1")
        if l == 1:
            packed = ex.pack(gm[l])
    return loss, dx, ex.finish(gm, gv, sums1)


def kernel(x, mem, ffn1_norm, ffn1_w_gu, ffn1_w_down, mix_norm, ev_w_in, ev_q_gain, ev_k_gain, ev_sinks, ev_w_out, od_w_in, od_q_gain, od_k_gain, od_w_out, xa_norm, xa_mem_norm, xa_w_q, xa_w_kv, xa_q_gain, xa_k_gain, xa_w_o, ffn2_norm, ffn2_w_gu, ffn2_w_down, loss_target, m_ffn1_norm, m_ffn1_w_gu, m_ffn1_w_down, m_mix_norm, m_ev_w_in, m_ev_q_gain, m_ev_k_gain, m_ev_sinks, m_ev_w_out, m_od_w_in, m_od_q_gain, m_od_k_gain, m_od_w_out, m_xa_norm, m_xa_mem_norm, m_xa_w_q, m_xa_w_kv, m_xa_q_gain, m_xa_k_gain, m_xa_w_o, m_ffn2_norm, m_ffn2_w_gu, m_ffn2_w_down, v_ffn1_norm, v_ffn1_w_gu, v_ffn1_w_down, v_mix_norm, v_ev_w_in, v_ev_q_gain, v_ev_k_gain, v_ev_sinks, v_ev_w_out, v_od_w_in, v_od_q_gain, v_od_k_gain, v_od_w_out, v_xa_norm, v_xa_mem_norm, v_xa_w_q, v_xa_w_kv, v_xa_q_gain, v_xa_k_gain, v_xa_w_o, v_ffn2_norm, v_ffn2_w_gu, v_ffn2_w_down):
    w = dict(ffn1_norm=ffn1_norm, ffn1_w_gu=ffn1_w_gu, ffn1_w_down=ffn1_w_down, mix_norm=mix_norm, ev_w_in=ev_w_in, ev_q_gain=ev_q_gain, ev_k_gain=ev_k_gain, ev_sinks=ev_sinks, ev_w_out=ev_w_out, od_w_in=od_w_in, od_q_gain=od_q_gain, od_k_gain=od_k_gain, od_w_out=od_w_out, xa_norm=xa_norm, xa_mem_norm=xa_mem_norm, xa_w_q=xa_w_q, xa_w_kv=xa_w_kv, xa_q_gain=xa_q_gain, xa_k_gain=xa_k_gain, xa_w_o=xa_w_o, ffn2_norm=ffn2_norm, ffn2_w_gu=ffn2_w_gu, ffn2_w_down=ffn2_w_down)
    m = dict(ffn1_norm=m_ffn1_norm, ffn1_w_gu=m_ffn1_w_gu, ffn1_w_down=m_ffn1_w_down, mix_norm=m_mix_norm, ev_w_in=m_ev_w_in, ev_q_gain=m_ev_q_gain, ev_k_gain=m_ev_k_gain, ev_sinks=m_ev_sinks, ev_w_out=m_ev_w_out, od_w_in=m_od_w_in, od_q_gain=m_od_q_gain, od_k_gain=m_od_k_gain, od_w_out=m_od_w_out, xa_norm=m_xa_norm, xa_mem_norm=m_xa_mem_norm, xa_w_q=m_xa_w_q, xa_w_kv=m_xa_w_kv, xa_q_gain=m_xa_q_gain, xa_k_gain=m_xa_k_gain, xa_w_o=m_xa_w_o, ffn2_norm=m_ffn2_norm, ffn2_w_gu=m_ffn2_w_gu, ffn2_w_down=m_ffn2_w_down)
    v = dict(ffn1_norm=v_ffn1_norm, ffn1_w_gu=v_ffn1_w_gu, ffn1_w_down=v_ffn1_w_down, mix_norm=v_mix_norm, ev_w_in=v_ev_w_in, ev_q_gain=v_ev_q_gain, ev_k_gain=v_ev_k_gain, ev_sinks=v_ev_sinks, ev_w_out=v_ev_w_out, od_w_in=v_od_w_in, od_q_gain=v_od_q_gain, od_k_gain=v_od_k_gain, od_w_out=v_od_w_out, xa_norm=v_xa_norm, xa_mem_norm=v_xa_mem_norm, xa_w_q=v_xa_w_q, xa_w_kv=v_xa_w_kv, xa_q_gain=v_xa_q_gain, xa_k_gain=v_xa_k_gain, xa_w_o=v_xa_w_o, ffn2_norm=v_ffn2_norm, ffn2_w_gu=v_ffn2_w_gu, ffn2_w_down=v_ffn2_w_down)

    c = lax.axis_index("c").astype(jnp.int32).reshape(1)
    loss, dx, grads = _local_step(x[0], mem[0], loss_target[0], w, _Exchange(w, c))
    loss = lax.psum(loss, ("x", "y", "c"))

    delta, new_m, new_v = {}, {}, {}
    for n in _WEIGHTS:
        delta[n], new_m[n], new_v[n] = adamw(w[n], grads[n], m[n], v[n], "adamw_" + n)
    return (loss, dx[None], *[grads[n] for n in _WEIGHTS], *[delta[n] for n in _WEIGHTS],
            *[new_m[n] for n in _WEIGHTS], *[new_v[n] for n in _WEIGHTS])
```

```python
import functools

import numpy as np
import jax
import jax.numpy as jnp
from jax import lax
from jax.experimental import pallas as pl
from jax.experimental.pallas import tpu as pltpu

F32 = jnp.float32
BF16 = jnp.bfloat16
MESH = pl.DeviceIdType.MESH

HEAD_DIM = 64
BLOCK = 128
RMS_EPS = 1e-6
A_Q_HEADS, A_KV_HEADS = 8, 2
B_HEADS = 8
C_HEADS = 16
C_PATTERNS = ((128, 1), (512, 4), (2048, 16))
X_HEADS = 4
N_DEV = 8
LANES = 1024
VMEM_LIMIT_BYTES = 56 * 1024 * 1024
SB_SKIP_LOG = -110.0
NEG_BIG = -1e30

ADAM_LR, ADAM_B1, ADAM_B2, ADAM_EPS, ADAM_WD, ADAM_STEP = 0.001, 0.9, 0.999, 1e-08, 0.01, 10

NN = (((1,), (0,)), ((), ()))
NT = (((1,), (1,)), ((), ()))
TN = (((0,), (0,)), ((), ()))


class Side:
    def __init__(self, arrays, out_shapes, n_remote, n_local, plan, aliased=False):
        self.arrays, self.out_shapes, self.plan, self.aliased = list(arrays), list(out_shapes), plan, aliased
        self.sems = [pltpu.SemaphoreType.DMA((n_remote,)), pltpu.SemaphoreType.DMA((n_remote,)),
                     pltpu.SemaphoreType.DMA((max(n_local, 1),))]

    def start(self, ins, outs, sems):
        local, sends, _ = self.plan(ins, outs, *sems)
        for make in local + sends:
            make().start()

    def wait(self, ins, outs, sems):
        local, sends, recvs = self.plan(ins, outs, *sems)
        for make in sends:
            make().wait_send()
        for make in recvs:
            make().wait_recv()
        for make in local:
            make().wait()


def _pcall(body, side=None, **kw):
    if side is None:
        return pl.pallas_call(body, **kw)
    grid = kw["grid"]
    single = not isinstance(kw["out_specs"], (list, tuple))
    out_specs = [kw["out_specs"]] if single else list(kw["out_specs"])
    out_shape = [kw["out_shape"]] if single else list(kw["out_shape"])
    scratch = list(kw.get("scratch_shapes", []))
    n_in, n_out, n_scr, n_side = len(kw["in_specs"]), len(out_specs), len(scratch), len(side.arrays)
    n_sout = len(side.out_shapes)

    def hosted(*refs):
        ins, s_in = refs[:n_in], refs[n_in:n_in + n_side]
        outs = refs[n_in + n_side:n_in + n_side + n_out]
        s_out = refs[n_in + n_side + n_out:n_in + n_side + n_out + n_sout]
        rest = refs[n_in + n_side + n_out + n_sout:]
        scr, sems = rest[:n_scr], rest[n_scr:]
        first = last = None
        for a, size in enumerate(grid):
            f, l = pl.program_id(a) == 0, pl.program_id(a) == size - 1
            first = f if first is None else jnp.logical_and(first, f)
            last = l if last is None else jnp.logical_and(last, l)

        @pl.when(first)
        def _():
            side.start(s_in, s_out, sems)

        body(*ins, *outs, *scr)

        @pl.when(last)
        def _():
            side.wait(s_in, s_out, sems)

    any_space = pl.BlockSpec(memory_space=pl.ANY)
    kw2 = dict(kw)
    kw2.update(in_specs=list(kw["in_specs"]) + [any_space] * n_side, out_specs=out_specs + [any_space] * n_sout,
               out_shape=out_shape + side.out_shapes, scratch_shapes=scratch + side.sems)
    if side.aliased:
        kw2["input_output_aliases"] = {n_in + i: n_out + i for i in range(n_side)}
    call = pl.pallas_call(hosted, **kw2)

    def run(*args):
        res = call(*args, *side.arrays)
        return (res[0] if single else list(res[:n_out])), list(res[n_out:])

    return run


def _params(**kw):
    return pltpu.CompilerParams(vmem_limit_bytes=VMEM_LIMIT_BYTES, **kw)


def _tile(dim, cap, unit=128):
    if dim <= cap:
        return dim
    t = (cap // unit) * unit
    while t >= unit:
        if dim % t == 0:
            return t
        t -= unit
    raise ValueError(f"no tile for {dim} under {cap}")


def _dot(a, b, dims):
    return lax.dot_general(a.astype(BF16), b.astype(BF16), dims, preferred_element_type=F32)


@functools.partial(jax.custom_vjp, nondiff_argnums=(2,))
def _dot_vjp(a, b, nt):
    return _dot(a, b, NT if nt else NN)


def _dot_vjp_fwd(a, b, nt):
    return _dot(a, b, NT if nt else NN), (a.astype(BF16), b.astype(BF16))


def _dot_vjp_bwd(nt, res, g):
    a, b = res
    if nt:
        return _dot(g, b, NN), _dot(g, a, TN)
    return _dot(g, b, NT), _dot(a, g, TN)


_dot_vjp.defvjp(_dot_vjp_fwd, _dot_vjp_bwd)


def _plain_dot(a, b, nt):
    return _dot(a, b, NT if nt else NN)


def _split_dot(x, mat, terms=2):
    out, rem = None, x
    for t in range(terms):
        part = rem.astype(BF16)
        d = lax.dot_general(part, mat, NN, preferred_element_type=F32)
        out = d if out is None else out + d
        if t + 1 < terms:
            rem = rem - part.astype(F32)
    return out


@functools.partial(jax.custom_vjp, nondiff_argnums=(3,))
def _split_dot_vjp(x, mat, mat_t, terms):
    return _split_dot(x, mat, terms)


def _split_dot_vjp_fwd(x, mat, mat_t, terms):
    return _split_dot(x, mat, terms), mat_t


def _split_dot_vjp_bwd(terms, mat_t, g):
    return _split_dot(g, mat_t, terms), None, None


_split_dot_vjp.defvjp(_split_dot_vjp_fwd, _split_dot_vjp_bwd)


def _plain_split(x, mat, mat_t, terms):
    return _split_dot(x, mat, terms)


def _tri(after):
    j = lax.broadcasted_iota(jnp.int32, (BLOCK, BLOCK), 0)
    s = lax.broadcasted_iota(jnp.int32, (BLOCK, BLOCK), 1)
    return jnp.where(j > s if after else j < s, 1.0, 0.0).astype(BF16)


def _in(a, block, imap):
    return (a, block, imap)


def _out(shape, dtype, block, imap, acc=False):
    return (shape, dtype, block, imap, acc)


def tcall(fn, grid, ins, outs, name, scratch=None, side=None):
    nin = len(ins)
    nout = len(outs)
    ngrid = len(grid)

    def body(*refs):
        ids = tuple(pl.program_id(a) for a in range(ngrid))
        extra = {} if scratch is None else {"scratch": refs[nin + nout]}
        res = fn(ids, *[r[...] for r in refs[:nin]], **extra)
        first = ids[0] == 0
        for a in range(1, ngrid):
            first = jnp.logical_and(first, ids[a] == 0)
        for o_ref, r, spec in zip(refs[nin:nin + nout], res, outs):
            if spec[4]:
                @pl.when(first)
                def _(o_ref=o_ref):
                    o_ref[...] = jnp.zeros(o_ref.shape, o_ref.dtype)
                o_ref[...] += r.astype(o_ref.dtype)
            else:
                o_ref[...] = r.astype(o_ref.dtype)

    return _pcall(
        body, side=side, name=name, grid=grid,
        in_specs=[pl.BlockSpec(b, m) for (_, b, m) in ins],
        out_specs=[pl.BlockSpec(b, m) for (_, _, b, m, _) in outs],
        out_shape=[jax.ShapeDtypeStruct(s, d) for (s, d, _, _, _) in outs],
        scratch_shapes=[] if scratch is None else [pltpu.VMEM(*scratch)],
        compiler_params=_params(),
    )(*[a for (a, _, _) in ins])


def _to_strided(scr, nat, d):
    if d == 1:
        return nat
    t, w = nat.shape
    nc = w // BLOCK
    for c in range(nc):
        scr[c * t:(c + 1) * t, :] = nat[:, c * BLOCK:(c + 1) * BLOCK]
    return jnp.concatenate([scr[pl.ds(c * t + r, t // d, stride=d), :] for r in range(d) for c in range(nc)], axis=1)


def _to_natural(scr, st, d):
    if d == 1:
        return st.astype(F32)
    t, w = st.shape[0] * d, st.shape[1] // d
    nc = w // BLOCK
    st = st.astype(F32)
    for r in range(d):
        for c in range(nc):
            scr[pl.ds(c * t + r, t // d, stride=d), :] = st[:, r * w + c * BLOCK:r * w + (c + 1) * BLOCK]
    return jnp.concatenate([scr[c * t:(c + 1) * t, :] for c in range(nc)], axis=1)


def _row(a, tm, width=None, cb=0):
    width = a.shape[1] if width is None else width
    return _in(a, (tm, width), lambda i, cb=cb: (i, cb))


def _full(a):
    zeros = (0,) * a.ndim
    return _in(a, a.shape, lambda *ids: zeros)


def _row_out(n, width, dtype, tm):
    return _out((n, width), dtype, (tm, width), lambda i: (i, 0))


def _acc_out(shape):
    zeros = (0,) * len(shape)
    return _out(shape, F32, shape, lambda *ids: zeros, acc=True)


def mm(a, b, mode, name, *, out_dtype=F32, scale=1.0, res=None, side=None):
    if mode == "nn":
        (m, k), (k2, n) = a.shape, b.shape
    elif mode == "nt":
        (m, k), (n, k2) = a.shape, b.shape
    else:
        (k, m), (k2, n) = a.shape, b.shape
    assert k == k2, (a.shape, b.shape, mode)
    tm, tn, tk = _tile(m, 512), _tile(n, 1408), _tile(k, 1408)
    nk = k // tk
    dims = {"nn": NN, "nt": NT, "tn": TN}[mode]
    has_res = res is not None

    def body(*refs):
        if has_res:
            a_ref, b_ref, r_ref, o_ref, acc_ref = refs
        else:
            a_ref, b_ref, o_ref, acc_ref = refs
        kk = pl.program_id(2)

        @pl.when(kk == 0)
        def _():
            acc_ref[...] = jnp.zeros(acc_ref.shape, F32)

        acc_ref[...] += _dot(a_ref[...], b_ref[...], dims)

        @pl.when(kk == nk - 1)
        def _():
            out = acc_ref[...]
            if scale != 1.0:
                out = out * scale
            if has_res:
                out = out + r_ref[...]
            o_ref[...] = out.astype(o_ref.dtype)

    a_spec = (pl.BlockSpec((tk, tm), lambda i, j, kk: (kk, i)) if mode == "tn"
              else pl.BlockSpec((tm, tk), lambda i, j, kk: (i, kk)))
    b_spec = (pl.BlockSpec((tn, tk), lambda i, j, kk: (j, kk)) if mode == "nt"
              else pl.BlockSpec((tk, tn), lambda i, j, kk: (kk, j)))
    in_specs = [a_spec, b_spec]
    args = [a, b]
    if has_res:
        in_specs.append(pl.BlockSpec((tm, tn), lambda i, j, kk: (i, j)))
        args.append(res)
    order = ("parallel", "parallel", "arbitrary") if side is None else ("arbitrary",) * 3
    return _pcall(
        body, side=side, name=name, grid=(m // tm, n // tn, nk),
        in_specs=in_specs,
        out_specs=pl.BlockSpec((tm, tn), lambda i, j, kk: (i, j)),
        out_shape=jax.ShapeDtypeStruct((m, n), out_dtype),
        scratch_shapes=[pltpu.VMEM((tm, tn), F32)],
        compiler_params=_params(dimension_semantics=order),
    )(*args)


def _rms(x, g):
    return x * lax.rsqrt(jnp.mean(x * x, axis=-1, keepdims=True) + RMS_EPS) * g


def _silu_mul(gate, up):
    return gate / (1.0 + jnp.exp(-gate)) * up


def mm_gate_up(h, w_gu, name, side=None):
    m, k = h.shape
    f = w_gu.shape[1] // 2
    tm, tn, tk = _tile(m, 512), _tile(f, 1408), _tile(k, 1408)
    nk, nj = k // tk, f // tn

    def body(h_ref, wg_ref, wu_ref, g_ref, u_ref, a_ref, accg_ref, accu_ref):
        kk = pl.program_id(2)

        @pl.when(kk == 0)
        def _():
            accg_ref[...] = jnp.zeros(accg_ref.shape, F32)
            accu_ref[...] = jnp.zeros(accu_ref.shape, F32)

        ht = h_ref[...]
        accg_ref[...] += _dot(ht, wg_ref[...], NN)
        accu_ref[...] += _dot(ht, wu_ref[...], NN)

        @pl.when(kk == nk - 1)
        def _():
            gate, up = accg_ref[...], accu_ref[...]
            g_ref[...] = gate
            u_ref[...] = up
            a_ref[...] = _silu_mul(gate, up).astype(a_ref.dtype)

    tile = pl.BlockSpec((tm, tn), lambda i, j, kk: (i, j))
    return _pcall(
        body, side=side, name=name, grid=(m // tm, nj, nk),
        in_specs=[pl.BlockSpec((tm, tk), lambda i, j, kk: (i, kk)),
                  pl.BlockSpec((tk, tn), lambda i, j, kk: (kk, j)),
                  pl.BlockSpec((tk, tn), lambda i, j, kk: (kk, j + nj))],
        out_specs=[tile, tile, tile],
        out_shape=[jax.ShapeDtypeStruct((m, f), F32), jax.ShapeDtypeStruct((m, f), F32),
                   jax.ShapeDtypeStruct((m, f), BF16)],
        scratch_shapes=[pltpu.VMEM((tm, tn), F32), pltpu.VMEM((tm, tn), F32)],
        compiler_params=_params(dimension_semantics=("arbitrary",) * 3),
    )(h, w_gu, w_gu)


def mm_norm_bwd(a, b, x, g, dres, name):
    m, k = a.shape
    d = b.shape[0]
    tm, tk = _tile(m, 512), _tile(k, 1408)
    nk = k // tk
    has_res = dres is not None

    def body(*refs):
        a_ref, b_ref, x_ref, g_ref = refs[:4]
        r_ref = refs[4] if has_res else None
        dx_ref, dg_ref, acc_ref = refs[-3:]
        i, kk = pl.program_id(0), pl.program_id(1)

        @pl.when(kk == 0)
        def _():
            acc_ref[...] = jnp.zeros(acc_ref.shape, F32)

        acc_ref[...] += _dot(a_ref[...], b_ref[...], NT)

        @pl.when(kk == nk - 1)
        def _():
            _, vjp = jax.vjp(_rms, x_ref[...], g_ref[...])
            dx, dg = vjp(acc_ref[...])
            dx_ref[...] = dx + r_ref[...] if has_res else dx

            @pl.when(i == 0)
            def _():
                dg_ref[...] = jnp.zeros(dg_ref.shape, F32)

            dg_ref[...] += dg

    rows = pl.BlockSpec((tm, d), lambda i, kk: (i, 0))
    in_specs = [pl.BlockSpec((tm, tk), lambda i, kk: (i, kk)), pl.BlockSpec((d, tk), lambda i, kk: (0, kk)), rows,
                pl.BlockSpec(g.shape, lambda i, kk: (0, 0))] + ([rows] if has_res else [])
    return _pcall(
        body, name=name, grid=(m // tm, nk),
        in_specs=in_specs,
        out_specs=[rows, pl.BlockSpec(g.shape, lambda i, kk: (0, 0))],
        out_shape=[jax.ShapeDtypeStruct((m, d), F32), jax.ShapeDtypeStruct(g.shape, F32)],
        scratch_shapes=[pltpu.VMEM((tm, d), F32)],
        compiler_params=_params(dimension_semantics=("arbitrary", "arbitrary")),
    )(*([a, b, x, g] + ([dres] if has_res else [])))


def _indicator(shape, head_axis, mod):
    lane = lax.broadcasted_iota(jnp.int32, shape, head_axis)
    other = lax.broadcasted_iota(jnp.int32, shape, 1 - head_axis)
    lane = jnp.bitwise_and(lane, HEAD_DIM - 1) if mod else jnp.right_shift(lane, 6)
    return jnp.where(lane == other, 1.0, 0.0).astype(BF16)


def _head_rms(split, xs, g):
    w = xs.shape[1]
    to_head, from_head = _indicator((w, BLOCK), 0, False), _indicator((BLOCK, w), 1, False)
    to_lane, from_lane = _indicator((HEAD_DIM, w), 1, True), _indicator((w, HEAD_DIM), 0, True)
    ss = split(xs * xs, to_head, from_head, 3)
    r = lax.rsqrt(ss * (1.0 / HEAD_DIM) + RMS_EPS)
    g_all = split(jnp.broadcast_to(g, (8, HEAD_DIM)), to_lane, from_lane, 3)[0:1]
    return xs * split(r, from_head, to_head, 3) * g_all


def _prep(split, x, qg, kg, segs):
    parts = []
    for start, width, kind in segs:
        xs = x[:, start:start + width]
        parts.append(xs if kind == "raw" else _head_rms(split, xs, qg if kind == "q" else kg))
    return jnp.concatenate(parts, axis=1)


def prep_fwd(x, qg, kg, segs, dils, name):
    n, w = x.shape
    tm = _tile(n, 256, 8)

    def fn(ids, xt, a, b, scratch):
        ops = _prep(_plain_split, xt, a, b, segs)
        return tuple(_to_strided(scratch, ops, d) for d in dils)

    return tcall(fn, (n // tm,), [_row(x, tm), _full(qg), _full(kg)],
                 [_out((n // d, d * w), BF16, (tm // d, d * w), lambda i: (i, 0)) for d in dils], name,
                 scratch=((w // BLOCK * tm, BLOCK), F32))


def prep_bwd(x, qg, kg, segs, grads, gather, name):
    n, w = x.shape
    tm = BLOCK
    nblk = n // tm

    def fn(ids, xt, a, b, *t, scratch):
        t = [_to_natural(scratch, ti, d) for ti, (_, _, d) in zip(t, grads)]
        t = [jnp.where(ids[0] + sh < nblk, ti, 0.0) if sh else ti for ti, (_, sh, _) in zip(t, grads)]
        _, vjp = jax.vjp(lambda x_, a_, b_: _prep(_split_dot_vjp, x_, a_, b_, segs), xt, a, b)
        return vjp(gather(*t))

    specs = [_in(a, (tm // d, a.shape[1]), (lambda i, sh=sh: (jnp.minimum(i + sh, nblk - 1), 0))) for a, sh, d in grads]
    wmax = max(a.shape[1] // d for a, _, d in grads)
    return tcall(fn, (nblk,), [_row(x, tm), _full(qg), _full(kg)] + specs,
                 [_row_out(n, w, BF16, tm), _acc_out(qg.shape), _acc_out(kg.shape)], name,
                 scratch=((wmax // BLOCK * tm, BLOCK), F32))


def rmsnorm_fwd(x, g, name, side=None):
    n, d = x.shape
    tm = _tile(n, 512, 8)
    res = tcall(lambda ids, xt, gt: (_rms(xt, gt),), (n // tm,), [_row(x, tm), _full(g)],
                [_row_out(n, d, BF16, tm)], name, side=side)
    if side is None:
        return res[0]
    return res[0][0], res[1]


def ffn_fwd(x, g, w_gu, w_down, tag, carry=None):
    h = rmsnorm_fwd(x, g, tag + "_norm")
    if carry is None:
        gate, up, a = mm_gate_up(h, w_gu, tag + "_gu")
        return mm(a, w_down, "nn", tag + "_down", scale=0.5, res=x), (x, h, gate, up, a)
    phase, bufs = carry
    (gate, up, a), bufs = mm_gate_up(h, w_gu, tag + "_gu", side=gather_side(phase, bufs))
    y, bufs = mm(a, w_down, "nn", tag + "_down", scale=0.5, res=x, side=gather_side(phase + 1, bufs))
    return y, (x, h, gate, up, a), bufs


def ffn_bwd(dy, saved, g, w_gu, w_down, tag, pair=None):
    x, h, gate, up, a = saved
    n = x.shape[0]
    f = w_down.shape[0]
    da = mm(dy, w_down, "nt", tag + "_da", scale=0.5)
    d_wdown = mm(a, dy, "tn", tag + "_dwd", scale=0.5, side=None if pair is None else pair_side(pair))
    got = None
    if pair is not None:
        d_wdown, got = d_wdown
    tm = _tile(n, 128, 8)

    def act_bwd(ids, gt, ut, dat):
        _, vjp = jax.vjp(_silu_mul, gt, ut)
        dg, du = vjp(dat)
        return (jnp.concatenate([dg, du], axis=1),)

    (dgu,) = tcall(act_bwd, (n // tm,), [_row(gate, tm), _row(up, tm), _row(da, tm)],
                   [_row_out(n, 2 * f, BF16, tm)], tag + "_dact")
    d_wgu = mm(h, dgu, "tn", tag + "_dwgu")
    dx, dg = mm_norm_bwd(dgu, w_gu, x, g, dy, tag + "_dh")
    if pair is not None:
        return dx, dg, d_wgu, d_wdown, got
    return dx, dg, d_wgu, d_wdown


def _alibi(n_heads):
    return [float(s) for s in np.asarray(2.0 ** (-8.0 * np.arange(1, n_heads + 1) / n_heads), dtype=np.float32)]


def _banded_tile(dot, first, q, kp, kc, vp, vc, sinks, *, hkv, grp, max_dist, step, slopes, want_lse):
    row = lax.broadcasted_iota(jnp.int32, (BLOCK, 2 * BLOCK), 0)
    col = lax.broadcasted_iota(jnp.int32, (BLOCK, 2 * BLOCK), 1)
    dist = row + BLOCK - col
    valid = (dist >= 0) & (dist <= max_dist) & ((col >= BLOCK) | jnp.logical_not(first))
    distf = dist.astype(F32)

    def head(hd, qh, k2, v2):
        s = dot(qh, k2, True) * (HEAD_DIM ** -0.5)
        s = jnp.where(valid, s - (slopes[hd] * step) * distf, NEG_BIG)
        m = jnp.max(s, axis=-1, keepdims=True)
        if sinks is not None:
            pick = lax.broadcasted_iota(jnp.int32, sinks.shape, 1) == hd
            sk = jnp.sum(jnp.where(pick, sinks, 0.0), axis=1, keepdims=True)
            m = jnp.maximum(m, sk)
        m = lax.stop_gradient(m)
        p = jnp.exp(s - m)
        denom = jnp.sum(p, axis=-1, keepdims=True)
        if sinks is not None:
            denom = denom + jnp.exp(sk - m)
        return dot(p / denom, v2, False), m + jnp.log(denom)

    outs, lses = [], []
    if grp == 1:
        low = lax.broadcasted_iota(jnp.int32, (BLOCK, BLOCK), 1) < HEAD_DIM
        for pr in range(hkv // 2):
            sl = slice(pr * BLOCK, (pr + 1) * BLOCK)
            q2 = q[:, sl]
            k2 = jnp.concatenate([kp[:, sl], kc[:, sl]], axis=0)
            v2 = jnp.concatenate([vp[:, sl], vc[:, sl]], axis=0)
            o0, l0 = head(2 * pr, jnp.where(low, q2, 0.0), k2, v2)
            o1, l1 = head(2 * pr + 1, jnp.where(low, 0.0, q2), k2, v2)
            outs.append(jnp.where(low, o0, o1))
            lses.append(jnp.where(low, l0, l1))
    else:
        for hk in range(hkv):
            sl = slice(hk * HEAD_DIM, (hk + 1) * HEAD_DIM)
            k2 = jnp.concatenate([kp[:, sl], kc[:, sl]], axis=0)
            v2 = jnp.concatenate([vp[:, sl], vc[:, sl]], axis=0)
            for gi in range(grp):
                hd = hk * grp + gi
                o_h, l_h = head(hd, q[:, hd * HEAD_DIM:(hd + 1) * HEAD_DIM], k2, v2)
                outs.append(o_h)
                lses.append(jnp.broadcast_to(l_h, (BLOCK, HEAD_DIM)))
    o = jnp.concatenate(outs, axis=1)
    if want_lse:
        return o, jnp.concatenate(lses, axis=1)
    return (o,)


def _banded_specs(view, qcol, kcol, vcol, wq, wkv):
    def at(colfn, prev):
        if prev:
            return lambda r, n: (jnp.maximum(n - 1, 0), colfn(r))
        return lambda r, n: (n, colfn(r))
    return [
        _in(view, (BLOCK, wq), at(qcol, False)),
        _in(view, (BLOCK, wkv), at(kcol, True)),
        _in(view, (BLOCK, wkv), at(kcol, False)),
        _in(view, (BLOCK, wkv), at(vcol, True)),
        _in(view, (BLOCK, wkv), at(vcol, False)),
    ]


def banded_fwd(view, dil, cols, sinks, cfg, name):
    ns = view.shape[0]
    nb = ns // BLOCK
    wq, wkv = cfg["hkv"] * cfg["grp"] * HEAD_DIM, cfg["hkv"] * HEAD_DIM
    has_sinks = sinks is not None

    def fn(ids, q, kp, kc, vp, vc, *rest):
        q, kp, kc, vp, vc = [a.astype(F32) for a in (q, kp, kc, vp, vc)]
        return _banded_tile(_plain_dot, ids[1] == 0, q, kp, kc, vp, vc, rest[0] if has_sinks else None, **cfg)

    ins = _banded_specs(view, *cols, wq, wkv) + ([_full(sinks)] if has_sinks else [])
    outs = [_out((ns, dil * wq), F32 if cfg["want_lse"] else BF16, (BLOCK, wq), lambda r, n: (n, r))]
    if cfg["want_lse"]:
        outs.append(_out((ns, dil * wq), F32, (BLOCK, wq), lambda r, n: (n, r)))
    return tcall(fn, (dil, nb), ins, outs, name)


def banded_bwd(view, dil, cols, sinks, cfg, cts, name):
    ns = view.shape[0]
    nb = ns // BLOCK
    wq, wkv = cfg["hkv"] * cfg["grp"] * HEAD_DIM, cfg["hkv"] * HEAD_DIM
    has_sinks = sinks is not None
    assert len(cts) == (2 if cfg["want_lse"] else 1)

    def fn(ids, q, kp, kc, vp, vc, *rest):
        sk = rest[0] if has_sinks else None
        ct = rest[1 if has_sinks else 0:]
        first = ids[1] == 0

        def f(q, kp, kc, vp, vc, *s):
            return _banded_tile(_dot_vjp, first, q, kp, kc, vp, vc, s[0] if has_sinks else None, **cfg)

        prim = tuple(a.astype(F32) for a in (q, kp, kc, vp, vc)) + ((sk,) if has_sinks else ())
        _, vjp = jax.vjp(f, *prim)
        return vjp(tuple(c.astype(F32) for c in ct))

    ins = (_banded_specs(view, *cols, wq, wkv) + ([_full(sinks)] if has_sinks else [])
           + [_in(a, (BLOCK, wq), (lambda r, n, cf=cf: (n, cf(r)))) for (a, cf) in cts])
    blk = lambda w: _out((ns, dil * w), F32, (BLOCK, w), lambda r, n: (n, r))
    outs = [blk(wq), blk(wkv), blk(wkv), blk(wkv), blk(wkv)]
    if has_sinks:
        outs.append(_acc_out(sinks.shape))
    return tcall(fn, (dil, nb), ins, outs, name)


def _log_sigmoid(z):
    return jnp.minimum(z, 0.0) - jnp.log(1.0 + jnp.exp(-jnp.abs(z)))


SB_PAIRS = 4


def _sb_pair(dot, suffix, qh, kb, vb, r_in, mask):
    z = dot(qh, kb, True) * (HEAD_DIM ** -0.5)
    lsp = _log_sigmoid(z)
    log_keep = jnp.where(mask, lsp - z, 0.0)
    log_after = suffix(log_keep) + r_in
    a = jnp.where(mask, jnp.exp(lsp + log_after), 0.0)
    return dot(a, vb, False), r_in + jnp.sum(log_keep, axis=1, keepdims=True)


def sb_fwd(qkv, qcb, kcb, vcb, name, side=None):
    s = qkv.shape[0]
    nb = s // BLOCK
    pairs = B_HEADS // 2
    wide = SB_PAIRS * BLOCK
    nh = 2 * SB_PAIRS
    assert pairs % SB_PAIRS == 0 and qcb % SB_PAIRS == 0 and kcb % SB_PAIRS == 0 and vcb % SB_PAIRS == 0

    def body(q_ref, k_ref, v_ref, o_ref):
        n = pl.program_id(1)
        low = lax.broadcasted_iota(jnp.int32, (BLOCK, BLOCK), 1) < HEAD_DIM
        before = (lax.broadcasted_iota(jnp.int32, (BLOCK, BLOCK), 1)
                  < lax.broadcasted_iota(jnp.int32, (BLOCK, BLOCK), 0))
        after = _tri(True)
        suffix = lambda t: _split_dot(t, after)
        qs = []
        for p in range(SB_PAIRS):
            q2 = q_ref[:, p * BLOCK:(p + 1) * BLOCK].astype(F32)
            qs += [jnp.where(low, q2, 0.0), jnp.where(low, 0.0, q2)]

        def cond(c):
            return jnp.logical_and(c[0] >= 0, c[1] > SB_SKIP_LOG)

        def step(c):
            kb, _, rs, accs = c
            rows = pl.ds(pl.multiple_of(kb * BLOCK, BLOCK), BLOCK)
            mask = jnp.logical_or(before, kb != n)
            new_r, new_acc, top = [], [], None
            for h in range(nh):
                cols = slice((h // 2) * BLOCK, (h // 2 + 1) * BLOCK)
                o_part, r_out = _sb_pair(_plain_dot, suffix, qs[h], k_ref[rows, cols], v_ref[rows, cols], rs[h], mask)
                new_r.append(r_out)
                new_acc.append(accs[h] + o_part)
                top = jnp.max(r_out) if top is None else jnp.maximum(top, jnp.max(r_out))
            return kb - 1, top, tuple(new_r), tuple(new_acc)

        init = (n, jnp.float32(0.0), tuple(jnp.zeros((BLOCK, 1), F32) for _ in range(nh)),
                tuple(jnp.zeros((BLOCK, BLOCK), F32) for _ in range(nh)))
        accs = lax.while_loop(cond, step, init)[3]
        for p in range(SB_PAIRS):
            o_ref[:, p * BLOCK:(p + 1) * BLOCK] = jnp.where(low, accs[2 * p], accs[2 * p + 1]).astype(o_ref.dtype)

    return _pcall(
        body, side=side, name=name, grid=(pairs // SB_PAIRS, nb),
        in_specs=[pl.BlockSpec((BLOCK, wide), lambda g, n: (n, qcb // SB_PAIRS + g)),
                  pl.BlockSpec((s, wide), lambda g, n: (0, kcb // SB_PAIRS + g), pipeline_mode=pl.Buffered(1)),
                  pl.BlockSpec((s, wide), lambda g, n: (0, vcb // SB_PAIRS + g), pipeline_mode=pl.Buffered(1))],
        out_specs=pl.BlockSpec((BLOCK, wide), lambda g, n: (n, g)),
        out_shape=jax.ShapeDtypeStruct((s, pairs * BLOCK), BF16),
        compiler_params=_params(),
    )(qkv, qkv, qkv)


def sb_bwd(qkv, qcb, kcb, vcb, do, docb, name, side=None):
    s = qkv.shape[0]
    nb = s // BLOCK
    pairs = B_HEADS // 2
    wide = SB_PAIRS * BLOCK
    nh = 2 * SB_PAIRS
    assert docb % SB_PAIRS == 0

    def body(q_ref, k_ref, v_ref, do_ref, dq_ref, dk_ref, dv_ref, r_ref):
        n = pl.program_id(1)

        @pl.when(n == 0)
        def _():
            dk_ref[...] = jnp.zeros(dk_ref.shape, F32)
            dv_ref[...] = jnp.zeros(dv_ref.shape, F32)

        low = lax.broadcasted_iota(jnp.int32, (BLOCK, BLOCK), 1) < HEAD_DIM
        before = (lax.broadcasted_iota(jnp.int32, (BLOCK, BLOCK), 1)
                  < lax.broadcasted_iota(jnp.int32, (BLOCK, BLOCK), 0))
        after, earlier = _tri(True), _tri(False)
        suffix = lambda t: _split_dot_vjp(t, after, earlier, 2)
        qs, dos = [], []
        for p in range(SB_PAIRS):
            q2 = q_ref[:, p * BLOCK:(p + 1) * BLOCK].astype(F32)
            do2 = do_ref[:, p * BLOCK:(p + 1) * BLOCK].astype(F32)
            qs += [jnp.where(low, q2, 0.0), jnp.where(low, 0.0, q2)]
            dos += [jnp.where(low, do2, 0.0), jnp.where(low, 0.0, do2)]

        def cond(c):
            return jnp.logical_and(c[0] >= 0, c[1] > SB_SKIP_LOG)

        def down(c):
            kb, _, rs = c
            rows = pl.ds(pl.multiple_of(kb * BLOCK, BLOCK), BLOCK)
            mask = jnp.logical_or(before, kb != n)
            new_r, top = [], None
            for h in range(nh):
                cols = slice((h // 2) * BLOCK, (h // 2 + 1) * BLOCK)
                r_ref[h, kb] = rs[h]
                z = _dot(qs[h], k_ref[rows, cols], NT) * (HEAD_DIM ** -0.5)
                log_keep = jnp.where(mask, _log_sigmoid(z) - z, 0.0)
                r_out = rs[h] + jnp.sum(log_keep, axis=1, keepdims=True)
                new_r.append(r_out)
                top = jnp.max(r_out) if top is None else jnp.maximum(top, jnp.max(r_out))
            return kb - 1, top, tuple(new_r)

        init = (n, jnp.float32(0.0), tuple(jnp.zeros((BLOCK, 1), F32) for _ in range(nh)))
        last = lax.while_loop(cond, down, init)[0] + 1

        def up(kb, c):
            dqs, g_rs = c
            rows = pl.ds(pl.multiple_of(kb * BLOCK, BLOCK), BLOCK)
            mask = jnp.logical_or(before, kb != n)
            new_dq, new_g = [], []
            for h in range(nh):
                cols = slice((h // 2) * BLOCK, (h // 2 + 1) * BLOCK)
                _, vjp = jax.vjp(lambda q_, k_, v_, r_: _sb_pair(_dot_vjp, suffix, q_, k_, v_, r_, mask),
                                 qs[h], k_ref[rows, cols].astype(F32), v_ref[rows, cols].astype(F32), r_ref[h, kb])
                dq_c, dk_c, dv_c, g_in = vjp((dos[h], g_rs[h]))
                dk_ref[rows, cols] += dk_c
                dv_ref[rows, cols] += dv_c
                new_dq.append(dqs[h] + dq_c)
                new_g.append(g_in)
            return tuple(new_dq), tuple(new_g)

        init = (tuple(jnp.zeros((BLOCK, BLOCK), F32) for _ in range(nh)),
                tuple(jnp.zeros((BLOCK, 1), F32) for _ in range(nh)))
        dqs = lax.fori_loop(last, n + 1, up, init)[0]
        for p in range(SB_PAIRS):
            dq_ref[:, p * BLOCK:(p + 1) * BLOCK] = jnp.where(low, dqs[2 * p], dqs[2 * p + 1])

    full = jax.ShapeDtypeStruct((s, pairs * BLOCK), F32)
    return _pcall(
        body, side=side, name=name, grid=(pairs // SB_PAIRS, nb),
        in_specs=[pl.BlockSpec((BLOCK, wide), lambda g, n: (n, qcb // SB_PAIRS + g)),
                  pl.BlockSpec((s, wide), lambda g, n: (0, kcb // SB_PAIRS + g), pipeline_mode=pl.Buffered(1)),
                  pl.BlockSpec((s, wide), lambda g, n: (0, vcb // SB_PAIRS + g), pipeline_mode=pl.Buffered(1)),
                  pl.BlockSpec((BLOCK, wide), lambda g, n: (n, docb // SB_PAIRS + g))],
        out_specs=[pl.BlockSpec((BLOCK, wide), lambda g, n: (n, g)),
                   pl.BlockSpec((s, wide), lambda g, n: (0, g), pipeline_mode=pl.Buffered(1)),
                   pl.BlockSpec((s, wide), lambda g, n: (0, g), pipeline_mode=pl.Buffered(1))],
        out_shape=[full, full, full],
        scratch_shapes=[pltpu.VMEM((nh, nb, BLOCK, 1), F32)],
        compiler_params=_params(),
    )(qkv, qkv, qkv, do)


def _xa_tile(dot, q, kv, qg, kg):
    hd = q.shape[1] // X_HEADS
    outs = []
    for h in range(X_HEADS):
        qh = _rms(q[:, h * hd:(h + 1) * hd], qg)
        kh = _rms(kv[:, h * hd:(h + 1) * hd], kg)
        vh = kv[:, (X_HEADS + h) * hd:(X_HEADS + h + 1) * hd]
        sc = dot(qh, kh, True) * (hd ** -0.5)
        m = lax.stop_gradient(jnp.max(sc, axis=-1, keepdims=True))
        p = jnp.exp(sc - m)
        outs.append(dot(p / jnp.sum(p, axis=-1, keepdims=True), vh, False))
    return jnp.concatenate(outs, axis=1)


def xa_core_fwd(q, kv, qg, kg, name):
    n, d = q.shape
    tm = _tile(n, 256, 8)
    (o,) = tcall(lambda ids, qt, kvt, qgt, kgt: (_xa_tile(_plain_dot, qt, kvt, qgt, kgt),), (n // tm,),
                 [_row(q, tm), _full(kv), _full(qg), _full(kg)], [_row_out(n, d, BF16, tm)], name)
    return o


def xa_core_bwd(q, kv, qg, kg, do, name):
    n, d = q.shape
    tm = _tile(n, 256, 8)

    def fn(ids, qt, kvt, qgt, kgt, dot_):
        _, vjp = jax.vjp(functools.partial(_xa_tile, _dot_vjp), qt, kvt, qgt, kgt)
        return vjp(dot_.astype(F32))

    return tcall(fn, (n // tm,), [_row(q, tm), _full(kv), _full(qg), _full(kg), _row(do, tm)],
                 [_row_out(n, d, BF16, tm), _acc_out(kv.shape), _acc_out(qg.shape), _acc_out(kg.shape)], name)


def _ev_reorder(a):
    return jnp.concatenate([a[..., 0:512], a[..., 768:2304], a[..., 512:768]], axis=-1)


def _ev_restore(a):
    return jnp.concatenate([a[..., 0:512], a[..., 2048:2304], a[..., 512:2048]], axis=-1)


_EV_SEGS = ((0, 512, "q"), (512, 1536, "raw"), (2048, 128, "k"), (2176, 128, "raw"))
_A_CFG = dict(hkv=A_KV_HEADS, grp=A_Q_HEADS // A_KV_HEADS, max_dist=BLOCK - 1, step=1.0, slopes=_alibi(A_Q_HEADS),
              want_lse=False)
_A_COLS = (lambda r: 0, lambda r: 16, lambda r: 17)


def even_mixer_fwd(x, h, w_in, qg, kg, sinks, w_out, tag, side=None):
    qkv = mm(h, w_in, "nn", tag + "_in")
    (ops,) = prep_fwd(qkv, qg, kg, _EV_SEGS, (1,), tag + "_prep")
    (o_a,) = banded_fwd(ops, 1, _A_COLS, sinks, _A_CFG, tag + "_swa")
    o_b = sb_fwd(ops, 4, 8, 12, tag + "_sb", side=side)
    carried = None
    if side is not None:
        o_b, carried = o_b
    o = jnp.concatenate([o_a, o_b], axis=1)
    y = mm(o, w_out, "nn", tag + "_out", res=x)
    return y, (x, h, qkv, ops, o), carried


def even_mixer_bwd(dy, saved, g, w_in, qg, kg, sinks, w_out, tag, side=None):
    x, h, qkv, ops, o = saved
    do = mm(dy, w_out, "nt", tag + "_do")
    d_wout = mm(o, dy, "tn", tag + "_dwout")
    dqa, dkp, dkc, dvp, dvc, dsinks = banded_bwd(ops, 1, _A_COLS, sinks, _A_CFG, [(do, lambda r: 0)], tag + "_dswa")
    res = sb_bwd(ops, 4, 8, 12, do, 4, tag + "_dsb", side=side)
    carried = None
    if side is not None:
        res, carried = res
    dqb, dkb, dvb = res
    dqkv, dqg, dkg = prep_bwd(
        qkv, qg, kg, _EV_SEGS,
        [(dqa, 0, 1), (dqb, 0, 1), (dkb, 0, 1), (dvb, 0, 1), (dkc, 0, 1), (dkp, 1, 1), (dvc, 0, 1), (dvp, 1, 1)],
        lambda qa, qb, kb, vb, kc, kp, vc, vp: jnp.concatenate([qa, qb, kb, vb, kc + kp, vc + vp], axis=1),
        tag + "_dqkv")
    d_win = mm(h, dqkv, "tn", tag + "_dwin")
    dx, dg = mm_norm_bwd(dqkv, w_in, x, g, dy, tag + "_dh")
    return dx, dg, d_win, dqg, dkg, dsinks, d_wout, carried


def _c_cfg(window, dil):
    return dict(hkv=C_HEADS, grp=1, max_dist=window // dil, step=float(dil), slopes=_alibi(C_HEADS), want_lse=True)


_C_COLS = (lambda r: 3 * r, lambda r: 3 * r + 1, lambda r: 3 * r + 2)
_OD_SEGS = ((0, 1024, "q"), (1024, 1024, "k"), (2048, 1024, "raw"))


def _combine(o1, o2, o3, l1, l2, l3):
    m = lax.stop_gradient(jnp.maximum(jnp.maximum(l1, l2), l3))
    e1, e2, e3 = jnp.exp(l1 - m), jnp.exp(l2 - m), jnp.exp(l3 - m)
    tot = e1 + e2 + e3
    return (e1 / tot) * o1 + (e2 / tot) * o2 + (e3 / tot) * o3


def odd_mixer_fwd(x, g, w_in, qg, kg, w_out, tag):
    n, d = x.shape
    h = rmsnorm_fwd(x, g, tag + "_norm")
    qkv = mm(h, w_in, "nn", tag + "_in")
    dils = [dil for _, dil in C_PATTERNS]
    ops = prep_fwd(qkv, qg, kg, _OD_SEGS, dils, tag + "_prep")
    os_, ls_ = [], []
    for (window, dil), ops_d in zip(C_PATTERNS, ops):
        o_p, l_p = banded_fwd(ops_d, dil, _C_COLS, None, _c_cfg(window, dil), f"{tag}_dil{dil}")
        os_.append(o_p)
        ls_.append(l_p)
    tm = BLOCK
    lay = lambda a, dil: _in(a, (tm // dil, a.shape[1]), lambda i: (i, 0))
    views = [lay(a, dil) for a, dil in zip(os_ + ls_, dils + dils)]

    def comb(ids, *t, scratch):
        return (_combine(*[_to_natural(scratch, a, dil) for a, dil in zip(t, dils + dils)]),)

    (o,) = tcall(comb, (n // tm,), views, [_row_out(n, d, BF16, tm)], tag + "_comb",
                 scratch=((d // BLOCK * tm, BLOCK), F32))
    y = mm(o, w_out, "nn", tag + "_out", res=x)
    return y, (x, h, qkv, ops, views, o)


def odd_mixer_bwd(dy, saved, g, w_in, qg, kg, w_out, tag):
    x, h, qkv, ops, views, o = saved
    n, d = x.shape
    do = mm(dy, w_out, "nt", tag + "_do")
    d_wout = mm(o, dy, "tn", tag + "_dwout")
    tm = BLOCK
    dils = [dil for _, dil in C_PATTERNS]

    def comb_bwd(ids, *t, scratch):
        _, vjp = jax.vjp(_combine, *[_to_natural(scratch, a, dil) for a, dil in zip(t[:6], dils + dils)])
        return tuple(_to_strided(scratch, c, dil) for c, dil in zip(vjp(t[6]), dils + dils))

    cts = tcall(comb_bwd, (n // tm,), views + [_row(do, tm)],
                [_out((n // dil, dil * d), F32, (tm // dil, dil * d), lambda i: (i, 0)) for dil in dils + dils],
                tag + "_dcomb", scratch=((d // BLOCK * tm, BLOCK), F32))
    dqs, dks, dvs = [], [], []
    for p, ((window, dil), ops_d) in enumerate(zip(C_PATTERNS, ops)):
        dq, dkp, dkc, dvp, dvc = banded_bwd(ops_d, dil, _C_COLS, None, _c_cfg(window, dil),
                                            [(cts[p], lambda r: r), (cts[3 + p], lambda r: r)], f"{tag}_ddil{dil}")
        dqs.append((dq, 0, dil))
        dks += [(dkc, 0, dil), (dkp, dil, dil)]
        dvs += [(dvc, 0, dil), (dvp, dil, dil)]

    def gather(*t):
        total = lambda parts: functools.reduce(lambda a, b: a + b, parts)
        return jnp.concatenate([total(t[0:3]), total(t[3:9]), total(t[9:15])], axis=1)

    dqkv, dqg, dkg = prep_bwd(qkv, qg, kg, _OD_SEGS, dqs + dks + dvs, gather, tag + "_dqkv")
    d_win = mm(h, dqkv, "tn", tag + "_dwin")
    dx, dg = mm_norm_bwd(dqkv, w_in, x, g, dy, tag + "_dh")
    return dx, dg, d_win, dqg, dkg, d_wout


def xa_fwd(x, mem, g, gm, w_q, w_kv, qg, kg, w_o, tag):
    h = rmsnorm_fwd(x, g, tag + "_norm")
    q = mm(h, w_q, "nn", tag + "_q")
    mn = rmsnorm_fwd(mem, gm, tag + "_mnorm")
    kv = mm(mn, w_kv, "nn", tag + "_kv")
    o = xa_core_fwd(q, kv, qg, kg, tag + "_core")
    y = mm(o, w_o, "nn", tag + "_o", res=x)
    return y, (x, h, q, mn, kv, o)


def xa_bwd(dy, saved, mem, g, gm, w_q, w_kv, qg, kg, w_o, tag):
    x, h, q, mn, kv, o = saved
    do = mm(dy, w_o, "nt", tag + "_do", out_dtype=BF16)
    d_wo = mm(o, dy, "tn", tag + "_dwo")
    dq, dkv, dqg, dkg = xa_core_bwd(q, kv, qg, kg, do, tag + "_dcore")
    d_wq = mm(h, dq, "tn", tag + "_dwq")
    dx, dg = mm_norm_bwd(dq, w_q, x, g, dy, tag + "_dh")
    d_wkv = mm(mn, dkv, "tn", tag + "_dwkv")
    _, dgm = mm_norm_bwd(dkv, w_kv, mem, gm, None, tag + "_dmn")
    return dx, dg, dgm, d_wq, d_wkv, dqg, dkg, d_wo


def loss_head(y, target, name):
    n, d = y.shape
    tm = _tile(n, 512, 8)

    def fn(ids, yt, tt):
        e = yt - tt
        return e * (1.0 / d), jnp.sum(e * e, axis=0, keepdims=True)

    return tcall(fn, (n // tm,), [_row(y, tm), _row(target, tm)], [_row_out(n, d, F32, tm), _acc_out((1, d))], name)


_ANY = pl.BlockSpec(memory_space=pl.ANY)


def all_gather_blocks(blocks):
    nb = len(blocks)

    def body(*refs):
        x_refs, out_refs = refs[:nb], refs[nb:2 * nb]
        send_sems, recv_sems, local_sems = refs[2 * nb:]
        x, y, c = lax.axis_index("x"), lax.axis_index("y"), lax.axis_index("c")
        me, sibling = (x, y, c), (x, y, 1 - c)
        over_x, over_y, diagonal = (1 - x, y), (x, 1 - y), (1 - x, 1 - y)
        relay_of = ((1 - x) * (1 - c) + x * c, y * (1 - c) + (1 - y) * c)
        relay_to = (x * (1 - c) + (1 - x) * c, (1 - y) * (1 - c) + y * c)

        def copy(b, k, blk, to, own=False):
            px, py, pc = blk
            slot = out_refs[b].at[4 * px + 2 * py + pc]
            return pltpu.make_async_remote_copy(
                src_ref=x_refs[b] if own else slot, dst_ref=slot,
                send_sem=send_sems.at[7 * b + k], recv_sem=recv_sems.at[7 * b + k], device_id=to, device_id_type=MESH)

        mine = [pltpu.make_async_copy(x_refs[b], out_refs[b].at[4 * x + 2 * y + c], local_sems.at[b]) for b in range(nb)]
        for cp in mine:
            cp.start()
        sent = []
        for b in range(nb):
            sent += [copy(b, 0, me, sibling, own=True), copy(b, 1, me, (*over_x, c), own=True),
                     copy(b, 2, me, (*over_y, c), own=True)]
        for cp in sent:
            cp.start()
        for b in range(nb):
            copy(b, 1, (*over_x, c), me).wait_recv()
            copy(b, 2, (*over_y, c), me).wait_recv()
            later = [copy(b, 3, (*relay_of, c), (*relay_to, c)), copy(b, 4, (*over_x, c), sibling),
                     copy(b, 5, (*over_y, c), sibling)]
            for cp in later:
                cp.start()
            sent += later
        for b in range(nb):
            copy(b, 3, (*diagonal, c), me).wait_recv()
            fwd = copy(b, 6, (*diagonal, c), sibling)
            fwd.start()
            sent.append(fwd)
        for b in range(nb):
            copy(b, 0, sibling, me).wait_recv()
            for k, chip in ((4, over_x), (5, over_y), (6, diagonal)):
                copy(b, k, (*chip, 1 - c), me).wait_recv()
        for cp in sent:
            cp.wait_send()
        for cp in mine:
            cp.wait()

    return _pcall(
        body, name="weights_all_gather",
        in_specs=[_ANY] * nb, out_specs=[_ANY] * nb,
        out_shape=[jax.ShapeDtypeStruct((N_DEV,) + a.shape, a.dtype) for a in blocks],
        scratch_shapes=[pltpu.SemaphoreType.DMA((7 * nb,)), pltpu.SemaphoreType.DMA((7 * nb,)),
                        pltpu.SemaphoreType.DMA((nb,))],
    )(*blocks)


def pair_exchange(bufs):
    nb = len(bufs)

    def body(*refs):
        srcs, dsts = refs[:nb], refs[nb:2 * nb]
        send_sems, recv_sems = refs[2 * nb:]
        x, y, c = lax.axis_index("x"), lax.axis_index("y"), lax.axis_index("c")
        copies = []
        for b in range(nb):
            for j in range(4):
                cp = pltpu.make_async_remote_copy(
                    src_ref=srcs[b].at[2 * j + (1 - c)], dst_ref=dsts[b].at[j], send_sem=send_sems.at[4 * b + j],
                    recv_sem=recv_sems.at[4 * b + j], device_id=(x, y, 1 - c), device_id_type=MESH)
                cp.start()
                copies.append(cp)
        for cp in copies:
            cp.wait()

    return _pcall(
        body, name="grads_pair_exchange",
        in_specs=[_ANY] * nb, out_specs=[_ANY] * nb,
        out_shape=[jax.ShapeDtypeStruct((4,) + a.shape[1:], a.dtype) for a in bufs],
        scratch_shapes=[pltpu.SemaphoreType.DMA((4 * nb,)), pltpu.SemaphoreType.DMA((4 * nb,))],
    )(*bufs)


def pair_sum(g, got, c, out_dtype, name):
    r, w = g.shape[1:]
    tr = _tile(r, 512, 16)

    def body(c_ref, a_ref, b_ref, o_ref):
        o_ref[...] = (a_ref[...].astype(F32) + b_ref[...].astype(F32)).astype(o_ref.dtype)

    return _pcall(
        body, name=name,
        grid_spec=pltpu.PrefetchScalarGridSpec(
            num_scalar_prefetch=1, grid=(4, r // tr),
            in_specs=[pl.BlockSpec((None, tr, w), lambda j, i, c_ref: (2 * j + c_ref[0], i, 0)),
                      pl.BlockSpec((None, tr, w), lambda j, i, c_ref: (j, i, 0))],
            out_specs=pl.BlockSpec((None, tr, w), lambda j, i, c_ref: (j, i, 0))),
        out_shape=jax.ShapeDtypeStruct((4,) + g.shape[1:], out_dtype),
        compiler_params=_params(),
    )(c, g, got)


def chip_exchange(parts):
    nb = len(parts)

    def body(*refs):
        srcs, dsts = refs[:nb], refs[nb:2 * nb]
        send_sems, recv_sems, local_sems = refs[2 * nb:]
        x, y, c = lax.axis_index("x"), lax.axis_index("y"), lax.axis_index("c")
        my_chip = 2 * x + y
        copies = []
        for b in range(nb):
            mine = pltpu.make_async_copy(srcs[b].at[my_chip], dsts[b].at[my_chip], local_sems.at[b])
            mine.start()
            copies.append(mine)
            for k, (tx, ty) in enumerate([(1 - x, y), (x, 1 - y), (1 - x, 1 - y)]):
                cp = pltpu.make_async_remote_copy(
                    src_ref=srcs[b].at[2 * tx + ty], dst_ref=dsts[b].at[my_chip], send_sem=send_sems.at[3 * b + k],
                    recv_sem=recv_sems.at[3 * b + k], device_id=(tx, ty, c), device_id_type=MESH)
                cp.start()
                copies.append(cp)
        for cp in copies:
            cp.wait()

    return _pcall(
        body, name="grads_chip_exchange",
        in_specs=[_ANY] * nb, out_specs=[_ANY] * nb,
        out_shape=[jax.ShapeDtypeStruct(a.shape, a.dtype) for a in parts],
        scratch_shapes=[pltpu.SemaphoreType.DMA((3 * nb,)), pltpu.SemaphoreType.DMA((3 * nb,)),
                        pltpu.SemaphoreType.DMA((nb,))],
    )(*parts)


def chip_sum(parts, name):
    r, w = parts.shape[1:]
    tr = _tile(r, 512, 16)
    spec = lambda j: _in(parts, (None, tr, w), lambda i, j=j: (j, i, 0))

    def fn(ids, a, b, c_, d):
        a, b, c_, d = [t.astype(F32) for t in (a, b, c_, d)]
        return (((a + b) + c_) + d,)

    (out,) = tcall(fn, (r // tr,), [spec(j) for j in range(4)],
                   [_out((r, w), F32, (tr, w), lambda i: (i, 0))], name)
    return out


def _remote(src, dst, send_sems, recv_sems, k, to):
    return functools.partial(pltpu.make_async_remote_copy, src_ref=src, dst_ref=dst, send_sem=send_sems.at[k],
                             recv_sem=recv_sems.at[k], device_id=to, device_id_type=MESH)


def _gather_plan(phase, nb):
    def plan(ins, outs, send_sems, recv_sems, local_sems):
        x, y, c = lax.axis_index("x"), lax.axis_index("y"), lax.axis_index("c")
        me, sibling = (x, y, c), (x, y, 1 - c)
        over_x, over_y, diagonal = (1 - x, y), (x, 1 - y), (1 - x, 1 - y)
        relay_of = ((1 - x) * (1 - c) + x * c, y * (1 - c) + (1 - y) * c)
        relay_to = (x * (1 - c) + (1 - x) * c, (1 - y) * (1 - c) + y * c)
        local, sends, recvs = [], [], []
        for b in range(nb):
            slot = lambda chip, core, b=b: outs[b].at[4 * chip[0] + 2 * chip[1] + core]
            if phase == 0:
                local.append(functools.partial(pltpu.make_async_copy, ins[b], slot((x, y), c), local_sems.at[b]))
                moves = [(ins[b], slot((x, y), c), to) for to in (sibling, (*over_x, c), (*over_y, c))]
                arrive = [slot((x, y), 1 - c), slot(over_x, c), slot(over_y, c)]
            elif phase == 1:
                moves = [(slot(relay_of, c), slot(relay_of, c), (*relay_to, c)),
                         (slot(over_x, c), slot(over_x, c), sibling), (slot(over_y, c), slot(over_y, c), sibling)]
                arrive = [slot(diagonal, c), slot(over_x, 1 - c), slot(over_y, 1 - c)]
            else:
                moves = [(slot(diagonal, c), slot(diagonal, c), sibling)]
                arrive = [slot(diagonal, 1 - c)]
            sends += [_remote(src, dst, send_sems, recv_sems, 3 * b + k, to) for k, (src, dst, to) in enumerate(moves)]
            recvs += [_remote(dst, dst, send_sems, recv_sems, 3 * b + k, me) for k, dst in enumerate(arrive)]
        return local, sends, recvs
    return plan


def gather_side(phase, arrays):
    nb = len(arrays)
    if phase == 0:
        shapes = [jax.ShapeDtypeStruct((N_DEV,) + a.shape, a.dtype) for a in arrays]
        return Side(arrays, shapes, 3 * nb, nb, _gather_plan(0, nb))
    shapes = [jax.ShapeDtypeStruct(a.shape, a.dtype) for a in arrays]
    return Side(arrays, shapes, 3 * nb, 0, _gather_plan(phase, nb), aliased=True)


def pair_side(bufs):
    nb = len(bufs)

    def plan(ins, outs, send_sems, recv_sems, local_sems):
        x, y, c = lax.axis_index("x"), lax.axis_index("y"), lax.axis_index("c")
        sends = [_remote(ins[b].at[2 * j + (1 - c)], outs[b].at[j], send_sems, recv_sems, 4 * b + j, (x, y, 1 - c))
                 for b in range(nb) for j in range(4)]
        recvs = [_remote(outs[b].at[j], outs[b].at[j], send_sems, recv_sems, 4 * b + j, (x, y, c))
                 for b in range(nb) for j in range(4)]
        return [], sends, recvs

    shapes = [jax.ShapeDtypeStruct((4,) + a.shape[1:], a.dtype) for a in bufs]
    return Side(bufs, shapes, 4 * nb, 0, plan)


def chip_side(parts):
    nb = len(parts)

    def plan(ins, outs, send_sems, recv_sems, local_sems):
        x, y, c = lax.axis_index("x"), lax.axis_index("y"), lax.axis_index("c")
        my_chip = 2 * x + y
        peers = [(1 - x, y), (x, 1 - y), (1 - x, 1 - y)]
        local = [functools.partial(pltpu.make_async_copy, ins[b].at[my_chip], outs[b].at[my_chip], local_sems.at[b])
                 for b in range(nb)]
        sends = [_remote(ins[b].at[2 * tx + ty], outs[b].at[my_chip], send_sems, recv_sems, 3 * b + k, (tx, ty, c))
                 for b in range(nb) for k, (tx, ty) in enumerate(peers)]
        recvs = [_remote(outs[b].at[2 * tx + ty], outs[b].at[2 * tx + ty], send_sems, recv_sems, 3 * b + k, (x, y, c))
                 for b in range(nb) for k, (tx, ty) in enumerate(peers)]
        return local, sends, recvs

    shapes = [jax.ShapeDtypeStruct(a.shape, a.dtype) for a in parts]
    return Side(parts, shapes, 3 * nb, nb, plan)


def adamw(w, g, m, v, name):
    shape = w.shape
    cols = shape[-1]
    rows = int(np.prod(shape[:-1]))
    w2, g2, m2, v2 = [a.reshape(rows, cols) for a in (w, g, m, v)]
    tr = _tile(rows, 256, 8) if rows % 8 == 0 else rows

    def fn(ids, wt, gt, mt, vt):
        m_new = ADAM_B1 * mt + (1.0 - ADAM_B1) * gt
        v_new = ADAM_B2 * vt + (1.0 - ADAM_B2) * (gt * gt)
        m_hat = m_new / (1.0 - ADAM_B1 ** ADAM_STEP)
        v_hat = v_new / (1.0 - ADAM_B2 ** ADAM_STEP)
        delta = -ADAM_LR * (m_hat / (jnp.sqrt(v_hat) + ADAM_EPS) + ADAM_WD * wt)
        return delta, m_new, v_new

    res = tcall(fn, (rows // tr,), [_row(a, tr) for a in (w2, g2, m2, v2)],
                [_row_out(rows, cols, F32, tr) for _ in range(3)], name)
    return [a.reshape(shape) for a in res]


_MATS = [("ffn1_w_gu", "col"), ("ffn1_w_down", "row"), ("ev_w_in", "col"), ("ev_w_out", "row"),
         ("od_w_in", "col"), ("od_w_out", "row"), ("xa_w_q", "row"), ("xa_w_kv", "col"), ("xa_w_o", "row"),
         ("ffn2_w_gu", "col"), ("ffn2_w_down", "row")]
_VECS = ["ffn1_norm", "mix_norm", "ev_q_gain", "ev_k_gain", "ev_sinks", "od_q_gain", "od_k_gain", "xa_norm",
         "xa_mem_norm", "xa_q_gain", "xa_k_gain", "ffn2_norm"]
_WEIGHTS = ["ffn1_norm", "ffn1_w_gu", "ffn1_w_down", "mix_norm", "ev_w_in", "ev_q_gain", "ev_k_gain", "ev_sinks",
            "ev_w_out", "od_w_in", "od_q_gain", "od_k_gain", "od_w_out", "xa_norm", "xa_mem_norm", "xa_w_q", "xa_w_kv",
            "xa_q_gain", "xa_k_gain", "xa_w_o", "ffn2_norm", "ffn2_w_gu", "ffn2_w_down"]


_AXIS = dict(_MATS)
DEPTH = 2


def _layer_groups(l):
    w_in, w_out = ("ev_w_in", "ev_w_out") if l % 2 == 0 else ("od_w_in", "od_w_out")
    return [[("ffn1_w_gu", l), ("ffn2_w_gu", l)], [(w_in, l // 2)], [("xa_w_kv", l)],
            [("ffn1_w_down", l), ("ffn2_w_down", l), (w_out, l // 2), ("xa_w_q", l), ("xa_w_o", l)]]


def _first_block_groups(l):
    first = [[("ffn1_w_gu", l)], [("ffn1_w_down", l)]]
    rest = [[m for m in group if m[0] not in ("ffn1_w_gu", "ffn1_w_down")] for group in _layer_groups(l)]
    return first, rest


def _weight_blocks(shards, groups):
    blocks = []
    for group in groups:
        rows = [shards[n][j].astype(BF16) for n, j in group]
        blocks.append(rows[0] if len(rows) == 1 else jnp.concatenate(rows, axis=0))
    return blocks


def _whole_weights(shards, groups, gathered):
    full = {}
    for group, got in zip(groups, gathered):
        off = 0
        for n, j in group:
            a, b = shards[n].shape[1:]
            seg = got[:, off:off + a, :]
            off += a
            full[n] = seg.reshape(N_DEV * a, b) if _AXIS[n] == "row" else seg.transpose(1, 0, 2).reshape(a, N_DEV * b)
    return full


def _gradient_buffers(grads, l):
    bufs = []
    for group in _layer_groups(l):
        rows = []
        for n, _ in group:
            a, b = grads[n].shape
            if _AXIS[n] == "row":
                rows.append(grads[n].reshape(N_DEV, a // N_DEV, b))
            else:
                rows.append(grads[n].reshape(a, N_DEV, b // N_DEV).transpose(1, 0, 2))
        bufs.append((rows[0] if len(rows) == 1 else jnp.concatenate(rows, axis=1)).astype(BF16))
    return bufs


def _gradient_blocks(shards, l, sums):
    out = {}
    for group, tot in zip(_layer_groups(l), sums):
        off = 0
        for n, j in group:
            a = shards[n].shape[1]
            out[n, j] = tot[off:off + a]
            off += a
    return out


class _Exchange:
    def __init__(self, shards, c):
        self.shards, self.c = shards, c

    def weights_first(self):
        first, _ = _first_block_groups(0)
        return _whole_weights(self.shards, first, all_gather_blocks(_weight_blocks(self.shards, first)))

    def rest_blocks(self):
        return _weight_blocks(self.shards, _first_block_groups(0)[1])

    def weights_rest(self, gathered):
        return _whole_weights(self.shards, _first_block_groups(0)[1], gathered)

    def gather_start(self):
        return gather_side(0, _weight_blocks(self.shards, _layer_groups(1)))

    def weights_next(self, gathered):
        return _whole_weights(self.shards, _layer_groups(1), gathered)

    def pack(self, grads):
        return _gradient_buffers(grads, 1)

    def pair_sums(self, bufs, got, tag="l1"):
        return [pair_sum(b, g, self.c, b.dtype, f"grads_pair_sum_{tag}_{i}") for i, (b, g) in enumerate(zip(bufs, got))]

    def chip_sums(self, parts, tag="l1"):
        return [chip_sum(p, f"grads_chip_sum_{tag}_{i}") for i, p in enumerate(parts)]

    def finish(self, gm, gv, sums1):
        vecs = {n: jnp.concatenate(v, axis=0) for n, v in gv.items()}
        bufs = _gradient_buffers(gm[0], 0)
        vec = jnp.concatenate([vecs[n].reshape(-1) for n in _VECS])
        vec = jnp.pad(vec, (0, -vec.shape[0] % (16 * LANES)))
        bufs.append(jnp.broadcast_to(vec.reshape(1, -1, LANES), (N_DEV, vec.shape[0] // LANES, LANES)))
        parts = self.pair_sums(bufs, pair_exchange(bufs), "l0")
        sums0 = self.chip_sums(chip_exchange(parts), "l0")
        blocks = {**_gradient_blocks(self.shards, 0, sums0[:-1]), **_gradient_blocks(self.shards, 1, sums1)}
        out = {n: jnp.stack([blocks[n, j] for j in range(self.shards[n].shape[0])]) for n, _ in _MATS}
        flat, off = sums0[-1].reshape(-1), 0
        for n in _VECS:
            out[n] = flat[off:off + vecs[n].size].reshape(vecs[n].shape)
            off += vecs[n].size
        return out


class _NoExchange:
    def __init__(self, full):
        self.full = full

    def weights_first(self):
        return self.full[0]

    def rest_blocks(self):
        return None

    def gather_start(self):
        return None

    def weights_next(self, gathered):
        return self.full[1]

    def pack(self, grads):
        return None

    def finish(self, gm, gv, sums1):
        mats = {}
        for l in range(DEPTH):
            for group in _layer_groups(l):
                for n, j in group:
                    mats.setdefault(n, {})[j] = gm[l][n]
        mats = {n: jnp.stack([v[j] for j in sorted(v)]) for n, v in mats.items()}
        return mats, {n: jnp.concatenate(v, axis=0) for n, v in gv.items()}


def _local_step(x, mem, target, w, ex):
    assert w["ffn1_norm"].shape[0] == DEPTH
    row = lambda a, l: a[l:l + 1]
    full = [ex.weights_first(), None]
    saved = []
    for l in range(DEPTH):
        t, j, f = f"l{l}", l // 2, full[l]
        rest = ex.rest_blocks() if l == 0 else None
        if rest is None:
            x, s1 = ffn_fwd(x, row(w["ffn1_norm"], l), f["ffn1_w_gu"], f["ffn1_w_down"], t + "_ffn1")
        else:
            x, s1, rest = ffn_fwd(x, row(w["ffn1_norm"], l), f["ffn1_w_gu"], f["ffn1_w_down"], t + "_ffn1", (0, rest))
        relay = None
        if l % 2 == 0:
            h = rmsnorm_fwd(x, row(w["mix_norm"], l), t + "_ev_norm", None if rest is None else gather_side(2, rest))
            if rest is not None:
                h, rest = h
                f = full[l] = {**f, **ex.weights_rest(rest)}
            side = ex.gather_start() if l + 1 < DEPTH else None
            x, s2, relay = even_mixer_fwd(x, h, _ev_reorder(f["ev_w_in"]), row(w["ev_q_gain"], j),
                                          row(w["ev_k_gain"], j), row(w["ev_sinks"], j), f["ev_w_out"], t + "_ev", side)
        else:
            x, s2 = odd_mixer_fwd(x, row(w["mix_norm"], l), f["od_w_in"], row(w["od_q_gain"], j),
                                  row(w["od_k_gain"], j), f["od_w_out"], t + "_od")
        x, s3 = xa_fwd(x, mem, row(w["xa_norm"], l), row(w["xa_mem_norm"], l), f["xa_w_q"], f["xa_w_kv"],
                       row(w["xa_q_gain"], l), row(w["xa_k_gain"], l), f["xa_w_o"], t + "_xa")
        if relay is None:
            x, s4 = ffn_fwd(x, row(w["ffn2_norm"], l), f["ffn2_w_gu"], f["ffn2_w_down"], t + "_ffn2")
        else:
            x, s4, relay = ffn_fwd(x, row(w["ffn2_norm"], l), f["ffn2_w_gu"], f["ffn2_w_down"], t + "_ffn2", (1, relay))
        if l + 1 < DEPTH:
            full[l + 1] = ex.weights_next(relay)
        saved.append((s1, s2, s3, s4))
    dx, sq = loss_head(x, target, "loss_head")
    loss = 0.5 * jnp.sum(sq) / x.shape[1]

    gm = [dict() for _ in range(DEPTH)]
    gv = {n: [None] * w[n].shape[0] for n in _VECS}
    packed = sums1 = None
    for l in reversed(range(DEPTH)):
        t, j, f = f"l{l}", l // 2, full[l]
        s1, s2, s3, s4 = saved[l]
        parts = None
        if l == 0 and packed is not None:
            dx, gv["ffn2_norm"][l], gm[l]["ffn2_w_gu"], gm[l]["ffn2_w_down"], got = ffn_bwd(
                dx, s4, row(w["ffn2_norm"], l), f["ffn2_w_gu"], f["ffn2_w_down"], t + "_ffn2", packed)
            parts = ex.pair_sums(packed, got)
        else:
            dx, gv["ffn2_norm"][l], gm[l]["ffn2_w_gu"], gm[l]["ffn2_w_down"] = ffn_bwd(
                dx, s4, row(w["ffn2_norm"], l), f["ffn2_w_gu"], f["ffn2_w_down"], t + "_ffn2")
        (dx, gv["xa_norm"][l], gv["xa_mem_norm"][l], gm[l]["xa_w_q"], gm[l]["xa_w_kv"], gv["xa_q_gain"][l],
         gv["xa_k_gain"][l], gm[l]["xa_w_o"]) = xa_bwd(
            dx, s3, mem, row(w["xa_norm"], l), row(w["xa_mem_norm"], l), f["xa_w_q"], f["xa_w_kv"],
            row(w["xa_q_gain"], l), row(w["xa_k_gain"], l), f["xa_w_o"], t + "_xa")
        if l % 2 == 0:
            (dx, gv["mix_norm"][l], d_win, gv["ev_q_gain"][j], gv["ev_k_gain"][j], gv["ev_sinks"][j],
             gm[l]["ev_w_out"], carried) = even_mixer_bwd(
                dx, s2, row(w["mix_norm"], l), _ev_reorder(f["ev_w_in"]), row(w["ev_q_gain"], j), row(w["ev_k_gain"], j),
                row(w["ev_sinks"], j), f["ev_w_out"], t + "_ev", None if parts is None else chip_side(parts))
            gm[l]["ev_w_in"] = _ev_restore(d_win)
            if carried is not None:
                sums1 = ex.chip_sums(carried)
        else:
            (dx, gv["mix_norm"][l], gm[l]["od_w_in"], gv["od_q_gain"][j], gv["od_k_gain"][j],
             gm[l]["od_w_out"]) = odd_mixer_bwd(
                dx, s2, row(w["mix_norm"], l), f["od_w_in"], row(w["od_q_gain"], j), row(w["od_k_gain"], j),
                f["od_w_out"], t + "_od")
        dx, gv["ffn1_norm"][l], gm[l]["ffn1_w_gu"], gm[l]["ffn1_w_down"] = ffn_bwd(
            dx, s1, row(w["ffn1_norm"], l), f["ffn1_w_gu"], f["ffn1_w_down"], t + "_ffn1")
        if l == 1:
            packed = ex.pack(gm[l])
    return loss, dx, ex.finish(gm, gv, sums1)


def kernel(x, mem, ffn1_norm, ffn1_w_gu, ffn1_w_down, mix_norm, ev_w_in, ev_q_gain, ev_k_gain, ev_sinks, ev_w_out, od_w_in, od_q_gain, od_k_gain, od_w_out, xa_norm, xa_mem_norm, xa_w_q, xa_w_kv, xa_q_gain, xa_k_gain, xa_w_o, ffn2_norm, ffn2_w_gu, ffn2_w_down, loss_target, m_ffn1_norm, m_ffn1_w_gu, m_ffn1_w_down, m_mix_norm, m_ev_w_in, m_ev_q_gain, m_ev_k_gain, m_ev_sinks, m_ev_w_out, m_od_w_in, m_od_q_gain, m_od_k_gain, m_od_w_out, m_xa_norm, m_xa_mem_norm, m_xa_w_q, m_xa_w_kv, m_xa_q_gain, m_xa_k_gain, m_xa_w_o, m_ffn2_norm, m_ffn2_w_gu, m_ffn2_w_down, v_ffn1_norm, v_ffn1_w_gu, v_ffn1_w_down, v_mix_norm, v_ev_w_in, v_ev_q_gain, v_ev_k_gain, v_ev_sinks, v_ev_w_out, v_od_w_in, v_od_q_gain, v_od_k_gain, v_od_w_out, v_xa_norm, v_xa_mem_norm, v_xa_w_q, v_xa_w_kv, v_xa_q_gain, v_xa_k_gain, v_xa_w_o, v_ffn2_norm, v_ffn2_w_gu, v_ffn2_w_down):
    w = dict(ffn1_norm=ffn1_norm, ffn1_w_gu=ffn1_w_gu, ffn1_w_down=ffn1_w_down, mix_norm=mix_norm, ev_w_in=ev_w_in, ev_q_gain=ev_q_gain, ev_k_gain=ev_k_gain, ev_sinks=ev_sinks, ev_w_out=ev_w_out, od_w_in=od_w_in, od_q_gain=od_q_gain, od_k_gain=od_k_gain, od_w_out=od_w_out, xa_norm=xa_norm, xa_mem_norm=xa_mem_norm, xa_w_q=xa_w_q, xa_w_kv=xa_w_kv, xa_q_gain=xa_q_gain, xa_k_gain=xa_k_gain, xa_w_o=xa_w_o, ffn2_norm=ffn2_norm, ffn2_w_gu=ffn2_w_gu, ffn2_w_down=ffn2_w_down)
    m = dict(ffn1_norm=m_ffn1_norm, ffn1_w_gu=m_ffn1_w_gu, ffn1_w_down=m_ffn1_w_down, mix_norm=m_mix_norm, ev_w_in=m_ev_w_in, ev_q_gain=m_ev_q_gain, ev_k_gain=m_ev_k_gain, ev_sinks=m_ev_sinks, ev_w_out=m_ev_w_out, od_w_in=m_od_w_in, od_q_gain=m_od_q_gain, od_k_gain=m_od_k_gain, od_w_out=m_od_w_out, xa_norm=m_xa_norm, xa_mem_norm=m_xa_mem_norm, xa_w_q=m_xa_w_q, xa_w_kv=m_xa_w_kv, xa_q_gain=m_xa_q_gain, xa_k_gain=m_xa_k_gain, xa_w_o=m_xa_w_o, ffn2_norm=m_ffn2_norm, ffn2_w_gu=m_ffn2_w_gu, ffn2_w_down=m_ffn2_w_down)
    v = dict(ffn1_norm=v_ffn1_norm, ffn1_w_gu=v_ffn1_w_gu, ffn1_w_down=v_ffn1_w_down, mix_norm=v_mix_norm, ev_w_in=v_ev_w_in, ev_q_gain=v_ev_q_gain, ev_k_gain=v_ev_k_gain, ev_sinks=v_ev_sinks, ev_w_out=v_ev_w_out, od_w_in=v_od_w_in, od_q_gain=v_od_q_gain, od_k_gain=v_od_k_gain, od_w_out=v_od_w_out, xa_norm=v_xa_norm, xa_mem_norm=v_xa_mem_norm, xa_w_q=v_xa_w_q, xa_w_kv=v_xa_w_kv, xa_q_gain=v_xa_q_gain, xa_k_gain=v_xa_k_gain, xa_w_o=v_xa_w_o, ffn2_norm=v_ffn2_norm, ffn2_w_gu=v_ffn2_w_gu, ffn2_w_down=v_ffn2_w_down)

    c = lax.axis_index("c").astype(jnp.int32).reshape(1)
    loss, dx, grads = _local_step(x[0], mem[0], loss_target[0], w, _Exchange(w, c))
    loss = lax.psum(loss, ("x", "y", "c"))

    delta, new_m, new_v = {}, {}, {}
    for n in _WEIGHTS:
        delta[n], new_m[n], new_v[n] = adamw(w[n], grads[n], m[n], v[n], "adamw_" + n)
    return (loss, dx[None], *[grads[n] for n in _WEIGHTS], *[delta[n] for n in _WEIGHTS],
            *[new_m[n] for n in _WEIGHTS], *[new_v[n] for n in _WEIGHTS])
```

```python
import functools

import numpy as np
import jax
import jax.numpy as jnp
from jax import lax
from jax.experimental import pallas as pl
from jax.experimental.pallas import tpu as pltpu

F32 = jnp.float32
BF16 = jnp.bfloat16
MESH = pl.DeviceIdType.MESH

HEAD_DIM = 64
BLOCK = 128
RMS_EPS = 1e-6
A_Q_HEADS, A_KV_HEADS = 8, 2
B_HEADS = 8
C_HEADS = 16
C_PATTERNS = ((128, 1), (512, 4), (2048, 16))
X_HEADS = 4
N_DEV = 8
LANES = 1024
VMEM_LIMIT_BYTES = 56 * 1024 * 1024
SB_SKIP_LOG = -110.0
NEG_BIG = -1e30

ADAM_LR, ADAM_B1, ADAM_B2, ADAM_EPS, ADAM_WD, ADAM_STEP = 0.001, 0.9, 0.999, 1e-08, 0.01, 10

NN = (((1,), (0,)), ((), ()))
NT = (((1,), (1,)), ((), ()))
TN = (((0,), (0,)), ((), ()))


class Side:
    def __init__(self, arrays, out_shapes, n_remote, n_local, plan, aliased=False):
        self.arrays, self.out_shapes, self.plan, self.aliased = list(arrays), list(out_shapes), plan, aliased
        self.sems = [pltpu.SemaphoreType.DMA((n_remote,)), pltpu.SemaphoreType.DMA((n_remote,)),
                     pltpu.SemaphoreType.DMA((max(n_local, 1),))]

    def start(self, ins, outs, sems):
        local, sends, _ = self.plan(ins, outs, *sems)
        for make in local + sends:
            make().start()

    def wait(self, ins, outs, sems):
        local, sends, recvs = self.plan(ins, outs, *sems)
        for make in sends:
            make().wait_send()
        for make in recvs:
            make().wait_recv()
        for make in local:
            make().wait()


def _pcall(body, side=None, **kw):
    if side is None:
        return pl.pallas_call(body, **kw)
    grid = kw["grid"]
    single = not isinstance(kw["out_specs"], (list, tuple))
    out_specs = [kw["out_specs"]] if single else list(kw["out_specs"])
    out_shape = [kw["out_shape"]] if single else list(kw["out_shape"])
    scratch = list(kw.get("scratch_shapes", []))
    n_in, n_out, n_scr, n_side = len(kw["in_specs"]), len(out_specs), len(scratch), len(side.arrays)
    n_sout = len(side.out_shapes)

    def hosted(*refs):
        ins, s_in = refs[:n_in], refs[n_in:n_in + n_side]
        outs = refs[n_in + n_side:n_in + n_side + n_out]
        s_out = refs[n_in + n_side + n_out:n_in + n_side + n_out + n_sout]
        rest = refs[n_in + n_side + n_out + n_sout:]
        scr, sems = rest[:n_scr], rest[n_scr:]
        first = last = None
        for a, size in enumerate(grid):
            f, l = pl.program_id(a) == 0, pl.program_id(a) == size - 1
            first = f if first is None else jnp.logical_and(first, f)
            last = l if last is None else jnp.logical_and(last, l)

        @pl.when(first)
        def _():
            side.start(s_in, s_out, sems)

        body(*ins, *outs, *scr)

        @pl.when(last)
        def _():
            side.wait(s_in, s_out, sems)

    any_space = pl.BlockSpec(memory_space=pl.ANY)
    kw2 = dict(kw)
    kw2.update(in_specs=list(kw["in_specs"]) + [any_space] * n_side, out_specs=out_specs + [any_space] * n_sout,
               out_shape=out_shape + side.out_shapes, scratch_shapes=scratch + side.sems)
    if side.aliased:
        kw2["input_output_aliases"] = {n_in + i: n_out + i for i in range(n_side)}
    call = pl.pallas_call(hosted, **kw2)

    def run(*args):
        res = call(*args, *side.arrays)
        return (res[0] if single else list(res[:n_out])), list(res[n_out:])

    return run


def _params(**kw):
    return pltpu.CompilerParams(vmem_limit_bytes=VMEM_LIMIT_BYTES, **kw)


def _tile(dim, cap, unit=128):
    if dim <= cap:
        return dim
    t = (cap // unit) * unit
    while t >= unit:
        if dim % t == 0:
            return t
        t -= unit
    raise ValueError(f"no tile for {dim} under {cap}")


def _dot(a, b, dims):
    return lax.dot_general(a.astype(BF16), b.astype(BF16), dims, preferred_element_type=F32)


@functools.partial(jax.custom_vjp, nondiff_argnums=(2,))
def _dot_vjp(a, b, nt):
    return _dot(a, b, NT if nt else NN)


def _dot_vjp_fwd(a, b, nt):
    return _dot(a, b, NT if nt else NN), (a.astype(BF16), b.astype(BF16))


def _dot_vjp_bwd(nt, res, g):
    a, b = res
    if nt:
        return _dot(g, b, NN), _dot(g, a, TN)
    return _dot(g, b, NT), _dot(a, g, TN)


_dot_vjp.defvjp(_dot_vjp_fwd, _dot_vjp_bwd)


def _plain_dot(a, b, nt):
    return _dot(a, b, NT if nt else NN)


def _split_dot(x, mat, terms=2):
    out, rem = None, x
    for t in range(terms):
        part = rem.astype(BF16)
        d = lax.dot_general(part, mat, NN, preferred_element_type=F32)
        out = d if out is None else out + d
        if t + 1 < terms:
            rem = rem - part.astype(F32)
    return out


@functools.partial(jax.custom_vjp, nondiff_argnums=(3,))
def _split_dot_vjp(x, mat, mat_t, terms):
    return _split_dot(x, mat, terms)


def _split_dot_vjp_fwd(x, mat, mat_t, terms):
    return _split_dot(x, mat, terms), mat_t


def _split_dot_vjp_bwd(terms, mat_t, g):
    return _split_dot(g, mat_t, terms), None, None


_split_dot_vjp.defvjp(_split_dot_vjp_fwd, _split_dot_vjp_bwd)


def _plain_split(x, mat, mat_t, terms):
    return _split_dot(x, mat, terms)


def _tri(after):
    j = lax.broadcasted_iota(jnp.int32, (BLOCK, BLOCK), 0)
    s = lax.broadcasted_iota(jnp.int32, (BLOCK, BLOCK), 1)
    return jnp.where(j > s if after else j < s, 1.0, 0.0).astype(BF16)


def _in(a, block, imap):
    return (a, block, imap)


def _out(shape, dtype, block, imap, acc=False):
    return (shape, dtype, block, imap, acc)


def tcall(fn, grid, ins, outs, name, scratch=None, side=None):
    nin = len(ins)
    nout = len(outs)
    ngrid = len(grid)

    def body(*refs):
        ids = tuple(pl.program_id(a) for a in range(ngrid))
        extra = {} if scratch is None else {"scratch": refs[nin + nout]}
        res = fn(ids, *[r[...] for r in refs[:nin]], **extra)
        first = ids[0] == 0
        for a in range(1, ngrid):
            first = jnp.logical_and(first, ids[a] == 0)
        for o_ref, r, spec in zip(refs[nin:nin + nout], res, outs):
            if spec[4]:
                @pl.when(first)
                def _(o_ref=o_ref):
                    o_ref[...] = jnp.zeros(o_ref.shape, o_ref.dtype)
                o_ref[...] += r.astype(o_ref.dtype)
            else:
                o_ref[...] = r.astype(o_ref.dtype)

    return _pcall(
        body, side=side, name=name, grid=grid,
        in_specs=[pl.BlockSpec(b, m) for (_, b, m) in ins],
        out_specs=[pl.BlockSpec(b, m) for (_, _, b, m, _) in outs],
        out_shape=[jax.ShapeDtypeStruct(s, d) for (s, d, _, _, _) in outs],
        scratch_shapes=[] if scratch is None else [pltpu.VMEM(*scratch)],
        compiler_params=_params(),
    )(*[a for (a, _, _) in ins])


def _to_strided(scr, nat, d):
    if d == 1:
        return nat
    t, w = nat.shape
    nc = w // BLOCK
    for c in range(nc):
        scr[c * t:(c + 1) * t, :] = nat[:, c * BLOCK:(c + 1) * BLOCK]
    return jnp.concatenate([scr[pl.ds(c * t + r, t // d, stride=d), :] for r in range(d) for c in range(nc)], axis=1)


def _to_natural(scr, st, d):
    if d == 1:
        return st.astype(F32)
    t, w = st.shape[0] * d, st.shape[1] // d
    nc = w // BLOCK
    st = st.astype(F32)
    for r in range(d):
        for c in range(nc):
            scr[pl.ds(c * t + r, t // d, stride=d), :] = st[:, r * w + c * BLOCK:r * w + (c + 1) * BLOCK]
    return jnp.concatenate([scr[c * t:(c + 1) * t, :] for c in range(nc)], axis=1)


def _row(a, tm, width=None, cb=0):
    width = a.shape[1] if width is None else width
    return _in(a, (tm, width), lambda i, cb=cb: (i, cb))


def _full(a):
    zeros = (0,) * a.ndim
    return _in(a, a.shape, lambda *ids: zeros)


def _row_out(n, width, dtype, tm):
    return _out((n, width), dtype, (tm, width), lambda i: (i, 0))


def _acc_out(shape):
    zeros = (0,) * len(shape)
    return _out(shape, F32, shape, lambda *ids: zeros, acc=True)


def mm(a, b, mode, name, *, out_dtype=F32, scale=1.0, res=None, side=None):
    if mode == "nn":
        (m, k), (k2, n) = a.shape, b.shape
    elif mode == "nt":
        (m, k), (n, k2) = a.shape, b.shape
    else:
        (k, m), (k2, n) = a.shape, b.shape
    assert k == k2, (a.shape, b.shape, mode)
    tm, tn, tk = _tile(m, 512), _tile(n, 1408), _tile(k, 1408)
    nk = k // tk
    dims = {"nn": NN, "nt": NT, "tn": TN}[mode]
    has_res = res is not None

    def body(*refs):
        if has_res:
            a_ref, b_ref, r_ref, o_ref, acc_ref = refs
        else:
            a_ref, b_ref, o_ref, acc_ref = refs
        kk = pl.program_id(2)

        @pl.when(kk == 0)
        def _():
            acc_ref[...] = jnp.zeros(acc_ref.shape, F32)

        acc_ref[...] += _dot(a_ref[...], b_ref[...], dims)

        @pl.when(kk == nk - 1)
        def _():
            out = acc_ref[...]
            if scale != 1.0:
                out = out * scale
            if has_res:
                out = out + r_ref[...]
            o_ref[...] = out.astype(o_ref.dtype)

    a_spec = (pl.BlockSpec((tk, tm), lambda i, j, kk: (kk, i)) if mode == "tn"
              else pl.BlockSpec((tm, tk), lambda i, j, kk: (i, kk)))
    b_spec = (pl.BlockSpec((tn, tk), lambda i, j, kk: (j, kk)) if mode == "nt"
              else pl.BlockSpec((tk, tn), lambda i, j, kk: (kk, j)))
    in_specs = [a_spec, b_spec]
    args = [a, b]
    if has_res:
        in_specs.append(pl.BlockSpec((tm, tn), lambda i, j, kk: (i, j)))
        args.append(res)
    order = ("parallel", "parallel", "arbitrary") if side is None else ("arbitrary",) * 3
    return _pcall(
        body, side=side, name=name, grid=(m // tm, n // tn, nk),
        in_specs=in_specs,
        out_specs=pl.BlockSpec((tm, tn), lambda i, j, kk: (i, j)),
        out_shape=jax.ShapeDtypeStruct((m, n), out_dtype),
        scratch_shapes=[pltpu.VMEM((tm, tn), F32)],
        compiler_params=_params(dimension_semantics=order),
    )(*args)


def _rms(x, g):
    return x * lax.rsqrt(jnp.mean(x * x, axis=-1, keepdims=True) + RMS_EPS) * g


def _silu_mul(gate, up):
    return gate / (1.0 + jnp.exp(-gate)) * up


def mm_gate_up(h, w_gu, name, side=None):
    m, k = h.shape
    f = w_gu.shape[1] // 2
    tm, tn, tk = _tile(m, 512), _tile(f, 1408), _tile(k, 1408)
    nk, nj = k // tk, f // tn

    def body(h_ref, wg_ref, wu_ref, g_ref, u_ref, a_ref, accg_ref, accu_ref):
        kk = pl.program_id(2)

        @pl.when(kk == 0)
        def _():
            accg_ref[...] = jnp.zeros(accg_ref.shape, F32)
            accu_ref[...] = jnp.zeros(accu_ref.shape, F32)

        ht = h_ref[...]
        accg_ref[...] += _dot(ht, wg_ref[...], NN)
        accu_ref[...] += _dot(ht, wu_ref[...], NN)

        @pl.when(kk == nk - 1)
        def _():
            gate, up = accg_ref[...], accu_ref[...]
            g_ref[...] = gate
            u_ref[...] = up
            a_ref[...] = _silu_mul(gate, up).astype(a_ref.dtype)

    tile = pl.BlockSpec((tm, tn), lambda i, j, kk: (i, j))
    return _pcall(
        body, side=side, name=name, grid=(m // tm, nj, nk),
        in_specs=[pl.BlockSpec((tm, tk), lambda i, j, kk: (i, kk)),
                  pl.BlockSpec((tk, tn), lambda i, j, kk: (kk, j)),
                  pl.BlockSpec((tk, tn), lambda i, j, kk: (kk, j + nj))],
        out_specs=[tile, tile, tile],
        out_shape=[jax.ShapeDtypeStruct((m, f), F32), jax.ShapeDtypeStruct((m, f), F32),
                   jax.ShapeDtypeStruct((m, f), BF16)],
        scratch_shapes=[pltpu.VMEM((tm, tn), F32), pltpu.VMEM((tm, tn), F32)],
        compiler_params=_params(dimension_semantics=("arbitrary",) * 3),
    )(h, w_gu, w_gu)


def mm_norm_bwd(a, b, x, g, dres, name):
    m, k = a.shape
    d = b.shape[0]
    tm, tk = _tile(m, 512), _tile(k, 1408)
    nk = k // tk
    has_res = dres is not None

    def body(*refs):
        a_ref, b_ref, x_ref, g_ref = refs[:4]
        r_ref = refs[4] if has_res else None
        dx_ref, dg_ref, acc_ref = refs[-3:]
        i, kk = pl.program_id(0), pl.program_id(1)

        @pl.when(kk == 0)
        def _():
            acc_ref[...] = jnp.zeros(acc_ref.shape, F32)

        acc_ref[...] += _dot(a_ref[...], b_ref[...], NT)

        @pl.when(kk == nk - 1)
        def _():
            _, vjp = jax.vjp(_rms, x_ref[...], g_ref[...])
            dx, dg = vjp(acc_ref[...])
            dx_ref[...] = dx + r_ref[...] if has_res else dx

            @pl.when(i == 0)
            def _():
                dg_ref[...] = jnp.zeros(dg_ref.shape, F32)

            dg_ref[...] += dg

    rows = pl.BlockSpec((tm, d), lambda i, kk: (i, 0))
    in_specs = [pl.BlockSpec((tm, tk), lambda i, kk: (i, kk)), pl.BlockSpec((d, tk), lambda i, kk: (0, kk)), rows,
                pl.BlockSpec(g.shape, lambda i, kk: (0, 0))] + ([rows] if has_res else [])
    return _pcall(
        body, name=name, grid=(m // tm, nk),
        in_specs=in_specs,
        out_specs=[rows, pl.BlockSpec(g.shape, lambda i, kk: (0, 0))],
        out_shape=[jax.ShapeDtypeStruct((m, d), F32), jax.ShapeDtypeStruct(g.shape, F32)],
        scratch_shapes=[pltpu.VMEM((tm, d), F32)],
        compiler_params=_params(dimension_semantics=("arbitrary", "arbitrary")),
    )(*([a, b, x, g] + ([dres] if has_res else [])))


def _indicator(shape, head_axis, mod):
    lane = lax.broadcasted_iota(jnp.int32, shape, head_axis)
    other = lax.broadcasted_iota(jnp.int32, shape, 1 - head_axis)
    lane = jnp.bitwise_and(lane, HEAD_DIM - 1) if mod else jnp.right_shift(lane, 6)
    return jnp.where(lane == other, 1.0, 0.0).astype(BF16)


def _head_rms(split, xs, g):
    w = xs.shape[1]
    to_head, from_head = _indicator((w, BLOCK), 0, False), _indicator((BLOCK, w), 1, False)
    to_lane, from_lane = _indicator((HEAD_DIM, w), 1, True), _indicator((w, HEAD_DIM), 0, True)
    ss = split(xs * xs, to_head, from_head, 3)
    r = lax.rsqrt(ss * (1.0 / HEAD_DIM) + RMS_EPS)
    g_all = split(jnp.broadcast_to(g, (8, HEAD_DIM)), to_lane, from_lane, 3)[0:1]
    return xs * split(r, from_head, to_head, 3) * g_all


def _prep(split, x, qg, kg, segs):
    parts = []
    for start, width, kind in segs:
        xs = x[:, start:start + width]
        parts.append(xs if kind == "raw" else _head_rms(split, xs, qg if kind == "q" else kg))
    return jnp.concatenate(parts, axis=1)


def prep_fwd(x, qg, kg, segs, dils, name):
    n, w = x.shape
    tm = _tile(n, 256, 8)

    def fn(ids, xt, a, b, scratch):
        ops = _prep(_plain_split, xt, a, b, segs)
        return tuple(_to_strided(scratch, ops, d) for d in dils)

    return tcall(fn, (n // tm,), [_row(x, tm), _full(qg), _full(kg)],
                 [_out((n // d, d * w), BF16, (tm // d, d * w), lambda i: (i, 0)) for d in dils], name,
                 scratch=((w // BLOCK * tm, BLOCK), F32))


def prep_bwd(x, qg, kg, segs, grads, gather, name):
    n, w = x.shape
    tm = BLOCK
    nblk = n // tm

    def fn(ids, xt, a, b, *t, scratch):
        t = [_to_natural(scratch, ti, d) for ti, (_, _, d) in zip(t, grads)]
        t = [jnp.where(ids[0] + sh < nblk, ti, 0.0) if sh else ti for ti, (_, sh, _) in zip(t, grads)]
        _, vjp = jax.vjp(lambda x_, a_, b_: _prep(_split_dot_vjp, x_, a_, b_, segs), xt, a, b)
        return vjp(gather(*t))

    specs = [_in(a, (tm // d, a.shape[1]), (lambda i, sh=sh: (jnp.minimum(i + sh, nblk - 1), 0))) for a, sh, d in grads]
    wmax = max(a.shape[1] // d for a, _, d in grads)
    return tcall(fn, (nblk,), [_row(x, tm), _full(qg), _full(kg)] + specs,
                 [_row_out(n, w, BF16, tm), _acc_out(qg.shape), _acc_out(kg.shape)], name,
                 scratch=((wmax // BLOCK * tm, BLOCK), F32))


def rmsnorm_fwd(x, g, name, side=None):
    n, d = x.shape
    tm = _tile(n, 512, 8)
    res = tcall(lambda ids, xt, gt: (_rms(xt, gt),), (n // tm,), [_row(x, tm), _full(g)],
                [_row_out(n, d, BF16, tm)], name, side=side)
    if side is None:
        return res[0]
    return res[0][0], res[1]


def ffn_fwd(x, g, w_gu, w_down, tag, carry=None):
    h = rmsnorm_fwd(x, g, tag + "_norm")
    if carry is None:
        gate, up, a = mm_gate_up(h, w_gu, tag + "_gu")
        return mm(a, w_down, "nn", tag + "_down", scale=0.5, res=x), (x, h, gate, up, a)
    phase, bufs = carry
    (gate, up, a), bufs = mm_gate_up(h, w_gu, tag + "_gu", side=gather_side(phase, bufs))
    y, bufs = mm(a, w_down, "nn", tag + "_down", scale=0.5, res=x, side=gather_side(phase + 1, bufs))
    return y, (x, h, gate, up, a), bufs


def ffn_bwd(dy, saved, g, w_gu, w_down, tag, chain=None):
    x, h, gate, up, a = saved
    n = x.shape[0]
    f = w_down.shape[0]

    def carrying(name, **kw):
        side = None if chain is None else chain.side(name)
        out = mm(name=tag + "_" + name, side=side, **kw)
        if side is None:
            return out
        chain.done(name, out[1])
        return out[0]

    da = carrying("da", a=dy, b=w_down, mode="nt", scale=0.5)
    d_wdown = carrying("dwd", a=a, b=dy, mode="tn", scale=0.5)
    tm = _tile(n, 128, 8)

    def act_bwd(ids, gt, ut, dat):
        _, vjp = jax.vjp(_silu_mul, gt, ut)
        dg, du = vjp(dat)
        return (jnp.concatenate([dg, du], axis=1),)

    (dgu,) = tcall(act_bwd, (n // tm,), [_row(gate, tm), _row(up, tm), _row(da, tm)],
                   [_row_out(n, 2 * f, BF16, tm)], tag + "_dact")
    d_wgu = carrying("dwgu", a=h, b=dgu, mode="tn")
    dx, dg = mm_norm_bwd(dgu, w_gu, x, g, dy, tag + "_dh")
    return dx, dg, d_wgu, d_wdown


def _alibi(n_heads):
    return [float(s) for s in np.asarray(2.0 ** (-8.0 * np.arange(1, n_heads + 1) / n_heads), dtype=np.float32)]


def _banded_tile(dot, first, q, kp, kc, vp, vc, sinks, *, hkv, grp, max_dist, step, slopes, want_lse):
    row = lax.broadcasted_iota(jnp.int32, (BLOCK, 2 * BLOCK), 0)
    col = lax.broadcasted_iota(jnp.int32, (BLOCK, 2 * BLOCK), 1)
    dist = row + BLOCK - col
    valid = (dist >= 0) & (dist <= max_dist) & ((col >= BLOCK) | jnp.logical_not(first))
    distf = dist.astype(F32)

    def head(hd, qh, k2, v2):
        s = dot(qh, k2, True) * (HEAD_DIM ** -0.5)
        s = jnp.where(valid, s - (slopes[hd] * step) * distf, NEG_BIG)
        m = jnp.max(s, axis=-1, keepdims=True)
        if sinks is not None:
            pick = lax.broadcasted_iota(jnp.int32, sinks.shape, 1) == hd
            sk = jnp.sum(jnp.where(pick, sinks, 0.0), axis=1, keepdims=True)
            m = jnp.maximum(m, sk)
        m = lax.stop_gradient(m)
        p = jnp.exp(s - m)
        denom = jnp.sum(p, axis=-1, keepdims=True)
        if sinks is not None:
            denom = denom + jnp.exp(sk - m)
        return dot(p / denom, v2, False), m + jnp.log(denom)

    outs, lses = [], []
    if grp == 1:
        low = lax.broadcasted_iota(jnp.int32, (BLOCK, BLOCK), 1) < HEAD_DIM
        for pr in range(hkv // 2):
            sl = slice(pr * BLOCK, (pr + 1) * BLOCK)
            q2 = q[:, sl]
            k2 = jnp.concatenate([kp[:, sl], kc[:, sl]], axis=0)
            v2 = jnp.concatenate([vp[:, sl], vc[:, sl]], axis=0)
            o0, l0 = head(2 * pr, jnp.where(low, q2, 0.0), k2, v2)
            o1, l1 = head(2 * pr + 1, jnp.where(low, 0.0, q2), k2, v2)
            outs.append(jnp.where(low, o0, o1))
            lses.append(jnp.where(low, l0, l1))
    else:
        for hk in range(hkv):
            sl = slice(hk * HEAD_DIM, (hk + 1) * HEAD_DIM)
            k2 = jnp.concatenate([kp[:, sl], kc[:, sl]], axis=0)
            v2 = jnp.concatenate([vp[:, sl], vc[:, sl]], axis=0)
            for gi in range(grp):
                hd = hk * grp + gi
                o_h, l_h = head(hd, q[:, hd * HEAD_DIM:(hd + 1) * HEAD_DIM], k2, v2)
                outs.append(o_h)
                lses.append(jnp.broadcast_to(l_h, (BLOCK, HEAD_DIM)))
    o = jnp.concatenate(outs, axis=1)
    if want_lse:
        return o, jnp.concatenate(lses, axis=1)
    return (o,)


def _banded_specs(view, qcol, kcol, vcol, wq, wkv):
    def at(colfn, prev):
        if prev:
            return lambda r, n: (jnp.maximum(n - 1, 0), colfn(r))
        return lambda r, n: (n, colfn(r))
    return [
        _in(view, (BLOCK, wq), at(qcol, False)),
        _in(view, (BLOCK, wkv), at(kcol, True)),
        _in(view, (BLOCK, wkv), at(kcol, False)),
        _in(view, (BLOCK, wkv), at(vcol, True)),
        _in(view, (BLOCK, wkv), at(vcol, False)),
    ]


def banded_fwd(view, dil, cols, sinks, cfg, name):
    ns = view.shape[0]
    nb = ns // BLOCK
    wq, wkv = cfg["hkv"] * cfg["grp"] * HEAD_DIM, cfg["hkv"] * HEAD_DIM
    has_sinks = sinks is not None

    def fn(ids, q, kp, kc, vp, vc, *rest):
        q, kp, kc, vp, vc = [a.astype(F32) for a in (q, kp, kc, vp, vc)]
        return _banded_tile(_plain_dot, ids[1] == 0, q, kp, kc, vp, vc, rest[0] if has_sinks else None, **cfg)

    ins = _banded_specs(view, *cols, wq, wkv) + ([_full(sinks)] if has_sinks else [])
    outs = [_out((ns, dil * wq), F32 if cfg["want_lse"] else BF16, (BLOCK, wq), lambda r, n: (n, r))]
    if cfg["want_lse"]:
        outs.append(_out((ns, dil * wq), F32, (BLOCK, wq), lambda r, n: (n, r)))
    return tcall(fn, (dil, nb), ins, outs, name)


def banded_bwd(view, dil, cols, sinks, cfg, cts, name):
    ns = view.shape[0]
    nb = ns // BLOCK
    wq, wkv = cfg["hkv"] * cfg["grp"] * HEAD_DIM, cfg["hkv"] * HEAD_DIM
    has_sinks = sinks is not None
    assert len(cts) == (2 if cfg["want_lse"] else 1)

    def fn(ids, q, kp, kc, vp, vc, *rest):
        sk = rest[0] if has_sinks else None
        ct = rest[1 if has_sinks else 0:]
        first = ids[1] == 0

        def f(q, kp, kc, vp, vc, *s):
            return _banded_tile(_dot_vjp, first, q, kp, kc, vp, vc, s[0] if has_sinks else None, **cfg)

        prim = tuple(a.astype(F32) for a in (q, kp, kc, vp, vc)) + ((sk,) if has_sinks else ())
        _, vjp = jax.vjp(f, *prim)
        return vjp(tuple(c.astype(F32) for c in ct))

    ins = (_banded_specs(view, *cols, wq, wkv) + ([_full(sinks)] if has_sinks else [])
           + [_in(a, (BLOCK, wq), (lambda r, n, cf=cf: (n, cf(r)))) for (a, cf) in cts])
    blk = lambda w: _out((ns, dil * w), F32, (BLOCK, w), lambda r, n: (n, r))
    outs = [blk(wq), blk(wkv), blk(wkv), blk(wkv), blk(wkv)]
    if has_sinks:
        outs.append(_acc_out(sinks.shape))
    return tcall(fn, (dil, nb), ins, outs, name)


def _log_sigmoid(z):
    return jnp.minimum(z, 0.0) - jnp.log(1.0 + jnp.exp(-jnp.abs(z)))


SB_PAIRS = 4


def _sb_pair(dot, suffix, qh, kb, vb, r_in, mask):
    z = dot(qh, kb, True) * (HEAD_DIM ** -0.5)
    lsp = _log_sigmoid(z)
    log_keep = jnp.where(mask, lsp - z, 0.0)
    log_after = suffix(log_keep) + r_in
    a = jnp.where(mask, jnp.exp(lsp + log_after), 0.0)
    return dot(a, vb, False), r_in + jnp.sum(log_keep, axis=1, keepdims=True)


def sb_fwd(qkv, qcb, kcb, vcb, name, side=None):
    s = qkv.shape[0]
    nb = s // BLOCK
    pairs = B_HEADS // 2
    wide = SB_PAIRS * BLOCK
    nh = 2 * SB_PAIRS
    assert pairs % SB_PAIRS == 0 and qcb % SB_PAIRS == 0 and kcb % SB_PAIRS == 0 and vcb % SB_PAIRS == 0

    def body(q_ref, k_ref, v_ref, o_ref):
        n = pl.program_id(1)
        low = lax.broadcasted_iota(jnp.int32, (BLOCK, BLOCK), 1) < HEAD_DIM
        before = (lax.broadcasted_iota(jnp.int32, (BLOCK, BLOCK), 1)
                  < lax.broadcasted_iota(jnp.int32, (BLOCK, BLOCK), 0))
        after = _tri(True)
        suffix = lambda t: _split_dot(t, after)
        qs = []
        for p in range(SB_PAIRS):
            q2 = q_ref[:, p * BLOCK:(p + 1) * BLOCK].astype(F32)
            qs += [jnp.where(low, q2, 0.0), jnp.where(low, 0.0, q2)]

        def cond(c):
            return jnp.logical_and(c[0] >= 0, c[1] > SB_SKIP_LOG)

        def step(c):
            kb, _, rs, accs = c
            rows = pl.ds(pl.multiple_of(kb * BLOCK, BLOCK), BLOCK)
            mask = jnp.logical_or(before, kb != n)
            new_r, new_acc, top = [], [], None
            for h in range(nh):
                cols = slice((h // 2) * BLOCK, (h // 2 + 1) * BLOCK)
                o_part, r_out = _sb_pair(_plain_dot, suffix, qs[h], k_ref[rows, cols], v_ref[rows, cols], rs[h], mask)
                new_r.append(r_out)
                new_acc.append(accs[h] + o_part)
                top = jnp.max(r_out) if top is None else jnp.maximum(top, jnp.max(r_out))
            return kb - 1, top, tuple(new_r), tuple(new_acc)

        init = (n, jnp.float32(0.0), tuple(jnp.zeros((BLOCK, 1), F32) for _ in range(nh)),
                tuple(jnp.zeros((BLOCK, BLOCK), F32) for _ in range(nh)))
        accs = lax.while_loop(cond, step, init)[3]
        for p in range(SB_PAIRS):
            o_ref[:, p * BLOCK:(p + 1) * BLOCK] = jnp.where(low, accs[2 * p], accs[2 * p + 1]).astype(o_ref.dtype)

    return _pcall(
        body, side=side, name=name, grid=(pairs // SB_PAIRS, nb),
        in_specs=[pl.BlockSpec((BLOCK, wide), lambda g, n: (n, qcb // SB_PAIRS + g)),
                  pl.BlockSpec((s, wide), lambda g, n: (0, kcb // SB_PAIRS + g), pipeline_mode=pl.Buffered(1)),
                  pl.BlockSpec((s, wide), lambda g, n: (0, vcb // SB_PAIRS + g), pipeline_mode=pl.Buffered(1))],
        out_specs=pl.BlockSpec((BLOCK, wide), lambda g, n: (n, g)),
        out_shape=jax.ShapeDtypeStruct((s, pairs * BLOCK), BF16),
        compiler_params=_params(),
    )(qkv, qkv, qkv)


def sb_bwd(qkv, qcb, kcb, vcb, do, docb, name, side=None):
    s = qkv.shape[0]
    nb = s // BLOCK
    pairs = B_HEADS // 2
    wide = SB_PAIRS * BLOCK
    nh = 2 * SB_PAIRS
    assert docb % SB_PAIRS == 0

    def body(q_ref, k_ref, v_ref, do_ref, dq_ref, dk_ref, dv_ref, r_ref):
        n = pl.program_id(1)

        @pl.when(n == 0)
        def _():
            dk_ref[...] = jnp.zeros(dk_ref.shape, F32)
            dv_ref[...] = jnp.zeros(dv_ref.shape, F32)

        low = lax.broadcasted_iota(jnp.int32, (BLOCK, BLOCK), 1) < HEAD_DIM
        before = (lax.broadcasted_iota(jnp.int32, (BLOCK, BLOCK), 1)
                  < lax.broadcasted_iota(jnp.int32, (BLOCK, BLOCK), 0))
        after, earlier = _tri(True), _tri(False)
        suffix = lambda t: _split_dot_vjp(t, after, earlier, 2)
        qs, dos = [], []
        for p in range(SB_PAIRS):
            q2 = q_ref[:, p * BLOCK:(p + 1) * BLOCK].astype(F32)
            do2 = do_ref[:, p * BLOCK:(p + 1) * BLOCK].astype(F32)
            qs += [jnp.where(low, q2, 0.0), jnp.where(low, 0.0, q2)]
            dos += [jnp.where(low, do2, 0.0), jnp.where(low, 0.0, do2)]

        def cond(c):
            return jnp.logical_and(c[0] >= 0, c[1] > SB_SKIP_LOG)

        def down(c):
            kb, _, rs = c
            rows = pl.ds(pl.multiple_of(kb * BLOCK, BLOCK), BLOCK)
            mask = jnp.logical_or(before, kb != n)
            new_r, top = [], None
            for h in range(nh):
                cols = slice((h // 2) * BLOCK, (h // 2 + 1) * BLOCK)
                r_ref[h, kb] = rs[h]
                z = _dot(qs[h], k_ref[rows, cols], NT) * (HEAD_DIM ** -0.5)
                log_keep = jnp.where(mask, _log_sigmoid(z) - z, 0.0)
                r_out = rs[h] + jnp.sum(log_keep, axis=1, keepdims=True)
                new_r.append(r_out)
                top = jnp.max(r_out) if top is None else jnp.maximum(top, jnp.max(r_out))
            return kb - 1, top, tuple(new_r)

        init = (n, jnp.float32(0.0), tuple(jnp.zeros((BLOCK, 1), F32) for _ in range(nh)))
        last = lax.while_loop(cond, down, init)[0] + 1

        def up(kb, c):
            dqs, g_rs = c
            rows = pl.ds(pl.multiple_of(kb * BLOCK, BLOCK), BLOCK)
            mask = jnp.logical_or(before, kb != n)
            new_dq, new_g = [], []
            for h in range(nh):
                cols = slice((h // 2) * BLOCK, (h // 2 + 1) * BLOCK)
                _, vjp = jax.vjp(lambda q_, k_, v_, r_: _sb_pair(_dot_vjp, suffix, q_, k_, v_, r_, mask),
                                 qs[h], k_ref[rows, cols].astype(F32), v_ref[rows, cols].astype(F32), r_ref[h, kb])
                dq_c, dk_c, dv_c, g_in = vjp((dos[h], g_rs[h]))
                dk_ref[rows, cols] += dk_c
                dv_ref[rows, cols] += dv_c
                new_dq.append(dqs[h] + dq_c)
                new_g.append(g_in)
            return tuple(new_dq), tuple(new_g)

        init = (tuple(jnp.zeros((BLOCK, BLOCK), F32) for _ in range(nh)),
                tuple(jnp.zeros((BLOCK, 1), F32) for _ in range(nh)))
        dqs = lax.fori_loop(last, n + 1, up, init)[0]
        for p in range(SB_PAIRS):
            dq_ref[:, p * BLOCK:(p + 1) * BLOCK] = jnp.where(low, dqs[2 * p], dqs[2 * p + 1])

    full = jax.ShapeDtypeStruct((s, pairs * BLOCK), F32)
    return _pcall(
        body, side=side, name=name, grid=(pairs // SB_PAIRS, nb),
        in_specs=[pl.BlockSpec((BLOCK, wide), lambda g, n: (n, qcb // SB_PAIRS + g)),
                  pl.BlockSpec((s, wide), lambda g, n: (0, kcb // SB_PAIRS + g), pipeline_mode=pl.Buffered(1)),
                  pl.BlockSpec((s, wide), lambda g, n: (0, vcb // SB_PAIRS + g), pipeline_mode=pl.Buffered(1)),
                  pl.BlockSpec((BLOCK, wide), lambda g, n: (n, docb // SB_PAIRS + g))],
        out_specs=[pl.BlockSpec((BLOCK, wide), lambda g, n: (n, g)),
                   pl.BlockSpec((s, wide), lambda g, n: (0, g), pipeline_mode=pl.Buffered(1)),
                   pl.BlockSpec((s, wide), lambda g, n: (0, g), pipeline_mode=pl.Buffered(1))],
        out_shape=[full, full, full],
        scratch_shapes=[pltpu.VMEM((nh, nb, BLOCK, 1), F32)],
        compiler_params=_params(),
    )(qkv, qkv, qkv, do)


def _xa_tile(dot, q, kv, qg, kg):
    hd = q.shape[1] // X_HEADS
    outs = []
    for h in range(X_HEADS):
        qh = _rms(q[:, h * hd:(h + 1) * hd], qg)
        kh = _rms(kv[:, h * hd:(h + 1) * hd], kg)
        vh = kv[:, (X_HEADS + h) * hd:(X_HEADS + h + 1) * hd]
        sc = dot(qh, kh, True) * (hd ** -0.5)
        m = lax.stop_gradient(jnp.max(sc, axis=-1, keepdims=True))
        p = jnp.exp(sc - m)
        outs.append(dot(p / jnp.sum(p, axis=-1, keepdims=True), vh, False))
    return jnp.concatenate(outs, axis=1)


def xa_core_fwd(q, kv, qg, kg, name):
    n, d = q.shape
    tm = _tile(n, 256, 8)
    (o,) = tcall(lambda ids, qt, kvt, qgt, kgt: (_xa_tile(_plain_dot, qt, kvt, qgt, kgt),), (n // tm,),
                 [_row(q, tm), _full(kv), _full(qg), _full(kg)], [_row_out(n, d, BF16, tm)], name)
    return o


def xa_core_bwd(q, kv, qg, kg, do, name):
    n, d = q.shape
    tm = _tile(n, 256, 8)

    def fn(ids, qt, kvt, qgt, kgt, dot_):
        _, vjp = jax.vjp(functools.partial(_xa_tile, _dot_vjp), qt, kvt, qgt, kgt)
        return vjp(dot_.astype(F32))

    return tcall(fn, (n // tm,), [_row(q, tm), _full(kv), _full(qg), _full(kg), _row(do, tm)],
                 [_row_out(n, d, BF16, tm), _acc_out(kv.shape), _acc_out(qg.shape), _acc_out(kg.shape)], name)


def _ev_reorder(a):
    return jnp.concatenate([a[..., 0:512], a[..., 768:2304], a[..., 512:768]], axis=-1)


def _ev_restore(a):
    return jnp.concatenate([a[..., 0:512], a[..., 2048:2304], a[..., 512:2048]], axis=-1)


_EV_SEGS = ((0, 512, "q"), (512, 1536, "raw"), (2048, 128, "k"), (2176, 128, "raw"))
_A_CFG = dict(hkv=A_KV_HEADS, grp=A_Q_HEADS // A_KV_HEADS, max_dist=BLOCK - 1, step=1.0, slopes=_alibi(A_Q_HEADS),
              want_lse=False)
_A_COLS = (lambda r: 0, lambda r: 16, lambda r: 17)


def even_mixer_fwd(x, h, w_in, qg, kg, sinks, w_out, tag, side=None):
    qkv = mm(h, w_in, "nn", tag + "_in")
    (ops,) = prep_fwd(qkv, qg, kg, _EV_SEGS, (1,), tag + "_prep")
    (o_a,) = banded_fwd(ops, 1, _A_COLS, sinks, _A_CFG, tag + "_swa")
    o_b = sb_fwd(ops, 4, 8, 12, tag + "_sb", side=side)
    carried = None
    if side is not None:
        o_b, carried = o_b
    o = jnp.concatenate([o_a, o_b], axis=1)
    y = mm(o, w_out, "nn", tag + "_out", res=x)
    return y, (x, h, qkv, ops, o), carried


def even_mixer_bwd(dy, saved, g, w_in, qg, kg, sinks, w_out, tag, side=None):
    x, h, qkv, ops, o = saved
    do = mm(dy, w_out, "nt", tag + "_do")
    d_wout = mm(o, dy, "tn", tag + "_dwout")
    dqa, dkp, dkc, dvp, dvc, dsinks = banded_bwd(ops, 1, _A_COLS, sinks, _A_CFG, [(do, lambda r: 0)], tag + "_dswa")
    res = sb_bwd(ops, 4, 8, 12, do, 4, tag + "_dsb", side=side)
    carried = None
    if side is not None:
        res, carried = res
    dqb, dkb, dvb = res
    dqkv, dqg, dkg = prep_bwd(
        qkv, qg, kg, _EV_SEGS,
        [(dqa, 0, 1), (dqb, 0, 1), (dkb, 0, 1), (dvb, 0, 1), (dkc, 0, 1), (dkp, 1, 1), (dvc, 0, 1), (dvp, 1, 1)],
        lambda qa, qb, kb, vb, kc, kp, vc, vp: jnp.concatenate([qa, qb, kb, vb, kc + kp, vc + vp], axis=1),
        tag + "_dqkv")
    d_win = mm(h, dqkv, "tn", tag + "_dwin")
    dx, dg = mm_norm_bwd(dqkv, w_in, x, g, dy, tag + "_dh")
    return dx, dg, d_win, dqg, dkg, dsinks, d_wout, carried


def _c_cfg(window, dil):
    return dict(hkv=C_HEADS, grp=1, max_dist=window // dil, step=float(dil), slopes=_alibi(C_HEADS), want_lse=True)


_C_COLS = (lambda r: 3 * r, lambda r: 3 * r + 1, lambda r: 3 * r + 2)
_OD_SEGS = ((0, 1024, "q"), (1024, 1024, "k"), (2048, 1024, "raw"))


def _combine(o1, o2, o3, l1, l2, l3):
    m = lax.stop_gradient(jnp.maximum(jnp.maximum(l1, l2), l3))
    e1, e2, e3 = jnp.exp(l1 - m), jnp.exp(l2 - m), jnp.exp(l3 - m)
    tot = e1 + e2 + e3
    return (e1 / tot) * o1 + (e2 / tot) * o2 + (e3 / tot) * o3


def odd_mixer_fwd(x, g, w_in, qg, kg, w_out, tag):
    n, d = x.shape
    h = rmsnorm_fwd(x, g, tag + "_norm")
    qkv = mm(h, w_in, "nn", tag + "_in")
    dils = [dil for _, dil in C_PATTERNS]
    ops = prep_fwd(qkv, qg, kg, _OD_SEGS, dils, tag + "_prep")
    os_, ls_ = [], []
    for (window, dil), ops_d in zip(C_PATTERNS, ops):
        o_p, l_p = banded_fwd(ops_d, dil, _C_COLS, None, _c_cfg(window, dil), f"{tag}_dil{dil}")
        os_.append(o_p)
        ls_.append(l_p)
    tm = BLOCK
    lay = lambda a, dil: _in(a, (tm // dil, a.shape[1]), lambda i: (i, 0))
    views = [lay(a, dil) for a, dil in zip(os_ + ls_, dils + dils)]

    def comb(ids, *t, scratch):
        return (_combine(*[_to_natural(scratch, a, dil) for a, dil in zip(t, dils + dils)]),)

    (o,) = tcall(comb, (n // tm,), views, [_row_out(n, d, BF16, tm)], tag + "_comb",
                 scratch=((d // BLOCK * tm, BLOCK), F32))
    y = mm(o, w_out, "nn", tag + "_out", res=x)
    return y, (x, h, qkv, ops, views, o)


def odd_mixer_bwd(dy, saved, g, w_in, qg, kg, w_out, tag):
    x, h, qkv, ops, views, o = saved
    n, d = x.shape
    do = mm(dy, w_out, "nt", tag + "_do")
    d_wout = mm(o, dy, "tn", tag + "_dwout")
    tm = BLOCK
    dils = [dil for _, dil in C_PATTERNS]

    def comb_bwd(ids, *t, scratch):
        _, vjp = jax.vjp(_combine, *[_to_natural(scratch, a, dil) for a, dil in zip(t[:6], dils + dils)])
        return tuple(_to_strided(scratch, c, dil) for c, dil in zip(vjp(t[6]), dils + dils))

    cts = tcall(comb_bwd, (n // tm,), views + [_row(do, tm)],
                [_out((n // dil, dil * d), F32, (tm // dil, dil * d), lambda i: (i, 0)) for dil in dils + dils],
                tag + "_dcomb", scratch=((d // BLOCK * tm, BLOCK), F32))
    dqs, dks, dvs = [], [], []
    for p, ((window, dil), ops_d) in enumerate(zip(C_PATTERNS, ops)):
        dq, dkp, dkc, dvp, dvc = banded_bwd(ops_d, dil, _C_COLS, None, _c_cfg(window, dil),
                                            [(cts[p], lambda r: r), (cts[3 + p], lambda r: r)], f"{tag}_ddil{dil}")
        dqs.append((dq, 0, dil))
        dks += [(dkc, 0, dil), (dkp, dil, dil)]
        dvs += [(dvc, 0, dil), (dvp, dil, dil)]

    def gather(*t):
        total = lambda parts: functools.reduce(lambda a, b: a + b, parts)
        return jnp.concatenate([total(t[0:3]), total(t[3:9]), total(t[9:15])], axis=1)

    dqkv, dqg, dkg = prep_bwd(qkv, qg, kg, _OD_SEGS, dqs + dks + dvs, gather, tag + "_dqkv")
    d_win = mm(h, dqkv, "tn", tag + "_dwin")
    dx, dg = mm_norm_bwd(dqkv, w_in, x, g, dy, tag + "_dh")
    return dx, dg, d_win, dqg, dkg, d_wout


def xa_fwd(x, mem, g, gm, w_q, w_kv, qg, kg, w_o, tag):
    h = rmsnorm_fwd(x, g, tag + "_norm")
    q = mm(h, w_q, "nn", tag + "_q")
    mn = rmsnorm_fwd(mem, gm, tag + "_mnorm")
    kv = mm(mn, w_kv, "nn", tag + "_kv")
    o = xa_core_fwd(q, kv, qg, kg, tag + "_core")
    y = mm(o, w_o, "nn", tag + "_o", res=x)
    return y, (x, h, q, mn, kv, o)


def xa_bwd(dy, saved, mem, g, gm, w_q, w_kv, qg, kg, w_o, tag):
    x, h, q, mn, kv, o = saved
    do = mm(dy, w_o, "nt", tag + "_do", out_dtype=BF16)
    d_wo = mm(o, dy, "tn", tag + "_dwo")
    dq, dkv, dqg, dkg = xa_core_bwd(q, kv, qg, kg, do, tag + "_dcore")
    d_wq = mm(h, dq, "tn", tag + "_dwq")
    dx, dg = mm_norm_bwd(dq, w_q, x, g, dy, tag + "_dh")
    d_wkv = mm(mn, dkv, "tn", tag + "_dwkv")
    _, dgm = mm_norm_bwd(dkv, w_kv, mem, gm, None, tag + "_dmn")
    return dx, dg, dgm, d_wq, d_wkv, dqg, dkg, d_wo


def loss_head(y, target, name):
    n, d = y.shape
    tm = _tile(n, 512, 8)

    def fn(ids, yt, tt):
        e = yt - tt
        return e * (1.0 / d), jnp.sum(e * e, axis=0, keepdims=True)

    return tcall(fn, (n // tm,), [_row(y, tm), _row(target, tm)], [_row_out(n, d, F32, tm), _acc_out((1, d))], name)


_ANY = pl.BlockSpec(memory_space=pl.ANY)


def all_gather_blocks(blocks):
    nb = len(blocks)

    def body(*refs):
        x_refs, out_refs = refs[:nb], refs[nb:2 * nb]
        send_sems, recv_sems, local_sems = refs[2 * nb:]
        x, y, c = lax.axis_index("x"), lax.axis_index("y"), lax.axis_index("c")
        me, sibling = (x, y, c), (x, y, 1 - c)
        over_x, over_y, diagonal = (1 - x, y), (x, 1 - y), (1 - x, 1 - y)
        relay_of = ((1 - x) * (1 - c) + x * c, y * (1 - c) + (1 - y) * c)
        relay_to = (x * (1 - c) + (1 - x) * c, (1 - y) * (1 - c) + y * c)

        def copy(b, k, blk, to, own=False):
            px, py, pc = blk
            slot = out_refs[b].at[4 * px + 2 * py + pc]
            return pltpu.make_async_remote_copy(
                src_ref=x_refs[b] if own else slot, dst_ref=slot,
                send_sem=send_sems.at[7 * b + k], recv_sem=recv_sems.at[7 * b + k], device_id=to, device_id_type=MESH)

        mine = [pltpu.make_async_copy(x_refs[b], out_refs[b].at[4 * x + 2 * y + c], local_sems.at[b]) for b in range(nb)]
        for cp in mine:
            cp.start()
        sent = []
        for b in range(nb):
            sent += [copy(b, 0, me, sibling, own=True), copy(b, 1, me, (*over_x, c), own=True),
                     copy(b, 2, me, (*over_y, c), own=True)]
        for cp in sent:
            cp.start()
        for b in range(nb):
            copy(b, 1, (*over_x, c), me).wait_recv()
            copy(b, 2, (*over_y, c), me).wait_recv()
            later = [copy(b, 3, (*relay_of, c), (*relay_to, c)), copy(b, 4, (*over_x, c), sibling),
                     copy(b, 5, (*over_y, c), sibling)]
            for cp in later:
                cp.start()
            sent += later
        for b in range(nb):
            copy(b, 3, (*diagonal, c), me).wait_recv()
            fwd = copy(b, 6, (*diagonal, c), sibling)
            fwd.start()
            sent.append(fwd)
        for b in range(nb):
            copy(b, 0, sibling, me).wait_recv()
            for k, chip in ((4, over_x), (5, over_y), (6, diagonal)):
                copy(b, k, (*chip, 1 - c), me).wait_recv()
        for cp in sent:
            cp.wait_send()
        for cp in mine:
            cp.wait()

    return _pcall(
        body, name="weights_all_gather",
        in_specs=[_ANY] * nb, out_specs=[_ANY] * nb,
        out_shape=[jax.ShapeDtypeStruct((N_DEV,) + a.shape, a.dtype) for a in blocks],
        scratch_shapes=[pltpu.SemaphoreType.DMA((7 * nb,)), pltpu.SemaphoreType.DMA((7 * nb,)),
                        pltpu.SemaphoreType.DMA((nb,))],
    )(*blocks)


def pair_exchange(bufs):
    nb = len(bufs)

    def body(*refs):
        srcs, dsts = refs[:nb], refs[nb:2 * nb]
        send_sems, recv_sems = refs[2 * nb:]
        x, y, c = lax.axis_index("x"), lax.axis_index("y"), lax.axis_index("c")
        copies = []
        for b in range(nb):
            for j in range(4):
                cp = pltpu.make_async_remote_copy(
                    src_ref=srcs[b].at[2 * j + (1 - c)], dst_ref=dsts[b].at[j], send_sem=send_sems.at[4 * b + j],
                    recv_sem=recv_sems.at[4 * b + j], device_id=(x, y, 1 - c), device_id_type=MESH)
                cp.start()
                copies.append(cp)
        for cp in copies:
            cp.wait()

    return _pcall(
        body, name="grads_pair_exchange",
        in_specs=[_ANY] * nb, out_specs=[_ANY] * nb,
        out_shape=[jax.ShapeDtypeStruct((4,) + a.shape[1:], a.dtype) for a in bufs],
        scratch_shapes=[pltpu.SemaphoreType.DMA((4 * nb,)), pltpu.SemaphoreType.DMA((4 * nb,))],
    )(*bufs)


def pair_sum(g, got, c, out_dtype, name):
    r, w = g.shape[1:]
    tr = _tile(r, 512, 16)

    def body(c_ref, a_ref, b_ref, o_ref):
        o_ref[...] = (a_ref[...].astype(F32) + b_ref[...].astype(F32)).astype(o_ref.dtype)

    return _pcall(
        body, name=name,
        grid_spec=pltpu.PrefetchScalarGridSpec(
            num_scalar_prefetch=1, grid=(4, r // tr),
            in_specs=[pl.BlockSpec((None, tr, w), lambda j, i, c_ref: (2 * j + c_ref[0], i, 0)),
                      pl.BlockSpec((None, tr, w), lambda j, i, c_ref: (j, i, 0))],
            out_specs=pl.BlockSpec((None, tr, w), lambda j, i, c_ref: (j, i, 0))),
        out_shape=jax.ShapeDtypeStruct((4,) + g.shape[1:], out_dtype),
        compiler_params=_params(),
    )(c, g, got)


def chip_exchange(parts):
    nb = len(parts)

    def body(*refs):
        srcs, dsts = refs[:nb], refs[nb:2 * nb]
        send_sems, recv_sems, local_sems = refs[2 * nb:]
        x, y, c = lax.axis_index("x"), lax.axis_index("y"), lax.axis_index("c")
        my_chip = 2 * x + y
        copies = []
        for b in range(nb):
            mine = pltpu.make_async_copy(srcs[b].at[my_chip], dsts[b].at[my_chip], local_sems.at[b])
            mine.start()
            copies.append(mine)
            for k, (tx, ty) in enumerate([(1 - x, y), (x, 1 - y), (1 - x, 1 - y)]):
                cp = pltpu.make_async_remote_copy(
                    src_ref=srcs[b].at[2 * tx + ty], dst_ref=dsts[b].at[my_chip], send_sem=send_sems.at[3 * b + k],
                    recv_sem=recv_sems.at[3 * b + k], device_id=(tx, ty, c), device_id_type=MESH)
                cp.start()
                copies.append(cp)
        for cp in copies:
            cp.wait()

    return _pcall(
        body, name="grads_chip_exchange",
        in_specs=[_ANY] * nb, out_specs=[_ANY] * nb,
        out_shape=[jax.ShapeDtypeStruct(a.shape, a.dtype) for a in parts],
        scratch_shapes=[pltpu.SemaphoreType.DMA((3 * nb,)), pltpu.SemaphoreType.DMA((3 * nb,)),
                        pltpu.SemaphoreType.DMA((nb,))],
    )(*parts)


def chip_sum(parts, name):
    r, w = parts.shape[1:]
    tr = _tile(r, 512, 16)
    spec = lambda j: _in(parts, (None, tr, w), lambda i, j=j: (j, i, 0))

    def fn(ids, a, b, c_, d):
        a, b, c_, d = [t.astype(F32) for t in (a, b, c_, d)]
        return (((a + b) + c_) + d,)

    (out,) = tcall(fn, (r // tr,), [spec(j) for j in range(4)],
                   [_out((r, w), F32, (tr, w), lambda i: (i, 0))], name)
    return out


def _remote(src, dst, send_sems, recv_sems, k, to):
    return functools.partial(pltpu.make_async_remote_copy, src_ref=src, dst_ref=dst, send_sem=send_sems.at[k],
                             recv_sem=recv_sems.at[k], device_id=to, device_id_type=MESH)


def _gather_plan(phase, nb):
    def plan(ins, outs, send_sems, recv_sems, local_sems):
        x, y, c = lax.axis_index("x"), lax.axis_index("y"), lax.axis_index("c")
        me, sibling = (x, y, c), (x, y, 1 - c)
        over_x, over_y, diagonal = (1 - x, y), (x, 1 - y), (1 - x, 1 - y)
        relay_of = ((1 - x) * (1 - c) + x * c, y * (1 - c) + (1 - y) * c)
        relay_to = (x * (1 - c) + (1 - x) * c, (1 - y) * (1 - c) + y * c)
        local, sends, recvs = [], [], []
        for b in range(nb):
            slot = lambda chip, core, b=b: outs[b].at[4 * chip[0] + 2 * chip[1] + core]
            if phase == 0:
                local.append(functools.partial(pltpu.make_async_copy, ins[b], slot((x, y), c), local_sems.at[b]))
                moves = [(ins[b], slot((x, y), c), to) for to in (sibling, (*over_x, c), (*over_y, c))]
                arrive = [slot((x, y), 1 - c), slot(over_x, c), slot(over_y, c)]
            elif phase == 1:
                moves = [(slot(relay_of, c), slot(relay_of, c), (*relay_to, c)),
                         (slot(over_x, c), slot(over_x, c), sibling), (slot(over_y, c), slot(over_y, c), sibling)]
                arrive = [slot(diagonal, c), slot(over_x, 1 - c), slot(over_y, 1 - c)]
            else:
                moves = [(slot(diagonal, c), slot(diagonal, c), sibling)]
                arrive = [slot(diagonal, 1 - c)]
            sends += [_remote(src, dst, send_sems, recv_sems, 3 * b + k, to) for k, (src, dst, to) in enumerate(moves)]
            recvs += [_remote(dst, dst, send_sems, recv_sems, 3 * b + k, me) for k, dst in enumerate(arrive)]
        return local, sends, recvs
    return plan


def gather_side(phase, arrays):
    nb = len(arrays)
    if phase == 0:
        shapes = [jax.ShapeDtypeStruct((N_DEV,) + a.shape, a.dtype) for a in arrays]
        return Side(arrays, shapes, 3 * nb, nb, _gather_plan(0, nb))
    shapes = [jax.ShapeDtypeStruct(a.shape, a.dtype) for a in arrays]
    return Side(arrays, shapes, 3 * nb, 0, _gather_plan(phase, nb), aliased=True)


def pair_side(bufs):
    nb = len(bufs)

    def plan(ins, outs, send_sems, recv_sems, local_sems):
        x, y, c = lax.axis_index("x"), lax.axis_index("y"), lax.axis_index("c")
        sends = [_remote(ins[b].at[2 * j + (1 - c)], outs[b].at[j], send_sems, recv_sems, 4 * b + j, (x, y, 1 - c))
                 for b in range(nb) for j in range(4)]
        recvs = [_remote(outs[b].at[j], outs[b].at[j], send_sems, recv_sems, 4 * b + j, (x, y, c))
                 for b in range(nb) for j in range(4)]
        return [], sends, recvs

    shapes = [jax.ShapeDtypeStruct((4,) + a.shape[1:], a.dtype) for a in bufs]
    return Side(bufs, shapes, 4 * nb, 0, plan)


def chip_side(parts):
    nb = len(parts)

    def plan(ins, outs, send_sems, recv_sems, local_sems):
        x, y, c = lax.axis_index("x"), lax.axis_index("y"), lax.axis_index("c")
        my_chip = 2 * x + y
        peers = [(1 - x, y), (x, 1 - y), (1 - x, 1 - y)]
        local = [functools.partial(pltpu.make_async_copy, ins[b].at[my_chip], outs[b].at[my_chip], local_sems.at[b])
                 for b in range(nb)]
        sends = [_remote(ins[b].at[2 * tx + ty], outs[b].at[my_chip], send_sems, recv_sems, 3 * b + k, (tx, ty, c))
                 for b in range(nb) for k, (tx, ty) in enumerate(peers)]
        recvs = [_remote(outs[b].at[2 * tx + ty], outs[b].at[2 * tx + ty], send_sems, recv_sems, 3 * b + k, (x, y, c))
                 for b in range(nb) for k, (tx, ty) in enumerate(peers)]
        return local, sends, recvs

    shapes = [jax.ShapeDtypeStruct(a.shape, a.dtype) for a in parts]
    return Side(parts, shapes, 3 * nb, nb, plan)


def adamw(w, g, m, v, name):
    shape = w.shape
    cols = shape[-1]
    rows = int(np.prod(shape[:-1]))
    w2, g2, m2, v2 = [a.reshape(rows, cols) for a in (w, g, m, v)]
    tr = _tile(rows, 256, 8) if rows % 8 == 0 else rows

    def fn(ids, wt, gt, mt, vt):
        m_new = ADAM_B1 * mt + (1.0 - ADAM_B1) * gt
        v_new = ADAM_B2 * vt + (1.0 - ADAM_B2) * (gt * gt)
        m_hat = m_new / (1.0 - ADAM_B1 ** ADAM_STEP)
        v_hat = v_new / (1.0 - ADAM_B2 ** ADAM_STEP)
        delta = -ADAM_LR * (m_hat / (jnp.sqrt(v_hat) + ADAM_EPS) + ADAM_WD * wt)
        return delta, m_new, v_new

    res = tcall(fn, (rows // tr,), [_row(a, tr) for a in (w2, g2, m2, v2)],
                [_row_out(rows, cols, F32, tr) for _ in range(3)], name)
    return [a.reshape(shape) for a in res]


_MATS = [("ffn1_w_gu", "col"), ("ffn1_w_down", "row"), ("ev_w_in", "col"), ("ev_w_out", "row"),
         ("od_w_in", "col"), ("od_w_out", "row"), ("xa_w_q", "row"), ("xa_w_kv", "col"), ("xa_w_o", "row"),
         ("ffn2_w_gu", "col"), ("ffn2_w_down", "row")]
_VECS = ["ffn1_norm", "mix_norm", "ev_q_gain", "ev_k_gain", "ev_sinks", "od_q_gain", "od_k_gain", "xa_norm",
         "xa_mem_norm", "xa_q_gain", "xa_k_gain", "ffn2_norm"]
_WEIGHTS = ["ffn1_norm", "ffn1_w_gu", "ffn1_w_down", "mix_norm", "ev_w_in", "ev_q_gain", "ev_k_gain", "ev_sinks",
            "ev_w_out", "od_w_in", "od_q_gain", "od_k_gain", "od_w_out", "xa_norm", "xa_mem_norm", "xa_w_q", "xa_w_kv",
            "xa_q_gain", "xa_k_gain", "xa_w_o", "ffn2_norm", "ffn2_w_gu", "ffn2_w_down"]


_AXIS = dict(_MATS)
DEPTH = 2


def _layer_groups(l):
    w_in, w_out = ("ev_w_in", "ev_w_out") if l % 2 == 0 else ("od_w_in", "od_w_out")
    return [[("ffn1_w_gu", l), ("ffn2_w_gu", l)], [(w_in, l // 2)], [("xa_w_kv", l)],
            [("ffn1_w_down", l), ("ffn2_w_down", l), (w_out, l // 2), ("xa_w_q", l), ("xa_w_o", l)]]


def _first_block_groups(l):
    first = [[("ffn1_w_gu", l)], [("ffn1_w_down", l)]]
    rest = [[m for m in group if m[0] not in ("ffn1_w_gu", "ffn1_w_down")] for group in _layer_groups(l)]
    return first, rest


def _weight_blocks(shards, groups):
    blocks = []
    for group in groups:
        rows = [shards[n][j].astype(BF16) for n, j in group]
        blocks.append(rows[0] if len(rows) == 1 else jnp.concatenate(rows, axis=0))
    return blocks


def _whole_weights(shards, groups, gathered):
    full = {}
    for group, got in zip(groups, gathered):
        off = 0
        for n, j in group:
            a, b = shards[n].shape[1:]
            seg = got[:, off:off + a, :]
            off += a
            full[n] = seg.reshape(N_DEV * a, b) if _AXIS[n] == "row" else seg.transpose(1, 0, 2).reshape(a, N_DEV * b)
    return full


def _gradient_buffers(grads, groups):
    bufs = []
    for group in groups:
        rows = []
        for n, _ in group:
            a, b = grads[n].shape
            if _AXIS[n] == "row":
                rows.append(grads[n].reshape(N_DEV, a // N_DEV, b))
            else:
                rows.append(grads[n].reshape(a, N_DEV, b // N_DEV).transpose(1, 0, 2))
        bufs.append((rows[0] if len(rows) == 1 else jnp.concatenate(rows, axis=1)).astype(BF16))
    return bufs


def _gradient_blocks(shards, groups, sums):
    out = {}
    for group, tot in zip(groups, sums):
        off = 0
        for n, j in group:
            a = shards[n].shape[1]
            out[n, j] = tot[off:off + a]
            off += a
    return out


class _PairChain:
    def __init__(self, ex, bufs):
        self.ex, self.bufs, self.parts = ex, bufs, None

    def side(self, name):
        return pair_side(self.bufs) if name == "dwd" else None

    def done(self, name, carried):
        self.parts = self.ex.pair_sums(self.bufs, carried, "l1")


class _RestChain:
    HALF = {"dwd": (0, 2), "dwgu": (1, 3)}

    def __init__(self, ex, bufs):
        self.ex, self.bufs, self.parts, self.sums = ex, bufs, None, [None] * len(bufs)

    def side(self, name):
        if name == "da":
            return pair_side(self.bufs)
        return chip_side([self.parts[i] for i in self.HALF[name]])

    def done(self, name, carried):
        if name == "da":
            self.parts = self.ex.pair_sums(self.bufs, carried, "l0r")
        else:
            for i, tot in zip(self.HALF[name], self.ex.chip_sums(carried, "l0r_" + name)):
                self.sums[i] = tot


class _Exchange:
    def __init__(self, shards, c):
        self.shards, self.c = shards, c

    def weights_first(self):
        first, _ = _first_block_groups(0)
        return _whole_weights(self.shards, first, all_gather_blocks(_weight_blocks(self.shards, first)))

    def rest_blocks(self):
        return _weight_blocks(self.shards, _first_block_groups(0)[1])

    def weights_rest(self, gathered):
        return _whole_weights(self.shards, _first_block_groups(0)[1], gathered)

    def gather_start(self):
        return gather_side(0, _weight_blocks(self.shards, _layer_groups(1)))

    def weights_next(self, gathered):
        return _whole_weights(self.shards, _layer_groups(1), gathered)

    def chain_next(self, grads):
        return _PairChain(self, _gradient_buffers(grads, _layer_groups(1)))

    def chain_rest(self, grads):
        return _RestChain(self, _gradient_buffers(grads, _first_block_groups(0)[1]))

    def pair_sums(self, bufs, got, tag):
        return [pair_sum(b, g, self.c, b.dtype, f"grads_pair_sum_{tag}_{i}") for i, (b, g) in enumerate(zip(bufs, got))]

    def chip_sums(self, parts, tag):
        return [chip_sum(p, f"grads_chip_sum_{tag}_{i}") for i, p in enumerate(parts)]

    def finish(self, gm, gv, sums1, sums_rest):
        vecs = {n: jnp.concatenate(v, axis=0) for n, v in gv.items()}
        first, rest = _first_block_groups(0)
        bufs = _gradient_buffers(gm[0], first)
        vec = jnp.concatenate([vecs[n].reshape(-1) for n in _VECS])
        vec = jnp.pad(vec, (0, -vec.shape[0] % (16 * LANES)))
        bufs.append(jnp.broadcast_to(vec.reshape(1, -1, LANES), (N_DEV, vec.shape[0] // LANES, LANES)))
        parts = self.pair_sums(bufs, pair_exchange(bufs), "l0")
        sums0 = self.chip_sums(chip_exchange(parts), "l0")
        blocks = {**_gradient_blocks(self.shards, first, sums0[:-1]), **_gradient_blocks(self.shards, rest, sums_rest),
                  **_gradient_blocks(self.shards, _layer_groups(1), sums1)}
        out = {n: jnp.stack([blocks[n, j] for j in range(self.shards[n].shape[0])]) for n, _ in _MATS}
        flat, off = sums0[-1].reshape(-1), 0
        for n in _VECS:
            out[n] = flat[off:off + vecs[n].size].reshape(vecs[n].shape)
            off += vecs[n].size
        return out


class _NoExchange:
    def __init__(self, full):
        self.full = full

    def weights_first(self):
        return self.full[0]

    def rest_blocks(self):
        return None

    def gather_start(self):
        return None

    def weights_next(self, gathered):
        return self.full[1]

    def chain_next(self, grads):
        return None

    def chain_rest(self, grads):
        return None

    def finish(self, gm, gv, sums1, sums_rest):
        mats = {}
        for l in range(DEPTH):
            for group in _layer_groups(l):
                for n, j in group:
                    mats.setdefault(n, {})[j] = gm[l][n]
        mats = {n: jnp.stack([v[j] for j in sorted(v)]) for n, v in mats.items()}
        return mats, {n: jnp.concatenate(v, axis=0) for n, v in gv.items()}


def _local_step(x, mem, target, w, ex):
    assert w["ffn1_norm"].shape[0] == DEPTH
    row = lambda a, l: a[l:l + 1]
    full = [ex.weights_first(), None]
    saved = []
    for l in range(DEPTH):
        t, j, f = f"l{l}", l // 2, full[l]
        rest = ex.rest_blocks() if l == 0 else None
        if rest is None:
            x, s1 = ffn_fwd(x, row(w["ffn1_norm"], l), f["ffn1_w_gu"], f["ffn1_w_down"], t + "_ffn1")
        else:
            x, s1, rest = ffn_fwd(x, row(w["ffn1_norm"], l), f["ffn1_w_gu"], f["ffn1_w_down"], t + "_ffn1", (0, rest))
        relay = None
        if l % 2 == 0:
            h = rmsnorm_fwd(x, row(w["mix_norm"], l), t + "_ev_norm", None if rest is None else gather_side(2, rest))
            if rest is not None:
                h, rest = h
                f = full[l] = {**f, **ex.weights_rest(rest)}
            side = ex.gather_start() if l + 1 < DEPTH else None
            x, s2, relay = even_mixer_fwd(x, h, _ev_reorder(f["ev_w_in"]), row(w["ev_q_gain"], j),
                                          row(w["ev_k_gain"], j), row(w["ev_sinks"], j), f["ev_w_out"], t + "_ev", side)
        else:
            x, s2 = odd_mixer_fwd(x, row(w["mix_norm"], l), f["od_w_in"], row(w["od_q_gain"], j),
                                  row(w["od_k_gain"], j), f["od_w_out"], t + "_od")
        x, s3 = xa_fwd(x, mem, row(w["xa_norm"], l), row(w["xa_mem_norm"], l), f["xa_w_q"], f["xa_w_kv"],
                       row(w["xa_q_gain"], l), row(w["xa_k_gain"], l), f["xa_w_o"], t + "_xa")
        if relay is None:
            x, s4 = ffn_fwd(x, row(w["ffn2_norm"], l), f["ffn2_w_gu"], f["ffn2_w_down"], t + "_ffn2")
        else:
            x, s4, relay = ffn_fwd(x, row(w["ffn2_norm"], l), f["ffn2_w_gu"], f["ffn2_w_down"], t + "_ffn2", (1, relay))
        if l + 1 < DEPTH:
            full[l + 1] = ex.weights_next(relay)
        saved.append((s1, s2, s3, s4))
    dx, sq = loss_head(x, target, "loss_head")
    loss = 0.5 * jnp.sum(sq) / x.shape[1]

    gm = [dict() for _ in range(DEPTH)]
    gv = {n: [None] * w[n].shape[0] for n in _VECS}
    chain1 = chain0 = sums1 = None
    for l in reversed(range(DEPTH)):
        t, j, f = f"l{l}", l // 2, full[l]
        s1, s2, s3, s4 = saved[l]
        dx, gv["ffn2_norm"][l], gm[l]["ffn2_w_gu"], gm[l]["ffn2_w_down"] = ffn_bwd(
            dx, s4, row(w["ffn2_norm"], l), f["ffn2_w_gu"], f["ffn2_w_down"], t + "_ffn2", chain1 if l == 0 else None)
        parts = chain1.parts if l == 0 and chain1 is not None else None
        (dx, gv["xa_norm"][l], gv["xa_mem_norm"][l], gm[l]["xa_w_q"], gm[l]["xa_w_kv"], gv["xa_q_gain"][l],
         gv["xa_k_gain"][l], gm[l]["xa_w_o"]) = xa_bwd(
            dx, s3, mem, row(w["xa_norm"], l), row(w["xa_mem_norm"], l), f["xa_w_q"], f["xa_w_kv"],
            row(w["xa_q_gain"], l), row(w["xa_k_gain"], l), f["xa_w_o"], t + "_xa")
        if l % 2 == 0:
            (dx, gv["mix_norm"][l], d_win, gv["ev_q_gain"][j], gv["ev_k_gain"][j], gv["ev_sinks"][j],
             gm[l]["ev_w_out"], carried) = even_mixer_bwd(
                dx, s2, row(w["mix_norm"], l), _ev_reorder(f["ev_w_in"]), row(w["ev_q_gain"], j), row(w["ev_k_gain"], j),
                row(w["ev_sinks"], j), f["ev_w_out"], t + "_ev", None if parts is None else chip_side(parts))
            gm[l]["ev_w_in"] = _ev_restore(d_win)
            if carried is not None:
                sums1 = ex.chip_sums(carried, "l1")
        else:
            (dx, gv["mix_norm"][l], gm[l]["od_w_in"], gv["od_q_gain"][j], gv["od_k_gain"][j],
             gm[l]["od_w_out"]) = odd_mixer_bwd(
                dx, s2, row(w["mix_norm"], l), f["od_w_in"], row(w["od_q_gain"], j), row(w["od_k_gain"], j),
                f["od_w_out"], t + "_od")
        if l == 0:
            chain0 = ex.chain_rest(gm[l])
        dx, gv["ffn1_norm"][l], gm[l]["ffn1_w_gu"], gm[l]["ffn1_w_down"] = ffn_bwd(
            dx, s1, row(w["ffn1_norm"], l), f["ffn1_w_gu"], f["ffn1_w_down"], t + "_ffn1", chain0 if l == 0 else None)
        if l == 1:
            chain1 = ex.chain_next(gm[l])
    return loss, dx, ex.finish(gm, gv, sums1, None if chain0 is None else chain0.sums)


def kernel(x, mem, ffn1_norm, ffn1_w_gu, ffn1_w_down, mix_norm, ev_w_in, ev_q_gain, ev_k_gain, ev_sinks, ev_w_out, od_w_in, od_q_gain, od_k_gain, od_w_out, xa_norm, xa_mem_norm, xa_w_q, xa_w_kv, xa_q_gain, xa_k_gain, xa_w_o, ffn2_norm, ffn2_w_gu, ffn2_w_down, loss_target, m_ffn1_norm, m_ffn1_w_gu, m_ffn1_w_down, m_mix_norm, m_ev_w_in, m_ev_q_gain, m_ev_k_gain, m_ev_sinks, m_ev_w_out, m_od_w_in, m_od_q_gain, m_od_k_gain, m_od_w_out, m_xa_norm, m_xa_mem_norm, m_xa_w_q, m_xa_w_kv, m_xa_q_gain, m_xa_k_gain, m_xa_w_o, m_ffn2_norm, m_ffn2_w_gu, m_ffn2_w_down, v_ffn1_norm, v_ffn1_w_gu, v_ffn1_w_down, v_mix_norm, v_ev_w_in, v_ev_q_gain, v_ev_k_gain, v_ev_sinks, v_ev_w_out, v_od_w_in, v_od_q_gain, v_od_k_gain, v_od_w_out, v_xa_norm, v_xa_mem_norm, v_xa_w_q, v_xa_w_kv, v_xa_q_gain, v_xa_k_gain, v_xa_w_o, v_ffn2_norm, v_ffn2_w_gu, v_ffn2_w_down):
    w = dict(ffn1_norm=ffn1_norm, ffn1_w_gu=ffn1_w_gu, ffn1_w_down=ffn1_w_down, mix_norm=mix_norm, ev_w_in=ev_w_in, ev_q_gain=ev_q_gain, ev_k_gain=ev_k_gain, ev_sinks=ev_sinks, ev_w_out=ev_w_out, od_w_in=od_w_in, od_q_gain=od_q_gain, od_k_gain=od_k_gain, od_w_out=od_w_out, xa_norm=xa_norm, xa_mem_norm=xa_mem_norm, xa_w_q=xa_w_q, xa_w_kv=xa_w_kv, xa_q_gain=xa_q_gain, xa_k_gain=xa_k_gain, xa_w_o=xa_w_o, ffn2_norm=ffn2_norm, ffn2_w_gu=ffn2_w_gu, ffn2_w_down=ffn2_w_down)
    m = dict(ffn1_norm=m_ffn1_norm, ffn1_w_gu=m_ffn1_w_gu, ffn1_w_down=m_ffn1_w_down, mix_norm=m_mix_norm, ev_w_in=m_ev_w_in, ev_q_gain=m_ev_q_gain, ev_k_gain=m_ev_k_gain, ev_sinks=m_ev_sinks, ev_w_out=m_ev_w_out, od_w_in=m_od_w_in, od_q_gain=m_od_q_gain, od_k_gain=m_od_k_gain, od_w_out=m_od_w_out, xa_norm=m_xa_norm, xa_mem_norm=m_xa_mem_norm, xa_w_q=m_xa_w_q, xa_w_kv=m_xa_w_kv, xa_q_gain=m_xa_q_gain, xa_k_gain=m_xa_k_gain, xa_w_o=m_xa_w_o, ffn2_norm=m_ffn2_norm, ffn2_w_gu=m_ffn2_w_gu, ffn2_w_down=m_ffn2_w_down)
    v = dict(ffn1_norm=v_ffn1_norm, ffn1_w_gu=v_ffn1_w_gu, ffn1_w_down=v_ffn1_w_down, mix_norm=v_mix_norm, ev_w_in=v_ev_w_in, ev_q_gain=v_ev_q_gain, ev_k_gain=v_ev_k_gain, ev_sinks=v_ev_sinks, ev_w_out=v_ev_w_out, od_w_in=v_od_w_in, od_q_gain=v_od_q_gain, od_k_gain=v_od_k_gain, od_w_out=v_od_w_out, xa_norm=v_xa_norm, xa_mem_norm=v_xa_mem_norm, xa_w_q=v_xa_w_q, xa_w_kv=v_xa_w_kv, xa_q_gain=v_xa_q_gain, xa_k_gain=v_xa_k_gain, xa_w_o=v_xa_w_o, ffn2_norm=v_ffn2_norm, ffn2_w_gu=v_ffn2_w_gu, ffn2_w_down=v_ffn2_w_down)

    c = lax.axis_index("c").astype(jnp.int32).reshape(1)
    loss, dx, grads = _local_step(x[0], mem[0], loss_target[0], w, _Exchange(w, c))
    loss = lax.psum(loss, ("x", "y", "c"))

    delta, new_m, new_v = {}, {}, {}
    for n in _WEIGHTS:
        delta[n], new_m[n], new_v[n] = adamw(w[n], grads[n], m[n], v[n], "adamw_" + n)
    return (loss, dx[None], *[grads[n] for n in _WEIGHTS], *[delta[n] for n in _WEIGHTS],
            *[new_m[n] for n in _WEIGHTS], *[new_v[n] for n in _WEIGHTS])
```

```python
import functools

import numpy as np
import jax
import jax.numpy as jnp
from jax import lax
from jax.experimental import pallas as pl
from jax.experimental.pallas import tpu as pltpu

F32 = jnp.float32
BF16 = jnp.bfloat16
MESH = pl.DeviceIdType.MESH

HEAD_DIM = 64
BLOCK = 128
RMS_EPS = 1e-6
A_Q_HEADS, A_KV_HEADS = 8, 2
B_HEADS = 8
C_HEADS = 16
C_PATTERNS = ((128, 1), (512, 4), (2048, 16))
X_HEADS = 4
N_DEV = 8
LANES = 1024
VMEM_LIMIT_BYTES = 56 * 1024 * 1024
SB_SKIP_LOG = -110.0
NEG_BIG = -1e30

ADAM_LR, ADAM_B1, ADAM_B2, ADAM_EPS, ADAM_WD, ADAM_STEP = 0.001, 0.9, 0.999, 1e-08, 0.01, 10

NN = (((1,), (0,)), ((), ()))
NT = (((1,), (1,)), ((), ()))
TN = (((0,), (0,)), ((), ()))


class Side:
    def __init__(self, arrays, out_shapes, n_remote, n_local, plan, aliased=False):
        self.arrays, self.out_shapes, self.plan, self.aliased = list(arrays), list(out_shapes), plan, aliased
        self.sems = [pltpu.SemaphoreType.DMA((n_remote,)), pltpu.SemaphoreType.DMA((n_remote,)),
                     pltpu.SemaphoreType.DMA((max(n_local, 1),))]

    def start(self, ins, outs, sems):
        local, sends, _ = self.plan(ins, outs, *sems)
        for make in local + sends:
            make().start()

    def wait(self, ins, outs, sems):
        local, sends, recvs = self.plan(ins, outs, *sems)
        for make in sends:
            make().wait_send()
        for make in recvs:
            make().wait_recv()
        for make in local:
            make().wait()


def _pcall(body, side=None, **kw):
    if side is None:
        return pl.pallas_call(body, **kw)
    grid = kw["grid"]
    single = not isinstance(kw["out_specs"], (list, tuple))
    out_specs = [kw["out_specs"]] if single else list(kw["out_specs"])
    out_shape = [kw["out_shape"]] if single else list(kw["out_shape"])
    scratch = list(kw.get("scratch_shapes", []))
    n_in, n_out, n_scr, n_side = len(kw["in_specs"]), len(out_specs), len(scratch), len(side.arrays)
    n_sout = len(side.out_shapes)

    def hosted(*refs):
        ins, s_in = refs[:n_in], refs[n_in:n_in + n_side]
        outs = refs[n_in + n_side:n_in + n_side + n_out]
        s_out = refs[n_in + n_side + n_out:n_in + n_side + n_out + n_sout]
        rest = refs[n_in + n_side + n_out + n_sout:]
        scr, sems = rest[:n_scr], rest[n_scr:]
        first = last = None
        for a, size in enumerate(grid):
            f, l = pl.program_id(a) == 0, pl.program_id(a) == size - 1
            first = f if first is None else jnp.logical_and(first, f)
            last = l if last is None else jnp.logical_and(last, l)

        @pl.when(first)
        def _():
            side.start(s_in, s_out, sems)

        body(*ins, *outs, *scr)

        @pl.when(last)
        def _():
            side.wait(s_in, s_out, sems)

    any_space = pl.BlockSpec(memory_space=pl.ANY)
    kw2 = dict(kw)
    kw2.update(in_specs=list(kw["in_specs"]) + [any_space] * n_side, out_specs=out_specs + [any_space] * n_sout,
               out_shape=out_shape + side.out_shapes, scratch_shapes=scratch + side.sems)
    if side.aliased:
        kw2["input_output_aliases"] = {n_in + i: n_out + i for i in range(n_side)}
    call = pl.pallas_call(hosted, **kw2)

    def run(*args):
        res = call(*args, *side.arrays)
        return (res[0] if single else list(res[:n_out])), list(res[n_out:])

    return run


def _params(**kw):
    return pltpu.CompilerParams(vmem_limit_bytes=VMEM_LIMIT_BYTES, **kw)


def _tile(dim, cap, unit=128):
    if dim <= cap:
        return dim
    t = (cap // unit) * unit
    while t >= unit:
        if dim % t == 0:
            return t
        t -= unit
    raise ValueError(f"no tile for {dim} under {cap}")


def _dot(a, b, dims):
    return lax.dot_general(a.astype(BF16), b.astype(BF16), dims, preferred_element_type=F32)


@functools.partial(jax.custom_vjp, nondiff_argnums=(2,))
def _dot_vjp(a, b, nt):
    return _dot(a, b, NT if nt else NN)


def _dot_vjp_fwd(a, b, nt):
    return _dot(a, b, NT if nt else NN), (a.astype(BF16), b.astype(BF16))


def _dot_vjp_bwd(nt, res, g):
    a, b = res
    if nt:
        return _dot(g, b, NN), _dot(g, a, TN)
    return _dot(g, b, NT), _dot(a, g, TN)


_dot_vjp.defvjp(_dot_vjp_fwd, _dot_vjp_bwd)


def _plain_dot(a, b, nt):
    return _dot(a, b, NT if nt else NN)


def _split_dot(x, mat, terms=2):
    out, rem = None, x
    for t in range(terms):
        part = rem.astype(BF16)
        d = lax.dot_general(part, mat, NN, preferred_element_type=F32)
        out = d if out is None else out + d
        if t + 1 < terms:
            rem = rem - part.astype(F32)
    return out


@functools.partial(jax.custom_vjp, nondiff_argnums=(3,))
def _split_dot_vjp(x, mat, mat_t, terms):
    return _split_dot(x, mat, terms)


def _split_dot_vjp_fwd(x, mat, mat_t, terms):
    return _split_dot(x, mat, terms), mat_t


def _split_dot_vjp_bwd(terms, mat_t, g):
    return _split_dot(g, mat_t, terms), None, None


_split_dot_vjp.defvjp(_split_dot_vjp_fwd, _split_dot_vjp_bwd)


def _plain_split(x, mat, mat_t, terms):
    return _split_dot(x, mat, terms)


def _tri(after):
    j = lax.broadcasted_iota(jnp.int32, (BLOCK, BLOCK), 0)
    s = lax.broadcasted_iota(jnp.int32, (BLOCK, BLOCK), 1)
    return jnp.where(j > s if after else j < s, 1.0, 0.0).astype(BF16)


def _in(a, block, imap):
    return (a, block, imap)


def _out(shape, dtype, block, imap, acc=False):
    return (shape, dtype, block, imap, acc)


def tcall(fn, grid, ins, outs, name, scratch=None, side=None):
    nin = len(ins)
    nout = len(outs)
    ngrid = len(grid)

    def body(*refs):
        ids = tuple(pl.program_id(a) for a in range(ngrid))
        extra = {} if scratch is None else {"scratch": refs[nin + nout]}
        res = fn(ids, *[r[...] for r in refs[:nin]], **extra)
        first = ids[0] == 0
        for a in range(1, ngrid):
            first = jnp.logical_and(first, ids[a] == 0)
        for o_ref, r, spec in zip(refs[nin:nin + nout], res, outs):
            if spec[4]:
                @pl.when(first)
                def _(o_ref=o_ref):
                    o_ref[...] = jnp.zeros(o_ref.shape, o_ref.dtype)
                o_ref[...] += r.astype(o_ref.dtype)
            else:
                o_ref[...] = r.astype(o_ref.dtype)

    return _pcall(
        body, side=side, name=name, grid=grid,
        in_specs=[pl.BlockSpec(b, m) for (_, b, m) in ins],
        out_specs=[pl.BlockSpec(b, m) for (_, _, b, m, _) in outs],
        out_shape=[jax.ShapeDtypeStruct(s, d) for (s, d, _, _, _) in outs],
        scratch_shapes=[] if scratch is None else [pltpu.VMEM(*scratch)],
        compiler_params=_params(),
    )(*[a for (a, _, _) in ins])


def _to_strided(scr, nat, d):
    if d == 1:
        return nat
    t, w = nat.shape
    nc = w // BLOCK
    for c in range(nc):
        scr[c * t:(c + 1) * t, :] = nat[:, c * BLOCK:(c + 1) * BLOCK]
    return jnp.concatenate([scr[pl.ds(c * t + r, t // d, stride=d), :] for r in range(d) for c in range(nc)], axis=1)


def _to_natural(scr, st, d):
    if d == 1:
        return st.astype(F32)
    t, w = st.shape[0] * d, st.shape[1] // d
    nc = w // BLOCK
    st = st.astype(F32)
    for r in range(d):
        for c in range(nc):
            scr[pl.ds(c * t + r, t // d, stride=d), :] = st[:, r * w + c * BLOCK:r * w + (c + 1) * BLOCK]
    return jnp.concatenate([scr[c * t:(c + 1) * t, :] for c in range(nc)], axis=1)


def _row(a, tm, width=None, cb=0):
    width = a.shape[1] if width is None else width
    return _in(a, (tm, width), lambda i, cb=cb: (i, cb))


def _full(a):
    zeros = (0,) * a.ndim
    return _in(a, a.shape, lambda *ids: zeros)


def _row_out(n, width, dtype, tm):
    return _out((n, width), dtype, (tm, width), lambda i: (i, 0))


def _acc_out(shape):
    zeros = (0,) * len(shape)
    return _out(shape, F32, shape, lambda *ids: zeros, acc=True)


def mm(a, b, mode, name, *, out_dtype=F32, scale=1.0, res=None, side=None):
    if mode == "nn":
        (m, k), (k2, n) = a.shape, b.shape
    elif mode == "nt":
        (m, k), (n, k2) = a.shape, b.shape
    else:
        (k, m), (k2, n) = a.shape, b.shape
    assert k == k2, (a.shape, b.shape, mode)
    tm, tn, tk = _tile(m, 512), _tile(n, 1408), _tile(k, 1408)
    nk = k // tk
    dims = {"nn": NN, "nt": NT, "tn": TN}[mode]
    has_res = res is not None

    def body(*refs):
        if has_res:
            a_ref, b_ref, r_ref, o_ref, acc_ref = refs
        else:
            a_ref, b_ref, o_ref, acc_ref = refs
        kk = pl.program_id(2)

        @pl.when(kk == 0)
        def _():
            acc_ref[...] = jnp.zeros(acc_ref.shape, F32)

        acc_ref[...] += _dot(a_ref[...], b_ref[...], dims)

        @pl.when(kk == nk - 1)
        def _():
            out = acc_ref[...]
            if scale != 1.0:
                out = out * scale
            if has_res:
                out = out + r_ref[...]
            o_ref[...] = out.astype(o_ref.dtype)

    a_spec = (pl.BlockSpec((tk, tm), lambda i, j, kk: (kk, i)) if mode == "tn"
              else pl.BlockSpec((tm, tk), lambda i, j, kk: (i, kk)))
    b_spec = (pl.BlockSpec((tn, tk), lambda i, j, kk: (j, kk)) if mode == "nt"
              else pl.BlockSpec((tk, tn), lambda i, j, kk: (kk, j)))
    in_specs = [a_spec, b_spec]
    args = [a, b]
    if has_res:
        in_specs.append(pl.BlockSpec((tm, tn), lambda i, j, kk: (i, j)))
        args.append(res)
    order = ("parallel", "parallel", "arbitrary") if side is None else ("arbitrary",) * 3
    return _pcall(
        body, side=side, name=name, grid=(m // tm, n // tn, nk),
        in_specs=in_specs,
        out_specs=pl.BlockSpec((tm, tn), lambda i, j, kk: (i, j)),
        out_shape=jax.ShapeDtypeStruct((m, n), out_dtype),
        scratch_shapes=[pltpu.VMEM((tm, tn), F32)],
        compiler_params=_params(dimension_semantics=order),
    )(*args)


def _rms(x, g):
    return x * lax.rsqrt(jnp.mean(x * x, axis=-1, keepdims=True) + RMS_EPS) * g


def _silu_mul(gate, up):
    return gate / (1.0 + jnp.exp(-gate)) * up


def mm_gate_up(h, w_gu, name, side=None):
    m, k = h.shape
    f = w_gu.shape[1] // 2
    tm, tn, tk = _tile(m, 512), _tile(f, 1408), _tile(k, 1408)
    nk, nj = k // tk, f // tn

    def body(h_ref, wg_ref, wu_ref, g_ref, u_ref, a_ref, accg_ref, accu_ref):
        kk = pl.program_id(2)

        @pl.when(kk == 0)
        def _():
            accg_ref[...] = jnp.zeros(accg_ref.shape, F32)
            accu_ref[...] = jnp.zeros(accu_ref.shape, F32)

        ht = h_ref[...]
        accg_ref[...] += _dot(ht, wg_ref[...], NN)
        accu_ref[...] += _dot(ht, wu_ref[...], NN)

        @pl.when(kk == nk - 1)
        def _():
            gate, up = accg_ref[...], accu_ref[...]
            g_ref[...] = gate
            u_ref[...] = up
            a_ref[...] = _silu_mul(gate, up).astype(a_ref.dtype)

    tile = pl.BlockSpec((tm, tn), lambda i, j, kk: (i, j))
    return _pcall(
        body, side=side, name=name, grid=(m // tm, nj, nk),
        in_specs=[pl.BlockSpec((tm, tk), lambda i, j, kk: (i, kk)),
                  pl.BlockSpec((tk, tn), lambda i, j, kk: (kk, j)),
                  pl.BlockSpec((tk, tn), lambda i, j, kk: (kk, j + nj))],
        out_specs=[tile, tile, tile],
        out_shape=[jax.ShapeDtypeStruct((m, f), F32), jax.ShapeDtypeStruct((m, f), F32),
                   jax.ShapeDtypeStruct((m, f), BF16)],
        scratch_shapes=[pltpu.VMEM((tm, tn), F32), pltpu.VMEM((tm, tn), F32)],
        compiler_params=_params(dimension_semantics=("arbitrary",) * 3),
    )(h, w_gu, w_gu)


def mm_down_act_bwd(dy, w_down, gate, up, name, side=None):
    m, d = dy.shape
    f = w_down.shape[0]
    tm, tn = _tile(m, 512), _tile(f, 1408)
    assert d <= 1408

    def body(dy_ref, w_ref, g_ref, u_ref, dg_ref, du_ref):
        da = _dot(dy_ref[...], w_ref[...], NT) * 0.5
        _, vjp = jax.vjp(_silu_mul, g_ref[...], u_ref[...])
        dg, du = vjp(da)
        dg_ref[...] = dg.astype(dg_ref.dtype)
        du_ref[...] = du.astype(du_ref.dtype)

    tile = pl.BlockSpec((tm, tn), lambda i, j: (i, j))
    return _pcall(
        body, side=side, name=name, grid=(m // tm, f // tn),
        in_specs=[pl.BlockSpec((tm, d), lambda i, j: (i, 0)), pl.BlockSpec((tn, d), lambda i, j: (j, 0)), tile, tile],
        out_specs=[tile, tile],
        out_shape=[jax.ShapeDtypeStruct((m, f), BF16), jax.ShapeDtypeStruct((m, f), BF16)],
        compiler_params=_params(dimension_semantics=("arbitrary", "arbitrary")),
    )(dy, w_down, gate, up)


def mm_norm_bwd(a, b, x, g, dres, name):
    halves = isinstance(a, (tuple, list))
    a0, a1 = a if halves else (a, None)
    m, k = a0.shape[0], a0.shape[1] * (2 if halves else 1)
    d = b.shape[0]
    tm, tk = _tile(m, 512), _tile(a0.shape[1], 1408)
    nk = k // tk
    nkh = a0.shape[1] // tk
    has_res = dres is not None

    def body(*refs):
        a_ref, b_ref, x_ref, g_ref = refs[:4]
        rest = refs[4:-3]
        a1_ref = rest[0] if halves else None
        r_ref = rest[-1] if has_res else None
        dx_ref, dg_ref, acc_ref = refs[-3:]
        i, kk = pl.program_id(0), pl.program_id(1)

        @pl.when(kk == 0)
        def _():
            acc_ref[...] = jnp.zeros(acc_ref.shape, F32)

        if halves:
            @pl.when(kk < nkh)
            def _():
                acc_ref[...] += _dot(a_ref[...], b_ref[...], NT)

            @pl.when(kk >= nkh)
            def _():
                acc_ref[...] += _dot(a1_ref[...], b_ref[...], NT)
        else:
            acc_ref[...] += _dot(a_ref[...], b_ref[...], NT)

        @pl.when(kk == nk - 1)
        def _():
            _, vjp = jax.vjp(_rms, x_ref[...], g_ref[...])
            dx, dg = vjp(acc_ref[...])
            dx_ref[...] = dx + r_ref[...] if has_res else dx

            @pl.when(i == 0)
            def _():
                dg_ref[...] = jnp.zeros(dg_ref.shape, F32)

            dg_ref[...] += dg

    rows = pl.BlockSpec((tm, d), lambda i, kk: (i, 0))
    first = pl.BlockSpec((tm, tk), lambda i, kk: (i, jnp.minimum(kk, nkh - 1)))
    second = pl.BlockSpec((tm, tk), lambda i, kk: (i, jnp.maximum(kk - nkh, 0)))
    in_specs = ([first, pl.BlockSpec((d, tk), lambda i, kk: (0, kk)), rows, pl.BlockSpec(g.shape, lambda i, kk: (0, 0))]
                + ([second] if halves else []) + ([rows] if has_res else []))
    return _pcall(
        body, name=name, grid=(m // tm, nk),
        in_specs=in_specs,
        out_specs=[rows, pl.BlockSpec(g.shape, lambda i, kk: (0, 0))],
        out_shape=[jax.ShapeDtypeStruct((m, d), F32), jax.ShapeDtypeStruct(g.shape, F32)],
        scratch_shapes=[pltpu.VMEM((tm, d), F32)],
        compiler_params=_params(dimension_semantics=("arbitrary", "arbitrary")),
    )(*([a0, b, x, g] + ([a1] if halves else []) + ([dres] if has_res else [])))


def _indicator(shape, head_axis, mod):
    lane = lax.broadcasted_iota(jnp.int32, shape, head_axis)
    other = lax.broadcasted_iota(jnp.int32, shape, 1 - head_axis)
    lane = jnp.bitwise_and(lane, HEAD_DIM - 1) if mod else jnp.right_shift(lane, 6)
    return jnp.where(lane == other, 1.0, 0.0).astype(BF16)


def _head_rms(split, xs, g):
    w = xs.shape[1]
    to_head, from_head = _indicator((w, BLOCK), 0, False), _indicator((BLOCK, w), 1, False)
    to_lane, from_lane = _indicator((HEAD_DIM, w), 1, True), _indicator((w, HEAD_DIM), 0, True)
    ss = split(xs * xs, to_head, from_head, 3)
    r = lax.rsqrt(ss * (1.0 / HEAD_DIM) + RMS_EPS)
    g_all = split(jnp.broadcast_to(g, (8, HEAD_DIM)), to_lane, from_lane, 3)[0:1]
    return xs * split(r, from_head, to_head, 3) * g_all


def _prep(split, x, qg, kg, segs):
    parts = []
    for start, width, kind in segs:
        xs = x[:, start:start + width]
        parts.append(xs if kind == "raw" else _head_rms(split, xs, qg if kind == "q" else kg))
    return jnp.concatenate(parts, axis=1)


def prep_fwd(x, qg, kg, segs, dils, name):
    n, w = x.shape
    tm = _tile(n, 256, 8)

    def fn(ids, xt, a, b, scratch):
        ops = _prep(_plain_split, xt, a, b, segs)
        return tuple(_to_strided(scratch, ops, d) for d in dils)

    return tcall(fn, (n // tm,), [_row(x, tm), _full(qg), _full(kg)],
                 [_out((n // d, d * w), BF16, (tm // d, d * w), lambda i: (i, 0)) for d in dils], name,
                 scratch=((w // BLOCK * tm, BLOCK), F32))


def prep_bwd(x, qg, kg, segs, grads, gather, name):
    n, w = x.shape
    tm = BLOCK
    nblk = n // tm

    def fn(ids, xt, a, b, *t, scratch):
        t = [_to_natural(scratch, ti, d) for ti, (_, _, d) in zip(t, grads)]
        t = [jnp.where(ids[0] + sh < nblk, ti, 0.0) if sh else ti for ti, (_, sh, _) in zip(t, grads)]
        _, vjp = jax.vjp(lambda x_, a_, b_: _prep(_split_dot_vjp, x_, a_, b_, segs), xt, a, b)
        return vjp(gather(*t))

    specs = [_in(a, (tm // d, a.shape[1]), (lambda i, sh=sh: (jnp.minimum(i + sh, nblk - 1), 0))) for a, sh, d in grads]
    wmax = max(a.shape[1] // d for a, _, d in grads)
    return tcall(fn, (nblk,), [_row(x, tm), _full(qg), _full(kg)] + specs,
                 [_row_out(n, w, BF16, tm), _acc_out(qg.shape), _acc_out(kg.shape)], name,
                 scratch=((wmax // BLOCK * tm, BLOCK), F32))


def rmsnorm_fwd(x, g, name, side=None):
    n, d = x.shape
    tm = _tile(n, 512, 8)
    res = tcall(lambda ids, xt, gt: (_rms(xt, gt),), (n // tm,), [_row(x, tm), _full(g)],
                [_row_out(n, d, BF16, tm)], name, side=side)
    if side is None:
        return res[0]
    return res[0][0], res[1]


def ffn_fwd(x, g, w_gu, w_down, tag, carry=None):
    h = rmsnorm_fwd(x, g, tag + "_norm")
    if carry is None:
        gate, up, a = mm_gate_up(h, w_gu, tag + "_gu")
        return mm(a, w_down, "nn", tag + "_down", scale=0.5, res=x), (x, h, gate, up, a)
    phase, bufs = carry
    (gate, up, a), bufs = mm_gate_up(h, w_gu, tag + "_gu", side=gather_side(phase, bufs))
    y, bufs = mm(a, w_down, "nn", tag + "_down", scale=0.5, res=x, side=gather_side(phase + 1, bufs))
    return y, (x, h, gate, up, a), bufs


def ffn_bwd(dy, saved, g, w_gu, w_down, tag, chain=None):
    x, h, gate, up, a = saved

    def carrying(name, call, **kw):
        side = None if chain is None else chain.side(name)
        out = call(name=tag + "_" + name, side=side, **kw)
        if side is None:
            return out
        chain.done(name, out[1])
        return out[0]

    dgate, dup = carrying("da", mm_down_act_bwd, dy=dy, w_down=w_down, gate=gate, up=up)
    d_wdown = carrying("dwd", mm, a=a, b=dy, mode="tn", scale=0.5)
    d_wgu = (carrying("dwgu", mm, a=h, b=dgate, mode="tn"), mm(h, dup, "tn", tag + "_dwup"))
    dx, dg = mm_norm_bwd((dgate, dup), w_gu, x, g, dy, tag + "_dh")
    return dx, dg, d_wgu, d_wdown


def _alibi(n_heads):
    return [float(s) for s in np.asarray(2.0 ** (-8.0 * np.arange(1, n_heads + 1) / n_heads), dtype=np.float32)]


def _banded_tile(dot, first, q, kp, kc, vp, vc, sinks, *, hkv, grp, max_dist, step, slopes, want_lse):
    row = lax.broadcasted_iota(jnp.int32, (BLOCK, 2 * BLOCK), 0)
    col = lax.broadcasted_iota(jnp.int32, (BLOCK, 2 * BLOCK), 1)
    dist = row + BLOCK - col
    valid = (dist >= 0) & (dist <= max_dist) & ((col >= BLOCK) | jnp.logical_not(first))
    distf = dist.astype(F32)

    def head(hd, qh, k2, v2):
        s = dot(qh, k2, True) * (HEAD_DIM ** -0.5)
        s = jnp.where(valid, s - (slopes[hd] * step) * distf, NEG_BIG)
        m = jnp.max(s, axis=-1, keepdims=True)
        if sinks is not None:
            pick = lax.broadcasted_iota(jnp.int32, sinks.shape, 1) == hd
            sk = jnp.sum(jnp.where(pick, sinks, 0.0), axis=1, keepdims=True)
            m = jnp.maximum(m, sk)
        m = lax.stop_gradient(m)
        p = jnp.exp(s - m)
        denom = jnp.sum(p, axis=-1, keepdims=True)
        if sinks is not None:
            denom = denom + jnp.exp(sk - m)
        return dot(p / denom, v2, False), m + jnp.log(denom)

    outs, lses = [], []
    if grp == 1:
        low = lax.broadcasted_iota(jnp.int32, (BLOCK, BLOCK), 1) < HEAD_DIM
        for pr in range(hkv // 2):
            sl = slice(pr * BLOCK, (pr + 1) * BLOCK)
            q2 = q[:, sl]
            k2 = jnp.concatenate([kp[:, sl], kc[:, sl]], axis=0)
            v2 = jnp.concatenate([vp[:, sl], vc[:, sl]], axis=0)
            o0, l0 = head(2 * pr, jnp.where(low, q2, 0.0), k2, v2)
            o1, l1 = head(2 * pr + 1, jnp.where(low, 0.0, q2), k2, v2)
            outs.append(jnp.where(low, o0, o1))
            lses.append(jnp.where(low, l0, l1))
    else:
        for hk in range(hkv):
            sl = slice(hk * HEAD_DIM, (hk + 1) * HEAD_DIM)
            k2 = jnp.concatenate([kp[:, sl], kc[:, sl]], axis=0)
            v2 = jnp.concatenate([vp[:, sl], vc[:, sl]], axis=0)
            for gi in range(grp):
                hd = hk * grp + gi
                o_h, l_h = head(hd, q[:, hd * HEAD_DIM:(hd + 1) * HEAD_DIM], k2, v2)
                outs.append(o_h)
                lses.append(jnp.broadcast_to(l_h, (BLOCK, HEAD_DIM)))
    o = jnp.concatenate(outs, axis=1)
    if want_lse:
        return o, jnp.concatenate(lses, axis=1)
    return (o,)


def _banded_specs(view, qcol, kcol, vcol, wq, wkv):
    def at(colfn, prev):
        if prev:
            return lambda r, n: (jnp.maximum(n - 1, 0), colfn(r))
        return lambda r, n: (n, colfn(r))
    return [
        _in(view, (BLOCK, wq), at(qcol, False)),
        _in(view, (BLOCK, wkv), at(kcol, True)),
        _in(view, (BLOCK, wkv), at(kcol, False)),
        _in(view, (BLOCK, wkv), at(vcol, True)),
        _in(view, (BLOCK, wkv), at(vcol, False)),
    ]


def banded_fwd(view, dil, cols, sinks, cfg, name):
    ns = view.shape[0]
    nb = ns // BLOCK
    wq, wkv = cfg["hkv"] * cfg["grp"] * HEAD_DIM, cfg["hkv"] * HEAD_DIM
    has_sinks = sinks is not None

    def fn(ids, q, kp, kc, vp, vc, *rest):
        q, kp, kc, vp, vc = [a.astype(F32) for a in (q, kp, kc, vp, vc)]
        return _banded_tile(_plain_dot, ids[1] == 0, q, kp, kc, vp, vc, rest[0] if has_sinks else None, **cfg)

    ins = _banded_specs(view, *cols, wq, wkv) + ([_full(sinks)] if has_sinks else [])
    outs = [_out((ns, dil * wq), F32 if cfg["want_lse"] else BF16, (BLOCK, wq), lambda r, n: (n, r))]
    if cfg["want_lse"]:
        outs.append(_out((ns, dil * wq), F32, (BLOCK, wq), lambda r, n: (n, r)))
    return tcall(fn, (dil, nb), ins, outs, name)


def banded_bwd(view, dil, cols, sinks, cfg, cts, name):
    ns = view.shape[0]
    nb = ns // BLOCK
    wq, wkv = cfg["hkv"] * cfg["grp"] * HEAD_DIM, cfg["hkv"] * HEAD_DIM
    has_sinks = sinks is not None
    assert len(cts) == (2 if cfg["want_lse"] else 1)

    def fn(ids, q, kp, kc, vp, vc, *rest):
        sk = rest[0] if has_sinks else None
        ct = rest[1 if has_sinks else 0:]
        first = ids[1] == 0

        def f(q, kp, kc, vp, vc, *s):
            return _banded_tile(_dot_vjp, first, q, kp, kc, vp, vc, s[0] if has_sinks else None, **cfg)

        prim = tuple(a.astype(F32) for a in (q, kp, kc, vp, vc)) + ((sk,) if has_sinks else ())
        _, vjp = jax.vjp(f, *prim)
        return vjp(tuple(c.astype(F32) for c in ct))

    ins = (_banded_specs(view, *cols, wq, wkv) + ([_full(sinks)] if has_sinks else [])
           + [_in(a, (BLOCK, wq), (lambda r, n, cf=cf: (n, cf(r)))) for (a, cf) in cts])
    blk = lambda w: _out((ns, dil * w), F32, (BLOCK, w), lambda r, n: (n, r))
    outs = [blk(wq), blk(wkv), blk(wkv), blk(wkv), blk(wkv)]
    if has_sinks:
        outs.append(_acc_out(sinks.shape))
    return tcall(fn, (dil, nb), ins, outs, name)


def _log_sigmoid(z):
    return jnp.minimum(z, 0.0) - jnp.log(1.0 + jnp.exp(-jnp.abs(z)))


SB_PAIRS = 4


def _sb_pair(dot, suffix, qh, kb, vb, r_in, mask):
    z = dot(qh, kb, True) * (HEAD_DIM ** -0.5)
    lsp = _log_sigmoid(z)
    log_keep = jnp.where(mask, lsp - z, 0.0)
    log_after = suffix(log_keep) + r_in
    a = jnp.where(mask, jnp.exp(lsp + log_after), 0.0)
    return dot(a, vb, False), r_in + jnp.sum(log_keep, axis=1, keepdims=True)


def sb_fwd(qkv, qcb, kcb, vcb, name, side=None):
    s = qkv.shape[0]
    nb = s // BLOCK
    pairs = B_HEADS // 2
    wide = SB_PAIRS * BLOCK
    assert pairs % SB_PAIRS == 0 and qcb % SB_PAIRS == 0 and kcb % SB_PAIRS == 0 and vcb % SB_PAIRS == 0

    def body(q_ref, k_ref, v_ref, o_ref):
        n = pl.program_id(1)
        low = lax.broadcasted_iota(jnp.int32, (BLOCK, BLOCK), 1) < HEAD_DIM
        before = (lax.broadcasted_iota(jnp.int32, (2 * BLOCK, BLOCK), 1)
                  < jnp.bitwise_and(lax.broadcasted_iota(jnp.int32, (2 * BLOCK, BLOCK), 0), BLOCK - 1))
        after = _tri(True)
        suffix = lambda t: _split_dot(t, after)
        qs = []
        for p in range(SB_PAIRS):
            q2 = q_ref[:, p * BLOCK:(p + 1) * BLOCK].astype(F32)
            qs.append(jnp.concatenate([jnp.where(low, q2, 0.0), jnp.where(low, 0.0, q2)], axis=0))

        def cond(c):
            return jnp.logical_and(c[0] >= 0, c[1] > SB_SKIP_LOG)

        def step(c):
            kb, _, rs, accs = c
            rows = pl.ds(pl.multiple_of(kb * BLOCK, BLOCK), BLOCK)
            mask = jnp.logical_or(before, kb != n)
            new_r, new_acc, top = [], [], None
            for p in range(SB_PAIRS):
                cols = slice(p * BLOCK, (p + 1) * BLOCK)
                o_part, r_out = _sb_pair(_plain_dot, suffix, qs[p], k_ref[rows, cols], v_ref[rows, cols], rs[p], mask)
                new_r.append(r_out)
                new_acc.append(accs[p] + o_part)
                top = jnp.max(r_out) if top is None else jnp.maximum(top, jnp.max(r_out))
            return kb - 1, top, tuple(new_r), tuple(new_acc)

        init = (n, jnp.float32(0.0), tuple(jnp.zeros((2 * BLOCK, 1), F32) for _ in range(SB_PAIRS)),
                tuple(jnp.zeros((2 * BLOCK, BLOCK), F32) for _ in range(SB_PAIRS)))
        accs = lax.while_loop(cond, step, init)[3]
        for p in range(SB_PAIRS):
            o_ref[:, p * BLOCK:(p + 1) * BLOCK] = jnp.where(low, accs[p][:BLOCK], accs[p][BLOCK:]).astype(o_ref.dtype)

    return _pcall(
        body, side=side, name=name, grid=(pairs // SB_PAIRS, nb),
        in_specs=[pl.BlockSpec((BLOCK, wide), lambda g, n: (n, qcb // SB_PAIRS + g)),
                  pl.BlockSpec((s, wide), lambda g, n: (0, kcb // SB_PAIRS + g), pipeline_mode=pl.Buffered(1)),
                  pl.BlockSpec((s, wide), lambda g, n: (0, vcb // SB_PAIRS + g), pipeline_mode=pl.Buffered(1))],
        out_specs=pl.BlockSpec((BLOCK, wide), lambda g, n: (n, g)),
        out_shape=jax.ShapeDtypeStruct((s, pairs * BLOCK), BF16),
        compiler_params=_params(),
    )(qkv, qkv, qkv)


def sb_bwd(qkv, qcb, kcb, vcb, do, docb, name, side=None):
    s = qkv.shape[0]
    nb = s // BLOCK
    pairs = B_HEADS // 2
    wide = SB_PAIRS * BLOCK
    assert docb % SB_PAIRS == 0

    def body(q_ref, k_ref, v_ref, do_ref, dq_ref, dk_ref, dv_ref, r_ref):
        n = pl.program_id(1)

        @pl.when(n == 0)
        def _():
            dk_ref[...] = jnp.zeros(dk_ref.shape, F32)
            dv_ref[...] = jnp.zeros(dv_ref.shape, F32)

        low = lax.broadcasted_iota(jnp.int32, (BLOCK, BLOCK), 1) < HEAD_DIM
        before = (lax.broadcasted_iota(jnp.int32, (2 * BLOCK, BLOCK), 1)
                  < jnp.bitwise_and(lax.broadcasted_iota(jnp.int32, (2 * BLOCK, BLOCK), 0), BLOCK - 1))
        after, earlier = _tri(True), _tri(False)
        suffix = lambda t: _split_dot_vjp(t, after, earlier, 2)
        stack = lambda t: jnp.concatenate([jnp.where(low, t, 0.0), jnp.where(low, 0.0, t)], axis=0)
        qs = [stack(q_ref[:, p * BLOCK:(p + 1) * BLOCK].astype(F32)) for p in range(SB_PAIRS)]
        dos = [stack(do_ref[:, p * BLOCK:(p + 1) * BLOCK].astype(F32)) for p in range(SB_PAIRS)]

        def cond(c):
            return jnp.logical_and(c[0] >= 0, c[1] > SB_SKIP_LOG)

        def down(c):
            kb, _, rs = c
            rows = pl.ds(pl.multiple_of(kb * BLOCK, BLOCK), BLOCK)
            mask = jnp.logical_or(before, kb != n)
            new_r, top = [], None
            for h in range(SB_PAIRS):
                cols = slice(h * BLOCK, (h + 1) * BLOCK)
                r_ref[h, kb] = rs[h]
                z = _dot(qs[h], k_ref[rows, cols], NT) * (HEAD_DIM ** -0.5)
                log_keep = jnp.where(mask, _log_sigmoid(z) - z, 0.0)
                r_out = rs[h] + jnp.sum(log_keep, axis=1, keepdims=True)
                new_r.append(r_out)
                top = jnp.max(r_out) if top is None else jnp.maximum(top, jnp.max(r_out))
            return kb - 1, top, tuple(new_r)

        init = (n, jnp.float32(0.0), tuple(jnp.zeros((2 * BLOCK, 1), F32) for _ in range(SB_PAIRS)))
        last = lax.while_loop(cond, down, init)[0] + 1

        def up(kb, c):
            dqs, g_rs = c
            rows = pl.ds(pl.multiple_of(kb * BLOCK, BLOCK), BLOCK)
            mask = jnp.logical_or(before, kb != n)
            new_dq, new_g = [], []
            for h in range(SB_PAIRS):
                cols = slice(h * BLOCK, (h + 1) * BLOCK)
                _, vjp = jax.vjp(lambda q_, k_, v_, r_: _sb_pair(_dot_vjp, suffix, q_, k_, v_, r_, mask),
                                 qs[h], k_ref[rows, cols].astype(F32), v_ref[rows, cols].astype(F32), r_ref[h, kb])
                dq_c, dk_c, dv_c, g_in = vjp((dos[h], g_rs[h]))
                dk_ref[rows, cols] += dk_c
                dv_ref[rows, cols] += dv_c
                new_dq.append(dqs[h] + dq_c)
                new_g.append(g_in)
            return tuple(new_dq), tuple(new_g)

        init = (tuple(jnp.zeros((2 * BLOCK, BLOCK), F32) for _ in range(SB_PAIRS)),
                tuple(jnp.zeros((2 * BLOCK, 1), F32) for _ in range(SB_PAIRS)))
        dqs = lax.fori_loop(last, n + 1, up, init)[0]
        for p in range(SB_PAIRS):
            dq_ref[:, p * BLOCK:(p + 1) * BLOCK] = jnp.where(low, dqs[p][:BLOCK], dqs[p][BLOCK:])

    full = jax.ShapeDtypeStruct((s, pairs * BLOCK), F32)
    return _pcall(
        body, side=side, name=name, grid=(pairs // SB_PAIRS, nb),
        in_specs=[pl.BlockSpec((BLOCK, wide), lambda g, n: (n, qcb // SB_PAIRS + g)),
                  pl.BlockSpec((s, wide), lambda g, n: (0, kcb // SB_PAIRS + g), pipeline_mode=pl.Buffered(1)),
                  pl.BlockSpec((s, wide), lambda g, n: (0, vcb // SB_PAIRS + g), pipeline_mode=pl.Buffered(1)),
                  pl.BlockSpec((BLOCK, wide), lambda g, n: (n, docb // SB_PAIRS + g))],
        out_specs=[pl.BlockSpec((BLOCK, wide), lambda g, n: (n, g)),
                   pl.BlockSpec((s, wide), lambda g, n: (0, g), pipeline_mode=pl.Buffered(1)),
                   pl.BlockSpec((s, wide), lambda g, n: (0, g), pipeline_mode=pl.Buffered(1))],
        out_shape=[full, full, full],
        scratch_shapes=[pltpu.VMEM((SB_PAIRS, nb, 2 * BLOCK, 1), F32)],
        compiler_params=_params(),
    )(qkv, qkv, qkv, do)


def _xa_tile(dot, q, kv, qg, kg):
    hd = q.shape[1] // X_HEADS
    outs = []
    for h in range(X_HEADS):
        qh = _rms(q[:, h * hd:(h + 1) * hd], qg)
        kh = _rms(kv[:, h * hd:(h + 1) * hd], kg)
        vh = kv[:, (X_HEADS + h) * hd:(X_HEADS + h + 1) * hd]
        sc = dot(qh, kh, True) * (hd ** -0.5)
        m = lax.stop_gradient(jnp.max(sc, axis=-1, keepdims=True))
        p = jnp.exp(sc - m)
        outs.append(dot(p / jnp.sum(p, axis=-1, keepdims=True), vh, False))
    return jnp.concatenate(outs, axis=1)


def xa_core_fwd(q, kv, qg, kg, name):
    n, d = q.shape
    tm = _tile(n, 256, 8)
    (o,) = tcall(lambda ids, qt, kvt, qgt, kgt: (_xa_tile(_plain_dot, qt, kvt, qgt, kgt),), (n // tm,),
                 [_row(q, tm), _full(kv), _full(qg), _full(kg)], [_row_out(n, d, BF16, tm)], name)
    return o


def xa_core_bwd(q, kv, qg, kg, do, name):
    n, d = q.shape
    tm = _tile(n, 256, 8)

    def fn(ids, qt, kvt, qgt, kgt, dot_):
        _, vjp = jax.vjp(functools.partial(_xa_tile, _dot_vjp), qt, kvt, qgt, kgt)
        return vjp(dot_.astype(F32))

    return tcall(fn, (n // tm,), [_row(q, tm), _full(kv), _full(qg), _full(kg), _row(do, tm)],
                 [_row_out(n, d, BF16, tm), _acc_out(kv.shape), _acc_out(qg.shape), _acc_out(kg.shape)], name)


def _ev_reorder(a):
    return jnp.concatenate([a[..., 0:512], a[..., 768:2304], a[..., 512:768]], axis=-1)


def _ev_restore(a):
    return jnp.concatenate([a[..., 0:512], a[..., 2048:2304], a[..., 512:2048]], axis=-1)


_EV_SEGS = ((0, 512, "q"), (512, 1536, "raw"), (2048, 128, "k"), (2176, 128, "raw"))
_A_CFG = dict(hkv=A_KV_HEADS, grp=A_Q_HEADS // A_KV_HEADS, max_dist=BLOCK - 1, step=1.0, slopes=_alibi(A_Q_HEADS),
              want_lse=False)
_A_COLS = (lambda r: 0, lambda r: 16, lambda r: 17)


def even_mixer_fwd(x, h, w_in, qg, kg, sinks, w_out, tag, side=None):
    qkv = mm(h, w_in, "nn", tag + "_in")
    (ops,) = prep_fwd(qkv, qg, kg, _EV_SEGS, (1,), tag + "_prep")
    (o_a,) = banded_fwd(ops, 1, _A_COLS, sinks, _A_CFG, tag + "_swa")
    o_b = sb_fwd(ops, 4, 8, 12, tag + "_sb", side=side)
    carried = None
    if side is not None:
        o_b, carried = o_b
    o = jnp.concatenate([o_a, o_b], axis=1)
    y = mm(o, w_out, "nn", tag + "_out", res=x)
    return y, (x, h, qkv, ops, o), carried


def even_mixer_bwd(dy, saved, g, w_in, qg, kg, sinks, w_out, tag, side=None):
    x, h, qkv, ops, o = saved
    do = mm(dy, w_out, "nt", tag + "_do")
    d_wout = mm(o, dy, "tn", tag + "_dwout")
    dqa, dkp, dkc, dvp, dvc, dsinks = banded_bwd(ops, 1, _A_COLS, sinks, _A_CFG, [(do, lambda r: 0)], tag + "_dswa")
    res = sb_bwd(ops, 4, 8, 12, do, 4, tag + "_dsb", side=side)
    carried = None
    if side is not None:
        res, carried = res
    dqb, dkb, dvb = res
    dqkv, dqg, dkg = prep_bwd(
        qkv, qg, kg, _EV_SEGS,
        [(dqa, 0, 1), (dqb, 0, 1), (dkb, 0, 1), (dvb, 0, 1), (dkc, 0, 1), (dkp, 1, 1), (dvc, 0, 1), (dvp, 1, 1)],
        lambda qa, qb, kb, vb, kc, kp, vc, vp: jnp.concatenate([qa, qb, kb, vb, kc + kp, vc + vp], axis=1),
        tag + "_dqkv")
    d_win = mm(h, dqkv, "tn", tag + "_dwin")
    dx, dg = mm_norm_bwd(dqkv, w_in, x, g, dy, tag + "_dh")
    return dx, dg, d_win, dqg, dkg, dsinks, d_wout, carried


def _c_cfg(window, dil):
    return dict(hkv=C_HEADS, grp=1, max_dist=window // dil, step=float(dil), slopes=_alibi(C_HEADS), want_lse=True)


_C_COLS = (lambda r: 3 * r, lambda r: 3 * r + 1, lambda r: 3 * r + 2)
_OD_SEGS = ((0, 1024, "q"), (1024, 1024, "k"), (2048, 1024, "raw"))


def _combine(o1, o2, o3, l1, l2, l3):
    m = lax.stop_gradient(jnp.maximum(jnp.maximum(l1, l2), l3))
    e1, e2, e3 = jnp.exp(l1 - m), jnp.exp(l2 - m), jnp.exp(l3 - m)
    tot = e1 + e2 + e3
    return (e1 / tot) * o1 + (e2 / tot) * o2 + (e3 / tot) * o3


def odd_mixer_fwd(x, g, w_in, qg, kg, w_out, tag):
    n, d = x.shape
    h = rmsnorm_fwd(x, g, tag + "_norm")
    qkv = mm(h, w_in, "nn", tag + "_in")
    dils = [dil for _, dil in C_PATTERNS]
    ops = prep_fwd(qkv, qg, kg, _OD_SEGS, dils, tag + "_prep")
    os_, ls_ = [], []
    for (window, dil), ops_d in zip(C_PATTERNS, ops):
        o_p, l_p = banded_fwd(ops_d, dil, _C_COLS, None, _c_cfg(window, dil), f"{tag}_dil{dil}")
        os_.append(o_p)
        ls_.append(l_p)
    tm = BLOCK
    lay = lambda a, dil: _in(a, (tm // dil, a.shape[1]), lambda i: (i, 0))
    views = [lay(a, dil) for a, dil in zip(os_ + ls_, dils + dils)]

    def comb(ids, *t, scratch):
        return (_combine(*[_to_natural(scratch, a, dil) for a, dil in zip(t, dils + dils)]),)

    (o,) = tcall(comb, (n // tm,), views, [_row_out(n, d, BF16, tm)], tag + "_comb",
                 scratch=((d // BLOCK * tm, BLOCK), F32))
    y = mm(o, w_out, "nn", tag + "_out", res=x)
    return y, (x, h, qkv, ops, views, o)


def odd_mixer_bwd(dy, saved, g, w_in, qg, kg, w_out, tag):
    x, h, qkv, ops, views, o = saved
    n, d = x.shape
    do = mm(dy, w_out, "nt", tag + "_do")
    d_wout = mm(o, dy, "tn", tag + "_dwout")
    tm = BLOCK
    dils = [dil for _, dil in C_PATTERNS]

    def comb_bwd(ids, *t, scratch):
        _, vjp = jax.vjp(_combine, *[_to_natural(scratch, a, dil) for a, dil in zip(t[:6], dils + dils)])
        return tuple(_to_strided(scratch, c, dil) for c, dil in zip(vjp(t[6]), dils + dils))

    cts = tcall(comb_bwd, (n // tm,), views + [_row(do, tm)],
                [_out((n // dil, dil * d), F32, (tm // dil, dil * d), lambda i: (i, 0)) for dil in dils + dils],
                tag + "_dcomb", scratch=((d // BLOCK * tm, BLOCK), F32))
    dqs, dks, dvs = [], [], []
    for p, ((window, dil), ops_d) in enumerate(zip(C_PATTERNS, ops)):
        dq, dkp, dkc, dvp, dvc = banded_bwd(ops_d, dil, _C_COLS, None, _c_cfg(window, dil),
                                            [(cts[p], lambda r: r), (cts[3 + p], lambda r: r)], f"{tag}_ddil{dil}")
        dqs.append((dq, 0, dil))
        dks += [(dkc, 0, dil), (dkp, dil, dil)]
        dvs += [(dvc, 0, dil), (dvp, dil, dil)]

    def gather(*t):
        total = lambda parts: functools.reduce(lambda a, b: a + b, parts)
        return jnp.concatenate([total(t[0:3]), total(t[3:9]), total(t[9:15])], axis=1)

    dqkv, dqg, dkg = prep_bwd(qkv, qg, kg, _OD_SEGS, dqs + dks + dvs, gather, tag + "_dqkv")
    d_win = mm(h, dqkv, "tn", tag + "_dwin")
    dx, dg = mm_norm_bwd(dqkv, w_in, x, g, dy, tag + "_dh")
    return dx, dg, d_win, dqg, dkg, d_wout


def xa_fwd(x, mem, g, gm, w_q, w_kv, qg, kg, w_o, tag):
    h = rmsnorm_fwd(x, g, tag + "_norm")
    q = mm(h, w_q, "nn", tag + "_q")
    mn = rmsnorm_fwd(mem, gm, tag + "_mnorm")
    kv = mm(mn, w_kv, "nn", tag + "_kv")
    o = xa_core_fwd(q, kv, qg, kg, tag + "_core")
    y = mm(o, w_o, "nn", tag + "_o", res=x)
    return y, (x, h, q, mn, kv, o)


def xa_bwd(dy, saved, mem, g, gm, w_q, w_kv, qg, kg, w_o, tag):
    x, h, q, mn, kv, o = saved
    do = mm(dy, w_o, "nt", tag + "_do", out_dtype=BF16)
    d_wo = mm(o, dy, "tn", tag + "_dwo")
    dq, dkv, dqg, dkg = xa_core_bwd(q, kv, qg, kg, do, tag + "_dcore")
    d_wq = mm(h, dq, "tn", tag + "_dwq")
    dx, dg = mm_norm_bwd(dq, w_q, x, g, dy, tag + "_dh")
    d_wkv = mm(mn, dkv, "tn", tag + "_dwkv")
    _, dgm = mm_norm_bwd(dkv, w_kv, mem, gm, None, tag + "_dmn")
    return dx, dg, dgm, d_wq, d_wkv, dqg, dkg, d_wo


def loss_head(y, target, name):
    n, d = y.shape
    tm = _tile(n, 512, 8)

    def fn(ids, yt, tt):
        e = yt - tt
        return e * (1.0 / d), jnp.sum(e * e, axis=0, keepdims=True)

    return tcall(fn, (n // tm,), [_row(y, tm), _row(target, tm)], [_row_out(n, d, F32, tm), _acc_out((1, d))], name)


_ANY = pl.BlockSpec(memory_space=pl.ANY)


def all_gather_blocks(blocks):
    nb = len(blocks)

    def body(*refs):
        x_refs, out_refs = refs[:nb], refs[nb:2 * nb]
        send_sems, recv_sems, local_sems = refs[2 * nb:]
        x, y, c = lax.axis_index("x"), lax.axis_index("y"), lax.axis_index("c")
        me, sibling = (x, y, c), (x, y, 1 - c)
        over_x, over_y, diagonal = (1 - x, y), (x, 1 - y), (1 - x, 1 - y)
        relay_of = ((1 - x) * (1 - c) + x * c, y * (1 - c) + (1 - y) * c)
        relay_to = (x * (1 - c) + (1 - x) * c, (1 - y) * (1 - c) + y * c)

        def copy(b, k, blk, to, own=False):
            px, py, pc = blk
            slot = out_refs[b].at[4 * px + 2 * py + pc]
            return pltpu.make_async_remote_copy(
                src_ref=x_refs[b] if own else slot, dst_ref=slot,
                send_sem=send_sems.at[7 * b + k], recv_sem=recv_sems.at[7 * b + k], device_id=to, device_id_type=MESH)

        mine = [pltpu.make_async_copy(x_refs[b], out_refs[b].at[4 * x + 2 * y + c], local_sems.at[b]) for b in range(nb)]
        for cp in mine:
            cp.start()
        sent = []
        for b in range(nb):
            sent += [copy(b, 0, me, sibling, own=True), copy(b, 1, me, (*over_x, c), own=True),
                     copy(b, 2, me, (*over_y, c), own=True)]
        for cp in sent:
            cp.start()
        for b in range(nb):
            copy(b, 1, (*over_x, c), me).wait_recv()
            copy(b, 2, (*over_y, c), me).wait_recv()
            later = [copy(b, 3, (*relay_of, c), (*relay_to, c)), copy(b, 4, (*over_x, c), sibling),
                     copy(b, 5, (*over_y, c), sibling)]
            for cp in later:
                cp.start()
            sent += later
        for b in range(nb):
            copy(b, 3, (*diagonal, c), me).wait_recv()
            fwd = copy(b, 6, (*diagonal, c), sibling)
            fwd.start()
            sent.append(fwd)
        for b in range(nb):
            copy(b, 0, sibling, me).wait_recv()
            for k, chip in ((4, over_x), (5, over_y), (6, diagonal)):
                copy(b, k, (*chip, 1 - c), me).wait_recv()
        for cp in sent:
            cp.wait_send()
        for cp in mine:
            cp.wait()

    return _pcall(
        body, name="weights_all_gather",
        in_specs=[_ANY] * nb, out_specs=[_ANY] * nb,
        out_shape=[jax.ShapeDtypeStruct((N_DEV,) + a.shape, a.dtype) for a in blocks],
        scratch_shapes=[pltpu.SemaphoreType.DMA((7 * nb,)), pltpu.SemaphoreType.DMA((7 * nb,)),
                        pltpu.SemaphoreType.DMA((nb,))],
    )(*blocks)


def pair_exchange(bufs):
    nb = len(bufs)

    def body(*refs):
        srcs, dsts = refs[:nb], refs[nb:2 * nb]
        send_sems, recv_sems = refs[2 * nb:]
        x, y, c = lax.axis_index("x"), lax.axis_index("y"), lax.axis_index("c")
        copies = []
        for b in range(nb):
            for j in range(4):
                cp = pltpu.make_async_remote_copy(
                    src_ref=srcs[b].at[2 * j + (1 - c)], dst_ref=dsts[b].at[j], send_sem=send_sems.at[4 * b + j],
                    recv_sem=recv_sems.at[4 * b + j], device_id=(x, y, 1 - c), device_id_type=MESH)
                cp.start()
                copies.append(cp)
        for cp in copies:
            cp.wait()

    return _pcall(
        body, name="grads_pair_exchange",
        in_specs=[_ANY] * nb, out_specs=[_ANY] * nb,
        out_shape=[jax.ShapeDtypeStruct((4,) + a.shape[1:], a.dtype) for a in bufs],
        scratch_shapes=[pltpu.SemaphoreType.DMA((4 * nb,)), pltpu.SemaphoreType.DMA((4 * nb,))],
    )(*bufs)


def pair_sum(g, got, c, out_dtype, name):
    r, w = g.shape[1:]
    tr = _tile(r, 512, 16)

    def body(c_ref, a_ref, b_ref, o_ref):
        o_ref[...] = (a_ref[...].astype(F32) + b_ref[...].astype(F32)).astype(o_ref.dtype)

    return _pcall(
        body, name=name,
        grid_spec=pltpu.PrefetchScalarGridSpec(
            num_scalar_prefetch=1, grid=(4, r // tr),
            in_specs=[pl.BlockSpec((None, tr, w), lambda j, i, c_ref: (2 * j + c_ref[0], i, 0)),
                      pl.BlockSpec((None, tr, w), lambda j, i, c_ref: (j, i, 0))],
            out_specs=pl.BlockSpec((None, tr, w), lambda j, i, c_ref: (j, i, 0))),
        out_shape=jax.ShapeDtypeStruct((4,) + g.shape[1:], out_dtype),
        compiler_params=_params(),
    )(c, g, got)


def chip_exchange(parts):
    nb = len(parts)

    def body(*refs):
        srcs, dsts = refs[:nb], refs[nb:2 * nb]
        send_sems, recv_sems, local_sems = refs[2 * nb:]
        x, y, c = lax.axis_index("x"), lax.axis_index("y"), lax.axis_index("c")
        my_chip = 2 * x + y
        copies = []
        for b in range(nb):
            mine = pltpu.make_async_copy(srcs[b].at[my_chip], dsts[b].at[my_chip], local_sems.at[b])
            mine.start()
            copies.append(mine)
            for k, (tx, ty) in enumerate([(1 - x, y), (x, 1 - y), (1 - x, 1 - y)]):
                cp = pltpu.make_async_remote_copy(
                    src_ref=srcs[b].at[2 * tx + ty], dst_ref=dsts[b].at[my_chip], send_sem=send_sems.at[3 * b + k],
                    recv_sem=recv_sems.at[3 * b + k], device_id=(tx, ty, c), device_id_type=MESH)
                cp.start()
                copies.append(cp)
        for cp in copies:
            cp.wait()

    return _pcall(
        body, name="grads_chip_exchange",
        in_specs=[_ANY] * nb, out_specs=[_ANY] * nb,
        out_shape=[jax.ShapeDtypeStruct(a.shape, a.dtype) for a in parts],
        scratch_shapes=[pltpu.SemaphoreType.DMA((3 * nb,)), pltpu.SemaphoreType.DMA((3 * nb,)),
                        pltpu.SemaphoreType.DMA((nb,))],
    )(*parts)


def chip_sum(parts, name):
    r, w = parts.shape[1:]
    tr = _tile(r, 512, 16)
    spec = lambda j: _in(parts, (None, tr, w), lambda i, j=j: (j, i, 0))

    def fn(ids, a, b, c_, d):
        a, b, c_, d = [t.astype(F32) for t in (a, b, c_, d)]
        return (((a + b) + c_) + d,)

    (out,) = tcall(fn, (r // tr,), [spec(j) for j in range(4)],
                   [_out((r, w), F32, (tr, w), lambda i: (i, 0))], name)
    return out


def _remote(src, dst, send_sems, recv_sems, k, to):
    return functools.partial(pltpu.make_async_remote_copy, src_ref=src, dst_ref=dst, send_sem=send_sems.at[k],
                             recv_sem=recv_sems.at[k], device_id=to, device_id_type=MESH)


def _gather_plan(phase, nb):
    def plan(ins, outs, send_sems, recv_sems, local_sems):
        x, y, c = lax.axis_index("x"), lax.axis_index("y"), lax.axis_index("c")
        me, sibling = (x, y, c), (x, y, 1 - c)
        over_x, over_y, diagonal = (1 - x, y), (x, 1 - y), (1 - x, 1 - y)
        relay_of = ((1 - x) * (1 - c) + x * c, y * (1 - c) + (1 - y) * c)
        relay_to = (x * (1 - c) + (1 - x) * c, (1 - y) * (1 - c) + y * c)
        local, sends, recvs = [], [], []
        for b in range(nb):
            slot = lambda chip, core, b=b: outs[b].at[4 * chip[0] + 2 * chip[1] + core]
            if phase == 0:
                local.append(functools.partial(pltpu.make_async_copy, ins[b], slot((x, y), c), local_sems.at[b]))
                moves = [(ins[b], slot((x, y), c), to) for to in (sibling, (*over_x, c), (*over_y, c))]
                arrive = [slot((x, y), 1 - c), slot(over_x, c), slot(over_y, c)]
            elif phase == 1:
                moves = [(slot(relay_of, c), slot(relay_of, c), (*relay_to, c)),
                         (slot(over_x, c), slot(over_x, c), sibling), (slot(over_y, c), slot(over_y, c), sibling)]
                arrive = [slot(diagonal, c), slot(over_x, 1 - c), slot(over_y, 1 - c)]
            else:
                moves = [(slot(diagonal, c), slot(diagonal, c), sibling)]
                arrive = [slot(diagonal, 1 - c)]
            sends += [_remote(src, dst, send_sems, recv_sems, 3 * b + k, to) for k, (src, dst, to) in enumerate(moves)]
            recvs += [_remote(dst, dst, send_sems, recv_sems, 3 * b + k, me) for k, dst in enumerate(arrive)]
        return local, sends, recvs
    return plan


def gather_side(phase, arrays):
    nb = len(arrays)
    if phase == 0:
        shapes = [jax.ShapeDtypeStruct((N_DEV,) + a.shape, a.dtype) for a in arrays]
        return Side(arrays, shapes, 3 * nb, nb, _gather_plan(0, nb))
    shapes = [jax.ShapeDtypeStruct(a.shape, a.dtype) for a in arrays]
    return Side(arrays, shapes, 3 * nb, 0, _gather_plan(phase, nb), aliased=True)


def pair_side(bufs):
    nb = len(bufs)

    def plan(ins, outs, send_sems, recv_sems, local_sems):
        x, y, c = lax.axis_index("x"), lax.axis_index("y"), lax.axis_index("c")
        sends = [_remote(ins[b].at[2 * j + (1 - c)], outs[b].at[j], send_sems, recv_sems, 4 * b + j, (x, y, 1 - c))
                 for b in range(nb) for j in range(4)]
        recvs = [_remote(outs[b].at[j], outs[b].at[j], send_sems, recv_sems, 4 * b + j, (x, y, c))
                 for b in range(nb) for j in range(4)]
        return [], sends, recvs

    shapes = [jax.ShapeDtypeStruct((4,) + a.shape[1:], a.dtype) for a in bufs]
    return Side(bufs, shapes, 4 * nb, 0, plan)


def chip_side(parts):
    nb = len(parts)

    def plan(ins, outs, send_sems, recv_sems, local_sems):
        x, y, c = lax.axis_index("x"), lax.axis_index("y"), lax.axis_index("c")
        my_chip = 2 * x + y
        peers = [(1 - x, y), (x, 1 - y), (1 - x, 1 - y)]
        local = [functools.partial(pltpu.make_async_copy, ins[b].at[my_chip], outs[b].at[my_chip], local_sems.at[b])
                 for b in range(nb)]
        sends = [_remote(ins[b].at[2 * tx + ty], outs[b].at[my_chip], send_sems, recv_sems, 3 * b + k, (tx, ty, c))
                 for b in range(nb) for k, (tx, ty) in enumerate(peers)]
        recvs = [_remote(outs[b].at[2 * tx + ty], outs[b].at[2 * tx + ty], send_sems, recv_sems, 3 * b + k, (x, y, c))
                 for b in range(nb) for k, (tx, ty) in enumerate(peers)]
        return local, sends, recvs

    shapes = [jax.ShapeDtypeStruct(a.shape, a.dtype) for a in parts]
    return Side(parts, shapes, 3 * nb, nb, plan)


def adamw(w, g, m, v, name):
    shape = w.shape
    cols = shape[-1]
    rows = int(np.prod(shape[:-1]))
    w2, g2, m2, v2 = [a.reshape(rows, cols) for a in (w, g, m, v)]
    tr = _tile(rows, 256, 8) if rows % 8 == 0 else rows

    def fn(ids, wt, gt, mt, vt):
        m_new = ADAM_B1 * mt + (1.0 - ADAM_B1) * gt
        v_new = ADAM_B2 * vt + (1.0 - ADAM_B2) * (gt * gt)
        m_hat = m_new / (1.0 - ADAM_B1 ** ADAM_STEP)
        v_hat = v_new / (1.0 - ADAM_B2 ** ADAM_STEP)
        delta = -ADAM_LR * (m_hat / (jnp.sqrt(v_hat) + ADAM_EPS) + ADAM_WD * wt)
        return delta, m_new, v_new

    res = tcall(fn, (rows // tr,), [_row(a, tr) for a in (w2, g2, m2, v2)],
                [_row_out(rows, cols, F32, tr) for _ in range(3)], name)
    return [a.reshape(shape) for a in res]


_MATS = [("ffn1_w_gu", "col"), ("ffn1_w_down", "row"), ("ev_w_in", "col"), ("ev_w_out", "row"),
         ("od_w_in", "col"), ("od_w_out", "row"), ("xa_w_q", "row"), ("xa_w_kv", "col"), ("xa_w_o", "row"),
         ("ffn2_w_gu", "col"), ("ffn2_w_down", "row")]
_VECS = ["ffn1_norm", "mix_norm", "ev_q_gain", "ev_k_gain", "ev_sinks", "od_q_gain", "od_k_gain", "xa_norm",
         "xa_mem_norm", "xa_q_gain", "xa_k_gain", "ffn2_norm"]
_WEIGHTS = ["ffn1_norm", "ffn1_w_gu", "ffn1_w_down", "mix_norm", "ev_w_in", "ev_q_gain", "ev_k_gain", "ev_sinks",
            "ev_w_out", "od_w_in", "od_q_gain", "od_k_gain", "od_w_out", "xa_norm", "xa_mem_norm", "xa_w_q", "xa_w_kv",
            "xa_q_gain", "xa_k_gain", "xa_w_o", "ffn2_norm", "ffn2_w_gu", "ffn2_w_down"]


_AXIS = dict(_MATS)
DEPTH = 2


def _layer_groups(l):
    w_in, w_out = ("ev_w_in", "ev_w_out") if l % 2 == 0 else ("od_w_in", "od_w_out")
    return [[("ffn1_w_gu", l), ("ffn2_w_gu", l)], [(w_in, l // 2)], [("xa_w_kv", l)],
            [("ffn1_w_down", l), ("ffn2_w_down", l), (w_out, l // 2), ("xa_w_q", l), ("xa_w_o", l)]]


def _first_block_groups(l):
    first = [[("ffn1_w_gu", l)], [("ffn1_w_down", l)]]
    rest = [[m for m in group if m[0] not in ("ffn1_w_gu", "ffn1_w_down")] for group in _layer_groups(l)]
    return first, rest


def _weight_blocks(shards, groups):
    blocks = []
    for group in groups:
        rows = [shards[n][j].astype(BF16) for n, j in group]
        blocks.append(rows[0] if len(rows) == 1 else jnp.concatenate(rows, axis=0))
    return blocks


def _whole_weights(shards, groups, gathered):
    full = {}
    for group, got in zip(groups, gathered):
        off = 0
        for n, j in group:
            a, b = shards[n].shape[1:]
            seg = got[:, off:off + a, :]
            off += a
            full[n] = seg.reshape(N_DEV * a, b) if _AXIS[n] == "row" else seg.transpose(1, 0, 2).reshape(a, N_DEV * b)
    return full


def _gradient_buffers(grads, groups):
    bufs = []
    for group in groups:
        rows = []
        for n, _ in group:
            if isinstance(grads[n], tuple):
                for half in grads[n]:
                    a, b = half.shape
                    rows.append(half.reshape(a, N_DEV // 2, 2 * b // N_DEV).transpose(1, 0, 2))
                rows[-2:] = [jnp.concatenate(rows[-2:], axis=0)]
                continue
            a, b = grads[n].shape
            if _AXIS[n] == "row":
                rows.append(grads[n].reshape(N_DEV, a // N_DEV, b))
            else:
                rows.append(grads[n].reshape(a, N_DEV, b // N_DEV).transpose(1, 0, 2))
        bufs.append((rows[0] if len(rows) == 1 else jnp.concatenate(rows, axis=1)).astype(BF16))
    return bufs


def _gradient_blocks(shards, groups, sums):
    out = {}
    for group, tot in zip(groups, sums):
        off = 0
        for n, j in group:
            a = shards[n].shape[1]
            out[n, j] = tot[off:off + a]
            off += a
    return out


class _PairChain:
    def __init__(self, ex, bufs):
        self.ex, self.bufs, self.parts = ex, bufs, None

    def side(self, name):
        return pair_side(self.bufs) if name == "dwd" else None

    def done(self, name, carried):
        self.parts = self.ex.pair_sums(self.bufs, carried, "l1")


class _RestChain:
    HALF = {"dwd": (0, 2), "dwgu": (1, 3)}

    def __init__(self, ex, bufs):
        self.ex, self.bufs, self.parts, self.sums = ex, bufs, None, [None] * len(bufs)

    def side(self, name):
        if name == "da":
            return pair_side(self.bufs)
        return chip_side([self.parts[i] for i in self.HALF[name]])

    def done(self, name, carried):
        if name == "da":
            self.parts = self.ex.pair_sums(self.bufs, carried, "l0r")
        else:
            for i, tot in zip(self.HALF[name], self.ex.chip_sums(carried, "l0r_" + name)):
                self.sums[i] = tot


class _Exchange:
    def __init__(self, shards, c):
        self.shards, self.c = shards, c

    def weights_first(self):
        first, _ = _first_block_groups(0)
        return _whole_weights(self.shards, first, all_gather_blocks(_weight_blocks(self.shards, first)))

    def rest_blocks(self):
        return _weight_blocks(self.shards, _first_block_groups(0)[1])

    def weights_rest(self, gathered):
        return _whole_weights(self.shards, _first_block_groups(0)[1], gathered)

    def gather_start(self):
        return gather_side(0, _weight_blocks(self.shards, _layer_groups(1)))

    def weights_next(self, gathered):
        return _whole_weights(self.shards, _layer_groups(1), gathered)

    def chain_next(self, grads):
        return _PairChain(self, _gradient_buffers(grads, _layer_groups(1)))

    def chain_rest(self, grads):
        return _RestChain(self, _gradient_buffers(grads, _first_block_groups(0)[1]))

    def pair_sums(self, bufs, got, tag):
        return [pair_sum(b, g, self.c, b.dtype, f"grads_pair_sum_{tag}_{i}") for i, (b, g) in enumerate(zip(bufs, got))]

    def chip_sums(self, parts, tag):
        return [chip_sum(p, f"grads_chip_sum_{tag}_{i}") for i, p in enumerate(parts)]

    def finish(self, gm, gv, sums1, sums_rest):
        vecs = {n: jnp.concatenate(v, axis=0) for n, v in gv.items()}
        first, rest = _first_block_groups(0)
        bufs = _gradient_buffers(gm[0], first)
        vec = jnp.concatenate([vecs[n].reshape(-1) for n in _VECS])
        vec = jnp.pad(vec, (0, -vec.shape[0] % (16 * LANES)))
        bufs.append(jnp.broadcast_to(vec.reshape(1, -1, LANES), (N_DEV, vec.shape[0] // LANES, LANES)))
        parts = self.pair_sums(bufs, pair_exchange(bufs), "l0")
        sums0 = self.chip_sums(chip_exchange(parts), "l0")
        blocks = {**_gradient_blocks(self.shards, first, sums0[:-1]), **_gradient_blocks(self.shards, rest, sums_rest),
                  **_gradient_blocks(self.shards, _layer_groups(1), sums1)}
        out = {n: jnp.stack([blocks[n, j] for j in range(self.shards[n].shape[0])]) for n, _ in _MATS}
        flat, off = sums0[-1].reshape(-1), 0
        for n in _VECS:
            out[n] = flat[off:off + vecs[n].size].reshape(vecs[n].shape)
            off += vecs[n].size
        return out


class _NoExchange:
    def __init__(self, full):
        self.full = full

    def weights_first(self):
        return self.full[0]

    def rest_blocks(self):
        return None

    def gather_start(self):
        return None

    def weights_next(self, gathered):
        return self.full[1]

    def chain_next(self, grads):
        return None

    def chain_rest(self, grads):
        return None

    def finish(self, gm, gv, sums1, sums_rest):
        mats = {}
        for l in range(DEPTH):
            for group in _layer_groups(l):
                for n, j in group:
                    whole = gm[l][n]
                    mats.setdefault(n, {})[j] = jnp.concatenate(whole, axis=1) if isinstance(whole, tuple) else whole
        mats = {n: jnp.stack([v[j] for j in sorted(v)]) for n, v in mats.items()}
        return mats, {n: jnp.concatenate(v, axis=0) for n, v in gv.items()}


def _local_step(x, mem, target, w, ex):
    assert w["ffn1_norm"].shape[0] == DEPTH
    row = lambda a, l: a[l:l + 1]
    full = [ex.weights_first(), None]
    saved = []
    for l in range(DEPTH):
        t, j, f = f"l{l}", l // 2, full[l]
        rest = ex.rest_blocks() if l == 0 else None
        if rest is None:
            x, s1 = ffn_fwd(x, row(w["ffn1_norm"], l), f["ffn1_w_gu"], f["ffn1_w_down"], t + "_ffn1")
        else:
            x, s1, rest = ffn_fwd(x, row(w["ffn1_norm"], l), f["ffn1_w_gu"], f["ffn1_w_down"], t + "_ffn1", (0, rest))
        relay = None
        if l % 2 == 0:
            h = rmsnorm_fwd(x, row(w["mix_norm"], l), t + "_ev_norm", None if rest is None else gather_side(2, rest))
            if rest is not None:
                h, rest = h
                f = full[l] = {**f, **ex.weights_rest(rest)}
            side = ex.gather_start() if l + 1 < DEPTH else None
            x, s2, relay = even_mixer_fwd(x, h, _ev_reorder(f["ev_w_in"]), row(w["ev_q_gain"], j),
                                          row(w["ev_k_gain"], j), row(w["ev_sinks"], j), f["ev_w_out"], t + "_ev", side)
        else:
            x, s2 = odd_mixer_fwd(x, row(w["mix_norm"], l), f["od_w_in"], row(w["od_q_gain"], j),
                                  row(w["od_k_gain"], j), f["od_w_out"], t + "_od")
        x, s3 = xa_fwd(x, mem, row(w["xa_norm"], l), row(w["xa_mem_norm"], l), f["xa_w_q"], f["xa_w_kv"],
                       row(w["xa_q_gain"], l), row(w["xa_k_gain"], l), f["xa_w_o"], t + "_xa")
        if relay is None:
            x, s4 = ffn_fwd(x, row(w["ffn2_norm"], l), f["ffn2_w_gu"], f["ffn2_w_down"], t + "_ffn2")
        else:
            x, s4, relay = ffn_fwd(x, row(w["ffn2_norm"], l), f["ffn2_w_gu"], f["ffn2_w_down"], t + "_ffn2", (1, relay))
        if l + 1 < DEPTH:
            full[l + 1] = ex.weights_next(relay)
        saved.append((s1, s2, s3, s4))
    dx, sq = loss_head(x, target, "loss_head")
    loss = 0.5 * jnp.sum(sq) / x.shape[1]

    gm = [dict() for _ in range(DEPTH)]
    gv = {n: [None] * w[n].shape[0] for n in _VECS}
    chain1 = chain0 = sums1 = None
    for l in reversed(range(DEPTH)):
        t, j, f = f"l{l}", l // 2, full[l]
        s1, s2, s3, s4 = saved[l]
        dx, gv["ffn2_norm"][l], gm[l]["ffn2_w_gu"], gm[l]["ffn2_w_down"] = ffn_bwd(
            dx, s4, row(w["ffn2_norm"], l), f["ffn2_w_gu"], f["ffn2_w_down"], t + "_ffn2", chain1 if l == 0 else None)
        parts = chain1.parts if l == 0 and chain1 is not None else None
        (dx, gv["xa_norm"][l], gv["xa_mem_norm"][l], gm[l]["xa_w_q"], gm[l]["xa_w_kv"], gv["xa_q_gain"][l],
         gv["xa_k_gain"][l], gm[l]["xa_w_o"]) = xa_bwd(
            dx, s3, mem, row(w["xa_norm"], l), row(w["xa_mem_norm"], l), f["xa_w_q"], f["xa_w_kv"],
            row(w["xa_q_gain"], l), row(w["xa_k_gain"], l), f["xa_w_o"], t + "_xa")
        if l % 2 == 0:
            (dx, gv["mix_norm"][l], d_win, gv["ev_q_gain"][j], gv["ev_k_gain"][j], gv["ev_sinks"][j],
             gm[l]["ev_w_out"], carried) = even_mixer_bwd(
                dx, s2, row(w["mix_norm"], l), _ev_reorder(f["ev_w_in"]), row(w["ev_q_gain"], j), row(w["ev_k_gain"], j),
                row(w["ev_sinks"], j), f["ev_w_out"], t + "_ev", None if parts is None else chip_side(parts))
            gm[l]["ev_w_in"] = _ev_restore(d_win)
            if carried is not None:
                sums1 = ex.chip_sums(carried, "l1")
        else:
            (dx, gv["mix_norm"][l], gm[l]["od_w_in"], gv["od_q_gain"][j], gv["od_k_gain"][j],
             gm[l]["od_w_out"]) = odd_mixer_bwd(
                dx, s2, row(w["mix_norm"], l), f["od_w_in"], row(w["od_q_gain"], j), row(w["od_k_gain"], j),
                f["od_w_out"], t + "_od")
        if l == 0:
            chain0 = ex.chain_rest(gm[l])
        dx, gv["ffn1_norm"][l], gm[l]["ffn1_w_gu"], gm[l]["ffn1_w_down"] = ffn_bwd(
            dx, s1, row(w["ffn1_norm"], l), f["ffn1_w_gu"], f["ffn1_w_down"], t + "_ffn1", chain0 if l == 0 else None)
        if l == 1:
            chain1 = ex.chain_next(gm[l])
    return loss, dx, ex.finish(gm, gv, sums1, None if chain0 is None else chain0.sums)


def kernel(x, mem, ffn1_norm, ffn1_w_gu, ffn1_w_down, mix_norm, ev_w_in, ev_q_gain, ev_k_gain, ev_sinks, ev_w_out, od_w_in, od_q_gain, od_k_gain, od_w_out, xa_norm, xa_mem_norm, xa_w_q, xa_w_kv, xa_q_gain, xa_k_gain, xa_w_o, ffn2_norm, ffn2_w_gu, ffn2_w_down, loss_target, m_ffn1_norm, m_ffn1_w_gu, m_ffn1_w_down, m_mix_norm, m_ev_w_in, m_ev_q_gain, m_ev_k_gain, m_ev_sinks, m_ev_w_out, m_od_w_in, m_od_q_gain, m_od_k_gain, m_od_w_out, m_xa_norm, m_xa_mem_norm, m_xa_w_q, m_xa_w_kv, m_xa_q_gain, m_xa_k_gain, m_xa_w_o, m_ffn2_norm, m_ffn2_w_gu, m_ffn2_w_down, v_ffn1_norm, v_ffn1_w_gu, v_ffn1_w_down, v_mix_norm, v_ev_w_in, v_ev_q_gain, v_ev_k_gain, v_ev_sinks, v_ev_w_out, v_od_w_in, v_od_q_gain, v_od_k_gain, v_od_w_out, v_xa_norm, v_xa_mem_norm, v_xa_w_q, v_xa_w_kv, v_xa_q_gain, v_xa_k_gain, v_xa_w_o, v_ffn2_norm, v_ffn2_w_gu, v_ffn2_w_down):
    w = dict(ffn1_norm=ffn1_norm, ffn1_w_gu=ffn1_w_gu, ffn1_w_down=ffn1_w_down, mix_norm=mix_norm, ev_w_in=ev_w_in, ev_q_gain=ev_q_gain, ev_k_gain=ev_k_gain, ev_sinks=ev_sinks, ev_w_out=ev_w_out, od_w_in=od_w_in, od_q_gain=od_q_gain, od_k_gain=od_k_gain, od_w_out=od_w_out, xa_norm=xa_norm, xa_mem_norm=xa_mem_norm, xa_w_q=xa_w_q, xa_w_kv=xa_w_kv, xa_q_gain=xa_q_gain, xa_k_gain=xa_k_gain, xa_w_o=xa_w_o, ffn2_norm=ffn2_norm, ffn2_w_gu=ffn2_w_gu, ffn2_w_down=ffn2_w_down)
    m = dict(ffn1_norm=m_ffn1_norm, ffn1_w_gu=m_ffn1_w_gu, ffn1_w_down=m_ffn1_w_down, mix_norm=m_mix_norm, ev_w_in=m_ev_w_in, ev_q_gain=m_ev_q_gain, ev_k_gain=m_ev_k_gain, ev_sinks=m_ev_sinks, ev_w_out=m_ev_w_out, od_w_in=m_od_w_in, od_q_gain=m_od_q_gain, od_k_gain=m_od_k_gain, od_w_out=m_od_w_out, xa_norm=m_xa_norm, xa_mem_norm=m_xa_mem_norm, xa_w_q=m_xa_w_q, xa_w_kv=m_xa_w_kv, xa_q_gain=m_xa_q_gain, xa_k_gain=m_xa_k_gain, xa_w_o=m_xa_w_o, ffn2_norm=m_ffn2_norm, ffn2_w_gu=m_ffn2_w_gu, ffn2_w_down=m_ffn2_w_down)
    v = dict(ffn1_norm=v_ffn1_norm, ffn1_w_gu=v_ffn1_w_gu, ffn1_w_down=v_ffn1_w_down, mix_norm=v_mix_norm, ev_w_in=v_ev_w_in, ev_q_gain=v_ev_q_gain, ev_k_gain=v_ev_k_gain, ev_sinks=v_ev_sinks, ev_w_out=v_ev_w_out, od_w_in=v_od_w_in, od_q_gain=v_od_q_gain, od_k_gain=v_od_k_gain, od_w_out=v_od_w_out, xa_norm=v_xa_norm, xa_mem_norm=v_xa_mem_norm, xa_w_q=v_xa_w_q, xa_w_kv=v_xa_w_kv, xa_q_gain=v_xa_q_gain, xa_k_gain=v_xa_k_gain, xa_w_o=v_xa_w_o, ffn2_norm=v_ffn2_norm, ffn2_w_gu=v_ffn2_w_gu, ffn2_w_down=v_ffn2_w_down)

    c = lax.axis_index("c").astype(jnp.int32).reshape(1)
    loss, dx, grads = _local_step(x[0], mem[0], loss_target[0], w, _Exchange(w, c))
    loss = lax.psum(loss, ("x", "y", "c"))

    delta, new_m, new_v = {}, {}, {}
    for n in _WEIGHTS:
        delta[n], new_m[n], new_v[n] = adamw(w[n], grads[n], m[n], v[n], "adamw_" + n)
    return (loss, dx[None], *[grads[n] for n in _WEIGHTS], *[delta[n] for n in _WEIGHTS],
            *[new_m[n] for n in _WEIGHTS], *[new_v[n] for n in _WEIGHTS])
```

```python
import functools

import numpy as np
import jax
import jax.numpy as jnp
from jax import lax
from jax.experimental import pallas as pl
from jax.experimental.pallas import tpu as pltpu

F32 = jnp.float32
BF16 = jnp.bfloat16
MESH = pl.DeviceIdType.MESH

HEAD_DIM = 64
BLOCK = 128
RMS_EPS = 1e-6
A_Q_HEADS, A_KV_HEADS = 8, 2
B_HEADS = 8
C_HEADS = 16
C_PATTERNS = ((128, 1), (512, 4), (2048, 16))
X_HEADS = 4
N_DEV = 8
LANES = 1024
VMEM_LIMIT_BYTES = 56 * 1024 * 1024
SB_SKIP_LOG = -110.0
NEG_BIG = -1e30

ADAM_LR, ADAM_B1, ADAM_B2, ADAM_EPS, ADAM_WD, ADAM_STEP = 0.001, 0.9, 0.999, 1e-08, 0.01, 10

NN = (((1,), (0,)), ((), ()))
NT = (((1,), (1,)), ((), ()))
TN = (((0,), (0,)), ((), ()))


class Side:
    def __init__(self, arrays, out_shapes, n_remote, n_local, plan, aliased=False):
        self.arrays, self.out_shapes, self.plan, self.aliased = list(arrays), list(out_shapes), plan, aliased
        self.sems = [pltpu.SemaphoreType.DMA((n_remote,)), pltpu.SemaphoreType.DMA((n_remote,)),
                     pltpu.SemaphoreType.DMA((max(n_local, 1),))]

    def start(self, ins, outs, sems):
        local, sends, _ = self.plan(ins, outs, *sems)
        for make in local + sends:
            make().start()

    def wait(self, ins, outs, sems):
        local, sends, recvs = self.plan(ins, outs, *sems)
        for make in sends:
            make().wait_send()
        for make in recvs:
            make().wait_recv()
        for make in local:
            make().wait()


def _pcall(body, side=None, **kw):
    if side is None:
        return pl.pallas_call(body, **kw)
    grid = kw["grid"]
    single = not isinstance(kw["out_specs"], (list, tuple))
    out_specs = [kw["out_specs"]] if single else list(kw["out_specs"])
    out_shape = [kw["out_shape"]] if single else list(kw["out_shape"])
    scratch = list(kw.get("scratch_shapes", []))
    n_in, n_out, n_scr, n_side = len(kw["in_specs"]), len(out_specs), len(scratch), len(side.arrays)
    n_sout = len(side.out_shapes)

    def hosted(*refs):
        ins, s_in = refs[:n_in], refs[n_in:n_in + n_side]
        outs = refs[n_in + n_side:n_in + n_side + n_out]
        s_out = refs[n_in + n_side + n_out:n_in + n_side + n_out + n_sout]
        rest = refs[n_in + n_side + n_out + n_sout:]
        scr, sems = rest[:n_scr], rest[n_scr:]
        first = last = None
        for a, size in enumerate(grid):
            f, l = pl.program_id(a) == 0, pl.program_id(a) == size - 1
            first = f if first is None else jnp.logical_and(first, f)
            last = l if last is None else jnp.logical_and(last, l)

        @pl.when(first)
        def _():
            side.start(s_in, s_out, sems)

        body(*ins, *outs, *scr)

        @pl.when(last)
        def _():
            side.wait(s_in, s_out, sems)

    any_space = pl.BlockSpec(memory_space=pl.ANY)
    kw2 = dict(kw)
    kw2.update(in_specs=list(kw["in_specs"]) + [any_space] * n_side, out_specs=out_specs + [any_space] * n_sout,
               out_shape=out_shape + side.out_shapes, scratch_shapes=scratch + side.sems)
    if side.aliased:
        kw2["input_output_aliases"] = {n_in + i: n_out + i for i in range(n_side)}
    call = pl.pallas_call(hosted, **kw2)

    def run(*args):
        res = call(*args, *side.arrays)
        return (res[0] if single else list(res[:n_out])), list(res[n_out:])

    return run


def _params(**kw):
    return pltpu.CompilerParams(vmem_limit_bytes=VMEM_LIMIT_BYTES, **kw)


def _tile(dim, cap, unit=128):
    if dim <= cap:
        return dim
    t = (cap // unit) * unit
    while t >= unit:
        if dim % t == 0:
            return t
        t -= unit
    raise ValueError(f"no tile for {dim} under {cap}")


def _dot(a, b, dims):
    return lax.dot_general(a.astype(BF16), b.astype(BF16), dims, preferred_element_type=F32)


@functools.partial(jax.custom_vjp, nondiff_argnums=(2,))
def _dot_vjp(a, b, nt):
    return _dot(a, b, NT if nt else NN)


def _dot_vjp_fwd(a, b, nt):
    return _dot(a, b, NT if nt else NN), (a.astype(BF16), b.astype(BF16))


def _dot_vjp_bwd(nt, res, g):
    a, b = res
    if nt:
        return _dot(g, b, NN), _dot(g, a, TN)
    return _dot(g, b, NT), _dot(a, g, TN)


_dot_vjp.defvjp(_dot_vjp_fwd, _dot_vjp_bwd)


def _plain_dot(a, b, nt):
    return _dot(a, b, NT if nt else NN)


def _split_dot(x, mat, terms=2):
    out, rem = None, x
    for t in range(terms):
        part = rem.astype(BF16)
        d = lax.dot_general(part, mat, NN, preferred_element_type=F32)
        out = d if out is None else out + d
        if t + 1 < terms:
            rem = rem - part.astype(F32)
    return out


@functools.partial(jax.custom_vjp, nondiff_argnums=(3,))
def _split_dot_vjp(x, mat, mat_t, terms):
    return _split_dot(x, mat, terms)


def _split_dot_vjp_fwd(x, mat, mat_t, terms):
    return _split_dot(x, mat, terms), mat_t


def _split_dot_vjp_bwd(terms, mat_t, g):
    return _split_dot(g, mat_t, terms), None, None


_split_dot_vjp.defvjp(_split_dot_vjp_fwd, _split_dot_vjp_bwd)


def _plain_split(x, mat, mat_t, terms):
    return _split_dot(x, mat, terms)


def _tri(after):
    j = lax.broadcasted_iota(jnp.int32, (BLOCK, BLOCK), 0)
    s = lax.broadcasted_iota(jnp.int32, (BLOCK, BLOCK), 1)
    return jnp.where(j > s if after else j < s, 1.0, 0.0).astype(BF16)


def _in(a, block, imap):
    return (a, block, imap)


def _out(shape, dtype, block, imap, acc=False):
    return (shape, dtype, block, imap, acc)


def tcall(fn, grid, ins, outs, name, scratch=None, side=None):
    nin = len(ins)
    nout = len(outs)
    ngrid = len(grid)

    def body(*refs):
        ids = tuple(pl.program_id(a) for a in range(ngrid))
        extra = {} if scratch is None else {"scratch": refs[nin + nout]}
        res = fn(ids, *[r[...] for r in refs[:nin]], **extra)
        first = ids[0] == 0
        for a in range(1, ngrid):
            first = jnp.logical_and(first, ids[a] == 0)
        for o_ref, r, spec in zip(refs[nin:nin + nout], res, outs):
            if spec[4]:
                @pl.when(first)
                def _(o_ref=o_ref):
                    o_ref[...] = jnp.zeros(o_ref.shape, o_ref.dtype)
                o_ref[...] += r.astype(o_ref.dtype)
            else:
                o_ref[...] = r.astype(o_ref.dtype)

    return _pcall(
        body, side=side, name=name, grid=grid,
        in_specs=[pl.BlockSpec(b, m) for (_, b, m) in ins],
        out_specs=[pl.BlockSpec(b, m) for (_, _, b, m, _) in outs],
        out_shape=[jax.ShapeDtypeStruct(s, d) for (s, d, _, _, _) in outs],
        scratch_shapes=[] if scratch is None else [pltpu.VMEM(*scratch)],
        compiler_params=_params(),
    )(*[a for (a, _, _) in ins])


def _to_strided(scr, nat, d):
    if d == 1:
        return nat
    t, w = nat.shape
    nc = w // BLOCK
    for c in range(nc):
        scr[c * t:(c + 1) * t, :] = nat[:, c * BLOCK:(c + 1) * BLOCK]
    return jnp.concatenate([scr[pl.ds(c * t + r, t // d, stride=d), :] for r in range(d) for c in range(nc)], axis=1)


def _to_natural(scr, st, d):
    if d == 1:
        return st.astype(F32)
    t, w = st.shape[0] * d, st.shape[1] // d
    nc = w // BLOCK
    st = st.astype(F32)
    for r in range(d):
        for c in range(nc):
            scr[pl.ds(c * t + r, t // d, stride=d), :] = st[:, r * w + c * BLOCK:r * w + (c + 1) * BLOCK]
    return jnp.concatenate([scr[c * t:(c + 1) * t, :] for c in range(nc)], axis=1)


def _row(a, tm, width=None, cb=0):
    width = a.shape[1] if width is None else width
    return _in(a, (tm, width), lambda i, cb=cb: (i, cb))


def _full(a):
    zeros = (0,) * a.ndim
    return _in(a, a.shape, lambda *ids: zeros)


def _row_out(n, width, dtype, tm):
    return _out((n, width), dtype, (tm, width), lambda i: (i, 0))


def _acc_out(shape):
    zeros = (0,) * len(shape)
    return _out(shape, F32, shape, lambda *ids: zeros, acc=True)


def mm(a, b, mode, name, *, out_dtype=F32, scale=1.0, res=None, side=None):
    if mode == "nn":
        (m, k), (k2, n) = a.shape, b.shape
    elif mode == "nt":
        (m, k), (n, k2) = a.shape, b.shape
    else:
        (k, m), (k2, n) = a.shape, b.shape
    assert k == k2, (a.shape, b.shape, mode)
    tm, tn, tk = _tile(m, 512), _tile(n, 1408), _tile(k, 1408)
    nk = k // tk
    dims = {"nn": NN, "nt": NT, "tn": TN}[mode]
    has_res = res is not None

    def body(*refs):
        if has_res:
            a_ref, b_ref, r_ref, o_ref, acc_ref = refs
        else:
            a_ref, b_ref, o_ref, acc_ref = refs
        kk = pl.program_id(2)

        @pl.when(kk == 0)
        def _():
            acc_ref[...] = jnp.zeros(acc_ref.shape, F32)

        acc_ref[...] += _dot(a_ref[...], b_ref[...], dims)

        @pl.when(kk == nk - 1)
        def _():
            out = acc_ref[...]
            if scale != 1.0:
                out = out * scale
            if has_res:
                out = out + r_ref[...]
            o_ref[...] = out.astype(o_ref.dtype)

    a_spec = (pl.BlockSpec((tk, tm), lambda i, j, kk: (kk, i)) if mode == "tn"
              else pl.BlockSpec((tm, tk), lambda i, j, kk: (i, kk)))
    b_spec = (pl.BlockSpec((tn, tk), lambda i, j, kk: (j, kk)) if mode == "nt"
              else pl.BlockSpec((tk, tn), lambda i, j, kk: (kk, j)))
    in_specs = [a_spec, b_spec]
    args = [a, b]
    if has_res:
        in_specs.append(pl.BlockSpec((tm, tn), lambda i, j, kk: (i, j)))
        args.append(res)
    order = ("parallel", "parallel", "arbitrary") if side is None else ("arbitrary",) * 3
    return _pcall(
        body, side=side, name=name, grid=(m // tm, n // tn, nk),
        in_specs=in_specs,
        out_specs=pl.BlockSpec((tm, tn), lambda i, j, kk: (i, j)),
        out_shape=jax.ShapeDtypeStruct((m, n), out_dtype),
        scratch_shapes=[pltpu.VMEM((tm, tn), F32)],
        compiler_params=_params(dimension_semantics=order),
    )(*args)


def _rms(x, g):
    return x * lax.rsqrt(jnp.mean(x * x, axis=-1, keepdims=True) + RMS_EPS) * g


def _silu_mul(gate, up):
    return gate / (1.0 + jnp.exp(-gate)) * up


def mm_gate_up(h, w_gu, name, side=None):
    m, k = h.shape
    f = w_gu.shape[1] // 2
    tm, tn, tk = _tile(m, 512), _tile(f, 1408), _tile(k, 1408)
    nk, nj = k // tk, f // tn

    def body(h_ref, wg_ref, wu_ref, g_ref, u_ref, a_ref, accg_ref, accu_ref):
        kk = pl.program_id(2)

        @pl.when(kk == 0)
        def _():
            accg_ref[...] = jnp.zeros(accg_ref.shape, F32)
            accu_ref[...] = jnp.zeros(accu_ref.shape, F32)

        ht = h_ref[...]
        accg_ref[...] += _dot(ht, wg_ref[...], NN)
        accu_ref[...] += _dot(ht, wu_ref[...], NN)

        @pl.when(kk == nk - 1)
        def _():
            gate, up = accg_ref[...], accu_ref[...]
            g_ref[...] = gate
            u_ref[...] = up
            a_ref[...] = _silu_mul(gate, up).astype(a_ref.dtype)

    tile = pl.BlockSpec((tm, tn), lambda i, j, kk: (i, j))
    return _pcall(
        body, side=side, name=name, grid=(m // tm, nj, nk),
        in_specs=[pl.BlockSpec((tm, tk), lambda i, j, kk: (i, kk)),
                  pl.BlockSpec((tk, tn), lambda i, j, kk: (kk, j)),
                  pl.BlockSpec((tk, tn), lambda i, j, kk: (kk, j + nj))],
        out_specs=[tile, tile, tile],
        out_shape=[jax.ShapeDtypeStruct((m, f), F32), jax.ShapeDtypeStruct((m, f), F32),
                   jax.ShapeDtypeStruct((m, f), BF16)],
        scratch_shapes=[pltpu.VMEM((tm, tn), F32), pltpu.VMEM((tm, tn), F32)],
        compiler_params=_params(dimension_semantics=("arbitrary",) * 3),
    )(h, w_gu, w_gu)


def mm_down_act_bwd(dy, w_down, gate, up, name, side=None):
    m, d = dy.shape
    f = w_down.shape[0]
    tm, tn = _tile(m, 512), _tile(f, 1408)
    assert d <= 1408

    def body(dy_ref, w_ref, g_ref, u_ref, dg_ref, du_ref):
        da = _dot(dy_ref[...], w_ref[...], NT) * 0.5
        _, vjp = jax.vjp(_silu_mul, g_ref[...], u_ref[...])
        dg, du = vjp(da)
        dg_ref[...] = dg.astype(dg_ref.dtype)
        du_ref[...] = du.astype(du_ref.dtype)

    tile = pl.BlockSpec((tm, tn), lambda i, j: (i, j))
    return _pcall(
        body, side=side, name=name, grid=(m // tm, f // tn),
        in_specs=[pl.BlockSpec((tm, d), lambda i, j: (i, 0)), pl.BlockSpec((tn, d), lambda i, j: (j, 0)), tile, tile],
        out_specs=[tile, tile],
        out_shape=[jax.ShapeDtypeStruct((m, f), BF16), jax.ShapeDtypeStruct((m, f), BF16)],
        compiler_params=_params(dimension_semantics=("arbitrary", "arbitrary")),
    )(dy, w_down, gate, up)


def mm_norm_bwd(a, b, x, g, dres, name):
    halves = isinstance(a, (tuple, list))
    a0, a1 = a if halves else (a, None)
    m, k = a0.shape[0], a0.shape[1] * (2 if halves else 1)
    d = b.shape[0]
    tm, tk = _tile(m, 512), _tile(a0.shape[1], 1408)
    nk = k // tk
    nkh = a0.shape[1] // tk
    has_res = dres is not None

    def body(*refs):
        a_ref, b_ref, x_ref, g_ref = refs[:4]
        rest = refs[4:-3]
        a1_ref = rest[0] if halves else None
        r_ref = rest[-1] if has_res else None
        dx_ref, dg_ref, acc_ref = refs[-3:]
        i, kk = pl.program_id(0), pl.program_id(1)

        @pl.when(kk == 0)
        def _():
            acc_ref[...] = jnp.zeros(acc_ref.shape, F32)

        if halves:
            @pl.when(kk < nkh)
            def _():
                acc_ref[...] += _dot(a_ref[...], b_ref[...], NT)

            @pl.when(kk >= nkh)
            def _():
                acc_ref[...] += _dot(a1_ref[...], b_ref[...], NT)
        else:
            acc_ref[...] += _dot(a_ref[...], b_ref[...], NT)

        @pl.when(kk == nk - 1)
        def _():
            _, vjp = jax.vjp(_rms, x_ref[...], g_ref[...])
            dx, dg = vjp(acc_ref[...])
            dx_ref[...] = dx + r_ref[...] if has_res else dx

            @pl.when(i == 0)
            def _():
                dg_ref[...] = jnp.zeros(dg_ref.shape, F32)

            dg_ref[...] += dg

    rows = pl.BlockSpec((tm, d), lambda i, kk: (i, 0))
    first = pl.BlockSpec((tm, tk), lambda i, kk: (i, jnp.minimum(kk, nkh - 1)))
    second = pl.BlockSpec((tm, tk), lambda i, kk: (i, jnp.maximum(kk - nkh, 0)))
    in_specs = ([first, pl.BlockSpec((d, tk), lambda i, kk: (0, kk)), rows, pl.BlockSpec(g.shape, lambda i, kk: (0, 0))]
                + ([second] if halves else []) + ([rows] if has_res else []))
    return _pcall(
        body, name=name, grid=(m // tm, nk),
        in_specs=in_specs,
        out_specs=[rows, pl.BlockSpec(g.shape, lambda i, kk: (0, 0))],
        out_shape=[jax.ShapeDtypeStruct((m, d), F32), jax.ShapeDtypeStruct(g.shape, F32)],
        scratch_shapes=[pltpu.VMEM((tm, d), F32)],
        compiler_params=_params(dimension_semantics=("arbitrary", "arbitrary")),
    )(*([a0, b, x, g] + ([a1] if halves else []) + ([dres] if has_res else [])))


def _indicator(shape, head_axis, mod):
    lane = lax.broadcasted_iota(jnp.int32, shape, head_axis)
    other = lax.broadcasted_iota(jnp.int32, shape, 1 - head_axis)
    lane = jnp.bitwise_and(lane, HEAD_DIM - 1) if mod else jnp.right_shift(lane, 6)
    return jnp.where(lane == other, 1.0, 0.0).astype(BF16)


def _head_rms(split, xs, g):
    w = xs.shape[1]
    to_head, from_head = _indicator((w, BLOCK), 0, False), _indicator((BLOCK, w), 1, False)
    to_lane, from_lane = _indicator((HEAD_DIM, w), 1, True), _indicator((w, HEAD_DIM), 0, True)
    ss = split(xs * xs, to_head, from_head, 3)
    r = lax.rsqrt(ss * (1.0 / HEAD_DIM) + RMS_EPS)
    g_all = split(jnp.broadcast_to(g, (8, HEAD_DIM)), to_lane, from_lane, 3)[0:1]
    return xs * split(r, from_head, to_head, 3) * g_all


def _prep(split, x, qg, kg, segs):
    parts = []
    for start, width, kind in segs:
        xs = x[:, start:start + width]
        parts.append(xs if kind == "raw" else _head_rms(split, xs, qg if kind == "q" else kg))
    return jnp.concatenate(parts, axis=1)


def prep_fwd(x, qg, kg, segs, dils, name):
    n, w = x.shape
    tm = _tile(n, 256, 8)

    def fn(ids, xt, a, b, scratch):
        ops = _prep(_plain_split, xt, a, b, segs)
        return tuple(_to_strided(scratch, ops, d) for d in dils)

    return tcall(fn, (n // tm,), [_row(x, tm), _full(qg), _full(kg)],
                 [_out((n // d, d * w), BF16, (tm // d, d * w), lambda i: (i, 0)) for d in dils], name,
                 scratch=((w // BLOCK * tm, BLOCK), F32))


def prep_bwd(x, qg, kg, segs, grads, gather, name):
    n, w = x.shape
    tm = BLOCK
    nblk = n // tm
    nslot = 1 + max(slot for _, _, _, slot in grads)

    def fn(ids, xt, a, b, *t, scratch):
        tiles, dils = [None] * nslot, [None] * nslot
        for ti, (_, sh, d, slot) in zip(t, grads):
            ti = jnp.where(ids[0] + sh < nblk, ti, 0.0) if sh else ti
            tiles[slot] = ti if tiles[slot] is None else tiles[slot] + ti
            dils[slot] = d
        tiles = [_to_natural(scratch, ti, d) for ti, d in zip(tiles, dils)]
        _, vjp = jax.vjp(lambda x_, a_, b_: _prep(_split_dot_vjp, x_, a_, b_, segs), xt, a, b)
        return vjp(gather(*tiles))

    specs = [_in(a, (tm // d, a.shape[1]), (lambda i, sh=sh: (jnp.minimum(i + sh, nblk - 1), 0)))
             for a, sh, d, _ in grads]
    wmax = max(a.shape[1] // d for a, _, d, _ in grads)
    return tcall(fn, (nblk,), [_row(x, tm), _full(qg), _full(kg)] + specs,
                 [_row_out(n, w, BF16, tm), _acc_out(qg.shape), _acc_out(kg.shape)], name,
                 scratch=((wmax // BLOCK * tm, BLOCK), F32))


def rmsnorm_fwd(x, g, name, side=None):
    n, d = x.shape
    tm = _tile(n, 512, 8)
    res = tcall(lambda ids, xt, gt: (_rms(xt, gt),), (n // tm,), [_row(x, tm), _full(g)],
                [_row_out(n, d, BF16, tm)], name, side=side)
    if side is None:
        return res[0]
    return res[0][0], res[1]


def ffn_fwd(x, g, w_gu, w_down, tag, carry=None):
    h = rmsnorm_fwd(x, g, tag + "_norm")
    if carry is None:
        gate, up, a = mm_gate_up(h, w_gu, tag + "_gu")
        return mm(a, w_down, "nn", tag + "_down", scale=0.5, res=x), (x, h, gate, up, a)
    phase, bufs = carry
    (gate, up, a), bufs = mm_gate_up(h, w_gu, tag + "_gu", side=gather_side(phase, bufs))
    y, bufs = mm(a, w_down, "nn", tag + "_down", scale=0.5, res=x, side=gather_side(phase + 1, bufs))
    return y, (x, h, gate, up, a), bufs


def ffn_bwd(dy, saved, g, w_gu, w_down, tag, chain=None):
    x, h, gate, up, a = saved

    def carrying(name, call, **kw):
        side = None if chain is None else chain.side(name)
        out = call(name=tag + "_" + name, side=side, **kw)
        if side is None:
            return out
        chain.done(name, out[1])
        return out[0]

    dgate, dup = carrying("da", mm_down_act_bwd, dy=dy, w_down=w_down, gate=gate, up=up)
    d_wdown = carrying("dwd", mm, a=a, b=dy, mode="tn", scale=0.5)
    d_wgu = (carrying("dwgu", mm, a=h, b=dgate, mode="tn"), mm(h, dup, "tn", tag + "_dwup"))
    dx, dg = mm_norm_bwd((dgate, dup), w_gu, x, g, dy, tag + "_dh")
    return dx, dg, d_wgu, d_wdown


def _alibi(n_heads):
    return [float(s) for s in np.asarray(2.0 ** (-8.0 * np.arange(1, n_heads + 1) / n_heads), dtype=np.float32)]


def _banded_tile(dot, first, q, kp, kc, vp, vc, sinks, *, hkv, grp, max_dist, step, slopes, want_lse):
    nrow = 2 * BLOCK if grp == 1 else BLOCK
    row = jnp.bitwise_and(lax.broadcasted_iota(jnp.int32, (nrow, 2 * BLOCK), 0), BLOCK - 1)
    col = lax.broadcasted_iota(jnp.int32, (nrow, 2 * BLOCK), 1)
    dist = row + BLOCK - col
    valid = (dist >= 0) & (dist <= max_dist) & ((col >= BLOCK) | jnp.logical_not(first))
    distf = dist.astype(F32)
    second = lax.broadcasted_iota(jnp.int32, (nrow, 1), 0) >= BLOCK

    def head(hd, qh, k2, v2):
        s = dot(qh, k2, True) * (HEAD_DIM ** -0.5)
        slope = jnp.where(second, slopes[hd + 1] * step, slopes[hd] * step) if grp == 1 else slopes[hd] * step
        s = jnp.where(valid, s - slope * distf, NEG_BIG)
        m = jnp.max(s, axis=-1, keepdims=True)
        if sinks is not None:
            pick = lax.broadcasted_iota(jnp.int32, sinks.shape, 1) == hd
            sk = jnp.sum(jnp.where(pick, sinks, 0.0), axis=1, keepdims=True)
            m = jnp.maximum(m, sk)
        m = lax.stop_gradient(m)
        p = jnp.exp(s - m)
        denom = jnp.sum(p, axis=-1, keepdims=True)
        if sinks is not None:
            denom = denom + jnp.exp(sk - m)
        return dot(p / denom, v2, False), m + jnp.log(denom)

    outs, lses = [], []
    if grp == 1:
        low = lax.broadcasted_iota(jnp.int32, (BLOCK, BLOCK), 1) < HEAD_DIM
        for pr in range(hkv // 2):
            sl = slice(pr * BLOCK, (pr + 1) * BLOCK)
            q2 = q[:, sl]
            k2 = jnp.concatenate([kp[:, sl], kc[:, sl]], axis=0)
            v2 = jnp.concatenate([vp[:, sl], vc[:, sl]], axis=0)
            o2, l2 = head(2 * pr, jnp.concatenate([jnp.where(low, q2, 0.0), jnp.where(low, 0.0, q2)], axis=0), k2, v2)
            outs.append(jnp.where(low, o2[:BLOCK], o2[BLOCK:]))
            lses.append(jnp.where(low, l2[:BLOCK], l2[BLOCK:]))
    else:
        for hk in range(hkv):
            sl = slice(hk * HEAD_DIM, (hk + 1) * HEAD_DIM)
            k2 = jnp.concatenate([kp[:, sl], kc[:, sl]], axis=0)
            v2 = jnp.concatenate([vp[:, sl], vc[:, sl]], axis=0)
            for gi in range(grp):
                hd = hk * grp + gi
                o_h, l_h = head(hd, q[:, hd * HEAD_DIM:(hd + 1) * HEAD_DIM], k2, v2)
                outs.append(o_h)
                lses.append(jnp.broadcast_to(l_h, (BLOCK, HEAD_DIM)))
    o = jnp.concatenate(outs, axis=1)
    if want_lse:
        return o, jnp.concatenate(lses, axis=1)
    return (o,)


def _banded_specs(view, qcol, kcol, vcol, wq, wkv):
    def at(colfn, prev):
        if prev:
            return lambda r, n: (jnp.maximum(n - 1, 0), colfn(r))
        return lambda r, n: (n, colfn(r))
    return [
        _in(view, (BLOCK, wq), at(qcol, False)),
        _in(view, (BLOCK, wkv), at(kcol, True)),
        _in(view, (BLOCK, wkv), at(kcol, False)),
        _in(view, (BLOCK, wkv), at(vcol, True)),
        _in(view, (BLOCK, wkv), at(vcol, False)),
    ]


def banded_fwd(view, dil, cols, sinks, cfg, name):
    ns = view.shape[0]
    nb = ns // BLOCK
    wq, wkv = cfg["hkv"] * cfg["grp"] * HEAD_DIM, cfg["hkv"] * HEAD_DIM
    has_sinks = sinks is not None

    def fn(ids, q, kp, kc, vp, vc, *rest):
        q, kp, kc, vp, vc = [a.astype(F32) for a in (q, kp, kc, vp, vc)]
        return _banded_tile(_plain_dot, ids[1] == 0, q, kp, kc, vp, vc, rest[0] if has_sinks else None, **cfg)

    ins = _banded_specs(view, *cols, wq, wkv) + ([_full(sinks)] if has_sinks else [])
    outs = [_out((ns, dil * wq), F32 if cfg["want_lse"] else BF16, (BLOCK, wq), lambda r, n: (n, r))]
    if cfg["want_lse"]:
        outs.append(_out((ns, dil * wq), F32, (BLOCK, wq), lambda r, n: (n, r)))
    return tcall(fn, (dil, nb), ins, outs, name)


def banded_bwd(view, dil, cols, sinks, cfg, cts, name):
    ns = view.shape[0]
    nb = ns // BLOCK
    wq, wkv = cfg["hkv"] * cfg["grp"] * HEAD_DIM, cfg["hkv"] * HEAD_DIM
    has_sinks = sinks is not None
    assert len(cts) == (2 if cfg["want_lse"] else 1)

    def fn(ids, q, kp, kc, vp, vc, *rest):
        sk = rest[0] if has_sinks else None
        ct = rest[1 if has_sinks else 0:]
        first = ids[1] == 0

        def f(q, kp, kc, vp, vc, *s):
            return _banded_tile(_dot_vjp, first, q, kp, kc, vp, vc, s[0] if has_sinks else None, **cfg)

        prim = tuple(a.astype(F32) for a in (q, kp, kc, vp, vc)) + ((sk,) if has_sinks else ())
        _, vjp = jax.vjp(f, *prim)
        return vjp(tuple(c.astype(F32) for c in ct))

    ins = (_banded_specs(view, *cols, wq, wkv) + ([_full(sinks)] if has_sinks else [])
           + [_in(a, (BLOCK, wq), (lambda r, n, cf=cf: (n, cf(r)))) for (a, cf) in cts])
    blk = lambda w: _out((ns, dil * w), F32, (BLOCK, w), lambda r, n: (n, r))
    outs = [blk(wq), blk(wkv), blk(wkv), blk(wkv), blk(wkv)]
    if has_sinks:
        outs.append(_acc_out(sinks.shape))
    return tcall(fn, (dil, nb), ins, outs, name)


def _log_sigmoid(z):
    return jnp.minimum(z, 0.0) - jnp.log(1.0 + jnp.exp(-jnp.abs(z)))


SB_PAIRS = 4


def _sb_pair(dot, suffix, qh, kb, vb, r_in, mask):
    z = dot(qh, kb, True) * (HEAD_DIM ** -0.5)
    lsp = _log_sigmoid(z)
    log_keep = jnp.where(mask, lsp - z, 0.0)
    log_after = suffix(log_keep) + r_in
    a = jnp.where(mask, jnp.exp(lsp + log_after), 0.0)
    return dot(a, vb, False), r_in + jnp.sum(log_keep, axis=1, keepdims=True)


def sb_fwd(qkv, qcb, kcb, vcb, name, side=None):
    s = qkv.shape[0]
    nb = s // BLOCK
    pairs = B_HEADS // 2
    wide = SB_PAIRS * BLOCK
    assert pairs % SB_PAIRS == 0 and qcb % SB_PAIRS == 0 and kcb % SB_PAIRS == 0 and vcb % SB_PAIRS == 0

    def body(q_ref, k_ref, v_ref, o_ref):
        n = pl.program_id(1)
        low = lax.broadcasted_iota(jnp.int32, (BLOCK, BLOCK), 1) < HEAD_DIM
        before = (lax.broadcasted_iota(jnp.int32, (2 * BLOCK, BLOCK), 1)
                  < jnp.bitwise_and(lax.broadcasted_iota(jnp.int32, (2 * BLOCK, BLOCK), 0), BLOCK - 1))
        after = _tri(True)
        suffix = lambda t: _split_dot(t, after)
        qs = []
        for p in range(SB_PAIRS):
            q2 = q_ref[:, p * BLOCK:(p + 1) * BLOCK].astype(F32)
            qs.append(jnp.concatenate([jnp.where(low, q2, 0.0), jnp.where(low, 0.0, q2)], axis=0))

        def cond(c):
            return jnp.logical_and(c[0] >= 0, c[1] > SB_SKIP_LOG)

        def step(c):
            kb, _, rs, accs = c
            rows = pl.ds(pl.multiple_of(kb * BLOCK, BLOCK), BLOCK)
            mask = jnp.logical_or(before, kb != n)
            new_r, new_acc, top = [], [], None
            for p in range(SB_PAIRS):
                cols = slice(p * BLOCK, (p + 1) * BLOCK)
                o_part, r_out = _sb_pair(_plain_dot, suffix, qs[p], k_ref[rows, cols], v_ref[rows, cols], rs[p], mask)
                new_r.append(r_out)
                new_acc.append(accs[p] + o_part)
                top = jnp.max(r_out) if top is None else jnp.maximum(top, jnp.max(r_out))
            return kb - 1, top, tuple(new_r), tuple(new_acc)

        init = (n, jnp.float32(0.0), tuple(jnp.zeros((2 * BLOCK, 1), F32) for _ in range(SB_PAIRS)),
                tuple(jnp.zeros((2 * BLOCK, BLOCK), F32) for _ in range(SB_PAIRS)))
        accs = lax.while_loop(cond, step, init)[3]
        for p in range(SB_PAIRS):
            o_ref[:, p * BLOCK:(p + 1) * BLOCK] = jnp.where(low, accs[p][:BLOCK], accs[p][BLOCK:]).astype(o_ref.dtype)

    return _pcall(
        body, side=side, name=name, grid=(pairs // SB_PAIRS, nb),
        in_specs=[pl.BlockSpec((BLOCK, wide), lambda g, n: (n, qcb // SB_PAIRS + g)),
                  pl.BlockSpec((s, wide), lambda g, n: (0, kcb // SB_PAIRS + g), pipeline_mode=pl.Buffered(1)),
                  pl.BlockSpec((s, wide), lambda g, n: (0, vcb // SB_PAIRS + g), pipeline_mode=pl.Buffered(1))],
        out_specs=pl.BlockSpec((BLOCK, wide), lambda g, n: (n, g)),
        out_shape=jax.ShapeDtypeStruct((s, pairs * BLOCK), BF16),
        compiler_params=_params(),
    )(qkv, qkv, qkv)


def sb_bwd(qkv, qcb, kcb, vcb, do, docb, name, side=None):
    s = qkv.shape[0]
    nb = s // BLOCK
    pairs = B_HEADS // 2
    wide = SB_PAIRS * BLOCK
    assert docb % SB_PAIRS == 0

    def body(q_ref, k_ref, v_ref, do_ref, dq_ref, dk_ref, dv_ref, r_ref):
        n = pl.program_id(1)

        @pl.when(n == 0)
        def _():
            dk_ref[...] = jnp.zeros(dk_ref.shape, F32)
            dv_ref[...] = jnp.zeros(dv_ref.shape, F32)

        low = lax.broadcasted_iota(jnp.int32, (BLOCK, BLOCK), 1) < HEAD_DIM
        before = (lax.broadcasted_iota(jnp.int32, (2 * BLOCK, BLOCK), 1)
                  < jnp.bitwise_and(lax.broadcasted_iota(jnp.int32, (2 * BLOCK, BLOCK), 0), BLOCK - 1))
        after, earlier = _tri(True), _tri(False)
        suffix = lambda t: _split_dot_vjp(t, after, earlier, 2)
        stack = lambda t: jnp.concatenate([jnp.where(low, t, 0.0), jnp.where(low, 0.0, t)], axis=0)
        qs = [stack(q_ref[:, p * BLOCK:(p + 1) * BLOCK].astype(F32)) for p in range(SB_PAIRS)]
        dos = [stack(do_ref[:, p * BLOCK:(p + 1) * BLOCK].astype(F32)) for p in range(SB_PAIRS)]

        def cond(c):
            return jnp.logical_and(c[0] >= 0, c[1] > SB_SKIP_LOG)

        def down(c):
            kb, _, rs = c
            rows = pl.ds(pl.multiple_of(kb * BLOCK, BLOCK), BLOCK)
            mask = jnp.logical_or(before, kb != n)
            new_r, top = [], None
            for h in range(SB_PAIRS):
                cols = slice(h * BLOCK, (h + 1) * BLOCK)
                r_ref[h, kb] = rs[h]
                z = _dot(qs[h], k_ref[rows, cols], NT) * (HEAD_DIM ** -0.5)
                log_keep = jnp.where(mask, _log_sigmoid(z) - z, 0.0)
                r_out = rs[h] + jnp.sum(log_keep, axis=1, keepdims=True)
                new_r.append(r_out)
                top = jnp.max(r_out) if top is None else jnp.maximum(top, jnp.max(r_out))
            return kb - 1, top, tuple(new_r)

        init = (n, jnp.float32(0.0), tuple(jnp.zeros((2 * BLOCK, 1), F32) for _ in range(SB_PAIRS)))
        last = lax.while_loop(cond, down, init)[0] + 1

        def up(kb, c):
            dqs, g_rs = c
            rows = pl.ds(pl.multiple_of(kb * BLOCK, BLOCK), BLOCK)
            mask = jnp.logical_or(before, kb != n)
            new_dq, new_g = [], []
            for h in range(SB_PAIRS):
                cols = slice(h * BLOCK, (h + 1) * BLOCK)
                _, vjp = jax.vjp(lambda q_, k_, v_, r_: _sb_pair(_dot_vjp, suffix, q_, k_, v_, r_, mask),
                                 qs[h], k_ref[rows, cols].astype(F32), v_ref[rows, cols].astype(F32), r_ref[h, kb])
                dq_c, dk_c, dv_c, g_in = vjp((dos[h], g_rs[h]))
                dk_ref[rows, cols] += dk_c
                dv_ref[rows, cols] += dv_c
                new_dq.append(dqs[h] + dq_c)
                new_g.append(g_in)
            return tuple(new_dq), tuple(new_g)

        init = (tuple(jnp.zeros((2 * BLOCK, BLOCK), F32) for _ in range(SB_PAIRS)),
                tuple(jnp.zeros((2 * BLOCK, 1), F32) for _ in range(SB_PAIRS)))
        dqs = lax.fori_loop(last, n + 1, up, init)[0]
        for p in range(SB_PAIRS):
            dq_ref[:, p * BLOCK:(p + 1) * BLOCK] = jnp.where(low, dqs[p][:BLOCK], dqs[p][BLOCK:])

    full = jax.ShapeDtypeStruct((s, pairs * BLOCK), F32)
    return _pcall(
        body, side=side, name=name, grid=(pairs // SB_PAIRS, nb),
        in_specs=[pl.BlockSpec((BLOCK, wide), lambda g, n: (n, qcb // SB_PAIRS + g)),
                  pl.BlockSpec((s, wide), lambda g, n: (0, kcb // SB_PAIRS + g), pipeline_mode=pl.Buffered(1)),
                  pl.BlockSpec((s, wide), lambda g, n: (0, vcb // SB_PAIRS + g), pipeline_mode=pl.Buffered(1)),
                  pl.BlockSpec((BLOCK, wide), lambda g, n: (n, docb // SB_PAIRS + g))],
        out_specs=[pl.BlockSpec((BLOCK, wide), lambda g, n: (n, g)),
                   pl.BlockSpec((s, wide), lambda g, n: (0, g), pipeline_mode=pl.Buffered(1)),
                   pl.BlockSpec((s, wide), lambda g, n: (0, g), pipeline_mode=pl.Buffered(1))],
        out_shape=[full, full, full],
        scratch_shapes=[pltpu.VMEM((SB_PAIRS, nb, 2 * BLOCK, 1), F32)],
        compiler_params=_params(),
    )(qkv, qkv, qkv, do)


def _xa_tile(dot, q, kv, qg, kg):
    hd = q.shape[1] // X_HEADS
    outs = []
    for h in range(X_HEADS):
        qh = _rms(q[:, h * hd:(h + 1) * hd], qg)
        kh = _rms(kv[:, h * hd:(h + 1) * hd], kg)
        vh = kv[:, (X_HEADS + h) * hd:(X_HEADS + h + 1) * hd]
        sc = dot(qh, kh, True) * (hd ** -0.5)
        m = lax.stop_gradient(jnp.max(sc, axis=-1, keepdims=True))
        p = jnp.exp(sc - m)
        outs.append(dot(p / jnp.sum(p, axis=-1, keepdims=True), vh, False))
    return jnp.concatenate(outs, axis=1)


def xa_core_fwd(q, kv, qg, kg, name):
    n, d = q.shape
    tm = _tile(n, 256, 8)
    (o,) = tcall(lambda ids, qt, kvt, qgt, kgt: (_xa_tile(_plain_dot, qt, kvt, qgt, kgt),), (n // tm,),
                 [_row(q, tm), _full(kv), _full(qg), _full(kg)], [_row_out(n, d, BF16, tm)], name)
    return o


def xa_core_bwd(q, kv, qg, kg, do, name):
    n, d = q.shape
    tm = _tile(n, 256, 8)

    def fn(ids, qt, kvt, qgt, kgt, dot_):
        _, vjp = jax.vjp(functools.partial(_xa_tile, _dot_vjp), qt, kvt, qgt, kgt)
        return vjp(dot_.astype(F32))

    return tcall(fn, (n // tm,), [_row(q, tm), _full(kv), _full(qg), _full(kg), _row(do, tm)],
                 [_row_out(n, d, BF16, tm), _acc_out(kv.shape), _acc_out(qg.shape), _acc_out(kg.shape)], name)


def _ev_reorder(a):
    return jnp.concatenate([a[..., 0:512], a[..., 768:2304], a[..., 512:768]], axis=-1)


def _ev_restore(a):
    return jnp.concatenate([a[..., 0:512], a[..., 2048:2304], a[..., 512:2048]], axis=-1)


_EV_SEGS = ((0, 512, "q"), (512, 1536, "raw"), (2048, 128, "k"), (2176, 128, "raw"))
_A_CFG = dict(hkv=A_KV_HEADS, grp=A_Q_HEADS // A_KV_HEADS, max_dist=BLOCK - 1, step=1.0, slopes=_alibi(A_Q_HEADS),
              want_lse=False)
_A_COLS = (lambda r: 0, lambda r: 16, lambda r: 17)


def even_mixer_fwd(x, h, w_in, qg, kg, sinks, w_out, tag, side=None):
    qkv = mm(h, w_in, "nn", tag + "_in")
    (ops,) = prep_fwd(qkv, qg, kg, _EV_SEGS, (1,), tag + "_prep")
    (o_a,) = banded_fwd(ops, 1, _A_COLS, sinks, _A_CFG, tag + "_swa")
    o_b = sb_fwd(ops, 4, 8, 12, tag + "_sb", side=side)
    carried = None
    if side is not None:
        o_b, carried = o_b
    o = jnp.concatenate([o_a, o_b], axis=1)
    y = mm(o, w_out, "nn", tag + "_out", res=x)
    return y, (x, h, qkv, ops, o), carried


def even_mixer_bwd(dy, saved, g, w_in, qg, kg, sinks, w_out, tag, side=None):
    x, h, qkv, ops, o = saved
    do = mm(dy, w_out, "nt", tag + "_do")
    d_wout = mm(o, dy, "tn", tag + "_dwout")
    dqa, dkp, dkc, dvp, dvc, dsinks = banded_bwd(ops, 1, _A_COLS, sinks, _A_CFG, [(do, lambda r: 0)], tag + "_dswa")
    res = sb_bwd(ops, 4, 8, 12, do, 4, tag + "_dsb", side=side)
    carried = None
    if side is not None:
        res, carried = res
    dqb, dkb, dvb = res
    dqkv, dqg, dkg = prep_bwd(
        qkv, qg, kg, _EV_SEGS,
        [(dqa, 0, 1, 0), (dqb, 0, 1, 1), (dkb, 0, 1, 2), (dvb, 0, 1, 3), (dkc, 0, 1, 4), (dkp, 1, 1, 4), (dvc, 0, 1, 5),
         (dvp, 1, 1, 5)],
        lambda *t: jnp.concatenate(t, axis=1), tag + "_dqkv")
    d_win = mm(h, dqkv, "tn", tag + "_dwin")
    dx, dg = mm_norm_bwd(dqkv, w_in, x, g, dy, tag + "_dh")
    return dx, dg, d_win, dqg, dkg, dsinks, d_wout, carried


def _c_cfg(window, dil):
    return dict(hkv=C_HEADS, grp=1, max_dist=window // dil, step=float(dil), slopes=_alibi(C_HEADS), want_lse=True)


_C_COLS = (lambda r: 3 * r, lambda r: 3 * r + 1, lambda r: 3 * r + 2)
_OD_SEGS = ((0, 1024, "q"), (1024, 1024, "k"), (2048, 1024, "raw"))


def _combine(o1, o2, o3, l1, l2, l3):
    m = lax.stop_gradient(jnp.maximum(jnp.maximum(l1, l2), l3))
    e1, e2, e3 = jnp.exp(l1 - m), jnp.exp(l2 - m), jnp.exp(l3 - m)
    tot = e1 + e2 + e3
    return (e1 / tot) * o1 + (e2 / tot) * o2 + (e3 / tot) * o3


def odd_mixer_fwd(x, g, w_in, qg, kg, w_out, tag):
    n, d = x.shape
    h = rmsnorm_fwd(x, g, tag + "_norm")
    qkv = mm(h, w_in, "nn", tag + "_in")
    dils = [dil for _, dil in C_PATTERNS]
    ops = prep_fwd(qkv, qg, kg, _OD_SEGS, dils, tag + "_prep")
    os_, ls_ = [], []
    for (window, dil), ops_d in zip(C_PATTERNS, ops):
        o_p, l_p = banded_fwd(ops_d, dil, _C_COLS, None, _c_cfg(window, dil), f"{tag}_dil{dil}")
        os_.append(o_p)
        ls_.append(l_p)
    tm = BLOCK
    lay = lambda a, dil: _in(a, (tm // dil, a.shape[1]), lambda i: (i, 0))
    views = [lay(a, dil) for a, dil in zip(os_ + ls_, dils + dils)]

    def comb(ids, *t, scratch):
        return (_combine(*[_to_natural(scratch, a, dil) for a, dil in zip(t, dils + dils)]),)

    (o,) = tcall(comb, (n // tm,), views, [_row_out(n, d, BF16, tm)], tag + "_comb",
                 scratch=((d // BLOCK * tm, BLOCK), F32))
    y = mm(o, w_out, "nn", tag + "_out", res=x)
    return y, (x, h, qkv, ops, views, o)


def odd_mixer_bwd(dy, saved, g, w_in, qg, kg, w_out, tag):
    x, h, qkv, ops, views, o = saved
    n, d = x.shape
    do = mm(dy, w_out, "nt", tag + "_do")
    d_wout = mm(o, dy, "tn", tag + "_dwout")
    tm = BLOCK
    dils = [dil for _, dil in C_PATTERNS]

    def comb_bwd(ids, *t, scratch):
        _, vjp = jax.vjp(_combine, *[_to_natural(scratch, a, dil) for a, dil in zip(t[:6], dils + dils)])
        return tuple(_to_strided(scratch, c, dil) for c, dil in zip(vjp(t[6]), dils + dils))

    cts = tcall(comb_bwd, (n // tm,), views + [_row(do, tm)],
                [_out((n // dil, dil * d), F32, (tm // dil, dil * d), lambda i: (i, 0)) for dil in dils + dils],
                tag + "_dcomb", scratch=((d // BLOCK * tm, BLOCK), F32))
    dqs, dks, dvs = [], [], []
    for p, ((window, dil), ops_d) in enumerate(zip(C_PATTERNS, ops)):
        dq, dkp, dkc, dvp, dvc = banded_bwd(ops_d, dil, _C_COLS, None, _c_cfg(window, dil),
                                            [(cts[p], lambda r: r), (cts[3 + p], lambda r: r)], f"{tag}_ddil{dil}")
        dqs.append((dq, 0, dil, p))
        dks += [(dkc, 0, dil, 3 + p), (dkp, dil, dil, 3 + p)]
        dvs += [(dvc, 0, dil, 6 + p), (dvp, dil, dil, 6 + p)]

    def gather(*t):
        return jnp.concatenate([t[0] + t[1] + t[2], t[3] + t[4] + t[5], t[6] + t[7] + t[8]], axis=1)

    dqkv, dqg, dkg = prep_bwd(qkv, qg, kg, _OD_SEGS, dqs + dks + dvs, gather, tag + "_dqkv")
    d_win = mm(h, dqkv, "tn", tag + "_dwin")
    dx, dg = mm_norm_bwd(dqkv, w_in, x, g, dy, tag + "_dh")
    return dx, dg, d_win, dqg, dkg, d_wout


def xa_fwd(x, mem, g, gm, w_q, w_kv, qg, kg, w_o, tag):
    h = rmsnorm_fwd(x, g, tag + "_norm")
    q = mm(h, w_q, "nn", tag + "_q")
    mn = rmsnorm_fwd(mem, gm, tag + "_mnorm")
    kv = mm(mn, w_kv, "nn", tag + "_kv")
    o = xa_core_fwd(q, kv, qg, kg, tag + "_core")
    y = mm(o, w_o, "nn", tag + "_o", res=x)
    return y, (x, h, q, mn, kv, o)


def xa_bwd(dy, saved, mem, g, gm, w_q, w_kv, qg, kg, w_o, tag):
    x, h, q, mn, kv, o = saved
    do = mm(dy, w_o, "nt", tag + "_do", out_dtype=BF16)
    d_wo = mm(o, dy, "tn", tag + "_dwo")
    dq, dkv, dqg, dkg = xa_core_bwd(q, kv, qg, kg, do, tag + "_dcore")
    d_wq = mm(h, dq, "tn", tag + "_dwq")
    dx, dg = mm_norm_bwd(dq, w_q, x, g, dy, tag + "_dh")
    d_wkv = mm(mn, dkv, "tn", tag + "_dwkv")
    _, dgm = mm_norm_bwd(dkv, w_kv, mem, gm, None, tag + "_dmn")
    return dx, dg, dgm, d_wq, d_wkv, dqg, dkg, d_wo


def loss_head(y, target, name):
    n, d = y.shape
    tm = _tile(n, 512, 8)

    def fn(ids, yt, tt):
        e = yt - tt
        return e * (1.0 / d), jnp.sum(e * e, axis=0, keepdims=True)

    return tcall(fn, (n // tm,), [_row(y, tm), _row(target, tm)], [_row_out(n, d, F32, tm), _acc_out((1, d))], name)


_ANY = pl.BlockSpec(memory_space=pl.ANY)


def all_gather_blocks(blocks):
    nb = len(blocks)

    def body(*refs):
        x_refs, out_refs = refs[:nb], refs[nb:2 * nb]
        send_sems, recv_sems, local_sems = refs[2 * nb:]
        x, y, c = lax.axis_index("x"), lax.axis_index("y"), lax.axis_index("c")
        me, sibling = (x, y, c), (x, y, 1 - c)
        over_x, over_y, diagonal = (1 - x, y), (x, 1 - y), (1 - x, 1 - y)
        relay_of = ((1 - x) * (1 - c) + x * c, y * (1 - c) + (1 - y) * c)
        relay_to = (x * (1 - c) + (1 - x) * c, (1 - y) * (1 - c) + y * c)

        def copy(b, k, blk, to, own=False):
            px, py, pc = blk
            slot = out_refs[b].at[4 * px + 2 * py + pc]
            return pltpu.make_async_remote_copy(
                src_ref=x_refs[b] if own else slot, dst_ref=slot,
                send_sem=send_sems.at[7 * b + k], recv_sem=recv_sems.at[7 * b + k], device_id=to, device_id_type=MESH)

        mine = [pltpu.make_async_copy(x_refs[b], out_refs[b].at[4 * x + 2 * y + c], local_sems.at[b]) for b in range(nb)]
        for cp in mine:
            cp.start()
        sent = []
        for b in range(nb):
            sent += [copy(b, 0, me, sibling, own=True), copy(b, 1, me, (*over_x, c), own=True),
                     copy(b, 2, me, (*over_y, c), own=True)]
        for cp in sent:
            cp.start()
        for b in range(nb):
            copy(b, 1, (*over_x, c), me).wait_recv()
            copy(b, 2, (*over_y, c), me).wait_recv()
            later = [copy(b, 3, (*relay_of, c), (*relay_to, c)), copy(b, 4, (*over_x, c), sibling),
                     copy(b, 5, (*over_y, c), sibling)]
            for cp in later:
                cp.start()
            sent += later
        for b in range(nb):
            copy(b, 3, (*diagonal, c), me).wait_recv()
            fwd = copy(b, 6, (*diagonal, c), sibling)
            fwd.start()
            sent.append(fwd)
        for b in range(nb):
            copy(b, 0, sibling, me).wait_recv()
            for k, chip in ((4, over_x), (5, over_y), (6, diagonal)):
                copy(b, k, (*chip, 1 - c), me).wait_recv()
        for cp in sent:
            cp.wait_send()
        for cp in mine:
            cp.wait()

    return _pcall(
        body, name="weights_all_gather",
        in_specs=[_ANY] * nb, out_specs=[_ANY] * nb,
        out_shape=[jax.ShapeDtypeStruct((N_DEV,) + a.shape, a.dtype) for a in blocks],
        scratch_shapes=[pltpu.SemaphoreType.DMA((7 * nb,)), pltpu.SemaphoreType.DMA((7 * nb,)),
                        pltpu.SemaphoreType.DMA((nb,))],
    )(*blocks)


def pair_exchange(bufs):
    nb = len(bufs)

    def body(*refs):
        srcs, dsts = refs[:nb], refs[nb:2 * nb]
        send_sems, recv_sems = refs[2 * nb:]
        x, y, c = lax.axis_index("x"), lax.axis_index("y"), lax.axis_index("c")
        copies = []
        for b in range(nb):
            for j in range(4):
                cp = pltpu.make_async_remote_copy(
                    src_ref=srcs[b].at[2 * j + (1 - c)], dst_ref=dsts[b].at[j], send_sem=send_sems.at[4 * b + j],
                    recv_sem=recv_sems.at[4 * b + j], device_id=(x, y, 1 - c), device_id_type=MESH)
                cp.start()
                copies.append(cp)
        for cp in copies:
            cp.wait()

    return _pcall(
        body, name="grads_pair_exchange",
        in_specs=[_ANY] * nb, out_specs=[_ANY] * nb,
        out_shape=[jax.ShapeDtypeStruct((4,) + a.shape[1:], a.dtype) for a in bufs],
        scratch_shapes=[pltpu.SemaphoreType.DMA((4 * nb,)), pltpu.SemaphoreType.DMA((4 * nb,))],
    )(*bufs)


def pair_sum(g, got, c, out_dtype, name):
    r, w = g.shape[1:]
    tr = _tile(r, 512, 16)

    def body(c_ref, a_ref, b_ref, o_ref):
        o_ref[...] = (a_ref[...].astype(F32) + b_ref[...].astype(F32)).astype(o_ref.dtype)

    return _pcall(
        body, name=name,
        grid_spec=pltpu.PrefetchScalarGridSpec(
            num_scalar_prefetch=1, grid=(4, r // tr),
            in_specs=[pl.BlockSpec((None, tr, w), lambda j, i, c_ref: (2 * j + c_ref[0], i, 0)),
                      pl.BlockSpec((None, tr, w), lambda j, i, c_ref: (j, i, 0))],
            out_specs=pl.BlockSpec((None, tr, w), lambda j, i, c_ref: (j, i, 0))),
        out_shape=jax.ShapeDtypeStruct((4,) + g.shape[1:], out_dtype),
        compiler_params=_params(),
    )(c, g, got)


def chip_exchange(parts):
    nb = len(parts)

    def body(*refs):
        srcs, dsts = refs[:nb], refs[nb:2 * nb]
        send_sems, recv_sems, local_sems = refs[2 * nb:]
        x, y, c = lax.axis_index("x"), lax.axis_index("y"), lax.axis_index("c")
        my_chip = 2 * x + y
        copies = []
        for b in range(nb):
            mine = pltpu.make_async_copy(srcs[b].at[my_chip], dsts[b].at[my_chip], local_sems.at[b])
            mine.start()
            copies.append(mine)
            for k, (tx, ty) in enumerate([(1 - x, y), (x, 1 - y), (1 - x, 1 - y)]):
                cp = pltpu.make_async_remote_copy(
                    src_ref=srcs[b].at[2 * tx + ty], dst_ref=dsts[b].at[my_chip], send_sem=send_sems.at[3 * b + k],
                    recv_sem=recv_sems.at[3 * b + k], device_id=(tx, ty, c), device_id_type=MESH)
                cp.start()
                copies.append(cp)
        for cp in copies:
            cp.wait()

    return _pcall(
        body, name="grads_chip_exchange",
        in_specs=[_ANY] * nb, out_specs=[_ANY] * nb,
        out_shape=[jax.ShapeDtypeStruct(a.shape, a.dtype) for a in parts],
        scratch_shapes=[pltpu.SemaphoreType.DMA((3 * nb,)), pltpu.SemaphoreType.DMA((3 * nb,)),
                        pltpu.SemaphoreType.DMA((nb,))],
    )(*parts)


def chip_sum(parts, name):
    r, w = parts.shape[1:]
    tr = _tile(r, 512, 16)
    spec = lambda j: _in(parts, (None, tr, w), lambda i, j=j: (j, i, 0))

    def fn(ids, a, b, c_, d):
        a, b, c_, d = [t.astype(F32) for t in (a, b, c_, d)]
        return (((a + b) + c_) + d,)

    (out,) = tcall(fn, (r // tr,), [spec(j) for j in range(4)],
                   [_out((r, w), F32, (tr, w), lambda i: (i, 0))], name)
    return out


def _remote(src, dst, send_sems, recv_sems, k, to):
    return functools.partial(pltpu.make_async_remote_copy, src_ref=src, dst_ref=dst, send_sem=send_sems.at[k],
                             recv_sem=recv_sems.at[k], device_id=to, device_id_type=MESH)


def _gather_plan(phase, nb):
    def plan(ins, outs, send_sems, recv_sems, local_sems):
        x, y, c = lax.axis_index("x"), lax.axis_index("y"), lax.axis_index("c")
        me, sibling = (x, y, c), (x, y, 1 - c)
        over_x, over_y, diagonal = (1 - x, y), (x, 1 - y), (1 - x, 1 - y)
        relay_of = ((1 - x) * (1 - c) + x * c, y * (1 - c) + (1 - y) * c)
        relay_to = (x * (1 - c) + (1 - x) * c, (1 - y) * (1 - c) + y * c)
        local, sends, recvs = [], [], []
        for b in range(nb):
            slot = lambda chip, core, b=b: outs[b].at[4 * chip[0] + 2 * chip[1] + core]
            if phase == 0:
                local.append(functools.partial(pltpu.make_async_copy, ins[b], slot((x, y), c), local_sems.at[b]))
                moves = [(ins[b], slot((x, y), c), to) for to in (sibling, (*over_x, c), (*over_y, c))]
                arrive = [slot((x, y), 1 - c), slot(over_x, c), slot(over_y, c)]
            elif phase == 1:
                moves = [(slot(relay_of, c), slot(relay_of, c), (*relay_to, c)),
                         (slot(over_x, c), slot(over_x, c), sibling), (slot(over_y, c), slot(over_y, c), sibling)]
                arrive = [slot(diagonal, c), slot(over_x, 1 - c), slot(over_y, 1 - c)]
            else:
                moves = [(slot(diagonal, c), slot(diagonal, c), sibling)]
                arrive = [slot(diagonal, 1 - c)]
            sends += [_remote(src, dst, send_sems, recv_sems, 3 * b + k, to) for k, (src, dst, to) in enumerate(moves)]
            recvs += [_remote(dst, dst, send_sems, recv_sems, 3 * b + k, me) for k, dst in enumerate(arrive)]
        return local, sends, recvs
    return plan


def gather_side(phase, arrays):
    nb = len(arrays)
    if phase == 0:
        shapes = [jax.ShapeDtypeStruct((N_DEV,) + a.shape, a.dtype) for a in arrays]
        return Side(arrays, shapes, 3 * nb, nb, _gather_plan(0, nb))
    shapes = [jax.ShapeDtypeStruct(a.shape, a.dtype) for a in arrays]
    return Side(arrays, shapes, 3 * nb, 0, _gather_plan(phase, nb), aliased=True)


def pair_side(bufs):
    nb = len(bufs)

    def plan(ins, outs, send_sems, recv_sems, local_sems):
        x, y, c = lax.axis_index("x"), lax.axis_index("y"), lax.axis_index("c")
        sends = [_remote(ins[b].at[2 * j + (1 - c)], outs[b].at[j], send_sems, recv_sems, 4 * b + j, (x, y, 1 - c))
                 for b in range(nb) for j in range(4)]
        recvs = [_remote(outs[b].at[j], outs[b].at[j], send_sems, recv_sems, 4 * b + j, (x, y, c))
                 for b in range(nb) for j in range(4)]
        return [], sends, recvs

    shapes = [jax.ShapeDtypeStruct((4,) + a.shape[1:], a.dtype) for a in bufs]
    return Side(bufs, shapes, 4 * nb, 0, plan)


def chip_side(parts):
    nb = len(parts)

    def plan(ins, outs, send_sems, recv_sems, local_sems):
        x, y, c = lax.axis_index("x"), lax.axis_index("y"), lax.axis_index("c")
        my_chip = 2 * x + y
        peers = [(1 - x, y), (x, 1 - y), (1 - x, 1 - y)]
        local = [functools.partial(pltpu.make_async_copy, ins[b].at[my_chip], outs[b].at[my_chip], local_sems.at[b])
                 for b in range(nb)]
        sends = [_remote(ins[b].at[2 * tx + ty], outs[b].at[my_chip], send_sems, recv_sems, 3 * b + k, (tx, ty, c))
                 for b in range(nb) for k, (tx, ty) in enumerate(peers)]
        recvs = [_remote(outs[b].at[2 * tx + ty], outs[b].at[2 * tx + ty], send_sems, recv_sems, 3 * b + k, (x, y, c))
                 for b in range(nb) for k, (tx, ty) in enumerate(peers)]
        return local, sends, recvs

    shapes = [jax.ShapeDtypeStruct(a.shape, a.dtype) for a in parts]
    return Side(parts, shapes, 3 * nb, nb, plan)


def adamw(w, g, m, v, name):
    shape = w.shape
    cols = shape[-1]
    rows = int(np.prod(shape[:-1]))
    w2, g2, m2, v2 = [a.reshape(rows, cols) for a in (w, g, m, v)]
    tr = _tile(rows, 256, 8) if rows % 8 == 0 else rows

    def fn(ids, wt, gt, mt, vt):
        m_new = ADAM_B1 * mt + (1.0 - ADAM_B1) * gt
        v_new = ADAM_B2 * vt + (1.0 - ADAM_B2) * (gt * gt)
        m_hat = m_new / (1.0 - ADAM_B1 ** ADAM_STEP)
        v_hat = v_new / (1.0 - ADAM_B2 ** ADAM_STEP)
        delta = -ADAM_LR * (m_hat / (jnp.sqrt(v_hat) + ADAM_EPS) + ADAM_WD * wt)
        return delta, m_new, v_new

    res = tcall(fn, (rows // tr,), [_row(a, tr) for a in (w2, g2, m2, v2)],
                [_row_out(rows, cols, F32, tr) for _ in range(3)], name)
    return [a.reshape(shape) for a in res]


_MATS = [("ffn1_w_gu", "col"), ("ffn1_w_down", "row"), ("ev_w_in", "col"), ("ev_w_out", "row"),
         ("od_w_in", "col"), ("od_w_out", "row"), ("xa_w_q", "row"), ("xa_w_kv", "col"), ("xa_w_o", "row"),
         ("ffn2_w_gu", "col"), ("ffn2_w_down", "row")]
_VECS = ["ffn1_norm", "mix_norm", "ev_q_gain", "ev_k_gain", "ev_sinks", "od_q_gain", "od_k_gain", "xa_norm",
         "xa_mem_norm", "xa_q_gain", "xa_k_gain", "ffn2_norm"]
_WEIGHTS = ["ffn1_norm", "ffn1_w_gu", "ffn1_w_down", "mix_norm", "ev_w_in", "ev_q_gain", "ev_k_gain", "ev_sinks",
            "ev_w_out", "od_w_in", "od_q_gain", "od_k_gain", "od_w_out", "xa_norm", "xa_mem_norm", "xa_w_q", "xa_w_kv",
            "xa_q_gain", "xa_k_gain", "xa_w_o", "ffn2_norm", "ffn2_w_gu", "ffn2_w_down"]


_AXIS = dict(_MATS)
DEPTH = 2


def _layer_groups(l):
    w_in, w_out = ("ev_w_in", "ev_w_out") if l % 2 == 0 else ("od_w_in", "od_w_out")
    return [[("ffn1_w_gu", l), ("ffn2_w_gu", l)], [(w_in, l // 2)], [("xa_w_kv", l)],
            [("ffn1_w_down", l), ("ffn2_w_down", l), (w_out, l // 2), ("xa_w_q", l), ("xa_w_o", l)]]


def _first_block_groups(l):
    first = [[("ffn1_w_gu", l)], [("ffn1_w_down", l)]]
    rest = [[m for m in group if m[0] not in ("ffn1_w_gu", "ffn1_w_down")] for group in _layer_groups(l)]
    return first, rest


def _weight_blocks(shards, groups):
    blocks = []
    for group in groups:
        rows = [shards[n][j].astype(BF16) for n, j in group]
        blocks.append(rows[0] if len(rows) == 1 else jnp.concatenate(rows, axis=0))
    return blocks


def _whole_weights(shards, groups, gathered):
    full = {}
    for group, got in zip(groups, gathered):
        off = 0
        for n, j in group:
            a, b = shards[n].shape[1:]
            seg = got[:, off:off + a, :]
            off += a
            full[n] = seg.reshape(N_DEV * a, b) if _AXIS[n] == "row" else seg.transpose(1, 0, 2).reshape(a, N_DEV * b)
    return full


def _gradient_buffers(grads, groups):
    bufs = []
    for group in groups:
        rows = []
        for n, _ in group:
            if isinstance(grads[n], tuple):
                for half in grads[n]:
                    a, b = half.shape
                    rows.append(half.reshape(a, N_DEV // 2, 2 * b // N_DEV).transpose(1, 0, 2))
                rows[-2:] = [jnp.concatenate(rows[-2:], axis=0)]
                continue
            a, b = grads[n].shape
            if _AXIS[n] == "row":
                rows.append(grads[n].reshape(N_DEV, a // N_DEV, b))
            else:
                rows.append(grads[n].reshape(a, N_DEV, b // N_DEV).transpose(1, 0, 2))
        bufs.append((rows[0] if len(rows) == 1 else jnp.concatenate(rows, axis=1)).astype(BF16))
    return bufs


def _gradient_blocks(shards, groups, sums):
    out = {}
    for group, tot in zip(groups, sums):
        off = 0
        for n, j in group:
            a = shards[n].shape[1]
            out[n, j] = tot[off:off + a]
            off += a
    return out


class _PairChain:
    def __init__(self, ex, bufs):
        self.ex, self.bufs, self.parts = ex, bufs, None

    def side(self, name):
        return pair_side(self.bufs) if name == "dwd" else None

    def done(self, name, carried):
        self.parts = self.ex.pair_sums(self.bufs, carried, "l1")


class _RestChain:
    HALF = {"dwd": (0, 2), "dwgu": (1, 3)}

    def __init__(self, ex, bufs):
        self.ex, self.bufs, self.parts, self.sums = ex, bufs, None, [None] * len(bufs)

    def side(self, name):
        if name == "da":
            return pair_side(self.bufs)
        return chip_side([self.parts[i] for i in self.HALF[name]])

    def done(self, name, carried):
        if name == "da":
            self.parts = self.ex.pair_sums(self.bufs, carried, "l0r")
        else:
            for i, tot in zip(self.HALF[name], self.ex.chip_sums(carried, "l0r_" + name)):
                self.sums[i] = tot


class _Exchange:
    def __init__(self, shards, c):
        self.shards, self.c = shards, c

    def weights_first(self):
        first, _ = _first_block_groups(0)
        return _whole_weights(self.shards, first, all_gather_blocks(_weight_blocks(self.shards, first)))

    def rest_blocks(self):
        return _weight_blocks(self.shards, _first_block_groups(0)[1])

    def weights_rest(self, gathered):
        return _whole_weights(self.shards, _first_block_groups(0)[1], gathered)

    def gather_start(self):
        return gather_side(0, _weight_blocks(self.shards, _layer_groups(1)))

    def weights_next(self, gathered):
        return _whole_weights(self.shards, _layer_groups(1), gathered)

    def chain_next(self, grads):
        return _PairChain(self, _gradient_buffers(grads, _layer_groups(1)))

    def chain_rest(self, grads):
        return _RestChain(self, _gradient_buffers(grads, _first_block_groups(0)[1]))

    def pair_sums(self, bufs, got, tag):
        return [pair_sum(b, g, self.c, b.dtype, f"grads_pair_sum_{tag}_{i}") for i, (b, g) in enumerate(zip(bufs, got))]

    def chip_sums(self, parts, tag):
        return [chip_sum(p, f"grads_chip_sum_{tag}_{i}") for i, p in enumerate(parts)]

    def finish(self, gm, gv, sums1, sums_rest):
        vecs = {n: jnp.concatenate(v, axis=0) for n, v in gv.items()}
        first, rest = _first_block_groups(0)
        bufs = _gradient_buffers(gm[0], first)
        vec = jnp.concatenate([vecs[n].reshape(-1) for n in _VECS])
        vec = jnp.pad(vec, (0, -vec.shape[0] % (16 * LANES)))
        bufs.append(jnp.broadcast_to(vec.reshape(1, -1, LANES), (N_DEV, vec.shape[0] // LANES, LANES)))
        parts = self.pair_sums(bufs, pair_exchange(bufs), "l0")
        sums0 = self.chip_sums(chip_exchange(parts), "l0")
        blocks = {**_gradient_blocks(self.shards, first, sums0[:-1]), **_gradient_blocks(self.shards, rest, sums_rest),
                  **_gradient_blocks(self.shards, _layer_groups(1), sums1)}
        out = {n: jnp.stack([blocks[n, j] for j in range(self.shards[n].shape[0])]) for n, _ in _MATS}
        flat, off = sums0[-1].reshape(-1), 0
        for n in _VECS:
            out[n] = flat[off:off + vecs[n].size].reshape(vecs[n].shape)
            off += vecs[n].size
        return out


class _NoExchange:
    def __init__(self, full):
        self.full = full

    def weights_first(self):
        return self.full[0]

    def rest_blocks(self):
        return None

    def gather_start(self):
        return None

    def weights_next(self, gathered):
        return self.full[1]

    def chain_next(self, grads):
        return None

    def chain_rest(self, grads):
        return None

    def finish(self, gm, gv, sums1, sums_rest):
        mats = {}
        for l in range(DEPTH):
            for group in _layer_groups(l):
                for n, j in group:
                    whole = gm[l][n]
                    mats.setdefault(n, {})[j] = jnp.concatenate(whole, axis=1) if isinstance(whole, tuple) else whole
        mats = {n: jnp.stack([v[j] for j in sorted(v)]) for n, v in mats.items()}
        return mats, {n: jnp.concatenate(v, axis=0) for n, v in gv.items()}


def _local_step(x, mem, target, w, ex):
    assert w["ffn1_norm"].shape[0] == DEPTH
    row = lambda a, l: a[l:l + 1]
    full = [ex.weights_first(), None]
    saved = []
    for l in range(DEPTH):
        t, j, f = f"l{l}", l // 2, full[l]
        rest = ex.rest_blocks() if l == 0 else None
        if rest is None:
            x, s1 = ffn_fwd(x, row(w["ffn1_norm"], l), f["ffn1_w_gu"], f["ffn1_w_down"], t + "_ffn1")
        else:
            x, s1, rest = ffn_fwd(x, row(w["ffn1_norm"], l), f["ffn1_w_gu"], f["ffn1_w_down"], t + "_ffn1", (0, rest))
        relay = None
        if l % 2 == 0:
            h = rmsnorm_fwd(x, row(w["mix_norm"], l), t + "_ev_norm", None if rest is None else gather_side(2, rest))
            if rest is not None:
                h, rest = h
                f = full[l] = {**f, **ex.weights_rest(rest)}
            side = ex.gather_start() if l + 1 < DEPTH else None
            x, s2, relay = even_mixer_fwd(x, h, _ev_reorder(f["ev_w_in"]), row(w["ev_q_gain"], j),
                                          row(w["ev_k_gain"], j), row(w["ev_sinks"], j), f["ev_w_out"], t + "_ev", side)
        else:
            x, s2 = odd_mixer_fwd(x, row(w["mix_norm"], l), f["od_w_in"], row(w["od_q_gain"], j),
                                  row(w["od_k_gain"], j), f["od_w_out"], t + "_od")
        x, s3 = xa_fwd(x, mem, row(w["xa_norm"], l), row(w["xa_mem_norm"], l), f["xa_w_q"], f["xa_w_kv"],
                       row(w["xa_q_gain"], l), row(w["xa_k_gain"], l), f["xa_w_o"], t + "_xa")
        if relay is None:
            x, s4 = ffn_fwd(x, row(w["ffn2_norm"], l), f["ffn2_w_gu"], f["ffn2_w_down"], t + "_ffn2")
        else:
            x, s4, relay = ffn_fwd(x, row(w["ffn2_norm"], l), f["ffn2_w_gu"], f["ffn2_w_down"], t + "_ffn2", (1, relay))
        if l + 1 < DEPTH:
            full[l + 1] = ex.weights_next(relay)
        saved.append((s1, s2, s3, s4))
    dx, sq = loss_head(x, target, "loss_head")
    loss = 0.5 * jnp.sum(sq) / x.shape[1]

    gm = [dict() for _ in range(DEPTH)]
    gv = {n: [None] * w[n].shape[0] for n in _VECS}
    chain1 = chain0 = sums1 = None
    for l in reversed(range(DEPTH)):
        t, j, f = f"l{l}", l // 2, full[l]
        s1, s2, s3, s4 = saved[l]
        dx, gv["ffn2_norm"][l], gm[l]["ffn2_w_gu"], gm[l]["ffn2_w_down"] = ffn_bwd(
            dx, s4, row(w["ffn2_norm"], l), f["ffn2_w_gu"], f["ffn2_w_down"], t + "_ffn2", chain1 if l == 0 else None)
        parts = chain1.parts if l == 0 and chain1 is not None else None
        (dx, gv["xa_norm"][l], gv["xa_mem_norm"][l], gm[l]["xa_w_q"], gm[l]["xa_w_kv"], gv["xa_q_gain"][l],
         gv["xa_k_gain"][l], gm[l]["xa_w_o"]) = xa_bwd(
            dx, s3, mem, row(w["xa_norm"], l), row(w["xa_mem_norm"], l), f["xa_w_q"], f["xa_w_kv"],
            row(w["xa_q_gain"], l), row(w["xa_k_gain"], l), f["xa_w_o"], t + "_xa")
        if l % 2 == 0:
            (dx, gv["mix_norm"][l], d_win, gv["ev_q_gain"][j], gv["ev_k_gain"][j], gv["ev_sinks"][j],
             gm[l]["ev_w_out"], carried) = even_mixer_bwd(
                dx, s2, row(w["mix_norm"], l), _ev_reorder(f["ev_w_in"]), row(w["ev_q_gain"], j), row(w["ev_k_gain"], j),
                row(w["ev_sinks"], j), f["ev_w_out"], t + "_ev", None if parts is None else chip_side(parts))
            gm[l]["ev_w_in"] = _ev_restore(d_win)
            if carried is not None:
                sums1 = ex.chip_sums(carried, "l1")
        else:
            (dx, gv["mix_norm"][l], gm[l]["od_w_in"], gv["od_q_gain"][j], gv["od_k_gain"][j],
             gm[l]["od_w_out"]) = odd_mixer_bwd(
                dx, s2, row(w["mix_norm"], l), f["od_w_in"], row(w["od_q_gain"], j), row(w["od_k_gain"], j),
                f["od_w_out"], t + "_od")
        if l == 0:
            chain0 = ex.chain_rest(gm[l])
        dx, gv["ffn1_norm"][l], gm[l]["ffn1_w_gu"], gm[l]["ffn1_w_down"] = ffn_bwd(
            dx, s1, row(w["ffn1_norm"], l), f["ffn1_w_gu"], f["ffn1_w_down"], t + "_ffn1", chain0 if l == 0 else None)
        if l == 1:
            chain1 = ex.chain_next(gm[l])
    return loss, dx, ex.finish(gm, gv, sums1, None if chain0 is None else chain0.sums)


def kernel(x, mem, ffn1_norm, ffn1_w_gu, ffn1_w_down, mix_norm, ev_w_in, ev_q_gain, ev_k_gain, ev_sinks, ev_w_out, od_w_in, od_q_gain, od_k_gain, od_w_out, xa_norm, xa_mem_norm, xa_w_q, xa_w_kv, xa_q_gain, xa_k_gain, xa_w_o, ffn2_norm, ffn2_w_gu, ffn2_w_down, loss_target, m_ffn1_norm, m_ffn1_w_gu, m_ffn1_w_down, m_mix_norm, m_ev_w_in, m_ev_q_gain, m_ev_k_gain, m_ev_sinks, m_ev_w_out, m_od_w_in, m_od_q_gain, m_od_k_gain, m_od_w_out, m_xa_norm, m_xa_mem_norm, m_xa_w_q, m_xa_w_kv, m_xa_q_gain, m_xa_k_gain, m_xa_w_o, m_ffn2_norm, m_ffn2_w_gu, m_ffn2_w_down, v_ffn1_norm, v_ffn1_w_gu, v_ffn1_w_down, v_mix_norm, v_ev_w_in, v_ev_q_gain, v_ev_k_gain, v_ev_sinks, v_ev_w_out, v_od_w_in, v_od_q_gain, v_od_k_gain, v_od_w_out, v_xa_norm, v_xa_mem_norm, v_xa_w_q, v_xa_w_kv, v_xa_q_gain, v_xa_k_gain, v_xa_w_o, v_ffn2_norm, v_ffn2_w_gu, v_ffn2_w_down):
    w = dict(ffn1_norm=ffn1_norm, ffn1_w_gu=ffn1_w_gu, ffn1_w_down=ffn1_w_down, mix_norm=mix_norm, ev_w_in=ev_w_in, ev_q_gain=ev_q_gain, ev_k_gain=ev_k_gain, ev_sinks=ev_sinks, ev_w_out=ev_w_out, od_w_in=od_w_in, od_q_gain=od_q_gain, od_k_gain=od_k_gain, od_w_out=od_w_out, xa_norm=xa_norm, xa_mem_norm=xa_mem_norm, xa_w_q=xa_w_q, xa_w_kv=xa_w_kv, xa_q_gain=xa_q_gain, xa_k_gain=xa_k_gain, xa_w_o=xa_w_o, ffn2_norm=ffn2_norm, ffn2_w_gu=ffn2_w_gu, ffn2_w_down=ffn2_w_down)
    m = dict(ffn1_norm=m_ffn1_norm, ffn1_w_gu=m_ffn1_w_gu, ffn1_w_down=m_ffn1_w_down, mix_norm=m_mix_norm, ev_w_in=m_ev_w_in, ev_q_gain=m_ev_q_gain, ev_k_gain=m_ev_k_gain, ev_sinks=m_ev_sinks, ev_w_out=m_ev_w_out, od_w_in=m_od_w_in, od_q_gain=m_od_q_gain, od_k_gain=m_od_k_gain, od_w_out=m_od_w_out, xa_norm=m_xa_norm, xa_mem_norm=m_xa_mem_norm, xa_w_q=m_xa_w_q, xa_w_kv=m_xa_w_kv, xa_q_gain=m_xa_q_gain, xa_k_gain=m_xa_k_gain, xa_w_o=m_xa_w_o, ffn2_norm=m_ffn2_norm, ffn2_w_gu=m_ffn2_w_gu, ffn2_w_down=m_ffn2_w_down)
    v = dict(ffn1_norm=v_ffn1_norm, ffn1_w_gu=v_ffn1_w_gu, ffn1_w_down=v_ffn1_w_down, mix_norm=v_mix_norm, ev_w_in=v_ev_w_in, ev_q_gain=v_ev_q_gain, ev_k_gain=v_ev_k_gain, ev_sinks=v_ev_sinks, ev_w_out=v_ev_w_out, od_w_in=v_od_w_in, od_q_gain=v_od_q_gain, od_k_gain=v_od_k_gain, od_w_out=v_od_w_out, xa_norm=v_xa_norm, xa_mem_norm=v_xa_mem_norm, xa_w_q=v_xa_w_q, xa_w_kv=v_xa_w_kv, xa_q_gain=v_xa_q_gain, xa_k_gain=v_xa_k_gain, xa_w_o=v_xa_w_o, ffn2_norm=v_ffn2_norm, ffn2_w_gu=v_ffn2_w_gu, ffn2_w_down=v_ffn2_w_down)

    c = lax.axis_index("c").astype(jnp.int32).reshape(1)
    loss, dx, grads = _local_step(x[0], mem[0], loss_target[0], w, _Exchange(w, c))
    loss = lax.psum(loss, ("x", "y", "c"))

    delta, new_m, new_v = {}, {}, {}
    for n in _WEIGHTS:
        delta[n], new_m[n], new_v[n] = adamw(w[n], grads[n], m[n], v[n], "adamw_" + n)
    return (loss, dx[None], *[grads[n] for n in _WEIGHTS], *[delta[n] for n in _WEIGHTS],
            *[new_m[n] for n in _WEIGHTS], *[new_v[n] for n in _WEIGHTS])
```

```python
import functools

import numpy as np
import jax
import jax.numpy as jnp
from jax import lax
from jax.experimental import pallas as pl
from jax.experimental.pallas import tpu as pltpu

F32 = jnp.float32
BF16 = jnp.bfloat16
MESH = pl.DeviceIdType.MESH

HEAD_DIM = 64
BLOCK = 128
RMS_EPS = 1e-6
A_Q_HEADS, A_KV_HEADS = 8, 2
B_HEADS = 8
C_HEADS = 16
C_PATTERNS = ((128, 1), (512, 4), (2048, 16))
X_HEADS = 4
N_DEV = 8
LANES = 1024
VMEM_LIMIT_BYTES = 56 * 1024 * 1024
SB_SKIP_LOG = -110.0
NEG_BIG = -1e30

ADAM_LR, ADAM_B1, ADAM_B2, ADAM_EPS, ADAM_WD, ADAM_STEP = 0.001, 0.9, 0.999, 1e-08, 0.01, 10

NN = (((1,), (0,)), ((), ()))
NT = (((1,), (1,)), ((), ()))
TN = (((0,), (0,)), ((), ()))


class Side:
    def __init__(self, arrays, out_shapes, n_remote, n_local, plan, aliased=False):
        self.arrays, self.out_shapes, self.plan, self.aliased = list(arrays), list(out_shapes), plan, aliased
        self.sems = [pltpu.SemaphoreType.DMA((n_remote,)), pltpu.SemaphoreType.DMA((n_remote,)),
                     pltpu.SemaphoreType.DMA((max(n_local, 1),))]

    def start(self, ins, outs, sems):
        local, sends, _ = self.plan(ins, outs, *sems)
        for make in local + sends:
            make().start()

    def wait(self, ins, outs, sems):
        local, sends, recvs = self.plan(ins, outs, *sems)
        for make in sends:
            make().wait_send()
        for make in recvs:
            make().wait_recv()
        for make in local:
            make().wait()


def _pcall(body, side=None, **kw):
    if side is None:
        return pl.pallas_call(body, **kw)
    grid = kw["grid"]
    single = not isinstance(kw["out_specs"], (list, tuple))
    out_specs = [kw["out_specs"]] if single else list(kw["out_specs"])
    out_shape = [kw["out_shape"]] if single else list(kw["out_shape"])
    scratch = list(kw.get("scratch_shapes", []))
    n_in, n_out, n_scr, n_side = len(kw["in_specs"]), len(out_specs), len(scratch), len(side.arrays)
    n_sout = len(side.out_shapes)

    def hosted(*refs):
        ins, s_in = refs[:n_in], refs[n_in:n_in + n_side]
        outs = refs[n_in + n_side:n_in + n_side + n_out]
        s_out = refs[n_in + n_side + n_out:n_in + n_side + n_out + n_sout]
        rest = refs[n_in + n_side + n_out + n_sout:]
        scr, sems = rest[:n_scr], rest[n_scr:]
        first = last = None
        for a, size in enumerate(grid):
            f, l = pl.program_id(a) == 0, pl.program_id(a) == size - 1
            first = f if first is None else jnp.logical_and(first, f)
            last = l if last is None else jnp.logical_and(last, l)

        @pl.when(first)
        def _():
            side.start(s_in, s_out, sems)

        body(*ins, *outs, *scr)

        @pl.when(last)
        def _():
            side.wait(s_in, s_out, sems)

    any_space = pl.BlockSpec(memory_space=pl.ANY)
    kw2 = dict(kw)
    kw2.update(in_specs=list(kw["in_specs"]) + [any_space] * n_side, out_specs=out_specs + [any_space] * n_sout,
               out_shape=out_shape + side.out_shapes, scratch_shapes=scratch + side.sems)
    if side.aliased:
        kw2["input_output_aliases"] = {n_in + i: n_out + i for i in range(n_side)}
    call = pl.pallas_call(hosted, **kw2)

    def run(*args):
        res = call(*args, *side.arrays)
        return (res[0] if single else list(res[:n_out])), list(res[n_out:])

    return run


def _params(**kw):
    return pltpu.CompilerParams(vmem_limit_bytes=VMEM_LIMIT_BYTES, **kw)


def _tile(dim, cap, unit=128):
    if dim <= cap:
        return dim
    t = (cap // unit) * unit
    while t >= unit:
        if dim % t == 0:
            return t
        t -= unit
    raise ValueError(f"no tile for {dim} under {cap}")


def _dot(a, b, dims):
    return lax.dot_general(a.astype(BF16), b.astype(BF16), dims, preferred_element_type=F32)


@functools.partial(jax.custom_vjp, nondiff_argnums=(2,))
def _dot_vjp(a, b, nt):
    return _dot(a, b, NT if nt else NN)


def _dot_vjp_fwd(a, b, nt):
    return _dot(a, b, NT if nt else NN), (a.astype(BF16), b.astype(BF16))


def _dot_vjp_bwd(nt, res, g):
    a, b = res
    if nt:
        return _dot(g, b, NN), _dot(g, a, TN)
    return _dot(g, b, NT), _dot(a, g, TN)


_dot_vjp.defvjp(_dot_vjp_fwd, _dot_vjp_bwd)


def _plain_dot(a, b, nt):
    return _dot(a, b, NT if nt else NN)


def _split_dot(x, mat, terms=2):
    out, rem = None, x
    for t in range(terms):
        part = rem.astype(BF16)
        d = lax.dot_general(part, mat, NN, preferred_element_type=F32)
        out = d if out is None else out + d
        if t + 1 < terms:
            rem = rem - part.astype(F32)
    return out


@functools.partial(jax.custom_vjp, nondiff_argnums=(3,))
def _split_dot_vjp(x, mat, mat_t, terms):
    return _split_dot(x, mat, terms)


def _split_dot_vjp_fwd(x, mat, mat_t, terms):
    return _split_dot(x, mat, terms), mat_t


def _split_dot_vjp_bwd(terms, mat_t, g):
    return _split_dot(g, mat_t, terms), None, None


_split_dot_vjp.defvjp(_split_dot_vjp_fwd, _split_dot_vjp_bwd)


def _plain_split(x, mat, mat_t, terms):
    return _split_dot(x, mat, terms)


def _tri(after):
    j = lax.broadcasted_iota(jnp.int32, (BLOCK, BLOCK), 0)
    s = lax.broadcasted_iota(jnp.int32, (BLOCK, BLOCK), 1)
    return jnp.where(j > s if after else j < s, 1.0, 0.0).astype(BF16)


def _in(a, block, imap):
    return (a, block, imap)


def _out(shape, dtype, block, imap, acc=False):
    return (shape, dtype, block, imap, acc)


def tcall(fn, grid, ins, outs, name, scratch=None, side=None):
    nin = len(ins)
    nout = len(outs)
    ngrid = len(grid)

    def body(*refs):
        ids = tuple(pl.program_id(a) for a in range(ngrid))
        extra = {} if scratch is None else {"scratch": refs[nin + nout]}
        res = fn(ids, *[r[...] for r in refs[:nin]], **extra)
        first = ids[0] == 0
        for a in range(1, ngrid):
            first = jnp.logical_and(first, ids[a] == 0)
        for o_ref, r, spec in zip(refs[nin:nin + nout], res, outs):
            if spec[4]:
                @pl.when(first)
                def _(o_ref=o_ref):
                    o_ref[...] = jnp.zeros(o_ref.shape, o_ref.dtype)
                o_ref[...] += r.astype(o_ref.dtype)
            else:
                o_ref[...] = r.astype(o_ref.dtype)

    return _pcall(
        body, side=side, name=name, grid=grid,
        in_specs=[pl.BlockSpec(b, m) for (_, b, m) in ins],
        out_specs=[pl.BlockSpec(b, m) for (_, _, b, m, _) in outs],
        out_shape=[jax.ShapeDtypeStruct(s, d) for (s, d, _, _, _) in outs],
        scratch_shapes=[] if scratch is None else [pltpu.VMEM(*scratch)],
        compiler_params=_params(),
    )(*[a for (a, _, _) in ins])


def _to_strided(scr, nat, d):
    if d == 1:
        return nat
    t, w = nat.shape
    nc = w // BLOCK
    for c in range(nc):
        scr[c * t:(c + 1) * t, :] = nat[:, c * BLOCK:(c + 1) * BLOCK]
    return jnp.concatenate([scr[pl.ds(c * t + r, t // d, stride=d), :] for r in range(d) for c in range(nc)], axis=1)


def _to_natural(scr, st, d):
    if d == 1:
        return st.astype(F32)
    t, w = st.shape[0] * d, st.shape[1] // d
    nc = w // BLOCK
    st = st.astype(F32)
    for r in range(d):
        for c in range(nc):
            scr[pl.ds(c * t + r, t // d, stride=d), :] = st[:, r * w + c * BLOCK:r * w + (c + 1) * BLOCK]
    return jnp.concatenate([scr[c * t:(c + 1) * t, :] for c in range(nc)], axis=1)


def _row(a, tm, width=None, cb=0):
    width = a.shape[1] if width is None else width
    return _in(a, (tm, width), lambda i, cb=cb: (i, cb))


def _full(a):
    zeros = (0,) * a.ndim
    return _in(a, a.shape, lambda *ids: zeros)


def _row_out(n, width, dtype, tm):
    return _out((n, width), dtype, (tm, width), lambda i: (i, 0))


def _acc_out(shape):
    zeros = (0,) * len(shape)
    return _out(shape, F32, shape, lambda *ids: zeros, acc=True)


def mm(a, b, mode, name, *, out_dtype=F32, scale=1.0, res=None, side=None):
    if mode == "nn":
        (m, k), (k2, n) = a.shape, b.shape
    elif mode == "nt":
        (m, k), (n, k2) = a.shape, b.shape
    else:
        (k, m), (k2, n) = a.shape, b.shape
    assert k == k2, (a.shape, b.shape, mode)
    tm, tn, tk = _tile(m, 512), _tile(n, 1408), _tile(k, 1408)
    nk = k // tk
    dims = {"nn": NN, "nt": NT, "tn": TN}[mode]
    has_res = res is not None

    def body(*refs):
        if has_res:
            a_ref, b_ref, r_ref, o_ref, acc_ref = refs
        else:
            a_ref, b_ref, o_ref, acc_ref = refs
        kk = pl.program_id(2)

        @pl.when(kk == 0)
        def _():
            acc_ref[...] = jnp.zeros(acc_ref.shape, F32)

        acc_ref[...] += _dot(a_ref[...], b_ref[...], dims)

        @pl.when(kk == nk - 1)
        def _():
            out = acc_ref[...]
            if scale != 1.0:
                out = out * scale
            if has_res:
                out = out + r_ref[...]
            o_ref[...] = out.astype(o_ref.dtype)

    a_spec = (pl.BlockSpec((tk, tm), lambda i, j, kk: (kk, i)) if mode == "tn"
              else pl.BlockSpec((tm, tk), lambda i, j, kk: (i, kk)))
    b_spec = (pl.BlockSpec((tn, tk), lambda i, j, kk: (j, kk)) if mode == "nt"
              else pl.BlockSpec((tk, tn), lambda i, j, kk: (kk, j)))
    in_specs = [a_spec, b_spec]
    args = [a, b]
    if has_res:
        in_specs.append(pl.BlockSpec((tm, tn), lambda i, j, kk: (i, j)))
        args.append(res)
    order = ("parallel", "parallel", "arbitrary") if side is None else ("arbitrary",) * 3
    return _pcall(
        body, side=side, name=name, grid=(m // tm, n // tn, nk),
        in_specs=in_specs,
        out_specs=pl.BlockSpec((tm, tn), lambda i, j, kk: (i, j)),
        out_shape=jax.ShapeDtypeStruct((m, n), out_dtype),
        scratch_shapes=[pltpu.VMEM((tm, tn), F32)],
        compiler_params=_params(dimension_semantics=order),
    )(*args)


def _rms(x, g):
    return x * lax.rsqrt(jnp.mean(x * x, axis=-1, keepdims=True) + RMS_EPS) * g


def _silu_mul(gate, up):
    return gate / (1.0 + jnp.exp(-gate)) * up


def mm_gate_up(h, w_gu, name, side=None):
    m, k = h.shape
    f = w_gu.shape[0] // 2
    tm, tn, tk = _tile(m, 512), _tile(f, 1408), _tile(k, 1408)
    nk, nj = k // tk, f // tn

    def body(h_ref, wg_ref, wu_ref, g_ref, u_ref, a_ref, accg_ref, accu_ref):
        kk = pl.program_id(2)

        @pl.when(kk == 0)
        def _():
            accg_ref[...] = jnp.zeros(accg_ref.shape, F32)
            accu_ref[...] = jnp.zeros(accu_ref.shape, F32)

        ht = h_ref[...]
        accg_ref[...] += _dot(ht, wg_ref[...], NT)
        accu_ref[...] += _dot(ht, wu_ref[...], NT)

        @pl.when(kk == nk - 1)
        def _():
            gate, up = accg_ref[...], accu_ref[...]
            g_ref[...] = gate
            u_ref[...] = up
            a_ref[...] = _silu_mul(gate, up).astype(a_ref.dtype)

    tile = pl.BlockSpec((tm, tn), lambda i, j, kk: (i, j))
    return _pcall(
        body, side=side, name=name, grid=(m // tm, nj, nk),
        in_specs=[pl.BlockSpec((tm, tk), lambda i, j, kk: (i, kk)),
                  pl.BlockSpec((tn, tk), lambda i, j, kk: (j, kk)),
                  pl.BlockSpec((tn, tk), lambda i, j, kk: (j + nj, kk))],
        out_specs=[tile, tile, tile],
        out_shape=[jax.ShapeDtypeStruct((m, f), F32), jax.ShapeDtypeStruct((m, f), F32),
                   jax.ShapeDtypeStruct((m, f), BF16)],
        scratch_shapes=[pltpu.VMEM((tm, tn), F32), pltpu.VMEM((tm, tn), F32)],
        compiler_params=_params(dimension_semantics=("arbitrary",) * 3),
    )(h, w_gu, w_gu)


def mm_down_act_bwd(dy, w_down, gate, up, name, side=None):
    m, d = dy.shape
    f = w_down.shape[0]
    tm, tn = _tile(m, 512), _tile(f, 1408)
    assert d <= 1408

    def body(dy_ref, w_ref, g_ref, u_ref, dg_ref, du_ref):
        da = _dot(dy_ref[...], w_ref[...], NT) * 0.5
        _, vjp = jax.vjp(_silu_mul, g_ref[...], u_ref[...])
        dg, du = vjp(da)
        dg_ref[...] = dg.astype(dg_ref.dtype)
        du_ref[...] = du.astype(du_ref.dtype)

    tile = pl.BlockSpec((tm, tn), lambda i, j: (i, j))
    return _pcall(
        body, side=side, name=name, grid=(m // tm, f // tn),
        in_specs=[pl.BlockSpec((tm, d), lambda i, j: (i, 0)), pl.BlockSpec((tn, d), lambda i, j: (j, 0)), tile, tile],
        out_specs=[tile, tile],
        out_shape=[jax.ShapeDtypeStruct((m, f), BF16), jax.ShapeDtypeStruct((m, f), BF16)],
        compiler_params=_params(dimension_semantics=("arbitrary", "arbitrary")),
    )(dy, w_down, gate, up)


def mm_norm_bwd(a, b, x, g, dres, name, b_kd=False):
    halves = isinstance(a, (tuple, list))
    a0, a1 = a if halves else (a, None)
    m, k = a0.shape[0], a0.shape[1] * (2 if halves else 1)
    d = b.shape[1] if b_kd else b.shape[0]
    dims = NN if b_kd else NT
    tm, tk = _tile(m, 512), _tile(a0.shape[1], 1408)
    nk = k // tk
    nkh = a0.shape[1] // tk
    has_res = dres is not None

    def body(*refs):
        a_ref, b_ref, x_ref, g_ref = refs[:4]
        rest = refs[4:-3]
        a1_ref = rest[0] if halves else None
        r_ref = rest[-1] if has_res else None
        dx_ref, dg_ref, acc_ref = refs[-3:]
        i, kk = pl.program_id(0), pl.program_id(1)

        @pl.when(kk == 0)
        def _():
            acc_ref[...] = jnp.zeros(acc_ref.shape, F32)

        if halves:
            @pl.when(kk < nkh)
            def _():
                acc_ref[...] += _dot(a_ref[...], b_ref[...], dims)

            @pl.when(kk >= nkh)
            def _():
                acc_ref[...] += _dot(a1_ref[...], b_ref[...], dims)
        else:
            acc_ref[...] += _dot(a_ref[...], b_ref[...], dims)

        @pl.when(kk == nk - 1)
        def _():
            _, vjp = jax.vjp(_rms, x_ref[...], g_ref[...])
            dx, dg = vjp(acc_ref[...])
            dx_ref[...] = dx + r_ref[...] if has_res else dx

            @pl.when(i == 0)
            def _():
                dg_ref[...] = jnp.zeros(dg_ref.shape, F32)

            dg_ref[...] += dg

    rows = pl.BlockSpec((tm, d), lambda i, kk: (i, 0))
    first = pl.BlockSpec((tm, tk), lambda i, kk: (i, jnp.minimum(kk, nkh - 1)))
    second = pl.BlockSpec((tm, tk), lambda i, kk: (i, jnp.maximum(kk - nkh, 0)))
    b_spec = pl.BlockSpec((tk, d), lambda i, kk: (kk, 0)) if b_kd else pl.BlockSpec((d, tk), lambda i, kk: (0, kk))
    in_specs = ([first, b_spec, rows, pl.BlockSpec(g.shape, lambda i, kk: (0, 0))]
                + ([second] if halves else []) + ([rows] if has_res else []))
    return _pcall(
        body, name=name, grid=(m // tm, nk),
        in_specs=in_specs,
        out_specs=[rows, pl.BlockSpec(g.shape, lambda i, kk: (0, 0))],
        out_shape=[jax.ShapeDtypeStruct((m, d), F32), jax.ShapeDtypeStruct(g.shape, F32)],
        scratch_shapes=[pltpu.VMEM((tm, d), F32)],
        compiler_params=_params(dimension_semantics=("arbitrary", "arbitrary")),
    )(*([a0, b, x, g] + ([a1] if halves else []) + ([dres] if has_res else [])))


def _indicator(shape, head_axis, mod):
    lane = lax.broadcasted_iota(jnp.int32, shape, head_axis)
    other = lax.broadcasted_iota(jnp.int32, shape, 1 - head_axis)
    lane = jnp.bitwise_and(lane, HEAD_DIM - 1) if mod else jnp.right_shift(lane, 6)
    return jnp.where(lane == other, 1.0, 0.0).astype(BF16)


def _head_rms(split, xs, g):
    w = xs.shape[1]
    to_head, from_head = _indicator((w, BLOCK), 0, False), _indicator((BLOCK, w), 1, False)
    to_lane, from_lane = _indicator((HEAD_DIM, w), 1, True), _indicator((w, HEAD_DIM), 0, True)
    ss = split(xs * xs, to_head, from_head, 3)
    r = lax.rsqrt(ss * (1.0 / HEAD_DIM) + RMS_EPS)
    g_all = split(jnp.broadcast_to(g, (8, HEAD_DIM)), to_lane, from_lane, 3)[0:1]
    return xs * split(r, from_head, to_head, 3) * g_all


def _prep(split, x, qg, kg, segs):
    parts = []
    for start, width, kind in segs:
        xs = x[:, start:start + width]
        parts.append(xs if kind == "raw" else _head_rms(split, xs, qg if kind == "q" else kg))
    return jnp.concatenate(parts, axis=1)


def prep_fwd(x, qg, kg, segs, dils, name):
    n, w = x.shape
    tm = _tile(n, 256, 8)

    def fn(ids, xt, a, b, scratch):
        ops = _prep(_plain_split, xt, a, b, segs)
        return tuple(_to_strided(scratch, ops, d) for d in dils)

    return tcall(fn, (n // tm,), [_row(x, tm), _full(qg), _full(kg)],
                 [_out((n // d, d * w), BF16, (tm // d, d * w), lambda i: (i, 0)) for d in dils], name,
                 scratch=((w // BLOCK * tm, BLOCK), F32))


def prep_bwd(x, qg, kg, segs, grads, gather, name):
    n, w = x.shape
    tm = BLOCK
    nblk = n // tm
    nslot = 1 + max(slot for _, _, _, slot in grads)

    def fn(ids, xt, a, b, *t, scratch):
        tiles, dils = [None] * nslot, [None] * nslot
        for ti, (_, sh, d, slot) in zip(t, grads):
            ti = jnp.where(ids[0] + sh < nblk, ti, 0.0) if sh else ti
            tiles[slot] = ti if tiles[slot] is None else tiles[slot] + ti
            dils[slot] = d
        tiles = [_to_natural(scratch, ti, d) for ti, d in zip(tiles, dils)]
        _, vjp = jax.vjp(lambda x_, a_, b_: _prep(_split_dot_vjp, x_, a_, b_, segs), xt, a, b)
        return vjp(gather(*tiles))

    specs = [_in(a, (tm // d, a.shape[1]), (lambda i, sh=sh: (jnp.minimum(i + sh, nblk - 1), 0)))
             for a, sh, d, _ in grads]
    wmax = max(a.shape[1] // d for a, _, d, _ in grads)
    return tcall(fn, (nblk,), [_row(x, tm), _full(qg), _full(kg)] + specs,
                 [_row_out(n, w, BF16, tm), _acc_out(qg.shape), _acc_out(kg.shape)], name,
                 scratch=((wmax // BLOCK * tm, BLOCK), F32))


def rmsnorm_fwd(x, g, name, side=None):
    n, d = x.shape
    tm = _tile(n, 512, 8)
    res = tcall(lambda ids, xt, gt: (_rms(xt, gt),), (n // tm,), [_row(x, tm), _full(g)],
                [_row_out(n, d, BF16, tm)], name, side=side)
    if side is None:
        return res[0]
    return res[0][0], res[1]


def ffn_fwd(x, g, w_gu, w_down, tag, carry=None):
    h = rmsnorm_fwd(x, g, tag + "_norm")
    if carry is None:
        gate, up, a = mm_gate_up(h, w_gu, tag + "_gu")
        return mm(a, w_down, "nn", tag + "_down", scale=0.5, res=x), (x, h, gate, up, a)
    phase, bufs = carry
    (gate, up, a), bufs = mm_gate_up(h, w_gu, tag + "_gu", side=gather_side(phase, bufs))
    y, bufs = mm(a, w_down, "nn", tag + "_down", scale=0.5, res=x, side=gather_side(phase + 1, bufs))
    return y, (x, h, gate, up, a), bufs


def ffn_bwd(dy, saved, g, w_gu, w_down, tag, chain=None):
    x, h, gate, up, a = saved

    def carrying(name, call, **kw):
        side = None if chain is None else chain.side(name)
        out = call(name=tag + "_" + name, side=side, **kw)
        if side is None:
            return out
        chain.done(name, out[1])
        return out[0]

    dgate, dup = carrying("da", mm_down_act_bwd, dy=dy, w_down=w_down, gate=gate, up=up)
    d_wdown = carrying("dwd", mm, a=a, b=dy, mode="tn", scale=0.5)
    d_wgu = (carrying("dwgu", mm, a=dgate, b=h, mode="tn"), mm(dup, h, "tn", tag + "_dwup"))
    dx, dg = mm_norm_bwd((dgate, dup), w_gu, x, g, dy, tag + "_dh", b_kd=True)
    return dx, dg, d_wgu, d_wdown


def _alibi(n_heads):
    return [float(s) for s in np.asarray(2.0 ** (-8.0 * np.arange(1, n_heads + 1) / n_heads), dtype=np.float32)]


def _banded_tile(dot, first, q, kp, kc, vp, vc, sinks, *, hkv, grp, max_dist, step, slopes, want_lse):
    row = lax.broadcasted_iota(jnp.int32, (BLOCK, 2 * BLOCK), 0)
    col = lax.broadcasted_iota(jnp.int32, (BLOCK, 2 * BLOCK), 1)
    dist = row + BLOCK - col
    valid = (dist >= 0) & (dist <= max_dist) & ((col >= BLOCK) | jnp.logical_not(first))
    distf = dist.astype(F32)

    def head(hd, qh, k2, v2):
        s = dot(qh, k2, True) * (HEAD_DIM ** -0.5)
        s = jnp.where(valid, s - (slopes[hd] * step) * distf, NEG_BIG)
        m = jnp.max(s, axis=-1, keepdims=True)
        if sinks is not None:
            pick = lax.broadcasted_iota(jnp.int32, sinks.shape, 1) == hd
            sk = jnp.sum(jnp.where(pick, sinks, 0.0), axis=1, keepdims=True)
            m = jnp.maximum(m, sk)
        m = lax.stop_gradient(m)
        p = jnp.exp(s - m)
        denom = jnp.sum(p, axis=-1, keepdims=True)
        if sinks is not None:
            denom = denom + jnp.exp(sk - m)
        return dot(p / denom, v2, False), m + jnp.log(denom)

    outs, lses = [], []
    if grp == 1:
        low = lax.broadcasted_iota(jnp.int32, (BLOCK, BLOCK), 1) < HEAD_DIM
        for pr in range(hkv // 2):
            sl = slice(pr * BLOCK, (pr + 1) * BLOCK)
            q2 = q[:, sl]
            k2 = jnp.concatenate([kp[:, sl], kc[:, sl]], axis=0)
            v2 = jnp.concatenate([vp[:, sl], vc[:, sl]], axis=0)
            o0, l0 = head(2 * pr, jnp.where(low, q2, 0.0), k2, v2)
            o1, l1 = head(2 * pr + 1, jnp.where(low, 0.0, q2), k2, v2)
            outs.append(jnp.where(low, o0, o1))
            lses.append(jnp.where(low, l0, l1))
    else:
        for hk in range(hkv):
            sl = slice(hk * HEAD_DIM, (hk + 1) * HEAD_DIM)
            k2 = jnp.concatenate([kp[:, sl], kc[:, sl]], axis=0)
            v2 = jnp.concatenate([vp[:, sl], vc[:, sl]], axis=0)
            for gi in range(grp):
                hd = hk * grp + gi
                o_h, l_h = head(hd, q[:, hd * HEAD_DIM:(hd + 1) * HEAD_DIM], k2, v2)
                outs.append(o_h)
                lses.append(jnp.broadcast_to(l_h, (BLOCK, HEAD_DIM)))
    o = jnp.concatenate(outs, axis=1)
    if want_lse:
        return o, jnp.concatenate(lses, axis=1)
    return (o,)


def _banded_specs(view, qcol, kcol, vcol, wq, wkv):
    def at(colfn, prev):
        if prev:
            return lambda r, n: (jnp.maximum(n - 1, 0), colfn(r))
        return lambda r, n: (n, colfn(r))
    return [
        _in(view, (BLOCK, wq), at(qcol, False)),
        _in(view, (BLOCK, wkv), at(kcol, True)),
        _in(view, (BLOCK, wkv), at(kcol, False)),
        _in(view, (BLOCK, wkv), at(vcol, True)),
        _in(view, (BLOCK, wkv), at(vcol, False)),
    ]


def banded_fwd(view, dil, cols, sinks, cfg, name):
    ns = view.shape[0]
    nb = ns // BLOCK
    wq, wkv = cfg["hkv"] * cfg["grp"] * HEAD_DIM, cfg["hkv"] * HEAD_DIM
    has_sinks = sinks is not None

    def fn(ids, q, kp, kc, vp, vc, *rest):
        q, kp, kc, vp, vc = [a.astype(F32) for a in (q, kp, kc, vp, vc)]
        return _banded_tile(_plain_dot, ids[1] == 0, q, kp, kc, vp, vc, rest[0] if has_sinks else None, **cfg)

    ins = _banded_specs(view, *cols, wq, wkv) + ([_full(sinks)] if has_sinks else [])
    outs = [_out((ns, dil * wq), F32 if cfg["want_lse"] else BF16, (BLOCK, wq), lambda r, n: (n, r))]
    if cfg["want_lse"]:
        outs.append(_out((ns, dil * wq), F32, (BLOCK, wq), lambda r, n: (n, r)))
    return tcall(fn, (dil, nb), ins, outs, name)


def banded_bwd(view, dil, cols, sinks, cfg, cts, name):
    ns = view.shape[0]
    nb = ns // BLOCK
    wq, wkv = cfg["hkv"] * cfg["grp"] * HEAD_DIM, cfg["hkv"] * HEAD_DIM
    has_sinks = sinks is not None
    assert len(cts) == (2 if cfg["want_lse"] else 1)

    def fn(ids, q, kp, kc, vp, vc, *rest):
        sk = rest[0] if has_sinks else None
        ct = rest[1 if has_sinks else 0:]
        first = ids[1] == 0

        def f(q, kp, kc, vp, vc, *s):
            return _banded_tile(_dot_vjp, first, q, kp, kc, vp, vc, s[0] if has_sinks else None, **cfg)

        prim = tuple(a.astype(F32) for a in (q, kp, kc, vp, vc)) + ((sk,) if has_sinks else ())
        _, vjp = jax.vjp(f, *prim)
        return vjp(tuple(c.astype(F32) for c in ct))

    ins = (_banded_specs(view, *cols, wq, wkv) + ([_full(sinks)] if has_sinks else [])
           + [_in(a, (BLOCK, wq), (lambda r, n, cf=cf: (n, cf(r)))) for (a, cf) in cts])
    blk = lambda w: _out((ns, dil * w), F32, (BLOCK, w), lambda r, n: (n, r))
    outs = [blk(wq), blk(wkv), blk(wkv), blk(wkv), blk(wkv)]
    if has_sinks:
        outs.append(_acc_out(sinks.shape))
    return tcall(fn, (dil, nb), ins, outs, name)


def _log_sigmoid(z):
    return jnp.minimum(z, 0.0) - jnp.log(1.0 + jnp.exp(-jnp.abs(z)))


SB_PAIRS = 4


def _sb_pair(dot, suffix, qh, kb, vb, r_in, mask):
    z = dot(qh, kb, True) * (HEAD_DIM ** -0.5)
    lsp = _log_sigmoid(z)
    log_keep = jnp.where(mask, lsp - z, 0.0)
    log_after = suffix(log_keep) + r_in
    a = jnp.where(mask, jnp.exp(lsp + log_after), 0.0)
    return dot(a, vb, False), r_in + jnp.sum(log_keep, axis=1, keepdims=True)


def sb_fwd(qkv, qcb, kcb, vcb, name, side=None):
    s = qkv.shape[0]
    nb = s // BLOCK
    pairs = B_HEADS // 2
    wide = SB_PAIRS * BLOCK
    assert pairs % SB_PAIRS == 0 and qcb % SB_PAIRS == 0 and kcb % SB_PAIRS == 0 and vcb % SB_PAIRS == 0

    def body(q_ref, k_ref, v_ref, o_ref):
        n = pl.program_id(1)
        low = lax.broadcasted_iota(jnp.int32, (BLOCK, BLOCK), 1) < HEAD_DIM
        before = (lax.broadcasted_iota(jnp.int32, (2 * BLOCK, BLOCK), 1)
                  < jnp.bitwise_and(lax.broadcasted_iota(jnp.int32, (2 * BLOCK, BLOCK), 0), BLOCK - 1))
        after = _tri(True)
        suffix = lambda t: _split_dot(t, after)
        qs = []
        for p in range(SB_PAIRS):
            q2 = q_ref[:, p * BLOCK:(p + 1) * BLOCK].astype(F32)
            qs.append(jnp.concatenate([jnp.where(low, q2, 0.0), jnp.where(low, 0.0, q2)], axis=0))

        def cond(c):
            return jnp.logical_and(c[0] >= 0, c[1] > SB_SKIP_LOG)

        def step(c):
            kb, _, rs, accs = c
            rows = pl.ds(pl.multiple_of(kb * BLOCK, BLOCK), BLOCK)
            mask = jnp.logical_or(before, kb != n)
            new_r, new_acc, top = [], [], None
            for p in range(SB_PAIRS):
                cols = slice(p * BLOCK, (p + 1) * BLOCK)
                o_part, r_out = _sb_pair(_plain_dot, suffix, qs[p], k_ref[rows, cols], v_ref[rows, cols], rs[p], mask)
                new_r.append(r_out)
                new_acc.append(accs[p] + o_part)
                top = jnp.max(r_out) if top is None else jnp.maximum(top, jnp.max(r_out))
            return kb - 1, top, tuple(new_r), tuple(new_acc)

        init = (n, jnp.float32(0.0), tuple(jnp.zeros((2 * BLOCK, 1), F32) for _ in range(SB_PAIRS)),
                tuple(jnp.zeros((2 * BLOCK, BLOCK), F32) for _ in range(SB_PAIRS)))
        accs = lax.while_loop(cond, step, init)[3]
        for p in range(SB_PAIRS):
            o_ref[:, p * BLOCK:(p + 1) * BLOCK] = jnp.where(low, accs[p][:BLOCK], accs[p][BLOCK:]).astype(o_ref.dtype)

    return _pcall(
        body, side=side, name=name, grid=(pairs // SB_PAIRS, nb),
        in_specs=[pl.BlockSpec((BLOCK, wide), lambda g, n: (n, qcb // SB_PAIRS + g)),
                  pl.BlockSpec((s, wide), lambda g, n: (0, kcb // SB_PAIRS + g), pipeline_mode=pl.Buffered(1)),
                  pl.BlockSpec((s, wide), lambda g, n: (0, vcb // SB_PAIRS + g), pipeline_mode=pl.Buffered(1))],
        out_specs=pl.BlockSpec((BLOCK, wide), lambda g, n: (n, g)),
        out_shape=jax.ShapeDtypeStruct((s, pairs * BLOCK), BF16),
        compiler_params=_params(),
    )(qkv, qkv, qkv)


def sb_bwd(qkv, qcb, kcb, vcb, do, docb, name, side=None):
    s = qkv.shape[0]
    nb = s // BLOCK
    pairs = B_HEADS // 2
    wide = SB_PAIRS * BLOCK
    assert docb % SB_PAIRS == 0

    def body(q_ref, k_ref, v_ref, do_ref, dq_ref, dk_ref, dv_ref, r_ref):
        n = pl.program_id(1)

        @pl.when(n == 0)
        def _():
            dk_ref[...] = jnp.zeros(dk_ref.shape, F32)
            dv_ref[...] = jnp.zeros(dv_ref.shape, F32)

        low = lax.broadcasted_iota(jnp.int32, (BLOCK, BLOCK), 1) < HEAD_DIM
        before = (lax.broadcasted_iota(jnp.int32, (2 * BLOCK, BLOCK), 1)
                  < jnp.bitwise_and(lax.broadcasted_iota(jnp.int32, (2 * BLOCK, BLOCK), 0), BLOCK - 1))
        after, earlier = _tri(True), _tri(False)
        suffix = lambda t: _split_dot_vjp(t, after, earlier, 2)
        stack = lambda t: jnp.concatenate([jnp.where(low, t, 0.0), jnp.where(low, 0.0, t)], axis=0)
        qs = [stack(q_ref[:, p * BLOCK:(p + 1) * BLOCK].astype(F32)) for p in range(SB_PAIRS)]
        dos = [stack(do_ref[:, p * BLOCK:(p + 1) * BLOCK].astype(F32)) for p in range(SB_PAIRS)]

        def cond(c):
            return jnp.logical_and(c[0] >= 0, c[1] > SB_SKIP_LOG)

        def down(c):
            kb, _, rs = c
            rows = pl.ds(pl.multiple_of(kb * BLOCK, BLOCK), BLOCK)
            mask = jnp.logical_or(before, kb != n)
            new_r, top = [], None
            for h in range(SB_PAIRS):
                cols = slice(h * BLOCK, (h + 1) * BLOCK)
                r_ref[h, kb] = rs[h]
                z = _dot(qs[h], k_ref[rows, cols], NT) * (HEAD_DIM ** -0.5)
                log_keep = jnp.where(mask, _log_sigmoid(z) - z, 0.0)
                r_out = rs[h] + jnp.sum(log_keep, axis=1, keepdims=True)
                new_r.append(r_out)
                top = jnp.max(r_out) if top is None else jnp.maximum(top, jnp.max(r_out))
            return kb - 1, top, tuple(new_r)

        init = (n, jnp.float32(0.0), tuple(jnp.zeros((2 * BLOCK, 1), F32) for _ in range(SB_PAIRS)))
        last = lax.while_loop(cond, down, init)[0] + 1

        def up(kb, c):
            dqs, g_rs = c
            rows = pl.ds(pl.multiple_of(kb * BLOCK, BLOCK), BLOCK)
            mask = jnp.logical_or(before, kb != n)
            new_dq, new_g = [], []
            for h in range(SB_PAIRS):
                cols = slice(h * BLOCK, (h + 1) * BLOCK)
                _, vjp = jax.vjp(lambda q_, k_, v_, r_: _sb_pair(_dot_vjp, suffix, q_, k_, v_, r_, mask),
                                 qs[h], k_ref[rows, cols].astype(F32), v_ref[rows, cols].astype(F32), r_ref[h, kb])
                dq_c, dk_c, dv_c, g_in = vjp((dos[h], g_rs[h]))
                dk_ref[rows, cols] += dk_c
                dv_ref[rows, cols] += dv_c
                new_dq.append(dqs[h] + dq_c)
                new_g.append(g_in)
            return tuple(new_dq), tuple(new_g)

        init = (tuple(jnp.zeros((2 * BLOCK, BLOCK), F32) for _ in range(SB_PAIRS)),
                tuple(jnp.zeros((2 * BLOCK, 1), F32) for _ in range(SB_PAIRS)))
        dqs = lax.fori_loop(last, n + 1, up, init)[0]
        for p in range(SB_PAIRS):
            dq_ref[:, p * BLOCK:(p + 1) * BLOCK] = jnp.where(low, dqs[p][:BLOCK], dqs[p][BLOCK:])

    full = jax.ShapeDtypeStruct((s, pairs * BLOCK), F32)
    return _pcall(
        body, side=side, name=name, grid=(pairs // SB_PAIRS, nb),
        in_specs=[pl.BlockSpec((BLOCK, wide), lambda g, n: (n, qcb // SB_PAIRS + g)),
                  pl.BlockSpec((s, wide), lambda g, n: (0, kcb // SB_PAIRS + g), pipeline_mode=pl.Buffered(1)),
                  pl.BlockSpec((s, wide), lambda g, n: (0, vcb // SB_PAIRS + g), pipeline_mode=pl.Buffered(1)),
                  pl.BlockSpec((BLOCK, wide), lambda g, n: (n, docb // SB_PAIRS + g))],
        out_specs=[pl.BlockSpec((BLOCK, wide), lambda g, n: (n, g)),
                   pl.BlockSpec((s, wide), lambda g, n: (0, g), pipeline_mode=pl.Buffered(1)),
                   pl.BlockSpec((s, wide), lambda g, n: (0, g), pipeline_mode=pl.Buffered(1))],
        out_shape=[full, full, full],
        scratch_shapes=[pltpu.VMEM((SB_PAIRS, nb, 2 * BLOCK, 1), F32)],
        compiler_params=_params(),
    )(qkv, qkv, qkv, do)


def _xa_tile(dot, q, kv, qg, kg):
    hd = q.shape[1] // X_HEADS
    outs = []
    for h in range(X_HEADS):
        qh = _rms(q[:, h * hd:(h + 1) * hd], qg)
        kh = _rms(kv[:, h * hd:(h + 1) * hd], kg)
        vh = kv[:, (X_HEADS + h) * hd:(X_HEADS + h + 1) * hd]
        sc = dot(qh, kh, True) * (hd ** -0.5)
        m = lax.stop_gradient(jnp.max(sc, axis=-1, keepdims=True))
        p = jnp.exp(sc - m)
        outs.append(dot(p / jnp.sum(p, axis=-1, keepdims=True), vh, False))
    return jnp.concatenate(outs, axis=1)


def xa_core_fwd(q, kv, qg, kg, name):
    n, d = q.shape
    tm = _tile(n, 256, 8)
    (o,) = tcall(lambda ids, qt, kvt, qgt, kgt: (_xa_tile(_plain_dot, qt, kvt, qgt, kgt),), (n // tm,),
                 [_row(q, tm), _full(kv), _full(qg), _full(kg)], [_row_out(n, d, BF16, tm)], name)
    return o


def xa_core_bwd(q, kv, qg, kg, do, name):
    n, d = q.shape
    tm = _tile(n, 256, 8)

    def fn(ids, qt, kvt, qgt, kgt, dot_):
        _, vjp = jax.vjp(functools.partial(_xa_tile, _dot_vjp), qt, kvt, qgt, kgt)
        return vjp(dot_.astype(F32))

    return tcall(fn, (n // tm,), [_row(q, tm), _full(kv), _full(qg), _full(kg), _row(do, tm)],
                 [_row_out(n, d, BF16, tm), _acc_out(kv.shape), _acc_out(qg.shape), _acc_out(kg.shape)], name)


def _ev_reorder(a):
    return jnp.concatenate([a[0:512], a[768:2304], a[512:768]], axis=0)


def _ev_restore(a):
    return jnp.concatenate([a[0:512], a[2048:2304], a[512:2048]], axis=0)


_EV_SEGS = ((0, 512, "q"), (512, 1536, "raw"), (2048, 128, "k"), (2176, 128, "raw"))
_A_CFG = dict(hkv=A_KV_HEADS, grp=A_Q_HEADS // A_KV_HEADS, max_dist=BLOCK - 1, step=1.0, slopes=_alibi(A_Q_HEADS),
              want_lse=False)
_A_COLS = (lambda r: 0, lambda r: 16, lambda r: 17)


def even_mixer_fwd(x, h, w_in, qg, kg, sinks, w_out, tag, side=None):
    qkv = mm(h, w_in, "nt", tag + "_in")
    (ops,) = prep_fwd(qkv, qg, kg, _EV_SEGS, (1,), tag + "_prep")
    (o_a,) = banded_fwd(ops, 1, _A_COLS, sinks, _A_CFG, tag + "_swa")
    o_b = sb_fwd(ops, 4, 8, 12, tag + "_sb", side=side)
    carried = None
    if side is not None:
        o_b, carried = o_b
    o = jnp.concatenate([o_a, o_b], axis=1)
    y = mm(o, w_out, "nn", tag + "_out", res=x)
    return y, (x, h, qkv, ops, o), carried


def even_mixer_bwd(dy, saved, g, w_in, qg, kg, sinks, w_out, tag, side=None):
    x, h, qkv, ops, o = saved
    do = mm(dy, w_out, "nt", tag + "_do")
    d_wout = mm(o, dy, "tn", tag + "_dwout")
    dqa, dkp, dkc, dvp, dvc, dsinks = banded_bwd(ops, 1, _A_COLS, sinks, _A_CFG, [(do, lambda r: 0)], tag + "_dswa")
    res = sb_bwd(ops, 4, 8, 12, do, 4, tag + "_dsb", side=side)
    carried = None
    if side is not None:
        res, carried = res
    dqb, dkb, dvb = res
    dqkv, dqg, dkg = prep_bwd(
        qkv, qg, kg, _EV_SEGS,
        [(dqa, 0, 1, 0), (dqb, 0, 1, 1), (dkb, 0, 1, 2), (dvb, 0, 1, 3), (dkc, 0, 1, 4), (dkp, 1, 1, 4), (dvc, 0, 1, 5),
         (dvp, 1, 1, 5)],
        lambda *t: jnp.concatenate(t, axis=1), tag + "_dqkv")
    d_win = mm(dqkv, h, "tn", tag + "_dwin")
    dx, dg = mm_norm_bwd(dqkv, w_in, x, g, dy, tag + "_dh", b_kd=True)
    return dx, dg, d_win, dqg, dkg, dsinks, d_wout, carried


def _c_cfg(window, dil):
    return dict(hkv=C_HEADS, grp=1, max_dist=window // dil, step=float(dil), slopes=_alibi(C_HEADS), want_lse=True)


_C_COLS = (lambda r: 3 * r, lambda r: 3 * r + 1, lambda r: 3 * r + 2)
_OD_SEGS = ((0, 1024, "q"), (1024, 1024, "k"), (2048, 1024, "raw"))


def _combine(o1, o2, o3, l1, l2, l3):
    m = lax.stop_gradient(jnp.maximum(jnp.maximum(l1, l2), l3))
    e1, e2, e3 = jnp.exp(l1 - m), jnp.exp(l2 - m), jnp.exp(l3 - m)
    tot = e1 + e2 + e3
    return (e1 / tot) * o1 + (e2 / tot) * o2 + (e3 / tot) * o3


def odd_mixer_fwd(x, g, w_in, qg, kg, w_out, tag):
    n, d = x.shape
    h = rmsnorm_fwd(x, g, tag + "_norm")
    qkv = mm(h, w_in, "nt", tag + "_in")
    dils = [dil for _, dil in C_PATTERNS]
    ops = prep_fwd(qkv, qg, kg, _OD_SEGS, dils, tag + "_prep")
    os_, ls_ = [], []
    for (window, dil), ops_d in zip(C_PATTERNS, ops):
        o_p, l_p = banded_fwd(ops_d, dil, _C_COLS, None, _c_cfg(window, dil), f"{tag}_dil{dil}")
        os_.append(o_p)
        ls_.append(l_p)
    tm = BLOCK
    lay = lambda a, dil: _in(a, (tm // dil, a.shape[1]), lambda i: (i, 0))
    views = [lay(a, dil) for a, dil in zip(os_ + ls_, dils + dils)]

    def comb(ids, *t, scratch):
        return (_combine(*[_to_natural(scratch, a, dil) for a, dil in zip(t, dils + dils)]),)

    (o,) = tcall(comb, (n // tm,), views, [_row_out(n, d, BF16, tm)], tag + "_comb",
                 scratch=((d // BLOCK * tm, BLOCK), F32))
    y = mm(o, w_out, "nn", tag + "_out", res=x)
    return y, (x, h, qkv, ops, views, o)


def odd_mixer_bwd(dy, saved, g, w_in, qg, kg, w_out, tag):
    x, h, qkv, ops, views, o = saved
    n, d = x.shape
    do = mm(dy, w_out, "nt", tag + "_do")
    d_wout = mm(o, dy, "tn", tag + "_dwout")
    tm = BLOCK
    dils = [dil for _, dil in C_PATTERNS]

    def comb_bwd(ids, *t, scratch):
        _, vjp = jax.vjp(_combine, *[_to_natural(scratch, a, dil) for a, dil in zip(t[:6], dils + dils)])
        return tuple(_to_strided(scratch, c, dil) for c, dil in zip(vjp(t[6]), dils + dils))

    cts = tcall(comb_bwd, (n // tm,), views + [_row(do, tm)],
                [_out((n // dil, dil * d), F32, (tm // dil, dil * d), lambda i: (i, 0)) for dil in dils + dils],
                tag + "_dcomb", scratch=((d // BLOCK * tm, BLOCK), F32))
    dqs, dks, dvs = [], [], []
    for p, ((window, dil), ops_d) in enumerate(zip(C_PATTERNS, ops)):
        dq, dkp, dkc, dvp, dvc = banded_bwd(ops_d, dil, _C_COLS, None, _c_cfg(window, dil),
                                            [(cts[p], lambda r: r), (cts[3 + p], lambda r: r)], f"{tag}_ddil{dil}")
        dqs.append((dq, 0, dil, p))
        dks += [(dkc, 0, dil, 3 + p), (dkp, dil, dil, 3 + p)]
        dvs += [(dvc, 0, dil, 6 + p), (dvp, dil, dil, 6 + p)]

    def gather(*t):
        return jnp.concatenate([t[0] + t[1] + t[2], t[3] + t[4] + t[5], t[6] + t[7] + t[8]], axis=1)

    dqkv, dqg, dkg = prep_bwd(qkv, qg, kg, _OD_SEGS, dqs + dks + dvs, gather, tag + "_dqkv")
    d_win = mm(dqkv, h, "tn", tag + "_dwin")
    dx, dg = mm_norm_bwd(dqkv, w_in, x, g, dy, tag + "_dh", b_kd=True)
    return dx, dg, d_win, dqg, dkg, d_wout


def xa_fwd(x, mem, g, gm, w_q, w_kv, qg, kg, w_o, tag):
    h = rmsnorm_fwd(x, g, tag + "_norm")
    q = mm(h, w_q, "nn", tag + "_q")
    mn = rmsnorm_fwd(mem, gm, tag + "_mnorm")
    kv = mm(mn, w_kv, "nt", tag + "_kv")
    o = xa_core_fwd(q, kv, qg, kg, tag + "_core")
    y = mm(o, w_o, "nn", tag + "_o", res=x)
    return y, (x, h, q, mn, kv, o)


def xa_bwd(dy, saved, mem, g, gm, w_q, w_kv, qg, kg, w_o, tag):
    x, h, q, mn, kv, o = saved
    do = mm(dy, w_o, "nt", tag + "_do", out_dtype=BF16)
    d_wo = mm(o, dy, "tn", tag + "_dwo")
    dq, dkv, dqg, dkg = xa_core_bwd(q, kv, qg, kg, do, tag + "_dcore")
    d_wq = mm(h, dq, "tn", tag + "_dwq")
    dx, dg = mm_norm_bwd(dq, w_q, x, g, dy, tag + "_dh")
    d_wkv = mm(dkv, mn, "tn", tag + "_dwkv")
    _, dgm = mm_norm_bwd(dkv, w_kv, mem, gm, None, tag + "_dmn", b_kd=True)
    return dx, dg, dgm, d_wq, d_wkv, dqg, dkg, d_wo


def loss_head(y, target, name):
    n, d = y.shape
    tm = _tile(n, 512, 8)

    def fn(ids, yt, tt):
        e = yt - tt
        return e * (1.0 / d), jnp.sum(e * e, axis=0, keepdims=True)

    return tcall(fn, (n // tm,), [_row(y, tm), _row(target, tm)], [_row_out(n, d, F32, tm), _acc_out((1, d))], name)


_ANY = pl.BlockSpec(memory_space=pl.ANY)


def all_gather_blocks(blocks):
    nb = len(blocks)

    def body(*refs):
        x_refs, out_refs = refs[:nb], refs[nb:2 * nb]
        send_sems, recv_sems, local_sems = refs[2 * nb:]
        x, y, c = lax.axis_index("x"), lax.axis_index("y"), lax.axis_index("c")
        me, sibling = (x, y, c), (x, y, 1 - c)
        over_x, over_y, diagonal = (1 - x, y), (x, 1 - y), (1 - x, 1 - y)
        relay_of = ((1 - x) * (1 - c) + x * c, y * (1 - c) + (1 - y) * c)
        relay_to = (x * (1 - c) + (1 - x) * c, (1 - y) * (1 - c) + y * c)

        def copy(b, k, blk, to, own=False):
            px, py, pc = blk
            slot = out_refs[b].at[4 * px + 2 * py + pc]
            return pltpu.make_async_remote_copy(
                src_ref=x_refs[b] if own else slot, dst_ref=slot,
                send_sem=send_sems.at[7 * b + k], recv_sem=recv_sems.at[7 * b + k], device_id=to, device_id_type=MESH)

        mine = [pltpu.make_async_copy(x_refs[b], out_refs[b].at[4 * x + 2 * y + c], local_sems.at[b]) for b in range(nb)]
        for cp in mine:
            cp.start()
        sent = []
        for b in range(nb):
            sent += [copy(b, 0, me, sibling, own=True), copy(b, 1, me, (*over_x, c), own=True),
                     copy(b, 2, me, (*over_y, c), own=True)]
        for cp in sent:
            cp.start()
        for b in range(nb):
            copy(b, 1, (*over_x, c), me).wait_recv()
            copy(b, 2, (*over_y, c), me).wait_recv()
            later = [copy(b, 3, (*relay_of, c), (*relay_to, c)), copy(b, 4, (*over_x, c), sibling),
                     copy(b, 5, (*over_y, c), sibling)]
            for cp in later:
                cp.start()
            sent += later
        for b in range(nb):
            copy(b, 3, (*diagonal, c), me).wait_recv()
            fwd = copy(b, 6, (*diagonal, c), sibling)
            fwd.start()
            sent.append(fwd)
        for b in range(nb):
            copy(b, 0, sibling, me).wait_recv()
            for k, chip in ((4, over_x), (5, over_y), (6, diagonal)):
                copy(b, k, (*chip, 1 - c), me).wait_recv()
        for cp in sent:
            cp.wait_send()
        for cp in mine:
            cp.wait()

    return _pcall(
        body, name="weights_all_gather",
        in_specs=[_ANY] * nb, out_specs=[_ANY] * nb,
        out_shape=[jax.ShapeDtypeStruct((N_DEV,) + a.shape, a.dtype) for a in blocks],
        scratch_shapes=[pltpu.SemaphoreType.DMA((7 * nb,)), pltpu.SemaphoreType.DMA((7 * nb,)),
                        pltpu.SemaphoreType.DMA((nb,))],
    )(*blocks)


def pair_exchange(bufs):
    nb = len(bufs)

    def body(*refs):
        srcs, dsts = refs[:nb], refs[nb:2 * nb]
        send_sems, recv_sems = refs[2 * nb:]
        x, y, c = lax.axis_index("x"), lax.axis_index("y"), lax.axis_index("c")
        copies = []
        for b in range(nb):
            for j in range(4):
                cp = pltpu.make_async_remote_copy(
                    src_ref=srcs[b].at[2 * j + (1 - c)], dst_ref=dsts[b].at[j], send_sem=send_sems.at[4 * b + j],
                    recv_sem=recv_sems.at[4 * b + j], device_id=(x, y, 1 - c), device_id_type=MESH)
                cp.start()
                copies.append(cp)
        for cp in copies:
            cp.wait()

    return _pcall(
        body, name="grads_pair_exchange",
        in_specs=[_ANY] * nb, out_specs=[_ANY] * nb,
        out_shape=[jax.ShapeDtypeStruct((4,) + a.shape[1:], a.dtype) for a in bufs],
        scratch_shapes=[pltpu.SemaphoreType.DMA((4 * nb,)), pltpu.SemaphoreType.DMA((4 * nb,))],
    )(*bufs)


def pair_sum(g, got, c, out_dtype, name):
    r, w = g.shape[1:]
    tr = _tile(r, 512, 16)

    def body(c_ref, a_ref, b_ref, o_ref):
        o_ref[...] = (a_ref[...].astype(F32) + b_ref[...].astype(F32)).astype(o_ref.dtype)

    return _pcall(
        body, name=name,
        grid_spec=pltpu.PrefetchScalarGridSpec(
            num_scalar_prefetch=1, grid=(4, r // tr),
            in_specs=[pl.BlockSpec((None, tr, w), lambda j, i, c_ref: (2 * j + c_ref[0], i, 0)),
                      pl.BlockSpec((None, tr, w), lambda j, i, c_ref: (j, i, 0))],
            out_specs=pl.BlockSpec((None, tr, w), lambda j, i, c_ref: (j, i, 0))),
        out_shape=jax.ShapeDtypeStruct((4,) + g.shape[1:], out_dtype),
        compiler_params=_params(),
    )(c, g, got)


def chip_exchange(parts):
    nb = len(parts)

    def body(*refs):
        srcs, dsts = refs[:nb], refs[nb:2 * nb]
        send_sems, recv_sems, local_sems = refs[2 * nb:]
        x, y, c = lax.axis_index("x"), lax.axis_index("y"), lax.axis_index("c")
        my_chip = 2 * x + y
        copies = []
        for b in range(nb):
            mine = pltpu.make_async_copy(srcs[b].at[my_chip], dsts[b].at[my_chip], local_sems.at[b])
            mine.start()
            copies.append(mine)
            for k, (tx, ty) in enumerate([(1 - x, y), (x, 1 - y), (1 - x, 1 - y)]):
                cp = pltpu.make_async_remote_copy(
                    src_ref=srcs[b].at[2 * tx + ty], dst_ref=dsts[b].at[my_chip], send_sem=send_sems.at[3 * b + k],
                    recv_sem=recv_sems.at[3 * b + k], device_id=(tx, ty, c), device_id_type=MESH)
                cp.start()
                copies.append(cp)
        for cp in copies:
            cp.wait()

    return _pcall(
        body, name="grads_chip_exchange",
        in_specs=[_ANY] * nb, out_specs=[_ANY] * nb,
        out_shape=[jax.ShapeDtypeStruct(a.shape, a.dtype) for a in parts],
        scratch_shapes=[pltpu.SemaphoreType.DMA((3 * nb,)), pltpu.SemaphoreType.DMA((3 * nb,)),
                        pltpu.SemaphoreType.DMA((nb,))],
    )(*parts)


def chip_sum(parts, name):
    r, w = parts.shape[1:]
    tr = _tile(r, 512, 16)
    spec = lambda j: _in(parts, (None, tr, w), lambda i, j=j: (j, i, 0))

    def fn(ids, a, b, c_, d):
        a, b, c_, d = [t.astype(F32) for t in (a, b, c_, d)]
        return (((a + b) + c_) + d,)

    (out,) = tcall(fn, (r // tr,), [spec(j) for j in range(4)],
                   [_out((r, w), F32, (tr, w), lambda i: (i, 0))], name)
    return out


def _remote(src, dst, send_sems, recv_sems, k, to):
    return functools.partial(pltpu.make_async_remote_copy, src_ref=src, dst_ref=dst, send_sem=send_sems.at[k],
                             recv_sem=recv_sems.at[k], device_id=to, device_id_type=MESH)


def _gather_plan(phase, nb):
    def plan(ins, outs, send_sems, recv_sems, local_sems):
        x, y, c = lax.axis_index("x"), lax.axis_index("y"), lax.axis_index("c")
        me, sibling = (x, y, c), (x, y, 1 - c)
        over_x, over_y, diagonal = (1 - x, y), (x, 1 - y), (1 - x, 1 - y)
        relay_of = ((1 - x) * (1 - c) + x * c, y * (1 - c) + (1 - y) * c)
        relay_to = (x * (1 - c) + (1 - x) * c, (1 - y) * (1 - c) + y * c)
        local, sends, recvs = [], [], []
        for b in range(nb):
            slot = lambda chip, core, b=b: outs[b].at[4 * chip[0] + 2 * chip[1] + core]
            if phase == 0:
                local.append(functools.partial(pltpu.make_async_copy, ins[b], slot((x, y), c), local_sems.at[b]))
                moves = [(ins[b], slot((x, y), c), to) for to in (sibling, (*over_x, c), (*over_y, c))]
                arrive = [slot((x, y), 1 - c), slot(over_x, c), slot(over_y, c)]
            elif phase == 1:
                moves = [(slot(relay_of, c), slot(relay_of, c), (*relay_to, c)),
                         (slot(over_x, c), slot(over_x, c), sibling), (slot(over_y, c), slot(over_y, c), sibling)]
                arrive = [slot(diagonal, c), slot(over_x, 1 - c), slot(over_y, 1 - c)]
            else:
                moves = [(slot(diagonal, c), slot(diagonal, c), sibling)]
                arrive = [slot(diagonal, 1 - c)]
            sends += [_remote(src, dst, send_sems, recv_sems, 3 * b + k, to) for k, (src, dst, to) in enumerate(moves)]
            recvs += [_remote(dst, dst, send_sems, recv_sems, 3 * b + k, me) for k, dst in enumerate(arrive)]
        return local, sends, recvs
    return plan


def gather_side(phase, arrays):
    nb = len(arrays)
    if phase == 0:
        shapes = [jax.ShapeDtypeStruct((N_DEV,) + a.shape, a.dtype) for a in arrays]
        return Side(arrays, shapes, 3 * nb, nb, _gather_plan(0, nb))
    shapes = [jax.ShapeDtypeStruct(a.shape, a.dtype) for a in arrays]
    return Side(arrays, shapes, 3 * nb, 0, _gather_plan(phase, nb), aliased=True)


def pair_side(bufs):
    nb = len(bufs)

    def plan(ins, outs, send_sems, recv_sems, local_sems):
        x, y, c = lax.axis_index("x"), lax.axis_index("y"), lax.axis_index("c")
        sends = [_remote(ins[b].at[2 * j + (1 - c)], outs[b].at[j], send_sems, recv_sems, 4 * b + j, (x, y, 1 - c))
                 for b in range(nb) for j in range(4)]
        recvs = [_remote(outs[b].at[j], outs[b].at[j], send_sems, recv_sems, 4 * b + j, (x, y, c))
                 for b in range(nb) for j in range(4)]
        return [], sends, recvs

    shapes = [jax.ShapeDtypeStruct((4,) + a.shape[1:], a.dtype) for a in bufs]
    return Side(bufs, shapes, 4 * nb, 0, plan)


def chip_side(parts):
    nb = len(parts)

    def plan(ins, outs, send_sems, recv_sems, local_sems):
        x, y, c = lax.axis_index("x"), lax.axis_index("y"), lax.axis_index("c")
        my_chip = 2 * x + y
        peers = [(1 - x, y), (x, 1 - y), (1 - x, 1 - y)]
        local = [functools.partial(pltpu.make_async_copy, ins[b].at[my_chip], outs[b].at[my_chip], local_sems.at[b])
                 for b in range(nb)]
        sends = [_remote(ins[b].at[2 * tx + ty], outs[b].at[my_chip], send_sems, recv_sems, 3 * b + k, (tx, ty, c))
                 for b in range(nb) for k, (tx, ty) in enumerate(peers)]
        recvs = [_remote(outs[b].at[2 * tx + ty], outs[b].at[2 * tx + ty], send_sems, recv_sems, 3 * b + k, (x, y, c))
                 for b in range(nb) for k, (tx, ty) in enumerate(peers)]
        return local, sends, recvs

    shapes = [jax.ShapeDtypeStruct(a.shape, a.dtype) for a in parts]
    return Side(parts, shapes, 3 * nb, nb, plan)


def adamw(w, g, m, v, name):
    shape = w.shape
    cols = shape[-1]
    rows = int(np.prod(shape[:-1]))
    w2, g2, m2, v2 = [a.reshape(rows, cols) for a in (w, g, m, v)]
    tr = _tile(rows, 256, 8) if rows % 8 == 0 else rows

    def fn(ids, wt, gt, mt, vt):
        m_new = ADAM_B1 * mt + (1.0 - ADAM_B1) * gt
        v_new = ADAM_B2 * vt + (1.0 - ADAM_B2) * (gt * gt)
        m_hat = m_new / (1.0 - ADAM_B1 ** ADAM_STEP)
        v_hat = v_new / (1.0 - ADAM_B2 ** ADAM_STEP)
        delta = -ADAM_LR * (m_hat / (jnp.sqrt(v_hat) + ADAM_EPS) + ADAM_WD * wt)
        return delta, m_new, v_new

    res = tcall(fn, (rows // tr,), [_row(a, tr) for a in (w2, g2, m2, v2)],
                [_row_out(rows, cols, F32, tr) for _ in range(3)], name)
    return [a.reshape(shape) for a in res]


_MATS = [("ffn1_w_gu", "col"), ("ffn1_w_down", "row"), ("ev_w_in", "col"), ("ev_w_out", "row"),
         ("od_w_in", "col"), ("od_w_out", "row"), ("xa_w_q", "row"), ("xa_w_kv", "col"), ("xa_w_o", "row"),
         ("ffn2_w_gu", "col"), ("ffn2_w_down", "row")]
_VECS = ["ffn1_norm", "mix_norm", "ev_q_gain", "ev_k_gain", "ev_sinks", "od_q_gain", "od_k_gain", "xa_norm",
         "xa_mem_norm", "xa_q_gain", "xa_k_gain", "ffn2_norm"]
_WEIGHTS = ["ffn1_norm", "ffn1_w_gu", "ffn1_w_down", "mix_norm", "ev_w_in", "ev_q_gain", "ev_k_gain", "ev_sinks",
            "ev_w_out", "od_w_in", "od_q_gain", "od_k_gain", "od_w_out", "xa_norm", "xa_mem_norm", "xa_w_q", "xa_w_kv",
            "xa_q_gain", "xa_k_gain", "xa_w_o", "ffn2_norm", "ffn2_w_gu", "ffn2_w_down"]


_AXIS = dict(_MATS)
DEPTH = 2


def _layer_groups(l):
    first, rest = _first_block_groups(l)
    return [first[0] + rest[0] + rest[1]]


def _first_block_groups(l):
    w_in, w_out = ("ev_w_in", "ev_w_out") if l % 2 == 0 else ("od_w_in", "od_w_out")
    first = [[("ffn1_w_gu", l), ("ffn1_w_down", l)]]
    rest = [[("ffn2_w_gu", l), ("xa_w_kv", l)],
            [(w_in, l // 2), ("ffn2_w_down", l), (w_out, l // 2), ("xa_w_q", l), ("xa_w_o", l)]]
    return first, rest


def _block_rows(shards, n):
    a, b = shards[n].shape[1:]
    return a if _AXIS[n] == "row" else b


def _weight_blocks(shards, groups):
    blocks = []
    for group in groups:
        rows = [(shards[n][j] if _AXIS[n] == "row" else shards[n][j].T).astype(BF16) for n, j in group]
        blocks.append(rows[0] if len(rows) == 1 else jnp.concatenate(rows, axis=0))
    return blocks


def _whole_weights(shards, groups, gathered):
    full = {}
    for group, got in zip(groups, gathered):
        off = 0
        for n, j in group:
            r = _block_rows(shards, n)
            full[n] = got[:, off:off + r, :].reshape(N_DEV * r, got.shape[2])
            off += r
    return full


def _gradient_buffers(grads, groups):
    bufs = []
    for group in groups:
        rows = []
        for n, _ in group:
            whole = jnp.concatenate(grads[n], axis=0) if isinstance(grads[n], tuple) else grads[n]
            rows.append(whole.reshape(N_DEV, whole.shape[0] // N_DEV, whole.shape[1]))
        bufs.append((rows[0] if len(rows) == 1 else jnp.concatenate(rows, axis=1)).astype(BF16))
    return bufs


def _gradient_blocks(shards, groups, sums):
    out = {}
    for group, tot in zip(groups, sums):
        off = 0
        for n, j in group:
            r = _block_rows(shards, n)
            out[n, j] = tot[off:off + r] if _AXIS[n] == "row" else tot[off:off + r].T
            off += r
    return out


class _PairChain:
    def __init__(self, ex, bufs):
        self.ex, self.bufs, self.parts = ex, bufs, None

    def side(self, name):
        return pair_side(self.bufs) if name == "dwd" else None

    def done(self, name, carried):
        self.parts = self.ex.pair_sums(self.bufs, carried, "l1")


class _RestChain:
    HALF = {"dwd": (0,), "dwgu": (1,)}

    def __init__(self, ex, bufs):
        self.ex, self.bufs, self.parts, self.sums = ex, bufs, None, [None] * len(bufs)

    def side(self, name):
        if name == "da":
            return pair_side(self.bufs)
        return chip_side([self.parts[i] for i in self.HALF[name]])

    def done(self, name, carried):
        if name == "da":
            self.parts = self.ex.pair_sums(self.bufs, carried, "l0r")
        else:
            for i, tot in zip(self.HALF[name], self.ex.chip_sums(carried, "l0r_" + name)):
                self.sums[i] = tot


class _Exchange:
    def __init__(self, shards, c):
        self.shards, self.c = shards, c

    def weights_first(self):
        first, _ = _first_block_groups(0)
        return _whole_weights(self.shards, first, all_gather_blocks(_weight_blocks(self.shards, first)))

    def rest_blocks(self):
        return _weight_blocks(self.shards, _first_block_groups(0)[1])

    def weights_rest(self, gathered):
        return _whole_weights(self.shards, _first_block_groups(0)[1], gathered)

    def gather_start(self):
        return gather_side(0, _weight_blocks(self.shards, _layer_groups(1)))

    def weights_next(self, gathered):
        return _whole_weights(self.shards, _layer_groups(1), gathered)

    def chain_next(self, grads):
        return _PairChain(self, _gradient_buffers(grads, _layer_groups(1)))

    def chain_rest(self, grads):
        return _RestChain(self, _gradient_buffers(grads, _first_block_groups(0)[1]))

    def pair_sums(self, bufs, got, tag):
        return [pair_sum(b, g, self.c, b.dtype, f"grads_pair_sum_{tag}_{i}") for i, (b, g) in enumerate(zip(bufs, got))]

    def chip_sums(self, parts, tag):
        return [chip_sum(p, f"grads_chip_sum_{tag}_{i}") for i, p in enumerate(parts)]

    def finish(self, gm, gv, sums1, sums_rest):
        vecs = {n: jnp.concatenate(v, axis=0) for n, v in gv.items()}
        first, rest = _first_block_groups(0)
        bufs = _gradient_buffers(gm[0], first)
        vec = jnp.concatenate([vecs[n].reshape(-1) for n in _VECS])
        vec = jnp.pad(vec, (0, -vec.shape[0] % (16 * LANES)))
        bufs.append(jnp.broadcast_to(vec.reshape(1, -1, LANES), (N_DEV, vec.shape[0] // LANES, LANES)))
        parts = self.pair_sums(bufs, pair_exchange(bufs), "l0")
        sums0 = self.chip_sums(chip_exchange(parts), "l0")
        blocks = {**_gradient_blocks(self.shards, first, sums0[:-1]), **_gradient_blocks(self.shards, rest, sums_rest),
                  **_gradient_blocks(self.shards, _layer_groups(1), sums1)}
        out = {n: jnp.stack([blocks[n, j] for j in range(self.shards[n].shape[0])]) for n, _ in _MATS}
        flat, off = sums0[-1].reshape(-1), 0
        for n in _VECS:
            out[n] = flat[off:off + vecs[n].size].reshape(vecs[n].shape)
            off += vecs[n].size
        return out


class _NoExchange:
    def __init__(self, full):
        self.full = full

    def weights_first(self):
        return self.full[0]

    def rest_blocks(self):
        return None

    def gather_start(self):
        return None

    def weights_next(self, gathered):
        return self.full[1]

    def chain_next(self, grads):
        return None

    def chain_rest(self, grads):
        return None

    def finish(self, gm, gv, sums1, sums_rest):
        mats = {}
        for l in range(DEPTH):
            for group in _layer_groups(l):
                for n, j in group:
                    whole = jnp.concatenate(gm[l][n], axis=0) if isinstance(gm[l][n], tuple) else gm[l][n]
                    mats.setdefault(n, {})[j] = whole if _AXIS[n] == "row" else whole.T
        mats = {n: jnp.stack([v[j] for j in sorted(v)]) for n, v in mats.items()}
        return mats, {n: jnp.concatenate(v, axis=0) for n, v in gv.items()}


def _local_step(x, mem, target, w, ex):
    assert w["ffn1_norm"].shape[0] == DEPTH
    row = lambda a, l: a[l:l + 1]
    full = [ex.weights_first(), None]
    saved = []
    for l in range(DEPTH):
        t, j, f = f"l{l}", l // 2, full[l]
        rest = ex.rest_blocks() if l == 0 else None
        if rest is None:
            x, s1 = ffn_fwd(x, row(w["ffn1_norm"], l), f["ffn1_w_gu"], f["ffn1_w_down"], t + "_ffn1")
        else:
            x, s1, rest = ffn_fwd(x, row(w["ffn1_norm"], l), f["ffn1_w_gu"], f["ffn1_w_down"], t + "_ffn1", (0, rest))
        relay = None
        if l % 2 == 0:
            h = rmsnorm_fwd(x, row(w["mix_norm"], l), t + "_ev_norm", None if rest is None else gather_side(2, rest))
            if rest is not None:
                h, rest = h
                f = full[l] = {**f, **ex.weights_rest(rest)}
            side = ex.gather_start() if l + 1 < DEPTH else None
            x, s2, relay = even_mixer_fwd(x, h, _ev_reorder(f["ev_w_in"]), row(w["ev_q_gain"], j),
                                          row(w["ev_k_gain"], j), row(w["ev_sinks"], j), f["ev_w_out"], t + "_ev", side)
        else:
            x, s2 = odd_mixer_fwd(x, row(w["mix_norm"], l), f["od_w_in"], row(w["od_q_gain"], j),
                                  row(w["od_k_gain"], j), f["od_w_out"], t + "_od")
        x, s3 = xa_fwd(x, mem, row(w["xa_norm"], l), row(w["xa_mem_norm"], l), f["xa_w_q"], f["xa_w_kv"],
                       row(w["xa_q_gain"], l), row(w["xa_k_gain"], l), f["xa_w_o"], t + "_xa")
        if relay is None:
            x, s4 = ffn_fwd(x, row(w["ffn2_norm"], l), f["ffn2_w_gu"], f["ffn2_w_down"], t + "_ffn2")
        else:
            x, s4, relay = ffn_fwd(x, row(w["ffn2_norm"], l), f["ffn2_w_gu"], f["ffn2_w_down"], t + "_ffn2", (1, relay))
        if l + 1 < DEPTH:
            full[l + 1] = ex.weights_next(relay)
        saved.append((s1, s2, s3, s4))
    dx, sq = loss_head(x, target, "loss_head")
    loss = 0.5 * jnp.sum(sq) / x.shape[1]

    gm = [dict() for _ in range(DEPTH)]
    gv = {n: [None] * w[n].shape[0] for n in _VECS}
    chain1 = chain0 = sums1 = None
    for l in reversed(range(DEPTH)):
        t, j, f = f"l{l}", l // 2, full[l]
        s1, s2, s3, s4 = saved[l]
        dx, gv["ffn2_norm"][l], gm[l]["ffn2_w_gu"], gm[l]["ffn2_w_down"] = ffn_bwd(
            dx, s4, row(w["ffn2_norm"], l), f["ffn2_w_gu"], f["ffn2_w_down"], t + "_ffn2", chain1 if l == 0 else None)
        parts = chain1.parts if l == 0 and chain1 is not None else None
        (dx, gv["xa_norm"][l], gv["xa_mem_norm"][l], gm[l]["xa_w_q"], gm[l]["xa_w_kv"], gv["xa_q_gain"][l],
         gv["xa_k_gain"][l], gm[l]["xa_w_o"]) = xa_bwd(
            dx, s3, mem, row(w["xa_norm"], l), row(w["xa_mem_norm"], l), f["xa_w_q"], f["xa_w_kv"],
            row(w["xa_q_gain"], l), row(w["xa_k_gain"], l), f["xa_w_o"], t + "_xa")
        if l % 2 == 0:
            (dx, gv["mix_norm"][l], d_win, gv["ev_q_gain"][j], gv["ev_k_gain"][j], gv["ev_sinks"][j],
             gm[l]["ev_w_out"], carried) = even_mixer_bwd(
                dx, s2, row(w["mix_norm"], l), _ev_reorder(f["ev_w_in"]), row(w["ev_q_gain"], j), row(w["ev_k_gain"], j),
                row(w["ev_sinks"], j), f["ev_w_out"], t + "_ev", None if parts is None else chip_side(parts))
            gm[l]["ev_w_in"] = _ev_restore(d_win)
            if carried is not None:
                sums1 = ex.chip_sums(carried, "l1")
        else:
            (dx, gv["mix_norm"][l], gm[l]["od_w_in"], gv["od_q_gain"][j], gv["od_k_gain"][j],
             gm[l]["od_w_out"]) = odd_mixer_bwd(
                dx, s2, row(w["mix_norm"], l), f["od_w_in"], row(w["od_q_gain"], j), row(w["od_k_gain"], j),
                f["od_w_out"], t + "_od")
        if l == 0:
            chain0 = ex.chain_rest(gm[l])
        dx, gv["ffn1_norm"][l], gm[l]["ffn1_w_gu"], gm[l]["ffn1_w_down"] = ffn_bwd(
            dx, s1, row(w["ffn1_norm"], l), f["ffn1_w_gu"], f["ffn1_w_down"], t + "_ffn1", chain0 if l == 0 else None)
        if l == 1:
            chain1 = ex.chain_next(gm[l])
    return loss, dx, ex.finish(gm, gv, sums1, None if chain0 is None else chain0.sums)


def kernel(x, mem, ffn1_norm, ffn1_w_gu, ffn1_w_down, mix_norm, ev_w_in, ev_q_gain, ev_k_gain, ev_sinks, ev_w_out, od_w_in, od_q_gain, od_k_gain, od_w_out, xa_norm, xa_mem_norm, xa_w_q, xa_w_kv, xa_q_gain, xa_k_gain, xa_w_o, ffn2_norm, ffn2_w_gu, ffn2_w_down, loss_target, m_ffn1_norm, m_ffn1_w_gu, m_ffn1_w_down, m_mix_norm, m_ev_w_in, m_ev_q_gain, m_ev_k_gain, m_ev_sinks, m_ev_w_out, m_od_w_in, m_od_q_gain, m_od_k_gain, m_od_w_out, m_xa_norm, m_xa_mem_norm, m_xa_w_q, m_xa_w_kv, m_xa_q_gain, m_xa_k_gain, m_xa_w_o, m_ffn2_norm, m_ffn2_w_gu, m_ffn2_w_down, v_ffn1_norm, v_ffn1_w_gu, v_ffn1_w_down, v_mix_norm, v_ev_w_in, v_ev_q_gain, v_ev_k_gain, v_ev_sinks, v_ev_w_out, v_od_w_in, v_od_q_gain, v_od_k_gain, v_od_w_out, v_xa_norm, v_xa_mem_norm, v_xa_w_q, v_xa_w_kv, v_xa_q_gain, v_xa_k_gain, v_xa_w_o, v_ffn2_norm, v_ffn2_w_gu, v_ffn2_w_down):
    w = dict(ffn1_norm=ffn1_norm, ffn1_w_gu=ffn1_w_gu, ffn1_w_down=ffn1_w_down, mix_norm=mix_norm, ev_w_in=ev_w_in, ev_q_gain=ev_q_gain, ev_k_gain=ev_k_gain, ev_sinks=ev_sinks, ev_w_out=ev_w_out, od_w_in=od_w_in, od_q_gain=od_q_gain, od_k_gain=od_k_gain, od_w_out=od_w_out, xa_norm=xa_norm, xa_mem_norm=xa_mem_norm, xa_w_q=xa_w_q, xa_w_kv=xa_w_kv, xa_q_gain=xa_q_gain, xa_k_gain=xa_k_gain, xa_w_o=xa_w_o, ffn2_norm=ffn2_norm, ffn2_w_gu=ffn2_w_gu, ffn2_w_down=ffn2_w_down)
    m = dict(ffn1_norm=m_ffn1_norm, ffn1_w_gu=m_ffn1_w_gu, ffn1_w_down=m_ffn1_w_down, mix_norm=m_mix_norm, ev_w_in=m_ev_w_in, ev_q_gain=m_ev_q_gain, ev_k_gain=m_ev_k_gain, ev_sinks=m_ev_sinks, ev_w_out=m_ev_w_out, od_w_in=m_od_w_in, od_q_gain=m_od_q_gain, od_k_gain=m_od_k_gain, od_w_out=m_od_w_out, xa_norm=m_xa_norm, xa_mem_norm=m_xa_mem_norm, xa_w_q=m_xa_w_q, xa_w_kv=m_xa_w_kv, xa_q_gain=m_xa_q_gain, xa_k_gain=m_xa_k_gain, xa_w_o=m_xa_w_o, ffn2_norm=m_ffn2_norm, ffn2_w_gu=m_ffn2_w_gu, ffn2_w_down=m_ffn2_w_down)
    v = dict(ffn1_norm=v_ffn1_norm, ffn1_w_gu=v_ffn1_w_gu, ffn1_w_down=v_ffn1_w_down, mix_norm=v_mix_norm, ev_w_in=v_ev_w_in, ev_q_gain=v_ev_q_gain, ev_k_gain=v_ev_k_gain, ev_sinks=v_ev_sinks, ev_w_out=v_ev_w_out, od_w_in=v_od_w_in, od_q_gain=v_od_q_gain, od_k_gain=v_od_k_gain, od_w_out=v_od_w_out, xa_norm=v_xa_norm, xa_mem_norm=v_xa_mem_norm, xa_w_q=v_xa_w_q, xa_w_kv=v_xa_w_kv, xa_q_gain=v_xa_q_gain, xa_k_gain=v_xa_k_gain, xa_w_o=v_xa_w_o, ffn2_norm=v_ffn2_norm, ffn2_w_gu=v_ffn2_w_gu, ffn2_w_down=v_ffn2_w_down)

    c = lax.axis_index("c").astype(jnp.int32).reshape(1)
    loss, dx, grads = _local_step(x[0], mem[0], loss_target[0], w, _Exchange(w, c))
    loss = lax.psum(loss, ("x", "y", "c"))

    delta, new_m, new_v = {}, {}, {}
    for n in _WEIGHTS:
        delta[n], new_m[n], new_v[n] = adamw(w[n], grads[n], m[n], v[n], "adamw_" + n)
    return (loss, dx[None], *[grads[n] for n in _WEIGHTS], *[delta[n] for n in _WEIGHTS],
            *[new_m[n] for n in _WEIGHTS], *[new_v[n] for n in _WEIGHTS])
```

```python
import functools

import numpy as np
import jax
import jax.numpy as jnp
from jax import lax
from jax.experimental import pallas as pl
from jax.experimental.pallas import tpu as pltpu

F32 = jnp.float32
BF16 = jnp.bfloat16
MESH = pl.DeviceIdType.MESH

HEAD_DIM = 64
BLOCK = 128
RMS_EPS = 1e-6
A_Q_HEADS, A_KV_HEADS = 8, 2
B_HEADS = 8
C_HEADS = 16
C_PATTERNS = ((128, 1), (512, 4), (2048, 16))
X_HEADS = 4
N_DEV = 8
LANES = 1024
VMEM_LIMIT_BYTES = 56 * 1024 * 1024
SB_SKIP_LOG = -110.0
NEG_BIG = -1e30

ADAM_LR, ADAM_B1, ADAM_B2, ADAM_EPS, ADAM_WD, ADAM_STEP = 0.001, 0.9, 0.999, 1e-08, 0.01, 10

NN = (((1,), (0,)), ((), ()))
NT = (((1,), (1,)), ((), ()))
TN = (((0,), (0,)), ((), ()))


class Side:
    def __init__(self, arrays, out_shapes, n_remote, n_local, plan, aliased=False):
        self.arrays, self.out_shapes, self.plan, self.aliased = list(arrays), list(out_shapes), plan, aliased
        self.sems = [pltpu.SemaphoreType.DMA((n_remote,)), pltpu.SemaphoreType.DMA((n_remote,)),
                     pltpu.SemaphoreType.DMA((max(n_local, 1),))]

    def start(self, ins, outs, sems):
        local, sends, _ = self.plan(ins, outs, *sems)
        for make in local + sends:
            make().start()

    def wait(self, ins, outs, sems):
        local, sends, recvs = self.plan(ins, outs, *sems)
        for make in sends:
            make().wait_send()
        for make in recvs:
            make().wait_recv()
        for make in local:
            make().wait()


def _pcall(body, side=None, **kw):
    if side is None:
        return pl.pallas_call(body, **kw)
    grid = kw["grid"]
    single = not isinstance(kw["out_specs"], (list, tuple))
    out_specs = [kw["out_specs"]] if single else list(kw["out_specs"])
    out_shape = [kw["out_shape"]] if single else list(kw["out_shape"])
    scratch = list(kw.get("scratch_shapes", []))
    n_in, n_out, n_scr, n_side = len(kw["in_specs"]), len(out_specs), len(scratch), len(side.arrays)
    n_sout = len(side.out_shapes)

    def hosted(*refs):
        ins, s_in = refs[:n_in], refs[n_in:n_in + n_side]
        outs = refs[n_in + n_side:n_in + n_side + n_out]
        s_out = refs[n_in + n_side + n_out:n_in + n_side + n_out + n_sout]
        rest = refs[n_in + n_side + n_out + n_sout:]
        scr, sems = rest[:n_scr], rest[n_scr:]
        first = last = None
        for a, size in enumerate(grid):
            f, l = pl.program_id(a) == 0, pl.program_id(a) == size - 1
            first = f if first is None else jnp.logical_and(first, f)
            last = l if last is None else jnp.logical_and(last, l)

        @pl.when(first)
        def _():
            side.start(s_in, s_out, sems)

        body(*ins, *outs, *scr)

        @pl.when(last)
        def _():
            side.wait(s_in, s_out, sems)

    any_space = pl.BlockSpec(memory_space=pl.ANY)
    kw2 = dict(kw)
    kw2.update(in_specs=list(kw["in_specs"]) + [any_space] * n_side, out_specs=out_specs + [any_space] * n_sout,
               out_shape=out_shape + side.out_shapes, scratch_shapes=scratch + side.sems)
    if side.aliased:
        kw2["input_output_aliases"] = {n_in + i: n_out + i for i in range(n_side)}
    call = pl.pallas_call(hosted, **kw2)

    def run(*args):
        res = call(*args, *side.arrays)
        return (res[0] if single else list(res[:n_out])), list(res[n_out:])

    return run


def _params(**kw):
    return pltpu.CompilerParams(vmem_limit_bytes=VMEM_LIMIT_BYTES, **kw)


def _tile(dim, cap, unit=128):
    if dim <= cap:
        return dim
    t = (cap // unit) * unit
    while t >= unit:
        if dim % t == 0:
            return t
        t -= unit
    raise ValueError(f"no tile for {dim} under {cap}")


def _dot(a, b, dims):
    return lax.dot_general(a.astype(BF16), b.astype(BF16), dims, preferred_element_type=F32)


@functools.partial(jax.custom_vjp, nondiff_argnums=(2,))
def _dot_vjp(a, b, nt):
    return _dot(a, b, NT if nt else NN)


def _dot_vjp_fwd(a, b, nt):
    return _dot(a, b, NT if nt else NN), (a.astype(BF16), b.astype(BF16))


def _dot_vjp_bwd(nt, res, g):
    a, b = res
    if nt:
        return _dot(g, b, NN), _dot(g, a, TN)
    return _dot(g, b, NT), _dot(a, g, TN)


_dot_vjp.defvjp(_dot_vjp_fwd, _dot_vjp_bwd)


def _plain_dot(a, b, nt):
    return _dot(a, b, NT if nt else NN)


def _split_dot(x, mat, terms=2):
    out, rem = None, x
    for t in range(terms):
        part = rem.astype(BF16)
        d = lax.dot_general(part, mat, NN, preferred_element_type=F32)
        out = d if out is None else out + d
        if t + 1 < terms:
            rem = rem - part.astype(F32)
    return out


@functools.partial(jax.custom_vjp, nondiff_argnums=(3,))
def _split_dot_vjp(x, mat, mat_t, terms):
    return _split_dot(x, mat, terms)


def _split_dot_vjp_fwd(x, mat, mat_t, terms):
    return _split_dot(x, mat, terms), mat_t


def _split_dot_vjp_bwd(terms, mat_t, g):
    return _split_dot(g, mat_t, terms), None, None


_split_dot_vjp.defvjp(_split_dot_vjp_fwd, _split_dot_vjp_bwd)


def _plain_split(x, mat, mat_t, terms):
    return _split_dot(x, mat, terms)


def _tri(after):
    j = lax.broadcasted_iota(jnp.int32, (BLOCK, BLOCK), 0)
    s = lax.broadcasted_iota(jnp.int32, (BLOCK, BLOCK), 1)
    return jnp.where(j > s if after else j < s, 1.0, 0.0).astype(BF16)


def _in(a, block, imap):
    return (a, block, imap)


def _out(shape, dtype, block, imap, acc=False):
    return (shape, dtype, block, imap, acc)


def tcall(fn, grid, ins, outs, name, scratch=None, side=None):
    nin = len(ins)
    nout = len(outs)
    ngrid = len(grid)

    def body(*refs):
        ids = tuple(pl.program_id(a) for a in range(ngrid))
        extra = {} if scratch is None else {"scratch": refs[nin + nout]}
        res = fn(ids, *[r[...] for r in refs[:nin]], **extra)
        first = ids[0] == 0
        for a in range(1, ngrid):
            first = jnp.logical_and(first, ids[a] == 0)
        for o_ref, r, spec in zip(refs[nin:nin + nout], res, outs):
            if spec[4]:
                @pl.when(first)
                def _(o_ref=o_ref):
                    o_ref[...] = jnp.zeros(o_ref.shape, o_ref.dtype)
                o_ref[...] += r.astype(o_ref.dtype)
            else:
                o_ref[...] = r.astype(o_ref.dtype)

    return _pcall(
        body, side=side, name=name, grid=grid,
        in_specs=[pl.BlockSpec(b, m) for (_, b, m) in ins],
        out_specs=[pl.BlockSpec(b, m) for (_, _, b, m, _) in outs],
        out_shape=[jax.ShapeDtypeStruct(s, d) for (s, d, _, _, _) in outs],
        scratch_shapes=[] if scratch is None else [pltpu.VMEM(*scratch)],
        compiler_params=_params(),
    )(*[a for (a, _, _) in ins])


def _to_strided(scr, nat, d):
    if d == 1:
        return nat
    t, w = nat.shape
    nc = w // BLOCK
    for c in range(nc):
        scr[c * t:(c + 1) * t, :] = nat[:, c * BLOCK:(c + 1) * BLOCK]
    return jnp.concatenate([scr[pl.ds(c * t + r, t // d, stride=d), :] for r in range(d) for c in range(nc)], axis=1)


def _to_natural(scr, st, d):
    if d == 1:
        return st.astype(F32)
    t, w = st.shape[0] * d, st.shape[1] // d
    nc = w // BLOCK
    st = st.astype(F32)
    for r in range(d):
        for c in range(nc):
            scr[pl.ds(c * t + r, t // d, stride=d), :] = st[:, r * w + c * BLOCK:r * w + (c + 1) * BLOCK]
    return jnp.concatenate([scr[c * t:(c + 1) * t, :] for c in range(nc)], axis=1)


def _row(a, tm, width=None, cb=0):
    width = a.shape[1] if width is None else width
    return _in(a, (tm, width), lambda i, cb=cb: (i, cb))


def _full(a):
    zeros = (0,) * a.ndim
    return _in(a, a.shape, lambda *ids: zeros)


def _row_out(n, width, dtype, tm):
    return _out((n, width), dtype, (tm, width), lambda i: (i, 0))


def _acc_out(shape):
    zeros = (0,) * len(shape)
    return _out(shape, F32, shape, lambda *ids: zeros, acc=True)


def mm(a, b, mode, name, *, out_dtype=F32, scale=1.0, res=None, side=None):
    if mode == "nn":
        (m, k), (k2, n) = a.shape, b.shape
    elif mode == "nt":
        (m, k), (n, k2) = a.shape, b.shape
    else:
        (k, m), (k2, n) = a.shape, b.shape
    assert k == k2, (a.shape, b.shape, mode)
    tm, tn, tk = _tile(m, 1408 if mode == "tn" else 512), _tile(n, 1408), _tile(k, 1408)
    nk = k // tk
    dims = {"nn": NN, "nt": NT, "tn": TN}[mode]
    has_res = res is not None

    def body(*refs):
        if has_res:
            a_ref, b_ref, r_ref, o_ref, acc_ref = refs
        else:
            a_ref, b_ref, o_ref, acc_ref = refs
        kk = pl.program_id(2)

        @pl.when(kk == 0)
        def _():
            acc_ref[...] = jnp.zeros(acc_ref.shape, F32)

        acc_ref[...] += _dot(a_ref[...], b_ref[...], dims)

        @pl.when(kk == nk - 1)
        def _():
            out = acc_ref[...]
            if scale != 1.0:
                out = out * scale
            if has_res:
                out = out + r_ref[...]
            o_ref[...] = out.astype(o_ref.dtype)

    a_spec = (pl.BlockSpec((tk, tm), lambda i, j, kk: (kk, i)) if mode == "tn"
              else pl.BlockSpec((tm, tk), lambda i, j, kk: (i, kk)))
    b_spec = (pl.BlockSpec((tn, tk), lambda i, j, kk: (j, kk)) if mode == "nt"
              else pl.BlockSpec((tk, tn), lambda i, j, kk: (kk, j)))
    in_specs = [a_spec, b_spec]
    args = [a, b]
    if has_res:
        in_specs.append(pl.BlockSpec((tm, tn), lambda i, j, kk: (i, j)))
        args.append(res)
    order = ("parallel", "parallel", "arbitrary") if side is None else ("arbitrary",) * 3
    return _pcall(
        body, side=side, name=name, grid=(m // tm, n // tn, nk),
        in_specs=in_specs,
        out_specs=pl.BlockSpec((tm, tn), lambda i, j, kk: (i, j)),
        out_shape=jax.ShapeDtypeStruct((m, n), out_dtype),
        scratch_shapes=[pltpu.VMEM((tm, tn), F32)],
        compiler_params=_params(dimension_semantics=order),
    )(*args)


def _rms(x, g):
    return x * lax.rsqrt(jnp.mean(x * x, axis=-1, keepdims=True) + RMS_EPS) * g


def _silu_mul(gate, up):
    return gate / (1.0 + jnp.exp(-gate)) * up


def mm_gate_up(h, w_gu, name, side=None):
    m, k = h.shape
    f = w_gu.shape[0] // 2
    tm, tn = _tile(m, 512), _tile(f, 1408)
    nj = f // tn
    assert k <= 1408

    def body(h_ref, wg_ref, wu_ref, g_ref, u_ref, a_ref):
        ht = h_ref[...]
        for lo in range(0, tn, 512):
            cols = slice(lo, min(lo + 512, tn))
            gate, up = _dot(ht, wg_ref[cols, :], NT), _dot(ht, wu_ref[cols, :], NT)
            g_ref[:, cols] = gate
            u_ref[:, cols] = up
            a_ref[:, cols] = _silu_mul(gate, up).astype(a_ref.dtype)

    tile = pl.BlockSpec((tm, tn), lambda i, j: (i, j))
    return _pcall(
        body, side=side, name=name, grid=(m // tm, nj),
        in_specs=[pl.BlockSpec((tm, k), lambda i, j: (i, 0)),
                  pl.BlockSpec((tn, k), lambda i, j: (j, 0)),
                  pl.BlockSpec((tn, k), lambda i, j: (j + nj, 0))],
        out_specs=[tile, tile, tile],
        out_shape=[jax.ShapeDtypeStruct((m, f), F32), jax.ShapeDtypeStruct((m, f), F32),
                   jax.ShapeDtypeStruct((m, f), BF16)],
        compiler_params=_params(dimension_semantics=("arbitrary",) * 2),
    )(h, w_gu, w_gu)


def mm_down_act_bwd(dy, w_down, gate, up, name, side=None):
    m, d = dy.shape
    f = w_down.shape[0]
    tm, tn = _tile(m, 512), _tile(f, 1408)
    assert d <= 1408

    def body(dy_ref, w_ref, g_ref, u_ref, dg_ref, du_ref):
        dyt = dy_ref[...].astype(BF16)
        for lo in range(0, tn, 512):
            cols = slice(lo, min(lo + 512, tn))
            da = _dot(dyt, w_ref[cols, :], NT) * 0.5
            _, vjp = jax.vjp(_silu_mul, g_ref[:, cols], u_ref[:, cols])
            dg, du = vjp(da)
            dg_ref[:, cols] = dg.astype(dg_ref.dtype)
            du_ref[:, cols] = du.astype(du_ref.dtype)

    tile = pl.BlockSpec((tm, tn), lambda i, j: (i, j))
    return _pcall(
        body, side=side, name=name, grid=(m // tm, f // tn),
        in_specs=[pl.BlockSpec((tm, d), lambda i, j: (i, 0)), pl.BlockSpec((tn, d), lambda i, j: (j, 0)), tile, tile],
        out_specs=[tile, tile],
        out_shape=[jax.ShapeDtypeStruct((m, f), BF16), jax.ShapeDtypeStruct((m, f), BF16)],
        compiler_params=_params(dimension_semantics=("arbitrary", "arbitrary")),
    )(dy, w_down, gate, up)


def mm_norm_bwd(a, b, x, g, dres, name, b_kd=False):
    halves = isinstance(a, (tuple, list))
    a0, a1 = a if halves else (a, None)
    m, k = a0.shape[0], a0.shape[1] * (2 if halves else 1)
    d = b.shape[1] if b_kd else b.shape[0]
    dims = NN if b_kd else NT
    tm, tk = _tile(m, 512), _tile(a0.shape[1], 1408)
    nk = k // tk
    nkh = a0.shape[1] // tk
    has_res = dres is not None

    def body(*refs):
        a_ref, b_ref, x_ref, g_ref = refs[:4]
        rest = refs[4:-3]
        a1_ref = rest[0] if halves else None
        r_ref = rest[-1] if has_res else None
        dx_ref, dg_ref, acc_ref = refs[-3:]
        i, kk = pl.program_id(0), pl.program_id(1)

        @pl.when(kk == 0)
        def _():
            acc_ref[...] = jnp.zeros(acc_ref.shape, F32)

        if halves:
            @pl.when(kk < nkh)
            def _():
                acc_ref[...] += _dot(a_ref[...], b_ref[...], dims)

            @pl.when(kk >= nkh)
            def _():
                acc_ref[...] += _dot(a1_ref[...], b_ref[...], dims)
        else:
            acc_ref[...] += _dot(a_ref[...], b_ref[...], dims)

        @pl.when(kk == nk - 1)
        def _():
            _, vjp = jax.vjp(_rms, x_ref[...], g_ref[...])
            dx, dg = vjp(acc_ref[...])
            dx_ref[...] = dx + r_ref[...] if has_res else dx

            @pl.when(i == 0)
            def _():
                dg_ref[...] = jnp.zeros(dg_ref.shape, F32)

            dg_ref[...] += dg

    rows = pl.BlockSpec((tm, d), lambda i, kk: (i, 0))
    first = pl.BlockSpec((tm, tk), lambda i, kk: (i, jnp.minimum(kk, nkh - 1)))
    second = pl.BlockSpec((tm, tk), lambda i, kk: (i, jnp.maximum(kk - nkh, 0)))
    b_spec = pl.BlockSpec((tk, d), lambda i, kk: (kk, 0)) if b_kd else pl.BlockSpec((d, tk), lambda i, kk: (0, kk))
    in_specs = ([first, b_spec, rows, pl.BlockSpec(g.shape, lambda i, kk: (0, 0))]
                + ([second] if halves else []) + ([rows] if has_res else []))
    return _pcall(
        body, name=name, grid=(m // tm, nk),
        in_specs=in_specs,
        out_specs=[rows, pl.BlockSpec(g.shape, lambda i, kk: (0, 0))],
        out_shape=[jax.ShapeDtypeStruct((m, d), F32), jax.ShapeDtypeStruct(g.shape, F32)],
        scratch_shapes=[pltpu.VMEM((tm, d), F32)],
        compiler_params=_params(dimension_semantics=("arbitrary", "arbitrary")),
    )(*([a0, b, x, g] + ([a1] if halves else []) + ([dres] if has_res else [])))


def _indicator(shape, head_axis, mod):
    lane = lax.broadcasted_iota(jnp.int32, shape, head_axis)
    other = lax.broadcasted_iota(jnp.int32, shape, 1 - head_axis)
    lane = jnp.bitwise_and(lane, HEAD_DIM - 1) if mod else jnp.right_shift(lane, 6)
    return jnp.where(lane == other, 1.0, 0.0).astype(BF16)


def _head_rms(split, xs, g):
    w = xs.shape[1]
    to_head, from_head = _indicator((w, BLOCK), 0, False), _indicator((BLOCK, w), 1, False)
    to_lane, from_lane = _indicator((HEAD_DIM, w), 1, True), _indicator((w, HEAD_DIM), 0, True)
    ss = split(xs * xs, to_head, from_head, 3)
    r = lax.rsqrt(ss * (1.0 / HEAD_DIM) + RMS_EPS)
    g_all = split(jnp.broadcast_to(g, (8, HEAD_DIM)), to_lane, from_lane, 3)[0:1]
    return xs * split(r, from_head, to_head, 3) * g_all


def _prep(split, x, qg, kg, segs):
    parts = []
    for start, width, kind in segs:
        xs = x[:, start:start + width]
        parts.append(xs if kind == "raw" else _head_rms(split, xs, qg if kind == "q" else kg))
    return jnp.concatenate(parts, axis=1)


def prep_fwd(x, qg, kg, segs, dils, name):
    n, w = x.shape
    tm = _tile(n, 256, 8)

    def fn(ids, xt, a, b, scratch):
        ops = _prep(_plain_split, xt, a, b, segs)
        return tuple(_to_strided(scratch, ops, d) for d in dils)

    return tcall(fn, (n // tm,), [_row(x, tm), _full(qg), _full(kg)],
                 [_out((n // d, d * w), BF16, (tm // d, d * w), lambda i: (i, 0)) for d in dils], name,
                 scratch=((w // BLOCK * tm, BLOCK), F32))


def prep_bwd(x, qg, kg, segs, grads, gather, name):
    n, w = x.shape
    tm = BLOCK
    nblk = n // tm
    nslot = 1 + max(slot for _, _, _, slot in grads)

    def fn(ids, xt, a, b, *t, scratch):
        tiles, dils = [None] * nslot, [None] * nslot
        for ti, (_, sh, d, slot) in zip(t, grads):
            ti = jnp.where(ids[0] + sh < nblk, ti, 0.0) if sh else ti
            tiles[slot] = ti if tiles[slot] is None else tiles[slot] + ti
            dils[slot] = d
        tiles = [_to_natural(scratch, ti, d) for ti, d in zip(tiles, dils)]
        _, vjp = jax.vjp(lambda x_, a_, b_: _prep(_split_dot_vjp, x_, a_, b_, segs), xt, a, b)
        return vjp(gather(*tiles))

    specs = [_in(a, (tm // d, a.shape[1]), (lambda i, sh=sh: (jnp.minimum(i + sh, nblk - 1), 0)))
             for a, sh, d, _ in grads]
    wmax = max(a.shape[1] // d for a, _, d, _ in grads)
    return tcall(fn, (nblk,), [_row(x, tm), _full(qg), _full(kg)] + specs,
                 [_row_out(n, w, BF16, tm), _acc_out(qg.shape), _acc_out(kg.shape)], name,
                 scratch=((wmax // BLOCK * tm, BLOCK), F32))


def rmsnorm_fwd(x, g, name, side=None):
    n, d = x.shape
    tm = _tile(n, 512, 8)
    res = tcall(lambda ids, xt, gt: (_rms(xt, gt),), (n // tm,), [_row(x, tm), _full(g)],
                [_row_out(n, d, BF16, tm)], name, side=side)
    if side is None:
        return res[0]
    return res[0][0], res[1]


def ffn_fwd(x, g, w_gu, w_down, tag, carry=None):
    h = rmsnorm_fwd(x, g, tag + "_norm")
    if carry is None:
        gate, up, a = mm_gate_up(h, w_gu, tag + "_gu")
        return mm(a, w_down, "nn", tag + "_down", scale=0.5, res=x), (x, h, gate, up, a)
    phase, bufs = carry
    (gate, up, a), bufs = mm_gate_up(h, w_gu, tag + "_gu", side=gather_side(phase, bufs))
    y, bufs = mm(a, w_down, "nn", tag + "_down", scale=0.5, res=x, side=gather_side(phase + 1, bufs))
    return y, (x, h, gate, up, a), bufs


def ffn_bwd(dy, saved, g, w_gu, w_down, tag, chain=None):
    x, h, gate, up, a = saved

    def carrying(name, call, **kw):
        side = None if chain is None else chain.side(name)
        out = call(name=tag + "_" + name, side=side, **kw)
        if side is None:
            return out
        chain.done(name, out[1])
        return out[0]

    dgate, dup = carrying("da", mm_down_act_bwd, dy=dy, w_down=w_down, gate=gate, up=up)
    d_wdown = carrying("dwd", mm, a=a, b=dy, mode="tn", scale=0.5)
    d_wgu = (carrying("dwgu", mm, a=dgate, b=h, mode="tn"), mm(dup, h, "tn", tag + "_dwup"))
    dx, dg = mm_norm_bwd((dgate, dup), w_gu, x, g, dy, tag + "_dh", b_kd=True)
    return dx, dg, d_wgu, d_wdown


def _alibi(n_heads):
    return [float(s) for s in np.asarray(2.0 ** (-8.0 * np.arange(1, n_heads + 1) / n_heads), dtype=np.float32)]


def _banded_tile(dot, first, q, kp, kc, vp, vc, sinks, *, hkv, grp, max_dist, step, slopes, want_lse):
    row = lax.broadcasted_iota(jnp.int32, (BLOCK, 2 * BLOCK), 0)
    col = lax.broadcasted_iota(jnp.int32, (BLOCK, 2 * BLOCK), 1)
    dist = row + BLOCK - col
    valid = (dist >= 0) & (dist <= max_dist) & ((col >= BLOCK) | jnp.logical_not(first))
    distf = dist.astype(F32)

    def head(hd, qh, k2, v2):
        s = dot(qh, k2, True) * (HEAD_DIM ** -0.5)
        s = jnp.where(valid, s - (slopes[hd] * step) * distf, NEG_BIG)
        m = jnp.max(s, axis=-1, keepdims=True)
        if sinks is not None:
            pick = lax.broadcasted_iota(jnp.int32, sinks.shape, 1) == hd
            sk = jnp.sum(jnp.where(pick, sinks, 0.0), axis=1, keepdims=True)
            m = jnp.maximum(m, sk)
        m = lax.stop_gradient(m)
        p = jnp.exp(s - m)
        denom = jnp.sum(p, axis=-1, keepdims=True)
        if sinks is not None:
            denom = denom + jnp.exp(sk - m)
        return dot(p / denom, v2, False), m + jnp.log(denom)

    outs, lses = [], []
    if grp == 1:
        low = lax.broadcasted_iota(jnp.int32, (BLOCK, BLOCK), 1) < HEAD_DIM
        for pr in range(hkv // 2):
            sl = slice(pr * BLOCK, (pr + 1) * BLOCK)
            q2 = q[:, sl]
            k2 = jnp.concatenate([kp[:, sl], kc[:, sl]], axis=0)
            v2 = jnp.concatenate([vp[:, sl], vc[:, sl]], axis=0)
            o0, l0 = head(2 * pr, jnp.where(low, q2, 0.0), k2, v2)
            o1, l1 = head(2 * pr + 1, jnp.where(low, 0.0, q2), k2, v2)
            outs.append(jnp.where(low, o0, o1))
            lses.append(jnp.where(low, l0, l1))
    else:
        for hk in range(hkv):
            sl = slice(hk * HEAD_DIM, (hk + 1) * HEAD_DIM)
            k2 = jnp.concatenate([kp[:, sl], kc[:, sl]], axis=0)
            v2 = jnp.concatenate([vp[:, sl], vc[:, sl]], axis=0)
            for gi in range(grp):
                hd = hk * grp + gi
                o_h, l_h = head(hd, q[:, hd * HEAD_DIM:(hd + 1) * HEAD_DIM], k2, v2)
                outs.append(o_h)
                lses.append(jnp.broadcast_to(l_h, (BLOCK, HEAD_DIM)))
    o = jnp.concatenate(outs, axis=1)
    if want_lse:
        return o, jnp.concatenate(lses, axis=1)
    return (o,)


def _banded_specs(view, qcol, kcol, vcol, wq, wkv):
    def at(colfn, prev):
        if prev:
            return lambda r, n: (jnp.maximum(n - 1, 0), colfn(r))
        return lambda r, n: (n, colfn(r))
    return [
        _in(view, (BLOCK, wq), at(qcol, False)),
        _in(view, (BLOCK, wkv), at(kcol, True)),
        _in(view, (BLOCK, wkv), at(kcol, False)),
        _in(view, (BLOCK, wkv), at(vcol, True)),
        _in(view, (BLOCK, wkv), at(vcol, False)),
    ]


def banded_fwd(view, dil, cols, sinks, cfg, name):
    ns = view.shape[0]
    nb = ns // BLOCK
    wq, wkv = cfg["hkv"] * cfg["grp"] * HEAD_DIM, cfg["hkv"] * HEAD_DIM
    has_sinks = sinks is not None

    def fn(ids, q, kp, kc, vp, vc, *rest):
        q, kp, kc, vp, vc = [a.astype(F32) for a in (q, kp, kc, vp, vc)]
        return _banded_tile(_plain_dot, ids[1] == 0, q, kp, kc, vp, vc, rest[0] if has_sinks else None, **cfg)

    ins = _banded_specs(view, *cols, wq, wkv) + ([_full(sinks)] if has_sinks else [])
    outs = [_out((ns, dil * wq), F32 if cfg["want_lse"] else BF16, (BLOCK, wq), lambda r, n: (n, r))]
    if cfg["want_lse"]:
        outs.append(_out((ns, dil * wq), F32, (BLOCK, wq), lambda r, n: (n, r)))
    return tcall(fn, (dil, nb), ins, outs, name)


def banded_bwd(view, dil, cols, sinks, cfg, cts, name):
    ns = view.shape[0]
    nb = ns // BLOCK
    wq, wkv = cfg["hkv"] * cfg["grp"] * HEAD_DIM, cfg["hkv"] * HEAD_DIM
    has_sinks = sinks is not None
    assert len(cts) == (2 if cfg["want_lse"] else 1)

    def fn(ids, q, kp, kc, vp, vc, *rest):
        sk = rest[0] if has_sinks else None
        ct = rest[1 if has_sinks else 0:]
        first = ids[1] == 0

        def f(q, kp, kc, vp, vc, *s):
            return _banded_tile(_dot_vjp, first, q, kp, kc, vp, vc, s[0] if has_sinks else None, **cfg)

        prim = tuple(a.astype(F32) for a in (q, kp, kc, vp, vc)) + ((sk,) if has_sinks else ())
        _, vjp = jax.vjp(f, *prim)
        return vjp(tuple(c.astype(F32) for c in ct))

    ins = (_banded_specs(view, *cols, wq, wkv) + ([_full(sinks)] if has_sinks else [])
           + [_in(a, (BLOCK, wq), (lambda r, n, cf=cf: (n, cf(r)))) for (a, cf) in cts])
    blk = lambda w: _out((ns, dil * w), F32, (BLOCK, w), lambda r, n: (n, r))
    outs = [blk(wq), blk(wkv), blk(wkv), blk(wkv), blk(wkv)]
    if has_sinks:
        outs.append(_acc_out(sinks.shape))
    return tcall(fn, (dil, nb), ins, outs, name)


def _log_sigmoid(z):
    return jnp.minimum(z, 0.0) - jnp.log(1.0 + jnp.exp(-jnp.abs(z)))


SB_PAIRS = 4


def _sb_pair(dot, suffix, qh, kb, vb, r_in, mask):
    z = dot(qh, kb, True) * (HEAD_DIM ** -0.5)
    lsp = _log_sigmoid(z)
    log_keep = jnp.where(mask, lsp - z, 0.0)
    log_after = suffix(log_keep) + r_in
    a = jnp.where(mask, jnp.exp(lsp + log_after), 0.0)
    return dot(a, vb, False), r_in + jnp.sum(log_keep, axis=1, keepdims=True)


def sb_fwd(qkv, qcb, kcb, vcb, name, side=None):
    s = qkv.shape[0]
    nb = s // BLOCK
    pairs = B_HEADS // 2
    wide = SB_PAIRS * BLOCK
    assert pairs % SB_PAIRS == 0 and qcb % SB_PAIRS == 0 and kcb % SB_PAIRS == 0 and vcb % SB_PAIRS == 0

    def body(q_ref, k_ref, v_ref, o_ref):
        n = pl.program_id(1)
        low = lax.broadcasted_iota(jnp.int32, (BLOCK, BLOCK), 1) < HEAD_DIM
        before = (lax.broadcasted_iota(jnp.int32, (2 * BLOCK, BLOCK), 1)
                  < jnp.bitwise_and(lax.broadcasted_iota(jnp.int32, (2 * BLOCK, BLOCK), 0), BLOCK - 1))
        after = _tri(True)
        suffix = lambda t: _split_dot(t, after)
        qs = []
        for p in range(SB_PAIRS):
            q2 = q_ref[:, p * BLOCK:(p + 1) * BLOCK].astype(F32)
            qs.append(jnp.concatenate([jnp.where(low, q2, 0.0), jnp.where(low, 0.0, q2)], axis=0))

        def cond(c):
            return jnp.logical_and(c[0] >= 0, c[1] > SB_SKIP_LOG)

        def step(c):
            kb, _, rs, accs = c
            rows = pl.ds(pl.multiple_of(kb * BLOCK, BLOCK), BLOCK)
            mask = jnp.logical_or(before, kb != n)
            new_r, new_acc, top = [], [], None
            for p in range(SB_PAIRS):
                cols = slice(p * BLOCK, (p + 1) * BLOCK)
                o_part, r_out = _sb_pair(_plain_dot, suffix, qs[p], k_ref[rows, cols], v_ref[rows, cols], rs[p], mask)
                new_r.append(r_out)
                new_acc.append(accs[p] + o_part)
                top = jnp.max(r_out) if top is None else jnp.maximum(top, jnp.max(r_out))
            return kb - 1, top, tuple(new_r), tuple(new_acc)

        init = (n, jnp.float32(0.0), tuple(jnp.zeros((2 * BLOCK, 1), F32) for _ in range(SB_PAIRS)),
                tuple(jnp.zeros((2 * BLOCK, BLOCK), F32) for _ in range(SB_PAIRS)))
        accs = lax.while_loop(cond, step, init)[3]
        for p in range(SB_PAIRS):
            o_ref[:, p * BLOCK:(p + 1) * BLOCK] = jnp.where(low, accs[p][:BLOCK], accs[p][BLOCK:]).astype(o_ref.dtype)

    return _pcall(
        body, side=side, name=name, grid=(pairs // SB_PAIRS, nb),
        in_specs=[pl.BlockSpec((BLOCK, wide), lambda g, n: (n, qcb // SB_PAIRS + g)),
                  pl.BlockSpec((s, wide), lambda g, n: (0, kcb // SB_PAIRS + g), pipeline_mode=pl.Buffered(1)),
                  pl.BlockSpec((s, wide), lambda g, n: (0, vcb // SB_PAIRS + g), pipeline_mode=pl.Buffered(1))],
        out_specs=pl.BlockSpec((BLOCK, wide), lambda g, n: (n, g)),
        out_shape=jax.ShapeDtypeStruct((s, pairs * BLOCK), BF16),
        compiler_params=_params(),
    )(qkv, qkv, qkv)


def sb_bwd(qkv, qcb, kcb, vcb, do, docb, name, side=None):
    s = qkv.shape[0]
    nb = s // BLOCK
    pairs = B_HEADS // 2
    wide = SB_PAIRS * BLOCK
    assert docb % SB_PAIRS == 0

    def body(q_ref, k_ref, v_ref, do_ref, dq_ref, dk_ref, dv_ref, r_ref):
        n = pl.program_id(1)

        @pl.when(n == 0)
        def _():
            dk_ref[...] = jnp.zeros(dk_ref.shape, F32)
            dv_ref[...] = jnp.zeros(dv_ref.shape, F32)

        low = lax.broadcasted_iota(jnp.int32, (BLOCK, BLOCK), 1) < HEAD_DIM
        before = (lax.broadcasted_iota(jnp.int32, (2 * BLOCK, BLOCK), 1)
                  < jnp.bitwise_and(lax.broadcasted_iota(jnp.int32, (2 * BLOCK, BLOCK), 0), BLOCK - 1))
        after, earlier = _tri(True), _tri(False)
        suffix = lambda t: _split_dot_vjp(t, after, earlier, 2)
        stack = lambda t: jnp.concatenate([jnp.where(low, t, 0.0), jnp.where(low, 0.0, t)], axis=0)
        qs = [stack(q_ref[:, p * BLOCK:(p + 1) * BLOCK].astype(F32)) for p in range(SB_PAIRS)]
        dos = [stack(do_ref[:, p * BLOCK:(p + 1) * BLOCK].astype(F32)) for p in range(SB_PAIRS)]

        def cond(c):
            return jnp.logical_and(c[0] >= 0, c[1] > SB_SKIP_LOG)

        def down(c):
            kb, _, rs = c
            rows = pl.ds(pl.multiple_of(kb * BLOCK, BLOCK), BLOCK)
            mask = jnp.logical_or(before, kb != n)
            new_r, top = [], None
            for h in range(SB_PAIRS):
                cols = slice(h * BLOCK, (h + 1) * BLOCK)
                r_ref[h, kb] = rs[h]
                z = _dot(qs[h], k_ref[rows, cols], NT) * (HEAD_DIM ** -0.5)
                log_keep = jnp.where(mask, _log_sigmoid(z) - z, 0.0)
                r_out = rs[h] + jnp.sum(log_keep, axis=1, keepdims=True)
                new_r.append(r_out)
                top = jnp.max(r_out) if top is None else jnp.maximum(top, jnp.max(r_out))
            return kb - 1, top, tuple(new_r)

        init = (n, jnp.float32(0.0), tuple(jnp.zeros((2 * BLOCK, 1), F32) for _ in range(SB_PAIRS)))
        last = lax.while_loop(cond, down, init)[0] + 1

        def up(kb, c):
            dqs, g_rs = c
            rows = pl.ds(pl.multiple_of(kb * BLOCK, BLOCK), BLOCK)
            mask = jnp.logical_or(before, kb != n)
            new_dq, new_g = [], []
            for h in range(SB_PAIRS):
                cols = slice(h * BLOCK, (h + 1) * BLOCK)
                _, vjp = jax.vjp(lambda q_, k_, v_, r_: _sb_pair(_dot_vjp, suffix, q_, k_, v_, r_, mask),
                                 qs[h], k_ref[rows, cols].astype(F32), v_ref[rows, cols].astype(F32), r_ref[h, kb])
                dq_c, dk_c, dv_c, g_in = vjp((dos[h], g_rs[h]))
                dk_ref[rows, cols] += dk_c
                dv_ref[rows, cols] += dv_c
                new_dq.append(dqs[h] + dq_c)
                new_g.append(g_in)
            return tuple(new_dq), tuple(new_g)

        init = (tuple(jnp.zeros((2 * BLOCK, BLOCK), F32) for _ in range(SB_PAIRS)),
                tuple(jnp.zeros((2 * BLOCK, 1), F32) for _ in range(SB_PAIRS)))
        dqs = lax.fori_loop(last, n + 1, up, init)[0]
        for p in range(SB_PAIRS):
            dq_ref[:, p * BLOCK:(p + 1) * BLOCK] = jnp.where(low, dqs[p][:BLOCK], dqs[p][BLOCK:])

    full = jax.ShapeDtypeStruct((s, pairs * BLOCK), F32)
    return _pcall(
        body, side=side, name=name, grid=(pairs // SB_PAIRS, nb),
        in_specs=[pl.BlockSpec((BLOCK, wide), lambda g, n: (n, qcb // SB_PAIRS + g)),
                  pl.BlockSpec((s, wide), lambda g, n: (0, kcb // SB_PAIRS + g), pipeline_mode=pl.Buffered(1)),
                  pl.BlockSpec((s, wide), lambda g, n: (0, vcb // SB_PAIRS + g), pipeline_mode=pl.Buffered(1)),
                  pl.BlockSpec((BLOCK, wide), lambda g, n: (n, docb // SB_PAIRS + g))],
        out_specs=[pl.BlockSpec((BLOCK, wide), lambda g, n: (n, g)),
                   pl.BlockSpec((s, wide), lambda g, n: (0, g), pipeline_mode=pl.Buffered(1)),
                   pl.BlockSpec((s, wide), lambda g, n: (0, g), pipeline_mode=pl.Buffered(1))],
        out_shape=[full, full, full],
        scratch_shapes=[pltpu.VMEM((SB_PAIRS, nb, 2 * BLOCK, 1), F32)],
        compiler_params=_params(),
    )(qkv, qkv, qkv, do)


def _xa_tile(dot, q, kv, qg, kg):
    hd = q.shape[1] // X_HEADS
    outs = []
    for h in range(X_HEADS):
        qh = _rms(q[:, h * hd:(h + 1) * hd], qg)
        kh = _rms(kv[:, h * hd:(h + 1) * hd], kg)
        vh = kv[:, (X_HEADS + h) * hd:(X_HEADS + h + 1) * hd]
        sc = dot(qh, kh, True) * (hd ** -0.5)
        m = lax.stop_gradient(jnp.max(sc, axis=-1, keepdims=True))
        p = jnp.exp(sc - m)
        outs.append(dot(p / jnp.sum(p, axis=-1, keepdims=True), vh, False))
    return jnp.concatenate(outs, axis=1)


def xa_core_fwd(q, kv, qg, kg, name):
    n, d = q.shape
    tm = _tile(n, 256, 8)
    (o,) = tcall(lambda ids, qt, kvt, qgt, kgt: (_xa_tile(_plain_dot, qt, kvt, qgt, kgt),), (n // tm,),
                 [_row(q, tm), _full(kv), _full(qg), _full(kg)], [_row_out(n, d, BF16, tm)], name)
    return o


def xa_core_bwd(q, kv, qg, kg, do, name):
    n, d = q.shape
    tm = _tile(n, 256, 8)

    def fn(ids, qt, kvt, qgt, kgt, dot_):
        _, vjp = jax.vjp(functools.partial(_xa_tile, _dot_vjp), qt, kvt, qgt, kgt)
        return vjp(dot_.astype(F32))

    return tcall(fn, (n // tm,), [_row(q, tm), _full(kv), _full(qg), _full(kg), _row(do, tm)],
                 [_row_out(n, d, BF16, tm), _acc_out(kv.shape), _acc_out(qg.shape), _acc_out(kg.shape)], name)


def _ev_reorder(a):
    return jnp.concatenate([a[0:512], a[768:2304], a[512:768]], axis=0)


def _ev_restore(a):
    return jnp.concatenate([a[0:512], a[2048:2304], a[512:2048]], axis=0)


_EV_SEGS = ((0, 512, "q"), (512, 1536, "raw"), (2048, 128, "k"), (2176, 128, "raw"))
_A_CFG = dict(hkv=A_KV_HEADS, grp=A_Q_HEADS // A_KV_HEADS, max_dist=BLOCK - 1, step=1.0, slopes=_alibi(A_Q_HEADS),
              want_lse=False)
_A_COLS = (lambda r: 0, lambda r: 16, lambda r: 17)


def even_mixer_fwd(x, h, w_in, qg, kg, sinks, w_out, tag, side=None):
    qkv = mm(h, w_in, "nt", tag + "_in")
    (ops,) = prep_fwd(qkv, qg, kg, _EV_SEGS, (1,), tag + "_prep")
    (o_a,) = banded_fwd(ops, 1, _A_COLS, sinks, _A_CFG, tag + "_swa")
    o_b = sb_fwd(ops, 4, 8, 12, tag + "_sb", side=side)
    carried = None
    if side is not None:
        o_b, carried = o_b
    o = jnp.concatenate([o_a, o_b], axis=1)
    y = mm(o, w_out, "nn", tag + "_out", res=x)
    return y, (x, h, qkv, ops, o), carried


def even_mixer_bwd(dy, saved, g, w_in, qg, kg, sinks, w_out, tag, side=None):
    x, h, qkv, ops, o = saved
    do = mm(dy, w_out, "nt", tag + "_do")
    d_wout = mm(o, dy, "tn", tag + "_dwout")
    dqa, dkp, dkc, dvp, dvc, dsinks = banded_bwd(ops, 1, _A_COLS, sinks, _A_CFG, [(do, lambda r: 0)], tag + "_dswa")
    res = sb_bwd(ops, 4, 8, 12, do, 4, tag + "_dsb", side=side)
    carried = None
    if side is not None:
        res, carried = res
    dqb, dkb, dvb = res
    dqkv, dqg, dkg = prep_bwd(
        qkv, qg, kg, _EV_SEGS,
        [(dqa, 0, 1, 0), (dqb, 0, 1, 1), (dkb, 0, 1, 2), (dvb, 0, 1, 3), (dkc, 0, 1, 4), (dkp, 1, 1, 4), (dvc, 0, 1, 5),
         (dvp, 1, 1, 5)],
        lambda *t: jnp.concatenate(t, axis=1), tag + "_dqkv")
    d_win = mm(dqkv, h, "tn", tag + "_dwin")
    dx, dg = mm_norm_bwd(dqkv, w_in, x, g, dy, tag + "_dh", b_kd=True)
    return dx, dg, d_win, dqg, dkg, dsinks, d_wout, carried


def _c_cfg(window, dil):
    return dict(hkv=C_HEADS, grp=1, max_dist=window // dil, step=float(dil), slopes=_alibi(C_HEADS), want_lse=True)


_C_COLS = (lambda r: 3 * r, lambda r: 3 * r + 1, lambda r: 3 * r + 2)
_OD_SEGS = ((0, 1024, "q"), (1024, 1024, "k"), (2048, 1024, "raw"))


def _combine(o1, o2, o3, l1, l2, l3):
    m = lax.stop_gradient(jnp.maximum(jnp.maximum(l1, l2), l3))
    e1, e2, e3 = jnp.exp(l1 - m), jnp.exp(l2 - m), jnp.exp(l3 - m)
    tot = e1 + e2 + e3
    return (e1 / tot) * o1 + (e2 / tot) * o2 + (e3 / tot) * o3


def odd_mixer_fwd(x, g, w_in, qg, kg, w_out, tag):
    n, d = x.shape
    h = rmsnorm_fwd(x, g, tag + "_norm")
    qkv = mm(h, w_in, "nt", tag + "_in")
    dils = [dil for _, dil in C_PATTERNS]
    ops = prep_fwd(qkv, qg, kg, _OD_SEGS, dils, tag + "_prep")
    os_, ls_ = [], []
    for (window, dil), ops_d in zip(C_PATTERNS, ops):
        o_p, l_p = banded_fwd(ops_d, dil, _C_COLS, None, _c_cfg(window, dil), f"{tag}_dil{dil}")
        os_.append(o_p)
        ls_.append(l_p)
    tm = BLOCK
    lay = lambda a, dil: _in(a, (tm // dil, a.shape[1]), lambda i: (i, 0))
    views = [lay(a, dil) for a, dil in zip(os_ + ls_, dils + dils)]

    def comb(ids, *t, scratch):
        return (_combine(*[_to_natural(scratch, a, dil) for a, dil in zip(t, dils + dils)]),)

    (o,) = tcall(comb, (n // tm,), views, [_row_out(n, d, BF16, tm)], tag + "_comb",
                 scratch=((d // BLOCK * tm, BLOCK), F32))
    y = mm(o, w_out, "nn", tag + "_out", res=x)
    return y, (x, h, qkv, ops, views, o)


def odd_mixer_bwd(dy, saved, g, w_in, qg, kg, w_out, tag):
    x, h, qkv, ops, views, o = saved
    n, d = x.shape
    do = mm(dy, w_out, "nt", tag + "_do")
    d_wout = mm(o, dy, "tn", tag + "_dwout")
    tm = BLOCK
    dils = [dil for _, dil in C_PATTERNS]

    def comb_bwd(ids, *t, scratch):
        _, vjp = jax.vjp(_combine, *[_to_natural(scratch, a, dil) for a, dil in zip(t[:6], dils + dils)])
        return tuple(_to_strided(scratch, c, dil) for c, dil in zip(vjp(t[6]), dils + dils))

    cts = tcall(comb_bwd, (n // tm,), views + [_row(do, tm)],
                [_out((n // dil, dil * d), F32, (tm // dil, dil * d), lambda i: (i, 0)) for dil in dils + dils],
                tag + "_dcomb", scratch=((d // BLOCK * tm, BLOCK), F32))
    dqs, dks, dvs = [], [], []
    for p, ((window, dil), ops_d) in enumerate(zip(C_PATTERNS, ops)):
        dq, dkp, dkc, dvp, dvc = banded_bwd(ops_d, dil, _C_COLS, None, _c_cfg(window, dil),
                                            [(cts[p], lambda r: r), (cts[3 + p], lambda r: r)], f"{tag}_ddil{dil}")
        dqs.append((dq, 0, dil, p))
        dks += [(dkc, 0, dil, 3 + p), (dkp, dil, dil, 3 + p)]
        dvs += [(dvc, 0, dil, 6 + p), (dvp, dil, dil, 6 + p)]

    def gather(*t):
        return jnp.concatenate([t[0] + t[1] + t[2], t[3] + t[4] + t[5], t[6] + t[7] + t[8]], axis=1)

    dqkv, dqg, dkg = prep_bwd(qkv, qg, kg, _OD_SEGS, dqs + dks + dvs, gather, tag + "_dqkv")
    d_win = mm(dqkv, h, "tn", tag + "_dwin")
    dx, dg = mm_norm_bwd(dqkv, w_in, x, g, dy, tag + "_dh", b_kd=True)
    return dx, dg, d_win, dqg, dkg, d_wout


def xa_fwd(x, mem, g, gm, w_q, w_kv, qg, kg, w_o, tag):
    h = rmsnorm_fwd(x, g, tag + "_norm")
    q = mm(h, w_q, "nn", tag + "_q")
    mn = rmsnorm_fwd(mem, gm, tag + "_mnorm")
    kv = mm(mn, w_kv, "nt", tag + "_kv")
    o = xa_core_fwd(q, kv, qg, kg, tag + "_core")
    y = mm(o, w_o, "nn", tag + "_o", res=x)
    return y, (x, h, q, mn, kv, o)


def xa_bwd(dy, saved, mem, g, gm, w_q, w_kv, qg, kg, w_o, tag):
    x, h, q, mn, kv, o = saved
    do = mm(dy, w_o, "nt", tag + "_do", out_dtype=BF16)
    d_wo = mm(o, dy, "tn", tag + "_dwo")
    dq, dkv, dqg, dkg = xa_core_bwd(q, kv, qg, kg, do, tag + "_dcore")
    d_wq = mm(h, dq, "tn", tag + "_dwq")
    dx, dg = mm_norm_bwd(dq, w_q, x, g, dy, tag + "_dh")
    d_wkv = mm(dkv, mn, "tn", tag + "_dwkv")
    _, dgm = mm_norm_bwd(dkv, w_kv, mem, gm, None, tag + "_dmn", b_kd=True)
    return dx, dg, dgm, d_wq, d_wkv, dqg, dkg, d_wo


def loss_head(y, target, name):
    n, d = y.shape
    tm = _tile(n, 512, 8)

    def fn(ids, yt, tt):
        e = yt - tt
        return e * (1.0 / d), jnp.sum(e * e, axis=0, keepdims=True)

    return tcall(fn, (n // tm,), [_row(y, tm), _row(target, tm)], [_row_out(n, d, F32, tm), _acc_out((1, d))], name)


_ANY = pl.BlockSpec(memory_space=pl.ANY)


def all_gather_blocks(blocks):
    nb = len(blocks)

    def body(*refs):
        x_refs, out_refs = refs[:nb], refs[nb:2 * nb]
        send_sems, recv_sems, local_sems = refs[2 * nb:]
        x, y, c = lax.axis_index("x"), lax.axis_index("y"), lax.axis_index("c")
        me, sibling = (x, y, c), (x, y, 1 - c)
        over_x, over_y, diagonal = (1 - x, y), (x, 1 - y), (1 - x, 1 - y)
        relay_of = ((1 - x) * (1 - c) + x * c, y * (1 - c) + (1 - y) * c)
        relay_to = (x * (1 - c) + (1 - x) * c, (1 - y) * (1 - c) + y * c)

        def copy(b, k, blk, to, own=False):
            px, py, pc = blk
            slot = out_refs[b].at[4 * px + 2 * py + pc]
            return pltpu.make_async_remote_copy(
                src_ref=x_refs[b] if own else slot, dst_ref=slot,
                send_sem=send_sems.at[7 * b + k], recv_sem=recv_sems.at[7 * b + k], device_id=to, device_id_type=MESH)

        mine = [pltpu.make_async_copy(x_refs[b], out_refs[b].at[4 * x + 2 * y + c], local_sems.at[b]) for b in range(nb)]
        for cp in mine:
            cp.start()
        sent = []
        for b in range(nb):
            sent += [copy(b, 0, me, sibling, own=True), copy(b, 1, me, (*over_x, c), own=True),
                     copy(b, 2, me, (*over_y, c), own=True)]
        for cp in sent:
            cp.start()
        for b in range(nb):
            copy(b, 1, (*over_x, c), me).wait_recv()
            copy(b, 2, (*over_y, c), me).wait_recv()
            later = [copy(b, 3, (*relay_of, c), (*relay_to, c)), copy(b, 4, (*over_x, c), sibling),
                     copy(b, 5, (*over_y, c), sibling)]
            for cp in later:
                cp.start()
            sent += later
        for b in range(nb):
            copy(b, 3, (*diagonal, c), me).wait_recv()
            fwd = copy(b, 6, (*diagonal, c), sibling)
            fwd.start()
            sent.append(fwd)
        for b in range(nb):
            copy(b, 0, sibling, me).wait_recv()
            for k, chip in ((4, over_x), (5, over_y), (6, diagonal)):
                copy(b, k, (*chip, 1 - c), me).wait_recv()
        for cp in sent:
            cp.wait_send()
        for cp in mine:
            cp.wait()

    return _pcall(
        body, name="weights_all_gather",
        in_specs=[_ANY] * nb, out_specs=[_ANY] * nb,
        out_shape=[jax.ShapeDtypeStruct((N_DEV,) + a.shape, a.dtype) for a in blocks],
        scratch_shapes=[pltpu.SemaphoreType.DMA((7 * nb,)), pltpu.SemaphoreType.DMA((7 * nb,)),
                        pltpu.SemaphoreType.DMA((nb,))],
    )(*blocks)


def pair_exchange(bufs):
    nb = len(bufs)

    def body(*refs):
        srcs, dsts = refs[:nb], refs[nb:2 * nb]
        send_sems, recv_sems = refs[2 * nb:]
        x, y, c = lax.axis_index("x"), lax.axis_index("y"), lax.axis_index("c")
        copies = []
        for b in range(nb):
            for j in range(4):
                cp = pltpu.make_async_remote_copy(
                    src_ref=srcs[b].at[2 * j + (1 - c)], dst_ref=dsts[b].at[j], send_sem=send_sems.at[4 * b + j],
                    recv_sem=recv_sems.at[4 * b + j], device_id=(x, y, 1 - c), device_id_type=MESH)
                cp.start()
                copies.append(cp)
        for cp in copies:
            cp.wait()

    return _pcall(
        body, name="grads_pair_exchange",
        in_specs=[_ANY] * nb, out_specs=[_ANY] * nb,
        out_shape=[jax.ShapeDtypeStruct((4,) + a.shape[1:], a.dtype) for a in bufs],
        scratch_shapes=[pltpu.SemaphoreType.DMA((4 * nb,)), pltpu.SemaphoreType.DMA((4 * nb,))],
    )(*bufs)


def pair_sum(g, got, c, out_dtype, name):
    r, w = g.shape[1:]
    tr = _tile(r, 512, 16)

    def body(c_ref, a_ref, b_ref, o_ref):
        o_ref[...] = (a_ref[...].astype(F32) + b_ref[...].astype(F32)).astype(o_ref.dtype)

    return _pcall(
        body, name=name,
        grid_spec=pltpu.PrefetchScalarGridSpec(
            num_scalar_prefetch=1, grid=(4, r // tr),
            in_specs=[pl.BlockSpec((None, tr, w), lambda j, i, c_ref: (2 * j + c_ref[0], i, 0)),
                      pl.BlockSpec((None, tr, w), lambda j, i, c_ref: (j, i, 0))],
            out_specs=pl.BlockSpec((None, tr, w), lambda j, i, c_ref: (j, i, 0))),
        out_shape=jax.ShapeDtypeStruct((4,) + g.shape[1:], out_dtype),
        compiler_params=_params(),
    )(c, g, got)


def chip_exchange(parts):
    nb = len(parts)

    def body(*refs):
        srcs, dsts = refs[:nb], refs[nb:2 * nb]
        send_sems, recv_sems, local_sems = refs[2 * nb:]
        x, y, c = lax.axis_index("x"), lax.axis_index("y"), lax.axis_index("c")
        my_chip = 2 * x + y
        copies = []
        for b in range(nb):
            mine = pltpu.make_async_copy(srcs[b].at[my_chip], dsts[b].at[my_chip], local_sems.at[b])
            mine.start()
            copies.append(mine)
            for k, (tx, ty) in enumerate([(1 - x, y), (x, 1 - y), (1 - x, 1 - y)]):
                cp = pltpu.make_async_remote_copy(
                    src_ref=srcs[b].at[2 * tx + ty], dst_ref=dsts[b].at[my_chip], send_sem=send_sems.at[3 * b + k],
                    recv_sem=recv_sems.at[3 * b + k], device_id=(tx, ty, c), device_id_type=MESH)
                cp.start()
                copies.append(cp)
        for cp in copies:
            cp.wait()

    return _pcall(
        body, name="grads_chip_exchange",
        in_specs=[_ANY] * nb, out_specs=[_ANY] * nb,
        out_shape=[jax.ShapeDtypeStruct(a.shape, a.dtype) for a in parts],
        scratch_shapes=[pltpu.SemaphoreType.DMA((3 * nb,)), pltpu.SemaphoreType.DMA((3 * nb,)),
                        pltpu.SemaphoreType.DMA((nb,))],
    )(*parts)


def chip_sum(parts, name):
    r, w = parts.shape[1:]
    tr = _tile(r, 512, 16)
    spec = lambda j: _in(parts, (None, tr, w), lambda i, j=j: (j, i, 0))

    def fn(ids, a, b, c_, d):
        a, b, c_, d = [t.astype(F32) for t in (a, b, c_, d)]
        return (((a + b) + c_) + d,)

    (out,) = tcall(fn, (r // tr,), [spec(j) for j in range(4)],
                   [_out((r, w), F32, (tr, w), lambda i: (i, 0))], name)
    return out


def _remote(src, dst, send_sems, recv_sems, k, to):
    return functools.partial(pltpu.make_async_remote_copy, src_ref=src, dst_ref=dst, send_sem=send_sems.at[k],
                             recv_sem=recv_sems.at[k], device_id=to, device_id_type=MESH)


def _gather_plan(phase, nb):
    def plan(ins, outs, send_sems, recv_sems, local_sems):
        x, y, c = lax.axis_index("x"), lax.axis_index("y"), lax.axis_index("c")
        me, sibling = (x, y, c), (x, y, 1 - c)
        over_x, over_y, diagonal = (1 - x, y), (x, 1 - y), (1 - x, 1 - y)
        relay_of = ((1 - x) * (1 - c) + x * c, y * (1 - c) + (1 - y) * c)
        relay_to = (x * (1 - c) + (1 - x) * c, (1 - y) * (1 - c) + y * c)
        local, sends, recvs = [], [], []
        for b in range(nb):
            slot = lambda chip, core, b=b: outs[b].at[4 * chip[0] + 2 * chip[1] + core]
            if phase == 0:
                local.append(functools.partial(pltpu.make_async_copy, ins[b], slot((x, y), c), local_sems.at[b]))
                moves = [(ins[b], slot((x, y), c), to) for to in (sibling, (*over_x, c), (*over_y, c))]
                arrive = [slot((x, y), 1 - c), slot(over_x, c), slot(over_y, c)]
            elif phase == 1:
                moves = [(slot(relay_of, c), slot(relay_of, c), (*relay_to, c)),
                         (slot(over_x, c), slot(over_x, c), sibling), (slot(over_y, c), slot(over_y, c), sibling)]
                arrive = [slot(diagonal, c), slot(over_x, 1 - c), slot(over_y, 1 - c)]
            else:
                moves = [(slot(diagonal, c), slot(diagonal, c), sibling)]
                arrive = [slot(diagonal, 1 - c)]
            sends += [_remote(src, dst, send_sems, recv_sems, 3 * b + k, to) for k, (src, dst, to) in enumerate(moves)]
            recvs += [_remote(dst, dst, send_sems, recv_sems, 3 * b + k, me) for k, dst in enumerate(arrive)]
        return local, sends, recvs
    return plan


def gather_side(phase, arrays):
    nb = len(arrays)
    if phase == 0:
        shapes = [jax.ShapeDtypeStruct((N_DEV,) + a.shape, a.dtype) for a in arrays]
        return Side(arrays, shapes, 3 * nb, nb, _gather_plan(0, nb))
    shapes = [jax.ShapeDtypeStruct(a.shape, a.dtype) for a in arrays]
    return Side(arrays, shapes, 3 * nb, 0, _gather_plan(phase, nb), aliased=True)


def pair_side(bufs):
    nb = len(bufs)

    def plan(ins, outs, send_sems, recv_sems, local_sems):
        x, y, c = lax.axis_index("x"), lax.axis_index("y"), lax.axis_index("c")
        sends = [_remote(ins[b].at[2 * j + (1 - c)], outs[b].at[j], send_sems, recv_sems, 4 * b + j, (x, y, 1 - c))
                 for b in range(nb) for j in range(4)]
        recvs = [_remote(outs[b].at[j], outs[b].at[j], send_sems, recv_sems, 4 * b + j, (x, y, c))
                 for b in range(nb) for j in range(4)]
        return [], sends, recvs

    shapes = [jax.ShapeDtypeStruct((4,) + a.shape[1:], a.dtype) for a in bufs]
    return Side(bufs, shapes, 4 * nb, 0, plan)


def chip_side(parts):
    nb = len(parts)

    def plan(ins, outs, send_sems, recv_sems, local_sems):
        x, y, c = lax.axis_index("x"), lax.axis_index("y"), lax.axis_index("c")
        my_chip = 2 * x + y
        peers = [(1 - x, y), (x, 1 - y), (1 - x, 1 - y)]
        local = [functools.partial(pltpu.make_async_copy, ins[b].at[my_chip], outs[b].at[my_chip], local_sems.at[b])
                 for b in range(nb)]
        sends = [_remote(ins[b].at[2 * tx + ty], outs[b].at[my_chip], send_sems, recv_sems, 3 * b + k, (tx, ty, c))
                 for b in range(nb) for k, (tx, ty) in enumerate(peers)]
        recvs = [_remote(outs[b].at[2 * tx + ty], outs[b].at[2 * tx + ty], send_sems, recv_sems, 3 * b + k, (x, y, c))
                 for b in range(nb) for k, (tx, ty) in enumerate(peers)]
        return local, sends, recvs

    shapes = [jax.ShapeDtypeStruct(a.shape, a.dtype) for a in parts]
    return Side(parts, shapes, 3 * nb, nb, plan)


def adamw(w, g, m, v, name):
    shape = w.shape
    cols = shape[-1]
    rows = int(np.prod(shape[:-1]))
    w2, g2, m2, v2 = [a.reshape(rows, cols) for a in (w, g, m, v)]
    tr = _tile(rows, 256, 8) if rows % 8 == 0 else rows

    def fn(ids, wt, gt, mt, vt):
        m_new = ADAM_B1 * mt + (1.0 - ADAM_B1) * gt
        v_new = ADAM_B2 * vt + (1.0 - ADAM_B2) * (gt * gt)
        m_hat = m_new / (1.0 - ADAM_B1 ** ADAM_STEP)
        v_hat = v_new / (1.0 - ADAM_B2 ** ADAM_STEP)
        delta = -ADAM_LR * (m_hat / (jnp.sqrt(v_hat) + ADAM_EPS) + ADAM_WD * wt)
        return delta, m_new, v_new

    res = tcall(fn, (rows // tr,), [_row(a, tr) for a in (w2, g2, m2, v2)],
                [_row_out(rows, cols, F32, tr) for _ in range(3)], name)
    return [a.reshape(shape) for a in res]


_MATS = [("ffn1_w_gu", "col"), ("ffn1_w_down", "row"), ("ev_w_in", "col"), ("ev_w_out", "row"),
         ("od_w_in", "col"), ("od_w_out", "row"), ("xa_w_q", "row"), ("xa_w_kv", "col"), ("xa_w_o", "row"),
         ("ffn2_w_gu", "col"), ("ffn2_w_down", "row")]
_VECS = ["ffn1_norm", "mix_norm", "ev_q_gain", "ev_k_gain", "ev_sinks", "od_q_gain", "od_k_gain", "xa_norm",
         "xa_mem_norm", "xa_q_gain", "xa_k_gain", "ffn2_norm"]
_WEIGHTS = ["ffn1_norm", "ffn1_w_gu", "ffn1_w_down", "mix_norm", "ev_w_in", "ev_q_gain", "ev_k_gain", "ev_sinks",
            "ev_w_out", "od_w_in", "od_q_gain", "od_k_gain", "od_w_out", "xa_norm", "xa_mem_norm", "xa_w_q", "xa_w_kv",
            "xa_q_gain", "xa_k_gain", "xa_w_o", "ffn2_norm", "ffn2_w_gu", "ffn2_w_down"]


_AXIS = dict(_MATS)
DEPTH = 2


def _layer_groups(l):
    first, rest = _first_block_groups(l)
    return [first[0] + rest[0] + rest[1]]


def _first_block_groups(l):
    w_in, w_out = ("ev_w_in", "ev_w_out") if l % 2 == 0 else ("od_w_in", "od_w_out")
    first = [[("ffn1_w_gu", l), ("ffn1_w_down", l)]]
    rest = [[("ffn2_w_gu", l), ("xa_w_kv", l)],
            [(w_in, l // 2), ("ffn2_w_down", l), (w_out, l // 2), ("xa_w_q", l), ("xa_w_o", l)]]
    return first, rest


def _block_rows(shards, n):
    a, b = shards[n].shape[1:]
    return a if _AXIS[n] == "row" else b


def _weight_blocks(shards, groups):
    blocks = []
    for group in groups:
        rows = [(shards[n][j] if _AXIS[n] == "row" else shards[n][j].T).astype(BF16) for n, j in group]
        blocks.append(rows[0] if len(rows) == 1 else jnp.concatenate(rows, axis=0))
    return blocks


def _whole_weights(shards, groups, gathered):
    full = {}
    for group, got in zip(groups, gathered):
        off = 0
        for n, j in group:
            r = _block_rows(shards, n)
            full[n] = got[:, off:off + r, :].reshape(N_DEV * r, got.shape[2])
            off += r
    return full


def _gradient_buffers(grads, groups):
    bufs = []
    for group in groups:
        rows = []
        for n, _ in group:
            whole = jnp.concatenate(grads[n], axis=0) if isinstance(grads[n], tuple) else grads[n]
            rows.append(whole.reshape(N_DEV, whole.shape[0] // N_DEV, whole.shape[1]))
        bufs.append((rows[0] if len(rows) == 1 else jnp.concatenate(rows, axis=1)).astype(BF16))
    return bufs


def _gradient_blocks(shards, groups, sums):
    out = {}
    for group, tot in zip(groups, sums):
        off = 0
        for n, j in group:
            r = _block_rows(shards, n)
            out[n, j] = tot[off:off + r] if _AXIS[n] == "row" else tot[off:off + r].T
            off += r
    return out


class _PairChain:
    def __init__(self, ex, bufs):
        self.ex, self.bufs, self.parts = ex, bufs, None

    def side(self, name):
        return pair_side(self.bufs) if name == "dwd" else None

    def done(self, name, carried):
        self.parts = self.ex.pair_sums(self.bufs, carried, "l1")


class _RestChain:
    HALF = {"dwd": (0,), "dwgu": (1,)}

    def __init__(self, ex, bufs):
        self.ex, self.bufs, self.parts, self.sums = ex, bufs, None, [None] * len(bufs)

    def side(self, name):
        if name == "da":
            return pair_side(self.bufs)
        return chip_side([self.parts[i] for i in self.HALF[name]])

    def done(self, name, carried):
        if name == "da":
            self.parts = self.ex.pair_sums(self.bufs, carried, "l0r")
        else:
            for i, tot in zip(self.HALF[name], self.ex.chip_sums(carried, "l0r_" + name)):
                self.sums[i] = tot


class _Exchange:
    def __init__(self, shards, c):
        self.shards, self.c = shards, c

    def weights_first(self):
        first, _ = _first_block_groups(0)
        return _whole_weights(self.shards, first, all_gather_blocks(_weight_blocks(self.shards, first)))

    def rest_blocks(self):
        return _weight_blocks(self.shards, _first_block_groups(0)[1])

    def weights_rest(self, gathered):
        return _whole_weights(self.shards, _first_block_groups(0)[1], gathered)

    def gather_start(self):
        return gather_side(0, _weight_blocks(self.shards, _layer_groups(1)))

    def weights_next(self, gathered):
        return _whole_weights(self.shards, _layer_groups(1), gathered)

    def chain_next(self, grads):
        return _PairChain(self, _gradient_buffers(grads, _layer_groups(1)))

    def chain_rest(self, grads):
        return _RestChain(self, _gradient_buffers(grads, _first_block_groups(0)[1]))

    def pair_sums(self, bufs, got, tag):
        return [pair_sum(b, g, self.c, b.dtype, f"grads_pair_sum_{tag}_{i}") for i, (b, g) in enumerate(zip(bufs, got))]

    def chip_sums(self, parts, tag):
        return [chip_sum(p, f"grads_chip_sum_{tag}_{i}") for i, p in enumerate(parts)]

    def finish(self, gm, gv, sums1, sums_rest):
        vecs = {n: jnp.concatenate(v, axis=0) for n, v in gv.items()}
        first, rest = _first_block_groups(0)
        bufs = _gradient_buffers(gm[0], first)
        vec = jnp.concatenate([vecs[n].reshape(-1) for n in _VECS])
        vec = jnp.pad(vec, (0, -vec.shape[0] % (16 * LANES)))
        bufs.append(jnp.broadcast_to(vec.reshape(1, -1, LANES), (N_DEV, vec.shape[0] // LANES, LANES)))
        parts = self.pair_sums(bufs, pair_exchange(bufs), "l0")
        sums0 = self.chip_sums(chip_exchange(parts), "l0")
        blocks = {**_gradient_blocks(self.shards, first, sums0[:-1]), **_gradient_blocks(self.shards, rest, sums_rest),
                  **_gradient_blocks(self.shards, _layer_groups(1), sums1)}
        out = {n: jnp.stack([blocks[n, j] for j in range(self.shards[n].shape[0])]) for n, _ in _MATS}
        flat, off = sums0[-1].reshape(-1), 0
        for n in _VECS:
            out[n] = flat[off:off + vecs[n].size].reshape(vecs[n].shape)
            off += vecs[n].size
        return out


class _NoExchange:
    def __init__(self, full):
        self.full = full

    def weights_first(self):
        return self.full[0]

    def rest_blocks(self):
        return None

    def gather_start(self):
        return None

    def weights_next(self, gathered):
        return self.full[1]

    def chain_next(self, grads):
        return None

    def chain_rest(self, grads):
        return None

    def finish(self, gm, gv, sums1, sums_rest):
        mats = {}
        for l in range(DEPTH):
            for group in _layer_groups(l):
                for n, j in group:
                    whole = jnp.concatenate(gm[l][n], axis=0) if isinstance(gm[l][n], tuple) else gm[l][n]
                    mats.setdefault(n, {})[j] = whole if _AXIS[n] == "row" else whole.T
        mats = {n: jnp.stack([v[j] for j in sorted(v)]) for n, v in mats.items()}
        return mats, {n: jnp.concatenate(v, axis=0) for n, v in gv.items()}


def _local_step(x, mem, target, w, ex):
    assert w["ffn1_norm"].shape[0] == DEPTH
    row = lambda a, l: a[l:l + 1]
    full = [ex.weights_first(), None]
    saved = []
    for l in range(DEPTH):
        t, j, f = f"l{l}", l // 2, full[l]
        rest = ex.rest_blocks() if l == 0 else None
        if rest is None:
            x, s1 = ffn_fwd(x, row(w["ffn1_norm"], l), f["ffn1_w_gu"], f["ffn1_w_down"], t + "_ffn1")
        else:
            x, s1, rest = ffn_fwd(x, row(w["ffn1_norm"], l), f["ffn1_w_gu"], f["ffn1_w_down"], t + "_ffn1", (0, rest))
        relay = None
        if l % 2 == 0:
            h = rmsnorm_fwd(x, row(w["mix_norm"], l), t + "_ev_norm", None if rest is None else gather_side(2, rest))
            if rest is not None:
                h, rest = h
                f = full[l] = {**f, **ex.weights_rest(rest)}
            side = ex.gather_start() if l + 1 < DEPTH else None
            x, s2, relay = even_mixer_fwd(x, h, _ev_reorder(f["ev_w_in"]), row(w["ev_q_gain"], j),
                                          row(w["ev_k_gain"], j), row(w["ev_sinks"], j), f["ev_w_out"], t + "_ev", side)
        else:
            x, s2 = odd_mixer_fwd(x, row(w["mix_norm"], l), f["od_w_in"], row(w["od_q_gain"], j),
                                  row(w["od_k_gain"], j), f["od_w_out"], t + "_od")
        x, s3 = xa_fwd(x, mem, row(w["xa_norm"], l), row(w["xa_mem_norm"], l), f["xa_w_q"], f["xa_w_kv"],
                       row(w["xa_q_gain"], l), row(w["xa_k_gain"], l), f["xa_w_o"], t + "_xa")
        if relay is None:
            x, s4 = ffn_fwd(x, row(w["ffn2_norm"], l), f["ffn2_w_gu"], f["ffn2_w_down"], t + "_ffn2")
        else:
            x, s4, relay = ffn_fwd(x, row(w["ffn2_norm"], l), f["ffn2_w_gu"], f["ffn2_w_down"], t + "_ffn2", (1, relay))
        if l + 1 < DEPTH:
            full[l + 1] = ex.weights_next(relay)
        saved.append((s1, s2, s3, s4))
    dx, sq = loss_head(x, target, "loss_head")
    loss = 0.5 * jnp.sum(sq) / x.shape[1]

    gm = [dict() for _ in range(DEPTH)]
    gv = {n: [None] * w[n].shape[0] for n in _VECS}
    chain1 = chain0 = sums1 = None
    for l in reversed(range(DEPTH)):
        t, j, f = f"l{l}", l // 2, full[l]
        s1, s2, s3, s4 = saved[l]
        dx, gv["ffn2_norm"][l], gm[l]["ffn2_w_gu"], gm[l]["ffn2_w_down"] = ffn_bwd(
            dx, s4, row(w["ffn2_norm"], l), f["ffn2_w_gu"], f["ffn2_w_down"], t + "_ffn2", chain1 if l == 0 else None)
        parts = chain1.parts if l == 0 and chain1 is not None else None
        (dx, gv["xa_norm"][l], gv["xa_mem_norm"][l], gm[l]["xa_w_q"], gm[l]["xa_w_kv"], gv["xa_q_gain"][l],
         gv["xa_k_gain"][l], gm[l]["xa_w_o"]) = xa_bwd(
            dx, s3, mem, row(w["xa_norm"], l), row(w["xa_mem_norm"], l), f["xa_w_q"], f["xa_w_kv"],
            row(w["xa_q_gain"], l), row(w["xa_k_gain"], l), f["xa_w_o"], t + "_xa")
        if l % 2 == 0:
            (dx, gv["mix_norm"][l], d_win, gv["ev_q_gain"][j], gv["ev_k_gain"][j], gv["ev_sinks"][j],
             gm[l]["ev_w_out"], carried) = even_mixer_bwd(
                dx, s2, row(w["mix_norm"], l), _ev_reorder(f["ev_w_in"]), row(w["ev_q_gain"], j), row(w["ev_k_gain"], j),
                row(w["ev_sinks"], j), f["ev_w_out"], t + "_ev", None if parts is None else chip_side(parts))
            gm[l]["ev_w_in"] = _ev_restore(d_win)
            if carried is not None:
                sums1 = ex.chip_sums(carried, "l1")
        else:
            (dx, gv["mix_norm"][l], gm[l]["od_w_in"], gv["od_q_gain"][j], gv["od_k_gain"][j],
             gm[l]["od_w_out"]) = odd_mixer_bwd(
                dx, s2, row(w["mix_norm"], l), f["od_w_in"], row(w["od_q_gain"], j), row(w["od_k_gain"], j),
                f["od_w_out"], t + "_od")
        if l == 0:
            chain0 = ex.chain_rest(gm[l])
        dx, gv["ffn1_norm"][l], gm[l]["ffn1_w_gu"], gm[l]["ffn1_w_down"] = ffn_bwd(
            dx, s1, row(w["ffn1_norm"], l), f["ffn1_w_gu"], f["ffn1_w_down"], t + "_ffn1", chain0 if l == 0 else None)
        if l == 1:
            chain1 = ex.chain_next(gm[l])
    return loss, dx, ex.finish(gm, gv, sums1, None if chain0 is None else chain0.sums)


def kernel(x, mem, ffn1_norm, ffn1_w_gu, ffn1_w_down, mix_norm, ev_w_in, ev_q_gain, ev_k_gain, ev_sinks, ev_w_out, od_w_in, od_q_gain, od_k_gain, od_w_out, xa_norm, xa_mem_norm, xa_w_q, xa_w_kv, xa_q_gain, xa_k_gain, xa_w_o, ffn2_norm, ffn2_w_gu, ffn2_w_down, loss_target, m_ffn1_norm, m_ffn1_w_gu, m_ffn1_w_down, m_mix_norm, m_ev_w_in, m_ev_q_gain, m_ev_k_gain, m_ev_sinks, m_ev_w_out, m_od_w_in, m_od_q_gain, m_od_k_gain, m_od_w_out, m_xa_norm, m_xa_mem_norm, m_xa_w_q, m_xa_w_kv, m_xa_q_gain, m_xa_k_gain, m_xa_w_o, m_ffn2_norm, m_ffn2_w_gu, m_ffn2_w_down, v_ffn1_norm, v_ffn1_w_gu, v_ffn1_w_down, v_mix_norm, v_ev_w_in, v_ev_q_gain, v_ev_k_gain, v_ev_sinks, v_ev_w_out, v_od_w_in, v_od_q_gain, v_od_k_gain, v_od_w_out, v_xa_norm, v_xa_mem_norm, v_xa_w_q, v_xa_w_kv, v_xa_q_gain, v_xa_k_gain, v_xa_w_o, v_ffn2_norm, v_ffn2_w_gu, v_ffn2_w_down):
    w = dict(ffn1_norm=ffn1_norm, ffn1_w_gu=ffn1_w_gu, ffn1_w_down=ffn1_w_down, mix_norm=mix_norm, ev_w_in=ev_w_in, ev_q_gain=ev_q_gain, ev_k_gain=ev_k_gain, ev_sinks=ev_sinks, ev_w_out=ev_w_out, od_w_in=od_w_in, od_q_gain=od_q_gain, od_k_gain=od_k_gain, od_w_out=od_w_out, xa_norm=xa_norm, xa_mem_norm=xa_mem_norm, xa_w_q=xa_w_q, xa_w_kv=xa_w_kv, xa_q_gain=xa_q_gain, xa_k_gain=xa_k_gain, xa_w_o=xa_w_o, ffn2_norm=ffn2_norm, ffn2_w_gu=ffn2_w_gu, ffn2_w_down=ffn2_w_down)
    m = dict(ffn1_norm=m_ffn1_norm, ffn1_w_gu=m_ffn1_w_gu, ffn1_w_down=m_ffn1_w_down, mix_norm=m_mix_norm, ev_w_in=m_ev_w_in, ev_q_gain=m_ev_q_gain, ev_k_gain=m_ev_k_gain, ev_sinks=m_ev_sinks, ev_w_out=m_ev_w_out, od_w_in=m_od_w_in, od_q_gain=m_od_q_gain, od_k_gain=m_od_k_gain, od_w_out=m_od_w_out, xa_norm=m_xa_norm, xa_mem_norm=m_xa_mem_norm, xa_w_q=m_xa_w_q, xa_w_kv=m_xa_w_kv, xa_q_gain=m_xa_q_gain, xa_k_gain=m_xa_k_gain, xa_w_o=m_xa_w_o, ffn2_norm=m_ffn2_norm, ffn2_w_gu=m_ffn2_w_gu, ffn2_w_down=m_ffn2_w_down)
    v = dict(ffn1_norm=v_ffn1_norm, ffn1_w_gu=v_ffn1_w_gu, ffn1_w_down=v_ffn1_w_down, mix_norm=v_mix_norm, ev_w_in=v_ev_w_in, ev_q_gain=v_ev_q_gain, ev_k_gain=v_ev_k_gain, ev_sinks=v_ev_sinks, ev_w_out=v_ev_w_out, od_w_in=v_od_w_in, od_q_gain=v_od_q_gain, od_k_gain=v_od_k_gain, od_w_out=v_od_w_out, xa_norm=v_xa_norm, xa_mem_norm=v_xa_mem_norm, xa_w_q=v_xa_w_q, xa_w_kv=v_xa_w_kv, xa_q_gain=v_xa_q_gain, xa_k_gain=v_xa_k_gain, xa_w_o=v_xa_w_o, ffn2_norm=v_ffn2_norm, ffn2_w_gu=v_ffn2_w_gu, ffn2_w_down=v_ffn2_w_down)

    c = lax.axis_index("c").astype(jnp.int32).reshape(1)
    loss, dx, grads = _local_step(x[0], mem[0], loss_target[0], w, _Exchange(w, c))
    loss = lax.psum(loss, ("x", "y", "c"))

    delta, new_m, new_v = {}, {}, {}
    for n in _WEIGHTS:
        delta[n], new_m[n], new_v[n] = adamw(w[n], grads[n], m[n], v[n], "adamw_" + n)
    return (loss, dx[None], *[grads[n] for n in _WEIGHTS], *[delta[n] for n in _WEIGHTS],
            *[new_m[n] for n in _WEIGHTS], *[new_v[n] for n in _WEIGHTS])
```

```python
import functools

import numpy as np
import jax
import jax.numpy as jnp
from jax import lax
from jax.experimental import pallas as pl
from jax.experimental.pallas import tpu as pltpu

F32 = jnp.float32
BF16 = jnp.bfloat16
MESH = pl.DeviceIdType.MESH

HEAD_DIM = 64
BLOCK = 128
RMS_EPS = 1e-6
A_Q_HEADS, A_KV_HEADS = 8, 2
B_HEADS = 8
C_HEADS = 16
C_PATTERNS = ((128, 1), (512, 4), (2048, 16))
X_HEADS = 4
N_DEV = 8
LANES = 1024
VMEM_LIMIT_BYTES = 56 * 1024 * 1024
SB_SKIP_LOG = -110.0
NEG_BIG = -1e30

ADAM_LR, ADAM_B1, ADAM_B2, ADAM_EPS, ADAM_WD, ADAM_STEP = 0.001, 0.9, 0.999, 1e-08, 0.01, 10

NN = (((1,), (0,)), ((), ()))
NT = (((1,), (1,)), ((), ()))
TN = (((0,), (0,)), ((), ()))


class Side:
    def __init__(self, arrays, out_shapes, n_remote, n_local, plan, aliased=False):
        self.arrays, self.out_shapes, self.plan, self.aliased = list(arrays), list(out_shapes), plan, aliased
        self.sems = [pltpu.SemaphoreType.DMA((n_remote,)), pltpu.SemaphoreType.DMA((n_remote,)),
                     pltpu.SemaphoreType.DMA((max(n_local, 1),))]

    def start(self, ins, outs, sems):
        local, sends, _ = self.plan(ins, outs, *sems)
        for make in local + sends:
            make().start()

    def wait(self, ins, outs, sems):
        local, sends, recvs = self.plan(ins, outs, *sems)
        for make in sends:
            make().wait_send()
        for make in recvs:
            make().wait_recv()
        for make in local:
            make().wait()


def _pcall(body, side=None, **kw):
    if side is None:
        return pl.pallas_call(body, **kw)
    grid = kw["grid"]
    single = not isinstance(kw["out_specs"], (list, tuple))
    out_specs = [kw["out_specs"]] if single else list(kw["out_specs"])
    out_shape = [kw["out_shape"]] if single else list(kw["out_shape"])
    scratch = list(kw.get("scratch_shapes", []))
    n_in, n_out, n_scr, n_side = len(kw["in_specs"]), len(out_specs), len(scratch), len(side.arrays)
    n_sout = len(side.out_shapes)

    def hosted(*refs):
        ins, s_in = refs[:n_in], refs[n_in:n_in + n_side]
        outs = refs[n_in + n_side:n_in + n_side + n_out]
        s_out = refs[n_in + n_side + n_out:n_in + n_side + n_out + n_sout]
        rest = refs[n_in + n_side + n_out + n_sout:]
        scr, sems = rest[:n_scr], rest[n_scr:]
        first = last = None
        for a, size in enumerate(grid):
            f, l = pl.program_id(a) == 0, pl.program_id(a) == size - 1
            first = f if first is None else jnp.logical_and(first, f)
            last = l if last is None else jnp.logical_and(last, l)

        @pl.when(first)
        def _():
            side.start(s_in, s_out, sems)

        body(*ins, *outs, *scr)

        @pl.when(last)
        def _():
            side.wait(s_in, s_out, sems)

    any_space = pl.BlockSpec(memory_space=pl.ANY)
    kw2 = dict(kw)
    kw2.update(in_specs=list(kw["in_specs"]) + [any_space] * n_side, out_specs=out_specs + [any_space] * n_sout,
               out_shape=out_shape + side.out_shapes, scratch_shapes=scratch + side.sems)
    if side.aliased:
        kw2["input_output_aliases"] = {n_in + i: n_out + i for i in range(n_side)}
    call = pl.pallas_call(hosted, **kw2)

    def run(*args):
        res = call(*args, *side.arrays)
        return (res[0] if single else list(res[:n_out])), list(res[n_out:])

    return run


def _params(**kw):
    return pltpu.CompilerParams(vmem_limit_bytes=VMEM_LIMIT_BYTES, **kw)


def _tile(dim, cap, unit=128):
    if dim <= cap:
        return dim
    t = (cap // unit) * unit
    while t >= unit:
        if dim % t == 0:
            return t
        t -= unit
    raise ValueError(f"no tile for {dim} under {cap}")


def _dot(a, b, dims):
    return lax.dot_general(a.astype(BF16), b.astype(BF16), dims, preferred_element_type=F32)


@functools.partial(jax.custom_vjp, nondiff_argnums=(2,))
def _dot_vjp(a, b, nt):
    return _dot(a, b, NT if nt else NN)


def _dot_vjp_fwd(a, b, nt):
    return _dot(a, b, NT if nt else NN), (a.astype(BF16), b.astype(BF16))


def _dot_vjp_bwd(nt, res, g):
    a, b = res
    if nt:
        return _dot(g, b, NN), _dot(g, a, TN)
    return _dot(g, b, NT), _dot(a, g, TN)


_dot_vjp.defvjp(_dot_vjp_fwd, _dot_vjp_bwd)


def _plain_dot(a, b, nt):
    return _dot(a, b, NT if nt else NN)


def _split_dot(x, mat, terms=2):
    out, rem = None, x
    for t in range(terms):
        part = rem.astype(BF16)
        d = lax.dot_general(part, mat, NN, preferred_element_type=F32)
        out = d if out is None else out + d
        if t + 1 < terms:
            rem = rem - part.astype(F32)
    return out


@functools.partial(jax.custom_vjp, nondiff_argnums=(3,))
def _split_dot_vjp(x, mat, mat_t, terms):
    return _split_dot(x, mat, terms)


def _split_dot_vjp_fwd(x, mat, mat_t, terms):
    return _split_dot(x, mat, terms), mat_t


def _split_dot_vjp_bwd(terms, mat_t, g):
    return _split_dot(g, mat_t, terms), None, None


_split_dot_vjp.defvjp(_split_dot_vjp_fwd, _split_dot_vjp_bwd)


def _plain_split(x, mat, mat_t, terms):
    return _split_dot(x, mat, terms)


def _tri(after):
    j = lax.broadcasted_iota(jnp.int32, (BLOCK, BLOCK), 0)
    s = lax.broadcasted_iota(jnp.int32, (BLOCK, BLOCK), 1)
    return jnp.where(j > s if after else j < s, 1.0, 0.0).astype(BF16)


def _in(a, block, imap):
    return (a, block, imap)


def _out(shape, dtype, block, imap, acc=False):
    return (shape, dtype, block, imap, acc)


def tcall(fn, grid, ins, outs, name, scratch=None, side=None):
    nin = len(ins)
    nout = len(outs)
    ngrid = len(grid)

    def body(*refs):
        ids = tuple(pl.program_id(a) for a in range(ngrid))
        extra = {} if scratch is None else {"scratch": refs[nin + nout]}
        res = fn(ids, *[r[...] for r in refs[:nin]], **extra)
        first = ids[0] == 0
        for a in range(1, ngrid):
            first = jnp.logical_and(first, ids[a] == 0)
        for o_ref, r, spec in zip(refs[nin:nin + nout], res, outs):
            if spec[4]:
                @pl.when(first)
                def _(o_ref=o_ref):
                    o_ref[...] = jnp.zeros(o_ref.shape, o_ref.dtype)
                o_ref[...] += r.astype(o_ref.dtype)
            else:
                o_ref[...] = r.astype(o_ref.dtype)

    return _pcall(
        body, side=side, name=name, grid=grid,
        in_specs=[pl.BlockSpec(b, m) for (_, b, m) in ins],
        out_specs=[pl.BlockSpec(b, m) for (_, _, b, m, _) in outs],
        out_shape=[jax.ShapeDtypeStruct(s, d) for (s, d, _, _, _) in outs],
        scratch_shapes=[] if scratch is None else [pltpu.VMEM(*scratch)],
        compiler_params=_params(),
    )(*[a for (a, _, _) in ins])


def _to_strided(scr, nat, d):
    if d == 1:
        return nat
    t, w = nat.shape
    nc = w // BLOCK
    for c in range(nc):
        scr[c * t:(c + 1) * t, :] = nat[:, c * BLOCK:(c + 1) * BLOCK]
    return jnp.concatenate([scr[pl.ds(c * t + r, t // d, stride=d), :] for r in range(d) for c in range(nc)], axis=1)


def _to_natural(scr, st, d):
    if d == 1:
        return st.astype(F32)
    t, w = st.shape[0] * d, st.shape[1] // d
    nc = w // BLOCK
    st = st.astype(F32)
    for r in range(d):
        for c in range(nc):
            scr[pl.ds(c * t + r, t // d, stride=d), :] = st[:, r * w + c * BLOCK:r * w + (c + 1) * BLOCK]
    return jnp.concatenate([scr[c * t:(c + 1) * t, :] for c in range(nc)], axis=1)


def _row(a, tm, width=None, cb=0):
    width = a.shape[1] if width is None else width
    return _in(a, (tm, width), lambda i, cb=cb: (i, cb))


def _full(a):
    zeros = (0,) * a.ndim
    return _in(a, a.shape, lambda *ids: zeros)


def _row_out(n, width, dtype, tm):
    return _out((n, width), dtype, (tm, width), lambda i: (i, 0))


def _acc_out(shape):
    zeros = (0,) * len(shape)
    return _out(shape, F32, shape, lambda *ids: zeros, acc=True)


def mm(a, b, mode, name, *, out_dtype=F32, scale=1.0, res=None, side=None):
    if mode == "nn":
        (m, k), (k2, n) = a.shape, b.shape
    elif mode == "nt":
        (m, k), (n, k2) = a.shape, b.shape
    else:
        (k, m), (k2, n) = a.shape, b.shape
    assert k == k2, (a.shape, b.shape, mode)
    tm, tn, tk = _tile(m, 1408 if mode == "tn" else 512), _tile(n, 1408), _tile(k, 1408)
    nk = k // tk
    dims = {"nn": NN, "nt": NT, "tn": TN}[mode]
    has_res = res is not None

    def body(*refs):
        if has_res:
            a_ref, b_ref, r_ref, o_ref, acc_ref = refs
        else:
            a_ref, b_ref, o_ref, acc_ref = refs
        kk = pl.program_id(2)

        @pl.when(kk == 0)
        def _():
            acc_ref[...] = jnp.zeros(acc_ref.shape, F32)

        acc_ref[...] += _dot(a_ref[...], b_ref[...], dims)

        @pl.when(kk == nk - 1)
        def _():
            out = acc_ref[...]
            if scale != 1.0:
                out = out * scale
            if has_res:
                out = out + r_ref[...]
            o_ref[...] = out.astype(o_ref.dtype)

    a_spec = (pl.BlockSpec((tk, tm), lambda i, j, kk: (kk, i)) if mode == "tn"
              else pl.BlockSpec((tm, tk), lambda i, j, kk: (i, kk)))
    b_spec = (pl.BlockSpec((tn, tk), lambda i, j, kk: (j, kk)) if mode == "nt"
              else pl.BlockSpec((tk, tn), lambda i, j, kk: (kk, j)))
    in_specs = [a_spec, b_spec]
    args = [a, b]
    if has_res:
        in_specs.append(pl.BlockSpec((tm, tn), lambda i, j, kk: (i, j)))
        args.append(res)
    order = ("parallel", "parallel", "arbitrary") if side is None else ("arbitrary",) * 3
    return _pcall(
        body, side=side, name=name, grid=(m // tm, n // tn, nk),
        in_specs=in_specs,
        out_specs=pl.BlockSpec((tm, tn), lambda i, j, kk: (i, j)),
        out_shape=jax.ShapeDtypeStruct((m, n), out_dtype),
        scratch_shapes=[pltpu.VMEM((tm, tn), F32)],
        compiler_params=_params(dimension_semantics=order),
    )(*args)


def _rms(x, g):
    return x * lax.rsqrt(jnp.mean(x * x, axis=-1, keepdims=True) + RMS_EPS) * g


def _silu_mul(gate, up):
    return gate / (1.0 + jnp.exp(-gate)) * up


def mm_gate_up(h, w_gu, name, side=None):
    m, k = h.shape
    f = w_gu.shape[0] // 2
    tm, tn = _tile(m, 512), _tile(f, 1408)
    nj = f // tn
    assert k <= 1408

    def body(h_ref, wg_ref, wu_ref, g_ref, u_ref, a_ref):
        ht = h_ref[...]
        for lo in range(0, tn, 512):
            cols = slice(lo, min(lo + 512, tn))
            gate, up = _dot(ht, wg_ref[cols, :], NT), _dot(ht, wu_ref[cols, :], NT)
            g_ref[:, cols] = gate
            u_ref[:, cols] = up
            a_ref[:, cols] = _silu_mul(gate, up).astype(a_ref.dtype)

    tile = pl.BlockSpec((tm, tn), lambda i, j: (i, j))
    return _pcall(
        body, side=side, name=name, grid=(m // tm, nj),
        in_specs=[pl.BlockSpec((tm, k), lambda i, j: (i, 0)),
                  pl.BlockSpec((tn, k), lambda i, j: (j, 0)),
                  pl.BlockSpec((tn, k), lambda i, j: (j + nj, 0))],
        out_specs=[tile, tile, tile],
        out_shape=[jax.ShapeDtypeStruct((m, f), F32), jax.ShapeDtypeStruct((m, f), F32),
                   jax.ShapeDtypeStruct((m, f), BF16)],
        compiler_params=_params(dimension_semantics=("arbitrary",) * 2),
    )(h, w_gu, w_gu)


def mm_down_act_bwd(dy, w_down, gate, up, name, side=None):
    m, d = dy.shape
    f = w_down.shape[0]
    tm, tn = _tile(m, 512), _tile(f, 1408)
    assert d <= 1408

    def body(dy_ref, w_ref, g_ref, u_ref, dg_ref, du_ref):
        dyt = dy_ref[...].astype(BF16)
        for lo in range(0, tn, 512):
            cols = slice(lo, min(lo + 512, tn))
            da = _dot(dyt, w_ref[cols, :], NT) * 0.5
            gate, up = g_ref[:, cols], u_ref[:, cols]
            s = 1.0 / (1.0 + jnp.exp(-gate))
            gs = gate * s
            du_ref[:, cols] = (da * gs).astype(du_ref.dtype)
            dg_ref[:, cols] = (da * up * s * (1.0 + gate - gs)).astype(dg_ref.dtype)

    tile = pl.BlockSpec((tm, tn), lambda i, j: (i, j))
    return _pcall(
        body, side=side, name=name, grid=(m // tm, f // tn),
        in_specs=[pl.BlockSpec((tm, d), lambda i, j: (i, 0)), pl.BlockSpec((tn, d), lambda i, j: (j, 0)), tile, tile],
        out_specs=[tile, tile],
        out_shape=[jax.ShapeDtypeStruct((m, f), BF16), jax.ShapeDtypeStruct((m, f), BF16)],
        compiler_params=_params(dimension_semantics=("arbitrary", "arbitrary")),
    )(dy, w_down, gate, up)


def mm_norm_bwd(a, b, x, g, dres, name, b_kd=False):
    halves = isinstance(a, (tuple, list))
    a0, a1 = a if halves else (a, None)
    m, k = a0.shape[0], a0.shape[1] * (2 if halves else 1)
    d = b.shape[1] if b_kd else b.shape[0]
    dims = NN if b_kd else NT
    tm, tk = _tile(m, 512), _tile(a0.shape[1], 1408)
    nk = k // tk
    nkh = a0.shape[1] // tk
    has_res = dres is not None

    def body(*refs):
        a_ref, b_ref, x_ref, g_ref = refs[:4]
        rest = refs[4:-3]
        a1_ref = rest[0] if halves else None
        r_ref = rest[-1] if has_res else None
        dx_ref, dg_ref, acc_ref = refs[-3:]
        i, kk = pl.program_id(0), pl.program_id(1)

        @pl.when(kk == 0)
        def _():
            acc_ref[...] = jnp.zeros(acc_ref.shape, F32)

        if halves:
            @pl.when(kk < nkh)
            def _():
                acc_ref[...] += _dot(a_ref[...], b_ref[...], dims)

            @pl.when(kk >= nkh)
            def _():
                acc_ref[...] += _dot(a1_ref[...], b_ref[...], dims)
        else:
            acc_ref[...] += _dot(a_ref[...], b_ref[...], dims)

        @pl.when(kk == nk - 1)
        def _():
            _, vjp = jax.vjp(_rms, x_ref[...], g_ref[...])
            dx, dg = vjp(acc_ref[...])
            dx_ref[...] = dx + r_ref[...] if has_res else dx

            @pl.when(i == 0)
            def _():
                dg_ref[...] = jnp.zeros(dg_ref.shape, F32)

            dg_ref[...] += dg

    rows = pl.BlockSpec((tm, d), lambda i, kk: (i, 0))
    first = pl.BlockSpec((tm, tk), lambda i, kk: (i, jnp.minimum(kk, nkh - 1)))
    second = pl.BlockSpec((tm, tk), lambda i, kk: (i, jnp.maximum(kk - nkh, 0)))
    b_spec = pl.BlockSpec((tk, d), lambda i, kk: (kk, 0)) if b_kd else pl.BlockSpec((d, tk), lambda i, kk: (0, kk))
    in_specs = ([first, b_spec, rows, pl.BlockSpec(g.shape, lambda i, kk: (0, 0))]
                + ([second] if halves else []) + ([rows] if has_res else []))
    return _pcall(
        body, name=name, grid=(m // tm, nk),
        in_specs=in_specs,
        out_specs=[rows, pl.BlockSpec(g.shape, lambda i, kk: (0, 0))],
        out_shape=[jax.ShapeDtypeStruct((m, d), F32), jax.ShapeDtypeStruct(g.shape, F32)],
        scratch_shapes=[pltpu.VMEM((tm, d), F32)],
        compiler_params=_params(dimension_semantics=("arbitrary", "arbitrary")),
    )(*([a0, b, x, g] + ([a1] if halves else []) + ([dres] if has_res else [])))


def _indicator(shape, head_axis, mod):
    lane = lax.broadcasted_iota(jnp.int32, shape, head_axis)
    other = lax.broadcasted_iota(jnp.int32, shape, 1 - head_axis)
    lane = jnp.bitwise_and(lane, HEAD_DIM - 1) if mod else jnp.right_shift(lane, 6)
    return jnp.where(lane == other, 1.0, 0.0).astype(BF16)


def _head_rms(split, xs, g):
    w = xs.shape[1]
    to_head, from_head = _indicator((w, BLOCK), 0, False), _indicator((BLOCK, w), 1, False)
    to_lane, from_lane = _indicator((HEAD_DIM, w), 1, True), _indicator((w, HEAD_DIM), 0, True)
    ss = split(xs * xs, to_head, from_head, 3)
    r = lax.rsqrt(ss * (1.0 / HEAD_DIM) + RMS_EPS)
    g_all = split(jnp.broadcast_to(g, (8, HEAD_DIM)), to_lane, from_lane, 3)[0:1]
    return xs * split(r, from_head, to_head, 3) * g_all


def _prep(split, x, qg, kg, segs):
    parts = []
    for start, width, kind in segs:
        xs = x[:, start:start + width]
        parts.append(xs if kind == "raw" else _head_rms(split, xs, qg if kind == "q" else kg))
    return jnp.concatenate(parts, axis=1)


def prep_fwd(x, qg, kg, segs, dils, name):
    n, w = x.shape
    tm = _tile(n, 256, 8)

    def fn(ids, xt, a, b, scratch):
        ops = _prep(_plain_split, xt, a, b, segs)
        return tuple(_to_strided(scratch, ops, d) for d in dils)

    return tcall(fn, (n // tm,), [_row(x, tm), _full(qg), _full(kg)],
                 [_out((n // d, d * w), BF16, (tm // d, d * w), lambda i: (i, 0)) for d in dils], name,
                 scratch=((w // BLOCK * tm, BLOCK), F32))


def prep_bwd(x, qg, kg, segs, grads, gather, name):
    n, w = x.shape
    tm = BLOCK
    nblk = n // tm
    nslot = 1 + max(slot for _, _, _, slot in grads)

    def fn(ids, xt, a, b, *t, scratch):
        tiles, dils = [None] * nslot, [None] * nslot
        for ti, (_, sh, d, slot) in zip(t, grads):
            ti = jnp.where(ids[0] + sh < nblk, ti, 0.0) if sh else ti
            tiles[slot] = ti if tiles[slot] is None else tiles[slot] + ti
            dils[slot] = d
        tiles = [_to_natural(scratch, ti, d) for ti, d in zip(tiles, dils)]
        _, vjp = jax.vjp(lambda x_, a_, b_: _prep(_split_dot_vjp, x_, a_, b_, segs), xt, a, b)
        return vjp(gather(*tiles))

    specs = [_in(a, (tm // d, a.shape[1]), (lambda i, sh=sh: (jnp.minimum(i + sh, nblk - 1), 0)))
             for a, sh, d, _ in grads]
    wmax = max(a.shape[1] // d for a, _, d, _ in grads)
    return tcall(fn, (nblk,), [_row(x, tm), _full(qg), _full(kg)] + specs,
                 [_row_out(n, w, BF16, tm), _acc_out(qg.shape), _acc_out(kg.shape)], name,
                 scratch=((wmax // BLOCK * tm, BLOCK), F32))


def rmsnorm_fwd(x, g, name, side=None):
    n, d = x.shape
    tm = _tile(n, 512, 8)
    res = tcall(lambda ids, xt, gt: (_rms(xt, gt),), (n // tm,), [_row(x, tm), _full(g)],
                [_row_out(n, d, BF16, tm)], name, side=side)
    if side is None:
        return res[0]
    return res[0][0], res[1]


def ffn_fwd(x, g, w_gu, w_down, tag, carry=None):
    h = rmsnorm_fwd(x, g, tag + "_norm")
    if carry is None:
        gate, up, a = mm_gate_up(h, w_gu, tag + "_gu")
        return mm(a, w_down, "nn", tag + "_down", scale=0.5, res=x), (x, h, gate, up, a)
    phase, bufs = carry
    (gate, up, a), bufs = mm_gate_up(h, w_gu, tag + "_gu", side=gather_side(phase, bufs))
    y, bufs = mm(a, w_down, "nn", tag + "_down", scale=0.5, res=x, side=gather_side(phase + 1, bufs))
    return y, (x, h, gate, up, a), bufs


def ffn_bwd(dy, saved, g, w_gu, w_down, tag, chain=None):
    x, h, gate, up, a = saved

    def carrying(name, call, **kw):
        side = None if chain is None else chain.side(name)
        out = call(name=tag + "_" + name, side=side, **kw)
        if side is None:
            return out
        chain.done(name, out[1])
        return out[0]

    dgate, dup = carrying("da", mm_down_act_bwd, dy=dy, w_down=w_down, gate=gate, up=up)
    d_wdown = carrying("dwd", mm, a=a, b=dy, mode="tn", scale=0.5)
    d_wgu = (carrying("dwgu", mm, a=dgate, b=h, mode="tn"), mm(dup, h, "tn", tag + "_dwup"))
    dx, dg = mm_norm_bwd((dgate, dup), w_gu, x, g, dy, tag + "_dh", b_kd=True)
    return dx, dg, d_wgu, d_wdown


def _alibi(n_heads):
    return [float(s) for s in np.asarray(2.0 ** (-8.0 * np.arange(1, n_heads + 1) / n_heads), dtype=np.float32)]


def _banded_tile(dot, first, q, kp, kc, vp, vc, sinks, *, hkv, grp, max_dist, step, slopes, want_lse):
    row = lax.broadcasted_iota(jnp.int32, (BLOCK, 2 * BLOCK), 0)
    col = lax.broadcasted_iota(jnp.int32, (BLOCK, 2 * BLOCK), 1)
    dist = row + BLOCK - col
    valid = (dist >= 0) & (dist <= max_dist) & ((col >= BLOCK) | jnp.logical_not(first))
    distf = dist.astype(F32)

    def head(hd, qh, k2, v2):
        s = dot(qh, k2, True) * (HEAD_DIM ** -0.5)
        s = jnp.where(valid, s - (slopes[hd] * step) * distf, NEG_BIG)
        m = jnp.max(s, axis=-1, keepdims=True)
        if sinks is not None:
            pick = lax.broadcasted_iota(jnp.int32, sinks.shape, 1) == hd
            sk = jnp.sum(jnp.where(pick, sinks, 0.0), axis=1, keepdims=True)
            m = jnp.maximum(m, sk)
        m = lax.stop_gradient(m)
        p = jnp.exp(s - m)
        denom = jnp.sum(p, axis=-1, keepdims=True)
        if sinks is not None:
            denom = denom + jnp.exp(sk - m)
        return dot(p * (1.0 / denom), v2, False), m + jnp.log(denom)

    outs, lses = [], []
    if grp == 1:
        low = lax.broadcasted_iota(jnp.int32, (BLOCK, BLOCK), 1) < HEAD_DIM
        for pr in range(hkv // 2):
            sl = slice(pr * BLOCK, (pr + 1) * BLOCK)
            q2 = q[:, sl]
            k2 = jnp.concatenate([kp[:, sl], kc[:, sl]], axis=0)
            v2 = jnp.concatenate([vp[:, sl], vc[:, sl]], axis=0)
            o0, l0 = head(2 * pr, jnp.where(low, q2, 0.0), k2, v2)
            o1, l1 = head(2 * pr + 1, jnp.where(low, 0.0, q2), k2, v2)
            outs.append(jnp.where(low, o0, o1))
            lses.append(jnp.where(low, l0, l1))
    else:
        for hk in range(hkv):
            sl = slice(hk * HEAD_DIM, (hk + 1) * HEAD_DIM)
            k2 = jnp.concatenate([kp[:, sl], kc[:, sl]], axis=0)
            v2 = jnp.concatenate([vp[:, sl], vc[:, sl]], axis=0)
            for gi in range(grp):
                hd = hk * grp + gi
                o_h, l_h = head(hd, q[:, hd * HEAD_DIM:(hd + 1) * HEAD_DIM], k2, v2)
                outs.append(o_h)
                lses.append(jnp.broadcast_to(l_h, (BLOCK, HEAD_DIM)))
    o = jnp.concatenate(outs, axis=1)
    if want_lse:
        return o, jnp.concatenate(lses, axis=1)
    return (o,)


def _banded_specs(view, qcol, kcol, vcol, wq, wkv):
    def at(colfn, prev):
        if prev:
            return lambda r, n: (jnp.maximum(n - 1, 0), colfn(r))
        return lambda r, n: (n, colfn(r))
    return [
        _in(view, (BLOCK, wq), at(qcol, False)),
        _in(view, (BLOCK, wkv), at(kcol, True)),
        _in(view, (BLOCK, wkv), at(kcol, False)),
        _in(view, (BLOCK, wkv), at(vcol, True)),
        _in(view, (BLOCK, wkv), at(vcol, False)),
    ]


def banded_fwd(view, dil, cols, sinks, cfg, name):
    ns = view.shape[0]
    nb = ns // BLOCK
    wq, wkv = cfg["hkv"] * cfg["grp"] * HEAD_DIM, cfg["hkv"] * HEAD_DIM
    has_sinks = sinks is not None

    def fn(ids, q, kp, kc, vp, vc, *rest):
        q, kp, kc, vp, vc = [a.astype(F32) for a in (q, kp, kc, vp, vc)]
        return _banded_tile(_plain_dot, ids[1] == 0, q, kp, kc, vp, vc, rest[0] if has_sinks else None, **cfg)

    ins = _banded_specs(view, *cols, wq, wkv) + ([_full(sinks)] if has_sinks else [])
    outs = [_out((ns, dil * wq), F32 if cfg["want_lse"] else BF16, (BLOCK, wq), lambda r, n: (n, r))]
    if cfg["want_lse"]:
        outs.append(_out((ns, dil * wq), F32, (BLOCK, wq), lambda r, n: (n, r)))
    return tcall(fn, (dil, nb), ins, outs, name)


def banded_bwd(view, dil, cols, sinks, cfg, cts, name):
    ns = view.shape[0]
    nb = ns // BLOCK
    wq, wkv = cfg["hkv"] * cfg["grp"] * HEAD_DIM, cfg["hkv"] * HEAD_DIM
    has_sinks = sinks is not None
    assert len(cts) == (2 if cfg["want_lse"] else 1)

    def fn(ids, q, kp, kc, vp, vc, *rest):
        sk = rest[0] if has_sinks else None
        ct = rest[1 if has_sinks else 0:]
        first = ids[1] == 0

        def f(q, kp, kc, vp, vc, *s):
            return _banded_tile(_dot_vjp, first, q, kp, kc, vp, vc, s[0] if has_sinks else None, **cfg)

        prim = tuple(a.astype(F32) for a in (q, kp, kc, vp, vc)) + ((sk,) if has_sinks else ())
        _, vjp = jax.vjp(f, *prim)
        return vjp(tuple(c.astype(F32) for c in ct))

    ins = (_banded_specs(view, *cols, wq, wkv) + ([_full(sinks)] if has_sinks else [])
           + [_in(a, (BLOCK, wq), (lambda r, n, cf=cf: (n, cf(r)))) for (a, cf) in cts])
    blk = lambda w: _out((ns, dil * w), F32, (BLOCK, w), lambda r, n: (n, r))
    outs = [blk(wq), blk(wkv), blk(wkv), blk(wkv), blk(wkv)]
    if has_sinks:
        outs.append(_acc_out(sinks.shape))
    return tcall(fn, (dil, nb), ins, outs, name)


def _log_sigmoid(z):
    return jnp.minimum(z, 0.0) - jnp.log(1.0 + jnp.exp(-jnp.abs(z)))


SB_PAIRS = 4


def _sb_pair(dot, suffix, qh, kb, vb, r_in, mask):
    z = dot(qh, kb, True) * (HEAD_DIM ** -0.5)
    lsp = _log_sigmoid(z)
    log_keep = jnp.where(mask, lsp - z, 0.0)
    log_after = suffix(log_keep) + r_in
    a = jnp.where(mask, jnp.exp(lsp + log_after), 0.0)
    return dot(a, vb, False), r_in + jnp.sum(log_keep, axis=1, keepdims=True)


def sb_fwd(qkv, qcb, kcb, vcb, name, side=None):
    s = qkv.shape[0]
    nb = s // BLOCK
    pairs = B_HEADS // 2
    wide = SB_PAIRS * BLOCK
    assert pairs % SB_PAIRS == 0 and qcb % SB_PAIRS == 0 and kcb % SB_PAIRS == 0 and vcb % SB_PAIRS == 0

    def body(q_ref, k_ref, v_ref, o_ref):
        n = pl.program_id(1)
        low = lax.broadcasted_iota(jnp.int32, (BLOCK, BLOCK), 1) < HEAD_DIM
        before = (lax.broadcasted_iota(jnp.int32, (2 * BLOCK, BLOCK), 1)
                  < jnp.bitwise_and(lax.broadcasted_iota(jnp.int32, (2 * BLOCK, BLOCK), 0), BLOCK - 1))
        after = _tri(True)
        suffix = lambda t: _split_dot(t, after)
        qs = []
        for p in range(SB_PAIRS):
            q2 = q_ref[:, p * BLOCK:(p + 1) * BLOCK].astype(F32)
            qs.append(jnp.concatenate([jnp.where(low, q2, 0.0), jnp.where(low, 0.0, q2)], axis=0))

        def cond(c):
            return jnp.logical_and(c[0] >= 0, c[1] > SB_SKIP_LOG)

        def step(c):
            kb, _, rs, accs = c
            rows = pl.ds(pl.multiple_of(kb * BLOCK, BLOCK), BLOCK)
            mask = jnp.logical_or(before, kb != n)
            new_r, new_acc, top = [], [], None
            for p in range(SB_PAIRS):
                cols = slice(p * BLOCK, (p + 1) * BLOCK)
                o_part, r_out = _sb_pair(_plain_dot, suffix, qs[p], k_ref[rows, cols], v_ref[rows, cols], rs[p], mask)
                new_r.append(r_out)
                new_acc.append(accs[p] + o_part)
                top = jnp.max(r_out) if top is None else jnp.maximum(top, jnp.max(r_out))
            return kb - 1, top, tuple(new_r), tuple(new_acc)

        init = (n, jnp.float32(0.0), tuple(jnp.zeros((2 * BLOCK, 1), F32) for _ in range(SB_PAIRS)),
                tuple(jnp.zeros((2 * BLOCK, BLOCK), F32) for _ in range(SB_PAIRS)))
        accs = lax.while_loop(cond, step, init)[3]
        for p in range(SB_PAIRS):
            o_ref[:, p * BLOCK:(p + 1) * BLOCK] = jnp.where(low, accs[p][:BLOCK], accs[p][BLOCK:]).astype(o_ref.dtype)

    return _pcall(
        body, side=side, name=name, grid=(pairs // SB_PAIRS, nb),
        in_specs=[pl.BlockSpec((BLOCK, wide), lambda g, n: (n, qcb // SB_PAIRS + g)),
                  pl.BlockSpec((s, wide), lambda g, n: (0, kcb // SB_PAIRS + g), pipeline_mode=pl.Buffered(1)),
                  pl.BlockSpec((s, wide), lambda g, n: (0, vcb // SB_PAIRS + g), pipeline_mode=pl.Buffered(1))],
        out_specs=pl.BlockSpec((BLOCK, wide), lambda g, n: (n, g)),
        out_shape=jax.ShapeDtypeStruct((s, pairs * BLOCK), BF16),
        compiler_params=_params(),
    )(qkv, qkv, qkv)


def sb_bwd(qkv, qcb, kcb, vcb, do, docb, name, side=None):
    s = qkv.shape[0]
    nb = s // BLOCK
    pairs = B_HEADS // 2
    wide = SB_PAIRS * BLOCK
    assert docb % SB_PAIRS == 0

    def body(q_ref, k_ref, v_ref, do_ref, dq_ref, dk_ref, dv_ref, r_ref):
        n = pl.program_id(1)

        @pl.when(n == 0)
        def _():
            dk_ref[...] = jnp.zeros(dk_ref.shape, F32)
            dv_ref[...] = jnp.zeros(dv_ref.shape, F32)

        low = lax.broadcasted_iota(jnp.int32, (BLOCK, BLOCK), 1) < HEAD_DIM
        before = (lax.broadcasted_iota(jnp.int32, (2 * BLOCK, BLOCK), 1)
                  < jnp.bitwise_and(lax.broadcasted_iota(jnp.int32, (2 * BLOCK, BLOCK), 0), BLOCK - 1))
        after, earlier = _tri(True), _tri(False)
        suffix = lambda t: _split_dot_vjp(t, after, earlier, 2)
        stack = lambda t: jnp.concatenate([jnp.where(low, t, 0.0), jnp.where(low, 0.0, t)], axis=0)
        qs = [stack(q_ref[:, p * BLOCK:(p + 1) * BLOCK].astype(F32)) for p in range(SB_PAIRS)]
        dos = [stack(do_ref[:, p * BLOCK:(p + 1) * BLOCK].astype(F32)) for p in range(SB_PAIRS)]

        def cond(c):
            return jnp.logical_and(c[0] >= 0, c[1] > SB_SKIP_LOG)

        def down(c):
            kb, _, rs = c
            rows = pl.ds(pl.multiple_of(kb * BLOCK, BLOCK), BLOCK)
            mask = jnp.logical_or(before, kb != n)
            new_r, top = [], None
            for h in range(SB_PAIRS):
                cols = slice(h * BLOCK, (h + 1) * BLOCK)
                r_ref[h, kb] = rs[h]
                z = _dot(qs[h], k_ref[rows, cols], NT) * (HEAD_DIM ** -0.5)
                log_keep = jnp.where(mask, _log_sigmoid(z) - z, 0.0)
                r_out = rs[h] + jnp.sum(log_keep, axis=1, keepdims=True)
                new_r.append(r_out)
                top = jnp.max(r_out) if top is None else jnp.maximum(top, jnp.max(r_out))
            return kb - 1, top, tuple(new_r)

        init = (n, jnp.float32(0.0), tuple(jnp.zeros((2 * BLOCK, 1), F32) for _ in range(SB_PAIRS)))
        last = lax.while_loop(cond, down, init)[0] + 1

        def up(kb, c):
            dqs, g_rs = c
            rows = pl.ds(pl.multiple_of(kb * BLOCK, BLOCK), BLOCK)
            mask = jnp.logical_or(before, kb != n)
            new_dq, new_g = [], []
            for h in range(SB_PAIRS):
                cols = slice(h * BLOCK, (h + 1) * BLOCK)
                _, vjp = jax.vjp(lambda q_, k_, v_, r_: _sb_pair(_dot_vjp, suffix, q_, k_, v_, r_, mask),
                                 qs[h], k_ref[rows, cols].astype(F32), v_ref[rows, cols].astype(F32), r_ref[h, kb])
                dq_c, dk_c, dv_c, g_in = vjp((dos[h], g_rs[h]))
                dk_ref[rows, cols] += dk_c
                dv_ref[rows, cols] += dv_c
                new_dq.append(dqs[h] + dq_c)
                new_g.append(g_in)
            return tuple(new_dq), tuple(new_g)

        init = (tuple(jnp.zeros((2 * BLOCK, BLOCK), F32) for _ in range(SB_PAIRS)),
                tuple(jnp.zeros((2 * BLOCK, 1), F32) for _ in range(SB_PAIRS)))
        dqs = lax.fori_loop(last, n + 1, up, init)[0]
        for p in range(SB_PAIRS):
            dq_ref[:, p * BLOCK:(p + 1) * BLOCK] = jnp.where(low, dqs[p][:BLOCK], dqs[p][BLOCK:])

    full = jax.ShapeDtypeStruct((s, pairs * BLOCK), F32)
    return _pcall(
        body, side=side, name=name, grid=(pairs // SB_PAIRS, nb),
        in_specs=[pl.BlockSpec((BLOCK, wide), lambda g, n: (n, qcb // SB_PAIRS + g)),
                  pl.BlockSpec((s, wide), lambda g, n: (0, kcb // SB_PAIRS + g), pipeline_mode=pl.Buffered(1)),
                  pl.BlockSpec((s, wide), lambda g, n: (0, vcb // SB_PAIRS + g), pipeline_mode=pl.Buffered(1)),
                  pl.BlockSpec((BLOCK, wide), lambda g, n: (n, docb // SB_PAIRS + g))],
        out_specs=[pl.BlockSpec((BLOCK, wide), lambda g, n: (n, g)),
                   pl.BlockSpec((s, wide), lambda g, n: (0, g), pipeline_mode=pl.Buffered(1)),
                   pl.BlockSpec((s, wide), lambda g, n: (0, g), pipeline_mode=pl.Buffered(1))],
        out_shape=[full, full, full],
        scratch_shapes=[pltpu.VMEM((SB_PAIRS, nb, 2 * BLOCK, 1), F32)],
        compiler_params=_params(),
    )(qkv, qkv, qkv, do)


def _xa_tile(dot, q, kv, qg, kg):
    hd = q.shape[1] // X_HEADS
    outs = []
    for h in range(X_HEADS):
        qh = _rms(q[:, h * hd:(h + 1) * hd], qg)
        kh = _rms(kv[:, h * hd:(h + 1) * hd], kg)
        vh = kv[:, (X_HEADS + h) * hd:(X_HEADS + h + 1) * hd]
        sc = dot(qh, kh, True) * (hd ** -0.5)
        m = lax.stop_gradient(jnp.max(sc, axis=-1, keepdims=True))
        p = jnp.exp(sc - m)
        outs.append(dot(p * (1.0 / jnp.sum(p, axis=-1, keepdims=True)), vh, False))
    return jnp.concatenate(outs, axis=1)


def xa_core_fwd(q, kv, qg, kg, name):
    n, d = q.shape
    tm = _tile(n, 256, 8)
    (o,) = tcall(lambda ids, qt, kvt, qgt, kgt: (_xa_tile(_plain_dot, qt, kvt, qgt, kgt),), (n // tm,),
                 [_row(q, tm), _full(kv), _full(qg), _full(kg)], [_row_out(n, d, BF16, tm)], name)
    return o


def xa_core_bwd(q, kv, qg, kg, do, name):
    n, d = q.shape
    tm = _tile(n, 256, 8)

    def fn(ids, qt, kvt, qgt, kgt, dot_):
        _, vjp = jax.vjp(functools.partial(_xa_tile, _dot_vjp), qt, kvt, qgt, kgt)
        return vjp(dot_.astype(F32))

    return tcall(fn, (n // tm,), [_row(q, tm), _full(kv), _full(qg), _full(kg), _row(do, tm)],
                 [_row_out(n, d, BF16, tm), _acc_out(kv.shape), _acc_out(qg.shape), _acc_out(kg.shape)], name)


def _ev_reorder(a):
    return jnp.concatenate([a[0:512], a[768:2304], a[512:768]], axis=0)


def _ev_restore(a):
    return jnp.concatenate([a[0:512], a[2048:2304], a[512:2048]], axis=0)


_EV_SEGS = ((0, 512, "q"), (512, 1536, "raw"), (2048, 128, "k"), (2176, 128, "raw"))
_A_CFG = dict(hkv=A_KV_HEADS, grp=A_Q_HEADS // A_KV_HEADS, max_dist=BLOCK - 1, step=1.0, slopes=_alibi(A_Q_HEADS),
              want_lse=False)
_A_COLS = (lambda r: 0, lambda r: 16, lambda r: 17)


def even_mixer_fwd(x, h, w_in, qg, kg, sinks, w_out, tag, side=None):
    qkv = mm(h, w_in, "nt", tag + "_in")
    (ops,) = prep_fwd(qkv, qg, kg, _EV_SEGS, (1,), tag + "_prep")
    (o_a,) = banded_fwd(ops, 1, _A_COLS, sinks, _A_CFG, tag + "_swa")
    o_b = sb_fwd(ops, 4, 8, 12, tag + "_sb", side=side)
    carried = None
    if side is not None:
        o_b, carried = o_b
    o = jnp.concatenate([o_a, o_b], axis=1)
    y = mm(o, w_out, "nn", tag + "_out", res=x)
    return y, (x, h, qkv, ops, o), carried


def even_mixer_bwd(dy, saved, g, w_in, qg, kg, sinks, w_out, tag, side=None):
    x, h, qkv, ops, o = saved
    do = mm(dy, w_out, "nt", tag + "_do")
    d_wout = mm(o, dy, "tn", tag + "_dwout")
    dqa, dkp, dkc, dvp, dvc, dsinks = banded_bwd(ops, 1, _A_COLS, sinks, _A_CFG, [(do, lambda r: 0)], tag + "_dswa")
    res = sb_bwd(ops, 4, 8, 12, do, 4, tag + "_dsb", side=side)
    carried = None
    if side is not None:
        res, carried = res
    dqb, dkb, dvb = res
    dqkv, dqg, dkg = prep_bwd(
        qkv, qg, kg, _EV_SEGS,
        [(dqa, 0, 1, 0), (dqb, 0, 1, 1), (dkb, 0, 1, 2), (dvb, 0, 1, 3), (dkc, 0, 1, 4), (dkp, 1, 1, 4), (dvc, 0, 1, 5),
         (dvp, 1, 1, 5)],
        lambda *t: jnp.concatenate(t, axis=1), tag + "_dqkv")
    d_win = mm(dqkv, h, "tn", tag + "_dwin")
    dx, dg = mm_norm_bwd(dqkv, w_in, x, g, dy, tag + "_dh", b_kd=True)
    return dx, dg, d_win, dqg, dkg, dsinks, d_wout, carried


def _c_cfg(window, dil):
    return dict(hkv=C_HEADS, grp=1, max_dist=window // dil, step=float(dil), slopes=_alibi(C_HEADS), want_lse=True)


_C_COLS = (lambda r: 3 * r, lambda r: 3 * r + 1, lambda r: 3 * r + 2)
_OD_SEGS = ((0, 1024, "q"), (1024, 1024, "k"), (2048, 1024, "raw"))


def _combine(o1, o2, o3, l1, l2, l3):
    m = lax.stop_gradient(jnp.maximum(jnp.maximum(l1, l2), l3))
    e1, e2, e3 = jnp.exp(l1 - m), jnp.exp(l2 - m), jnp.exp(l3 - m)
    tot = e1 + e2 + e3
    return (e1 / tot) * o1 + (e2 / tot) * o2 + (e3 / tot) * o3


def odd_mixer_fwd(x, g, w_in, qg, kg, w_out, tag):
    n, d = x.shape
    h = rmsnorm_fwd(x, g, tag + "_norm")
    qkv = mm(h, w_in, "nt", tag + "_in")
    dils = [dil for _, dil in C_PATTERNS]
    ops = prep_fwd(qkv, qg, kg, _OD_SEGS, dils, tag + "_prep")
    os_, ls_ = [], []
    for (window, dil), ops_d in zip(C_PATTERNS, ops):
        o_p, l_p = banded_fwd(ops_d, dil, _C_COLS, None, _c_cfg(window, dil), f"{tag}_dil{dil}")
        os_.append(o_p)
        ls_.append(l_p)
    tm = BLOCK
    lay = lambda a, dil: _in(a, (tm // dil, a.shape[1]), lambda i: (i, 0))
    views = [lay(a, dil) for a, dil in zip(os_ + ls_, dils + dils)]

    def comb(ids, *t, scratch):
        return (_combine(*[_to_natural(scratch, a, dil) for a, dil in zip(t, dils + dils)]),)

    (o,) = tcall(comb, (n // tm,), views, [_row_out(n, d, BF16, tm)], tag + "_comb",
                 scratch=((d // BLOCK * tm, BLOCK), F32))
    y = mm(o, w_out, "nn", tag + "_out", res=x)
    return y, (x, h, qkv, ops, views, o)


def odd_mixer_bwd(dy, saved, g, w_in, qg, kg, w_out, tag):
    x, h, qkv, ops, views, o = saved
    n, d = x.shape
    do = mm(dy, w_out, "nt", tag + "_do")
    d_wout = mm(o, dy, "tn", tag + "_dwout")
    tm = BLOCK
    dils = [dil for _, dil in C_PATTERNS]

    def comb_bwd(ids, *t, scratch):
        _, vjp = jax.vjp(_combine, *[_to_natural(scratch, a, dil) for a, dil in zip(t[:6], dils + dils)])
        return tuple(_to_strided(scratch, c, dil) for c, dil in zip(vjp(t[6]), dils + dils))

    cts = tcall(comb_bwd, (n // tm,), views + [_row(do, tm)],
                [_out((n // dil, dil * d), F32, (tm // dil, dil * d), lambda i: (i, 0)) for dil in dils + dils],
                tag + "_dcomb", scratch=((d // BLOCK * tm, BLOCK), F32))
    dqs, dks, dvs = [], [], []
    for p, ((window, dil), ops_d) in enumerate(zip(C_PATTERNS, ops)):
        dq, dkp, dkc, dvp, dvc = banded_bwd(ops_d, dil, _C_COLS, None, _c_cfg(window, dil),
                                            [(cts[p], lambda r: r), (cts[3 + p], lambda r: r)], f"{tag}_ddil{dil}")
        dqs.append((dq, 0, dil, p))
        dks += [(dkc, 0, dil, 3 + p), (dkp, dil, dil, 3 + p)]
        dvs += [(dvc, 0, dil, 6 + p), (dvp, dil, dil, 6 + p)]

    def gather(*t):
        return jnp.concatenate([t[0] + t[1] + t[2], t[3] + t[4] + t[5], t[6] + t[7] + t[8]], axis=1)

    dqkv, dqg, dkg = prep_bwd(qkv, qg, kg, _OD_SEGS, dqs + dks + dvs, gather, tag + "_dqkv")
    d_win = mm(dqkv, h, "tn", tag + "_dwin")
    dx, dg = mm_norm_bwd(dqkv, w_in, x, g, dy, tag + "_dh", b_kd=True)
    return dx, dg, d_win, dqg, dkg, d_wout


def xa_fwd(x, mem, g, gm, w_q, w_kv, qg, kg, w_o, tag):
    h = rmsnorm_fwd(x, g, tag + "_norm")
    q = mm(h, w_q, "nn", tag + "_q")
    mn = rmsnorm_fwd(mem, gm, tag + "_mnorm")
    kv = mm(mn, w_kv, "nt", tag + "_kv")
    o = xa_core_fwd(q, kv, qg, kg, tag + "_core")
    y = mm(o, w_o, "nn", tag + "_o", res=x)
    return y, (x, h, q, mn, kv, o)


def xa_bwd(dy, saved, mem, g, gm, w_q, w_kv, qg, kg, w_o, tag):
    x, h, q, mn, kv, o = saved
    do = mm(dy, w_o, "nt", tag + "_do", out_dtype=BF16)
    d_wo = mm(o, dy, "tn", tag + "_dwo")
    dq, dkv, dqg, dkg = xa_core_bwd(q, kv, qg, kg, do, tag + "_dcore")
    d_wq = mm(h, dq, "tn", tag + "_dwq")
    dx, dg = mm_norm_bwd(dq, w_q, x, g, dy, tag + "_dh")
    d_wkv = mm(dkv, mn, "tn", tag + "_dwkv")
    _, dgm = mm_norm_bwd(dkv, w_kv, mem, gm, None, tag + "_dmn", b_kd=True)
    return dx, dg, dgm, d_wq, d_wkv, dqg, dkg, d_wo


def loss_head(y, target, name):
    n, d = y.shape
    tm = _tile(n, 512, 8)

    def fn(ids, yt, tt):
        e = yt - tt
        return e * (1.0 / d), jnp.sum(e * e, axis=0, keepdims=True)

    return tcall(fn, (n // tm,), [_row(y, tm), _row(target, tm)], [_row_out(n, d, F32, tm), _acc_out((1, d))], name)


_ANY = pl.BlockSpec(memory_space=pl.ANY)


def all_gather_blocks(blocks):
    nb = len(blocks)

    def body(*refs):
        x_refs, out_refs = refs[:nb], refs[nb:2 * nb]
        send_sems, recv_sems, local_sems = refs[2 * nb:]
        x, y, c = lax.axis_index("x"), lax.axis_index("y"), lax.axis_index("c")
        me, sibling = (x, y, c), (x, y, 1 - c)
        over_x, over_y, diagonal = (1 - x, y), (x, 1 - y), (1 - x, 1 - y)
        relay_of = ((1 - x) * (1 - c) + x * c, y * (1 - c) + (1 - y) * c)
        relay_to = (x * (1 - c) + (1 - x) * c, (1 - y) * (1 - c) + y * c)

        def copy(b, k, blk, to, own=False):
            px, py, pc = blk
            slot = out_refs[b].at[4 * px + 2 * py + pc]
            return pltpu.make_async_remote_copy(
                src_ref=x_refs[b] if own else slot, dst_ref=slot,
                send_sem=send_sems.at[7 * b + k], recv_sem=recv_sems.at[7 * b + k], device_id=to, device_id_type=MESH)

        mine = [pltpu.make_async_copy(x_refs[b], out_refs[b].at[4 * x + 2 * y + c], local_sems.at[b]) for b in range(nb)]
        for cp in mine:
            cp.start()
        sent = []
        for b in range(nb):
            sent += [copy(b, 0, me, sibling, own=True), copy(b, 1, me, (*over_x, c), own=True),
                     copy(b, 2, me, (*over_y, c), own=True)]
        for cp in sent:
            cp.start()
        for b in range(nb):
            copy(b, 1, (*over_x, c), me).wait_recv()
            copy(b, 2, (*over_y, c), me).wait_recv()
            later = [copy(b, 3, (*relay_of, c), (*relay_to, c)), copy(b, 4, (*over_x, c), sibling),
                     copy(b, 5, (*over_y, c), sibling)]
            for cp in later:
                cp.start()
            sent += later
        for b in range(nb):
            copy(b, 3, (*diagonal, c), me).wait_recv()
            fwd = copy(b, 6, (*diagonal, c), sibling)
            fwd.start()
            sent.append(fwd)
        for b in range(nb):
            copy(b, 0, sibling, me).wait_recv()
            for k, chip in ((4, over_x), (5, over_y), (6, diagonal)):
                copy(b, k, (*chip, 1 - c), me).wait_recv()
        for cp in sent:
            cp.wait_send()
        for cp in mine:
            cp.wait()

    return _pcall(
        body, name="weights_all_gather",
        in_specs=[_ANY] * nb, out_specs=[_ANY] * nb,
        out_shape=[jax.ShapeDtypeStruct((N_DEV,) + a.shape, a.dtype) for a in blocks],
        scratch_shapes=[pltpu.SemaphoreType.DMA((7 * nb,)), pltpu.SemaphoreType.DMA((7 * nb,)),
                        pltpu.SemaphoreType.DMA((nb,))],
    )(*blocks)


def pair_exchange(bufs):
    nb = len(bufs)

    def body(*refs):
        srcs, dsts = refs[:nb], refs[nb:2 * nb]
        send_sems, recv_sems = refs[2 * nb:]
        x, y, c = lax.axis_index("x"), lax.axis_index("y"), lax.axis_index("c")
        copies = []
        for b in range(nb):
            for j in range(4):
                cp = pltpu.make_async_remote_copy(
                    src_ref=srcs[b].at[2 * j + (1 - c)], dst_ref=dsts[b].at[j], send_sem=send_sems.at[4 * b + j],
                    recv_sem=recv_sems.at[4 * b + j], device_id=(x, y, 1 - c), device_id_type=MESH)
                cp.start()
                copies.append(cp)
        for cp in copies:
            cp.wait()

    return _pcall(
        body, name="grads_pair_exchange",
        in_specs=[_ANY] * nb, out_specs=[_ANY] * nb,
        out_shape=[jax.ShapeDtypeStruct((4,) + a.shape[1:], a.dtype) for a in bufs],
        scratch_shapes=[pltpu.SemaphoreType.DMA((4 * nb,)), pltpu.SemaphoreType.DMA((4 * nb,))],
    )(*bufs)


def pair_sum(g, got, c, out_dtype, name):
    r, w = g.shape[1:]
    tr = _tile(r, 512, 16)

    def body(c_ref, a_ref, b_ref, o_ref):
        o_ref[...] = (a_ref[...].astype(F32) + b_ref[...].astype(F32)).astype(o_ref.dtype)

    return _pcall(
        body, name=name,
        grid_spec=pltpu.PrefetchScalarGridSpec(
            num_scalar_prefetch=1, grid=(4, r // tr),
            in_specs=[pl.BlockSpec((None, tr, w), lambda j, i, c_ref: (2 * j + c_ref[0], i, 0)),
                      pl.BlockSpec((None, tr, w), lambda j, i, c_ref: (j, i, 0))],
            out_specs=pl.BlockSpec((None, tr, w), lambda j, i, c_ref: (j, i, 0))),
        out_shape=jax.ShapeDtypeStruct((4,) + g.shape[1:], out_dtype),
        compiler_params=_params(),
    )(c, g, got)


def chip_exchange(parts):
    nb = len(parts)

    def body(*refs):
        srcs, dsts = refs[:nb], refs[nb:2 * nb]
        send_sems, recv_sems, local_sems = refs[2 * nb:]
        x, y, c = lax.axis_index("x"), lax.axis_index("y"), lax.axis_index("c")
        my_chip = 2 * x + y
        copies = []
        for b in range(nb):
            mine = pltpu.make_async_copy(srcs[b].at[my_chip], dsts[b].at[my_chip], local_sems.at[b])
            mine.start()
            copies.append(mine)
            for k, (tx, ty) in enumerate([(1 - x, y), (x, 1 - y), (1 - x, 1 - y)]):
                cp = pltpu.make_async_remote_copy(
                    src_ref=srcs[b].at[2 * tx + ty], dst_ref=dsts[b].at[my_chip], send_sem=send_sems.at[3 * b + k],
                    recv_sem=recv_sems.at[3 * b + k], device_id=(tx, ty, c), device_id_type=MESH)
                cp.start()
                copies.append(cp)
        for cp in copies:
            cp.wait()

    return _pcall(
        body, name="grads_chip_exchange",
        in_specs=[_ANY] * nb, out_specs=[_ANY] * nb,
        out_shape=[jax.ShapeDtypeStruct(a.shape, a.dtype) for a in parts],
        scratch_shapes=[pltpu.SemaphoreType.DMA((3 * nb,)), pltpu.SemaphoreType.DMA((3 * nb,)),
                        pltpu.SemaphoreType.DMA((nb,))],
    )(*parts)


def chip_sum(parts, name):
    r, w = parts.shape[1:]
    tr = _tile(r, 512, 16)
    spec = lambda j: _in(parts, (None, tr, w), lambda i, j=j: (j, i, 0))

    def fn(ids, a, b, c_, d):
        a, b, c_, d = [t.astype(F32) for t in (a, b, c_, d)]
        return (((a + b) + c_) + d,)

    (out,) = tcall(fn, (r // tr,), [spec(j) for j in range(4)],
                   [_out((r, w), F32, (tr, w), lambda i: (i, 0))], name)
    return out


def _remote(src, dst, send_sems, recv_sems, k, to):
    return functools.partial(pltpu.make_async_remote_copy, src_ref=src, dst_ref=dst, send_sem=send_sems.at[k],
                             recv_sem=recv_sems.at[k], device_id=to, device_id_type=MESH)


def _gather_plan(phase, nb):
    def plan(ins, outs, send_sems, recv_sems, local_sems):
        x, y, c = lax.axis_index("x"), lax.axis_index("y"), lax.axis_index("c")
        me, sibling = (x, y, c), (x, y, 1 - c)
        over_x, over_y, diagonal = (1 - x, y), (x, 1 - y), (1 - x, 1 - y)
        relay_of = ((1 - x) * (1 - c) + x * c, y * (1 - c) + (1 - y) * c)
        relay_to = (x * (1 - c) + (1 - x) * c, (1 - y) * (1 - c) + y * c)
        local, sends, recvs = [], [], []
        for b in range(nb):
            slot = lambda chip, core, b=b: outs[b].at[4 * chip[0] + 2 * chip[1] + core]
            if phase == 0:
                local.append(functools.partial(pltpu.make_async_copy, ins[b], slot((x, y), c), local_sems.at[b]))
                moves = [(ins[b], slot((x, y), c), to) for to in (sibling, (*over_x, c), (*over_y, c))]
                arrive = [slot((x, y), 1 - c), slot(over_x, c), slot(over_y, c)]
            elif phase == 1:
                moves = [(slot(relay_of, c), slot(relay_of, c), (*relay_to, c)),
                         (slot(over_x, c), slot(over_x, c), sibling), (slot(over_y, c), slot(over_y, c), sibling)]
                arrive = [slot(diagonal, c), slot(over_x, 1 - c), slot(over_y, 1 - c)]
            else:
                moves = [(slot(diagonal, c), slot(diagonal, c), sibling)]
                arrive = [slot(diagonal, 1 - c)]
            sends += [_remote(src, dst, send_sems, recv_sems, 3 * b + k, to) for k, (src, dst, to) in enumerate(moves)]
            recvs += [_remote(dst, dst, send_sems, recv_sems, 3 * b + k, me) for k, dst in enumerate(arrive)]
        return local, sends, recvs
    return plan


def gather_side(phase, arrays):
    nb = len(arrays)
    if phase == 0:
        shapes = [jax.ShapeDtypeStruct((N_DEV,) + a.shape, a.dtype) for a in arrays]
        return Side(arrays, shapes, 3 * nb, nb, _gather_plan(0, nb))
    shapes = [jax.ShapeDtypeStruct(a.shape, a.dtype) for a in arrays]
    return Side(arrays, shapes, 3 * nb, 0, _gather_plan(phase, nb), aliased=True)


def pair_side(bufs):
    nb = len(bufs)

    def plan(ins, outs, send_sems, recv_sems, local_sems):
        x, y, c = lax.axis_index("x"), lax.axis_index("y"), lax.axis_index("c")
        sends = [_remote(ins[b].at[2 * j + (1 - c)], outs[b].at[j], send_sems, recv_sems, 4 * b + j, (x, y, 1 - c))
                 for b in range(nb) for j in range(4)]
        recvs = [_remote(outs[b].at[j], outs[b].at[j], send_sems, recv_sems, 4 * b + j, (x, y, c))
                 for b in range(nb) for j in range(4)]
        return [], sends, recvs

    shapes = [jax.ShapeDtypeStruct((4,) + a.shape[1:], a.dtype) for a in bufs]
    return Side(bufs, shapes, 4 * nb, 0, plan)


def chip_side(parts):
    nb = len(parts)

    def plan(ins, outs, send_sems, recv_sems, local_sems):
        x, y, c = lax.axis_index("x"), lax.axis_index("y"), lax.axis_index("c")
        my_chip = 2 * x + y
        peers = [(1 - x, y), (x, 1 - y), (1 - x, 1 - y)]
        local = [functools.partial(pltpu.make_async_copy, ins[b].at[my_chip], outs[b].at[my_chip], local_sems.at[b])
                 for b in range(nb)]
        sends = [_remote(ins[b].at[2 * tx + ty], outs[b].at[my_chip], send_sems, recv_sems, 3 * b + k, (tx, ty, c))
                 for b in range(nb) for k, (tx, ty) in enumerate(peers)]
        recvs = [_remote(outs[b].at[2 * tx + ty], outs[b].at[2 * tx + ty], send_sems, recv_sems, 3 * b + k, (x, y, c))
                 for b in range(nb) for k, (tx, ty) in enumerate(peers)]
        return local, sends, recvs

    shapes = [jax.ShapeDtypeStruct(a.shape, a.dtype) for a in parts]
    return Side(parts, shapes, 3 * nb, nb, plan)


def adamw(w, g, m, v, name):
    shape = w.shape
    cols = shape[-1]
    rows = int(np.prod(shape[:-1]))
    w2, g2, m2, v2 = [a.reshape(rows, cols) for a in (w, g, m, v)]
    tr = _tile(rows, 256, 8) if rows % 8 == 0 else rows

    def fn(ids, wt, gt, mt, vt):
        m_new = ADAM_B1 * mt + (1.0 - ADAM_B1) * gt
        v_new = ADAM_B2 * vt + (1.0 - ADAM_B2) * (gt * gt)
        m_hat = m_new / (1.0 - ADAM_B1 ** ADAM_STEP)
        v_hat = v_new / (1.0 - ADAM_B2 ** ADAM_STEP)
        delta = -ADAM_LR * (m_hat / (jnp.sqrt(v_hat) + ADAM_EPS) + ADAM_WD * wt)
        return delta, m_new, v_new

    res = tcall(fn, (rows // tr,), [_row(a, tr) for a in (w2, g2, m2, v2)],
                [_row_out(rows, cols, F32, tr) for _ in range(3)], name)
    return [a.reshape(shape) for a in res]


_MATS = [("ffn1_w_gu", "col"), ("ffn1_w_down", "row"), ("ev_w_in", "col"), ("ev_w_out", "row"),
         ("od_w_in", "col"), ("od_w_out", "row"), ("xa_w_q", "row"), ("xa_w_kv", "col"), ("xa_w_o", "row"),
         ("ffn2_w_gu", "col"), ("ffn2_w_down", "row")]
_VECS = ["ffn1_norm", "mix_norm", "ev_q_gain", "ev_k_gain", "ev_sinks", "od_q_gain", "od_k_gain", "xa_norm",
         "xa_mem_norm", "xa_q_gain", "xa_k_gain", "ffn2_norm"]
_WEIGHTS = ["ffn1_norm", "ffn1_w_gu", "ffn1_w_down", "mix_norm", "ev_w_in", "ev_q_gain", "ev_k_gain", "ev_sinks",
            "ev_w_out", "od_w_in", "od_q_gain", "od_k_gain", "od_w_out", "xa_norm", "xa_mem_norm", "xa_w_q", "xa_w_kv",
            "xa_q_gain", "xa_k_gain", "xa_w_o", "ffn2_norm", "ffn2_w_gu", "ffn2_w_down"]


_AXIS = dict(_MATS)
DEPTH = 2


def _layer_groups(l):
    first, rest = _first_block_groups(l)
    return [first[0] + rest[0] + rest[1]]


def _first_block_groups(l):
    w_in, w_out = ("ev_w_in", "ev_w_out") if l % 2 == 0 else ("od_w_in", "od_w_out")
    first = [[("ffn1_w_gu", l), ("ffn1_w_down", l)]]
    rest = [[("ffn2_w_gu", l), ("xa_w_kv", l)],
            [(w_in, l // 2), ("ffn2_w_down", l), (w_out, l // 2), ("xa_w_q", l), ("xa_w_o", l)]]
    return first, rest


def _block_rows(shards, n):
    a, b = shards[n].shape[1:]
    return a if _AXIS[n] == "row" else b


def _weight_blocks(shards, groups):
    blocks = []
    for group in groups:
        rows = [(shards[n][j] if _AXIS[n] == "row" else shards[n][j].T).astype(BF16) for n, j in group]
        blocks.append(rows[0] if len(rows) == 1 else jnp.concatenate(rows, axis=0))
    return blocks


def _whole_weights(shards, groups, gathered):
    full = {}
    for group, got in zip(groups, gathered):
        off = 0
        for n, j in group:
            r = _block_rows(shards, n)
            full[n] = got[:, off:off + r, :].reshape(N_DEV * r, got.shape[2])
            off += r
    return full


def _gradient_buffers(grads, groups):
    bufs = []
    for group in groups:
        rows = []
        for n, _ in group:
            whole = jnp.concatenate(grads[n], axis=0) if isinstance(grads[n], tuple) else grads[n]
            rows.append(whole.reshape(N_DEV, whole.shape[0] // N_DEV, whole.shape[1]))
        bufs.append((rows[0] if len(rows) == 1 else jnp.concatenate(rows, axis=1)).astype(BF16))
    return bufs


def _gradient_blocks(shards, groups, sums):
    out = {}
    for group, tot in zip(groups, sums):
        off = 0
        for n, j in group:
            r = _block_rows(shards, n)
            out[n, j] = tot[off:off + r] if _AXIS[n] == "row" else tot[off:off + r].T
            off += r
    return out


class _PairChain:
    def __init__(self, ex, bufs):
        self.ex, self.bufs, self.parts = ex, bufs, None

    def side(self, name):
        return pair_side(self.bufs) if name == "dwd" else None

    def done(self, name, carried):
        self.parts = self.ex.pair_sums(self.bufs, carried, "l1")


class _RestChain:
    HALF = {"dwd": (0,), "dwgu": (1,)}

    def __init__(self, ex, bufs):
        self.ex, self.bufs, self.parts, self.sums = ex, bufs, None, [None] * len(bufs)

    def side(self, name):
        if name == "da":
            return pair_side(self.bufs)
        return chip_side([self.parts[i] for i in self.HALF[name]])

    def done(self, name, carried):
        if name == "da":
            self.parts = self.ex.pair_sums(self.bufs, carried, "l0r")
        else:
            for i, tot in zip(self.HALF[name], self.ex.chip_sums(carried, "l0r_" + name)):
                self.sums[i] = tot


class _Exchange:
    def __init__(self, shards, c):
        self.shards, self.c = shards, c

    def weights_first(self):
        first, _ = _first_block_groups(0)
        return _whole_weights(self.shards, first, all_gather_blocks(_weight_blocks(self.shards, first)))

    def rest_blocks(self):
        return _weight_blocks(self.shards, _first_block_groups(0)[1])

    def weights_rest(self, gathered):
        return _whole_weights(self.shards, _first_block_groups(0)[1], gathered)

    def gather_start(self):
        return gather_side(0, _weight_blocks(self.shards, _layer_groups(1)))

    def weights_next(self, gathered):
        return _whole_weights(self.shards, _layer_groups(1), gathered)

    def chain_next(self, grads):
        return _PairChain(self, _gradient_buffers(grads, _layer_groups(1)))

    def chain_rest(self, grads):
        return _RestChain(self, _gradient_buffers(grads, _first_block_groups(0)[1]))

    def pair_sums(self, bufs, got, tag):
        return [pair_sum(b, g, self.c, b.dtype, f"grads_pair_sum_{tag}_{i}") for i, (b, g) in enumerate(zip(bufs, got))]

    def chip_sums(self, parts, tag):
        return [chip_sum(p, f"grads_chip_sum_{tag}_{i}") for i, p in enumerate(parts)]

    def finish(self, gm, gv, sums1, sums_rest):
        vecs = {n: jnp.concatenate(v, axis=0) for n, v in gv.items()}
        first, rest = _first_block_groups(0)
        bufs = _gradient_buffers(gm[0], first)
        vec = jnp.concatenate([vecs[n].reshape(-1) for n in _VECS])
        vec = jnp.pad(vec, (0, -vec.shape[0] % (16 * LANES)))
        bufs.append(jnp.broadcast_to(vec.reshape(1, -1, LANES), (N_DEV, vec.shape[0] // LANES, LANES)))
        parts = self.pair_sums(bufs, pair_exchange(bufs), "l0")
        sums0 = self.chip_sums(chip_exchange(parts), "l0")
        blocks = {**_gradient_blocks(self.shards, first, sums0[:-1]), **_gradient_blocks(self.shards, rest, sums_rest),
                  **_gradient_blocks(self.shards, _layer_groups(1), sums1)}
        out = {n: jnp.stack([blocks[n, j] for j in range(self.shards[n].shape[0])]) for n, _ in _MATS}
        flat, off = sums0[-1].reshape(-1), 0
        for n in _VECS:
            out[n] = flat[off:off + vecs[n].size].reshape(vecs[n].shape)
            off += vecs[n].size
        return out


class _NoExchange:
    def __init__(self, full):
        self.full = full

    def weights_first(self):
        return self.full[0]

    def rest_blocks(self):
        return None

    def gather_start(self):
        return None

    def weights_next(self, gathered):
        return self.full[1]

    def chain_next(self, grads):
        return None

    def chain_rest(self, grads):
        return None

    def finish(self, gm, gv, sums1, sums_rest):
        mats = {}
        for l in range(DEPTH):
            for group in _layer_groups(l):
                for n, j in group:
                    whole = jnp.concatenate(gm[l][n], axis=0) if isinstance(gm[l][n], tuple) else gm[l][n]
                    mats.setdefault(n, {})[j] = whole if _AXIS[n] == "row" else whole.T
        mats = {n: jnp.stack([v[j] for j in sorted(v)]) for n, v in mats.items()}
        return mats, {n: jnp.concatenate(v, axis=0) for n, v in gv.items()}


def _local_step(x, mem, target, w, ex):
    assert w["ffn1_norm"].shape[0] == DEPTH
    row = lambda a, l: a[l:l + 1]
    full = [ex.weights_first(), None]
    saved = []
    for l in range(DEPTH):
        t, j, f = f"l{l}", l // 2, full[l]
        rest = ex.rest_blocks() if l == 0 else None
        if rest is None:
            x, s1 = ffn_fwd(x, row(w["ffn1_norm"], l), f["ffn1_w_gu"], f["ffn1_w_down"], t + "_ffn1")
        else:
            x, s1, rest = ffn_fwd(x, row(w["ffn1_norm"], l), f["ffn1_w_gu"], f["ffn1_w_down"], t + "_ffn1", (0, rest))
        relay = None
        if l % 2 == 0:
            h = rmsnorm_fwd(x, row(w["mix_norm"], l), t + "_ev_norm", None if rest is None else gather_side(2, rest))
            if rest is not None:
                h, rest = h
                f = full[l] = {**f, **ex.weights_rest(rest)}
            side = ex.gather_start() if l + 1 < DEPTH else None
            x, s2, relay = even_mixer_fwd(x, h, _ev_reorder(f["ev_w_in"]), row(w["ev_q_gain"], j),
                                          row(w["ev_k_gain"], j), row(w["ev_sinks"], j), f["ev_w_out"], t + "_ev", side)
        else:
            x, s2 = odd_mixer_fwd(x, row(w["mix_norm"], l), f["od_w_in"], row(w["od_q_gain"], j),
                                  row(w["od_k_gain"], j), f["od_w_out"], t + "_od")
        x, s3 = xa_fwd(x, mem, row(w["xa_norm"], l), row(w["xa_mem_norm"], l), f["xa_w_q"], f["xa_w_kv"],
                       row(w["xa_q_gain"], l), row(w["xa_k_gain"], l), f["xa_w_o"], t + "_xa")
        if relay is None:
            x, s4 = ffn_fwd(x, row(w["ffn2_norm"], l), f["ffn2_w_gu"], f["ffn2_w_down"], t + "_ffn2")
        else:
            x, s4, relay = ffn_fwd(x, row(w["ffn2_norm"], l), f["ffn2_w_gu"], f["ffn2_w_down"], t + "_ffn2", (1, relay))
        if l + 1 < DEPTH:
            full[l + 1] = ex.weights_next(relay)
        saved.append((s1, s2, s3, s4))
    dx, sq = loss_head(x, target, "loss_head")
    loss = 0.5 * jnp.sum(sq) / x.shape[1]

    gm = [dict() for _ in range(DEPTH)]
    gv = {n: [None] * w[n].shape[0] for n in _VECS}
    chain1 = chain0 = sums1 = None
    for l in reversed(range(DEPTH)):
        t, j, f = f"l{l}", l // 2, full[l]
        s1, s2, s3, s4 = saved[l]
        dx, gv["ffn2_norm"][l], gm[l]["ffn2_w_gu"], gm[l]["ffn2_w_down"] = ffn_bwd(
            dx, s4, row(w["ffn2_norm"], l), f["ffn2_w_gu"], f["ffn2_w_down"], t + "_ffn2", chain1 if l == 0 else None)
        parts = chain1.parts if l == 0 and chain1 is not None else None
        (dx, gv["xa_norm"][l], gv["xa_mem_norm"][l], gm[l]["xa_w_q"], gm[l]["xa_w_kv"], gv["xa_q_gain"][l],
         gv["xa_k_gain"][l], gm[l]["xa_w_o"]) = xa_bwd(
            dx, s3, mem, row(w["xa_norm"], l), row(w["xa_mem_norm"], l), f["xa_w_q"], f["xa_w_kv"],
            row(w["xa_q_gain"], l), row(w["xa_k_gain"], l), f["xa_w_o"], t + "_xa")
        if l % 2 == 0:
            (dx, gv["mix_norm"][l], d_win, gv["ev_q_gain"][j], gv["ev_k_gain"][j], gv["ev_sinks"][j],
             gm[l]["ev_w_out"], carried) = even_mixer_bwd(
                dx, s2, row(w["mix_norm"], l), _ev_reorder(f["ev_w_in"]), row(w["ev_q_gain"], j), row(w["ev_k_gain"], j),
                row(w["ev_sinks"], j), f["ev_w_out"], t + "_ev", None if parts is None else chip_side(parts))
            gm[l]["ev_w_in"] = _ev_restore(d_win)
            if carried is not None:
                sums1 = ex.chip_sums(carried, "l1")
        else:
            (dx, gv["mix_norm"][l], gm[l]["od_w_in"], gv["od_q_gain"][j], gv["od_k_gain"][j],
             gm[l]["od_w_out"]) = odd_mixer_bwd(
                dx, s2, row(w["mix_norm"], l), f["od_w_in"], row(w["od_q_gain"], j), row(w["od_k_gain"], j),
                f["od_w_out"], t + "_od")
        if l == 0:
            chain0 = ex.chain_rest(gm[l])
        dx, gv["ffn1_norm"][l], gm[l]["ffn1_w_gu"], gm[l]["ffn1_w_down"] = ffn_bwd(
            dx, s1, row(w["ffn1_norm"], l), f["ffn1_w_gu"], f["ffn1_w_down"], t + "_ffn1", chain0 if l == 0 else None)
        if l == 1:
            chain1 = ex.chain_next(gm[l])
    return loss, dx, ex.finish(gm, gv, sums1, None if chain0 is None else chain0.sums)


def kernel(x, mem, ffn1_norm, ffn1_w_gu, ffn1_w_down, mix_norm, ev_w_in, ev_q_gain, ev_k_gain, ev_sinks, ev_w_out, od_w_in, od_q_gain, od_k_gain, od_w_out, xa_norm, xa_mem_norm, xa_w_q, xa_w_kv, xa_q_gain, xa_k_gain, xa_w_o, ffn2_norm, ffn2_w_gu, ffn2_w_down, loss_target, m_ffn1_norm, m_ffn1_w_gu, m_ffn1_w_down, m_mix_norm, m_ev_w_in, m_ev_q_gain, m_ev_k_gain, m_ev_sinks, m_ev_w_out, m_od_w_in, m_od_q_gain, m_od_k_gain, m_od_w_out, m_xa_norm, m_xa_mem_norm, m_xa_w_q, m_xa_w_kv, m_xa_q_gain, m_xa_k_gain, m_xa_w_o, m_ffn2_norm, m_ffn2_w_gu, m_ffn2_w_down, v_ffn1_norm, v_ffn1_w_gu, v_ffn1_w_down, v_mix_norm, v_ev_w_in, v_ev_q_gain, v_ev_k_gain, v_ev_sinks, v_ev_w_out, v_od_w_in, v_od_q_gain, v_od_k_gain, v_od_w_out, v_xa_norm, v_xa_mem_norm, v_xa_w_q, v_xa_w_kv, v_xa_q_gain, v_xa_k_gain, v_xa_w_o, v_ffn2_norm, v_ffn2_w_gu, v_ffn2_w_down):
    w = dict(ffn1_norm=ffn1_norm, ffn1_w_gu=ffn1_w_gu, ffn1_w_down=ffn1_w_down, mix_norm=mix_norm, ev_w_in=ev_w_in, ev_q_gain=ev_q_gain, ev_k_gain=ev_k_gain, ev_sinks=ev_sinks, ev_w_out=ev_w_out, od_w_in=od_w_in, od_q_gain=od_q_gain, od_k_gain=od_k_gain, od_w_out=od_w_out, xa_norm=xa_norm, xa_mem_norm=xa_mem_norm, xa_w_q=xa_w_q, xa_w_kv=xa_w_kv, xa_q_gain=xa_q_gain, xa_k_gain=xa_k_gain, xa_w_o=xa_w_o, ffn2_norm=ffn2_norm, ffn2_w_gu=ffn2_w_gu, ffn2_w_down=ffn2_w_down)
    m = dict(ffn1_norm=m_ffn1_norm, ffn1_w_gu=m_ffn1_w_gu, ffn1_w_down=m_ffn1_w_down, mix_norm=m_mix_norm, ev_w_in=m_ev_w_in, ev_q_gain=m_ev_q_gain, ev_k_gain=m_ev_k_gain, ev_sinks=m_ev_sinks, ev_w_out=m_ev_w_out, od_w_in=m_od_w_in, od_q_gain=m_od_q_gain, od_k_gain=m_od_k_gain, od_w_out=m_od_w_out, xa_norm=m_xa_norm, xa_mem_norm=m_xa_mem_norm, xa_w_q=m_xa_w_q, xa_w_kv=m_xa_w_kv, xa_q_gain=m_xa_q_gain, xa_k_gain=m_xa_k_gain, xa_w_o=m_xa_w_o, ffn2_norm=m_ffn2_norm, ffn2_w_gu=m_ffn2_w_gu, ffn2_w_down=m_ffn2_w_down)
    v = dict(ffn1_norm=v_ffn1_norm, ffn1_w_gu=v_ffn1_w_gu, ffn1_w_down=v_ffn1_w_down, mix_norm=v_mix_norm, ev_w_in=v_ev_w_in, ev_q_gain=v_ev_q_gain, ev_k_gain=v_ev_k_gain, ev_sinks=v_ev_sinks, ev_w_out=v_ev_w_out, od_w_in=v_od_w_in, od_q_gain=v_od_q_gain, od_k_gain=v_od_k_gain, od_w_out=v_od_w_out, xa_norm=v_xa_norm, xa_mem_norm=v_xa_mem_norm, xa_w_q=v_xa_w_q, xa_w_kv=v_xa_w_kv, xa_q_gain=v_xa_q_gain, xa_k_gain=v_xa_k_gain, xa_w_o=v_xa_w_o, ffn2_norm=v_ffn2_norm, ffn2_w_gu=v_ffn2_w_gu, ffn2_w_down=v_ffn2_w_down)

    c = lax.axis_index("c").astype(jnp.int32).reshape(1)
    loss, dx, grads = _local_step(x[0], mem[0], loss_target[0], w, _Exchange(w, c))
    loss = lax.psum(loss, ("x", "y", "c"))

    delta, new_m, new_v = {}, {}, {}
    for n in _WEIGHTS:
        delta[n], new_m[n], new_v[n] = adamw(w[n], grads[n], m[n], v[n], "adamw_" + n)
    return (loss, dx[None], *[grads[n] for n in _WEIGHTS], *[delta[n] for n in _WEIGHTS],
            *[new_m[n] for n in _WEIGHTS], *[new_v[n] for n in _WEIGHTS])
```

```python
import functools

import numpy as np
import jax
import jax.numpy as jnp
from jax import lax
from jax.experimental import pallas as pl
from jax.experimental.pallas import tpu as pltpu

F32 = jnp.float32
BF16 = jnp.bfloat16
MESH = pl.DeviceIdType.MESH

HEAD_DIM = 64
BLOCK = 128
RMS_EPS = 1e-6
A_Q_HEADS, A_KV_HEADS = 8, 2
B_HEADS = 8
C_HEADS = 16
C_PATTERNS = ((128, 1), (512, 4), (2048, 16))
X_HEADS = 4
N_DEV = 8
LANES = 1024
VMEM_LIMIT_BYTES = 56 * 1024 * 1024
SB_SKIP_LOG = -110.0
NEG_BIG = -1e30

ADAM_LR, ADAM_B1, ADAM_B2, ADAM_EPS, ADAM_WD, ADAM_STEP = 0.001, 0.9, 0.999, 1e-08, 0.01, 10

NN = (((1,), (0,)), ((), ()))
NT = (((1,), (1,)), ((), ()))
TN = (((0,), (0,)), ((), ()))


class Side:
    def __init__(self, arrays, out_shapes, n_remote, n_local, plan, aliased=False):
        self.arrays, self.out_shapes, self.plan, self.aliased = list(arrays), list(out_shapes), plan, aliased
        self.sems = [pltpu.SemaphoreType.DMA((n_remote,)), pltpu.SemaphoreType.DMA((n_remote,)),
                     pltpu.SemaphoreType.DMA((max(n_local, 1),))]

    def start(self, ins, outs, sems):
        local, sends, _ = self.plan(ins, outs, *sems)
        for make in local + sends:
            make().start()

    def wait(self, ins, outs, sems):
        local, sends, recvs = self.plan(ins, outs, *sems)
        for make in sends:
            make().wait_send()
        for make in recvs:
            make().wait_recv()
        for make in local:
            make().wait()


def _pcall(body, side=None, **kw):
    if side is None:
        return pl.pallas_call(body, **kw)
    grid = kw["grid"]
    single = not isinstance(kw["out_specs"], (list, tuple))
    out_specs = [kw["out_specs"]] if single else list(kw["out_specs"])
    out_shape = [kw["out_shape"]] if single else list(kw["out_shape"])
    scratch = list(kw.get("scratch_shapes", []))
    n_in, n_out, n_scr, n_side = len(kw["in_specs"]), len(out_specs), len(scratch), len(side.arrays)
    n_sout = len(side.out_shapes)

    def hosted(*refs):
        ins, s_in = refs[:n_in], refs[n_in:n_in + n_side]
        outs = refs[n_in + n_side:n_in + n_side + n_out]
        s_out = refs[n_in + n_side + n_out:n_in + n_side + n_out + n_sout]
        rest = refs[n_in + n_side + n_out + n_sout:]
        scr, sems = rest[:n_scr], rest[n_scr:]
        first = last = None
        for a, size in enumerate(grid):
            f, l = pl.program_id(a) == 0, pl.program_id(a) == size - 1
            first = f if first is None else jnp.logical_and(first, f)
            last = l if last is None else jnp.logical_and(last, l)

        @pl.when(first)
        def _():
            side.start(s_in, s_out, sems)

        body(*ins, *outs, *scr)

        @pl.when(last)
        def _():
            side.wait(s_in, s_out, sems)

    any_space = pl.BlockSpec(memory_space=pl.ANY)
    kw2 = dict(kw)
    kw2.update(in_specs=list(kw["in_specs"]) + [any_space] * n_side, out_specs=out_specs + [any_space] * n_sout,
               out_shape=out_shape + side.out_shapes, scratch_shapes=scratch + side.sems)
    if side.aliased:
        kw2["input_output_aliases"] = {n_in + i: n_out + i for i in range(n_side)}
    call = pl.pallas_call(hosted, **kw2)

    def run(*args):
        res = call(*args, *side.arrays)
        return (res[0] if single else list(res[:n_out])), list(res[n_out:])

    return run


def _params(**kw):
    return pltpu.CompilerParams(vmem_limit_bytes=VMEM_LIMIT_BYTES, **kw)


def _tile(dim, cap, unit=128):
    if dim <= cap:
        return dim
    t = (cap // unit) * unit
    while t >= unit:
        if dim % t == 0:
            return t
        t -= unit
    raise ValueError(f"no tile for {dim} under {cap}")


def _dot(a, b, dims):
    return lax.dot_general(a.astype(BF16), b.astype(BF16), dims, preferred_element_type=F32)


@functools.partial(jax.custom_vjp, nondiff_argnums=(2,))
def _dot_vjp(a, b, nt):
    return _dot(a, b, NT if nt else NN)


def _dot_vjp_fwd(a, b, nt):
    return _dot(a, b, NT if nt else NN), (a.astype(BF16), b.astype(BF16))


def _dot_vjp_bwd(nt, res, g):
    a, b = res
    if nt:
        return _dot(g, b, NN), _dot(g, a, TN)
    return _dot(g, b, NT), _dot(a, g, TN)


_dot_vjp.defvjp(_dot_vjp_fwd, _dot_vjp_bwd)


def _plain_dot(a, b, nt):
    return _dot(a, b, NT if nt else NN)


def _split_dot(x, mat, terms=2):
    out, rem = None, x
    for t in range(terms):
        part = rem.astype(BF16)
        d = lax.dot_general(part, mat, NN, preferred_element_type=F32)
        out = d if out is None else out + d
        if t + 1 < terms:
            rem = rem - part.astype(F32)
    return out


@functools.partial(jax.custom_vjp, nondiff_argnums=(3,))
def _split_dot_vjp(x, mat, mat_t, terms):
    return _split_dot(x, mat, terms)


def _split_dot_vjp_fwd(x, mat, mat_t, terms):
    return _split_dot(x, mat, terms), mat_t


def _split_dot_vjp_bwd(terms, mat_t, g):
    return _split_dot(g, mat_t, terms), None, None


_split_dot_vjp.defvjp(_split_dot_vjp_fwd, _split_dot_vjp_bwd)


def _plain_split(x, mat, mat_t, terms):
    return _split_dot(x, mat, terms)


def _tri(after):
    j = lax.broadcasted_iota(jnp.int32, (BLOCK, BLOCK), 0)
    s = lax.broadcasted_iota(jnp.int32, (BLOCK, BLOCK), 1)
    return jnp.where(j > s if after else j < s, 1.0, 0.0).astype(BF16)


def _in(a, block, imap):
    return (a, block, imap)


def _out(shape, dtype, block, imap, acc=False):
    return (shape, dtype, block, imap, acc)


def tcall(fn, grid, ins, outs, name, scratch=None, side=None):
    nin = len(ins)
    nout = len(outs)
    ngrid = len(grid)

    def body(*refs):
        ids = tuple(pl.program_id(a) for a in range(ngrid))
        extra = {} if scratch is None else {"scratch": refs[nin + nout]}
        res = fn(ids, *[r[...] for r in refs[:nin]], **extra)
        first = ids[0] == 0
        for a in range(1, ngrid):
            first = jnp.logical_and(first, ids[a] == 0)
        for o_ref, r, spec in zip(refs[nin:nin + nout], res, outs):
            if spec[4]:
                @pl.when(first)
                def _(o_ref=o_ref):
                    o_ref[...] = jnp.zeros(o_ref.shape, o_ref.dtype)
                o_ref[...] += r.astype(o_ref.dtype)
            else:
                o_ref[...] = r.astype(o_ref.dtype)

    return _pcall(
        body, side=side, name=name, grid=grid,
        in_specs=[pl.BlockSpec(b, m) for (_, b, m) in ins],
        out_specs=[pl.BlockSpec(b, m) for (_, _, b, m, _) in outs],
        out_shape=[jax.ShapeDtypeStruct(s, d) for (s, d, _, _, _) in outs],
        scratch_shapes=[] if scratch is None else [pltpu.VMEM(*scratch)],
        compiler_params=_params(),
    )(*[a for (a, _, _) in ins])


def _to_strided(scr, nat, d):
    if d == 1:
        return nat
    t, w = nat.shape
    nc = w // BLOCK
    for c in range(nc):
        scr[c * t:(c + 1) * t, :] = nat[:, c * BLOCK:(c + 1) * BLOCK]
    return jnp.concatenate([scr[pl.ds(c * t + r, t // d, stride=d), :] for r in range(d) for c in range(nc)], axis=1)


def _to_natural(scr, st, d):
    if d == 1:
        return st.astype(F32)
    t, w = st.shape[0] * d, st.shape[1] // d
    nc = w // BLOCK
    st = st.astype(F32)
    for r in range(d):
        for c in range(nc):
            scr[pl.ds(c * t + r, t // d, stride=d), :] = st[:, r * w + c * BLOCK:r * w + (c + 1) * BLOCK]
    return jnp.concatenate([scr[c * t:(c + 1) * t, :] for c in range(nc)], axis=1)


def _row(a, tm, width=None, cb=0):
    width = a.shape[1] if width is None else width
    return _in(a, (tm, width), lambda i, cb=cb: (i, cb))


def _full(a):
    zeros = (0,) * a.ndim
    return _in(a, a.shape, lambda *ids: zeros)


def _row_out(n, width, dtype, tm):
    return _out((n, width), dtype, (tm, width), lambda i: (i, 0))


def _acc_out(shape):
    zeros = (0,) * len(shape)
    return _out(shape, F32, shape, lambda *ids: zeros, acc=True)


def mm(a, b, mode, name, *, out_dtype=F32, scale=1.0, res=None, side=None):
    if mode == "nn":
        (m, k), (k2, n) = a.shape, b.shape
    elif mode == "nt":
        (m, k), (n, k2) = a.shape, b.shape
    else:
        (k, m), (k2, n) = a.shape, b.shape
    assert k == k2, (a.shape, b.shape, mode)
    tm, tn, tk = _tile(m, 1408 if mode == "tn" else 512), _tile(n, 1408), _tile(k, 1408)
    nk = k // tk
    dims = {"nn": NN, "nt": NT, "tn": TN}[mode]
    has_res = res is not None

    def body(*refs):
        if has_res:
            a_ref, b_ref, r_ref, o_ref, acc_ref = refs
        else:
            a_ref, b_ref, o_ref, acc_ref = refs
        kk = pl.program_id(2)

        @pl.when(kk == 0)
        def _():
            acc_ref[...] = jnp.zeros(acc_ref.shape, F32)

        acc_ref[...] += _dot(a_ref[...], b_ref[...], dims)

        @pl.when(kk == nk - 1)
        def _():
            out = acc_ref[...]
            if scale != 1.0:
                out = out * scale
            if has_res:
                out = out + r_ref[...]
            o_ref[...] = out.astype(o_ref.dtype)

    a_spec = (pl.BlockSpec((tk, tm), lambda i, j, kk: (kk, i)) if mode == "tn"
              else pl.BlockSpec((tm, tk), lambda i, j, kk: (i, kk)))
    b_spec = (pl.BlockSpec((tn, tk), lambda i, j, kk: (j, kk)) if mode == "nt"
              else pl.BlockSpec((tk, tn), lambda i, j, kk: (kk, j)))
    in_specs = [a_spec, b_spec]
    args = [a, b]
    if has_res:
        in_specs.append(pl.BlockSpec((tm, tn), lambda i, j, kk: (i, j)))
        args.append(res)
    order = ("parallel", "parallel", "arbitrary") if side is None else ("arbitrary",) * 3
    return _pcall(
        body, side=side, name=name, grid=(m // tm, n // tn, nk),
        in_specs=in_specs,
        out_specs=pl.BlockSpec((tm, tn), lambda i, j, kk: (i, j)),
        out_shape=jax.ShapeDtypeStruct((m, n), out_dtype),
        scratch_shapes=[pltpu.VMEM((tm, tn), F32)],
        compiler_params=_params(dimension_semantics=order),
    )(*args)


def _rms(x, g):
    return x * lax.rsqrt(jnp.mean(x * x, axis=-1, keepdims=True) + RMS_EPS) * g


def _silu_mul(gate, up):
    return gate / (1.0 + jnp.exp(-gate)) * up


def mm_gate_up(h, w_gu, name, side=None):
    m, k = h.shape
    f = w_gu.shape[0] // 2
    tm, tn = _tile(m, 512), _tile(f, 1408)
    nj = f // tn
    assert k <= 1408

    def body(h_ref, wg_ref, wu_ref, g_ref, u_ref, a_ref):
        ht = h_ref[...]
        for lo in range(0, tn, 512):
            cols = slice(lo, min(lo + 512, tn))
            gate, up = _dot(ht, wg_ref[cols, :], NT), _dot(ht, wu_ref[cols, :], NT)
            g_ref[:, cols] = gate.astype(g_ref.dtype)
            u_ref[:, cols] = up.astype(u_ref.dtype)
            a_ref[:, cols] = _silu_mul(gate, up).astype(a_ref.dtype)

    tile = pl.BlockSpec((tm, tn), lambda i, j: (i, j))
    return _pcall(
        body, side=side, name=name, grid=(m // tm, nj),
        in_specs=[pl.BlockSpec((tm, k), lambda i, j: (i, 0)),
                  pl.BlockSpec((tn, k), lambda i, j: (j, 0)),
                  pl.BlockSpec((tn, k), lambda i, j: (j + nj, 0))],
        out_specs=[tile, tile, tile],
        out_shape=[jax.ShapeDtypeStruct((m, f), BF16), jax.ShapeDtypeStruct((m, f), BF16),
                   jax.ShapeDtypeStruct((m, f), BF16)],
        compiler_params=_params(dimension_semantics=("arbitrary",) * 2),
    )(h, w_gu, w_gu)


def mm_down_act_bwd(dy, w_down, gate, up, name, side=None):
    m, d = dy.shape
    f = w_down.shape[0]
    tm, tn = _tile(m, 512), _tile(f, 1408)
    assert d <= 1408

    def body(dy_ref, w_ref, g_ref, u_ref, dg_ref, du_ref):
        dyt = dy_ref[...].astype(BF16)
        for lo in range(0, tn, 512):
            cols = slice(lo, min(lo + 512, tn))
            da = _dot(dyt, w_ref[cols, :], NT) * 0.5
            gate, up = g_ref[:, cols].astype(F32), u_ref[:, cols].astype(F32)
            s = 1.0 / (1.0 + jnp.exp(-gate))
            gs = gate * s
            du_ref[:, cols] = (da * gs).astype(du_ref.dtype)
            dg_ref[:, cols] = (da * up * s * (1.0 + gate - gs)).astype(dg_ref.dtype)

    tile = pl.BlockSpec((tm, tn), lambda i, j: (i, j))
    return _pcall(
        body, side=side, name=name, grid=(m // tm, f // tn),
        in_specs=[pl.BlockSpec((tm, d), lambda i, j: (i, 0)), pl.BlockSpec((tn, d), lambda i, j: (j, 0)), tile, tile],
        out_specs=[tile, tile],
        out_shape=[jax.ShapeDtypeStruct((m, f), BF16), jax.ShapeDtypeStruct((m, f), BF16)],
        compiler_params=_params(dimension_semantics=("arbitrary", "arbitrary")),
    )(dy, w_down, gate, up)


def mm_norm_bwd(a, b, x, g, dres, name, b_kd=False):
    halves = isinstance(a, (tuple, list))
    a0, a1 = a if halves else (a, None)
    m, k = a0.shape[0], a0.shape[1] * (2 if halves else 1)
    d = b.shape[1] if b_kd else b.shape[0]
    dims = NN if b_kd else NT
    tm, tk = _tile(m, 512), _tile(a0.shape[1], 1408)
    nk = k // tk
    nkh = a0.shape[1] // tk
    has_res = dres is not None

    def body(*refs):
        a_ref, b_ref, x_ref, g_ref = refs[:4]
        rest = refs[4:-3]
        a1_ref = rest[0] if halves else None
        r_ref = rest[-1] if has_res else None
        dx_ref, dg_ref, acc_ref = refs[-3:]
        i, kk = pl.program_id(0), pl.program_id(1)

        @pl.when(kk == 0)
        def _():
            acc_ref[...] = jnp.zeros(acc_ref.shape, F32)

        if halves:
            @pl.when(kk < nkh)
            def _():
                acc_ref[...] += _dot(a_ref[...], b_ref[...], dims)

            @pl.when(kk >= nkh)
            def _():
                acc_ref[...] += _dot(a1_ref[...], b_ref[...], dims)
        else:
            acc_ref[...] += _dot(a_ref[...], b_ref[...], dims)

        @pl.when(kk == nk - 1)
        def _():
            _, vjp = jax.vjp(_rms, x_ref[...], g_ref[...])
            dx, dg = vjp(acc_ref[...])
            dx_ref[...] = dx + r_ref[...] if has_res else dx

            @pl.when(i == 0)
            def _():
                dg_ref[...] = jnp.zeros(dg_ref.shape, F32)

            dg_ref[...] += dg

    rows = pl.BlockSpec((tm, d), lambda i, kk: (i, 0))
    first = pl.BlockSpec((tm, tk), lambda i, kk: (i, jnp.minimum(kk, nkh - 1)))
    second = pl.BlockSpec((tm, tk), lambda i, kk: (i, jnp.maximum(kk - nkh, 0)))
    b_spec = pl.BlockSpec((tk, d), lambda i, kk: (kk, 0)) if b_kd else pl.BlockSpec((d, tk), lambda i, kk: (0, kk))
    in_specs = ([first, b_spec, rows, pl.BlockSpec(g.shape, lambda i, kk: (0, 0))]
                + ([second] if halves else []) + ([rows] if has_res else []))
    return _pcall(
        body, name=name, grid=(m // tm, nk),
        in_specs=in_specs,
        out_specs=[rows, pl.BlockSpec(g.shape, lambda i, kk: (0, 0))],
        out_shape=[jax.ShapeDtypeStruct((m, d), F32), jax.ShapeDtypeStruct(g.shape, F32)],
        scratch_shapes=[pltpu.VMEM((tm, d), F32)],
        compiler_params=_params(dimension_semantics=("arbitrary", "arbitrary")),
    )(*([a0, b, x, g] + ([a1] if halves else []) + ([dres] if has_res else [])))


def _indicator(shape, head_axis, mod):
    lane = lax.broadcasted_iota(jnp.int32, shape, head_axis)
    other = lax.broadcasted_iota(jnp.int32, shape, 1 - head_axis)
    lane = jnp.bitwise_and(lane, HEAD_DIM - 1) if mod else jnp.right_shift(lane, 6)
    return jnp.where(lane == other, 1.0, 0.0).astype(BF16)


def _head_rms(split, xs, g):
    w = xs.shape[1]
    to_head, from_head = _indicator((w, BLOCK), 0, False), _indicator((BLOCK, w), 1, False)
    to_lane, from_lane = _indicator((HEAD_DIM, w), 1, True), _indicator((w, HEAD_DIM), 0, True)
    ss = split(xs * xs, to_head, from_head, 3)
    r = lax.rsqrt(ss * (1.0 / HEAD_DIM) + RMS_EPS)
    g_all = split(jnp.broadcast_to(g, (8, HEAD_DIM)), to_lane, from_lane, 3)[0:1]
    return xs * split(r, from_head, to_head, 3) * g_all


def _prep(split, x, qg, kg, segs):
    parts = []
    for start, width, kind in segs:
        xs = x[:, start:start + width]
        parts.append(xs if kind == "raw" else _head_rms(split, xs, qg if kind == "q" else kg))
    return jnp.concatenate(parts, axis=1)


def prep_fwd(x, qg, kg, segs, dils, name):
    n, w = x.shape
    tm = _tile(n, 256, 8)

    def fn(ids, xt, a, b, scratch):
        ops = _prep(_plain_split, xt, a, b, segs)
        return tuple(_to_strided(scratch, ops, d) for d in dils)

    return tcall(fn, (n // tm,), [_row(x, tm), _full(qg), _full(kg)],
                 [_out((n // d, d * w), BF16, (tm // d, d * w), lambda i: (i, 0)) for d in dils], name,
                 scratch=((w // BLOCK * tm, BLOCK), F32))


def prep_bwd(x, qg, kg, segs, grads, gather, name):
    n, w = x.shape
    tm = BLOCK
    nblk = n // tm
    nslot = 1 + max(slot for _, _, _, slot in grads)

    def fn(ids, xt, a, b, *t, scratch):
        tiles, dils = [None] * nslot, [None] * nslot
        for ti, (_, sh, d, slot) in zip(t, grads):
            ti = jnp.where(ids[0] + sh < nblk, ti, 0.0) if sh else ti
            tiles[slot] = ti if tiles[slot] is None else tiles[slot] + ti
            dils[slot] = d
        tiles = [_to_natural(scratch, ti, d) for ti, d in zip(tiles, dils)]
        _, vjp = jax.vjp(lambda x_, a_, b_: _prep(_split_dot_vjp, x_, a_, b_, segs), xt, a, b)
        return vjp(gather(*tiles))

    specs = [_in(a, (tm // d, a.shape[1]), (lambda i, sh=sh: (jnp.minimum(i + sh, nblk - 1), 0)))
             for a, sh, d, _ in grads]
    wmax = max(a.shape[1] // d for a, _, d, _ in grads)
    return tcall(fn, (nblk,), [_row(x, tm), _full(qg), _full(kg)] + specs,
                 [_row_out(n, w, BF16, tm), _acc_out(qg.shape), _acc_out(kg.shape)], name,
                 scratch=((wmax // BLOCK * tm, BLOCK), F32))


def rmsnorm_fwd(x, g, name, side=None):
    n, d = x.shape
    tm = _tile(n, 512, 8)
    res = tcall(lambda ids, xt, gt: (_rms(xt, gt),), (n // tm,), [_row(x, tm), _full(g)],
                [_row_out(n, d, BF16, tm)], name, side=side)
    if side is None:
        return res[0]
    return res[0][0], res[1]


def ffn_fwd(x, g, w_gu, w_down, tag, carry=None):
    h = rmsnorm_fwd(x, g, tag + "_norm")
    if carry is None:
        gate, up, a = mm_gate_up(h, w_gu, tag + "_gu")
        return mm(a, w_down, "nn", tag + "_down", scale=0.5, res=x), (x, h, gate, up, a)
    phase, bufs = carry
    (gate, up, a), bufs = mm_gate_up(h, w_gu, tag + "_gu", side=gather_side(phase, bufs))
    y, bufs = mm(a, w_down, "nn", tag + "_down", scale=0.5, res=x, side=gather_side(phase + 1, bufs))
    return y, (x, h, gate, up, a), bufs


def ffn_bwd(dy, saved, g, w_gu, w_down, tag, chain=None):
    x, h, gate, up, a = saved

    def carrying(name, call, **kw):
        side = None if chain is None else chain.side(name)
        out = call(name=tag + "_" + name, side=side, **kw)
        if side is None:
            return out
        chain.done(name, out[1])
        return out[0]

    dgate, dup = carrying("da", mm_down_act_bwd, dy=dy, w_down=w_down, gate=gate, up=up)
    d_wdown = carrying("dwd", mm, a=a, b=dy, mode="tn", scale=0.5)
    d_wgu = (carrying("dwgu", mm, a=dgate, b=h, mode="tn"), mm(dup, h, "tn", tag + "_dwup"))
    dx, dg = mm_norm_bwd((dgate, dup), w_gu, x, g, dy, tag + "_dh", b_kd=True)
    return dx, dg, d_wgu, d_wdown


def _alibi(n_heads):
    return [float(s) for s in np.asarray(2.0 ** (-8.0 * np.arange(1, n_heads + 1) / n_heads), dtype=np.float32)]


def _banded_tile(dot, first, q, kp, kc, vp, vc, sinks, *, hkv, grp, max_dist, step, slopes, want_lse):
    row = lax.broadcasted_iota(jnp.int32, (BLOCK, 2 * BLOCK), 0)
    col = lax.broadcasted_iota(jnp.int32, (BLOCK, 2 * BLOCK), 1)
    dist = row + BLOCK - col
    valid = (dist >= 0) & (dist <= max_dist) & ((col >= BLOCK) | jnp.logical_not(first))
    distf = dist.astype(F32)

    def head(hd, qh, k2, v2):
        s = dot(qh, k2, True) * (HEAD_DIM ** -0.5)
        s = jnp.where(valid, s - (slopes[hd] * step) * distf, NEG_BIG)
        m = jnp.max(s, axis=-1, keepdims=True)
        if sinks is not None:
            pick = lax.broadcasted_iota(jnp.int32, sinks.shape, 1) == hd
            sk = jnp.sum(jnp.where(pick, sinks, 0.0), axis=1, keepdims=True)
            m = jnp.maximum(m, sk)
        m = lax.stop_gradient(m)
        p = jnp.exp(s - m)
        denom = jnp.sum(p, axis=-1, keepdims=True)
        if sinks is not None:
            denom = denom + jnp.exp(sk - m)
        return dot(p * (1.0 / denom), v2, False), m + jnp.log(denom)

    outs, lses = [], []
    if grp == 1:
        low = lax.broadcasted_iota(jnp.int32, (BLOCK, BLOCK), 1) < HEAD_DIM
        for pr in range(hkv // 2):
            sl = slice(pr * BLOCK, (pr + 1) * BLOCK)
            q2 = q[:, sl]
            k2 = jnp.concatenate([kp[:, sl], kc[:, sl]], axis=0)
            v2 = jnp.concatenate([vp[:, sl], vc[:, sl]], axis=0)
            o0, l0 = head(2 * pr, jnp.where(low, q2, 0.0), k2, v2)
            o1, l1 = head(2 * pr + 1, jnp.where(low, 0.0, q2), k2, v2)
            outs.append(jnp.where(low, o0, o1))
            lses.append(jnp.where(low, l0, l1))
    else:
        for hk in range(hkv):
            sl = slice(hk * HEAD_DIM, (hk + 1) * HEAD_DIM)
            k2 = jnp.concatenate([kp[:, sl], kc[:, sl]], axis=0)
            v2 = jnp.concatenate([vp[:, sl], vc[:, sl]], axis=0)
            for gi in range(grp):
                hd = hk * grp + gi
                o_h, l_h = head(hd, q[:, hd * HEAD_DIM:(hd + 1) * HEAD_DIM], k2, v2)
                outs.append(o_h)
                lses.append(jnp.broadcast_to(l_h, (BLOCK, HEAD_DIM)))
    o = jnp.concatenate(outs, axis=1)
    if want_lse:
        return o, jnp.concatenate(lses, axis=1)
    return (o,)


def _banded_specs(view, qcol, kcol, vcol, wq, wkv):
    def at(colfn, prev):
        if prev:
            return lambda r, n: (jnp.maximum(n - 1, 0), colfn(r))
        return lambda r, n: (n, colfn(r))
    return [
        _in(view, (BLOCK, wq), at(qcol, False)),
        _in(view, (BLOCK, wkv), at(kcol, True)),
        _in(view, (BLOCK, wkv), at(kcol, False)),
        _in(view, (BLOCK, wkv), at(vcol, True)),
        _in(view, (BLOCK, wkv), at(vcol, False)),
    ]


def banded_fwd(view, dil, cols, sinks, cfg, name):
    ns = view.shape[0]
    nb = ns // BLOCK
    wq, wkv = cfg["hkv"] * cfg["grp"] * HEAD_DIM, cfg["hkv"] * HEAD_DIM
    has_sinks = sinks is not None

    def fn(ids, q, kp, kc, vp, vc, *rest):
        q, kp, kc, vp, vc = [a.astype(F32) for a in (q, kp, kc, vp, vc)]
        return _banded_tile(_plain_dot, ids[1] == 0, q, kp, kc, vp, vc, rest[0] if has_sinks else None, **cfg)

    ins = _banded_specs(view, *cols, wq, wkv) + ([_full(sinks)] if has_sinks else [])
    outs = [_out((ns, dil * wq), F32 if cfg["want_lse"] else BF16, (BLOCK, wq), lambda r, n: (n, r))]
    if cfg["want_lse"]:
        outs.append(_out((ns, dil * wq), F32, (BLOCK, wq), lambda r, n: (n, r)))
    return tcall(fn, (dil, nb), ins, outs, name)


def banded_bwd(view, dil, cols, sinks, cfg, cts, name):
    ns = view.shape[0]
    nb = ns // BLOCK
    wq, wkv = cfg["hkv"] * cfg["grp"] * HEAD_DIM, cfg["hkv"] * HEAD_DIM
    has_sinks = sinks is not None
    assert len(cts) == (2 if cfg["want_lse"] else 1)

    def fn(ids, q, kp, kc, vp, vc, *rest):
        sk = rest[0] if has_sinks else None
        ct = rest[1 if has_sinks else 0:]
        first = ids[1] == 0

        def f(q, kp, kc, vp, vc, *s):
            return _banded_tile(_dot_vjp, first, q, kp, kc, vp, vc, s[0] if has_sinks else None, **cfg)

        prim = tuple(a.astype(F32) for a in (q, kp, kc, vp, vc)) + ((sk,) if has_sinks else ())
        _, vjp = jax.vjp(f, *prim)
        return vjp(tuple(c.astype(F32) for c in ct))

    ins = (_banded_specs(view, *cols, wq, wkv) + ([_full(sinks)] if has_sinks else [])
           + [_in(a, (BLOCK, wq), (lambda r, n, cf=cf: (n, cf(r)))) for (a, cf) in cts])
    blk = lambda w: _out((ns, dil * w), F32, (BLOCK, w), lambda r, n: (n, r))
    outs = [blk(wq), blk(wkv), blk(wkv), blk(wkv), blk(wkv)]
    if has_sinks:
        outs.append(_acc_out(sinks.shape))
    return tcall(fn, (dil, nb), ins, outs, name)


def _log_sigmoid(z):
    return jnp.minimum(z, 0.0) - jnp.log(1.0 + jnp.exp(-jnp.abs(z)))


SB_PAIRS = 4


def _sb_pair(dot, suffix, qh, kb, vb, r_in, mask):
    z = dot(qh, kb, True) * (HEAD_DIM ** -0.5)
    lsp = _log_sigmoid(z)
    log_keep = jnp.where(mask, lsp - z, 0.0)
    log_after = suffix(log_keep) + r_in
    a = jnp.where(mask, jnp.exp(lsp + log_after), 0.0)
    return dot(a, vb, False), r_in + jnp.sum(log_keep, axis=1, keepdims=True)


def sb_fwd(qkv, qcb, kcb, vcb, name, side=None):
    s = qkv.shape[0]
    nb = s // BLOCK
    pairs = B_HEADS // 2
    wide = SB_PAIRS * BLOCK
    assert pairs % SB_PAIRS == 0 and qcb % SB_PAIRS == 0 and kcb % SB_PAIRS == 0 and vcb % SB_PAIRS == 0

    def body(q_ref, k_ref, v_ref, o_ref):
        n = pl.program_id(1)
        low = lax.broadcasted_iota(jnp.int32, (BLOCK, BLOCK), 1) < HEAD_DIM
        before = (lax.broadcasted_iota(jnp.int32, (2 * BLOCK, BLOCK), 1)
                  < jnp.bitwise_and(lax.broadcasted_iota(jnp.int32, (2 * BLOCK, BLOCK), 0), BLOCK - 1))
        after = _tri(True)
        suffix = lambda t: _split_dot(t, after)
        qs = []
        for p in range(SB_PAIRS):
            q2 = q_ref[:, p * BLOCK:(p + 1) * BLOCK].astype(F32)
            qs.append(jnp.concatenate([jnp.where(low, q2, 0.0), jnp.where(low, 0.0, q2)], axis=0))

        def cond(c):
            return jnp.logical_and(c[0] >= 0, c[1] > SB_SKIP_LOG)

        def step(c):
            kb, _, rs, accs = c
            rows = pl.ds(pl.multiple_of(kb * BLOCK, BLOCK), BLOCK)
            mask = jnp.logical_or(before, kb != n)
            new_r, new_acc, top = [], [], None
            for p in range(SB_PAIRS):
                cols = slice(p * BLOCK, (p + 1) * BLOCK)
                o_part, r_out = _sb_pair(_plain_dot, suffix, qs[p], k_ref[rows, cols], v_ref[rows, cols], rs[p], mask)
                new_r.append(r_out)
                new_acc.append(accs[p] + o_part)
                top = jnp.max(r_out) if top is None else jnp.maximum(top, jnp.max(r_out))
            return kb - 1, top, tuple(new_r), tuple(new_acc)

        init = (n, jnp.float32(0.0), tuple(jnp.zeros((2 * BLOCK, 1), F32) for _ in range(SB_PAIRS)),
                tuple(jnp.zeros((2 * BLOCK, BLOCK), F32) for _ in range(SB_PAIRS)))
        accs = lax.while_loop(cond, step, init)[3]
        for p in range(SB_PAIRS):
            o_ref[:, p * BLOCK:(p + 1) * BLOCK] = jnp.where(low, accs[p][:BLOCK], accs[p][BLOCK:]).astype(o_ref.dtype)

    return _pcall(
        body, side=side, name=name, grid=(pairs // SB_PAIRS, nb),
        in_specs=[pl.BlockSpec((BLOCK, wide), lambda g, n: (n, qcb // SB_PAIRS + g)),
                  pl.BlockSpec((s, wide), lambda g, n: (0, kcb // SB_PAIRS + g), pipeline_mode=pl.Buffered(1)),
                  pl.BlockSpec((s, wide), lambda g, n: (0, vcb // SB_PAIRS + g), pipeline_mode=pl.Buffered(1))],
        out_specs=pl.BlockSpec((BLOCK, wide), lambda g, n: (n, g)),
        out_shape=jax.ShapeDtypeStruct((s, pairs * BLOCK), BF16),
        compiler_params=_params(),
    )(qkv, qkv, qkv)


def sb_bwd(qkv, qcb, kcb, vcb, do, docb, name, side=None):
    s = qkv.shape[0]
    nb = s // BLOCK
    pairs = B_HEADS // 2
    wide = SB_PAIRS * BLOCK
    assert docb % SB_PAIRS == 0

    def body(q_ref, k_ref, v_ref, do_ref, dq_ref, dk_ref, dv_ref, r_ref):
        n = pl.program_id(1)

        @pl.when(n == 0)
        def _():
            dk_ref[...] = jnp.zeros(dk_ref.shape, F32)
            dv_ref[...] = jnp.zeros(dv_ref.shape, F32)

        low = lax.broadcasted_iota(jnp.int32, (BLOCK, BLOCK), 1) < HEAD_DIM
        before = (lax.broadcasted_iota(jnp.int32, (2 * BLOCK, BLOCK), 1)
                  < jnp.bitwise_and(lax.broadcasted_iota(jnp.int32, (2 * BLOCK, BLOCK), 0), BLOCK - 1))
        after, earlier = _tri(True), _tri(False)
        suffix = lambda t: _split_dot_vjp(t, after, earlier, 2)
        stack = lambda t: jnp.concatenate([jnp.where(low, t, 0.0), jnp.where(low, 0.0, t)], axis=0)
        qs = [stack(q_ref[:, p * BLOCK:(p + 1) * BLOCK].astype(F32)) for p in range(SB_PAIRS)]
        dos = [stack(do_ref[:, p * BLOCK:(p + 1) * BLOCK].astype(F32)) for p in range(SB_PAIRS)]

        def cond(c):
            return jnp.logical_and(c[0] >= 0, c[1] > SB_SKIP_LOG)

        def down(c):
            kb, _, rs = c
            rows = pl.ds(pl.multiple_of(kb * BLOCK, BLOCK), BLOCK)
            mask = jnp.logical_or(before, kb != n)
            new_r, top = [], None
            for h in range(SB_PAIRS):
                cols = slice(h * BLOCK, (h + 1) * BLOCK)
                r_ref[h, kb] = rs[h]
                z = _dot(qs[h], k_ref[rows, cols], NT) * (HEAD_DIM ** -0.5)
                log_keep = jnp.where(mask, _log_sigmoid(z) - z, 0.0)
                r_out = rs[h] + jnp.sum(log_keep, axis=1, keepdims=True)
                new_r.append(r_out)
                top = jnp.max(r_out) if top is None else jnp.maximum(top, jnp.max(r_out))
            return kb - 1, top, tuple(new_r)

        init = (n, jnp.float32(0.0), tuple(jnp.zeros((2 * BLOCK, 1), F32) for _ in range(SB_PAIRS)))
        last = lax.while_loop(cond, down, init)[0] + 1

        def up(kb, c):
            dqs, g_rs = c
            rows = pl.ds(pl.multiple_of(kb * BLOCK, BLOCK), BLOCK)
            mask = jnp.logical_or(before, kb != n)
            new_dq, new_g = [], []
            for h in range(SB_PAIRS):
                cols = slice(h * BLOCK, (h + 1) * BLOCK)
                _, vjp = jax.vjp(lambda q_, k_, v_, r_: _sb_pair(_dot_vjp, suffix, q_, k_, v_, r_, mask),
                                 qs[h], k_ref[rows, cols].astype(F32), v_ref[rows, cols].astype(F32), r_ref[h, kb])
                dq_c, dk_c, dv_c, g_in = vjp((dos[h], g_rs[h]))
                dk_ref[rows, cols] += dk_c
                dv_ref[rows, cols] += dv_c
                new_dq.append(dqs[h] + dq_c)
                new_g.append(g_in)
            return tuple(new_dq), tuple(new_g)

        init = (tuple(jnp.zeros((2 * BLOCK, BLOCK), F32) for _ in range(SB_PAIRS)),
                tuple(jnp.zeros((2 * BLOCK, 1), F32) for _ in range(SB_PAIRS)))
        dqs = lax.fori_loop(last, n + 1, up, init)[0]
        for p in range(SB_PAIRS):
            dq_ref[:, p * BLOCK:(p + 1) * BLOCK] = jnp.where(low, dqs[p][:BLOCK], dqs[p][BLOCK:])

    full = jax.ShapeDtypeStruct((s, pairs * BLOCK), F32)
    return _pcall(
        body, side=side, name=name, grid=(pairs // SB_PAIRS, nb),
        in_specs=[pl.BlockSpec((BLOCK, wide), lambda g, n: (n, qcb // SB_PAIRS + g)),
                  pl.BlockSpec((s, wide), lambda g, n: (0, kcb // SB_PAIRS + g), pipeline_mode=pl.Buffered(1)),
                  pl.BlockSpec((s, wide), lambda g, n: (0, vcb // SB_PAIRS + g), pipeline_mode=pl.Buffered(1)),
                  pl.BlockSpec((BLOCK, wide), lambda g, n: (n, docb // SB_PAIRS + g))],
        out_specs=[pl.BlockSpec((BLOCK, wide), lambda g, n: (n, g)),
                   pl.BlockSpec((s, wide), lambda g, n: (0, g), pipeline_mode=pl.Buffered(1)),
                   pl.BlockSpec((s, wide), lambda g, n: (0, g), pipeline_mode=pl.Buffered(1))],
        out_shape=[full, full, full],
        scratch_shapes=[pltpu.VMEM((SB_PAIRS, nb, 2 * BLOCK, 1), F32)],
        compiler_params=_params(),
    )(qkv, qkv, qkv, do)


def _xa_tile(dot, q, kv, qg, kg):
    hd = q.shape[1] // X_HEADS
    outs = []
    for h in range(X_HEADS):
        qh = _rms(q[:, h * hd:(h + 1) * hd], qg)
        kh = _rms(kv[:, h * hd:(h + 1) * hd], kg)
        vh = kv[:, (X_HEADS + h) * hd:(X_HEADS + h + 1) * hd]
        sc = dot(qh, kh, True) * (hd ** -0.5)
        m = lax.stop_gradient(jnp.max(sc, axis=-1, keepdims=True))
        p = jnp.exp(sc - m)
        outs.append(dot(p * (1.0 / jnp.sum(p, axis=-1, keepdims=True)), vh, False))
    return jnp.concatenate(outs, axis=1)


def xa_core_fwd(q, kv, qg, kg, name):
    n, d = q.shape
    tm = _tile(n, 256, 8)
    (o,) = tcall(lambda ids, qt, kvt, qgt, kgt: (_xa_tile(_plain_dot, qt, kvt, qgt, kgt),), (n // tm,),
                 [_row(q, tm), _full(kv), _full(qg), _full(kg)], [_row_out(n, d, BF16, tm)], name)
    return o


def xa_core_bwd(q, kv, qg, kg, do, name):
    n, d = q.shape
    tm = _tile(n, 256, 8)

    def fn(ids, qt, kvt, qgt, kgt, dot_):
        _, vjp = jax.vjp(functools.partial(_xa_tile, _dot_vjp), qt, kvt, qgt, kgt)
        return vjp(dot_.astype(F32))

    return tcall(fn, (n // tm,), [_row(q, tm), _full(kv), _full(qg), _full(kg), _row(do, tm)],
                 [_row_out(n, d, BF16, tm), _acc_out(kv.shape), _acc_out(qg.shape), _acc_out(kg.shape)], name)


def _ev_reorder(a):
    return jnp.concatenate([a[0:512], a[768:2304], a[512:768]], axis=0)


def _ev_restore(a):
    return jnp.concatenate([a[0:512], a[2048:2304], a[512:2048]], axis=0)


_EV_SEGS = ((0, 512, "q"), (512, 1536, "raw"), (2048, 128, "k"), (2176, 128, "raw"))
_A_CFG = dict(hkv=A_KV_HEADS, grp=A_Q_HEADS // A_KV_HEADS, max_dist=BLOCK - 1, step=1.0, slopes=_alibi(A_Q_HEADS),
              want_lse=False)
_A_COLS = (lambda r: 0, lambda r: 16, lambda r: 17)


def even_mixer_fwd(x, h, w_in, qg, kg, sinks, w_out, tag, side=None):
    qkv = mm(h, w_in, "nt", tag + "_in")
    (ops,) = prep_fwd(qkv, qg, kg, _EV_SEGS, (1,), tag + "_prep")
    (o_a,) = banded_fwd(ops, 1, _A_COLS, sinks, _A_CFG, tag + "_swa")
    o_b = sb_fwd(ops, 4, 8, 12, tag + "_sb", side=side)
    carried = None
    if side is not None:
        o_b, carried = o_b
    o = jnp.concatenate([o_a, o_b], axis=1)
    y = mm(o, w_out, "nn", tag + "_out", res=x)
    return y, (x, h, qkv, ops, o), carried


def even_mixer_bwd(dy, saved, g, w_in, qg, kg, sinks, w_out, tag, side=None):
    x, h, qkv, ops, o = saved
    do = mm(dy, w_out, "nt", tag + "_do")
    d_wout = mm(o, dy, "tn", tag + "_dwout")
    dqa, dkp, dkc, dvp, dvc, dsinks = banded_bwd(ops, 1, _A_COLS, sinks, _A_CFG, [(do, lambda r: 0)], tag + "_dswa")
    res = sb_bwd(ops, 4, 8, 12, do, 4, tag + "_dsb", side=side)
    carried = None
    if side is not None:
        res, carried = res
    dqb, dkb, dvb = res
    dqkv, dqg, dkg = prep_bwd(
        qkv, qg, kg, _EV_SEGS,
        [(dqa, 0, 1, 0), (dqb, 0, 1, 1), (dkb, 0, 1, 2), (dvb, 0, 1, 3), (dkc, 0, 1, 4), (dkp, 1, 1, 4), (dvc, 0, 1, 5),
         (dvp, 1, 1, 5)],
        lambda *t: jnp.concatenate(t, axis=1), tag + "_dqkv")
    d_win = mm(dqkv, h, "tn", tag + "_dwin")
    dx, dg = mm_norm_bwd(dqkv, w_in, x, g, dy, tag + "_dh", b_kd=True)
    return dx, dg, d_win, dqg, dkg, dsinks, d_wout, carried


def _c_cfg(window, dil):
    return dict(hkv=C_HEADS, grp=1, max_dist=window // dil, step=float(dil), slopes=_alibi(C_HEADS), want_lse=True)


_C_COLS = (lambda r: 3 * r, lambda r: 3 * r + 1, lambda r: 3 * r + 2)
_OD_SEGS = ((0, 1024, "q"), (1024, 1024, "k"), (2048, 1024, "raw"))


def _combine(o1, o2, o3, l1, l2, l3):
    m = lax.stop_gradient(jnp.maximum(jnp.maximum(l1, l2), l3))
    e1, e2, e3 = jnp.exp(l1 - m), jnp.exp(l2 - m), jnp.exp(l3 - m)
    tot = e1 + e2 + e3
    return (e1 / tot) * o1 + (e2 / tot) * o2 + (e3 / tot) * o3


def odd_mixer_fwd(x, g, w_in, qg, kg, w_out, tag):
    n, d = x.shape
    h = rmsnorm_fwd(x, g, tag + "_norm")
    qkv = mm(h, w_in, "nt", tag + "_in")
    dils = [dil for _, dil in C_PATTERNS]
    ops = prep_fwd(qkv, qg, kg, _OD_SEGS, dils, tag + "_prep")
    os_, ls_ = [], []
    for (window, dil), ops_d in zip(C_PATTERNS, ops):
        o_p, l_p = banded_fwd(ops_d, dil, _C_COLS, None, _c_cfg(window, dil), f"{tag}_dil{dil}")
        os_.append(o_p)
        ls_.append(l_p)
    tm = BLOCK
    lay = lambda a, dil: _in(a, (tm // dil, a.shape[1]), lambda i: (i, 0))
    views = [lay(a, dil) for a, dil in zip(os_ + ls_, dils + dils)]

    def comb(ids, *t, scratch):
        return (_combine(*[_to_natural(scratch, a, dil) for a, dil in zip(t, dils + dils)]),)

    (o,) = tcall(comb, (n // tm,), views, [_row_out(n, d, BF16, tm)], tag + "_comb",
                 scratch=((d // BLOCK * tm, BLOCK), F32))
    y = mm(o, w_out, "nn", tag + "_out", res=x)
    return y, (x, h, qkv, ops, views, o)


def odd_mixer_bwd(dy, saved, g, w_in, qg, kg, w_out, tag):
    x, h, qkv, ops, views, o = saved
    n, d = x.shape
    do = mm(dy, w_out, "nt", tag + "_do")
    d_wout = mm(o, dy, "tn", tag + "_dwout")
    tm = BLOCK
    dils = [dil for _, dil in C_PATTERNS]

    def comb_bwd(ids, *t, scratch):
        _, vjp = jax.vjp(_combine, *[_to_natural(scratch, a, dil) for a, dil in zip(t[:6], dils + dils)])
        return tuple(_to_strided(scratch, c, dil) for c, dil in zip(vjp(t[6]), dils + dils))

    cts = tcall(comb_bwd, (n // tm,), views + [_row(do, tm)],
                [_out((n // dil, dil * d), F32, (tm // dil, dil * d), lambda i: (i, 0)) for dil in dils + dils],
                tag + "_dcomb", scratch=((d // BLOCK * tm, BLOCK), F32))
    dqs, dks, dvs = [], [], []
    for p, ((window, dil), ops_d) in enumerate(zip(C_PATTERNS, ops)):
        dq, dkp, dkc, dvp, dvc = banded_bwd(ops_d, dil, _C_COLS, None, _c_cfg(window, dil),
                                            [(cts[p], lambda r: r), (cts[3 + p], lambda r: r)], f"{tag}_ddil{dil}")
        dqs.append((dq, 0, dil, p))
        dks += [(dkc, 0, dil, 3 + p), (dkp, dil, dil, 3 + p)]
        dvs += [(dvc, 0, dil, 6 + p), (dvp, dil, dil, 6 + p)]

    def gather(*t):
        return jnp.concatenate([t[0] + t[1] + t[2], t[3] + t[4] + t[5], t[6] + t[7] + t[8]], axis=1)

    dqkv, dqg, dkg = prep_bwd(qkv, qg, kg, _OD_SEGS, dqs + dks + dvs, gather, tag + "_dqkv")
    d_win = mm(dqkv, h, "tn", tag + "_dwin")
    dx, dg = mm_norm_bwd(dqkv, w_in, x, g, dy, tag + "_dh", b_kd=True)
    return dx, dg, d_win, dqg, dkg, d_wout


def xa_fwd(x, mem, g, gm, w_q, w_kv, qg, kg, w_o, tag):
    h = rmsnorm_fwd(x, g, tag + "_norm")
    q = mm(h, w_q, "nn", tag + "_q")
    mn = rmsnorm_fwd(mem, gm, tag + "_mnorm")
    kv = mm(mn, w_kv, "nt", tag + "_kv")
    o = xa_core_fwd(q, kv, qg, kg, tag + "_core")
    y = mm(o, w_o, "nn", tag + "_o", res=x)
    return y, (x, h, q, mn, kv, o)


def xa_bwd(dy, saved, mem, g, gm, w_q, w_kv, qg, kg, w_o, tag):
    x, h, q, mn, kv, o = saved
    do = mm(dy, w_o, "nt", tag + "_do", out_dtype=BF16)
    d_wo = mm(o, dy, "tn", tag + "_dwo")
    dq, dkv, dqg, dkg = xa_core_bwd(q, kv, qg, kg, do, tag + "_dcore")
    d_wq = mm(h, dq, "tn", tag + "_dwq")
    dx, dg = mm_norm_bwd(dq, w_q, x, g, dy, tag + "_dh")
    d_wkv = mm(dkv, mn, "tn", tag + "_dwkv")
    _, dgm = mm_norm_bwd(dkv, w_kv, mem, gm, None, tag + "_dmn", b_kd=True)
    return dx, dg, dgm, d_wq, d_wkv, dqg, dkg, d_wo


def loss_head(y, target, name):
    n, d = y.shape
    tm = _tile(n, 512, 8)

    def fn(ids, yt, tt):
        e = yt - tt
        return e * (1.0 / d), jnp.sum(e * e, axis=0, keepdims=True)

    return tcall(fn, (n // tm,), [_row(y, tm), _row(target, tm)], [_row_out(n, d, F32, tm), _acc_out((1, d))], name)


_ANY = pl.BlockSpec(memory_space=pl.ANY)


def all_gather_blocks(blocks):
    nb = len(blocks)

    def body(*refs):
        x_refs, out_refs = refs[:nb], refs[nb:2 * nb]
        send_sems, recv_sems, local_sems = refs[2 * nb:]
        x, y, c = lax.axis_index("x"), lax.axis_index("y"), lax.axis_index("c")
        me, sibling = (x, y, c), (x, y, 1 - c)
        over_x, over_y, diagonal = (1 - x, y), (x, 1 - y), (1 - x, 1 - y)
        relay_of = ((1 - x) * (1 - c) + x * c, y * (1 - c) + (1 - y) * c)
        relay_to = (x * (1 - c) + (1 - x) * c, (1 - y) * (1 - c) + y * c)

        def copy(b, k, blk, to, own=False):
            px, py, pc = blk
            slot = out_refs[b].at[4 * px + 2 * py + pc]
            return pltpu.make_async_remote_copy(
                src_ref=x_refs[b] if own else slot, dst_ref=slot,
                send_sem=send_sems.at[7 * b + k], recv_sem=recv_sems.at[7 * b + k], device_id=to, device_id_type=MESH)

        mine = [pltpu.make_async_copy(x_refs[b], out_refs[b].at[4 * x + 2 * y + c], local_sems.at[b]) for b in range(nb)]
        for cp in mine:
            cp.start()
        sent = []
        for b in range(nb):
            sent += [copy(b, 0, me, sibling, own=True), copy(b, 1, me, (*over_x, c), own=True),
                     copy(b, 2, me, (*over_y, c), own=True)]
        for cp in sent:
            cp.start()
        for b in range(nb):
            copy(b, 1, (*over_x, c), me).wait_recv()
            copy(b, 2, (*over_y, c), me).wait_recv()
            later = [copy(b, 3, (*relay_of, c), (*relay_to, c)), copy(b, 4, (*over_x, c), sibling),
                     copy(b, 5, (*over_y, c), sibling)]
            for cp in later:
                cp.start()
            sent += later
        for b in range(nb):
            copy(b, 3, (*diagonal, c), me).wait_recv()
            fwd = copy(b, 6, (*diagonal, c), sibling)
            fwd.start()
            sent.append(fwd)
        for b in range(nb):
            copy(b, 0, sibling, me).wait_recv()
            for k, chip in ((4, over_x), (5, over_y), (6, diagonal)):
                copy(b, k, (*chip, 1 - c), me).wait_recv()
        for cp in sent:
            cp.wait_send()
        for cp in mine:
            cp.wait()

    return _pcall(
        body, name="weights_all_gather",
        in_specs=[_ANY] * nb, out_specs=[_ANY] * nb,
        out_shape=[jax.ShapeDtypeStruct((N_DEV,) + a.shape, a.dtype) for a in blocks],
        scratch_shapes=[pltpu.SemaphoreType.DMA((7 * nb,)), pltpu.SemaphoreType.DMA((7 * nb,)),
                        pltpu.SemaphoreType.DMA((nb,))],
    )(*blocks)


def pair_exchange(bufs):
    nb = len(bufs)

    def body(*refs):
        srcs, dsts = refs[:nb], refs[nb:2 * nb]
        send_sems, recv_sems = refs[2 * nb:]
        x, y, c = lax.axis_index("x"), lax.axis_index("y"), lax.axis_index("c")
        copies = []
        for b in range(nb):
            for j in range(4):
                cp = pltpu.make_async_remote_copy(
                    src_ref=srcs[b].at[2 * j + (1 - c)], dst_ref=dsts[b].at[j], send_sem=send_sems.at[4 * b + j],
                    recv_sem=recv_sems.at[4 * b + j], device_id=(x, y, 1 - c), device_id_type=MESH)
                cp.start()
                copies.append(cp)
        for cp in copies:
            cp.wait()

    return _pcall(
        body, name="grads_pair_exchange",
        in_specs=[_ANY] * nb, out_specs=[_ANY] * nb,
        out_shape=[jax.ShapeDtypeStruct((4,) + a.shape[1:], a.dtype) for a in bufs],
        scratch_shapes=[pltpu.SemaphoreType.DMA((4 * nb,)), pltpu.SemaphoreType.DMA((4 * nb,))],
    )(*bufs)


def pair_sum(g, got, c, out_dtype, name):
    r, w = g.shape[1:]
    tr = _tile(r, 512, 16)

    def body(c_ref, a_ref, b_ref, o_ref):
        o_ref[...] = (a_ref[...].astype(F32) + b_ref[...].astype(F32)).astype(o_ref.dtype)

    return _pcall(
        body, name=name,
        grid_spec=pltpu.PrefetchScalarGridSpec(
            num_scalar_prefetch=1, grid=(4, r // tr),
            in_specs=[pl.BlockSpec((None, tr, w), lambda j, i, c_ref: (2 * j + c_ref[0], i, 0)),
                      pl.BlockSpec((None, tr, w), lambda j, i, c_ref: (j, i, 0))],
            out_specs=pl.BlockSpec((None, tr, w), lambda j, i, c_ref: (j, i, 0))),
        out_shape=jax.ShapeDtypeStruct((4,) + g.shape[1:], out_dtype),
        compiler_params=_params(),
    )(c, g, got)


def chip_exchange(parts):
    nb = len(parts)

    def body(*refs):
        srcs, dsts = refs[:nb], refs[nb:2 * nb]
        send_sems, recv_sems, local_sems = refs[2 * nb:]
        x, y, c = lax.axis_index("x"), lax.axis_index("y"), lax.axis_index("c")
        my_chip = 2 * x + y
        copies = []
        for b in range(nb):
            mine = pltpu.make_async_copy(srcs[b].at[my_chip], dsts[b].at[my_chip], local_sems.at[b])
            mine.start()
            copies.append(mine)
            for k, (tx, ty) in enumerate([(1 - x, y), (x, 1 - y), (1 - x, 1 - y)]):
                cp = pltpu.make_async_remote_copy(
                    src_ref=srcs[b].at[2 * tx + ty], dst_ref=dsts[b].at[my_chip], send_sem=send_sems.at[3 * b + k],
                    recv_sem=recv_sems.at[3 * b + k], device_id=(tx, ty, c), device_id_type=MESH)
                cp.start()
                copies.append(cp)
        for cp in copies:
            cp.wait()

    return _pcall(
        body, name="grads_chip_exchange",
        in_specs=[_ANY] * nb, out_specs=[_ANY] * nb,
        out_shape=[jax.ShapeDtypeStruct(a.shape, a.dtype) for a in parts],
        scratch_shapes=[pltpu.SemaphoreType.DMA((3 * nb,)), pltpu.SemaphoreType.DMA((3 * nb,)),
                        pltpu.SemaphoreType.DMA((nb,))],
    )(*parts)


def chip_sum(parts, name):
    r, w = parts.shape[1:]
    tr = _tile(r, 512, 16)
    spec = lambda j: _in(parts, (None, tr, w), lambda i, j=j: (j, i, 0))

    def fn(ids, a, b, c_, d):
        a, b, c_, d = [t.astype(F32) for t in (a, b, c_, d)]
        return (((a + b) + c_) + d,)

    (out,) = tcall(fn, (r // tr,), [spec(j) for j in range(4)],
                   [_out((r, w), F32, (tr, w), lambda i: (i, 0))], name)
    return out


def _remote(src, dst, send_sems, recv_sems, k, to):
    return functools.partial(pltpu.make_async_remote_copy, src_ref=src, dst_ref=dst, send_sem=send_sems.at[k],
                             recv_sem=recv_sems.at[k], device_id=to, device_id_type=MESH)


def _gather_plan(phase, nb):
    def plan(ins, outs, send_sems, recv_sems, local_sems):
        x, y, c = lax.axis_index("x"), lax.axis_index("y"), lax.axis_index("c")
        me, sibling = (x, y, c), (x, y, 1 - c)
        over_x, over_y, diagonal = (1 - x, y), (x, 1 - y), (1 - x, 1 - y)
        relay_of = ((1 - x) * (1 - c) + x * c, y * (1 - c) + (1 - y) * c)
        relay_to = (x * (1 - c) + (1 - x) * c, (1 - y) * (1 - c) + y * c)
        local, sends, recvs = [], [], []
        for b in range(nb):
            slot = lambda chip, core, b=b: outs[b].at[4 * chip[0] + 2 * chip[1] + core]
            if phase == 0:
                local.append(functools.partial(pltpu.make_async_copy, ins[b], slot((x, y), c), local_sems.at[b]))
                moves = [(ins[b], slot((x, y), c), to) for to in (sibling, (*over_x, c), (*over_y, c))]
                arrive = [slot((x, y), 1 - c), slot(over_x, c), slot(over_y, c)]
            elif phase == 1:
                moves = [(slot(relay_of, c), slot(relay_of, c), (*relay_to, c)),
                         (slot(over_x, c), slot(over_x, c), sibling), (slot(over_y, c), slot(over_y, c), sibling)]
                arrive = [slot(diagonal, c), slot(over_x, 1 - c), slot(over_y, 1 - c)]
            else:
                moves = [(slot(diagonal, c), slot(diagonal, c), sibling)]
                arrive = [slot(diagonal, 1 - c)]
            sends += [_remote(src, dst, send_sems, recv_sems, 3 * b + k, to) for k, (src, dst, to) in enumerate(moves)]
            recvs += [_remote(dst, dst, send_sems, recv_sems, 3 * b + k, me) for k, dst in enumerate(arrive)]
        return local, sends, recvs
    return plan


def gather_side(phase, arrays):
    nb = len(arrays)
    if phase == 0:
        shapes = [jax.ShapeDtypeStruct((N_DEV,) + a.shape, a.dtype) for a in arrays]
        return Side(arrays, shapes, 3 * nb, nb, _gather_plan(0, nb))
    shapes = [jax.ShapeDtypeStruct(a.shape, a.dtype) for a in arrays]
    return Side(arrays, shapes, 3 * nb, 0, _gather_plan(phase, nb), aliased=True)


def pair_side(bufs):
    nb = len(bufs)

    def plan(ins, outs, send_sems, recv_sems, local_sems):
        x, y, c = lax.axis_index("x"), lax.axis_index("y"), lax.axis_index("c")
        sends = [_remote(ins[b].at[2 * j + (1 - c)], outs[b].at[j], send_sems, recv_sems, 4 * b + j, (x, y, 1 - c))
                 for b in range(nb) for j in range(4)]
        recvs = [_remote(outs[b].at[j], outs[b].at[j], send_sems, recv_sems, 4 * b + j, (x, y, c))
                 for b in range(nb) for j in range(4)]
        return [], sends, recvs

    shapes = [jax.ShapeDtypeStruct((4,) + a.shape[1:], a.dtype) for a in bufs]
    return Side(bufs, shapes, 4 * nb, 0, plan)


def chip_side(parts):
    nb = len(parts)

    def plan(ins, outs, send_sems, recv_sems, local_sems):
        x, y, c = lax.axis_index("x"), lax.axis_index("y"), lax.axis_index("c")
        my_chip = 2 * x + y
        peers = [(1 - x, y), (x, 1 - y), (1 - x, 1 - y)]
        local = [functools.partial(pltpu.make_async_copy, ins[b].at[my_chip], outs[b].at[my_chip], local_sems.at[b])
                 for b in range(nb)]
        sends = [_remote(ins[b].at[2 * tx + ty], outs[b].at[my_chip], send_sems, recv_sems, 3 * b + k, (tx, ty, c))
                 for b in range(nb) for k, (tx, ty) in enumerate(peers)]
        recvs = [_remote(outs[b].at[2 * tx + ty], outs[b].at[2 * tx + ty], send_sems, recv_sems, 3 * b + k, (x, y, c))
                 for b in range(nb) for k, (tx, ty) in enumerate(peers)]
        return local, sends, recvs

    shapes = [jax.ShapeDtypeStruct(a.shape, a.dtype) for a in parts]
    return Side(parts, shapes, 3 * nb, nb, plan)


def adamw(w, g, m, v, name):
    shape = w.shape
    cols = shape[-1]
    rows = int(np.prod(shape[:-1]))
    w2, g2, m2, v2 = [a.reshape(rows, cols) for a in (w, g, m, v)]
    tr = _tile(rows, 256, 8) if rows % 8 == 0 else rows

    def fn(ids, wt, gt, mt, vt):
        m_new = ADAM_B1 * mt + (1.0 - ADAM_B1) * gt
        v_new = ADAM_B2 * vt + (1.0 - ADAM_B2) * (gt * gt)
        m_hat = m_new / (1.0 - ADAM_B1 ** ADAM_STEP)
        v_hat = v_new / (1.0 - ADAM_B2 ** ADAM_STEP)
        delta = -ADAM_LR * (m_hat / (jnp.sqrt(v_hat) + ADAM_EPS) + ADAM_WD * wt)
        return delta, m_new, v_new

    res = tcall(fn, (rows // tr,), [_row(a, tr) for a in (w2, g2, m2, v2)],
                [_row_out(rows, cols, F32, tr) for _ in range(3)], name)
    return [a.reshape(shape) for a in res]


_MATS = [("ffn1_w_gu", "col"), ("ffn1_w_down", "row"), ("ev_w_in", "col"), ("ev_w_out", "row"),
         ("od_w_in", "col"), ("od_w_out", "row"), ("xa_w_q", "row"), ("xa_w_kv", "col"), ("xa_w_o", "row"),
         ("ffn2_w_gu", "col"), ("ffn2_w_down", "row")]
_VECS = ["ffn1_norm", "mix_norm", "ev_q_gain", "ev_k_gain", "ev_sinks", "od_q_gain", "od_k_gain", "xa_norm",
         "xa_mem_norm", "xa_q_gain", "xa_k_gain", "ffn2_norm"]
_WEIGHTS = ["ffn1_norm", "ffn1_w_gu", "ffn1_w_down", "mix_norm", "ev_w_in", "ev_q_gain", "ev_k_gain", "ev_sinks",
            "ev_w_out", "od_w_in", "od_q_gain", "od_k_gain", "od_w_out", "xa_norm", "xa_mem_norm", "xa_w_q", "xa_w_kv",
            "xa_q_gain", "xa_k_gain", "xa_w_o", "ffn2_norm", "ffn2_w_gu", "ffn2_w_down"]


_AXIS = dict(_MATS)
DEPTH = 2


def _layer_groups(l):
    first, rest = _first_block_groups(l)
    return [first[0] + rest[0] + rest[1]]


def _first_block_groups(l):
    w_in, w_out = ("ev_w_in", "ev_w_out") if l % 2 == 0 else ("od_w_in", "od_w_out")
    first = [[("ffn1_w_gu", l), ("ffn1_w_down", l)]]
    rest = [[("ffn2_w_gu", l), ("xa_w_kv", l)],
            [(w_in, l // 2), ("ffn2_w_down", l), (w_out, l // 2), ("xa_w_q", l), ("xa_w_o", l)]]
    return first, rest


def _block_rows(shards, n):
    a, b = shards[n].shape[1:]
    return a if _AXIS[n] == "row" else b


def _weight_blocks(shards, groups):
    blocks = []
    for group in groups:
        rows = [(shards[n][j] if _AXIS[n] == "row" else shards[n][j].T).astype(BF16) for n, j in group]
        blocks.append(rows[0] if len(rows) == 1 else jnp.concatenate(rows, axis=0))
    return blocks


def _whole_weights(shards, groups, gathered):
    full = {}
    for group, got in zip(groups, gathered):
        off = 0
        for n, j in group:
            r = _block_rows(shards, n)
            full[n] = got[:, off:off + r, :].reshape(N_DEV * r, got.shape[2])
            off += r
    return full


def _gradient_buffers(grads, groups):
    bufs = []
    for group in groups:
        rows = []
        for n, _ in group:
            whole = jnp.concatenate(grads[n], axis=0) if isinstance(grads[n], tuple) else grads[n]
            rows.append(whole.reshape(N_DEV, whole.shape[0] // N_DEV, whole.shape[1]))
        bufs.append((rows[0] if len(rows) == 1 else jnp.concatenate(rows, axis=1)).astype(BF16))
    return bufs


def _gradient_blocks(shards, groups, sums):
    out = {}
    for group, tot in zip(groups, sums):
        off = 0
        for n, j in group:
            r = _block_rows(shards, n)
            out[n, j] = tot[off:off + r] if _AXIS[n] == "row" else tot[off:off + r].T
            off += r
    return out


class _PairChain:
    def __init__(self, ex, bufs):
        self.ex, self.bufs, self.parts = ex, bufs, None

    def side(self, name):
        return pair_side(self.bufs) if name == "dwd" else None

    def done(self, name, carried):
        self.parts = self.ex.pair_sums(self.bufs, carried, "l1")


class _RestChain:
    HALF = {"dwd": (0,), "dwgu": (1,)}

    def __init__(self, ex, bufs):
        self.ex, self.bufs, self.parts, self.sums = ex, bufs, None, [None] * len(bufs)

    def side(self, name):
        if name == "da":
            return pair_side(self.bufs)
        return chip_side([self.parts[i] for i in self.HALF[name]])

    def done(self, name, carried):
        if name == "da":
            self.parts = self.ex.pair_sums(self.bufs, carried, "l0r")
        else:
            for i, tot in zip(self.HALF[name], self.ex.chip_sums(carried, "l0r_" + name)):
                self.sums[i] = tot


class _Exchange:
    def __init__(self, shards, c):
        self.shards, self.c = shards, c

    def weights_first(self):
        first, _ = _first_block_groups(0)
        return _whole_weights(self.shards, first, all_gather_blocks(_weight_blocks(self.shards, first)))

    def rest_blocks(self):
        return _weight_blocks(self.shards, _first_block_groups(0)[1])

    def weights_rest(self, gathered):
        return _whole_weights(self.shards, _first_block_groups(0)[1], gathered)

    def gather_start(self):
        return gather_side(0, _weight_blocks(self.shards, _layer_groups(1)))

    def weights_next(self, gathered):
        return _whole_weights(self.shards, _layer_groups(1), gathered)

    def chain_next(self, grads):
        return _PairChain(self, _gradient_buffers(grads, _layer_groups(1)))

    def chain_rest(self, grads):
        return _RestChain(self, _gradient_buffers(grads, _first_block_groups(0)[1]))

    def pair_sums(self, bufs, got, tag):
        return [pair_sum(b, g, self.c, b.dtype, f"grads_pair_sum_{tag}_{i}") for i, (b, g) in enumerate(zip(bufs, got))]

    def chip_sums(self, parts, tag):
        return [chip_sum(p, f"grads_chip_sum_{tag}_{i}") for i, p in enumerate(parts)]

    def finish(self, gm, gv, sums1, sums_rest):
        vecs = {n: jnp.concatenate(v, axis=0) for n, v in gv.items()}
        first, rest = _first_block_groups(0)
        bufs = _gradient_buffers(gm[0], first)
        vec = jnp.concatenate([vecs[n].reshape(-1) for n in _VECS])
        vec = jnp.pad(vec, (0, -vec.shape[0] % (16 * LANES)))
        bufs.append(jnp.broadcast_to(vec.reshape(1, -1, LANES), (N_DEV, vec.shape[0] // LANES, LANES)))
        parts = self.pair_sums(bufs, pair_exchange(bufs), "l0")
        sums0 = self.chip_sums(chip_exchange(parts), "l0")
        blocks = {**_gradient_blocks(self.shards, first, sums0[:-1]), **_gradient_blocks(self.shards, rest, sums_rest),
                  **_gradient_blocks(self.shards, _layer_groups(1), sums1)}
        out = {n: jnp.stack([blocks[n, j] for j in range(self.shards[n].shape[0])]) for n, _ in _MATS}
        flat, off = sums0[-1].reshape(-1), 0
        for n in _VECS:
            out[n] = flat[off:off + vecs[n].size].reshape(vecs[n].shape)
            off += vecs[n].size
        return out


class _NoExchange:
    def __init__(self, full):
        self.full = full

    def weights_first(self):
        return self.full[0]

    def rest_blocks(self):
        return None

    def gather_start(self):
        return None

    def weights_next(self, gathered):
        return self.full[1]

    def chain_next(self, grads):
        return None

    def chain_rest(self, grads):
        return None

    def finish(self, gm, gv, sums1, sums_rest):
        mats = {}
        for l in range(DEPTH):
            for group in _layer_groups(l):
                for n, j in group:
                    whole = jnp.concatenate(gm[l][n], axis=0) if isinstance(gm[l][n], tuple) else gm[l][n]
                    mats.setdefault(n, {})[j] = whole if _AXIS[n] == "row" else whole.T
        mats = {n: jnp.stack([v[j] for j in sorted(v)]) for n, v in mats.items()}
        return mats, {n: jnp.concatenate(v, axis=0) for n, v in gv.items()}


def _local_step(x, mem, target, w, ex):
    assert w["ffn1_norm"].shape[0] == DEPTH
    row = lambda a, l: a[l:l + 1]
    full = [ex.weights_first(), None]
    saved = []
    for l in range(DEPTH):
        t, j, f = f"l{l}", l // 2, full[l]
        rest = ex.rest_blocks() if l == 0 else None
        if rest is None:
            x, s1 = ffn_fwd(x, row(w["ffn1_norm"], l), f["ffn1_w_gu"], f["ffn1_w_down"], t + "_ffn1")
        else:
            x, s1, rest = ffn_fwd(x, row(w["ffn1_norm"], l), f["ffn1_w_gu"], f["ffn1_w_down"], t + "_ffn1", (0, rest))
        relay = None
        if l % 2 == 0:
            h = rmsnorm_fwd(x, row(w["mix_norm"], l), t + "_ev_norm", None if rest is None else gather_side(2, rest))
            if rest is not None:
                h, rest = h
                f = full[l] = {**f, **ex.weights_rest(rest)}
            side = ex.gather_start() if l + 1 < DEPTH else None
            x, s2, relay = even_mixer_fwd(x, h, _ev_reorder(f["ev_w_in"]), row(w["ev_q_gain"], j),
                                          row(w["ev_k_gain"], j), row(w["ev_sinks"], j), f["ev_w_out"], t + "_ev", side)
        else:
            x, s2 = odd_mixer_fwd(x, row(w["mix_norm"], l), f["od_w_in"], row(w["od_q_gain"], j),
                                  row(w["od_k_gain"], j), f["od_w_out"], t + "_od")
        x, s3 = xa_fwd(x, mem, row(w["xa_norm"], l), row(w["xa_mem_norm"], l), f["xa_w_q"], f["xa_w_kv"],
                       row(w["xa_q_gain"], l), row(w["xa_k_gain"], l), f["xa_w_o"], t + "_xa")
        if relay is None:
            x, s4 = ffn_fwd(x, row(w["ffn2_norm"], l), f["ffn2_w_gu"], f["ffn2_w_down"], t + "_ffn2")
        else:
            x, s4, relay = ffn_fwd(x, row(w["ffn2_norm"], l), f["ffn2_w_gu"], f["ffn2_w_down"], t + "_ffn2", (1, relay))
        if l + 1 < DEPTH:
            full[l + 1] = ex.weights_next(relay)
        saved.append((s1, s2, s3, s4))
    dx, sq = loss_head(x, target, "loss_head")
    loss = 0.5 * jnp.sum(sq) / x.shape[1]

    gm = [dict() for _ in range(DEPTH)]
    gv = {n: [None] * w[n].shape[0] for n in _VECS}
    chain1 = chain0 = sums1 = None
    for l in reversed(range(DEPTH)):
        t, j, f = f"l{l}", l // 2, full[l]
        s1, s2, s3, s4 = saved[l]
        dx, gv["ffn2_norm"][l], gm[l]["ffn2_w_gu"], gm[l]["ffn2_w_down"] = ffn_bwd(
            dx, s4, row(w["ffn2_norm"], l), f["ffn2_w_gu"], f["ffn2_w_down"], t + "_ffn2", chain1 if l == 0 else None)
        parts = chain1.parts if l == 0 and chain1 is not None else None
        (dx, gv["xa_norm"][l], gv["xa_mem_norm"][l], gm[l]["xa_w_q"], gm[l]["xa_w_kv"], gv["xa_q_gain"][l],
         gv["xa_k_gain"][l], gm[l]["xa_w_o"]) = xa_bwd(
            dx, s3, mem, row(w["xa_norm"], l), row(w["xa_mem_norm"], l), f["xa_w_q"], f["xa_w_kv"],
            row(w["xa_q_gain"], l), row(w["xa_k_gain"], l), f["xa_w_o"], t + "_xa")
        if l % 2 == 0:
            (dx, gv["mix_norm"][l], d_win, gv["ev_q_gain"][j], gv["ev_k_gain"][j], gv["ev_sinks"][j],
             gm[l]["ev_w_out"], carried) = even_mixer_bwd(
                dx, s2, row(w["mix_norm"], l), _ev_reorder(f["ev_w_in"]), row(w["ev_q_gain"], j), row(w["ev_k_gain"], j),
                row(w["ev_sinks"], j), f["ev_w_out"], t + "_ev", None if parts is None else chip_side(parts))
            gm[l]["ev_w_in"] = _ev_restore(d_win)
            if carried is not None:
                sums1 = ex.chip_sums(carried, "l1")
        else:
            (dx, gv["mix_norm"][l], gm[l]["od_w_in"], gv["od_q_gain"][j], gv["od_k_gain"][j],
             gm[l]["od_w_out"]) = odd_mixer_bwd(
                dx, s2, row(w["mix_norm"], l), f["od_w_in"], row(w["od_q_gain"], j), row(w["od_k_gain"], j),
                f["od_w_out"], t + "_od")
        if l == 0:
            chain0 = ex.chain_rest(gm[l])
        dx, gv["ffn1_norm"][l], gm[l]["ffn1_w_gu"], gm[l]["ffn1_w_down"] = ffn_bwd(
            dx, s1, row(w["ffn1_norm"], l), f["ffn1_w_gu"], f["ffn1_w_down"], t + "_ffn1", chain0 if l == 0 else None)
        if l == 1:
            chain1 = ex.chain_next(gm[l])
    return loss, dx, ex.finish(gm, gv, sums1, None if chain0 is None else chain0.sums)


def kernel(x, mem, ffn1_norm, ffn1_w_gu, ffn1_w_down, mix_norm, ev_w_in, ev_q_gain, ev_k_gain, ev_sinks, ev_w_out, od_w_in, od_q_gain, od_k_gain, od_w_out, xa_norm, xa_mem_norm, xa_w_q, xa_w_kv, xa_q_gain, xa_k_gain, xa_w_o, ffn2_norm, ffn2_w_gu, ffn2_w_down, loss_target, m_ffn1_norm, m_ffn1_w_gu, m_ffn1_w_down, m_mix_norm, m_ev_w_in, m_ev_q_gain, m_ev_k_gain, m_ev_sinks, m_ev_w_out, m_od_w_in, m_od_q_gain, m_od_k_gain, m_od_w_out, m_xa_norm, m_xa_mem_norm, m_xa_w_q, m_xa_w_kv, m_xa_q_gain, m_xa_k_gain, m_xa_w_o, m_ffn2_norm, m_ffn2_w_gu, m_ffn2_w_down, v_ffn1_norm, v_ffn1_w_gu, v_ffn1_w_down, v_mix_norm, v_ev_w_in, v_ev_q_gain, v_ev_k_gain, v_ev_sinks, v_ev_w_out, v_od_w_in, v_od_q_gain, v_od_k_gain, v_od_w_out, v_xa_norm, v_xa_mem_norm, v_xa_w_q, v_xa_w_kv, v_xa_q_gain, v_xa_k_gain, v_xa_w_o, v_ffn2_norm, v_ffn2_w_gu, v_ffn2_w_down):
    w = dict(ffn1_norm=ffn1_norm, ffn1_w_gu=ffn1_w_gu, ffn1_w_down=ffn1_w_down, mix_norm=mix_norm, ev_w_in=ev_w_in, ev_q_gain=ev_q_gain, ev_k_gain=ev_k_gain, ev_sinks=ev_sinks, ev_w_out=ev_w_out, od_w_in=od_w_in, od_q_gain=od_q_gain, od_k_gain=od_k_gain, od_w_out=od_w_out, xa_norm=xa_norm, xa_mem_norm=xa_mem_norm, xa_w_q=xa_w_q, xa_w_kv=xa_w_kv, xa_q_gain=xa_q_gain, xa_k_gain=xa_k_gain, xa_w_o=xa_w_o, ffn2_norm=ffn2_norm, ffn2_w_gu=ffn2_w_gu, ffn2_w_down=ffn2_w_down)
    m = dict(ffn1_norm=m_ffn1_norm, ffn1_w_gu=m_ffn1_w_gu, ffn1_w_down=m_ffn1_w_down, mix_norm=m_mix_norm, ev_w_in=m_ev_w_in, ev_q_gain=m_ev_q_gain, ev_k_gain=m_ev_k_gain, ev_sinks=m_ev_sinks, ev_w_out=m_ev_w_out, od_w_in=m_od_w_in, od_q_gain=m_od_q_gain, od_k_gain=m_od_k_gain, od_w_out=m_od_w_out, xa_norm=m_xa_norm, xa_mem_norm=m_xa_mem_norm, xa_w_q=m_xa_w_q, xa_w_kv=m_xa_w_kv, xa_q_gain=m_xa_q_gain, xa_k_gain=m_xa_k_gain, xa_w_o=m_xa_w_o, ffn2_norm=m_ffn2_norm, ffn2_w_gu=m_ffn2_w_gu, ffn2_w_down=m_ffn2_w_down)
    v = dict(ffn1_norm=v_ffn1_norm, ffn1_w_gu=v_ffn1_w_gu, ffn1_w_down=v_ffn1_w_down, mix_norm=v_mix_norm, ev_w_in=v_ev_w_in, ev_q_gain=v_ev_q_gain, ev_k_gain=v_ev_k_gain, ev_sinks=v_ev_sinks, ev_w_out=v_ev_w_out, od_w_in=v_od_w_in, od_q_gain=v_od_q_gain, od_k_gain=v_od_k_gain, od_w_out=v_od_w_out, xa_norm=v_xa_norm, xa_mem_norm=v_xa_mem_norm, xa_w_q=v_xa_w_q, xa_w_kv=v_xa_w_kv, xa_q_gain=v_xa_q_gain, xa_k_gain=v_xa_k_gain, xa_w_o=v_xa_w_o, ffn2_norm=v_ffn2_norm, ffn2_w_gu=v_ffn2_w_gu, ffn2_w_down=v_ffn2_w_down)

    c = lax.axis_index("c").astype(jnp.int32).reshape(1)
    loss, dx, grads = _local_step(x[0], mem[0], loss_target[0], w, _Exchange(w, c))
    loss = lax.psum(loss, ("x", "y", "c"))

    delta, new_m, new_v = {}, {}, {}
    for n in _WEIGHTS:
        delta[n], new_m[n], new_v[n] = adamw(w[n], grads[n], m[n], v[n], "adamw_" + n)
    return (loss, dx[None], *[grads[n] for n in _WEIGHTS], *[delta[n] for n in _WEIGHTS],
            *[new_m[n] for n in _WEIGHTS], *[new_v[n] for n in _WEIGHTS])
```

```python
import functools

import numpy as np
import jax
import jax.numpy as jnp
from jax import lax
from jax.experimental import pallas as pl
from jax.experimental.pallas import tpu as pltpu

F32 = jnp.float32
BF16 = jnp.bfloat16
MESH = pl.DeviceIdType.MESH

HEAD_DIM = 64
BLOCK = 128
RMS_EPS = 1e-6
A_Q_HEADS, A_KV_HEADS = 8, 2
B_HEADS = 8
C_HEADS = 16
C_PATTERNS = ((128, 1), (512, 4), (2048, 16))
X_HEADS = 4
N_DEV = 8
LANES = 1024
VMEM_LIMIT_BYTES = 56 * 1024 * 1024
SB_SKIP_LOG = -110.0
NEG_BIG = -1e30

ADAM_LR, ADAM_B1, ADAM_B2, ADAM_EPS, ADAM_WD, ADAM_STEP = 0.001, 0.9, 0.999, 1e-08, 0.01, 10

NN = (((1,), (0,)), ((), ()))
NT = (((1,), (1,)), ((), ()))
TN = (((0,), (0,)), ((), ()))


class Side:
    def __init__(self, arrays, out_shapes, n_remote, n_local, plan, aliased=False):
        self.arrays, self.out_shapes, self.plan, self.aliased = list(arrays), list(out_shapes), plan, aliased
        self.sems = [pltpu.SemaphoreType.DMA((n_remote,)), pltpu.SemaphoreType.DMA((n_remote,)),
                     pltpu.SemaphoreType.DMA((max(n_local, 1),))]

    def start(self, ins, outs, sems):
        local, sends, _ = self.plan(ins, outs, *sems)
        for make in local + sends:
            make().start()

    def wait(self, ins, outs, sems):
        local, sends, recvs = self.plan(ins, outs, *sems)
        for make in sends:
            make().wait_send()
        for make in recvs:
            make().wait_recv()
        for make in local:
            make().wait()


def _pcall(body, side=None, **kw):
    if side is None:
        return pl.pallas_call(body, **kw)
    grid = kw["grid"]
    single = not isinstance(kw["out_specs"], (list, tuple))
    out_specs = [kw["out_specs"]] if single else list(kw["out_specs"])
    out_shape = [kw["out_shape"]] if single else list(kw["out_shape"])
    scratch = list(kw.get("scratch_shapes", []))
    n_in, n_out, n_scr, n_side = len(kw["in_specs"]), len(out_specs), len(scratch), len(side.arrays)
    n_sout = len(side.out_shapes)

    def hosted(*refs):
        ins, s_in = refs[:n_in], refs[n_in:n_in + n_side]
        outs = refs[n_in + n_side:n_in + n_side + n_out]
        s_out = refs[n_in + n_side + n_out:n_in + n_side + n_out + n_sout]
        rest = refs[n_in + n_side + n_out + n_sout:]
        scr, sems = rest[:n_scr], rest[n_scr:]
        first = last = None
        for a, size in enumerate(grid):
            f, l = pl.program_id(a) == 0, pl.program_id(a) == size - 1
            first = f if first is None else jnp.logical_and(first, f)
            last = l if last is None else jnp.logical_and(last, l)

        @pl.when(first)
        def _():
            side.start(s_in, s_out, sems)

        body(*ins, *outs, *scr)

        @pl.when(last)
        def _():
            side.wait(s_in, s_out, sems)

    any_space = pl.BlockSpec(memory_space=pl.ANY)
    kw2 = dict(kw)
    kw2.update(in_specs=list(kw["in_specs"]) + [any_space] * n_side, out_specs=out_specs + [any_space] * n_sout,
               out_shape=out_shape + side.out_shapes, scratch_shapes=scratch + side.sems)
    if side.aliased:
        kw2["input_output_aliases"] = {n_in + i: n_out + i for i in range(n_side)}
    call = pl.pallas_call(hosted, **kw2)

    def run(*args):
        res = call(*args, *side.arrays)
        return (res[0] if single else list(res[:n_out])), list(res[n_out:])

    return run


def _params(**kw):
    return pltpu.CompilerParams(vmem_limit_bytes=VMEM_LIMIT_BYTES, **kw)


def _tile(dim, cap, unit=128):
    if dim <= cap:
        return dim
    t = (cap // unit) * unit
    while t >= unit:
        if dim % t == 0:
            return t
        t -= unit
    raise ValueError(f"no tile for {dim} under {cap}")


def _dot(a, b, dims):
    return lax.dot_general(a.astype(BF16), b.astype(BF16), dims, preferred_element_type=F32)


@functools.partial(jax.custom_vjp, nondiff_argnums=(2,))
def _dot_vjp(a, b, nt):
    return _dot(a, b, NT if nt else NN)


def _dot_vjp_fwd(a, b, nt):
    return _dot(a, b, NT if nt else NN), (a.astype(BF16), b.astype(BF16))


def _dot_vjp_bwd(nt, res, g):
    a, b = res
    if nt:
        return _dot(g, b, NN), _dot(g, a, TN)
    return _dot(g, b, NT), _dot(a, g, TN)


_dot_vjp.defvjp(_dot_vjp_fwd, _dot_vjp_bwd)


def _plain_dot(a, b, nt):
    return _dot(a, b, NT if nt else NN)


def _split_dot(x, mat, terms=2):
    out, rem = None, x
    for t in range(terms):
        part = rem.astype(BF16)
        d = lax.dot_general(part, mat, NN, preferred_element_type=F32)
        out = d if out is None else out + d
        if t + 1 < terms:
            rem = rem - part.astype(F32)
    return out


@functools.partial(jax.custom_vjp, nondiff_argnums=(3,))
def _split_dot_vjp(x, mat, mat_t, terms):
    return _split_dot(x, mat, terms)


def _split_dot_vjp_fwd(x, mat, mat_t, terms):
    return _split_dot(x, mat, terms), mat_t


def _split_dot_vjp_bwd(terms, mat_t, g):
    return _split_dot(g, mat_t, terms), None, None


_split_dot_vjp.defvjp(_split_dot_vjp_fwd, _split_dot_vjp_bwd)


def _plain_split(x, mat, mat_t, terms):
    return _split_dot(x, mat, terms)


def _tri(after):
    j = lax.broadcasted_iota(jnp.int32, (BLOCK, BLOCK), 0)
    s = lax.broadcasted_iota(jnp.int32, (BLOCK, BLOCK), 1)
    return jnp.where(j > s if after else j < s, 1.0, 0.0).astype(BF16)


def _in(a, block, imap):
    return (a, block, imap)


def _out(shape, dtype, block, imap, acc=False):
    return (shape, dtype, block, imap, acc)


def tcall(fn, grid, ins, outs, name, scratch=None, side=None):
    nin = len(ins)
    nout = len(outs)
    ngrid = len(grid)

    def body(*refs):
        ids = tuple(pl.program_id(a) for a in range(ngrid))
        extra = {} if scratch is None else {"scratch": refs[nin + nout]}
        res = fn(ids, *[r[...] for r in refs[:nin]], **extra)
        first = ids[0] == 0
        for a in range(1, ngrid):
            first = jnp.logical_and(first, ids[a] == 0)
        for o_ref, r, spec in zip(refs[nin:nin + nout], res, outs):
            if spec[4]:
                @pl.when(first)
                def _(o_ref=o_ref):
                    o_ref[...] = jnp.zeros(o_ref.shape, o_ref.dtype)
                o_ref[...] += r.astype(o_ref.dtype)
            else:
                o_ref[...] = r.astype(o_ref.dtype)

    return _pcall(
        body, side=side, name=name, grid=grid,
        in_specs=[pl.BlockSpec(b, m) for (_, b, m) in ins],
        out_specs=[pl.BlockSpec(b, m) for (_, _, b, m, _) in outs],
        out_shape=[jax.ShapeDtypeStruct(s, d) for (s, d, _, _, _) in outs],
        scratch_shapes=[] if scratch is None else [pltpu.VMEM(*scratch)],
        compiler_params=_params(),
    )(*[a for (a, _, _) in ins])


def _to_strided(scr, nat, d):
    if d == 1:
        return nat
    t, w = nat.shape
    nc = w // BLOCK
    for c in range(nc):
        scr[c * t:(c + 1) * t, :] = nat[:, c * BLOCK:(c + 1) * BLOCK]
    return jnp.concatenate([scr[pl.ds(c * t + r, t // d, stride=d), :] for r in range(d) for c in range(nc)], axis=1)


def _to_natural(scr, st, d):
    if d == 1:
        return st.astype(F32)
    t, w = st.shape[0] * d, st.shape[1] // d
    nc = w // BLOCK
    st = st.astype(F32)
    for r in range(d):
        for c in range(nc):
            scr[pl.ds(c * t + r, t // d, stride=d), :] = st[:, r * w + c * BLOCK:r * w + (c + 1) * BLOCK]
    return jnp.concatenate([scr[c * t:(c + 1) * t, :] for c in range(nc)], axis=1)


def _row(a, tm, width=None, cb=0):
    width = a.shape[1] if width is None else width
    return _in(a, (tm, width), lambda i, cb=cb: (i, cb))


def _full(a):
    zeros = (0,) * a.ndim
    return _in(a, a.shape, lambda *ids: zeros)


def _row_out(n, width, dtype, tm):
    return _out((n, width), dtype, (tm, width), lambda i: (i, 0))


def _acc_out(shape):
    zeros = (0,) * len(shape)
    return _out(shape, F32, shape, lambda *ids: zeros, acc=True)


def mm(a, b, mode, name, *, out_dtype=None, scale=1.0, res=None, side=None):
    if out_dtype is None:
        out_dtype = BF16 if mode == "tn" else F32
    if mode == "nn":
        (m, k), (k2, n) = a.shape, b.shape
    elif mode == "nt":
        (m, k), (n, k2) = a.shape, b.shape
    else:
        (k, m), (k2, n) = a.shape, b.shape
    assert k == k2, (a.shape, b.shape, mode)
    tm, tn, tk = _tile(m, 1408 if mode == "tn" else 512), _tile(n, 1408), _tile(k, 1408)
    nk = k // tk
    dims = {"nn": NN, "nt": NT, "tn": TN}[mode]
    has_res = res is not None

    def body(*refs):
        if has_res:
            a_ref, b_ref, r_ref, o_ref, acc_ref = refs
        else:
            a_ref, b_ref, o_ref, acc_ref = refs
        kk = pl.program_id(2)

        @pl.when(kk == 0)
        def _():
            acc_ref[...] = jnp.zeros(acc_ref.shape, F32)

        acc_ref[...] += _dot(a_ref[...], b_ref[...], dims)

        @pl.when(kk == nk - 1)
        def _():
            out = acc_ref[...]
            if scale != 1.0:
                out = out * scale
            if has_res:
                out = out + r_ref[...]
            o_ref[...] = out.astype(o_ref.dtype)

    a_spec = (pl.BlockSpec((tk, tm), lambda i, j, kk: (kk, i)) if mode == "tn"
              else pl.BlockSpec((tm, tk), lambda i, j, kk: (i, kk)))
    b_spec = (pl.BlockSpec((tn, tk), lambda i, j, kk: (j, kk)) if mode == "nt"
              else pl.BlockSpec((tk, tn), lambda i, j, kk: (kk, j)))
    in_specs = [a_spec, b_spec]
    args = [a, b]
    if has_res:
        in_specs.append(pl.BlockSpec((tm, tn), lambda i, j, kk: (i, j)))
        args.append(res)
    order = ("parallel", "parallel", "arbitrary") if side is None else ("arbitrary",) * 3
    return _pcall(
        body, side=side, name=name, grid=(m // tm, n // tn, nk),
        in_specs=in_specs,
        out_specs=pl.BlockSpec((tm, tn), lambda i, j, kk: (i, j)),
        out_shape=jax.ShapeDtypeStruct((m, n), out_dtype),
        scratch_shapes=[pltpu.VMEM((tm, tn), F32)],
        compiler_params=_params(dimension_semantics=order),
    )(*args)


def _rms(x, g):
    return x * lax.rsqrt(jnp.mean(x * x, axis=-1, keepdims=True) + RMS_EPS) * g


def _silu_mul(gate, up):
    return gate / (1.0 + jnp.exp(-gate)) * up


def mm_gate_up(h, w_gu, name, side=None):
    m, k = h.shape
    f = w_gu.shape[0] // 2
    tm, tn = _tile(m, 512), _tile(f, 1408)
    nj = f // tn
    assert k <= 1408

    def body(h_ref, wg_ref, wu_ref, g_ref, u_ref, a_ref):
        ht = h_ref[...]
        for lo in range(0, tn, 512):
            cols = slice(lo, min(lo + 512, tn))
            gate, up = _dot(ht, wg_ref[cols, :], NT), _dot(ht, wu_ref[cols, :], NT)
            g_ref[:, cols] = gate.astype(g_ref.dtype)
            u_ref[:, cols] = up.astype(u_ref.dtype)
            a_ref[:, cols] = _silu_mul(gate, up).astype(a_ref.dtype)

    tile = pl.BlockSpec((tm, tn), lambda i, j: (i, j))
    return _pcall(
        body, side=side, name=name, grid=(m // tm, nj),
        in_specs=[pl.BlockSpec((tm, k), lambda i, j: (i, 0)),
                  pl.BlockSpec((tn, k), lambda i, j: (j, 0)),
                  pl.BlockSpec((tn, k), lambda i, j: (j + nj, 0))],
        out_specs=[tile, tile, tile],
        out_shape=[jax.ShapeDtypeStruct((m, f), BF16), jax.ShapeDtypeStruct((m, f), BF16),
                   jax.ShapeDtypeStruct((m, f), BF16)],
        compiler_params=_params(dimension_semantics=("arbitrary",) * 2),
    )(h, w_gu, w_gu)


def mm_down_act_bwd(dy, w_down, gate, up, name, side=None):
    m, d = dy.shape
    f = w_down.shape[0]
    tm, tn = _tile(m, 512), _tile(f, 1408)
    assert d <= 1408

    def body(dy_ref, w_ref, g_ref, u_ref, dg_ref, du_ref):
        dyt = dy_ref[...].astype(BF16)
        for lo in range(0, tn, 512):
            cols = slice(lo, min(lo + 512, tn))
            da = _dot(dyt, w_ref[cols, :], NT) * 0.5
            gate, up = g_ref[:, cols].astype(F32), u_ref[:, cols].astype(F32)
            s = 1.0 / (1.0 + jnp.exp(-gate))
            gs = gate * s
            du_ref[:, cols] = (da * gs).astype(du_ref.dtype)
            dg_ref[:, cols] = (da * up * s * (1.0 + gate - gs)).astype(dg_ref.dtype)

    tile = pl.BlockSpec((tm, tn), lambda i, j: (i, j))
    return _pcall(
        body, side=side, name=name, grid=(m // tm, f // tn),
        in_specs=[pl.BlockSpec((tm, d), lambda i, j: (i, 0)), pl.BlockSpec((tn, d), lambda i, j: (j, 0)), tile, tile],
        out_specs=[tile, tile],
        out_shape=[jax.ShapeDtypeStruct((m, f), BF16), jax.ShapeDtypeStruct((m, f), BF16)],
        compiler_params=_params(dimension_semantics=("arbitrary", "arbitrary")),
    )(dy, w_down, gate, up)


def mm_norm_bwd(a, b, x, g, dres, name, b_kd=False):
    halves = isinstance(a, (tuple, list))
    a0, a1 = a if halves else (a, None)
    m, k = a0.shape[0], a0.shape[1] * (2 if halves else 1)
    d = b.shape[1] if b_kd else b.shape[0]
    dims = NN if b_kd else NT
    tm, tk = _tile(m, 512), _tile(a0.shape[1], 1408)
    nk = k // tk
    nkh = a0.shape[1] // tk
    has_res = dres is not None

    def body(*refs):
        a_ref, b_ref, x_ref, g_ref = refs[:4]
        rest = refs[4:-3]
        a1_ref = rest[0] if halves else None
        r_ref = rest[-1] if has_res else None
        dx_ref, dg_ref, acc_ref = refs[-3:]
        i, kk = pl.program_id(0), pl.program_id(1)

        @pl.when(kk == 0)
        def _():
            acc_ref[...] = jnp.zeros(acc_ref.shape, F32)

        if halves:
            @pl.when(kk < nkh)
            def _():
                acc_ref[...] += _dot(a_ref[...], b_ref[...], dims)

            @pl.when(kk >= nkh)
            def _():
                acc_ref[...] += _dot(a1_ref[...], b_ref[...], dims)
        else:
            acc_ref[...] += _dot(a_ref[...], b_ref[...], dims)

        @pl.when(kk == nk - 1)
        def _():
            _, vjp = jax.vjp(_rms, x_ref[...], g_ref[...])
            dx, dg = vjp(acc_ref[...])
            dx_ref[...] = dx + r_ref[...] if has_res else dx

            @pl.when(i == 0)
            def _():
                dg_ref[...] = jnp.zeros(dg_ref.shape, F32)

            dg_ref[...] += dg

    rows = pl.BlockSpec((tm, d), lambda i, kk: (i, 0))
    first = pl.BlockSpec((tm, tk), lambda i, kk: (i, jnp.minimum(kk, nkh - 1)))
    second = pl.BlockSpec((tm, tk), lambda i, kk: (i, jnp.maximum(kk - nkh, 0)))
    b_spec = pl.BlockSpec((tk, d), lambda i, kk: (kk, 0)) if b_kd else pl.BlockSpec((d, tk), lambda i, kk: (0, kk))
    in_specs = ([first, b_spec, rows, pl.BlockSpec(g.shape, lambda i, kk: (0, 0))]
                + ([second] if halves else []) + ([rows] if has_res else []))
    return _pcall(
        body, name=name, grid=(m // tm, nk),
        in_specs=in_specs,
        out_specs=[rows, pl.BlockSpec(g.shape, lambda i, kk: (0, 0))],
        out_shape=[jax.ShapeDtypeStruct((m, d), F32), jax.ShapeDtypeStruct(g.shape, F32)],
        scratch_shapes=[pltpu.VMEM((tm, d), F32)],
        compiler_params=_params(dimension_semantics=("arbitrary", "arbitrary")),
    )(*([a0, b, x, g] + ([a1] if halves else []) + ([dres] if has_res else [])))


def _indicator(shape, head_axis, mod):
    lane = lax.broadcasted_iota(jnp.int32, shape, head_axis)
    other = lax.broadcasted_iota(jnp.int32, shape, 1 - head_axis)
    lane = jnp.bitwise_and(lane, HEAD_DIM - 1) if mod else jnp.right_shift(lane, 6)
    return jnp.where(lane == other, 1.0, 0.0).astype(BF16)


def _head_rms(split, xs, g):
    w = xs.shape[1]
    to_head, from_head = _indicator((w, BLOCK), 0, False), _indicator((BLOCK, w), 1, False)
    to_lane, from_lane = _indicator((HEAD_DIM, w), 1, True), _indicator((w, HEAD_DIM), 0, True)
    ss = split(xs * xs, to_head, from_head, 3)
    r = lax.rsqrt(ss * (1.0 / HEAD_DIM) + RMS_EPS)
    g_all = split(jnp.broadcast_to(g, (8, HEAD_DIM)), to_lane, from_lane, 3)[0:1]
    return xs * split(r, from_head, to_head, 3) * g_all


def _prep(split, x, qg, kg, segs):
    parts = []
    for start, width, kind in segs:
        xs = x[:, start:start + width]
        parts.append(xs if kind == "raw" else _head_rms(split, xs, qg if kind == "q" else kg))
    return jnp.concatenate(parts, axis=1)


def prep_fwd(x, qg, kg, segs, dils, name):
    n, w = x.shape
    tm = _tile(n, 256, 8)

    def fn(ids, xt, a, b, scratch):
        ops = _prep(_plain_split, xt, a, b, segs)
        return tuple(_to_strided(scratch, ops, d) for d in dils)

    return tcall(fn, (n // tm,), [_row(x, tm), _full(qg), _full(kg)],
                 [_out((n // d, d * w), BF16, (tm // d, d * w), lambda i: (i, 0)) for d in dils], name,
                 scratch=((w // BLOCK * tm, BLOCK), F32))


def prep_bwd(x, qg, kg, segs, grads, gather, name):
    n, w = x.shape
    tm = BLOCK
    nblk = n // tm
    nslot = 1 + max(slot for _, _, _, slot in grads)

    def fn(ids, xt, a, b, *t, scratch):
        tiles, dils = [None] * nslot, [None] * nslot
        for ti, (_, sh, d, slot) in zip(t, grads):
            ti = jnp.where(ids[0] + sh < nblk, ti, 0.0) if sh else ti
            tiles[slot] = ti if tiles[slot] is None else tiles[slot] + ti
            dils[slot] = d
        tiles = [_to_natural(scratch, ti, d) for ti, d in zip(tiles, dils)]
        _, vjp = jax.vjp(lambda x_, a_, b_: _prep(_split_dot_vjp, x_, a_, b_, segs), xt, a, b)
        return vjp(gather(*tiles))

    specs = [_in(a, (tm // d, a.shape[1]), (lambda i, sh=sh: (jnp.minimum(i + sh, nblk - 1), 0)))
             for a, sh, d, _ in grads]
    wmax = max(a.shape[1] // d for a, _, d, _ in grads)
    return tcall(fn, (nblk,), [_row(x, tm), _full(qg), _full(kg)] + specs,
                 [_row_out(n, w, BF16, tm), _acc_out(qg.shape), _acc_out(kg.shape)], name,
                 scratch=((wmax // BLOCK * tm, BLOCK), F32))


def rmsnorm_fwd(x, g, name, side=None):
    n, d = x.shape
    tm = _tile(n, 512, 8)
    res = tcall(lambda ids, xt, gt: (_rms(xt, gt),), (n // tm,), [_row(x, tm), _full(g)],
                [_row_out(n, d, BF16, tm)], name, side=side)
    if side is None:
        return res[0]
    return res[0][0], res[1]


def ffn_fwd(x, g, w_gu, w_down, tag, carry=None):
    h = rmsnorm_fwd(x, g, tag + "_norm")
    if carry is None:
        gate, up, a = mm_gate_up(h, w_gu, tag + "_gu")
        return mm(a, w_down, "nn", tag + "_down", scale=0.5, res=x), (x, h, gate, up, a)
    phase, bufs = carry
    (gate, up, a), bufs = mm_gate_up(h, w_gu, tag + "_gu", side=gather_side(phase, bufs))
    y, bufs = mm(a, w_down, "nn", tag + "_down", scale=0.5, res=x, side=gather_side(phase + 1, bufs))
    return y, (x, h, gate, up, a), bufs


def ffn_bwd(dy, saved, g, w_gu, w_down, tag, chain=None):
    x, h, gate, up, a = saved

    def carrying(name, call, **kw):
        side = None if chain is None else chain.side(name)
        out = call(name=tag + "_" + name, side=side, **kw)
        if side is None:
            return out
        chain.done(name, out[1])
        return out[0]

    dgate, dup = carrying("da", mm_down_act_bwd, dy=dy, w_down=w_down, gate=gate, up=up)
    d_wdown = carrying("dwd", mm, a=a, b=dy, mode="tn", scale=0.5)
    d_wgu = (carrying("dwgu", mm, a=dgate, b=h, mode="tn"), mm(dup, h, "tn", tag + "_dwup"))
    dx, dg = mm_norm_bwd((dgate, dup), w_gu, x, g, dy, tag + "_dh", b_kd=True)
    return dx, dg, d_wgu, d_wdown


def _alibi(n_heads):
    return [float(s) for s in np.asarray(2.0 ** (-8.0 * np.arange(1, n_heads + 1) / n_heads), dtype=np.float32)]


def _banded_tile(dot, first, q, kp, kc, vp, vc, sinks, *, hkv, grp, max_dist, step, slopes, want_lse):
    row = lax.broadcasted_iota(jnp.int32, (BLOCK, 2 * BLOCK), 0)
    col = lax.broadcasted_iota(jnp.int32, (BLOCK, 2 * BLOCK), 1)
    dist = row + BLOCK - col
    valid = (dist >= 0) & (dist <= max_dist) & ((col >= BLOCK) | jnp.logical_not(first))
    distf = dist.astype(F32)

    def head(hd, qh, k2, v2):
        s = dot(qh, k2, True) * (HEAD_DIM ** -0.5)
        s = jnp.where(valid, s - (slopes[hd] * step) * distf, NEG_BIG)
        m = jnp.max(s, axis=-1, keepdims=True)
        if sinks is not None:
            pick = lax.broadcasted_iota(jnp.int32, sinks.shape, 1) == hd
            sk = jnp.sum(jnp.where(pick, sinks, 0.0), axis=1, keepdims=True)
            m = jnp.maximum(m, sk)
        m = lax.stop_gradient(m)
        p = jnp.exp(s - m)
        denom = jnp.sum(p, axis=-1, keepdims=True)
        if sinks is not None:
            denom = denom + jnp.exp(sk - m)
        return dot(p * (1.0 / denom), v2, False), m + jnp.log(denom)

    outs, lses = [], []
    if grp == 1:
        low = lax.broadcasted_iota(jnp.int32, (BLOCK, BLOCK), 1) < HEAD_DIM
        for pr in range(hkv // 2):
            sl = slice(pr * BLOCK, (pr + 1) * BLOCK)
            q2 = q[:, sl]
            k2 = jnp.concatenate([kp[:, sl], kc[:, sl]], axis=0)
            v2 = jnp.concatenate([vp[:, sl], vc[:, sl]], axis=0)
            o0, l0 = head(2 * pr, jnp.where(low, q2, 0.0), k2, v2)
            o1, l1 = head(2 * pr + 1, jnp.where(low, 0.0, q2), k2, v2)
            outs.append(jnp.where(low, o0, o1))
            lses.append(jnp.where(low, l0, l1))
    else:
        for hk in range(hkv):
            sl = slice(hk * HEAD_DIM, (hk + 1) * HEAD_DIM)
            k2 = jnp.concatenate([kp[:, sl], kc[:, sl]], axis=0)
            v2 = jnp.concatenate([vp[:, sl], vc[:, sl]], axis=0)
            for gi in range(grp):
                hd = hk * grp + gi
                o_h, l_h = head(hd, q[:, hd * HEAD_DIM:(hd + 1) * HEAD_DIM], k2, v2)
                outs.append(o_h)
                lses.append(jnp.broadcast_to(l_h, (BLOCK, HEAD_DIM)))
    o = jnp.concatenate(outs, axis=1)
    if want_lse:
        return o, jnp.concatenate(lses, axis=1)
    return (o,)


def _banded_specs(view, qcol, kcol, vcol, wq, wkv):
    def at(colfn, prev):
        if prev:
            return lambda r, n: (jnp.maximum(n - 1, 0), colfn(r))
        return lambda r, n: (n, colfn(r))
    return [
        _in(view, (BLOCK, wq), at(qcol, False)),
        _in(view, (BLOCK, wkv), at(kcol, True)),
        _in(view, (BLOCK, wkv), at(kcol, False)),
        _in(view, (BLOCK, wkv), at(vcol, True)),
        _in(view, (BLOCK, wkv), at(vcol, False)),
    ]


def banded_fwd(view, dil, cols, sinks, cfg, name):
    ns = view.shape[0]
    nb = ns // BLOCK
    wq, wkv = cfg["hkv"] * cfg["grp"] * HEAD_DIM, cfg["hkv"] * HEAD_DIM
    has_sinks = sinks is not None

    def fn(ids, q, kp, kc, vp, vc, *rest):
        q, kp, kc, vp, vc = [a.astype(F32) for a in (q, kp, kc, vp, vc)]
        return _banded_tile(_plain_dot, ids[1] == 0, q, kp, kc, vp, vc, rest[0] if has_sinks else None, **cfg)

    ins = _banded_specs(view, *cols, wq, wkv) + ([_full(sinks)] if has_sinks else [])
    outs = [_out((ns, dil * wq), F32 if cfg["want_lse"] else BF16, (BLOCK, wq), lambda r, n: (n, r))]
    if cfg["want_lse"]:
        outs.append(_out((ns, dil * wq), F32, (BLOCK, wq), lambda r, n: (n, r)))
    return tcall(fn, (dil, nb), ins, outs, name)


def banded_bwd(view, dil, cols, sinks, cfg, cts, name):
    ns = view.shape[0]
    nb = ns // BLOCK
    wq, wkv = cfg["hkv"] * cfg["grp"] * HEAD_DIM, cfg["hkv"] * HEAD_DIM
    has_sinks = sinks is not None
    assert len(cts) == (2 if cfg["want_lse"] else 1)

    def fn(ids, q, kp, kc, vp, vc, *rest):
        sk = rest[0] if has_sinks else None
        ct = rest[1 if has_sinks else 0:]
        first = ids[1] == 0

        def f(q, kp, kc, vp, vc, *s):
            return _banded_tile(_dot_vjp, first, q, kp, kc, vp, vc, s[0] if has_sinks else None, **cfg)

        prim = tuple(a.astype(F32) for a in (q, kp, kc, vp, vc)) + ((sk,) if has_sinks else ())
        _, vjp = jax.vjp(f, *prim)
        return vjp(tuple(c.astype(F32) for c in ct))

    ins = (_banded_specs(view, *cols, wq, wkv) + ([_full(sinks)] if has_sinks else [])
           + [_in(a, (BLOCK, wq), (lambda r, n, cf=cf: (n, cf(r)))) for (a, cf) in cts])
    blk = lambda w: _out((ns, dil * w), F32, (BLOCK, w), lambda r, n: (n, r))
    outs = [blk(wq), blk(wkv), blk(wkv), blk(wkv), blk(wkv)]
    if has_sinks:
        outs.append(_acc_out(sinks.shape))
    return tcall(fn, (dil, nb), ins, outs, name)


def _log_sigmoid(z):
    return jnp.minimum(z, 0.0) - jnp.log(1.0 + jnp.exp(-jnp.abs(z)))


SB_PAIRS = 4


def _sb_pair(dot, suffix, qh, kb, vb, r_in, mask):
    z = dot(qh, kb, True) * (HEAD_DIM ** -0.5)
    lsp = _log_sigmoid(z)
    log_keep = jnp.where(mask, lsp - z, 0.0)
    log_after = suffix(log_keep) + r_in
    a = jnp.where(mask, jnp.exp(lsp + log_after), 0.0)
    return dot(a, vb, False), r_in + jnp.sum(log_keep, axis=1, keepdims=True)


def sb_fwd(qkv, qcb, kcb, vcb, name, side=None):
    s = qkv.shape[0]
    nb = s // BLOCK
    pairs = B_HEADS // 2
    wide = SB_PAIRS * BLOCK
    assert pairs % SB_PAIRS == 0 and qcb % SB_PAIRS == 0 and kcb % SB_PAIRS == 0 and vcb % SB_PAIRS == 0

    def body(q_ref, k_ref, v_ref, o_ref):
        n = pl.program_id(1)
        low = lax.broadcasted_iota(jnp.int32, (BLOCK, BLOCK), 1) < HEAD_DIM
        before = (lax.broadcasted_iota(jnp.int32, (2 * BLOCK, BLOCK), 1)
                  < jnp.bitwise_and(lax.broadcasted_iota(jnp.int32, (2 * BLOCK, BLOCK), 0), BLOCK - 1))
        after = _tri(True)
        suffix = lambda t: _split_dot(t, after)
        qs = []
        for p in range(SB_PAIRS):
            q2 = q_ref[:, p * BLOCK:(p + 1) * BLOCK].astype(F32)
            qs.append(jnp.concatenate([jnp.where(low, q2, 0.0), jnp.where(low, 0.0, q2)], axis=0))

        def cond(c):
            return jnp.logical_and(c[0] >= 0, c[1] > SB_SKIP_LOG)

        def step(c):
            kb, _, rs, accs = c
            rows = pl.ds(pl.multiple_of(kb * BLOCK, BLOCK), BLOCK)
            mask = jnp.logical_or(before, kb != n)
            new_r, new_acc, top = [], [], None
            for p in range(SB_PAIRS):
                cols = slice(p * BLOCK, (p + 1) * BLOCK)
                o_part, r_out = _sb_pair(_plain_dot, suffix, qs[p], k_ref[rows, cols], v_ref[rows, cols], rs[p], mask)
                new_r.append(r_out)
                new_acc.append(accs[p] + o_part)
                top = jnp.max(r_out) if top is None else jnp.maximum(top, jnp.max(r_out))
            return kb - 1, top, tuple(new_r), tuple(new_acc)

        init = (n, jnp.float32(0.0), tuple(jnp.zeros((2 * BLOCK, 1), F32) for _ in range(SB_PAIRS)),
                tuple(jnp.zeros((2 * BLOCK, BLOCK), F32) for _ in range(SB_PAIRS)))
        accs = lax.while_loop(cond, step, init)[3]
        for p in range(SB_PAIRS):
            o_ref[:, p * BLOCK:(p + 1) * BLOCK] = jnp.where(low, accs[p][:BLOCK], accs[p][BLOCK:]).astype(o_ref.dtype)

    return _pcall(
        body, side=side, name=name, grid=(pairs // SB_PAIRS, nb),
        in_specs=[pl.BlockSpec((BLOCK, wide), lambda g, n: (n, qcb // SB_PAIRS + g)),
                  pl.BlockSpec((s, wide), lambda g, n: (0, kcb // SB_PAIRS + g), pipeline_mode=pl.Buffered(1)),
                  pl.BlockSpec((s, wide), lambda g, n: (0, vcb // SB_PAIRS + g), pipeline_mode=pl.Buffered(1))],
        out_specs=pl.BlockSpec((BLOCK, wide), lambda g, n: (n, g)),
        out_shape=jax.ShapeDtypeStruct((s, pairs * BLOCK), BF16),
        compiler_params=_params(),
    )(qkv, qkv, qkv)


def sb_bwd(qkv, qcb, kcb, vcb, do, docb, name, side=None):
    s = qkv.shape[0]
    nb = s // BLOCK
    pairs = B_HEADS // 2
    wide = SB_PAIRS * BLOCK
    assert docb % SB_PAIRS == 0

    def body(q_ref, k_ref, v_ref, do_ref, dq_ref, dk_ref, dv_ref, r_ref):
        n = pl.program_id(1)

        @pl.when(n == 0)
        def _():
            dk_ref[...] = jnp.zeros(dk_ref.shape, F32)
            dv_ref[...] = jnp.zeros(dv_ref.shape, F32)

        low = lax.broadcasted_iota(jnp.int32, (BLOCK, BLOCK), 1) < HEAD_DIM
        before = (lax.broadcasted_iota(jnp.int32, (2 * BLOCK, BLOCK), 1)
                  < jnp.bitwise_and(lax.broadcasted_iota(jnp.int32, (2 * BLOCK, BLOCK), 0), BLOCK - 1))
        after, earlier = _tri(True), _tri(False)
        suffix = lambda t: _split_dot_vjp(t, after, earlier, 2)
        stack = lambda t: jnp.concatenate([jnp.where(low, t, 0.0), jnp.where(low, 0.0, t)], axis=0)
        qs = [stack(q_ref[:, p * BLOCK:(p + 1) * BLOCK].astype(F32)) for p in range(SB_PAIRS)]
        dos = [stack(do_ref[:, p * BLOCK:(p + 1) * BLOCK].astype(F32)) for p in range(SB_PAIRS)]

        def cond(c):
            return jnp.logical_and(c[0] >= 0, c[1] > SB_SKIP_LOG)

        def down(c):
            kb, _, rs = c
            rows = pl.ds(pl.multiple_of(kb * BLOCK, BLOCK), BLOCK)
            mask = jnp.logical_or(before, kb != n)
            new_r, top = [], None
            for h in range(SB_PAIRS):
                cols = slice(h * BLOCK, (h + 1) * BLOCK)
                r_ref[h, kb] = rs[h]
                z = _dot(qs[h], k_ref[rows, cols], NT) * (HEAD_DIM ** -0.5)
                log_keep = jnp.where(mask, _log_sigmoid(z) - z, 0.0)
                r_out = rs[h] + jnp.sum(log_keep, axis=1, keepdims=True)
                new_r.append(r_out)
                top = jnp.max(r_out) if top is None else jnp.maximum(top, jnp.max(r_out))
            return kb - 1, top, tuple(new_r)

        init = (n, jnp.float32(0.0), tuple(jnp.zeros((2 * BLOCK, 1), F32) for _ in range(SB_PAIRS)))
        last = lax.while_loop(cond, down, init)[0] + 1

        def up(kb, c):
            dqs, g_rs = c
            rows = pl.ds(pl.multiple_of(kb * BLOCK, BLOCK), BLOCK)
            mask = jnp.logical_or(before, kb != n)
            new_dq, new_g = [], []
            for h in range(SB_PAIRS):
                cols = slice(h * BLOCK, (h + 1) * BLOCK)
                _, vjp = jax.vjp(lambda q_, k_, v_, r_: _sb_pair(_dot_vjp, suffix, q_, k_, v_, r_, mask),
                                 qs[h], k_ref[rows, cols].astype(F32), v_ref[rows, cols].astype(F32), r_ref[h, kb])
                dq_c, dk_c, dv_c, g_in = vjp((dos[h], g_rs[h]))
                dk_ref[rows, cols] += dk_c
                dv_ref[rows, cols] += dv_c
                new_dq.append(dqs[h] + dq_c)
                new_g.append(g_in)
            return tuple(new_dq), tuple(new_g)

        init = (tuple(jnp.zeros((2 * BLOCK, BLOCK), F32) for _ in range(SB_PAIRS)),
                tuple(jnp.zeros((2 * BLOCK, 1), F32) for _ in range(SB_PAIRS)))
        dqs = lax.fori_loop(last, n + 1, up, init)[0]
        for p in range(SB_PAIRS):
            dq_ref[:, p * BLOCK:(p + 1) * BLOCK] = jnp.where(low, dqs[p][:BLOCK], dqs[p][BLOCK:])

    full = jax.ShapeDtypeStruct((s, pairs * BLOCK), F32)
    return _pcall(
        body, side=side, name=name, grid=(pairs // SB_PAIRS, nb),
        in_specs=[pl.BlockSpec((BLOCK, wide), lambda g, n: (n, qcb // SB_PAIRS + g)),
                  pl.BlockSpec((s, wide), lambda g, n: (0, kcb // SB_PAIRS + g), pipeline_mode=pl.Buffered(1)),
                  pl.BlockSpec((s, wide), lambda g, n: (0, vcb // SB_PAIRS + g), pipeline_mode=pl.Buffered(1)),
                  pl.BlockSpec((BLOCK, wide), lambda g, n: (n, docb // SB_PAIRS + g))],
        out_specs=[pl.BlockSpec((BLOCK, wide), lambda g, n: (n, g)),
                   pl.BlockSpec((s, wide), lambda g, n: (0, g), pipeline_mode=pl.Buffered(1)),
                   pl.BlockSpec((s, wide), lambda g, n: (0, g), pipeline_mode=pl.Buffered(1))],
        out_shape=[full, full, full],
        scratch_shapes=[pltpu.VMEM((SB_PAIRS, nb, 2 * BLOCK, 1), F32)],
        compiler_params=_params(),
    )(qkv, qkv, qkv, do)


def _xa_tile(dot, q, kv, qg, kg):
    hd = q.shape[1] // X_HEADS
    outs = []
    for h in range(X_HEADS):
        qh = _rms(q[:, h * hd:(h + 1) * hd], qg)
        kh = _rms(kv[:, h * hd:(h + 1) * hd], kg)
        vh = kv[:, (X_HEADS + h) * hd:(X_HEADS + h + 1) * hd]
        sc = dot(qh, kh, True) * (hd ** -0.5)
        m = lax.stop_gradient(jnp.max(sc, axis=-1, keepdims=True))
        p = jnp.exp(sc - m)
        outs.append(dot(p * (1.0 / jnp.sum(p, axis=-1, keepdims=True)), vh, False))
    return jnp.concatenate(outs, axis=1)


def xa_core_fwd(q, kv, qg, kg, name):
    n, d = q.shape
    tm = _tile(n, 256, 8)
    (o,) = tcall(lambda ids, qt, kvt, qgt, kgt: (_xa_tile(_plain_dot, qt, kvt, qgt, kgt),), (n // tm,),
                 [_row(q, tm), _full(kv), _full(qg), _full(kg)], [_row_out(n, d, BF16, tm)], name)
    return o


def xa_core_bwd(q, kv, qg, kg, do, name):
    n, d = q.shape
    tm = _tile(n, 256, 8)

    def fn(ids, qt, kvt, qgt, kgt, dot_):
        _, vjp = jax.vjp(functools.partial(_xa_tile, _dot_vjp), qt, kvt, qgt, kgt)
        return vjp(dot_.astype(F32))

    return tcall(fn, (n // tm,), [_row(q, tm), _full(kv), _full(qg), _full(kg), _row(do, tm)],
                 [_row_out(n, d, BF16, tm), _acc_out(kv.shape), _acc_out(qg.shape), _acc_out(kg.shape)], name)


def _ev_reorder(a):
    return jnp.concatenate([a[0:512], a[768:2304], a[512:768]], axis=0)


def _ev_restore(a):
    return jnp.concatenate([a[0:512], a[2048:2304], a[512:2048]], axis=0)


_EV_SEGS = ((0, 512, "q"), (512, 1536, "raw"), (2048, 128, "k"), (2176, 128, "raw"))
_A_CFG = dict(hkv=A_KV_HEADS, grp=A_Q_HEADS // A_KV_HEADS, max_dist=BLOCK - 1, step=1.0, slopes=_alibi(A_Q_HEADS),
              want_lse=False)
_A_COLS = (lambda r: 0, lambda r: 16, lambda r: 17)


def even_mixer_fwd(x, h, w_in, qg, kg, sinks, w_out, tag, side=None):
    qkv = mm(h, w_in, "nt", tag + "_in")
    (ops,) = prep_fwd(qkv, qg, kg, _EV_SEGS, (1,), tag + "_prep")
    (o_a,) = banded_fwd(ops, 1, _A_COLS, sinks, _A_CFG, tag + "_swa")
    o_b = sb_fwd(ops, 4, 8, 12, tag + "_sb", side=side)
    carried = None
    if side is not None:
        o_b, carried = o_b
    o = jnp.concatenate([o_a, o_b], axis=1)
    y = mm(o, w_out, "nn", tag + "_out", res=x)
    return y, (x, h, qkv, ops, o), carried


def even_mixer_bwd(dy, saved, g, w_in, qg, kg, sinks, w_out, tag, side=None):
    x, h, qkv, ops, o = saved
    do = mm(dy, w_out, "nt", tag + "_do", out_dtype=BF16)
    d_wout = mm(o, dy, "tn", tag + "_dwout")
    dqa, dkp, dkc, dvp, dvc, dsinks = banded_bwd(ops, 1, _A_COLS, sinks, _A_CFG, [(do, lambda r: 0)], tag + "_dswa")
    res = sb_bwd(ops, 4, 8, 12, do, 4, tag + "_dsb", side=side)
    carried = None
    if side is not None:
        res, carried = res
    dqb, dkb, dvb = res
    dqkv, dqg, dkg = prep_bwd(
        qkv, qg, kg, _EV_SEGS,
        [(dqa, 0, 1, 0), (dqb, 0, 1, 1), (dkb, 0, 1, 2), (dvb, 0, 1, 3), (dkc, 0, 1, 4), (dkp, 1, 1, 4), (dvc, 0, 1, 5),
         (dvp, 1, 1, 5)],
        lambda *t: jnp.concatenate(t, axis=1), tag + "_dqkv")
    d_win = mm(dqkv, h, "tn", tag + "_dwin")
    dx, dg = mm_norm_bwd(dqkv, w_in, x, g, dy, tag + "_dh", b_kd=True)
    return dx, dg, d_win, dqg, dkg, dsinks, d_wout, carried


def _c_cfg(window, dil):
    return dict(hkv=C_HEADS, grp=1, max_dist=window // dil, step=float(dil), slopes=_alibi(C_HEADS), want_lse=True)


_C_COLS = (lambda r: 3 * r, lambda r: 3 * r + 1, lambda r: 3 * r + 2)
_OD_SEGS = ((0, 1024, "q"), (1024, 1024, "k"), (2048, 1024, "raw"))


def _combine(o1, o2, o3, l1, l2, l3):
    m = lax.stop_gradient(jnp.maximum(jnp.maximum(l1, l2), l3))
    e1, e2, e3 = jnp.exp(l1 - m), jnp.exp(l2 - m), jnp.exp(l3 - m)
    tot = e1 + e2 + e3
    return (e1 / tot) * o1 + (e2 / tot) * o2 + (e3 / tot) * o3


def odd_mixer_fwd(x, g, w_in, qg, kg, w_out, tag):
    n, d = x.shape
    h = rmsnorm_fwd(x, g, tag + "_norm")
    qkv = mm(h, w_in, "nt", tag + "_in")
    dils = [dil for _, dil in C_PATTERNS]
    ops = prep_fwd(qkv, qg, kg, _OD_SEGS, dils, tag + "_prep")
    os_, ls_ = [], []
    for (window, dil), ops_d in zip(C_PATTERNS, ops):
        o_p, l_p = banded_fwd(ops_d, dil, _C_COLS, None, _c_cfg(window, dil), f"{tag}_dil{dil}")
        os_.append(o_p)
        ls_.append(l_p)
    tm = BLOCK
    lay = lambda a, dil: _in(a, (tm // dil, a.shape[1]), lambda i: (i, 0))
    views = [lay(a, dil) for a, dil in zip(os_ + ls_, dils + dils)]

    def comb(ids, *t, scratch):
        return (_combine(*[_to_natural(scratch, a, dil) for a, dil in zip(t, dils + dils)]),)

    (o,) = tcall(comb, (n // tm,), views, [_row_out(n, d, BF16, tm)], tag + "_comb",
                 scratch=((d // BLOCK * tm, BLOCK), F32))
    y = mm(o, w_out, "nn", tag + "_out", res=x)
    return y, (x, h, qkv, ops, views, o)


def odd_mixer_bwd(dy, saved, g, w_in, qg, kg, w_out, tag):
    x, h, qkv, ops, views, o = saved
    n, d = x.shape
    do = mm(dy, w_out, "nt", tag + "_do")
    d_wout = mm(o, dy, "tn", tag + "_dwout")
    tm = BLOCK
    dils = [dil for _, dil in C_PATTERNS]

    def comb_bwd(ids, *t, scratch):
        _, vjp = jax.vjp(_combine, *[_to_natural(scratch, a, dil) for a, dil in zip(t[:6], dils + dils)])
        return tuple(_to_strided(scratch, c, dil) for c, dil in zip(vjp(t[6]), dils + dils))

    cts = tcall(comb_bwd, (n // tm,), views + [_row(do, tm)],
                [_out((n // dil, dil * d), F32, (tm // dil, dil * d), lambda i: (i, 0)) for dil in dils + dils],
                tag + "_dcomb", scratch=((d // BLOCK * tm, BLOCK), F32))
    dqs, dks, dvs = [], [], []
    for p, ((window, dil), ops_d) in enumerate(zip(C_PATTERNS, ops)):
        dq, dkp, dkc, dvp, dvc = banded_bwd(ops_d, dil, _C_COLS, None, _c_cfg(window, dil),
                                            [(cts[p], lambda r: r), (cts[3 + p], lambda r: r)], f"{tag}_ddil{dil}")
        dqs.append((dq, 0, dil, p))
        dks += [(dkc, 0, dil, 3 + p), (dkp, dil, dil, 3 + p)]
        dvs += [(dvc, 0, dil, 6 + p), (dvp, dil, dil, 6 + p)]

    def gather(*t):
        return jnp.concatenate([t[0] + t[1] + t[2], t[3] + t[4] + t[5], t[6] + t[7] + t[8]], axis=1)

    dqkv, dqg, dkg = prep_bwd(qkv, qg, kg, _OD_SEGS, dqs + dks + dvs, gather, tag + "_dqkv")
    d_win = mm(dqkv, h, "tn", tag + "_dwin")
    dx, dg = mm_norm_bwd(dqkv, w_in, x, g, dy, tag + "_dh", b_kd=True)
    return dx, dg, d_win, dqg, dkg, d_wout


def xa_fwd(x, mem, g, gm, w_q, w_kv, qg, kg, w_o, tag):
    h = rmsnorm_fwd(x, g, tag + "_norm")
    q = mm(h, w_q, "nn", tag + "_q")
    mn = rmsnorm_fwd(mem, gm, tag + "_mnorm")
    kv = mm(mn, w_kv, "nt", tag + "_kv")
    o = xa_core_fwd(q, kv, qg, kg, tag + "_core")
    y = mm(o, w_o, "nn", tag + "_o", res=x)
    return y, (x, h, q, mn, kv, o)


def xa_bwd(dy, saved, mem, g, gm, w_q, w_kv, qg, kg, w_o, tag):
    x, h, q, mn, kv, o = saved
    do = mm(dy, w_o, "nt", tag + "_do", out_dtype=BF16)
    d_wo = mm(o, dy, "tn", tag + "_dwo")
    dq, dkv, dqg, dkg = xa_core_bwd(q, kv, qg, kg, do, tag + "_dcore")
    d_wq = mm(h, dq, "tn", tag + "_dwq")
    dx, dg = mm_norm_bwd(dq, w_q, x, g, dy, tag + "_dh")
    d_wkv = mm(dkv, mn, "tn", tag + "_dwkv")
    _, dgm = mm_norm_bwd(dkv, w_kv, mem, gm, None, tag + "_dmn", b_kd=True)
    return dx, dg, dgm, d_wq, d_wkv, dqg, dkg, d_wo


def loss_head(y, target, name):
    n, d = y.shape
    tm = _tile(n, 512, 8)

    def fn(ids, yt, tt):
        e = yt - tt
        return e * (1.0 / d), jnp.sum(e * e, axis=0, keepdims=True)

    return tcall(fn, (n // tm,), [_row(y, tm), _row(target, tm)], [_row_out(n, d, F32, tm), _acc_out((1, d))], name)


_ANY = pl.BlockSpec(memory_space=pl.ANY)


def all_gather_blocks(blocks):
    nb = len(blocks)

    def body(*refs):
        x_refs, out_refs = refs[:nb], refs[nb:2 * nb]
        send_sems, recv_sems, local_sems = refs[2 * nb:]
        x, y, c = lax.axis_index("x"), lax.axis_index("y"), lax.axis_index("c")
        me, sibling = (x, y, c), (x, y, 1 - c)
        over_x, over_y, diagonal = (1 - x, y), (x, 1 - y), (1 - x, 1 - y)
        relay_of = ((1 - x) * (1 - c) + x * c, y * (1 - c) + (1 - y) * c)
        relay_to = (x * (1 - c) + (1 - x) * c, (1 - y) * (1 - c) + y * c)

        def copy(b, k, blk, to, own=False):
            px, py, pc = blk
            slot = out_refs[b].at[4 * px + 2 * py + pc]
            return pltpu.make_async_remote_copy(
                src_ref=x_refs[b] if own else slot, dst_ref=slot,
                send_sem=send_sems.at[7 * b + k], recv_sem=recv_sems.at[7 * b + k], device_id=to, device_id_type=MESH)

        mine = [pltpu.make_async_copy(x_refs[b], out_refs[b].at[4 * x + 2 * y + c], local_sems.at[b]) for b in range(nb)]
        for cp in mine:
            cp.start()
        sent = []
        for b in range(nb):
            sent += [copy(b, 0, me, sibling, own=True), copy(b, 1, me, (*over_x, c), own=True),
                     copy(b, 2, me, (*over_y, c), own=True)]
        for cp in sent:
            cp.start()
        for b in range(nb):
            copy(b, 1, (*over_x, c), me).wait_recv()
            copy(b, 2, (*over_y, c), me).wait_recv()
            later = [copy(b, 3, (*relay_of, c), (*relay_to, c)), copy(b, 4, (*over_x, c), sibling),
                     copy(b, 5, (*over_y, c), sibling)]
            for cp in later:
                cp.start()
            sent += later
        for b in range(nb):
            copy(b, 3, (*diagonal, c), me).wait_recv()
            fwd = copy(b, 6, (*diagonal, c), sibling)
            fwd.start()
            sent.append(fwd)
        for b in range(nb):
            copy(b, 0, sibling, me).wait_recv()
            for k, chip in ((4, over_x), (5, over_y), (6, diagonal)):
                copy(b, k, (*chip, 1 - c), me).wait_recv()
        for cp in sent:
            cp.wait_send()
        for cp in mine:
            cp.wait()

    return _pcall(
        body, name="weights_all_gather",
        in_specs=[_ANY] * nb, out_specs=[_ANY] * nb,
        out_shape=[jax.ShapeDtypeStruct((N_DEV,) + a.shape, a.dtype) for a in blocks],
        scratch_shapes=[pltpu.SemaphoreType.DMA((7 * nb,)), pltpu.SemaphoreType.DMA((7 * nb,)),
                        pltpu.SemaphoreType.DMA((nb,))],
    )(*blocks)


def pair_exchange(bufs):
    nb = len(bufs)

    def body(*refs):
        srcs, dsts = refs[:nb], refs[nb:2 * nb]
        send_sems, recv_sems = refs[2 * nb:]
        x, y, c = lax.axis_index("x"), lax.axis_index("y"), lax.axis_index("c")
        copies = []
        for b in range(nb):
            for j in range(4):
                cp = pltpu.make_async_remote_copy(
                    src_ref=srcs[b].at[2 * j + (1 - c)], dst_ref=dsts[b].at[j], send_sem=send_sems.at[4 * b + j],
                    recv_sem=recv_sems.at[4 * b + j], device_id=(x, y, 1 - c), device_id_type=MESH)
                cp.start()
                copies.append(cp)
        for cp in copies:
            cp.wait()

    return _pcall(
        body, name="grads_pair_exchange",
        in_specs=[_ANY] * nb, out_specs=[_ANY] * nb,
        out_shape=[jax.ShapeDtypeStruct((4,) + a.shape[1:], a.dtype) for a in bufs],
        scratch_shapes=[pltpu.SemaphoreType.DMA((4 * nb,)), pltpu.SemaphoreType.DMA((4 * nb,))],
    )(*bufs)


def pair_sum(g, got, c, out_dtype, name):
    r, w = g.shape[1:]
    tr = _tile(r, 512, 16)

    def body(c_ref, a_ref, b_ref, o_ref):
        o_ref[...] = (a_ref[...].astype(F32) + b_ref[...].astype(F32)).astype(o_ref.dtype)

    return _pcall(
        body, name=name,
        grid_spec=pltpu.PrefetchScalarGridSpec(
            num_scalar_prefetch=1, grid=(4, r // tr),
            in_specs=[pl.BlockSpec((None, tr, w), lambda j, i, c_ref: (2 * j + c_ref[0], i, 0)),
                      pl.BlockSpec((None, tr, w), lambda j, i, c_ref: (j, i, 0))],
            out_specs=pl.BlockSpec((None, tr, w), lambda j, i, c_ref: (j, i, 0))),
        out_shape=jax.ShapeDtypeStruct((4,) + g.shape[1:], out_dtype),
        compiler_params=_params(),
    )(c, g, got)


def chip_exchange(parts):
    nb = len(parts)

    def body(*refs):
        srcs, dsts = refs[:nb], refs[nb:2 * nb]
        send_sems, recv_sems, local_sems = refs[2 * nb:]
        x, y, c = lax.axis_index("x"), lax.axis_index("y"), lax.axis_index("c")
        my_chip = 2 * x + y
        copies = []
        for b in range(nb):
            mine = pltpu.make_async_copy(srcs[b].at[my_chip], dsts[b].at[my_chip], local_sems.at[b])
            mine.start()
            copies.append(mine)
            for k, (tx, ty) in enumerate([(1 - x, y), (x, 1 - y), (1 - x, 1 - y)]):
                cp = pltpu.make_async_remote_copy(
                    src_ref=srcs[b].at[2 * tx + ty], dst_ref=dsts[b].at[my_chip], send_sem=send_sems.at[3 * b + k],
                    recv_sem=recv_sems.at[3 * b + k], device_id=(tx, ty, c), device_id_type=MESH)
                cp.start()
                copies.append(cp)
        for cp in copies:
            cp.wait()

    return _pcall(
        body, name="grads_chip_exchange",
        in_specs=[_ANY] * nb, out_specs=[_ANY] * nb,
        out_shape=[jax.ShapeDtypeStruct(a.shape, a.dtype) for a in parts],
        scratch_shapes=[pltpu.SemaphoreType.DMA((3 * nb,)), pltpu.SemaphoreType.DMA((3 * nb,)),
                        pltpu.SemaphoreType.DMA((nb,))],
    )(*parts)


def chip_sum(parts, name):
    r, w = parts.shape[1:]
    tr = _tile(r, 512, 16)
    spec = lambda j: _in(parts, (None, tr, w), lambda i, j=j: (j, i, 0))

    def fn(ids, a, b, c_, d):
        a, b, c_, d = [t.astype(F32) for t in (a, b, c_, d)]
        return (((a + b) + c_) + d,)

    (out,) = tcall(fn, (r // tr,), [spec(j) for j in range(4)],
                   [_out((r, w), F32, (tr, w), lambda i: (i, 0))], name)
    return out


def _remote(src, dst, send_sems, recv_sems, k, to):
    return functools.partial(pltpu.make_async_remote_copy, src_ref=src, dst_ref=dst, send_sem=send_sems.at[k],
                             recv_sem=recv_sems.at[k], device_id=to, device_id_type=MESH)


def _gather_plan(phase, nb):
    def plan(ins, outs, send_sems, recv_sems, local_sems):
        x, y, c = lax.axis_index("x"), lax.axis_index("y"), lax.axis_index("c")
        me, sibling = (x, y, c), (x, y, 1 - c)
        over_x, over_y, diagonal = (1 - x, y), (x, 1 - y), (1 - x, 1 - y)
        relay_of = ((1 - x) * (1 - c) + x * c, y * (1 - c) + (1 - y) * c)
        relay_to = (x * (1 - c) + (1 - x) * c, (1 - y) * (1 - c) + y * c)
        local, sends, recvs = [], [], []
        for b in range(nb):
            slot = lambda chip, core, b=b: outs[b].at[4 * chip[0] + 2 * chip[1] + core]
            if phase == 0:
                local.append(functools.partial(pltpu.make_async_copy, ins[b], slot((x, y), c), local_sems.at[b]))
                moves = [(ins[b], slot((x, y), c), to) for to in (sibling, (*over_x, c), (*over_y, c))]
                arrive = [slot((x, y), 1 - c), slot(over_x, c), slot(over_y, c)]
            elif phase == 1:
                moves = [(slot(relay_of, c), slot(relay_of, c), (*relay_to, c)),
                         (slot(over_x, c), slot(over_x, c), sibling), (slot(over_y, c), slot(over_y, c), sibling)]
                arrive = [slot(diagonal, c), slot(over_x, 1 - c), slot(over_y, 1 - c)]
            else:
                moves = [(slot(diagonal, c), slot(diagonal, c), sibling)]
                arrive = [slot(diagonal, 1 - c)]
            sends += [_remote(src, dst, send_sems, recv_sems, 3 * b + k, to) for k, (src, dst, to) in enumerate(moves)]
            recvs += [_remote(dst, dst, send_sems, recv_sems, 3 * b + k, me) for k, dst in enumerate(arrive)]
        return local, sends, recvs
    return plan


def gather_side(phase, arrays):
    nb = len(arrays)
    if phase == 0:
        shapes = [jax.ShapeDtypeStruct((N_DEV,) + a.shape, a.dtype) for a in arrays]
        return Side(arrays, shapes, 3 * nb, nb, _gather_plan(0, nb))
    shapes = [jax.ShapeDtypeStruct(a.shape, a.dtype) for a in arrays]
    return Side(arrays, shapes, 3 * nb, 0, _gather_plan(phase, nb), aliased=True)


def pair_side(bufs):
    nb = len(bufs)

    def plan(ins, outs, send_sems, recv_sems, local_sems):
        x, y, c = lax.axis_index("x"), lax.axis_index("y"), lax.axis_index("c")
        sends = [_remote(ins[b].at[2 * j + (1 - c)], outs[b].at[j], send_sems, recv_sems, 4 * b + j, (x, y, 1 - c))
                 for b in range(nb) for j in range(4)]
        recvs = [_remote(outs[b].at[j], outs[b].at[j], send_sems, recv_sems, 4 * b + j, (x, y, c))
                 for b in range(nb) for j in range(4)]
        return [], sends, recvs

    shapes = [jax.ShapeDtypeStruct((4,) + a.shape[1:], a.dtype) for a in bufs]
    return Side(bufs, shapes, 4 * nb, 0, plan)


def chip_side(parts):
    nb = len(parts)

    def plan(ins, outs, send_sems, recv_sems, local_sems):
        x, y, c = lax.axis_index("x"), lax.axis_index("y"), lax.axis_index("c")
        my_chip = 2 * x + y
        peers = [(1 - x, y), (x, 1 - y), (1 - x, 1 - y)]
        local = [functools.partial(pltpu.make_async_copy, ins[b].at[my_chip], outs[b].at[my_chip], local_sems.at[b])
                 for b in range(nb)]
        sends = [_remote(ins[b].at[2 * tx + ty], outs[b].at[my_chip], send_sems, recv_sems, 3 * b + k, (tx, ty, c))
                 for b in range(nb) for k, (tx, ty) in enumerate(peers)]
        recvs = [_remote(outs[b].at[2 * tx + ty], outs[b].at[2 * tx + ty], send_sems, recv_sems, 3 * b + k, (x, y, c))
                 for b in range(nb) for k, (tx, ty) in enumerate(peers)]
        return local, sends, recvs

    shapes = [jax.ShapeDtypeStruct(a.shape, a.dtype) for a in parts]
    return Side(parts, shapes, 3 * nb, nb, plan)


def adamw(w, g, m, v, name):
    shape = w.shape
    cols = shape[-1]
    rows = int(np.prod(shape[:-1]))
    w2, g2, m2, v2 = [a.reshape(rows, cols) for a in (w, g, m, v)]
    tr = _tile(rows, 256, 8) if rows % 8 == 0 else rows

    def fn(ids, wt, gt, mt, vt):
        m_new = ADAM_B1 * mt + (1.0 - ADAM_B1) * gt
        v_new = ADAM_B2 * vt + (1.0 - ADAM_B2) * (gt * gt)
        m_hat = m_new / (1.0 - ADAM_B1 ** ADAM_STEP)
        v_hat = v_new / (1.0 - ADAM_B2 ** ADAM_STEP)
        delta = -ADAM_LR * (m_hat / (jnp.sqrt(v_hat) + ADAM_EPS) + ADAM_WD * wt)
        return delta, m_new, v_new

    res = tcall(fn, (rows // tr,), [_row(a, tr) for a in (w2, g2, m2, v2)],
                [_row_out(rows, cols, F32, tr) for _ in range(3)], name)
    return [a.reshape(shape) for a in res]


_MATS = [("ffn1_w_gu", "col"), ("ffn1_w_down", "row"), ("ev_w_in", "col"), ("ev_w_out", "row"),
         ("od_w_in", "col"), ("od_w_out", "row"), ("xa_w_q", "row"), ("xa_w_kv", "col"), ("xa_w_o", "row"),
         ("ffn2_w_gu", "col"), ("ffn2_w_down", "row")]
_VECS = ["ffn1_norm", "mix_norm", "ev_q_gain", "ev_k_gain", "ev_sinks", "od_q_gain", "od_k_gain", "xa_norm",
         "xa_mem_norm", "xa_q_gain", "xa_k_gain", "ffn2_norm"]
_WEIGHTS = ["ffn1_norm", "ffn1_w_gu", "ffn1_w_down", "mix_norm", "ev_w_in", "ev_q_gain", "ev_k_gain", "ev_sinks",
            "ev_w_out", "od_w_in", "od_q_gain", "od_k_gain", "od_w_out", "xa_norm", "xa_mem_norm", "xa_w_q", "xa_w_kv",
            "xa_q_gain", "xa_k_gain", "xa_w_o", "ffn2_norm", "ffn2_w_gu", "ffn2_w_down"]


_AXIS = dict(_MATS)
DEPTH = 2


def _layer_groups(l):
    first, rest = _first_block_groups(l)
    return [first[0] + rest[0] + rest[1]]


def _first_block_groups(l):
    w_in, w_out = ("ev_w_in", "ev_w_out") if l % 2 == 0 else ("od_w_in", "od_w_out")
    first = [[("ffn1_w_gu", l), ("ffn1_w_down", l)]]
    rest = [[("ffn2_w_gu", l), ("xa_w_kv", l)],
            [(w_in, l // 2), ("ffn2_w_down", l), (w_out, l // 2), ("xa_w_q", l), ("xa_w_o", l)]]
    return first, rest


def _block_rows(shards, n):
    a, b = shards[n].shape[1:]
    return a if _AXIS[n] == "row" else b


def _weight_blocks(shards, groups):
    blocks = []
    for group in groups:
        rows = [(shards[n][j] if _AXIS[n] == "row" else shards[n][j].T).astype(BF16) for n, j in group]
        blocks.append(rows[0] if len(rows) == 1 else jnp.concatenate(rows, axis=0))
    return blocks


def _whole_weights(shards, groups, gathered):
    full = {}
    for group, got in zip(groups, gathered):
        off = 0
        for n, j in group:
            r = _block_rows(shards, n)
            full[n] = got[:, off:off + r, :].reshape(N_DEV * r, got.shape[2])
            off += r
    return full


def _gradient_buffers(grads, groups):
    bufs = []
    for group in groups:
        rows = []
        for n, _ in group:
            whole = jnp.concatenate(grads[n], axis=0) if isinstance(grads[n], tuple) else grads[n]
            rows.append(whole.reshape(N_DEV, whole.shape[0] // N_DEV, whole.shape[1]))
        bufs.append((rows[0] if len(rows) == 1 else jnp.concatenate(rows, axis=1)).astype(BF16))
    return bufs


def _gradient_blocks(shards, groups, sums):
    out = {}
    for group, tot in zip(groups, sums):
        off = 0
        for n, j in group:
            r = _block_rows(shards, n)
            out[n, j] = tot[off:off + r] if _AXIS[n] == "row" else tot[off:off + r].T
            off += r
    return out


class _PairChain:
    def __init__(self, ex, bufs):
        self.ex, self.bufs, self.parts = ex, bufs, None

    def side(self, name):
        return pair_side(self.bufs) if name == "dwd" else None

    def done(self, name, carried):
        self.parts = self.ex.pair_sums(self.bufs, carried, "l1")


class _RestChain:
    HALF = {"dwd": (0,), "dwgu": (1,)}

    def __init__(self, ex, bufs):
        self.ex, self.bufs, self.parts, self.sums = ex, bufs, None, [None] * len(bufs)

    def side(self, name):
        if name == "da":
            return pair_side(self.bufs)
        return chip_side([self.parts[i] for i in self.HALF[name]])

    def done(self, name, carried):
        if name == "da":
            self.parts = self.ex.pair_sums(self.bufs, carried, "l0r")
        else:
            for i, tot in zip(self.HALF[name], self.ex.chip_sums(carried, "l0r_" + name)):
                self.sums[i] = tot


class _Exchange:
    def __init__(self, shards, c):
        self.shards, self.c = shards, c

    def weights_first(self):
        first, _ = _first_block_groups(0)
        return _whole_weights(self.shards, first, all_gather_blocks(_weight_blocks(self.shards, first)))

    def rest_blocks(self):
        return _weight_blocks(self.shards, _first_block_groups(0)[1])

    def weights_rest(self, gathered):
        return _whole_weights(self.shards, _first_block_groups(0)[1], gathered)

    def gather_start(self):
        return gather_side(0, _weight_blocks(self.shards, _layer_groups(1)))

    def weights_next(self, gathered):
        return _whole_weights(self.shards, _layer_groups(1), gathered)

    def chain_next(self, grads):
        return _PairChain(self, _gradient_buffers(grads, _layer_groups(1)))

    def chain_rest(self, grads):
        return _RestChain(self, _gradient_buffers(grads, _first_block_groups(0)[1]))

    def pair_sums(self, bufs, got, tag):
        return [pair_sum(b, g, self.c, b.dtype, f"grads_pair_sum_{tag}_{i}") for i, (b, g) in enumerate(zip(bufs, got))]

    def chip_sums(self, parts, tag):
        return [chip_sum(p, f"grads_chip_sum_{tag}_{i}") for i, p in enumerate(parts)]

    def finish(self, gm, gv, sums1, sums_rest):
        vecs = {n: jnp.concatenate(v, axis=0) for n, v in gv.items()}
        first, rest = _first_block_groups(0)
        bufs = _gradient_buffers(gm[0], first)
        vec = jnp.concatenate([vecs[n].reshape(-1) for n in _VECS])
        vec = jnp.pad(vec, (0, -vec.shape[0] % (16 * LANES)))
        bufs.append(jnp.broadcast_to(vec.reshape(1, -1, LANES), (N_DEV, vec.shape[0] // LANES, LANES)))
        parts = self.pair_sums(bufs, pair_exchange(bufs), "l0")
        sums0 = self.chip_sums(chip_exchange(parts), "l0")
        blocks = {**_gradient_blocks(self.shards, first, sums0[:-1]), **_gradient_blocks(self.shards, rest, sums_rest),
                  **_gradient_blocks(self.shards, _layer_groups(1), sums1)}
        out = {n: jnp.stack([blocks[n, j] for j in range(self.shards[n].shape[0])]) for n, _ in _MATS}
        flat, off = sums0[-1].reshape(-1), 0
        for n in _VECS:
            out[n] = flat[off:off + vecs[n].size].reshape(vecs[n].shape)
            off += vecs[n].size
        return out


class _NoExchange:
    def __init__(self, full):
        self.full = full

    def weights_first(self):
        return self.full[0]

    def rest_blocks(self):
        return None

    def gather_start(self):
        return None

    def weights_next(self, gathered):
        return self.full[1]

    def chain_next(self, grads):
        return None

    def chain_rest(self, grads):
        return None

    def finish(self, gm, gv, sums1, sums_rest):
        mats = {}
        for l in range(DEPTH):
            for group in _layer_groups(l):
                for n, j in group:
                    whole = jnp.concatenate(gm[l][n], axis=0) if isinstance(gm[l][n], tuple) else gm[l][n]
                    mats.setdefault(n, {})[j] = whole if _AXIS[n] == "row" else whole.T
        mats = {n: jnp.stack([v[j] for j in sorted(v)]) for n, v in mats.items()}
        return mats, {n: jnp.concatenate(v, axis=0) for n, v in gv.items()}


def _local_step(x, mem, target, w, ex):
    assert w["ffn1_norm"].shape[0] == DEPTH
    row = lambda a, l: a[l:l + 1]
    full = [ex.weights_first(), None]
    saved = []
    for l in range(DEPTH):
        t, j, f = f"l{l}", l // 2, full[l]
        rest = ex.rest_blocks() if l == 0 else None
        if rest is None:
            x, s1 = ffn_fwd(x, row(w["ffn1_norm"], l), f["ffn1_w_gu"], f["ffn1_w_down"], t + "_ffn1")
        else:
            x, s1, rest = ffn_fwd(x, row(w["ffn1_norm"], l), f["ffn1_w_gu"], f["ffn1_w_down"], t + "_ffn1", (0, rest))
        relay = None
        if l % 2 == 0:
            h = rmsnorm_fwd(x, row(w["mix_norm"], l), t + "_ev_norm", None if rest is None else gather_side(2, rest))
            if rest is not None:
                h, rest = h
                f = full[l] = {**f, **ex.weights_rest(rest)}
            side = ex.gather_start() if l + 1 < DEPTH else None
            x, s2, relay = even_mixer_fwd(x, h, _ev_reorder(f["ev_w_in"]), row(w["ev_q_gain"], j),
                                          row(w["ev_k_gain"], j), row(w["ev_sinks"], j), f["ev_w_out"], t + "_ev", side)
        else:
            x, s2 = odd_mixer_fwd(x, row(w["mix_norm"], l), f["od_w_in"], row(w["od_q_gain"], j),
                                  row(w["od_k_gain"], j), f["od_w_out"], t + "_od")
        x, s3 = xa_fwd(x, mem, row(w["xa_norm"], l), row(w["xa_mem_norm"], l), f["xa_w_q"], f["xa_w_kv"],
                       row(w["xa_q_gain"], l), row(w["xa_k_gain"], l), f["xa_w_o"], t + "_xa")
        if relay is None:
            x, s4 = ffn_fwd(x, row(w["ffn2_norm"], l), f["ffn2_w_gu"], f["ffn2_w_down"], t + "_ffn2")
        else:
            x, s4, relay = ffn_fwd(x, row(w["ffn2_norm"], l), f["ffn2_w_gu"], f["ffn2_w_down"], t + "_ffn2", (1, relay))
        if l + 1 < DEPTH:
            full[l + 1] = ex.weights_next(relay)
        saved.append((s1, s2, s3, s4))
    dx, sq = loss_head(x, target, "loss_head")
    loss = 0.5 * jnp.sum(sq) / x.shape[1]

    gm = [dict() for _ in range(DEPTH)]
    gv = {n: [None] * w[n].shape[0] for n in _VECS}
    chain1 = chain0 = sums1 = None
    for l in reversed(range(DEPTH)):
        t, j, f = f"l{l}", l // 2, full[l]
        s1, s2, s3, s4 = saved[l]
        dx, gv["ffn2_norm"][l], gm[l]["ffn2_w_gu"], gm[l]["ffn2_w_down"] = ffn_bwd(
            dx, s4, row(w["ffn2_norm"], l), f["ffn2_w_gu"], f["ffn2_w_down"], t + "_ffn2", chain1 if l == 0 else None)
        parts = chain1.parts if l == 0 and chain1 is not None else None
        (dx, gv["xa_norm"][l], gv["xa_mem_norm"][l], gm[l]["xa_w_q"], gm[l]["xa_w_kv"], gv["xa_q_gain"][l],
         gv["xa_k_gain"][l], gm[l]["xa_w_o"]) = xa_bwd(
            dx, s3, mem, row(w["xa_norm"], l), row(w["xa_mem_norm"], l), f["xa_w_q"], f["xa_w_kv"],
            row(w["xa_q_gain"], l), row(w["xa_k_gain"], l), f["xa_w_o"], t + "_xa")
        if l % 2 == 0:
            (dx, gv["mix_norm"][l], d_win, gv["ev_q_gain"][j], gv["ev_k_gain"][j], gv["ev_sinks"][j],
             gm[l]["ev_w_out"], carried) = even_mixer_bwd(
                dx, s2, row(w["mix_norm"], l), _ev_reorder(f["ev_w_in"]), row(w["ev_q_gain"], j), row(w["ev_k_gain"], j),
                row(w["ev_sinks"], j), f["ev_w_out"], t + "_ev", None if parts is None else chip_side(parts))
            gm[l]["ev_w_in"] = _ev_restore(d_win)
            if carried is not None:
                sums1 = ex.chip_sums(carried, "l1")
        else:
            (dx, gv["mix_norm"][l], gm[l]["od_w_in"], gv["od_q_gain"][j], gv["od_k_gain"][j],
             gm[l]["od_w_out"]) = odd_mixer_bwd(
                dx, s2, row(w["mix_norm"], l), f["od_w_in"], row(w["od_q_gain"], j), row(w["od_k_gain"], j),
                f["od_w_out"], t + "_od")
        if l == 0:
            chain0 = ex.chain_rest(gm[l])
        dx, gv["ffn1_norm"][l], gm[l]["ffn1_w_gu"], gm[l]["ffn1_w_down"] = ffn_bwd(
            dx, s1, row(w["ffn1_norm"], l), f["ffn1_w_gu"], f["ffn1_w_down"], t + "_ffn1", chain0 if l == 0 else None)
        if l == 1:
            chain1 = ex.chain_next(gm[l])
    return loss, dx, ex.finish(gm, gv, sums1, None if chain0 is None else chain0.sums)


def kernel(x, mem, ffn1_norm, ffn1_w_gu, ffn1_w_down, mix_norm, ev_w_in, ev_q_gain, ev_k_gain, ev_sinks, ev_w_out, od_w_in, od_q_gain, od_k_gain, od_w_out, xa_norm, xa_mem_norm, xa_w_q, xa_w_kv, xa_q_gain, xa_k_gain, xa_w_o, ffn2_norm, ffn2_w_gu, ffn2_w_down, loss_target, m_ffn1_norm, m_ffn1_w_gu, m_ffn1_w_down, m_mix_norm, m_ev_w_in, m_ev_q_gain, m_ev_k_gain, m_ev_sinks, m_ev_w_out, m_od_w_in, m_od_q_gain, m_od_k_gain, m_od_w_out, m_xa_norm, m_xa_mem_norm, m_xa_w_q, m_xa_w_kv, m_xa_q_gain, m_xa_k_gain, m_xa_w_o, m_ffn2_norm, m_ffn2_w_gu, m_ffn2_w_down, v_ffn1_norm, v_ffn1_w_gu, v_ffn1_w_down, v_mix_norm, v_ev_w_in, v_ev_q_gain, v_ev_k_gain, v_ev_sinks, v_ev_w_out, v_od_w_in, v_od_q_gain, v_od_k_gain, v_od_w_out, v_xa_norm, v_xa_mem_norm, v_xa_w_q, v_xa_w_kv, v_xa_q_gain, v_xa_k_gain, v_xa_w_o, v_ffn2_norm, v_ffn2_w_gu, v_ffn2_w_down):
    w = dict(ffn1_norm=ffn1_norm, ffn1_w_gu=ffn1_w_gu, ffn1_w_down=ffn1_w_down, mix_norm=mix_norm, ev_w_in=ev_w_in, ev_q_gain=ev_q_gain, ev_k_gain=ev_k_gain, ev_sinks=ev_sinks, ev_w_out=ev_w_out, od_w_in=od_w_in, od_q_gain=od_q_gain, od_k_gain=od_k_gain, od_w_out=od_w_out, xa_norm=xa_norm, xa_mem_norm=xa_mem_norm, xa_w_q=xa_w_q, xa_w_kv=xa_w_kv, xa_q_gain=xa_q_gain, xa_k_gain=xa_k_gain, xa_w_o=xa_w_o, ffn2_norm=ffn2_norm, ffn2_w_gu=ffn2_w_gu, ffn2_w_down=ffn2_w_down)
    m = dict(ffn1_norm=m_ffn1_norm, ffn1_w_gu=m_ffn1_w_gu, ffn1_w_down=m_ffn1_w_down, mix_norm=m_mix_norm, ev_w_in=m_ev_w_in, ev_q_gain=m_ev_q_gain, ev_k_gain=m_ev_k_gain, ev_sinks=m_ev_sinks, ev_w_out=m_ev_w_out, od_w_in=m_od_w_in, od_q_gain=m_od_q_gain, od_k_gain=m_od_k_gain, od_w_out=m_od_w_out, xa_norm=m_xa_norm, xa_mem_norm=m_xa_mem_norm, xa_w_q=m_xa_w_q, xa_w_kv=m_xa_w_kv, xa_q_gain=m_xa_q_gain, xa_k_gain=m_xa_k_gain, xa_w_o=m_xa_w_o, ffn2_norm=m_ffn2_norm, ffn2_w_gu=m_ffn2_w_gu, ffn2_w_down=m_ffn2_w_down)
    v = dict(ffn1_norm=v_ffn1_norm, ffn1_w_gu=v_ffn1_w_gu, ffn1_w_down=v_ffn1_w_down, mix_norm=v_mix_norm, ev_w_in=v_ev_w_in, ev_q_gain=v_ev_q_gain, ev_k_gain=v_ev_k_gain, ev_sinks=v_ev_sinks, ev_w_out=v_ev_w_out, od_w_in=v_od_w_in, od_q_gain=v_od_q_gain, od_k_gain=v_od_k_gain, od_w_out=v_od_w_out, xa_norm=v_xa_norm, xa_mem_norm=v_xa_mem_norm, xa_w_q=v_xa_w_q, xa_w_kv=v_xa_w_kv, xa_q_gain=v_xa_q_gain, xa_k_gain=v_xa_k_gain, xa_w_o=v_xa_w_o, ffn2_norm=v_ffn2_norm, ffn2_w_gu=v_ffn2_w_gu, ffn2_w_down=v_ffn2_w_down)

    c = lax.axis_index("c").astype(jnp.int32).reshape(1)
    loss, dx, grads = _local_step(x[0], mem[0], loss_target[0], w, _Exchange(w, c))
    loss = lax.psum(loss, ("x", "y", "c"))

    delta, new_m, new_v = {}, {}, {}
    for n in _WEIGHTS:
        delta[n], new_m[n], new_v[n] = adamw(w[n], grads[n], m[n], v[n], "adamw_" + n)
    return (loss, dx[None], *[grads[n] for n in _WEIGHTS], *[delta[n] for n in _WEIGHTS],
            *[new_m[n] for n in _WEIGHTS], *[new_v[n] for n in _WEIGHTS])
```

```python
import functools

import numpy as np
import jax
import jax.numpy as jnp
from jax import lax
from jax.experimental import pallas as pl
from jax.experimental.pallas import tpu as pltpu

F32 = jnp.float32
BF16 = jnp.bfloat16
MESH = pl.DeviceIdType.MESH

HEAD_DIM = 64
BLOCK = 128
RMS_EPS = 1e-6
A_Q_HEADS, A_KV_HEADS = 8, 2
B_HEADS = 8
C_HEADS = 16
C_PATTERNS = ((128, 1), (512, 4), (2048, 16))
X_HEADS = 4
N_DEV = 8
LANES = 1024
VMEM_LIMIT_BYTES = 56 * 1024 * 1024
SB_SKIP_LOG = -110.0
NEG_BIG = -1e30

ADAM_LR, ADAM_B1, ADAM_B2, ADAM_EPS, ADAM_WD, ADAM_STEP = 0.001, 0.9, 0.999, 1e-08, 0.01, 10

NN = (((1,), (0,)), ((), ()))
NT = (((1,), (1,)), ((), ()))
TN = (((0,), (0,)), ((), ()))


class Side:
    def __init__(self, arrays, out_shapes, n_remote, n_local, plan, aliased=False):
        self.arrays, self.out_shapes, self.plan, self.aliased = list(arrays), list(out_shapes), plan, aliased
        self.sems = [pltpu.SemaphoreType.DMA((n_remote,)), pltpu.SemaphoreType.DMA((n_remote,)),
                     pltpu.SemaphoreType.DMA((max(n_local, 1),))]

    def start(self, ins, outs, sems):
        local, sends, _ = self.plan(ins, outs, *sems)
        for make in local + sends:
            make().start()

    def wait(self, ins, outs, sems):
        local, sends, recvs = self.plan(ins, outs, *sems)
        for make in sends:
            make().wait_send()
        for make in recvs:
            make().wait_recv()
        for make in local:
            make().wait()


def _pcall(body, side=None, **kw):
    if side is None:
        return pl.pallas_call(body, **kw)
    grid = kw["grid"]
    single = not isinstance(kw["out_specs"], (list, tuple))
    out_specs = [kw["out_specs"]] if single else list(kw["out_specs"])
    out_shape = [kw["out_shape"]] if single else list(kw["out_shape"])
    scratch = list(kw.get("scratch_shapes", []))
    n_in, n_out, n_scr, n_side = len(kw["in_specs"]), len(out_specs), len(scratch), len(side.arrays)
    n_sout = len(side.out_shapes)

    def hosted(*refs):
        ins, s_in = refs[:n_in], refs[n_in:n_in + n_side]
        outs = refs[n_in + n_side:n_in + n_side + n_out]
        s_out = refs[n_in + n_side + n_out:n_in + n_side + n_out + n_sout]
        rest = refs[n_in + n_side + n_out + n_sout:]
        scr, sems = rest[:n_scr], rest[n_scr:]
        first = last = None
        for a, size in enumerate(grid):
            f, l = pl.program_id(a) == 0, pl.program_id(a) == size - 1
            first = f if first is None else jnp.logical_and(first, f)
            last = l if last is None else jnp.logical_and(last, l)

        @pl.when(first)
        def _():
            side.start(s_in, s_out, sems)

        body(*ins, *outs, *scr)

        @pl.when(last)
        def _():
            side.wait(s_in, s_out, sems)

    any_space = pl.BlockSpec(memory_space=pl.ANY)
    kw2 = dict(kw)
    kw2.update(in_specs=list(kw["in_specs"]) + [any_space] * n_side, out_specs=out_specs + [any_space] * n_sout,
               out_shape=out_shape + side.out_shapes, scratch_shapes=scratch + side.sems)
    if side.aliased:
        kw2["input_output_aliases"] = {n_in + i: n_out + i for i in range(n_side)}
    call = pl.pallas_call(hosted, **kw2)

    def run(*args):
        res = call(*args, *side.arrays)
        return (res[0] if single else list(res[:n_out])), list(res[n_out:])

    return run


def _params(**kw):
    return pltpu.CompilerParams(vmem_limit_bytes=VMEM_LIMIT_BYTES, **kw)


def _tile(dim, cap, unit=128):
    if dim <= cap:
        return dim
    t = (cap // unit) * unit
    while t >= unit:
        if dim % t == 0:
            return t
        t -= unit
    raise ValueError(f"no tile for {dim} under {cap}")


def _dot(a, b, dims):
    return lax.dot_general(a.astype(BF16), b.astype(BF16), dims, preferred_element_type=F32)


@functools.partial(jax.custom_vjp, nondiff_argnums=(2,))
def _dot_vjp(a, b, nt):
    return _dot(a, b, NT if nt else NN)


def _dot_vjp_fwd(a, b, nt):
    return _dot(a, b, NT if nt else NN), (a.astype(BF16), b.astype(BF16))


def _dot_vjp_bwd(nt, res, g):
    a, b = res
    if nt:
        return _dot(g, b, NN), _dot(g, a, TN)
    return _dot(g, b, NT), _dot(a, g, TN)


_dot_vjp.defvjp(_dot_vjp_fwd, _dot_vjp_bwd)


def _plain_dot(a, b, nt):
    return _dot(a, b, NT if nt else NN)


def _split_dot(x, mat, terms=2):
    out, rem = None, x
    for t in range(terms):
        part = rem.astype(BF16)
        d = lax.dot_general(part, mat, NN, preferred_element_type=F32)
        out = d if out is None else out + d
        if t + 1 < terms:
            rem = rem - part.astype(F32)
    return out


@functools.partial(jax.custom_vjp, nondiff_argnums=(3,))
def _split_dot_vjp(x, mat, mat_t, terms):
    return _split_dot(x, mat, terms)


def _split_dot_vjp_fwd(x, mat, mat_t, terms):
    return _split_dot(x, mat, terms), mat_t


def _split_dot_vjp_bwd(terms, mat_t, g):
    return _split_dot(g, mat_t, terms), None, None


_split_dot_vjp.defvjp(_split_dot_vjp_fwd, _split_dot_vjp_bwd)


def _plain_split(x, mat, mat_t, terms):
    return _split_dot(x, mat, terms)


def _tri(after):
    j = lax.broadcasted_iota(jnp.int32, (BLOCK, BLOCK), 0)
    s = lax.broadcasted_iota(jnp.int32, (BLOCK, BLOCK), 1)
    return jnp.where(j > s if after else j < s, 1.0, 0.0).astype(BF16)


def _in(a, block, imap):
    return (a, block, imap)


def _out(shape, dtype, block, imap, acc=False):
    return (shape, dtype, block, imap, acc)


def tcall(fn, grid, ins, outs, name, scratch=None, side=None):
    nin = len(ins)
    nout = len(outs)
    ngrid = len(grid)

    def body(*refs):
        ids = tuple(pl.program_id(a) for a in range(ngrid))
        extra = {} if scratch is None else {"scratch": refs[nin + nout]}
        res = fn(ids, *[r[...] for r in refs[:nin]], **extra)
        first = ids[0] == 0
        for a in range(1, ngrid):
            first = jnp.logical_and(first, ids[a] == 0)
        for o_ref, r, spec in zip(refs[nin:nin + nout], res, outs):
            if spec[4]:
                @pl.when(first)
                def _(o_ref=o_ref):
                    o_ref[...] = jnp.zeros(o_ref.shape, o_ref.dtype)
                o_ref[...] += r.astype(o_ref.dtype)
            else:
                o_ref[...] = r.astype(o_ref.dtype)

    return _pcall(
        body, side=side, name=name, grid=grid,
        in_specs=[pl.BlockSpec(b, m) for (_, b, m) in ins],
        out_specs=[pl.BlockSpec(b, m) for (_, _, b, m, _) in outs],
        out_shape=[jax.ShapeDtypeStruct(s, d) for (s, d, _, _, _) in outs],
        scratch_shapes=[] if scratch is None else [pltpu.VMEM(*scratch)],
        compiler_params=_params(),
    )(*[a for (a, _, _) in ins])


def _to_strided(scr, nat, d):
    if d == 1:
        return nat
    t, w = nat.shape
    nc = w // BLOCK
    for c in range(nc):
        scr[c * t:(c + 1) * t, :] = nat[:, c * BLOCK:(c + 1) * BLOCK]
    return jnp.concatenate([scr[pl.ds(c * t + r, t // d, stride=d), :] for r in range(d) for c in range(nc)], axis=1)


def _to_natural(scr, st, d):
    if d == 1:
        return st.astype(F32)
    t, w = st.shape[0] * d, st.shape[1] // d
    nc = w // BLOCK
    st = st.astype(F32)
    for r in range(d):
        for c in range(nc):
            scr[pl.ds(c * t + r, t // d, stride=d), :] = st[:, r * w + c * BLOCK:r * w + (c + 1) * BLOCK]
    return jnp.concatenate([scr[c * t:(c + 1) * t, :] for c in range(nc)], axis=1)


def _row(a, tm, width=None, cb=0):
    width = a.shape[1] if width is None else width
    return _in(a, (tm, width), lambda i, cb=cb: (i, cb))


def _full(a):
    zeros = (0,) * a.ndim
    return _in(a, a.shape, lambda *ids: zeros)


def _row_out(n, width, dtype, tm):
    return _out((n, width), dtype, (tm, width), lambda i: (i, 0))


def _acc_out(shape):
    zeros = (0,) * len(shape)
    return _out(shape, F32, shape, lambda *ids: zeros, acc=True)


def mm(a, b, mode, name, *, out_dtype=None, scale=1.0, res=None, side=None):
    if out_dtype is None:
        out_dtype = BF16 if mode == "tn" else F32
    if mode == "nn":
        (m, k), (k2, n) = a.shape, b.shape
    elif mode == "nt":
        (m, k), (n, k2) = a.shape, b.shape
    else:
        (k, m), (k2, n) = a.shape, b.shape
    assert k == k2, (a.shape, b.shape, mode)
    tm, tn, tk = _tile(m, 1408 if mode == "tn" else 512), _tile(n, 1408), _tile(k, 1408)
    nk = k // tk
    dims = {"nn": NN, "nt": NT, "tn": TN}[mode]
    has_res = res is not None

    def body(*refs):
        if has_res:
            a_ref, b_ref, r_ref, o_ref, acc_ref = refs
        else:
            a_ref, b_ref, o_ref, acc_ref = refs
        kk = pl.program_id(2)

        @pl.when(kk == 0)
        def _():
            acc_ref[...] = jnp.zeros(acc_ref.shape, F32)

        acc_ref[...] += _dot(a_ref[...], b_ref[...], dims)

        @pl.when(kk == nk - 1)
        def _():
            out = acc_ref[...]
            if scale != 1.0:
                out = out * scale
            if has_res:
                out = out + r_ref[...]
            o_ref[...] = out.astype(o_ref.dtype)

    a_spec = (pl.BlockSpec((tk, tm), lambda i, j, kk: (kk, i)) if mode == "tn"
              else pl.BlockSpec((tm, tk), lambda i, j, kk: (i, kk)))
    b_spec = (pl.BlockSpec((tn, tk), lambda i, j, kk: (j, kk)) if mode == "nt"
              else pl.BlockSpec((tk, tn), lambda i, j, kk: (kk, j)))
    in_specs = [a_spec, b_spec]
    args = [a, b]
    if has_res:
        in_specs.append(pl.BlockSpec((tm, tn), lambda i, j, kk: (i, j)))
        args.append(res)
    order = ("parallel", "parallel", "arbitrary") if side is None else ("arbitrary",) * 3
    return _pcall(
        body, side=side, name=name, grid=(m // tm, n // tn, nk),
        in_specs=in_specs,
        out_specs=pl.BlockSpec((tm, tn), lambda i, j, kk: (i, j)),
        out_shape=jax.ShapeDtypeStruct((m, n), out_dtype),
        scratch_shapes=[pltpu.VMEM((tm, tn), F32)],
        compiler_params=_params(dimension_semantics=order),
    )(*args)


def _rms(x, g):
    return x * lax.rsqrt(jnp.mean(x * x, axis=-1, keepdims=True) + RMS_EPS) * g


def _silu_mul(gate, up):
    return gate / (1.0 + jnp.exp(-gate)) * up


def mm_gate_up(h, w_gu, name, side=None):
    m, k = h.shape
    f = w_gu.shape[0] // 2
    tm, tn = _tile(m, 512), _tile(f, 1408)
    nj = f // tn
    assert k <= 1408

    def body(h_ref, wg_ref, wu_ref, g_ref, u_ref, a_ref):
        ht = h_ref[...]
        for lo in range(0, tn, 512):
            cols = slice(lo, min(lo + 512, tn))
            gate, up = _dot(ht, wg_ref[cols, :], NT), _dot(ht, wu_ref[cols, :], NT)
            g_ref[:, cols] = gate.astype(g_ref.dtype)
            u_ref[:, cols] = up.astype(u_ref.dtype)
            a_ref[:, cols] = _silu_mul(gate, up).astype(a_ref.dtype)

    tile = pl.BlockSpec((tm, tn), lambda i, j: (i, j))
    return _pcall(
        body, side=side, name=name, grid=(m // tm, nj),
        in_specs=[pl.BlockSpec((tm, k), lambda i, j: (i, 0)),
                  pl.BlockSpec((tn, k), lambda i, j: (j, 0)),
                  pl.BlockSpec((tn, k), lambda i, j: (j + nj, 0))],
        out_specs=[tile, tile, tile],
        out_shape=[jax.ShapeDtypeStruct((m, f), BF16), jax.ShapeDtypeStruct((m, f), BF16),
                   jax.ShapeDtypeStruct((m, f), BF16)],
        compiler_params=_params(dimension_semantics=("arbitrary",) * 2),
    )(h, w_gu, w_gu)


def mm_down_act_bwd(dy, w_down, gate, up, name, side=None):
    m, d = dy.shape
    f = w_down.shape[0]
    tm, tn = _tile(m, 512), _tile(f, 1408)
    assert d <= 1408

    def body(dy_ref, w_ref, g_ref, u_ref, dg_ref, du_ref):
        dyt = dy_ref[...].astype(BF16)
        for lo in range(0, tn, 512):
            cols = slice(lo, min(lo + 512, tn))
            da = _dot(dyt, w_ref[cols, :], NT) * 0.5
            gate, up = g_ref[:, cols].astype(F32), u_ref[:, cols].astype(F32)
            s = 1.0 / (1.0 + jnp.exp(-gate))
            gs = gate * s
            du_ref[:, cols] = (da * gs).astype(du_ref.dtype)
            dg_ref[:, cols] = (da * up * s * (1.0 + gate - gs)).astype(dg_ref.dtype)

    tile = pl.BlockSpec((tm, tn), lambda i, j: (i, j))
    return _pcall(
        body, side=side, name=name, grid=(m // tm, f // tn),
        in_specs=[pl.BlockSpec((tm, d), lambda i, j: (i, 0)), pl.BlockSpec((tn, d), lambda i, j: (j, 0)), tile, tile],
        out_specs=[tile, tile],
        out_shape=[jax.ShapeDtypeStruct((m, f), BF16), jax.ShapeDtypeStruct((m, f), BF16)],
        compiler_params=_params(dimension_semantics=("arbitrary", "arbitrary")),
    )(dy, w_down, gate, up)


def mm_norm_bwd(a, b, x, g, dres, name, b_kd=False, side=None):
    halves = isinstance(a, (tuple, list))
    a0, a1 = a if halves else (a, None)
    m, k = a0.shape[0], a0.shape[1] * (2 if halves else 1)
    d = b.shape[1] if b_kd else b.shape[0]
    dims = NN if b_kd else NT
    tm, tk = _tile(m, 512), _tile(a0.shape[1], 1408)
    nk = k // tk
    nkh = a0.shape[1] // tk
    has_res = dres is not None

    def body(*refs):
        a_ref, b_ref, x_ref, g_ref = refs[:4]
        rest = refs[4:-3]
        a1_ref = rest[0] if halves else None
        r_ref = rest[-1] if has_res else None
        dx_ref, dg_ref, acc_ref = refs[-3:]
        i, kk = pl.program_id(0), pl.program_id(1)

        @pl.when(kk == 0)
        def _():
            acc_ref[...] = jnp.zeros(acc_ref.shape, F32)

        if halves:
            @pl.when(kk < nkh)
            def _():
                acc_ref[...] += _dot(a_ref[...], b_ref[...], dims)

            @pl.when(kk >= nkh)
            def _():
                acc_ref[...] += _dot(a1_ref[...], b_ref[...], dims)
        else:
            acc_ref[...] += _dot(a_ref[...], b_ref[...], dims)

        @pl.when(kk == nk - 1)
        def _():
            _, vjp = jax.vjp(_rms, x_ref[...], g_ref[...])
            dx, dg = vjp(acc_ref[...])
            dx_ref[...] = dx + r_ref[...] if has_res else dx

            @pl.when(i == 0)
            def _():
                dg_ref[...] = jnp.zeros(dg_ref.shape, F32)

            dg_ref[...] += dg

    rows = pl.BlockSpec((tm, d), lambda i, kk: (i, 0))
    first = pl.BlockSpec((tm, tk), lambda i, kk: (i, jnp.minimum(kk, nkh - 1)))
    second = pl.BlockSpec((tm, tk), lambda i, kk: (i, jnp.maximum(kk - nkh, 0)))
    b_spec = pl.BlockSpec((tk, d), lambda i, kk: (kk, 0)) if b_kd else pl.BlockSpec((d, tk), lambda i, kk: (0, kk))
    in_specs = ([first, b_spec, rows, pl.BlockSpec(g.shape, lambda i, kk: (0, 0))]
                + ([second] if halves else []) + ([rows] if has_res else []))
    return _pcall(
        body, side=side, name=name, grid=(m // tm, nk),
        in_specs=in_specs,
        out_specs=[rows, pl.BlockSpec(g.shape, lambda i, kk: (0, 0))],
        out_shape=[jax.ShapeDtypeStruct((m, d), F32), jax.ShapeDtypeStruct(g.shape, F32)],
        scratch_shapes=[pltpu.VMEM((tm, d), F32)],
        compiler_params=_params(dimension_semantics=("arbitrary", "arbitrary")),
    )(*([a0, b, x, g] + ([a1] if halves else []) + ([dres] if has_res else [])))


def _indicator(shape, head_axis, mod):
    lane = lax.broadcasted_iota(jnp.int32, shape, head_axis)
    other = lax.broadcasted_iota(jnp.int32, shape, 1 - head_axis)
    lane = jnp.bitwise_and(lane, HEAD_DIM - 1) if mod else jnp.right_shift(lane, 6)
    return jnp.where(lane == other, 1.0, 0.0).astype(BF16)


def _head_rms(split, xs, g):
    w = xs.shape[1]
    to_head, from_head = _indicator((w, BLOCK), 0, False), _indicator((BLOCK, w), 1, False)
    to_lane, from_lane = _indicator((HEAD_DIM, w), 1, True), _indicator((w, HEAD_DIM), 0, True)
    ss = split(xs * xs, to_head, from_head, 3)
    r = lax.rsqrt(ss * (1.0 / HEAD_DIM) + RMS_EPS)
    g_all = split(jnp.broadcast_to(g, (8, HEAD_DIM)), to_lane, from_lane, 3)[0:1]
    return xs * split(r, from_head, to_head, 3) * g_all


def _prep(split, x, qg, kg, segs):
    parts = []
    for start, width, kind in segs:
        xs = x[:, start:start + width]
        parts.append(xs if kind == "raw" else _head_rms(split, xs, qg if kind == "q" else kg))
    return jnp.concatenate(parts, axis=1)


def prep_fwd(x, qg, kg, segs, dils, name):
    n, w = x.shape
    tm = _tile(n, 256, 8)

    def fn(ids, xt, a, b, scratch):
        ops = _prep(_plain_split, xt, a, b, segs)
        return tuple(_to_strided(scratch, ops, d) for d in dils)

    return tcall(fn, (n // tm,), [_row(x, tm), _full(qg), _full(kg)],
                 [_out((n // d, d * w), BF16, (tm // d, d * w), lambda i: (i, 0)) for d in dils], name,
                 scratch=((w // BLOCK * tm, BLOCK), F32))


def prep_bwd(x, qg, kg, segs, grads, gather, name):
    n, w = x.shape
    tm = BLOCK
    nblk = n // tm
    nslot = 1 + max(slot for _, _, _, slot in grads)

    def fn(ids, xt, a, b, *t, scratch):
        tiles, dils = [None] * nslot, [None] * nslot
        for ti, (_, sh, d, slot) in zip(t, grads):
            ti = jnp.where(ids[0] + sh < nblk, ti, 0.0) if sh else ti
            tiles[slot] = ti if tiles[slot] is None else tiles[slot] + ti
            dils[slot] = d
        tiles = [_to_natural(scratch, ti, d) for ti, d in zip(tiles, dils)]
        _, vjp = jax.vjp(lambda x_, a_, b_: _prep(_split_dot_vjp, x_, a_, b_, segs), xt, a, b)
        return vjp(gather(*tiles))

    specs = [_in(a, (tm // d, a.shape[1]), (lambda i, sh=sh: (jnp.minimum(i + sh, nblk - 1), 0)))
             for a, sh, d, _ in grads]
    wmax = max(a.shape[1] // d for a, _, d, _ in grads)
    return tcall(fn, (nblk,), [_row(x, tm), _full(qg), _full(kg)] + specs,
                 [_row_out(n, w, BF16, tm), _acc_out(qg.shape), _acc_out(kg.shape)], name,
                 scratch=((wmax // BLOCK * tm, BLOCK), F32))


def rmsnorm_fwd(x, g, name, side=None):
    n, d = x.shape
    tm = _tile(n, 512, 8)
    res = tcall(lambda ids, xt, gt: (_rms(xt, gt),), (n // tm,), [_row(x, tm), _full(g)],
                [_row_out(n, d, BF16, tm)], name, side=side)
    if side is None:
        return res[0]
    return res[0][0], res[1]


def ffn_fwd(x, g, w_gu, w_down, tag, carry=None):
    h = rmsnorm_fwd(x, g, tag + "_norm")
    if carry is None:
        gate, up, a = mm_gate_up(h, w_gu, tag + "_gu")
        return mm(a, w_down, "nn", tag + "_down", scale=0.5, res=x), (x, h, gate, up, a)
    phase, bufs = carry
    (gate, up, a), bufs = mm_gate_up(h, w_gu, tag + "_gu", side=gather_side(phase, bufs))
    y, bufs = mm(a, w_down, "nn", tag + "_down", scale=0.5, res=x, side=gather_side(phase + 1, bufs))
    return y, (x, h, gate, up, a), bufs


def ffn_bwd(dy, saved, g, w_gu, w_down, tag, chain=None):
    x, h, gate, up, a = saved

    def carrying(name, call, **kw):
        side = None if chain is None else chain.side(name)
        out = call(name=tag + "_" + name, side=side, **kw)
        if side is None:
            return out
        chain.done(name, out[1])
        return out[0]

    dgate, dup = carrying("da", mm_down_act_bwd, dy=dy, w_down=w_down, gate=gate, up=up)
    d_wdown = carrying("dwd", mm, a=a, b=dy, mode="tn", scale=0.5)
    d_wgu = (carrying("dwgu", mm, a=dgate, b=h, mode="tn"), mm(dup, h, "tn", tag + "_dwup"))
    dx, dg = carrying("dh", mm_norm_bwd, a=(dgate, dup), b=w_gu, x=x, g=g, dres=dy, b_kd=True)
    return dx, dg, d_wgu, d_wdown


def _alibi(n_heads):
    return [float(s) for s in np.asarray(2.0 ** (-8.0 * np.arange(1, n_heads + 1) / n_heads), dtype=np.float32)]


def _banded_tile(dot, first, q, kp, kc, vp, vc, sinks, *, hkv, grp, max_dist, step, slopes, want_lse):
    row = lax.broadcasted_iota(jnp.int32, (BLOCK, 2 * BLOCK), 0)
    col = lax.broadcasted_iota(jnp.int32, (BLOCK, 2 * BLOCK), 1)
    dist = row + BLOCK - col
    valid = (dist >= 0) & (dist <= max_dist) & ((col >= BLOCK) | jnp.logical_not(first))
    distf = dist.astype(F32)

    def head(hd, qh, k2, v2):
        s = dot(qh, k2, True) * (HEAD_DIM ** -0.5)
        s = jnp.where(valid, s - (slopes[hd] * step) * distf, NEG_BIG)
        m = jnp.max(s, axis=-1, keepdims=True)
        if sinks is not None:
            pick = lax.broadcasted_iota(jnp.int32, sinks.shape, 1) == hd
            sk = jnp.sum(jnp.where(pick, sinks, 0.0), axis=1, keepdims=True)
            m = jnp.maximum(m, sk)
        m = lax.stop_gradient(m)
        p = jnp.exp(s - m)
        denom = jnp.sum(p, axis=-1, keepdims=True)
        if sinks is not None:
            denom = denom + jnp.exp(sk - m)
        return dot(p * (1.0 / denom), v2, False), m + jnp.log(denom)

    outs, lses = [], []
    if grp == 1:
        low = lax.broadcasted_iota(jnp.int32, (BLOCK, BLOCK), 1) < HEAD_DIM
        for pr in range(hkv // 2):
            sl = slice(pr * BLOCK, (pr + 1) * BLOCK)
            q2 = q[:, sl]
            k2 = jnp.concatenate([kp[:, sl], kc[:, sl]], axis=0)
            v2 = jnp.concatenate([vp[:, sl], vc[:, sl]], axis=0)
            o0, l0 = head(2 * pr, jnp.where(low, q2, 0.0), k2, v2)
            o1, l1 = head(2 * pr + 1, jnp.where(low, 0.0, q2), k2, v2)
            outs.append(jnp.where(low, o0, o1))
            lses.append(jnp.where(low, l0, l1))
    else:
        for hk in range(hkv):
            sl = slice(hk * HEAD_DIM, (hk + 1) * HEAD_DIM)
            k2 = jnp.concatenate([kp[:, sl], kc[:, sl]], axis=0)
            v2 = jnp.concatenate([vp[:, sl], vc[:, sl]], axis=0)
            for gi in range(grp):
                hd = hk * grp + gi
                o_h, l_h = head(hd, q[:, hd * HEAD_DIM:(hd + 1) * HEAD_DIM], k2, v2)
                outs.append(o_h)
                lses.append(jnp.broadcast_to(l_h, (BLOCK, HEAD_DIM)))
    o = jnp.concatenate(outs, axis=1)
    if want_lse:
        return o, jnp.concatenate(lses, axis=1)
    return (o,)


def _banded_specs(view, qcol, kcol, vcol, wq, wkv):
    def at(colfn, prev):
        if prev:
            return lambda r, n: (jnp.maximum(n - 1, 0), colfn(r))
        return lambda r, n: (n, colfn(r))
    return [
        _in(view, (BLOCK, wq), at(qcol, False)),
        _in(view, (BLOCK, wkv), at(kcol, True)),
        _in(view, (BLOCK, wkv), at(kcol, False)),
        _in(view, (BLOCK, wkv), at(vcol, True)),
        _in(view, (BLOCK, wkv), at(vcol, False)),
    ]


def banded_fwd(view, dil, cols, sinks, cfg, name):
    ns = view.shape[0]
    nb = ns // BLOCK
    wq, wkv = cfg["hkv"] * cfg["grp"] * HEAD_DIM, cfg["hkv"] * HEAD_DIM
    has_sinks = sinks is not None

    def fn(ids, q, kp, kc, vp, vc, *rest):
        q, kp, kc, vp, vc = [a.astype(F32) for a in (q, kp, kc, vp, vc)]
        return _banded_tile(_plain_dot, ids[1] == 0, q, kp, kc, vp, vc, rest[0] if has_sinks else None, **cfg)

    ins = _banded_specs(view, *cols, wq, wkv) + ([_full(sinks)] if has_sinks else [])
    outs = [_out((ns, dil * wq), F32 if cfg["want_lse"] else BF16, (BLOCK, wq), lambda r, n: (n, r))]
    if cfg["want_lse"]:
        outs.append(_out((ns, dil * wq), F32, (BLOCK, wq), lambda r, n: (n, r)))
    return tcall(fn, (dil, nb), ins, outs, name)


def banded_bwd(view, dil, cols, sinks, cfg, cts, name):
    ns = view.shape[0]
    nb = ns // BLOCK
    wq, wkv = cfg["hkv"] * cfg["grp"] * HEAD_DIM, cfg["hkv"] * HEAD_DIM
    has_sinks = sinks is not None
    assert len(cts) == (2 if cfg["want_lse"] else 1)

    def fn(ids, q, kp, kc, vp, vc, *rest):
        sk = rest[0] if has_sinks else None
        ct = rest[1 if has_sinks else 0:]
        first = ids[1] == 0

        def f(q, kp, kc, vp, vc, *s):
            return _banded_tile(_dot_vjp, first, q, kp, kc, vp, vc, s[0] if has_sinks else None, **cfg)

        prim = tuple(a.astype(F32) for a in (q, kp, kc, vp, vc)) + ((sk,) if has_sinks else ())
        _, vjp = jax.vjp(f, *prim)
        return vjp(tuple(c.astype(F32) for c in ct))

    ins = (_banded_specs(view, *cols, wq, wkv) + ([_full(sinks)] if has_sinks else [])
           + [_in(a, (BLOCK, wq), (lambda r, n, cf=cf: (n, cf(r)))) for (a, cf) in cts])
    blk = lambda w: _out((ns, dil * w), F32, (BLOCK, w), lambda r, n: (n, r))
    outs = [blk(wq), blk(wkv), blk(wkv), blk(wkv), blk(wkv)]
    if has_sinks:
        outs.append(_acc_out(sinks.shape))
    return tcall(fn, (dil, nb), ins, outs, name)


def _log_sigmoid(z):
    return jnp.minimum(z, 0.0) - jnp.log(1.0 + jnp.exp(-jnp.abs(z)))


SB_PAIRS = 4


def _sb_pair(dot, suffix, qh, kb, vb, r_in, mask):
    z = dot(qh, kb, True) * (HEAD_DIM ** -0.5)
    lsp = _log_sigmoid(z)
    log_keep = jnp.where(mask, lsp - z, 0.0)
    log_after = suffix(log_keep) + r_in
    a = jnp.where(mask, jnp.exp(lsp + log_after), 0.0)
    return dot(a, vb, False), r_in + jnp.sum(log_keep, axis=1, keepdims=True)


def sb_fwd(qkv, qcb, kcb, vcb, name, side=None):
    s = qkv.shape[0]
    nb = s // BLOCK
    pairs = B_HEADS // 2
    wide = SB_PAIRS * BLOCK
    assert pairs % SB_PAIRS == 0 and qcb % SB_PAIRS == 0 and kcb % SB_PAIRS == 0 and vcb % SB_PAIRS == 0

    def body(q_ref, k_ref, v_ref, o_ref):
        n = pl.program_id(1)
        low = lax.broadcasted_iota(jnp.int32, (BLOCK, BLOCK), 1) < HEAD_DIM
        before = (lax.broadcasted_iota(jnp.int32, (2 * BLOCK, BLOCK), 1)
                  < jnp.bitwise_and(lax.broadcasted_iota(jnp.int32, (2 * BLOCK, BLOCK), 0), BLOCK - 1))
        after = _tri(True)
        suffix = lambda t: _split_dot(t, after)
        qs = []
        for p in range(SB_PAIRS):
            q2 = q_ref[:, p * BLOCK:(p + 1) * BLOCK].astype(F32)
            qs.append(jnp.concatenate([jnp.where(low, q2, 0.0), jnp.where(low, 0.0, q2)], axis=0))

        def cond(c):
            return jnp.logical_and(c[0] >= 0, c[1] > SB_SKIP_LOG)

        def step(c):
            kb, _, rs, accs = c
            rows = pl.ds(pl.multiple_of(kb * BLOCK, BLOCK), BLOCK)
            mask = jnp.logical_or(before, kb != n)
            new_r, new_acc, top = [], [], None
            for p in range(SB_PAIRS):
                cols = slice(p * BLOCK, (p + 1) * BLOCK)
                o_part, r_out = _sb_pair(_plain_dot, suffix, qs[p], k_ref[rows, cols], v_ref[rows, cols], rs[p], mask)
                new_r.append(r_out)
                new_acc.append(accs[p] + o_part)
                top = jnp.max(r_out) if top is None else jnp.maximum(top, jnp.max(r_out))
            return kb - 1, top, tuple(new_r), tuple(new_acc)

        init = (n, jnp.float32(0.0), tuple(jnp.zeros((2 * BLOCK, 1), F32) for _ in range(SB_PAIRS)),
                tuple(jnp.zeros((2 * BLOCK, BLOCK), F32) for _ in range(SB_PAIRS)))
        accs = lax.while_loop(cond, step, init)[3]
        for p in range(SB_PAIRS):
            o_ref[:, p * BLOCK:(p + 1) * BLOCK] = jnp.where(low, accs[p][:BLOCK], accs[p][BLOCK:]).astype(o_ref.dtype)

    return _pcall(
        body, side=side, name=name, grid=(pairs // SB_PAIRS, nb),
        in_specs=[pl.BlockSpec((BLOCK, wide), lambda g, n: (n, qcb // SB_PAIRS + g)),
                  pl.BlockSpec((s, wide), lambda g, n: (0, kcb // SB_PAIRS + g), pipeline_mode=pl.Buffered(1)),
                  pl.BlockSpec((s, wide), lambda g, n: (0, vcb // SB_PAIRS + g), pipeline_mode=pl.Buffered(1))],
        out_specs=pl.BlockSpec((BLOCK, wide), lambda g, n: (n, g)),
        out_shape=jax.ShapeDtypeStruct((s, pairs * BLOCK), BF16),
        compiler_params=_params(),
    )(qkv, qkv, qkv)


def sb_bwd(qkv, qcb, kcb, vcb, do, docb, name, side=None):
    s = qkv.shape[0]
    nb = s // BLOCK
    pairs = B_HEADS // 2
    wide = SB_PAIRS * BLOCK
    assert docb % SB_PAIRS == 0

    def body(q_ref, k_ref, v_ref, do_ref, dq_ref, dk_ref, dv_ref, r_ref):
        n = pl.program_id(1)

        @pl.when(n == 0)
        def _():
            dk_ref[...] = jnp.zeros(dk_ref.shape, F32)
            dv_ref[...] = jnp.zeros(dv_ref.shape, F32)

        low = lax.broadcasted_iota(jnp.int32, (BLOCK, BLOCK), 1) < HEAD_DIM
        before = (lax.broadcasted_iota(jnp.int32, (2 * BLOCK, BLOCK), 1)
                  < jnp.bitwise_and(lax.broadcasted_iota(jnp.int32, (2 * BLOCK, BLOCK), 0), BLOCK - 1))
        after, earlier = _tri(True), _tri(False)
        suffix = lambda t: _split_dot_vjp(t, after, earlier, 2)
        stack = lambda t: jnp.concatenate([jnp.where(low, t, 0.0), jnp.where(low, 0.0, t)], axis=0)
        qs = [stack(q_ref[:, p * BLOCK:(p + 1) * BLOCK].astype(F32)) for p in range(SB_PAIRS)]
        dos = [stack(do_ref[:, p * BLOCK:(p + 1) * BLOCK].astype(F32)) for p in range(SB_PAIRS)]

        def cond(c):
            return jnp.logical_and(c[0] >= 0, c[1] > SB_SKIP_LOG)

        def down(c):
            kb, _, rs = c
            rows = pl.ds(pl.multiple_of(kb * BLOCK, BLOCK), BLOCK)
            mask = jnp.logical_or(before, kb != n)
            new_r, top = [], None
            for h in range(SB_PAIRS):
                cols = slice(h * BLOCK, (h + 1) * BLOCK)
                r_ref[h, kb] = rs[h]
                z = _dot(qs[h], k_ref[rows, cols], NT) * (HEAD_DIM ** -0.5)
                log_keep = jnp.where(mask, _log_sigmoid(z) - z, 0.0)
                r_out = rs[h] + jnp.sum(log_keep, axis=1, keepdims=True)
                new_r.append(r_out)
                top = jnp.max(r_out) if top is None else jnp.maximum(top, jnp.max(r_out))
            return kb - 1, top, tuple(new_r)

        init = (n, jnp.float32(0.0), tuple(jnp.zeros((2 * BLOCK, 1), F32) for _ in range(SB_PAIRS)))
        last = lax.while_loop(cond, down, init)[0] + 1

        def up(kb, c):
            dqs, g_rs = c
            rows = pl.ds(pl.multiple_of(kb * BLOCK, BLOCK), BLOCK)
            mask = jnp.logical_or(before, kb != n)
            new_dq, new_g = [], []
            for h in range(SB_PAIRS):
                cols = slice(h * BLOCK, (h + 1) * BLOCK)
                _, vjp = jax.vjp(lambda q_, k_, v_, r_: _sb_pair(_dot_vjp, suffix, q_, k_, v_, r_, mask),
                                 qs[h], k_ref[rows, cols].astype(F32), v_ref[rows, cols].astype(F32), r_ref[h, kb])
                dq_c, dk_c, dv_c, g_in = vjp((dos[h], g_rs[h]))
                dk_ref[rows, cols] += dk_c
                dv_ref[rows, cols] += dv_c
                new_dq.append(dqs[h] + dq_c)
                new_g.append(g_in)
            return tuple(new_dq), tuple(new_g)

        init = (tuple(jnp.zeros((2 * BLOCK, BLOCK), F32) for _ in range(SB_PAIRS)),
                tuple(jnp.zeros((2 * BLOCK, 1), F32) for _ in range(SB_PAIRS)))
        dqs = lax.fori_loop(last, n + 1, up, init)[0]
        for p in range(SB_PAIRS):
            dq_ref[:, p * BLOCK:(p + 1) * BLOCK] = jnp.where(low, dqs[p][:BLOCK], dqs[p][BLOCK:])

    full = jax.ShapeDtypeStruct((s, pairs * BLOCK), F32)
    return _pcall(
        body, side=side, name=name, grid=(pairs // SB_PAIRS, nb),
        in_specs=[pl.BlockSpec((BLOCK, wide), lambda g, n: (n, qcb // SB_PAIRS + g)),
                  pl.BlockSpec((s, wide), lambda g, n: (0, kcb // SB_PAIRS + g), pipeline_mode=pl.Buffered(1)),
                  pl.BlockSpec((s, wide), lambda g, n: (0, vcb // SB_PAIRS + g), pipeline_mode=pl.Buffered(1)),
                  pl.BlockSpec((BLOCK, wide), lambda g, n: (n, docb // SB_PAIRS + g))],
        out_specs=[pl.BlockSpec((BLOCK, wide), lambda g, n: (n, g)),
                   pl.BlockSpec((s, wide), lambda g, n: (0, g), pipeline_mode=pl.Buffered(1)),
                   pl.BlockSpec((s, wide), lambda g, n: (0, g), pipeline_mode=pl.Buffered(1))],
        out_shape=[full, full, full],
        scratch_shapes=[pltpu.VMEM((SB_PAIRS, nb, 2 * BLOCK, 1), F32)],
        compiler_params=_params(),
    )(qkv, qkv, qkv, do)


def _xa_tile(dot, q, kv, qg, kg):
    hd = q.shape[1] // X_HEADS
    outs = []
    for h in range(X_HEADS):
        qh = _rms(q[:, h * hd:(h + 1) * hd], qg)
        kh = _rms(kv[:, h * hd:(h + 1) * hd], kg)
        vh = kv[:, (X_HEADS + h) * hd:(X_HEADS + h + 1) * hd]
        sc = dot(qh, kh, True) * (hd ** -0.5)
        m = lax.stop_gradient(jnp.max(sc, axis=-1, keepdims=True))
        p = jnp.exp(sc - m)
        outs.append(dot(p * (1.0 / jnp.sum(p, axis=-1, keepdims=True)), vh, False))
    return jnp.concatenate(outs, axis=1)


def xa_core_fwd(q, kv, qg, kg, name):
    n, d = q.shape
    tm = _tile(n, 256, 8)
    (o,) = tcall(lambda ids, qt, kvt, qgt, kgt: (_xa_tile(_plain_dot, qt, kvt, qgt, kgt),), (n // tm,),
                 [_row(q, tm), _full(kv), _full(qg), _full(kg)], [_row_out(n, d, BF16, tm)], name)
    return o


def xa_core_bwd(q, kv, qg, kg, do, name):
    n, d = q.shape
    tm = _tile(n, 256, 8)

    def fn(ids, qt, kvt, qgt, kgt, dot_):
        _, vjp = jax.vjp(functools.partial(_xa_tile, _dot_vjp), qt, kvt, qgt, kgt)
        return vjp(dot_.astype(F32))

    return tcall(fn, (n // tm,), [_row(q, tm), _full(kv), _full(qg), _full(kg), _row(do, tm)],
                 [_row_out(n, d, BF16, tm), _acc_out(kv.shape), _acc_out(qg.shape), _acc_out(kg.shape)], name)


def _ev_reorder(a):
    return jnp.concatenate([a[0:512], a[768:2304], a[512:768]], axis=0)


def _ev_restore(a):
    return jnp.concatenate([a[0:512], a[2048:2304], a[512:2048]], axis=0)


_EV_SEGS = ((0, 512, "q"), (512, 1536, "raw"), (2048, 128, "k"), (2176, 128, "raw"))
_A_CFG = dict(hkv=A_KV_HEADS, grp=A_Q_HEADS // A_KV_HEADS, max_dist=BLOCK - 1, step=1.0, slopes=_alibi(A_Q_HEADS),
              want_lse=False)
_A_COLS = (lambda r: 0, lambda r: 16, lambda r: 17)


def even_mixer_fwd(x, h, w_in, qg, kg, sinks, w_out, tag, side=None):
    qkv = mm(h, w_in, "nt", tag + "_in")
    (ops,) = prep_fwd(qkv, qg, kg, _EV_SEGS, (1,), tag + "_prep")
    (o_a,) = banded_fwd(ops, 1, _A_COLS, sinks, _A_CFG, tag + "_swa")
    o_b = sb_fwd(ops, 4, 8, 12, tag + "_sb", side=side)
    carried = None
    if side is not None:
        o_b, carried = o_b
    o = jnp.concatenate([o_a, o_b], axis=1)
    y = mm(o, w_out, "nn", tag + "_out", res=x)
    return y, (x, h, qkv, ops, o), carried


def even_mixer_bwd(dy, saved, g, w_in, qg, kg, sinks, w_out, tag, side=None, last_side=None):
    x, h, qkv, ops, o = saved
    do = mm(dy, w_out, "nt", tag + "_do", out_dtype=BF16)
    d_wout = mm(o, dy, "tn", tag + "_dwout")
    dqa, dkp, dkc, dvp, dvc, dsinks = banded_bwd(ops, 1, _A_COLS, sinks, _A_CFG, [(do, lambda r: 0)], tag + "_dswa")
    res = sb_bwd(ops, 4, 8, 12, do, 4, tag + "_dsb", side=side)
    carried = None
    if side is not None:
        res, carried = res
    dqb, dkb, dvb = res
    dqkv, dqg, dkg = prep_bwd(
        qkv, qg, kg, _EV_SEGS,
        [(dqa, 0, 1, 0), (dqb, 0, 1, 1), (dkb, 0, 1, 2), (dvb, 0, 1, 3), (dkc, 0, 1, 4), (dkp, 1, 1, 4), (dvc, 0, 1, 5),
         (dvp, 1, 1, 5)],
        lambda *t: jnp.concatenate(t, axis=1), tag + "_dqkv")
    d_win = mm(dqkv, h, "tn", tag + "_dwin")
    last = None if last_side is None else last_side(d_win, d_wout)
    if last is None:
        dx, dg = mm_norm_bwd(dqkv, w_in, x, g, dy, tag + "_dh", b_kd=True)
    else:
        (dx, dg), got = mm_norm_bwd(dqkv, w_in, x, g, dy, tag + "_dh", b_kd=True, side=last[0])
        last[1](got)
    return dx, dg, d_win, dqg, dkg, dsinks, d_wout, carried


def _c_cfg(window, dil):
    return dict(hkv=C_HEADS, grp=1, max_dist=window // dil, step=float(dil), slopes=_alibi(C_HEADS), want_lse=True)


_C_COLS = (lambda r: 3 * r, lambda r: 3 * r + 1, lambda r: 3 * r + 2)
_OD_SEGS = ((0, 1024, "q"), (1024, 1024, "k"), (2048, 1024, "raw"))


def _combine(o1, o2, o3, l1, l2, l3):
    m = lax.stop_gradient(jnp.maximum(jnp.maximum(l1, l2), l3))
    e1, e2, e3 = jnp.exp(l1 - m), jnp.exp(l2 - m), jnp.exp(l3 - m)
    tot = e1 + e2 + e3
    return (e1 / tot) * o1 + (e2 / tot) * o2 + (e3 / tot) * o3


def odd_mixer_fwd(x, g, w_in, qg, kg, w_out, tag):
    n, d = x.shape
    h = rmsnorm_fwd(x, g, tag + "_norm")
    qkv = mm(h, w_in, "nt", tag + "_in")
    dils = [dil for _, dil in C_PATTERNS]
    ops = prep_fwd(qkv, qg, kg, _OD_SEGS, dils, tag + "_prep")
    os_, ls_ = [], []
    for (window, dil), ops_d in zip(C_PATTERNS, ops):
        o_p, l_p = banded_fwd(ops_d, dil, _C_COLS, None, _c_cfg(window, dil), f"{tag}_dil{dil}")
        os_.append(o_p)
        ls_.append(l_p)
    tm = BLOCK
    lay = lambda a, dil: _in(a, (tm // dil, a.shape[1]), lambda i: (i, 0))
    views = [lay(a, dil) for a, dil in zip(os_ + ls_, dils + dils)]

    def comb(ids, *t, scratch):
        return (_combine(*[_to_natural(scratch, a, dil) for a, dil in zip(t, dils + dils)]),)

    (o,) = tcall(comb, (n // tm,), views, [_row_out(n, d, BF16, tm)], tag + "_comb",
                 scratch=((d // BLOCK * tm, BLOCK), F32))
    y = mm(o, w_out, "nn", tag + "_out", res=x)
    return y, (x, h, qkv, ops, views, o)


def odd_mixer_bwd(dy, saved, g, w_in, qg, kg, w_out, tag):
    x, h, qkv, ops, views, o = saved
    n, d = x.shape
    do = mm(dy, w_out, "nt", tag + "_do")
    d_wout = mm(o, dy, "tn", tag + "_dwout")
    tm = BLOCK
    dils = [dil for _, dil in C_PATTERNS]

    def comb_bwd(ids, *t, scratch):
        _, vjp = jax.vjp(_combine, *[_to_natural(scratch, a, dil) for a, dil in zip(t[:6], dils + dils)])
        return tuple(_to_strided(scratch, c, dil) for c, dil in zip(vjp(t[6]), dils + dils))

    cts = tcall(comb_bwd, (n // tm,), views + [_row(do, tm)],
                [_out((n // dil, dil * d), F32, (tm // dil, dil * d), lambda i: (i, 0)) for dil in dils + dils],
                tag + "_dcomb", scratch=((d // BLOCK * tm, BLOCK), F32))
    dqs, dks, dvs = [], [], []
    for p, ((window, dil), ops_d) in enumerate(zip(C_PATTERNS, ops)):
        dq, dkp, dkc, dvp, dvc = banded_bwd(ops_d, dil, _C_COLS, None, _c_cfg(window, dil),
                                            [(cts[p], lambda r: r), (cts[3 + p], lambda r: r)], f"{tag}_ddil{dil}")
        dqs.append((dq, 0, dil, p))
        dks += [(dkc, 0, dil, 3 + p), (dkp, dil, dil, 3 + p)]
        dvs += [(dvc, 0, dil, 6 + p), (dvp, dil, dil, 6 + p)]

    def gather(*t):
        return jnp.concatenate([t[0] + t[1] + t[2], t[3] + t[4] + t[5], t[6] + t[7] + t[8]], axis=1)

    dqkv, dqg, dkg = prep_bwd(qkv, qg, kg, _OD_SEGS, dqs + dks + dvs, gather, tag + "_dqkv")
    d_win = mm(dqkv, h, "tn", tag + "_dwin")
    dx, dg = mm_norm_bwd(dqkv, w_in, x, g, dy, tag + "_dh", b_kd=True)
    return dx, dg, d_win, dqg, dkg, d_wout


def xa_fwd(x, mem, g, gm, w_q, w_kv, qg, kg, w_o, tag):
    h = rmsnorm_fwd(x, g, tag + "_norm")
    q = mm(h, w_q, "nn", tag + "_q")
    mn = rmsnorm_fwd(mem, gm, tag + "_mnorm")
    kv = mm(mn, w_kv, "nt", tag + "_kv")
    o = xa_core_fwd(q, kv, qg, kg, tag + "_core")
    y = mm(o, w_o, "nn", tag + "_o", res=x)
    return y, (x, h, q, mn, kv, o)


def xa_bwd(dy, saved, mem, g, gm, w_q, w_kv, qg, kg, w_o, tag):
    x, h, q, mn, kv, o = saved
    do = mm(dy, w_o, "nt", tag + "_do", out_dtype=BF16)
    d_wo = mm(o, dy, "tn", tag + "_dwo")
    dq, dkv, dqg, dkg = xa_core_bwd(q, kv, qg, kg, do, tag + "_dcore")
    d_wq = mm(h, dq, "tn", tag + "_dwq")
    dx, dg = mm_norm_bwd(dq, w_q, x, g, dy, tag + "_dh")
    d_wkv = mm(dkv, mn, "tn", tag + "_dwkv")
    _, dgm = mm_norm_bwd(dkv, w_kv, mem, gm, None, tag + "_dmn", b_kd=True)
    return dx, dg, dgm, d_wq, d_wkv, dqg, dkg, d_wo


def loss_head(y, target, name):
    n, d = y.shape
    tm = _tile(n, 512, 8)

    def fn(ids, yt, tt):
        e = yt - tt
        return e * (1.0 / d), jnp.sum(e * e, axis=0, keepdims=True)

    return tcall(fn, (n // tm,), [_row(y, tm), _row(target, tm)], [_row_out(n, d, F32, tm), _acc_out((1, d))], name)


_ANY = pl.BlockSpec(memory_space=pl.ANY)


def all_gather_blocks(blocks):
    nb = len(blocks)

    def body(*refs):
        x_refs, out_refs = refs[:nb], refs[nb:2 * nb]
        send_sems, recv_sems, local_sems = refs[2 * nb:]
        x, y, c = lax.axis_index("x"), lax.axis_index("y"), lax.axis_index("c")
        me, sibling = (x, y, c), (x, y, 1 - c)
        over_x, over_y, diagonal = (1 - x, y), (x, 1 - y), (1 - x, 1 - y)
        relay_of = ((1 - x) * (1 - c) + x * c, y * (1 - c) + (1 - y) * c)
        relay_to = (x * (1 - c) + (1 - x) * c, (1 - y) * (1 - c) + y * c)

        def copy(b, k, blk, to, own=False):
            px, py, pc = blk
            slot = out_refs[b].at[4 * px + 2 * py + pc]
            return pltpu.make_async_remote_copy(
                src_ref=x_refs[b] if own else slot, dst_ref=slot,
                send_sem=send_sems.at[7 * b + k], recv_sem=recv_sems.at[7 * b + k], device_id=to, device_id_type=MESH)

        mine = [pltpu.make_async_copy(x_refs[b], out_refs[b].at[4 * x + 2 * y + c], local_sems.at[b]) for b in range(nb)]
        for cp in mine:
            cp.start()
        sent = []
        for b in range(nb):
            sent += [copy(b, 0, me, sibling, own=True), copy(b, 1, me, (*over_x, c), own=True),
                     copy(b, 2, me, (*over_y, c), own=True)]
        for cp in sent:
            cp.start()
        for b in range(nb):
            copy(b, 1, (*over_x, c), me).wait_recv()
            copy(b, 2, (*over_y, c), me).wait_recv()
            later = [copy(b, 3, (*relay_of, c), (*relay_to, c)), copy(b, 4, (*over_x, c), sibling),
                     copy(b, 5, (*over_y, c), sibling)]
            for cp in later:
                cp.start()
            sent += later
        for b in range(nb):
            copy(b, 3, (*diagonal, c), me).wait_recv()
            fwd = copy(b, 6, (*diagonal, c), sibling)
            fwd.start()
            sent.append(fwd)
        for b in range(nb):
            copy(b, 0, sibling, me).wait_recv()
            for k, chip in ((4, over_x), (5, over_y), (6, diagonal)):
                copy(b, k, (*chip, 1 - c), me).wait_recv()
        for cp in sent:
            cp.wait_send()
        for cp in mine:
            cp.wait()

    return _pcall(
        body, name="weights_all_gather",
        in_specs=[_ANY] * nb, out_specs=[_ANY] * nb,
        out_shape=[jax.ShapeDtypeStruct((N_DEV,) + a.shape, a.dtype) for a in blocks],
        scratch_shapes=[pltpu.SemaphoreType.DMA((7 * nb,)), pltpu.SemaphoreType.DMA((7 * nb,)),
                        pltpu.SemaphoreType.DMA((nb,))],
    )(*blocks)


def pair_exchange(bufs):
    nb = len(bufs)

    def body(*refs):
        srcs, dsts = refs[:nb], refs[nb:2 * nb]
        send_sems, recv_sems = refs[2 * nb:]
        x, y, c = lax.axis_index("x"), lax.axis_index("y"), lax.axis_index("c")
        copies = []
        for b in range(nb):
            for j in range(4):
                cp = pltpu.make_async_remote_copy(
                    src_ref=srcs[b].at[2 * j + (1 - c)], dst_ref=dsts[b].at[j], send_sem=send_sems.at[4 * b + j],
                    recv_sem=recv_sems.at[4 * b + j], device_id=(x, y, 1 - c), device_id_type=MESH)
                cp.start()
                copies.append(cp)
        for cp in copies:
            cp.wait()

    return _pcall(
        body, name="grads_pair_exchange",
        in_specs=[_ANY] * nb, out_specs=[_ANY] * nb,
        out_shape=[jax.ShapeDtypeStruct((4,) + a.shape[1:], a.dtype) for a in bufs],
        scratch_shapes=[pltpu.SemaphoreType.DMA((4 * nb,)), pltpu.SemaphoreType.DMA((4 * nb,))],
    )(*bufs)


def pair_sum(g, got, c, out_dtype, name):
    r, w = g.shape[1:]
    tr = _tile(r, 512, 16)

    def body(c_ref, a_ref, b_ref, o_ref):
        o_ref[...] = (a_ref[...].astype(F32) + b_ref[...].astype(F32)).astype(o_ref.dtype)

    return _pcall(
        body, name=name,
        grid_spec=pltpu.PrefetchScalarGridSpec(
            num_scalar_prefetch=1, grid=(4, r // tr),
            in_specs=[pl.BlockSpec((None, tr, w), lambda j, i, c_ref: (2 * j + c_ref[0], i, 0)),
                      pl.BlockSpec((None, tr, w), lambda j, i, c_ref: (j, i, 0))],
            out_specs=pl.BlockSpec((None, tr, w), lambda j, i, c_ref: (j, i, 0))),
        out_shape=jax.ShapeDtypeStruct((4,) + g.shape[1:], out_dtype),
        compiler_params=_params(),
    )(c, g, got)


def chip_exchange(parts):
    nb = len(parts)

    def body(*refs):
        srcs, dsts = refs[:nb], refs[nb:2 * nb]
        send_sems, recv_sems, local_sems = refs[2 * nb:]
        x, y, c = lax.axis_index("x"), lax.axis_index("y"), lax.axis_index("c")
        my_chip = 2 * x + y
        copies = []
        for b in range(nb):
            mine = pltpu.make_async_copy(srcs[b].at[my_chip], dsts[b].at[my_chip], local_sems.at[b])
            mine.start()
            copies.append(mine)
            for k, (tx, ty) in enumerate([(1 - x, y), (x, 1 - y), (1 - x, 1 - y)]):
                cp = pltpu.make_async_remote_copy(
                    src_ref=srcs[b].at[2 * tx + ty], dst_ref=dsts[b].at[my_chip], send_sem=send_sems.at[3 * b + k],
                    recv_sem=recv_sems.at[3 * b + k], device_id=(tx, ty, c), device_id_type=MESH)
                cp.start()
                copies.append(cp)
        for cp in copies:
            cp.wait()

    return _pcall(
        body, name="grads_chip_exchange",
        in_specs=[_ANY] * nb, out_specs=[_ANY] * nb,
        out_shape=[jax.ShapeDtypeStruct(a.shape, a.dtype) for a in parts],
        scratch_shapes=[pltpu.SemaphoreType.DMA((3 * nb,)), pltpu.SemaphoreType.DMA((3 * nb,)),
                        pltpu.SemaphoreType.DMA((nb,))],
    )(*parts)


def chip_sum(parts, name):
    r, w = parts.shape[1:]
    tr = _tile(r, 512, 16)
    spec = lambda j: _in(parts, (None, tr, w), lambda i, j=j: (j, i, 0))

    def fn(ids, a, b, c_, d):
        a, b, c_, d = [t.astype(F32) for t in (a, b, c_, d)]
        return (((a + b) + c_) + d,)

    (out,) = tcall(fn, (r // tr,), [spec(j) for j in range(4)],
                   [_out((r, w), F32, (tr, w), lambda i: (i, 0))], name)
    return out


def _remote(src, dst, send_sems, recv_sems, k, to):
    return functools.partial(pltpu.make_async_remote_copy, src_ref=src, dst_ref=dst, send_sem=send_sems.at[k],
                             recv_sem=recv_sems.at[k], device_id=to, device_id_type=MESH)


def _gather_plan(phase, nb):
    def plan(ins, outs, send_sems, recv_sems, local_sems):
        x, y, c = lax.axis_index("x"), lax.axis_index("y"), lax.axis_index("c")
        me, sibling = (x, y, c), (x, y, 1 - c)
        over_x, over_y, diagonal = (1 - x, y), (x, 1 - y), (1 - x, 1 - y)
        relay_of = ((1 - x) * (1 - c) + x * c, y * (1 - c) + (1 - y) * c)
        relay_to = (x * (1 - c) + (1 - x) * c, (1 - y) * (1 - c) + y * c)
        local, sends, recvs = [], [], []
        for b in range(nb):
            slot = lambda chip, core, b=b: outs[b].at[4 * chip[0] + 2 * chip[1] + core]
            if phase == 0:
                local.append(functools.partial(pltpu.make_async_copy, ins[b], slot((x, y), c), local_sems.at[b]))
                moves = [(ins[b], slot((x, y), c), to) for to in (sibling, (*over_x, c), (*over_y, c))]
                arrive = [slot((x, y), 1 - c), slot(over_x, c), slot(over_y, c)]
            elif phase == 1:
                moves = [(slot(relay_of, c), slot(relay_of, c), (*relay_to, c)),
                         (slot(over_x, c), slot(over_x, c), sibling), (slot(over_y, c), slot(over_y, c), sibling)]
                arrive = [slot(diagonal, c), slot(over_x, 1 - c), slot(over_y, 1 - c)]
            else:
                moves = [(slot(diagonal, c), slot(diagonal, c), sibling)]
                arrive = [slot(diagonal, 1 - c)]
            sends += [_remote(src, dst, send_sems, recv_sems, 3 * b + k, to) for k, (src, dst, to) in enumerate(moves)]
            recvs += [_remote(dst, dst, send_sems, recv_sems, 3 * b + k, me) for k, dst in enumerate(arrive)]
        return local, sends, recvs
    return plan


def gather_side(phase, arrays):
    nb = len(arrays)
    if phase == 0:
        shapes = [jax.ShapeDtypeStruct((N_DEV,) + a.shape, a.dtype) for a in arrays]
        return Side(arrays, shapes, 3 * nb, nb, _gather_plan(0, nb))
    shapes = [jax.ShapeDtypeStruct(a.shape, a.dtype) for a in arrays]
    return Side(arrays, shapes, 3 * nb, 0, _gather_plan(phase, nb), aliased=True)


def pair_side(bufs):
    nb = len(bufs)

    def plan(ins, outs, send_sems, recv_sems, local_sems):
        x, y, c = lax.axis_index("x"), lax.axis_index("y"), lax.axis_index("c")
        sends = [_remote(ins[b].at[2 * j + (1 - c)], outs[b].at[j], send_sems, recv_sems, 4 * b + j, (x, y, 1 - c))
                 for b in range(nb) for j in range(4)]
        recvs = [_remote(outs[b].at[j], outs[b].at[j], send_sems, recv_sems, 4 * b + j, (x, y, c))
                 for b in range(nb) for j in range(4)]
        return [], sends, recvs

    shapes = [jax.ShapeDtypeStruct((4,) + a.shape[1:], a.dtype) for a in bufs]
    return Side(bufs, shapes, 4 * nb, 0, plan)


def chip_side(parts):
    nb = len(parts)

    def plan(ins, outs, send_sems, recv_sems, local_sems):
        x, y, c = lax.axis_index("x"), lax.axis_index("y"), lax.axis_index("c")
        my_chip = 2 * x + y
        peers = [(1 - x, y), (x, 1 - y), (1 - x, 1 - y)]
        local = [functools.partial(pltpu.make_async_copy, ins[b].at[my_chip], outs[b].at[my_chip], local_sems.at[b])
                 for b in range(nb)]
        sends = [_remote(ins[b].at[2 * tx + ty], outs[b].at[my_chip], send_sems, recv_sems, 3 * b + k, (tx, ty, c))
                 for b in range(nb) for k, (tx, ty) in enumerate(peers)]
        recvs = [_remote(outs[b].at[2 * tx + ty], outs[b].at[2 * tx + ty], send_sems, recv_sems, 3 * b + k, (x, y, c))
                 for b in range(nb) for k, (tx, ty) in enumerate(peers)]
        return local, sends, recvs

    shapes = [jax.ShapeDtypeStruct(a.shape, a.dtype) for a in parts]
    return Side(parts, shapes, 3 * nb, nb, plan)


def adamw(w, g, m, v, name):
    shape = w.shape
    cols = shape[-1]
    rows = int(np.prod(shape[:-1]))
    w2, g2, m2, v2 = [a.reshape(rows, cols) for a in (w, g, m, v)]
    tr = _tile(rows, 256, 8) if rows % 8 == 0 else rows

    def fn(ids, wt, gt, mt, vt):
        m_new = ADAM_B1 * mt + (1.0 - ADAM_B1) * gt
        v_new = ADAM_B2 * vt + (1.0 - ADAM_B2) * (gt * gt)
        m_hat = m_new / (1.0 - ADAM_B1 ** ADAM_STEP)
        v_hat = v_new / (1.0 - ADAM_B2 ** ADAM_STEP)
        delta = -ADAM_LR * (m_hat / (jnp.sqrt(v_hat) + ADAM_EPS) + ADAM_WD * wt)
        return delta, m_new, v_new

    res = tcall(fn, (rows // tr,), [_row(a, tr) for a in (w2, g2, m2, v2)],
                [_row_out(rows, cols, F32, tr) for _ in range(3)], name)
    return [a.reshape(shape) for a in res]


_MATS = [("ffn1_w_gu", "col"), ("ffn1_w_down", "row"), ("ev_w_in", "col"), ("ev_w_out", "row"),
         ("od_w_in", "col"), ("od_w_out", "row"), ("xa_w_q", "row"), ("xa_w_kv", "col"), ("xa_w_o", "row"),
         ("ffn2_w_gu", "col"), ("ffn2_w_down", "row")]
_VECS = ["ffn1_norm", "mix_norm", "ev_q_gain", "ev_k_gain", "ev_sinks", "od_q_gain", "od_k_gain", "xa_norm",
         "xa_mem_norm", "xa_q_gain", "xa_k_gain", "ffn2_norm"]
_WEIGHTS = ["ffn1_norm", "ffn1_w_gu", "ffn1_w_down", "mix_norm", "ev_w_in", "ev_q_gain", "ev_k_gain", "ev_sinks",
            "ev_w_out", "od_w_in", "od_q_gain", "od_k_gain", "od_w_out", "xa_norm", "xa_mem_norm", "xa_w_q", "xa_w_kv",
            "xa_q_gain", "xa_k_gain", "xa_w_o", "ffn2_norm", "ffn2_w_gu", "ffn2_w_down"]


_AXIS = dict(_MATS)
DEPTH = 2


def _layer_groups(l):
    first, rest = _first_block_groups(l)
    return [first[0] + rest[0] + rest[1]]


def _first_block_groups(l):
    w_in, w_out = ("ev_w_in", "ev_w_out") if l % 2 == 0 else ("od_w_in", "od_w_out")
    first = [[("ffn1_w_gu", l), ("ffn1_w_down", l)]]
    rest = [[("ffn2_w_gu", l), ("xa_w_kv", l)],
            [(w_in, l // 2), ("ffn2_w_down", l), (w_out, l // 2), ("xa_w_q", l), ("xa_w_o", l)]]
    return first, rest


def _block_rows(shards, n):
    a, b = shards[n].shape[1:]
    return a if _AXIS[n] == "row" else b


def _weight_blocks(shards, groups):
    blocks = []
    for group in groups:
        rows = [(shards[n][j] if _AXIS[n] == "row" else shards[n][j].T).astype(BF16) for n, j in group]
        blocks.append(rows[0] if len(rows) == 1 else jnp.concatenate(rows, axis=0))
    return blocks


def _whole_weights(shards, groups, gathered):
    full = {}
    for group, got in zip(groups, gathered):
        off = 0
        for n, j in group:
            r = _block_rows(shards, n)
            full[n] = got[:, off:off + r, :].reshape(N_DEV * r, got.shape[2])
            off += r
    return full


def _gradient_buffers(grads, groups):
    bufs = []
    for group in groups:
        rows = []
        for n, _ in group:
            whole = jnp.concatenate(grads[n], axis=0) if isinstance(grads[n], tuple) else grads[n]
            rows.append(whole.reshape(N_DEV, whole.shape[0] // N_DEV, whole.shape[1]))
        bufs.append((rows[0] if len(rows) == 1 else jnp.concatenate(rows, axis=1)).astype(BF16))
    return bufs


def _gradient_blocks(shards, groups, sums):
    out = {}
    for group, tot in zip(groups, sums):
        off = 0
        for n, j in group:
            r = _block_rows(shards, n)
            out[n, j] = tot[off:off + r] if _AXIS[n] == "row" else tot[off:off + r].T
            off += r
    return out


class _PairChain:
    def __init__(self, ex, bufs):
        self.ex, self.bufs, self.parts = ex, bufs, None

    def side(self, name):
        return pair_side(self.bufs) if name == "da" else None

    def done(self, name, carried):
        self.parts = self.ex.pair_sums(self.bufs, carried, "l1")


class _RestChain:
    HALF = {"da": (0,), "dh": (1,)}

    def __init__(self, ex, bufs):
        self.ex, self.bufs, self.parts, self.sums = ex, bufs, None, [None] * len(bufs)

    def side(self, name):
        if name == "pair":
            return pair_side(self.bufs)
        if name in self.HALF:
            return chip_side([self.parts[i] for i in self.HALF[name]])
        return None

    def done(self, name, carried):
        if name == "pair":
            self.parts = self.ex.pair_sums(self.bufs, carried, "l0r")
        else:
            for i, tot in zip(self.HALF[name], self.ex.chip_sums(carried, "l0r_" + name)):
                self.sums[i] = tot


class _Exchange:
    def __init__(self, shards, c):
        self.shards, self.c = shards, c

    def weights_first(self):
        first, _ = _first_block_groups(0)
        return _whole_weights(self.shards, first, all_gather_blocks(_weight_blocks(self.shards, first)))

    def rest_blocks(self):
        return _weight_blocks(self.shards, _first_block_groups(0)[1])

    def weights_rest(self, gathered):
        return _whole_weights(self.shards, _first_block_groups(0)[1], gathered)

    def gather_start(self):
        return gather_side(0, _weight_blocks(self.shards, _layer_groups(1)))

    def weights_next(self, gathered):
        return _whole_weights(self.shards, _layer_groups(1), gathered)

    def chain_next(self, grads):
        return _PairChain(self, _gradient_buffers(grads, _layer_groups(1)))

    def chain_rest(self, grads):
        return _RestChain(self, _gradient_buffers(grads, _first_block_groups(0)[1]))

    def pair_sums(self, bufs, got, tag):
        return [pair_sum(b, g, self.c, b.dtype, f"grads_pair_sum_{tag}_{i}") for i, (b, g) in enumerate(zip(bufs, got))]

    def chip_sums(self, parts, tag):
        return [chip_sum(p, f"grads_chip_sum_{tag}_{i}") for i, p in enumerate(parts)]

    def finish(self, gm, gv, sums1, sums_rest):
        vecs = {n: jnp.concatenate(v, axis=0) for n, v in gv.items()}
        first, rest = _first_block_groups(0)
        bufs = _gradient_buffers(gm[0], first)
        vec = jnp.concatenate([vecs[n].reshape(-1) for n in _VECS])
        vec = jnp.pad(vec, (0, -vec.shape[0] % (16 * LANES)))
        bufs.append(jnp.broadcast_to(vec.reshape(1, -1, LANES), (N_DEV, vec.shape[0] // LANES, LANES)))
        parts = self.pair_sums(bufs, pair_exchange(bufs), "l0")
        sums0 = self.chip_sums(chip_exchange(parts), "l0")
        blocks = {**_gradient_blocks(self.shards, first, sums0[:-1]), **_gradient_blocks(self.shards, rest, sums_rest),
                  **_gradient_blocks(self.shards, _layer_groups(1), sums1)}
        out = {n: jnp.stack([blocks[n, j] for j in range(self.shards[n].shape[0])]) for n, _ in _MATS}
        flat, off = sums0[-1].reshape(-1), 0
        for n in _VECS:
            out[n] = flat[off:off + vecs[n].size].reshape(vecs[n].shape)
            off += vecs[n].size
        return out


class _NoExchange:
    def __init__(self, full):
        self.full = full

    def weights_first(self):
        return self.full[0]

    def rest_blocks(self):
        return None

    def gather_start(self):
        return None

    def weights_next(self, gathered):
        return self.full[1]

    def chain_next(self, grads):
        return None

    def chain_rest(self, grads):
        return None

    def finish(self, gm, gv, sums1, sums_rest):
        mats = {}
        for l in range(DEPTH):
            for group in _layer_groups(l):
                for n, j in group:
                    whole = jnp.concatenate(gm[l][n], axis=0) if isinstance(gm[l][n], tuple) else gm[l][n]
                    mats.setdefault(n, {})[j] = whole if _AXIS[n] == "row" else whole.T
        mats = {n: jnp.stack([v[j] for j in sorted(v)]) for n, v in mats.items()}
        return mats, {n: jnp.concatenate(v, axis=0) for n, v in gv.items()}


def _local_step(x, mem, target, w, ex):
    assert w["ffn1_norm"].shape[0] == DEPTH
    row = lambda a, l: a[l:l + 1]
    full = [ex.weights_first(), None]
    saved = []
    for l in range(DEPTH):
        t, j, f = f"l{l}", l // 2, full[l]
        rest = ex.rest_blocks() if l == 0 else None
        if rest is None:
            x, s1 = ffn_fwd(x, row(w["ffn1_norm"], l), f["ffn1_w_gu"], f["ffn1_w_down"], t + "_ffn1")
        else:
            x, s1, rest = ffn_fwd(x, row(w["ffn1_norm"], l), f["ffn1_w_gu"], f["ffn1_w_down"], t + "_ffn1", (0, rest))
        relay = None
        if l % 2 == 0:
            h = rmsnorm_fwd(x, row(w["mix_norm"], l), t + "_ev_norm", None if rest is None else gather_side(2, rest))
            if rest is not None:
                h, rest = h
                f = full[l] = {**f, **ex.weights_rest(rest)}
            side = ex.gather_start() if l + 1 < DEPTH else None
            x, s2, relay = even_mixer_fwd(x, h, _ev_reorder(f["ev_w_in"]), row(w["ev_q_gain"], j),
                                          row(w["ev_k_gain"], j), row(w["ev_sinks"], j), f["ev_w_out"], t + "_ev", side)
        else:
            x, s2 = odd_mixer_fwd(x, row(w["mix_norm"], l), f["od_w_in"], row(w["od_q_gain"], j),
                                  row(w["od_k_gain"], j), f["od_w_out"], t + "_od")
        x, s3 = xa_fwd(x, mem, row(w["xa_norm"], l), row(w["xa_mem_norm"], l), f["xa_w_q"], f["xa_w_kv"],
                       row(w["xa_q_gain"], l), row(w["xa_k_gain"], l), f["xa_w_o"], t + "_xa")
        if relay is None:
            x, s4 = ffn_fwd(x, row(w["ffn2_norm"], l), f["ffn2_w_gu"], f["ffn2_w_down"], t + "_ffn2")
        else:
            x, s4, relay = ffn_fwd(x, row(w["ffn2_norm"], l), f["ffn2_w_gu"], f["ffn2_w_down"], t + "_ffn2", (1, relay))
        if l + 1 < DEPTH:
            full[l + 1] = ex.weights_next(relay)
        saved.append((s1, s2, s3, s4))
    dx, sq = loss_head(x, target, "loss_head")
    loss = 0.5 * jnp.sum(sq) / x.shape[1]

    gm = [dict() for _ in range(DEPTH)]
    gv = {n: [None] * w[n].shape[0] for n in _VECS}
    chain1 = chain0 = sums1 = None
    started = []
    for l in reversed(range(DEPTH)):
        t, j, f = f"l{l}", l // 2, full[l]
        s1, s2, s3, s4 = saved[l]
        dx, gv["ffn2_norm"][l], gm[l]["ffn2_w_gu"], gm[l]["ffn2_w_down"] = ffn_bwd(
            dx, s4, row(w["ffn2_norm"], l), f["ffn2_w_gu"], f["ffn2_w_down"], t + "_ffn2", chain1 if l == 0 else None)
        parts = chain1.parts if l == 0 and chain1 is not None else None
        (dx, gv["xa_norm"][l], gv["xa_mem_norm"][l], gm[l]["xa_w_q"], gm[l]["xa_w_kv"], gv["xa_q_gain"][l],
         gv["xa_k_gain"][l], gm[l]["xa_w_o"]) = xa_bwd(
            dx, s3, mem, row(w["xa_norm"], l), row(w["xa_mem_norm"], l), f["xa_w_q"], f["xa_w_kv"],
            row(w["xa_q_gain"], l), row(w["xa_k_gain"], l), f["xa_w_o"], t + "_xa")
        if l % 2 == 0:
            def start_rest(d_win, d_wout, l=l):
                gm[l]["ev_w_in"], gm[l]["ev_w_out"] = _ev_restore(d_win), d_wout
                chain = ex.chain_rest(gm[l]) if l == 0 else None
                if chain is None:
                    return None
                started.append(chain)
                return chain.side("pair"), lambda got: chain.done("pair", got)

            (dx, gv["mix_norm"][l], d_win, gv["ev_q_gain"][j], gv["ev_k_gain"][j], gv["ev_sinks"][j],
             gm[l]["ev_w_out"], carried) = even_mixer_bwd(
                dx, s2, row(w["mix_norm"], l), _ev_reorder(f["ev_w_in"]), row(w["ev_q_gain"], j), row(w["ev_k_gain"], j),
                row(w["ev_sinks"], j), f["ev_w_out"], t + "_ev", None if parts is None else chip_side(parts), start_rest)
            gm[l]["ev_w_in"] = _ev_restore(d_win)
            if carried is not None:
                sums1 = ex.chip_sums(carried, "l1")
        else:
            (dx, gv["mix_norm"][l], gm[l]["od_w_in"], gv["od_q_gain"][j], gv["od_k_gain"][j],
             gm[l]["od_w_out"]) = odd_mixer_bwd(
                dx, s2, row(w["mix_norm"], l), f["od_w_in"], row(w["od_q_gain"], j), row(w["od_k_gain"], j),
                f["od_w_out"], t + "_od")
        if l == 0 and started:
            chain0 = started[0]
        dx, gv["ffn1_norm"][l], gm[l]["ffn1_w_gu"], gm[l]["ffn1_w_down"] = ffn_bwd(
            dx, s1, row(w["ffn1_norm"], l), f["ffn1_w_gu"], f["ffn1_w_down"], t + "_ffn1", chain0 if l == 0 else None)
        if l == 1:
            chain1 = ex.chain_next(gm[l])
    return loss, dx, ex.finish(gm, gv, sums1, None if chain0 is None else chain0.sums)


def kernel(x, mem, ffn1_norm, ffn1_w_gu, ffn1_w_down, mix_norm, ev_w_in, ev_q_gain, ev_k_gain, ev_sinks, ev_w_out, od_w_in, od_q_gain, od_k_gain, od_w_out, xa_norm, xa_mem_norm, xa_w_q, xa_w_kv, xa_q_gain, xa_k_gain, xa_w_o, ffn2_norm, ffn2_w_gu, ffn2_w_down, loss_target, m_ffn1_norm, m_ffn1_w_gu, m_ffn1_w_down, m_mix_norm, m_ev_w_in, m_ev_q_gain, m_ev_k_gain, m_ev_sinks, m_ev_w_out, m_od_w_in, m_od_q_gain, m_od_k_gain, m_od_w_out, m_xa_norm, m_xa_mem_norm, m_xa_w_q, m_xa_w_kv, m_xa_q_gain, m_xa_k_gain, m_xa_w_o, m_ffn2_norm, m_ffn2_w_gu, m_ffn2_w_down, v_ffn1_norm, v_ffn1_w_gu, v_ffn1_w_down, v_mix_norm, v_ev_w_in, v_ev_q_gain, v_ev_k_gain, v_ev_sinks, v_ev_w_out, v_od_w_in, v_od_q_gain, v_od_k_gain, v_od_w_out, v_xa_norm, v_xa_mem_norm, v_xa_w_q, v_xa_w_kv, v_xa_q_gain, v_xa_k_gain, v_xa_w_o, v_ffn2_norm, v_ffn2_w_gu, v_ffn2_w_down):
    w = dict(ffn1_norm=ffn1_norm, ffn1_w_gu=ffn1_w_gu, ffn1_w_down=ffn1_w_down, mix_norm=mix_norm, ev_w_in=ev_w_in, ev_q_gain=ev_q_gain, ev_k_gain=ev_k_gain, ev_sinks=ev_sinks, ev_w_out=ev_w_out, od_w_in=od_w_in, od_q_gain=od_q_gain, od_k_gain=od_k_gain, od_w_out=od_w_out, xa_norm=xa_norm, xa_mem_norm=xa_mem_norm, xa_w_q=xa_w_q, xa_w_kv=xa_w_kv, xa_q_gain=xa_q_gain, xa_k_gain=xa_k_gain, xa_w_o=xa_w_o, ffn2_norm=ffn2_norm, ffn2_w_gu=ffn2_w_gu, ffn2_w_down=ffn2_w_down)
    m = dict(ffn1_norm=m_ffn1_norm, ffn1_w_gu=m_ffn1_w_gu, ffn1_w_down=m_ffn1_w_down, mix_norm=m_mix_norm, ev_w_in=m_ev_w_in, ev_q_gain=m_ev_q_gain, ev_k_gain=m_ev_k_gain, ev_sinks=m_ev_sinks, ev_w_out=m_ev_w_out, od_w_in=m_od_w_in, od_q_gain=m_od_q_gain, od_k_gain=m_od_k_gain, od_w_out=m_od_w_out, xa_norm=m_xa_norm, xa_mem_norm=m_xa_mem_norm, xa_w_q=m_xa_w_q, xa_w_kv=m_xa_w_kv, xa_q_gain=m_xa_q_gain, xa_k_gain=m_xa_k_gain, xa_w_o=m_xa_w_o, ffn2_norm=m_ffn2_norm, ffn2_w_gu=m_ffn2_w_gu, ffn2_w_down=m_ffn2_w_down)
    v = dict(ffn1_norm=v_ffn1_norm, ffn1_w_gu=v_ffn1_w_gu, ffn1_w_down=v_ffn1_w_down, mix_norm=v_mix_norm, ev_w_in=v_ev_w_in, ev_q_gain=v_ev_q_gain, ev_k_gain=v_ev_k_gain, ev_sinks=v_ev_sinks, ev_w_out=v_ev_w_out, od_w_in=v_od_w_in, od_q_gain=v_od_q_gain, od_k_gain=v_od_k_gain, od_w_out=v_od_w_out, xa_norm=v_xa_norm, xa_mem_norm=v_xa_mem_norm, xa_w_q=v_xa_w_q, xa_w_kv=v_xa_w_kv, xa_q_gain=v_xa_q_gain, xa_k_gain=v_xa_k_gain, xa_w_o=v_xa_w_o, ffn2_norm=v_ffn2_norm, ffn2_w_gu=v_ffn2_w_gu, ffn2_w_down=v_ffn2_w_down)

    c = lax.axis_index("c").astype(jnp.int32).reshape(1)
    loss, dx, grads = _local_step(x[0], mem[0], loss_target[0], w, _Exchange(w, c))
    loss = lax.psum(loss, ("x", "y", "c"))

    delta, new_m, new_v = {}, {}, {}
    for n in _WEIGHTS:
        delta[n], new_m[n], new_v[n] = adamw(w[n], grads[n], m[n], v[n], "adamw_" + n)
    return (loss, dx[None], *[grads[n] for n in _WEIGHTS], *[delta[n] for n in _WEIGHTS],
            *[new_m[n] for n in _WEIGHTS], *[new_v[n] for n in _WEIGHTS])
```

```python
import functools

import numpy as np
import jax
import jax.numpy as jnp
from jax import lax
from jax.experimental import pallas as pl
from jax.experimental.pallas import tpu as pltpu

F32 = jnp.float32
BF16 = jnp.bfloat16
MESH = pl.DeviceIdType.MESH

HEAD_DIM = 64
BLOCK = 128
RMS_EPS = 1e-6
A_Q_HEADS, A_KV_HEADS = 8, 2
B_HEADS = 8
C_HEADS = 16
C_PATTERNS = ((128, 1), (512, 4), (2048, 16))
X_HEADS = 4
N_DEV = 8
LANES = 1024
VMEM_LIMIT_BYTES = 56 * 1024 * 1024
SB_SKIP_LOG = -110.0
NEG_BIG = -1e30

ADAM_LR, ADAM_B1, ADAM_B2, ADAM_EPS, ADAM_WD, ADAM_STEP = 0.001, 0.9, 0.999, 1e-08, 0.01, 10

NN = (((1,), (0,)), ((), ()))
NT = (((1,), (1,)), ((), ()))
TN = (((0,), (0,)), ((), ()))


class Side:
    def __init__(self, arrays, out_shapes, n_remote, n_local, plan, aliased=False):
        self.arrays, self.out_shapes, self.plan, self.aliased = list(arrays), list(out_shapes), plan, aliased
        self.sems = [pltpu.SemaphoreType.DMA((n_remote,)), pltpu.SemaphoreType.DMA((n_remote,)),
                     pltpu.SemaphoreType.DMA((max(n_local, 1),))]

    def start(self, ins, outs, sems):
        local, sends, _ = self.plan(ins, outs, *sems)
        for make in local + sends:
            make().start()

    def wait(self, ins, outs, sems):
        local, sends, recvs = self.plan(ins, outs, *sems)
        for make in sends:
            make().wait_send()
        for make in recvs:
            make().wait_recv()
        for make in local:
            make().wait()


def _pcall(body, side=None, **kw):
    if side is None:
        return pl.pallas_call(body, **kw)
    grid = kw["grid"]
    single = not isinstance(kw["out_specs"], (list, tuple))
    out_specs = [kw["out_specs"]] if single else list(kw["out_specs"])
    out_shape = [kw["out_shape"]] if single else list(kw["out_shape"])
    scratch = list(kw.get("scratch_shapes", []))
    n_in, n_out, n_scr, n_side = len(kw["in_specs"]), len(out_specs), len(scratch), len(side.arrays)
    n_sout = len(side.out_shapes)

    def hosted(*refs):
        ins, s_in = refs[:n_in], refs[n_in:n_in + n_side]
        outs = refs[n_in + n_side:n_in + n_side + n_out]
        s_out = refs[n_in + n_side + n_out:n_in + n_side + n_out + n_sout]
        rest = refs[n_in + n_side + n_out + n_sout:]
        scr, sems = rest[:n_scr], rest[n_scr:]
        first = last = None
        for a, size in enumerate(grid):
            f, l = pl.program_id(a) == 0, pl.program_id(a) == size - 1
            first = f if first is None else jnp.logical_and(first, f)
            last = l if last is None else jnp.logical_and(last, l)

        @pl.when(first)
        def _():
            side.start(s_in, s_out, sems)

        body(*ins, *outs, *scr)

        @pl.when(last)
        def _():
            side.wait(s_in, s_out, sems)

    any_space = pl.BlockSpec(memory_space=pl.ANY)
    kw2 = dict(kw)
    kw2.update(in_specs=list(kw["in_specs"]) + [any_space] * n_side, out_specs=out_specs + [any_space] * n_sout,
               out_shape=out_shape + side.out_shapes, scratch_shapes=scratch + side.sems)
    if side.aliased:
        kw2["input_output_aliases"] = {n_in + i: n_out + i for i in range(n_side)}
    call = pl.pallas_call(hosted, **kw2)

    def run(*args):
        res = call(*args, *side.arrays)
        return (res[0] if single else list(res[:n_out])), list(res[n_out:])

    return run


def _params(**kw):
    return pltpu.CompilerParams(vmem_limit_bytes=VMEM_LIMIT_BYTES, **kw)


def _tile(dim, cap, unit=128):
    if dim <= cap:
        return dim
    t = (cap // unit) * unit
    while t >= unit:
        if dim % t == 0:
            return t
        t -= unit
    raise ValueError(f"no tile for {dim} under {cap}")


def _dot(a, b, dims):
    return lax.dot_general(a.astype(BF16), b.astype(BF16), dims, preferred_element_type=F32)


@functools.partial(jax.custom_vjp, nondiff_argnums=(2,))
def _dot_vjp(a, b, nt):
    return _dot(a, b, NT if nt else NN)


def _dot_vjp_fwd(a, b, nt):
    return _dot(a, b, NT if nt else NN), (a.astype(BF16), b.astype(BF16))


def _dot_vjp_bwd(nt, res, g):
    a, b = res
    if nt:
        return _dot(g, b, NN), _dot(g, a, TN)
    return _dot(g, b, NT), _dot(a, g, TN)


_dot_vjp.defvjp(_dot_vjp_fwd, _dot_vjp_bwd)


def _plain_dot(a, b, nt):
    return _dot(a, b, NT if nt else NN)


def _split_dot(x, mat, terms=2):
    out, rem = None, x
    for t in range(terms):
        part = rem.astype(BF16)
        d = lax.dot_general(part, mat, NN, preferred_element_type=F32)
        out = d if out is None else out + d
        if t + 1 < terms:
            rem = rem - part.astype(F32)
    return out


@functools.partial(jax.custom_vjp, nondiff_argnums=(3,))
def _split_dot_vjp(x, mat, mat_t, terms):
    return _split_dot(x, mat, terms)


def _split_dot_vjp_fwd(x, mat, mat_t, terms):
    return _split_dot(x, mat, terms), mat_t


def _split_dot_vjp_bwd(terms, mat_t, g):
    return _split_dot(g, mat_t, terms), None, None


_split_dot_vjp.defvjp(_split_dot_vjp_fwd, _split_dot_vjp_bwd)


def _plain_split(x, mat, mat_t, terms):
    return _split_dot(x, mat, terms)


def _tri(after):
    j = lax.broadcasted_iota(jnp.int32, (BLOCK, BLOCK), 0)
    s = lax.broadcasted_iota(jnp.int32, (BLOCK, BLOCK), 1)
    return jnp.where(j > s if after else j < s, 1.0, 0.0).astype(BF16)


def _in(a, block, imap):
    return (a, block, imap)


def _out(shape, dtype, block, imap, acc=False):
    return (shape, dtype, block, imap, acc)


def tcall(fn, grid, ins, outs, name, scratch=None, side=None):
    nin = len(ins)
    nout = len(outs)
    ngrid = len(grid)

    def body(*refs):
        ids = tuple(pl.program_id(a) for a in range(ngrid))
        extra = {} if scratch is None else {"scratch": refs[nin + nout]}
        res = fn(ids, *[r[...] for r in refs[:nin]], **extra)
        first = ids[0] == 0
        for a in range(1, ngrid):
            first = jnp.logical_and(first, ids[a] == 0)
        for o_ref, r, spec in zip(refs[nin:nin + nout], res, outs):
            if spec[4]:
                @pl.when(first)
                def _(o_ref=o_ref):
                    o_ref[...] = jnp.zeros(o_ref.shape, o_ref.dtype)
                o_ref[...] += r.astype(o_ref.dtype)
            else:
                o_ref[...] = r.astype(o_ref.dtype)

    return _pcall(
        body, side=side, name=name, grid=grid,
        in_specs=[pl.BlockSpec(b, m) for (_, b, m) in ins],
        out_specs=[pl.BlockSpec(b, m) for (_, _, b, m, _) in outs],
        out_shape=[jax.ShapeDtypeStruct(s, d) for (s, d, _, _, _) in outs],
        scratch_shapes=[] if scratch is None else [pltpu.VMEM(*scratch)],
        compiler_params=_params(),
    )(*[a for (a, _, _) in ins])


def _to_strided(scr, nat, d):
    if d == 1:
        return nat
    t, w = nat.shape
    nc = w // BLOCK
    for c in range(nc):
        scr[c * t:(c + 1) * t, :] = nat[:, c * BLOCK:(c + 1) * BLOCK]
    return jnp.concatenate([scr[pl.ds(c * t + r, t // d, stride=d), :] for r in range(d) for c in range(nc)], axis=1)


def _to_natural(scr, st, d):
    if d == 1:
        return st.astype(F32)
    t, w = st.shape[0] * d, st.shape[1] // d
    nc = w // BLOCK
    st = st.astype(F32)
    for r in range(d):
        for c in range(nc):
            scr[pl.ds(c * t + r, t // d, stride=d), :] = st[:, r * w + c * BLOCK:r * w + (c + 1) * BLOCK]
    return jnp.concatenate([scr[c * t:(c + 1) * t, :] for c in range(nc)], axis=1)


def _row(a, tm, width=None, cb=0):
    width = a.shape[1] if width is None else width
    return _in(a, (tm, width), lambda i, cb=cb: (i, cb))


def _full(a):
    zeros = (0,) * a.ndim
    return _in(a, a.shape, lambda *ids: zeros)


def _row_out(n, width, dtype, tm):
    return _out((n, width), dtype, (tm, width), lambda i: (i, 0))


def _acc_out(shape):
    zeros = (0,) * len(shape)
    return _out(shape, F32, shape, lambda *ids: zeros, acc=True)


def mm(a, b, mode, name, *, out_dtype=None, scale=1.0, res=None, side=None):
    if out_dtype is None:
        out_dtype = BF16 if mode == "tn" else F32
    if mode == "nn":
        (m, k), (k2, n) = a.shape, b.shape
    elif mode == "nt":
        (m, k), (n, k2) = a.shape, b.shape
    else:
        (k, m), (k2, n) = a.shape, b.shape
    assert k == k2, (a.shape, b.shape, mode)
    tm, tn, tk = _tile(m, 1408 if mode == "tn" else 1024), _tile(n, 1408), _tile(k, 1408)
    nk = k // tk
    dims = {"nn": NN, "nt": NT, "tn": TN}[mode]
    has_res = res is not None

    def body(*refs):
        if has_res:
            a_ref, b_ref, r_ref, o_ref, acc_ref = refs
        else:
            a_ref, b_ref, o_ref, acc_ref = refs
        kk = pl.program_id(2)

        @pl.when(kk == 0)
        def _():
            acc_ref[...] = jnp.zeros(acc_ref.shape, F32)

        acc_ref[...] += _dot(a_ref[...], b_ref[...], dims)

        @pl.when(kk == nk - 1)
        def _():
            out = acc_ref[...]
            if scale != 1.0:
                out = out * scale
            if has_res:
                out = out + r_ref[...]
            o_ref[...] = out.astype(o_ref.dtype)

    a_spec = (pl.BlockSpec((tk, tm), lambda i, j, kk: (kk, i)) if mode == "tn"
              else pl.BlockSpec((tm, tk), lambda i, j, kk: (i, kk)))
    b_spec = (pl.BlockSpec((tn, tk), lambda i, j, kk: (j, kk)) if mode == "nt"
              else pl.BlockSpec((tk, tn), lambda i, j, kk: (kk, j)))
    in_specs = [a_spec, b_spec]
    args = [a, b]
    if has_res:
        in_specs.append(pl.BlockSpec((tm, tn), lambda i, j, kk: (i, j)))
        args.append(res)
    order = ("parallel", "parallel", "arbitrary") if side is None else ("arbitrary",) * 3
    return _pcall(
        body, side=side, name=name, grid=(m // tm, n // tn, nk),
        in_specs=in_specs,
        out_specs=pl.BlockSpec((tm, tn), lambda i, j, kk: (i, j)),
        out_shape=jax.ShapeDtypeStruct((m, n), out_dtype),
        scratch_shapes=[pltpu.VMEM((tm, tn), F32)],
        compiler_params=_params(dimension_semantics=order),
    )(*args)


def _rms(x, g):
    return x * lax.rsqrt(jnp.mean(x * x, axis=-1, keepdims=True) + RMS_EPS) * g


def _silu_mul(gate, up):
    return gate / (1.0 + jnp.exp(-gate)) * up


def mm_gate_up(h, w_gu, name, side=None):
    m, k = h.shape
    f = w_gu.shape[0] // 2
    tm, tn = _tile(m, 512), _tile(f, 1408)
    nj = f // tn
    assert k <= 1408

    def body(h_ref, wg_ref, wu_ref, g_ref, u_ref, a_ref):
        ht = h_ref[...]
        for lo in range(0, tn, 512):
            cols = slice(lo, min(lo + 512, tn))
            gate, up = _dot(ht, wg_ref[cols, :], NT), _dot(ht, wu_ref[cols, :], NT)
            g_ref[:, cols] = gate.astype(g_ref.dtype)
            u_ref[:, cols] = up.astype(u_ref.dtype)
            a_ref[:, cols] = _silu_mul(gate, up).astype(a_ref.dtype)

    tile = pl.BlockSpec((tm, tn), lambda i, j: (i, j))
    return _pcall(
        body, side=side, name=name, grid=(m // tm, nj),
        in_specs=[pl.BlockSpec((tm, k), lambda i, j: (i, 0)),
                  pl.BlockSpec((tn, k), lambda i, j: (j, 0)),
                  pl.BlockSpec((tn, k), lambda i, j: (j + nj, 0))],
        out_specs=[tile, tile, tile],
        out_shape=[jax.ShapeDtypeStruct((m, f), BF16), jax.ShapeDtypeStruct((m, f), BF16),
                   jax.ShapeDtypeStruct((m, f), BF16)],
        compiler_params=_params(dimension_semantics=("arbitrary",) * 2),
    )(h, w_gu, w_gu)


def mm_down_act_bwd(dy, w_down, gate, up, name, side=None):
    m, d = dy.shape
    f = w_down.shape[0]
    tm, tn = _tile(m, 512), _tile(f, 1408)
    assert d <= 1408

    def body(dy_ref, w_ref, g_ref, u_ref, dg_ref, du_ref):
        dyt = dy_ref[...].astype(BF16)
        for lo in range(0, tn, 512):
            cols = slice(lo, min(lo + 512, tn))
            da = _dot(dyt, w_ref[cols, :], NT) * 0.5
            gate, up = g_ref[:, cols].astype(F32), u_ref[:, cols].astype(F32)
            s = 1.0 / (1.0 + jnp.exp(-gate))
            gs = gate * s
            du_ref[:, cols] = (da * gs).astype(du_ref.dtype)
            dg_ref[:, cols] = (da * up * s * (1.0 + gate - gs)).astype(dg_ref.dtype)

    tile = pl.BlockSpec((tm, tn), lambda i, j: (i, j))
    return _pcall(
        body, side=side, name=name, grid=(m // tm, f // tn),
        in_specs=[pl.BlockSpec((tm, d), lambda i, j: (i, 0)), pl.BlockSpec((tn, d), lambda i, j: (j, 0)), tile, tile],
        out_specs=[tile, tile],
        out_shape=[jax.ShapeDtypeStruct((m, f), BF16), jax.ShapeDtypeStruct((m, f), BF16)],
        compiler_params=_params(dimension_semantics=("arbitrary", "arbitrary")),
    )(dy, w_down, gate, up)


def mm_norm_bwd(a, b, x, g, dres, name, b_kd=False, side=None):
    halves = isinstance(a, (tuple, list))
    a0, a1 = a if halves else (a, None)
    m, k = a0.shape[0], a0.shape[1] * (2 if halves else 1)
    d = b.shape[1] if b_kd else b.shape[0]
    dims = NN if b_kd else NT
    tm, tk = _tile(m, 512), _tile(a0.shape[1], 1408)
    nk = k // tk
    nkh = a0.shape[1] // tk
    has_res = dres is not None

    def body(*refs):
        a_ref, b_ref, x_ref, g_ref = refs[:4]
        rest = refs[4:-3]
        a1_ref = rest[0] if halves else None
        r_ref = rest[-1] if has_res else None
        dx_ref, dg_ref, acc_ref = refs[-3:]
        i, kk = pl.program_id(0), pl.program_id(1)

        @pl.when(kk == 0)
        def _():
            acc_ref[...] = jnp.zeros(acc_ref.shape, F32)

        if halves:
            @pl.when(kk < nkh)
            def _():
                acc_ref[...] += _dot(a_ref[...], b_ref[...], dims)

            @pl.when(kk >= nkh)
            def _():
                acc_ref[...] += _dot(a1_ref[...], b_ref[...], dims)
        else:
            acc_ref[...] += _dot(a_ref[...], b_ref[...], dims)

        @pl.when(kk == nk - 1)
        def _():
            _, vjp = jax.vjp(_rms, x_ref[...], g_ref[...])
            dx, dg = vjp(acc_ref[...])
            dx_ref[...] = dx + r_ref[...] if has_res else dx

            @pl.when(i == 0)
            def _():
                dg_ref[...] = jnp.zeros(dg_ref.shape, F32)

            dg_ref[...] += dg

    rows = pl.BlockSpec((tm, d), lambda i, kk: (i, 0))
    first = pl.BlockSpec((tm, tk), lambda i, kk: (i, jnp.minimum(kk, nkh - 1)))
    second = pl.BlockSpec((tm, tk), lambda i, kk: (i, jnp.maximum(kk - nkh, 0)))
    b_spec = pl.BlockSpec((tk, d), lambda i, kk: (kk, 0)) if b_kd else pl.BlockSpec((d, tk), lambda i, kk: (0, kk))
    in_specs = ([first, b_spec, rows, pl.BlockSpec(g.shape, lambda i, kk: (0, 0))]
                + ([second] if halves else []) + ([rows] if has_res else []))
    return _pcall(
        body, side=side, name=name, grid=(m // tm, nk),
        in_specs=in_specs,
        out_specs=[rows, pl.BlockSpec(g.shape, lambda i, kk: (0, 0))],
        out_shape=[jax.ShapeDtypeStruct((m, d), F32), jax.ShapeDtypeStruct(g.shape, F32)],
        scratch_shapes=[pltpu.VMEM((tm, d), F32)],
        compiler_params=_params(dimension_semantics=("arbitrary", "arbitrary")),
    )(*([a0, b, x, g] + ([a1] if halves else []) + ([dres] if has_res else [])))


def _indicator(shape, head_axis, mod):
    lane = lax.broadcasted_iota(jnp.int32, shape, head_axis)
    other = lax.broadcasted_iota(jnp.int32, shape, 1 - head_axis)
    lane = jnp.bitwise_and(lane, HEAD_DIM - 1) if mod else jnp.right_shift(lane, 6)
    return jnp.where(lane == other, 1.0, 0.0).astype(BF16)


def _head_rms(split, xs, g):
    w = xs.shape[1]
    to_head, from_head = _indicator((w, BLOCK), 0, False), _indicator((BLOCK, w), 1, False)
    to_lane, from_lane = _indicator((HEAD_DIM, w), 1, True), _indicator((w, HEAD_DIM), 0, True)
    ss = split(xs * xs, to_head, from_head, 3)
    r = lax.rsqrt(ss * (1.0 / HEAD_DIM) + RMS_EPS)
    g_all = split(jnp.broadcast_to(g, (8, HEAD_DIM)), to_lane, from_lane, 3)[0:1]
    return xs * split(r, from_head, to_head, 3) * g_all


def _prep(split, x, qg, kg, segs):
    parts = []
    for start, width, kind in segs:
        xs = x[:, start:start + width]
        parts.append(xs if kind == "raw" else _head_rms(split, xs, qg if kind == "q" else kg))
    return jnp.concatenate(parts, axis=1)


def prep_fwd(x, qg, kg, segs, dils, name):
    n, w = x.shape
    tm = _tile(n, 256, 8)

    def fn(ids, xt, a, b, scratch):
        ops = _prep(_plain_split, xt, a, b, segs)
        return tuple(_to_strided(scratch, ops, d) for d in dils)

    return tcall(fn, (n // tm,), [_row(x, tm), _full(qg), _full(kg)],
                 [_out((n // d, d * w), BF16, (tm // d, d * w), lambda i: (i, 0)) for d in dils], name,
                 scratch=((w // BLOCK * tm, BLOCK), F32))


def prep_bwd(x, qg, kg, segs, grads, gather, name):
    n, w = x.shape
    tm = BLOCK
    nblk = n // tm
    nslot = 1 + max(slot for _, _, _, slot in grads)

    def fn(ids, xt, a, b, *t, scratch):
        tiles, dils = [None] * nslot, [None] * nslot
        for ti, (_, sh, d, slot) in zip(t, grads):
            ti = jnp.where(ids[0] + sh < nblk, ti, 0.0) if sh else ti
            tiles[slot] = ti if tiles[slot] is None else tiles[slot] + ti
            dils[slot] = d
        tiles = [_to_natural(scratch, ti, d) for ti, d in zip(tiles, dils)]
        _, vjp = jax.vjp(lambda x_, a_, b_: _prep(_split_dot_vjp, x_, a_, b_, segs), xt, a, b)
        return vjp(gather(*tiles))

    specs = [_in(a, (tm // d, a.shape[1]), (lambda i, sh=sh: (jnp.minimum(i + sh, nblk - 1), 0)))
             for a, sh, d, _ in grads]
    wmax = max(a.shape[1] // d for a, _, d, _ in grads)
    return tcall(fn, (nblk,), [_row(x, tm), _full(qg), _full(kg)] + specs,
                 [_row_out(n, w, BF16, tm), _acc_out(qg.shape), _acc_out(kg.shape)], name,
                 scratch=((wmax // BLOCK * tm, BLOCK), F32))


def rmsnorm_fwd(x, g, name, side=None):
    n, d = x.shape
    tm = _tile(n, 512, 8)
    res = tcall(lambda ids, xt, gt: (_rms(xt, gt),), (n // tm,), [_row(x, tm), _full(g)],
                [_row_out(n, d, BF16, tm)], name, side=side)
    if side is None:
        return res[0]
    return res[0][0], res[1]


def ffn_fwd(x, g, w_gu, w_down, tag, carry=None):
    h = rmsnorm_fwd(x, g, tag + "_norm")
    if carry is None:
        gate, up, a = mm_gate_up(h, w_gu, tag + "_gu")
        return mm(a, w_down, "nn", tag + "_down", scale=0.5, res=x), (x, h, gate, up, a)
    phase, bufs = carry
    (gate, up, a), bufs = mm_gate_up(h, w_gu, tag + "_gu", side=gather_side(phase, bufs))
    y, bufs = mm(a, w_down, "nn", tag + "_down", scale=0.5, res=x, side=gather_side(phase + 1, bufs))
    return y, (x, h, gate, up, a), bufs


def ffn_bwd(dy, saved, g, w_gu, w_down, tag, chain=None):
    x, h, gate, up, a = saved

    def carrying(name, call, **kw):
        side = None if chain is None else chain.side(name)
        out = call(name=tag + "_" + name, side=side, **kw)
        if side is None:
            return out
        chain.done(name, out[1])
        return out[0]

    dgate, dup = carrying("da", mm_down_act_bwd, dy=dy, w_down=w_down, gate=gate, up=up)
    d_wdown = carrying("dwd", mm, a=a, b=dy, mode="tn", scale=0.5)
    d_wgu = (carrying("dwgu", mm, a=dgate, b=h, mode="tn"), mm(dup, h, "tn", tag + "_dwup"))
    dx, dg = carrying("dh", mm_norm_bwd, a=(dgate, dup), b=w_gu, x=x, g=g, dres=dy, b_kd=True)
    return dx, dg, d_wgu, d_wdown


def _alibi(n_heads):
    return [float(s) for s in np.asarray(2.0 ** (-8.0 * np.arange(1, n_heads + 1) / n_heads), dtype=np.float32)]


def _banded_tile(dot, first, q, kp, kc, vp, vc, sinks, *, hkv, grp, max_dist, step, slopes, want_lse):
    row = lax.broadcasted_iota(jnp.int32, (BLOCK, 2 * BLOCK), 0)
    col = lax.broadcasted_iota(jnp.int32, (BLOCK, 2 * BLOCK), 1)
    dist = row + BLOCK - col
    valid = (dist >= 0) & (dist <= max_dist) & ((col >= BLOCK) | jnp.logical_not(first))
    distf = dist.astype(F32)

    def head(hd, qh, k2, v2):
        s = dot(qh, k2, True) * (HEAD_DIM ** -0.5)
        s = jnp.where(valid, s - (slopes[hd] * step) * distf, NEG_BIG)
        m = jnp.max(s, axis=-1, keepdims=True)
        if sinks is not None:
            pick = lax.broadcasted_iota(jnp.int32, sinks.shape, 1) == hd
            sk = jnp.sum(jnp.where(pick, sinks, 0.0), axis=1, keepdims=True)
            m = jnp.maximum(m, sk)
        m = lax.stop_gradient(m)
        p = jnp.exp(s - m)
        denom = jnp.sum(p, axis=-1, keepdims=True)
        if sinks is not None:
            denom = denom + jnp.exp(sk - m)
        return dot(p * (1.0 / denom), v2, False), m + jnp.log(denom)

    outs, lses = [], []
    if grp == 1:
        low = lax.broadcasted_iota(jnp.int32, (BLOCK, BLOCK), 1) < HEAD_DIM
        for pr in range(hkv // 2):
            sl = slice(pr * BLOCK, (pr + 1) * BLOCK)
            q2 = q[:, sl]
            k2 = jnp.concatenate([kp[:, sl], kc[:, sl]], axis=0)
            v2 = jnp.concatenate([vp[:, sl], vc[:, sl]], axis=0)
            o0, l0 = head(2 * pr, jnp.where(low, q2, 0.0), k2, v2)
            o1, l1 = head(2 * pr + 1, jnp.where(low, 0.0, q2), k2, v2)
            outs.append(jnp.where(low, o0, o1))
            lses.append(jnp.where(low, l0, l1))
    else:
        for hk in range(hkv):
            sl = slice(hk * HEAD_DIM, (hk + 1) * HEAD_DIM)
            k2 = jnp.concatenate([kp[:, sl], kc[:, sl]], axis=0)
            v2 = jnp.concatenate([vp[:, sl], vc[:, sl]], axis=0)
            for gi in range(grp):
                hd = hk * grp + gi
                o_h, l_h = head(hd, q[:, hd * HEAD_DIM:(hd + 1) * HEAD_DIM], k2, v2)
                outs.append(o_h)
                lses.append(jnp.broadcast_to(l_h, (BLOCK, HEAD_DIM)))
    o = jnp.concatenate(outs, axis=1)
    if want_lse:
        return o, jnp.concatenate(lses, axis=1)
    return (o,)


def _banded_specs(view, qcol, kcol, vcol, wq, wkv):
    def at(colfn, prev):
        if prev:
            return lambda r, n: (jnp.maximum(n - 1, 0), colfn(r))
        return lambda r, n: (n, colfn(r))
    return [
        _in(view, (BLOCK, wq), at(qcol, False)),
        _in(view, (BLOCK, wkv), at(kcol, True)),
        _in(view, (BLOCK, wkv), at(kcol, False)),
        _in(view, (BLOCK, wkv), at(vcol, True)),
        _in(view, (BLOCK, wkv), at(vcol, False)),
    ]


def banded_fwd(view, dil, cols, sinks, cfg, name):
    ns = view.shape[0]
    nb = ns // BLOCK
    wq, wkv = cfg["hkv"] * cfg["grp"] * HEAD_DIM, cfg["hkv"] * HEAD_DIM
    has_sinks = sinks is not None

    def fn(ids, q, kp, kc, vp, vc, *rest):
        q, kp, kc, vp, vc = [a.astype(F32) for a in (q, kp, kc, vp, vc)]
        return _banded_tile(_plain_dot, ids[1] == 0, q, kp, kc, vp, vc, rest[0] if has_sinks else None, **cfg)

    ins = _banded_specs(view, *cols, wq, wkv) + ([_full(sinks)] if has_sinks else [])
    outs = [_out((ns, dil * wq), F32 if cfg["want_lse"] else BF16, (BLOCK, wq), lambda r, n: (n, r))]
    if cfg["want_lse"]:
        outs.append(_out((ns, dil * wq), F32, (BLOCK, wq), lambda r, n: (n, r)))
    return tcall(fn, (dil, nb), ins, outs, name)


def banded_bwd(view, dil, cols, sinks, cfg, cts, name):
    ns = view.shape[0]
    nb = ns // BLOCK
    wq, wkv = cfg["hkv"] * cfg["grp"] * HEAD_DIM, cfg["hkv"] * HEAD_DIM
    has_sinks = sinks is not None
    assert len(cts) == (2 if cfg["want_lse"] else 1)

    def fn(ids, q, kp, kc, vp, vc, *rest):
        sk = rest[0] if has_sinks else None
        ct = rest[1 if has_sinks else 0:]
        first = ids[1] == 0

        def f(q, kp, kc, vp, vc, *s):
            return _banded_tile(_dot_vjp, first, q, kp, kc, vp, vc, s[0] if has_sinks else None, **cfg)

        prim = tuple(a.astype(F32) for a in (q, kp, kc, vp, vc)) + ((sk,) if has_sinks else ())
        _, vjp = jax.vjp(f, *prim)
        return vjp(tuple(c.astype(F32) for c in ct))

    ins = (_banded_specs(view, *cols, wq, wkv) + ([_full(sinks)] if has_sinks else [])
           + [_in(a, (BLOCK, wq), (lambda r, n, cf=cf: (n, cf(r)))) for (a, cf) in cts])
    blk = lambda w: _out((ns, dil * w), F32, (BLOCK, w), lambda r, n: (n, r))
    outs = [blk(wq), blk(wkv), blk(wkv), blk(wkv), blk(wkv)]
    if has_sinks:
        outs.append(_acc_out(sinks.shape))
    return tcall(fn, (dil, nb), ins, outs, name)


def _log_sigmoid(z):
    return jnp.minimum(z, 0.0) - jnp.log(1.0 + jnp.exp(-jnp.abs(z)))


SB_PAIRS = 4


def _sb_pair(dot, suffix, qh, kb, vb, r_in, mask):
    z = dot(qh, kb, True) * (HEAD_DIM ** -0.5)
    lsp = _log_sigmoid(z)
    log_keep = jnp.where(mask, lsp - z, 0.0)
    log_after = suffix(log_keep) + r_in
    a = jnp.where(mask, jnp.exp(lsp + log_after), 0.0)
    return dot(a, vb, False), r_in + jnp.sum(log_keep, axis=1, keepdims=True)


def sb_fwd(qkv, qcb, kcb, vcb, name, side=None):
    s = qkv.shape[0]
    nb = s // BLOCK
    pairs = B_HEADS // 2
    wide = SB_PAIRS * BLOCK
    assert pairs % SB_PAIRS == 0 and qcb % SB_PAIRS == 0 and kcb % SB_PAIRS == 0 and vcb % SB_PAIRS == 0

    def body(q_ref, k_ref, v_ref, o_ref):
        n = pl.program_id(1)
        low = lax.broadcasted_iota(jnp.int32, (BLOCK, BLOCK), 1) < HEAD_DIM
        before = (lax.broadcasted_iota(jnp.int32, (2 * BLOCK, BLOCK), 1)
                  < jnp.bitwise_and(lax.broadcasted_iota(jnp.int32, (2 * BLOCK, BLOCK), 0), BLOCK - 1))
        after = _tri(True)
        suffix = lambda t: _split_dot(t, after)
        qs = []
        for p in range(SB_PAIRS):
            q2 = q_ref[:, p * BLOCK:(p + 1) * BLOCK].astype(F32)
            qs.append(jnp.concatenate([jnp.where(low, q2, 0.0), jnp.where(low, 0.0, q2)], axis=0))

        def cond(c):
            return jnp.logical_and(c[0] >= 0, c[1] > SB_SKIP_LOG)

        def step(c):
            kb, _, rs, accs = c
            rows = pl.ds(pl.multiple_of(kb * BLOCK, BLOCK), BLOCK)
            mask = jnp.logical_or(before, kb != n)
            new_r, new_acc, top = [], [], None
            for p in range(SB_PAIRS):
                cols = slice(p * BLOCK, (p + 1) * BLOCK)
                o_part, r_out = _sb_pair(_plain_dot, suffix, qs[p], k_ref[rows, cols], v_ref[rows, cols], rs[p], mask)
                new_r.append(r_out)
                new_acc.append(accs[p] + o_part)
                top = jnp.max(r_out) if top is None else jnp.maximum(top, jnp.max(r_out))
            return kb - 1, top, tuple(new_r), tuple(new_acc)

        init = (n, jnp.float32(0.0), tuple(jnp.zeros((2 * BLOCK, 1), F32) for _ in range(SB_PAIRS)),
                tuple(jnp.zeros((2 * BLOCK, BLOCK), F32) for _ in range(SB_PAIRS)))
        accs = lax.while_loop(cond, step, init)[3]
        for p in range(SB_PAIRS):
            o_ref[:, p * BLOCK:(p + 1) * BLOCK] = jnp.where(low, accs[p][:BLOCK], accs[p][BLOCK:]).astype(o_ref.dtype)

    return _pcall(
        body, side=side, name=name, grid=(pairs // SB_PAIRS, nb),
        in_specs=[pl.BlockSpec((BLOCK, wide), lambda g, n: (n, qcb // SB_PAIRS + g)),
                  pl.BlockSpec((s, wide), lambda g, n: (0, kcb // SB_PAIRS + g), pipeline_mode=pl.Buffered(1)),
                  pl.BlockSpec((s, wide), lambda g, n: (0, vcb // SB_PAIRS + g), pipeline_mode=pl.Buffered(1))],
        out_specs=pl.BlockSpec((BLOCK, wide), lambda g, n: (n, g)),
        out_shape=jax.ShapeDtypeStruct((s, pairs * BLOCK), BF16),
        compiler_params=_params(),
    )(qkv, qkv, qkv)


def sb_bwd(qkv, qcb, kcb, vcb, do, docb, name, side=None):
    s = qkv.shape[0]
    nb = s // BLOCK
    pairs = B_HEADS // 2
    wide = SB_PAIRS * BLOCK
    assert docb % SB_PAIRS == 0

    def body(q_ref, k_ref, v_ref, do_ref, dq_ref, dk_ref, dv_ref, r_ref):
        n = pl.program_id(1)

        @pl.when(n == 0)
        def _():
            dk_ref[...] = jnp.zeros(dk_ref.shape, F32)
            dv_ref[...] = jnp.zeros(dv_ref.shape, F32)

        low = lax.broadcasted_iota(jnp.int32, (BLOCK, BLOCK), 1) < HEAD_DIM
        before = (lax.broadcasted_iota(jnp.int32, (2 * BLOCK, BLOCK), 1)
                  < jnp.bitwise_and(lax.broadcasted_iota(jnp.int32, (2 * BLOCK, BLOCK), 0), BLOCK - 1))
        after, earlier = _tri(True), _tri(False)
        suffix = lambda t: _split_dot_vjp(t, after, earlier, 2)
        stack = lambda t: jnp.concatenate([jnp.where(low, t, 0.0), jnp.where(low, 0.0, t)], axis=0)
        qs = [stack(q_ref[:, p * BLOCK:(p + 1) * BLOCK].astype(F32)) for p in range(SB_PAIRS)]
        dos = [stack(do_ref[:, p * BLOCK:(p + 1) * BLOCK].astype(F32)) for p in range(SB_PAIRS)]

        def cond(c):
            return jnp.logical_and(c[0] >= 0, c[1] > SB_SKIP_LOG)

        def down(c):
            kb, _, rs = c
            rows = pl.ds(pl.multiple_of(kb * BLOCK, BLOCK), BLOCK)
            mask = jnp.logical_or(before, kb != n)
            new_r, top = [], None
            for h in range(SB_PAIRS):
                cols = slice(h * BLOCK, (h + 1) * BLOCK)
                r_ref[h, kb] = rs[h]
                z = _dot(qs[h], k_ref[rows, cols], NT) * (HEAD_DIM ** -0.5)
                log_keep = jnp.where(mask, _log_sigmoid(z) - z, 0.0)
                r_out = rs[h] + jnp.sum(log_keep, axis=1, keepdims=True)
                new_r.append(r_out)
                top = jnp.max(r_out) if top is None else jnp.maximum(top, jnp.max(r_out))
            return kb - 1, top, tuple(new_r)

        init = (n, jnp.float32(0.0), tuple(jnp.zeros((2 * BLOCK, 1), F32) for _ in range(SB_PAIRS)))
        last = lax.while_loop(cond, down, init)[0] + 1

        def up(kb, c):
            dqs, g_rs = c
            rows = pl.ds(pl.multiple_of(kb * BLOCK, BLOCK), BLOCK)
            mask = jnp.logical_or(before, kb != n)
            new_dq, new_g = [], []
            for h in range(SB_PAIRS):
                cols = slice(h * BLOCK, (h + 1) * BLOCK)
                _, vjp = jax.vjp(lambda q_, k_, v_, r_: _sb_pair(_dot_vjp, suffix, q_, k_, v_, r_, mask),
                                 qs[h], k_ref[rows, cols].astype(F32), v_ref[rows, cols].astype(F32), r_ref[h, kb])
                dq_c, dk_c, dv_c, g_in = vjp((dos[h], g_rs[h]))
                dk_ref[rows, cols] += dk_c
                dv_ref[rows, cols] += dv_c
                new_dq.append(dqs[h] + dq_c)
                new_g.append(g_in)
            return tuple(new_dq), tuple(new_g)

        init = (tuple(jnp.zeros((2 * BLOCK, BLOCK), F32) for _ in range(SB_PAIRS)),
                tuple(jnp.zeros((2 * BLOCK, 1), F32) for _ in range(SB_PAIRS)))
        dqs = lax.fori_loop(last, n + 1, up, init)[0]
        for p in range(SB_PAIRS):
            dq_ref[:, p * BLOCK:(p + 1) * BLOCK] = jnp.where(low, dqs[p][:BLOCK], dqs[p][BLOCK:])

    full = jax.ShapeDtypeStruct((s, pairs * BLOCK), F32)
    return _pcall(
        body, side=side, name=name, grid=(pairs // SB_PAIRS, nb),
        in_specs=[pl.BlockSpec((BLOCK, wide), lambda g, n: (n, qcb // SB_PAIRS + g)),
                  pl.BlockSpec((s, wide), lambda g, n: (0, kcb // SB_PAIRS + g), pipeline_mode=pl.Buffered(1)),
                  pl.BlockSpec((s, wide), lambda g, n: (0, vcb // SB_PAIRS + g), pipeline_mode=pl.Buffered(1)),
                  pl.BlockSpec((BLOCK, wide), lambda g, n: (n, docb // SB_PAIRS + g))],
        out_specs=[pl.BlockSpec((BLOCK, wide), lambda g, n: (n, g)),
                   pl.BlockSpec((s, wide), lambda g, n: (0, g), pipeline_mode=pl.Buffered(1)),
                   pl.BlockSpec((s, wide), lambda g, n: (0, g), pipeline_mode=pl.Buffered(1))],
        out_shape=[full, full, full],
        scratch_shapes=[pltpu.VMEM((SB_PAIRS, nb, 2 * BLOCK, 1), F32)],
        compiler_params=_params(),
    )(qkv, qkv, qkv, do)


def _xa_tile(dot, q, kv, qg, kg):
    hd = q.shape[1] // X_HEADS
    outs = []
    for h in range(X_HEADS):
        qh = _rms(q[:, h * hd:(h + 1) * hd], qg)
        kh = _rms(kv[:, h * hd:(h + 1) * hd], kg)
        vh = kv[:, (X_HEADS + h) * hd:(X_HEADS + h + 1) * hd]
        sc = dot(qh, kh, True) * (hd ** -0.5)
        m = lax.stop_gradient(jnp.max(sc, axis=-1, keepdims=True))
        p = jnp.exp(sc - m)
        outs.append(dot(p * (1.0 / jnp.sum(p, axis=-1, keepdims=True)), vh, False))
    return jnp.concatenate(outs, axis=1)


def xa_core_fwd(q, kv, qg, kg, name):
    n, d = q.shape
    tm = _tile(n, 256, 8)
    (o,) = tcall(lambda ids, qt, kvt, qgt, kgt: (_xa_tile(_plain_dot, qt, kvt, qgt, kgt),), (n // tm,),
                 [_row(q, tm), _full(kv), _full(qg), _full(kg)], [_row_out(n, d, BF16, tm)], name)
    return o


def xa_core_bwd(q, kv, qg, kg, do, name):
    n, d = q.shape
    tm = _tile(n, 256, 8)

    def fn(ids, qt, kvt, qgt, kgt, dot_):
        _, vjp = jax.vjp(functools.partial(_xa_tile, _dot_vjp), qt, kvt, qgt, kgt)
        return vjp(dot_.astype(F32))

    return tcall(fn, (n // tm,), [_row(q, tm), _full(kv), _full(qg), _full(kg), _row(do, tm)],
                 [_row_out(n, d, BF16, tm), _acc_out(kv.shape), _acc_out(qg.shape), _acc_out(kg.shape)], name)


def _ev_reorder(a):
    return jnp.concatenate([a[0:512], a[768:2304], a[512:768]], axis=0)


def _ev_restore(a):
    return jnp.concatenate([a[0:512], a[2048:2304], a[512:2048]], axis=0)


_EV_SEGS = ((0, 512, "q"), (512, 1536, "raw"), (2048, 128, "k"), (2176, 128, "raw"))
_A_CFG = dict(hkv=A_KV_HEADS, grp=A_Q_HEADS // A_KV_HEADS, max_dist=BLOCK - 1, step=1.0, slopes=_alibi(A_Q_HEADS),
              want_lse=False)
_A_COLS = (lambda r: 0, lambda r: 16, lambda r: 17)


def even_mixer_fwd(x, h, w_in, qg, kg, sinks, w_out, tag, side=None):
    qkv = mm(h, w_in, "nt", tag + "_in")
    (ops,) = prep_fwd(qkv, qg, kg, _EV_SEGS, (1,), tag + "_prep")
    (o_a,) = banded_fwd(ops, 1, _A_COLS, sinks, _A_CFG, tag + "_swa")
    o_b = sb_fwd(ops, 4, 8, 12, tag + "_sb", side=side)
    carried = None
    if side is not None:
        o_b, carried = o_b
    o = jnp.concatenate([o_a, o_b], axis=1)
    y = mm(o, w_out, "nn", tag + "_out", res=x)
    return y, (x, h, qkv, ops, o), carried


def even_mixer_bwd(dy, saved, g, w_in, qg, kg, sinks, w_out, tag, side=None, last_side=None):
    x, h, qkv, ops, o = saved
    do = mm(dy, w_out, "nt", tag + "_do", out_dtype=BF16)
    d_wout = mm(o, dy, "tn", tag + "_dwout")
    dqa, dkp, dkc, dvp, dvc, dsinks = banded_bwd(ops, 1, _A_COLS, sinks, _A_CFG, [(do, lambda r: 0)], tag + "_dswa")
    res = sb_bwd(ops, 4, 8, 12, do, 4, tag + "_dsb", side=side)
    carried = None
    if side is not None:
        res, carried = res
    dqb, dkb, dvb = res
    dqkv, dqg, dkg = prep_bwd(
        qkv, qg, kg, _EV_SEGS,
        [(dqa, 0, 1, 0), (dqb, 0, 1, 1), (dkb, 0, 1, 2), (dvb, 0, 1, 3), (dkc, 0, 1, 4), (dkp, 1, 1, 4), (dvc, 0, 1, 5),
         (dvp, 1, 1, 5)],
        lambda *t: jnp.concatenate(t, axis=1), tag + "_dqkv")
    d_win = mm(dqkv, h, "tn", tag + "_dwin")
    last = None if last_side is None else last_side(d_win, d_wout)
    if last is None:
        dx, dg = mm_norm_bwd(dqkv, w_in, x, g, dy, tag + "_dh", b_kd=True)
    else:
        (dx, dg), got = mm_norm_bwd(dqkv, w_in, x, g, dy, tag + "_dh", b_kd=True, side=last[0])
        last[1](got)
    return dx, dg, d_win, dqg, dkg, dsinks, d_wout, carried


def _c_cfg(window, dil):
    return dict(hkv=C_HEADS, grp=1, max_dist=window // dil, step=float(dil), slopes=_alibi(C_HEADS), want_lse=True)


_C_COLS = (lambda r: 3 * r, lambda r: 3 * r + 1, lambda r: 3 * r + 2)
_OD_SEGS = ((0, 1024, "q"), (1024, 1024, "k"), (2048, 1024, "raw"))


def _combine(o1, o2, o3, l1, l2, l3):
    m = lax.stop_gradient(jnp.maximum(jnp.maximum(l1, l2), l3))
    e1, e2, e3 = jnp.exp(l1 - m), jnp.exp(l2 - m), jnp.exp(l3 - m)
    tot = e1 + e2 + e3
    return (e1 / tot) * o1 + (e2 / tot) * o2 + (e3 / tot) * o3


def odd_mixer_fwd(x, g, w_in, qg, kg, w_out, tag):
    n, d = x.shape
    h = rmsnorm_fwd(x, g, tag + "_norm")
    qkv = mm(h, w_in, "nt", tag + "_in")
    dils = [dil for _, dil in C_PATTERNS]
    ops = prep_fwd(qkv, qg, kg, _OD_SEGS, dils, tag + "_prep")
    os_, ls_ = [], []
    for (window, dil), ops_d in zip(C_PATTERNS, ops):
        o_p, l_p = banded_fwd(ops_d, dil, _C_COLS, None, _c_cfg(window, dil), f"{tag}_dil{dil}")
        os_.append(o_p)
        ls_.append(l_p)
    tm = BLOCK
    lay = lambda a, dil: _in(a, (tm // dil, a.shape[1]), lambda i: (i, 0))
    views = [lay(a, dil) for a, dil in zip(os_ + ls_, dils + dils)]

    def comb(ids, *t, scratch):
        return (_combine(*[_to_natural(scratch, a, dil) for a, dil in zip(t, dils + dils)]),)

    (o,) = tcall(comb, (n // tm,), views, [_row_out(n, d, BF16, tm)], tag + "_comb",
                 scratch=((d // BLOCK * tm, BLOCK), F32))
    y = mm(o, w_out, "nn", tag + "_out", res=x)
    return y, (x, h, qkv, ops, views, o)


def odd_mixer_bwd(dy, saved, g, w_in, qg, kg, w_out, tag):
    x, h, qkv, ops, views, o = saved
    n, d = x.shape
    do = mm(dy, w_out, "nt", tag + "_do")
    d_wout = mm(o, dy, "tn", tag + "_dwout")
    tm = BLOCK
    dils = [dil for _, dil in C_PATTERNS]

    def comb_bwd(ids, *t, scratch):
        _, vjp = jax.vjp(_combine, *[_to_natural(scratch, a, dil) for a, dil in zip(t[:6], dils + dils)])
        return tuple(_to_strided(scratch, c, dil) for c, dil in zip(vjp(t[6]), dils + dils))

    cts = tcall(comb_bwd, (n // tm,), views + [_row(do, tm)],
                [_out((n // dil, dil * d), F32, (tm // dil, dil * d), lambda i: (i, 0)) for dil in dils + dils],
                tag + "_dcomb", scratch=((d // BLOCK * tm, BLOCK), F32))
    dqs, dks, dvs = [], [], []
    for p, ((window, dil), ops_d) in enumerate(zip(C_PATTERNS, ops)):
        dq, dkp, dkc, dvp, dvc = banded_bwd(ops_d, dil, _C_COLS, None, _c_cfg(window, dil),
                                            [(cts[p], lambda r: r), (cts[3 + p], lambda r: r)], f"{tag}_ddil{dil}")
        dqs.append((dq, 0, dil, p))
        dks += [(dkc, 0, dil, 3 + p), (dkp, dil, dil, 3 + p)]
        dvs += [(dvc, 0, dil, 6 + p), (dvp, dil, dil, 6 + p)]

    def gather(*t):
        return jnp.concatenate([t[0] + t[1] + t[2], t[3] + t[4] + t[5], t[6] + t[7] + t[8]], axis=1)

    dqkv, dqg, dkg = prep_bwd(qkv, qg, kg, _OD_SEGS, dqs + dks + dvs, gather, tag + "_dqkv")
    d_win = mm(dqkv, h, "tn", tag + "_dwin")
    dx, dg = mm_norm_bwd(dqkv, w_in, x, g, dy, tag + "_dh", b_kd=True)
    return dx, dg, d_win, dqg, dkg, d_wout


def xa_fwd(x, mem, g, gm, w_q, w_kv, qg, kg, w_o, tag):
    h = rmsnorm_fwd(x, g, tag + "_norm")
    q = mm(h, w_q, "nn", tag + "_q")
    mn = rmsnorm_fwd(mem, gm, tag + "_mnorm")
    kv = mm(mn, w_kv, "nt", tag + "_kv")
    o = xa_core_fwd(q, kv, qg, kg, tag + "_core")
    y = mm(o, w_o, "nn", tag + "_o", res=x)
    return y, (x, h, q, mn, kv, o)


def xa_bwd(dy, saved, mem, g, gm, w_q, w_kv, qg, kg, w_o, tag):
    x, h, q, mn, kv, o = saved
    do = mm(dy, w_o, "nt", tag + "_do", out_dtype=BF16)
    d_wo = mm(o, dy, "tn", tag + "_dwo")
    dq, dkv, dqg, dkg = xa_core_bwd(q, kv, qg, kg, do, tag + "_dcore")
    d_wq = mm(h, dq, "tn", tag + "_dwq")
    dx, dg = mm_norm_bwd(dq, w_q, x, g, dy, tag + "_dh")
    d_wkv = mm(dkv, mn, "tn", tag + "_dwkv")
    _, dgm = mm_norm_bwd(dkv, w_kv, mem, gm, None, tag + "_dmn", b_kd=True)
    return dx, dg, dgm, d_wq, d_wkv, dqg, dkg, d_wo


def loss_head(y, target, name):
    n, d = y.shape
    tm = _tile(n, 512, 8)

    def fn(ids, yt, tt):
        e = yt - tt
        return e * (1.0 / d), jnp.sum(e * e, axis=0, keepdims=True)

    return tcall(fn, (n // tm,), [_row(y, tm), _row(target, tm)], [_row_out(n, d, F32, tm), _acc_out((1, d))], name)


_ANY = pl.BlockSpec(memory_space=pl.ANY)


def all_gather_blocks(blocks):
    nb = len(blocks)

    def body(*refs):
        x_refs, out_refs = refs[:nb], refs[nb:2 * nb]
        send_sems, recv_sems, local_sems = refs[2 * nb:]
        x, y, c = lax.axis_index("x"), lax.axis_index("y"), lax.axis_index("c")
        me, sibling = (x, y, c), (x, y, 1 - c)
        over_x, over_y, diagonal = (1 - x, y), (x, 1 - y), (1 - x, 1 - y)
        relay_of = ((1 - x) * (1 - c) + x * c, y * (1 - c) + (1 - y) * c)
        relay_to = (x * (1 - c) + (1 - x) * c, (1 - y) * (1 - c) + y * c)

        def copy(b, k, blk, to, own=False):
            px, py, pc = blk
            slot = out_refs[b].at[4 * px + 2 * py + pc]
            return pltpu.make_async_remote_copy(
                src_ref=x_refs[b] if own else slot, dst_ref=slot,
                send_sem=send_sems.at[7 * b + k], recv_sem=recv_sems.at[7 * b + k], device_id=to, device_id_type=MESH)

        mine = [pltpu.make_async_copy(x_refs[b], out_refs[b].at[4 * x + 2 * y + c], local_sems.at[b]) for b in range(nb)]
        for cp in mine:
            cp.start()
        sent = []
        for b in range(nb):
            sent += [copy(b, 0, me, sibling, own=True), copy(b, 1, me, (*over_x, c), own=True),
                     copy(b, 2, me, (*over_y, c), own=True)]
        for cp in sent:
            cp.start()
        for b in range(nb):
            copy(b, 1, (*over_x, c), me).wait_recv()
            copy(b, 2, (*over_y, c), me).wait_recv()
            later = [copy(b, 3, (*relay_of, c), (*relay_to, c)), copy(b, 4, (*over_x, c), sibling),
                     copy(b, 5, (*over_y, c), sibling)]
            for cp in later:
                cp.start()
            sent += later
        for b in range(nb):
            copy(b, 3, (*diagonal, c), me).wait_recv()
            fwd = copy(b, 6, (*diagonal, c), sibling)
            fwd.start()
            sent.append(fwd)
        for b in range(nb):
            copy(b, 0, sibling, me).wait_recv()
            for k, chip in ((4, over_x), (5, over_y), (6, diagonal)):
                copy(b, k, (*chip, 1 - c), me).wait_recv()
        for cp in sent:
            cp.wait_send()
        for cp in mine:
            cp.wait()

    return _pcall(
        body, name="weights_all_gather",
        in_specs=[_ANY] * nb, out_specs=[_ANY] * nb,
        out_shape=[jax.ShapeDtypeStruct((N_DEV,) + a.shape, a.dtype) for a in blocks],
        scratch_shapes=[pltpu.SemaphoreType.DMA((7 * nb,)), pltpu.SemaphoreType.DMA((7 * nb,)),
                        pltpu.SemaphoreType.DMA((nb,))],
    )(*blocks)


def pair_exchange(bufs):
    nb = len(bufs)

    def body(*refs):
        srcs, dsts = refs[:nb], refs[nb:2 * nb]
        send_sems, recv_sems = refs[2 * nb:]
        x, y, c = lax.axis_index("x"), lax.axis_index("y"), lax.axis_index("c")
        copies = []
        for b in range(nb):
            for j in range(4):
                cp = pltpu.make_async_remote_copy(
                    src_ref=srcs[b].at[2 * j + (1 - c)], dst_ref=dsts[b].at[j], send_sem=send_sems.at[4 * b + j],
                    recv_sem=recv_sems.at[4 * b + j], device_id=(x, y, 1 - c), device_id_type=MESH)
                cp.start()
                copies.append(cp)
        for cp in copies:
            cp.wait()

    return _pcall(
        body, name="grads_pair_exchange",
        in_specs=[_ANY] * nb, out_specs=[_ANY] * nb,
        out_shape=[jax.ShapeDtypeStruct((4,) + a.shape[1:], a.dtype) for a in bufs],
        scratch_shapes=[pltpu.SemaphoreType.DMA((4 * nb,)), pltpu.SemaphoreType.DMA((4 * nb,))],
    )(*bufs)


def pair_sum(g, got, c, out_dtype, name):
    r, w = g.shape[1:]
    tr = _tile(r, 512, 16)

    def body(c_ref, a_ref, b_ref, o_ref):
        o_ref[...] = (a_ref[...].astype(F32) + b_ref[...].astype(F32)).astype(o_ref.dtype)

    return _pcall(
        body, name=name,
        grid_spec=pltpu.PrefetchScalarGridSpec(
            num_scalar_prefetch=1, grid=(4, r // tr),
            in_specs=[pl.BlockSpec((None, tr, w), lambda j, i, c_ref: (2 * j + c_ref[0], i, 0)),
                      pl.BlockSpec((None, tr, w), lambda j, i, c_ref: (j, i, 0))],
            out_specs=pl.BlockSpec((None, tr, w), lambda j, i, c_ref: (j, i, 0))),
        out_shape=jax.ShapeDtypeStruct((4,) + g.shape[1:], out_dtype),
        compiler_params=_params(),
    )(c, g, got)


def chip_exchange(parts):
    nb = len(parts)

    def body(*refs):
        srcs, dsts = refs[:nb], refs[nb:2 * nb]
        send_sems, recv_sems, local_sems = refs[2 * nb:]
        x, y, c = lax.axis_index("x"), lax.axis_index("y"), lax.axis_index("c")
        my_chip = 2 * x + y
        copies = []
        for b in range(nb):
            mine = pltpu.make_async_copy(srcs[b].at[my_chip], dsts[b].at[my_chip], local_sems.at[b])
            mine.start()
            copies.append(mine)
            for k, (tx, ty) in enumerate([(1 - x, y), (x, 1 - y), (1 - x, 1 - y)]):
                cp = pltpu.make_async_remote_copy(
                    src_ref=srcs[b].at[2 * tx + ty], dst_ref=dsts[b].at[my_chip], send_sem=send_sems.at[3 * b + k],
                    recv_sem=recv_sems.at[3 * b + k], device_id=(tx, ty, c), device_id_type=MESH)
                cp.start()
                copies.append(cp)
        for cp in copies:
            cp.wait()

    return _pcall(
        body, name="grads_chip_exchange",
        in_specs=[_ANY] * nb, out_specs=[_ANY] * nb,
        out_shape=[jax.ShapeDtypeStruct(a.shape, a.dtype) for a in parts],
        scratch_shapes=[pltpu.SemaphoreType.DMA((3 * nb,)), pltpu.SemaphoreType.DMA((3 * nb,)),
                        pltpu.SemaphoreType.DMA((nb,))],
    )(*parts)


def chip_sum(parts, name):
    r, w = parts.shape[1:]
    tr = _tile(r, 512, 16)
    spec = lambda j: _in(parts, (None, tr, w), lambda i, j=j: (j, i, 0))

    def fn(ids, a, b, c_, d):
        a, b, c_, d = [t.astype(F32) for t in (a, b, c_, d)]
        return (((a + b) + c_) + d,)

    (out,) = tcall(fn, (r // tr,), [spec(j) for j in range(4)],
                   [_out((r, w), F32, (tr, w), lambda i: (i, 0))], name)
    return out


def _remote(src, dst, send_sems, recv_sems, k, to):
    return functools.partial(pltpu.make_async_remote_copy, src_ref=src, dst_ref=dst, send_sem=send_sems.at[k],
                             recv_sem=recv_sems.at[k], device_id=to, device_id_type=MESH)


def _gather_plan(phase, nb):
    def plan(ins, outs, send_sems, recv_sems, local_sems):
        x, y, c = lax.axis_index("x"), lax.axis_index("y"), lax.axis_index("c")
        me, sibling = (x, y, c), (x, y, 1 - c)
        over_x, over_y, diagonal = (1 - x, y), (x, 1 - y), (1 - x, 1 - y)
        relay_of = ((1 - x) * (1 - c) + x * c, y * (1 - c) + (1 - y) * c)
        relay_to = (x * (1 - c) + (1 - x) * c, (1 - y) * (1 - c) + y * c)
        local, sends, recvs = [], [], []
        for b in range(nb):
            slot = lambda chip, core, b=b: outs[b].at[4 * chip[0] + 2 * chip[1] + core]
            if phase == 0:
                local.append(functools.partial(pltpu.make_async_copy, ins[b], slot((x, y), c), local_sems.at[b]))
                moves = [(ins[b], slot((x, y), c), to) for to in (sibling, (*over_x, c), (*over_y, c))]
                arrive = [slot((x, y), 1 - c), slot(over_x, c), slot(over_y, c)]
            elif phase == 1:
                moves = [(slot(relay_of, c), slot(relay_of, c), (*relay_to, c)),
                         (slot(over_x, c), slot(over_x, c), sibling), (slot(over_y, c), slot(over_y, c), sibling)]
                arrive = [slot(diagonal, c), slot(over_x, 1 - c), slot(over_y, 1 - c)]
            else:
                moves = [(slot(diagonal, c), slot(diagonal, c), sibling)]
                arrive = [slot(diagonal, 1 - c)]
            sends += [_remote(src, dst, send_sems, recv_sems, 3 * b + k, to) for k, (src, dst, to) in enumerate(moves)]
            recvs += [_remote(dst, dst, send_sems, recv_sems, 3 * b + k, me) for k, dst in enumerate(arrive)]
        return local, sends, recvs
    return plan


def gather_side(phase, arrays):
    nb = len(arrays)
    if phase == 0:
        shapes = [jax.ShapeDtypeStruct((N_DEV,) + a.shape, a.dtype) for a in arrays]
        return Side(arrays, shapes, 3 * nb, nb, _gather_plan(0, nb))
    shapes = [jax.ShapeDtypeStruct(a.shape, a.dtype) for a in arrays]
    return Side(arrays, shapes, 3 * nb, 0, _gather_plan(phase, nb), aliased=True)


def pair_side(bufs):
    nb = len(bufs)

    def plan(ins, outs, send_sems, recv_sems, local_sems):
        x, y, c = lax.axis_index("x"), lax.axis_index("y"), lax.axis_index("c")
        sends = [_remote(ins[b].at[2 * j + (1 - c)], outs[b].at[j], send_sems, recv_sems, 4 * b + j, (x, y, 1 - c))
                 for b in range(nb) for j in range(4)]
        recvs = [_remote(outs[b].at[j], outs[b].at[j], send_sems, recv_sems, 4 * b + j, (x, y, c))
                 for b in range(nb) for j in range(4)]
        return [], sends, recvs

    shapes = [jax.ShapeDtypeStruct((4,) + a.shape[1:], a.dtype) for a in bufs]
    return Side(bufs, shapes, 4 * nb, 0, plan)


def chip_side(parts):
    nb = len(parts)

    def plan(ins, outs, send_sems, recv_sems, local_sems):
        x, y, c = lax.axis_index("x"), lax.axis_index("y"), lax.axis_index("c")
        my_chip = 2 * x + y
        peers = [(1 - x, y), (x, 1 - y), (1 - x, 1 - y)]
        local = [functools.partial(pltpu.make_async_copy, ins[b].at[my_chip], outs[b].at[my_chip], local_sems.at[b])
                 for b in range(nb)]
        sends = [_remote(ins[b].at[2 * tx + ty], outs[b].at[my_chip], send_sems, recv_sems, 3 * b + k, (tx, ty, c))
                 for b in range(nb) for k, (tx, ty) in enumerate(peers)]
        recvs = [_remote(outs[b].at[2 * tx + ty], outs[b].at[2 * tx + ty], send_sems, recv_sems, 3 * b + k, (x, y, c))
                 for b in range(nb) for k, (tx, ty) in enumerate(peers)]
        return local, sends, recvs

    shapes = [jax.ShapeDtypeStruct(a.shape, a.dtype) for a in parts]
    return Side(parts, shapes, 3 * nb, nb, plan)


def adamw(w, g, m, v, name):
    shape = w.shape
    cols = shape[-1]
    rows = int(np.prod(shape[:-1]))
    w2, g2, m2, v2 = [a.reshape(rows, cols) for a in (w, g, m, v)]
    tr = _tile(rows, 256, 8) if rows % 8 == 0 else rows

    def fn(ids, wt, gt, mt, vt):
        m_new = ADAM_B1 * mt + (1.0 - ADAM_B1) * gt
        v_new = ADAM_B2 * vt + (1.0 - ADAM_B2) * (gt * gt)
        m_hat = m_new / (1.0 - ADAM_B1 ** ADAM_STEP)
        v_hat = v_new / (1.0 - ADAM_B2 ** ADAM_STEP)
        delta = -ADAM_LR * (m_hat / (jnp.sqrt(v_hat) + ADAM_EPS) + ADAM_WD * wt)
        return delta, m_new, v_new

    res = tcall(fn, (rows // tr,), [_row(a, tr) for a in (w2, g2, m2, v2)],
                [_row_out(rows, cols, F32, tr) for _ in range(3)], name)
    return [a.reshape(shape) for a in res]


_MATS = [("ffn1_w_gu", "col"), ("ffn1_w_down", "row"), ("ev_w_in", "col"), ("ev_w_out", "row"),
         ("od_w_in", "col"), ("od_w_out", "row"), ("xa_w_q", "row"), ("xa_w_kv", "col"), ("xa_w_o", "row"),
         ("ffn2_w_gu", "col"), ("ffn2_w_down", "row")]
_VECS = ["ffn1_norm", "mix_norm", "ev_q_gain", "ev_k_gain", "ev_sinks", "od_q_gain", "od_k_gain", "xa_norm",
         "xa_mem_norm", "xa_q_gain", "xa_k_gain", "ffn2_norm"]
_WEIGHTS = ["ffn1_norm", "ffn1_w_gu", "ffn1_w_down", "mix_norm", "ev_w_in", "ev_q_gain", "ev_k_gain", "ev_sinks",
            "ev_w_out", "od_w_in", "od_q_gain", "od_k_gain", "od_w_out", "xa_norm", "xa_mem_norm", "xa_w_q", "xa_w_kv",
            "xa_q_gain", "xa_k_gain", "xa_w_o", "ffn2_norm", "ffn2_w_gu", "ffn2_w_down"]


_AXIS = dict(_MATS)
DEPTH = 2


def _layer_groups(l):
    first, rest = _first_block_groups(l)
    return [first[0] + rest[0] + rest[1]]


def _first_block_groups(l):
    w_in, w_out = ("ev_w_in", "ev_w_out") if l % 2 == 0 else ("od_w_in", "od_w_out")
    first = [[("ffn1_w_gu", l), ("ffn1_w_down", l)]]
    rest = [[("ffn2_w_gu", l), ("xa_w_kv", l)],
            [(w_in, l // 2), ("ffn2_w_down", l), (w_out, l // 2), ("xa_w_q", l), ("xa_w_o", l)]]
    return first, rest


def _block_rows(shards, n):
    a, b = shards[n].shape[1:]
    return a if _AXIS[n] == "row" else b


def _weight_blocks(shards, groups):
    blocks = []
    for group in groups:
        rows = [(shards[n][j] if _AXIS[n] == "row" else shards[n][j].T).astype(BF16) for n, j in group]
        blocks.append(rows[0] if len(rows) == 1 else jnp.concatenate(rows, axis=0))
    return blocks


def _whole_weights(shards, groups, gathered):
    full = {}
    for group, got in zip(groups, gathered):
        off = 0
        for n, j in group:
            r = _block_rows(shards, n)
            full[n] = got[:, off:off + r, :].reshape(N_DEV * r, got.shape[2])
            off += r
    return full


def _gradient_buffers(grads, groups):
    bufs = []
    for group in groups:
        rows = []
        for n, _ in group:
            whole = jnp.concatenate(grads[n], axis=0) if isinstance(grads[n], tuple) else grads[n]
            rows.append(whole.reshape(N_DEV, whole.shape[0] // N_DEV, whole.shape[1]))
        bufs.append((rows[0] if len(rows) == 1 else jnp.concatenate(rows, axis=1)).astype(BF16))
    return bufs


def _gradient_blocks(shards, groups, sums):
    out = {}
    for group, tot in zip(groups, sums):
        off = 0
        for n, j in group:
            r = _block_rows(shards, n)
            out[n, j] = tot[off:off + r] if _AXIS[n] == "row" else tot[off:off + r].T
            off += r
    return out


class _PairChain:
    def __init__(self, ex, bufs):
        self.ex, self.bufs, self.parts = ex, bufs, None

    def side(self, name):
        return pair_side(self.bufs) if name == "da" else None

    def done(self, name, carried):
        self.parts = self.ex.pair_sums(self.bufs, carried, "l1")


class _RestChain:
    HALF = {"da": (0,), "dh": (1,)}

    def __init__(self, ex, bufs):
        self.ex, self.bufs, self.parts, self.sums = ex, bufs, None, [None] * len(bufs)

    def side(self, name):
        if name == "pair":
            return pair_side(self.bufs)
        if name in self.HALF:
            return chip_side([self.parts[i] for i in self.HALF[name]])
        return None

    def done(self, name, carried):
        if name == "pair":
            self.parts = self.ex.pair_sums(self.bufs, carried, "l0r")
        else:
            for i, tot in zip(self.HALF[name], self.ex.chip_sums(carried, "l0r_" + name)):
                self.sums[i] = tot


class _Exchange:
    def __init__(self, shards, c):
        self.shards, self.c = shards, c

    def weights_first(self):
        first, _ = _first_block_groups(0)
        return _whole_weights(self.shards, first, all_gather_blocks(_weight_blocks(self.shards, first)))

    def rest_blocks(self):
        return _weight_blocks(self.shards, _first_block_groups(0)[1])

    def weights_rest(self, gathered):
        return _whole_weights(self.shards, _first_block_groups(0)[1], gathered)

    def gather_start(self):
        return gather_side(0, _weight_blocks(self.shards, _layer_groups(1)))

    def weights_next(self, gathered):
        return _whole_weights(self.shards, _layer_groups(1), gathered)

    def chain_next(self, grads):
        return _PairChain(self, _gradient_buffers(grads, _layer_groups(1)))

    def chain_rest(self, grads):
        return _RestChain(self, _gradient_buffers(grads, _first_block_groups(0)[1]))

    def pair_sums(self, bufs, got, tag):
        return [pair_sum(b, g, self.c, b.dtype, f"grads_pair_sum_{tag}_{i}") for i, (b, g) in enumerate(zip(bufs, got))]

    def chip_sums(self, parts, tag):
        return [chip_sum(p, f"grads_chip_sum_{tag}_{i}") for i, p in enumerate(parts)]

    def finish(self, gm, gv, sums1, sums_rest):
        vecs = {n: jnp.concatenate(v, axis=0) for n, v in gv.items()}
        first, rest = _first_block_groups(0)
        bufs = _gradient_buffers(gm[0], first)
        vec = jnp.concatenate([vecs[n].reshape(-1) for n in _VECS])
        vec = jnp.pad(vec, (0, -vec.shape[0] % (16 * LANES)))
        bufs.append(jnp.broadcast_to(vec.reshape(1, -1, LANES), (N_DEV, vec.shape[0] // LANES, LANES)))
        parts = self.pair_sums(bufs, pair_exchange(bufs), "l0")
        sums0 = self.chip_sums(chip_exchange(parts), "l0")
        blocks = {**_gradient_blocks(self.shards, first, sums0[:-1]), **_gradient_blocks(self.shards, rest, sums_rest),
                  **_gradient_blocks(self.shards, _layer_groups(1), sums1)}
        out = {n: jnp.stack([blocks[n, j] for j in range(self.shards[n].shape[0])]) for n, _ in _MATS}
        flat, off = sums0[-1].reshape(-1), 0
        for n in _VECS:
            out[n] = flat[off:off + vecs[n].size].reshape(vecs[n].shape)
            off += vecs[n].size
        return out


class _NoExchange:
    def __init__(self, full):
        self.full = full

    def weights_first(self):
        return self.full[0]

    def rest_blocks(self):
        return None

    def gather_start(self):
        return None

    def weights_next(self, gathered):
        return self.full[1]

    def chain_next(self, grads):
        return None

    def chain_rest(self, grads):
        return None

    def finish(self, gm, gv, sums1, sums_rest):
        mats = {}
        for l in range(DEPTH):
            for group in _layer_groups(l):
                for n, j in group:
                    whole = jnp.concatenate(gm[l][n], axis=0) if isinstance(gm[l][n], tuple) else gm[l][n]
                    mats.setdefault(n, {})[j] = whole if _AXIS[n] == "row" else whole.T
        mats = {n: jnp.stack([v[j] for j in sorted(v)]) for n, v in mats.items()}
        return mats, {n: jnp.concatenate(v, axis=0) for n, v in gv.items()}


def _local_step(x, mem, target, w, ex):
    assert w["ffn1_norm"].shape[0] == DEPTH
    row = lambda a, l: a[l:l + 1]
    full = [ex.weights_first(), None]
    saved = []
    for l in range(DEPTH):
        t, j, f = f"l{l}", l // 2, full[l]
        rest = ex.rest_blocks() if l == 0 else None
        if rest is None:
            x, s1 = ffn_fwd(x, row(w["ffn1_norm"], l), f["ffn1_w_gu"], f["ffn1_w_down"], t + "_ffn1")
        else:
            x, s1, rest = ffn_fwd(x, row(w["ffn1_norm"], l), f["ffn1_w_gu"], f["ffn1_w_down"], t + "_ffn1", (0, rest))
        relay = None
        if l % 2 == 0:
            h = rmsnorm_fwd(x, row(w["mix_norm"], l), t + "_ev_norm", None if rest is None else gather_side(2, rest))
            if rest is not None:
                h, rest = h
                f = full[l] = {**f, **ex.weights_rest(rest)}
            side = ex.gather_start() if l + 1 < DEPTH else None
            x, s2, relay = even_mixer_fwd(x, h, _ev_reorder(f["ev_w_in"]), row(w["ev_q_gain"], j),
                                          row(w["ev_k_gain"], j), row(w["ev_sinks"], j), f["ev_w_out"], t + "_ev", side)
        else:
            x, s2 = odd_mixer_fwd(x, row(w["mix_norm"], l), f["od_w_in"], row(w["od_q_gain"], j),
                                  row(w["od_k_gain"], j), f["od_w_out"], t + "_od")
        x, s3 = xa_fwd(x, mem, row(w["xa_norm"], l), row(w["xa_mem_norm"], l), f["xa_w_q"], f["xa_w_kv"],
                       row(w["xa_q_gain"], l), row(w["xa_k_gain"], l), f["xa_w_o"], t + "_xa")
        if relay is None:
            x, s4 = ffn_fwd(x, row(w["ffn2_norm"], l), f["ffn2_w_gu"], f["ffn2_w_down"], t + "_ffn2")
        else:
            x, s4, relay = ffn_fwd(x, row(w["ffn2_norm"], l), f["ffn2_w_gu"], f["ffn2_w_down"], t + "_ffn2", (1, relay))
        if l + 1 < DEPTH:
            full[l + 1] = ex.weights_next(relay)
        saved.append((s1, s2, s3, s4))
    dx, sq = loss_head(x, target, "loss_head")
    loss = 0.5 * jnp.sum(sq) / x.shape[1]

    gm = [dict() for _ in range(DEPTH)]
    gv = {n: [None] * w[n].shape[0] for n in _VECS}
    chain1 = chain0 = sums1 = None
    started = []
    for l in reversed(range(DEPTH)):
        t, j, f = f"l{l}", l // 2, full[l]
        s1, s2, s3, s4 = saved[l]
        dx, gv["ffn2_norm"][l], gm[l]["ffn2_w_gu"], gm[l]["ffn2_w_down"] = ffn_bwd(
            dx, s4, row(w["ffn2_norm"], l), f["ffn2_w_gu"], f["ffn2_w_down"], t + "_ffn2", chain1 if l == 0 else None)
        parts = chain1.parts if l == 0 and chain1 is not None else None
        (dx, gv["xa_norm"][l], gv["xa_mem_norm"][l], gm[l]["xa_w_q"], gm[l]["xa_w_kv"], gv["xa_q_gain"][l],
         gv["xa_k_gain"][l], gm[l]["xa_w_o"]) = xa_bwd(
            dx, s3, mem, row(w["xa_norm"], l), row(w["xa_mem_norm"], l), f["xa_w_q"], f["xa_w_kv"],
            row(w["xa_q_gain"], l), row(w["xa_k_gain"], l), f["xa_w_o"], t + "_xa")
        if l % 2 == 0:
            def start_rest(d_win, d_wout, l=l):
                gm[l]["ev_w_in"], gm[l]["ev_w_out"] = _ev_restore(d_win), d_wout
                chain = ex.chain_rest(gm[l]) if l == 0 else None
                if chain is None:
                    return None
                started.append(chain)
                return chain.side("pair"), lambda got: chain.done("pair", got)

            (dx, gv["mix_norm"][l], d_win, gv["ev_q_gain"][j], gv["ev_k_gain"][j], gv["ev_sinks"][j],
             gm[l]["ev_w_out"], carried) = even_mixer_bwd(
                dx, s2, row(w["mix_norm"], l), _ev_reorder(f["ev_w_in"]), row(w["ev_q_gain"], j), row(w["ev_k_gain"], j),
                row(w["ev_sinks"], j), f["ev_w_out"], t + "_ev", None if parts is None else chip_side(parts), start_rest)
            gm[l]["ev_w_in"] = _ev_restore(d_win)
            if carried is not None:
                sums1 = ex.chip_sums(carried, "l1")
        else:
            (dx, gv["mix_norm"][l], gm[l]["od_w_in"], gv["od_q_gain"][j], gv["od_k_gain"][j],
             gm[l]["od_w_out"]) = odd_mixer_bwd(
                dx, s2, row(w["mix_norm"], l), f["od_w_in"], row(w["od_q_gain"], j), row(w["od_k_gain"], j),
                f["od_w_out"], t + "_od")
        if l == 0 and started:
            chain0 = started[0]
        dx, gv["ffn1_norm"][l], gm[l]["ffn1_w_gu"], gm[l]["ffn1_w_down"] = ffn_bwd(
            dx, s1, row(w["ffn1_norm"], l), f["ffn1_w_gu"], f["ffn1_w_down"], t + "_ffn1", chain0 if l == 0 else None)
        if l == 1:
            chain1 = ex.chain_next(gm[l])
    return loss, dx, ex.finish(gm, gv, sums1, None if chain0 is None else chain0.sums)


def kernel(x, mem, ffn1_norm, ffn1_w_gu, ffn1_w_down, mix_norm, ev_w_in, ev_q_gain, ev_k_gain, ev_sinks, ev_w_out, od_w_in, od_q_gain, od_k_gain, od_w_out, xa_norm, xa_mem_norm, xa_w_q, xa_w_kv, xa_q_gain, xa_k_gain, xa_w_o, ffn2_norm, ffn2_w_gu, ffn2_w_down, loss_target, m_ffn1_norm, m_ffn1_w_gu, m_ffn1_w_down, m_mix_norm, m_ev_w_in, m_ev_q_gain, m_ev_k_gain, m_ev_sinks, m_ev_w_out, m_od_w_in, m_od_q_gain, m_od_k_gain, m_od_w_out, m_xa_norm, m_xa_mem_norm, m_xa_w_q, m_xa_w_kv, m_xa_q_gain, m_xa_k_gain, m_xa_w_o, m_ffn2_norm, m_ffn2_w_gu, m_ffn2_w_down, v_ffn1_norm, v_ffn1_w_gu, v_ffn1_w_down, v_mix_norm, v_ev_w_in, v_ev_q_gain, v_ev_k_gain, v_ev_sinks, v_ev_w_out, v_od_w_in, v_od_q_gain, v_od_k_gain, v_od_w_out, v_xa_norm, v_xa_mem_norm, v_xa_w_q, v_xa_w_kv, v_xa_q_gain, v_xa_k_gain, v_xa_w_o, v_ffn2_norm, v_ffn2_w_gu, v_ffn2_w_down):
    w = dict(ffn1_norm=ffn1_norm, ffn1_w_gu=ffn1_w_gu, ffn1_w_down=ffn1_w_down, mix_norm=mix_norm, ev_w_in=ev_w_in, ev_q_gain=ev_q_gain, ev_k_gain=ev_k_gain, ev_sinks=ev_sinks, ev_w_out=ev_w_out, od_w_in=od_w_in, od_q_gain=od_q_gain, od_k_gain=od_k_gain, od_w_out=od_w_out, xa_norm=xa_norm, xa_mem_norm=xa_mem_norm, xa_w_q=xa_w_q, xa_w_kv=xa_w_kv, xa_q_gain=xa_q_gain, xa_k_gain=xa_k_gain, xa_w_o=xa_w_o, ffn2_norm=ffn2_norm, ffn2_w_gu=ffn2_w_gu, ffn2_w_down=ffn2_w_down)
    m = dict(ffn1_norm=m_ffn1_norm, ffn1_w_gu=m_ffn1_w_gu, ffn1_w_down=m_ffn1_w_down, mix_norm=m_mix_norm, ev_w_in=m_ev_w_in, ev_q_gain=m_ev_q_gain, ev_k_gain=m_ev_k_gain, ev_sinks=m_ev_sinks, ev_w_out=m_ev_w_out, od_w_in=m_od_w_in, od_q_gain=m_od_q_gain, od_k_gain=m_od_k_gain, od_w_out=m_od_w_out, xa_norm=m_xa_norm, xa_mem_norm=m_xa_mem_norm, xa_w_q=m_xa_w_q, xa_w_kv=m_xa_w_kv, xa_q_gain=m_xa_q_gain, xa_k_gain=m_xa_k_gain, xa_w_o=m_xa_w_o, ffn2_norm=m_ffn2_norm, ffn2_w_gu=m_ffn2_w_gu, ffn2_w_down=m_ffn2_w_down)
    v = dict(ffn1_norm=v_ffn1_norm, ffn1_w_gu=v_ffn1_w_gu, ffn1_w_down=v_ffn1_w_down, mix_norm=v_mix_norm, ev_w_in=v_ev_w_in, ev_q_gain=v_ev_q_gain, ev_k_gain=v_ev_k_gain, ev_sinks=v_ev_sinks, ev_w_out=v_ev_w_out, od_w_in=v_od_w_in, od_q_gain=v_od_q_gain, od_k_gain=v_od_k_gain, od_w_out=v_od_w_out, xa_norm=v_xa_norm, xa_mem_norm=v_xa_mem_norm, xa_w_q=v_xa_w_q, xa_w_kv=v_xa_w_kv, xa_q_gain=v_xa_q_gain, xa_k_gain=v_xa_k_gain, xa_w_o=v_xa_w_o, ffn2_norm=v_ffn2_norm, ffn2_w_gu=v_ffn2_w_gu, ffn2_w_down=v_ffn2_w_down)

    c = lax.axis_index("c").astype(jnp.int32).reshape(1)
    loss, dx, grads = _local_step(x[0], mem[0], loss_target[0], w, _Exchange(w, c))
    loss = lax.psum(loss, ("x", "y", "c"))

    delta, new_m, new_v = {}, {}, {}
    for n in _WEIGHTS:
        delta[n], new_m[n], new_v[n] = adamw(w[n], grads[n], m[n], v[n], "adamw_" + n)
    return (loss, dx[None], *[grads[n] for n in _WEIGHTS], *[delta[n] for n in _WEIGHTS],
            *[new_m[n] for n in _WEIGHTS], *[new_v[n] for n in _WEIGHTS])
```

```python
import functools

import numpy as np
import jax
import jax.numpy as jnp
from jax import lax
from jax.experimental import pallas as pl
from jax.experimental.pallas import tpu as pltpu

F32 = jnp.float32
BF16 = jnp.bfloat16
MESH = pl.DeviceIdType.MESH

HEAD_DIM = 64
BLOCK = 128
RMS_EPS = 1e-6
A_Q_HEADS, A_KV_HEADS = 8, 2
B_HEADS = 8
C_HEADS = 16
C_PATTERNS = ((128, 1), (512, 4), (2048, 16))
X_HEADS = 4
N_DEV = 8
LANES = 1024
VMEM_LIMIT_BYTES = 56 * 1024 * 1024
SB_SKIP_LOG = -110.0
NEG_BIG = -1e30

ADAM_LR, ADAM_B1, ADAM_B2, ADAM_EPS, ADAM_WD, ADAM_STEP = 0.001, 0.9, 0.999, 1e-08, 0.01, 10

NN = (((1,), (0,)), ((), ()))
NT = (((1,), (1,)), ((), ()))
TN = (((0,), (0,)), ((), ()))


class Side:
    def __init__(self, arrays, out_shapes, n_remote, n_local, plan, aliased=False):
        self.arrays, self.out_shapes, self.plan, self.aliased = list(arrays), list(out_shapes), plan, aliased
        self.sems = [pltpu.SemaphoreType.DMA((n_remote,)), pltpu.SemaphoreType.DMA((n_remote,)),
                     pltpu.SemaphoreType.DMA((max(n_local, 1),))]

    def start(self, ins, outs, sems):
        local, sends, _ = self.plan(ins, outs, *sems)
        for make in local + sends:
            make().start()

    def wait(self, ins, outs, sems):
        local, sends, recvs = self.plan(ins, outs, *sems)
        for make in sends:
            make().wait_send()
        for make in recvs:
            make().wait_recv()
        for make in local:
            make().wait()


def _pcall(body, side=None, **kw):
    if side is None:
        return pl.pallas_call(body, **kw)
    grid = kw["grid"]
    single = not isinstance(kw["out_specs"], (list, tuple))
    out_specs = [kw["out_specs"]] if single else list(kw["out_specs"])
    out_shape = [kw["out_shape"]] if single else list(kw["out_shape"])
    scratch = list(kw.get("scratch_shapes", []))
    n_in, n_out, n_scr, n_side = len(kw["in_specs"]), len(out_specs), len(scratch), len(side.arrays)
    n_sout = len(side.out_shapes)

    def hosted(*refs):
        ins, s_in = refs[:n_in], refs[n_in:n_in + n_side]
        outs = refs[n_in + n_side:n_in + n_side + n_out]
        s_out = refs[n_in + n_side + n_out:n_in + n_side + n_out + n_sout]
        rest = refs[n_in + n_side + n_out + n_sout:]
        scr, sems = rest[:n_scr], rest[n_scr:]
        first = last = None
        for a, size in enumerate(grid):
            f, l = pl.program_id(a) == 0, pl.program_id(a) == size - 1
            first = f if first is None else jnp.logical_and(first, f)
            last = l if last is None else jnp.logical_and(last, l)

        @pl.when(first)
        def _():
            side.start(s_in, s_out, sems)

        body(*ins, *outs, *scr)

        @pl.when(last)
        def _():
            side.wait(s_in, s_out, sems)

    any_space = pl.BlockSpec(memory_space=pl.ANY)
    kw2 = dict(kw)
    kw2.update(in_specs=list(kw["in_specs"]) + [any_space] * n_side, out_specs=out_specs + [any_space] * n_sout,
               out_shape=out_shape + side.out_shapes, scratch_shapes=scratch + side.sems)
    if side.aliased:
        kw2["input_output_aliases"] = {n_in + i: n_out + i for i in range(n_side)}
    call = pl.pallas_call(hosted, **kw2)

    def run(*args):
        res = call(*args, *side.arrays)
        return (res[0] if single else list(res[:n_out])), list(res[n_out:])

    return run


def _params(**kw):
    return pltpu.CompilerParams(vmem_limit_bytes=VMEM_LIMIT_BYTES, **kw)


def _tile(dim, cap, unit=128):
    if dim <= cap:
        return dim
    t = (cap // unit) * unit
    while t >= unit:
        if dim % t == 0:
            return t
        t -= unit
    raise ValueError(f"no tile for {dim} under {cap}")


def _dot(a, b, dims):
    return lax.dot_general(a.astype(BF16), b.astype(BF16), dims, preferred_element_type=F32)


@functools.partial(jax.custom_vjp, nondiff_argnums=(2,))
def _dot_vjp(a, b, nt):
    return _dot(a, b, NT if nt else NN)


def _dot_vjp_fwd(a, b, nt):
    return _dot(a, b, NT if nt else NN), (a.astype(BF16), b.astype(BF16))


def _dot_vjp_bwd(nt, res, g):
    a, b = res
    if nt:
        return _dot(g, b, NN), _dot(g, a, TN)
    return _dot(g, b, NT), _dot(a, g, TN)


_dot_vjp.defvjp(_dot_vjp_fwd, _dot_vjp_bwd)


def _plain_dot(a, b, nt):
    return _dot(a, b, NT if nt else NN)


def _split_dot(x, mat, terms=2):
    out, rem = None, x
    for t in range(terms):
        part = rem.astype(BF16)
        d = lax.dot_general(part, mat, NN, preferred_element_type=F32)
        out = d if out is None else out + d
        if t + 1 < terms:
            rem = rem - part.astype(F32)
    return out


@functools.partial(jax.custom_vjp, nondiff_argnums=(3,))
def _split_dot_vjp(x, mat, mat_t, terms):
    return _split_dot(x, mat, terms)


def _split_dot_vjp_fwd(x, mat, mat_t, terms):
    return _split_dot(x, mat, terms), mat_t


def _split_dot_vjp_bwd(terms, mat_t, g):
    return _split_dot(g, mat_t, terms), None, None


_split_dot_vjp.defvjp(_split_dot_vjp_fwd, _split_dot_vjp_bwd)


def _plain_split(x, mat, mat_t, terms):
    return _split_dot(x, mat, terms)


def _tri(after):
    j = lax.broadcasted_iota(jnp.int32, (BLOCK, BLOCK), 0)
    s = lax.broadcasted_iota(jnp.int32, (BLOCK, BLOCK), 1)
    return jnp.where(j > s if after else j < s, 1.0, 0.0).astype(BF16)


def _in(a, block, imap):
    return (a, block, imap)


def _out(shape, dtype, block, imap, acc=False):
    return (shape, dtype, block, imap, acc)


def tcall(fn, grid, ins, outs, name, scratch=None, side=None):
    nin = len(ins)
    nout = len(outs)
    ngrid = len(grid)

    def body(*refs):
        ids = tuple(pl.program_id(a) for a in range(ngrid))
        extra = {} if scratch is None else {"scratch": refs[nin + nout]}
        res = fn(ids, *[r[...] for r in refs[:nin]], **extra)
        first = ids[0] == 0
        for a in range(1, ngrid):
            first = jnp.logical_and(first, ids[a] == 0)
        for o_ref, r, spec in zip(refs[nin:nin + nout], res, outs):
            if spec[4]:
                @pl.when(first)
                def _(o_ref=o_ref):
                    o_ref[...] = jnp.zeros(o_ref.shape, o_ref.dtype)
                o_ref[...] += r.astype(o_ref.dtype)
            else:
                o_ref[...] = r.astype(o_ref.dtype)

    return _pcall(
        body, side=side, name=name, grid=grid,
        in_specs=[pl.BlockSpec(b, m) for (_, b, m) in ins],
        out_specs=[pl.BlockSpec(b, m) for (_, _, b, m, _) in outs],
        out_shape=[jax.ShapeDtypeStruct(s, d) for (s, d, _, _, _) in outs],
        scratch_shapes=[] if scratch is None else [pltpu.VMEM(*scratch)],
        compiler_params=_params(),
    )(*[a for (a, _, _) in ins])


def _to_strided(scr, nat, d):
    if d == 1:
        return nat
    t, w = nat.shape
    nc = w // BLOCK
    for c in range(nc):
        scr[c * t:(c + 1) * t, :] = nat[:, c * BLOCK:(c + 1) * BLOCK]
    return jnp.concatenate([scr[pl.ds(c * t + r, t // d, stride=d), :] for r in range(d) for c in range(nc)], axis=1)


def _to_natural(scr, st, d):
    if d == 1:
        return st.astype(F32)
    t, w = st.shape[0] * d, st.shape[1] // d
    nc = w // BLOCK
    st = st.astype(F32)
    for r in range(d):
        for c in range(nc):
            scr[pl.ds(c * t + r, t // d, stride=d), :] = st[:, r * w + c * BLOCK:r * w + (c + 1) * BLOCK]
    return jnp.concatenate([scr[c * t:(c + 1) * t, :] for c in range(nc)], axis=1)


def _row(a, tm, width=None, cb=0):
    width = a.shape[1] if width is None else width
    return _in(a, (tm, width), lambda i, cb=cb: (i, cb))


def _full(a):
    zeros = (0,) * a.ndim
    return _in(a, a.shape, lambda *ids: zeros)


def _row_out(n, width, dtype, tm):
    return _out((n, width), dtype, (tm, width), lambda i: (i, 0))


def _acc_out(shape):
    zeros = (0,) * len(shape)
    return _out(shape, F32, shape, lambda *ids: zeros, acc=True)


def mm(a, b, mode, name, *, out_dtype=None, scale=1.0, res=None, side=None):
    if out_dtype is None:
        out_dtype = BF16 if mode == "tn" else F32
    if mode == "nn":
        (m, k), (k2, n) = a.shape, b.shape
    elif mode == "nt":
        (m, k), (n, k2) = a.shape, b.shape
    else:
        (k, m), (k2, n) = a.shape, b.shape
    assert k == k2, (a.shape, b.shape, mode)
    tm, tn, tk = _tile(m, 1408 if mode == "tn" else 1024), _tile(n, 1408), _tile(k, 1408)
    nk = k // tk
    dims = {"nn": NN, "nt": NT, "tn": TN}[mode]
    has_res = res is not None

    def body(*refs):
        if has_res:
            a_ref, b_ref, r_ref, o_ref, acc_ref = refs
        else:
            a_ref, b_ref, o_ref, acc_ref = refs
        kk = pl.program_id(2)

        @pl.when(kk == 0)
        def _():
            acc_ref[...] = jnp.zeros(acc_ref.shape, F32)

        acc_ref[...] += _dot(a_ref[...], b_ref[...], dims)

        @pl.when(kk == nk - 1)
        def _():
            out = acc_ref[...]
            if scale != 1.0:
                out = out * scale
            if has_res:
                out = out + r_ref[...]
            o_ref[...] = out.astype(o_ref.dtype)

    a_spec = (pl.BlockSpec((tk, tm), lambda i, j, kk: (kk, i)) if mode == "tn"
              else pl.BlockSpec((tm, tk), lambda i, j, kk: (i, kk)))
    b_spec = (pl.BlockSpec((tn, tk), lambda i, j, kk: (j, kk)) if mode == "nt"
              else pl.BlockSpec((tk, tn), lambda i, j, kk: (kk, j)))
    in_specs = [a_spec, b_spec]
    args = [a, b]
    if has_res:
        in_specs.append(pl.BlockSpec((tm, tn), lambda i, j, kk: (i, j)))
        args.append(res)
    order = ("parallel", "parallel", "arbitrary") if side is None else ("arbitrary",) * 3
    return _pcall(
        body, side=side, name=name, grid=(m // tm, n // tn, nk),
        in_specs=in_specs,
        out_specs=pl.BlockSpec((tm, tn), lambda i, j, kk: (i, j)),
        out_shape=jax.ShapeDtypeStruct((m, n), out_dtype),
        scratch_shapes=[pltpu.VMEM((tm, tn), F32)],
        compiler_params=_params(dimension_semantics=order),
    )(*args)


def _rms(x, g):
    return x * lax.rsqrt(jnp.mean(x * x, axis=-1, keepdims=True) + RMS_EPS) * g


def _silu_mul(gate, up):
    return gate / (1.0 + jnp.exp(-gate)) * up


def mm_gate_up(h, w_gu, name, side=None):
    m, k = h.shape
    f = w_gu.shape[0] // 2
    tm, tn = _tile(m, 1024), _tile(f, 1408)
    nj = f // tn
    assert k <= 1408

    def body(h_ref, wg_ref, wu_ref, g_ref, u_ref, a_ref):
        ht = h_ref[...]
        for lo in range(0, tn, 512):
            cols = slice(lo, min(lo + 512, tn))
            gate, up = _dot(ht, wg_ref[cols, :], NT), _dot(ht, wu_ref[cols, :], NT)
            g_ref[:, cols] = gate.astype(g_ref.dtype)
            u_ref[:, cols] = up.astype(u_ref.dtype)
            a_ref[:, cols] = _silu_mul(gate, up).astype(a_ref.dtype)

    tile = pl.BlockSpec((tm, tn), lambda i, j: (i, j))
    return _pcall(
        body, side=side, name=name, grid=(m // tm, nj),
        in_specs=[pl.BlockSpec((tm, k), lambda i, j: (i, 0)),
                  pl.BlockSpec((tn, k), lambda i, j: (j, 0)),
                  pl.BlockSpec((tn, k), lambda i, j: (j + nj, 0))],
        out_specs=[tile, tile, tile],
        out_shape=[jax.ShapeDtypeStruct((m, f), BF16), jax.ShapeDtypeStruct((m, f), BF16),
                   jax.ShapeDtypeStruct((m, f), BF16)],
        compiler_params=_params(dimension_semantics=("arbitrary",) * 2),
    )(h, w_gu, w_gu)


def mm_down_act_bwd(dy, w_down, gate, up, name, side=None):
    m, d = dy.shape
    f = w_down.shape[0]
    tm, tn = _tile(m, 1024), _tile(f, 1408)
    assert d <= 1408

    def body(dy_ref, w_ref, g_ref, u_ref, dg_ref, du_ref):
        dyt = dy_ref[...].astype(BF16)
        for lo in range(0, tn, 512):
            cols = slice(lo, min(lo + 512, tn))
            da = _dot(dyt, w_ref[cols, :], NT) * 0.5
            gate, up = g_ref[:, cols].astype(F32), u_ref[:, cols].astype(F32)
            s = 1.0 / (1.0 + jnp.exp(-gate))
            gs = gate * s
            du_ref[:, cols] = (da * gs).astype(du_ref.dtype)
            dg_ref[:, cols] = (da * up * s * (1.0 + gate - gs)).astype(dg_ref.dtype)

    tile = pl.BlockSpec((tm, tn), lambda i, j: (i, j))
    return _pcall(
        body, side=side, name=name, grid=(m // tm, f // tn),
        in_specs=[pl.BlockSpec((tm, d), lambda i, j: (i, 0)), pl.BlockSpec((tn, d), lambda i, j: (j, 0)), tile, tile],
        out_specs=[tile, tile],
        out_shape=[jax.ShapeDtypeStruct((m, f), BF16), jax.ShapeDtypeStruct((m, f), BF16)],
        compiler_params=_params(dimension_semantics=("arbitrary", "arbitrary")),
    )(dy, w_down, gate, up)


def mm_norm_bwd(a, b, x, g, dres, name, b_kd=False, side=None):
    halves = isinstance(a, (tuple, list))
    a0, a1 = a if halves else (a, None)
    m, k = a0.shape[0], a0.shape[1] * (2 if halves else 1)
    d = b.shape[1] if b_kd else b.shape[0]
    dims = NN if b_kd else NT
    tm, tk = _tile(m, 512), _tile(a0.shape[1], 1408)
    nk = k // tk
    nkh = a0.shape[1] // tk
    has_res = dres is not None

    def body(*refs):
        a_ref, b_ref, x_ref, g_ref = refs[:4]
        rest = refs[4:-3]
        a1_ref = rest[0] if halves else None
        r_ref = rest[-1] if has_res else None
        dx_ref, dg_ref, acc_ref = refs[-3:]
        i, kk = pl.program_id(0), pl.program_id(1)

        @pl.when(kk == 0)
        def _():
            acc_ref[...] = jnp.zeros(acc_ref.shape, F32)

        if halves:
            @pl.when(kk < nkh)
            def _():
                acc_ref[...] += _dot(a_ref[...], b_ref[...], dims)

            @pl.when(kk >= nkh)
            def _():
                acc_ref[...] += _dot(a1_ref[...], b_ref[...], dims)
        else:
            acc_ref[...] += _dot(a_ref[...], b_ref[...], dims)

        @pl.when(kk == nk - 1)
        def _():
            _, vjp = jax.vjp(_rms, x_ref[...], g_ref[...])
            dx, dg = vjp(acc_ref[...])
            dx_ref[...] = dx + r_ref[...] if has_res else dx

            @pl.when(i == 0)
            def _():
                dg_ref[...] = jnp.zeros(dg_ref.shape, F32)

            dg_ref[...] += dg

    rows = pl.BlockSpec((tm, d), lambda i, kk: (i, 0))
    first = pl.BlockSpec((tm, tk), lambda i, kk: (i, jnp.minimum(kk, nkh - 1)))
    second = pl.BlockSpec((tm, tk), lambda i, kk: (i, jnp.maximum(kk - nkh, 0)))
    b_spec = pl.BlockSpec((tk, d), lambda i, kk: (kk, 0)) if b_kd else pl.BlockSpec((d, tk), lambda i, kk: (0, kk))
    in_specs = ([first, b_spec, rows, pl.BlockSpec(g.shape, lambda i, kk: (0, 0))]
                + ([second] if halves else []) + ([rows] if has_res else []))
    return _pcall(
        body, side=side, name=name, grid=(m // tm, nk),
        in_specs=in_specs,
        out_specs=[rows, pl.BlockSpec(g.shape, lambda i, kk: (0, 0))],
        out_shape=[jax.ShapeDtypeStruct((m, d), F32), jax.ShapeDtypeStruct(g.shape, F32)],
        scratch_shapes=[pltpu.VMEM((tm, d), F32)],
        compiler_params=_params(dimension_semantics=("arbitrary", "arbitrary")),
    )(*([a0, b, x, g] + ([a1] if halves else []) + ([dres] if has_res else [])))


def _indicator(shape, head_axis, mod):
    lane = lax.broadcasted_iota(jnp.int32, shape, head_axis)
    other = lax.broadcasted_iota(jnp.int32, shape, 1 - head_axis)
    lane = jnp.bitwise_and(lane, HEAD_DIM - 1) if mod else jnp.right_shift(lane, 6)
    return jnp.where(lane == other, 1.0, 0.0).astype(BF16)


def _head_rms(split, xs, g):
    w = xs.shape[1]
    to_head, from_head = _indicator((w, BLOCK), 0, False), _indicator((BLOCK, w), 1, False)
    to_lane, from_lane = _indicator((HEAD_DIM, w), 1, True), _indicator((w, HEAD_DIM), 0, True)
    ss = split(xs * xs, to_head, from_head, 3)
    r = lax.rsqrt(ss * (1.0 / HEAD_DIM) + RMS_EPS)
    g_all = split(jnp.broadcast_to(g, (8, HEAD_DIM)), to_lane, from_lane, 3)[0:1]
    return xs * split(r, from_head, to_head, 3) * g_all


def _prep(split, x, qg, kg, segs):
    parts = []
    for start, width, kind in segs:
        xs = x[:, start:start + width]
        parts.append(xs if kind == "raw" else _head_rms(split, xs, qg if kind == "q" else kg))
    return jnp.concatenate(parts, axis=1)


def prep_fwd(x, qg, kg, segs, dils, name):
    n, w = x.shape
    tm = _tile(n, 256, 8)

    def fn(ids, xt, a, b, scratch):
        ops = _prep(_plain_split, xt, a, b, segs)
        return tuple(_to_strided(scratch, ops, d) for d in dils)

    return tcall(fn, (n // tm,), [_row(x, tm), _full(qg), _full(kg)],
                 [_out((n // d, d * w), BF16, (tm // d, d * w), lambda i: (i, 0)) for d in dils], name,
                 scratch=((w // BLOCK * tm, BLOCK), F32))


def prep_bwd(x, qg, kg, segs, grads, gather, name):
    n, w = x.shape
    tm = BLOCK
    nblk = n // tm
    nslot = 1 + max(slot for _, _, _, slot in grads)

    def fn(ids, xt, a, b, *t, scratch):
        tiles, dils = [None] * nslot, [None] * nslot
        for ti, (_, sh, d, slot) in zip(t, grads):
            ti = jnp.where(ids[0] + sh < nblk, ti, 0.0) if sh else ti
            tiles[slot] = ti if tiles[slot] is None else tiles[slot] + ti
            dils[slot] = d
        tiles = [_to_natural(scratch, ti, d) for ti, d in zip(tiles, dils)]
        _, vjp = jax.vjp(lambda x_, a_, b_: _prep(_split_dot_vjp, x_, a_, b_, segs), xt, a, b)
        return vjp(gather(*tiles))

    specs = [_in(a, (tm // d, a.shape[1]), (lambda i, sh=sh: (jnp.minimum(i + sh, nblk - 1), 0)))
             for a, sh, d, _ in grads]
    wmax = max(a.shape[1] // d for a, _, d, _ in grads)
    return tcall(fn, (nblk,), [_row(x, tm), _full(qg), _full(kg)] + specs,
                 [_row_out(n, w, BF16, tm), _acc_out(qg.shape), _acc_out(kg.shape)], name,
                 scratch=((wmax // BLOCK * tm, BLOCK), F32))


def rmsnorm_fwd(x, g, name, side=None):
    n, d = x.shape
    tm = _tile(n, 512, 8)
    res = tcall(lambda ids, xt, gt: (_rms(xt, gt),), (n // tm,), [_row(x, tm), _full(g)],
                [_row_out(n, d, BF16, tm)], name, side=side)
    if side is None:
        return res[0]
    return res[0][0], res[1]


def ffn_fwd(x, g, w_gu, w_down, tag, carry=None):
    h = rmsnorm_fwd(x, g, tag + "_norm")
    if carry is None:
        gate, up, a = mm_gate_up(h, w_gu, tag + "_gu")
        return mm(a, w_down, "nn", tag + "_down", scale=0.5, res=x), (x, h, gate, up, a)
    phase, bufs = carry
    (gate, up, a), bufs = mm_gate_up(h, w_gu, tag + "_gu", side=gather_side(phase, bufs))
    y, bufs = mm(a, w_down, "nn", tag + "_down", scale=0.5, res=x, side=gather_side(phase + 1, bufs))
    return y, (x, h, gate, up, a), bufs


def ffn_bwd(dy, saved, g, w_gu, w_down, tag, chain=None):
    x, h, gate, up, a = saved

    def carrying(name, call, **kw):
        side = None if chain is None else chain.side(name)
        out = call(name=tag + "_" + name, side=side, **kw)
        if side is None:
            return out
        chain.done(name, out[1])
        return out[0]

    dgate, dup = carrying("da", mm_down_act_bwd, dy=dy, w_down=w_down, gate=gate, up=up)
    d_wdown = carrying("dwd", mm, a=a, b=dy, mode="tn", scale=0.5)
    d_wgu = (carrying("dwgu", mm, a=dgate, b=h, mode="tn"), mm(dup, h, "tn", tag + "_dwup"))
    dx, dg = carrying("dh", mm_norm_bwd, a=(dgate, dup), b=w_gu, x=x, g=g, dres=dy, b_kd=True)
    return dx, dg, d_wgu, d_wdown


def _alibi(n_heads):
    return [float(s) for s in np.asarray(2.0 ** (-8.0 * np.arange(1, n_heads + 1) / n_heads), dtype=np.float32)]


def _banded_tile(dot, first, q, kp, kc, vp, vc, sinks, *, hkv, grp, max_dist, step, slopes, want_lse):
    row = lax.broadcasted_iota(jnp.int32, (BLOCK, 2 * BLOCK), 0)
    col = lax.broadcasted_iota(jnp.int32, (BLOCK, 2 * BLOCK), 1)
    dist = row + BLOCK - col
    valid = (dist >= 0) & (dist <= max_dist) & ((col >= BLOCK) | jnp.logical_not(first))
    distf = dist.astype(F32)

    def head(hd, qh, k2, v2):
        s = dot(qh, k2, True) * (HEAD_DIM ** -0.5)
        s = jnp.where(valid, s - (slopes[hd] * step) * distf, NEG_BIG)
        m = jnp.max(s, axis=-1, keepdims=True)
        if sinks is not None:
            pick = lax.broadcasted_iota(jnp.int32, sinks.shape, 1) == hd
            sk = jnp.sum(jnp.where(pick, sinks, 0.0), axis=1, keepdims=True)
            m = jnp.maximum(m, sk)
        m = lax.stop_gradient(m)
        p = jnp.exp(s - m)
        denom = jnp.sum(p, axis=-1, keepdims=True)
        if sinks is not None:
            denom = denom + jnp.exp(sk - m)
        return dot(p * (1.0 / denom), v2, False), m + jnp.log(denom)

    outs, lses = [], []
    if grp == 1:
        low = lax.broadcasted_iota(jnp.int32, (BLOCK, BLOCK), 1) < HEAD_DIM
        for pr in range(hkv // 2):
            sl = slice(pr * BLOCK, (pr + 1) * BLOCK)
            q2 = q[:, sl]
            k2 = jnp.concatenate([kp[:, sl], kc[:, sl]], axis=0)
            v2 = jnp.concatenate([vp[:, sl], vc[:, sl]], axis=0)
            o0, l0 = head(2 * pr, jnp.where(low, q2, 0.0), k2, v2)
            o1, l1 = head(2 * pr + 1, jnp.where(low, 0.0, q2), k2, v2)
            outs.append(jnp.where(low, o0, o1))
            lses.append(jnp.where(low, l0, l1))
    else:
        for hk in range(hkv):
            sl = slice(hk * HEAD_DIM, (hk + 1) * HEAD_DIM)
            k2 = jnp.concatenate([kp[:, sl], kc[:, sl]], axis=0)
            v2 = jnp.concatenate([vp[:, sl], vc[:, sl]], axis=0)
            for gi in range(grp):
                hd = hk * grp + gi
                o_h, l_h = head(hd, q[:, hd * HEAD_DIM:(hd + 1) * HEAD_DIM], k2, v2)
                outs.append(o_h)
                lses.append(jnp.broadcast_to(l_h, (BLOCK, HEAD_DIM)))
    o = jnp.concatenate(outs, axis=1)
    if want_lse:
        return o, jnp.concatenate(lses, axis=1)
    return (o,)


def _banded_specs(view, qcol, kcol, vcol, wq, wkv):
    def at(colfn, prev):
        if prev:
            return lambda r, n: (jnp.maximum(n - 1, 0), colfn(r))
        return lambda r, n: (n, colfn(r))
    return [
        _in(view, (BLOCK, wq), at(qcol, False)),
        _in(view, (BLOCK, wkv), at(kcol, True)),
        _in(view, (BLOCK, wkv), at(kcol, False)),
        _in(view, (BLOCK, wkv), at(vcol, True)),
        _in(view, (BLOCK, wkv), at(vcol, False)),
    ]


def banded_fwd(view, dil, cols, sinks, cfg, name):
    ns = view.shape[0]
    nb = ns // BLOCK
    wq, wkv = cfg["hkv"] * cfg["grp"] * HEAD_DIM, cfg["hkv"] * HEAD_DIM
    has_sinks = sinks is not None

    def fn(ids, q, kp, kc, vp, vc, *rest):
        q, kp, kc, vp, vc = [a.astype(F32) for a in (q, kp, kc, vp, vc)]
        return _banded_tile(_plain_dot, ids[1] == 0, q, kp, kc, vp, vc, rest[0] if has_sinks else None, **cfg)

    ins = _banded_specs(view, *cols, wq, wkv) + ([_full(sinks)] if has_sinks else [])
    outs = [_out((ns, dil * wq), F32 if cfg["want_lse"] else BF16, (BLOCK, wq), lambda r, n: (n, r))]
    if cfg["want_lse"]:
        outs.append(_out((ns, dil * wq), F32, (BLOCK, wq), lambda r, n: (n, r)))
    return tcall(fn, (dil, nb), ins, outs, name)


def banded_bwd(view, dil, cols, sinks, cfg, cts, name):
    ns = view.shape[0]
    nb = ns // BLOCK
    wq, wkv = cfg["hkv"] * cfg["grp"] * HEAD_DIM, cfg["hkv"] * HEAD_DIM
    has_sinks = sinks is not None
    assert len(cts) == (2 if cfg["want_lse"] else 1)

    def fn(ids, q, kp, kc, vp, vc, *rest):
        sk = rest[0] if has_sinks else None
        ct = rest[1 if has_sinks else 0:]
        first = ids[1] == 0

        def f(q, kp, kc, vp, vc, *s):
            return _banded_tile(_dot_vjp, first, q, kp, kc, vp, vc, s[0] if has_sinks else None, **cfg)

        prim = tuple(a.astype(F32) for a in (q, kp, kc, vp, vc)) + ((sk,) if has_sinks else ())
        _, vjp = jax.vjp(f, *prim)
        return vjp(tuple(c.astype(F32) for c in ct))

    ins = (_banded_specs(view, *cols, wq, wkv) + ([_full(sinks)] if has_sinks else [])
           + [_in(a, (BLOCK, wq), (lambda r, n, cf=cf: (n, cf(r)))) for (a, cf) in cts])
    blk = lambda w: _out((ns, dil * w), F32, (BLOCK, w), lambda r, n: (n, r))
    outs = [blk(wq), blk(wkv), blk(wkv), blk(wkv), blk(wkv)]
    if has_sinks:
        outs.append(_acc_out(sinks.shape))
    return tcall(fn, (dil, nb), ins, outs, name)


def _log_sigmoid(z):
    return jnp.minimum(z, 0.0) - jnp.log(1.0 + jnp.exp(-jnp.abs(z)))


SB_PAIRS = 4


def _sb_pair(dot, suffix, qh, kb, vb, r_in, mask):
    z = dot(qh, kb, True) * (HEAD_DIM ** -0.5)
    lsp = _log_sigmoid(z)
    log_keep = jnp.where(mask, lsp - z, 0.0)
    log_after = suffix(log_keep) + r_in
    a = jnp.where(mask, jnp.exp(lsp + log_after), 0.0)
    return dot(a, vb, False), r_in + jnp.sum(log_keep, axis=1, keepdims=True)


def sb_fwd(qkv, qcb, kcb, vcb, name, side=None):
    s = qkv.shape[0]
    nb = s // BLOCK
    pairs = B_HEADS // 2
    wide = SB_PAIRS * BLOCK
    assert pairs % SB_PAIRS == 0 and qcb % SB_PAIRS == 0 and kcb % SB_PAIRS == 0 and vcb % SB_PAIRS == 0

    def body(q_ref, k_ref, v_ref, o_ref):
        n = pl.program_id(1)
        low = lax.broadcasted_iota(jnp.int32, (BLOCK, BLOCK), 1) < HEAD_DIM
        before = (lax.broadcasted_iota(jnp.int32, (2 * BLOCK, BLOCK), 1)
                  < jnp.bitwise_and(lax.broadcasted_iota(jnp.int32, (2 * BLOCK, BLOCK), 0), BLOCK - 1))
        after = _tri(True)
        suffix = lambda t: _split_dot(t, after)
        qs = []
        for p in range(SB_PAIRS):
            q2 = q_ref[:, p * BLOCK:(p + 1) * BLOCK].astype(F32)
            qs.append(jnp.concatenate([jnp.where(low, q2, 0.0), jnp.where(low, 0.0, q2)], axis=0))

        def cond(c):
            return jnp.logical_and(c[0] >= 0, c[1] > SB_SKIP_LOG)

        def step(c):
            kb, _, rs, accs = c
            rows = pl.ds(pl.multiple_of(kb * BLOCK, BLOCK), BLOCK)
            mask = jnp.logical_or(before, kb != n)
            new_r, new_acc, top = [], [], None
            for p in range(SB_PAIRS):
                cols = slice(p * BLOCK, (p + 1) * BLOCK)
                o_part, r_out = _sb_pair(_plain_dot, suffix, qs[p], k_ref[rows, cols], v_ref[rows, cols], rs[p], mask)
                new_r.append(r_out)
                new_acc.append(accs[p] + o_part)
                top = jnp.max(r_out) if top is None else jnp.maximum(top, jnp.max(r_out))
            return kb - 1, top, tuple(new_r), tuple(new_acc)

        init = (n, jnp.float32(0.0), tuple(jnp.zeros((2 * BLOCK, 1), F32) for _ in range(SB_PAIRS)),
                tuple(jnp.zeros((2 * BLOCK, BLOCK), F32) for _ in range(SB_PAIRS)))
        accs = lax.while_loop(cond, step, init)[3]
        for p in range(SB_PAIRS):
            o_ref[:, p * BLOCK:(p + 1) * BLOCK] = jnp.where(low, accs[p][:BLOCK], accs[p][BLOCK:]).astype(o_ref.dtype)

    return _pcall(
        body, side=side, name=name, grid=(pairs // SB_PAIRS, nb),
        in_specs=[pl.BlockSpec((BLOCK, wide), lambda g, n: (n, qcb // SB_PAIRS + g)),
                  pl.BlockSpec((s, wide), lambda g, n: (0, kcb // SB_PAIRS + g), pipeline_mode=pl.Buffered(1)),
                  pl.BlockSpec((s, wide), lambda g, n: (0, vcb // SB_PAIRS + g), pipeline_mode=pl.Buffered(1))],
        out_specs=pl.BlockSpec((BLOCK, wide), lambda g, n: (n, g)),
        out_shape=jax.ShapeDtypeStruct((s, pairs * BLOCK), BF16),
        compiler_params=_params(),
    )(qkv, qkv, qkv)


def sb_bwd(qkv, qcb, kcb, vcb, do, docb, name, side=None):
    s = qkv.shape[0]
    nb = s // BLOCK
    pairs = B_HEADS // 2
    wide = SB_PAIRS * BLOCK
    assert docb % SB_PAIRS == 0

    def body(q_ref, k_ref, v_ref, do_ref, dq_ref, dk_ref, dv_ref, r_ref):
        n = pl.program_id(1)

        @pl.when(n == 0)
        def _():
            dk_ref[...] = jnp.zeros(dk_ref.shape, F32)
            dv_ref[...] = jnp.zeros(dv_ref.shape, F32)

        low = lax.broadcasted_iota(jnp.int32, (BLOCK, BLOCK), 1) < HEAD_DIM
        before = (lax.broadcasted_iota(jnp.int32, (2 * BLOCK, BLOCK), 1)
                  < jnp.bitwise_and(lax.broadcasted_iota(jnp.int32, (2 * BLOCK, BLOCK), 0), BLOCK - 1))
        after, earlier = _tri(True), _tri(False)
        suffix = lambda t: _split_dot_vjp(t, after, earlier, 2)
        stack = lambda t: jnp.concatenate([jnp.where(low, t, 0.0), jnp.where(low, 0.0, t)], axis=0)
        qs = [stack(q_ref[:, p * BLOCK:(p + 1) * BLOCK].astype(F32)) for p in range(SB_PAIRS)]
        dos = [stack(do_ref[:, p * BLOCK:(p + 1) * BLOCK].astype(F32)) for p in range(SB_PAIRS)]

        def cond(c):
            return jnp.logical_and(c[0] >= 0, c[1] > SB_SKIP_LOG)

        def down(c):
            kb, _, rs = c
            rows = pl.ds(pl.multiple_of(kb * BLOCK, BLOCK), BLOCK)
            mask = jnp.logical_or(before, kb != n)
            new_r, top = [], None
            for h in range(SB_PAIRS):
                cols = slice(h * BLOCK, (h + 1) * BLOCK)
                r_ref[h, kb] = rs[h]
                z = _dot(qs[h], k_ref[rows, cols], NT) * (HEAD_DIM ** -0.5)
                log_keep = jnp.where(mask, _log_sigmoid(z) - z, 0.0)
                r_out = rs[h] + jnp.sum(log_keep, axis=1, keepdims=True)
                new_r.append(r_out)
                top = jnp.max(r_out) if top is None else jnp.maximum(top, jnp.max(r_out))
            return kb - 1, top, tuple(new_r)

        init = (n, jnp.float32(0.0), tuple(jnp.zeros((2 * BLOCK, 1), F32) for _ in range(SB_PAIRS)))
        last = lax.while_loop(cond, down, init)[0] + 1

        def up(kb, c):
            dqs, g_rs = c
            rows = pl.ds(pl.multiple_of(kb * BLOCK, BLOCK), BLOCK)
            mask = jnp.logical_or(before, kb != n)
            new_dq, new_g = [], []
            for h in range(SB_PAIRS):
                cols = slice(h * BLOCK, (h + 1) * BLOCK)
                _, vjp = jax.vjp(lambda q_, k_, v_, r_: _sb_pair(_dot_vjp, suffix, q_, k_, v_, r_, mask),
                                 qs[h], k_ref[rows, cols].astype(F32), v_ref[rows, cols].astype(F32), r_ref[h, kb])
                dq_c, dk_c, dv_c, g_in = vjp((dos[h], g_rs[h]))
                dk_ref[rows, cols] += dk_c
                dv_ref[rows, cols] += dv_c
                new_dq.append(dqs[h] + dq_c)
                new_g.append(g_in)
            return tuple(new_dq), tuple(new_g)

        init = (tuple(jnp.zeros((2 * BLOCK, BLOCK), F32) for _ in range(SB_PAIRS)),
                tuple(jnp.zeros((2 * BLOCK, 1), F32) for _ in range(SB_PAIRS)))
        dqs = lax.fori_loop(last, n + 1, up, init)[0]
        for p in range(SB_PAIRS):
            dq_ref[:, p * BLOCK:(p + 1) * BLOCK] = jnp.where(low, dqs[p][:BLOCK], dqs[p][BLOCK:])

    full = jax.ShapeDtypeStruct((s, pairs * BLOCK), F32)
    return _pcall(
        body, side=side, name=name, grid=(pairs // SB_PAIRS, nb),
        in_specs=[pl.BlockSpec((BLOCK, wide), lambda g, n: (n, qcb // SB_PAIRS + g)),
                  pl.BlockSpec((s, wide), lambda g, n: (0, kcb // SB_PAIRS + g), pipeline_mode=pl.Buffered(1)),
                  pl.BlockSpec((s, wide), lambda g, n: (0, vcb // SB_PAIRS + g), pipeline_mode=pl.Buffered(1)),
                  pl.BlockSpec((BLOCK, wide), lambda g, n: (n, docb // SB_PAIRS + g))],
        out_specs=[pl.BlockSpec((BLOCK, wide), lambda g, n: (n, g)),
                   pl.BlockSpec((s, wide), lambda g, n: (0, g), pipeline_mode=pl.Buffered(1)),
                   pl.BlockSpec((s, wide), lambda g, n: (0, g), pipeline_mode=pl.Buffered(1))],
        out_shape=[full, full, full],
        scratch_shapes=[pltpu.VMEM((SB_PAIRS, nb, 2 * BLOCK, 1), F32)],
        compiler_params=_params(),
    )(qkv, qkv, qkv, do)


def _xa_tile(dot, q, kv, qg, kg):
    hd = q.shape[1] // X_HEADS
    outs = []
    for h in range(X_HEADS):
        qh = _rms(q[:, h * hd:(h + 1) * hd], qg)
        kh = _rms(kv[:, h * hd:(h + 1) * hd], kg)
        vh = kv[:, (X_HEADS + h) * hd:(X_HEADS + h + 1) * hd]
        sc = dot(qh, kh, True) * (hd ** -0.5)
        m = lax.stop_gradient(jnp.max(sc, axis=-1, keepdims=True))
        p = jnp.exp(sc - m)
        outs.append(dot(p * (1.0 / jnp.sum(p, axis=-1, keepdims=True)), vh, False))
    return jnp.concatenate(outs, axis=1)


def xa_core_fwd(q, kv, qg, kg, name):
    n, d = q.shape
    tm = _tile(n, 256, 8)
    (o,) = tcall(lambda ids, qt, kvt, qgt, kgt: (_xa_tile(_plain_dot, qt, kvt, qgt, kgt),), (n // tm,),
                 [_row(q, tm), _full(kv), _full(qg), _full(kg)], [_row_out(n, d, BF16, tm)], name)
    return o


def xa_core_bwd(q, kv, qg, kg, do, name):
    n, d = q.shape
    tm = _tile(n, 256, 8)

    def fn(ids, qt, kvt, qgt, kgt, dot_):
        _, vjp = jax.vjp(functools.partial(_xa_tile, _dot_vjp), qt, kvt, qgt, kgt)
        return vjp(dot_.astype(F32))

    return tcall(fn, (n // tm,), [_row(q, tm), _full(kv), _full(qg), _full(kg), _row(do, tm)],
                 [_row_out(n, d, BF16, tm), _acc_out(kv.shape), _acc_out(qg.shape), _acc_out(kg.shape)], name)


def _ev_reorder(a):
    return jnp.concatenate([a[0:512], a[768:2304], a[512:768]], axis=0)


def _ev_restore(a):
    return jnp.concatenate([a[0:512], a[2048:2304], a[512:2048]], axis=0)


_EV_SEGS = ((0, 512, "q"), (512, 1536, "raw"), (2048, 128, "k"), (2176, 128, "raw"))
_A_CFG = dict(hkv=A_KV_HEADS, grp=A_Q_HEADS // A_KV_HEADS, max_dist=BLOCK - 1, step=1.0, slopes=_alibi(A_Q_HEADS),
              want_lse=False)
_A_COLS = (lambda r: 0, lambda r: 16, lambda r: 17)


def even_mixer_fwd(x, h, w_in, qg, kg, sinks, w_out, tag, side=None):
    qkv = mm(h, w_in, "nt", tag + "_in")
    (ops,) = prep_fwd(qkv, qg, kg, _EV_SEGS, (1,), tag + "_prep")
    (o_a,) = banded_fwd(ops, 1, _A_COLS, sinks, _A_CFG, tag + "_swa")
    o_b = sb_fwd(ops, 4, 8, 12, tag + "_sb", side=side)
    carried = None
    if side is not None:
        o_b, carried = o_b
    o = jnp.concatenate([o_a, o_b], axis=1)
    y = mm(o, w_out, "nn", tag + "_out", res=x)
    return y, (x, h, qkv, ops, o), carried


def even_mixer_bwd(dy, saved, g, w_in, qg, kg, sinks, w_out, tag, side=None, last_side=None):
    x, h, qkv, ops, o = saved
    do = mm(dy, w_out, "nt", tag + "_do", out_dtype=BF16)
    d_wout = mm(o, dy, "tn", tag + "_dwout")
    dqa, dkp, dkc, dvp, dvc, dsinks = banded_bwd(ops, 1, _A_COLS, sinks, _A_CFG, [(do, lambda r: 0)], tag + "_dswa")
    res = sb_bwd(ops, 4, 8, 12, do, 4, tag + "_dsb", side=side)
    carried = None
    if side is not None:
        res, carried = res
    dqb, dkb, dvb = res
    dqkv, dqg, dkg = prep_bwd(
        qkv, qg, kg, _EV_SEGS,
        [(dqa, 0, 1, 0), (dqb, 0, 1, 1), (dkb, 0, 1, 2), (dvb, 0, 1, 3), (dkc, 0, 1, 4), (dkp, 1, 1, 4), (dvc, 0, 1, 5),
         (dvp, 1, 1, 5)],
        lambda *t: jnp.concatenate(t, axis=1), tag + "_dqkv")
    d_win = mm(dqkv, h, "tn", tag + "_dwin")
    last = None if last_side is None else last_side(d_win, d_wout)
    if last is None:
        dx, dg = mm_norm_bwd(dqkv, w_in, x, g, dy, tag + "_dh", b_kd=True)
    else:
        (dx, dg), got = mm_norm_bwd(dqkv, w_in, x, g, dy, tag + "_dh", b_kd=True, side=last[0])
        last[1](got)
    return dx, dg, d_win, dqg, dkg, dsinks, d_wout, carried


def _c_cfg(window, dil):
    return dict(hkv=C_HEADS, grp=1, max_dist=window // dil, step=float(dil), slopes=_alibi(C_HEADS), want_lse=True)


_C_COLS = (lambda r: 3 * r, lambda r: 3 * r + 1, lambda r: 3 * r + 2)
_OD_SEGS = ((0, 1024, "q"), (1024, 1024, "k"), (2048, 1024, "raw"))


def _combine(o1, o2, o3, l1, l2, l3):
    m = lax.stop_gradient(jnp.maximum(jnp.maximum(l1, l2), l3))
    e1, e2, e3 = jnp.exp(l1 - m), jnp.exp(l2 - m), jnp.exp(l3 - m)
    tot = e1 + e2 + e3
    return (e1 / tot) * o1 + (e2 / tot) * o2 + (e3 / tot) * o3


def odd_mixer_fwd(x, g, w_in, qg, kg, w_out, tag):
    n, d = x.shape
    h = rmsnorm_fwd(x, g, tag + "_norm")
    qkv = mm(h, w_in, "nt", tag + "_in")
    dils = [dil for _, dil in C_PATTERNS]
    ops = prep_fwd(qkv, qg, kg, _OD_SEGS, dils, tag + "_prep")
    os_, ls_ = [], []
    for (window, dil), ops_d in zip(C_PATTERNS, ops):
        o_p, l_p = banded_fwd(ops_d, dil, _C_COLS, None, _c_cfg(window, dil), f"{tag}_dil{dil}")
        os_.append(o_p)
        ls_.append(l_p)
    tm = BLOCK
    lay = lambda a, dil: _in(a, (tm // dil, a.shape[1]), lambda i: (i, 0))
    views = [lay(a, dil) for a, dil in zip(os_ + ls_, dils + dils)]

    def comb(ids, *t, scratch):
        return (_combine(*[_to_natural(scratch, a, dil) for a, dil in zip(t, dils + dils)]),)

    (o,) = tcall(comb, (n // tm,), views, [_row_out(n, d, BF16, tm)], tag + "_comb",
                 scratch=((d // BLOCK * tm, BLOCK), F32))
    y = mm(o, w_out, "nn", tag + "_out", res=x)
    return y, (x, h, qkv, ops, views, o)


def odd_mixer_bwd(dy, saved, g, w_in, qg, kg, w_out, tag):
    x, h, qkv, ops, views, o = saved
    n, d = x.shape
    do = mm(dy, w_out, "nt", tag + "_do")
    d_wout = mm(o, dy, "tn", tag + "_dwout")
    tm = BLOCK
    dils = [dil for _, dil in C_PATTERNS]

    def comb_bwd(ids, *t, scratch):
        _, vjp = jax.vjp(_combine, *[_to_natural(scratch, a, dil) for a, dil in zip(t[:6], dils + dils)])
        return tuple(_to_strided(scratch, c, dil) for c, dil in zip(vjp(t[6]), dils + dils))

    cts = tcall(comb_bwd, (n // tm,), views + [_row(do, tm)],
                [_out((n // dil, dil * d), F32, (tm // dil, dil * d), lambda i: (i, 0)) for dil in dils + dils],
                tag + "_dcomb", scratch=((d // BLOCK * tm, BLOCK), F32))
    dqs, dks, dvs = [], [], []
    for p, ((window, dil), ops_d) in enumerate(zip(C_PATTERNS, ops)):
        dq, dkp, dkc, dvp, dvc = banded_bwd(ops_d, dil, _C_COLS, None, _c_cfg(window, dil),
                                            [(cts[p], lambda r: r), (cts[3 + p], lambda r: r)], f"{tag}_ddil{dil}")
        dqs.append((dq, 0, dil, p))
        dks += [(dkc, 0, dil, 3 + p), (dkp, dil, dil, 3 + p)]
        dvs += [(dvc, 0, dil, 6 + p), (dvp, dil, dil, 6 + p)]

    def gather(*t):
        return jnp.concatenate([t[0] + t[1] + t[2], t[3] + t[4] + t[5], t[6] + t[7] + t[8]], axis=1)

    dqkv, dqg, dkg = prep_bwd(qkv, qg, kg, _OD_SEGS, dqs + dks + dvs, gather, tag + "_dqkv")
    d_win = mm(dqkv, h, "tn", tag + "_dwin")
    dx, dg = mm_norm_bwd(dqkv, w_in, x, g, dy, tag + "_dh", b_kd=True)
    return dx, dg, d_win, dqg, dkg, d_wout


def xa_fwd(x, mem, g, gm, w_q, w_kv, qg, kg, w_o, tag):
    h = rmsnorm_fwd(x, g, tag + "_norm")
    q = mm(h, w_q, "nn", tag + "_q")
    mn = rmsnorm_fwd(mem, gm, tag + "_mnorm")
    kv = mm(mn, w_kv, "nt", tag + "_kv")
    o = xa_core_fwd(q, kv, qg, kg, tag + "_core")
    y = mm(o, w_o, "nn", tag + "_o", res=x)
    return y, (x, h, q, mn, kv, o)


def xa_bwd(dy, saved, mem, g, gm, w_q, w_kv, qg, kg, w_o, tag):
    x, h, q, mn, kv, o = saved
    do = mm(dy, w_o, "nt", tag + "_do", out_dtype=BF16)
    d_wo = mm(o, dy, "tn", tag + "_dwo")
    dq, dkv, dqg, dkg = xa_core_bwd(q, kv, qg, kg, do, tag + "_dcore")
    d_wq = mm(h, dq, "tn", tag + "_dwq")
    dx, dg = mm_norm_bwd(dq, w_q, x, g, dy, tag + "_dh")
    d_wkv = mm(dkv, mn, "tn", tag + "_dwkv")
    _, dgm = mm_norm_bwd(dkv, w_kv, mem, gm, None, tag + "_dmn", b_kd=True)
    return dx, dg, dgm, d_wq, d_wkv, dqg, dkg, d_wo


def loss_head(y, target, name):
    n, d = y.shape
    tm = _tile(n, 512, 8)

    def fn(ids, yt, tt):
        e = yt - tt
        return e * (1.0 / d), jnp.sum(e * e, axis=0, keepdims=True)

    return tcall(fn, (n // tm,), [_row(y, tm), _row(target, tm)], [_row_out(n, d, F32, tm), _acc_out((1, d))], name)


_ANY = pl.BlockSpec(memory_space=pl.ANY)


def all_gather_blocks(blocks):
    nb = len(blocks)

    def body(*refs):
        x_refs, out_refs = refs[:nb], refs[nb:2 * nb]
        send_sems, recv_sems, local_sems = refs[2 * nb:]
        x, y, c = lax.axis_index("x"), lax.axis_index("y"), lax.axis_index("c")
        me, sibling = (x, y, c), (x, y, 1 - c)
        over_x, over_y, diagonal = (1 - x, y), (x, 1 - y), (1 - x, 1 - y)
        relay_of = ((1 - x) * (1 - c) + x * c, y * (1 - c) + (1 - y) * c)
        relay_to = (x * (1 - c) + (1 - x) * c, (1 - y) * (1 - c) + y * c)

        def copy(b, k, blk, to, own=False):
            px, py, pc = blk
            slot = out_refs[b].at[4 * px + 2 * py + pc]
            return pltpu.make_async_remote_copy(
                src_ref=x_refs[b] if own else slot, dst_ref=slot,
                send_sem=send_sems.at[7 * b + k], recv_sem=recv_sems.at[7 * b + k], device_id=to, device_id_type=MESH)

        mine = [pltpu.make_async_copy(x_refs[b], out_refs[b].at[4 * x + 2 * y + c], local_sems.at[b]) for b in range(nb)]
        for cp in mine:
            cp.start()
        sent = []
        for b in range(nb):
            sent += [copy(b, 0, me, sibling, own=True), copy(b, 1, me, (*over_x, c), own=True),
                     copy(b, 2, me, (*over_y, c), own=True)]
        for cp in sent:
            cp.start()
        for b in range(nb):
            copy(b, 1, (*over_x, c), me).wait_recv()
            copy(b, 2, (*over_y, c), me).wait_recv()
            later = [copy(b, 3, (*relay_of, c), (*relay_to, c)), copy(b, 4, (*over_x, c), sibling),
                     copy(b, 5, (*over_y, c), sibling)]
            for cp in later:
                cp.start()
            sent += later
        for b in range(nb):
            copy(b, 3, (*diagonal, c), me).wait_recv()
            fwd = copy(b, 6, (*diagonal, c), sibling)
            fwd.start()
            sent.append(fwd)
        for b in range(nb):
            copy(b, 0, sibling, me).wait_recv()
            for k, chip in ((4, over_x), (5, over_y), (6, diagonal)):
                copy(b, k, (*chip, 1 - c), me).wait_recv()
        for cp in sent:
            cp.wait_send()
        for cp in mine:
            cp.wait()

    return _pcall(
        body, name="weights_all_gather",
        in_specs=[_ANY] * nb, out_specs=[_ANY] * nb,
        out_shape=[jax.ShapeDtypeStruct((N_DEV,) + a.shape, a.dtype) for a in blocks],
        scratch_shapes=[pltpu.SemaphoreType.DMA((7 * nb,)), pltpu.SemaphoreType.DMA((7 * nb,)),
                        pltpu.SemaphoreType.DMA((nb,))],
    )(*blocks)


def pair_exchange(bufs):
    nb = len(bufs)

    def body(*refs):
        srcs, dsts = refs[:nb], refs[nb:2 * nb]
        send_sems, recv_sems = refs[2 * nb:]
        x, y, c = lax.axis_index("x"), lax.axis_index("y"), lax.axis_index("c")
        copies = []
        for b in range(nb):
            for j in range(4):
                cp = pltpu.make_async_remote_copy(
                    src_ref=srcs[b].at[2 * j + (1 - c)], dst_ref=dsts[b].at[j], send_sem=send_sems.at[4 * b + j],
                    recv_sem=recv_sems.at[4 * b + j], device_id=(x, y, 1 - c), device_id_type=MESH)
                cp.start()
                copies.append(cp)
        for cp in copies:
            cp.wait()

    return _pcall(
        body, name="grads_pair_exchange",
        in_specs=[_ANY] * nb, out_specs=[_ANY] * nb,
        out_shape=[jax.ShapeDtypeStruct((4,) + a.shape[1:], a.dtype) for a in bufs],
        scratch_shapes=[pltpu.SemaphoreType.DMA((4 * nb,)), pltpu.SemaphoreType.DMA((4 * nb,))],
    )(*bufs)


def pair_sum(g, got, c, out_dtype, name):
    r, w = g.shape[1:]
    tr = _tile(r, 512, 16)

    def body(c_ref, a_ref, b_ref, o_ref):
        o_ref[...] = (a_ref[...].astype(F32) + b_ref[...].astype(F32)).astype(o_ref.dtype)

    return _pcall(
        body, name=name,
        grid_spec=pltpu.PrefetchScalarGridSpec(
            num_scalar_prefetch=1, grid=(4, r // tr),
            in_specs=[pl.BlockSpec((None, tr, w), lambda j, i, c_ref: (2 * j + c_ref[0], i, 0)),
                      pl.BlockSpec((None, tr, w), lambda j, i, c_ref: (j, i, 0))],
            out_specs=pl.BlockSpec((None, tr, w), lambda j, i, c_ref: (j, i, 0))),
        out_shape=jax.ShapeDtypeStruct((4,) + g.shape[1:], out_dtype),
        compiler_params=_params(),
    )(c, g, got)


def chip_exchange(parts):
    nb = len(parts)

    def body(*refs):
        srcs, dsts = refs[:nb], refs[nb:2 * nb]
        send_sems, recv_sems, local_sems = refs[2 * nb:]
        x, y, c = lax.axis_index("x"), lax.axis_index("y"), lax.axis_index("c")
        my_chip = 2 * x + y
        copies = []
        for b in range(nb):
            mine = pltpu.make_async_copy(srcs[b].at[my_chip], dsts[b].at[my_chip], local_sems.at[b])
            mine.start()
            copies.append(mine)
            for k, (tx, ty) in enumerate([(1 - x, y), (x, 1 - y), (1 - x, 1 - y)]):
                cp = pltpu.make_async_remote_copy(
                    src_ref=srcs[b].at[2 * tx + ty], dst_ref=dsts[b].at[my_chip], send_sem=send_sems.at[3 * b + k],
                    recv_sem=recv_sems.at[3 * b + k], device_id=(tx, ty, c), device_id_type=MESH)
                cp.start()
                copies.append(cp)
        for cp in copies:
            cp.wait()

    return _pcall(
        body, name="grads_chip_exchange",
        in_specs=[_ANY] * nb, out_specs=[_ANY] * nb,
        out_shape=[jax.ShapeDtypeStruct(a.shape, a.dtype) for a in parts],
        scratch_shapes=[pltpu.SemaphoreType.DMA((3 * nb,)), pltpu.SemaphoreType.DMA((3 * nb,)),
                        pltpu.SemaphoreType.DMA((nb,))],
    )(*parts)


def chip_sum(parts, name):
    r, w = parts.shape[1:]
    tr = _tile(r, 512, 16)
    spec = lambda j: _in(parts, (None, tr, w), lambda i, j=j: (j, i, 0))

    def fn(ids, a, b, c_, d):
        a, b, c_, d = [t.astype(F32) for t in (a, b, c_, d)]
        return (((a + b) + c_) + d,)

    (out,) = tcall(fn, (r // tr,), [spec(j) for j in range(4)],
                   [_out((r, w), F32, (tr, w), lambda i: (i, 0))], name)
    return out


def _remote(src, dst, send_sems, recv_sems, k, to):
    return functools.partial(pltpu.make_async_remote_copy, src_ref=src, dst_ref=dst, send_sem=send_sems.at[k],
                             recv_sem=recv_sems.at[k], device_id=to, device_id_type=MESH)


def _gather_plan(phase, nb):
    def plan(ins, outs, send_sems, recv_sems, local_sems):
        x, y, c = lax.axis_index("x"), lax.axis_index("y"), lax.axis_index("c")
        me, sibling = (x, y, c), (x, y, 1 - c)
        over_x, over_y, diagonal = (1 - x, y), (x, 1 - y), (1 - x, 1 - y)
        relay_of = ((1 - x) * (1 - c) + x * c, y * (1 - c) + (1 - y) * c)
        relay_to = (x * (1 - c) + (1 - x) * c, (1 - y) * (1 - c) + y * c)
        local, sends, recvs = [], [], []
        for b in range(nb):
            slot = lambda chip, core, b=b: outs[b].at[4 * chip[0] + 2 * chip[1] + core]
            if phase == 0:
                local.append(functools.partial(pltpu.make_async_copy, ins[b], slot((x, y), c), local_sems.at[b]))
                moves = [(ins[b], slot((x, y), c), to) for to in (sibling, (*over_x, c), (*over_y, c))]
                arrive = [slot((x, y), 1 - c), slot(over_x, c), slot(over_y, c)]
            elif phase == 1:
                moves = [(slot(relay_of, c), slot(relay_of, c), (*relay_to, c)),
                         (slot(over_x, c), slot(over_x, c), sibling), (slot(over_y, c), slot(over_y, c), sibling)]
                arrive = [slot(diagonal, c), slot(over_x, 1 - c), slot(over_y, 1 - c)]
            else:
                moves = [(slot(diagonal, c), slot(diagonal, c), sibling)]
                arrive = [slot(diagonal, 1 - c)]
            sends += [_remote(src, dst, send_sems, recv_sems, 3 * b + k, to) for k, (src, dst, to) in enumerate(moves)]
            recvs += [_remote(dst, dst, send_sems, recv_sems, 3 * b + k, me) for k, dst in enumerate(arrive)]
        return local, sends, recvs
    return plan


def gather_side(phase, arrays):
    nb = len(arrays)
    if phase == 0:
        shapes = [jax.ShapeDtypeStruct((N_DEV,) + a.shape, a.dtype) for a in arrays]
        return Side(arrays, shapes, 3 * nb, nb, _gather_plan(0, nb))
    shapes = [jax.ShapeDtypeStruct(a.shape, a.dtype) for a in arrays]
    return Side(arrays, shapes, 3 * nb, 0, _gather_plan(phase, nb), aliased=True)


def pair_side(bufs):
    nb = len(bufs)

    def plan(ins, outs, send_sems, recv_sems, local_sems):
        x, y, c = lax.axis_index("x"), lax.axis_index("y"), lax.axis_index("c")
        sends = [_remote(ins[b].at[2 * j + (1 - c)], outs[b].at[j], send_sems, recv_sems, 4 * b + j, (x, y, 1 - c))
                 for b in range(nb) for j in range(4)]
        recvs = [_remote(outs[b].at[j], outs[b].at[j], send_sems, recv_sems, 4 * b + j, (x, y, c))
                 for b in range(nb) for j in range(4)]
        return [], sends, recvs

    shapes = [jax.ShapeDtypeStruct((4,) + a.shape[1:], a.dtype) for a in bufs]
    return Side(bufs, shapes, 4 * nb, 0, plan)


def chip_side(parts):
    nb = len(parts)

    def plan(ins, outs, send_sems, recv_sems, local_sems):
        x, y, c = lax.axis_index("x"), lax.axis_index("y"), lax.axis_index("c")
        my_chip = 2 * x + y
        peers = [(1 - x, y), (x, 1 - y), (1 - x, 1 - y)]
        local = [functools.partial(pltpu.make_async_copy, ins[b].at[my_chip], outs[b].at[my_chip], local_sems.at[b])
                 for b in range(nb)]
        sends = [_remote(ins[b].at[2 * tx + ty], outs[b].at[my_chip], send_sems, recv_sems, 3 * b + k, (tx, ty, c))
                 for b in range(nb) for k, (tx, ty) in enumerate(peers)]
        recvs = [_remote(outs[b].at[2 * tx + ty], outs[b].at[2 * tx + ty], send_sems, recv_sems, 3 * b + k, (x, y, c))
                 for b in range(nb) for k, (tx, ty) in enumerate(peers)]
        return local, sends, recvs

    shapes = [jax.ShapeDtypeStruct(a.shape, a.dtype) for a in parts]
    return Side(parts, shapes, 3 * nb, nb, plan)


def adamw(w, g, m, v, name):
    shape = w.shape
    cols = shape[-1]
    rows = int(np.prod(shape[:-1]))
    w2, g2, m2, v2 = [a.reshape(rows, cols) for a in (w, g, m, v)]
    tr = _tile(rows, 256, 8) if rows % 8 == 0 else rows

    def fn(ids, wt, gt, mt, vt):
        m_new = ADAM_B1 * mt + (1.0 - ADAM_B1) * gt
        v_new = ADAM_B2 * vt + (1.0 - ADAM_B2) * (gt * gt)
        m_hat = m_new / (1.0 - ADAM_B1 ** ADAM_STEP)
        v_hat = v_new / (1.0 - ADAM_B2 ** ADAM_STEP)
        delta = -ADAM_LR * (m_hat / (jnp.sqrt(v_hat) + ADAM_EPS) + ADAM_WD * wt)
        return delta, m_new, v_new

    res = tcall(fn, (rows // tr,), [_row(a, tr) for a in (w2, g2, m2, v2)],
                [_row_out(rows, cols, F32, tr) for _ in range(3)], name)
    return [a.reshape(shape) for a in res]


_MATS = [("ffn1_w_gu", "col"), ("ffn1_w_down", "row"), ("ev_w_in", "col"), ("ev_w_out", "row"),
         ("od_w_in", "col"), ("od_w_out", "row"), ("xa_w_q", "row"), ("xa_w_kv", "col"), ("xa_w_o", "row"),
         ("ffn2_w_gu", "col"), ("ffn2_w_down", "row")]
_VECS = ["ffn1_norm", "mix_norm", "ev_q_gain", "ev_k_gain", "ev_sinks", "od_q_gain", "od_k_gain", "xa_norm",
         "xa_mem_norm", "xa_q_gain", "xa_k_gain", "ffn2_norm"]
_WEIGHTS = ["ffn1_norm", "ffn1_w_gu", "ffn1_w_down", "mix_norm", "ev_w_in", "ev_q_gain", "ev_k_gain", "ev_sinks",
            "ev_w_out", "od_w_in", "od_q_gain", "od_k_gain", "od_w_out", "xa_norm", "xa_mem_norm", "xa_w_q", "xa_w_kv",
            "xa_q_gain", "xa_k_gain", "xa_w_o", "ffn2_norm", "ffn2_w_gu", "ffn2_w_down"]


_AXIS = dict(_MATS)
DEPTH = 2


def _layer_groups(l):
    first, rest = _first_block_groups(l)
    return [first[0] + rest[0] + rest[1]]


def _first_block_groups(l):
    w_in, w_out = ("ev_w_in", "ev_w_out") if l % 2 == 0 else ("od_w_in", "od_w_out")
    first = [[("ffn1_w_gu", l), ("ffn1_w_down", l)]]
    rest = [[("ffn2_w_gu", l), ("xa_w_kv", l)],
            [(w_in, l // 2), ("ffn2_w_down", l), (w_out, l // 2), ("xa_w_q", l), ("xa_w_o", l)]]
    return first, rest


def _block_rows(shards, n):
    a, b = shards[n].shape[1:]
    return a if _AXIS[n] == "row" else b


def _weight_blocks(shards, groups):
    blocks = []
    for group in groups:
        rows = [(shards[n][j] if _AXIS[n] == "row" else shards[n][j].T).astype(BF16) for n, j in group]
        blocks.append(rows[0] if len(rows) == 1 else jnp.concatenate(rows, axis=0))
    return blocks


def _whole_weights(shards, groups, gathered):
    full = {}
    for group, got in zip(groups, gathered):
        off = 0
        for n, j in group:
            r = _block_rows(shards, n)
            full[n] = got[:, off:off + r, :].reshape(N_DEV * r, got.shape[2])
            off += r
    return full


def _gradient_buffers(grads, groups):
    bufs = []
    for group in groups:
        rows = []
        for n, _ in group:
            whole = jnp.concatenate(grads[n], axis=0) if isinstance(grads[n], tuple) else grads[n]
            rows.append(whole.reshape(N_DEV, whole.shape[0] // N_DEV, whole.shape[1]))
        bufs.append((rows[0] if len(rows) == 1 else jnp.concatenate(rows, axis=1)).astype(BF16))
    return bufs


def _gradient_blocks(shards, groups, sums):
    out = {}
    for group, tot in zip(groups, sums):
        off = 0
        for n, j in group:
            r = _block_rows(shards, n)
            out[n, j] = tot[off:off + r] if _AXIS[n] == "row" else tot[off:off + r].T
            off += r
    return out


class _PairChain:
    def __init__(self, ex, bufs):
        self.ex, self.bufs, self.parts = ex, bufs, None

    def side(self, name):
        return pair_side(self.bufs) if name == "da" else None

    def done(self, name, carried):
        self.parts = self.ex.pair_sums(self.bufs, carried, "l1")


class _RestChain:
    HALF = {"da": (0,), "dh": (1,)}

    def __init__(self, ex, bufs):
        self.ex, self.bufs, self.parts, self.sums = ex, bufs, None, [None] * len(bufs)

    def side(self, name):
        if name == "pair":
            return pair_side(self.bufs)
        if name in self.HALF:
            return chip_side([self.parts[i] for i in self.HALF[name]])
        return None

    def done(self, name, carried):
        if name == "pair":
            self.parts = self.ex.pair_sums(self.bufs, carried, "l0r")
        else:
            for i, tot in zip(self.HALF[name], self.ex.chip_sums(carried, "l0r_" + name)):
                self.sums[i] = tot


class _Exchange:
    def __init__(self, shards, c):
        self.shards, self.c = shards, c

    def weights_first(self):
        first, _ = _first_block_groups(0)
        return _whole_weights(self.shards, first, all_gather_blocks(_weight_blocks(self.shards, first)))

    def rest_blocks(self):
        return _weight_blocks(self.shards, _first_block_groups(0)[1])

    def weights_rest(self, gathered):
        return _whole_weights(self.shards, _first_block_groups(0)[1], gathered)

    def gather_start(self):
        return gather_side(0, _weight_blocks(self.shards, _layer_groups(1)))

    def weights_next(self, gathered):
        return _whole_weights(self.shards, _layer_groups(1), gathered)

    def chain_next(self, grads):
        return _PairChain(self, _gradient_buffers(grads, _layer_groups(1)))

    def chain_rest(self, grads):
        return _RestChain(self, _gradient_buffers(grads, _first_block_groups(0)[1]))

    def pair_sums(self, bufs, got, tag):
        return [pair_sum(b, g, self.c, b.dtype, f"grads_pair_sum_{tag}_{i}") for i, (b, g) in enumerate(zip(bufs, got))]

    def chip_sums(self, parts, tag):
        return [chip_sum(p, f"grads_chip_sum_{tag}_{i}") for i, p in enumerate(parts)]

    def finish(self, gm, gv, sums1, sums_rest):
        vecs = {n: jnp.concatenate(v, axis=0) for n, v in gv.items()}
        first, rest = _first_block_groups(0)
        bufs = _gradient_buffers(gm[0], first)
        vec = jnp.concatenate([vecs[n].reshape(-1) for n in _VECS])
        vec = jnp.pad(vec, (0, -vec.shape[0] % (16 * LANES)))
        bufs.append(jnp.broadcast_to(vec.reshape(1, -1, LANES), (N_DEV, vec.shape[0] // LANES, LANES)))
        parts = self.pair_sums(bufs, pair_exchange(bufs), "l0")
        sums0 = self.chip_sums(chip_exchange(parts), "l0")
        blocks = {**_gradient_blocks(self.shards, first, sums0[:-1]), **_gradient_blocks(self.shards, rest, sums_rest),
                  **_gradient_blocks(self.shards, _layer_groups(1), sums1)}
        out = {n: jnp.stack([blocks[n, j] for j in range(self.shards[n].shape[0])]) for n, _ in _MATS}
        flat, off = sums0[-1].reshape(-1), 0
        for n in _VECS:
            out[n] = flat[off:off + vecs[n].size].reshape(vecs[n].shape)
            off += vecs[n].size
        return out


class _NoExchange:
    def __init__(self, full):
        self.full = full

    def weights_first(self):
        return self.full[0]

    def rest_blocks(self):
        return None

    def gather_start(self):
        return None

    def weights_next(self, gathered):
        return self.full[1]

    def chain_next(self, grads):
        return None

    def chain_rest(self, grads):
        return None

    def finish(self, gm, gv, sums1, sums_rest):
        mats = {}
        for l in range(DEPTH):
            for group in _layer_groups(l):
                for n, j in group:
                    whole = jnp.concatenate(gm[l][n], axis=0) if isinstance(gm[l][n], tuple) else gm[l][n]
                    mats.setdefault(n, {})[j] = whole if _AXIS[n] == "row" else whole.T
        mats = {n: jnp.stack([v[j] for j in sorted(v)]) for n, v in mats.items()}
        return mats, {n: jnp.concatenate(v, axis=0) for n, v in gv.items()}


def _local_step(x, mem, target, w, ex):
    assert w["ffn1_norm"].shape[0] == DEPTH
    row = lambda a, l: a[l:l + 1]
    full = [ex.weights_first(), None]
    saved = []
    for l in range(DEPTH):
        t, j, f = f"l{l}", l // 2, full[l]
        rest = ex.rest_blocks() if l == 0 else None
        if rest is None:
            x, s1 = ffn_fwd(x, row(w["ffn1_norm"], l), f["ffn1_w_gu"], f["ffn1_w_down"], t + "_ffn1")
        else:
            x, s1, rest = ffn_fwd(x, row(w["ffn1_norm"], l), f["ffn1_w_gu"], f["ffn1_w_down"], t + "_ffn1", (0, rest))
        relay = None
        if l % 2 == 0:
            h = rmsnorm_fwd(x, row(w["mix_norm"], l), t + "_ev_norm", None if rest is None else gather_side(2, rest))
            if rest is not None:
                h, rest = h
                f = full[l] = {**f, **ex.weights_rest(rest)}
            side = ex.gather_start() if l + 1 < DEPTH else None
            x, s2, relay = even_mixer_fwd(x, h, _ev_reorder(f["ev_w_in"]), row(w["ev_q_gain"], j),
                                          row(w["ev_k_gain"], j), row(w["ev_sinks"], j), f["ev_w_out"], t + "_ev", side)
        else:
            x, s2 = odd_mixer_fwd(x, row(w["mix_norm"], l), f["od_w_in"], row(w["od_q_gain"], j),
                                  row(w["od_k_gain"], j), f["od_w_out"], t + "_od")
        x, s3 = xa_fwd(x, mem, row(w["xa_norm"], l), row(w["xa_mem_norm"], l), f["xa_w_q"], f["xa_w_kv"],
                       row(w["xa_q_gain"], l), row(w["xa_k_gain"], l), f["xa_w_o"], t + "_xa")
        if relay is None:
            x, s4 = ffn_fwd(x, row(w["ffn2_norm"], l), f["ffn2_w_gu"], f["ffn2_w_down"], t + "_ffn2")
        else:
            x, s4, relay = ffn_fwd(x, row(w["ffn2_norm"], l), f["ffn2_w_gu"], f["ffn2_w_down"], t + "_ffn2", (1, relay))
        if l + 1 < DEPTH:
            full[l + 1] = ex.weights_next(relay)
        saved.append((s1, s2, s3, s4))
    dx, sq = loss_head(x, target, "loss_head")
    loss = 0.5 * jnp.sum(sq) / x.shape[1]

    gm = [dict() for _ in range(DEPTH)]
    gv = {n: [None] * w[n].shape[0] for n in _VECS}
    chain1 = chain0 = sums1 = None
    started = []
    for l in reversed(range(DEPTH)):
        t, j, f = f"l{l}", l // 2, full[l]
        s1, s2, s3, s4 = saved[l]
        dx, gv["ffn2_norm"][l], gm[l]["ffn2_w_gu"], gm[l]["ffn2_w_down"] = ffn_bwd(
            dx, s4, row(w["ffn2_norm"], l), f["ffn2_w_gu"], f["ffn2_w_down"], t + "_ffn2", chain1 if l == 0 else None)
        parts = chain1.parts if l == 0 and chain1 is not None else None
        (dx, gv["xa_norm"][l], gv["xa_mem_norm"][l], gm[l]["xa_w_q"], gm[l]["xa_w_kv"], gv["xa_q_gain"][l],
         gv["xa_k_gain"][l], gm[l]["xa_w_o"]) = xa_bwd(
            dx, s3, mem, row(w["xa_norm"], l), row(w["xa_mem_norm"], l), f["xa_w_q"], f["xa_w_kv"],
            row(w["xa_q_gain"], l), row(w["xa_k_gain"], l), f["xa_w_o"], t + "_xa")
        if l % 2 == 0:
            def start_rest(d_win, d_wout, l=l):
                gm[l]["ev_w_in"], gm[l]["ev_w_out"] = _ev_restore(d_win), d_wout
                chain = ex.chain_rest(gm[l]) if l == 0 else None
                if chain is None:
                    return None
                started.append(chain)
                return chain.side("pair"), lambda got: chain.done("pair", got)

            (dx, gv["mix_norm"][l], d_win, gv["ev_q_gain"][j], gv["ev_k_gain"][j], gv["ev_sinks"][j],
             gm[l]["ev_w_out"], carried) = even_mixer_bwd(
                dx, s2, row(w["mix_norm"], l), _ev_reorder(f["ev_w_in"]), row(w["ev_q_gain"], j), row(w["ev_k_gain"], j),
                row(w["ev_sinks"], j), f["ev_w_out"], t + "_ev", None if parts is None else chip_side(parts), start_rest)
            gm[l]["ev_w_in"] = _ev_restore(d_win)
            if carried is not None:
                sums1 = ex.chip_sums(carried, "l1")
        else:
            (dx, gv["mix_norm"][l], gm[l]["od_w_in"], gv["od_q_gain"][j], gv["od_k_gain"][j],
             gm[l]["od_w_out"]) = odd_mixer_bwd(
                dx, s2, row(w["mix_norm"], l), f["od_w_in"], row(w["od_q_gain"], j), row(w["od_k_gain"], j),
                f["od_w_out"], t + "_od")
        if l == 0 and started:
            chain0 = started[0]
        dx, gv["ffn1_norm"][l], gm[l]["ffn1_w_gu"], gm[l]["ffn1_w_down"] = ffn_bwd(
            dx, s1, row(w["ffn1_norm"], l), f["ffn1_w_gu"], f["ffn1_w_down"], t + "_ffn1", chain0 if l == 0 else None)
        if l == 1:
            chain1 = ex.chain_next(gm[l])
    return loss, dx, ex.finish(gm, gv, sums1, None if chain0 is None else chain0.sums)


def kernel(x, mem, ffn1_norm, ffn1_w_gu, ffn1_w_down, mix_norm, ev_w_in, ev_q_gain, ev_k_gain, ev_sinks, ev_w_out, od_w_in, od_q_gain, od_k_gain, od_w_out, xa_norm, xa_mem_norm, xa_w_q, xa_w_kv, xa_q_gain, xa_k_gain, xa_w_o, ffn2_norm, ffn2_w_gu, ffn2_w_down, loss_target, m_ffn1_norm, m_ffn1_w_gu, m_ffn1_w_down, m_mix_norm, m_ev_w_in, m_ev_q_gain, m_ev_k_gain, m_ev_sinks, m_ev_w_out, m_od_w_in, m_od_q_gain, m_od_k_gain, m_od_w_out, m_xa_norm, m_xa_mem_norm, m_xa_w_q, m_xa_w_kv, m_xa_q_gain, m_xa_k_gain, m_xa_w_o, m_ffn2_norm, m_ffn2_w_gu, m_ffn2_w_down, v_ffn1_norm, v_ffn1_w_gu, v_ffn1_w_down, v_mix_norm, v_ev_w_in, v_ev_q_gain, v_ev_k_gain, v_ev_sinks, v_ev_w_out, v_od_w_in, v_od_q_gain, v_od_k_gain, v_od_w_out, v_xa_norm, v_xa_mem_norm, v_xa_w_q, v_xa_w_kv, v_xa_q_gain, v_xa_k_gain, v_xa_w_o, v_ffn2_norm, v_ffn2_w_gu, v_ffn2_w_down):
    w = dict(ffn1_norm=ffn1_norm, ffn1_w_gu=ffn1_w_gu, ffn1_w_down=ffn1_w_down, mix_norm=mix_norm, ev_w_in=ev_w_in, ev_q_gain=ev_q_gain, ev_k_gain=ev_k_gain, ev_sinks=ev_sinks, ev_w_out=ev_w_out, od_w_in=od_w_in, od_q_gain=od_q_gain, od_k_gain=od_k_gain, od_w_out=od_w_out, xa_norm=xa_norm, xa_mem_norm=xa_mem_norm, xa_w_q=xa_w_q, xa_w_kv=xa_w_kv, xa_q_gain=xa_q_gain, xa_k_gain=xa_k_gain, xa_w_o=xa_w_o, ffn2_norm=ffn2_norm, ffn2_w_gu=ffn2_w_gu, ffn2_w_down=ffn2_w_down)
    m = dict(ffn1_norm=m_ffn1_norm, ffn1_w_gu=m_ffn1_w_gu, ffn1_w_down=m_ffn1_w_down, mix_norm=m_mix_norm, ev_w_in=m_ev_w_in, ev_q_gain=m_ev_q_gain, ev_k_gain=m_ev_k_gain, ev_sinks=m_ev_sinks, ev_w_out=m_ev_w_out, od_w_in=m_od_w_in, od_q_gain=m_od_q_gain, od_k_gain=m_od_k_gain, od_w_out=m_od_w_out, xa_norm=m_xa_norm, xa_mem_norm=m_xa_mem_norm, xa_w_q=m_xa_w_q, xa_w_kv=m_xa_w_kv, xa_q_gain=m_xa_q_gain, xa_k_gain=m_xa_k_gain, xa_w_o=m_xa_w_o, ffn2_norm=m_ffn2_norm, ffn2_w_gu=m_ffn2_w_gu, ffn2_w_down=m_ffn2_w_down)
    v = dict(ffn1_norm=v_ffn1_norm, ffn1_w_gu=v_ffn1_w_gu, ffn1_w_down=v_ffn1_w_down, mix_norm=v_mix_norm, ev_w_in=v_ev_w_in, ev_q_gain=v_ev_q_gain, ev_k_gain=v_ev_k_gain, ev_sinks=v_ev_sinks, ev_w_out=v_ev_w_out, od_w_in=v_od_w_in, od_q_gain=v_od_q_gain, od_k_gain=v_od_k_gain, od_w_out=v_od_w_out, xa_norm=v_xa_norm, xa_mem_norm=v_xa_mem_norm, xa_w_q=v_xa_w_q, xa_w_kv=v_xa_w_kv, xa_q_gain=v_xa_q_gain, xa_k_gain=v_xa_k_gain, xa_w_o=v_xa_w_o, ffn2_norm=v_ffn2_norm, ffn2_w_gu=v_ffn2_w_gu, ffn2_w_down=v_ffn2_w_down)

    c = lax.axis_index("c").astype(jnp.int32).reshape(1)
    loss, dx, grads = _local_step(x[0], mem[0], loss_target[0], w, _Exchange(w, c))
    loss = lax.psum(loss, ("x", "y", "c"))

    delta, new_m, new_v = {}, {}, {}
    for n in _WEIGHTS:
        delta[n], new_m[n], new_v[n] = adamw(w[n], grads[n], m[n], v[n], "adamw_" + n)
    return (loss, dx[None], *[grads[n] for n in _WEIGHTS], *[delta[n] for n in _WEIGHTS],
            *[new_m[n] for n in _WEIGHTS], *[new_v[n] for n in _WEIGHTS])
```

```python
import functools

import numpy as np
import jax
import jax.numpy as jnp
from jax import lax
from jax.experimental import pallas as pl
from jax.experimental.pallas import tpu as pltpu

F32 = jnp.float32
BF16 = jnp.bfloat16
MESH = pl.DeviceIdType.MESH

HEAD_DIM = 64
BLOCK = 128
RMS_EPS = 1e-6
A_Q_HEADS, A_KV_HEADS = 8, 2
B_HEADS = 8
C_HEADS = 16
C_PATTERNS = ((128, 1), (512, 4), (2048, 16))
X_HEADS = 4
N_DEV = 8
LANES = 1024
VMEM_LIMIT_BYTES = 56 * 1024 * 1024
SB_SKIP_LOG = -110.0
NEG_BIG = -1e30

ADAM_LR, ADAM_B1, ADAM_B2, ADAM_EPS, ADAM_WD, ADAM_STEP = 0.001, 0.9, 0.999, 1e-08, 0.01, 10

NN = (((1,), (0,)), ((), ()))
NT = (((1,), (1,)), ((), ()))
TN = (((0,), (0,)), ((), ()))


class Side:
    def __init__(self, arrays, out_shapes, n_remote, n_local, plan, aliased=False):
        self.arrays, self.out_shapes, self.plan, self.aliased = list(arrays), list(out_shapes), plan, aliased
        self.sems = [pltpu.SemaphoreType.DMA((n_remote,)), pltpu.SemaphoreType.DMA((n_remote,)),
                     pltpu.SemaphoreType.DMA((max(n_local, 1),))]

    def start(self, ins, outs, sems):
        local, sends, _ = self.plan(ins, outs, *sems)
        for make in local + sends:
            make().start()

    def wait(self, ins, outs, sems):
        local, sends, recvs = self.plan(ins, outs, *sems)
        for make in sends:
            make().wait_send()
        for make in recvs:
            make().wait_recv()
        for make in local:
            make().wait()


def _pcall(body, side=None, **kw):
    if side is None:
        return pl.pallas_call(body, **kw)
    grid = kw["grid"]
    single = not isinstance(kw["out_specs"], (list, tuple))
    out_specs = [kw["out_specs"]] if single else list(kw["out_specs"])
    out_shape = [kw["out_shape"]] if single else list(kw["out_shape"])
    scratch = list(kw.get("scratch_shapes", []))
    n_in, n_out, n_scr, n_side = len(kw["in_specs"]), len(out_specs), len(scratch), len(side.arrays)
    n_sout = len(side.out_shapes)

    def hosted(*refs):
        ins, s_in = refs[:n_in], refs[n_in:n_in + n_side]
        outs = refs[n_in + n_side:n_in + n_side + n_out]
        s_out = refs[n_in + n_side + n_out:n_in + n_side + n_out + n_sout]
        rest = refs[n_in + n_side + n_out + n_sout:]
        scr, sems = rest[:n_scr], rest[n_scr:]
        first = last = None
        for a, size in enumerate(grid):
            f, l = pl.program_id(a) == 0, pl.program_id(a) == size - 1
            first = f if first is None else jnp.logical_and(first, f)
            last = l if last is None else jnp.logical_and(last, l)

        @pl.when(first)
        def _():
            side.start(s_in, s_out, sems)

        body(*ins, *outs, *scr)

        @pl.when(last)
        def _():
            side.wait(s_in, s_out, sems)

    any_space = pl.BlockSpec(memory_space=pl.ANY)
    kw2 = dict(kw)
    kw2.update(in_specs=list(kw["in_specs"]) + [any_space] * n_side, out_specs=out_specs + [any_space] * n_sout,
               out_shape=out_shape + side.out_shapes, scratch_shapes=scratch + side.sems)
    if side.aliased:
        kw2["input_output_aliases"] = {n_in + i: n_out + i for i in range(n_side)}
    call = pl.pallas_call(hosted, **kw2)

    def run(*args):
        res = call(*args, *side.arrays)
        return (res[0] if single else list(res[:n_out])), list(res[n_out:])

    return run


def _params(**kw):
    return pltpu.CompilerParams(vmem_limit_bytes=VMEM_LIMIT_BYTES, **kw)


def _tile(dim, cap, unit=128):
    if dim <= cap:
        return dim
    t = (cap // unit) * unit
    while t >= unit:
        if dim % t == 0:
            return t
        t -= unit
    raise ValueError(f"no tile for {dim} under {cap}")


def _dot(a, b, dims):
    return lax.dot_general(a.astype(BF16), b.astype(BF16), dims, preferred_element_type=F32)


@functools.partial(jax.custom_vjp, nondiff_argnums=(2,))
def _dot_vjp(a, b, nt):
    return _dot(a, b, NT if nt else NN)


def _dot_vjp_fwd(a, b, nt):
    return _dot(a, b, NT if nt else NN), (a.astype(BF16), b.astype(BF16))


def _dot_vjp_bwd(nt, res, g):
    a, b = res
    if nt:
        return _dot(g, b, NN), _dot(g, a, TN)
    return _dot(g, b, NT), _dot(a, g, TN)


_dot_vjp.defvjp(_dot_vjp_fwd, _dot_vjp_bwd)


def _plain_dot(a, b, nt):
    return _dot(a, b, NT if nt else NN)


def _split_dot(x, mat, terms=2):
    out, rem = None, x
    for t in range(terms):
        part = rem.astype(BF16)
        d = lax.dot_general(part, mat, NN, preferred_element_type=F32)
        out = d if out is None else out + d
        if t + 1 < terms:
            rem = rem - part.astype(F32)
    return out


@functools.partial(jax.custom_vjp, nondiff_argnums=(3,))
def _split_dot_vjp(x, mat, mat_t, terms):
    return _split_dot(x, mat, terms)


def _split_dot_vjp_fwd(x, mat, mat_t, terms):
    return _split_dot(x, mat, terms), mat_t


def _split_dot_vjp_bwd(terms, mat_t, g):
    return _split_dot(g, mat_t, terms), None, None


_split_dot_vjp.defvjp(_split_dot_vjp_fwd, _split_dot_vjp_bwd)


def _plain_split(x, mat, mat_t, terms):
    return _split_dot(x, mat, terms)


def _tri(after):
    j = lax.broadcasted_iota(jnp.int32, (BLOCK, BLOCK), 0)
    s = lax.broadcasted_iota(jnp.int32, (BLOCK, BLOCK), 1)
    return jnp.where(j > s if after else j < s, 1.0, 0.0).astype(BF16)


def _in(a, block, imap):
    return (a, block, imap)


def _out(shape, dtype, block, imap, acc=False):
    return (shape, dtype, block, imap, acc)


def tcall(fn, grid, ins, outs, name, scratch=None, side=None):
    nin = len(ins)
    nout = len(outs)
    ngrid = len(grid)

    def body(*refs):
        ids = tuple(pl.program_id(a) for a in range(ngrid))
        extra = {} if scratch is None else {"scratch": refs[nin + nout]}
        res = fn(ids, *[r[...] for r in refs[:nin]], **extra)
        first = ids[0] == 0
        for a in range(1, ngrid):
            first = jnp.logical_and(first, ids[a] == 0)
        for o_ref, r, spec in zip(refs[nin:nin + nout], res, outs):
            if spec[4]:
                @pl.when(first)
                def _(o_ref=o_ref):
                    o_ref[...] = jnp.zeros(o_ref.shape, o_ref.dtype)
                o_ref[...] += r.astype(o_ref.dtype)
            else:
                o_ref[...] = r.astype(o_ref.dtype)

    return _pcall(
        body, side=side, name=name, grid=grid,
        in_specs=[pl.BlockSpec(b, m) for (_, b, m) in ins],
        out_specs=[pl.BlockSpec(b, m) for (_, _, b, m, _) in outs],
        out_shape=[jax.ShapeDtypeStruct(s, d) for (s, d, _, _, _) in outs],
        scratch_shapes=[] if scratch is None else [pltpu.VMEM(*scratch)],
        compiler_params=_params(),
    )(*[a for (a, _, _) in ins])


def _to_strided(scr, nat, d):
    if d == 1:
        return nat
    t, w = nat.shape
    nc = w // BLOCK
    for c in range(nc):
        scr[c * t:(c + 1) * t, :] = nat[:, c * BLOCK:(c + 1) * BLOCK]
    return jnp.concatenate([scr[pl.ds(c * t + r, t // d, stride=d), :] for r in range(d) for c in range(nc)], axis=1)


def _to_natural(scr, st, d):
    if d == 1:
        return st.astype(F32)
    t, w = st.shape[0] * d, st.shape[1] // d
    nc = w // BLOCK
    st = st.astype(F32)
    for r in range(d):
        for c in range(nc):
            scr[pl.ds(c * t + r, t // d, stride=d), :] = st[:, r * w + c * BLOCK:r * w + (c + 1) * BLOCK]
    return jnp.concatenate([scr[c * t:(c + 1) * t, :] for c in range(nc)], axis=1)


def _row(a, tm, width=None, cb=0):
    width = a.shape[1] if width is None else width
    return _in(a, (tm, width), lambda i, cb=cb: (i, cb))


def _full(a):
    zeros = (0,) * a.ndim
    return _in(a, a.shape, lambda *ids: zeros)


def _row_out(n, width, dtype, tm):
    return _out((n, width), dtype, (tm, width), lambda i: (i, 0))


def _acc_out(shape):
    zeros = (0,) * len(shape)
    return _out(shape, F32, shape, lambda *ids: zeros, acc=True)


def mm(a, b, mode, name, *, out_dtype=None, scale=1.0, res=None, side=None):
    if out_dtype is None:
        out_dtype = BF16 if mode == "tn" else F32
    if mode == "nn":
        (m, k), (k2, n) = a.shape, b.shape
    elif mode == "nt":
        (m, k), (n, k2) = a.shape, b.shape
    else:
        (k, m), (k2, n) = a.shape, b.shape
    assert k == k2, (a.shape, b.shape, mode)
    tm, tn, tk = _tile(m, 1408 if mode == "tn" else 1024), _tile(n, 1408), _tile(k, 1408)
    nk = k // tk
    dims = {"nn": NN, "nt": NT, "tn": TN}[mode]
    has_res = res is not None

    def body(*refs):
        if has_res:
            a_ref, b_ref, r_ref, o_ref, acc_ref = refs
        else:
            a_ref, b_ref, o_ref, acc_ref = refs
        kk = pl.program_id(2)

        @pl.when(kk == 0)
        def _():
            acc_ref[...] = jnp.zeros(acc_ref.shape, F32)

        acc_ref[...] += _dot(a_ref[...], b_ref[...], dims)

        @pl.when(kk == nk - 1)
        def _():
            out = acc_ref[...]
            if scale != 1.0:
                out = out * scale
            if has_res:
                out = out + r_ref[...]
            o_ref[...] = out.astype(o_ref.dtype)

    a_spec = (pl.BlockSpec((tk, tm), lambda i, j, kk: (kk, i)) if mode == "tn"
              else pl.BlockSpec((tm, tk), lambda i, j, kk: (i, kk)))
    b_spec = (pl.BlockSpec((tn, tk), lambda i, j, kk: (j, kk)) if mode == "nt"
              else pl.BlockSpec((tk, tn), lambda i, j, kk: (kk, j)))
    in_specs = [a_spec, b_spec]
    args = [a, b]
    if has_res:
        in_specs.append(pl.BlockSpec((tm, tn), lambda i, j, kk: (i, j)))
        args.append(res)
    order = ("parallel", "parallel", "arbitrary") if side is None else ("arbitrary",) * 3
    return _pcall(
        body, side=side, name=name, grid=(m // tm, n // tn, nk),
        in_specs=in_specs,
        out_specs=pl.BlockSpec((tm, tn), lambda i, j, kk: (i, j)),
        out_shape=jax.ShapeDtypeStruct((m, n), out_dtype),
        scratch_shapes=[pltpu.VMEM((tm, tn), F32)],
        compiler_params=_params(dimension_semantics=order),
    )(*args)


def _rms(x, g):
    return x * lax.rsqrt(jnp.mean(x * x, axis=-1, keepdims=True) + RMS_EPS) * g


def _silu_mul(gate, up):
    return gate / (1.0 + jnp.exp(-gate)) * up


def mm_gate_up(h, w_gu, name, side=None):
    m, k = h.shape
    f = w_gu.shape[0] // 2
    tm, tn = _tile(m, 1024), _tile(f, 1408)
    nj = f // tn
    assert k <= 1408

    def body(h_ref, wg_ref, wu_ref, g_ref, u_ref, a_ref):
        ht = h_ref[...]
        for lo in range(0, tn, 512):
            cols = slice(lo, min(lo + 512, tn))
            gate, up = _dot(ht, wg_ref[cols, :], NT), _dot(ht, wu_ref[cols, :], NT)
            g_ref[:, cols] = gate.astype(g_ref.dtype)
            u_ref[:, cols] = up.astype(u_ref.dtype)
            a_ref[:, cols] = _silu_mul(gate, up).astype(a_ref.dtype)

    tile = pl.BlockSpec((tm, tn), lambda i, j: (i, j))
    return _pcall(
        body, side=side, name=name, grid=(m // tm, nj),
        in_specs=[pl.BlockSpec((tm, k), lambda i, j: (i, 0)),
                  pl.BlockSpec((tn, k), lambda i, j: (j, 0)),
                  pl.BlockSpec((tn, k), lambda i, j: (j + nj, 0))],
        out_specs=[tile, tile, tile],
        out_shape=[jax.ShapeDtypeStruct((m, f), BF16), jax.ShapeDtypeStruct((m, f), BF16),
                   jax.ShapeDtypeStruct((m, f), BF16)],
        compiler_params=_params(dimension_semantics=("arbitrary",) * 2),
    )(h, w_gu, w_gu)


def mm_down_act_bwd(dy, w_down, gate, up, name, side=None):
    m, d = dy.shape
    f = w_down.shape[0]
    tm, tn = _tile(m, 1024), _tile(f, 1408)
    assert d <= 1408

    def body(dy_ref, w_ref, g_ref, u_ref, dg_ref, du_ref):
        dyt = dy_ref[...].astype(BF16)
        for lo in range(0, tn, 512):
            cols = slice(lo, min(lo + 512, tn))
            da = _dot(dyt, w_ref[cols, :], NT) * 0.5
            gate, up = g_ref[:, cols].astype(F32), u_ref[:, cols].astype(F32)
            s = 1.0 / (1.0 + jnp.exp(-gate))
            gs = gate * s
            du_ref[:, cols] = (da * gs).astype(du_ref.dtype)
            dg_ref[:, cols] = (da * up * s * (1.0 + gate - gs)).astype(dg_ref.dtype)

    tile = pl.BlockSpec((tm, tn), lambda i, j: (i, j))
    return _pcall(
        body, side=side, name=name, grid=(m // tm, f // tn),
        in_specs=[pl.BlockSpec((tm, d), lambda i, j: (i, 0)), pl.BlockSpec((tn, d), lambda i, j: (j, 0)), tile, tile],
        out_specs=[tile, tile],
        out_shape=[jax.ShapeDtypeStruct((m, f), BF16), jax.ShapeDtypeStruct((m, f), BF16)],
        compiler_params=_params(dimension_semantics=("arbitrary", "arbitrary")),
    )(dy, w_down, gate, up)


def mm_norm_bwd(a, b, x, g, dres, name, b_kd=False, side=None):
    halves = isinstance(a, (tuple, list))
    a0, a1 = a if halves else (a, None)
    m, k = a0.shape[0], a0.shape[1] * (2 if halves else 1)
    d = b.shape[1] if b_kd else b.shape[0]
    dims = NN if b_kd else NT
    tm, tk = _tile(m, 512), _tile(a0.shape[1], 1408)
    nk = k // tk
    nkh = a0.shape[1] // tk
    has_res = dres is not None

    def body(*refs):
        a_ref, b_ref, x_ref, g_ref = refs[:4]
        rest = refs[4:-3]
        a1_ref = rest[0] if halves else None
        r_ref = rest[-1] if has_res else None
        dx_ref, dg_ref, acc_ref = refs[-3:]
        i, kk = pl.program_id(0), pl.program_id(1)

        @pl.when(kk == 0)
        def _():
            acc_ref[...] = jnp.zeros(acc_ref.shape, F32)

        if halves:
            @pl.when(kk < nkh)
            def _():
                acc_ref[...] += _dot(a_ref[...], b_ref[...], dims)

            @pl.when(kk >= nkh)
            def _():
                acc_ref[...] += _dot(a1_ref[...], b_ref[...], dims)
        else:
            acc_ref[...] += _dot(a_ref[...], b_ref[...], dims)

        @pl.when(kk == nk - 1)
        def _():
            _, vjp = jax.vjp(_rms, x_ref[...], g_ref[...])
            dx, dg = vjp(acc_ref[...])
            dx_ref[...] = dx + r_ref[...] if has_res else dx

            @pl.when(i == 0)
            def _():
                dg_ref[...] = jnp.zeros(dg_ref.shape, F32)

            dg_ref[...] += dg

    rows = pl.BlockSpec((tm, d), lambda i, kk: (i, 0))
    first = pl.BlockSpec((tm, tk), lambda i, kk: (i, jnp.minimum(kk, nkh - 1)))
    second = pl.BlockSpec((tm, tk), lambda i, kk: (i, jnp.maximum(kk - nkh, 0)))
    b_spec = pl.BlockSpec((tk, d), lambda i, kk: (kk, 0)) if b_kd else pl.BlockSpec((d, tk), lambda i, kk: (0, kk))
    in_specs = ([first, b_spec, rows, pl.BlockSpec(g.shape, lambda i, kk: (0, 0))]
                + ([second] if halves else []) + ([rows] if has_res else []))
    return _pcall(
        body, side=side, name=name, grid=(m // tm, nk),
        in_specs=in_specs,
        out_specs=[rows, pl.BlockSpec(g.shape, lambda i, kk: (0, 0))],
        out_shape=[jax.ShapeDtypeStruct((m, d), F32), jax.ShapeDtypeStruct(g.shape, F32)],
        scratch_shapes=[pltpu.VMEM((tm, d), F32)],
        compiler_params=_params(dimension_semantics=("arbitrary", "arbitrary")),
    )(*([a0, b, x, g] + ([a1] if halves else []) + ([dres] if has_res else [])))


def _indicator(shape, head_axis, mod):
    lane = lax.broadcasted_iota(jnp.int32, shape, head_axis)
    other = lax.broadcasted_iota(jnp.int32, shape, 1 - head_axis)
    lane = jnp.bitwise_and(lane, HEAD_DIM - 1) if mod else jnp.right_shift(lane, 6)
    return jnp.where(lane == other, 1.0, 0.0).astype(BF16)


def _head_rms(split, xs, g):
    w = xs.shape[1]
    to_head, from_head = _indicator((w, BLOCK), 0, False), _indicator((BLOCK, w), 1, False)
    to_lane, from_lane = _indicator((HEAD_DIM, w), 1, True), _indicator((w, HEAD_DIM), 0, True)
    ss = split(xs * xs, to_head, from_head, 3)
    r = lax.rsqrt(ss * (1.0 / HEAD_DIM) + RMS_EPS)
    g_all = split(jnp.broadcast_to(g, (8, HEAD_DIM)), to_lane, from_lane, 3)[0:1]
    return xs * split(r, from_head, to_head, 3) * g_all


def _prep(split, x, qg, kg, segs):
    parts = []
    for start, width, kind in segs:
        xs = x[:, start:start + width]
        parts.append(xs if kind == "raw" else _head_rms(split, xs, qg if kind == "q" else kg))
    return jnp.concatenate(parts, axis=1)


def prep_fwd(x, qg, kg, segs, dils, name):
    n, w = x.shape
    tm = _tile(n, 256, 8)

    def fn(ids, xt, a, b, scratch):
        ops = _prep(_plain_split, xt, a, b, segs)
        return tuple(_to_strided(scratch, ops, d) for d in dils)

    return tcall(fn, (n // tm,), [_row(x, tm), _full(qg), _full(kg)],
                 [_out((n // d, d * w), BF16, (tm // d, d * w), lambda i: (i, 0)) for d in dils], name,
                 scratch=((w // BLOCK * tm, BLOCK), F32))


def prep_bwd(x, qg, kg, segs, grads, gather, name):
    n, w = x.shape
    tm = BLOCK
    nblk = n // tm
    nslot = 1 + max(slot for _, _, _, slot in grads)

    def fn(ids, xt, a, b, *t, scratch):
        tiles, dils = [None] * nslot, [None] * nslot
        for ti, (_, sh, d, slot) in zip(t, grads):
            ti = jnp.where(ids[0] + sh < nblk, ti, 0.0) if sh else ti
            tiles[slot] = ti if tiles[slot] is None else tiles[slot] + ti
            dils[slot] = d
        tiles = [_to_natural(scratch, ti, d) for ti, d in zip(tiles, dils)]
        _, vjp = jax.vjp(lambda x_, a_, b_: _prep(_split_dot_vjp, x_, a_, b_, segs), xt, a, b)
        return vjp(gather(*tiles))

    specs = [_in(a, (tm // d, a.shape[1]), (lambda i, sh=sh: (jnp.minimum(i + sh, nblk - 1), 0)))
             for a, sh, d, _ in grads]
    wmax = max(a.shape[1] // d for a, _, d, _ in grads)
    return tcall(fn, (nblk,), [_row(x, tm), _full(qg), _full(kg)] + specs,
                 [_row_out(n, w, BF16, tm), _acc_out(qg.shape), _acc_out(kg.shape)], name,
                 scratch=((wmax // BLOCK * tm, BLOCK), F32))


def rmsnorm_fwd(x, g, name, side=None):
    n, d = x.shape
    tm = _tile(n, 512, 8)
    res = tcall(lambda ids, xt, gt: (_rms(xt, gt),), (n // tm,), [_row(x, tm), _full(g)],
                [_row_out(n, d, BF16, tm)], name, side=side)
    if side is None:
        return res[0]
    return res[0][0], res[1]


def ffn_fwd(x, g, w_gu, w_down, tag, carry=None):
    h = rmsnorm_fwd(x, g, tag + "_norm")
    if carry is None:
        gate, up, a = mm_gate_up(h, w_gu, tag + "_gu")
        return mm(a, w_down, "nn", tag + "_down", scale=0.5, res=x), (x, h, gate, up, a)
    phase, bufs = carry
    (gate, up, a), bufs = mm_gate_up(h, w_gu, tag + "_gu", side=gather_side(phase, bufs))
    y, bufs = mm(a, w_down, "nn", tag + "_down", scale=0.5, res=x, side=gather_side(phase + 1, bufs))
    return y, (x, h, gate, up, a), bufs


def ffn_bwd(dy, saved, g, w_gu, w_down, tag, chain=None):
    x, h, gate, up, a = saved

    def carrying(name, call, **kw):
        side = None if chain is None else chain.side(name)
        out = call(name=tag + "_" + name, side=side, **kw)
        if side is None:
            return out
        chain.done(name, out[1])
        return out[0]

    dgate, dup = carrying("da", mm_down_act_bwd, dy=dy, w_down=w_down, gate=gate, up=up)
    d_wdown = carrying("dwd", mm, a=a, b=dy, mode="tn", scale=0.5)
    d_wgu = (carrying("dwgu", mm, a=dgate, b=h, mode="tn"), mm(dup, h, "tn", tag + "_dwup"))
    dx, dg = carrying("dh", mm_norm_bwd, a=(dgate, dup), b=w_gu, x=x, g=g, dres=dy, b_kd=True)
    return dx, dg, d_wgu, d_wdown


def _alibi(n_heads):
    return [float(s) for s in np.asarray(2.0 ** (-8.0 * np.arange(1, n_heads + 1) / n_heads), dtype=np.float32)]


def _banded_tile(dot, first, q, kp, kc, vp, vc, sinks, *, hkv, grp, max_dist, step, slopes, want_lse):
    row = lax.broadcasted_iota(jnp.int32, (BLOCK, 2 * BLOCK), 0)
    col = lax.broadcasted_iota(jnp.int32, (BLOCK, 2 * BLOCK), 1)
    dist = row + BLOCK - col
    valid = (dist >= 0) & (dist <= max_dist) & ((col >= BLOCK) | jnp.logical_not(first))
    distf = dist.astype(F32)

    def head(hd, qh, k2, v2):
        s = dot(qh, k2, True) * (HEAD_DIM ** -0.5)
        s = jnp.where(valid, s - (slopes[hd] * step) * distf, NEG_BIG)
        m = jnp.max(s, axis=-1, keepdims=True)
        if sinks is not None:
            pick = lax.broadcasted_iota(jnp.int32, sinks.shape, 1) == hd
            sk = jnp.sum(jnp.where(pick, sinks, 0.0), axis=1, keepdims=True)
            m = jnp.maximum(m, sk)
        m = lax.stop_gradient(m)
        p = jnp.exp(s - m)
        denom = jnp.sum(p, axis=-1, keepdims=True)
        if sinks is not None:
            denom = denom + jnp.exp(sk - m)
        return dot(p * (1.0 / denom), v2, False), m + jnp.log(denom)

    outs, lses = [], []
    if grp == 1:
        low = lax.broadcasted_iota(jnp.int32, (BLOCK, BLOCK), 1) < HEAD_DIM
        for pr in range(hkv // 2):
            sl = slice(pr * BLOCK, (pr + 1) * BLOCK)
            q2 = q[:, sl]
            k2 = jnp.concatenate([kp[:, sl], kc[:, sl]], axis=0)
            v2 = jnp.concatenate([vp[:, sl], vc[:, sl]], axis=0)
            o0, l0 = head(2 * pr, jnp.where(low, q2, 0.0), k2, v2)
            o1, l1 = head(2 * pr + 1, jnp.where(low, 0.0, q2), k2, v2)
            outs.append(jnp.where(low, o0, o1))
            lses.append(jnp.where(low, l0, l1))
    else:
        for hk in range(hkv):
            sl = slice(hk * HEAD_DIM, (hk + 1) * HEAD_DIM)
            k2 = jnp.concatenate([kp[:, sl], kc[:, sl]], axis=0)
            v2 = jnp.concatenate([vp[:, sl], vc[:, sl]], axis=0)
            for gi in range(grp):
                hd = hk * grp + gi
                o_h, l_h = head(hd, q[:, hd * HEAD_DIM:(hd + 1) * HEAD_DIM], k2, v2)
                outs.append(o_h)
                lses.append(jnp.broadcast_to(l_h, (BLOCK, HEAD_DIM)))
    o = jnp.concatenate(outs, axis=1)
    if want_lse:
        return o, jnp.concatenate(lses, axis=1)
    return (o,)


def _banded_specs(view, qcol, kcol, vcol, wq, wkv):
    def at(colfn, prev):
        if prev:
            return lambda r, n: (jnp.maximum(n - 1, 0), colfn(r))
        return lambda r, n: (n, colfn(r))
    return [
        _in(view, (BLOCK, wq), at(qcol, False)),
        _in(view, (BLOCK, wkv), at(kcol, True)),
        _in(view, (BLOCK, wkv), at(kcol, False)),
        _in(view, (BLOCK, wkv), at(vcol, True)),
        _in(view, (BLOCK, wkv), at(vcol, False)),
    ]


def banded_fwd(view, dil, cols, sinks, cfg, name):
    ns = view.shape[0]
    nb = ns // BLOCK
    wq, wkv = cfg["hkv"] * cfg["grp"] * HEAD_DIM, cfg["hkv"] * HEAD_DIM
    has_sinks = sinks is not None

    def fn(ids, q, kp, kc, vp, vc, *rest):
        q, kp, kc, vp, vc = [a.astype(F32) for a in (q, kp, kc, vp, vc)]
        return _banded_tile(_plain_dot, ids[1] == 0, q, kp, kc, vp, vc, rest[0] if has_sinks else None, **cfg)

    ins = _banded_specs(view, *cols, wq, wkv) + ([_full(sinks)] if has_sinks else [])
    outs = [_out((ns, dil * wq), F32 if cfg["want_lse"] else BF16, (BLOCK, wq), lambda r, n: (n, r))]
    if cfg["want_lse"]:
        outs.append(_out((ns, dil * wq), F32, (BLOCK, wq), lambda r, n: (n, r)))
    return tcall(fn, (dil, nb), ins, outs, name)


def banded_bwd(view, dil, cols, sinks, cfg, cts, name):
    ns = view.shape[0]
    nb = ns // BLOCK
    wq, wkv = cfg["hkv"] * cfg["grp"] * HEAD_DIM, cfg["hkv"] * HEAD_DIM
    has_sinks = sinks is not None
    assert len(cts) == (2 if cfg["want_lse"] else 1)

    def fn(ids, q, kp, kc, vp, vc, *rest):
        sk = rest[0] if has_sinks else None
        ct = rest[1 if has_sinks else 0:]
        first = ids[1] == 0

        def f(q, kp, kc, vp, vc, *s):
            return _banded_tile(_dot_vjp, first, q, kp, kc, vp, vc, s[0] if has_sinks else None, **cfg)

        prim = tuple(a.astype(F32) for a in (q, kp, kc, vp, vc)) + ((sk,) if has_sinks else ())
        _, vjp = jax.vjp(f, *prim)
        return vjp(tuple(c.astype(F32) for c in ct))

    ins = (_banded_specs(view, *cols, wq, wkv) + ([_full(sinks)] if has_sinks else [])
           + [_in(a, (BLOCK, wq), (lambda r, n, cf=cf: (n, cf(r)))) for (a, cf) in cts])
    blk = lambda w: _out((ns, dil * w), F32, (BLOCK, w), lambda r, n: (n, r))
    outs = [blk(wq), blk(wkv), blk(wkv), blk(wkv), blk(wkv)]
    if has_sinks:
        outs.append(_acc_out(sinks.shape))
    return tcall(fn, (dil, nb), ins, outs, name)


def _log_sigmoid(z):
    return jnp.minimum(z, 0.0) - jnp.log(1.0 + jnp.exp(-jnp.abs(z)))


SB_PAIRS = 4


def _sb_pair(dot, suffix, qh, kb, vb, r_in, mask):
    z = dot(qh, kb, True) * (HEAD_DIM ** -0.5)
    lsp = _log_sigmoid(z)
    log_keep = jnp.where(mask, lsp - z, 0.0)
    log_after = suffix(log_keep) + r_in
    a = jnp.where(mask, jnp.exp(lsp + log_after), 0.0)
    return dot(a, vb, False), r_in + jnp.sum(log_keep, axis=1, keepdims=True)


def sb_fwd(qkv, qcb, kcb, vcb, name, side=None):
    s = qkv.shape[0]
    nb = s // BLOCK
    pairs = B_HEADS // 2
    wide = SB_PAIRS * BLOCK
    assert pairs % SB_PAIRS == 0 and qcb % SB_PAIRS == 0 and kcb % SB_PAIRS == 0 and vcb % SB_PAIRS == 0

    def body(q_ref, k_ref, v_ref, o_ref):
        n = pl.program_id(1)
        low = lax.broadcasted_iota(jnp.int32, (BLOCK, BLOCK), 1) < HEAD_DIM
        before = (lax.broadcasted_iota(jnp.int32, (2 * BLOCK, BLOCK), 1)
                  < jnp.bitwise_and(lax.broadcasted_iota(jnp.int32, (2 * BLOCK, BLOCK), 0), BLOCK - 1))
        after = _tri(True)
        suffix = lambda t: _split_dot(t, after)
        qs = []
        for p in range(SB_PAIRS):
            q2 = q_ref[:, p * BLOCK:(p + 1) * BLOCK].astype(F32)
            qs.append(jnp.concatenate([jnp.where(low, q2, 0.0), jnp.where(low, 0.0, q2)], axis=0))

        def cond(c):
            return jnp.logical_and(c[0] >= 0, c[1] > SB_SKIP_LOG)

        def step(c):
            kb, _, rs, accs = c
            rows = pl.ds(pl.multiple_of(kb * BLOCK, BLOCK), BLOCK)
            mask = jnp.logical_or(before, kb != n)
            new_r, new_acc, top = [], [], None
            for p in range(SB_PAIRS):
                cols = slice(p * BLOCK, (p + 1) * BLOCK)
                o_part, r_out = _sb_pair(_plain_dot, suffix, qs[p], k_ref[rows, cols], v_ref[rows, cols], rs[p], mask)
                new_r.append(r_out)
                new_acc.append(accs[p] + o_part)
                top = jnp.max(r_out) if top is None else jnp.maximum(top, jnp.max(r_out))
            return kb - 1, top, tuple(new_r), tuple(new_acc)

        init = (n, jnp.float32(0.0), tuple(jnp.zeros((2 * BLOCK, 1), F32) for _ in range(SB_PAIRS)),
                tuple(jnp.zeros((2 * BLOCK, BLOCK), F32) for _ in range(SB_PAIRS)))
        accs = lax.while_loop(cond, step, init)[3]
        for p in range(SB_PAIRS):
            o_ref[:, p * BLOCK:(p + 1) * BLOCK] = jnp.where(low, accs[p][:BLOCK], accs[p][BLOCK:]).astype(o_ref.dtype)

    return _pcall(
        body, side=side, name=name, grid=(pairs // SB_PAIRS, nb),
        in_specs=[pl.BlockSpec((BLOCK, wide), lambda g, n: (n, qcb // SB_PAIRS + g)),
                  pl.BlockSpec((s, wide), lambda g, n: (0, kcb // SB_PAIRS + g), pipeline_mode=pl.Buffered(1)),
                  pl.BlockSpec((s, wide), lambda g, n: (0, vcb // SB_PAIRS + g), pipeline_mode=pl.Buffered(1))],
        out_specs=pl.BlockSpec((BLOCK, wide), lambda g, n: (n, g)),
        out_shape=jax.ShapeDtypeStruct((s, pairs * BLOCK), BF16),
        compiler_params=_params(),
    )(qkv, qkv, qkv)


def sb_bwd(qkv, qcb, kcb, vcb, do, docb, name, side=None):
    s = qkv.shape[0]
    nb = s // BLOCK
    pairs = B_HEADS // 2
    wide = SB_PAIRS * BLOCK
    assert docb % SB_PAIRS == 0

    def body(q_ref, k_ref, v_ref, do_ref, dq_ref, dk_ref, dv_ref, r_ref):
        n = pl.program_id(1)

        @pl.when(n == 0)
        def _():
            dk_ref[...] = jnp.zeros(dk_ref.shape, F32)
            dv_ref[...] = jnp.zeros(dv_ref.shape, F32)

        low = lax.broadcasted_iota(jnp.int32, (BLOCK, BLOCK), 1) < HEAD_DIM
        before = (lax.broadcasted_iota(jnp.int32, (2 * BLOCK, BLOCK), 1)
                  < jnp.bitwise_and(lax.broadcasted_iota(jnp.int32, (2 * BLOCK, BLOCK), 0), BLOCK - 1))
        after, earlier = _tri(True), _tri(False)
        suffix = lambda t: _split_dot_vjp(t, after, earlier, 2)
        stack = lambda t: jnp.concatenate([jnp.where(low, t, 0.0), jnp.where(low, 0.0, t)], axis=0)
        qs = [stack(q_ref[:, p * BLOCK:(p + 1) * BLOCK].astype(F32)) for p in range(SB_PAIRS)]
        dos = [stack(do_ref[:, p * BLOCK:(p + 1) * BLOCK].astype(F32)) for p in range(SB_PAIRS)]

        def cond(c):
            return jnp.logical_and(c[0] >= 0, c[1] > SB_SKIP_LOG)

        def down(c):
            kb, _, rs = c
            rows = pl.ds(pl.multiple_of(kb * BLOCK, BLOCK), BLOCK)
            mask = jnp.logical_or(before, kb != n)
            new_r, top = [], None
            for h in range(SB_PAIRS):
                cols = slice(h * BLOCK, (h + 1) * BLOCK)
                r_ref[h, kb] = rs[h]
                z = _dot(qs[h], k_ref[rows, cols], NT) * (HEAD_DIM ** -0.5)
                log_keep = jnp.where(mask, _log_sigmoid(z) - z, 0.0)
                r_out = rs[h] + jnp.sum(log_keep, axis=1, keepdims=True)
                new_r.append(r_out)
                top = jnp.max(r_out) if top is None else jnp.maximum(top, jnp.max(r_out))
            return kb - 1, top, tuple(new_r)

        init = (n, jnp.float32(0.0), tuple(jnp.zeros((2 * BLOCK, 1), F32) for _ in range(SB_PAIRS)))
        last = lax.while_loop(cond, down, init)[0] + 1

        def up(kb, c):
            dqs, g_rs = c
            rows = pl.ds(pl.multiple_of(kb * BLOCK, BLOCK), BLOCK)
            mask = jnp.logical_or(before, kb != n)
            new_dq, new_g = [], []
            for h in range(SB_PAIRS):
                cols = slice(h * BLOCK, (h + 1) * BLOCK)
                _, vjp = jax.vjp(lambda q_, k_, v_, r_: _sb_pair(_dot_vjp, suffix, q_, k_, v_, r_, mask),
                                 qs[h], k_ref[rows, cols].astype(F32), v_ref[rows, cols].astype(F32), r_ref[h, kb])
                dq_c, dk_c, dv_c, g_in = vjp((dos[h], g_rs[h]))
                dk_ref[rows, cols] += dk_c
                dv_ref[rows, cols] += dv_c
                new_dq.append(dqs[h] + dq_c)
                new_g.append(g_in)
            return tuple(new_dq), tuple(new_g)

        init = (tuple(jnp.zeros((2 * BLOCK, BLOCK), F32) for _ in range(SB_PAIRS)),
                tuple(jnp.zeros((2 * BLOCK, 1), F32) for _ in range(SB_PAIRS)))
        dqs = lax.fori_loop(last, n + 1, up, init)[0]
        for p in range(SB_PAIRS):
            dq_ref[:, p * BLOCK:(p + 1) * BLOCK] = jnp.where(low, dqs[p][:BLOCK], dqs[p][BLOCK:])

    full = jax.ShapeDtypeStruct((s, pairs * BLOCK), F32)
    return _pcall(
        body, side=side, name=name, grid=(pairs // SB_PAIRS, nb),
        in_specs=[pl.BlockSpec((BLOCK, wide), lambda g, n: (n, qcb // SB_PAIRS + g)),
                  pl.BlockSpec((s, wide), lambda g, n: (0, kcb // SB_PAIRS + g), pipeline_mode=pl.Buffered(1)),
                  pl.BlockSpec((s, wide), lambda g, n: (0, vcb // SB_PAIRS + g), pipeline_mode=pl.Buffered(1)),
                  pl.BlockSpec((BLOCK, wide), lambda g, n: (n, docb // SB_PAIRS + g))],
        out_specs=[pl.BlockSpec((BLOCK, wide), lambda g, n: (n, g)),
                   pl.BlockSpec((s, wide), lambda g, n: (0, g), pipeline_mode=pl.Buffered(1)),
                   pl.BlockSpec((s, wide), lambda g, n: (0, g), pipeline_mode=pl.Buffered(1))],
        out_shape=[full, full, full],
        scratch_shapes=[pltpu.VMEM((SB_PAIRS, nb, 2 * BLOCK, 1), F32)],
        compiler_params=_params(),
    )(qkv, qkv, qkv, do)


def _xa_tile(dot, q, kv, qg, kg):
    hd = q.shape[1] // X_HEADS
    outs = []
    for h in range(X_HEADS):
        qh = _rms(q[:, h * hd:(h + 1) * hd], qg)
        kh = _rms(kv[:, h * hd:(h + 1) * hd], kg)
        vh = kv[:, (X_HEADS + h) * hd:(X_HEADS + h + 1) * hd]
        sc = dot(qh, kh, True) * (hd ** -0.5)
        m = lax.stop_gradient(jnp.max(sc, axis=-1, keepdims=True))
        p = jnp.exp(sc - m)
        outs.append(dot(p * (1.0 / jnp.sum(p, axis=-1, keepdims=True)), vh, False))
    return jnp.concatenate(outs, axis=1)


def xa_core_fwd(q, kv, qg, kg, name):
    n, d = q.shape
    tm = _tile(n, 256, 8)
    (o,) = tcall(lambda ids, qt, kvt, qgt, kgt: (_xa_tile(_plain_dot, qt, kvt, qgt, kgt),), (n // tm,),
                 [_row(q, tm), _full(kv), _full(qg), _full(kg)], [_row_out(n, d, BF16, tm)], name)
    return o


def xa_core_bwd(q, kv, qg, kg, do, name):
    n, d = q.shape
    tm = _tile(n, 256, 8)

    def fn(ids, qt, kvt, qgt, kgt, dot_):
        _, vjp = jax.vjp(functools.partial(_xa_tile, _dot_vjp), qt, kvt, qgt, kgt)
        return vjp(dot_.astype(F32))

    return tcall(fn, (n // tm,), [_row(q, tm), _full(kv), _full(qg), _full(kg), _row(do, tm)],
                 [_row_out(n, d, BF16, tm), _acc_out(kv.shape), _acc_out(qg.shape), _acc_out(kg.shape)], name)


def _ev_reorder(a):
    return jnp.concatenate([a[0:512], a[768:2304], a[512:768]], axis=0)


def _ev_restore(a):
    return jnp.concatenate([a[0:512], a[2048:2304], a[512:2048]], axis=0)


_EV_SEGS = ((0, 512, "q"), (512, 1536, "raw"), (2048, 128, "k"), (2176, 128, "raw"))
_A_CFG = dict(hkv=A_KV_HEADS, grp=A_Q_HEADS // A_KV_HEADS, max_dist=BLOCK - 1, step=1.0, slopes=_alibi(A_Q_HEADS),
              want_lse=False)
_A_COLS = (lambda r: 0, lambda r: 16, lambda r: 17)


def even_mixer_fwd(x, h, w_in, qg, kg, sinks, w_out, tag, side=None):
    qkv = mm(h, w_in, "nt", tag + "_in")
    (ops,) = prep_fwd(qkv, qg, kg, _EV_SEGS, (1,), tag + "_prep")
    (o_a,) = banded_fwd(ops, 1, _A_COLS, sinks, _A_CFG, tag + "_swa")
    o_b = sb_fwd(ops, 4, 8, 12, tag + "_sb", side=side)
    carried = None
    if side is not None:
        o_b, carried = o_b
    o = jnp.concatenate([o_a, o_b], axis=1)
    y = mm(o, w_out, "nn", tag + "_out", res=x)
    return y, (x, h, qkv, ops, o), carried


def even_mixer_bwd(dy, saved, g, w_in, qg, kg, sinks, w_out, tag, side=None, last_side=None):
    x, h, qkv, ops, o = saved
    do = mm(dy, w_out, "nt", tag + "_do", out_dtype=BF16)
    d_wout = mm(o, dy, "tn", tag + "_dwout")
    dqa, dkp, dkc, dvp, dvc, dsinks = banded_bwd(ops, 1, _A_COLS, sinks, _A_CFG, [(do, lambda r: 0)], tag + "_dswa")
    res = sb_bwd(ops, 4, 8, 12, do, 4, tag + "_dsb", side=side)
    carried = None
    if side is not None:
        res, carried = res
    dqb, dkb, dvb = res
    dqkv, dqg, dkg = prep_bwd(
        qkv, qg, kg, _EV_SEGS,
        [(dqa, 0, 1, 0), (dqb, 0, 1, 1), (dkb, 0, 1, 2), (dvb, 0, 1, 3), (dkc, 0, 1, 4), (dkp, 1, 1, 4), (dvc, 0, 1, 5),
         (dvp, 1, 1, 5)],
        lambda *t: jnp.concatenate(t, axis=1), tag + "_dqkv")
    d_win = mm(dqkv, h, "tn", tag + "_dwin")
    last = None if last_side is None else last_side(d_win, d_wout)
    if last is None:
        dx, dg = mm_norm_bwd(dqkv, w_in, x, g, dy, tag + "_dh", b_kd=True)
    else:
        (dx, dg), got = mm_norm_bwd(dqkv, w_in, x, g, dy, tag + "_dh", b_kd=True, side=last[0])
        last[1](got)
    return dx, dg, d_win, dqg, dkg, dsinks, d_wout, carried


def _c_cfg(window, dil):
    return dict(hkv=C_HEADS, grp=1, max_dist=window // dil, step=float(dil), slopes=_alibi(C_HEADS), want_lse=True)


_C_COLS = (lambda r: 3 * r, lambda r: 3 * r + 1, lambda r: 3 * r + 2)
_OD_SEGS = ((0, 1024, "q"), (1024, 1024, "k"), (2048, 1024, "raw"))


def _combine(o1, o2, o3, l1, l2, l3):
    m = lax.stop_gradient(jnp.maximum(jnp.maximum(l1, l2), l3))
    e1, e2, e3 = jnp.exp(l1 - m), jnp.exp(l2 - m), jnp.exp(l3 - m)
    tot = e1 + e2 + e3
    return (e1 / tot) * o1 + (e2 / tot) * o2 + (e3 / tot) * o3


def odd_mixer_fwd(x, g, w_in, qg, kg, w_out, tag):
    n, d = x.shape
    h = rmsnorm_fwd(x, g, tag + "_norm")
    qkv = mm(h, w_in, "nt", tag + "_in")
    dils = [dil for _, dil in C_PATTERNS]
    ops = prep_fwd(qkv, qg, kg, _OD_SEGS, dils, tag + "_prep")
    os_, ls_ = [], []
    for (window, dil), ops_d in zip(C_PATTERNS, ops):
        o_p, l_p = banded_fwd(ops_d, dil, _C_COLS, None, _c_cfg(window, dil), f"{tag}_dil{dil}")
        os_.append(o_p)
        ls_.append(l_p)
    tm = BLOCK
    lay = lambda a, dil: _in(a, (tm // dil, a.shape[1]), lambda i: (i, 0))
    views = [lay(a, dil) for a, dil in zip(os_ + ls_, dils + dils)]

    def comb(ids, *t, scratch):
        return (_combine(*[_to_natural(scratch, a, dil) for a, dil in zip(t, dils + dils)]),)

    (o,) = tcall(comb, (n // tm,), views, [_row_out(n, d, BF16, tm)], tag + "_comb",
                 scratch=((d // BLOCK * tm, BLOCK), F32))
    y = mm(o, w_out, "nn", tag + "_out", res=x)
    return y, (x, h, qkv, ops, views, o)


def odd_mixer_bwd(dy, saved, g, w_in, qg, kg, w_out, tag):
    x, h, qkv, ops, views, o = saved
    n, d = x.shape
    do = mm(dy, w_out, "nt", tag + "_do")
    d_wout = mm(o, dy, "tn", tag + "_dwout")
    tm = BLOCK
    dils = [dil for _, dil in C_PATTERNS]

    def comb_bwd(ids, *t, scratch):
        _, vjp = jax.vjp(_combine, *[_to_natural(scratch, a, dil) for a, dil in zip(t[:6], dils + dils)])
        return tuple(_to_strided(scratch, c, dil) for c, dil in zip(vjp(t[6]), dils + dils))

    cts = tcall(comb_bwd, (n // tm,), views + [_row(do, tm)],
                [_out((n // dil, dil * d), F32, (tm // dil, dil * d), lambda i: (i, 0)) for dil in dils + dils],
                tag + "_dcomb", scratch=((d // BLOCK * tm, BLOCK), F32))
    dqs, dks, dvs = [], [], []
    for p, ((window, dil), ops_d) in enumerate(zip(C_PATTERNS, ops)):
        dq, dkp, dkc, dvp, dvc = banded_bwd(ops_d, dil, _C_COLS, None, _c_cfg(window, dil),
                                            [(cts[p], lambda r: r), (cts[3 + p], lambda r: r)], f"{tag}_ddil{dil}")
        dqs.append((dq, 0, dil, p))
        dks += [(dkc, 0, dil, 3 + p), (dkp, dil, dil, 3 + p)]
        dvs += [(dvc, 0, dil, 6 + p), (dvp, dil, dil, 6 + p)]

    def gather(*t):
        return jnp.concatenate([t[0] + t[1] + t[2], t[3] + t[4] + t[5], t[6] + t[7] + t[8]], axis=1)

    dqkv, dqg, dkg = prep_bwd(qkv, qg, kg, _OD_SEGS, dqs + dks + dvs, gather, tag + "_dqkv")
    d_win = mm(dqkv, h, "tn", tag + "_dwin")
    dx, dg = mm_norm_bwd(dqkv, w_in, x, g, dy, tag + "_dh", b_kd=True)
    return dx, dg, d_win, dqg, dkg, d_wout


def xa_fwd(x, mem, g, gm, w_q, w_kv, qg, kg, w_o, tag):
    h = rmsnorm_fwd(x, g, tag + "_norm")
    q = mm(h, w_q, "nn", tag + "_q")
    mn = rmsnorm_fwd(mem, gm, tag + "_mnorm")
    kv = mm(mn, w_kv, "nt", tag + "_kv")
    o = xa_core_fwd(q, kv, qg, kg, tag + "_core")
    y = mm(o, w_o, "nn", tag + "_o", res=x)
    return y, (x, h, q, mn, kv, o)


def xa_bwd(dy, saved, mem, g, gm, w_q, w_kv, qg, kg, w_o, tag):
    x, h, q, mn, kv, o = saved
    do = mm(dy, w_o, "nt", tag + "_do", out_dtype=BF16)
    d_wo = mm(o, dy, "tn", tag + "_dwo")
    dq, dkv, dqg, dkg = xa_core_bwd(q, kv, qg, kg, do, tag + "_dcore")
    d_wq = mm(h, dq, "tn", tag + "_dwq")
    dx, dg = mm_norm_bwd(dq, w_q, x, g, dy, tag + "_dh")
    d_wkv = mm(dkv, mn, "tn", tag + "_dwkv")
    _, dgm = mm_norm_bwd(dkv, w_kv, mem, gm, None, tag + "_dmn", b_kd=True)
    return dx, dg, dgm, d_wq, d_wkv, dqg, dkg, d_wo


def loss_head(y, target, name):
    n, d = y.shape
    tm = _tile(n, 512, 8)

    def fn(ids, yt, tt):
        e = yt - tt
        return e * (1.0 / d), jnp.sum(e * e, axis=0, keepdims=True)

    return tcall(fn, (n // tm,), [_row(y, tm), _row(target, tm)], [_row_out(n, d, F32, tm), _acc_out((1, d))], name)


_ANY = pl.BlockSpec(memory_space=pl.ANY)


def all_gather_blocks(blocks):
    nb = len(blocks)

    def body(*refs):
        x_refs, out_refs = refs[:nb], refs[nb:2 * nb]
        send_sems, recv_sems, local_sems = refs[2 * nb:]
        x, y, c = lax.axis_index("x"), lax.axis_index("y"), lax.axis_index("c")
        me, sibling = (x, y, c), (x, y, 1 - c)
        over_x, over_y, diagonal = (1 - x, y), (x, 1 - y), (1 - x, 1 - y)
        relay_of = ((1 - x) * (1 - c) + x * c, y * (1 - c) + (1 - y) * c)
        relay_to = (x * (1 - c) + (1 - x) * c, (1 - y) * (1 - c) + y * c)

        def copy(b, k, blk, to, own=False):
            px, py, pc = blk
            slot = out_refs[b].at[4 * px + 2 * py + pc]
            return pltpu.make_async_remote_copy(
                src_ref=x_refs[b] if own else slot, dst_ref=slot,
                send_sem=send_sems.at[7 * b + k], recv_sem=recv_sems.at[7 * b + k], device_id=to, device_id_type=MESH)

        mine = [pltpu.make_async_copy(x_refs[b], out_refs[b].at[4 * x + 2 * y + c], local_sems.at[b]) for b in range(nb)]
        for cp in mine:
            cp.start()
        sent = []
        for b in range(nb):
            sent += [copy(b, 0, me, sibling, own=True), copy(b, 1, me, (*over_x, c), own=True),
                     copy(b, 2, me, (*over_y, c), own=True)]
        for cp in sent:
            cp.start()
        for b in range(nb):
            copy(b, 1, (*over_x, c), me).wait_recv()
            copy(b, 2, (*over_y, c), me).wait_recv()
            later = [copy(b, 3, (*relay_of, c), (*relay_to, c)), copy(b, 4, (*over_x, c), sibling),
                     copy(b, 5, (*over_y, c), sibling)]
            for cp in later:
                cp.start()
            sent += later
        for b in range(nb):
            copy(b, 3, (*diagonal, c), me).wait_recv()
            fwd = copy(b, 6, (*diagonal, c), sibling)
            fwd.start()
            sent.append(fwd)
        for b in range(nb):
            copy(b, 0, sibling, me).wait_recv()
            for k, chip in ((4, over_x), (5, over_y), (6, diagonal)):
                copy(b, k, (*chip, 1 - c), me).wait_recv()
        for cp in sent:
            cp.wait_send()
        for cp in mine:
            cp.wait()

    return _pcall(
        body, name="weights_all_gather",
        in_specs=[_ANY] * nb, out_specs=[_ANY] * nb,
        out_shape=[jax.ShapeDtypeStruct((N_DEV,) + a.shape, a.dtype) for a in blocks],
        scratch_shapes=[pltpu.SemaphoreType.DMA((7 * nb,)), pltpu.SemaphoreType.DMA((7 * nb,)),
                        pltpu.SemaphoreType.DMA((nb,))],
    )(*blocks)


def pair_exchange(bufs):
    nb = len(bufs)

    def body(*refs):
        srcs, dsts = refs[:nb], refs[nb:2 * nb]
        send_sems, recv_sems = refs[2 * nb:]
        x, y, c = lax.axis_index("x"), lax.axis_index("y"), lax.axis_index("c")
        copies = []
        for b in range(nb):
            for j in range(4):
                cp = pltpu.make_async_remote_copy(
                    src_ref=srcs[b].at[2 * j + (1 - c)], dst_ref=dsts[b].at[j], send_sem=send_sems.at[4 * b + j],
                    recv_sem=recv_sems.at[4 * b + j], device_id=(x, y, 1 - c), device_id_type=MESH)
                cp.start()
                copies.append(cp)
        for cp in copies:
            cp.wait()

    return _pcall(
        body, name="grads_pair_exchange",
        in_specs=[_ANY] * nb, out_specs=[_ANY] * nb,
        out_shape=[jax.ShapeDtypeStruct((4,) + a.shape[1:], a.dtype) for a in bufs],
        scratch_shapes=[pltpu.SemaphoreType.DMA((4 * nb,)), pltpu.SemaphoreType.DMA((4 * nb,))],
    )(*bufs)


def pair_sum(g, got, c, out_dtype, name):
    r, w = g.shape[1:]
    tr = _tile(r, 1024, 16)

    def body(c_ref, a_ref, b_ref, o_ref):
        o_ref[...] = (a_ref[...].astype(F32) + b_ref[...].astype(F32)).astype(o_ref.dtype)

    return _pcall(
        body, name=name,
        grid_spec=pltpu.PrefetchScalarGridSpec(
            num_scalar_prefetch=1, grid=(4, r // tr),
            in_specs=[pl.BlockSpec((None, tr, w), lambda j, i, c_ref: (2 * j + c_ref[0], i, 0)),
                      pl.BlockSpec((None, tr, w), lambda j, i, c_ref: (j, i, 0))],
            out_specs=pl.BlockSpec((None, tr, w), lambda j, i, c_ref: (j, i, 0))),
        out_shape=jax.ShapeDtypeStruct((4,) + g.shape[1:], out_dtype),
        compiler_params=_params(),
    )(c, g, got)


def chip_exchange(parts):
    nb = len(parts)

    def body(*refs):
        srcs, dsts = refs[:nb], refs[nb:2 * nb]
        send_sems, recv_sems, local_sems = refs[2 * nb:]
        x, y, c = lax.axis_index("x"), lax.axis_index("y"), lax.axis_index("c")
        my_chip = 2 * x + y
        copies = []
        for b in range(nb):
            mine = pltpu.make_async_copy(srcs[b].at[my_chip], dsts[b].at[my_chip], local_sems.at[b])
            mine.start()
            copies.append(mine)
            for k, (tx, ty) in enumerate([(1 - x, y), (x, 1 - y), (1 - x, 1 - y)]):
                cp = pltpu.make_async_remote_copy(
                    src_ref=srcs[b].at[2 * tx + ty], dst_ref=dsts[b].at[my_chip], send_sem=send_sems.at[3 * b + k],
                    recv_sem=recv_sems.at[3 * b + k], device_id=(tx, ty, c), device_id_type=MESH)
                cp.start()
                copies.append(cp)
        for cp in copies:
            cp.wait()

    return _pcall(
        body, name="grads_chip_exchange",
        in_specs=[_ANY] * nb, out_specs=[_ANY] * nb,
        out_shape=[jax.ShapeDtypeStruct(a.shape, a.dtype) for a in parts],
        scratch_shapes=[pltpu.SemaphoreType.DMA((3 * nb,)), pltpu.SemaphoreType.DMA((3 * nb,)),
                        pltpu.SemaphoreType.DMA((nb,))],
    )(*parts)


def chip_sum(parts, name):
    r, w = parts.shape[1:]
    tr = _tile(r, 1024, 16)
    spec = lambda j: _in(parts, (None, tr, w), lambda i, j=j: (j, i, 0))

    def fn(ids, a, b, c_, d):
        a, b, c_, d = [t.astype(F32) for t in (a, b, c_, d)]
        return (((a + b) + c_) + d,)

    (out,) = tcall(fn, (r // tr,), [spec(j) for j in range(4)],
                   [_out((r, w), F32, (tr, w), lambda i: (i, 0))], name)
    return out


def _remote(src, dst, send_sems, recv_sems, k, to):
    return functools.partial(pltpu.make_async_remote_copy, src_ref=src, dst_ref=dst, send_sem=send_sems.at[k],
                             recv_sem=recv_sems.at[k], device_id=to, device_id_type=MESH)


def _gather_plan(phase, nb):
    def plan(ins, outs, send_sems, recv_sems, local_sems):
        x, y, c = lax.axis_index("x"), lax.axis_index("y"), lax.axis_index("c")
        me, sibling = (x, y, c), (x, y, 1 - c)
        over_x, over_y, diagonal = (1 - x, y), (x, 1 - y), (1 - x, 1 - y)
        relay_of = ((1 - x) * (1 - c) + x * c, y * (1 - c) + (1 - y) * c)
        relay_to = (x * (1 - c) + (1 - x) * c, (1 - y) * (1 - c) + y * c)
        local, sends, recvs = [], [], []
        for b in range(nb):
            slot = lambda chip, core, b=b: outs[b].at[4 * chip[0] + 2 * chip[1] + core]
            if phase == 0:
                local.append(functools.partial(pltpu.make_async_copy, ins[b], slot((x, y), c), local_sems.at[b]))
                moves = [(ins[b], slot((x, y), c), to) for to in (sibling, (*over_x, c), (*over_y, c))]
                arrive = [slot((x, y), 1 - c), slot(over_x, c), slot(over_y, c)]
            elif phase == 1:
                moves = [(slot(relay_of, c), slot(relay_of, c), (*relay_to, c)),
                         (slot(over_x, c), slot(over_x, c), sibling), (slot(over_y, c), slot(over_y, c), sibling)]
                arrive = [slot(diagonal, c), slot(over_x, 1 - c), slot(over_y, 1 - c)]
            else:
                moves = [(slot(diagonal, c), slot(diagonal, c), sibling)]
                arrive = [slot(diagonal, 1 - c)]
            sends += [_remote(src, dst, send_sems, recv_sems, 3 * b + k, to) for k, (src, dst, to) in enumerate(moves)]
            recvs += [_remote(dst, dst, send_sems, recv_sems, 3 * b + k, me) for k, dst in enumerate(arrive)]
        return local, sends, recvs
    return plan


def gather_side(phase, arrays):
    nb = len(arrays)
    if phase == 0:
        shapes = [jax.ShapeDtypeStruct((N_DEV,) + a.shape, a.dtype) for a in arrays]
        return Side(arrays, shapes, 3 * nb, nb, _gather_plan(0, nb))
    shapes = [jax.ShapeDtypeStruct(a.shape, a.dtype) for a in arrays]
    return Side(arrays, shapes, 3 * nb, 0, _gather_plan(phase, nb), aliased=True)


def pair_side(bufs):
    nb = len(bufs)

    def plan(ins, outs, send_sems, recv_sems, local_sems):
        x, y, c = lax.axis_index("x"), lax.axis_index("y"), lax.axis_index("c")
        sends = [_remote(ins[b].at[2 * j + (1 - c)], outs[b].at[j], send_sems, recv_sems, 4 * b + j, (x, y, 1 - c))
                 for b in range(nb) for j in range(4)]
        recvs = [_remote(outs[b].at[j], outs[b].at[j], send_sems, recv_sems, 4 * b + j, (x, y, c))
                 for b in range(nb) for j in range(4)]
        return [], sends, recvs

    shapes = [jax.ShapeDtypeStruct((4,) + a.shape[1:], a.dtype) for a in bufs]
    return Side(bufs, shapes, 4 * nb, 0, plan)


def chip_side(parts):
    nb = len(parts)

    def plan(ins, outs, send_sems, recv_sems, local_sems):
        x, y, c = lax.axis_index("x"), lax.axis_index("y"), lax.axis_index("c")
        my_chip = 2 * x + y
        peers = [(1 - x, y), (x, 1 - y), (1 - x, 1 - y)]
        local = [functools.partial(pltpu.make_async_copy, ins[b].at[my_chip], outs[b].at[my_chip], local_sems.at[b])
                 for b in range(nb)]
        sends = [_remote(ins[b].at[2 * tx + ty], outs[b].at[my_chip], send_sems, recv_sems, 3 * b + k, (tx, ty, c))
                 for b in range(nb) for k, (tx, ty) in enumerate(peers)]
        recvs = [_remote(outs[b].at[2 * tx + ty], outs[b].at[2 * tx + ty], send_sems, recv_sems, 3 * b + k, (x, y, c))
                 for b in range(nb) for k, (tx, ty) in enumerate(peers)]
        return local, sends, recvs

    shapes = [jax.ShapeDtypeStruct(a.shape, a.dtype) for a in parts]
    return Side(parts, shapes, 3 * nb, nb, plan)


def adamw(w, g, m, v, name):
    shape = w.shape
    cols = shape[-1]
    rows = int(np.prod(shape[:-1]))
    w2, g2, m2, v2 = [a.reshape(rows, cols) for a in (w, g, m, v)]
    tr = _tile(rows, 512, 8) if rows % 8 == 0 else rows

    def fn(ids, wt, gt, mt, vt):
        m_new = ADAM_B1 * mt + (1.0 - ADAM_B1) * gt
        v_new = ADAM_B2 * vt + (1.0 - ADAM_B2) * (gt * gt)
        m_hat = m_new / (1.0 - ADAM_B1 ** ADAM_STEP)
        v_hat = v_new / (1.0 - ADAM_B2 ** ADAM_STEP)
        delta = -ADAM_LR * (m_hat / (jnp.sqrt(v_hat) + ADAM_EPS) + ADAM_WD * wt)
        return delta, m_new, v_new

    res = tcall(fn, (rows // tr,), [_row(a, tr) for a in (w2, g2, m2, v2)],
                [_row_out(rows, cols, F32, tr) for _ in range(3)], name)
    return [a.reshape(shape) for a in res]


_MATS = [("ffn1_w_gu", "col"), ("ffn1_w_down", "row"), ("ev_w_in", "col"), ("ev_w_out", "row"),
         ("od_w_in", "col"), ("od_w_out", "row"), ("xa_w_q", "row"), ("xa_w_kv", "col"), ("xa_w_o", "row"),
         ("ffn2_w_gu", "col"), ("ffn2_w_down", "row")]
_VECS = ["ffn1_norm", "mix_norm", "ev_q_gain", "ev_k_gain", "ev_sinks", "od_q_gain", "od_k_gain", "xa_norm",
         "xa_mem_norm", "xa_q_gain", "xa_k_gain", "ffn2_norm"]
_WEIGHTS = ["ffn1_norm", "ffn1_w_gu", "ffn1_w_down", "mix_norm", "ev_w_in", "ev_q_gain", "ev_k_gain", "ev_sinks",
            "ev_w_out", "od_w_in", "od_q_gain", "od_k_gain", "od_w_out", "xa_norm", "xa_mem_norm", "xa_w_q", "xa_w_kv",
            "xa_q_gain", "xa_k_gain", "xa_w_o", "ffn2_norm", "ffn2_w_gu", "ffn2_w_down"]


_AXIS = dict(_MATS)
DEPTH = 2


def _layer_groups(l):
    first, rest = _first_block_groups(l)
    return [first[0] + rest[0] + rest[1]]


def _first_block_groups(l):
    w_in, w_out = ("ev_w_in", "ev_w_out") if l % 2 == 0 else ("od_w_in", "od_w_out")
    first = [[("ffn1_w_gu", l), ("ffn1_w_down", l)]]
    rest = [[("ffn2_w_gu", l), ("xa_w_kv", l)],
            [(w_in, l // 2), ("ffn2_w_down", l), (w_out, l // 2), ("xa_w_q", l), ("xa_w_o", l)]]
    return first, rest


def _block_rows(shards, n):
    a, b = shards[n].shape[1:]
    return a if _AXIS[n] == "row" else b


def _weight_blocks(shards, groups):
    blocks = []
    for group in groups:
        rows = [(shards[n][j] if _AXIS[n] == "row" else shards[n][j].T).astype(BF16) for n, j in group]
        blocks.append(rows[0] if len(rows) == 1 else jnp.concatenate(rows, axis=0))
    return blocks


def _whole_weights(shards, groups, gathered):
    full = {}
    for group, got in zip(groups, gathered):
        off = 0
        for n, j in group:
            r = _block_rows(shards, n)
            full[n] = got[:, off:off + r, :].reshape(N_DEV * r, got.shape[2])
            off += r
    return full


def _gradient_buffers(grads, groups):
    bufs = []
    for group in groups:
        rows = []
        for n, _ in group:
            whole = jnp.concatenate(grads[n], axis=0) if isinstance(grads[n], tuple) else grads[n]
            rows.append(whole.reshape(N_DEV, whole.shape[0] // N_DEV, whole.shape[1]))
        bufs.append((rows[0] if len(rows) == 1 else jnp.concatenate(rows, axis=1)).astype(BF16))
    return bufs


def _gradient_blocks(shards, groups, sums):
    out = {}
    for group, tot in zip(groups, sums):
        off = 0
        for n, j in group:
            r = _block_rows(shards, n)
            out[n, j] = tot[off:off + r] if _AXIS[n] == "row" else tot[off:off + r].T
            off += r
    return out


class _PairChain:
    def __init__(self, ex, bufs):
        self.ex, self.bufs, self.parts = ex, bufs, None

    def side(self, name):
        return pair_side(self.bufs) if name == "da" else None

    def done(self, name, carried):
        self.parts = self.ex.pair_sums(self.bufs, carried, "l1")


class _RestChain:
    HALF = {"da": (0,), "dh": (1,)}

    def __init__(self, ex, bufs):
        self.ex, self.bufs, self.parts, self.sums = ex, bufs, None, [None] * len(bufs)

    def side(self, name):
        if name == "pair":
            return pair_side(self.bufs)
        if name in self.HALF:
            return chip_side([self.parts[i] for i in self.HALF[name]])
        return None

    def done(self, name, carried):
        if name == "pair":
            self.parts = self.ex.pair_sums(self.bufs, carried, "l0r")
        else:
            for i, tot in zip(self.HALF[name], self.ex.chip_sums(carried, "l0r_" + name)):
                self.sums[i] = tot


class _Exchange:
    def __init__(self, shards, c):
        self.shards, self.c = shards, c

    def weights_first(self):
        first, _ = _first_block_groups(0)
        return _whole_weights(self.shards, first, all_gather_blocks(_weight_blocks(self.shards, first)))

    def rest_blocks(self):
        return _weight_blocks(self.shards, _first_block_groups(0)[1])

    def weights_rest(self, gathered):
        return _whole_weights(self.shards, _first_block_groups(0)[1], gathered)

    def gather_start(self):
        return gather_side(0, _weight_blocks(self.shards, _layer_groups(1)))

    def weights_next(self, gathered):
        return _whole_weights(self.shards, _layer_groups(1), gathered)

    def chain_next(self, grads):
        return _PairChain(self, _gradient_buffers(grads, _layer_groups(1)))

    def chain_rest(self, grads):
        return _RestChain(self, _gradient_buffers(grads, _first_block_groups(0)[1]))

    def pair_sums(self, bufs, got, tag):
        return [pair_sum(b, g, self.c, b.dtype, f"grads_pair_sum_{tag}_{i}") for i, (b, g) in enumerate(zip(bufs, got))]

    def chip_sums(self, parts, tag):
        return [chip_sum(p, f"grads_chip_sum_{tag}_{i}") for i, p in enumerate(parts)]

    def finish(self, gm, gv, sums1, sums_rest):
        vecs = {n: jnp.concatenate(v, axis=0) for n, v in gv.items()}
        first, rest = _first_block_groups(0)
        bufs = _gradient_buffers(gm[0], first)
        vec = jnp.concatenate([vecs[n].reshape(-1) for n in _VECS])
        vec = jnp.pad(vec, (0, -vec.shape[0] % (16 * LANES)))
        bufs.append(jnp.broadcast_to(vec.reshape(1, -1, LANES), (N_DEV, vec.shape[0] // LANES, LANES)))
        parts = self.pair_sums(bufs, pair_exchange(bufs), "l0")
        sums0 = self.chip_sums(chip_exchange(parts), "l0")
        blocks = {**_gradient_blocks(self.shards, first, sums0[:-1]), **_gradient_blocks(self.shards, rest, sums_rest),
                  **_gradient_blocks(self.shards, _layer_groups(1), sums1)}
        out = {n: jnp.stack([blocks[n, j] for j in range(self.shards[n].shape[0])]) for n, _ in _MATS}
        flat, off = sums0[-1].reshape(-1), 0
        for n in _VECS:
            out[n] = flat[off:off + vecs[n].size].reshape(vecs[n].shape)
            off += vecs[n].size
        return out


class _NoExchange:
    def __init__(self, full):
        self.full = full

    def weights_first(self):
        return self.full[0]

    def rest_blocks(self):
        return None

    def gather_start(self):
        return None

    def weights_next(self, gathered):
        return self.full[1]

    def chain_next(self, grads):
        return None

    def chain_rest(self, grads):
        return None

    def finish(self, gm, gv, sums1, sums_rest):
        mats = {}
        for l in range(DEPTH):
            for group in _layer_groups(l):
                for n, j in group:
                    whole = jnp.concatenate(gm[l][n], axis=0) if isinstance(gm[l][n], tuple) else gm[l][n]
                    mats.setdefault(n, {})[j] = whole if _AXIS[n] == "row" else whole.T
        mats = {n: jnp.stack([v[j] for j in sorted(v)]) for n, v in mats.items()}
        return mats, {n: jnp.concatenate(v, axis=0) for n, v in gv.items()}


def _local_step(x, mem, target, w, ex):
    assert w["ffn1_norm"].shape[0] == DEPTH
    row = lambda a, l: a[l:l + 1]
    full = [ex.weights_first(), None]
    saved = []
    for l in range(DEPTH):
        t, j, f = f"l{l}", l // 2, full[l]
        rest = ex.rest_blocks() if l == 0 else None
        if rest is None:
            x, s1 = ffn_fwd(x, row(w["ffn1_norm"], l), f["ffn1_w_gu"], f["ffn1_w_down"], t + "_ffn1")
        else:
            x, s1, rest = ffn_fwd(x, row(w["ffn1_norm"], l), f["ffn1_w_gu"], f["ffn1_w_down"], t + "_ffn1", (0, rest))
        relay = None
        if l % 2 == 0:
            h = rmsnorm_fwd(x, row(w["mix_norm"], l), t + "_ev_norm", None if rest is None else gather_side(2, rest))
            if rest is not None:
                h, rest = h
                f = full[l] = {**f, **ex.weights_rest(rest)}
            side = ex.gather_start() if l + 1 < DEPTH else None
            x, s2, relay = even_mixer_fwd(x, h, _ev_reorder(f["ev_w_in"]), row(w["ev_q_gain"], j),
                                          row(w["ev_k_gain"], j), row(w["ev_sinks"], j), f["ev_w_out"], t + "_ev", side)
        else:
            x, s2 = odd_mixer_fwd(x, row(w["mix_norm"], l), f["od_w_in"], row(w["od_q_gain"], j),
                                  row(w["od_k_gain"], j), f["od_w_out"], t + "_od")
        x, s3 = xa_fwd(x, mem, row(w["xa_norm"], l), row(w["xa_mem_norm"], l), f["xa_w_q"], f["xa_w_kv"],
                       row(w["xa_q_gain"], l), row(w["xa_k_gain"], l), f["xa_w_o"], t + "_xa")
        if relay is None:
            x, s4 = ffn_fwd(x, row(w["ffn2_norm"], l), f["ffn2_w_gu"], f["ffn2_w_down"], t + "_ffn2")
        else:
            x, s4, relay = ffn_fwd(x, row(w["ffn2_norm"], l), f["ffn2_w_gu"], f["ffn2_w_down"], t + "_ffn2", (1, relay))
        if l + 1 < DEPTH:
            full[l + 1] = ex.weights_next(relay)
        saved.append((s1, s2, s3, s4))
    dx, sq = loss_head(x, target, "loss_head")
    loss = 0.5 * jnp.sum(sq) / x.shape[1]

    gm = [dict() for _ in range(DEPTH)]
    gv = {n: [None] * w[n].shape[0] for n in _VECS}
    chain1 = chain0 = sums1 = None
    started = []
    for l in reversed(range(DEPTH)):
        t, j, f = f"l{l}", l // 2, full[l]
        s1, s2, s3, s4 = saved[l]
        dx, gv["ffn2_norm"][l], gm[l]["ffn2_w_gu"], gm[l]["ffn2_w_down"] = ffn_bwd(
            dx, s4, row(w["ffn2_norm"], l), f["ffn2_w_gu"], f["ffn2_w_down"], t + "_ffn2", chain1 if l == 0 else None)
        parts = chain1.parts if l == 0 and chain1 is not None else None
        (dx, gv["xa_norm"][l], gv["xa_mem_norm"][l], gm[l]["xa_w_q"], gm[l]["xa_w_kv"], gv["xa_q_gain"][l],
         gv["xa_k_gain"][l], gm[l]["xa_w_o"]) = xa_bwd(
            dx, s3, mem, row(w["xa_norm"], l), row(w["xa_mem_norm"], l), f["xa_w_q"], f["xa_w_kv"],
            row(w["xa_q_gain"], l), row(w["xa_k_gain"], l), f["xa_w_o"], t + "_xa")
        if l % 2 == 0:
            def start_rest(d_win, d_wout, l=l):
                gm[l]["ev_w_in"], gm[l]["ev_w_out"] = _ev_restore(d_win), d_wout
                chain = ex.chain_rest(gm[l]) if l == 0 else None
                if chain is None:
                    return None
                started.append(chain)
                return chain.side("pair"), lambda got: chain.done("pair", got)

            (dx, gv["mix_norm"][l], d_win, gv["ev_q_gain"][j], gv["ev_k_gain"][j], gv["ev_sinks"][j],
             gm[l]["ev_w_out"], carried) = even_mixer_bwd(
                dx, s2, row(w["mix_norm"], l), _ev_reorder(f["ev_w_in"]), row(w["ev_q_gain"], j), row(w["ev_k_gain"], j),
                row(w["ev_sinks"], j), f["ev_w_out"], t + "_ev", None if parts is None else chip_side(parts), start_rest)
            gm[l]["ev_w_in"] = _ev_restore(d_win)
            if carried is not None:
                sums1 = ex.chip_sums(carried, "l1")
        else:
            (dx, gv["mix_norm"][l], gm[l]["od_w_in"], gv["od_q_gain"][j], gv["od_k_gain"][j],
             gm[l]["od_w_out"]) = odd_mixer_bwd(
                dx, s2, row(w["mix_norm"], l), f["od_w_in"], row(w["od_q_gain"], j), row(w["od_k_gain"], j),
                f["od_w_out"], t + "_od")
        if l == 0 and started:
            chain0 = started[0]
        dx, gv["ffn1_norm"][l], gm[l]["ffn1_w_gu"], gm[l]["ffn1_w_down"] = ffn_bwd(
            dx, s1, row(w["ffn1_norm"], l), f["ffn1_w_gu"], f["ffn1_w_down"], t + "_ffn1", chain0 if l == 0 else None)
        if l == 1:
            chain1 = ex.chain_next(gm[l])
    return loss, dx, ex.finish(gm, gv, sums1, None if chain0 is None else chain0.sums)


def kernel(x, mem, ffn1_norm, ffn1_w_gu, ffn1_w_down, mix_norm, ev_w_in, ev_q_gain, ev_k_gain, ev_sinks, ev_w_out, od_w_in, od_q_gain, od_k_gain, od_w_out, xa_norm, xa_mem_norm, xa_w_q, xa_w_kv, xa_q_gain, xa_k_gain, xa_w_o, ffn2_norm, ffn2_w_gu, ffn2_w_down, loss_target, m_ffn1_norm, m_ffn1_w_gu, m_ffn1_w_down, m_mix_norm, m_ev_w_in, m_ev_q_gain, m_ev_k_gain, m_ev_sinks, m_ev_w_out, m_od_w_in, m_od_q_gain, m_od_k_gain, m_od_w_out, m_xa_norm, m_xa_mem_norm, m_xa_w_q, m_xa_w_kv, m_xa_q_gain, m_xa_k_gain, m_xa_w_o, m_ffn2_norm, m_ffn2_w_gu, m_ffn2_w_down, v_ffn1_norm, v_ffn1_w_gu, v_ffn1_w_down, v_mix_norm, v_ev_w_in, v_ev_q_gain, v_ev_k_gain, v_ev_sinks, v_ev_w_out, v_od_w_in, v_od_q_gain, v_od_k_gain, v_od_w_out, v_xa_norm, v_xa_mem_norm, v_xa_w_q, v_xa_w_kv, v_xa_q_gain, v_xa_k_gain, v_xa_w_o, v_ffn2_norm, v_ffn2_w_gu, v_ffn2_w_down):
    w = dict(ffn1_norm=ffn1_norm, ffn1_w_gu=ffn1_w_gu, ffn1_w_down=ffn1_w_down, mix_norm=mix_norm, ev_w_in=ev_w_in, ev_q_gain=ev_q_gain, ev_k_gain=ev_k_gain, ev_sinks=ev_sinks, ev_w_out=ev_w_out, od_w_in=od_w_in, od_q_gain=od_q_gain, od_k_gain=od_k_gain, od_w_out=od_w_out, xa_norm=xa_norm, xa_mem_norm=xa_mem_norm, xa_w_q=xa_w_q, xa_w_kv=xa_w_kv, xa_q_gain=xa_q_gain, xa_k_gain=xa_k_gain, xa_w_o=xa_w_o, ffn2_norm=ffn2_norm, ffn2_w_gu=ffn2_w_gu, ffn2_w_down=ffn2_w_down)
    m = dict(ffn1_norm=m_ffn1_norm, ffn1_w_gu=m_ffn1_w_gu, ffn1_w_down=m_ffn1_w_down, mix_norm=m_mix_norm, ev_w_in=m_ev_w_in, ev_q_gain=m_ev_q_gain, ev_k_gain=m_ev_k_gain, ev_sinks=m_ev_sinks, ev_w_out=m_ev_w_out, od_w_in=m_od_w_in, od_q_gain=m_od_q_gain, od_k_gain=m_od_k_gain, od_w_out=m_od_w_out, xa_norm=m_xa_norm, xa_mem_norm=m_xa_mem_norm, xa_w_q=m_xa_w_q, xa_w_kv=m_xa_w_kv, xa_q_gain=m_xa_q_gain, xa_k_gain=m_xa_k_gain, xa_w_o=m_xa_w_o, ffn2_norm=m_ffn2_norm, ffn2_w_gu=m_ffn2_w_gu, ffn2_w_down=m_ffn2_w_down)
    v = dict(ffn1_norm=v_ffn1_norm, ffn1_w_gu=v_ffn1_w_gu, ffn1_w_down=v_ffn1_w_down, mix_norm=v_mix_norm, ev_w_in=v_ev_w_in, ev_q_gain=v_ev_q_gain, ev_k_gain=v_ev_k_gain, ev_sinks=v_ev_sinks, ev_w_out=v_ev_w_out, od_w_in=v_od_w_in, od_q_gain=v_od_q_gain, od_k_gain=v_od_k_gain, od_w_out=v_od_w_out, xa_norm=v_xa_norm, xa_mem_norm=v_xa_mem_norm, xa_w_q=v_xa_w_q, xa_w_kv=v_xa_w_kv, xa_q_gain=v_xa_q_gain, xa_k_gain=v_xa_k_gain, xa_w_o=v_xa_w_o, ffn2_norm=v_ffn2_norm, ffn2_w_gu=v_ffn2_w_gu, ffn2_w_down=v_ffn2_w_down)

    c = lax.axis_index("c").astype(jnp.int32).reshape(1)
    loss, dx, grads = _local_step(x[0], mem[0], loss_target[0], w, _Exchange(w, c))
    loss = lax.psum(loss, ("x", "y", "c"))

    delta, new_m, new_v = {}, {}, {}
    for n in _WEIGHTS:
        delta[n], new_m[n], new_v[n] = adamw(w[n], grads[n], m[n], v[n], "adamw_" + n)
    return (loss, dx[None], *[grads[n] for n in _WEIGHTS], *[delta[n] for n in _WEIGHTS],
            *[new_m[n] for n in _WEIGHTS], *[new_v[n] for n in _WEIGHTS])
```

```python
import functools

import numpy as np
import jax
import jax.numpy as jnp
from jax import lax
from jax.experimental import pallas as pl
from jax.experimental.pallas import tpu as pltpu

F32 = jnp.float32
BF16 = jnp.bfloat16
MESH = pl.DeviceIdType.MESH

HEAD_DIM = 64
BLOCK = 128
RMS_EPS = 1e-6
A_Q_HEADS, A_KV_HEADS = 8, 2
B_HEADS = 8
C_HEADS = 16
C_PATTERNS = ((128, 1), (512, 4), (2048, 16))
X_HEADS = 4
N_DEV = 8
LANES = 1024
VMEM_LIMIT_BYTES = 56 * 1024 * 1024
SB_SKIP_LOG = -110.0
NEG_BIG = -1e30

ADAM_LR, ADAM_B1, ADAM_B2, ADAM_EPS, ADAM_WD, ADAM_STEP = 0.001, 0.9, 0.999, 1e-08, 0.01, 10

NN = (((1,), (0,)), ((), ()))
NT = (((1,), (1,)), ((), ()))
TN = (((0,), (0,)), ((), ()))


class Side:
    def __init__(self, arrays, out_shapes, n_remote, n_local, plan, aliased=False):
        self.arrays, self.out_shapes, self.plan, self.aliased = list(arrays), list(out_shapes), plan, aliased
        self.sems = [pltpu.SemaphoreType.DMA((n_remote,)), pltpu.SemaphoreType.DMA((n_remote,)),
                     pltpu.SemaphoreType.DMA((max(n_local, 1),))]

    def start(self, ins, outs, sems):
        local, sends, _ = self.plan(ins, outs, *sems)
        for make in local + sends:
            make().start()

    def wait(self, ins, outs, sems):
        local, sends, recvs = self.plan(ins, outs, *sems)
        for make in sends:
            make().wait_send()
        for make in recvs:
            make().wait_recv()
        for make in local:
            make().wait()


def _pcall(body, side=None, **kw):
    if side is None:
        return pl.pallas_call(body, **kw)
    grid = kw["grid"]
    single = not isinstance(kw["out_specs"], (list, tuple))
    out_specs = [kw["out_specs"]] if single else list(kw["out_specs"])
    out_shape = [kw["out_shape"]] if single else list(kw["out_shape"])
    scratch = list(kw.get("scratch_shapes", []))
    n_in, n_out, n_scr, n_side = len(kw["in_specs"]), len(out_specs), len(scratch), len(side.arrays)
    n_sout = len(side.out_shapes)

    def hosted(*refs):
        ins, s_in = refs[:n_in], refs[n_in:n_in + n_side]
        outs = refs[n_in + n_side:n_in + n_side + n_out]
        s_out = refs[n_in + n_side + n_out:n_in + n_side + n_out + n_sout]
        rest = refs[n_in + n_side + n_out + n_sout:]
        scr, sems = rest[:n_scr], rest[n_scr:]
        first = last = None
        for a, size in enumerate(grid):
            f, l = pl.program_id(a) == 0, pl.program_id(a) == size - 1
            first = f if first is None else jnp.logical_and(first, f)
            last = l if last is None else jnp.logical_and(last, l)

        @pl.when(first)
        def _():
            side.start(s_in, s_out, sems)

        body(*ins, *outs, *scr)

        @pl.when(last)
        def _():
            side.wait(s_in, s_out, sems)

    any_space = pl.BlockSpec(memory_space=pl.ANY)
    kw2 = dict(kw)
    kw2.update(in_specs=list(kw["in_specs"]) + [any_space] * n_side, out_specs=out_specs + [any_space] * n_sout,
               out_shape=out_shape + side.out_shapes, scratch_shapes=scratch + side.sems)
    if side.aliased:
        kw2["input_output_aliases"] = {n_in + i: n_out + i for i in range(n_side)}
    call = pl.pallas_call(hosted, **kw2)

    def run(*args):
        res = call(*args, *side.arrays)
        return (res[0] if single else list(res[:n_out])), list(res[n_out:])

    return run


def _params(**kw):
    return pltpu.CompilerParams(vmem_limit_bytes=VMEM_LIMIT_BYTES, **kw)


def _tile(dim, cap, unit=128):
    if dim <= cap:
        return dim
    t = (cap // unit) * unit
    while t >= unit:
        if dim % t == 0:
            return t
        t -= unit
    raise ValueError(f"no tile for {dim} under {cap}")


def _dot(a, b, dims):
    return lax.dot_general(a.astype(BF16), b.astype(BF16), dims, preferred_element_type=F32)


@functools.partial(jax.custom_vjp, nondiff_argnums=(2,))
def _dot_vjp(a, b, nt):
    return _dot(a, b, NT if nt else NN)


def _dot_vjp_fwd(a, b, nt):
    return _dot(a, b, NT if nt else NN), (a.astype(BF16), b.astype(BF16))


def _dot_vjp_bwd(nt, res, g):
    a, b = res
    if nt:
        return _dot(g, b, NN), _dot(g, a, TN)
    return _dot(g, b, NT), _dot(a, g, TN)


_dot_vjp.defvjp(_dot_vjp_fwd, _dot_vjp_bwd)


def _plain_dot(a, b, nt):
    return _dot(a, b, NT if nt else NN)


def _split_dot(x, mat, terms=2):
    out, rem = None, x
    for t in range(terms):
        part = rem.astype(BF16)
        d = lax.dot_general(part, mat, NN, preferred_element_type=F32)
        out = d if out is None else out + d
        if t + 1 < terms:
            rem = rem - part.astype(F32)
    return out


@functools.partial(jax.custom_vjp, nondiff_argnums=(3,))
def _split_dot_vjp(x, mat, mat_t, terms):
    return _split_dot(x, mat, terms)


def _split_dot_vjp_fwd(x, mat, mat_t, terms):
    return _split_dot(x, mat, terms), mat_t


def _split_dot_vjp_bwd(terms, mat_t, g):
    return _split_dot(g, mat_t, terms), None, None


_split_dot_vjp.defvjp(_split_dot_vjp_fwd, _split_dot_vjp_bwd)


def _plain_split(x, mat, mat_t, terms):
    return _split_dot(x, mat, terms)


def _tri(after):
    j = lax.broadcasted_iota(jnp.int32, (BLOCK, BLOCK), 0)
    s = lax.broadcasted_iota(jnp.int32, (BLOCK, BLOCK), 1)
    return jnp.where(j > s if after else j < s, 1.0, 0.0).astype(BF16)


def _in(a, block, imap):
    return (a, block, imap)


def _out(shape, dtype, block, imap, acc=False):
    return (shape, dtype, block, imap, acc)


def tcall(fn, grid, ins, outs, name, scratch=None, side=None):
    nin = len(ins)
    nout = len(outs)
    ngrid = len(grid)

    def body(*refs):
        ids = tuple(pl.program_id(a) for a in range(ngrid))
        extra = {} if scratch is None else {"scratch": refs[nin + nout]}
        res = fn(ids, *[r[...] for r in refs[:nin]], **extra)
        first = ids[0] == 0
        for a in range(1, ngrid):
            first = jnp.logical_and(first, ids[a] == 0)
        for o_ref, r, spec in zip(refs[nin:nin + nout], res, outs):
            if spec[4]:
                @pl.when(first)
                def _(o_ref=o_ref):
                    o_ref[...] = jnp.zeros(o_ref.shape, o_ref.dtype)
                o_ref[...] += r.astype(o_ref.dtype)
            else:
                o_ref[...] = r.astype(o_ref.dtype)

    return _pcall(
        body, side=side, name=name, grid=grid,
        in_specs=[pl.BlockSpec(b, m) for (_, b, m) in ins],
        out_specs=[pl.BlockSpec(b, m) for (_, _, b, m, _) in outs],
        out_shape=[jax.ShapeDtypeStruct(s, d) for (s, d, _, _, _) in outs],
        scratch_shapes=[] if scratch is None else [pltpu.VMEM(*scratch)],
        compiler_params=_params(),
    )(*[a for (a, _, _) in ins])


def _to_strided(scr, nat, d):
    if d == 1:
        return nat
    t, w = nat.shape
    nc = w // BLOCK
    for c in range(nc):
        scr[c * t:(c + 1) * t, :] = nat[:, c * BLOCK:(c + 1) * BLOCK]
    return jnp.concatenate([scr[pl.ds(c * t + r, t // d, stride=d), :] for r in range(d) for c in range(nc)], axis=1)


def _to_natural(scr, st, d):
    if d == 1:
        return st.astype(F32)
    t, w = st.shape[0] * d, st.shape[1] // d
    nc = w // BLOCK
    st = st.astype(F32)
    for r in range(d):
        for c in range(nc):
            scr[pl.ds(c * t + r, t // d, stride=d), :] = st[:, r * w + c * BLOCK:r * w + (c + 1) * BLOCK]
    return jnp.concatenate([scr[c * t:(c + 1) * t, :] for c in range(nc)], axis=1)


def _row(a, tm, width=None, cb=0):
    width = a.shape[1] if width is None else width
    return _in(a, (tm, width), lambda i, cb=cb: (i, cb))


def _full(a):
    zeros = (0,) * a.ndim
    return _in(a, a.shape, lambda *ids: zeros)


def _row_out(n, width, dtype, tm):
    return _out((n, width), dtype, (tm, width), lambda i: (i, 0))


def _acc_out(shape):
    zeros = (0,) * len(shape)
    return _out(shape, F32, shape, lambda *ids: zeros, acc=True)


def mm(a, b, mode, name, *, out_dtype=None, scale=1.0, res=None, side=None):
    if out_dtype is None:
        out_dtype = BF16 if mode == "tn" else F32
    if mode == "nn":
        (m, k), (k2, n) = a.shape, b.shape
    elif mode == "nt":
        (m, k), (n, k2) = a.shape, b.shape
    else:
        (k, m), (k2, n) = a.shape, b.shape
    assert k == k2, (a.shape, b.shape, mode)
    tm, tn, tk = _tile(m, 1408 if mode == "tn" else 1024), _tile(n, 1408), _tile(k, 1408)
    nk = k // tk
    dims = {"nn": NN, "nt": NT, "tn": TN}[mode]
    has_res = res is not None

    def body(*refs):
        if has_res:
            a_ref, b_ref, r_ref, o_ref, acc_ref = refs
        else:
            a_ref, b_ref, o_ref, acc_ref = refs
        kk = pl.program_id(2)

        @pl.when(kk == 0)
        def _():
            acc_ref[...] = jnp.zeros(acc_ref.shape, F32)

        acc_ref[...] += _dot(a_ref[...], b_ref[...], dims)

        @pl.when(kk == nk - 1)
        def _():
            out = acc_ref[...]
            if scale != 1.0:
                out = out * scale
            if has_res:
                out = out + r_ref[...]
            o_ref[...] = out.astype(o_ref.dtype)

    a_spec = (pl.BlockSpec((tk, tm), lambda i, j, kk: (kk, i)) if mode == "tn"
              else pl.BlockSpec((tm, tk), lambda i, j, kk: (i, kk)))
    b_spec = (pl.BlockSpec((tn, tk), lambda i, j, kk: (j, kk)) if mode == "nt"
              else pl.BlockSpec((tk, tn), lambda i, j, kk: (kk, j)))
    in_specs = [a_spec, b_spec]
    args = [a, b]
    if has_res:
        in_specs.append(pl.BlockSpec((tm, tn), lambda i, j, kk: (i, j)))
        args.append(res)
    order = ("parallel", "parallel", "arbitrary") if side is None else ("arbitrary",) * 3
    return _pcall(
        body, side=side, name=name, grid=(m // tm, n // tn, nk),
        in_specs=in_specs,
        out_specs=pl.BlockSpec((tm, tn), lambda i, j, kk: (i, j)),
        out_shape=jax.ShapeDtypeStruct((m, n), out_dtype),
        scratch_shapes=[pltpu.VMEM((tm, tn), F32)],
        compiler_params=_params(dimension_semantics=order),
    )(*args)


def _rms(x, g):
    return x * lax.rsqrt(jnp.mean(x * x, axis=-1, keepdims=True) + RMS_EPS) * g


def _silu_mul(gate, up):
    return gate / (1.0 + jnp.exp(-gate)) * up


def mm_gate_up(h, w_gu, name, side=None):
    m, k = h.shape
    f = w_gu.shape[0] // 2
    tm, tn = _tile(m, 1024), _tile(f, 1408)
    nj = f // tn
    assert k <= 1408

    def body(h_ref, wg_ref, wu_ref, g_ref, u_ref, a_ref):
        ht = h_ref[...]
        for lo in range(0, tn, 512):
            cols = slice(lo, min(lo + 512, tn))
            gate, up = _dot(ht, wg_ref[cols, :], NT), _dot(ht, wu_ref[cols, :], NT)
            g_ref[:, cols] = gate.astype(g_ref.dtype)
            u_ref[:, cols] = up.astype(u_ref.dtype)
            a_ref[:, cols] = _silu_mul(gate, up).astype(a_ref.dtype)

    tile = pl.BlockSpec((tm, tn), lambda i, j: (i, j))
    return _pcall(
        body, side=side, name=name, grid=(m // tm, nj),
        in_specs=[pl.BlockSpec((tm, k), lambda i, j: (i, 0)),
                  pl.BlockSpec((tn, k), lambda i, j: (j, 0)),
                  pl.BlockSpec((tn, k), lambda i, j: (j + nj, 0))],
        out_specs=[tile, tile, tile],
        out_shape=[jax.ShapeDtypeStruct((m, f), BF16), jax.ShapeDtypeStruct((m, f), BF16),
                   jax.ShapeDtypeStruct((m, f), BF16)],
        compiler_params=_params(dimension_semantics=("arbitrary",) * 2),
    )(h, w_gu, w_gu)


def mm_down_act_bwd(dy, w_down, gate, up, name, side=None):
    m, d = dy.shape
    f = w_down.shape[0]
    tm, tn = _tile(m, 1024), _tile(f, 1408)
    assert d <= 1408

    def body(dy_ref, w_ref, g_ref, u_ref, dg_ref, du_ref):
        dyt = dy_ref[...].astype(BF16)
        for lo in range(0, tn, 512):
            cols = slice(lo, min(lo + 512, tn))
            da = _dot(dyt, w_ref[cols, :], NT) * 0.5
            gate, up = g_ref[:, cols].astype(F32), u_ref[:, cols].astype(F32)
            s = 1.0 / (1.0 + jnp.exp(-gate))
            gs = gate * s
            du_ref[:, cols] = (da * gs).astype(du_ref.dtype)
            dg_ref[:, cols] = (da * up * s * (1.0 + gate - gs)).astype(dg_ref.dtype)

    tile = pl.BlockSpec((tm, tn), lambda i, j: (i, j))
    return _pcall(
        body, side=side, name=name, grid=(m // tm, f // tn),
        in_specs=[pl.BlockSpec((tm, d), lambda i, j: (i, 0)), pl.BlockSpec((tn, d), lambda i, j: (j, 0)), tile, tile],
        out_specs=[tile, tile],
        out_shape=[jax.ShapeDtypeStruct((m, f), BF16), jax.ShapeDtypeStruct((m, f), BF16)],
        compiler_params=_params(dimension_semantics=("arbitrary", "arbitrary")),
    )(dy, w_down, gate, up)


def mm_norm_bwd(a, b, x, g, dres, name, b_kd=False, side=None):
    halves = isinstance(a, (tuple, list))
    a0, a1 = a if halves else (a, None)
    m, k = a0.shape[0], a0.shape[1] * (2 if halves else 1)
    d = b.shape[1] if b_kd else b.shape[0]
    dims = NN if b_kd else NT
    tm, tk = _tile(m, 512 if halves else 1024), _tile(a0.shape[1], 1408)
    nk = k // tk
    nkh = a0.shape[1] // tk
    has_res = dres is not None

    def body(*refs):
        a_ref, b_ref, x_ref, g_ref = refs[:4]
        rest = refs[4:-3]
        a1_ref = rest[0] if halves else None
        r_ref = rest[-1] if has_res else None
        dx_ref, dg_ref, acc_ref = refs[-3:]
        i, kk = pl.program_id(0), pl.program_id(1)

        @pl.when(kk == 0)
        def _():
            acc_ref[...] = jnp.zeros(acc_ref.shape, F32)

        if halves:
            @pl.when(kk < nkh)
            def _():
                acc_ref[...] += _dot(a_ref[...], b_ref[...], dims)

            @pl.when(kk >= nkh)
            def _():
                acc_ref[...] += _dot(a1_ref[...], b_ref[...], dims)
        else:
            acc_ref[...] += _dot(a_ref[...], b_ref[...], dims)

        @pl.when(kk == nk - 1)
        def _():
            _, vjp = jax.vjp(_rms, x_ref[...], g_ref[...])
            dx, dg = vjp(acc_ref[...])
            dx_ref[...] = dx + r_ref[...] if has_res else dx

            @pl.when(i == 0)
            def _():
                dg_ref[...] = jnp.zeros(dg_ref.shape, F32)

            dg_ref[...] += dg

    rows = pl.BlockSpec((tm, d), lambda i, kk: (i, 0))
    first = pl.BlockSpec((tm, tk), lambda i, kk: (i, jnp.minimum(kk, nkh - 1)))
    second = pl.BlockSpec((tm, tk), lambda i, kk: (i, jnp.maximum(kk - nkh, 0)))
    b_spec = pl.BlockSpec((tk, d), lambda i, kk: (kk, 0)) if b_kd else pl.BlockSpec((d, tk), lambda i, kk: (0, kk))
    in_specs = ([first, b_spec, rows, pl.BlockSpec(g.shape, lambda i, kk: (0, 0))]
                + ([second] if halves else []) + ([rows] if has_res else []))
    return _pcall(
        body, side=side, name=name, grid=(m // tm, nk),
        in_specs=in_specs,
        out_specs=[rows, pl.BlockSpec(g.shape, lambda i, kk: (0, 0))],
        out_shape=[jax.ShapeDtypeStruct((m, d), F32), jax.ShapeDtypeStruct(g.shape, F32)],
        scratch_shapes=[pltpu.VMEM((tm, d), F32)],
        compiler_params=_params(dimension_semantics=("arbitrary", "arbitrary")),
    )(*([a0, b, x, g] + ([a1] if halves else []) + ([dres] if has_res else [])))


def _indicator(shape, head_axis, mod):
    lane = lax.broadcasted_iota(jnp.int32, shape, head_axis)
    other = lax.broadcasted_iota(jnp.int32, shape, 1 - head_axis)
    lane = jnp.bitwise_and(lane, HEAD_DIM - 1) if mod else jnp.right_shift(lane, 6)
    return jnp.where(lane == other, 1.0, 0.0).astype(BF16)


def _head_rms(split, xs, g):
    w = xs.shape[1]
    to_head, from_head = _indicator((w, BLOCK), 0, False), _indicator((BLOCK, w), 1, False)
    to_lane, from_lane = _indicator((HEAD_DIM, w), 1, True), _indicator((w, HEAD_DIM), 0, True)
    ss = split(xs * xs, to_head, from_head, 3)
    r = lax.rsqrt(ss * (1.0 / HEAD_DIM) + RMS_EPS)
    g_all = split(jnp.broadcast_to(g, (8, HEAD_DIM)), to_lane, from_lane, 3)[0:1]
    return xs * split(r, from_head, to_head, 3) * g_all


def _prep(split, x, qg, kg, segs):
    parts = []
    for start, width, kind in segs:
        xs = x[:, start:start + width]
        parts.append(xs if kind == "raw" else _head_rms(split, xs, qg if kind == "q" else kg))
    return jnp.concatenate(parts, axis=1)


def prep_fwd(x, qg, kg, segs, dils, name):
    n, w = x.shape
    tm = _tile(n, 256, 8)

    def fn(ids, xt, a, b, scratch):
        ops = _prep(_plain_split, xt, a, b, segs)
        return tuple(_to_strided(scratch, ops, d) for d in dils)

    return tcall(fn, (n // tm,), [_row(x, tm), _full(qg), _full(kg)],
                 [_out((n // d, d * w), BF16, (tm // d, d * w), lambda i: (i, 0)) for d in dils], name,
                 scratch=((w // BLOCK * tm, BLOCK), F32))


def prep_bwd(x, qg, kg, segs, grads, gather, name):
    n, w = x.shape
    tm = BLOCK
    nblk = n // tm
    nslot = 1 + max(slot for _, _, _, slot in grads)

    def fn(ids, xt, a, b, *t, scratch):
        tiles, dils = [None] * nslot, [None] * nslot
        for ti, (_, sh, d, slot) in zip(t, grads):
            ti = jnp.where(ids[0] + sh < nblk, ti, 0.0) if sh else ti
            tiles[slot] = ti if tiles[slot] is None else tiles[slot] + ti
            dils[slot] = d
        tiles = [_to_natural(scratch, ti, d) for ti, d in zip(tiles, dils)]
        _, vjp = jax.vjp(lambda x_, a_, b_: _prep(_split_dot_vjp, x_, a_, b_, segs), xt, a, b)
        return vjp(gather(*tiles))

    specs = [_in(a, (tm // d, a.shape[1]), (lambda i, sh=sh: (jnp.minimum(i + sh, nblk - 1), 0)))
             for a, sh, d, _ in grads]
    wmax = max(a.shape[1] // d for a, _, d, _ in grads)
    return tcall(fn, (nblk,), [_row(x, tm), _full(qg), _full(kg)] + specs,
                 [_row_out(n, w, BF16, tm), _acc_out(qg.shape), _acc_out(kg.shape)], name,
                 scratch=((wmax // BLOCK * tm, BLOCK), F32))


def rmsnorm_fwd(x, g, name, side=None):
    n, d = x.shape
    tm = _tile(n, 512, 8)
    res = tcall(lambda ids, xt, gt: (_rms(xt, gt),), (n // tm,), [_row(x, tm), _full(g)],
                [_row_out(n, d, BF16, tm)], name, side=side)
    if side is None:
        return res[0]
    return res[0][0], res[1]


def ffn_fwd(x, g, w_gu, w_down, tag, carry=None):
    h = rmsnorm_fwd(x, g, tag + "_norm")
    if carry is None:
        gate, up, a = mm_gate_up(h, w_gu, tag + "_gu")
        return mm(a, w_down, "nn", tag + "_down", scale=0.5, res=x), (x, h, gate, up, a)
    phase, bufs = carry
    (gate, up, a), bufs = mm_gate_up(h, w_gu, tag + "_gu", side=gather_side(phase, bufs))
    y, bufs = mm(a, w_down, "nn", tag + "_down", scale=0.5, res=x, side=gather_side(phase + 1, bufs))
    return y, (x, h, gate, up, a), bufs


def ffn_bwd(dy, saved, g, w_gu, w_down, tag, chain=None):
    x, h, gate, up, a = saved

    def carrying(name, call, **kw):
        side = None if chain is None else chain.side(name)
        out = call(name=tag + "_" + name, side=side, **kw)
        if side is None:
            return out
        chain.done(name, out[1])
        return out[0]

    dgate, dup = carrying("da", mm_down_act_bwd, dy=dy, w_down=w_down, gate=gate, up=up)
    d_wdown = carrying("dwd", mm, a=a, b=dy, mode="tn", scale=0.5)
    d_wgu = (carrying("dwgu", mm, a=dgate, b=h, mode="tn"), mm(dup, h, "tn", tag + "_dwup"))
    dx, dg = carrying("dh", mm_norm_bwd, a=(dgate, dup), b=w_gu, x=x, g=g, dres=dy, b_kd=True)
    return dx, dg, d_wgu, d_wdown


def _alibi(n_heads):
    return [float(s) for s in np.asarray(2.0 ** (-8.0 * np.arange(1, n_heads + 1) / n_heads), dtype=np.float32)]


def _banded_tile(dot, first, q, kp, kc, vp, vc, sinks, *, hkv, grp, max_dist, step, slopes, want_lse):
    row = lax.broadcasted_iota(jnp.int32, (BLOCK, 2 * BLOCK), 0)
    col = lax.broadcasted_iota(jnp.int32, (BLOCK, 2 * BLOCK), 1)
    dist = row + BLOCK - col
    valid = (dist >= 0) & (dist <= max_dist) & ((col >= BLOCK) | jnp.logical_not(first))
    distf = dist.astype(F32)

    def head(hd, qh, k2, v2):
        s = dot(qh, k2, True) * (HEAD_DIM ** -0.5)
        s = jnp.where(valid, s - (slopes[hd] * step) * distf, NEG_BIG)
        m = jnp.max(s, axis=-1, keepdims=True)
        if sinks is not None:
            pick = lax.broadcasted_iota(jnp.int32, sinks.shape, 1) == hd
            sk = jnp.sum(jnp.where(pick, sinks, 0.0), axis=1, keepdims=True)
            m = jnp.maximum(m, sk)
        m = lax.stop_gradient(m)
        p = jnp.exp(s - m)
        denom = jnp.sum(p, axis=-1, keepdims=True)
        if sinks is not None:
            denom = denom + jnp.exp(sk - m)
        return dot(p * (1.0 / denom), v2, False), m + jnp.log(denom)

    outs, lses = [], []
    if grp == 1:
        low = lax.broadcasted_iota(jnp.int32, (BLOCK, BLOCK), 1) < HEAD_DIM
        for pr in range(hkv // 2):
            sl = slice(pr * BLOCK, (pr + 1) * BLOCK)
            q2 = q[:, sl]
            k2 = jnp.concatenate([kp[:, sl], kc[:, sl]], axis=0)
            v2 = jnp.concatenate([vp[:, sl], vc[:, sl]], axis=0)
            o0, l0 = head(2 * pr, jnp.where(low, q2, 0.0), k2, v2)
            o1, l1 = head(2 * pr + 1, jnp.where(low, 0.0, q2), k2, v2)
            outs.append(jnp.where(low, o0, o1))
            lses.append(jnp.where(low, l0, l1))
    else:
        for hk in range(hkv):
            sl = slice(hk * HEAD_DIM, (hk + 1) * HEAD_DIM)
            k2 = jnp.concatenate([kp[:, sl], kc[:, sl]], axis=0)
            v2 = jnp.concatenate([vp[:, sl], vc[:, sl]], axis=0)
            for gi in range(grp):
                hd = hk * grp + gi
                o_h, l_h = head(hd, q[:, hd * HEAD_DIM:(hd + 1) * HEAD_DIM], k2, v2)
                outs.append(o_h)
                lses.append(jnp.broadcast_to(l_h, (BLOCK, HEAD_DIM)))
    o = jnp.concatenate(outs, axis=1)
    if want_lse:
        return o, jnp.concatenate(lses, axis=1)
    return (o,)


def _banded_specs(view, qcol, kcol, vcol, wq, wkv):
    def at(colfn, prev):
        if prev:
            return lambda r, n: (jnp.maximum(n - 1, 0), colfn(r))
        return lambda r, n: (n, colfn(r))
    return [
        _in(view, (BLOCK, wq), at(qcol, False)),
        _in(view, (BLOCK, wkv), at(kcol, True)),
        _in(view, (BLOCK, wkv), at(kcol, False)),
        _in(view, (BLOCK, wkv), at(vcol, True)),
        _in(view, (BLOCK, wkv), at(vcol, False)),
    ]


def banded_fwd(view, dil, cols, sinks, cfg, name):
    ns = view.shape[0]
    nb = ns // BLOCK
    wq, wkv = cfg["hkv"] * cfg["grp"] * HEAD_DIM, cfg["hkv"] * HEAD_DIM
    has_sinks = sinks is not None

    def fn(ids, q, kp, kc, vp, vc, *rest):
        q, kp, kc, vp, vc = [a.astype(F32) for a in (q, kp, kc, vp, vc)]
        return _banded_tile(_plain_dot, ids[1] == 0, q, kp, kc, vp, vc, rest[0] if has_sinks else None, **cfg)

    ins = _banded_specs(view, *cols, wq, wkv) + ([_full(sinks)] if has_sinks else [])
    outs = [_out((ns, dil * wq), F32 if cfg["want_lse"] else BF16, (BLOCK, wq), lambda r, n: (n, r))]
    if cfg["want_lse"]:
        outs.append(_out((ns, dil * wq), F32, (BLOCK, wq), lambda r, n: (n, r)))
    return tcall(fn, (dil, nb), ins, outs, name)


def banded_bwd(view, dil, cols, sinks, cfg, cts, name):
    ns = view.shape[0]
    nb = ns // BLOCK
    wq, wkv = cfg["hkv"] * cfg["grp"] * HEAD_DIM, cfg["hkv"] * HEAD_DIM
    has_sinks = sinks is not None
    assert len(cts) == (2 if cfg["want_lse"] else 1)

    def fn(ids, q, kp, kc, vp, vc, *rest):
        sk = rest[0] if has_sinks else None
        ct = rest[1 if has_sinks else 0:]
        first = ids[1] == 0

        def f(q, kp, kc, vp, vc, *s):
            return _banded_tile(_dot_vjp, first, q, kp, kc, vp, vc, s[0] if has_sinks else None, **cfg)

        prim = tuple(a.astype(F32) for a in (q, kp, kc, vp, vc)) + ((sk,) if has_sinks else ())
        _, vjp = jax.vjp(f, *prim)
        return vjp(tuple(c.astype(F32) for c in ct))

    ins = (_banded_specs(view, *cols, wq, wkv) + ([_full(sinks)] if has_sinks else [])
           + [_in(a, (BLOCK, wq), (lambda r, n, cf=cf: (n, cf(r)))) for (a, cf) in cts])
    blk = lambda w: _out((ns, dil * w), F32, (BLOCK, w), lambda r, n: (n, r))
    outs = [blk(wq), blk(wkv), blk(wkv), blk(wkv), blk(wkv)]
    if has_sinks:
        outs.append(_acc_out(sinks.shape))
    return tcall(fn, (dil, nb), ins, outs, name)


def _log_sigmoid(z):
    return jnp.minimum(z, 0.0) - jnp.log(1.0 + jnp.exp(-jnp.abs(z)))


SB_PAIRS = 4


def _sb_pair(dot, suffix, qh, kb, vb, r_in, mask):
    z = dot(qh, kb, True) * (HEAD_DIM ** -0.5)
    lsp = _log_sigmoid(z)
    log_keep = jnp.where(mask, lsp - z, 0.0)
    log_after = suffix(log_keep) + r_in
    a = jnp.where(mask, jnp.exp(lsp + log_after), 0.0)
    return dot(a, vb, False), r_in + jnp.sum(log_keep, axis=1, keepdims=True)


def sb_fwd(qkv, qcb, kcb, vcb, name, side=None):
    s = qkv.shape[0]
    nb = s // BLOCK
    pairs = B_HEADS // 2
    wide = SB_PAIRS * BLOCK
    assert pairs % SB_PAIRS == 0 and qcb % SB_PAIRS == 0 and kcb % SB_PAIRS == 0 and vcb % SB_PAIRS == 0

    def body(q_ref, k_ref, v_ref, o_ref):
        n = pl.program_id(1)
        low = lax.broadcasted_iota(jnp.int32, (BLOCK, BLOCK), 1) < HEAD_DIM
        before = (lax.broadcasted_iota(jnp.int32, (2 * BLOCK, BLOCK), 1)
                  < jnp.bitwise_and(lax.broadcasted_iota(jnp.int32, (2 * BLOCK, BLOCK), 0), BLOCK - 1))
        after = _tri(True)
        suffix = lambda t: _split_dot(t, after)
        qs = []
        for p in range(SB_PAIRS):
            q2 = q_ref[:, p * BLOCK:(p + 1) * BLOCK].astype(F32)
            qs.append(jnp.concatenate([jnp.where(low, q2, 0.0), jnp.where(low, 0.0, q2)], axis=0))

        def cond(c):
            return jnp.logical_and(c[0] >= 0, c[1] > SB_SKIP_LOG)

        def step(c):
            kb, _, rs, accs = c
            rows = pl.ds(pl.multiple_of(kb * BLOCK, BLOCK), BLOCK)
            mask = jnp.logical_or(before, kb != n)
            new_r, new_acc, top = [], [], None
            for p in range(SB_PAIRS):
                cols = slice(p * BLOCK, (p + 1) * BLOCK)
                o_part, r_out = _sb_pair(_plain_dot, suffix, qs[p], k_ref[rows, cols], v_ref[rows, cols], rs[p], mask)
                new_r.append(r_out)
                new_acc.append(accs[p] + o_part)
                top = jnp.max(r_out) if top is None else jnp.maximum(top, jnp.max(r_out))
            return kb - 1, top, tuple(new_r), tuple(new_acc)

        init = (n, jnp.float32(0.0), tuple(jnp.zeros((2 * BLOCK, 1), F32) for _ in range(SB_PAIRS)),
                tuple(jnp.zeros((2 * BLOCK, BLOCK), F32) for _ in range(SB_PAIRS)))
        accs = lax.while_loop(cond, step, init)[3]
        for p in range(SB_PAIRS):
            o_ref[:, p * BLOCK:(p + 1) * BLOCK] = jnp.where(low, accs[p][:BLOCK], accs[p][BLOCK:]).astype(o_ref.dtype)

    return _pcall(
        body, side=side, name=name, grid=(pairs // SB_PAIRS, nb),
        in_specs=[pl.BlockSpec((BLOCK, wide), lambda g, n: (n, qcb // SB_PAIRS + g)),
                  pl.BlockSpec((s, wide), lambda g, n: (0, kcb // SB_PAIRS + g), pipeline_mode=pl.Buffered(1)),
                  pl.BlockSpec((s, wide), lambda g, n: (0, vcb // SB_PAIRS + g), pipeline_mode=pl.Buffered(1))],
        out_specs=pl.BlockSpec((BLOCK, wide), lambda g, n: (n, g)),
        out_shape=jax.ShapeDtypeStruct((s, pairs * BLOCK), BF16),
        compiler_params=_params(),
    )(qkv, qkv, qkv)


def sb_bwd(qkv, qcb, kcb, vcb, do, docb, name, side=None):
    s = qkv.shape[0]
    nb = s // BLOCK
    pairs = B_HEADS // 2
    wide = SB_PAIRS * BLOCK
    assert docb % SB_PAIRS == 0

    def body(q_ref, k_ref, v_ref, do_ref, dq_ref, dk_ref, dv_ref, r_ref):
        n = pl.program_id(1)

        @pl.when(n == 0)
        def _():
            dk_ref[...] = jnp.zeros(dk_ref.shape, F32)
            dv_ref[...] = jnp.zeros(dv_ref.shape, F32)

        low = lax.broadcasted_iota(jnp.int32, (BLOCK, BLOCK), 1) < HEAD_DIM
        before = (lax.broadcasted_iota(jnp.int32, (2 * BLOCK, BLOCK), 1)
                  < jnp.bitwise_and(lax.broadcasted_iota(jnp.int32, (2 * BLOCK, BLOCK), 0), BLOCK - 1))
        after, earlier = _tri(True), _tri(False)
        suffix = lambda t: _split_dot_vjp(t, after, earlier, 2)
        stack = lambda t: jnp.concatenate([jnp.where(low, t, 0.0), jnp.where(low, 0.0, t)], axis=0)
        qs = [stack(q_ref[:, p * BLOCK:(p + 1) * BLOCK].astype(F32)) for p in range(SB_PAIRS)]
        dos = [stack(do_ref[:, p * BLOCK:(p + 1) * BLOCK].astype(F32)) for p in range(SB_PAIRS)]

        def cond(c):
            return jnp.logical_and(c[0] >= 0, c[1] > SB_SKIP_LOG)

        def down(c):
            kb, _, rs = c
            rows = pl.ds(pl.multiple_of(kb * BLOCK, BLOCK), BLOCK)
            mask = jnp.logical_or(before, kb != n)
            new_r, top = [], None
            for h in range(SB_PAIRS):
                cols = slice(h * BLOCK, (h + 1) * BLOCK)
                r_ref[h, kb] = rs[h]
                z = _dot(qs[h], k_ref[rows, cols], NT) * (HEAD_DIM ** -0.5)
                log_keep = jnp.where(mask, _log_sigmoid(z) - z, 0.0)
                r_out = rs[h] + jnp.sum(log_keep, axis=1, keepdims=True)
                new_r.append(r_out)
                top = jnp.max(r_out) if top is None else jnp.maximum(top, jnp.max(r_out))
            return kb - 1, top, tuple(new_r)

        init = (n, jnp.float32(0.0), tuple(jnp.zeros((2 * BLOCK, 1), F32) for _ in range(SB_PAIRS)))
        last = lax.while_loop(cond, down, init)[0] + 1

        def up(kb, c):
            dqs, g_rs = c
            rows = pl.ds(pl.multiple_of(kb * BLOCK, BLOCK), BLOCK)
            mask = jnp.logical_or(before, kb != n)
            new_dq, new_g = [], []
            for h in range(SB_PAIRS):
                cols = slice(h * BLOCK, (h + 1) * BLOCK)
                _, vjp = jax.vjp(lambda q_, k_, v_, r_: _sb_pair(_dot_vjp, suffix, q_, k_, v_, r_, mask),
                                 qs[h], k_ref[rows, cols].astype(F32), v_ref[rows, cols].astype(F32), r_ref[h, kb])
                dq_c, dk_c, dv_c, g_in = vjp((dos[h], g_rs[h]))
                dk_ref[rows, cols] += dk_c
                dv_ref[rows, cols] += dv_c
                new_dq.append(dqs[h] + dq_c)
                new_g.append(g_in)
            return tuple(new_dq), tuple(new_g)

        init = (tuple(jnp.zeros((2 * BLOCK, BLOCK), F32) for _ in range(SB_PAIRS)),
                tuple(jnp.zeros((2 * BLOCK, 1), F32) for _ in range(SB_PAIRS)))
        dqs = lax.fori_loop(last, n + 1, up, init)[0]
        for p in range(SB_PAIRS):
            dq_ref[:, p * BLOCK:(p + 1) * BLOCK] = jnp.where(low, dqs[p][:BLOCK], dqs[p][BLOCK:])

    full = jax.ShapeDtypeStruct((s, pairs * BLOCK), F32)
    return _pcall(
        body, side=side, name=name, grid=(pairs // SB_PAIRS, nb),
        in_specs=[pl.BlockSpec((BLOCK, wide), lambda g, n: (n, qcb // SB_PAIRS + g)),
                  pl.BlockSpec((s, wide), lambda g, n: (0, kcb // SB_PAIRS + g), pipeline_mode=pl.Buffered(1)),
                  pl.BlockSpec((s, wide), lambda g, n: (0, vcb // SB_PAIRS + g), pipeline_mode=pl.Buffered(1)),
                  pl.BlockSpec((BLOCK, wide), lambda g, n: (n, docb // SB_PAIRS + g))],
        out_specs=[pl.BlockSpec((BLOCK, wide), lambda g, n: (n, g)),
                   pl.BlockSpec((s, wide), lambda g, n: (0, g), pipeline_mode=pl.Buffered(1)),
                   pl.BlockSpec((s, wide), lambda g, n: (0, g), pipeline_mode=pl.Buffered(1))],
        out_shape=[full, full, full],
        scratch_shapes=[pltpu.VMEM((SB_PAIRS, nb, 2 * BLOCK, 1), F32)],
        compiler_params=_params(),
    )(qkv, qkv, qkv, do)


def _xa_tile(dot, q, kv, qg, kg):
    hd = q.shape[1] // X_HEADS
    outs = []
    for h in range(X_HEADS):
        qh = _rms(q[:, h * hd:(h + 1) * hd], qg)
        kh = _rms(kv[:, h * hd:(h + 1) * hd], kg)
        vh = kv[:, (X_HEADS + h) * hd:(X_HEADS + h + 1) * hd]
        sc = dot(qh, kh, True) * (hd ** -0.5)
        m = lax.stop_gradient(jnp.max(sc, axis=-1, keepdims=True))
        p = jnp.exp(sc - m)
        outs.append(dot(p * (1.0 / jnp.sum(p, axis=-1, keepdims=True)), vh, False))
    return jnp.concatenate(outs, axis=1)


def xa_core_fwd(q, kv, qg, kg, name):
    n, d = q.shape
    tm = _tile(n, 256, 8)
    (o,) = tcall(lambda ids, qt, kvt, qgt, kgt: (_xa_tile(_plain_dot, qt, kvt, qgt, kgt),), (n // tm,),
                 [_row(q, tm), _full(kv), _full(qg), _full(kg)], [_row_out(n, d, BF16, tm)], name)
    return o


def xa_core_bwd(q, kv, qg, kg, do, name):
    n, d = q.shape
    tm = _tile(n, 256, 8)

    def fn(ids, qt, kvt, qgt, kgt, dot_):
        _, vjp = jax.vjp(functools.partial(_xa_tile, _dot_vjp), qt, kvt, qgt, kgt)
        return vjp(dot_.astype(F32))

    return tcall(fn, (n // tm,), [_row(q, tm), _full(kv), _full(qg), _full(kg), _row(do, tm)],
                 [_row_out(n, d, BF16, tm), _acc_out(kv.shape), _acc_out(qg.shape), _acc_out(kg.shape)], name)


def _ev_reorder(a):
    return jnp.concatenate([a[0:512], a[768:2304], a[512:768]], axis=0)


def _ev_restore(a):
    return jnp.concatenate([a[0:512], a[2048:2304], a[512:2048]], axis=0)


_EV_SEGS = ((0, 512, "q"), (512, 1536, "raw"), (2048, 128, "k"), (2176, 128, "raw"))
_A_CFG = dict(hkv=A_KV_HEADS, grp=A_Q_HEADS // A_KV_HEADS, max_dist=BLOCK - 1, step=1.0, slopes=_alibi(A_Q_HEADS),
              want_lse=False)
_A_COLS = (lambda r: 0, lambda r: 16, lambda r: 17)


def even_mixer_fwd(x, h, w_in, qg, kg, sinks, w_out, tag, side=None):
    qkv = mm(h, w_in, "nt", tag + "_in")
    (ops,) = prep_fwd(qkv, qg, kg, _EV_SEGS, (1,), tag + "_prep")
    (o_a,) = banded_fwd(ops, 1, _A_COLS, sinks, _A_CFG, tag + "_swa")
    o_b = sb_fwd(ops, 4, 8, 12, tag + "_sb", side=side)
    carried = None
    if side is not None:
        o_b, carried = o_b
    o = jnp.concatenate([o_a, o_b], axis=1)
    y = mm(o, w_out, "nn", tag + "_out", res=x)
    return y, (x, h, qkv, ops, o), carried


def even_mixer_bwd(dy, saved, g, w_in, qg, kg, sinks, w_out, tag, side=None, last_side=None):
    x, h, qkv, ops, o = saved
    do = mm(dy, w_out, "nt", tag + "_do", out_dtype=BF16)
    d_wout = mm(o, dy, "tn", tag + "_dwout")
    dqa, dkp, dkc, dvp, dvc, dsinks = banded_bwd(ops, 1, _A_COLS, sinks, _A_CFG, [(do, lambda r: 0)], tag + "_dswa")
    res = sb_bwd(ops, 4, 8, 12, do, 4, tag + "_dsb", side=side)
    carried = None
    if side is not None:
        res, carried = res
    dqb, dkb, dvb = res
    dqkv, dqg, dkg = prep_bwd(
        qkv, qg, kg, _EV_SEGS,
        [(dqa, 0, 1, 0), (dqb, 0, 1, 1), (dkb, 0, 1, 2), (dvb, 0, 1, 3), (dkc, 0, 1, 4), (dkp, 1, 1, 4), (dvc, 0, 1, 5),
         (dvp, 1, 1, 5)],
        lambda *t: jnp.concatenate(t, axis=1), tag + "_dqkv")
    d_win = mm(dqkv, h, "tn", tag + "_dwin")
    last = None if last_side is None else last_side(d_win, d_wout)
    if last is None:
        dx, dg = mm_norm_bwd(dqkv, w_in, x, g, dy, tag + "_dh", b_kd=True)
    else:
        (dx, dg), got = mm_norm_bwd(dqkv, w_in, x, g, dy, tag + "_dh", b_kd=True, side=last[0])
        last[1](got)
    return dx, dg, d_win, dqg, dkg, dsinks, d_wout, carried


def _c_cfg(window, dil):
    return dict(hkv=C_HEADS, grp=1, max_dist=window // dil, step=float(dil), slopes=_alibi(C_HEADS), want_lse=True)


_C_COLS = (lambda r: 3 * r, lambda r: 3 * r + 1, lambda r: 3 * r + 2)
_OD_SEGS = ((0, 1024, "q"), (1024, 1024, "k"), (2048, 1024, "raw"))


def _combine(o1, o2, o3, l1, l2, l3):
    m = lax.stop_gradient(jnp.maximum(jnp.maximum(l1, l2), l3))
    e1, e2, e3 = jnp.exp(l1 - m), jnp.exp(l2 - m), jnp.exp(l3 - m)
    tot = e1 + e2 + e3
    return (e1 / tot) * o1 + (e2 / tot) * o2 + (e3 / tot) * o3


def odd_mixer_fwd(x, g, w_in, qg, kg, w_out, tag):
    n, d = x.shape
    h = rmsnorm_fwd(x, g, tag + "_norm")
    qkv = mm(h, w_in, "nt", tag + "_in")
    dils = [dil for _, dil in C_PATTERNS]
    ops = prep_fwd(qkv, qg, kg, _OD_SEGS, dils, tag + "_prep")
    os_, ls_ = [], []
    for (window, dil), ops_d in zip(C_PATTERNS, ops):
        o_p, l_p = banded_fwd(ops_d, dil, _C_COLS, None, _c_cfg(window, dil), f"{tag}_dil{dil}")
        os_.append(o_p)
        ls_.append(l_p)
    tm = BLOCK
    lay = lambda a, dil: _in(a, (tm // dil, a.shape[1]), lambda i: (i, 0))
    views = [lay(a, dil) for a, dil in zip(os_ + ls_, dils + dils)]

    def comb(ids, *t, scratch):
        return (_combine(*[_to_natural(scratch, a, dil) for a, dil in zip(t, dils + dils)]),)

    (o,) = tcall(comb, (n // tm,), views, [_row_out(n, d, BF16, tm)], tag + "_comb",
                 scratch=((d // BLOCK * tm, BLOCK), F32))
    y = mm(o, w_out, "nn", tag + "_out", res=x)
    return y, (x, h, qkv, ops, views, o)


def odd_mixer_bwd(dy, saved, g, w_in, qg, kg, w_out, tag):
    x, h, qkv, ops, views, o = saved
    n, d = x.shape
    do = mm(dy, w_out, "nt", tag + "_do")
    d_wout = mm(o, dy, "tn", tag + "_dwout")
    tm = BLOCK
    dils = [dil for _, dil in C_PATTERNS]

    def comb_bwd(ids, *t, scratch):
        _, vjp = jax.vjp(_combine, *[_to_natural(scratch, a, dil) for a, dil in zip(t[:6], dils + dils)])
        return tuple(_to_strided(scratch, c, dil) for c, dil in zip(vjp(t[6]), dils + dils))

    cts = tcall(comb_bwd, (n // tm,), views + [_row(do, tm)],
                [_out((n // dil, dil * d), F32, (tm // dil, dil * d), lambda i: (i, 0)) for dil in dils + dils],
                tag + "_dcomb", scratch=((d // BLOCK * tm, BLOCK), F32))
    dqs, dks, dvs = [], [], []
    for p, ((window, dil), ops_d) in enumerate(zip(C_PATTERNS, ops)):
        dq, dkp, dkc, dvp, dvc = banded_bwd(ops_d, dil, _C_COLS, None, _c_cfg(window, dil),
                                            [(cts[p], lambda r: r), (cts[3 + p], lambda r: r)], f"{tag}_ddil{dil}")
        dqs.append((dq, 0, dil, p))
        dks += [(dkc, 0, dil, 3 + p), (dkp, dil, dil, 3 + p)]
        dvs += [(dvc, 0, dil, 6 + p), (dvp, dil, dil, 6 + p)]

    def gather(*t):
        return jnp.concatenate([t[0] + t[1] + t[2], t[3] + t[4] + t[5], t[6] + t[7] + t[8]], axis=1)

    dqkv, dqg, dkg = prep_bwd(qkv, qg, kg, _OD_SEGS, dqs + dks + dvs, gather, tag + "_dqkv")
    d_win = mm(dqkv, h, "tn", tag + "_dwin")
    dx, dg = mm_norm_bwd(dqkv, w_in, x, g, dy, tag + "_dh", b_kd=True)
    return dx, dg, d_win, dqg, dkg, d_wout


def xa_fwd(x, mem, g, gm, w_q, w_kv, qg, kg, w_o, tag):
    h = rmsnorm_fwd(x, g, tag + "_norm")
    q = mm(h, w_q, "nn", tag + "_q")
    mn = rmsnorm_fwd(mem, gm, tag + "_mnorm")
    kv = mm(mn, w_kv, "nt", tag + "_kv")
    o = xa_core_fwd(q, kv, qg, kg, tag + "_core")
    y = mm(o, w_o, "nn", tag + "_o", res=x)
    return y, (x, h, q, mn, kv, o)


def xa_bwd(dy, saved, mem, g, gm, w_q, w_kv, qg, kg, w_o, tag):
    x, h, q, mn, kv, o = saved
    do = mm(dy, w_o, "nt", tag + "_do", out_dtype=BF16)
    d_wo = mm(o, dy, "tn", tag + "_dwo")
    dq, dkv, dqg, dkg = xa_core_bwd(q, kv, qg, kg, do, tag + "_dcore")
    d_wq = mm(h, dq, "tn", tag + "_dwq")
    dx, dg = mm_norm_bwd(dq, w_q, x, g, dy, tag + "_dh")
    d_wkv = mm(dkv, mn, "tn", tag + "_dwkv")
    _, dgm = mm_norm_bwd(dkv, w_kv, mem, gm, None, tag + "_dmn", b_kd=True)
    return dx, dg, dgm, d_wq, d_wkv, dqg, dkg, d_wo


def loss_head(y, target, name):
    n, d = y.shape
    tm = _tile(n, 512, 8)

    def fn(ids, yt, tt):
        e = yt - tt
        return e * (1.0 / d), jnp.sum(e * e, axis=0, keepdims=True)

    return tcall(fn, (n // tm,), [_row(y, tm), _row(target, tm)], [_row_out(n, d, F32, tm), _acc_out((1, d))], name)


_ANY = pl.BlockSpec(memory_space=pl.ANY)


def all_gather_blocks(blocks):
    nb = len(blocks)

    def body(*refs):
        x_refs, out_refs = refs[:nb], refs[nb:2 * nb]
        send_sems, recv_sems, local_sems = refs[2 * nb:]
        x, y, c = lax.axis_index("x"), lax.axis_index("y"), lax.axis_index("c")
        me, sibling = (x, y, c), (x, y, 1 - c)
        over_x, over_y, diagonal = (1 - x, y), (x, 1 - y), (1 - x, 1 - y)
        relay_of = ((1 - x) * (1 - c) + x * c, y * (1 - c) + (1 - y) * c)
        relay_to = (x * (1 - c) + (1 - x) * c, (1 - y) * (1 - c) + y * c)

        def copy(b, k, blk, to, own=False):
            px, py, pc = blk
            slot = out_refs[b].at[4 * px + 2 * py + pc]
            return pltpu.make_async_remote_copy(
                src_ref=x_refs[b] if own else slot, dst_ref=slot,
                send_sem=send_sems.at[7 * b + k], recv_sem=recv_sems.at[7 * b + k], device_id=to, device_id_type=MESH)

        mine = [pltpu.make_async_copy(x_refs[b], out_refs[b].at[4 * x + 2 * y + c], local_sems.at[b]) for b in range(nb)]
        for cp in mine:
            cp.start()
        sent = []
        for b in range(nb):
            sent += [copy(b, 0, me, sibling, own=True), copy(b, 1, me, (*over_x, c), own=True),
                     copy(b, 2, me, (*over_y, c), own=True)]
        for cp in sent:
            cp.start()
        for b in range(nb):
            copy(b, 1, (*over_x, c), me).wait_recv()
            copy(b, 2, (*over_y, c), me).wait_recv()
            later = [copy(b, 3, (*relay_of, c), (*relay_to, c)), copy(b, 4, (*over_x, c), sibling),
                     copy(b, 5, (*over_y, c), sibling)]
            for cp in later:
                cp.start()
            sent += later
        for b in range(nb):
            copy(b, 3, (*diagonal, c), me).wait_recv()
            fwd = copy(b, 6, (*diagonal, c), sibling)
            fwd.start()
            sent.append(fwd)
        for b in range(nb):
            copy(b, 0, sibling, me).wait_recv()
            for k, chip in ((4, over_x), (5, over_y), (6, diagonal)):
                copy(b, k, (*chip, 1 - c), me).wait_recv()
        for cp in sent:
            cp.wait_send()
        for cp in mine:
            cp.wait()

    return _pcall(
        body, name="weights_all_gather",
        in_specs=[_ANY] * nb, out_specs=[_ANY] * nb,
        out_shape=[jax.ShapeDtypeStruct((N_DEV,) + a.shape, a.dtype) for a in blocks],
        scratch_shapes=[pltpu.SemaphoreType.DMA((7 * nb,)), pltpu.SemaphoreType.DMA((7 * nb,)),
                        pltpu.SemaphoreType.DMA((nb,))],
    )(*blocks)


def pair_exchange(bufs):
    nb = len(bufs)

    def body(*refs):
        srcs, dsts = refs[:nb], refs[nb:2 * nb]
        send_sems, recv_sems = refs[2 * nb:]
        x, y, c = lax.axis_index("x"), lax.axis_index("y"), lax.axis_index("c")
        copies = []
        for b in range(nb):
            for j in range(4):
                cp = pltpu.make_async_remote_copy(
                    src_ref=srcs[b].at[2 * j + (1 - c)], dst_ref=dsts[b].at[j], send_sem=send_sems.at[4 * b + j],
                    recv_sem=recv_sems.at[4 * b + j], device_id=(x, y, 1 - c), device_id_type=MESH)
                cp.start()
                copies.append(cp)
        for cp in copies:
            cp.wait()

    return _pcall(
        body, name="grads_pair_exchange",
        in_specs=[_ANY] * nb, out_specs=[_ANY] * nb,
        out_shape=[jax.ShapeDtypeStruct((4,) + a.shape[1:], a.dtype) for a in bufs],
        scratch_shapes=[pltpu.SemaphoreType.DMA((4 * nb,)), pltpu.SemaphoreType.DMA((4 * nb,))],
    )(*bufs)


def pair_sum(g, got, c, out_dtype, name):
    r, w = g.shape[1:]
    tr = _tile(r, 1024, 16)

    def body(c_ref, a_ref, b_ref, o_ref):
        o_ref[...] = (a_ref[...].astype(F32) + b_ref[...].astype(F32)).astype(o_ref.dtype)

    return _pcall(
        body, name=name,
        grid_spec=pltpu.PrefetchScalarGridSpec(
            num_scalar_prefetch=1, grid=(4, r // tr),
            in_specs=[pl.BlockSpec((None, tr, w), lambda j, i, c_ref: (2 * j + c_ref[0], i, 0)),
                      pl.BlockSpec((None, tr, w), lambda j, i, c_ref: (j, i, 0))],
            out_specs=pl.BlockSpec((None, tr, w), lambda j, i, c_ref: (j, i, 0))),
        out_shape=jax.ShapeDtypeStruct((4,) + g.shape[1:], out_dtype),
        compiler_params=_params(),
    )(c, g, got)


def chip_exchange(parts):
    nb = len(parts)

    def body(*refs):
        srcs, dsts = refs[:nb], refs[nb:2 * nb]
        send_sems, recv_sems, local_sems = refs[2 * nb:]
        x, y, c = lax.axis_index("x"), lax.axis_index("y"), lax.axis_index("c")
        my_chip = 2 * x + y
        copies = []
        for b in range(nb):
            mine = pltpu.make_async_copy(srcs[b].at[my_chip], dsts[b].at[my_chip], local_sems.at[b])
            mine.start()
            copies.append(mine)
            for k, (tx, ty) in enumerate([(1 - x, y), (x, 1 - y), (1 - x, 1 - y)]):
                cp = pltpu.make_async_remote_copy(
                    src_ref=srcs[b].at[2 * tx + ty], dst_ref=dsts[b].at[my_chip], send_sem=send_sems.at[3 * b + k],
                    recv_sem=recv_sems.at[3 * b + k], device_id=(tx, ty, c), device_id_type=MESH)
                cp.start()
                copies.append(cp)
        for cp in copies:
            cp.wait()

    return _pcall(
        body, name="grads_chip_exchange",
        in_specs=[_ANY] * nb, out_specs=[_ANY] * nb,
        out_shape=[jax.ShapeDtypeStruct(a.shape, a.dtype) for a in parts],
        scratch_shapes=[pltpu.SemaphoreType.DMA((3 * nb,)), pltpu.SemaphoreType.DMA((3 * nb,)),
                        pltpu.SemaphoreType.DMA((nb,))],
    )(*parts)


def chip_sum(parts, name):
    r, w = parts.shape[1:]
    tr = _tile(r, 1024, 16)
    spec = lambda j: _in(parts, (None, tr, w), lambda i, j=j: (j, i, 0))

    def fn(ids, a, b, c_, d):
        a, b, c_, d = [t.astype(F32) for t in (a, b, c_, d)]
        return (((a + b) + c_) + d,)

    (out,) = tcall(fn, (r // tr,), [spec(j) for j in range(4)],
                   [_out((r, w), F32, (tr, w), lambda i: (i, 0))], name)
    return out


def _remote(src, dst, send_sems, recv_sems, k, to):
    return functools.partial(pltpu.make_async_remote_copy, src_ref=src, dst_ref=dst, send_sem=send_sems.at[k],
                             recv_sem=recv_sems.at[k], device_id=to, device_id_type=MESH)


def _gather_plan(phase, nb):
    def plan(ins, outs, send_sems, recv_sems, local_sems):
        x, y, c = lax.axis_index("x"), lax.axis_index("y"), lax.axis_index("c")
        me, sibling = (x, y, c), (x, y, 1 - c)
        over_x, over_y, diagonal = (1 - x, y), (x, 1 - y), (1 - x, 1 - y)
        relay_of = ((1 - x) * (1 - c) + x * c, y * (1 - c) + (1 - y) * c)
        relay_to = (x * (1 - c) + (1 - x) * c, (1 - y) * (1 - c) + y * c)
        local, sends, recvs = [], [], []
        for b in range(nb):
            slot = lambda chip, core, b=b: outs[b].at[4 * chip[0] + 2 * chip[1] + core]
            if phase == 0:
                local.append(functools.partial(pltpu.make_async_copy, ins[b], slot((x, y), c), local_sems.at[b]))
                moves = [(ins[b], slot((x, y), c), to) for to in (sibling, (*over_x, c), (*over_y, c))]
                arrive = [slot((x, y), 1 - c), slot(over_x, c), slot(over_y, c)]
            elif phase == 1:
                moves = [(slot(relay_of, c), slot(relay_of, c), (*relay_to, c)),
                         (slot(over_x, c), slot(over_x, c), sibling), (slot(over_y, c), slot(over_y, c), sibling)]
                arrive = [slot(diagonal, c), slot(over_x, 1 - c), slot(over_y, 1 - c)]
            else:
                moves = [(slot(diagonal, c), slot(diagonal, c), sibling)]
                arrive = [slot(diagonal, 1 - c)]
            sends += [_remote(src, dst, send_sems, recv_sems, 3 * b + k, to) for k, (src, dst, to) in enumerate(moves)]
            recvs += [_remote(dst, dst, send_sems, recv_sems, 3 * b + k, me) for k, dst in enumerate(arrive)]
        return local, sends, recvs
    return plan


def gather_side(phase, arrays):
    nb = len(arrays)
    if phase == 0:
        shapes = [jax.ShapeDtypeStruct((N_DEV,) + a.shape, a.dtype) for a in arrays]
        return Side(arrays, shapes, 3 * nb, nb, _gather_plan(0, nb))
    shapes = [jax.ShapeDtypeStruct(a.shape, a.dtype) for a in arrays]
    return Side(arrays, shapes, 3 * nb, 0, _gather_plan(phase, nb), aliased=True)


def pair_side(bufs):
    nb = len(bufs)

    def plan(ins, outs, send_sems, recv_sems, local_sems):
        x, y, c = lax.axis_index("x"), lax.axis_index("y"), lax.axis_index("c")
        sends = [_remote(ins[b].at[2 * j + (1 - c)], outs[b].at[j], send_sems, recv_sems, 4 * b + j, (x, y, 1 - c))
                 for b in range(nb) for j in range(4)]
        recvs = [_remote(outs[b].at[j], outs[b].at[j], send_sems, recv_sems, 4 * b + j, (x, y, c))
                 for b in range(nb) for j in range(4)]
        return [], sends, recvs

    shapes = [jax.ShapeDtypeStruct((4,) + a.shape[1:], a.dtype) for a in bufs]
    return Side(bufs, shapes, 4 * nb, 0, plan)


def chip_side(parts):
    nb = len(parts)

    def plan(ins, outs, send_sems, recv_sems, local_sems):
        x, y, c = lax.axis_index("x"), lax.axis_index("y"), lax.axis_index("c")
        my_chip = 2 * x + y
        peers = [(1 - x, y), (x, 1 - y), (1 - x, 1 - y)]
        local = [functools.partial(pltpu.make_async_copy, ins[b].at[my_chip], outs[b].at[my_chip], local_sems.at[b])
                 for b in range(nb)]
        sends = [_remote(ins[b].at[2 * tx + ty], outs[b].at[my_chip], send_sems, recv_sems, 3 * b + k, (tx, ty, c))
                 for b in range(nb) for k, (tx, ty) in enumerate(peers)]
        recvs = [_remote(outs[b].at[2 * tx + ty], outs[b].at[2 * tx + ty], send_sems, recv_sems, 3 * b + k, (x, y, c))
                 for b in range(nb) for k, (tx, ty) in enumerate(peers)]
        return local, sends, recvs

    shapes = [jax.ShapeDtypeStruct(a.shape, a.dtype) for a in parts]
    return Side(parts, shapes, 3 * nb, nb, plan)


def adamw(w, g, m, v, name):
    shape = w.shape
    cols = shape[-1]
    rows = int(np.prod(shape[:-1]))
    w2, g2, m2, v2 = [a.reshape(rows, cols) for a in (w, g, m, v)]
    tr = _tile(rows, 512, 8) if rows % 8 == 0 else rows

    def fn(ids, wt, gt, mt, vt):
        m_new = ADAM_B1 * mt + (1.0 - ADAM_B1) * gt
        v_new = ADAM_B2 * vt + (1.0 - ADAM_B2) * (gt * gt)
        m_hat = m_new / (1.0 - ADAM_B1 ** ADAM_STEP)
        v_hat = v_new / (1.0 - ADAM_B2 ** ADAM_STEP)
        delta = -ADAM_LR * (m_hat / (jnp.sqrt(v_hat) + ADAM_EPS) + ADAM_WD * wt)
        return delta, m_new, v_new

    res = tcall(fn, (rows // tr,), [_row(a, tr) for a in (w2, g2, m2, v2)],
                [_row_out(rows, cols, F32, tr) for _ in range(3)], name)
    return [a.reshape(shape) for a in res]


_MATS = [("ffn1_w_gu", "col"), ("ffn1_w_down", "row"), ("ev_w_in", "col"), ("ev_w_out", "row"),
         ("od_w_in", "col"), ("od_w_out", "row"), ("xa_w_q", "row"), ("xa_w_kv", "col"), ("xa_w_o", "row"),
         ("ffn2_w_gu", "col"), ("ffn2_w_down", "row")]
_VECS = ["ffn1_norm", "mix_norm", "ev_q_gain", "ev_k_gain", "ev_sinks", "od_q_gain", "od_k_gain", "xa_norm",
         "xa_mem_norm", "xa_q_gain", "xa_k_gain", "ffn2_norm"]
_WEIGHTS = ["ffn1_norm", "ffn1_w_gu", "ffn1_w_down", "mix_norm", "ev_w_in", "ev_q_gain", "ev_k_gain", "ev_sinks",
            "ev_w_out", "od_w_in", "od_q_gain", "od_k_gain", "od_w_out", "xa_norm", "xa_mem_norm", "xa_w_q", "xa_w_kv",
            "xa_q_gain", "xa_k_gain", "xa_w_o", "ffn2_norm", "ffn2_w_gu", "ffn2_w_down"]


_AXIS = dict(_MATS)
DEPTH = 2


def _layer_groups(l):
    first, rest = _first_block_groups(l)
    return [first[0] + rest[0] + rest[1]]


def _first_block_groups(l):
    w_in, w_out = ("ev_w_in", "ev_w_out") if l % 2 == 0 else ("od_w_in", "od_w_out")
    first = [[("ffn1_w_gu", l), ("ffn1_w_down", l)]]
    rest = [[("ffn2_w_gu", l), ("xa_w_kv", l)],
            [(w_in, l // 2), ("ffn2_w_down", l), (w_out, l // 2), ("xa_w_q", l), ("xa_w_o", l)]]
    return first, rest


def _block_rows(shards, n):
    a, b = shards[n].shape[1:]
    return a if _AXIS[n] == "row" else b


def _weight_blocks(shards, groups):
    blocks = []
    for group in groups:
        rows = [(shards[n][j] if _AXIS[n] == "row" else shards[n][j].T).astype(BF16) for n, j in group]
        blocks.append(rows[0] if len(rows) == 1 else jnp.concatenate(rows, axis=0))
    return blocks


def _whole_weights(shards, groups, gathered):
    full = {}
    for group, got in zip(groups, gathered):
        off = 0
        for n, j in group:
            r = _block_rows(shards, n)
            full[n] = got[:, off:off + r, :].reshape(N_DEV * r, got.shape[2])
            off += r
    return full


def _gradient_buffers(grads, groups):
    bufs = []
    for group in groups:
        rows = []
        for n, _ in group:
            whole = jnp.concatenate(grads[n], axis=0) if isinstance(grads[n], tuple) else grads[n]
            rows.append(whole.reshape(N_DEV, whole.shape[0] // N_DEV, whole.shape[1]))
        bufs.append((rows[0] if len(rows) == 1 else jnp.concatenate(rows, axis=1)).astype(BF16))
    return bufs


def _gradient_blocks(shards, groups, sums):
    out = {}
    for group, tot in zip(groups, sums):
        off = 0
        for n, j in group:
            r = _block_rows(shards, n)
            out[n, j] = tot[off:off + r] if _AXIS[n] == "row" else tot[off:off + r].T
            off += r
    return out


class _PairChain:
    def __init__(self, ex, bufs):
        self.ex, self.bufs, self.parts = ex, bufs, None

    def side(self, name):
        return pair_side(self.bufs) if name == "da" else None

    def done(self, name, carried):
        self.parts = self.ex.pair_sums(self.bufs, carried, "l1")


class _RestChain:
    HALF = {"da": (0,), "dh": (1,)}

    def __init__(self, ex, bufs):
        self.ex, self.bufs, self.parts, self.sums = ex, bufs, None, [None] * len(bufs)

    def side(self, name):
        if name == "pair":
            return pair_side(self.bufs)
        if name in self.HALF:
            return chip_side([self.parts[i] for i in self.HALF[name]])
        return None

    def done(self, name, carried):
        if name == "pair":
            self.parts = self.ex.pair_sums(self.bufs, carried, "l0r")
        else:
            for i, tot in zip(self.HALF[name], self.ex.chip_sums(carried, "l0r_" + name)):
                self.sums[i] = tot


class _Exchange:
    def __init__(self, shards, c):
        self.shards, self.c = shards, c

    def weights_first(self):
        first, _ = _first_block_groups(0)
        return _whole_weights(self.shards, first, all_gather_blocks(_weight_blocks(self.shards, first)))

    def rest_blocks(self):
        return _weight_blocks(self.shards, _first_block_groups(0)[1])

    def weights_rest(self, gathered):
        return _whole_weights(self.shards, _first_block_groups(0)[1], gathered)

    def gather_start(self):
        return gather_side(0, _weight_blocks(self.shards, _layer_groups(1)))

    def weights_next(self, gathered):
        return _whole_weights(self.shards, _layer_groups(1), gathered)

    def chain_next(self, grads):
        return _PairChain(self, _gradient_buffers(grads, _layer_groups(1)))

    def chain_rest(self, grads):
        return _RestChain(self, _gradient_buffers(grads, _first_block_groups(0)[1]))

    def pair_sums(self, bufs, got, tag):
        return [pair_sum(b, g, self.c, b.dtype, f"grads_pair_sum_{tag}_{i}") for i, (b, g) in enumerate(zip(bufs, got))]

    def chip_sums(self, parts, tag):
        return [chip_sum(p, f"grads_chip_sum_{tag}_{i}") for i, p in enumerate(parts)]

    def finish(self, gm, gv, sums1, sums_rest):
        vecs = {n: jnp.concatenate(v, axis=0) for n, v in gv.items()}
        first, rest = _first_block_groups(0)
        bufs = _gradient_buffers(gm[0], first)
        vec = jnp.concatenate([vecs[n].reshape(-1) for n in _VECS])
        vec = jnp.pad(vec, (0, -vec.shape[0] % (16 * LANES)))
        bufs.append(jnp.broadcast_to(vec.reshape(1, -1, LANES), (N_DEV, vec.shape[0] // LANES, LANES)))
        parts = self.pair_sums(bufs, pair_exchange(bufs), "l0")
        sums0 = self.chip_sums(chip_exchange(parts), "l0")
        blocks = {**_gradient_blocks(self.shards, first, sums0[:-1]), **_gradient_blocks(self.shards, rest, sums_rest),
                  **_gradient_blocks(self.shards, _layer_groups(1), sums1)}
        out = {n: jnp.stack([blocks[n, j] for j in range(self.shards[n].shape[0])]) for n, _ in _MATS}
        flat, off = sums0[-1].reshape(-1), 0
        for n in _VECS:
            out[n] = flat[off:off + vecs[n].size].reshape(vecs[n].shape)
            off += vecs[n].size
        return out


class _NoExchange:
    def __init__(self, full):
        self.full = full

    def weights_first(self):
        return self.full[0]

    def rest_blocks(self):
        return None

    def gather_start(self):
        return None

    def weights_next(self, gathered):
        return self.full[1]

    def chain_next(self, grads):
        return None

    def chain_rest(self, grads):
        return None

    def finish(self, gm, gv, sums1, sums_rest):
        mats = {}
        for l in range(DEPTH):
            for group in _layer_groups(l):
                for n, j in group:
                    whole = jnp.concatenate(gm[l][n], axis=0) if isinstance(gm[l][n], tuple) else gm[l][n]
                    mats.setdefault(n, {})[j] = whole if _AXIS[n] == "row" else whole.T
        mats = {n: jnp.stack([v[j] for j in sorted(v)]) for n, v in mats.items()}
        return mats, {n: jnp.concatenate(v, axis=0) for n, v in gv.items()}


def _local_step(x, mem, target, w, ex):
    assert w["ffn1_norm"].shape[0] == DEPTH
    row = lambda a, l: a[l:l + 1]
    full = [ex.weights_first(), None]
    saved = []
    for l in range(DEPTH):
        t, j, f = f"l{l}", l // 2, full[l]
        rest = ex.rest_blocks() if l == 0 else None
        if rest is None:
            x, s1 = ffn_fwd(x, row(w["ffn1_norm"], l), f["ffn1_w_gu"], f["ffn1_w_down"], t + "_ffn1")
        else:
            x, s1, rest = ffn_fwd(x, row(w["ffn1_norm"], l), f["ffn1_w_gu"], f["ffn1_w_down"], t + "_ffn1", (0, rest))
        relay = None
        if l % 2 == 0:
            h = rmsnorm_fwd(x, row(w["mix_norm"], l), t + "_ev_norm", None if rest is None else gather_side(2, rest))
            if rest is not None:
                h, rest = h
                f = full[l] = {**f, **ex.weights_rest(rest)}
            side = ex.gather_start() if l + 1 < DEPTH else None
            x, s2, relay = even_mixer_fwd(x, h, _ev_reorder(f["ev_w_in"]), row(w["ev_q_gain"], j),
                                          row(w["ev_k_gain"], j), row(w["ev_sinks"], j), f["ev_w_out"], t + "_ev", side)
        else:
            x, s2 = odd_mixer_fwd(x, row(w["mix_norm"], l), f["od_w_in"], row(w["od_q_gain"], j),
                                  row(w["od_k_gain"], j), f["od_w_out"], t + "_od")
        x, s3 = xa_fwd(x, mem, row(w["xa_norm"], l), row(w["xa_mem_norm"], l), f["xa_w_q"], f["xa_w_kv"],
                       row(w["xa_q_gain"], l), row(w["xa_k_gain"], l), f["xa_w_o"], t + "_xa")
        if relay is None:
            x, s4 = ffn_fwd(x, row(w["ffn2_norm"], l), f["ffn2_w_gu"], f["ffn2_w_down"], t + "_ffn2")
        else:
            x, s4, relay = ffn_fwd(x, row(w["ffn2_norm"], l), f["ffn2_w_gu"], f["ffn2_w_down"], t + "_ffn2", (1, relay))
        if l + 1 < DEPTH:
            full[l + 1] = ex.weights_next(relay)
        saved.append((s1, s2, s3, s4))
    dx, sq = loss_head(x, target, "loss_head")
    loss = 0.5 * jnp.sum(sq) / x.shape[1]

    gm = [dict() for _ in range(DEPTH)]
    gv = {n: [None] * w[n].shape[0] for n in _VECS}
    chain1 = chain0 = sums1 = None
    started = []
    for l in reversed(range(DEPTH)):
        t, j, f = f"l{l}", l // 2, full[l]
        s1, s2, s3, s4 = saved[l]
        dx, gv["ffn2_norm"][l], gm[l]["ffn2_w_gu"], gm[l]["ffn2_w_down"] = ffn_bwd(
            dx, s4, row(w["ffn2_norm"], l), f["ffn2_w_gu"], f["ffn2_w_down"], t + "_ffn2", chain1 if l == 0 else None)
        parts = chain1.parts if l == 0 and chain1 is not None else None
        (dx, gv["xa_norm"][l], gv["xa_mem_norm"][l], gm[l]["xa_w_q"], gm[l]["xa_w_kv"], gv["xa_q_gain"][l],
         gv["xa_k_gain"][l], gm[l]["xa_w_o"]) = xa_bwd(
            dx, s3, mem, row(w["xa_norm"], l), row(w["xa_mem_norm"], l), f["xa_w_q"], f["xa_w_kv"],
            row(w["xa_q_gain"], l), row(w["xa_k_gain"], l), f["xa_w_o"], t + "_xa")
        if l % 2 == 0:
            def start_rest(d_win, d_wout, l=l):
                gm[l]["ev_w_in"], gm[l]["ev_w_out"] = _ev_restore(d_win), d_wout
                chain = ex.chain_rest(gm[l]) if l == 0 else None
                if chain is None:
                    return None
                started.append(chain)
                return chain.side("pair"), lambda got: chain.done("pair", got)

            (dx, gv["mix_norm"][l], d_win, gv["ev_q_gain"][j], gv["ev_k_gain"][j], gv["ev_sinks"][j],
             gm[l]["ev_w_out"], carried) = even_mixer_bwd(
                dx, s2, row(w["mix_norm"], l), _ev_reorder(f["ev_w_in"]), row(w["ev_q_gain"], j), row(w["ev_k_gain"], j),
                row(w["ev_sinks"], j), f["ev_w_out"], t + "_ev", None if parts is None else chip_side(parts), start_rest)
            gm[l]["ev_w_in"] = _ev_restore(d_win)
            if carried is not None:
                sums1 = ex.chip_sums(carried, "l1")
        else:
            (dx, gv["mix_norm"][l], gm[l]["od_w_in"], gv["od_q_gain"][j], gv["od_k_gain"][j],
             gm[l]["od_w_out"]) = odd_mixer_bwd(
                dx, s2, row(w["mix_norm"], l), f["od_w_in"], row(w["od_q_gain"], j), row(w["od_k_gain"], j),
                f["od_w_out"], t + "_od")
        if l == 0 and started:
            chain0 = started[0]
        dx, gv["ffn1_norm"][l], gm[l]["ffn1_w_gu"], gm[l]["ffn1_w_down"] = ffn_bwd(
            dx, s1, row(w["ffn1_norm"], l), f["ffn1_w_gu"], f["ffn1_w_down"], t + "_ffn1", chain0 if l == 0 else None)
        if l == 1:
            chain1 = ex.chain_next(gm[l])
    return loss, dx, ex.finish(gm, gv, sums1, None if chain0 is None else chain0.sums)


def kernel(x, mem, ffn1_norm, ffn1_w_gu, ffn1_w_down, mix_norm, ev_w_in, ev_q_gain, ev_k_gain, ev_sinks, ev_w_out, od_w_in, od_q_gain, od_k_gain, od_w_out, xa_norm, xa_mem_norm, xa_w_q, xa_w_kv, xa_q_gain, xa_k_gain, xa_w_o, ffn2_norm, ffn2_w_gu, ffn2_w_down, loss_target, m_ffn1_norm, m_ffn1_w_gu, m_ffn1_w_down, m_mix_norm, m_ev_w_in, m_ev_q_gain, m_ev_k_gain, m_ev_sinks, m_ev_w_out, m_od_w_in, m_od_q_gain, m_od_k_gain, m_od_w_out, m_xa_norm, m_xa_mem_norm, m_xa_w_q, m_xa_w_kv, m_xa_q_gain, m_xa_k_gain, m_xa_w_o, m_ffn2_norm, m_ffn2_w_gu, m_ffn2_w_down, v_ffn1_norm, v_ffn1_w_gu, v_ffn1_w_down, v_mix_norm, v_ev_w_in, v_ev_q_gain, v_ev_k_gain, v_ev_sinks, v_ev_w_out, v_od_w_in, v_od_q_gain, v_od_k_gain, v_od_w_out, v_xa_norm, v_xa_mem_norm, v_xa_w_q, v_xa_w_kv, v_xa_q_gain, v_xa_k_gain, v_xa_w_o, v_ffn2_norm, v_ffn2_w_gu, v_ffn2_w_down):
    w = dict(ffn1_norm=ffn1_norm, ffn1_w_gu=ffn1_w_gu, ffn1_w_down=ffn1_w_down, mix_norm=mix_norm, ev_w_in=ev_w_in, ev_q_gain=ev_q_gain, ev_k_gain=ev_k_gain, ev_sinks=ev_sinks, ev_w_out=ev_w_out, od_w_in=od_w_in, od_q_gain=od_q_gain, od_k_gain=od_k_gain, od_w_out=od_w_out, xa_norm=xa_norm, xa_mem_norm=xa_mem_norm, xa_w_q=xa_w_q, xa_w_kv=xa_w_kv, xa_q_gain=xa_q_gain, xa_k_gain=xa_k_gain, xa_w_o=xa_w_o, ffn2_norm=ffn2_norm, ffn2_w_gu=ffn2_w_gu, ffn2_w_down=ffn2_w_down)
    m = dict(ffn1_norm=m_ffn1_norm, ffn1_w_gu=m_ffn1_w_gu, ffn1_w_down=m_ffn1_w_down, mix_norm=m_mix_norm, ev_w_in=m_ev_w_in, ev_q_gain=m_ev_q_gain, ev_k_gain=m_ev_k_gain, ev_sinks=m_ev_sinks, ev_w_out=m_ev_w_out, od_w_in=m_od_w_in, od_q_gain=m_od_q_gain, od_k_gain=m_od_k_gain, od_w_out=m_od_w_out, xa_norm=m_xa_norm, xa_mem_norm=m_xa_mem_norm, xa_w_q=m_xa_w_q, xa_w_kv=m_xa_w_kv, xa_q_gain=m_xa_q_gain, xa_k_gain=m_xa_k_gain, xa_w_o=m_xa_w_o, ffn2_norm=m_ffn2_norm, ffn2_w_gu=m_ffn2_w_gu, ffn2_w_down=m_ffn2_w_down)
    v = dict(ffn1_norm=v_ffn1_norm, ffn1_w_gu=v_ffn1_w_gu, ffn1_w_down=v_ffn1_w_down, mix_norm=v_mix_norm, ev_w_in=v_ev_w_in, ev_q_gain=v_ev_q_gain, ev_k_gain=v_ev_k_gain, ev_sinks=v_ev_sinks, ev_w_out=v_ev_w_out, od_w_in=v_od_w_in, od_q_gain=v_od_q_gain, od_k_gain=v_od_k_gain, od_w_out=v_od_w_out, xa_norm=v_xa_norm, xa_mem_norm=v_xa_mem_norm, xa_w_q=v_xa_w_q, xa_w_kv=v_xa_w_kv, xa_q_gain=v_xa_q_gain, xa_k_gain=v_xa_k_gain, xa_w_o=v_xa_w_o, ffn2_norm=v_ffn2_norm, ffn2_w_gu=v_ffn2_w_gu, ffn2_w_down=v_ffn2_w_down)

    c = lax.axis_index("c").astype(jnp.int32).reshape(1)
    loss, dx, grads = _local_step(x[0], mem[0], loss_target[0], w, _Exchange(w, c))
    loss = lax.psum(loss, ("x", "y", "c"))

    delta, new_m, new_v = {}, {}, {}
    for n in _WEIGHTS:
        delta[n], new_m[n], new_v[n] = adamw(w[n], grads[n], m[n], v[n], "adamw_" + n)
    return (loss, dx[None], *[grads[n] for n in _WEIGHTS], *[delta[n] for n in _WEIGHTS],
            *[new_m[n] for n in _WEIGHTS], *[new_v[n] for n in _WEIGHTS])
```

```python
import functools

import numpy as np
import jax
import jax.numpy as jnp
from jax import lax
from jax.experimental import pallas as pl
from jax.experimental.pallas import tpu as pltpu

F32 = jnp.float32
BF16 = jnp.bfloat16
MESH = pl.DeviceIdType.MESH

HEAD_DIM = 64
BLOCK = 128
RMS_EPS = 1e-6
A_Q_HEADS, A_KV_HEADS = 8, 2
B_HEADS = 8
C_HEADS = 16
C_PATTERNS = ((128, 1), (512, 4), (2048, 16))
X_HEADS = 4
N_DEV = 8
LANES = 1024
VMEM_LIMIT_BYTES = 56 * 1024 * 1024
SB_SKIP_LOG = -110.0
NEG_BIG = -1e30

ADAM_LR, ADAM_B1, ADAM_B2, ADAM_EPS, ADAM_WD, ADAM_STEP = 0.001, 0.9, 0.999, 1e-08, 0.01, 10

NN = (((1,), (0,)), ((), ()))
NT = (((1,), (1,)), ((), ()))
TN = (((0,), (0,)), ((), ()))


class Side:
    def __init__(self, arrays, out_shapes, n_remote, n_local, plan, aliased=False):
        self.arrays, self.out_shapes, self.plan, self.aliased = list(arrays), list(out_shapes), plan, aliased
        self.sems = [pltpu.SemaphoreType.DMA((n_remote,)), pltpu.SemaphoreType.DMA((n_remote,)),
                     pltpu.SemaphoreType.DMA((max(n_local, 1),))]

    def start(self, ins, outs, sems):
        local, sends, _ = self.plan(ins, outs, *sems)
        for make in local + sends:
            make().start()

    def wait(self, ins, outs, sems):
        local, sends, recvs = self.plan(ins, outs, *sems)
        for make in sends:
            make().wait_send()
        for make in recvs:
            make().wait_recv()
        for make in local:
            make().wait()


def _pcall(body, side=None, **kw):
    if side is None:
        return pl.pallas_call(body, **kw)
    grid = kw["grid"]
    single = not isinstance(kw["out_specs"], (list, tuple))
    out_specs = [kw["out_specs"]] if single else list(kw["out_specs"])
    out_shape = [kw["out_shape"]] if single else list(kw["out_shape"])
    scratch = list(kw.get("scratch_shapes", []))
    n_in, n_out, n_scr, n_side = len(kw["in_specs"]), len(out_specs), len(scratch), len(side.arrays)
    n_sout = len(side.out_shapes)

    def hosted(*refs):
        ins, s_in = refs[:n_in], refs[n_in:n_in + n_side]
        outs = refs[n_in + n_side:n_in + n_side + n_out]
        s_out = refs[n_in + n_side + n_out:n_in + n_side + n_out + n_sout]
        rest = refs[n_in + n_side + n_out + n_sout:]
        scr, sems = rest[:n_scr], rest[n_scr:]
        first = last = None
        for a, size in enumerate(grid):
            f, l = pl.program_id(a) == 0, pl.program_id(a) == size - 1
            first = f if first is None else jnp.logical_and(first, f)
            last = l if last is None else jnp.logical_and(last, l)

        @pl.when(first)
        def _():
            side.start(s_in, s_out, sems)

        body(*ins, *outs, *scr)

        @pl.when(last)
        def _():
            side.wait(s_in, s_out, sems)

    any_space = pl.BlockSpec(memory_space=pl.ANY)
    kw2 = dict(kw)
    kw2.update(in_specs=list(kw["in_specs"]) + [any_space] * n_side, out_specs=out_specs + [any_space] * n_sout,
               out_shape=out_shape + side.out_shapes, scratch_shapes=scratch + side.sems)
    if side.aliased:
        kw2["input_output_aliases"] = {n_in + i: n_out + i for i in range(n_side)}
    call = pl.pallas_call(hosted, **kw2)

    def run(*args):
        res = call(*args, *side.arrays)
        return (res[0] if single else list(res[:n_out])), list(res[n_out:])

    return run


def _params(**kw):
    return pltpu.CompilerParams(vmem_limit_bytes=VMEM_LIMIT_BYTES, **kw)


def _tile(dim, cap, unit=128):
    if dim <= cap:
        return dim
    t = (cap // unit) * unit
    while t >= unit:
        if dim % t == 0:
            return t
        t -= unit
    raise ValueError(f"no tile for {dim} under {cap}")


def _dot(a, b, dims):
    return lax.dot_general(a.astype(BF16), b.astype(BF16), dims, preferred_element_type=F32)


@functools.partial(jax.custom_vjp, nondiff_argnums=(2,))
def _dot_vjp(a, b, nt):
    return _dot(a, b, NT if nt else NN)


def _dot_vjp_fwd(a, b, nt):
    return _dot(a, b, NT if nt else NN), (a.astype(BF16), b.astype(BF16))


def _dot_vjp_bwd(nt, res, g):
    a, b = res
    if nt:
        return _dot(g, b, NN), _dot(g, a, TN)
    return _dot(g, b, NT), _dot(a, g, TN)


_dot_vjp.defvjp(_dot_vjp_fwd, _dot_vjp_bwd)


def _plain_dot(a, b, nt):
    return _dot(a, b, NT if nt else NN)


def _split_dot(x, mat, terms=2):
    out, rem = None, x
    for t in range(terms):
        part = rem.astype(BF16)
        d = lax.dot_general(part, mat, NN, preferred_element_type=F32)
        out = d if out is None else out + d
        if t + 1 < terms:
            rem = rem - part.astype(F32)
    return out


@functools.partial(jax.custom_vjp, nondiff_argnums=(3,))
def _split_dot_vjp(x, mat, mat_t, terms):
    return _split_dot(x, mat, terms)


def _split_dot_vjp_fwd(x, mat, mat_t, terms):
    return _split_dot(x, mat, terms), mat_t


def _split_dot_vjp_bwd(terms, mat_t, g):
    return _split_dot(g, mat_t, terms), None, None


_split_dot_vjp.defvjp(_split_dot_vjp_fwd, _split_dot_vjp_bwd)


def _plain_split(x, mat, mat_t, terms):
    return _split_dot(x, mat, terms)


def _tri(after):
    j = lax.broadcasted_iota(jnp.int32, (BLOCK, BLOCK), 0)
    s = lax.broadcasted_iota(jnp.int32, (BLOCK, BLOCK), 1)
    return jnp.where(j > s if after else j < s, 1.0, 0.0).astype(BF16)


def _in(a, block, imap):
    return (a, block, imap)


def _out(shape, dtype, block, imap, acc=False):
    return (shape, dtype, block, imap, acc)


def tcall(fn, grid, ins, outs, name, scratch=None, side=None):
    nin = len(ins)
    nout = len(outs)
    ngrid = len(grid)

    def body(*refs):
        ids = tuple(pl.program_id(a) for a in range(ngrid))
        extra = {} if scratch is None else {"scratch": refs[nin + nout]}
        res = fn(ids, *[r[...] for r in refs[:nin]], **extra)
        first = ids[0] == 0
        for a in range(1, ngrid):
            first = jnp.logical_and(first, ids[a] == 0)
        for o_ref, r, spec in zip(refs[nin:nin + nout], res, outs):
            if spec[4]:
                @pl.when(first)
                def _(o_ref=o_ref):
                    o_ref[...] = jnp.zeros(o_ref.shape, o_ref.dtype)
                o_ref[...] += r.astype(o_ref.dtype)
            else:
                o_ref[...] = r.astype(o_ref.dtype)

    return _pcall(
        body, side=side, name=name, grid=grid,
        in_specs=[pl.BlockSpec(b, m) for (_, b, m) in ins],
        out_specs=[pl.BlockSpec(b, m) for (_, _, b, m, _) in outs],
        out_shape=[jax.ShapeDtypeStruct(s, d) for (s, d, _, _, _) in outs],
        scratch_shapes=[] if scratch is None else [pltpu.VMEM(*scratch)],
        compiler_params=_params(),
    )(*[a for (a, _, _) in ins])


def _to_strided(scr, nat, d):
    if d == 1:
        return nat
    t, w = nat.shape
    nc = w // BLOCK
    for c in range(nc):
        scr[c * t:(c + 1) * t, :] = nat[:, c * BLOCK:(c + 1) * BLOCK]
    return jnp.concatenate([scr[pl.ds(c * t + r, t // d, stride=d), :] for r in range(d) for c in range(nc)], axis=1)


def _to_natural(scr, st, d):
    if d == 1:
        return st.astype(F32)
    t, w = st.shape[0] * d, st.shape[1] // d
    nc = w // BLOCK
    st = st.astype(F32)
    for r in range(d):
        for c in range(nc):
            scr[pl.ds(c * t + r, t // d, stride=d), :] = st[:, r * w + c * BLOCK:r * w + (c + 1) * BLOCK]
    return jnp.concatenate([scr[c * t:(c + 1) * t, :] for c in range(nc)], axis=1)


def _row(a, tm, width=None, cb=0):
    width = a.shape[1] if width is None else width
    return _in(a, (tm, width), lambda i, cb=cb: (i, cb))


def _full(a):
    zeros = (0,) * a.ndim
    return _in(a, a.shape, lambda *ids: zeros)


def _row_out(n, width, dtype, tm):
    return _out((n, width), dtype, (tm, width), lambda i: (i, 0))


def _acc_out(shape):
    zeros = (0,) * len(shape)
    return _out(shape, F32, shape, lambda *ids: zeros, acc=True)


def mm(a, b, mode, name, *, out_dtype=None, scale=1.0, res=None, side=None):
    if out_dtype is None:
        out_dtype = BF16 if mode == "tn" else F32
    if mode == "nn":
        (m, k), (k2, n) = a.shape, b.shape
    elif mode == "nt":
        (m, k), (n, k2) = a.shape, b.shape
    else:
        (k, m), (k2, n) = a.shape, b.shape
    assert k == k2, (a.shape, b.shape, mode)
    tm, tn, tk = _tile(m, 1408 if mode == "tn" else 1024), _tile(n, 1408), _tile(k, 1408)
    nk = k // tk
    dims = {"nn": NN, "nt": NT, "tn": TN}[mode]
    has_res = res is not None

    def body(*refs):
        if has_res:
            a_ref, b_ref, r_ref, o_ref, acc_ref = refs
        else:
            a_ref, b_ref, o_ref, acc_ref = refs
        kk = pl.program_id(2)

        @pl.when(kk == 0)
        def _():
            acc_ref[...] = jnp.zeros(acc_ref.shape, F32)

        acc_ref[...] += _dot(a_ref[...], b_ref[...], dims)

        @pl.when(kk == nk - 1)
        def _():
            out = acc_ref[...]
            if scale != 1.0:
                out = out * scale
            if has_res:
                out = out + r_ref[...]
            o_ref[...] = out.astype(o_ref.dtype)

    a_spec = (pl.BlockSpec((tk, tm), lambda i, j, kk: (kk, i)) if mode == "tn"
              else pl.BlockSpec((tm, tk), lambda i, j, kk: (i, kk)))
    b_spec = (pl.BlockSpec((tn, tk), lambda i, j, kk: (j, kk)) if mode == "nt"
              else pl.BlockSpec((tk, tn), lambda i, j, kk: (kk, j)))
    in_specs = [a_spec, b_spec]
    args = [a, b]
    if has_res:
        in_specs.append(pl.BlockSpec((tm, tn), lambda i, j, kk: (i, j)))
        args.append(res)
    order = ("parallel", "parallel", "arbitrary") if side is None else ("arbitrary",) * 3
    return _pcall(
        body, side=side, name=name, grid=(m // tm, n // tn, nk),
        in_specs=in_specs,
        out_specs=pl.BlockSpec((tm, tn), lambda i, j, kk: (i, j)),
        out_shape=jax.ShapeDtypeStruct((m, n), out_dtype),
        scratch_shapes=[pltpu.VMEM((tm, tn), F32)],
        compiler_params=_params(dimension_semantics=order),
    )(*args)


def _rms(x, g):
    return x * lax.rsqrt(jnp.mean(x * x, axis=-1, keepdims=True) + RMS_EPS) * g


def _silu_mul(gate, up):
    return gate / (1.0 + jnp.exp(-gate)) * up


def mm_gate_up(h, w_gu, name, side=None):
    m, k = h.shape
    f = w_gu.shape[0] // 2
    tm, tn = _tile(m, 1024), _tile(f, 1408)
    nj = f // tn
    assert k <= 1408

    def body(h_ref, wg_ref, wu_ref, g_ref, u_ref, a_ref):
        ht = h_ref[...]
        for lo in range(0, tn, 512):
            cols = slice(lo, min(lo + 512, tn))
            gate, up = _dot(ht, wg_ref[cols, :], NT), _dot(ht, wu_ref[cols, :], NT)
            g_ref[:, cols] = gate.astype(g_ref.dtype)
            u_ref[:, cols] = up.astype(u_ref.dtype)
            a_ref[:, cols] = _silu_mul(gate, up).astype(a_ref.dtype)

    tile = pl.BlockSpec((tm, tn), lambda i, j: (i, j))
    return _pcall(
        body, side=side, name=name, grid=(m // tm, nj),
        in_specs=[pl.BlockSpec((tm, k), lambda i, j: (i, 0)),
                  pl.BlockSpec((tn, k), lambda i, j: (j, 0)),
                  pl.BlockSpec((tn, k), lambda i, j: (j + nj, 0))],
        out_specs=[tile, tile, tile],
        out_shape=[jax.ShapeDtypeStruct((m, f), BF16), jax.ShapeDtypeStruct((m, f), BF16),
                   jax.ShapeDtypeStruct((m, f), BF16)],
        compiler_params=_params(dimension_semantics=("arbitrary",) * 2),
    )(h, w_gu, w_gu)


def mm_down_act_bwd(dy, w_down, gate, up, name, side=None):
    m, d = dy.shape
    f = w_down.shape[0]
    tm, tn = _tile(m, 1024), _tile(f, 1408)
    assert d <= 1408

    def body(dy_ref, w_ref, g_ref, u_ref, dg_ref, du_ref):
        dyt = dy_ref[...].astype(BF16)
        for lo in range(0, tn, 512):
            cols = slice(lo, min(lo + 512, tn))
            da = _dot(dyt, w_ref[cols, :], NT) * 0.5
            gate, up = g_ref[:, cols].astype(F32), u_ref[:, cols].astype(F32)
            s = 1.0 / (1.0 + jnp.exp(-gate))
            gs = gate * s
            du_ref[:, cols] = (da * gs).astype(du_ref.dtype)
            dg_ref[:, cols] = (da * up * s * (1.0 + gate - gs)).astype(dg_ref.dtype)

    tile = pl.BlockSpec((tm, tn), lambda i, j: (i, j))
    return _pcall(
        body, side=side, name=name, grid=(m // tm, f // tn),
        in_specs=[pl.BlockSpec((tm, d), lambda i, j: (i, 0)), pl.BlockSpec((tn, d), lambda i, j: (j, 0)), tile, tile],
        out_specs=[tile, tile],
        out_shape=[jax.ShapeDtypeStruct((m, f), BF16), jax.ShapeDtypeStruct((m, f), BF16)],
        compiler_params=_params(dimension_semantics=("arbitrary", "arbitrary")),
    )(dy, w_down, gate, up)


def mm_norm_bwd(a, b, x, g, dres, name, b_kd=False, side=None):
    halves = isinstance(a, (tuple, list))
    a0, a1 = a if halves else (a, None)
    m, k = a0.shape[0], a0.shape[1] * (2 if halves else 1)
    d = b.shape[1] if b_kd else b.shape[0]
    dims = NN if b_kd else NT
    tm, tk = _tile(m, 512 if halves else 1024), _tile(a0.shape[1], 1408)
    nk = k // tk
    nkh = a0.shape[1] // tk
    has_res = dres is not None

    def body(*refs):
        a_ref, b_ref, x_ref, g_ref = refs[:4]
        rest = refs[4:-3]
        a1_ref = rest[0] if halves else None
        r_ref = rest[-1] if has_res else None
        dx_ref, dg_ref, acc_ref = refs[-3:]
        i, kk = pl.program_id(0), pl.program_id(1)

        @pl.when(kk == 0)
        def _():
            acc_ref[...] = jnp.zeros(acc_ref.shape, F32)

        if halves:
            @pl.when(kk < nkh)
            def _():
                acc_ref[...] += _dot(a_ref[...], b_ref[...], dims)

            @pl.when(kk >= nkh)
            def _():
                acc_ref[...] += _dot(a1_ref[...], b_ref[...], dims)
        else:
            acc_ref[...] += _dot(a_ref[...], b_ref[...], dims)

        @pl.when(kk == nk - 1)
        def _():
            _, vjp = jax.vjp(_rms, x_ref[...], g_ref[...])
            dx, dg = vjp(acc_ref[...])
            dx_ref[...] = dx + r_ref[...] if has_res else dx

            @pl.when(i == 0)
            def _():
                dg_ref[...] = jnp.zeros(dg_ref.shape, F32)

            dg_ref[...] += dg

    rows = pl.BlockSpec((tm, d), lambda i, kk: (i, 0))
    first = pl.BlockSpec((tm, tk), lambda i, kk: (i, jnp.minimum(kk, nkh - 1)))
    second = pl.BlockSpec((tm, tk), lambda i, kk: (i, jnp.maximum(kk - nkh, 0)))
    b_spec = pl.BlockSpec((tk, d), lambda i, kk: (kk, 0)) if b_kd else pl.BlockSpec((d, tk), lambda i, kk: (0, kk))
    in_specs = ([first, b_spec, rows, pl.BlockSpec(g.shape, lambda i, kk: (0, 0))]
                + ([second] if halves else []) + ([rows] if has_res else []))
    return _pcall(
        body, side=side, name=name, grid=(m // tm, nk),
        in_specs=in_specs,
        out_specs=[rows, pl.BlockSpec(g.shape, lambda i, kk: (0, 0))],
        out_shape=[jax.ShapeDtypeStruct((m, d), F32), jax.ShapeDtypeStruct(g.shape, F32)],
        scratch_shapes=[pltpu.VMEM((tm, d), F32)],
        compiler_params=_params(dimension_semantics=("arbitrary", "arbitrary")),
    )(*([a0, b, x, g] + ([a1] if halves else []) + ([dres] if has_res else [])))


def _indicator(shape, head_axis, mod):
    lane = lax.broadcasted_iota(jnp.int32, shape, head_axis)
    other = lax.broadcasted_iota(jnp.int32, shape, 1 - head_axis)
    lane = jnp.bitwise_and(lane, HEAD_DIM - 1) if mod else jnp.right_shift(lane, 6)
    return jnp.where(lane == other, 1.0, 0.0).astype(BF16)


def _head_rms(split, xs, g):
    w = xs.shape[1]
    to_head, from_head = _indicator((w, BLOCK), 0, False), _indicator((BLOCK, w), 1, False)
    to_lane, from_lane = _indicator((HEAD_DIM, w), 1, True), _indicator((w, HEAD_DIM), 0, True)
    ss = split(xs * xs, to_head, from_head, 3)
    r = lax.rsqrt(ss * (1.0 / HEAD_DIM) + RMS_EPS)
    g_all = split(jnp.broadcast_to(g, (8, HEAD_DIM)), to_lane, from_lane, 3)[0:1]
    return xs * split(r, from_head, to_head, 3) * g_all


def _prep(split, x, qg, kg, segs):
    parts = []
    for start, width, kind in segs:
        xs = x[:, start:start + width]
        parts.append(xs if kind == "raw" else _head_rms(split, xs, qg if kind == "q" else kg))
    return jnp.concatenate(parts, axis=1)


def prep_fwd(x, qg, kg, segs, dils, name):
    n, w = x.shape
    tm = _tile(n, 512, 8)

    def fn(ids, xt, a, b, scratch):
        ops = _prep(_plain_split, xt, a, b, segs)
        return tuple(_to_strided(scratch, ops, d) for d in dils)

    return tcall(fn, (n // tm,), [_row(x, tm), _full(qg), _full(kg)],
                 [_out((n // d, d * w), BF16, (tm // d, d * w), lambda i: (i, 0)) for d in dils], name,
                 scratch=((w // BLOCK * tm, BLOCK), F32))


def prep_bwd(x, qg, kg, segs, grads, gather, name):
    n, w = x.shape
    tm = BLOCK
    nblk = n // tm
    nslot = 1 + max(slot for _, _, _, slot in grads)

    def fn(ids, xt, a, b, *t, scratch):
        tiles, dils = [None] * nslot, [None] * nslot
        for ti, (_, sh, d, slot) in zip(t, grads):
            ti = jnp.where(ids[0] + sh < nblk, ti, 0.0) if sh else ti
            tiles[slot] = ti if tiles[slot] is None else tiles[slot] + ti
            dils[slot] = d
        tiles = [_to_natural(scratch, ti, d) for ti, d in zip(tiles, dils)]
        _, vjp = jax.vjp(lambda x_, a_, b_: _prep(_split_dot_vjp, x_, a_, b_, segs), xt, a, b)
        return vjp(gather(*tiles))

    specs = [_in(a, (tm // d, a.shape[1]), (lambda i, sh=sh: (jnp.minimum(i + sh, nblk - 1), 0)))
             for a, sh, d, _ in grads]
    wmax = max(a.shape[1] // d for a, _, d, _ in grads)
    return tcall(fn, (nblk,), [_row(x, tm), _full(qg), _full(kg)] + specs,
                 [_row_out(n, w, BF16, tm), _acc_out(qg.shape), _acc_out(kg.shape)], name,
                 scratch=((wmax // BLOCK * tm, BLOCK), F32))


def rmsnorm_fwd(x, g, name, side=None):
    n, d = x.shape
    tm = _tile(n, 512, 8)
    res = tcall(lambda ids, xt, gt: (_rms(xt, gt),), (n // tm,), [_row(x, tm), _full(g)],
                [_row_out(n, d, BF16, tm)], name, side=side)
    if side is None:
        return res[0]
    return res[0][0], res[1]


def ffn_fwd(x, g, w_gu, w_down, tag, carry=None):
    h = rmsnorm_fwd(x, g, tag + "_norm")
    if carry is None:
        gate, up, a = mm_gate_up(h, w_gu, tag + "_gu")
        return mm(a, w_down, "nn", tag + "_down", scale=0.5, res=x), (x, h, gate, up, a)
    phase, bufs = carry
    (gate, up, a), bufs = mm_gate_up(h, w_gu, tag + "_gu", side=gather_side(phase, bufs))
    y, bufs = mm(a, w_down, "nn", tag + "_down", scale=0.5, res=x, side=gather_side(phase + 1, bufs))
    return y, (x, h, gate, up, a), bufs


def ffn_bwd(dy, saved, g, w_gu, w_down, tag, chain=None):
    x, h, gate, up, a = saved

    def carrying(name, call, **kw):
        side = None if chain is None else chain.side(name)
        out = call(name=tag + "_" + name, side=side, **kw)
        if side is None:
            return out
        chain.done(name, out[1])
        return out[0]

    dgate, dup = carrying("da", mm_down_act_bwd, dy=dy, w_down=w_down, gate=gate, up=up)
    d_wdown = carrying("dwd", mm, a=a, b=dy, mode="tn", scale=0.5)
    d_wgu = (carrying("dwgu", mm, a=dgate, b=h, mode="tn"), mm(dup, h, "tn", tag + "_dwup"))
    dx, dg = carrying("dh", mm_norm_bwd, a=(dgate, dup), b=w_gu, x=x, g=g, dres=dy, b_kd=True)
    return dx, dg, d_wgu, d_wdown


def _alibi(n_heads):
    return [float(s) for s in np.asarray(2.0 ** (-8.0 * np.arange(1, n_heads + 1) / n_heads), dtype=np.float32)]


def _banded_tile(dot, first, q, kp, kc, vp, vc, sinks, *, hkv, grp, max_dist, step, slopes, want_lse):
    row = lax.broadcasted_iota(jnp.int32, (BLOCK, 2 * BLOCK), 0)
    col = lax.broadcasted_iota(jnp.int32, (BLOCK, 2 * BLOCK), 1)
    dist = row + BLOCK - col
    valid = (dist >= 0) & (dist <= max_dist) & ((col >= BLOCK) | jnp.logical_not(first))
    distf = dist.astype(F32)

    def head(hd, qh, k2, v2):
        s = dot(qh, k2, True) * (HEAD_DIM ** -0.5)
        s = jnp.where(valid, s - (slopes[hd] * step) * distf, NEG_BIG)
        m = jnp.max(s, axis=-1, keepdims=True)
        if sinks is not None:
            pick = lax.broadcasted_iota(jnp.int32, sinks.shape, 1) == hd
            sk = jnp.sum(jnp.where(pick, sinks, 0.0), axis=1, keepdims=True)
            m = jnp.maximum(m, sk)
        m = lax.stop_gradient(m)
        p = jnp.exp(s - m)
        denom = jnp.sum(p, axis=-1, keepdims=True)
        if sinks is not None:
            denom = denom + jnp.exp(sk - m)
        return dot(p * (1.0 / denom), v2, False), m + jnp.log(denom)

    outs, lses = [], []
    if grp == 1:
        low = lax.broadcasted_iota(jnp.int32, (BLOCK, BLOCK), 1) < HEAD_DIM
        for pr in range(hkv // 2):
            sl = slice(pr * BLOCK, (pr + 1) * BLOCK)
            q2 = q[:, sl]
            k2 = jnp.concatenate([kp[:, sl], kc[:, sl]], axis=0)
            v2 = jnp.concatenate([vp[:, sl], vc[:, sl]], axis=0)
            o0, l0 = head(2 * pr, jnp.where(low, q2, 0.0), k2, v2)
            o1, l1 = head(2 * pr + 1, jnp.where(low, 0.0, q2), k2, v2)
            outs.append(jnp.where(low, o0, o1))
            lses.append(jnp.where(low, l0, l1))
    else:
        for hk in range(hkv):
            sl = slice(hk * HEAD_DIM, (hk + 1) * HEAD_DIM)
            k2 = jnp.concatenate([kp[:, sl], kc[:, sl]], axis=0)
            v2 = jnp.concatenate([vp[:, sl], vc[:, sl]], axis=0)
            for gi in range(grp):
                hd = hk * grp + gi
                o_h, l_h = head(hd, q[:, hd * HEAD_DIM:(hd + 1) * HEAD_DIM], k2, v2)
                outs.append(o_h)
                lses.append(jnp.broadcast_to(l_h, (BLOCK, HEAD_DIM)))
    o = jnp.concatenate(outs, axis=1)
    if want_lse:
        return o, jnp.concatenate(lses, axis=1)
    return (o,)


def _banded_specs(view, qcol, kcol, vcol, wq, wkv):
    def at(colfn, prev):
        if prev:
            return lambda r, n: (jnp.maximum(n - 1, 0), colfn(r))
        return lambda r, n: (n, colfn(r))
    return [
        _in(view, (BLOCK, wq), at(qcol, False)),
        _in(view, (BLOCK, wkv), at(kcol, True)),
        _in(view, (BLOCK, wkv), at(kcol, False)),
        _in(view, (BLOCK, wkv), at(vcol, True)),
        _in(view, (BLOCK, wkv), at(vcol, False)),
    ]


def banded_fwd(view, dil, cols, sinks, cfg, name):
    ns = view.shape[0]
    nb = ns // BLOCK
    wq, wkv = cfg["hkv"] * cfg["grp"] * HEAD_DIM, cfg["hkv"] * HEAD_DIM
    has_sinks = sinks is not None

    def fn(ids, q, kp, kc, vp, vc, *rest):
        q, kp, kc, vp, vc = [a.astype(F32) for a in (q, kp, kc, vp, vc)]
        return _banded_tile(_plain_dot, ids[1] == 0, q, kp, kc, vp, vc, rest[0] if has_sinks else None, **cfg)

    ins = _banded_specs(view, *cols, wq, wkv) + ([_full(sinks)] if has_sinks else [])
    outs = [_out((ns, dil * wq), F32 if cfg["want_lse"] else BF16, (BLOCK, wq), lambda r, n: (n, r))]
    if cfg["want_lse"]:
        outs.append(_out((ns, dil * wq), F32, (BLOCK, wq), lambda r, n: (n, r)))
    return tcall(fn, (dil, nb), ins, outs, name)


def banded_bwd(view, dil, cols, sinks, cfg, cts, name):
    ns = view.shape[0]
    nb = ns // BLOCK
    wq, wkv = cfg["hkv"] * cfg["grp"] * HEAD_DIM, cfg["hkv"] * HEAD_DIM
    has_sinks = sinks is not None
    assert len(cts) == (2 if cfg["want_lse"] else 1)

    def fn(ids, q, kp, kc, vp, vc, *rest):
        sk = rest[0] if has_sinks else None
        ct = rest[1 if has_sinks else 0:]
        first = ids[1] == 0

        def f(q, kp, kc, vp, vc, *s):
            return _banded_tile(_dot_vjp, first, q, kp, kc, vp, vc, s[0] if has_sinks else None, **cfg)

        prim = tuple(a.astype(F32) for a in (q, kp, kc, vp, vc)) + ((sk,) if has_sinks else ())
        _, vjp = jax.vjp(f, *prim)
        return vjp(tuple(c.astype(F32) for c in ct))

    ins = (_banded_specs(view, *cols, wq, wkv) + ([_full(sinks)] if has_sinks else [])
           + [_in(a, (BLOCK, wq), (lambda r, n, cf=cf: (n, cf(r)))) for (a, cf) in cts])
    blk = lambda w: _out((ns, dil * w), F32, (BLOCK, w), lambda r, n: (n, r))
    outs = [blk(wq), blk(wkv), blk(wkv), blk(wkv), blk(wkv)]
    if has_sinks:
        outs.append(_acc_out(sinks.shape))
    return tcall(fn, (dil, nb), ins, outs, name)


def _log_sigmoid(z):
    return jnp.minimum(z, 0.0) - jnp.log(1.0 + jnp.exp(-jnp.abs(z)))


SB_PAIRS = 4


def _sb_pair(dot, suffix, qh, kb, vb, r_in, mask):
    z = dot(qh, kb, True) * (HEAD_DIM ** -0.5)
    lsp = _log_sigmoid(z)
    log_keep = jnp.where(mask, lsp - z, 0.0)
    log_after = suffix(log_keep) + r_in
    a = jnp.where(mask, jnp.exp(lsp + log_after), 0.0)
    return dot(a, vb, False), r_in + jnp.sum(log_keep, axis=1, keepdims=True)


def sb_fwd(qkv, qcb, kcb, vcb, name, side=None):
    s = qkv.shape[0]
    nb = s // BLOCK
    pairs = B_HEADS // 2
    wide = SB_PAIRS * BLOCK
    assert pairs % SB_PAIRS == 0 and qcb % SB_PAIRS == 0 and kcb % SB_PAIRS == 0 and vcb % SB_PAIRS == 0

    def body(q_ref, k_ref, v_ref, o_ref):
        n = pl.program_id(1)
        low = lax.broadcasted_iota(jnp.int32, (BLOCK, BLOCK), 1) < HEAD_DIM
        before = (lax.broadcasted_iota(jnp.int32, (2 * BLOCK, BLOCK), 1)
                  < jnp.bitwise_and(lax.broadcasted_iota(jnp.int32, (2 * BLOCK, BLOCK), 0), BLOCK - 1))
        after = _tri(True)
        suffix = lambda t: _split_dot(t, after)
        qs = []
        for p in range(SB_PAIRS):
            q2 = q_ref[:, p * BLOCK:(p + 1) * BLOCK].astype(F32)
            qs.append(jnp.concatenate([jnp.where(low, q2, 0.0), jnp.where(low, 0.0, q2)], axis=0))

        def cond(c):
            return jnp.logical_and(c[0] >= 0, c[1] > SB_SKIP_LOG)

        def step(c):
            kb, _, rs, accs = c
            rows = pl.ds(pl.multiple_of(kb * BLOCK, BLOCK), BLOCK)
            mask = jnp.logical_or(before, kb != n)
            new_r, new_acc, top = [], [], None
            for p in range(SB_PAIRS):
                cols = slice(p * BLOCK, (p + 1) * BLOCK)
                o_part, r_out = _sb_pair(_plain_dot, suffix, qs[p], k_ref[rows, cols], v_ref[rows, cols], rs[p], mask)
                new_r.append(r_out)
                new_acc.append(accs[p] + o_part)
                top = jnp.max(r_out) if top is None else jnp.maximum(top, jnp.max(r_out))
            return kb - 1, top, tuple(new_r), tuple(new_acc)

        init = (n, jnp.float32(0.0), tuple(jnp.zeros((2 * BLOCK, 1), F32) for _ in range(SB_PAIRS)),
                tuple(jnp.zeros((2 * BLOCK, BLOCK), F32) for _ in range(SB_PAIRS)))
        accs = lax.while_loop(cond, step, init)[3]
        for p in range(SB_PAIRS):
            o_ref[:, p * BLOCK:(p + 1) * BLOCK] = jnp.where(low, accs[p][:BLOCK], accs[p][BLOCK:]).astype(o_ref.dtype)

    return _pcall(
        body, side=side, name=name, grid=(pairs // SB_PAIRS, nb),
        in_specs=[pl.BlockSpec((BLOCK, wide), lambda g, n: (n, qcb // SB_PAIRS + g)),
                  pl.BlockSpec((s, wide), lambda g, n: (0, kcb // SB_PAIRS + g), pipeline_mode=pl.Buffered(1)),
                  pl.BlockSpec((s, wide), lambda g, n: (0, vcb // SB_PAIRS + g), pipeline_mode=pl.Buffered(1))],
        out_specs=pl.BlockSpec((BLOCK, wide), lambda g, n: (n, g)),
        out_shape=jax.ShapeDtypeStruct((s, pairs * BLOCK), BF16),
        compiler_params=_params(),
    )(qkv, qkv, qkv)


def sb_bwd(qkv, qcb, kcb, vcb, do, docb, name, side=None):
    s = qkv.shape[0]
    nb = s // BLOCK
    pairs = B_HEADS // 2
    wide = SB_PAIRS * BLOCK
    assert docb % SB_PAIRS == 0

    def body(q_ref, k_ref, v_ref, do_ref, dq_ref, dk_ref, dv_ref, r_ref):
        n = pl.program_id(1)

        @pl.when(n == 0)
        def _():
            dk_ref[...] = jnp.zeros(dk_ref.shape, F32)
            dv_ref[...] = jnp.zeros(dv_ref.shape, F32)

        low = lax.broadcasted_iota(jnp.int32, (BLOCK, BLOCK), 1) < HEAD_DIM
        before = (lax.broadcasted_iota(jnp.int32, (2 * BLOCK, BLOCK), 1)
                  < jnp.bitwise_and(lax.broadcasted_iota(jnp.int32, (2 * BLOCK, BLOCK), 0), BLOCK - 1))
        after, earlier = _tri(True), _tri(False)
        suffix = lambda t: _split_dot_vjp(t, after, earlier, 2)
        stack = lambda t: jnp.concatenate([jnp.where(low, t, 0.0), jnp.where(low, 0.0, t)], axis=0)
        qs = [stack(q_ref[:, p * BLOCK:(p + 1) * BLOCK].astype(F32)) for p in range(SB_PAIRS)]
        dos = [stack(do_ref[:, p * BLOCK:(p + 1) * BLOCK].astype(F32)) for p in range(SB_PAIRS)]

        def cond(c):
            return jnp.logical_and(c[0] >= 0, c[1] > SB_SKIP_LOG)

        def down(c):
            kb, _, rs = c
            rows = pl.ds(pl.multiple_of(kb * BLOCK, BLOCK), BLOCK)
            mask = jnp.logical_or(before, kb != n)
            new_r, top = [], None
            for h in range(SB_PAIRS):
                cols = slice(h * BLOCK, (h + 1) * BLOCK)
                r_ref[h, kb] = rs[h]
                z = _dot(qs[h], k_ref[rows, cols], NT) * (HEAD_DIM ** -0.5)
                log_keep = jnp.where(mask, _log_sigmoid(z) - z, 0.0)
                r_out = rs[h] + jnp.sum(log_keep, axis=1, keepdims=True)
                new_r.append(r_out)
                top = jnp.max(r_out) if top is None else jnp.maximum(top, jnp.max(r_out))
            return kb - 1, top, tuple(new_r)

        init = (n, jnp.float32(0.0), tuple(jnp.zeros((2 * BLOCK, 1), F32) for _ in range(SB_PAIRS)))
        last = lax.while_loop(cond, down, init)[0] + 1

        def up(kb, c):
            dqs, g_rs = c
            rows = pl.ds(pl.multiple_of(kb * BLOCK, BLOCK), BLOCK)
            mask = jnp.logical_or(before, kb != n)
            new_dq, new_g = [], []
            for h in range(SB_PAIRS):
                cols = slice(h * BLOCK, (h + 1) * BLOCK)
                _, vjp = jax.vjp(lambda q_, k_, v_, r_: _sb_pair(_dot_vjp, suffix, q_, k_, v_, r_, mask),
                                 qs[h], k_ref[rows, cols].astype(F32), v_ref[rows, cols].astype(F32), r_ref[h, kb])
                dq_c, dk_c, dv_c, g_in = vjp((dos[h], g_rs[h]))
                dk_ref[rows, cols] += dk_c
                dv_ref[rows, cols] += dv_c
                new_dq.append(dqs[h] + dq_c)
                new_g.append(g_in)
            return tuple(new_dq), tuple(new_g)

        init = (tuple(jnp.zeros((2 * BLOCK, BLOCK), F32) for _ in range(SB_PAIRS)),
                tuple(jnp.zeros((2 * BLOCK, 1), F32) for _ in range(SB_PAIRS)))
        dqs = lax.fori_loop(last, n + 1, up, init)[0]
        for p in range(SB_PAIRS):
            dq_ref[:, p * BLOCK:(p + 1) * BLOCK] = jnp.where(low, dqs[p][:BLOCK], dqs[p][BLOCK:])

    full = jax.ShapeDtypeStruct((s, pairs * BLOCK), F32)
    return _pcall(
        body, side=side, name=name, grid=(pairs // SB_PAIRS, nb),
        in_specs=[pl.BlockSpec((BLOCK, wide), lambda g, n: (n, qcb // SB_PAIRS + g)),
                  pl.BlockSpec((s, wide), lambda g, n: (0, kcb // SB_PAIRS + g), pipeline_mode=pl.Buffered(1)),
                  pl.BlockSpec((s, wide), lambda g, n: (0, vcb // SB_PAIRS + g), pipeline_mode=pl.Buffered(1)),
                  pl.BlockSpec((BLOCK, wide), lambda g, n: (n, docb // SB_PAIRS + g))],
        out_specs=[pl.BlockSpec((BLOCK, wide), lambda g, n: (n, g)),
                   pl.BlockSpec((s, wide), lambda g, n: (0, g), pipeline_mode=pl.Buffered(1)),
                   pl.BlockSpec((s, wide), lambda g, n: (0, g), pipeline_mode=pl.Buffered(1))],
        out_shape=[full, full, full],
        scratch_shapes=[pltpu.VMEM((SB_PAIRS, nb, 2 * BLOCK, 1), F32)],
        compiler_params=_params(),
    )(qkv, qkv, qkv, do)


def _xa_tile(dot, q, kv, qg, kg):
    hd = q.shape[1] // X_HEADS
    outs = []
    for h in range(X_HEADS):
        qh = _rms(q[:, h * hd:(h + 1) * hd], qg)
        kh = _rms(kv[:, h * hd:(h + 1) * hd], kg)
        vh = kv[:, (X_HEADS + h) * hd:(X_HEADS + h + 1) * hd]
        sc = dot(qh, kh, True) * (hd ** -0.5)
        m = lax.stop_gradient(jnp.max(sc, axis=-1, keepdims=True))
        p = jnp.exp(sc - m)
        outs.append(dot(p * (1.0 / jnp.sum(p, axis=-1, keepdims=True)), vh, False))
    return jnp.concatenate(outs, axis=1)


def xa_core_fwd(q, kv, qg, kg, name):
    n, d = q.shape
    tm = _tile(n, 256, 8)
    (o,) = tcall(lambda ids, qt, kvt, qgt, kgt: (_xa_tile(_plain_dot, qt, kvt, qgt, kgt),), (n // tm,),
                 [_row(q, tm), _full(kv), _full(qg), _full(kg)], [_row_out(n, d, BF16, tm)], name)
    return o


def xa_core_bwd(q, kv, qg, kg, do, name):
    n, d = q.shape
    tm = _tile(n, 256, 8)

    def fn(ids, qt, kvt, qgt, kgt, dot_):
        _, vjp = jax.vjp(functools.partial(_xa_tile, _dot_vjp), qt, kvt, qgt, kgt)
        return vjp(dot_.astype(F32))

    return tcall(fn, (n // tm,), [_row(q, tm), _full(kv), _full(qg), _full(kg), _row(do, tm)],
                 [_row_out(n, d, BF16, tm), _acc_out(kv.shape), _acc_out(qg.shape), _acc_out(kg.shape)], name)


def _ev_reorder(a):
    return jnp.concatenate([a[0:512], a[768:2304], a[512:768]], axis=0)


def _ev_restore(a):
    return jnp.concatenate([a[0:512], a[2048:2304], a[512:2048]], axis=0)


_EV_SEGS = ((0, 512, "q"), (512, 1536, "raw"), (2048, 128, "k"), (2176, 128, "raw"))
_A_CFG = dict(hkv=A_KV_HEADS, grp=A_Q_HEADS // A_KV_HEADS, max_dist=BLOCK - 1, step=1.0, slopes=_alibi(A_Q_HEADS),
              want_lse=False)
_A_COLS = (lambda r: 0, lambda r: 16, lambda r: 17)


def even_mixer_fwd(x, h, w_in, qg, kg, sinks, w_out, tag, side=None):
    qkv = mm(h, w_in, "nt", tag + "_in")
    (ops,) = prep_fwd(qkv, qg, kg, _EV_SEGS, (1,), tag + "_prep")
    (o_a,) = banded_fwd(ops, 1, _A_COLS, sinks, _A_CFG, tag + "_swa")
    o_b = sb_fwd(ops, 4, 8, 12, tag + "_sb", side=side)
    carried = None
    if side is not None:
        o_b, carried = o_b
    o = jnp.concatenate([o_a, o_b], axis=1)
    y = mm(o, w_out, "nn", tag + "_out", res=x)
    return y, (x, h, qkv, ops, o), carried


def even_mixer_bwd(dy, saved, g, w_in, qg, kg, sinks, w_out, tag, side=None, last_side=None):
    x, h, qkv, ops, o = saved
    do = mm(dy, w_out, "nt", tag + "_do", out_dtype=BF16)
    d_wout = mm(o, dy, "tn", tag + "_dwout")
    dqa, dkp, dkc, dvp, dvc, dsinks = banded_bwd(ops, 1, _A_COLS, sinks, _A_CFG, [(do, lambda r: 0)], tag + "_dswa")
    res = sb_bwd(ops, 4, 8, 12, do, 4, tag + "_dsb", side=side)
    carried = None
    if side is not None:
        res, carried = res
    dqb, dkb, dvb = res
    dqkv, dqg, dkg = prep_bwd(
        qkv, qg, kg, _EV_SEGS,
        [(dqa, 0, 1, 0), (dqb, 0, 1, 1), (dkb, 0, 1, 2), (dvb, 0, 1, 3), (dkc, 0, 1, 4), (dkp, 1, 1, 4), (dvc, 0, 1, 5),
         (dvp, 1, 1, 5)],
        lambda *t: jnp.concatenate(t, axis=1), tag + "_dqkv")
    d_win = mm(dqkv, h, "tn", tag + "_dwin")
    last = None if last_side is None else last_side(d_win, d_wout)
    if last is None:
        dx, dg = mm_norm_bwd(dqkv, w_in, x, g, dy, tag + "_dh", b_kd=True)
    else:
        (dx, dg), got = mm_norm_bwd(dqkv, w_in, x, g, dy, tag + "_dh", b_kd=True, side=last[0])
        last[1](got)
    return dx, dg, d_win, dqg, dkg, dsinks, d_wout, carried


def _c_cfg(window, dil):
    return dict(hkv=C_HEADS, grp=1, max_dist=window // dil, step=float(dil), slopes=_alibi(C_HEADS), want_lse=True)


_C_COLS = (lambda r: 3 * r, lambda r: 3 * r + 1, lambda r: 3 * r + 2)
_OD_SEGS = ((0, 1024, "q"), (1024, 1024, "k"), (2048, 1024, "raw"))


def _combine(o1, o2, o3, l1, l2, l3):
    m = lax.stop_gradient(jnp.maximum(jnp.maximum(l1, l2), l3))
    e1, e2, e3 = jnp.exp(l1 - m), jnp.exp(l2 - m), jnp.exp(l3 - m)
    tot = e1 + e2 + e3
    return (e1 / tot) * o1 + (e2 / tot) * o2 + (e3 / tot) * o3


def odd_mixer_fwd(x, g, w_in, qg, kg, w_out, tag):
    n, d = x.shape
    h = rmsnorm_fwd(x, g, tag + "_norm")
    qkv = mm(h, w_in, "nt", tag + "_in")
    dils = [dil for _, dil in C_PATTERNS]
    ops = prep_fwd(qkv, qg, kg, _OD_SEGS, dils, tag + "_prep")
    os_, ls_ = [], []
    for (window, dil), ops_d in zip(C_PATTERNS, ops):
        o_p, l_p = banded_fwd(ops_d, dil, _C_COLS, None, _c_cfg(window, dil), f"{tag}_dil{dil}")
        os_.append(o_p)
        ls_.append(l_p)
    tm = BLOCK
    lay = lambda a, dil: _in(a, (tm // dil, a.shape[1]), lambda i: (i, 0))
    views = [lay(a, dil) for a, dil in zip(os_ + ls_, dils + dils)]

    def comb(ids, *t, scratch):
        return (_combine(*[_to_natural(scratch, a, dil) for a, dil in zip(t, dils + dils)]),)

    (o,) = tcall(comb, (n // tm,), views, [_row_out(n, d, BF16, tm)], tag + "_comb",
                 scratch=((d // BLOCK * tm, BLOCK), F32))
    y = mm(o, w_out, "nn", tag + "_out", res=x)
    return y, (x, h, qkv, ops, views, o)


def odd_mixer_bwd(dy, saved, g, w_in, qg, kg, w_out, tag):
    x, h, qkv, ops, views, o = saved
    n, d = x.shape
    do = mm(dy, w_out, "nt", tag + "_do")
    d_wout = mm(o, dy, "tn", tag + "_dwout")
    tm = BLOCK
    dils = [dil for _, dil in C_PATTERNS]

    def comb_bwd(ids, *t, scratch):
        _, vjp = jax.vjp(_combine, *[_to_natural(scratch, a, dil) for a, dil in zip(t[:6], dils + dils)])
        return tuple(_to_strided(scratch, c, dil) for c, dil in zip(vjp(t[6]), dils + dils))

    cts = tcall(comb_bwd, (n // tm,), views + [_row(do, tm)],
                [_out((n // dil, dil * d), F32, (tm // dil, dil * d), lambda i: (i, 0)) for dil in dils + dils],
                tag + "_dcomb", scratch=((d // BLOCK * tm, BLOCK), F32))
    dqs, dks, dvs = [], [], []
    for p, ((window, dil), ops_d) in enumerate(zip(C_PATTERNS, ops)):
        dq, dkp, dkc, dvp, dvc = banded_bwd(ops_d, dil, _C_COLS, None, _c_cfg(window, dil),
                                            [(cts[p], lambda r: r), (cts[3 + p], lambda r: r)], f"{tag}_ddil{dil}")
        dqs.append((dq, 0, dil, p))
        dks += [(dkc, 0, dil, 3 + p), (dkp, dil, dil, 3 + p)]
        dvs += [(dvc, 0, dil, 6 + p), (dvp, dil, dil, 6 + p)]

    def gather(*t):
        return jnp.concatenate([t[0] + t[1] + t[2], t[3] + t[4] + t[5], t[6] + t[7] + t[8]], axis=1)

    dqkv, dqg, dkg = prep_bwd(qkv, qg, kg, _OD_SEGS, dqs + dks + dvs, gather, tag + "_dqkv")
    d_win = mm(dqkv, h, "tn", tag + "_dwin")
    dx, dg = mm_norm_bwd(dqkv, w_in, x, g, dy, tag + "_dh", b_kd=True)
    return dx, dg, d_win, dqg, dkg, d_wout


def xa_fwd(x, mem, g, gm, w_q, w_kv, qg, kg, w_o, tag):
    h = rmsnorm_fwd(x, g, tag + "_norm")
    q = mm(h, w_q, "nn", tag + "_q")
    mn = rmsnorm_fwd(mem, gm, tag + "_mnorm")
    kv = mm(mn, w_kv, "nt", tag + "_kv")
    o = xa_core_fwd(q, kv, qg, kg, tag + "_core")
    y = mm(o, w_o, "nn", tag + "_o", res=x)
    return y, (x, h, q, mn, kv, o)


def xa_bwd(dy, saved, mem, g, gm, w_q, w_kv, qg, kg, w_o, tag):
    x, h, q, mn, kv, o = saved
    do = mm(dy, w_o, "nt", tag + "_do", out_dtype=BF16)
    d_wo = mm(o, dy, "tn", tag + "_dwo")
    dq, dkv, dqg, dkg = xa_core_bwd(q, kv, qg, kg, do, tag + "_dcore")
    d_wq = mm(h, dq, "tn", tag + "_dwq")
    dx, dg = mm_norm_bwd(dq, w_q, x, g, dy, tag + "_dh")
    d_wkv = mm(dkv, mn, "tn", tag + "_dwkv")
    _, dgm = mm_norm_bwd(dkv, w_kv, mem, gm, None, tag + "_dmn", b_kd=True)
    return dx, dg, dgm, d_wq, d_wkv, dqg, dkg, d_wo


def loss_head(y, target, name):
    n, d = y.shape
    tm = _tile(n, 512, 8)

    def fn(ids, yt, tt):
        e = yt - tt
        return e * (1.0 / d), jnp.sum(e * e, axis=0, keepdims=True)

    return tcall(fn, (n // tm,), [_row(y, tm), _row(target, tm)], [_row_out(n, d, F32, tm), _acc_out((1, d))], name)


_ANY = pl.BlockSpec(memory_space=pl.ANY)


def all_gather_blocks(blocks):
    nb = len(blocks)

    def body(*refs):
        x_refs, out_refs = refs[:nb], refs[nb:2 * nb]
        send_sems, recv_sems, local_sems = refs[2 * nb:]
        x, y, c = lax.axis_index("x"), lax.axis_index("y"), lax.axis_index("c")
        me, sibling = (x, y, c), (x, y, 1 - c)
        over_x, over_y, diagonal = (1 - x, y), (x, 1 - y), (1 - x, 1 - y)
        relay_of = ((1 - x) * (1 - c) + x * c, y * (1 - c) + (1 - y) * c)
        relay_to = (x * (1 - c) + (1 - x) * c, (1 - y) * (1 - c) + y * c)

        def copy(b, k, blk, to, own=False):
            px, py, pc = blk
            slot = out_refs[b].at[4 * px + 2 * py + pc]
            return pltpu.make_async_remote_copy(
                src_ref=x_refs[b] if own else slot, dst_ref=slot,
                send_sem=send_sems.at[7 * b + k], recv_sem=recv_sems.at[7 * b + k], device_id=to, device_id_type=MESH)

        mine = [pltpu.make_async_copy(x_refs[b], out_refs[b].at[4 * x + 2 * y + c], local_sems.at[b]) for b in range(nb)]
        for cp in mine:
            cp.start()
        sent = []
        for b in range(nb):
            sent += [copy(b, 0, me, sibling, own=True), copy(b, 1, me, (*over_x, c), own=True),
                     copy(b, 2, me, (*over_y, c), own=True)]
        for cp in sent:
            cp.start()
        for b in range(nb):
            copy(b, 1, (*over_x, c), me).wait_recv()
            copy(b, 2, (*over_y, c), me).wait_recv()
            later = [copy(b, 3, (*relay_of, c), (*relay_to, c)), copy(b, 4, (*over_x, c), sibling),
                     copy(b, 5, (*over_y, c), sibling)]
            for cp in later:
                cp.start()
            sent += later
        for b in range(nb):
            copy(b, 3, (*diagonal, c), me).wait_recv()
            fwd = copy(b, 6, (*diagonal, c), sibling)
            fwd.start()
            sent.append(fwd)
        for b in range(nb):
            copy(b, 0, sibling, me).wait_recv()
            for k, chip in ((4, over_x), (5, over_y), (6, diagonal)):
                copy(b, k, (*chip, 1 - c), me).wait_recv()
        for cp in sent:
            cp.wait_send()
        for cp in mine:
            cp.wait()

    return _pcall(
        body, name="weights_all_gather",
        in_specs=[_ANY] * nb, out_specs=[_ANY] * nb,
        out_shape=[jax.ShapeDtypeStruct((N_DEV,) + a.shape, a.dtype) for a in blocks],
        scratch_shapes=[pltpu.SemaphoreType.DMA((7 * nb,)), pltpu.SemaphoreType.DMA((7 * nb,)),
                        pltpu.SemaphoreType.DMA((nb,))],
    )(*blocks)


def pair_exchange(bufs):
    nb = len(bufs)

    def body(*refs):
        srcs, dsts = refs[:nb], refs[nb:2 * nb]
        send_sems, recv_sems = refs[2 * nb:]
        x, y, c = lax.axis_index("x"), lax.axis_index("y"), lax.axis_index("c")
        copies = []
        for b in range(nb):
            for j in range(4):
                cp = pltpu.make_async_remote_copy(
                    src_ref=srcs[b].at[2 * j + (1 - c)], dst_ref=dsts[b].at[j], send_sem=send_sems.at[4 * b + j],
                    recv_sem=recv_sems.at[4 * b + j], device_id=(x, y, 1 - c), device_id_type=MESH)
                cp.start()
                copies.append(cp)
        for cp in copies:
            cp.wait()

    return _pcall(
        body, name="grads_pair_exchange",
        in_specs=[_ANY] * nb, out_specs=[_ANY] * nb,
        out_shape=[jax.ShapeDtypeStruct((4,) + a.shape[1:], a.dtype) for a in bufs],
        scratch_shapes=[pltpu.SemaphoreType.DMA((4 * nb,)), pltpu.SemaphoreType.DMA((4 * nb,))],
    )(*bufs)


def pair_sum(g, got, c, out_dtype, name):
    r, w = g.shape[1:]
    tr = _tile(r, 1024, 16)

    def body(c_ref, a_ref, b_ref, o_ref):
        o_ref[...] = (a_ref[...].astype(F32) + b_ref[...].astype(F32)).astype(o_ref.dtype)

    return _pcall(
        body, name=name,
        grid_spec=pltpu.PrefetchScalarGridSpec(
            num_scalar_prefetch=1, grid=(4, r // tr),
            in_specs=[pl.BlockSpec((None, tr, w), lambda j, i, c_ref: (2 * j + c_ref[0], i, 0)),
                      pl.BlockSpec((None, tr, w), lambda j, i, c_ref: (j, i, 0))],
            out_specs=pl.BlockSpec((None, tr, w), lambda j, i, c_ref: (j, i, 0))),
        out_shape=jax.ShapeDtypeStruct((4,) + g.shape[1:], out_dtype),
        compiler_params=_params(),
    )(c, g, got)


def chip_exchange(parts):
    nb = len(parts)

    def body(*refs):
        srcs, dsts = refs[:nb], refs[nb:2 * nb]
        send_sems, recv_sems, local_sems = refs[2 * nb:]
        x, y, c = lax.axis_index("x"), lax.axis_index("y"), lax.axis_index("c")
        my_chip = 2 * x + y
        copies = []
        for b in range(nb):
            mine = pltpu.make_async_copy(srcs[b].at[my_chip], dsts[b].at[my_chip], local_sems.at[b])
            mine.start()
            copies.append(mine)
            for k, (tx, ty) in enumerate([(1 - x, y), (x, 1 - y), (1 - x, 1 - y)]):
                cp = pltpu.make_async_remote_copy(
                    src_ref=srcs[b].at[2 * tx + ty], dst_ref=dsts[b].at[my_chip], send_sem=send_sems.at[3 * b + k],
                    recv_sem=recv_sems.at[3 * b + k], device_id=(tx, ty, c), device_id_type=MESH)
                cp.start()
                copies.append(cp)
        for cp in copies:
            cp.wait()

    return _pcall(
        body, name="grads_chip_exchange",
        in_specs=[_ANY] * nb, out_specs=[_ANY] * nb,
        out_shape=[jax.ShapeDtypeStruct(a.shape, a.dtype) for a in parts],
        scratch_shapes=[pltpu.SemaphoreType.DMA((3 * nb,)), pltpu.SemaphoreType.DMA((3 * nb,)),
                        pltpu.SemaphoreType.DMA((nb,))],
    )(*parts)


def chip_sum(parts, name):
    r, w = parts.shape[1:]
    tr = _tile(r, 1024, 16)
    spec = lambda j: _in(parts, (None, tr, w), lambda i, j=j: (j, i, 0))

    def fn(ids, a, b, c_, d):
        a, b, c_, d = [t.astype(F32) for t in (a, b, c_, d)]
        return (((a + b) + c_) + d,)

    (out,) = tcall(fn, (r // tr,), [spec(j) for j in range(4)],
                   [_out((r, w), F32, (tr, w), lambda i: (i, 0))], name)
    return out


def _remote(src, dst, send_sems, recv_sems, k, to):
    return functools.partial(pltpu.make_async_remote_copy, src_ref=src, dst_ref=dst, send_sem=send_sems.at[k],
                             recv_sem=recv_sems.at[k], device_id=to, device_id_type=MESH)


def _gather_plan(phase, nb):
    def plan(ins, outs, send_sems, recv_sems, local_sems):
        x, y, c = lax.axis_index("x"), lax.axis_index("y"), lax.axis_index("c")
        me, sibling = (x, y, c), (x, y, 1 - c)
        over_x, over_y, diagonal = (1 - x, y), (x, 1 - y), (1 - x, 1 - y)
        relay_of = ((1 - x) * (1 - c) + x * c, y * (1 - c) + (1 - y) * c)
        relay_to = (x * (1 - c) + (1 - x) * c, (1 - y) * (1 - c) + y * c)
        local, sends, recvs = [], [], []
        for b in range(nb):
            slot = lambda chip, core, b=b: outs[b].at[4 * chip[0] + 2 * chip[1] + core]
            if phase == 0:
                local.append(functools.partial(pltpu.make_async_copy, ins[b], slot((x, y), c), local_sems.at[b]))
                moves = [(ins[b], slot((x, y), c), to) for to in (sibling, (*over_x, c), (*over_y, c))]
                arrive = [slot((x, y), 1 - c), slot(over_x, c), slot(over_y, c)]
            elif phase == 1:
                moves = [(slot(relay_of, c), slot(relay_of, c), (*relay_to, c)),
                         (slot(over_x, c), slot(over_x, c), sibling), (slot(over_y, c), slot(over_y, c), sibling)]
                arrive = [slot(diagonal, c), slot(over_x, 1 - c), slot(over_y, 1 - c)]
            else:
                moves = [(slot(diagonal, c), slot(diagonal, c), sibling)]
                arrive = [slot(diagonal, 1 - c)]
            sends += [_remote(src, dst, send_sems, recv_sems, 3 * b + k, to) for k, (src, dst, to) in enumerate(moves)]
            recvs += [_remote(dst, dst, send_sems, recv_sems, 3 * b + k, me) for k, dst in enumerate(arrive)]
        return local, sends, recvs
    return plan


def gather_side(phase, arrays):
    nb = len(arrays)
    if phase == 0:
        shapes = [jax.ShapeDtypeStruct((N_DEV,) + a.shape, a.dtype) for a in arrays]
        return Side(arrays, shapes, 3 * nb, nb, _gather_plan(0, nb))
    shapes = [jax.ShapeDtypeStruct(a.shape, a.dtype) for a in arrays]
    return Side(arrays, shapes, 3 * nb, 0, _gather_plan(phase, nb), aliased=True)


def pair_side(bufs):
    nb = len(bufs)

    def plan(ins, outs, send_sems, recv_sems, local_sems):
        x, y, c = lax.axis_index("x"), lax.axis_index("y"), lax.axis_index("c")
        sends = [_remote(ins[b].at[2 * j + (1 - c)], outs[b].at[j], send_sems, recv_sems, 4 * b + j, (x, y, 1 - c))
                 for b in range(nb) for j in range(4)]
        recvs = [_remote(outs[b].at[j], outs[b].at[j], send_sems, recv_sems, 4 * b + j, (x, y, c))
                 for b in range(nb) for j in range(4)]
        return [], sends, recvs

    shapes = [jax.ShapeDtypeStruct((4,) + a.shape[1:], a.dtype) for a in bufs]
    return Side(bufs, shapes, 4 * nb, 0, plan)


def chip_side(parts):
    nb = len(parts)

    def plan(ins, outs, send_sems, recv_sems, local_sems):
        x, y, c = lax.axis_index("x"), lax.axis_index("y"), lax.axis_index("c")
        my_chip = 2 * x + y
        peers = [(1 - x, y), (x, 1 - y), (1 - x, 1 - y)]
        local = [functools.partial(pltpu.make_async_copy, ins[b].at[my_chip], outs[b].at[my_chip], local_sems.at[b])
                 for b in range(nb)]
        sends = [_remote(ins[b].at[2 * tx + ty], outs[b].at[my_chip], send_sems, recv_sems, 3 * b + k, (tx, ty, c))
                 for b in range(nb) for k, (tx, ty) in enumerate(peers)]
        recvs = [_remote(outs[b].at[2 * tx + ty], outs[b].at[2 * tx + ty], send_sems, recv_sems, 3 * b + k, (x, y, c))
                 for b in range(nb) for k, (tx, ty) in enumerate(peers)]
        return local, sends, recvs

    shapes = [jax.ShapeDtypeStruct(a.shape, a.dtype) for a in parts]
    return Side(parts, shapes, 3 * nb, nb, plan)


def adamw(w, g, m, v, name):
    shape = w.shape
    cols = shape[-1]
    rows = int(np.prod(shape[:-1]))
    w2, g2, m2, v2 = [a.reshape(rows, cols) for a in (w, g, m, v)]
    tr = _tile(rows, 512, 8) if rows % 8 == 0 else rows

    def fn(ids, wt, gt, mt, vt):
        m_new = ADAM_B1 * mt + (1.0 - ADAM_B1) * gt
        v_new = ADAM_B2 * vt + (1.0 - ADAM_B2) * (gt * gt)
        m_hat = m_new / (1.0 - ADAM_B1 ** ADAM_STEP)
        v_hat = v_new / (1.0 - ADAM_B2 ** ADAM_STEP)
        delta = -ADAM_LR * (m_hat / (jnp.sqrt(v_hat) + ADAM_EPS) + ADAM_WD * wt)
        return delta, m_new, v_new

    res = tcall(fn, (rows // tr,), [_row(a, tr) for a in (w2, g2, m2, v2)],
                [_row_out(rows, cols, F32, tr) for _ in range(3)], name)
    return [a.reshape(shape) for a in res]


_MATS = [("ffn1_w_gu", "col"), ("ffn1_w_down", "row"), ("ev_w_in", "col"), ("ev_w_out", "row"),
         ("od_w_in", "col"), ("od_w_out", "row"), ("xa_w_q", "row"), ("xa_w_kv", "col"), ("xa_w_o", "row"),
         ("ffn2_w_gu", "col"), ("ffn2_w_down", "row")]
_VECS = ["ffn1_norm", "mix_norm", "ev_q_gain", "ev_k_gain", "ev_sinks", "od_q_gain", "od_k_gain", "xa_norm",
         "xa_mem_norm", "xa_q_gain", "xa_k_gain", "ffn2_norm"]
_WEIGHTS = ["ffn1_norm", "ffn1_w_gu", "ffn1_w_down", "mix_norm", "ev_w_in", "ev_q_gain", "ev_k_gain", "ev_sinks",
            "ev_w_out", "od_w_in", "od_q_gain", "od_k_gain", "od_w_out", "xa_norm", "xa_mem_norm", "xa_w_q", "xa_w_kv",
            "xa_q_gain", "xa_k_gain", "xa_w_o", "ffn2_norm", "ffn2_w_gu", "ffn2_w_down"]


_AXIS = dict(_MATS)
DEPTH = 2


def _layer_groups(l):
    first, rest = _first_block_groups(l)
    return [first[0] + rest[0] + rest[1]]


def _first_block_groups(l):
    w_in, w_out = ("ev_w_in", "ev_w_out") if l % 2 == 0 else ("od_w_in", "od_w_out")
    first = [[("ffn1_w_gu", l), ("ffn1_w_down", l)]]
    rest = [[("ffn2_w_gu", l), ("xa_w_kv", l)],
            [(w_in, l // 2), ("ffn2_w_down", l), (w_out, l // 2), ("xa_w_q", l), ("xa_w_o", l)]]
    return first, rest


def _block_rows(shards, n):
    a, b = shards[n].shape[1:]
    return a if _AXIS[n] == "row" else b


def _weight_blocks(shards, groups):
    blocks = []
    for group in groups:
        rows = [(shards[n][j] if _AXIS[n] == "row" else shards[n][j].T).astype(BF16) for n, j in group]
        blocks.append(rows[0] if len(rows) == 1 else jnp.concatenate(rows, axis=0))
    return blocks


def _whole_weights(shards, groups, gathered):
    full = {}
    for group, got in zip(groups, gathered):
        off = 0
        for n, j in group:
            r = _block_rows(shards, n)
            full[n] = got[:, off:off + r, :].reshape(N_DEV * r, got.shape[2])
            off += r
    return full


def _gradient_buffers(grads, groups):
    bufs = []
    for group in groups:
        rows = []
        for n, _ in group:
            whole = jnp.concatenate(grads[n], axis=0) if isinstance(grads[n], tuple) else grads[n]
            rows.append(whole.reshape(N_DEV, whole.shape[0] // N_DEV, whole.shape[1]))
        bufs.append((rows[0] if len(rows) == 1 else jnp.concatenate(rows, axis=1)).astype(BF16))
    return bufs


def _gradient_blocks(shards, groups, sums):
    out = {}
    for group, tot in zip(groups, sums):
        off = 0
        for n, j in group:
            r = _block_rows(shards, n)
            out[n, j] = tot[off:off + r] if _AXIS[n] == "row" else tot[off:off + r].T
            off += r
    return out


class _PairChain:
    def __init__(self, ex, bufs):
        self.ex, self.bufs, self.parts = ex, bufs, None

    def side(self, name):
        return pair_side(self.bufs) if name == "da" else None

    def done(self, name, carried):
        self.parts = self.ex.pair_sums(self.bufs, carried, "l1")


class _RestChain:
    HALF = {"da": (0,), "dh": (1,)}

    def __init__(self, ex, bufs):
        self.ex, self.bufs, self.parts, self.sums = ex, bufs, None, [None] * len(bufs)

    def side(self, name):
        if name == "pair":
            return pair_side(self.bufs)
        if name in self.HALF:
            return chip_side([self.parts[i] for i in self.HALF[name]])
        return None

    def done(self, name, carried):
        if name == "pair":
            self.parts = self.ex.pair_sums(self.bufs, carried, "l0r")
        else:
            for i, tot in zip(self.HALF[name], self.ex.chip_sums(carried, "l0r_" + name)):
                self.sums[i] = tot


class _Exchange:
    def __init__(self, shards, c):
        self.shards, self.c = shards, c

    def weights_first(self):
        first, _ = _first_block_groups(0)
        return _whole_weights(self.shards, first, all_gather_blocks(_weight_blocks(self.shards, first)))

    def rest_blocks(self):
        return _weight_blocks(self.shards, _first_block_groups(0)[1])

    def weights_rest(self, gathered):
        return _whole_weights(self.shards, _first_block_groups(0)[1], gathered)

    def gather_start(self):
        return gather_side(0, _weight_blocks(self.shards, _layer_groups(1)))

    def weights_next(self, gathered):
        return _whole_weights(self.shards, _layer_groups(1), gathered)

    def chain_next(self, grads):
        return _PairChain(self, _gradient_buffers(grads, _layer_groups(1)))

    def chain_rest(self, grads):
        return _RestChain(self, _gradient_buffers(grads, _first_block_groups(0)[1]))

    def pair_sums(self, bufs, got, tag):
        return [pair_sum(b, g, self.c, b.dtype, f"grads_pair_sum_{tag}_{i}") for i, (b, g) in enumerate(zip(bufs, got))]

    def chip_sums(self, parts, tag):
        return [chip_sum(p, f"grads_chip_sum_{tag}_{i}") for i, p in enumerate(parts)]

    def finish(self, gm, gv, sums1, sums_rest):
        vecs = {n: jnp.concatenate(v, axis=0) for n, v in gv.items()}
        first, rest = _first_block_groups(0)
        bufs = _gradient_buffers(gm[0], first)
        vec = jnp.concatenate([vecs[n].reshape(-1) for n in _VECS])
        vec = jnp.pad(vec, (0, -vec.shape[0] % (16 * LANES)))
        bufs.append(jnp.broadcast_to(vec.reshape(1, -1, LANES), (N_DEV, vec.shape[0] // LANES, LANES)))
        parts = self.pair_sums(bufs, pair_exchange(bufs), "l0")
        sums0 = self.chip_sums(chip_exchange(parts), "l0")
        blocks = {**_gradient_blocks(self.shards, first, sums0[:-1]), **_gradient_blocks(self.shards, rest, sums_rest),
                  **_gradient_blocks(self.shards, _layer_groups(1), sums1)}
        out = {n: jnp.stack([blocks[n, j] for j in range(self.shards[n].shape[0])]) for n, _ in _MATS}
        flat, off = sums0[-1].reshape(-1), 0
        for n in _VECS:
            out[n] = flat[off:off + vecs[n].size].reshape(vecs[n].shape)
            off += vecs[n].size
        return out


class _NoExchange:
    def __init__(self, full):
        self.full = full

    def weights_first(self):
        return self.full[0]

    def rest_blocks(self):
        return None

    def gather_start(self):
        return None

    def weights_next(self, gathered):
        return self.full[1]

    def chain_next(self, grads):
        return None

    def chain_rest(self, grads):
        return None

    def finish(self, gm, gv, sums1, sums_rest):
        mats = {}
        for l in range(DEPTH):
            for group in _layer_groups(l):
                for n, j in group:
                    whole = jnp.concatenate(gm[l][n], axis=0) if isinstance(gm[l][n], tuple) else gm[l][n]
                    mats.setdefault(n, {})[j] = whole if _AXIS[n] == "row" else whole.T
        mats = {n: jnp.stack([v[j] for j in sorted(v)]) for n, v in mats.items()}
        return mats, {n: jnp.concatenate(v, axis=0) for n, v in gv.items()}


def _local_step(x, mem, target, w, ex):
    assert w["ffn1_norm"].shape[0] == DEPTH
    row = lambda a, l: a[l:l + 1]
    full = [ex.weights_first(), None]
    saved = []
    for l in range(DEPTH):
        t, j, f = f"l{l}", l // 2, full[l]
        rest = ex.rest_blocks() if l == 0 else None
        if rest is None:
            x, s1 = ffn_fwd(x, row(w["ffn1_norm"], l), f["ffn1_w_gu"], f["ffn1_w_down"], t + "_ffn1")
        else:
            x, s1, rest = ffn_fwd(x, row(w["ffn1_norm"], l), f["ffn1_w_gu"], f["ffn1_w_down"], t + "_ffn1", (0, rest))
        relay = None
        if l % 2 == 0:
            h = rmsnorm_fwd(x, row(w["mix_norm"], l), t + "_ev_norm", None if rest is None else gather_side(2, rest))
            if rest is not None:
                h, rest = h
                f = full[l] = {**f, **ex.weights_rest(rest)}
            side = ex.gather_start() if l + 1 < DEPTH else None
            x, s2, relay = even_mixer_fwd(x, h, _ev_reorder(f["ev_w_in"]), row(w["ev_q_gain"], j),
                                          row(w["ev_k_gain"], j), row(w["ev_sinks"], j), f["ev_w_out"], t + "_ev", side)
        else:
            x, s2 = odd_mixer_fwd(x, row(w["mix_norm"], l), f["od_w_in"], row(w["od_q_gain"], j),
                                  row(w["od_k_gain"], j), f["od_w_out"], t + "_od")
        x, s3 = xa_fwd(x, mem, row(w["xa_norm"], l), row(w["xa_mem_norm"], l), f["xa_w_q"], f["xa_w_kv"],
                       row(w["xa_q_gain"], l), row(w["xa_k_gain"], l), f["xa_w_o"], t + "_xa")
        if relay is None:
            x, s4 = ffn_fwd(x, row(w["ffn2_norm"], l), f["ffn2_w_gu"], f["ffn2_w_down"], t + "_ffn2")
        else:
            x, s4, relay = ffn_fwd(x, row(w["ffn2_norm"], l), f["ffn2_w_gu"], f["ffn2_w_down"], t + "_ffn2", (1, relay))
        if l + 1 < DEPTH:
            full[l + 1] = ex.weights_next(relay)
        saved.append((s1, s2, s3, s4))
    dx, sq = loss_head(x, target, "loss_head")
    loss = 0.5 * jnp.sum(sq) / x.shape[1]

    gm = [dict() for _ in range(DEPTH)]
    gv = {n: [None] * w[n].shape[0] for n in _VECS}
    chain1 = chain0 = sums1 = None
    started = []
    for l in reversed(range(DEPTH)):
        t, j, f = f"l{l}", l // 2, full[l]
        s1, s2, s3, s4 = saved[l]
        dx, gv["ffn2_norm"][l], gm[l]["ffn2_w_gu"], gm[l]["ffn2_w_down"] = ffn_bwd(
            dx, s4, row(w["ffn2_norm"], l), f["ffn2_w_gu"], f["ffn2_w_down"], t + "_ffn2", chain1 if l == 0 else None)
        parts = chain1.parts if l == 0 and chain1 is not None else None
        (dx, gv["xa_norm"][l], gv["xa_mem_norm"][l], gm[l]["xa_w_q"], gm[l]["xa_w_kv"], gv["xa_q_gain"][l],
         gv["xa_k_gain"][l], gm[l]["xa_w_o"]) = xa_bwd(
            dx, s3, mem, row(w["xa_norm"], l), row(w["xa_mem_norm"], l), f["xa_w_q"], f["xa_w_kv"],
            row(w["xa_q_gain"], l), row(w["xa_k_gain"], l), f["xa_w_o"], t + "_xa")
        if l % 2 == 0:
            def start_rest(d_win, d_wout, l=l):
                gm[l]["ev_w_in"], gm[l]["ev_w_out"] = _ev_restore(d_win), d_wout
                chain = ex.chain_rest(gm[l]) if l == 0 else None
                if chain is None:
                    return None
                started.append(chain)
                return chain.side("pair"), lambda got: chain.done("pair", got)

            (dx, gv["mix_norm"][l], d_win, gv["ev_q_gain"][j], gv["ev_k_gain"][j], gv["ev_sinks"][j],
             gm[l]["ev_w_out"], carried) = even_mixer_bwd(
                dx, s2, row(w["mix_norm"], l), _ev_reorder(f["ev_w_in"]), row(w["ev_q_gain"], j), row(w["ev_k_gain"], j),
                row(w["ev_sinks"], j), f["ev_w_out"], t + "_ev", None if parts is None else chip_side(parts), start_rest)
            gm[l]["ev_w_in"] = _ev_restore(d_win)
            if carried is not None:
                sums1 = ex.chip_sums(carried, "l1")
        else:
            (dx, gv["mix_norm"][l], gm[l]["od_w_in"], gv["od_q_gain"][j], gv["od_k_gain"][j],
             gm[l]["od_w_out"]) = odd_mixer_bwd(
                dx, s2, row(w["mix_norm"], l), f["od_w_in"], row(w["od_q_gain"], j), row(w["od_k_gain"], j),
                f["od_w_out"], t + "_od")
        if l == 0 and started:
            chain0 = started[0]
        dx, gv["ffn1_norm"][l], gm[l]["ffn1_w_gu"], gm[l]["ffn1_w_down"] = ffn_bwd(
            dx, s1, row(w["ffn1_norm"], l), f["ffn1_w_gu"], f["ffn1_w_down"], t + "_ffn1", chain0 if l == 0 else None)
        if l == 1:
            chain1 = ex.chain_next(gm[l])
    return loss, dx, ex.finish(gm, gv, sums1, None if chain0 is None else chain0.sums)


def kernel(x, mem, ffn1_norm, ffn1_w_gu, ffn1_w_down, mix_norm, ev_w_in, ev_q_gain, ev_k_gain, ev_sinks, ev_w_out, od_w_in, od_q_gain, od_k_gain, od_w_out, xa_norm, xa_mem_norm, xa_w_q, xa_w_kv, xa_q_gain, xa_k_gain, xa_w_o, ffn2_norm, ffn2_w_gu, ffn2_w_down, loss_target, m_ffn1_norm, m_ffn1_w_gu, m_ffn1_w_down, m_mix_norm, m_ev_w_in, m_ev_q_gain, m_ev_k_gain, m_ev_sinks, m_ev_w_out, m_od_w_in, m_od_q_gain, m_od_k_gain, m_od_w_out, m_xa_norm, m_xa_mem_norm, m_xa_w_q, m_xa_w_kv, m_xa_q_gain, m_xa_k_gain, m_xa_w_o, m_ffn2_norm, m_ffn2_w_gu, m_ffn2_w_down, v_ffn1_norm, v_ffn1_w_gu, v_ffn1_w_down, v_mix_norm, v_ev_w_in, v_ev_q_gain, v_ev_k_gain, v_ev_sinks, v_ev_w_out, v_od_w_in, v_od_q_gain, v_od_k_gain, v_od_w_out, v_xa_norm, v_xa_mem_norm, v_xa_w_q, v_xa_w_kv, v_xa_q_gain, v_xa_k_gain, v_xa_w_o, v_ffn2_norm, v_ffn2_w_gu, v_ffn2_w_down):
    w = dict(ffn1_norm=ffn1_norm, ffn1_w_gu=ffn1_w_gu, ffn1_w_down=ffn1_w_down, mix_norm=mix_norm, ev_w_in=ev_w_in, ev_q_gain=ev_q_gain, ev_k_gain=ev_k_gain, ev_sinks=ev_sinks, ev_w_out=ev_w_out, od_w_in=od_w_in, od_q_gain=od_q_gain, od_k_gain=od_k_gain, od_w_out=od_w_out, xa_norm=xa_norm, xa_mem_norm=xa_mem_norm, xa_w_q=xa_w_q, xa_w_kv=xa_w_kv, xa_q_gain=xa_q_gain, xa_k_gain=xa_k_gain, xa_w_o=xa_w_o, ffn2_norm=ffn2_norm, ffn2_w_gu=ffn2_w_gu, ffn2_w_down=ffn2_w_down)
    m = dict(ffn1_norm=m_ffn1_norm, ffn1_w_gu=m_ffn1_w_gu, ffn1_w_down=m_ffn1_w_down, mix_norm=m_mix_norm, ev_w_in=m_ev_w_in, ev_q_gain=m_ev_q_gain, ev_k_gain=m_ev_k_gain, ev_sinks=m_ev_sinks, ev_w_out=m_ev_w_out, od_w_in=m_od_w_in, od_q_gain=m_od_q_gain, od_k_gain=m_od_k_gain, od_w_out=m_od_w_out, xa_norm=m_xa_norm, xa_mem_norm=m_xa_mem_norm, xa_w_q=m_xa_w_q, xa_w_kv=m_xa_w_kv, xa_q_gain=m_xa_q_gain, xa_k_gain=m_xa_k_gain, xa_w_o=m_xa_w_o, ffn2_norm=m_ffn2_norm, ffn2_w_gu=m_ffn2_w_gu, ffn2_w_down=m_ffn2_w_down)
    v = dict(ffn1_norm=v_ffn1_norm, ffn1_w_gu=v_ffn1_w_gu, ffn1_w_down=v_ffn1_w_down, mix_norm=v_mix_norm, ev_w_in=v_ev_w_in, ev_q_gain=v_ev_q_gain, ev_k_gain=v_ev_k_gain, ev_sinks=v_ev_sinks, ev_w_out=v_ev_w_out, od_w_in=v_od_w_in, od_q_gain=v_od_q_gain, od_k_gain=v_od_k_gain, od_w_out=v_od_w_out, xa_norm=v_xa_norm, xa_mem_norm=v_xa_mem_norm, xa_w_q=v_xa_w_q, xa_w_kv=v_xa_w_kv, xa_q_gain=v_xa_q_gain, xa_k_gain=v_xa_k_gain, xa_w_o=v_xa_w_o, ffn2_norm=v_ffn2_norm, ffn2_w_gu=v_ffn2_w_gu, ffn2_w_down=v_ffn2_w_down)

    c = lax.axis_index("c").astype(jnp.int32).reshape(1)
    loss, dx, grads = _local_step(x[0], mem[0], loss_target[0], w, _Exchange(w, c))
    loss = lax.psum(loss, ("x", "y", "c"))

    delta, new_m, new_v = {}, {}, {}
    for n in _WEIGHTS:
        delta[n], new_m[n], new_v[n] = adamw(w[n], grads[n], m[n], v[n], "adamw_" + n)
    return (loss, dx[None], *[grads[n] for n in _WEIGHTS], *[delta[n] for n in _WEIGHTS],
            *[new_m[n] for n in _WEIGHTS], *[new_v[n] for n in _WEIGHTS])
```

```python
import functools

import numpy as np
import jax
import jax.numpy as jnp
from jax import lax
from jax.experimental import pallas as pl
from jax.experimental.pallas import tpu as pltpu

F32 = jnp.float32
BF16 = jnp.bfloat16
MESH = pl.DeviceIdType.MESH

HEAD_DIM = 64
BLOCK = 128
RMS_EPS = 1e-6
A_Q_HEADS, A_KV_HEADS = 8, 2
B_HEADS = 8
C_HEADS = 16
C_PATTERNS = ((128, 1), (512, 4), (2048, 16))
X_HEADS = 4
N_DEV = 8
LANES = 1024
VMEM_LIMIT_BYTES = 56 * 1024 * 1024
SB_SKIP_LOG = -110.0
NEG_BIG = -1e30

ADAM_LR, ADAM_B1, ADAM_B2, ADAM_EPS, ADAM_WD, ADAM_STEP = 0.001, 0.9, 0.999, 1e-08, 0.01, 10

NN = (((1,), (0,)), ((), ()))
NT = (((1,), (1,)), ((), ()))
TN = (((0,), (0,)), ((), ()))


class Side:
    def __init__(self, arrays, out_shapes, n_remote, n_local, plan, aliased=False):
        self.arrays, self.out_shapes, self.plan, self.aliased = list(arrays), list(out_shapes), plan, aliased
        self.sems = [pltpu.SemaphoreType.DMA((n_remote,)), pltpu.SemaphoreType.DMA((n_remote,)),
                     pltpu.SemaphoreType.DMA((max(n_local, 1),))]

    def start(self, ins, outs, sems):
        local, sends, _ = self.plan(ins, outs, *sems)
        for make in local + sends:
            make().start()

    def wait(self, ins, outs, sems):
        local, sends, recvs = self.plan(ins, outs, *sems)
        for make in sends:
            make().wait_send()
        for make in recvs:
            make().wait_recv()
        for make in local:
            make().wait()


def _pcall(body, side=None, **kw):
    if side is None:
        return pl.pallas_call(body, **kw)
    grid = kw["grid"]
    single = not isinstance(kw["out_specs"], (list, tuple))
    out_specs = [kw["out_specs"]] if single else list(kw["out_specs"])
    out_shape = [kw["out_shape"]] if single else list(kw["out_shape"])
    scratch = list(kw.get("scratch_shapes", []))
    n_in, n_out, n_scr, n_side = len(kw["in_specs"]), len(out_specs), len(scratch), len(side.arrays)
    n_sout = len(side.out_shapes)

    def hosted(*refs):
        ins, s_in = refs[:n_in], refs[n_in:n_in + n_side]
        outs = refs[n_in + n_side:n_in + n_side + n_out]
        s_out = refs[n_in + n_side + n_out:n_in + n_side + n_out + n_sout]
        rest = refs[n_in + n_side + n_out + n_sout:]
        scr, sems = rest[:n_scr], rest[n_scr:]
        first = last = None
        for a, size in enumerate(grid):
            f, l = pl.program_id(a) == 0, pl.program_id(a) == size - 1
            first = f if first is None else jnp.logical_and(first, f)
            last = l if last is None else jnp.logical_and(last, l)

        @pl.when(first)
        def _():
            side.start(s_in, s_out, sems)

        body(*ins, *outs, *scr)

        @pl.when(last)
        def _():
            side.wait(s_in, s_out, sems)

    any_space = pl.BlockSpec(memory_space=pl.ANY)
    kw2 = dict(kw)
    kw2.update(in_specs=list(kw["in_specs"]) + [any_space] * n_side, out_specs=out_specs + [any_space] * n_sout,
               out_shape=out_shape + side.out_shapes, scratch_shapes=scratch + side.sems)
    if side.aliased:
        kw2["input_output_aliases"] = {n_in + i: n_out + i for i in range(n_side)}
    call = pl.pallas_call(hosted, **kw2)

    def run(*args):
        res = call(*args, *side.arrays)
        return (res[0] if single else list(res[:n_out])), list(res[n_out:])

    return run


def _params(**kw):
    return pltpu.CompilerParams(vmem_limit_bytes=VMEM_LIMIT_BYTES, **kw)


def _tile(dim, cap, unit=128):
    if dim <= cap:
        return dim
    t = (cap // unit) * unit
    while t >= unit:
        if dim % t == 0:
            return t
        t -= unit
    raise ValueError(f"no tile for {dim} under {cap}")


def _dot(a, b, dims):
    return lax.dot_general(a.astype(BF16), b.astype(BF16), dims, preferred_element_type=F32)


@functools.partial(jax.custom_vjp, nondiff_argnums=(2,))
def _dot_vjp(a, b, nt):
    return _dot(a, b, NT if nt else NN)


def _dot_vjp_fwd(a, b, nt):
    return _dot(a, b, NT if nt else NN), (a.astype(BF16), b.astype(BF16))


def _dot_vjp_bwd(nt, res, g):
    a, b = res
    if nt:
        return _dot(g, b, NN), _dot(g, a, TN)
    return _dot(g, b, NT), _dot(a, g, TN)


_dot_vjp.defvjp(_dot_vjp_fwd, _dot_vjp_bwd)


def _plain_dot(a, b, nt):
    return _dot(a, b, NT if nt else NN)


def _split_dot(x, mat, terms=2):
    out, rem = None, x
    for t in range(terms):
        part = rem.astype(BF16)
        d = lax.dot_general(part, mat, NN, preferred_element_type=F32)
        out = d if out is None else out + d
        if t + 1 < terms:
            rem = rem - part.astype(F32)
    return out


@functools.partial(jax.custom_vjp, nondiff_argnums=(3,))
def _split_dot_vjp(x, mat, mat_t, terms):
    return _split_dot(x, mat, terms)


def _split_dot_vjp_fwd(x, mat, mat_t, terms):
    return _split_dot(x, mat, terms), mat_t


def _split_dot_vjp_bwd(terms, mat_t, g):
    return _split_dot(g, mat_t, terms), None, None


_split_dot_vjp.defvjp(_split_dot_vjp_fwd, _split_dot_vjp_bwd)


def _plain_split(x, mat, mat_t, terms):
    return _split_dot(x, mat, terms)


def _tri(after):
    j = lax.broadcasted_iota(jnp.int32, (BLOCK, BLOCK), 0)
    s = lax.broadcasted_iota(jnp.int32, (BLOCK, BLOCK), 1)
    return jnp.where(j > s if after else j < s, 1.0, 0.0).astype(BF16)


def _in(a, block, imap):
    return (a, block, imap)


def _out(shape, dtype, block, imap, acc=False):
    return (shape, dtype, block, imap, acc)


def tcall(fn, grid, ins, outs, name, scratch=None, side=None):
    nin = len(ins)
    nout = len(outs)
    ngrid = len(grid)

    def body(*refs):
        ids = tuple(pl.program_id(a) for a in range(ngrid))
        extra = {} if scratch is None else {"scratch": refs[nin + nout]}
        res = fn(ids, *[r[...] for r in refs[:nin]], **extra)
        first = ids[0] == 0
        for a in range(1, ngrid):
            first = jnp.logical_and(first, ids[a] == 0)
        for o_ref, r, spec in zip(refs[nin:nin + nout], res, outs):
            if spec[4]:
                @pl.when(first)
                def _(o_ref=o_ref):
                    o_ref[...] = jnp.zeros(o_ref.shape, o_ref.dtype)
                o_ref[...] += r.astype(o_ref.dtype)
            else:
                o_ref[...] = r.astype(o_ref.dtype)

    return _pcall(
        body, side=side, name=name, grid=grid,
        in_specs=[pl.BlockSpec(b, m) for (_, b, m) in ins],
        out_specs=[pl.BlockSpec(b, m) for (_, _, b, m, _) in outs],
        out_shape=[jax.ShapeDtypeStruct(s, d) for (s, d, _, _, _) in outs],
        scratch_shapes=[] if scratch is None else [pltpu.VMEM(*scratch)],
        compiler_params=_params(),
    )(*[a for (a, _, _) in ins])


def _to_strided(scr, nat, d):
    if d == 1:
        return nat
    t, w = nat.shape
    nc = w // BLOCK
    for c in range(nc):
        scr[c * t:(c + 1) * t, :] = nat[:, c * BLOCK:(c + 1) * BLOCK]
    return jnp.concatenate([scr[pl.ds(c * t + r, t // d, stride=d), :] for r in range(d) for c in range(nc)], axis=1)


def _to_natural(scr, st, d):
    if d == 1:
        return st.astype(F32)
    t, w = st.shape[0] * d, st.shape[1] // d
    nc = w // BLOCK
    st = st.astype(F32)
    for r in range(d):
        for c in range(nc):
            scr[pl.ds(c * t + r, t // d, stride=d), :] = st[:, r * w + c * BLOCK:r * w + (c + 1) * BLOCK]
    return jnp.concatenate([scr[c * t:(c + 1) * t, :] for c in range(nc)], axis=1)


def _row(a, tm, width=None, cb=0):
    width = a.shape[1] if width is None else width
    return _in(a, (tm, width), lambda i, cb=cb: (i, cb))


def _full(a):
    zeros = (0,) * a.ndim
    return _in(a, a.shape, lambda *ids: zeros)


def _row_out(n, width, dtype, tm):
    return _out((n, width), dtype, (tm, width), lambda i: (i, 0))


def _acc_out(shape):
    zeros = (0,) * len(shape)
    return _out(shape, F32, shape, lambda *ids: zeros, acc=True)


def mm(a, b, mode, name, *, out_dtype=None, scale=1.0, res=None, side=None):
    if out_dtype is None:
        out_dtype = BF16 if mode == "tn" else F32
    if mode == "nn":
        (m, k), (k2, n) = a.shape, b.shape
    elif mode == "nt":
        (m, k), (n, k2) = a.shape, b.shape
    else:
        (k, m), (k2, n) = a.shape, b.shape
    assert k == k2, (a.shape, b.shape, mode)
    tm, tn, tk = _tile(m, 1408 if mode == "tn" else 1024), _tile(n, 1408), _tile(k, 1408)
    nk = k // tk
    dims = {"nn": NN, "nt": NT, "tn": TN}[mode]
    has_res = res is not None

    def body(*refs):
        if has_res:
            a_ref, b_ref, r_ref, o_ref, acc_ref = refs
        else:
            a_ref, b_ref, o_ref, acc_ref = refs
        kk = pl.program_id(2)

        @pl.when(kk == 0)
        def _():
            acc_ref[...] = jnp.zeros(acc_ref.shape, F32)

        acc_ref[...] += _dot(a_ref[...], b_ref[...], dims)

        @pl.when(kk == nk - 1)
        def _():
            out = acc_ref[...]
            if scale != 1.0:
                out = out * scale
            if has_res:
                out = out + r_ref[...]
            o_ref[...] = out.astype(o_ref.dtype)

    a_spec = (pl.BlockSpec((tk, tm), lambda i, j, kk: (kk, i)) if mode == "tn"
              else pl.BlockSpec((tm, tk), lambda i, j, kk: (i, kk)))
    b_spec = (pl.BlockSpec((tn, tk), lambda i, j, kk: (j, kk)) if mode == "nt"
              else pl.BlockSpec((tk, tn), lambda i, j, kk: (kk, j)))
    in_specs = [a_spec, b_spec]
    args = [a, b]
    if has_res:
        in_specs.append(pl.BlockSpec((tm, tn), lambda i, j, kk: (i, j)))
        args.append(res)
    order = ("parallel", "parallel", "arbitrary") if side is None else ("arbitrary",) * 3
    return _pcall(
        body, side=side, name=name, grid=(m // tm, n // tn, nk),
        in_specs=in_specs,
        out_specs=pl.BlockSpec((tm, tn), lambda i, j, kk: (i, j)),
        out_shape=jax.ShapeDtypeStruct((m, n), out_dtype),
        scratch_shapes=[pltpu.VMEM((tm, tn), F32)],
        compiler_params=_params(dimension_semantics=order),
    )(*args)


def _rms(x, g):
    return x * lax.rsqrt(jnp.mean(x * x, axis=-1, keepdims=True) + RMS_EPS) * g


def _silu_mul(gate, up):
    return gate / (1.0 + jnp.exp(-gate)) * up


def mm_gate_up(h, w_gu, name, side=None):
    m, k = h.shape
    f = w_gu.shape[0] // 2
    tm, tn = _tile(m, 1024), _tile(f, 1408)
    nj = f // tn
    assert k <= 1408

    def body(h_ref, wg_ref, wu_ref, g_ref, u_ref, a_ref):
        ht = h_ref[...]
        for lo in range(0, tn, 512):
            cols = slice(lo, min(lo + 512, tn))
            gate, up = _dot(ht, wg_ref[cols, :], NT), _dot(ht, wu_ref[cols, :], NT)
            g_ref[:, cols] = gate.astype(g_ref.dtype)
            u_ref[:, cols] = up.astype(u_ref.dtype)
            a_ref[:, cols] = _silu_mul(gate, up).astype(a_ref.dtype)

    tile = pl.BlockSpec((tm, tn), lambda i, j: (i, j))
    return _pcall(
        body, side=side, name=name, grid=(m // tm, nj),
        in_specs=[pl.BlockSpec((tm, k), lambda i, j: (i, 0)),
                  pl.BlockSpec((tn, k), lambda i, j: (j, 0)),
                  pl.BlockSpec((tn, k), lambda i, j: (j + nj, 0))],
        out_specs=[tile, tile, tile],
        out_shape=[jax.ShapeDtypeStruct((m, f), BF16), jax.ShapeDtypeStruct((m, f), BF16),
                   jax.ShapeDtypeStruct((m, f), BF16)],
        compiler_params=_params(dimension_semantics=("arbitrary",) * 2),
    )(h, w_gu, w_gu)


def mm_down_act_bwd(dy, w_down, gate, up, name, side=None):
    m, d = dy.shape
    f = w_down.shape[0]
    tm, tn = _tile(m, 1024), _tile(f, 1408)
    assert d <= 1408

    def body(dy_ref, w_ref, g_ref, u_ref, dg_ref, du_ref):
        dyt = dy_ref[...].astype(BF16)
        for lo in range(0, tn, 512):
            cols = slice(lo, min(lo + 512, tn))
            da = _dot(dyt, w_ref[cols, :], NT) * 0.5
            gate, up = g_ref[:, cols].astype(F32), u_ref[:, cols].astype(F32)
            s = 1.0 / (1.0 + jnp.exp(-gate))
            gs = gate * s
            du_ref[:, cols] = (da * gs).astype(du_ref.dtype)
            dg_ref[:, cols] = (da * up * s * (1.0 + gate - gs)).astype(dg_ref.dtype)

    tile = pl.BlockSpec((tm, tn), lambda i, j: (i, j))
    return _pcall(
        body, side=side, name=name, grid=(m // tm, f // tn),
        in_specs=[pl.BlockSpec((tm, d), lambda i, j: (i, 0)), pl.BlockSpec((tn, d), lambda i, j: (j, 0)), tile, tile],
        out_specs=[tile, tile],
        out_shape=[jax.ShapeDtypeStruct((m, f), BF16), jax.ShapeDtypeStruct((m, f), BF16)],
        compiler_params=_params(dimension_semantics=("arbitrary", "arbitrary")),
    )(dy, w_down, gate, up)


def mm_norm_bwd(a, b, x, g, dres, name, b_kd=False, side=None):
    halves = isinstance(a, (tuple, list))
    a0, a1 = a if halves else (a, None)
    m, k = a0.shape[0], a0.shape[1] * (2 if halves else 1)
    d = b.shape[1] if b_kd else b.shape[0]
    dims = NN if b_kd else NT
    tm, tk = _tile(m, 512 if halves else 1024), _tile(a0.shape[1], 1408)
    nk = k // tk
    nkh = a0.shape[1] // tk
    has_res = dres is not None

    def body(*refs):
        a_ref, b_ref, x_ref, g_ref = refs[:4]
        rest = refs[4:-3]
        a1_ref = rest[0] if halves else None
        r_ref = rest[-1] if has_res else None
        dx_ref, dg_ref, acc_ref = refs[-3:]
        i, kk = pl.program_id(0), pl.program_id(1)

        @pl.when(kk == 0)
        def _():
            acc_ref[...] = jnp.zeros(acc_ref.shape, F32)

        if halves:
            @pl.when(kk < nkh)
            def _():
                acc_ref[...] += _dot(a_ref[...], b_ref[...], dims)

            @pl.when(kk >= nkh)
            def _():
                acc_ref[...] += _dot(a1_ref[...], b_ref[...], dims)
        else:
            acc_ref[...] += _dot(a_ref[...], b_ref[...], dims)

        @pl.when(kk == nk - 1)
        def _():
            _, vjp = jax.vjp(_rms, x_ref[...], g_ref[...])
            dx, dg = vjp(acc_ref[...])
            dx_ref[...] = dx + r_ref[...] if has_res else dx

            @pl.when(i == 0)
            def _():
                dg_ref[...] = jnp.zeros(dg_ref.shape, F32)

            dg_ref[...] += dg

    rows = pl.BlockSpec((tm, d), lambda i, kk: (i, 0))
    first = pl.BlockSpec((tm, tk), lambda i, kk: (i, jnp.minimum(kk, nkh - 1)))
    second = pl.BlockSpec((tm, tk), lambda i, kk: (i, jnp.maximum(kk - nkh, 0)))
    b_spec = pl.BlockSpec((tk, d), lambda i, kk: (kk, 0)) if b_kd else pl.BlockSpec((d, tk), lambda i, kk: (0, kk))
    in_specs = ([first, b_spec, rows, pl.BlockSpec(g.shape, lambda i, kk: (0, 0))]
                + ([second] if halves else []) + ([rows] if has_res else []))
    return _pcall(
        body, side=side, name=name, grid=(m // tm, nk),
        in_specs=in_specs,
        out_specs=[rows, pl.BlockSpec(g.shape, lambda i, kk: (0, 0))],
        out_shape=[jax.ShapeDtypeStruct((m, d), F32), jax.ShapeDtypeStruct(g.shape, F32)],
        scratch_shapes=[pltpu.VMEM((tm, d), F32)],
        compiler_params=_params(dimension_semantics=("arbitrary", "arbitrary")),
    )(*([a0, b, x, g] + ([a1] if halves else []) + ([dres] if has_res else [])))


def _indicator(shape, head_axis, mod):
    lane = lax.broadcasted_iota(jnp.int32, shape, head_axis)
    other = lax.broadcasted_iota(jnp.int32, shape, 1 - head_axis)
    lane = jnp.bitwise_and(lane, HEAD_DIM - 1) if mod else jnp.right_shift(lane, 6)
    return jnp.where(lane == other, 1.0, 0.0).astype(BF16)


def _head_rms(split, xs, g):
    w = xs.shape[1]
    to_head, from_head = _indicator((w, BLOCK), 0, False), _indicator((BLOCK, w), 1, False)
    to_lane, from_lane = _indicator((HEAD_DIM, w), 1, True), _indicator((w, HEAD_DIM), 0, True)
    ss = split(xs * xs, to_head, from_head, 3)
    r = lax.rsqrt(ss * (1.0 / HEAD_DIM) + RMS_EPS)
    g_all = split(jnp.broadcast_to(g, (8, HEAD_DIM)), to_lane, from_lane, 3)[0:1]
    return xs * split(r, from_head, to_head, 3) * g_all


def _prep(split, x, qg, kg, segs):
    parts = []
    for start, width, kind in segs:
        xs = x[:, start:start + width]
        parts.append(xs if kind == "raw" else _head_rms(split, xs, qg if kind == "q" else kg))
    return jnp.concatenate(parts, axis=1)


def prep_fwd(x, qg, kg, segs, dils, name):
    n, w = x.shape
    tm = _tile(n, 256, 8)

    def fn(ids, xt, a, b, scratch):
        ops = _prep(_plain_split, xt, a, b, segs)
        return tuple(_to_strided(scratch, ops, d) for d in dils)

    return tcall(fn, (n // tm,), [_row(x, tm), _full(qg), _full(kg)],
                 [_out((n // d, d * w), BF16, (tm // d, d * w), lambda i: (i, 0)) for d in dils], name,
                 scratch=((w // BLOCK * tm, BLOCK), F32))


def prep_bwd(x, qg, kg, segs, grads, gather, name):
    n, w = x.shape
    tm = BLOCK
    nblk = n // tm
    nslot = 1 + max(slot for _, _, _, slot in grads)

    def fn(ids, xt, a, b, *t, scratch):
        tiles, dils = [None] * nslot, [None] * nslot
        for ti, (_, sh, d, slot) in zip(t, grads):
            ti = jnp.where(ids[0] + sh < nblk, ti, 0.0) if sh else ti
            tiles[slot] = ti if tiles[slot] is None else tiles[slot] + ti
            dils[slot] = d
        tiles = [_to_natural(scratch, ti, d) for ti, d in zip(tiles, dils)]
        _, vjp = jax.vjp(lambda x_, a_, b_: _prep(_split_dot_vjp, x_, a_, b_, segs), xt, a, b)
        return vjp(gather(*tiles))

    specs = [_in(a, (tm // d, a.shape[1]), (lambda i, sh=sh: (jnp.minimum(i + sh, nblk - 1), 0)))
             for a, sh, d, _ in grads]
    wmax = max(a.shape[1] // d for a, _, d, _ in grads)
    return tcall(fn, (nblk,), [_row(x, tm), _full(qg), _full(kg)] + specs,
                 [_row_out(n, w, BF16, tm), _acc_out(qg.shape), _acc_out(kg.shape)], name,
                 scratch=((wmax // BLOCK * tm, BLOCK), F32))


def rmsnorm_fwd(x, g, name, side=None):
    n, d = x.shape
    tm = _tile(n, 512, 8)
    res = tcall(lambda ids, xt, gt: (_rms(xt, gt),), (n // tm,), [_row(x, tm), _full(g)],
                [_row_out(n, d, BF16, tm)], name, side=side)
    if side is None:
        return res[0]
    return res[0][0], res[1]


def ffn_fwd(x, g, w_gu, w_down, tag, carry=None):
    h = rmsnorm_fwd(x, g, tag + "_norm")
    if carry is None:
        gate, up, a = mm_gate_up(h, w_gu, tag + "_gu")
        return mm(a, w_down, "nn", tag + "_down", scale=0.5, res=x), (x, h, gate, up, a)
    phase, bufs = carry
    (gate, up, a), bufs = mm_gate_up(h, w_gu, tag + "_gu", side=gather_side(phase, bufs))
    y, bufs = mm(a, w_down, "nn", tag + "_down", scale=0.5, res=x, side=gather_side(phase + 1, bufs))
    return y, (x, h, gate, up, a), bufs


def ffn_bwd(dy, saved, g, w_gu, w_down, tag, chain=None):
    x, h, gate, up, a = saved

    def carrying(name, call, **kw):
        side = None if chain is None else chain.side(name)
        out = call(name=tag + "_" + name, side=side, **kw)
        if side is None:
            return out
        chain.done(name, out[1])
        return out[0]

    dgate, dup = carrying("da", mm_down_act_bwd, dy=dy, w_down=w_down, gate=gate, up=up)
    d_wdown = carrying("dwd", mm, a=a, b=dy, mode="tn", scale=0.5)
    d_wgu = (carrying("dwgu", mm, a=dgate, b=h, mode="tn"), mm(dup, h, "tn", tag + "_dwup"))
    dx, dg = carrying("dh", mm_norm_bwd, a=(dgate, dup), b=w_gu, x=x, g=g, dres=dy, b_kd=True)
    return dx, dg, d_wgu, d_wdown


def _alibi(n_heads):
    return [float(s) for s in np.asarray(2.0 ** (-8.0 * np.arange(1, n_heads + 1) / n_heads), dtype=np.float32)]


def _banded_tile(dot, first, q, kp, kc, vp, vc, sinks, *, hkv, grp, max_dist, step, slopes, want_lse):
    half, span = BLOCK // 2, BLOCK + BLOCK // 2
    row = lax.broadcasted_iota(jnp.int32, (half, span), 0)
    col = lax.broadcasted_iota(jnp.int32, (half, span), 1)
    dist = row + BLOCK - col
    inside = (dist >= 0) & (dist <= max_dist)
    distf = dist.astype(F32)

    def head(hd, qh, k2, v2):
        parts = [head_rows(hd, qh[r0:r0 + half], k2[r0:r0 + span], v2[r0:r0 + span],
                           inside & ((col + r0 >= BLOCK) | jnp.logical_not(first))) for r0 in (0, half)]
        return jnp.concatenate([p[0] for p in parts], axis=0), jnp.concatenate([p[1] for p in parts], axis=0)

    def head_rows(hd, qh, k2, v2, valid):
        s = dot(qh, k2, True) * (HEAD_DIM ** -0.5)
        s = jnp.where(valid, s - (slopes[hd] * step) * distf, NEG_BIG)
        m = jnp.max(s, axis=-1, keepdims=True)
        if sinks is not None:
            pick = lax.broadcasted_iota(jnp.int32, sinks.shape, 1) == hd
            sk = jnp.sum(jnp.where(pick, sinks, 0.0), axis=1, keepdims=True)
            m = jnp.maximum(m, sk)
        m = lax.stop_gradient(m)
        p = jnp.exp(s - m)
        denom = jnp.sum(p, axis=-1, keepdims=True)
        if sinks is not None:
            denom = denom + jnp.exp(sk - m)
        return dot(p * (1.0 / denom), v2, False), m + jnp.log(denom)

    outs, lses = [], []
    if grp == 1:
        low = lax.broadcasted_iota(jnp.int32, (BLOCK, BLOCK), 1) < HEAD_DIM
        for pr in range(hkv // 2):
            sl = slice(pr * BLOCK, (pr + 1) * BLOCK)
            q2 = q[:, sl]
            k2 = jnp.concatenate([kp[:, sl], kc[:, sl]], axis=0)
            v2 = jnp.concatenate([vp[:, sl], vc[:, sl]], axis=0)
            o0, l0 = head(2 * pr, jnp.where(low, q2, 0.0), k2, v2)
            o1, l1 = head(2 * pr + 1, jnp.where(low, 0.0, q2), k2, v2)
            outs.append(jnp.where(low, o0, o1))
            lses.append(jnp.where(low, l0, l1))
    else:
        for hk in range(hkv):
            sl = slice(hk * HEAD_DIM, (hk + 1) * HEAD_DIM)
            k2 = jnp.concatenate([kp[:, sl], kc[:, sl]], axis=0)
            v2 = jnp.concatenate([vp[:, sl], vc[:, sl]], axis=0)
            for gi in range(grp):
                hd = hk * grp + gi
                o_h, l_h = head(hd, q[:, hd * HEAD_DIM:(hd + 1) * HEAD_DIM], k2, v2)
                outs.append(o_h)
                lses.append(jnp.broadcast_to(l_h, (BLOCK, HEAD_DIM)))
    o = jnp.concatenate(outs, axis=1)
    if want_lse:
        return o, jnp.concatenate(lses, axis=1)
    return (o,)


def _banded_specs(view, qcol, kcol, vcol, wq, wkv):
    def at(colfn, prev):
        if prev:
            return lambda r, n: (jnp.maximum(n - 1, 0), colfn(r))
        return lambda r, n: (n, colfn(r))
    return [
        _in(view, (BLOCK, wq), at(qcol, False)),
        _in(view, (BLOCK, wkv), at(kcol, True)),
        _in(view, (BLOCK, wkv), at(kcol, False)),
        _in(view, (BLOCK, wkv), at(vcol, True)),
        _in(view, (BLOCK, wkv), at(vcol, False)),
    ]


def banded_fwd(view, dil, cols, sinks, cfg, name):
    ns = view.shape[0]
    nb = ns // BLOCK
    wq, wkv = cfg["hkv"] * cfg["grp"] * HEAD_DIM, cfg["hkv"] * HEAD_DIM
    has_sinks = sinks is not None

    def fn(ids, q, kp, kc, vp, vc, *rest):
        q, kp, kc, vp, vc = [a.astype(F32) for a in (q, kp, kc, vp, vc)]
        return _banded_tile(_plain_dot, ids[1] == 0, q, kp, kc, vp, vc, rest[0] if has_sinks else None, **cfg)

    ins = _banded_specs(view, *cols, wq, wkv) + ([_full(sinks)] if has_sinks else [])
    outs = [_out((ns, dil * wq), F32 if cfg["want_lse"] else BF16, (BLOCK, wq), lambda r, n: (n, r))]
    if cfg["want_lse"]:
        outs.append(_out((ns, dil * wq), F32, (BLOCK, wq), lambda r, n: (n, r)))
    return tcall(fn, (dil, nb), ins, outs, name)


def banded_bwd(view, dil, cols, sinks, cfg, cts, name):
    ns = view.shape[0]
    nb = ns // BLOCK
    wq, wkv = cfg["hkv"] * cfg["grp"] * HEAD_DIM, cfg["hkv"] * HEAD_DIM
    has_sinks = sinks is not None
    assert len(cts) == (2 if cfg["want_lse"] else 1)

    def fn(ids, q, kp, kc, vp, vc, *rest):
        sk = rest[0] if has_sinks else None
        ct = rest[1 if has_sinks else 0:]
        first = ids[1] == 0

        def f(q, kp, kc, vp, vc, *s):
            return _banded_tile(_dot_vjp, first, q, kp, kc, vp, vc, s[0] if has_sinks else None, **cfg)

        prim = tuple(a.astype(F32) for a in (q, kp, kc, vp, vc)) + ((sk,) if has_sinks else ())
        _, vjp = jax.vjp(f, *prim)
        return vjp(tuple(c.astype(F32) for c in ct))

    ins = (_banded_specs(view, *cols, wq, wkv) + ([_full(sinks)] if has_sinks else [])
           + [_in(a, (BLOCK, wq), (lambda r, n, cf=cf: (n, cf(r)))) for (a, cf) in cts])
    blk = lambda w: _out((ns, dil * w), F32, (BLOCK, w), lambda r, n: (n, r))
    outs = [blk(wq), blk(wkv), blk(wkv), blk(wkv), blk(wkv)]
    if has_sinks:
        outs.append(_acc_out(sinks.shape))
    return tcall(fn, (dil, nb), ins, outs, name)


def _log_sigmoid(z):
    return jnp.minimum(z, 0.0) - jnp.log(1.0 + jnp.exp(-jnp.abs(z)))


SB_PAIRS = 4


def _sb_pair(dot, suffix, qh, kb, vb, r_in, mask):
    z = dot(qh, kb, True) * (HEAD_DIM ** -0.5)
    lsp = _log_sigmoid(z)
    log_keep = jnp.where(mask, lsp - z, 0.0)
    log_after = suffix(log_keep) + r_in
    a = jnp.where(mask, jnp.exp(lsp + log_after), 0.0)
    return dot(a, vb, False), r_in + jnp.sum(log_keep, axis=1, keepdims=True)


def sb_fwd(qkv, qcb, kcb, vcb, name, side=None):
    s = qkv.shape[0]
    nb = s // BLOCK
    pairs = B_HEADS // 2
    wide = SB_PAIRS * BLOCK
    assert pairs % SB_PAIRS == 0 and qcb % SB_PAIRS == 0 and kcb % SB_PAIRS == 0 and vcb % SB_PAIRS == 0

    def body(q_ref, k_ref, v_ref, o_ref):
        n = pl.program_id(1)
        low = lax.broadcasted_iota(jnp.int32, (BLOCK, BLOCK), 1) < HEAD_DIM
        before = (lax.broadcasted_iota(jnp.int32, (2 * BLOCK, BLOCK), 1)
                  < jnp.bitwise_and(lax.broadcasted_iota(jnp.int32, (2 * BLOCK, BLOCK), 0), BLOCK - 1))
        after = _tri(True)
        suffix = lambda t: _split_dot(t, after)
        qs = []
        for p in range(SB_PAIRS):
            q2 = q_ref[:, p * BLOCK:(p + 1) * BLOCK].astype(F32)
            qs.append(jnp.concatenate([jnp.where(low, q2, 0.0), jnp.where(low, 0.0, q2)], axis=0))

        def cond(c):
            return jnp.logical_and(c[0] >= 0, c[1] > SB_SKIP_LOG)

        def step(c):
            kb, _, rs, accs = c
            rows = pl.ds(pl.multiple_of(kb * BLOCK, BLOCK), BLOCK)
            mask = jnp.logical_or(before, kb != n)
            new_r, new_acc, top = [], [], None
            for p in range(SB_PAIRS):
                cols = slice(p * BLOCK, (p + 1) * BLOCK)
                o_part, r_out = _sb_pair(_plain_dot, suffix, qs[p], k_ref[rows, cols], v_ref[rows, cols], rs[p], mask)
                new_r.append(r_out)
                new_acc.append(accs[p] + o_part)
                top = jnp.max(r_out) if top is None else jnp.maximum(top, jnp.max(r_out))
            return kb - 1, top, tuple(new_r), tuple(new_acc)

        init = (n, jnp.float32(0.0), tuple(jnp.zeros((2 * BLOCK, 1), F32) for _ in range(SB_PAIRS)),
                tuple(jnp.zeros((2 * BLOCK, BLOCK), F32) for _ in range(SB_PAIRS)))
        accs = lax.while_loop(cond, step, init)[3]
        for p in range(SB_PAIRS):
            o_ref[:, p * BLOCK:(p + 1) * BLOCK] = jnp.where(low, accs[p][:BLOCK], accs[p][BLOCK:]).astype(o_ref.dtype)

    return _pcall(
        body, side=side, name=name, grid=(pairs // SB_PAIRS, nb),
        in_specs=[pl.BlockSpec((BLOCK, wide), lambda g, n: (n, qcb // SB_PAIRS + g)),
                  pl.BlockSpec((s, wide), lambda g, n: (0, kcb // SB_PAIRS + g), pipeline_mode=pl.Buffered(1)),
                  pl.BlockSpec((s, wide), lambda g, n: (0, vcb // SB_PAIRS + g), pipeline_mode=pl.Buffered(1))],
        out_specs=pl.BlockSpec((BLOCK, wide), lambda g, n: (n, g)),
        out_shape=jax.ShapeDtypeStruct((s, pairs * BLOCK), BF16),
        compiler_params=_params(),
    )(qkv, qkv, qkv)


def sb_bwd(qkv, qcb, kcb, vcb, do, docb, name, side=None):
    s = qkv.shape[0]
    nb = s // BLOCK
    pairs = B_HEADS // 2
    wide = SB_PAIRS * BLOCK
    assert docb % SB_PAIRS == 0

    def body(q_ref, k_ref, v_ref, do_ref, dq_ref, dk_ref, dv_ref, r_ref):
        n = pl.program_id(1)

        @pl.when(n == 0)
        def _():
            dk_ref[...] = jnp.zeros(dk_ref.shape, F32)
            dv_ref[...] = jnp.zeros(dv_ref.shape, F32)

        low = lax.broadcasted_iota(jnp.int32, (BLOCK, BLOCK), 1) < HEAD_DIM
        before = (lax.broadcasted_iota(jnp.int32, (2 * BLOCK, BLOCK), 1)
                  < jnp.bitwise_and(lax.broadcasted_iota(jnp.int32, (2 * BLOCK, BLOCK), 0), BLOCK - 1))
        after, earlier = _tri(True), _tri(False)
        suffix = lambda t: _split_dot_vjp(t, after, earlier, 2)
        stack = lambda t: jnp.concatenate([jnp.where(low, t, 0.0), jnp.where(low, 0.0, t)], axis=0)
        qs = [stack(q_ref[:, p * BLOCK:(p + 1) * BLOCK].astype(F32)) for p in range(SB_PAIRS)]
        dos = [stack(do_ref[:, p * BLOCK:(p + 1) * BLOCK].astype(F32)) for p in range(SB_PAIRS)]

        def cond(c):
            return jnp.logical_and(c[0] >= 0, c[1] > SB_SKIP_LOG)

        def down(c):
            kb, _, rs = c
            rows = pl.ds(pl.multiple_of(kb * BLOCK, BLOCK), BLOCK)
            mask = jnp.logical_or(before, kb != n)
            new_r, top = [], None
            for h in range(SB_PAIRS):
                cols = slice(h * BLOCK, (h + 1) * BLOCK)
                r_ref[h, kb] = rs[h]
                z = _dot(qs[h], k_ref[rows, cols], NT) * (HEAD_DIM ** -0.5)
                log_keep = jnp.where(mask, _log_sigmoid(z) - z, 0.0)
                r_out = rs[h] + jnp.sum(log_keep, axis=1, keepdims=True)
                new_r.append(r_out)
                top = jnp.max(r_out) if top is None else jnp.maximum(top, jnp.max(r_out))
            return kb - 1, top, tuple(new_r)

        init = (n, jnp.float32(0.0), tuple(jnp.zeros((2 * BLOCK, 1), F32) for _ in range(SB_PAIRS)))
        last = lax.while_loop(cond, down, init)[0] + 1

        def up(kb, c):
            dqs, g_rs = c
            rows = pl.ds(pl.multiple_of(kb * BLOCK, BLOCK), BLOCK)
            mask = jnp.logical_or(before, kb != n)
            new_dq, new_g = [], []
            for h in range(SB_PAIRS):
                cols = slice(h * BLOCK, (h + 1) * BLOCK)
                _, vjp = jax.vjp(lambda q_, k_, v_, r_: _sb_pair(_dot_vjp, suffix, q_, k_, v_, r_, mask),
                                 qs[h], k_ref[rows, cols].astype(F32), v_ref[rows, cols].astype(F32), r_ref[h, kb])
                dq_c, dk_c, dv_c, g_in = vjp((dos[h], g_rs[h]))
                dk_ref[rows, cols] += dk_c
                dv_ref[rows, cols] += dv_c
                new_dq.append(dqs[h] + dq_c)
                new_g.append(g_in)
            return tuple(new_dq), tuple(new_g)

        init = (tuple(jnp.zeros((2 * BLOCK, BLOCK), F32) for _ in range(SB_PAIRS)),
                tuple(jnp.zeros((2 * BLOCK, 1), F32) for _ in range(SB_PAIRS)))
        dqs = lax.fori_loop(last, n + 1, up, init)[0]
        for p in range(SB_PAIRS):
            dq_ref[:, p * BLOCK:(p + 1) * BLOCK] = jnp.where(low, dqs[p][:BLOCK], dqs[p][BLOCK:])

    full = jax.ShapeDtypeStruct((s, pairs * BLOCK), F32)
    return _pcall(
        body, side=side, name=name, grid=(pairs // SB_PAIRS, nb),
        in_specs=[pl.BlockSpec((BLOCK, wide), lambda g, n: (n, qcb // SB_PAIRS + g)),
                  pl.BlockSpec((s, wide), lambda g, n: (0, kcb // SB_PAIRS + g), pipeline_mode=pl.Buffered(1)),
                  pl.BlockSpec((s, wide), lambda g, n: (0, vcb // SB_PAIRS + g), pipeline_mode=pl.Buffered(1)),
                  pl.BlockSpec((BLOCK, wide), lambda g, n: (n, docb // SB_PAIRS + g))],
        out_specs=[pl.BlockSpec((BLOCK, wide), lambda g, n: (n, g)),
                   pl.BlockSpec((s, wide), lambda g, n: (0, g), pipeline_mode=pl.Buffered(1)),
                   pl.BlockSpec((s, wide), lambda g, n: (0, g), pipeline_mode=pl.Buffered(1))],
        out_shape=[full, full, full],
        scratch_shapes=[pltpu.VMEM((SB_PAIRS, nb, 2 * BLOCK, 1), F32)],
        compiler_params=_params(),
    )(qkv, qkv, qkv, do)


def _xa_tile(dot, q, kv, qg, kg):
    hd = q.shape[1] // X_HEADS
    outs = []
    for h in range(X_HEADS):
        qh = _rms(q[:, h * hd:(h + 1) * hd], qg)
        kh = _rms(kv[:, h * hd:(h + 1) * hd], kg)
        vh = kv[:, (X_HEADS + h) * hd:(X_HEADS + h + 1) * hd]
        sc = dot(qh, kh, True) * (hd ** -0.5)
        m = lax.stop_gradient(jnp.max(sc, axis=-1, keepdims=True))
        p = jnp.exp(sc - m)
        outs.append(dot(p * (1.0 / jnp.sum(p, axis=-1, keepdims=True)), vh, False))
    return jnp.concatenate(outs, axis=1)


def xa_core_fwd(q, kv, qg, kg, name):
    n, d = q.shape
    tm = _tile(n, 256, 8)
    (o,) = tcall(lambda ids, qt, kvt, qgt, kgt: (_xa_tile(_plain_dot, qt, kvt, qgt, kgt),), (n // tm,),
                 [_row(q, tm), _full(kv), _full(qg), _full(kg)], [_row_out(n, d, BF16, tm)], name)
    return o


def xa_core_bwd(q, kv, qg, kg, do, name):
    n, d = q.shape
    tm = _tile(n, 256, 8)

    def fn(ids, qt, kvt, qgt, kgt, dot_):
        _, vjp = jax.vjp(functools.partial(_xa_tile, _dot_vjp), qt, kvt, qgt, kgt)
        return vjp(dot_.astype(F32))

    return tcall(fn, (n // tm,), [_row(q, tm), _full(kv), _full(qg), _full(kg), _row(do, tm)],
                 [_row_out(n, d, BF16, tm), _acc_out(kv.shape), _acc_out(qg.shape), _acc_out(kg.shape)], name)


def _ev_reorder(a):
    return jnp.concatenate([a[0:512], a[768:2304], a[512:768]], axis=0)


def _ev_restore(a):
    return jnp.concatenate([a[0:512], a[2048:2304], a[512:2048]], axis=0)


_EV_SEGS = ((0, 512, "q"), (512, 1536, "raw"), (2048, 128, "k"), (2176, 128, "raw"))
_A_CFG = dict(hkv=A_KV_HEADS, grp=A_Q_HEADS // A_KV_HEADS, max_dist=BLOCK - 1, step=1.0, slopes=_alibi(A_Q_HEADS),
              want_lse=False)
_A_COLS = (lambda r: 0, lambda r: 16, lambda r: 17)


def even_mixer_fwd(x, h, w_in, qg, kg, sinks, w_out, tag, side=None):
    qkv = mm(h, w_in, "nt", tag + "_in")
    (ops,) = prep_fwd(qkv, qg, kg, _EV_SEGS, (1,), tag + "_prep")
    (o_a,) = banded_fwd(ops, 1, _A_COLS, sinks, _A_CFG, tag + "_swa")
    o_b = sb_fwd(ops, 4, 8, 12, tag + "_sb", side=side)
    carried = None
    if side is not None:
        o_b, carried = o_b
    o = jnp.concatenate([o_a, o_b], axis=1)
    y = mm(o, w_out, "nn", tag + "_out", res=x)
    return y, (x, h, qkv, ops, o), carried


def even_mixer_bwd(dy, saved, g, w_in, qg, kg, sinks, w_out, tag, side=None, last_side=None):
    x, h, qkv, ops, o = saved
    do = mm(dy, w_out, "nt", tag + "_do", out_dtype=BF16)
    d_wout = mm(o, dy, "tn", tag + "_dwout")
    dqa, dkp, dkc, dvp, dvc, dsinks = banded_bwd(ops, 1, _A_COLS, sinks, _A_CFG, [(do, lambda r: 0)], tag + "_dswa")
    res = sb_bwd(ops, 4, 8, 12, do, 4, tag + "_dsb", side=side)
    carried = None
    if side is not None:
        res, carried = res
    dqb, dkb, dvb = res
    dqkv, dqg, dkg = prep_bwd(
        qkv, qg, kg, _EV_SEGS,
        [(dqa, 0, 1, 0), (dqb, 0, 1, 1), (dkb, 0, 1, 2), (dvb, 0, 1, 3), (dkc, 0, 1, 4), (dkp, 1, 1, 4), (dvc, 0, 1, 5),
         (dvp, 1, 1, 5)],
        lambda *t: jnp.concatenate(t, axis=1), tag + "_dqkv")
    d_win = mm(dqkv, h, "tn", tag + "_dwin")
    last = None if last_side is None else last_side(d_win, d_wout)
    if last is None:
        dx, dg = mm_norm_bwd(dqkv, w_in, x, g, dy, tag + "_dh", b_kd=True)
    else:
        (dx, dg), got = mm_norm_bwd(dqkv, w_in, x, g, dy, tag + "_dh", b_kd=True, side=last[0])
        last[1](got)
    return dx, dg, d_win, dqg, dkg, dsinks, d_wout, carried


def _c_cfg(window, dil):
    return dict(hkv=C_HEADS, grp=1, max_dist=window // dil, step=float(dil), slopes=_alibi(C_HEADS), want_lse=True)


_C_COLS = (lambda r: 3 * r, lambda r: 3 * r + 1, lambda r: 3 * r + 2)
_OD_SEGS = ((0, 1024, "q"), (1024, 1024, "k"), (2048, 1024, "raw"))


def _combine(o1, o2, o3, l1, l2, l3):
    m = lax.stop_gradient(jnp.maximum(jnp.maximum(l1, l2), l3))
    e1, e2, e3 = jnp.exp(l1 - m), jnp.exp(l2 - m), jnp.exp(l3 - m)
    tot = e1 + e2 + e3
    return (e1 / tot) * o1 + (e2 / tot) * o2 + (e3 / tot) * o3


def odd_mixer_fwd(x, g, w_in, qg, kg, w_out, tag):
    n, d = x.shape
    h = rmsnorm_fwd(x, g, tag + "_norm")
    qkv = mm(h, w_in, "nt", tag + "_in")
    dils = [dil for _, dil in C_PATTERNS]
    ops = prep_fwd(qkv, qg, kg, _OD_SEGS, dils, tag + "_prep")
    os_, ls_ = [], []
    for (window, dil), ops_d in zip(C_PATTERNS, ops):
        o_p, l_p = banded_fwd(ops_d, dil, _C_COLS, None, _c_cfg(window, dil), f"{tag}_dil{dil}")
        os_.append(o_p)
        ls_.append(l_p)
    tm = BLOCK
    lay = lambda a, dil: _in(a, (tm // dil, a.shape[1]), lambda i: (i, 0))
    views = [lay(a, dil) for a, dil in zip(os_ + ls_, dils + dils)]

    def comb(ids, *t, scratch):
        return (_combine(*[_to_natural(scratch, a, dil) for a, dil in zip(t, dils + dils)]),)

    (o,) = tcall(comb, (n // tm,), views, [_row_out(n, d, BF16, tm)], tag + "_comb",
                 scratch=((d // BLOCK * tm, BLOCK), F32))
    y = mm(o, w_out, "nn", tag + "_out", res=x)
    return y, (x, h, qkv, ops, views, o)


def odd_mixer_bwd(dy, saved, g, w_in, qg, kg, w_out, tag):
    x, h, qkv, ops, views, o = saved
    n, d = x.shape
    do = mm(dy, w_out, "nt", tag + "_do")
    d_wout = mm(o, dy, "tn", tag + "_dwout")
    tm = BLOCK
    dils = [dil for _, dil in C_PATTERNS]

    def comb_bwd(ids, *t, scratch):
        _, vjp = jax.vjp(_combine, *[_to_natural(scratch, a, dil) for a, dil in zip(t[:6], dils + dils)])
        return tuple(_to_strided(scratch, c, dil) for c, dil in zip(vjp(t[6]), dils + dils))

    cts = tcall(comb_bwd, (n // tm,), views + [_row(do, tm)],
                [_out((n // dil, dil * d), F32, (tm // dil, dil * d), lambda i: (i, 0)) for dil in dils + dils],
                tag + "_dcomb", scratch=((d // BLOCK * tm, BLOCK), F32))
    dqs, dks, dvs = [], [], []
    for p, ((window, dil), ops_d) in enumerate(zip(C_PATTERNS, ops)):
        dq, dkp, dkc, dvp, dvc = banded_bwd(ops_d, dil, _C_COLS, None, _c_cfg(window, dil),
                                            [(cts[p], lambda r: r), (cts[3 + p], lambda r: r)], f"{tag}_ddil{dil}")
        dqs.append((dq, 0, dil, p))
        dks += [(dkc, 0, dil, 3 + p), (dkp, dil, dil, 3 + p)]
        dvs += [(dvc, 0, dil, 6 + p), (dvp, dil, dil, 6 + p)]

    def gather(*t):
        return jnp.concatenate([t[0] + t[1] + t[2], t[3] + t[4] + t[5], t[6] + t[7] + t[8]], axis=1)

    dqkv, dqg, dkg = prep_bwd(qkv, qg, kg, _OD_SEGS, dqs + dks + dvs, gather, tag + "_dqkv")
    d_win = mm(dqkv, h, "tn", tag + "_dwin")
    dx, dg = mm_norm_bwd(dqkv, w_in, x, g, dy, tag + "_dh", b_kd=True)
    return dx, dg, d_win, dqg, dkg, d_wout


def xa_fwd(x, mem, g, gm, w_q, w_kv, qg, kg, w_o, tag):
    h = rmsnorm_fwd(x, g, tag + "_norm")
    q = mm(h, w_q, "nn", tag + "_q")
    mn = rmsnorm_fwd(mem, gm, tag + "_mnorm")
    kv = mm(mn, w_kv, "nt", tag + "_kv")
    o = xa_core_fwd(q, kv, qg, kg, tag + "_core")
    y = mm(o, w_o, "nn", tag + "_o", res=x)
    return y, (x, h, q, mn, kv, o)


def xa_bwd(dy, saved, mem, g, gm, w_q, w_kv, qg, kg, w_o, tag):
    x, h, q, mn, kv, o = saved
    do = mm(dy, w_o, "nt", tag + "_do", out_dtype=BF16)
    d_wo = mm(o, dy, "tn", tag + "_dwo")
    dq, dkv, dqg, dkg = xa_core_bwd(q, kv, qg, kg, do, tag + "_dcore")
    d_wq = mm(h, dq, "tn", tag + "_dwq")
    dx, dg = mm_norm_bwd(dq, w_q, x, g, dy, tag + "_dh")
    d_wkv = mm(dkv, mn, "tn", tag + "_dwkv")
    _, dgm = mm_norm_bwd(dkv, w_kv, mem, gm, None, tag + "_dmn", b_kd=True)
    return dx, dg, dgm, d_wq, d_wkv, dqg, dkg, d_wo


def loss_head(y, target, name):
    n, d = y.shape
    tm = _tile(n, 512, 8)

    def fn(ids, yt, tt):
        e = yt - tt
        return e * (1.0 / d), jnp.sum(e * e, axis=0, keepdims=True)

    return tcall(fn, (n // tm,), [_row(y, tm), _row(target, tm)], [_row_out(n, d, F32, tm), _acc_out((1, d))], name)


_ANY = pl.BlockSpec(memory_space=pl.ANY)


def all_gather_blocks(blocks):
    nb = len(blocks)

    def body(*refs):
        x_refs, out_refs = refs[:nb], refs[nb:2 * nb]
        send_sems, recv_sems, local_sems = refs[2 * nb:]
        x, y, c = lax.axis_index("x"), lax.axis_index("y"), lax.axis_index("c")
        me, sibling = (x, y, c), (x, y, 1 - c)
        over_x, over_y, diagonal = (1 - x, y), (x, 1 - y), (1 - x, 1 - y)
        relay_of = ((1 - x) * (1 - c) + x * c, y * (1 - c) + (1 - y) * c)
        relay_to = (x * (1 - c) + (1 - x) * c, (1 - y) * (1 - c) + y * c)

        def copy(b, k, blk, to, own=False):
            px, py, pc = blk
            slot = out_refs[b].at[4 * px + 2 * py + pc]
            return pltpu.make_async_remote_copy(
                src_ref=x_refs[b] if own else slot, dst_ref=slot,
                send_sem=send_sems.at[7 * b + k], recv_sem=recv_sems.at[7 * b + k], device_id=to, device_id_type=MESH)

        mine = [pltpu.make_async_copy(x_refs[b], out_refs[b].at[4 * x + 2 * y + c], local_sems.at[b]) for b in range(nb)]
        for cp in mine:
            cp.start()
        sent = []
        for b in range(nb):
            sent += [copy(b, 0, me, sibling, own=True), copy(b, 1, me, (*over_x, c), own=True),
                     copy(b, 2, me, (*over_y, c), own=True)]
        for cp in sent:
            cp.start()
        for b in range(nb):
            copy(b, 1, (*over_x, c), me).wait_recv()
            copy(b, 2, (*over_y, c), me).wait_recv()
            later = [copy(b, 3, (*relay_of, c), (*relay_to, c)), copy(b, 4, (*over_x, c), sibling),
                     copy(b, 5, (*over_y, c), sibling)]
            for cp in later:
                cp.start()
            sent += later
        for b in range(nb):
            copy(b, 3, (*diagonal, c), me).wait_recv()
            fwd = copy(b, 6, (*diagonal, c), sibling)
            fwd.start()
            sent.append(fwd)
        for b in range(nb):
            copy(b, 0, sibling, me).wait_recv()
            for k, chip in ((4, over_x), (5, over_y), (6, diagonal)):
                copy(b, k, (*chip, 1 - c), me).wait_recv()
        for cp in sent:
            cp.wait_send()
        for cp in mine:
            cp.wait()

    return _pcall(
        body, name="weights_all_gather",
        in_specs=[_ANY] * nb, out_specs=[_ANY] * nb,
        out_shape=[jax.ShapeDtypeStruct((N_DEV,) + a.shape, a.dtype) for a in blocks],
        scratch_shapes=[pltpu.SemaphoreType.DMA((7 * nb,)), pltpu.SemaphoreType.DMA((7 * nb,)),
                        pltpu.SemaphoreType.DMA((nb,))],
    )(*blocks)


def pair_exchange(bufs):
    nb = len(bufs)

    def body(*refs):
        srcs, dsts = refs[:nb], refs[nb:2 * nb]
        send_sems, recv_sems = refs[2 * nb:]
        x, y, c = lax.axis_index("x"), lax.axis_index("y"), lax.axis_index("c")
        copies = []
        for b in range(nb):
            for j in range(4):
                cp = pltpu.make_async_remote_copy(
                    src_ref=srcs[b].at[2 * j + (1 - c)], dst_ref=dsts[b].at[j], send_sem=send_sems.at[4 * b + j],
                    recv_sem=recv_sems.at[4 * b + j], device_id=(x, y, 1 - c), device_id_type=MESH)
                cp.start()
                copies.append(cp)
        for cp in copies:
            cp.wait()

    return _pcall(
        body, name="grads_pair_exchange",
        in_specs=[_ANY] * nb, out_specs=[_ANY] * nb,
        out_shape=[jax.ShapeDtypeStruct((4,) + a.shape[1:], a.dtype) for a in bufs],
        scratch_shapes=[pltpu.SemaphoreType.DMA((4 * nb,)), pltpu.SemaphoreType.DMA((4 * nb,))],
    )(*bufs)


def pair_sum(g, got, c, out_dtype, name):
    r, w = g.shape[1:]
    tr = _tile(r, 1024, 16)

    def body(c_ref, a_ref, b_ref, o_ref):
        o_ref[...] = (a_ref[...].astype(F32) + b_ref[...].astype(F32)).astype(o_ref.dtype)

    return _pcall(
        body, name=name,
        grid_spec=pltpu.PrefetchScalarGridSpec(
            num_scalar_prefetch=1, grid=(4, r // tr),
            in_specs=[pl.BlockSpec((None, tr, w), lambda j, i, c_ref: (2 * j + c_ref[0], i, 0)),
                      pl.BlockSpec((None, tr, w), lambda j, i, c_ref: (j, i, 0))],
            out_specs=pl.BlockSpec((None, tr, w), lambda j, i, c_ref: (j, i, 0))),
        out_shape=jax.ShapeDtypeStruct((4,) + g.shape[1:], out_dtype),
        compiler_params=_params(),
    )(c, g, got)


def chip_exchange(parts):
    nb = len(parts)

    def body(*refs):
        srcs, dsts = refs[:nb], refs[nb:2 * nb]
        send_sems, recv_sems, local_sems = refs[2 * nb:]
        x, y, c = lax.axis_index("x"), lax.axis_index("y"), lax.axis_index("c")
        my_chip = 2 * x + y
        copies = []
        for b in range(nb):
            mine = pltpu.make_async_copy(srcs[b].at[my_chip], dsts[b].at[my_chip], local_sems.at[b])
            mine.start()
            copies.append(mine)
            for k, (tx, ty) in enumerate([(1 - x, y), (x, 1 - y), (1 - x, 1 - y)]):
                cp = pltpu.make_async_remote_copy(
                    src_ref=srcs[b].at[2 * tx + ty], dst_ref=dsts[b].at[my_chip], send_sem=send_sems.at[3 * b + k],
                    recv_sem=recv_sems.at[3 * b + k], device_id=(tx, ty, c), device_id_type=MESH)
                cp.start()
                copies.append(cp)
        for cp in copies:
            cp.wait()

    return _pcall(
        body, name="grads_chip_exchange",
        in_specs=[_ANY] * nb, out_specs=[_ANY] * nb,
        out_shape=[jax.ShapeDtypeStruct(a.shape, a.dtype) for a in parts],
        scratch_shapes=[pltpu.SemaphoreType.DMA((3 * nb,)), pltpu.SemaphoreType.DMA((3 * nb,)),
                        pltpu.SemaphoreType.DMA((nb,))],
    )(*parts)


def chip_sum(parts, name):
    r, w = parts.shape[1:]
    tr = _tile(r, 1024, 16)
    spec = lambda j: _in(parts, (None, tr, w), lambda i, j=j: (j, i, 0))

    def fn(ids, a, b, c_, d):
        a, b, c_, d = [t.astype(F32) for t in (a, b, c_, d)]
        return (((a + b) + c_) + d,)

    (out,) = tcall(fn, (r // tr,), [spec(j) for j in range(4)],
                   [_out((r, w), F32, (tr, w), lambda i: (i, 0))], name)
    return out


def _remote(src, dst, send_sems, recv_sems, k, to):
    return functools.partial(pltpu.make_async_remote_copy, src_ref=src, dst_ref=dst, send_sem=send_sems.at[k],
                             recv_sem=recv_sems.at[k], device_id=to, device_id_type=MESH)


def _gather_plan(phase, nb):
    def plan(ins, outs, send_sems, recv_sems, local_sems):
        x, y, c = lax.axis_index("x"), lax.axis_index("y"), lax.axis_index("c")
        me, sibling = (x, y, c), (x, y, 1 - c)
        over_x, over_y, diagonal = (1 - x, y), (x, 1 - y), (1 - x, 1 - y)
        relay_of = ((1 - x) * (1 - c) + x * c, y * (1 - c) + (1 - y) * c)
        relay_to = (x * (1 - c) + (1 - x) * c, (1 - y) * (1 - c) + y * c)
        local, sends, recvs = [], [], []
        for b in range(nb):
            slot = lambda chip, core, b=b: outs[b].at[4 * chip[0] + 2 * chip[1] + core]
            if phase == 0:
                local.append(functools.partial(pltpu.make_async_copy, ins[b], slot((x, y), c), local_sems.at[b]))
                moves = [(ins[b], slot((x, y), c), to) for to in (sibling, (*over_x, c), (*over_y, c))]
                arrive = [slot((x, y), 1 - c), slot(over_x, c), slot(over_y, c)]
            elif phase == 1:
                moves = [(slot(relay_of, c), slot(relay_of, c), (*relay_to, c)),
                         (slot(over_x, c), slot(over_x, c), sibling), (slot(over_y, c), slot(over_y, c), sibling)]
                arrive = [slot(diagonal, c), slot(over_x, 1 - c), slot(over_y, 1 - c)]
            else:
                moves = [(slot(diagonal, c), slot(diagonal, c), sibling)]
                arrive = [slot(diagonal, 1 - c)]
            sends += [_remote(src, dst, send_sems, recv_sems, 3 * b + k, to) for k, (src, dst, to) in enumerate(moves)]
            recvs += [_remote(dst, dst, send_sems, recv_sems, 3 * b + k, me) for k, dst in enumerate(arrive)]
        return local, sends, recvs
    return plan


def gather_side(phase, arrays):
    nb = len(arrays)
    if phase == 0:
        shapes = [jax.ShapeDtypeStruct((N_DEV,) + a.shape, a.dtype) for a in arrays]
        return Side(arrays, shapes, 3 * nb, nb, _gather_plan(0, nb))
    shapes = [jax.ShapeDtypeStruct(a.shape, a.dtype) for a in arrays]
    return Side(arrays, shapes, 3 * nb, 0, _gather_plan(phase, nb), aliased=True)


def pair_side(bufs):
    nb = len(bufs)

    def plan(ins, outs, send_sems, recv_sems, local_sems):
        x, y, c = lax.axis_index("x"), lax.axis_index("y"), lax.axis_index("c")
        sends = [_remote(ins[b].at[2 * j + (1 - c)], outs[b].at[j], send_sems, recv_sems, 4 * b + j, (x, y, 1 - c))
                 for b in range(nb) for j in range(4)]
        recvs = [_remote(outs[b].at[j], outs[b].at[j], send_sems, recv_sems, 4 * b + j, (x, y, c))
                 for b in range(nb) for j in range(4)]
        return [], sends, recvs

    shapes = [jax.ShapeDtypeStruct((4,) + a.shape[1:], a.dtype) for a in bufs]
    return Side(bufs, shapes, 4 * nb, 0, plan)


def chip_side(parts):
    nb = len(parts)

    def plan(ins, outs, send_sems, recv_sems, local_sems):
        x, y, c = lax.axis_index("x"), lax.axis_index("y"), lax.axis_index("c")
        my_chip = 2 * x + y
        peers = [(1 - x, y), (x, 1 - y), (1 - x, 1 - y)]
        local = [functools.partial(pltpu.make_async_copy, ins[b].at[my_chip], outs[b].at[my_chip], local_sems.at[b])
                 for b in range(nb)]
        sends = [_remote(ins[b].at[2 * tx + ty], outs[b].at[my_chip], send_sems, recv_sems, 3 * b + k, (tx, ty, c))
                 for b in range(nb) for k, (tx, ty) in enumerate(peers)]
        recvs = [_remote(outs[b].at[2 * tx + ty], outs[b].at[2 * tx + ty], send_sems, recv_sems, 3 * b + k, (x, y, c))
                 for b in range(nb) for k, (tx, ty) in enumerate(peers)]
        return local, sends, recvs

    shapes = [jax.ShapeDtypeStruct(a.shape, a.dtype) for a in parts]
    return Side(parts, shapes, 3 * nb, nb, plan)


def adamw(w, g, m, v, name):
    shape = w.shape
    cols = shape[-1]
    rows = int(np.prod(shape[:-1]))
    w2, g2, m2, v2 = [a.reshape(rows, cols) for a in (w, g, m, v)]
    tr = _tile(rows, 512, 8) if rows % 8 == 0 else rows

    def fn(ids, wt, gt, mt, vt):
        m_new = ADAM_B1 * mt + (1.0 - ADAM_B1) * gt
        v_new = ADAM_B2 * vt + (1.0 - ADAM_B2) * (gt * gt)
        m_hat = m_new / (1.0 - ADAM_B1 ** ADAM_STEP)
        v_hat = v_new / (1.0 - ADAM_B2 ** ADAM_STEP)
        delta = -ADAM_LR * (m_hat / (jnp.sqrt(v_hat) + ADAM_EPS) + ADAM_WD * wt)
        return delta, m_new, v_new

    res = tcall(fn, (rows // tr,), [_row(a, tr) for a in (w2, g2, m2, v2)],
                [_row_out(rows, cols, F32, tr) for _ in range(3)], name)
    return [a.reshape(shape) for a in res]


_MATS = [("ffn1_w_gu", "col"), ("ffn1_w_down", "row"), ("ev_w_in", "col"), ("ev_w_out", "row"),
         ("od_w_in", "col"), ("od_w_out", "row"), ("xa_w_q", "row"), ("xa_w_kv", "col"), ("xa_w_o", "row"),
         ("ffn2_w_gu", "col"), ("ffn2_w_down", "row")]
_VECS = ["ffn1_norm", "mix_norm", "ev_q_gain", "ev_k_gain", "ev_sinks", "od_q_gain", "od_k_gain", "xa_norm",
         "xa_mem_norm", "xa_q_gain", "xa_k_gain", "ffn2_norm"]
_WEIGHTS = ["ffn1_norm", "ffn1_w_gu", "ffn1_w_down", "mix_norm", "ev_w_in", "ev_q_gain", "ev_k_gain", "ev_sinks",
            "ev_w_out", "od_w_in", "od_q_gain", "od_k_gain", "od_w_out", "xa_norm", "xa_mem_norm", "xa_w_q", "xa_w_kv",
            "xa_q_gain", "xa_k_gain", "xa_w_o", "ffn2_norm", "ffn2_w_gu", "ffn2_w_down"]


_AXIS = dict(_MATS)
DEPTH = 2


def _layer_groups(l):
    first, rest = _first_block_groups(l)
    return [first[0] + rest[0] + rest[1]]


def _first_block_groups(l):
    w_in, w_out = ("ev_w_in", "ev_w_out") if l % 2 == 0 else ("od_w_in", "od_w_out")
    first = [[("ffn1_w_gu", l), ("ffn1_w_down", l)]]
    rest = [[("ffn2_w_gu", l), ("xa_w_kv", l)],
            [(w_in, l // 2), ("ffn2_w_down", l), (w_out, l // 2), ("xa_w_q", l), ("xa_w_o", l)]]
    return first, rest


def _block_rows(shards, n):
    a, b = shards[n].shape[1:]
    return a if _AXIS[n] == "row" else b


def _weight_blocks(shards, groups):
    blocks = []
    for group in groups:
        rows = [(shards[n][j] if _AXIS[n] == "row" else shards[n][j].T).astype(BF16) for n, j in group]
        blocks.append(rows[0] if len(rows) == 1 else jnp.concatenate(rows, axis=0))
    return blocks


def _whole_weights(shards, groups, gathered):
    full = {}
    for group, got in zip(groups, gathered):
        off = 0
        for n, j in group:
            r = _block_rows(shards, n)
            full[n] = got[:, off:off + r, :].reshape(N_DEV * r, got.shape[2])
            off += r
    return full


def _gradient_buffers(grads, groups):
    bufs = []
    for group in groups:
        rows = []
        for n, _ in group:
            whole = jnp.concatenate(grads[n], axis=0) if isinstance(grads[n], tuple) else grads[n]
            rows.append(whole.reshape(N_DEV, whole.shape[0] // N_DEV, whole.shape[1]))
        bufs.append((rows[0] if len(rows) == 1 else jnp.concatenate(rows, axis=1)).astype(BF16))
    return bufs


def _gradient_blocks(shards, groups, sums):
    out = {}
    for group, tot in zip(groups, sums):
        off = 0
        for n, j in group:
            r = _block_rows(shards, n)
            out[n, j] = tot[off:off + r] if _AXIS[n] == "row" else tot[off:off + r].T
            off += r
    return out


class _PairChain:
    def __init__(self, ex, bufs):
        self.ex, self.bufs, self.parts = ex, bufs, None

    def side(self, name):
        return pair_side(self.bufs) if name == "da" else None

    def done(self, name, carried):
        self.parts = self.ex.pair_sums(self.bufs, carried, "l1")


class _RestChain:
    HALF = {"da": (0,), "dh": (1,)}

    def __init__(self, ex, bufs):
        self.ex, self.bufs, self.parts, self.sums = ex, bufs, None, [None] * len(bufs)

    def side(self, name):
        if name == "pair":
            return pair_side(self.bufs)
        if name in self.HALF:
            return chip_side([self.parts[i] for i in self.HALF[name]])
        return None

    def done(self, name, carried):
        if name == "pair":
            self.parts = self.ex.pair_sums(self.bufs, carried, "l0r")
        else:
            for i, tot in zip(self.HALF[name], self.ex.chip_sums(carried, "l0r_" + name)):
                self.sums[i] = tot


class _Exchange:
    def __init__(self, shards, c):
        self.shards, self.c = shards, c

    def weights_first(self):
        first, _ = _first_block_groups(0)
        return _whole_weights(self.shards, first, all_gather_blocks(_weight_blocks(self.shards, first)))

    def rest_blocks(self):
        return _weight_blocks(self.shards, _first_block_groups(0)[1])

    def weights_rest(self, gathered):
        return _whole_weights(self.shards, _first_block_groups(0)[1], gathered)

    def gather_start(self):
        return gather_side(0, _weight_blocks(self.shards, _layer_groups(1)))

    def weights_next(self, gathered):
        return _whole_weights(self.shards, _layer_groups(1), gathered)

    def chain_next(self, grads):
        return _PairChain(self, _gradient_buffers(grads, _layer_groups(1)))

    def chain_rest(self, grads):
        return _RestChain(self, _gradient_buffers(grads, _first_block_groups(0)[1]))

    def pair_sums(self, bufs, got, tag):
        return [pair_sum(b, g, self.c, b.dtype, f"grads_pair_sum_{tag}_{i}") for i, (b, g) in enumerate(zip(bufs, got))]

    def chip_sums(self, parts, tag):
        return [chip_sum(p, f"grads_chip_sum_{tag}_{i}") for i, p in enumerate(parts)]

    def finish(self, gm, gv, sums1, sums_rest):
        vecs = {n: jnp.concatenate(v, axis=0) for n, v in gv.items()}
        first, rest = _first_block_groups(0)
        bufs = _gradient_buffers(gm[0], first)
        vec = jnp.concatenate([vecs[n].reshape(-1) for n in _VECS])
        vec = jnp.pad(vec, (0, -vec.shape[0] % (16 * LANES)))
        bufs.append(jnp.broadcast_to(vec.reshape(1, -1, LANES), (N_DEV, vec.shape[0] // LANES, LANES)))
        parts = self.pair_sums(bufs, pair_exchange(bufs), "l0")
        sums0 = self.chip_sums(chip_exchange(parts), "l0")
        blocks = {**_gradient_blocks(self.shards, first, sums0[:-1]), **_gradient_blocks(self.shards, rest, sums_rest),
                  **_gradient_blocks(self.shards, _layer_groups(1), sums1)}
        out = {n: jnp.stack([blocks[n, j] for j in range(self.shards[n].shape[0])]) for n, _ in _MATS}
        flat, off = sums0[-1].reshape(-1), 0
        for n in _VECS:
            out[n] = flat[off:off + vecs[n].size].reshape(vecs[n].shape)
            off += vecs[n].size
        return out


class _NoExchange:
    def __init__(self, full):
        self.full = full

    def weights_first(self):
        return self.full[0]

    def rest_blocks(self):
        return None

    def gather_start(self):
        return None

    def weights_next(self, gathered):
        return self.full[1]

    def chain_next(self, grads):
        return None

    def chain_rest(self, grads):
        return None

    def finish(self, gm, gv, sums1, sums_rest):
        mats = {}
        for l in range(DEPTH):
            for group in _layer_groups(l):
                for n, j in group:
                    whole = jnp.concatenate(gm[l][n], axis=0) if isinstance(gm[l][n], tuple) else gm[l][n]
                    mats.setdefault(n, {})[j] = whole if _AXIS[n] == "row" else whole.T
        mats = {n: jnp.stack([v[j] for j in sorted(v)]) for n, v in mats.items()}
        return mats, {n: jnp.concatenate(v, axis=0) for n, v in gv.items()}


def _local_step(x, mem, target, w, ex):
    assert w["ffn1_norm"].shape[0] == DEPTH
    row = lambda a, l: a[l:l + 1]
    full = [ex.weights_first(), None]
    saved = []
    for l in range(DEPTH):
        t, j, f = f"l{l}", l // 2, full[l]
        rest = ex.rest_blocks() if l == 0 else None
        if rest is None:
            x, s1 = ffn_fwd(x, row(w["ffn1_norm"], l), f["ffn1_w_gu"], f["ffn1_w_down"], t + "_ffn1")
        else:
            x, s1, rest = ffn_fwd(x, row(w["ffn1_norm"], l), f["ffn1_w_gu"], f["ffn1_w_down"], t + "_ffn1", (0, rest))
        relay = None
        if l % 2 == 0:
            h = rmsnorm_fwd(x, row(w["mix_norm"], l), t + "_ev_norm", None if rest is None else gather_side(2, rest))
            if rest is not None:
                h, rest = h
                f = full[l] = {**f, **ex.weights_rest(rest)}
            side = ex.gather_start() if l + 1 < DEPTH else None
            x, s2, relay = even_mixer_fwd(x, h, _ev_reorder(f["ev_w_in"]), row(w["ev_q_gain"], j),
                                          row(w["ev_k_gain"], j), row(w["ev_sinks"], j), f["ev_w_out"], t + "_ev", side)
        else:
            x, s2 = odd_mixer_fwd(x, row(w["mix_norm"], l), f["od_w_in"], row(w["od_q_gain"], j),
                                  row(w["od_k_gain"], j), f["od_w_out"], t + "_od")
        x, s3 = xa_fwd(x, mem, row(w["xa_norm"], l), row(w["xa_mem_norm"], l), f["xa_w_q"], f["xa_w_kv"],
                       row(w["xa_q_gain"], l), row(w["xa_k_gain"], l), f["xa_w_o"], t + "_xa")
        if relay is None:
            x, s4 = ffn_fwd(x, row(w["ffn2_norm"], l), f["ffn2_w_gu"], f["ffn2_w_down"], t + "_ffn2")
        else:
            x, s4, relay = ffn_fwd(x, row(w["ffn2_norm"], l), f["ffn2_w_gu"], f["ffn2_w_down"], t + "_ffn2", (1, relay))
        if l + 1 < DEPTH:
            full[l + 1] = ex.weights_next(relay)
        saved.append((s1, s2, s3, s4))
    dx, sq = loss_head(x, target, "loss_head")
    loss = 0.5 * jnp.sum(sq) / x.shape[1]

    gm = [dict() for _ in range(DEPTH)]
    gv = {n: [None] * w[n].shape[0] for n in _VECS}
    chain1 = chain0 = sums1 = None
    started = []
    for l in reversed(range(DEPTH)):
        t, j, f = f"l{l}", l // 2, full[l]
        s1, s2, s3, s4 = saved[l]
        dx, gv["ffn2_norm"][l], gm[l]["ffn2_w_gu"], gm[l]["ffn2_w_down"] = ffn_bwd(
            dx, s4, row(w["ffn2_norm"], l), f["ffn2_w_gu"], f["ffn2_w_down"], t + "_ffn2", chain1 if l == 0 else None)
        parts = chain1.parts if l == 0 and chain1 is not None else None
        (dx, gv["xa_norm"][l], gv["xa_mem_norm"][l], gm[l]["xa_w_q"], gm[l]["xa_w_kv"], gv["xa_q_gain"][l],
         gv["xa_k_gain"][l], gm[l]["xa_w_o"]) = xa_bwd(
            dx, s3, mem, row(w["xa_norm"], l), row(w["xa_mem_norm"], l), f["xa_w_q"], f["xa_w_kv"],
            row(w["xa_q_gain"], l), row(w["xa_k_gain"], l), f["xa_w_o"], t + "_xa")
        if l % 2 == 0:
            def start_rest(d_win, d_wout, l=l):
                gm[l]["ev_w_in"], gm[l]["ev_w_out"] = _ev_restore(d_win), d_wout
                chain = ex.chain_rest(gm[l]) if l == 0 else None
                if chain is None:
                    return None
                started.append(chain)
                return chain.side("pair"), lambda got: chain.done("pair", got)

            (dx, gv["mix_norm"][l], d_win, gv["ev_q_gain"][j], gv["ev_k_gain"][j], gv["ev_sinks"][j],
             gm[l]["ev_w_out"], carried) = even_mixer_bwd(
                dx, s2, row(w["mix_norm"], l), _ev_reorder(f["ev_w_in"]), row(w["ev_q_gain"], j), row(w["ev_k_gain"], j),
                row(w["ev_sinks"], j), f["ev_w_out"], t + "_ev", None if parts is None else chip_side(parts), start_rest)
            gm[l]["ev_w_in"] = _ev_restore(d_win)
            if carried is not None:
                sums1 = ex.chip_sums(carried, "l1")
        else:
            (dx, gv["mix_norm"][l], gm[l]["od_w_in"], gv["od_q_gain"][j], gv["od_k_gain"][j],
             gm[l]["od_w_out"]) = odd_mixer_bwd(
                dx, s2, row(w["mix_norm"], l), f["od_w_in"], row(w["od_q_gain"], j), row(w["od_k_gain"], j),
                f["od_w_out"], t + "_od")
        if l == 0 and started:
            chain0 = started[0]
        dx, gv["ffn1_norm"][l], gm[l]["ffn1_w_gu"], gm[l]["ffn1_w_down"] = ffn_bwd(
            dx, s1, row(w["ffn1_norm"], l), f["ffn1_w_gu"], f["ffn1_w_down"], t + "_ffn1", chain0 if l == 0 else None)
        if l == 1:
            chain1 = ex.chain_next(gm[l])
    return loss, dx, ex.finish(gm, gv, sums1, None if chain0 is None else chain0.sums)


def kernel(x, mem, ffn1_norm, ffn1_w_gu, ffn1_w_down, mix_norm, ev_w_in, ev_q_gain, ev_k_gain, ev_sinks, ev_w_out, od_w_in, od_q_gain, od_k_gain, od_w_out, xa_norm, xa_mem_norm, xa_w_q, xa_w_kv, xa_q_gain, xa_k_gain, xa_w_o, ffn2_norm, ffn2_w_gu, ffn2_w_down, loss_target, m_ffn1_norm, m_ffn1_w_gu, m_ffn1_w_down, m_mix_norm, m_ev_w_in, m_ev_q_gain, m_ev_k_gain, m_ev_sinks, m_ev_w_out, m_od_w_in, m_od_q_gain, m_od_k_gain, m_od_w_out, m_xa_norm, m_xa_mem_norm, m_xa_w_q, m_xa_w_kv, m_xa_q_gain, m_xa_k_gain, m_xa_w_o, m_ffn2_norm, m_ffn2_w_gu, m_ffn2_w_down, v_ffn1_norm, v_ffn1_w_gu, v_ffn1_w_down, v_mix_norm, v_ev_w_in, v_ev_q_gain, v_ev_k_gain, v_ev_sinks, v_ev_w_out, v_od_w_in, v_od_q_gain, v_od_k_gain, v_od_w_out, v_xa_norm, v_xa_mem_norm, v_xa_w_q, v_xa_w_kv, v_xa_q_gain, v_xa_k_gain, v_xa_w_o, v_ffn2_norm, v_ffn2_w_gu, v_ffn2_w_down):
    w = dict(ffn1_norm=ffn1_norm, ffn1_w_gu=ffn1_w_gu, ffn1_w_down=ffn1_w_down, mix_norm=mix_norm, ev_w_in=ev_w_in, ev_q_gain=ev_q_gain, ev_k_gain=ev_k_gain, ev_sinks=ev_sinks, ev_w_out=ev_w_out, od_w_in=od_w_in, od_q_gain=od_q_gain, od_k_gain=od_k_gain, od_w_out=od_w_out, xa_norm=xa_norm, xa_mem_norm=xa_mem_norm, xa_w_q=xa_w_q, xa_w_kv=xa_w_kv, xa_q_gain=xa_q_gain, xa_k_gain=xa_k_gain, xa_w_o=xa_w_o, ffn2_norm=ffn2_norm, ffn2_w_gu=ffn2_w_gu, ffn2_w_down=ffn2_w_down)
    m = dict(ffn1_norm=m_ffn1_norm, ffn1_w_gu=m_ffn1_w_gu, ffn1_w_down=m_ffn1_w_down, mix_norm=m_mix_norm, ev_w_in=m_ev_w_in, ev_q_gain=m_ev_q_gain, ev_k_gain=m_ev_k_gain, ev_sinks=m_ev_sinks, ev_w_out=m_ev_w_out, od_w_in=m_od_w_in, od_q_gain=m_od_q_gain, od_k_gain=m_od_k_gain, od_w_out=m_od_w_out, xa_norm=m_xa_norm, xa_mem_norm=m_xa_mem_norm, xa_w_q=m_xa_w_q, xa_w_kv=m_xa_w_kv, xa_q_gain=m_xa_q_gain, xa_k_gain=m_xa_k_gain, xa_w_o=m_xa_w_o, ffn2_norm=m_ffn2_norm, ffn2_w_gu=m_ffn2_w_gu, ffn2_w_down=m_ffn2_w_down)
    v = dict(ffn1_norm=v_ffn1_norm, ffn1_w_gu=v_ffn1_w_gu, ffn1_w_down=v_ffn1_w_down, mix_norm=v_mix_norm, ev_w_in=v_ev_w_in, ev_q_gain=v_ev_q_gain, ev_k_gain=v_ev_k_gain, ev_sinks=v_ev_sinks, ev_w_out=v_ev_w_out, od_w_in=v_od_w_in, od_q_gain=v_od_q_gain, od_k_gain=v_od_k_gain, od_w_out=v_od_w_out, xa_norm=v_xa_norm, xa_mem_norm=v_xa_mem_norm, xa_w_q=v_xa_w_q, xa_w_kv=v_xa_w_kv, xa_q_gain=v_xa_q_gain, xa_k_gain=v_xa_k_gain, xa_w_o=v_xa_w_o, ffn2_norm=v_ffn2_norm, ffn2_w_gu=v_ffn2_w_gu, ffn2_w_down=v_ffn2_w_down)

    c = lax.axis_index("c").astype(jnp.int32).reshape(1)
    loss, dx, grads = _local_step(x[0], mem[0], loss_target[0], w, _Exchange(w, c))
    loss = lax.psum(loss, ("x", "y", "c"))

    delta, new_m, new_v = {}, {}, {}
    for n in _WEIGHTS:
        delta[n], new_m[n], new_v[n] = adamw(w[n], grads[n], m[n], v[n], "adamw_" + n)
    return (loss, dx[None], *[grads[n] for n in _WEIGHTS], *[delta[n] for n in _WEIGHTS],
            *[new_m[n] for n in _WEIGHTS], *[new_v[n] for n in _WEIGHTS])
```
